```python
import math
import jax, jax.numpy as jnp
from jax import lax
import numpy as np

D_MODEL = 1024
BATCH = 8
SEQ = 8192
DEPTH = 2

CHUNK = 64
N_LEFT_CHUNKS = 8
BAND = (N_LEFT_CHUNKS + 1) * CHUNK
HEAD_DIM = 64
N_HEADS_A = 8
N_HEADS_B = 8
WIDTH_A = N_HEADS_A * HEAD_DIM
WIDTH_B = N_HEADS_B * HEAD_DIM
POOL_WINDOWS = (2, 4, 8, 16)
N_POOL_GROUPS = len(POOL_WINDOWS)
POOL_GROUP_DIM = D_MODEL // 8
WIDTH_C = N_POOL_GROUPS * POOL_GROUP_DIM
N_BRANCHES = 3
MAX_REL_DIST = 2 * CHUNK
REL_TABLE = MAX_REL_DIST + CHUNK
D_FF = 2816
CONV_WIDTH = 3
Q_BLOCK = 128
EPS = 1e-6

IN_SIZES = (WIDTH_A, WIDTH_A, WIDTH_A, WIDTH_B, WIDTH_B, WIDTH_B, WIDTH_C,
            N_BRANCHES * D_MODEL)
IN_COLS = sum(IN_SIZES)
IN_SPLITS = [int(v) for v in np.cumsum(IN_SIZES)[:-1]]

kernel_name = "hybrid_chunk_causal_gated_branches"


def rms_norm(x, gain):
    xf = x.astype(jnp.float32)
    y = xf * lax.rsqrt(jnp.mean(xf * xf, axis=-1, keepdims=True) + EPS)
    return (y * gain.astype(jnp.float32)).astype(x.dtype)


def chunked_rel_attention(q, k, v, g_q, g_k, rel_bias):
    B, S, H, Dh = q.shape
    nc = S // CHUNK
    q = rms_norm(q, g_q)
    k = rms_norm(k, g_k)
    qc = q.reshape(B, nc, CHUNK, H, Dh)
    pad = ((0, 0), (N_LEFT_CHUNKS, 0), (0, 0), (0, 0), (0, 0))
    kp = jnp.pad(k.reshape(B, nc, CHUNK, H, Dh), pad)
    vp = jnp.pad(v.reshape(B, nc, CHUNK, H, Dh), pad)
    k_band = jnp.concatenate([kp[:, j:j + nc] for j in range(N_LEFT_CHUNKS + 1)], axis=2)
    v_band = jnp.concatenate([vp[:, j:j + nc] for j in range(N_LEFT_CHUNKS + 1)], axis=2)
    logits = jnp.einsum('bcqhd,bckhd->bhcqk', qc, k_band).astype(jnp.float32) / math.sqrt(Dh)
    q_off = jnp.arange(CHUNK)[:, None] + N_LEFT_CHUNKS * CHUNK
    k_off = jnp.arange(BAND)[None, :]
    rel = jnp.clip(q_off - k_off, -(CHUNK - 1), MAX_REL_DIST) + (CHUNK - 1)
    bias = rel_bias.astype(jnp.float32)[:, rel]
    valid = (jnp.arange(nc)[:, None] + k_off // CHUNK - N_LEFT_CHUNKS) >= 0
    logits = jnp.where(valid[None, None, :, None, :], logits + bias[None, :, None], -1e30)
    p = jax.nn.softmax(logits, axis=-1)
    out = jnp.einsum('bhcqk,bckhd->bcqhd', p.astype(v.dtype), v_band)
    return out.reshape(B, S, H * Dh)


def stick_breaking_attention(q, k, v):
    B, S, H, Dh = q.shape
    nb = S // Q_BLOCK
    scale = 1.0 / math.sqrt(Dh)
    qb = q.reshape(B, nb, Q_BLOCK, H, Dh).transpose(1, 0, 2, 3, 4)
    k_pos = jnp.arange(S)
    starts = jnp.arange(nb, dtype=jnp.int32) * Q_BLOCK

    def block(args):
        q_blk, start = args
        z = jnp.einsum('bqhd,bkhd->bhqk', q_blk, k).astype(jnp.float32) * scale
        q_pos = start + jnp.arange(Q_BLOCK)
        before = k_pos[None, :] < q_pos[:, None]
        log_keep = jnp.where(before, jax.nn.log_sigmoid(-z), 0.0)
        tail = lax.cumsum(log_keep, axis=3, reverse=True) - log_keep
        w = jnp.where(before, jnp.exp(jax.nn.log_sigmoid(z) + tail), 0.0)
        return jnp.einsum('bhqk,bkhd->bqhd', w.astype(v.dtype), v)

    out = lax.map(block, (qb, starts))
    return out.transpose(1, 0, 2, 3, 4).reshape(B, S, H * Dh)


def multiscale_pool(u, w_group, scale):
    B, S, _ = u.shape
    uf = u.astype(jnp.float32).reshape(B, S, N_POOL_GROUPS, POOL_GROUP_DIM)
    cs = jnp.pad(jnp.cumsum(uf, axis=1), ((0, 0), (1, 0), (0, 0), (0, 0)))
    t = jnp.arange(S)
    pooled = []
    for g, win in enumerate(POOL_WINDOWS):
        lo = jnp.maximum(t + 1 - win, 0)
        win_sum = cs[:, 1:, g] - cs[:, lo, g]
        count = (t + 1 - lo).astype(jnp.float32)
        pooled.append(win_sum / count[None, :, None])
    pooled = jnp.stack(pooled, axis=2) - uf
    mixed = jnp.einsum('bsgc,gce->bsge', pooled.astype(u.dtype), w_group)
    return mixed.reshape(B, S, WIDTH_C) * scale


def conv_gated_mlp(h, w_up, conv_w, conv_b, w_down):
    S = h.shape[1]
    u = h @ w_up
    up = jnp.pad(u, ((0, 0), (CONV_WIDTH - 1, 0), (0, 0)))
    c = conv_b + conv_w[0] * up[:, 0:S]
    for j in range(1, CONV_WIDTH):
        c = c + conv_w[j] * up[:, j:j + S]
    gate, val = jnp.split(c, 2, axis=-1)
    return (jax.nn.silu(gate) * val) @ w_down


def _fwd_setup_inputs(seed: int = 0) -> dict:
    key = jax.random.key(seed)
    ks = jax.random.split(key, 20)
    f32 = jnp.float32
    n = lambda k, shape, s: jax.random.normal(k, shape, f32) * s
    L = DEPTH
    return {
        "x": n(ks[0], (BATCH, SEQ, D_MODEL), 1.0),
        "norm_mix": 1.0 + n(ks[1], (L, D_MODEL), 0.05),
        "w_in": n(ks[2], (L, D_MODEL, IN_COLS), D_MODEL ** -0.5),
        "b_gate": n(ks[3], (L, N_BRANCHES * D_MODEL), 0.1),
        "q_norm_a": 1.0 + n(ks[4], (L, HEAD_DIM), 0.05),
        "k_norm_a": 1.0 + n(ks[5], (L, HEAD_DIM), 0.05),
        "rel_bias_a": n(ks[6], (L, N_HEADS_A, REL_TABLE), 0.5),
        "w_pool": n(ks[7], (L, N_POOL_GROUPS, POOL_GROUP_DIM, POOL_GROUP_DIM), POOL_GROUP_DIM ** -0.5),
        "pool_scale": 1.0 + n(ks[8], (L, WIDTH_C), 0.1),
        "w_branch_a": n(ks[9], (L, WIDTH_A, D_MODEL), WIDTH_A ** -0.5),
        "w_branch_b": n(ks[10], (L, WIDTH_B, D_MODEL), WIDTH_B ** -0.5),
        "w_branch_c": n(ks[11], (L, WIDTH_C, D_MODEL), WIDTH_C ** -0.5),
        "w_out": n(ks[12], (L, D_MODEL, D_MODEL), D_MODEL ** -0.5),
        "norm_ffn": 1.0 + n(ks[13], (L, D_MODEL), 0.05),
        "w_up": n(ks[14], (L, D_MODEL, 2 * D_FF), D_MODEL ** -0.5),
        "conv_w": n(ks[15], (L, CONV_WIDTH, 2 * D_FF), CONV_WIDTH ** -0.5),
        "conv_b": n(ks[16], (L, 2 * D_FF), 0.02),
        "w_down": n(ks[17], (L, D_FF, D_MODEL), D_FF ** -0.5),
    }


def _fwd_reference(x, norm_mix, w_in, b_gate, q_norm_a, k_norm_a, rel_bias_a, w_pool,
              pool_scale, w_branch_a, w_branch_b, w_branch_c, w_out, norm_ffn,
              w_up, conv_w, conv_b, w_down):
    B, S, D = x.shape
    for l in range(DEPTH):
        h = rms_norm(x, norm_mix[l])
        proj = h @ w_in[l]
        q_a, k_a, v_a, q_b, k_b, v_b, u_c, g_logits = jnp.split(proj, IN_SPLITS, axis=-1)
        heads_a = lambda t: t.reshape(B, S, N_HEADS_A, HEAD_DIM)
        heads_b = lambda t: t.reshape(B, S, N_HEADS_B, HEAD_DIM)
        o_a = chunked_rel_attention(heads_a(q_a), heads_a(k_a), heads_a(v_a),
                                    q_norm_a[l], k_norm_a[l], rel_bias_a[l])
        o_b = stick_breaking_attention(heads_b(q_b), heads_b(k_b), heads_b(v_b))
        o_c = multiscale_pool(u_c, w_pool[l], pool_scale[l])
        gates = jax.nn.sigmoid((g_logits + b_gate[l]).astype(jnp.float32)).astype(x.dtype)
        gates = gates.reshape(B, S, N_BRANCHES, D)
        merged = (gates[:, :, 0] * (o_a @ w_branch_a[l])
                  + gates[:, :, 1] * (o_b @ w_branch_b[l])
                  + gates[:, :, 2] * (o_c @ w_branch_c[l]))
        x = x + merged @ w_out[l]
        h2 = rms_norm(x, norm_ffn[l])
        x = x + conv_gated_mlp(h2, w_up[l], conv_w[l], conv_b[l], w_down[l])
    return x


import jax as _jax
import jax.numpy as _jnp

TWIN_FORMAT = 'train_step'
FWD_PARAMS = ['x', 'norm_mix', 'w_in', 'b_gate', 'q_norm_a', 'k_norm_a', 'rel_bias_a', 'w_pool', 'pool_scale', 'w_branch_a', 'w_branch_b', 'w_branch_c', 'w_out', 'norm_ffn', 'w_up', 'conv_w', 'conv_b', 'w_down']
TWIN_WEIGHTS = ['norm_mix', 'w_in', 'b_gate', 'q_norm_a', 'k_norm_a', 'rel_bias_a', 'w_pool', 'pool_scale', 'w_branch_a', 'w_branch_b', 'w_branch_c', 'w_out', 'norm_ffn', 'w_up', 'conv_w', 'conv_b', 'w_down']
TWIN_DIFF_INPUT = 'x'
TWIN_INPUTS = ['x', 'norm_mix', 'w_in', 'b_gate', 'q_norm_a', 'k_norm_a', 'rel_bias_a', 'w_pool', 'pool_scale', 'w_branch_a', 'w_branch_b', 'w_branch_c', 'w_out', 'norm_ffn', 'w_up', 'conv_w', 'conv_b', 'w_down', 'loss_target', 'm_norm_mix', 'm_w_in', 'm_b_gate', 'm_q_norm_a', 'm_k_norm_a', 'm_rel_bias_a', 'm_w_pool', 'm_pool_scale', 'm_w_branch_a', 'm_w_branch_b', 'm_w_branch_c', 'm_w_out', 'm_norm_ffn', 'm_w_up', 'm_conv_w', 'm_conv_b', 'm_w_down', 'v_norm_mix', 'v_w_in', 'v_b_gate', 'v_q_norm_a', 'v_k_norm_a', 'v_rel_bias_a', 'v_w_pool', 'v_pool_scale', 'v_w_branch_a', 'v_w_branch_b', 'v_w_branch_c', 'v_w_out', 'v_norm_ffn', 'v_w_up', 'v_conv_w', 'v_conv_b', 'v_w_down']
TWIN_OUTPUTS = ['loss', 'grad_x', 'grad_norm_mix', 'grad_w_in', 'grad_b_gate', 'grad_q_norm_a', 'grad_k_norm_a', 'grad_rel_bias_a', 'grad_w_pool', 'grad_pool_scale', 'grad_w_branch_a', 'grad_w_branch_b', 'grad_w_branch_c', 'grad_w_out', 'grad_norm_ffn', 'grad_w_up', 'grad_conv_w', 'grad_conv_b', 'grad_w_down', 'delta_norm_mix', 'delta_w_in', 'delta_b_gate', 'delta_q_norm_a', 'delta_k_norm_a', 'delta_rel_bias_a', 'delta_w_pool', 'delta_pool_scale', 'delta_w_branch_a', 'delta_w_branch_b', 'delta_w_branch_c', 'delta_w_out', 'delta_norm_ffn', 'delta_w_up', 'delta_conv_w', 'delta_conv_b', 'delta_w_down', 'new_m_norm_mix', 'new_m_w_in', 'new_m_b_gate', 'new_m_q_norm_a', 'new_m_k_norm_a', 'new_m_rel_bias_a', 'new_m_w_pool', 'new_m_pool_scale', 'new_m_w_branch_a', 'new_m_w_branch_b', 'new_m_w_branch_c', 'new_m_w_out', 'new_m_norm_ffn', 'new_m_w_up', 'new_m_conv_w', 'new_m_conv_b', 'new_m_w_down', 'new_v_norm_mix', 'new_v_w_in', 'new_v_b_gate', 'new_v_q_norm_a', 'new_v_k_norm_a', 'new_v_rel_bias_a', 'new_v_w_pool', 'new_v_pool_scale', 'new_v_w_branch_a', 'new_v_w_branch_b', 'new_v_w_branch_c', 'new_v_w_out', 'new_v_norm_ffn', 'new_v_w_up', 'new_v_conv_w', 'new_v_conv_b', 'new_v_w_down']
TWIN_LEAF_KINDS = {'loss': 'loss', 'grad_x': 'grad_x', 'grad_norm_mix': 'grad_w', 'grad_w_in': 'grad_w', 'grad_b_gate': 'grad_w', 'grad_q_norm_a': 'grad_w', 'grad_k_norm_a': 'grad_w', 'grad_rel_bias_a': 'grad_w', 'grad_w_pool': 'grad_w', 'grad_pool_scale': 'grad_w', 'grad_w_branch_a': 'grad_w', 'grad_w_branch_b': 'grad_w', 'grad_w_branch_c': 'grad_w', 'grad_w_out': 'grad_w', 'grad_norm_ffn': 'grad_w', 'grad_w_up': 'grad_w', 'grad_conv_w': 'grad_w', 'grad_conv_b': 'grad_w', 'grad_w_down': 'grad_w', 'delta_norm_mix': 'delta_w', 'delta_w_in': 'delta_w', 'delta_b_gate': 'delta_w', 'delta_q_norm_a': 'delta_w', 'delta_k_norm_a': 'delta_w', 'delta_rel_bias_a': 'delta_w', 'delta_w_pool': 'delta_w', 'delta_pool_scale': 'delta_w', 'delta_w_branch_a': 'delta_w', 'delta_w_branch_b': 'delta_w', 'delta_w_branch_c': 'delta_w', 'delta_w_out': 'delta_w', 'delta_norm_ffn': 'delta_w', 'delta_w_up': 'delta_w', 'delta_conv_w': 'delta_w', 'delta_conv_b': 'delta_w', 'delta_w_down': 'delta_w', 'new_m_norm_mix': 'new_m', 'new_m_w_in': 'new_m', 'new_m_b_gate': 'new_m', 'new_m_q_norm_a': 'new_m', 'new_m_k_norm_a': 'new_m', 'new_m_rel_bias_a': 'new_m', 'new_m_w_pool': 'new_m', 'new_m_pool_scale': 'new_m', 'new_m_w_branch_a': 'new_m', 'new_m_w_branch_b': 'new_m', 'new_m_w_branch_c': 'new_m', 'new_m_w_out': 'new_m', 'new_m_norm_ffn': 'new_m', 'new_m_w_up': 'new_m', 'new_m_conv_w': 'new_m', 'new_m_conv_b': 'new_m', 'new_m_w_down': 'new_m', 'new_v_norm_mix': 'new_v', 'new_v_w_in': 'new_v', 'new_v_b_gate': 'new_v', 'new_v_q_norm_a': 'new_v', 'new_v_k_norm_a': 'new_v', 'new_v_rel_bias_a': 'new_v', 'new_v_w_pool': 'new_v', 'new_v_pool_scale': 'new_v', 'new_v_w_branch_a': 'new_v', 'new_v_w_branch_b': 'new_v', 'new_v_w_branch_c': 'new_v', 'new_v_w_out': 'new_v', 'new_v_norm_ffn': 'new_v', 'new_v_w_up': 'new_v', 'new_v_conv_w': 'new_v', 'new_v_conv_b': 'new_v', 'new_v_w_down': 'new_v'}


def _forward(args):
    return _fwd_reference(*[args[k] for k in FWD_PARAMS])


def _output_shape():
    def fwd():
        inp = _fwd_setup_inputs(0)
        return _fwd_reference(*[inp[k] for k in FWD_PARAMS])
    out = _jax.eval_shape(fwd)
    return out.shape, out.dtype

N_MICROBATCH = 1
ADAM_LR = 0.001
ADAM_B1 = 0.9
ADAM_B2 = 0.999
ADAM_EPS = 1e-08
ADAM_WD = 0.01
ADAM_STEP = 10
PER_EXAMPLE_BATCH_AXIS = {'x': 0, 'loss_target': 0}
SHARED_INPUTS = []
_WEIGHT_DTYPES = {'norm_mix': _jnp.float32, 'w_in': _jnp.float32, 'b_gate': _jnp.float32, 'q_norm_a': _jnp.float32, 'k_norm_a': _jnp.float32, 'rel_bias_a': _jnp.float32, 'w_pool': _jnp.float32, 'pool_scale': _jnp.float32, 'w_branch_a': _jnp.float32, 'w_branch_b': _jnp.float32, 'w_branch_c': _jnp.float32, 'w_out': _jnp.float32, 'norm_ffn': _jnp.float32, 'w_up': _jnp.float32, 'conv_w': _jnp.float32, 'conv_b': _jnp.float32, 'w_down': _jnp.float32}
MOMENT_SCALE = {'norm_mix': 2.703370e+01, 'w_in': 4.977165e-01, 'b_gate': 3.390927e+00, 'q_norm_a': 1.123194e+00, 'k_norm_a': 1.118253e+00, 'rel_bias_a': 6.724279e-02, 'w_pool': 3.064271e+00, 'pool_scale': 3.029549e+01, 'w_branch_a': 9.185430e-02, 'w_branch_b': 4.829421e-01, 'w_branch_c': 1.469787e+00, 'w_out': 1.348828e+00, 'norm_ffn': 5.247878e+01, 'w_up': 4.908284e-01, 'conv_w': 7.247448e+00, 'conv_b': 6.575040e+00, 'w_down': 7.370976e-01}


def _to_microbatches(a, axis):
    t = _jnp.moveaxis(a, axis, 0)
    t = t.reshape((N_MICROBATCH, t.shape[0] // N_MICROBATCH) + t.shape[1:])
    return _jnp.moveaxis(t, 1, axis + 1)


def setup_inputs(seed: int = 0) -> dict:
    inp = _fwd_setup_inputs(seed)
    key = _jax.random.fold_in(_jax.random.key(seed), 7919)
    shape, _ = _output_shape()
    out = dict(inp)
    out["loss_target"] = _jax.random.normal(_jax.random.fold_in(key, 0), shape, _jnp.float32)
    for i, name in enumerate(TWIN_WEIGHTS):
        w = inp[name].astype(_jnp.float32)
        if MOMENT_SCALE is None:
            s = _jnp.sqrt(_jnp.mean(_jnp.square(w)) + 1e-30)
        else:
            s = MOMENT_SCALE[name]
        km, kv = _jax.random.split(_jax.random.fold_in(key, i + 1))
        out[name] = w
        out["m_" + name] = s * _jax.random.normal(km, w.shape, _jnp.float32)
        out["v_" + name] = (s * s) * _jax.random.uniform(kv, w.shape, _jnp.float32, 0.5, 1.5)
    if N_MICROBATCH > 1:
        for name, axis in PER_EXAMPLE_BATCH_AXIS.items():
            out[name] = _to_microbatches(out[name], axis)
    return {'x': out['x'], 'norm_mix': out['norm_mix'], 'w_in': out['w_in'], 'b_gate': out['b_gate'], 'q_norm_a': out['q_norm_a'], 'k_norm_a': out['k_norm_a'], 'rel_bias_a': out['rel_bias_a'], 'w_pool': out['w_pool'], 'pool_scale': out['pool_scale'], 'w_branch_a': out['w_branch_a'], 'w_branch_b': out['w_branch_b'], 'w_branch_c': out['w_branch_c'], 'w_out': out['w_out'], 'norm_ffn': out['norm_ffn'], 'w_up': out['w_up'], 'conv_w': out['conv_w'], 'conv_b': out['conv_b'], 'w_down': out['w_down'], 'loss_target': out['loss_target'], 'm_norm_mix': out['m_norm_mix'], 'm_w_in': out['m_w_in'], 'm_b_gate': out['m_b_gate'], 'm_q_norm_a': out['m_q_norm_a'], 'm_k_norm_a': out['m_k_norm_a'], 'm_rel_bias_a': out['m_rel_bias_a'], 'm_w_pool': out['m_w_pool'], 'm_pool_scale': out['m_pool_scale'], 'm_w_branch_a': out['m_w_branch_a'], 'm_w_branch_b': out['m_w_branch_b'], 'm_w_branch_c': out['m_w_branch_c'], 'm_w_out': out['m_w_out'], 'm_norm_ffn': out['m_norm_ffn'], 'm_w_up': out['m_w_up'], 'm_conv_w': out['m_conv_w'], 'm_conv_b': out['m_conv_b'], 'm_w_down': out['m_w_down'], 'v_norm_mix': out['v_norm_mix'], 'v_w_in': out['v_w_in'], 'v_b_gate': out['v_b_gate'], 'v_q_norm_a': out['v_q_norm_a'], 'v_k_norm_a': out['v_k_norm_a'], 'v_rel_bias_a': out['v_rel_bias_a'], 'v_w_pool': out['v_w_pool'], 'v_pool_scale': out['v_pool_scale'], 'v_w_branch_a': out['v_w_branch_a'], 'v_w_branch_b': out['v_w_branch_b'], 'v_w_branch_c': out['v_w_branch_c'], 'v_w_out': out['v_w_out'], 'v_norm_ffn': out['v_norm_ffn'], 'v_w_up': out['v_w_up'], 'v_conv_w': out['v_conv_w'], 'v_conv_b': out['v_conv_b'], 'v_w_down': out['v_w_down']}


def _loss(weights, diff, rest, loss_target):
    with _jax.named_scope("forward"):
        args = {**rest, TWIN_DIFF_INPUT: diff, **{k: w.astype(_WEIGHT_DTYPES[k]) for k, w in weights.items()}}
        y = _forward(args)
    with _jax.named_scope("loss_head"):
        err = _jnp.square(y.astype(_jnp.float32) - loss_target)
        return 0.5 * _jnp.sum(_jnp.mean(err, axis=-1)) if err.ndim else 0.5 * err


def _adamw(w, g, m, v):
    m = ADAM_B1 * m + (1.0 - ADAM_B1) * g
    v = ADAM_B2 * v + (1.0 - ADAM_B2) * _jnp.square(g)
    m_hat = m / (1.0 - ADAM_B1 ** ADAM_STEP)
    v_hat = v / (1.0 - ADAM_B2 ** ADAM_STEP)
    delta = -ADAM_LR * (m_hat / (_jnp.sqrt(v_hat) + ADAM_EPS) + ADAM_WD * w)
    return delta, m, v


def reference(x, norm_mix, w_in, b_gate, q_norm_a, k_norm_a, rel_bias_a, w_pool, pool_scale, w_branch_a, w_branch_b, w_branch_c, w_out, norm_ffn, w_up, conv_w, conv_b, w_down, loss_target, m_norm_mix, m_w_in, m_b_gate, m_q_norm_a, m_k_norm_a, m_rel_bias_a, m_w_pool, m_pool_scale, m_w_branch_a, m_w_branch_b, m_w_branch_c, m_w_out, m_norm_ffn, m_w_up, m_conv_w, m_conv_b, m_w_down, v_norm_mix, v_w_in, v_b_gate, v_q_norm_a, v_k_norm_a, v_rel_bias_a, v_w_pool, v_pool_scale, v_w_branch_a, v_w_branch_b, v_w_branch_c, v_w_out, v_norm_ffn, v_w_up, v_conv_w, v_conv_b, v_w_down):
    given = dict(x=x, norm_mix=norm_mix, w_in=w_in, b_gate=b_gate, q_norm_a=q_norm_a, k_norm_a=k_norm_a, rel_bias_a=rel_bias_a, w_pool=w_pool, pool_scale=pool_scale, w_branch_a=w_branch_a, w_branch_b=w_branch_b, w_branch_c=w_branch_c, w_out=w_out, norm_ffn=norm_ffn, w_up=w_up, conv_w=conv_w, conv_b=conv_b, w_down=w_down, loss_target=loss_target, m_norm_mix=m_norm_mix, m_w_in=m_w_in, m_b_gate=m_b_gate, m_q_norm_a=m_q_norm_a, m_k_norm_a=m_k_norm_a, m_rel_bias_a=m_rel_bias_a, m_w_pool=m_w_pool, m_pool_scale=m_pool_scale, m_w_branch_a=m_w_branch_a, m_w_branch_b=m_w_branch_b, m_w_branch_c=m_w_branch_c, m_w_out=m_w_out, m_norm_ffn=m_norm_ffn, m_w_up=m_w_up, m_conv_w=m_conv_w, m_conv_b=m_conv_b, m_w_down=m_w_down, v_norm_mix=v_norm_mix, v_w_in=v_w_in, v_b_gate=v_b_gate, v_q_norm_a=v_q_norm_a, v_k_norm_a=v_k_norm_a, v_rel_bias_a=v_rel_bias_a, v_w_pool=v_w_pool, v_pool_scale=v_pool_scale, v_w_branch_a=v_w_branch_a, v_w_branch_b=v_w_branch_b, v_w_branch_c=v_w_branch_c, v_w_out=v_w_out, v_norm_ffn=v_norm_ffn, v_w_up=v_w_up, v_conv_w=v_conv_w, v_conv_b=v_conv_b, v_w_down=v_w_down)
    weights = {n: given[n] for n in TWIN_WEIGHTS}
    shared = {n: given[n] for n in SHARED_INPUTS}
    per_example = {n: given[n] for n in ['x']}
    grad_fn = _jax.value_and_grad(_loss, argnums=(0, 1))

    def one_microbatch(ex, loss_target):
        ex = dict(ex)
        diff = ex.pop(TWIN_DIFF_INPUT)
        return grad_fn(weights, diff, {**shared, **ex}, loss_target)

    if N_MICROBATCH == 1:
        loss, (grad_w, grad_x) = one_microbatch(per_example, given["loss_target"])
    else:
        def body(carry, xs):
            loss_sum, grad_sum = carry
            l_k, (gw_k, gx_k) = one_microbatch(xs[0], xs[1])
            with _jax.named_scope("update"):
                return (loss_sum + l_k, _jax.tree.map(_jnp.add, grad_sum, gw_k)), gx_k

        init = (_jnp.zeros((), _jnp.float32), _jax.tree.map(_jnp.zeros_like, weights))
        (loss, grad_w), grad_x = _jax.lax.scan(body, init, (per_example, given["loss_target"]))
    with _jax.named_scope("update"):
        delta_w, new_m, new_v = {}, {}, {}
        for n in TWIN_WEIGHTS:
            delta_w[n], new_m[n], new_v[n] = _adamw(weights[n], grad_w[n], given["m_" + n], given["v_" + n])
    return (loss, grad_x, *[grad_w[n] for n in TWIN_WEIGHTS], *[delta_w[n] for n in TWIN_WEIGHTS],
            *[new_m[n] for n in TWIN_WEIGHTS], *[new_v[n] for n in TWIN_WEIGHTS])
```

```python
import functools
import math

import jax
import jax.numpy as jnp
import numpy as np
from jax import lax
from jax.experimental import pallas as pl
from jax.experimental.pallas import tpu as pltpu

F32 = jnp.float32
BF16 = jnp.bfloat16

D_MODEL = 1024
DEPTH = 2
CHUNK = 64
N_LEFT = 8
HEAD_DIM = 64
N_HEADS = 8
WIDTH = 512
POOL_WINDOWS = (2, 4, 8, 16)
GROUP_DIM = 128
MAX_REL = 2 * CHUNK
REL_TABLE = MAX_REL + CHUNK
D_FF = 2816
EPS = 1e-6
IN_COLS = 7 * WIDTH + 3 * D_MODEL
GATE_COL0 = 7 * WIDTH

ADAM_LR = 0.001
ADAM_B1 = 0.9
ADAM_B2 = 0.999
ADAM_EPS = 1e-08
ADAM_WD = 0.01
ADAM_STEP = 10

VMEM_LIMIT = 56 * 1024 * 1024
ATT_Q = 256
A_WIN = ATT_Q + N_LEFT * CHUNK
HALO = 16
CONV_HALO = 8
NEG = -1e30

MESH_AXES = ("x", "y", "c")
MESH_T = pl.DeviceIdType.MESH


def _cp(sem=None, vmem=VMEM_LIMIT):
    return pltpu.CompilerParams(dimension_semantics=sem, vmem_limit_bytes=vmem)


def _dot(a, b, ca, cb):
    return lax.dot_general(a, b, (((ca,), (cb,)), ((), ())), preferred_element_type=F32)


def _tile(n, cands=(512, 256, 128)):
    for c in cands:
        if n % c == 0:
            return c
    return n


def _split_hi_lo(v):
    hi = v.astype(BF16)
    lo = (v - hi.astype(F32)).astype(BF16)
    return hi, lo


def matmul(a, b, *, ta=False, tb=False, add=None, out_dtype=F32, name):
    if ta:
        K, M = a.shape
    else:
        M, K = a.shape
    if tb:
        N, K2 = b.shape
    else:
        K2, N = b.shape
    assert K == K2, (a.shape, b.shape, ta, tb)
    tm, tn, tk = _tile(M), _tile(N), _tile(K)
    nk = K // tk

    def body(*refs):
        if add is None:
            a_ref, b_ref, o_ref, acc = refs
        else:
            a_ref, b_ref, r_ref, o_ref, acc = refs
        k = pl.program_id(2)

        @pl.when(k == 0)
        def _():
            acc[...] = jnp.zeros_like(acc)

        av = a_ref[...].astype(BF16)
        bv = b_ref[...].astype(BF16)
        acc[...] += _dot(av, bv, 0 if ta else 1, 1 if tb else 0)

        @pl.when(k == nk - 1)
        def _():
            r = acc[...]
            if add is not None:
                r = r + r_ref[...].astype(F32)
            o_ref[...] = r.astype(out_dtype)

    a_spec = pl.BlockSpec((tk, tm), lambda i, j, k: (k, i)) if ta else pl.BlockSpec((tm, tk), lambda i, j, k: (i, k))
    b_spec = pl.BlockSpec((tn, tk), lambda i, j, k: (j, k)) if tb else pl.BlockSpec((tk, tn), lambda i, j, k: (k, j))
    o_spec = pl.BlockSpec((tm, tn), lambda i, j, k: (i, j))
    in_specs = [a_spec, b_spec]
    args = [a, b]
    if add is not None:
        in_specs.append(o_spec)
        args.append(add)
    return pl.pallas_call(
        body,
        out_shape=jax.ShapeDtypeStruct((M, N), out_dtype),
        grid=(M // tm, N // tn, nk),
        in_specs=in_specs,
        out_specs=o_spec,
        scratch_shapes=[pltpu.VMEM((tm, tn), F32)],
        compiler_params=_cp(("parallel", "parallel", "arbitrary")),
        name=name,
    )(*args)


def rmsnorm_fwd(x, g, *, name):
    S, D = x.shape
    T = _tile(S)

    def body(x_ref, g_ref, h_ref):
        xv = x_ref[...]
        r = lax.rsqrt(jnp.mean(xv * xv, axis=-1, keepdims=True) + EPS)
        h_ref[...] = (xv * r * g_ref[...]).astype(BF16)

    return pl.pallas_call(
        body,
        out_shape=jax.ShapeDtypeStruct((S, D), BF16),
        grid=(S // T,),
        in_specs=[pl.BlockSpec((T, D), lambda i: (i, 0)), pl.BlockSpec((1, D), lambda i: (0, 0))],
        out_specs=pl.BlockSpec((T, D), lambda i: (i, 0)),
        compiler_params=_cp(("parallel",)),
        name=name,
    )(x, g)


def rmsnorm_bwd(x, g, dh, dres, *, name):
    S, D = x.shape
    T = _tile(S)

    def body(x_ref, g_ref, dh_ref, dres_ref, dx_ref, dg_ref):
        i = pl.program_id(0)
        xv = x_ref[...]
        dhv = dh_ref[...].astype(F32)
        r = lax.rsqrt(jnp.mean(xv * xv, axis=-1, keepdims=True) + EPS)
        gd = dhv * g_ref[...]
        m = jnp.mean(xv * gd, axis=-1, keepdims=True)
        dx_ref[...] = dres_ref[...] + r * gd - xv * (r * r * r * m)

        @pl.when(i == 0)
        def _():
            dg_ref[...] = jnp.zeros_like(dg_ref)

        dg_ref[...] += jnp.sum(dhv * xv * r, axis=0, keepdims=True)

    row = pl.BlockSpec((T, D), lambda i: (i, 0))
    vec = pl.BlockSpec((1, D), lambda i: (0, 0))
    return pl.pallas_call(
        body,
        out_shape=(jax.ShapeDtypeStruct((S, D), F32), jax.ShapeDtypeStruct((1, D), F32)),
        grid=(S // T,),
        in_specs=[row, vec, row, row],
        out_specs=(row, vec),
        compiler_params=_cp(("arbitrary",)),
        name=name,
    )(x, g, dh, dres)


def _head_mean_matrix():
    r = lax.broadcasted_iota(jnp.int32, (WIDTH, WIDTH), 0) // HEAD_DIM
    c = lax.broadcasted_iota(jnp.int32, (WIDTH, WIDTH), 1) // HEAD_DIM
    return jnp.where(r == c, 1.0 / HEAD_DIM, 0.0).astype(BF16)


def _head_mean(v, mm):
    hi, lo = _split_hi_lo(v)
    return _dot(hi, mm, 1, 0) + _dot(lo, mm, 1, 0)


def qkv_prep(proj, gq, gk, *, name):
    S = proj.shape[0]
    T = _tile(S)

    def body(qa, ka, va, qb, kb, vb, gq_ref, gk_ref, oqa, oka, ova, oqb, okb, ovb):
        mm = _head_mean_matrix()
        for src, gref, dst in ((qa, gq_ref, oqa), (ka, gk_ref, oka)):
            v = src[...]
            r = lax.rsqrt(_head_mean(v * v, mm) + EPS)
            dst[...] = (v * r * gref[...]).astype(BF16)
        for src, dst in ((va, ova), (qb, oqb), (kb, okb), (vb, ovb)):
            dst[...] = src[...].astype(BF16)

    col = lambda j: pl.BlockSpec((T, WIDTH), lambda i, j=j: (i, j))
    vec = pl.BlockSpec((1, WIDTH), lambda i: (0, 0))
    out = pl.BlockSpec((T, WIDTH), lambda i: (i, 0))
    return pl.pallas_call(
        body,
        out_shape=tuple(jax.ShapeDtypeStruct((S, WIDTH), BF16) for _ in range(6)),
        grid=(S // T,),
        in_specs=[col(0), col(1), col(2), col(3), col(4), col(5), vec, vec],
        out_specs=tuple(out for _ in range(6)),
        compiler_params=_cp(("parallel",)),
        name=name,
    )(proj, proj, proj, proj, proj, proj, gq, gk)


def qknorm_bwd(proj, gq, gk, dqh, dkh, *, name):
    S = proj.shape[0]
    T = _tile(S)

    def body(qa, ka, gq_ref, gk_ref, dq_ref, dk_ref, oq, ok, ogq, ogk):
        i = pl.program_id(0)
        mm = _head_mean_matrix()

        @pl.when(i == 0)
        def _():
            ogq[...] = jnp.zeros_like(ogq)
            ogk[...] = jnp.zeros_like(ogk)

        for src, gref, dref, dst, gdst in ((qa, gq_ref, dq_ref, oq, ogq), (ka, gk_ref, dk_ref, ok, ogk)):
            v = src[...]
            dy = dref[...]
            r = lax.rsqrt(_head_mean(v * v, mm) + EPS)
            gd = dy * gref[...]
            m = _head_mean(v * gd, mm)
            dst[...] = (r * gd - v * (r * r * r * m)).astype(BF16)
            gdst[...] += jnp.sum(dy * v * r, axis=0, keepdims=True)

    col = lambda j: pl.BlockSpec((T, WIDTH), lambda i, j=j: (i, j))
    vec = pl.BlockSpec((1, WIDTH), lambda i: (0, 0))
    row = pl.BlockSpec((T, WIDTH), lambda i: (i, 0))
    return pl.pallas_call(
        body,
        out_shape=(jax.ShapeDtypeStruct((S, WIDTH), BF16), jax.ShapeDtypeStruct((S, WIDTH), BF16),
                   jax.ShapeDtypeStruct((1, WIDTH), F32), jax.ShapeDtypeStruct((1, WIDTH), F32)),
        grid=(S // T,),
        in_specs=[col(0), col(1), vec, vec, row, row],
        out_specs=(row, row, vec, vec),
        compiler_params=_cp(("arbitrary",)),
        name=name,
    )(proj, proj, gq, gk, dqh, dkh)


DIAG_W = 1024
N_VARIANTS = 3


def diagonal_onehot():
    jj = np.arange(DIAG_W)
    diff = np.where(jj < A_WIN, jj, jj - DIAG_W)
    out = np.zeros((N_VARIANTS, DIAG_W, REL_TABLE), np.float32)
    for v in range(N_VARIANTS):
        rel = np.clip(ATT_Q * v - diff, -(CHUNK - 1), MAX_REL) + (CHUNK - 1)
        out[v, jj, rel] = 1.0
    return out.reshape(N_VARIANTS * DIAG_W, REL_TABLE)


def exact_dot(a, b, *, name):
    def body(a_ref, b_ref, o_ref):
        o_ref[...] = jnp.dot(a_ref[...], b_ref[...], precision=lax.Precision.HIGHEST, preferred_element_type=F32)

    return pl.pallas_call(body, out_shape=jax.ShapeDtypeStruct((a.shape[0], b.shape[1]), F32),
                          compiler_params=_cp(), name=name)(a, b)


def _band_valid(v):
    qc = (lax.broadcasted_iota(jnp.int32, (ATT_Q, A_WIN), 0) + ATT_Q * v) // CHUNK
    kc = lax.broadcasted_iota(jnp.int32, (ATT_Q, A_WIN), 1) // CHUNK
    return (kc <= qc) & (kc >= qc - N_LEFT)


def bias_expand(diag, *, name):
    def body(d_ref, o_ref):
        rows = jnp.broadcast_to(d_ref[0, 0], (ATT_Q, DIAG_W))
        skew = pltpu.roll(rows, 0, 1, stride=1, stride_axis=0)
        o_ref[0, 0] = jnp.where(_band_valid(pl.program_id(0)), skew[:, :A_WIN], NEG)

    return pl.pallas_call(
        body,
        out_shape=jax.ShapeDtypeStruct((N_VARIANTS, N_HEADS, ATT_Q, A_WIN), F32),
        grid=(N_VARIANTS, N_HEADS),
        in_specs=[pl.BlockSpec((1, 1, 1, DIAG_W), lambda v, h: (v, h, 0, 0))],
        out_specs=pl.BlockSpec((1, 1, ATT_Q, A_WIN), lambda v, h: (v, h, 0, 0)),
        compiler_params=_cp(("parallel", "parallel")),
        name=name,
    )(diag)


def relbias_reduce(dbias, *, name):
    def body(db_ref, o_ref):
        x = jnp.concatenate([db_ref[0, 0], jnp.zeros((ATT_Q, DIAG_W - A_WIN), F32)], axis=1)
        row = lax.broadcasted_iota(jnp.int32, (ATT_Q, DIAG_W), 0)
        for b in range(8):
            x = jnp.where((row >> b) & 1 == 1, pltpu.roll(x, DIAG_W - (1 << b), 1), x)
        o_ref[0, 0] = jnp.sum(x, axis=0, keepdims=True)

    return pl.pallas_call(
        body,
        out_shape=jax.ShapeDtypeStruct((N_VARIANTS, N_HEADS, 1, DIAG_W), F32),
        grid=(N_VARIANTS, N_HEADS),
        in_specs=[pl.BlockSpec((1, 1, ATT_Q, A_WIN), lambda v, h: (v, h, 0, 0))],
        out_specs=pl.BlockSpec((1, 1, 1, DIAG_W), lambda v, h: (v, h, 0, 0)),
        compiler_params=_cp(("parallel", "parallel")),
        name=name,
    )(dbias)


def _a_window_start(qb):
    return pl.multiple_of(jnp.maximum(qb * ATT_Q - N_LEFT * CHUNK, 0), ATT_Q)


def attn_a_fwd(q, k, v, biasm, *, name):
    S = q.shape[0]
    nq = S // ATT_Q

    def body(q_ref, k_ref, v_ref, b_ref, o_ref):
        qb = pl.program_id(1)
        start = _a_window_start(qb)
        outs = []
        for h in range(2):
            lanes = slice(h * HEAD_DIM, (h + 1) * HEAD_DIM)
            qh = q_ref[:, lanes]
            kw = k_ref[pl.ds(start, A_WIN), lanes]
            vw = v_ref[pl.ds(start, A_WIN), lanes]
            s = _dot(qh, kw, 1, 1) * (1.0 / math.sqrt(HEAD_DIM)) + b_ref[0, h]
            m = jnp.max(s, axis=-1, keepdims=True)
            e = jnp.exp(s - m)
            p = e / jnp.sum(e, axis=-1, keepdims=True)
            outs.append(_dot(p.astype(BF16), vw, 1, 0))
        o_ref[...] = jnp.concatenate(outs, axis=1).astype(BF16)

    qspec = pl.BlockSpec((ATT_Q, 2 * HEAD_DIM), lambda hp, qb: (qb, hp))
    kvspec = pl.BlockSpec((S, 2 * HEAD_DIM), lambda hp, qb: (0, hp))
    bspec = pl.BlockSpec((1, 2, ATT_Q, A_WIN), lambda hp, qb: (jnp.minimum(qb, 2), hp, 0, 0))
    return pl.pallas_call(
        body,
        out_shape=jax.ShapeDtypeStruct((S, WIDTH), BF16),
        grid=(N_HEADS // 2, nq),
        in_specs=[qspec, kvspec, kvspec, bspec],
        out_specs=qspec,
        compiler_params=_cp(("parallel", "arbitrary")),
        name=name,
    )(q, k, v, biasm)


def attn_a_bwd(q, k, v, biasm, do, *, name):
    S = q.shape[0]
    nq = S // ATT_Q
    scale = 1.0 / math.sqrt(HEAD_DIM)

    def body(q_ref, k_ref, v_ref, b_ref, do_ref, dq_ref, dk_ref, dv_ref, db_ref):
        qb = pl.program_id(1)
        start = _a_window_start(qb)

        @pl.when(qb == 0)
        def _():
            dk_ref[...] = jnp.zeros_like(dk_ref)
            dv_ref[...] = jnp.zeros_like(dv_ref)

        @pl.when(qb <= 2)
        def _():
            db_ref[...] = jnp.zeros_like(db_ref)

        dqs = []
        for h in range(2):
            lanes = slice(h * HEAD_DIM, (h + 1) * HEAD_DIM)
            qh = q_ref[:, lanes]
            doh = do_ref[:, lanes]
            kw = k_ref[pl.ds(start, A_WIN), lanes]
            vw = v_ref[pl.ds(start, A_WIN), lanes]
            s = _dot(qh, kw, 1, 1) * scale + b_ref[0, h]
            m = jnp.max(s, axis=-1, keepdims=True)
            e = jnp.exp(s - m)
            p = e / jnp.sum(e, axis=-1, keepdims=True)
            dp = _dot(doh, vw, 1, 1)
            delta = jnp.sum(p * dp, axis=-1, keepdims=True)
            ds = p * (dp - delta)
            db_ref[0, h] += ds
            dsb = ds.astype(BF16)
            dqs.append(_dot(dsb, kw, 1, 0) * scale)
            dk_ref[pl.ds(start, A_WIN), lanes] += _dot(dsb, qh, 0, 0) * scale
            dv_ref[pl.ds(start, A_WIN), lanes] += _dot(p.astype(BF16), doh, 0, 0)
        dq_ref[...] = jnp.concatenate(dqs, axis=1)

    qspec = pl.BlockSpec((ATT_Q, 2 * HEAD_DIM), lambda hp, qb: (qb, hp))
    kvspec = pl.BlockSpec((S, 2 * HEAD_DIM), lambda hp, qb: (0, hp))
    bspec = pl.BlockSpec((1, 2, ATT_Q, A_WIN), lambda hp, qb: (jnp.minimum(qb, 2), hp, 0, 0))
    return pl.pallas_call(
        body,
        out_shape=(jax.ShapeDtypeStruct((S, WIDTH), F32), jax.ShapeDtypeStruct((S, WIDTH), F32),
                   jax.ShapeDtypeStruct((S, WIDTH), F32), jax.ShapeDtypeStruct((3, N_HEADS, ATT_Q, A_WIN), F32)),
        grid=(N_HEADS // 2, nq),
        in_specs=[qspec, kvspec, kvspec, bspec, qspec],
        out_specs=(qspec, kvspec, kvspec, bspec),
        compiler_params=_cp(("parallel", "arbitrary")),
        name=name,
    )(q, k, v, biasm, do)


def _tri(kind):
    j = lax.broadcasted_iota(jnp.int32, (ATT_Q, ATT_Q), 0)
    s = lax.broadcasted_iota(jnp.int32, (ATT_Q, ATT_Q), 1)
    if kind == "gt":
        m = j > s
    elif kind == "le":
        m = j <= s
    else:
        m = j < s
    return jnp.where(m, 1.0, 0.0).astype(BF16)


def _cum(v, tri):
    hi, lo = _split_hi_lo(v)
    return _dot(hi, tri, 1, 0) + _dot(lo, tri, 1, 0)


def _log_keep(z, mask):
    sp = jnp.maximum(z, 0.0) + jnp.log(1.0 + jnp.exp(-jnp.abs(z)))
    return jnp.where(mask, -sp, 0.0)


def _before(qb, kb):
    row = lax.broadcasted_iota(jnp.int32, (ATT_Q, ATT_Q), 0)
    col = lax.broadcasted_iota(jnp.int32, (ATT_Q, ATT_Q), 1)
    return (col - row) < (qb - kb) * ATT_Q


def attn_b_fwd(q, k, v, *, name):
    S = q.shape[0]
    nq = S // ATT_Q
    scale = 1.0 / math.sqrt(HEAD_DIM)

    def body(q_ref, k_ref, v_ref, o_ref, t_ref):
        qb = pl.program_id(1)
        tri = _tri("gt")

        def step(it, carry):
            kb = qb - it
            ks = pl.multiple_of(kb * ATT_Q, ATT_Q)
            mask = _before(qb, kb)
            new = []
            for h in range(2):
                lanes = slice(h * HEAD_DIM, (h + 1) * HEAD_DIM)
                c, acc = carry[h]
                z = _dot(q_ref[:, lanes], k_ref[pl.ds(ks, ATT_Q), lanes], 1, 1) * scale
                L = _log_keep(z, mask)
                tail = _cum(L, tri) + c
                w = jnp.where(mask, jnp.exp(z + L + tail), 0.0)
                acc = acc + _dot(w.astype(BF16), v_ref[pl.ds(ks, ATT_Q), lanes], 1, 0)
                c = c + jnp.sum(L, axis=-1, keepdims=True)
                new.append((c, acc))
            return tuple(new)

        init = tuple((jnp.zeros((ATT_Q, 1), F32), jnp.zeros((ATT_Q, HEAD_DIM), F32)) for _ in range(2))
        res = lax.fori_loop(0, qb + 1, step, init)
        o_ref[...] = jnp.concatenate([res[0][1], res[1][1]], axis=1).astype(BF16)
        t_ref[...] = jnp.concatenate([jnp.broadcast_to(res[h][0], (ATT_Q, HEAD_DIM)) for h in range(2)], axis=1)

    qspec = pl.BlockSpec((ATT_Q, 2 * HEAD_DIM), lambda hp, qb: (qb, hp))
    kvspec = pl.BlockSpec((S, 2 * HEAD_DIM), lambda hp, qb: (0, hp))
    return pl.pallas_call(
        body,
        out_shape=(jax.ShapeDtypeStruct((S, WIDTH), BF16), jax.ShapeDtypeStruct((S, WIDTH), F32)),
        grid=(N_HEADS // 2, nq),
        in_specs=[qspec, kvspec, kvspec],
        out_specs=(qspec, qspec),
        compiler_params=_cp(("parallel", "arbitrary")),
        name=name,
    )(q, k, v)


def attn_b_bwd(q, k, v, tot, do, *, name):
    S = q.shape[0]
    nq = S // ATT_Q
    scale = 1.0 / math.sqrt(HEAD_DIM)

    def body(q_ref, k_ref, v_ref, t_ref, do_ref, dq_ref, dk_ref, dv_ref):
        qb = pl.program_id(1)
        tri_le = _tri("le")
        tri_lt = _tri("lt")

        @pl.when(qb == 0)
        def _():
            dk_ref[...] = jnp.zeros_like(dk_ref)
            dv_ref[...] = jnp.zeros_like(dv_ref)

        def step(kb, carry):
            ks = pl.multiple_of(kb * ATT_Q, ATT_Q)
            mask = _before(qb, kb)
            new = []
            for h in range(2):
                lanes = slice(h * HEAD_DIM, (h + 1) * HEAD_DIM)
                cl, cg, dq = carry[h]
                qh = q_ref[:, lanes]
                doh = do_ref[:, lanes]
                kh = k_ref[pl.ds(ks, ATT_Q), lanes]
                vh = v_ref[pl.ds(ks, ATT_Q), lanes]
                totl = t_ref[:, h * HEAD_DIM:h * HEAD_DIM + 1]
                z = _dot(qh, kh, 1, 1) * scale
                L = _log_keep(z, mask)
                tail = totl - (_cum(L, tri_le) + cl)
                sig = jnp.exp(z + L)
                w = jnp.where(mask, sig * jnp.exp(tail), 0.0)
                g = w * _dot(doh, vh, 1, 1)
                G = _cum(g, tri_lt) + cg
                dz = jnp.where(mask, g * (1.0 - sig) - sig * G, 0.0).astype(BF16)
                dq = dq + _dot(dz, kh, 1, 0)
                dk_ref[pl.ds(ks, ATT_Q), lanes] += _dot(dz, qh, 0, 0) * scale
                dv_ref[pl.ds(ks, ATT_Q), lanes] += _dot(w.astype(BF16), doh, 0, 0)
                cl = cl + jnp.sum(L, axis=-1, keepdims=True)
                cg = cg + jnp.sum(g, axis=-1, keepdims=True)
                new.append((cl, cg, dq))
            return tuple(new)

        init = tuple((jnp.zeros((ATT_Q, 1), F32), jnp.zeros((ATT_Q, 1), F32), jnp.zeros((ATT_Q, HEAD_DIM), F32))
                     for _ in range(2))
        res = lax.fori_loop(0, qb + 1, step, init)
        dq_ref[...] = (jnp.concatenate([res[0][2], res[1][2]], axis=1) * scale).astype(BF16)

    qspec = pl.BlockSpec((ATT_Q, 2 * HEAD_DIM), lambda hp, qb: (qb, hp))
    kvspec = pl.BlockSpec((S, 2 * HEAD_DIM), lambda hp, qb: (0, hp))
    return pl.pallas_call(
        body,
        out_shape=(jax.ShapeDtypeStruct((S, WIDTH), BF16), jax.ShapeDtypeStruct((S, WIDTH), F32),
                   jax.ShapeDtypeStruct((S, WIDTH), F32)),
        grid=(N_HEADS // 2, nq),
        in_specs=[qspec, kvspec, kvspec, qspec, qspec],
        out_specs=(qspec, kvspec, kvspec),
        compiler_params=_cp(("parallel", "arbitrary")),
        name=name,
    )(q, k, v, tot, do)


U_COLBLK = 6


def _pool_counts(t0, rows):
    t = t0 + lax.broadcasted_iota(jnp.int32, (rows, WIDTH), 0)
    lane_grp = lax.broadcasted_iota(jnp.int32, (rows, WIDTH), 1) // GROUP_DIM
    win = jnp.where(lane_grp == 0, 2, jnp.where(lane_grp == 1, 4, jnp.where(lane_grp == 2, 8, 16)))
    cnt = jnp.minimum(t + 1, win)
    return 1.0 / cnt.astype(F32), lane_grp


def _window_sums(ext, shift_fn):
    s2 = ext + shift_fn(ext, 1)
    s4 = s2 + shift_fn(s2, 2)
    s8 = s4 + shift_fn(s4, 4)
    s16 = s8 + shift_fn(s8, 8)
    return s2, s4, s8, s16


def _select_group(lane_grp, s2, s4, s8, s16):
    return jnp.where(lane_grp == 0, s2, jnp.where(lane_grp == 1, s4, jnp.where(lane_grp == 2, s8, s16)))


def _pooled_tile(u_ref, h_ref, i, T):
    halo = jnp.where(i > 0, h_ref[...], 0.0)
    ext = jnp.concatenate([halo, u_ref[...]], axis=0)
    n = T + HALO
    sums = _window_sums(ext, lambda v, k: pltpu.roll(v, k, 0))
    inv, lane_grp = _pool_counts(i * T - HALO, n)
    pooled = _select_group(lane_grp, *sums) * inv - ext
    return pooled[HALO:, :]


def pool_fwd(proj, w_pool, scale, *, name):
    S = proj.shape[0]
    T = _tile(S)
    hb = T // HALO

    def body(u_ref, h_ref, w_ref, s_ref, o_ref):
        i = pl.program_id(0)
        pooled = _pooled_tile(u_ref, h_ref, i, T).astype(BF16)
        outs = [_dot(pooled[:, g * GROUP_DIM:(g + 1) * GROUP_DIM], w_ref[g], 1, 0) for g in range(4)]
        o_ref[...] = (jnp.concatenate(outs, axis=1) * s_ref[...]).astype(BF16)

    return pl.pallas_call(
        body,
        out_shape=jax.ShapeDtypeStruct((S, WIDTH), BF16),
        grid=(S // T,),
        in_specs=[pl.BlockSpec((T, WIDTH), lambda i: (i, U_COLBLK)),
                  pl.BlockSpec((HALO, WIDTH), lambda i: (jnp.maximum(i * hb - 1, 0), U_COLBLK)),
                  pl.BlockSpec((4, GROUP_DIM, GROUP_DIM), lambda i: (0, 0, 0)),
                  pl.BlockSpec((1, WIDTH), lambda i: (0, 0))],
        out_specs=pl.BlockSpec((T, WIDTH), lambda i: (i, 0)),
        compiler_params=_cp(("parallel",)),
        name=name,
    )(proj, proj, w_pool, scale)


def pool_bwd(proj, w_pool, scale, do, *, name):
    S = proj.shape[0]
    T = _tile(S)
    hb = T // HALO
    nt = S // T

    def body(u_ref, h_ref, w_ref, s_ref, do_ref, dof_ref, du_ref, dw_ref, ds_ref):
        i = pl.program_id(0)

        @pl.when(i == 0)
        def _():
            dw_ref[...] = jnp.zeros_like(dw_ref)
            ds_ref[...] = jnp.zeros_like(ds_ref)

        pooled = _pooled_tile(u_ref, h_ref, i, T).astype(BF16)
        dov = do_ref[...].astype(F32)
        fut = jnp.where(i < nt - 1, dof_ref[...].astype(F32), 0.0)
        dmix = (jnp.concatenate([dov, fut], axis=0) * s_ref[...]).astype(BF16)
        mixed, dpool = [], []
        for g in range(4):
            lanes = slice(g * GROUP_DIM, (g + 1) * GROUP_DIM)
            mixed.append(_dot(pooled[:, lanes], w_ref[g], 1, 0))
            dw_ref[g] += _dot(pooled[:, lanes], dmix[:T, lanes], 0, 0)
            dpool.append(_dot(dmix[:, lanes], w_ref[g], 1, 1))
        ds_ref[...] += jnp.sum(dov * jnp.concatenate(mixed, axis=1), axis=0, keepdims=True)
        dp = jnp.concatenate(dpool, axis=1)
        n = T + HALO
        inv, lane_grp = _pool_counts(i * T, n)
        sums = _window_sums(dp * inv, lambda v, k: pltpu.roll(v, n - k, 0))
        du = _select_group(lane_grp, *sums) - dp
        du_ref[...] = du[:T, :].astype(BF16)

    row = pl.BlockSpec((T, WIDTH), lambda i: (i, 0))
    return pl.pallas_call(
        body,
        out_shape=(jax.ShapeDtypeStruct((S, WIDTH), BF16), jax.ShapeDtypeStruct((4, GROUP_DIM, GROUP_DIM), F32),
                   jax.ShapeDtypeStruct((1, WIDTH), F32)),
        grid=(nt,),
        in_specs=[pl.BlockSpec((T, WIDTH), lambda i: (i, U_COLBLK)),
                  pl.BlockSpec((HALO, WIDTH), lambda i: (jnp.maximum(i * hb - 1, 0), U_COLBLK)),
                  pl.BlockSpec((4, GROUP_DIM, GROUP_DIM), lambda i: (0, 0, 0)),
                  pl.BlockSpec((1, WIDTH), lambda i: (0, 0)),
                  row,
                  pl.BlockSpec((HALO, WIDTH), lambda i: (jnp.minimum((i + 1) * hb, S // HALO - 1), 0))],
        out_specs=(row, pl.BlockSpec((4, GROUP_DIM, GROUP_DIM), lambda i: (0, 0, 0)),
                   pl.BlockSpec((1, WIDTH), lambda i: (0, 0))),
        compiler_params=_cp(("arbitrary",)),
        name=name,
    )(proj, proj, w_pool, scale, do, do)


GATE_BLK0 = GATE_COL0 // WIDTH


def merge_fwd(oa, ob, oc, proj, b_gate, wa, wb, wc, *, name):
    S = oa.shape[0]
    T = _tile(S)

    def body(oa_ref, ob_ref, oc_ref, ga, gb, gc, ba, bb, bc, wa_ref, wb_ref, wc_ref, m_ref):
        acc = None
        for o_ref, g_ref, b_ref, w_ref in ((oa_ref, ga, ba, wa_ref), (ob_ref, gb, bb, wb_ref), (oc_ref, gc, bc, wc_ref)):
            y = _dot(o_ref[...], w_ref[...], 1, 0)
            t = jax.nn.sigmoid(g_ref[...] + b_ref[...]) * y
            acc = t if acc is None else acc + t
        m_ref[...] = acc.astype(BF16)

    row = pl.BlockSpec((T, WIDTH), lambda i, n: (i, 0))
    gate = lambda b: pl.BlockSpec((T, WIDTH), lambda i, n, b=b: (i, GATE_BLK0 + 2 * b + n))
    bias = lambda b: pl.BlockSpec((1, WIDTH), lambda i, n, b=b: (0, 2 * b + n))
    wspec = pl.BlockSpec((WIDTH, WIDTH), lambda i, n: (0, n))
    return pl.pallas_call(
        body,
        out_shape=jax.ShapeDtypeStruct((S, D_MODEL), BF16),
        grid=(S // T, 2),
        in_specs=[row, row, row, gate(0), gate(1), gate(2), bias(0), bias(1), bias(2), wspec, wspec, wspec],
        out_specs=pl.BlockSpec((T, WIDTH), lambda i, n: (i, n)),
        compiler_params=_cp(("parallel", "parallel")),
        name=name,
    )(oa, ob, oc, proj, proj, proj, b_gate, b_gate, b_gate, wa, wb, wc)


def merge_bwd(dm, oa, ob, oc, proj, b_gate, wa, wb, wc, *, name):
    S = oa.shape[0]
    T = _tile(S)

    def body(dm_ref, oa_ref, ob_ref, oc_ref, ga, gb, gc, ba, bb, bc, wa_ref, wb_ref, wc_ref,
             ta, tb, tc, dga, dgb, dgc, dba, dbb, dbc):
        i = pl.program_id(1)
        dmv = dm_ref[...].astype(F32)
        for o_ref, g_ref, b_ref, w_ref, t_ref, dg_ref, db_ref in (
                (oa_ref, ga, ba, wa_ref, ta, dga, dba), (ob_ref, gb, bb, wb_ref, tb, dgb, dbb),
                (oc_ref, gc, bc, wc_ref, tc, dgc, dbc)):
            y = _dot(o_ref[...], w_ref[...], 1, 0)
            gate = jax.nn.sigmoid(g_ref[...] + b_ref[...])
            t_ref[...] = (gate * dmv).astype(BF16)
            dgl = dmv * y * gate * (1.0 - gate)
            dg_ref[...] = dgl.astype(BF16)

            @pl.when(i == 0)
            def _():
                db_ref[...] = jnp.zeros_like(db_ref)

            db_ref[...] += jnp.sum(dgl, axis=0, keepdims=True)

    row = pl.BlockSpec((T, WIDTH), lambda n, i: (i, 0))
    half = pl.BlockSpec((T, WIDTH), lambda n, i: (i, n))
    gate = lambda b: pl.BlockSpec((T, WIDTH), lambda n, i, b=b: (i, GATE_BLK0 + 2 * b + n))
    bias = lambda b: pl.BlockSpec((1, WIDTH), lambda n, i, b=b: (0, 2 * b + n))
    wspec = pl.BlockSpec((WIDTH, WIDTH), lambda n, i: (0, n))
    bvec = pl.BlockSpec((1, WIDTH), lambda n, i: (0, n))
    act = jax.ShapeDtypeStruct((S, D_MODEL), BF16)
    vec = jax.ShapeDtypeStruct((1, D_MODEL), F32)
    return pl.pallas_call(
        body,
        out_shape=(act, act, act, act, act, act, vec, vec, vec),
        grid=(2, S // T),
        in_specs=[half, row, row, row, gate(0), gate(1), gate(2), bias(0), bias(1), bias(2), wspec, wspec, wspec],
        out_specs=(half, half, half, half, half, half, bvec, bvec, bvec),
        compiler_params=_cp(("parallel", "arbitrary")),
        name=name,
    )(dm, oa, ob, oc, proj, proj, proj, b_gate, b_gate, b_gate, wa, wb, wc)


FF_T = 256
FF_BLKS = D_FF // FF_T


def _silu_parts(x):
    s = jax.nn.sigmoid(x)
    return x * s, s


def _conv3(ext, w_ref, b_ref):
    return (b_ref[...] + w_ref[0:1, :] * pltpu.roll(ext, 2, 0) + w_ref[1:2, :] * pltpu.roll(ext, 1, 0)
            + w_ref[2:3, :] * ext)


def conv_glu_fwd(u, conv_w, conv_b, *, name):
    S = u.shape[0]
    T = _tile(S)
    hb = T // CONV_HALO

    def body(ug, ugh, uv, uvh, wg, wv, bg, bv, a_ref):
        i = pl.program_id(1)
        cs = []
        for m_ref, h_ref, w_ref, b_ref in ((ug, ugh, wg, bg), (uv, uvh, wv, bv)):
            halo = jnp.where(i > 0, h_ref[...], 0.0)
            ext = jnp.concatenate([halo, m_ref[...]], axis=0)
            cs.append(_conv3(ext, w_ref, b_ref)[CONV_HALO:, :])
        act, _ = _silu_parts(cs[0])
        a_ref[...] = (act * cs[1]).astype(BF16)

    main = lambda o: pl.BlockSpec((T, FF_T), lambda c, i, o=o: (i, c + o))
    halo = lambda o: pl.BlockSpec((CONV_HALO, FF_T), lambda c, i, o=o: (jnp.maximum(i * hb - 1, 0), c + o))
    wsp = lambda o: pl.BlockSpec((3, FF_T), lambda c, i, o=o: (0, c + o))
    bsp = lambda o: pl.BlockSpec((1, FF_T), lambda c, i, o=o: (0, c + o))
    return pl.pallas_call(
        body,
        out_shape=jax.ShapeDtypeStruct((S, D_FF), BF16),
        grid=(FF_BLKS, S // T),
        in_specs=[main(0), halo(0), main(FF_BLKS), halo(FF_BLKS), wsp(0), wsp(FF_BLKS), bsp(0), bsp(FF_BLKS)],
        out_specs=pl.BlockSpec((T, FF_T), lambda c, i: (i, c)),
        compiler_params=_cp(("parallel", "parallel")),
        name=name,
    )(u, u, u, u, conv_w, conv_w, conv_b, conv_b)


def conv_glu_bwd(u, conv_w, conv_b, da, *, name):
    S = u.shape[0]
    T = _tile(S)
    hb = T // CONV_HALO
    nt = S // T
    n = T + 2 * CONV_HALO

    def body(ug, ugp, ugf, uv, uvp, uvf, wg, wv, bg, bv, da_ref, daf_ref,
             dug, duv, dwg, dwv, dbg, dbv):
        i = pl.program_id(1)
        first, last = i == 0, i == nt - 1
        exts, cs = [], []
        for m_ref, p_ref, f_ref, w_ref, b_ref in ((ug, ugp, ugf, wg, bg), (uv, uvp, uvf, wv, bv)):
            ext = jnp.concatenate([jnp.where(first, 0.0, p_ref[...]), m_ref[...], jnp.where(last, 0.0, f_ref[...])], axis=0)
            exts.append(ext)
            cs.append(_conv3(ext, w_ref, b_ref))
        dae = jnp.concatenate([jnp.zeros((CONV_HALO, FF_T), F32), da_ref[...].astype(F32),
                               jnp.where(last, 0.0, daf_ref[...].astype(F32))], axis=0)
        act, sg = _silu_parts(cs[0])
        dcs = (dae * cs[1] * (sg * (1.0 + cs[0] * (1.0 - sg))), dae * act)
        main = slice(CONV_HALO, CONV_HALO + T)
        for ext, dc, w_ref, du_ref, dw_ref, db_ref in ((exts[0], dcs[0], wg, dug, dwg, dbg),
                                                       (exts[1], dcs[1], wv, duv, dwv, dbv)):
            du = (w_ref[2:3, :] * dc + w_ref[1:2, :] * pltpu.roll(dc, n - 1, 0) + w_ref[0:1, :] * pltpu.roll(dc, n - 2, 0))
            du_ref[...] = du[main, :].astype(BF16)
            dcm = dc[main, :]
            rows = [jnp.sum(dcm * pltpu.roll(ext, 2 - j, 0)[main, :], axis=0, keepdims=True) if j < 2
                    else jnp.sum(dcm * ext[main, :], axis=0, keepdims=True) for j in range(3)]

            @pl.when(first)
            def _():
                dw_ref[...] = jnp.zeros_like(dw_ref)
                db_ref[...] = jnp.zeros_like(db_ref)

            dw_ref[...] += jnp.concatenate(rows, axis=0)
            db_ref[...] += jnp.sum(dcm, axis=0, keepdims=True)

    main = lambda o: pl.BlockSpec((T, FF_T), lambda c, i, o=o: (i, c + o))
    past = lambda o: pl.BlockSpec((CONV_HALO, FF_T), lambda c, i, o=o: (jnp.maximum(i * hb - 1, 0), c + o))
    fut = lambda o: pl.BlockSpec((CONV_HALO, FF_T), lambda c, i, o=o: (jnp.minimum((i + 1) * hb, S // CONV_HALO - 1), c + o))
    wsp = lambda o: pl.BlockSpec((3, FF_T), lambda c, i, o=o: (0, c + o))
    bsp = lambda o: pl.BlockSpec((1, FF_T), lambda c, i, o=o: (0, c + o))
    return pl.pallas_call(
        body,
        out_shape=(jax.ShapeDtypeStruct((S, D_FF), BF16), jax.ShapeDtypeStruct((S, D_FF), BF16),
                   jax.ShapeDtypeStruct((3, D_FF), F32), jax.ShapeDtypeStruct((3, D_FF), F32),
                   jax.ShapeDtypeStruct((1, D_FF), F32), jax.ShapeDtypeStruct((1, D_FF), F32)),
        grid=(FF_BLKS, nt),
        in_specs=[main(0), past(0), fut(0), main(FF_BLKS), past(FF_BLKS), fut(FF_BLKS),
                  wsp(0), wsp(FF_BLKS), bsp(0), bsp(FF_BLKS), main(0), fut(0)],
        out_specs=(main(0), main(0), wsp(0), wsp(0), bsp(0), bsp(0)),
        compiler_params=_cp(("parallel", "arbitrary")),
        name=name,
    )(u, u, u, u, u, u, conv_w, conv_w, conv_b, conv_b, da, da)


def loss_head(y, target, *, name):
    S, D = y.shape
    T = _tile(S)

    def body(y_ref, t_ref, dy_ref, l_ref):
        i = pl.program_id(0)
        err = y_ref[...] - t_ref[...]
        dy_ref[...] = err * (1.0 / D)

        @pl.when(i == 0)
        def _():
            l_ref[...] = jnp.zeros_like(l_ref)

        l_ref[...] += 0.5 * jnp.sum(jnp.mean(err * err, axis=-1, keepdims=True))

    row = pl.BlockSpec((T, D), lambda i: (i, 0))
    return pl.pallas_call(
        body,
        out_shape=(jax.ShapeDtypeStruct((S, D), F32), jax.ShapeDtypeStruct((8, 128), F32)),
        grid=(S // T,),
        in_specs=[row, row],
        out_specs=(row, pl.BlockSpec((8, 128), lambda i: (0, 0))),
        compiler_params=_cp(("arbitrary",)),
        name=name,
    )(y, target)


ELEMS_PER_BLOCK = 256 * 1024


def _rows_tile(rows, cols):
    if rows * cols <= ELEMS_PER_BLOCK or rows % 8:
        return rows
    best = 8
    for tr in range(8, rows + 1, 8):
        if rows % tr == 0 and tr * cols <= ELEMS_PER_BLOCK:
            best = tr
    return best


def sum_parts(parts, order, *, name):
    P, rows, cols = parts.shape
    tr = _rows_tile(rows, cols)

    def body(p_ref, o_ref):
        acc = p_ref[order[0]]
        for j in order[1:]:
            acc = acc + p_ref[j]
        o_ref[...] = acc

    return pl.pallas_call(
        body,
        out_shape=jax.ShapeDtypeStruct((rows, cols), F32),
        grid=(rows // tr,),
        in_specs=[pl.BlockSpec((P, tr, cols), lambda i: (0, i, 0))],
        out_specs=pl.BlockSpec((tr, cols), lambda i: (i, 0)),
        compiler_params=_cp(("parallel",)),
        name=name,
    )(parts)


def adamw(w, m, v, g_parts, *, name):
    rows, cols = w.shape
    tr = _rows_tile(rows, cols)
    n_parts = len(g_parts)

    def body(*refs):
        w_ref, m_ref, v_ref = refs[:3]
        g_refs = refs[3:3 + n_parts]
        g_out, d_out, m_out, v_out = refs[3 + n_parts:]
        g = g_refs[0][...]
        for r in g_refs[1:]:
            g = g + r[...]
        mn = ADAM_B1 * m_ref[...] + (1.0 - ADAM_B1) * g
        vn = ADAM_B2 * v_ref[...] + (1.0 - ADAM_B2) * (g * g)
        m_hat = mn / (1.0 - ADAM_B1 ** ADAM_STEP)
        v_hat = vn / (1.0 - ADAM_B2 ** ADAM_STEP)
        g_out[...] = g
        d_out[...] = -ADAM_LR * (m_hat / (jnp.sqrt(v_hat) + ADAM_EPS) + ADAM_WD * w_ref[...])
        m_out[...] = mn
        v_out[...] = vn

    spec = pl.BlockSpec((tr, cols), lambda i: (i, 0))
    shp = jax.ShapeDtypeStruct((rows, cols), F32)
    return pl.pallas_call(
        body,
        out_shape=(shp, shp, shp, shp),
        grid=(rows // tr,),
        in_specs=[spec] * (3 + n_parts),
        out_specs=(spec, spec, spec, spec),
        compiler_params=_cp(("parallel",)),
        name=name,
    )(w, m, v, *g_parts)


ANY = pl.BlockSpec(memory_space=pl.ANY)


def _mesh_pos():
    return lax.axis_index("x"), lax.axis_index("y"), lax.axis_index("c")


def _chip_peers(x, y):
    return [(1 - x, y), (x, 1 - y), (1 - x, 1 - y)]


def _all_peers(x, y, c):
    return [((1 - x) if (r >> 2) & 1 else x, (1 - y) if (r >> 1) & 1 else y, (1 - c) if r & 1 else c)
            for r in range(1, 8)]


def _shard_slice(ref, axis, j, size):
    idx = [slice(None)] * 3
    idx[axis] = pl.ds(pl.multiple_of(j * size, 128 if axis == 2 else 16), size)
    return ref.at[tuple(idx)]


def all_gather_shards(shards, axes, *, name):
    nt = len(shards)
    sizes = [s.shape[a] for s, a in zip(shards, axes)]

    def body(*refs):
        ins, outs = refs[:nt], refs[nt:2 * nt]
        send_sems, recv_sems, local_sems = refs[2 * nt:]
        x, y, c = _mesh_pos()
        mine = 2 * x + y
        local, sends, recvs = [], [], []
        for t in range(nt):
            cp = pltpu.make_async_copy(ins[t], _shard_slice(outs[t], axes[t], mine, sizes[t]), local_sems.at[t])
            cp.start()
            local.append(cp)
            for k, (px, py) in enumerate(_chip_peers(x, y)):
                def mk(blk, t=t, k=k, px=px, py=py):
                    return pltpu.make_async_remote_copy(
                        src_ref=ins[t], dst_ref=_shard_slice(outs[t], axes[t], blk, sizes[t]),
                        send_sem=send_sems.at[3 * t + k], recv_sem=recv_sems.at[3 * t + k],
                        device_id=(px, py, c), device_id_type=MESH_T)
                snd = mk(mine)
                snd.start()
                sends.append(snd)
                recvs.append(mk(2 * px + py))
        for r in recvs:
            r.wait_recv()
        for s in sends:
            s.wait_send()
        for cp in local:
            cp.wait()

    out_shape = []
    for s, a, sz in zip(shards, axes, sizes):
        shp = list(s.shape)
        shp[a] = 4 * sz
        out_shape.append(jax.ShapeDtypeStruct(tuple(shp), s.dtype))
    return pl.pallas_call(
        body,
        out_shape=tuple(out_shape),
        in_specs=[ANY] * nt,
        out_specs=tuple([ANY] * nt),
        scratch_shapes=[pltpu.SemaphoreType.DMA((3 * nt,)), pltpu.SemaphoreType.DMA((3 * nt,)),
                        pltpu.SemaphoreType.DMA((nt,))],
        name=name,
    )(*shards)


def scatter_grads(grads, axes, small, *, name):
    nt = len(grads)
    sizes = [g.shape[a] // 4 for g, a in zip(grads, axes)]

    def body(*refs):
        ins, small_in = refs[:nt], refs[nt]
        outs, small_out = refs[nt + 1:2 * nt + 1], refs[2 * nt + 1]
        send_sems, recv_sems, local_sems, ssend, srecv = refs[2 * nt + 2:]
        x, y, c = _mesh_pos()
        mine = 2 * x + y
        me = 4 * x + 2 * y + c
        local, sends, recvs = [], [], []
        for t in range(nt):
            cp = pltpu.make_async_copy(_shard_slice(ins[t], axes[t], mine, sizes[t]), outs[t].at[3], local_sems.at[t])
            cp.start()
            local.append(cp)
            for k, (px, py) in enumerate(_chip_peers(x, y)):
                snd = pltpu.make_async_remote_copy(
                    src_ref=_shard_slice(ins[t], axes[t], 2 * px + py, sizes[t]), dst_ref=outs[t].at[k],
                    send_sem=send_sems.at[3 * t + k], recv_sem=recv_sems.at[3 * t + k],
                    device_id=(px, py, c), device_id_type=MESH_T)
                snd.start()
                sends.append(snd)
                recvs.append(snd)
        cp = pltpu.make_async_copy(small_in, small_out.at[me], local_sems.at[nt])
        cp.start()
        local.append(cp)
        for r, (px, py, pc) in enumerate(_all_peers(x, y, c)):
            def mk(slot, r=r, px=px, py=py, pc=pc):
                return pltpu.make_async_remote_copy(
                    src_ref=small_in, dst_ref=small_out.at[slot], send_sem=ssend.at[r], recv_sem=srecv.at[r],
                    device_id=(px, py, pc), device_id_type=MESH_T)
            snd = mk(me)
            snd.start()
            sends.append(snd)
            recvs.append(mk(4 * px + 2 * py + pc))
        for r in recvs:
            r.wait_recv()
        for s in sends:
            s.wait_send()
        for cp in local:
            cp.wait()

    out_shape = []
    for g, a, sz in zip(grads, axes, sizes):
        shp = list(g.shape)
        shp[a] = sz
        out_shape.append(jax.ShapeDtypeStruct((4,) + tuple(shp), F32))
    out_shape.append(jax.ShapeDtypeStruct((8,) + small.shape, F32))
    return pl.pallas_call(
        body,
        out_shape=tuple(out_shape),
        in_specs=[ANY] * (nt + 1),
        out_specs=tuple([ANY] * (nt + 1)),
        scratch_shapes=[pltpu.SemaphoreType.DMA((3 * nt,)), pltpu.SemaphoreType.DMA((3 * nt,)),
                        pltpu.SemaphoreType.DMA((nt + 1,)), pltpu.SemaphoreType.DMA((7,)),
                        pltpu.SemaphoreType.DMA((7,))],
        name=name,
    )(*grads, small)


def sibling_exchange(parts, *, name):
    nt = len(parts)

    def body(*refs):
        ins, outs = refs[:nt], refs[nt:2 * nt]
        send_sems, recv_sems = refs[2 * nt:]
        x, y, c = _mesh_pos()
        cps = []
        for t in range(nt):
            cp = pltpu.make_async_remote_copy(src_ref=ins[t], dst_ref=outs[t], send_sem=send_sems.at[t],
                                              recv_sem=recv_sems.at[t], device_id=(x, y, 1 - c), device_id_type=MESH_T)
            cp.start()
            cps.append(cp)
        for cp in cps:
            cp.wait_recv()
        for cp in cps:
            cp.wait_send()

    return pl.pallas_call(
        body,
        out_shape=tuple(jax.ShapeDtypeStruct(p.shape, p.dtype) for p in parts),
        in_specs=[ANY] * nt,
        out_specs=tuple([ANY] * nt),
        scratch_shapes=[pltpu.SemaphoreType.DMA((nt,)), pltpu.SemaphoreType.DMA((nt,))],
        name=name,
    )(*parts)


WEIGHTS = ("norm_mix", "w_in", "b_gate", "q_norm_a", "k_norm_a", "rel_bias_a", "w_pool", "pool_scale",
           "w_branch_a", "w_branch_b", "w_branch_c", "w_out", "norm_ffn", "w_up", "conv_w", "conv_b", "w_down")
SHARDED = {"w_in": 2, "w_branch_a": 2, "w_branch_b": 2, "w_branch_c": 2, "w_out": 1, "w_up": 2, "conv_w": 2,
           "w_down": 1}
REPLICATED = tuple(n for n in WEIGHTS if n not in SHARDED)
SMALL_ROWS = 1232


def _layer_fwd(x, p, tables):
    diag = exact_dot(p["rel_bias_a"], tables["onehot_t"], name="bias_diagonals")
    diag = diag.reshape(N_HEADS, N_VARIANTS, 1, DIAG_W).transpose(1, 0, 2, 3)
    biasm = bias_expand(diag, name="bias_expand")
    gq8 = jnp.tile(p["q_norm_a"], N_HEADS)[None]
    gk8 = jnp.tile(p["k_norm_a"], N_HEADS)[None]
    h = rmsnorm_fwd(x, p["norm_mix"][None], name="rmsnorm_fwd")
    proj = matmul(h, p["w_in"], name="mm_in")
    qa, ka, va, qb, kb, vb = qkv_prep(proj, gq8, gk8, name="qkv_prep")
    oa = attn_a_fwd(qa, ka, va, biasm, name="attn_a_fwd")
    ob, tot = attn_b_fwd(qb, kb, vb, name="attn_b_fwd")
    wpool = p["w_pool"].astype(BF16)
    oc = pool_fwd(proj, wpool, p["pool_scale"][None], name="pool_fwd")
    merged = merge_fwd(oa, ob, oc, proj, p["b_gate"][None], p["w_branch_a"], p["w_branch_b"], p["w_branch_c"],
                       name="merge_fwd")
    x1 = matmul(merged, p["w_out"], add=x, name="mm_out")
    h2 = rmsnorm_fwd(x1, p["norm_ffn"][None], name="rmsnorm_fwd")
    u = matmul(h2, p["w_up"], name="mm_up")
    a = conv_glu_fwd(u, p["conv_w"], p["conv_b"][None], name="conv_glu_fwd")
    x2 = matmul(a, p["w_down"], add=x1, name="mm_down")
    saved = dict(x=x, h=h, proj=proj, qa=qa, ka=ka, va=va, qb=qb, kb=kb, vb=vb, oa=oa, ob=ob, tot=tot, oc=oc,
                 merged=merged, x1=x1, h2=h2, u=u, a=a, biasm=biasm, gq8=gq8, gk8=gk8, wpool=wpool)
    return x2, saved


def _layer_bwd(dx2, s, p, tables):
    g = {}
    da = matmul(dx2, p["w_down"], tb=True, name="mm_down_dx")
    g["w_down"] = matmul(s["a"], dx2, ta=True, name="mm_down_dw")
    dug, duv, dcwg, dcwv, dcbg, dcbv = conv_glu_bwd(s["u"], p["conv_w"], p["conv_b"][None], da, name="conv_glu_bwd")
    du = jnp.concatenate([dug, duv], axis=1)
    g["conv_w"] = jnp.concatenate([dcwg, dcwv], axis=1)
    g["conv_b"] = jnp.concatenate([dcbg, dcbv], axis=1)[0]
    g["w_up"] = matmul(s["h2"], du, ta=True, name="mm_up_dw")
    dh2 = matmul(du, p["w_up"], tb=True, name="mm_up_dx")
    dx1, dg2 = rmsnorm_bwd(s["x1"], p["norm_ffn"][None], dh2, dx2, name="rmsnorm_bwd")
    g["norm_ffn"] = dg2[0]
    dmerged = matmul(dx1, p["w_out"], tb=True, name="mm_out_dx")
    g["w_out"] = matmul(s["merged"], dx1, ta=True, name="mm_out_dw")
    t_a, t_b, t_c, dga, dgb, dgc, dba, dbb, dbc = merge_bwd(
        dmerged, s["oa"], s["ob"], s["oc"], s["proj"], p["b_gate"][None], p["w_branch_a"], p["w_branch_b"],
        p["w_branch_c"], name="merge_bwd")
    g["b_gate"] = jnp.concatenate([dba, dbb, dbc], axis=1)[0]
    g["w_branch_a"] = matmul(s["oa"], t_a, ta=True, name="mm_branch_dw")
    g["w_branch_b"] = matmul(s["ob"], t_b, ta=True, name="mm_branch_dw")
    g["w_branch_c"] = matmul(s["oc"], t_c, ta=True, name="mm_branch_dw")
    doa = matmul(t_a, p["w_branch_a"], tb=True, out_dtype=BF16, name="mm_branch_dx")
    dob = matmul(t_b, p["w_branch_b"], tb=True, out_dtype=BF16, name="mm_branch_dx")
    doc = matmul(t_c, p["w_branch_c"], tb=True, name="mm_branch_dx_f32")
    dqh, dkh, dva, dbias = attn_a_bwd(s["qa"], s["ka"], s["va"], s["biasm"], doa, name="attn_a_bwd")
    ddiag = relbias_reduce(dbias, name="relbias_reduce")
    ddiag = ddiag.transpose(1, 0, 2, 3).reshape(N_HEADS, N_VARIANTS * DIAG_W)
    g["rel_bias_a"] = exact_dot(ddiag, tables["onehot"], name="relbias_table")
    dqa, dka, dgq8, dgk8 = qknorm_bwd(s["proj"], s["gq8"], s["gk8"], dqh, dkh, name="qknorm_bwd")
    g["q_norm_a"] = dgq8.reshape(N_HEADS, HEAD_DIM).sum(axis=0)
    g["k_norm_a"] = dgk8.reshape(N_HEADS, HEAD_DIM).sum(axis=0)
    dqb, dkb, dvb = attn_b_bwd(s["qb"], s["kb"], s["vb"], s["tot"], dob, name="attn_b_bwd")
    duc, dwp, dsc = pool_bwd(s["proj"], s["wpool"], p["pool_scale"][None], doc, name="pool_bwd")
    g["w_pool"] = dwp
    g["pool_scale"] = dsc[0]
    dproj = jnp.concatenate([dqa, dka, dva.astype(BF16), dqb, dkb.astype(BF16), dvb.astype(BF16), duc,
                             dga, dgb, dgc], axis=1)
    g["w_in"] = matmul(s["h"], dproj, ta=True, name="mm_in_dw")
    dh = matmul(dproj, p["w_in"], tb=True, name="mm_in_dx")
    dx, dg1 = rmsnorm_bwd(s["x"], p["norm_mix"][None], dh, dx1, name="rmsnorm_bwd")
    g["norm_mix"] = dg1[0]
    return dx, g


def kernel(x, norm_mix, w_in, b_gate, q_norm_a, k_norm_a, rel_bias_a, w_pool, pool_scale, w_branch_a, w_branch_b, w_branch_c, w_out, norm_ffn, w_up, conv_w, conv_b, w_down, loss_target, m_norm_mix, m_w_in, m_b_gate, m_q_norm_a, m_k_norm_a, m_rel_bias_a, m_w_pool, m_pool_scale, m_w_branch_a, m_w_branch_b, m_w_branch_c, m_w_out, m_norm_ffn, m_w_up, m_conv_w, m_conv_b, m_w_down, v_norm_mix, v_w_in, v_b_gate, v_q_norm_a, v_k_norm_a, v_rel_bias_a, v_w_pool, v_pool_scale, v_w_branch_a, v_w_branch_b, v_w_branch_c, v_w_out, v_norm_ffn, v_w_up, v_conv_w, v_conv_b, v_w_down):
    w = dict(zip(WEIGHTS, (norm_mix, w_in, b_gate, q_norm_a, k_norm_a, rel_bias_a, w_pool, pool_scale, w_branch_a,
                           w_branch_b, w_branch_c, w_out, norm_ffn, w_up, conv_w, conv_b, w_down)))
    m = dict(zip(WEIGHTS, (m_norm_mix, m_w_in, m_b_gate, m_q_norm_a, m_k_norm_a, m_rel_bias_a, m_w_pool, m_pool_scale,
                           m_w_branch_a, m_w_branch_b, m_w_branch_c, m_w_out, m_norm_ffn, m_w_up, m_conv_w, m_conv_b,
                           m_w_down)))
    v = dict(zip(WEIGHTS, (v_norm_mix, v_w_in, v_b_gate, v_q_norm_a, v_k_norm_a, v_rel_bias_a, v_w_pool, v_pool_scale,
                           v_w_branch_a, v_w_branch_b, v_w_branch_c, v_w_out, v_norm_ffn, v_w_up, v_conv_w, v_conv_b,
                           v_w_down)))
    onehot = diagonal_onehot()
    tables = dict(onehot=jnp.asarray(onehot), onehot_t=jnp.asarray(np.ascontiguousarray(onehot.T)))

    names = tuple(SHARDED)
    shards = [w[n] if n == "conv_w" else w[n].astype(BF16) for n in names]
    full = dict(zip(names, all_gather_shards(shards, [SHARDED[n] for n in names], name="all_gather_weights")))

    def layer_params(l):
        p = {n: full[n][l] for n in names}
        p.update({n: w[n][l] for n in REPLICATED})
        return p

    xs = x[0]
    saved = []
    for l in range(DEPTH):
        xs, s = _layer_fwd(xs, layer_params(l), tables)
        saved.append(s)
    dx, lpart = loss_head(xs, loss_target[0], name="loss_head")
    loss = lax.psum(lpart[0, 0], MESH_AXES)
    grads = [None] * DEPTH
    for l in reversed(range(DEPTH)):
        dx, grads[l] = _layer_bwd(dx, saved[l], layer_params(l), tables)
    g = {n: jnp.stack([grads[l][n] for l in range(DEPTH)]) for n in WEIGHTS}

    flat = jnp.concatenate([g[n].reshape(-1) for n in REPLICATED])
    small = jnp.pad(flat, (0, SMALL_ROWS * 128 - flat.shape[0])).reshape(SMALL_ROWS, 128)
    outs = scatter_grads([g[n] for n in names], [SHARDED[n] for n in names], small, name="scatter_grads")
    sums = []
    for n, buf in zip(names, outs[:-1]):
        rows = buf.shape[1] * buf.shape[2]
        sums.append(sum_parts(buf.reshape(4, rows, buf.shape[3]), (3, 0, 1, 2), name="sum_chips"))
    others = sibling_exchange(sums, name="sibling_exchange")
    small_sum = sum_parts(outs[-1], tuple(range(8)), name="sum_devices").reshape(-1)

    res = {}
    for n, mine, other in zip(names, sums, others):
        shp = w[n].shape
        two_d = lambda t: t.reshape(mine.shape)
        res[n] = [t.reshape(shp) for t in adamw(two_d(w[n]), two_d(m[n]), two_d(v[n]), [mine, other], name="adamw")]
    off = 0
    for n in REPLICATED:
        shp = w[n].shape
        size = int(np.prod(shp))
        gn = small_sum[off:off + size]
        off += size
        cols = shp[-1]
        two_d = lambda t: t.reshape(size // cols, cols)
        res[n] = [t.reshape(shp) for t in adamw(two_d(w[n]), two_d(m[n]), two_d(v[n]), [two_d(gn)], name="adamw")]

    out = [loss, dx[None]]
    for k in range(4):
        out.extend(res[n][k] for n in WEIGHTS)
    return tuple(out)
```

```python
import functools
import math

import jax
import jax.numpy as jnp
import numpy as np
from jax import lax
from jax.experimental import pallas as pl
from jax.experimental.pallas import tpu as pltpu

F32 = jnp.float32
BF16 = jnp.bfloat16

D_MODEL = 1024
DEPTH = 2
CHUNK = 64
N_LEFT = 8
HEAD_DIM = 64
N_HEADS = 8
WIDTH = 512
POOL_WINDOWS = (2, 4, 8, 16)
GROUP_DIM = 128
MAX_REL = 2 * CHUNK
REL_TABLE = MAX_REL + CHUNK
D_FF = 2816
EPS = 1e-6
IN_COLS = 7 * WIDTH + 3 * D_MODEL
GATE_COL0 = 7 * WIDTH

ADAM_LR = 0.001
ADAM_B1 = 0.9
ADAM_B2 = 0.999
ADAM_EPS = 1e-08
ADAM_WD = 0.01
ADAM_STEP = 10

VMEM_LIMIT = 56 * 1024 * 1024
ATT_Q = 256
A_WIN = ATT_Q + N_LEFT * CHUNK
HALO = 16
CONV_HALO = 8
NEG = -1e30

MESH_AXES = ("x", "y", "c")
MESH_T = pl.DeviceIdType.MESH


def _cp(sem=None, vmem=VMEM_LIMIT):
    return pltpu.CompilerParams(dimension_semantics=sem, vmem_limit_bytes=vmem)


def _dot(a, b, ca, cb):
    return lax.dot_general(a, b, (((ca,), (cb,)), ((), ())), preferred_element_type=F32)


def _tile(n, cands=(512, 256, 128)):
    for c in cands:
        if n % c == 0:
            return c
    return n


def _split_hi_lo(v):
    hi = v.astype(BF16)
    lo = (v - hi.astype(F32)).astype(BF16)
    return hi, lo


def matmul(a, b, *, ta=False, tb=False, add=None, out_dtype=F32, name):
    if ta:
        K, M = a.shape
    else:
        M, K = a.shape
    if tb:
        N, K2 = b.shape
    else:
        K2, N = b.shape
    assert K == K2, (a.shape, b.shape, ta, tb)
    big = (1024, 1408, 512, 256, 128)
    tm, tn, tk = _tile(M, big), _tile(N, big), _tile(K, big)
    nk = K // tk

    def body(*refs):
        if add is None:
            a_ref, b_ref, o_ref, acc = refs
        else:
            a_ref, b_ref, r_ref, o_ref, acc = refs
        k = pl.program_id(2)

        @pl.when(k == 0)
        def _():
            acc[...] = jnp.zeros_like(acc)

        av = a_ref[...].astype(BF16)
        bv = b_ref[...].astype(BF16)
        acc[...] += _dot(av, bv, 0 if ta else 1, 1 if tb else 0)

        @pl.when(k == nk - 1)
        def _():
            r = acc[...]
            if add is not None:
                r = r + r_ref[...].astype(F32)
            o_ref[...] = r.astype(out_dtype)

    a_spec = pl.BlockSpec((tk, tm), lambda i, j, k: (k, i)) if ta else pl.BlockSpec((tm, tk), lambda i, j, k: (i, k))
    b_spec = pl.BlockSpec((tn, tk), lambda i, j, k: (j, k)) if tb else pl.BlockSpec((tk, tn), lambda i, j, k: (k, j))
    o_spec = pl.BlockSpec((tm, tn), lambda i, j, k: (i, j))
    in_specs = [a_spec, b_spec]
    args = [a, b]
    if add is not None:
        in_specs.append(o_spec)
        args.append(add)
    return pl.pallas_call(
        body,
        out_shape=jax.ShapeDtypeStruct((M, N), out_dtype),
        grid=(M // tm, N // tn, nk),
        in_specs=in_specs,
        out_specs=o_spec,
        scratch_shapes=[pltpu.VMEM((tm, tn), F32)],
        compiler_params=_cp(("parallel", "parallel", "arbitrary")),
        name=name,
    )(*args)


def rmsnorm_fwd(x, g, *, name):
    S, D = x.shape
    T = _tile(S)

    def body(x_ref, g_ref, h_ref):
        xv = x_ref[...]
        r = lax.rsqrt(jnp.mean(xv * xv, axis=-1, keepdims=True) + EPS)
        h_ref[...] = (xv * r * g_ref[...]).astype(BF16)

    return pl.pallas_call(
        body,
        out_shape=jax.ShapeDtypeStruct((S, D), BF16),
        grid=(S // T,),
        in_specs=[pl.BlockSpec((T, D), lambda i: (i, 0)), pl.BlockSpec((1, D), lambda i: (0, 0))],
        out_specs=pl.BlockSpec((T, D), lambda i: (i, 0)),
        compiler_params=_cp(("parallel",)),
        name=name,
    )(x, g)


def rmsnorm_bwd(x, g, dh, dres, *, name):
    S, D = x.shape
    T = _tile(S)

    def body(x_ref, g_ref, dh_ref, dres_ref, dx_ref, dg_ref):
        i = pl.program_id(0)
        xv = x_ref[...]
        dhv = dh_ref[...].astype(F32)
        r = lax.rsqrt(jnp.mean(xv * xv, axis=-1, keepdims=True) + EPS)
        gd = dhv * g_ref[...]
        m = jnp.mean(xv * gd, axis=-1, keepdims=True)
        dx_ref[...] = dres_ref[...] + r * gd - xv * (r * r * r * m)

        @pl.when(i == 0)
        def _():
            dg_ref[...] = jnp.zeros_like(dg_ref)

        dg_ref[...] += jnp.sum(dhv * xv * r, axis=0, keepdims=True)

    row = pl.BlockSpec((T, D), lambda i: (i, 0))
    vec = pl.BlockSpec((1, D), lambda i: (0, 0))
    return pl.pallas_call(
        body,
        out_shape=(jax.ShapeDtypeStruct((S, D), F32), jax.ShapeDtypeStruct((1, D), F32)),
        grid=(S // T,),
        in_specs=[row, vec, row, row],
        out_specs=(row, vec),
        compiler_params=_cp(("arbitrary",)),
        name=name,
    )(x, g, dh, dres)


def _head_mean_matrix():
    r = lax.broadcasted_iota(jnp.int32, (WIDTH, WIDTH), 0) // HEAD_DIM
    c = lax.broadcasted_iota(jnp.int32, (WIDTH, WIDTH), 1) // HEAD_DIM
    return jnp.where(r == c, 1.0 / HEAD_DIM, 0.0).astype(BF16)


def _head_mean(v, mm):
    hi, lo = _split_hi_lo(v)
    return _dot(hi, mm, 1, 0) + _dot(lo, mm, 1, 0)


def qkv_prep(proj, gq, gk, *, name):
    S = proj.shape[0]
    T = _tile(S)

    def body(qa, ka, va, qb, kb, vb, gq_ref, gk_ref, oqa, oka, ova, oqb, okb, ovb):
        mm = _head_mean_matrix()
        for src, gref, dst in ((qa, gq_ref, oqa), (ka, gk_ref, oka)):
            v = src[...]
            r = lax.rsqrt(_head_mean(v * v, mm) + EPS)
            dst[...] = (v * r * gref[...]).astype(BF16)
        for src, dst in ((va, ova), (qb, oqb), (kb, okb), (vb, ovb)):
            dst[...] = src[...].astype(BF16)

    col = lambda j: pl.BlockSpec((T, WIDTH), lambda i, j=j: (i, j))
    vec = pl.BlockSpec((1, WIDTH), lambda i: (0, 0))
    out = pl.BlockSpec((T, WIDTH), lambda i: (i, 0))
    return pl.pallas_call(
        body,
        out_shape=tuple(jax.ShapeDtypeStruct((S, WIDTH), BF16) for _ in range(6)),
        grid=(S // T,),
        in_specs=[col(0), col(1), col(2), col(3), col(4), col(5), vec, vec],
        out_specs=tuple(out for _ in range(6)),
        compiler_params=_cp(("parallel",)),
        name=name,
    )(proj, proj, proj, proj, proj, proj, gq, gk)


def qknorm_bwd(proj, gq, gk, dqh, dkh, *, name):
    S = proj.shape[0]
    T = _tile(S)

    def body(qa, ka, gq_ref, gk_ref, dq_ref, dk_ref, oq, ok, ogq, ogk):
        i = pl.program_id(0)
        mm = _head_mean_matrix()

        @pl.when(i == 0)
        def _():
            ogq[...] = jnp.zeros_like(ogq)
            ogk[...] = jnp.zeros_like(ogk)

        for src, gref, dref, dst, gdst in ((qa, gq_ref, dq_ref, oq, ogq), (ka, gk_ref, dk_ref, ok, ogk)):
            v = src[...]
            dy = dref[...]
            r = lax.rsqrt(_head_mean(v * v, mm) + EPS)
            gd = dy * gref[...]
            m = _head_mean(v * gd, mm)
            dst[...] = (r * gd - v * (r * r * r * m)).astype(BF16)
            gdst[...] += jnp.sum(dy * v * r, axis=0, keepdims=True)

    col = lambda j: pl.BlockSpec((T, WIDTH), lambda i, j=j: (i, j))
    vec = pl.BlockSpec((1, WIDTH), lambda i: (0, 0))
    row = pl.BlockSpec((T, WIDTH), lambda i: (i, 0))
    return pl.pallas_call(
        body,
        out_shape=(jax.ShapeDtypeStruct((S, WIDTH), BF16), jax.ShapeDtypeStruct((S, WIDTH), BF16),
                   jax.ShapeDtypeStruct((1, WIDTH), F32), jax.ShapeDtypeStruct((1, WIDTH), F32)),
        grid=(S // T,),
        in_specs=[col(0), col(1), vec, vec, row, row],
        out_specs=(row, row, vec, vec),
        compiler_params=_cp(("arbitrary",)),
        name=name,
    )(proj, proj, gq, gk, dqh, dkh)


DIAG_W = 1024
N_VARIANTS = 3


def diagonal_onehot():
    jj = np.arange(DIAG_W)
    diff = np.where(jj < A_WIN, jj, jj - DIAG_W)
    out = np.zeros((N_VARIANTS, DIAG_W, REL_TABLE), np.float32)
    for v in range(N_VARIANTS):
        rel = np.clip(ATT_Q * v - diff, -(CHUNK - 1), MAX_REL) + (CHUNK - 1)
        out[v, jj, rel] = 1.0
    return out.reshape(N_VARIANTS * DIAG_W, REL_TABLE)


def exact_dot(a, b, *, name):
    def body(a_ref, b_ref, o_ref):
        o_ref[...] = jnp.dot(a_ref[...], b_ref[...], precision=lax.Precision.HIGHEST, preferred_element_type=F32)

    return pl.pallas_call(body, out_shape=jax.ShapeDtypeStruct((a.shape[0], b.shape[1]), F32),
                          compiler_params=_cp(), name=name)(a, b)


def _band_valid(v):
    qc = (lax.broadcasted_iota(jnp.int32, (ATT_Q, A_WIN), 0) + ATT_Q * v) // CHUNK
    kc = lax.broadcasted_iota(jnp.int32, (ATT_Q, A_WIN), 1) // CHUNK
    return (kc <= qc) & (kc >= qc - N_LEFT)


def bias_expand(diag, *, name):
    def body(d_ref, o_ref):
        rows = jnp.broadcast_to(d_ref[0, 0], (ATT_Q, DIAG_W))
        skew = pltpu.roll(rows, 0, 1, stride=1, stride_axis=0)
        o_ref[0, 0] = jnp.where(_band_valid(pl.program_id(0)), skew[:, :A_WIN], NEG)

    return pl.pallas_call(
        body,
        out_shape=jax.ShapeDtypeStruct((N_VARIANTS, N_HEADS, ATT_Q, A_WIN), F32),
        grid=(N_VARIANTS, N_HEADS),
        in_specs=[pl.BlockSpec((1, 1, 1, DIAG_W), lambda v, h: (v, h, 0, 0))],
        out_specs=pl.BlockSpec((1, 1, ATT_Q, A_WIN), lambda v, h: (v, h, 0, 0)),
        compiler_params=_cp(("parallel", "parallel")),
        name=name,
    )(diag)


def relbias_reduce(dbias, *, name):
    def body(db_ref, o_ref):
        x = jnp.concatenate([db_ref[0, 0], jnp.zeros((ATT_Q, DIAG_W - A_WIN), F32)], axis=1)
        row = lax.broadcasted_iota(jnp.int32, (ATT_Q, DIAG_W), 0)
        for b in range(8):
            x = jnp.where((row >> b) & 1 == 1, pltpu.roll(x, DIAG_W - (1 << b), 1), x)
        o_ref[0, 0] = jnp.sum(x, axis=0, keepdims=True)

    return pl.pallas_call(
        body,
        out_shape=jax.ShapeDtypeStruct((N_VARIANTS, N_HEADS, 1, DIAG_W), F32),
        grid=(N_VARIANTS, N_HEADS),
        in_specs=[pl.BlockSpec((1, 1, ATT_Q, A_WIN), lambda v, h: (v, h, 0, 0))],
        out_specs=pl.BlockSpec((1, 1, 1, DIAG_W), lambda v, h: (v, h, 0, 0)),
        compiler_params=_cp(("parallel", "parallel")),
        name=name,
    )(dbias)


def _a_window_start(qb):
    return pl.multiple_of(jnp.maximum(qb * ATT_Q - N_LEFT * CHUNK, 0), ATT_Q)


def attn_a_fwd(q, k, v, biasm, *, name):
    S = q.shape[0]
    nq = S // ATT_Q

    def body(q_ref, k_ref, v_ref, b_ref, o_ref):
        qb = pl.program_id(1)
        start = _a_window_start(qb)
        outs = []
        for h in range(2):
            lanes = slice(h * HEAD_DIM, (h + 1) * HEAD_DIM)
            qh = q_ref[:, lanes]
            kw = k_ref[pl.ds(start, A_WIN), lanes]
            vw = v_ref[pl.ds(start, A_WIN), lanes]
            s = _dot(qh, kw, 1, 1) * (1.0 / math.sqrt(HEAD_DIM)) + b_ref[0, h]
            m = jnp.max(s, axis=-1, keepdims=True)
            e = jnp.exp(s - m)
            p = e / jnp.sum(e, axis=-1, keepdims=True)
            outs.append(_dot(p.astype(BF16), vw, 1, 0))
        o_ref[...] = jnp.concatenate(outs, axis=1).astype(BF16)

    qspec = pl.BlockSpec((ATT_Q, 2 * HEAD_DIM), lambda hp, qb: (qb, hp))
    kvspec = pl.BlockSpec((S, 2 * HEAD_DIM), lambda hp, qb: (0, hp))
    bspec = pl.BlockSpec((1, 2, ATT_Q, A_WIN), lambda hp, qb: (jnp.minimum(qb, 2), hp, 0, 0))
    return pl.pallas_call(
        body,
        out_shape=jax.ShapeDtypeStruct((S, WIDTH), BF16),
        grid=(N_HEADS // 2, nq),
        in_specs=[qspec, kvspec, kvspec, bspec],
        out_specs=qspec,
        compiler_params=_cp(("parallel", "arbitrary")),
        name=name,
    )(q, k, v, biasm)


def attn_a_bwd(q, k, v, biasm, do, *, name):
    S = q.shape[0]
    nq = S // ATT_Q
    scale = 1.0 / math.sqrt(HEAD_DIM)

    def body(q_ref, k_ref, v_ref, b_ref, do_ref, dq_ref, dk_ref, dv_ref, db_ref):
        qb = pl.program_id(1)
        start = _a_window_start(qb)

        @pl.when(qb == 0)
        def _():
            dk_ref[...] = jnp.zeros_like(dk_ref)
            dv_ref[...] = jnp.zeros_like(dv_ref)

        @pl.when(qb <= 2)
        def _():
            db_ref[...] = jnp.zeros_like(db_ref)

        dqs = []
        for h in range(2):
            lanes = slice(h * HEAD_DIM, (h + 1) * HEAD_DIM)
            qh = q_ref[:, lanes]
            doh = do_ref[:, lanes]
            kw = k_ref[pl.ds(start, A_WIN), lanes]
            vw = v_ref[pl.ds(start, A_WIN), lanes]
            s = _dot(qh, kw, 1, 1) * scale + b_ref[0, h]
            m = jnp.max(s, axis=-1, keepdims=True)
            e = jnp.exp(s - m)
            p = e / jnp.sum(e, axis=-1, keepdims=True)
            dp = _dot(doh, vw, 1, 1)
            delta = jnp.sum(p * dp, axis=-1, keepdims=True)
            ds = p * (dp - delta)
            db_ref[0, h] += ds
            dsb = ds.astype(BF16)
            dqs.append(_dot(dsb, kw, 1, 0) * scale)
            dk_ref[pl.ds(start, A_WIN), lanes] += _dot(dsb, qh, 0, 0) * scale
            dv_ref[pl.ds(start, A_WIN), lanes] += _dot(p.astype(BF16), doh, 0, 0)
        dq_ref[...] = jnp.concatenate(dqs, axis=1)

    qspec = pl.BlockSpec((ATT_Q, 2 * HEAD_DIM), lambda hp, qb: (qb, hp))
    kvspec = pl.BlockSpec((S, 2 * HEAD_DIM), lambda hp, qb: (0, hp))
    bspec = pl.BlockSpec((1, 2, ATT_Q, A_WIN), lambda hp, qb: (jnp.minimum(qb, 2), hp, 0, 0))
    return pl.pallas_call(
        body,
        out_shape=(jax.ShapeDtypeStruct((S, WIDTH), F32), jax.ShapeDtypeStruct((S, WIDTH), F32),
                   jax.ShapeDtypeStruct((S, WIDTH), F32), jax.ShapeDtypeStruct((3, N_HEADS, ATT_Q, A_WIN), F32)),
        grid=(N_HEADS // 2, nq),
        in_specs=[qspec, kvspec, kvspec, bspec, qspec],
        out_specs=(qspec, kvspec, kvspec, bspec),
        compiler_params=_cp(("parallel", "arbitrary")),
        name=name,
    )(q, k, v, biasm, do)


def _tri(kind):
    j = lax.broadcasted_iota(jnp.int32, (ATT_Q, ATT_Q), 0)
    s = lax.broadcasted_iota(jnp.int32, (ATT_Q, ATT_Q), 1)
    if kind == "gt":
        m = j > s
    elif kind == "le":
        m = j <= s
    else:
        m = j < s
    return jnp.where(m, 1.0, 0.0).astype(BF16)


def _cum(v, tri):
    hi, lo = _split_hi_lo(v)
    return _dot(hi, tri, 1, 0) + _dot(lo, tri, 1, 0)


def _log_keep(z, mask):
    sp = jnp.maximum(z, 0.0) + jnp.log(1.0 + jnp.exp(-jnp.abs(z)))
    return jnp.where(mask, -sp, 0.0)


def _before(qb, kb):
    row = lax.broadcasted_iota(jnp.int32, (ATT_Q, ATT_Q), 0)
    col = lax.broadcasted_iota(jnp.int32, (ATT_Q, ATT_Q), 1)
    return (col - row) < (qb - kb) * ATT_Q


EXIT_LOG = -104.0


def attn_b_fwd(q, k, v, *, name):
    S = q.shape[0]
    nq = S // ATT_Q
    scale = 1.0 / math.sqrt(HEAD_DIM)

    def body(q_ref, k_ref, v_ref, o_ref, t_ref, n_ref):
        hp = pl.program_id(0)
        qb = pl.program_id(1)
        tri = _tri("gt")

        def cond(state):
            it, cmax, _ = state
            return jnp.logical_and(it <= qb, cmax >= EXIT_LOG)

        def step(state):
            it, _, carry = state
            kb = qb - it
            ks = pl.multiple_of(kb * ATT_Q, ATT_Q)
            mask = _before(qb, kb)
            new = []
            for h in range(2):
                lanes = slice(h * HEAD_DIM, (h + 1) * HEAD_DIM)
                c, acc = carry[h]
                z = _dot(q_ref[:, lanes], k_ref[pl.ds(ks, ATT_Q), lanes], 1, 1) * scale
                L = _log_keep(z, mask)
                tail = _cum(L, tri) + c
                w = jnp.where(mask, jnp.exp(z + L + tail), 0.0)
                acc = acc + _dot(w.astype(BF16), v_ref[pl.ds(ks, ATT_Q), lanes], 1, 0)
                c = c + jnp.sum(L, axis=-1, keepdims=True)
                new.append((c, acc))
            cmax = jnp.maximum(jnp.max(new[0][0]), jnp.max(new[1][0]))
            return it + 1, cmax, tuple(new)

        init = tuple((jnp.zeros((ATT_Q, 1), F32), jnp.zeros((ATT_Q, HEAD_DIM), F32)) for _ in range(2))
        visited, _, res = lax.while_loop(cond, step, (jnp.int32(0), jnp.float32(0.0), init))
        o_ref[...] = jnp.concatenate([res[0][1], res[1][1]], axis=1).astype(BF16)
        t_ref[...] = jnp.concatenate([jnp.broadcast_to(res[h][0], (ATT_Q, HEAD_DIM)) for h in range(2)], axis=1)
        n_ref[hp, qb] = visited.astype(F32)

    qspec = pl.BlockSpec((ATT_Q, 2 * HEAD_DIM), lambda hp, qb: (qb, hp))
    kvspec = pl.BlockSpec((S, 2 * HEAD_DIM), lambda hp, qb: (0, hp))
    return pl.pallas_call(
        body,
        out_shape=(jax.ShapeDtypeStruct((S, WIDTH), BF16), jax.ShapeDtypeStruct((S, WIDTH), F32),
                   jax.ShapeDtypeStruct((N_HEADS // 2, nq), F32)),
        grid=(N_HEADS // 2, nq),
        in_specs=[qspec, kvspec, kvspec],
        out_specs=(qspec, qspec, pl.BlockSpec(memory_space=pltpu.SMEM)),
        compiler_params=_cp(("arbitrary", "arbitrary")),
        name=name,
    )(q, k, v)


def attn_b_bwd(q, k, v, tot, nblk, do, *, name):
    S = q.shape[0]
    nq = S // ATT_Q
    scale = 1.0 / math.sqrt(HEAD_DIM)

    def body(q_ref, k_ref, v_ref, t_ref, n_ref, do_ref, dq_ref, dk_ref, dv_ref):
        hp = pl.program_id(0)
        qb = pl.program_id(1)
        first = jnp.clip(qb + 1 - n_ref[hp, qb].astype(jnp.int32), 0, qb + 1)
        tri_le = _tri("le")
        tri_lt = _tri("lt")

        @pl.when(qb == 0)
        def _():
            dk_ref[...] = jnp.zeros_like(dk_ref)
            dv_ref[...] = jnp.zeros_like(dv_ref)

        def step(kb, carry):
            ks = pl.multiple_of(kb * ATT_Q, ATT_Q)
            mask = _before(qb, kb)
            new = []
            for h in range(2):
                lanes = slice(h * HEAD_DIM, (h + 1) * HEAD_DIM)
                cl, cg, dq = carry[h]
                qh = q_ref[:, lanes]
                doh = do_ref[:, lanes]
                kh = k_ref[pl.ds(ks, ATT_Q), lanes]
                vh = v_ref[pl.ds(ks, ATT_Q), lanes]
                totl = t_ref[:, h * HEAD_DIM:h * HEAD_DIM + 1]
                z = _dot(qh, kh, 1, 1) * scale
                L = _log_keep(z, mask)
                tail = totl - (_cum(L, tri_le) + cl)
                sig = jnp.exp(z + L)
                w = jnp.where(mask, sig * jnp.exp(tail), 0.0)
                g = w * _dot(doh, vh, 1, 1)
                G = _cum(g, tri_lt) + cg
                dz = jnp.where(mask, g * (1.0 - sig) - sig * G, 0.0).astype(BF16)
                dq = dq + _dot(dz, kh, 1, 0)
                dk_ref[pl.ds(ks, ATT_Q), lanes] += _dot(dz, qh, 0, 0) * scale
                dv_ref[pl.ds(ks, ATT_Q), lanes] += _dot(w.astype(BF16), doh, 0, 0)
                cl = cl + jnp.sum(L, axis=-1, keepdims=True)
                cg = cg + jnp.sum(g, axis=-1, keepdims=True)
                new.append((cl, cg, dq))
            return tuple(new)

        init = tuple((jnp.zeros((ATT_Q, 1), F32), jnp.zeros((ATT_Q, 1), F32), jnp.zeros((ATT_Q, HEAD_DIM), F32))
                     for _ in range(2))
        res = lax.fori_loop(first, qb + 1, step, init)
        dq_ref[...] = (jnp.concatenate([res[0][2], res[1][2]], axis=1) * scale).astype(BF16)

    qspec = pl.BlockSpec((ATT_Q, 2 * HEAD_DIM), lambda hp, qb: (qb, hp))
    kvspec = pl.BlockSpec((S, 2 * HEAD_DIM), lambda hp, qb: (0, hp))
    return pl.pallas_call(
        body,
        out_shape=(jax.ShapeDtypeStruct((S, WIDTH), BF16), jax.ShapeDtypeStruct((S, WIDTH), F32),
                   jax.ShapeDtypeStruct((S, WIDTH), F32)),
        grid=(N_HEADS // 2, nq),
        in_specs=[qspec, kvspec, kvspec, qspec, pl.BlockSpec(memory_space=pltpu.SMEM), qspec],
        out_specs=(qspec, kvspec, kvspec),
        compiler_params=_cp(("parallel", "arbitrary")),
        name=name,
    )(q, k, v, tot, nblk, do)


U_COLBLK = 6


def _pool_counts(t0, rows):
    t = t0 + lax.broadcasted_iota(jnp.int32, (rows, WIDTH), 0)
    lane_grp = lax.broadcasted_iota(jnp.int32, (rows, WIDTH), 1) // GROUP_DIM
    win = jnp.where(lane_grp == 0, 2, jnp.where(lane_grp == 1, 4, jnp.where(lane_grp == 2, 8, 16)))
    cnt = jnp.minimum(t + 1, win)
    return 1.0 / cnt.astype(F32), lane_grp


def _window_sums(ext, shift_fn):
    s2 = ext + shift_fn(ext, 1)
    s4 = s2 + shift_fn(s2, 2)
    s8 = s4 + shift_fn(s4, 4)
    s16 = s8 + shift_fn(s8, 8)
    return s2, s4, s8, s16


def _select_group(lane_grp, s2, s4, s8, s16):
    return jnp.where(lane_grp == 0, s2, jnp.where(lane_grp == 1, s4, jnp.where(lane_grp == 2, s8, s16)))


def _pooled_tile(u_ref, h_ref, i, T):
    halo = jnp.where(i > 0, h_ref[...], 0.0)
    ext = jnp.concatenate([halo, u_ref[...]], axis=0)
    n = T + HALO
    sums = _window_sums(ext, lambda v, k: pltpu.roll(v, k, 0))
    inv, lane_grp = _pool_counts(i * T - HALO, n)
    pooled = _select_group(lane_grp, *sums) * inv - ext
    return pooled[HALO:, :]


def pool_fwd(proj, w_pool, scale, *, name):
    S = proj.shape[0]
    T = _tile(S)
    hb = T // HALO

    def body(u_ref, h_ref, w_ref, s_ref, o_ref):
        i = pl.program_id(0)
        pooled = _pooled_tile(u_ref, h_ref, i, T).astype(BF16)
        outs = [_dot(pooled[:, g * GROUP_DIM:(g + 1) * GROUP_DIM], w_ref[g], 1, 0) for g in range(4)]
        o_ref[...] = (jnp.concatenate(outs, axis=1) * s_ref[...]).astype(BF16)

    return pl.pallas_call(
        body,
        out_shape=jax.ShapeDtypeStruct((S, WIDTH), BF16),
        grid=(S // T,),
        in_specs=[pl.BlockSpec((T, WIDTH), lambda i: (i, U_COLBLK)),
                  pl.BlockSpec((HALO, WIDTH), lambda i: (jnp.maximum(i * hb - 1, 0), U_COLBLK)),
                  pl.BlockSpec((4, GROUP_DIM, GROUP_DIM), lambda i: (0, 0, 0)),
                  pl.BlockSpec((1, WIDTH), lambda i: (0, 0))],
        out_specs=pl.BlockSpec((T, WIDTH), lambda i: (i, 0)),
        compiler_params=_cp(("parallel",)),
        name=name,
    )(proj, proj, w_pool, scale)


def pool_bwd(proj, w_pool, scale, do, *, name):
    S = proj.shape[0]
    T = _tile(S)
    hb = T // HALO
    nt = S // T

    def body(u_ref, h_ref, w_ref, s_ref, do_ref, dof_ref, du_ref, dw_ref, ds_ref):
        i = pl.program_id(0)

        @pl.when(i == 0)
        def _():
            dw_ref[...] = jnp.zeros_like(dw_ref)
            ds_ref[...] = jnp.zeros_like(ds_ref)

        pooled = _pooled_tile(u_ref, h_ref, i, T).astype(BF16)
        dov = do_ref[...].astype(F32)
        fut = jnp.where(i < nt - 1, dof_ref[...].astype(F32), 0.0)
        dmix = (jnp.concatenate([dov, fut], axis=0) * s_ref[...]).astype(BF16)
        mixed, dpool = [], []
        for g in range(4):
            lanes = slice(g * GROUP_DIM, (g + 1) * GROUP_DIM)
            mixed.append(_dot(pooled[:, lanes], w_ref[g], 1, 0))
            dw_ref[g] += _dot(pooled[:, lanes], dmix[:T, lanes], 0, 0)
            dpool.append(_dot(dmix[:, lanes], w_ref[g], 1, 1))
        ds_ref[...] += jnp.sum(dov * jnp.concatenate(mixed, axis=1), axis=0, keepdims=True)
        dp = jnp.concatenate(dpool, axis=1)
        n = T + HALO
        inv, lane_grp = _pool_counts(i * T, n)
        sums = _window_sums(dp * inv, lambda v, k: pltpu.roll(v, n - k, 0))
        du = _select_group(lane_grp, *sums) - dp
        du_ref[...] = du[:T, :].astype(BF16)

    row = pl.BlockSpec((T, WIDTH), lambda i: (i, 0))
    return pl.pallas_call(
        body,
        out_shape=(jax.ShapeDtypeStruct((S, WIDTH), BF16), jax.ShapeDtypeStruct((4, GROUP_DIM, GROUP_DIM), F32),
                   jax.ShapeDtypeStruct((1, WIDTH), F32)),
        grid=(nt,),
        in_specs=[pl.BlockSpec((T, WIDTH), lambda i: (i, U_COLBLK)),
                  pl.BlockSpec((HALO, WIDTH), lambda i: (jnp.maximum(i * hb - 1, 0), U_COLBLK)),
                  pl.BlockSpec((4, GROUP_DIM, GROUP_DIM), lambda i: (0, 0, 0)),
                  pl.BlockSpec((1, WIDTH), lambda i: (0, 0)),
                  row,
                  pl.BlockSpec((HALO, WIDTH), lambda i: (jnp.minimum((i + 1) * hb, S // HALO - 1), 0))],
        out_specs=(row, pl.BlockSpec((4, GROUP_DIM, GROUP_DIM), lambda i: (0, 0, 0)),
                   pl.BlockSpec((1, WIDTH), lambda i: (0, 0))),
        compiler_params=_cp(("arbitrary",)),
        name=name,
    )(proj, proj, w_pool, scale, do, do)


GATE_BLK0 = GATE_COL0 // WIDTH


def merge_fwd(oa, ob, oc, proj, b_gate, wa, wb, wc, *, name):
    S = oa.shape[0]
    T = _tile(S)

    def body(oa_ref, ob_ref, oc_ref, ga, gb, gc, ba, bb, bc, wa_ref, wb_ref, wc_ref, m_ref):
        acc = None
        for o_ref, g_ref, b_ref, w_ref in ((oa_ref, ga, ba, wa_ref), (ob_ref, gb, bb, wb_ref), (oc_ref, gc, bc, wc_ref)):
            y = _dot(o_ref[...], w_ref[...], 1, 0)
            t = jax.nn.sigmoid(g_ref[...] + b_ref[...]) * y
            acc = t if acc is None else acc + t
        m_ref[...] = acc.astype(BF16)

    row = pl.BlockSpec((T, WIDTH), lambda i, n: (i, 0))
    gate = lambda b: pl.BlockSpec((T, WIDTH), lambda i, n, b=b: (i, GATE_BLK0 + 2 * b + n))
    bias = lambda b: pl.BlockSpec((1, WIDTH), lambda i, n, b=b: (0, 2 * b + n))
    wspec = pl.BlockSpec((WIDTH, WIDTH), lambda i, n: (0, n))
    return pl.pallas_call(
        body,
        out_shape=jax.ShapeDtypeStruct((S, D_MODEL), BF16),
        grid=(S // T, 2),
        in_specs=[row, row, row, gate(0), gate(1), gate(2), bias(0), bias(1), bias(2), wspec, wspec, wspec],
        out_specs=pl.BlockSpec((T, WIDTH), lambda i, n: (i, n)),
        compiler_params=_cp(("parallel", "parallel")),
        name=name,
    )(oa, ob, oc, proj, proj, proj, b_gate, b_gate, b_gate, wa, wb, wc)


def merge_bwd(dm, oa, ob, oc, proj, b_gate, wa, wb, wc, *, name):
    S = oa.shape[0]
    T = _tile(S)

    def body(dm_ref, oa_ref, ob_ref, oc_ref, ga, gb, gc, ba, bb, bc, wa_ref, wb_ref, wc_ref,
             ta, tb, tc, dga, dgb, dgc, dba, dbb, dbc):
        i = pl.program_id(1)
        dmv = dm_ref[...].astype(F32)
        for o_ref, g_ref, b_ref, w_ref, t_ref, dg_ref, db_ref in (
                (oa_ref, ga, ba, wa_ref, ta, dga, dba), (ob_ref, gb, bb, wb_ref, tb, dgb, dbb),
                (oc_ref, gc, bc, wc_ref, tc, dgc, dbc)):
            y = _dot(o_ref[...], w_ref[...], 1, 0)
            gate = jax.nn.sigmoid(g_ref[...] + b_ref[...])
            t_ref[...] = (gate * dmv).astype(BF16)
            dgl = dmv * y * gate * (1.0 - gate)
            dg_ref[...] = dgl.astype(BF16)

            @pl.when(i == 0)
            def _():
                db_ref[...] = jnp.zeros_like(db_ref)

            db_ref[...] += jnp.sum(dgl, axis=0, keepdims=True)

    row = pl.BlockSpec((T, WIDTH), lambda n, i: (i, 0))
    half = pl.BlockSpec((T, WIDTH), lambda n, i: (i, n))
    gate = lambda b: pl.BlockSpec((T, WIDTH), lambda n, i, b=b: (i, GATE_BLK0 + 2 * b + n))
    bias = lambda b: pl.BlockSpec((1, WIDTH), lambda n, i, b=b: (0, 2 * b + n))
    wspec = pl.BlockSpec((WIDTH, WIDTH), lambda n, i: (0, n))
    bvec = pl.BlockSpec((1, WIDTH), lambda n, i: (0, n))
    act = jax.ShapeDtypeStruct((S, D_MODEL), BF16)
    vec = jax.ShapeDtypeStruct((1, D_MODEL), F32)
    return pl.pallas_call(
        body,
        out_shape=(act, act, act, act, act, act, vec, vec, vec),
        grid=(2, S // T),
        in_specs=[half, row, row, row, gate(0), gate(1), gate(2), bias(0), bias(1), bias(2), wspec, wspec, wspec],
        out_specs=(half, half, half, half, half, half, bvec, bvec, bvec),
        compiler_params=_cp(("parallel", "arbitrary")),
        name=name,
    )(dm, oa, ob, oc, proj, proj, proj, b_gate, b_gate, b_gate, wa, wb, wc)


FF_T = 256
FF_BLKS = D_FF // FF_T


def _silu_parts(x):
    s = jax.nn.sigmoid(x)
    return x * s, s


def _conv3(ext, w_ref, b_ref):
    return (b_ref[...] + w_ref[0:1, :] * pltpu.roll(ext, 2, 0) + w_ref[1:2, :] * pltpu.roll(ext, 1, 0)
            + w_ref[2:3, :] * ext)


def conv_glu_fwd(u, conv_w, conv_b, *, name):
    S = u.shape[0]
    T = _tile(S)
    hb = T // CONV_HALO

    def body(ug, ugh, uv, uvh, wg, wv, bg, bv, a_ref):
        i = pl.program_id(1)
        cs = []
        for m_ref, h_ref, w_ref, b_ref in ((ug, ugh, wg, bg), (uv, uvh, wv, bv)):
            halo = jnp.where(i > 0, h_ref[...], 0.0)
            ext = jnp.concatenate([halo, m_ref[...]], axis=0)
            cs.append(_conv3(ext, w_ref, b_ref)[CONV_HALO:, :])
        act, _ = _silu_parts(cs[0])
        a_ref[...] = (act * cs[1]).astype(BF16)

    main = lambda o: pl.BlockSpec((T, FF_T), lambda c, i, o=o: (i, c + o))
    halo = lambda o: pl.BlockSpec((CONV_HALO, FF_T), lambda c, i, o=o: (jnp.maximum(i * hb - 1, 0), c + o))
    wsp = lambda o: pl.BlockSpec((3, FF_T), lambda c, i, o=o: (0, c + o))
    bsp = lambda o: pl.BlockSpec((1, FF_T), lambda c, i, o=o: (0, c + o))
    return pl.pallas_call(
        body,
        out_shape=jax.ShapeDtypeStruct((S, D_FF), BF16),
        grid=(FF_BLKS, S // T),
        in_specs=[main(0), halo(0), main(FF_BLKS), halo(FF_BLKS), wsp(0), wsp(FF_BLKS), bsp(0), bsp(FF_BLKS)],
        out_specs=pl.BlockSpec((T, FF_T), lambda c, i: (i, c)),
        compiler_params=_cp(("parallel", "parallel")),
        name=name,
    )(u, u, u, u, conv_w, conv_w, conv_b, conv_b)


def conv_glu_bwd(u, conv_w, conv_b, da, *, name):
    S = u.shape[0]
    T = _tile(S)
    hb = T // CONV_HALO
    nt = S // T
    n = T + 2 * CONV_HALO

    def body(ug, ugp, ugf, uv, uvp, uvf, wg, wv, bg, bv, da_ref, daf_ref,
             dug, duv, dwg, dwv, dbg, dbv):
        i = pl.program_id(1)
        first, last = i == 0, i == nt - 1
        exts, cs = [], []
        for m_ref, p_ref, f_ref, w_ref, b_ref in ((ug, ugp, ugf, wg, bg), (uv, uvp, uvf, wv, bv)):
            ext = jnp.concatenate([jnp.where(first, 0.0, p_ref[...]), m_ref[...], jnp.where(last, 0.0, f_ref[...])], axis=0)
            exts.append(ext)
            cs.append(_conv3(ext, w_ref, b_ref))
        dae = jnp.concatenate([jnp.zeros((CONV_HALO, FF_T), F32), da_ref[...].astype(F32),
                               jnp.where(last, 0.0, daf_ref[...].astype(F32))], axis=0)
        act, sg = _silu_parts(cs[0])
        dcs = (dae * cs[1] * (sg * (1.0 + cs[0] * (1.0 - sg))), dae * act)
        main = slice(CONV_HALO, CONV_HALO + T)
        for ext, dc, w_ref, du_ref, dw_ref, db_ref in ((exts[0], dcs[0], wg, dug, dwg, dbg),
                                                       (exts[1], dcs[1], wv, duv, dwv, dbv)):
            du = (w_ref[2:3, :] * dc + w_ref[1:2, :] * pltpu.roll(dc, n - 1, 0) + w_ref[0:1, :] * pltpu.roll(dc, n - 2, 0))
            du_ref[...] = du[main, :].astype(BF16)
            dcm = dc[main, :]
            rows = [jnp.sum(dcm * pltpu.roll(ext, 2 - j, 0)[main, :], axis=0, keepdims=True) if j < 2
                    else jnp.sum(dcm * ext[main, :], axis=0, keepdims=True) for j in range(3)]

            @pl.when(first)
            def _():
                dw_ref[...] = jnp.zeros_like(dw_ref)
                db_ref[...] = jnp.zeros_like(db_ref)

            dw_ref[...] += jnp.concatenate(rows, axis=0)
            db_ref[...] += jnp.sum(dcm, axis=0, keepdims=True)

    main = lambda o: pl.BlockSpec((T, FF_T), lambda c, i, o=o: (i, c + o))
    past = lambda o: pl.BlockSpec((CONV_HALO, FF_T), lambda c, i, o=o: (jnp.maximum(i * hb - 1, 0), c + o))
    fut = lambda o: pl.BlockSpec((CONV_HALO, FF_T), lambda c, i, o=o: (jnp.minimum((i + 1) * hb, S // CONV_HALO - 1), c + o))
    wsp = lambda o: pl.BlockSpec((3, FF_T), lambda c, i, o=o: (0, c + o))
    bsp = lambda o: pl.BlockSpec((1, FF_T), lambda c, i, o=o: (0, c + o))
    return pl.pallas_call(
        body,
        out_shape=(jax.ShapeDtypeStruct((S, D_FF), BF16), jax.ShapeDtypeStruct((S, D_FF), BF16),
                   jax.ShapeDtypeStruct((3, D_FF), F32), jax.ShapeDtypeStruct((3, D_FF), F32),
                   jax.ShapeDtypeStruct((1, D_FF), F32), jax.ShapeDtypeStruct((1, D_FF), F32)),
        grid=(FF_BLKS, nt),
        in_specs=[main(0), past(0), fut(0), main(FF_BLKS), past(FF_BLKS), fut(FF_BLKS),
                  wsp(0), wsp(FF_BLKS), bsp(0), bsp(FF_BLKS), main(0), fut(0)],
        out_specs=(main(0), main(0), wsp(0), wsp(0), bsp(0), bsp(0)),
        compiler_params=_cp(("parallel", "arbitrary")),
        name=name,
    )(u, u, u, u, u, u, conv_w, conv_w, conv_b, conv_b, da, da)


def loss_head(y, target, *, name):
    S, D = y.shape
    T = _tile(S)

    def body(y_ref, t_ref, dy_ref, l_ref):
        i = pl.program_id(0)
        err = y_ref[...] - t_ref[...]
        dy_ref[...] = err * (1.0 / D)

        @pl.when(i == 0)
        def _():
            l_ref[...] = jnp.zeros_like(l_ref)

        l_ref[...] += 0.5 * jnp.sum(jnp.mean(err * err, axis=-1, keepdims=True))

    row = pl.BlockSpec((T, D), lambda i: (i, 0))
    return pl.pallas_call(
        body,
        out_shape=(jax.ShapeDtypeStruct((S, D), F32), jax.ShapeDtypeStruct((8, 128), F32)),
        grid=(S // T,),
        in_specs=[row, row],
        out_specs=(row, pl.BlockSpec((8, 128), lambda i: (0, 0))),
        compiler_params=_cp(("arbitrary",)),
        name=name,
    )(y, target)


ELEMS_PER_BLOCK = 256 * 1024


def _rows_tile(rows, cols):
    if rows * cols <= ELEMS_PER_BLOCK or rows % 8:
        return rows
    best = 8
    for tr in range(8, rows + 1, 8):
        if rows % tr == 0 and tr * cols <= ELEMS_PER_BLOCK:
            best = tr
    return best


def sum_parts(parts, order, *, name):
    P, rows, cols = parts.shape
    tr = _rows_tile(rows, cols)

    def body(p_ref, o_ref):
        acc = p_ref[order[0]]
        for j in order[1:]:
            acc = acc + p_ref[j]
        o_ref[...] = acc

    return pl.pallas_call(
        body,
        out_shape=jax.ShapeDtypeStruct((rows, cols), F32),
        grid=(rows // tr,),
        in_specs=[pl.BlockSpec((P, tr, cols), lambda i: (0, i, 0))],
        out_specs=pl.BlockSpec((tr, cols), lambda i: (i, 0)),
        compiler_params=_cp(("parallel",)),
        name=name,
    )(parts)


def adamw(w, m, v, g_parts, *, name):
    rows, cols = w.shape
    tr = _rows_tile(rows, cols)
    n_parts = len(g_parts)

    def body(*refs):
        w_ref, m_ref, v_ref = refs[:3]
        g_refs = refs[3:3 + n_parts]
        g_out, d_out, m_out, v_out = refs[3 + n_parts:]
        g = g_refs[0][...]
        for r in g_refs[1:]:
            g = g + r[...]
        mn = ADAM_B1 * m_ref[...] + (1.0 - ADAM_B1) * g
        vn = ADAM_B2 * v_ref[...] + (1.0 - ADAM_B2) * (g * g)
        m_hat = mn / (1.0 - ADAM_B1 ** ADAM_STEP)
        v_hat = vn / (1.0 - ADAM_B2 ** ADAM_STEP)
        g_out[...] = g
        d_out[...] = -ADAM_LR * (m_hat / (jnp.sqrt(v_hat) + ADAM_EPS) + ADAM_WD * w_ref[...])
        m_out[...] = mn
        v_out[...] = vn

    spec = pl.BlockSpec((tr, cols), lambda i: (i, 0))
    shp = jax.ShapeDtypeStruct((rows, cols), F32)
    return pl.pallas_call(
        body,
        out_shape=(shp, shp, shp, shp),
        grid=(rows // tr,),
        in_specs=[spec] * (3 + n_parts),
        out_specs=(spec, spec, spec, spec),
        compiler_params=_cp(("parallel",)),
        name=name,
    )(w, m, v, *g_parts)


ANY = pl.BlockSpec(memory_space=pl.ANY)


def _mesh_pos():
    return lax.axis_index("x"), lax.axis_index("y"), lax.axis_index("c")


def _chip_peers(x, y):
    return [(1 - x, y), (x, 1 - y), (1 - x, 1 - y)]


def _all_peers(x, y, c):
    return [((1 - x) if (r >> 2) & 1 else x, (1 - y) if (r >> 1) & 1 else y, (1 - c) if r & 1 else c)
            for r in range(1, 8)]


def _shard_slice(ref, axis, j, size):
    idx = [slice(None)] * 3
    idx[axis] = pl.ds(pl.multiple_of(j * size, 128 if axis == 2 else 16), size)
    return ref.at[tuple(idx)]


def all_gather_shards(shards, axes, *, name):
    nt = len(shards)
    sizes = [s.shape[a] for s, a in zip(shards, axes)]

    def body(*refs):
        ins, outs = refs[:nt], refs[nt:2 * nt]
        send_sems, recv_sems, local_sems = refs[2 * nt:]
        x, y, c = _mesh_pos()
        mine = 2 * x + y
        local, sends, recvs = [], [], []
        for t in range(nt):
            cp = pltpu.make_async_copy(ins[t], _shard_slice(outs[t], axes[t], mine, sizes[t]), local_sems.at[t])
            cp.start()
            local.append(cp)
            for k, (px, py) in enumerate(_chip_peers(x, y)):
                def mk(blk, t=t, k=k, px=px, py=py):
                    return pltpu.make_async_remote_copy(
                        src_ref=ins[t], dst_ref=_shard_slice(outs[t], axes[t], blk, sizes[t]),
                        send_sem=send_sems.at[3 * t + k], recv_sem=recv_sems.at[3 * t + k],
                        device_id=(px, py, c), device_id_type=MESH_T)
                snd = mk(mine)
                snd.start()
                sends.append(snd)
                recvs.append(mk(2 * px + py))
        for r in recvs:
            r.wait_recv()
        for s in sends:
            s.wait_send()
        for cp in local:
            cp.wait()

    out_shape = []
    for s, a, sz in zip(shards, axes, sizes):
        shp = list(s.shape)
        shp[a] = 4 * sz
        out_shape.append(jax.ShapeDtypeStruct(tuple(shp), s.dtype))
    return pl.pallas_call(
        body,
        out_shape=tuple(out_shape),
        in_specs=[ANY] * nt,
        out_specs=tuple([ANY] * nt),
        scratch_shapes=[pltpu.SemaphoreType.DMA((3 * nt,)), pltpu.SemaphoreType.DMA((3 * nt,)),
                        pltpu.SemaphoreType.DMA((nt,))],
        name=name,
    )(*shards)


def scatter_grads(grads, axes, small, *, name):
    nt = len(grads)
    sizes = [g.shape[a] // 4 for g, a in zip(grads, axes)]

    def body(*refs):
        ins, small_in = refs[:nt], refs[nt]
        outs, small_out = refs[nt + 1:2 * nt + 1], refs[2 * nt + 1]
        send_sems, recv_sems, local_sems, ssend, srecv = refs[2 * nt + 2:]
        x, y, c = _mesh_pos()
        mine = 2 * x + y
        me = 4 * x + 2 * y + c
        local, sends, recvs = [], [], []
        for t in range(nt):
            cp = pltpu.make_async_copy(_shard_slice(ins[t], axes[t], mine, sizes[t]), outs[t].at[3], local_sems.at[t])
            cp.start()
            local.append(cp)
            for k, (px, py) in enumerate(_chip_peers(x, y)):
                snd = pltpu.make_async_remote_copy(
                    src_ref=_shard_slice(ins[t], axes[t], 2 * px + py, sizes[t]), dst_ref=outs[t].at[k],
                    send_sem=send_sems.at[3 * t + k], recv_sem=recv_sems.at[3 * t + k],
                    device_id=(px, py, c), device_id_type=MESH_T)
                snd.start()
                sends.append(snd)
                recvs.append(snd)
        cp = pltpu.make_async_copy(small_in, small_out.at[me], local_sems.at[nt])
        cp.start()
        local.append(cp)
        for r, (px, py, pc) in enumerate(_all_peers(x, y, c)):
            def mk(slot, r=r, px=px, py=py, pc=pc):
                return pltpu.make_async_remote_copy(
                    src_ref=small_in, dst_ref=small_out.at[slot], send_sem=ssend.at[r], recv_sem=srecv.at[r],
                    device_id=(px, py, pc), device_id_type=MESH_T)
            snd = mk(me)
            snd.start()
            sends.append(snd)
            recvs.append(mk(4 * px + 2 * py + pc))
        for r in recvs:
            r.wait_recv()
        for s in sends:
            s.wait_send()
        for cp in local:
            cp.wait()

    out_shape = []
    for g, a, sz in zip(grads, axes, sizes):
        shp = list(g.shape)
        shp[a] = sz
        out_shape.append(jax.ShapeDtypeStruct((4,) + tuple(shp), F32))
    out_shape.append(jax.ShapeDtypeStruct((8,) + small.shape, F32))
    return pl.pallas_call(
        body,
        out_shape=tuple(out_shape),
        in_specs=[ANY] * (nt + 1),
        out_specs=tuple([ANY] * (nt + 1)),
        scratch_shapes=[pltpu.SemaphoreType.DMA((3 * nt,)), pltpu.SemaphoreType.DMA((3 * nt,)),
                        pltpu.SemaphoreType.DMA((nt + 1,)), pltpu.SemaphoreType.DMA((7,)),
                        pltpu.SemaphoreType.DMA((7,))],
        name=name,
    )(*grads, small)


def sibling_exchange(parts, *, name):
    nt = len(parts)

    def body(*refs):
        ins, outs = refs[:nt], refs[nt:2 * nt]
        send_sems, recv_sems = refs[2 * nt:]
        x, y, c = _mesh_pos()
        cps = []
        for t in range(nt):
            cp = pltpu.make_async_remote_copy(src_ref=ins[t], dst_ref=outs[t], send_sem=send_sems.at[t],
                                              recv_sem=recv_sems.at[t], device_id=(x, y, 1 - c), device_id_type=MESH_T)
            cp.start()
            cps.append(cp)
        for cp in cps:
            cp.wait_recv()
        for cp in cps:
            cp.wait_send()

    return pl.pallas_call(
        body,
        out_shape=tuple(jax.ShapeDtypeStruct(p.shape, p.dtype) for p in parts),
        in_specs=[ANY] * nt,
        out_specs=tuple([ANY] * nt),
        scratch_shapes=[pltpu.SemaphoreType.DMA((nt,)), pltpu.SemaphoreType.DMA((nt,))],
        name=name,
    )(*parts)


WEIGHTS = ("norm_mix", "w_in", "b_gate", "q_norm_a", "k_norm_a", "rel_bias_a", "w_pool", "pool_scale",
           "w_branch_a", "w_branch_b", "w_branch_c", "w_out", "norm_ffn", "w_up", "conv_w", "conv_b", "w_down")
SHARDED = {"w_in": 2, "w_branch_a": 2, "w_branch_b": 2, "w_branch_c": 2, "w_out": 1, "w_up": 2, "conv_w": 2,
           "w_down": 1}
REPLICATED = tuple(n for n in WEIGHTS if n not in SHARDED)
SMALL_ROWS = 1232


def _layer_fwd(x, p, tables):
    diag = exact_dot(p["rel_bias_a"], tables["onehot_t"], name="bias_diagonals")
    diag = diag.reshape(N_HEADS, N_VARIANTS, 1, DIAG_W).transpose(1, 0, 2, 3)
    biasm = bias_expand(diag, name="bias_expand")
    gq8 = jnp.tile(p["q_norm_a"], N_HEADS)[None]
    gk8 = jnp.tile(p["k_norm_a"], N_HEADS)[None]
    h = rmsnorm_fwd(x, p["norm_mix"][None], name="rmsnorm_fwd")
    proj = matmul(h, p["w_in"], name="mm_in")
    qa, ka, va, qb, kb, vb = qkv_prep(proj, gq8, gk8, name="qkv_prep")
    oa = attn_a_fwd(qa, ka, va, biasm, name="attn_a_fwd")
    ob, tot, nblk = attn_b_fwd(qb, kb, vb, name="attn_b_fwd")
    wpool = p["w_pool"].astype(BF16)
    oc = pool_fwd(proj, wpool, p["pool_scale"][None], name="pool_fwd")
    merged = merge_fwd(oa, ob, oc, proj, p["b_gate"][None], p["w_branch_a"], p["w_branch_b"], p["w_branch_c"],
                       name="merge_fwd")
    x1 = matmul(merged, p["w_out"], add=x, name="mm_out")
    h2 = rmsnorm_fwd(x1, p["norm_ffn"][None], name="rmsnorm_fwd")
    u = matmul(h2, p["w_up"], name="mm_up")
    a = conv_glu_fwd(u, p["conv_w"], p["conv_b"][None], name="conv_glu_fwd")
    x2 = matmul(a, p["w_down"], add=x1, name="mm_down")
    saved = dict(x=x, h=h, proj=proj, qa=qa, ka=ka, va=va, qb=qb, kb=kb, vb=vb, oa=oa, ob=ob, tot=tot, nblk=nblk, oc=oc,
                 merged=merged, x1=x1, h2=h2, u=u, a=a, biasm=biasm, gq8=gq8, gk8=gk8, wpool=wpool)
    return x2, saved


def _layer_bwd(dx2, s, p, tables):
    g = {}
    da = matmul(dx2, p["w_down"], tb=True, name="mm_down_dx")
    g["w_down"] = matmul(s["a"], dx2, ta=True, name="mm_down_dw")
    dug, duv, dcwg, dcwv, dcbg, dcbv = conv_glu_bwd(s["u"], p["conv_w"], p["conv_b"][None], da, name="conv_glu_bwd")
    du = jnp.concatenate([dug, duv], axis=1)
    g["conv_w"] = jnp.concatenate([dcwg, dcwv], axis=1)
    g["conv_b"] = jnp.concatenate([dcbg, dcbv], axis=1)[0]
    g["w_up"] = matmul(s["h2"], du, ta=True, name="mm_up_dw")
    dh2 = matmul(du, p["w_up"], tb=True, name="mm_up_dx")
    dx1, dg2 = rmsnorm_bwd(s["x1"], p["norm_ffn"][None], dh2, dx2, name="rmsnorm_bwd")
    g["norm_ffn"] = dg2[0]
    dmerged = matmul(dx1, p["w_out"], tb=True, name="mm_out_dx")
    g["w_out"] = matmul(s["merged"], dx1, ta=True, name="mm_out_dw")
    t_a, t_b, t_c, dga, dgb, dgc, dba, dbb, dbc = merge_bwd(
        dmerged, s["oa"], s["ob"], s["oc"], s["proj"], p["b_gate"][None], p["w_branch_a"], p["w_branch_b"],
        p["w_branch_c"], name="merge_bwd")
    g["b_gate"] = jnp.concatenate([dba, dbb, dbc], axis=1)[0]
    g["w_branch_a"] = matmul(s["oa"], t_a, ta=True, name="mm_branch_dw")
    g["w_branch_b"] = matmul(s["ob"], t_b, ta=True, name="mm_branch_dw")
    g["w_branch_c"] = matmul(s["oc"], t_c, ta=True, name="mm_branch_dw")
    doa = matmul(t_a, p["w_branch_a"], tb=True, out_dtype=BF16, name="mm_branch_dx")
    dob = matmul(t_b, p["w_branch_b"], tb=True, out_dtype=BF16, name="mm_branch_dx")
    doc = matmul(t_c, p["w_branch_c"], tb=True, name="mm_branch_dx_f32")
    dqh, dkh, dva, dbias = attn_a_bwd(s["qa"], s["ka"], s["va"], s["biasm"], doa, name="attn_a_bwd")
    ddiag = relbias_reduce(dbias, name="relbias_reduce")
    ddiag = ddiag.transpose(1, 0, 2, 3).reshape(N_HEADS, N_VARIANTS * DIAG_W)
    g["rel_bias_a"] = exact_dot(ddiag, tables["onehot"], name="relbias_table")
    dqa, dka, dgq8, dgk8 = qknorm_bwd(s["proj"], s["gq8"], s["gk8"], dqh, dkh, name="qknorm_bwd")
    g["q_norm_a"] = dgq8.reshape(N_HEADS, HEAD_DIM).sum(axis=0)
    g["k_norm_a"] = dgk8.reshape(N_HEADS, HEAD_DIM).sum(axis=0)
    dqb, dkb, dvb = attn_b_bwd(s["qb"], s["kb"], s["vb"], s["tot"], s["nblk"], dob, name="attn_b_bwd")
    duc, dwp, dsc = pool_bwd(s["proj"], s["wpool"], p["pool_scale"][None], doc, name="pool_bwd")
    g["w_pool"] = dwp
    g["pool_scale"] = dsc[0]
    dproj = jnp.concatenate([dqa, dka, dva.astype(BF16), dqb, dkb.astype(BF16), dvb.astype(BF16), duc,
                             dga, dgb, dgc], axis=1)
    g["w_in"] = matmul(s["h"], dproj, ta=True, name="mm_in_dw")
    dh = matmul(dproj, p["w_in"], tb=True, name="mm_in_dx")
    dx, dg1 = rmsnorm_bwd(s["x"], p["norm_mix"][None], dh, dx1, name="rmsnorm_bwd")
    g["norm_mix"] = dg1[0]
    return dx, g


def kernel(x, norm_mix, w_in, b_gate, q_norm_a, k_norm_a, rel_bias_a, w_pool, pool_scale, w_branch_a, w_branch_b, w_branch_c, w_out, norm_ffn, w_up, conv_w, conv_b, w_down, loss_target, m_norm_mix, m_w_in, m_b_gate, m_q_norm_a, m_k_norm_a, m_rel_bias_a, m_w_pool, m_pool_scale, m_w_branch_a, m_w_branch_b, m_w_branch_c, m_w_out, m_norm_ffn, m_w_up, m_conv_w, m_conv_b, m_w_down, v_norm_mix, v_w_in, v_b_gate, v_q_norm_a, v_k_norm_a, v_rel_bias_a, v_w_pool, v_pool_scale, v_w_branch_a, v_w_branch_b, v_w_branch_c, v_w_out, v_norm_ffn, v_w_up, v_conv_w, v_conv_b, v_w_down):
    w = dict(zip(WEIGHTS, (norm_mix, w_in, b_gate, q_norm_a, k_norm_a, rel_bias_a, w_pool, pool_scale, w_branch_a,
                           w_branch_b, w_branch_c, w_out, norm_ffn, w_up, conv_w, conv_b, w_down)))
    m = dict(zip(WEIGHTS, (m_norm_mix, m_w_in, m_b_gate, m_q_norm_a, m_k_norm_a, m_rel_bias_a, m_w_pool, m_pool_scale,
                           m_w_branch_a, m_w_branch_b, m_w_branch_c, m_w_out, m_norm_ffn, m_w_up, m_conv_w, m_conv_b,
                           m_w_down)))
    v = dict(zip(WEIGHTS, (v_norm_mix, v_w_in, v_b_gate, v_q_norm_a, v_k_norm_a, v_rel_bias_a, v_w_pool, v_pool_scale,
                           v_w_branch_a, v_w_branch_b, v_w_branch_c, v_w_out, v_norm_ffn, v_w_up, v_conv_w, v_conv_b,
                           v_w_down)))
    onehot = diagonal_onehot()
    tables = dict(onehot=jnp.asarray(onehot), onehot_t=jnp.asarray(np.ascontiguousarray(onehot.T)))

    names = tuple(SHARDED)
    shards = [w[n] if n == "conv_w" else w[n].astype(BF16) for n in names]
    full = dict(zip(names, all_gather_shards(shards, [SHARDED[n] for n in names], name="all_gather_weights")))

    def layer_params(l):
        p = {n: full[n][l] for n in names}
        p.update({n: w[n][l] for n in REPLICATED})
        return p

    xs = x[0]
    saved = []
    for l in range(DEPTH):
        xs, s = _layer_fwd(xs, layer_params(l), tables)
        saved.append(s)
    dx, lpart = loss_head(xs, loss_target[0], name="loss_head")
    loss = lax.psum(lpart[0, 0], MESH_AXES)
    grads = [None] * DEPTH
    for l in reversed(range(DEPTH)):
        dx, grads[l] = _layer_bwd(dx, saved[l], layer_params(l), tables)
    g = {n: jnp.stack([grads[l][n] for l in range(DEPTH)]) for n in WEIGHTS}

    flat = jnp.concatenate([g[n].reshape(-1) for n in REPLICATED])
    small = jnp.pad(flat, (0, SMALL_ROWS * 128 - flat.shape[0])).reshape(SMALL_ROWS, 128)
    outs = scatter_grads([g[n] for n in names], [SHARDED[n] for n in names], small, name="scatter_grads")
    sums = []
    for n, buf in zip(names, outs[:-1]):
        rows = buf.shape[1] * buf.shape[2]
        sums.append(sum_parts(buf.reshape(4, rows, buf.shape[3]), (3, 0, 1, 2), name="sum_chips"))
    others = sibling_exchange(sums, name="sibling_exchange")
    small_sum = sum_parts(outs[-1], tuple(range(8)), name="sum_devices").reshape(-1)

    res = {}
    for n, mine, other in zip(names, sums, others):
        shp = w[n].shape
        two_d = lambda t: t.reshape(mine.shape)
        res[n] = [t.reshape(shp) for t in adamw(two_d(w[n]), two_d(m[n]), two_d(v[n]), [mine, other], name="adamw")]
    off = 0
    for n in REPLICATED:
        shp = w[n].shape
        size = int(np.prod(shp))
        gn = small_sum[off:off + size]
        off += size
        cols = shp[-1]
        two_d = lambda t: t.reshape(size // cols, cols)
        res[n] = [t.reshape(shp) for t in adamw(two_d(w[n]), two_d(m[n]), two_d(v[n]), [two_d(gn)], name="adamw")]

    out = [loss, dx[None]]
    for k in range(4):
        out.extend(res[n][k] for n in WEIGHTS)
    return tuple(out)
```

```python
import functools
import math

import jax
import jax.numpy as jnp
import numpy as np
from jax import lax
from jax.experimental import pallas as pl
from jax.experimental.pallas import tpu as pltpu

F32 = jnp.float32
BF16 = jnp.bfloat16

D_MODEL = 1024
DEPTH = 2
CHUNK = 64
N_LEFT = 8
HEAD_DIM = 64
N_HEADS = 8
WIDTH = 512
POOL_WINDOWS = (2, 4, 8, 16)
GROUP_DIM = 128
MAX_REL = 2 * CHUNK
REL_TABLE = MAX_REL + CHUNK
D_FF = 2816
EPS = 1e-6
IN_COLS = 7 * WIDTH + 3 * D_MODEL
GATE_COL0 = 7 * WIDTH

ADAM_LR = 0.001
ADAM_B1 = 0.9
ADAM_B2 = 0.999
ADAM_EPS = 1e-08
ADAM_WD = 0.01
ADAM_STEP = 10

VMEM_LIMIT = 56 * 1024 * 1024
ATT_Q = 256
A_WIN = ATT_Q + N_LEFT * CHUNK
HALO = 16
CONV_HALO = 8
NEG = -1e30

MESH_AXES = ("x", "y", "c")
MESH_T = pl.DeviceIdType.MESH


def _cp(sem=None, vmem=VMEM_LIMIT):
    return pltpu.CompilerParams(dimension_semantics=sem, vmem_limit_bytes=vmem)


def _dot(a, b, ca, cb):
    return lax.dot_general(a, b, (((ca,), (cb,)), ((), ())), preferred_element_type=F32)


def _tile(n, cands=(512, 256, 128)):
    for c in cands:
        if n % c == 0:
            return c
    return n


def _split_hi_lo(v):
    hi = v.astype(BF16)
    lo = (v - hi.astype(F32)).astype(BF16)
    return hi, lo


def matmul(a, b, *, ta=False, tb=False, add=None, out_dtype=F32, name):
    if ta:
        K, M = a.shape
    else:
        M, K = a.shape
    if tb:
        N, K2 = b.shape
    else:
        K2, N = b.shape
    assert K == K2, (a.shape, b.shape, ta, tb)
    big = (1024, 1408, 512, 256, 128)
    tm, tn, tk = _tile(M, big), _tile(N, big), _tile(K, big)
    nk = K // tk

    def body(*refs):
        if add is None:
            a_ref, b_ref, o_ref, acc = refs
        else:
            a_ref, b_ref, r_ref, o_ref, acc = refs
        k = pl.program_id(2)

        @pl.when(k == 0)
        def _():
            acc[...] = jnp.zeros_like(acc)

        av = a_ref[...].astype(BF16)
        bv = b_ref[...].astype(BF16)
        acc[...] += _dot(av, bv, 0 if ta else 1, 1 if tb else 0)

        @pl.when(k == nk - 1)
        def _():
            r = acc[...]
            if add is not None:
                r = r + r_ref[...].astype(F32)
            o_ref[...] = r.astype(out_dtype)

    a_spec = pl.BlockSpec((tk, tm), lambda i, j, k: (k, i)) if ta else pl.BlockSpec((tm, tk), lambda i, j, k: (i, k))
    b_spec = pl.BlockSpec((tn, tk), lambda i, j, k: (j, k)) if tb else pl.BlockSpec((tk, tn), lambda i, j, k: (k, j))
    o_spec = pl.BlockSpec((tm, tn), lambda i, j, k: (i, j))
    in_specs = [a_spec, b_spec]
    args = [a, b]
    if add is not None:
        in_specs.append(o_spec)
        args.append(add)
    return pl.pallas_call(
        body,
        out_shape=jax.ShapeDtypeStruct((M, N), out_dtype),
        grid=(M // tm, N // tn, nk),
        in_specs=in_specs,
        out_specs=o_spec,
        scratch_shapes=[pltpu.VMEM((tm, tn), F32)],
        compiler_params=_cp(("parallel", "parallel", "arbitrary")),
        name=name,
    )(*args)


def rmsnorm_fwd(x, g, *, name):
    S, D = x.shape
    T = _tile(S)

    def body(x_ref, g_ref, h_ref):
        xv = x_ref[...]
        r = lax.rsqrt(jnp.mean(xv * xv, axis=-1, keepdims=True) + EPS)
        h_ref[...] = (xv * r * g_ref[...]).astype(BF16)

    return pl.pallas_call(
        body,
        out_shape=jax.ShapeDtypeStruct((S, D), BF16),
        grid=(S // T,),
        in_specs=[pl.BlockSpec((T, D), lambda i: (i, 0)), pl.BlockSpec((1, D), lambda i: (0, 0))],
        out_specs=pl.BlockSpec((T, D), lambda i: (i, 0)),
        compiler_params=_cp(("parallel",)),
        name=name,
    )(x, g)


def rmsnorm_bwd(x, g, dh, dres, *, name):
    S, D = x.shape
    T = _tile(S)

    def body(x_ref, g_ref, dh_ref, dres_ref, dx_ref, dg_ref):
        i = pl.program_id(0)
        xv = x_ref[...]
        dhv = dh_ref[...].astype(F32)
        r = lax.rsqrt(jnp.mean(xv * xv, axis=-1, keepdims=True) + EPS)
        gd = dhv * g_ref[...]
        m = jnp.mean(xv * gd, axis=-1, keepdims=True)
        dx_ref[...] = dres_ref[...] + r * gd - xv * (r * r * r * m)

        @pl.when(i == 0)
        def _():
            dg_ref[...] = jnp.zeros_like(dg_ref)

        dg_ref[...] += jnp.sum(dhv * xv * r, axis=0, keepdims=True)

    row = pl.BlockSpec((T, D), lambda i: (i, 0))
    vec = pl.BlockSpec((1, D), lambda i: (0, 0))
    return pl.pallas_call(
        body,
        out_shape=(jax.ShapeDtypeStruct((S, D), F32), jax.ShapeDtypeStruct((1, D), F32)),
        grid=(S // T,),
        in_specs=[row, vec, row, row],
        out_specs=(row, vec),
        compiler_params=_cp(("arbitrary",)),
        name=name,
    )(x, g, dh, dres)


def _head_mean_matrix():
    r = lax.broadcasted_iota(jnp.int32, (WIDTH, WIDTH), 0) // HEAD_DIM
    c = lax.broadcasted_iota(jnp.int32, (WIDTH, WIDTH), 1) // HEAD_DIM
    return jnp.where(r == c, 1.0 / HEAD_DIM, 0.0).astype(BF16)


def _head_mean(v, mm):
    hi, lo = _split_hi_lo(v)
    return _dot(hi, mm, 1, 0) + _dot(lo, mm, 1, 0)


def qkv_prep(proj, gq, gk, *, name):
    S = proj.shape[0]
    T = _tile(S)

    def body(qa, ka, va, qb, kb, vb, gq_ref, gk_ref, oqa, oka, ova, oqb, okb, ovb):
        mm = _head_mean_matrix()
        for src, gref, dst in ((qa, gq_ref, oqa), (ka, gk_ref, oka)):
            v = src[...]
            r = lax.rsqrt(_head_mean(v * v, mm) + EPS)
            dst[...] = (v * r * gref[...]).astype(BF16)
        for src, dst in ((va, ova), (qb, oqb), (kb, okb), (vb, ovb)):
            dst[...] = src[...].astype(BF16)

    col = lambda j: pl.BlockSpec((T, WIDTH), lambda i, j=j: (i, j))
    vec = pl.BlockSpec((1, WIDTH), lambda i: (0, 0))
    out = pl.BlockSpec((T, WIDTH), lambda i: (i, 0))
    return pl.pallas_call(
        body,
        out_shape=tuple(jax.ShapeDtypeStruct((S, WIDTH), BF16) for _ in range(6)),
        grid=(S // T,),
        in_specs=[col(0), col(1), col(2), col(3), col(4), col(5), vec, vec],
        out_specs=tuple(out for _ in range(6)),
        compiler_params=_cp(("parallel",)),
        name=name,
    )(proj, proj, proj, proj, proj, proj, gq, gk)


def qknorm_bwd(proj, gq, gk, dqh, dkh, *, name):
    S = proj.shape[0]
    T = _tile(S)

    def body(qa, ka, gq_ref, gk_ref, dq_ref, dk_ref, oq, ok, ogq, ogk):
        i = pl.program_id(0)
        mm = _head_mean_matrix()

        @pl.when(i == 0)
        def _():
            ogq[...] = jnp.zeros_like(ogq)
            ogk[...] = jnp.zeros_like(ogk)

        for src, gref, dref, dst, gdst in ((qa, gq_ref, dq_ref, oq, ogq), (ka, gk_ref, dk_ref, ok, ogk)):
            v = src[...]
            dy = dref[...]
            r = lax.rsqrt(_head_mean(v * v, mm) + EPS)
            gd = dy * gref[...]
            m = _head_mean(v * gd, mm)
            dst[...] = (r * gd - v * (r * r * r * m)).astype(BF16)
            gdst[...] += jnp.sum(dy * v * r, axis=0, keepdims=True)

    col = lambda j: pl.BlockSpec((T, WIDTH), lambda i, j=j: (i, j))
    vec = pl.BlockSpec((1, WIDTH), lambda i: (0, 0))
    row = pl.BlockSpec((T, WIDTH), lambda i: (i, 0))
    return pl.pallas_call(
        body,
        out_shape=(jax.ShapeDtypeStruct((S, WIDTH), BF16), jax.ShapeDtypeStruct((S, WIDTH), BF16),
                   jax.ShapeDtypeStruct((1, WIDTH), F32), jax.ShapeDtypeStruct((1, WIDTH), F32)),
        grid=(S // T,),
        in_specs=[col(0), col(1), vec, vec, row, row],
        out_specs=(row, row, vec, vec),
        compiler_params=_cp(("arbitrary",)),
        name=name,
    )(proj, proj, gq, gk, dqh, dkh)


DIAG_W = 1024
N_VARIANTS = 3


def diagonal_onehot():
    jj = np.arange(DIAG_W)
    diff = np.where(jj < A_WIN, jj, jj - DIAG_W)
    out = np.zeros((N_VARIANTS, DIAG_W, REL_TABLE), np.float32)
    for v in range(N_VARIANTS):
        rel = np.clip(ATT_Q * v - diff, -(CHUNK - 1), MAX_REL) + (CHUNK - 1)
        out[v, jj, rel] = 1.0
    return out.reshape(N_VARIANTS * DIAG_W, REL_TABLE)


def exact_dot(a, b, *, name):
    def body(a_ref, b_ref, o_ref):
        o_ref[...] = jnp.dot(a_ref[...], b_ref[...], precision=lax.Precision.HIGHEST, preferred_element_type=F32)

    return pl.pallas_call(body, out_shape=jax.ShapeDtypeStruct((a.shape[0], b.shape[1]), F32),
                          compiler_params=_cp(), name=name)(a, b)


def _band_valid(v):
    qc = (lax.broadcasted_iota(jnp.int32, (ATT_Q, A_WIN), 0) + ATT_Q * v) // CHUNK
    kc = lax.broadcasted_iota(jnp.int32, (ATT_Q, A_WIN), 1) // CHUNK
    return (kc <= qc) & (kc >= qc - N_LEFT)


def bias_expand(diag, *, name):
    def body(d_ref, o_ref):
        rows = jnp.broadcast_to(d_ref[0, 0], (ATT_Q, DIAG_W))
        skew = pltpu.roll(rows, 0, 1, stride=1, stride_axis=0)
        o_ref[0, 0] = jnp.where(_band_valid(pl.program_id(0)), skew[:, :A_WIN], NEG)

    return pl.pallas_call(
        body,
        out_shape=jax.ShapeDtypeStruct((N_VARIANTS, N_HEADS, ATT_Q, A_WIN), F32),
        grid=(N_VARIANTS, N_HEADS),
        in_specs=[pl.BlockSpec((1, 1, 1, DIAG_W), lambda v, h: (v, h, 0, 0))],
        out_specs=pl.BlockSpec((1, 1, ATT_Q, A_WIN), lambda v, h: (v, h, 0, 0)),
        compiler_params=_cp(("parallel", "parallel")),
        name=name,
    )(diag)


def relbias_reduce(dbias, *, name):
    def body(db_ref, o_ref):
        x = jnp.concatenate([db_ref[0, 0], jnp.zeros((ATT_Q, DIAG_W - A_WIN), F32)], axis=1)
        row = lax.broadcasted_iota(jnp.int32, (ATT_Q, DIAG_W), 0)
        for b in range(8):
            x = jnp.where((row >> b) & 1 == 1, pltpu.roll(x, DIAG_W - (1 << b), 1), x)
        o_ref[0, 0] = jnp.sum(x, axis=0, keepdims=True)

    return pl.pallas_call(
        body,
        out_shape=jax.ShapeDtypeStruct((N_VARIANTS, N_HEADS, 1, DIAG_W), F32),
        grid=(N_VARIANTS, N_HEADS),
        in_specs=[pl.BlockSpec((1, 1, ATT_Q, A_WIN), lambda v, h: (v, h, 0, 0))],
        out_specs=pl.BlockSpec((1, 1, 1, DIAG_W), lambda v, h: (v, h, 0, 0)),
        compiler_params=_cp(("parallel", "parallel")),
        name=name,
    )(dbias)


def _a_window_start(qb):
    return pl.multiple_of(jnp.maximum(qb * ATT_Q - N_LEFT * CHUNK, 0), ATT_Q)


def attn_a_fwd(q, k, v, biasm, *, name):
    S = q.shape[0]
    nq = S // ATT_Q

    def body(q_ref, k_ref, v_ref, b_ref, o_ref):
        qb = pl.program_id(1)
        start = _a_window_start(qb)
        outs = []
        for h in range(2):
            lanes = slice(h * HEAD_DIM, (h + 1) * HEAD_DIM)
            qh = q_ref[:, lanes]
            kw = k_ref[pl.ds(start, A_WIN), lanes]
            vw = v_ref[pl.ds(start, A_WIN), lanes]
            s = _dot(qh, kw, 1, 1) * (1.0 / math.sqrt(HEAD_DIM)) + b_ref[0, h]
            m = jnp.max(s, axis=-1, keepdims=True)
            e = jnp.exp(s - m)
            p = e / jnp.sum(e, axis=-1, keepdims=True)
            outs.append(_dot(p.astype(BF16), vw, 1, 0))
        o_ref[...] = jnp.concatenate(outs, axis=1).astype(BF16)

    qspec = pl.BlockSpec((ATT_Q, 2 * HEAD_DIM), lambda hp, qb: (qb, hp))
    kvspec = pl.BlockSpec((S, 2 * HEAD_DIM), lambda hp, qb: (0, hp))
    bspec = pl.BlockSpec((1, 2, ATT_Q, A_WIN), lambda hp, qb: (jnp.minimum(qb, 2), hp, 0, 0))
    return pl.pallas_call(
        body,
        out_shape=jax.ShapeDtypeStruct((S, WIDTH), BF16),
        grid=(N_HEADS // 2, nq),
        in_specs=[qspec, kvspec, kvspec, bspec],
        out_specs=qspec,
        compiler_params=_cp(("parallel", "arbitrary")),
        name=name,
    )(q, k, v, biasm)


def attn_a_bwd(q, k, v, biasm, do, *, name):
    S = q.shape[0]
    nq = S // ATT_Q
    scale = 1.0 / math.sqrt(HEAD_DIM)

    def body(q_ref, k_ref, v_ref, b_ref, do_ref, dq_ref, dk_ref, dv_ref, db_ref):
        qb = pl.program_id(1)
        start = _a_window_start(qb)

        @pl.when(qb == 0)
        def _():
            dk_ref[...] = jnp.zeros_like(dk_ref)
            dv_ref[...] = jnp.zeros_like(dv_ref)

        @pl.when(qb <= 2)
        def _():
            db_ref[...] = jnp.zeros_like(db_ref)

        dqs = []
        for h in range(2):
            lanes = slice(h * HEAD_DIM, (h + 1) * HEAD_DIM)
            qh = q_ref[:, lanes]
            doh = do_ref[:, lanes]
            kw = k_ref[pl.ds(start, A_WIN), lanes]
            vw = v_ref[pl.ds(start, A_WIN), lanes]
            s = _dot(qh, kw, 1, 1) * scale + b_ref[0, h]
            m = jnp.max(s, axis=-1, keepdims=True)
            e = jnp.exp(s - m)
            p = e / jnp.sum(e, axis=-1, keepdims=True)
            dp = _dot(doh, vw, 1, 1)
            delta = jnp.sum(p * dp, axis=-1, keepdims=True)
            ds = p * (dp - delta)
            db_ref[0, h] += ds
            dsb = ds.astype(BF16)
            dqs.append(_dot(dsb, kw, 1, 0) * scale)
            dk_ref[pl.ds(start, A_WIN), lanes] += _dot(dsb, qh, 0, 0) * scale
            dv_ref[pl.ds(start, A_WIN), lanes] += _dot(p.astype(BF16), doh, 0, 0)
        dq_ref[...] = jnp.concatenate(dqs, axis=1)

    qspec = pl.BlockSpec((ATT_Q, 2 * HEAD_DIM), lambda hp, qb: (qb, hp))
    kvspec = pl.BlockSpec((S, 2 * HEAD_DIM), lambda hp, qb: (0, hp))
    bspec = pl.BlockSpec((1, 2, ATT_Q, A_WIN), lambda hp, qb: (jnp.minimum(qb, 2), hp, 0, 0))
    return pl.pallas_call(
        body,
        out_shape=(jax.ShapeDtypeStruct((S, WIDTH), F32), jax.ShapeDtypeStruct((S, WIDTH), F32),
                   jax.ShapeDtypeStruct((S, WIDTH), F32), jax.ShapeDtypeStruct((3, N_HEADS, ATT_Q, A_WIN), F32)),
        grid=(N_HEADS // 2, nq),
        in_specs=[qspec, kvspec, kvspec, bspec, qspec],
        out_specs=(qspec, kvspec, kvspec, bspec),
        compiler_params=_cp(("parallel", "arbitrary")),
        name=name,
    )(q, k, v, biasm, do)


def _tri(kind):
    j = lax.broadcasted_iota(jnp.int32, (ATT_Q, ATT_Q), 0)
    s = lax.broadcasted_iota(jnp.int32, (ATT_Q, ATT_Q), 1)
    if kind == "gt":
        m = j > s
    elif kind == "le":
        m = j <= s
    else:
        m = j < s
    return jnp.where(m, 1.0, 0.0).astype(BF16)


def _cum(v, tri):
    hi, lo = _split_hi_lo(v)
    return _dot(hi, tri, 1, 0) + _dot(lo, tri, 1, 0)


def _log_keep(z, mask):
    sp = jnp.maximum(z, 0.0) + jnp.log(1.0 + jnp.exp(-jnp.abs(z)))
    return jnp.where(mask, -sp, 0.0)


def _before(qb, kb):
    row = lax.broadcasted_iota(jnp.int32, (ATT_Q, ATT_Q), 0)
    col = lax.broadcasted_iota(jnp.int32, (ATT_Q, ATT_Q), 1)
    return (col - row) < (qb - kb) * ATT_Q


EXIT_LOG = -104.0


def attn_b_fwd(q, k, v, *, name):
    S = q.shape[0]
    nq = S // ATT_Q
    scale = 1.0 / math.sqrt(HEAD_DIM)

    def body(q_ref, k_ref, v_ref, o_ref, t_ref, n_ref):
        hp = pl.program_id(0)
        qb = pl.program_id(1)
        tri = _tri("gt")

        def cond(state):
            it, cmax, _ = state
            return jnp.logical_and(it <= qb, cmax >= EXIT_LOG)

        def step(state):
            it, _, carry = state
            kb = qb - it
            ks = pl.multiple_of(kb * ATT_Q, ATT_Q)
            mask = _before(qb, kb)
            new = []
            for h in range(2):
                lanes = slice(h * HEAD_DIM, (h + 1) * HEAD_DIM)
                c, acc = carry[h]
                z = _dot(q_ref[:, lanes], k_ref[pl.ds(ks, ATT_Q), lanes], 1, 1) * scale
                L = _log_keep(z, mask)
                tail = _cum(L, tri) + c
                w = jnp.where(mask, jnp.exp(z + L + tail), 0.0)
                acc = acc + _dot(w.astype(BF16), v_ref[pl.ds(ks, ATT_Q), lanes], 1, 0)
                c = c + jnp.sum(L, axis=-1, keepdims=True)
                new.append((c, acc))
            cmax = jnp.maximum(jnp.max(new[0][0]), jnp.max(new[1][0]))
            return it + 1, cmax, tuple(new)

        init = tuple((jnp.zeros((ATT_Q, 1), F32), jnp.zeros((ATT_Q, HEAD_DIM), F32)) for _ in range(2))
        visited, _, res = lax.while_loop(cond, step, (jnp.int32(0), jnp.float32(0.0), init))
        o_ref[...] = jnp.concatenate([res[0][1], res[1][1]], axis=1).astype(BF16)
        t_ref[...] = jnp.concatenate([jnp.broadcast_to(res[h][0], (ATT_Q, HEAD_DIM)) for h in range(2)], axis=1)
        n_ref[hp, qb] = visited.astype(F32)

    qspec = pl.BlockSpec((ATT_Q, 2 * HEAD_DIM), lambda hp, qb: (qb, hp))
    kvspec = pl.BlockSpec((S, 2 * HEAD_DIM), lambda hp, qb: (0, hp))
    return pl.pallas_call(
        body,
        out_shape=(jax.ShapeDtypeStruct((S, WIDTH), BF16), jax.ShapeDtypeStruct((S, WIDTH), F32),
                   jax.ShapeDtypeStruct((N_HEADS // 2, nq), F32)),
        grid=(N_HEADS // 2, nq),
        in_specs=[qspec, kvspec, kvspec],
        out_specs=(qspec, qspec, pl.BlockSpec(memory_space=pltpu.SMEM)),
        compiler_params=_cp(("arbitrary", "arbitrary")),
        name=name,
    )(q, k, v)


def attn_b_bwd(q, k, v, tot, nblk, do, *, name):
    S = q.shape[0]
    nq = S // ATT_Q
    scale = 1.0 / math.sqrt(HEAD_DIM)

    def body(q_ref, k_ref, v_ref, t_ref, n_ref, do_ref, dq_ref, dk_ref, dv_ref):
        hp = pl.program_id(0)
        qb = pl.program_id(1)
        first = jnp.clip(qb + 1 - n_ref[hp, qb].astype(jnp.int32), 0, qb + 1)
        tri_le = _tri("le")
        tri_lt = _tri("lt")

        @pl.when(qb == 0)
        def _():
            dk_ref[...] = jnp.zeros_like(dk_ref)
            dv_ref[...] = jnp.zeros_like(dv_ref)

        def step(kb, carry):
            ks = pl.multiple_of(kb * ATT_Q, ATT_Q)
            mask = _before(qb, kb)
            new = []
            for h in range(2):
                lanes = slice(h * HEAD_DIM, (h + 1) * HEAD_DIM)
                cl, cg, dq = carry[h]
                qh = q_ref[:, lanes]
                doh = do_ref[:, lanes]
                kh = k_ref[pl.ds(ks, ATT_Q), lanes]
                vh = v_ref[pl.ds(ks, ATT_Q), lanes]
                totl = t_ref[:, h * HEAD_DIM:h * HEAD_DIM + 1]
                z = _dot(qh, kh, 1, 1) * scale
                L = _log_keep(z, mask)
                tail = totl - (_cum(L, tri_le) + cl)
                sig = jnp.exp(z + L)
                w = jnp.where(mask, sig * jnp.exp(tail), 0.0)
                g = w * _dot(doh, vh, 1, 1)
                G = _cum(g, tri_lt) + cg
                dz = jnp.where(mask, g * (1.0 - sig) - sig * G, 0.0).astype(BF16)
                dq = dq + _dot(dz, kh, 1, 0)
                dk_ref[pl.ds(ks, ATT_Q), lanes] += _dot(dz, qh, 0, 0) * scale
                dv_ref[pl.ds(ks, ATT_Q), lanes] += _dot(w.astype(BF16), doh, 0, 0)
                cl = cl + jnp.sum(L, axis=-1, keepdims=True)
                cg = cg + jnp.sum(g, axis=-1, keepdims=True)
                new.append((cl, cg, dq))
            return tuple(new)

        init = tuple((jnp.zeros((ATT_Q, 1), F32), jnp.zeros((ATT_Q, 1), F32), jnp.zeros((ATT_Q, HEAD_DIM), F32))
                     for _ in range(2))
        res = lax.fori_loop(first, qb + 1, step, init)
        dq_ref[...] = (jnp.concatenate([res[0][2], res[1][2]], axis=1) * scale).astype(BF16)

    qspec = pl.BlockSpec((ATT_Q, 2 * HEAD_DIM), lambda hp, qb: (qb, hp))
    kvspec = pl.BlockSpec((S, 2 * HEAD_DIM), lambda hp, qb: (0, hp))
    return pl.pallas_call(
        body,
        out_shape=(jax.ShapeDtypeStruct((S, WIDTH), BF16), jax.ShapeDtypeStruct((S, WIDTH), F32),
                   jax.ShapeDtypeStruct((S, WIDTH), F32)),
        grid=(N_HEADS // 2, nq),
        in_specs=[qspec, kvspec, kvspec, qspec, pl.BlockSpec(memory_space=pltpu.SMEM), qspec],
        out_specs=(qspec, kvspec, kvspec),
        compiler_params=_cp(("parallel", "arbitrary")),
        name=name,
    )(q, k, v, tot, nblk, do)


U_COLBLK = 6


def _pool_counts(t0, rows):
    t = t0 + lax.broadcasted_iota(jnp.int32, (rows, WIDTH), 0)
    lane_grp = lax.broadcasted_iota(jnp.int32, (rows, WIDTH), 1) // GROUP_DIM
    win = jnp.where(lane_grp == 0, 2, jnp.where(lane_grp == 1, 4, jnp.where(lane_grp == 2, 8, 16)))
    cnt = jnp.minimum(t + 1, win)
    return 1.0 / cnt.astype(F32), lane_grp


def _window_sums(ext, shift_fn):
    s2 = ext + shift_fn(ext, 1)
    s4 = s2 + shift_fn(s2, 2)
    s8 = s4 + shift_fn(s4, 4)
    s16 = s8 + shift_fn(s8, 8)
    return s2, s4, s8, s16


def _select_group(lane_grp, s2, s4, s8, s16):
    return jnp.where(lane_grp == 0, s2, jnp.where(lane_grp == 1, s4, jnp.where(lane_grp == 2, s8, s16)))


def _pooled_tile(u_ref, h_ref, i, T):
    halo = jnp.where(i > 0, h_ref[...], 0.0)
    ext = jnp.concatenate([halo, u_ref[...]], axis=0)
    n = T + HALO
    sums = _window_sums(ext, lambda v, k: pltpu.roll(v, k, 0))
    inv, lane_grp = _pool_counts(i * T - HALO, n)
    pooled = _select_group(lane_grp, *sums) * inv - ext
    return pooled[HALO:, :]


def pool_fwd(proj, w_pool, scale, *, name):
    S = proj.shape[0]
    T = _tile(S)
    hb = T // HALO

    def body(u_ref, h_ref, w_ref, s_ref, o_ref):
        i = pl.program_id(0)
        pooled = _pooled_tile(u_ref, h_ref, i, T).astype(BF16)
        outs = [_dot(pooled[:, g * GROUP_DIM:(g + 1) * GROUP_DIM], w_ref[g], 1, 0) for g in range(4)]
        o_ref[...] = (jnp.concatenate(outs, axis=1) * s_ref[...]).astype(BF16)

    return pl.pallas_call(
        body,
        out_shape=jax.ShapeDtypeStruct((S, WIDTH), BF16),
        grid=(S // T,),
        in_specs=[pl.BlockSpec((T, WIDTH), lambda i: (i, U_COLBLK)),
                  pl.BlockSpec((HALO, WIDTH), lambda i: (jnp.maximum(i * hb - 1, 0), U_COLBLK)),
                  pl.BlockSpec((4, GROUP_DIM, GROUP_DIM), lambda i: (0, 0, 0)),
                  pl.BlockSpec((1, WIDTH), lambda i: (0, 0))],
        out_specs=pl.BlockSpec((T, WIDTH), lambda i: (i, 0)),
        compiler_params=_cp(("parallel",)),
        name=name,
    )(proj, proj, w_pool, scale)


def pool_bwd(proj, w_pool, scale, do, *, name):
    S = proj.shape[0]
    T = _tile(S)
    hb = T // HALO
    nt = S // T

    def body(u_ref, h_ref, w_ref, s_ref, do_ref, dof_ref, du_ref, dw_ref, ds_ref):
        i = pl.program_id(0)

        @pl.when(i == 0)
        def _():
            dw_ref[...] = jnp.zeros_like(dw_ref)
            ds_ref[...] = jnp.zeros_like(ds_ref)

        pooled = _pooled_tile(u_ref, h_ref, i, T).astype(BF16)
        dov = do_ref[...].astype(F32)
        fut = jnp.where(i < nt - 1, dof_ref[...].astype(F32), 0.0)
        dmix = (jnp.concatenate([dov, fut], axis=0) * s_ref[...]).astype(BF16)
        mixed, dpool = [], []
        for g in range(4):
            lanes = slice(g * GROUP_DIM, (g + 1) * GROUP_DIM)
            mixed.append(_dot(pooled[:, lanes], w_ref[g], 1, 0))
            dw_ref[g] += _dot(pooled[:, lanes], dmix[:T, lanes], 0, 0)
            dpool.append(_dot(dmix[:, lanes], w_ref[g], 1, 1))
        ds_ref[...] += jnp.sum(dov * jnp.concatenate(mixed, axis=1), axis=0, keepdims=True)
        dp = jnp.concatenate(dpool, axis=1)
        n = T + HALO
        inv, lane_grp = _pool_counts(i * T, n)
        sums = _window_sums(dp * inv, lambda v, k: pltpu.roll(v, n - k, 0))
        du = _select_group(lane_grp, *sums) - dp
        du_ref[...] = du[:T, :].astype(BF16)

    row = pl.BlockSpec((T, WIDTH), lambda i: (i, 0))
    return pl.pallas_call(
        body,
        out_shape=(jax.ShapeDtypeStruct((S, WIDTH), BF16), jax.ShapeDtypeStruct((4, GROUP_DIM, GROUP_DIM), F32),
                   jax.ShapeDtypeStruct((1, WIDTH), F32)),
        grid=(nt,),
        in_specs=[pl.BlockSpec((T, WIDTH), lambda i: (i, U_COLBLK)),
                  pl.BlockSpec((HALO, WIDTH), lambda i: (jnp.maximum(i * hb - 1, 0), U_COLBLK)),
                  pl.BlockSpec((4, GROUP_DIM, GROUP_DIM), lambda i: (0, 0, 0)),
                  pl.BlockSpec((1, WIDTH), lambda i: (0, 0)),
                  row,
                  pl.BlockSpec((HALO, WIDTH), lambda i: (jnp.minimum((i + 1) * hb, S // HALO - 1), 0))],
        out_specs=(row, pl.BlockSpec((4, GROUP_DIM, GROUP_DIM), lambda i: (0, 0, 0)),
                   pl.BlockSpec((1, WIDTH), lambda i: (0, 0))),
        compiler_params=_cp(("arbitrary",)),
        name=name,
    )(proj, proj, w_pool, scale, do, do)


GATE_BLK0 = GATE_COL0 // WIDTH


def merge_fwd(oa, ob, oc, proj, b_gate, wa, wb, wc, *, name):
    S = oa.shape[0]
    T = _tile(S)

    def body(oa_ref, ob_ref, oc_ref, ga, gb, gc, ba, bb, bc, wa_ref, wb_ref, wc_ref, m_ref):
        acc = None
        for o_ref, g_ref, b_ref, w_ref in ((oa_ref, ga, ba, wa_ref), (ob_ref, gb, bb, wb_ref), (oc_ref, gc, bc, wc_ref)):
            y = _dot(o_ref[...], w_ref[...], 1, 0)
            t = jax.nn.sigmoid(g_ref[...] + b_ref[...]) * y
            acc = t if acc is None else acc + t
        m_ref[...] = acc.astype(BF16)

    row = pl.BlockSpec((T, WIDTH), lambda i, n: (i, 0))
    gate = lambda b: pl.BlockSpec((T, WIDTH), lambda i, n, b=b: (i, GATE_BLK0 + 2 * b + n))
    bias = lambda b: pl.BlockSpec((1, WIDTH), lambda i, n, b=b: (0, 2 * b + n))
    wspec = pl.BlockSpec((WIDTH, WIDTH), lambda i, n: (0, n))
    return pl.pallas_call(
        body,
        out_shape=jax.ShapeDtypeStruct((S, D_MODEL), BF16),
        grid=(S // T, 2),
        in_specs=[row, row, row, gate(0), gate(1), gate(2), bias(0), bias(1), bias(2), wspec, wspec, wspec],
        out_specs=pl.BlockSpec((T, WIDTH), lambda i, n: (i, n)),
        compiler_params=_cp(("parallel", "parallel")),
        name=name,
    )(oa, ob, oc, proj, proj, proj, b_gate, b_gate, b_gate, wa, wb, wc)


def merge_bwd(dm, oa, ob, oc, proj, b_gate, wa, wb, wc, *, name):
    S = oa.shape[0]
    T = _tile(S)

    def body(dm_ref, oa_ref, ob_ref, oc_ref, ga, gb, gc, ba, bb, bc, wa_ref, wb_ref, wc_ref,
             ta, tb, tc, dga, dgb, dgc, dba, dbb, dbc):
        i = pl.program_id(1)
        dmv = dm_ref[...].astype(F32)
        for o_ref, g_ref, b_ref, w_ref, t_ref, dg_ref, db_ref in (
                (oa_ref, ga, ba, wa_ref, ta, dga, dba), (ob_ref, gb, bb, wb_ref, tb, dgb, dbb),
                (oc_ref, gc, bc, wc_ref, tc, dgc, dbc)):
            y = _dot(o_ref[...], w_ref[...], 1, 0)
            gate = jax.nn.sigmoid(g_ref[...] + b_ref[...])
            t_ref[...] = (gate * dmv).astype(BF16)
            dgl = dmv * y * gate * (1.0 - gate)
            dg_ref[...] = dgl.astype(BF16)

            @pl.when(i == 0)
            def _():
                db_ref[...] = jnp.zeros_like(db_ref)

            db_ref[...] += jnp.sum(dgl, axis=0, keepdims=True)

    row = pl.BlockSpec((T, WIDTH), lambda n, i: (i, 0))
    half = pl.BlockSpec((T, WIDTH), lambda n, i: (i, n))
    gate = lambda b: pl.BlockSpec((T, WIDTH), lambda n, i, b=b: (i, GATE_BLK0 + 2 * b + n))
    bias = lambda b: pl.BlockSpec((1, WIDTH), lambda n, i, b=b: (0, 2 * b + n))
    wspec = pl.BlockSpec((WIDTH, WIDTH), lambda n, i: (0, n))
    bvec = pl.BlockSpec((1, WIDTH), lambda n, i: (0, n))
    act = jax.ShapeDtypeStruct((S, D_MODEL), BF16)
    vec = jax.ShapeDtypeStruct((1, D_MODEL), F32)
    return pl.pallas_call(
        body,
        out_shape=(act, act, act, act, act, act, vec, vec, vec),
        grid=(2, S // T),
        in_specs=[half, row, row, row, gate(0), gate(1), gate(2), bias(0), bias(1), bias(2), wspec, wspec, wspec],
        out_specs=(half, half, half, half, half, half, bvec, bvec, bvec),
        compiler_params=_cp(("parallel", "arbitrary")),
        name=name,
    )(dm, oa, ob, oc, proj, proj, proj, b_gate, b_gate, b_gate, wa, wb, wc)


FF_T = 256
FF_BLKS = D_FF // FF_T


def _silu_parts(x):
    s = jax.nn.sigmoid(x)
    return x * s, s


def _conv3(ext, w_ref, b_ref):
    return (b_ref[...] + w_ref[0:1, :] * pltpu.roll(ext, 2, 0) + w_ref[1:2, :] * pltpu.roll(ext, 1, 0)
            + w_ref[2:3, :] * ext)


def conv_glu_fwd(u, conv_w, conv_b, *, name):
    S = u.shape[0]
    T = _tile(S)
    hb = T // CONV_HALO

    def body(ug, ugh, uv, uvh, wg, wv, bg, bv, a_ref):
        i = pl.program_id(1)
        cs = []
        for m_ref, h_ref, w_ref, b_ref in ((ug, ugh, wg, bg), (uv, uvh, wv, bv)):
            halo = jnp.where(i > 0, h_ref[...], 0.0)
            ext = jnp.concatenate([halo, m_ref[...]], axis=0)
            cs.append(_conv3(ext, w_ref, b_ref)[CONV_HALO:, :])
        act, _ = _silu_parts(cs[0])
        a_ref[...] = (act * cs[1]).astype(BF16)

    main = lambda o: pl.BlockSpec((T, FF_T), lambda c, i, o=o: (i, c + o))
    halo = lambda o: pl.BlockSpec((CONV_HALO, FF_T), lambda c, i, o=o: (jnp.maximum(i * hb - 1, 0), c + o))
    wsp = lambda o: pl.BlockSpec((3, FF_T), lambda c, i, o=o: (0, c + o))
    bsp = lambda o: pl.BlockSpec((1, FF_T), lambda c, i, o=o: (0, c + o))
    return pl.pallas_call(
        body,
        out_shape=jax.ShapeDtypeStruct((S, D_FF), BF16),
        grid=(FF_BLKS, S // T),
        in_specs=[main(0), halo(0), main(FF_BLKS), halo(FF_BLKS), wsp(0), wsp(FF_BLKS), bsp(0), bsp(FF_BLKS)],
        out_specs=pl.BlockSpec((T, FF_T), lambda c, i: (i, c)),
        compiler_params=_cp(("parallel", "parallel")),
        name=name,
    )(u, u, u, u, conv_w, conv_w, conv_b, conv_b)


def conv_glu_bwd(u, conv_w, conv_b, da, *, name):
    S = u.shape[0]
    T = _tile(S)
    hb = T // CONV_HALO
    nt = S // T
    n = T + 2 * CONV_HALO

    def body(ug, ugp, ugf, uv, uvp, uvf, wg, wv, bg, bv, da_ref, daf_ref,
             dug, duv, dwg, dwv, dbg, dbv):
        i = pl.program_id(1)
        first, last = i == 0, i == nt - 1
        exts, cs = [], []
        for m_ref, p_ref, f_ref, w_ref, b_ref in ((ug, ugp, ugf, wg, bg), (uv, uvp, uvf, wv, bv)):
            ext = jnp.concatenate([jnp.where(first, 0.0, p_ref[...]), m_ref[...], jnp.where(last, 0.0, f_ref[...])], axis=0)
            exts.append(ext)
            cs.append(_conv3(ext, w_ref, b_ref))
        dae = jnp.concatenate([jnp.zeros((CONV_HALO, FF_T), F32), da_ref[...].astype(F32),
                               jnp.where(last, 0.0, daf_ref[...].astype(F32))], axis=0)
        act, sg = _silu_parts(cs[0])
        dcs = (dae * cs[1] * (sg * (1.0 + cs[0] * (1.0 - sg))), dae * act)
        main = slice(CONV_HALO, CONV_HALO + T)
        for ext, dc, w_ref, du_ref, dw_ref, db_ref in ((exts[0], dcs[0], wg, dug, dwg, dbg),
                                                       (exts[1], dcs[1], wv, duv, dwv, dbv)):
            du = (w_ref[2:3, :] * dc + w_ref[1:2, :] * pltpu.roll(dc, n - 1, 0) + w_ref[0:1, :] * pltpu.roll(dc, n - 2, 0))
            du_ref[...] = du[main, :].astype(BF16)
            dcm = dc[main, :]
            rows = [jnp.sum(dcm * pltpu.roll(ext, 2 - j, 0)[main, :], axis=0, keepdims=True) if j < 2
                    else jnp.sum(dcm * ext[main, :], axis=0, keepdims=True) for j in range(3)]

            @pl.when(first)
            def _():
                dw_ref[...] = jnp.zeros_like(dw_ref)
                db_ref[...] = jnp.zeros_like(db_ref)

            dw_ref[...] += jnp.concatenate(rows, axis=0)
            db_ref[...] += jnp.sum(dcm, axis=0, keepdims=True)

    main = lambda o: pl.BlockSpec((T, FF_T), lambda c, i, o=o: (i, c + o))
    past = lambda o: pl.BlockSpec((CONV_HALO, FF_T), lambda c, i, o=o: (jnp.maximum(i * hb - 1, 0), c + o))
    fut = lambda o: pl.BlockSpec((CONV_HALO, FF_T), lambda c, i, o=o: (jnp.minimum((i + 1) * hb, S // CONV_HALO - 1), c + o))
    wsp = lambda o: pl.BlockSpec((3, FF_T), lambda c, i, o=o: (0, c + o))
    bsp = lambda o: pl.BlockSpec((1, FF_T), lambda c, i, o=o: (0, c + o))
    return pl.pallas_call(
        body,
        out_shape=(jax.ShapeDtypeStruct((S, D_FF), BF16), jax.ShapeDtypeStruct((S, D_FF), BF16),
                   jax.ShapeDtypeStruct((3, D_FF), F32), jax.ShapeDtypeStruct((3, D_FF), F32),
                   jax.ShapeDtypeStruct((1, D_FF), F32), jax.ShapeDtypeStruct((1, D_FF), F32)),
        grid=(FF_BLKS, nt),
        in_specs=[main(0), past(0), fut(0), main(FF_BLKS), past(FF_BLKS), fut(FF_BLKS),
                  wsp(0), wsp(FF_BLKS), bsp(0), bsp(FF_BLKS), main(0), fut(0)],
        out_specs=(main(0), main(0), wsp(0), wsp(0), bsp(0), bsp(0)),
        compiler_params=_cp(("parallel", "arbitrary")),
        name=name,
    )(u, u, u, u, u, u, conv_w, conv_w, conv_b, conv_b, da, da)


def loss_head(y, target, *, name):
    S, D = y.shape
    T = _tile(S)

    def body(y_ref, t_ref, dy_ref, l_ref):
        i = pl.program_id(0)
        err = y_ref[...] - t_ref[...]
        dy_ref[...] = err * (1.0 / D)

        @pl.when(i == 0)
        def _():
            l_ref[...] = jnp.zeros_like(l_ref)

        l_ref[...] += 0.5 * jnp.sum(jnp.mean(err * err, axis=-1, keepdims=True))

    row = pl.BlockSpec((T, D), lambda i: (i, 0))
    return pl.pallas_call(
        body,
        out_shape=(jax.ShapeDtypeStruct((S, D), F32), jax.ShapeDtypeStruct((8, 128), F32)),
        grid=(S // T,),
        in_specs=[row, row],
        out_specs=(row, pl.BlockSpec((8, 128), lambda i: (0, 0))),
        compiler_params=_cp(("arbitrary",)),
        name=name,
    )(y, target)


ELEMS_PER_BLOCK = 256 * 1024


def _rows_tile(rows, cols):
    if rows * cols <= ELEMS_PER_BLOCK or rows % 8:
        return rows
    best = 8
    for tr in range(8, rows + 1, 8):
        if rows % tr == 0 and tr * cols <= ELEMS_PER_BLOCK:
            best = tr
    return best


def sum_parts(parts, order, *, name):
    P, rows, cols = parts.shape
    tr = _rows_tile(rows, cols)

    def body(p_ref, o_ref):
        acc = p_ref[order[0]]
        for j in order[1:]:
            acc = acc + p_ref[j]
        o_ref[...] = acc

    return pl.pallas_call(
        body,
        out_shape=jax.ShapeDtypeStruct((rows, cols), F32),
        grid=(rows // tr,),
        in_specs=[pl.BlockSpec((P, tr, cols), lambda i: (0, i, 0))],
        out_specs=pl.BlockSpec((tr, cols), lambda i: (i, 0)),
        compiler_params=_cp(("parallel",)),
        name=name,
    )(parts)


def adamw(w, m, v, g_parts, *, name):
    rows, cols = w.shape
    tr = _rows_tile(rows, cols)
    n_parts = len(g_parts)

    def body(*refs):
        w_ref, m_ref, v_ref = refs[:3]
        g_refs = refs[3:3 + n_parts]
        g_out, d_out, m_out, v_out = refs[3 + n_parts:]
        g = g_refs[0][...]
        for r in g_refs[1:]:
            g = g + r[...]
        mn = ADAM_B1 * m_ref[...] + (1.0 - ADAM_B1) * g
        vn = ADAM_B2 * v_ref[...] + (1.0 - ADAM_B2) * (g * g)
        m_hat = mn / (1.0 - ADAM_B1 ** ADAM_STEP)
        v_hat = vn / (1.0 - ADAM_B2 ** ADAM_STEP)
        g_out[...] = g
        d_out[...] = -ADAM_LR * (m_hat / (jnp.sqrt(v_hat) + ADAM_EPS) + ADAM_WD * w_ref[...])
        m_out[...] = mn
        v_out[...] = vn

    spec = pl.BlockSpec((tr, cols), lambda i: (i, 0))
    shp = jax.ShapeDtypeStruct((rows, cols), F32)
    return pl.pallas_call(
        body,
        out_shape=(shp, shp, shp, shp),
        grid=(rows // tr,),
        in_specs=[spec] * (3 + n_parts),
        out_specs=(spec, spec, spec, spec),
        compiler_params=_cp(("parallel",)),
        name=name,
    )(w, m, v, *g_parts)


ANY = pl.BlockSpec(memory_space=pl.ANY)


def _mesh_pos():
    return lax.axis_index("x"), lax.axis_index("y"), lax.axis_index("c")


def _chip_peers(x, y):
    return [(1 - x, y), (x, 1 - y), (1 - x, 1 - y)]


def _all_peers(x, y, c):
    return [((1 - x) if (r >> 2) & 1 else x, (1 - y) if (r >> 1) & 1 else y, (1 - c) if r & 1 else c)
            for r in range(1, 8)]


def _shard_slice(ref, axis, j, size, layer=None):
    idx = [slice(None)] * 3
    idx[axis] = pl.ds(pl.multiple_of(j * size, 128 if axis == 2 else 16), size)
    if layer is not None:
        idx[0] = pl.ds(layer, 1)
    return ref.at[tuple(idx)]


def all_gather_shards(shards, axes, *, name):
    nt = len(shards)
    sizes = [s.shape[a] for s, a in zip(shards, axes)]

    def body(*refs):
        ins, outs = refs[:nt], refs[nt:2 * nt]
        ici_send, ici_recv, d2d_send, d2d_recv, local_sems = refs[2 * nt:]
        x, y, c = _mesh_pos()
        mine = 2 * x + y
        peers = _chip_peers(x, y)

        def ici(t, k, blk):
            px, py = peers[k]
            return pltpu.make_async_remote_copy(
                src_ref=ins[t].at[pl.ds(c, 1)], dst_ref=_shard_slice(outs[t], axes[t], blk, sizes[t], c),
                send_sem=ici_send.at[3 * t + k], recv_sem=ici_recv.at[3 * t + k],
                device_id=(px, py, c), device_id_type=MESH_T)

        def d2d(t, k, layer):
            px, py = peers[k]
            piece = _shard_slice(outs[t], axes[t], 2 * px + py, sizes[t], layer)
            return pltpu.make_async_remote_copy(
                src_ref=piece, dst_ref=piece, send_sem=d2d_send.at[3 * t + k], recv_sem=d2d_recv.at[3 * t + k],
                device_id=(x, y, 1 - c), device_id_type=MESH_T)

        local, sends = [], []
        for t in range(nt):
            cp = pltpu.make_async_copy(ins[t], _shard_slice(outs[t], axes[t], mine, sizes[t]), local_sems.at[t])
            cp.start()
            local.append(cp)
            for k in range(3):
                snd = ici(t, k, mine)
                snd.start()
                sends.append(snd)
        for t in range(nt):
            for k, (px, py) in enumerate(peers):
                ici(t, k, 2 * px + py).wait_recv()
                fwd = d2d(t, k, c)
                fwd.start()
                sends.append(fwd)
        for t in range(nt):
            for k in range(3):
                d2d(t, k, 1 - c).wait_recv()
        for s in sends:
            s.wait_send()
        for cp in local:
            cp.wait()

    out_shape = []
    for s, a, sz in zip(shards, axes, sizes):
        shp = list(s.shape)
        shp[a] = 4 * sz
        out_shape.append(jax.ShapeDtypeStruct(tuple(shp), s.dtype))
    return pl.pallas_call(
        body,
        out_shape=tuple(out_shape),
        in_specs=[ANY] * nt,
        out_specs=tuple([ANY] * nt),
        scratch_shapes=[pltpu.SemaphoreType.DMA((3 * nt,)) for _ in range(4)] + [pltpu.SemaphoreType.DMA((nt,))],
        name=name,
    )(*shards)


def sibling_layer_exchange(grads, *, name):
    nt = len(grads)

    def body(*refs):
        ins, outs = refs[:nt], refs[nt:2 * nt]
        send_sems, recv_sems = refs[2 * nt:]
        x, y, c = _mesh_pos()
        cps = []
        for t in range(nt):
            cp = pltpu.make_async_remote_copy(src_ref=ins[t].at[pl.ds(1 - c, 1)], dst_ref=outs[t],
                                              send_sem=send_sems.at[t], recv_sem=recv_sems.at[t],
                                              device_id=(x, y, 1 - c), device_id_type=MESH_T)
            cp.start()
            cps.append(cp)
        for cp in cps:
            cp.wait_recv()
        for cp in cps:
            cp.wait_send()

    return pl.pallas_call(
        body,
        out_shape=tuple(jax.ShapeDtypeStruct((1,) + g.shape[1:], F32) for g in grads),
        in_specs=[ANY] * nt,
        out_specs=tuple([ANY] * nt),
        scratch_shapes=[pltpu.SemaphoreType.DMA((nt,)), pltpu.SemaphoreType.DMA((nt,))],
        name=name,
    )(*grads)


def pair_sum(grad, other, layer, *, name):
    _, R, C = grad.shape
    tr = _rows_tile(R, C)

    def body(l_ref, g_ref, o_ref, s32_ref, s16_ref):
        s = g_ref[...] + o_ref[...]
        s32_ref[...] = s
        s16_ref[...] = s.astype(BF16)

    blk = pl.BlockSpec((1, tr, C), lambda i, l: (0, i, 0))
    return pl.pallas_call(
        body,
        out_shape=(jax.ShapeDtypeStruct((1, R, C), F32), jax.ShapeDtypeStruct((1, R, C), BF16)),
        grid_spec=pltpu.PrefetchScalarGridSpec(
            num_scalar_prefetch=1, grid=(R // tr,),
            in_specs=[pl.BlockSpec((1, tr, C), lambda i, l: (l[0], i, 0)), blk],
            out_specs=(blk, blk)),
        compiler_params=_cp(("parallel",)),
        name=name,
    )(layer, grad, other)


def scatter_pairs(pairs32, pairs16, axes, small, *, name):
    nt = len(pairs32)
    sizes = [g.shape[a] // 4 for g, a in zip(pairs32, axes)]

    def body(*refs):
        in32, in16, small_in = refs[:nt], refs[nt:2 * nt], refs[2 * nt]
        recv, own, small_out = refs[2 * nt + 1:3 * nt + 1], refs[3 * nt + 1:4 * nt + 1], refs[4 * nt + 1]
        send_sems, recv_sems, local_sems, ssend, srecv = refs[4 * nt + 2:]
        x, y, c = _mesh_pos()
        mine = 2 * x + y
        me = 4 * x + 2 * y + c
        local, sends, recvs = [], [], []
        for t in range(nt):
            cp = pltpu.make_async_copy(_shard_slice(in32[t], axes[t], mine, sizes[t]), own[t], local_sems.at[t])
            cp.start()
            local.append(cp)
            for k, (px, py) in enumerate(_chip_peers(x, y)):
                snd = pltpu.make_async_remote_copy(
                    src_ref=_shard_slice(in16[t], axes[t], 2 * px + py, sizes[t]), dst_ref=recv[t].at[k],
                    send_sem=send_sems.at[3 * t + k], recv_sem=recv_sems.at[3 * t + k],
                    device_id=(px, py, c), device_id_type=MESH_T)
                snd.start()
                sends.append(snd)
                recvs.append(snd)
        cp = pltpu.make_async_copy(small_in, small_out.at[me], local_sems.at[nt])
        cp.start()
        local.append(cp)
        for r, (px, py, pc) in enumerate(_all_peers(x, y, c)):
            def mk(slot, r=r, px=px, py=py, pc=pc):
                return pltpu.make_async_remote_copy(
                    src_ref=small_in, dst_ref=small_out.at[slot], send_sem=ssend.at[r], recv_sem=srecv.at[r],
                    device_id=(px, py, pc), device_id_type=MESH_T)
            snd = mk(me)
            snd.start()
            sends.append(snd)
            recvs.append(mk(4 * px + 2 * py + pc))
        for r in recvs:
            r.wait_recv()
        for s in sends:
            s.wait_send()
        for cp in local:
            cp.wait()

    shard_shapes = []
    for g, a, sz in zip(pairs32, axes, sizes):
        shp = list(g.shape)
        shp[a] = sz
        shard_shapes.append(tuple(shp))
    out_shape = ([jax.ShapeDtypeStruct((3,) + s, BF16) for s in shard_shapes]
                 + [jax.ShapeDtypeStruct(s, F32) for s in shard_shapes]
                 + [jax.ShapeDtypeStruct((8,) + small.shape, F32)])
    return pl.pallas_call(
        body,
        out_shape=tuple(out_shape),
        in_specs=[ANY] * (2 * nt + 1),
        out_specs=tuple([ANY] * (2 * nt + 1)),
        scratch_shapes=[pltpu.SemaphoreType.DMA((3 * nt,)), pltpu.SemaphoreType.DMA((3 * nt,)),
                        pltpu.SemaphoreType.DMA((nt + 1,)), pltpu.SemaphoreType.DMA((7,)),
                        pltpu.SemaphoreType.DMA((7,))],
        name=name,
    )(*pairs32, *pairs16, small)


def sum_chips(own, recv, *, name):
    rows, cols = own.shape
    tr = _rows_tile(rows, cols)

    def body(o_ref, r_ref, s_ref):
        acc = o_ref[...]
        for k in range(3):
            acc = acc + r_ref[k].astype(F32)
        s_ref[...] = acc

    spec = pl.BlockSpec((tr, cols), lambda i: (i, 0))
    return pl.pallas_call(
        body,
        out_shape=jax.ShapeDtypeStruct((rows, cols), F32),
        grid=(rows // tr,),
        in_specs=[spec, pl.BlockSpec((3, tr, cols), lambda i: (0, i, 0))],
        out_specs=spec,
        compiler_params=_cp(("parallel",)),
        name=name,
    )(own, recv)


def sibling_assemble(finals, *, name):
    nt = len(finals)

    def body(*refs):
        ins, outs = refs[:nt], refs[nt:2 * nt]
        send_sems, recv_sems, local_sems = refs[2 * nt:]
        x, y, c = _mesh_pos()
        cps, local = [], []
        for t in range(nt):
            lc = pltpu.make_async_copy(ins[t], outs[t].at[pl.ds(c, 1)], local_sems.at[t])
            lc.start()
            local.append(lc)
            cp = pltpu.make_async_remote_copy(src_ref=ins[t], dst_ref=outs[t].at[pl.ds(c, 1)],
                                              send_sem=send_sems.at[t], recv_sem=recv_sems.at[t],
                                              device_id=(x, y, 1 - c), device_id_type=MESH_T)
            cp.start()
            cps.append(cp)
        for t in range(nt):
            pltpu.make_async_remote_copy(src_ref=ins[t], dst_ref=outs[t].at[pl.ds(1 - c, 1)],
                                         send_sem=send_sems.at[t], recv_sem=recv_sems.at[t],
                                         device_id=(x, y, 1 - c), device_id_type=MESH_T).wait_recv()
        for cp in cps:
            cp.wait_send()
        for lc in local:
            lc.wait()

    return pl.pallas_call(
        body,
        out_shape=tuple(jax.ShapeDtypeStruct((2,) + f.shape[1:], F32) for f in finals),
        in_specs=[ANY] * nt,
        out_specs=tuple([ANY] * nt),
        scratch_shapes=[pltpu.SemaphoreType.DMA((nt,)), pltpu.SemaphoreType.DMA((nt,)),
                        pltpu.SemaphoreType.DMA((nt,))],
        name=name,
    )(*finals)


WEIGHTS = ("norm_mix", "w_in", "b_gate", "q_norm_a", "k_norm_a", "rel_bias_a", "w_pool", "pool_scale",
           "w_branch_a", "w_branch_b", "w_branch_c", "w_out", "norm_ffn", "w_up", "conv_w", "conv_b", "w_down")
SHARDED = {"w_in": 2, "w_branch_a": 2, "w_branch_b": 2, "w_branch_c": 2, "w_out": 1, "w_up": 2, "conv_w": 2,
           "w_down": 1}
REPLICATED = tuple(n for n in WEIGHTS if n not in SHARDED)
SMALL_ROWS = 1232


def _layer_fwd(x, p, tables):
    diag = exact_dot(p["rel_bias_a"], tables["onehot_t"], name="bias_diagonals")
    diag = diag.reshape(N_HEADS, N_VARIANTS, 1, DIAG_W).transpose(1, 0, 2, 3)
    biasm = bias_expand(diag, name="bias_expand")
    gq8 = jnp.tile(p["q_norm_a"], N_HEADS)[None]
    gk8 = jnp.tile(p["k_norm_a"], N_HEADS)[None]
    h = rmsnorm_fwd(x, p["norm_mix"][None], name="rmsnorm_fwd")
    proj = matmul(h, p["w_in"], name="mm_in")
    qa, ka, va, qb, kb, vb = qkv_prep(proj, gq8, gk8, name="qkv_prep")
    oa = attn_a_fwd(qa, ka, va, biasm, name="attn_a_fwd")
    ob, tot, nblk = attn_b_fwd(qb, kb, vb, name="attn_b_fwd")
    wpool = p["w_pool"].astype(BF16)
    oc = pool_fwd(proj, wpool, p["pool_scale"][None], name="pool_fwd")
    merged = merge_fwd(oa, ob, oc, proj, p["b_gate"][None], p["w_branch_a"], p["w_branch_b"], p["w_branch_c"],
                       name="merge_fwd")
    x1 = matmul(merged, p["w_out"], add=x, name="mm_out")
    h2 = rmsnorm_fwd(x1, p["norm_ffn"][None], name="rmsnorm_fwd")
    u = matmul(h2, p["w_up"], name="mm_up")
    a = conv_glu_fwd(u, p["conv_w"], p["conv_b"][None], name="conv_glu_fwd")
    x2 = matmul(a, p["w_down"], add=x1, name="mm_down")
    saved = dict(x=x, h=h, proj=proj, qa=qa, ka=ka, va=va, qb=qb, kb=kb, vb=vb, oa=oa, ob=ob, tot=tot, nblk=nblk, oc=oc,
                 merged=merged, x1=x1, h2=h2, u=u, a=a, biasm=biasm, gq8=gq8, gk8=gk8, wpool=wpool)
    return x2, saved


def _layer_bwd(dx2, s, p, tables):
    g = {}
    da = matmul(dx2, p["w_down"], tb=True, name="mm_down_dx")
    g["w_down"] = matmul(s["a"], dx2, ta=True, name="mm_down_dw")
    dug, duv, dcwg, dcwv, dcbg, dcbv = conv_glu_bwd(s["u"], p["conv_w"], p["conv_b"][None], da, name="conv_glu_bwd")
    du = jnp.concatenate([dug, duv], axis=1)
    g["conv_w"] = jnp.concatenate([dcwg, dcwv], axis=1)
    g["conv_b"] = jnp.concatenate([dcbg, dcbv], axis=1)[0]
    g["w_up"] = matmul(s["h2"], du, ta=True, name="mm_up_dw")
    dh2 = matmul(du, p["w_up"], tb=True, name="mm_up_dx")
    dx1, dg2 = rmsnorm_bwd(s["x1"], p["norm_ffn"][None], dh2, dx2, name="rmsnorm_bwd")
    g["norm_ffn"] = dg2[0]
    dmerged = matmul(dx1, p["w_out"], tb=True, name="mm_out_dx")
    g["w_out"] = matmul(s["merged"], dx1, ta=True, name="mm_out_dw")
    t_a, t_b, t_c, dga, dgb, dgc, dba, dbb, dbc = merge_bwd(
        dmerged, s["oa"], s["ob"], s["oc"], s["proj"], p["b_gate"][None], p["w_branch_a"], p["w_branch_b"],
        p["w_branch_c"], name="merge_bwd")
    g["b_gate"] = jnp.concatenate([dba, dbb, dbc], axis=1)[0]
    g["w_branch_a"] = matmul(s["oa"], t_a, ta=True, name="mm_branch_dw")
    g["w_branch_b"] = matmul(s["ob"], t_b, ta=True, name="mm_branch_dw")
    g["w_branch_c"] = matmul(s["oc"], t_c, ta=True, name="mm_branch_dw")
    doa = matmul(t_a, p["w_branch_a"], tb=True, out_dtype=BF16, name="mm_branch_dx")
    dob = matmul(t_b, p["w_branch_b"], tb=True, out_dtype=BF16, name="mm_branch_dx")
    doc = matmul(t_c, p["w_branch_c"], tb=True, name="mm_branch_dx_f32")
    dqh, dkh, dva, dbias = attn_a_bwd(s["qa"], s["ka"], s["va"], s["biasm"], doa, name="attn_a_bwd")
    ddiag = relbias_reduce(dbias, name="relbias_reduce")
    ddiag = ddiag.transpose(1, 0, 2, 3).reshape(N_HEADS, N_VARIANTS * DIAG_W)
    g["rel_bias_a"] = exact_dot(ddiag, tables["onehot"], name="relbias_table")
    dqa, dka, dgq8, dgk8 = qknorm_bwd(s["proj"], s["gq8"], s["gk8"], dqh, dkh, name="qknorm_bwd")
    g["q_norm_a"] = dgq8.reshape(N_HEADS, HEAD_DIM).sum(axis=0)
    g["k_norm_a"] = dgk8.reshape(N_HEADS, HEAD_DIM).sum(axis=0)
    dqb, dkb, dvb = attn_b_bwd(s["qb"], s["kb"], s["vb"], s["tot"], s["nblk"], dob, name="attn_b_bwd")
    duc, dwp, dsc = pool_bwd(s["proj"], s["wpool"], p["pool_scale"][None], doc, name="pool_bwd")
    g["w_pool"] = dwp
    g["pool_scale"] = dsc[0]
    dproj = jnp.concatenate([dqa, dka, dva.astype(BF16), dqb, dkb.astype(BF16), dvb.astype(BF16), duc,
                             dga, dgb, dgc], axis=1)
    g["w_in"] = matmul(s["h"], dproj, ta=True, name="mm_in_dw")
    dh = matmul(dproj, p["w_in"], tb=True, name="mm_in_dx")
    dx, dg1 = rmsnorm_bwd(s["x"], p["norm_mix"][None], dh, dx1, name="rmsnorm_bwd")
    g["norm_mix"] = dg1[0]
    return dx, g


def kernel(x, norm_mix, w_in, b_gate, q_norm_a, k_norm_a, rel_bias_a, w_pool, pool_scale, w_branch_a, w_branch_b, w_branch_c, w_out, norm_ffn, w_up, conv_w, conv_b, w_down, loss_target, m_norm_mix, m_w_in, m_b_gate, m_q_norm_a, m_k_norm_a, m_rel_bias_a, m_w_pool, m_pool_scale, m_w_branch_a, m_w_branch_b, m_w_branch_c, m_w_out, m_norm_ffn, m_w_up, m_conv_w, m_conv_b, m_w_down, v_norm_mix, v_w_in, v_b_gate, v_q_norm_a, v_k_norm_a, v_rel_bias_a, v_w_pool, v_pool_scale, v_w_branch_a, v_w_branch_b, v_w_branch_c, v_w_out, v_norm_ffn, v_w_up, v_conv_w, v_conv_b, v_w_down):
    w = dict(zip(WEIGHTS, (norm_mix, w_in, b_gate, q_norm_a, k_norm_a, rel_bias_a, w_pool, pool_scale, w_branch_a,
                           w_branch_b, w_branch_c, w_out, norm_ffn, w_up, conv_w, conv_b, w_down)))
    m = dict(zip(WEIGHTS, (m_norm_mix, m_w_in, m_b_gate, m_q_norm_a, m_k_norm_a, m_rel_bias_a, m_w_pool, m_pool_scale,
                           m_w_branch_a, m_w_branch_b, m_w_branch_c, m_w_out, m_norm_ffn, m_w_up, m_conv_w, m_conv_b,
                           m_w_down)))
    v = dict(zip(WEIGHTS, (v_norm_mix, v_w_in, v_b_gate, v_q_norm_a, v_k_norm_a, v_rel_bias_a, v_w_pool, v_pool_scale,
                           v_w_branch_a, v_w_branch_b, v_w_branch_c, v_w_out, v_norm_ffn, v_w_up, v_conv_w, v_conv_b,
                           v_w_down)))
    onehot = diagonal_onehot()
    tables = dict(onehot=jnp.asarray(onehot), onehot_t=jnp.asarray(np.ascontiguousarray(onehot.T)))

    names = tuple(SHARDED)
    shards = [w[n] if n == "conv_w" else w[n].astype(BF16) for n in names]
    full = dict(zip(names, all_gather_shards(shards, [SHARDED[n] for n in names], name="all_gather_weights")))

    def layer_params(l):
        p = {n: full[n][l] for n in names}
        p.update({n: w[n][l] for n in REPLICATED})
        return p

    xs = x[0]
    saved = []
    for l in range(DEPTH):
        xs, s = _layer_fwd(xs, layer_params(l), tables)
        saved.append(s)
    dx, lpart = loss_head(xs, loss_target[0], name="loss_head")
    loss = lax.psum(lpart[0, 0], MESH_AXES)
    grads = [None] * DEPTH
    for l in reversed(range(DEPTH)):
        dx, grads[l] = _layer_bwd(dx, saved[l], layer_params(l), tables)
    g = {n: jnp.stack([grads[l][n] for l in range(DEPTH)]) for n in WEIGHTS}

    flat = jnp.concatenate([g[n].reshape(-1) for n in REPLICATED])
    small = jnp.pad(flat, (0, SMALL_ROWS * 128 - flat.shape[0])).reshape(SMALL_ROWS, 128)
    axes = [SHARDED[n] for n in names]
    nt = len(names)
    layer = jnp.reshape(lax.axis_index("c"), (1,)).astype(jnp.int32)
    others = sibling_layer_exchange([g[n] for n in names], name="sibling_layer_exchange")
    pairs = [pair_sum(g[n], o, layer, name="pair_sum") for n, o in zip(names, others)]
    outs = scatter_pairs([p[0] for p in pairs], [p[1] for p in pairs], axes, small, name="scatter_pairs")
    finals = []
    for recv, own in zip(outs[:nt], outs[nt:2 * nt]):
        _, r, c = own.shape
        finals.append(sum_chips(own.reshape(r, c), recv.reshape(3, r, c), name="sum_chips").reshape(1, r, c))
    reduced = sibling_assemble(finals, name="sibling_assemble")
    small_sum = sum_parts(outs[-1], tuple(range(8)), name="sum_devices").reshape(-1)

    res = {}
    for n, gn in zip(names, reduced):
        shp = w[n].shape
        two_d = lambda t: t.reshape(shp[0] * shp[1], shp[2])
        res[n] = [t.reshape(shp) for t in adamw(two_d(w[n]), two_d(m[n]), two_d(v[n]), [two_d(gn)], name="adamw")]
    off = 0
    for n in REPLICATED:
        shp = w[n].shape
        size = int(np.prod(shp))
        gn = small_sum[off:off + size]
        off += size
        cols = shp[-1]
        two_d = lambda t: t.reshape(size // cols, cols)
        res[n] = [t.reshape(shp) for t in adamw(two_d(w[n]), two_d(m[n]), two_d(v[n]), [two_d(gn)], name="adamw")]

    out = [loss, dx[None]]
    for k in range(4):
        out.extend(res[n][k] for n in WEIGHTS)
    return tuple(out)
```

```python
import math

import jax
import jax.numpy as jnp
import numpy as np
from jax import lax
from jax.experimental import pallas as pl
from jax.experimental.pallas import tpu as pltpu

F32 = jnp.float32
BF16 = jnp.bfloat16

D_MODEL = 1024
DEPTH = 2
CHUNK = 64
N_LEFT = 8
HEAD_DIM = 64
N_HEADS = 8
WIDTH = 512
POOL_WINDOWS = (2, 4, 8, 16)
GROUP_DIM = 128
MAX_REL = 2 * CHUNK
REL_TABLE = MAX_REL + CHUNK
D_FF = 2816
EPS = 1e-6
IN_COLS = 7 * WIDTH + 3 * D_MODEL
GATE_COL0 = 7 * WIDTH

ADAM_LR = 0.001
ADAM_B1 = 0.9
ADAM_B2 = 0.999
ADAM_EPS = 1e-08
ADAM_WD = 0.01
ADAM_STEP = 10

VMEM_LIMIT = 56 * 1024 * 1024
ATT_Q = 256
A_WIN = ATT_Q + N_LEFT * CHUNK
HALO = 16
CONV_HALO = 8
NEG = -1e30

MESH_AXES = ("x", "y", "c")
MESH_T = pl.DeviceIdType.MESH


def _cp(sem=None, vmem=VMEM_LIMIT):
    return pltpu.CompilerParams(dimension_semantics=sem, vmem_limit_bytes=vmem)


def _dot(a, b, ca, cb):
    return lax.dot_general(a, b, (((ca,), (cb,)), ((), ())), preferred_element_type=F32)


def _tile(n, cands=(512, 256, 128)):
    for c in cands:
        if n % c == 0:
            return c
    return n


def _split_hi_lo(v):
    hi = v.astype(BF16)
    lo = (v - hi.astype(F32)).astype(BF16)
    return hi, lo


def matmul(a, b, *, ta=False, tb=False, add=None, out_dtype=F32, name):
    if ta:
        K, M = a.shape
    else:
        M, K = a.shape
    if tb:
        N, K2 = b.shape
    else:
        K2, N = b.shape
    assert K == K2, (a.shape, b.shape, ta, tb)
    big = (1024, 1408, 512, 256, 128)
    tm, tn, tk = _tile(M, big), _tile(N, big), _tile(K, big)
    nk = K // tk

    def body(*refs):
        if add is None:
            a_ref, b_ref, o_ref, acc = refs
        else:
            a_ref, b_ref, r_ref, o_ref, acc = refs
        k = pl.program_id(2)

        @pl.when(k == 0)
        def _():
            acc[...] = jnp.zeros_like(acc)

        av = a_ref[...].astype(BF16)
        bv = b_ref[...].astype(BF16)
        acc[...] += _dot(av, bv, 0 if ta else 1, 1 if tb else 0)

        @pl.when(k == nk - 1)
        def _():
            r = acc[...]
            if add is not None:
                r = r + r_ref[...].astype(F32)
            o_ref[...] = r.astype(out_dtype)

    a_spec = pl.BlockSpec((tk, tm), lambda i, j, k: (k, i)) if ta else pl.BlockSpec((tm, tk), lambda i, j, k: (i, k))
    b_spec = pl.BlockSpec((tn, tk), lambda i, j, k: (j, k)) if tb else pl.BlockSpec((tk, tn), lambda i, j, k: (k, j))
    o_spec = pl.BlockSpec((tm, tn), lambda i, j, k: (i, j))
    in_specs = [a_spec, b_spec]
    args = [a, b]
    if add is not None:
        in_specs.append(o_spec)
        args.append(add)
    return pl.pallas_call(
        body,
        out_shape=jax.ShapeDtypeStruct((M, N), out_dtype),
        grid=(M // tm, N // tn, nk),
        in_specs=in_specs,
        out_specs=o_spec,
        scratch_shapes=[pltpu.VMEM((tm, tn), F32)],
        compiler_params=_cp(("parallel", "parallel", "arbitrary")),
        name=name,
    )(*args)


def rmsnorm_fwd(x, g, *, name):
    S, D = x.shape
    T = _tile(S)

    def body(x_ref, g_ref, h_ref):
        xv = x_ref[...]
        r = lax.rsqrt(jnp.mean(xv * xv, axis=-1, keepdims=True) + EPS)
        h_ref[...] = (xv * r * g_ref[...]).astype(BF16)

    return pl.pallas_call(
        body,
        out_shape=jax.ShapeDtypeStruct((S, D), BF16),
        grid=(S // T,),
        in_specs=[pl.BlockSpec((T, D), lambda i: (i, 0)), pl.BlockSpec((1, D), lambda i: (0, 0))],
        out_specs=pl.BlockSpec((T, D), lambda i: (i, 0)),
        compiler_params=_cp(("parallel",)),
        name=name,
    )(x, g)


def rmsnorm_bwd(x, g, dh, dres, *, name):
    S, D = x.shape
    T = _tile(S)

    def body(x_ref, g_ref, dh_ref, dres_ref, dx_ref, dg_ref):
        i = pl.program_id(0)
        xv = x_ref[...]
        dhv = dh_ref[...].astype(F32)
        r = lax.rsqrt(jnp.mean(xv * xv, axis=-1, keepdims=True) + EPS)
        gd = dhv * g_ref[...]
        m = jnp.mean(xv * gd, axis=-1, keepdims=True)
        dx_ref[...] = dres_ref[...] + r * gd - xv * (r * r * r * m)

        @pl.when(i == 0)
        def _():
            dg_ref[...] = jnp.zeros_like(dg_ref)

        dg_ref[...] += jnp.sum(dhv * xv * r, axis=0, keepdims=True)

    row = pl.BlockSpec((T, D), lambda i: (i, 0))
    vec = pl.BlockSpec((1, D), lambda i: (0, 0))
    return pl.pallas_call(
        body,
        out_shape=(jax.ShapeDtypeStruct((S, D), F32), jax.ShapeDtypeStruct((1, D), F32)),
        grid=(S // T,),
        in_specs=[row, vec, row, row],
        out_specs=(row, vec),
        compiler_params=_cp(("arbitrary",)),
        name=name,
    )(x, g, dh, dres)


def _head_mean_matrix():
    r = lax.broadcasted_iota(jnp.int32, (WIDTH, WIDTH), 0) // HEAD_DIM
    c = lax.broadcasted_iota(jnp.int32, (WIDTH, WIDTH), 1) // HEAD_DIM
    return jnp.where(r == c, 1.0 / HEAD_DIM, 0.0).astype(BF16)


def _head_mean(v, mm):
    hi, lo = _split_hi_lo(v)
    return _dot(hi, mm, 1, 0) + _dot(lo, mm, 1, 0)


def qkv_prep(proj, gq, gk, *, name):
    S = proj.shape[0]
    T = _tile(S)

    def body(qa, ka, va, qb, kb, vb, gq_ref, gk_ref, oqa, oka, ova, oqb, okb, ovb):
        mm = _head_mean_matrix()
        for src, gref, dst in ((qa, gq_ref, oqa), (ka, gk_ref, oka)):
            v = src[...]
            r = lax.rsqrt(_head_mean(v * v, mm) + EPS)
            dst[...] = (v * r * gref[...]).astype(BF16)
        for src, dst in ((va, ova), (qb, oqb), (kb, okb), (vb, ovb)):
            dst[...] = src[...].astype(BF16)

    col = lambda j: pl.BlockSpec((T, WIDTH), lambda i, j=j: (i, j))
    vec = pl.BlockSpec((1, WIDTH), lambda i: (0, 0))
    out = pl.BlockSpec((T, WIDTH), lambda i: (i, 0))
    return pl.pallas_call(
        body,
        out_shape=tuple(jax.ShapeDtypeStruct((S, WIDTH), BF16) for _ in range(6)),
        grid=(S // T,),
        in_specs=[col(0), col(1), col(2), col(3), col(4), col(5), vec, vec],
        out_specs=tuple(out for _ in range(6)),
        compiler_params=_cp(("parallel",)),
        name=name,
    )(proj, proj, proj, proj, proj, proj, gq, gk)


def qknorm_bwd(proj, gq, gk, dqh, dkh, *, name):
    S = proj.shape[0]
    T = _tile(S)

    def body(qa, ka, gq_ref, gk_ref, dq_ref, dk_ref, oq, ok, ogq, ogk):
        i = pl.program_id(0)
        mm = _head_mean_matrix()

        @pl.when(i == 0)
        def _():
            ogq[...] = jnp.zeros_like(ogq)
            ogk[...] = jnp.zeros_like(ogk)

        for src, gref, dref, dst, gdst in ((qa, gq_ref, dq_ref, oq, ogq), (ka, gk_ref, dk_ref, ok, ogk)):
            v = src[...]
            dy = dref[...]
            r = lax.rsqrt(_head_mean(v * v, mm) + EPS)
            gd = dy * gref[...]
            m = _head_mean(v * gd, mm)
            dst[...] = (r * gd - v * (r * r * r * m)).astype(BF16)
            gdst[...] += jnp.sum(dy * v * r, axis=0, keepdims=True)

    col = lambda j: pl.BlockSpec((T, WIDTH), lambda i, j=j: (i, j))
    vec = pl.BlockSpec((1, WIDTH), lambda i: (0, 0))
    row = pl.BlockSpec((T, WIDTH), lambda i: (i, 0))
    return pl.pallas_call(
        body,
        out_shape=(jax.ShapeDtypeStruct((S, WIDTH), BF16), jax.ShapeDtypeStruct((S, WIDTH), BF16),
                   jax.ShapeDtypeStruct((1, WIDTH), F32), jax.ShapeDtypeStruct((1, WIDTH), F32)),
        grid=(S // T,),
        in_specs=[col(0), col(1), vec, vec, row, row],
        out_specs=(row, row, vec, vec),
        compiler_params=_cp(("arbitrary",)),
        name=name,
    )(proj, proj, gq, gk, dqh, dkh)


DIAG_W = 1024
N_VARIANTS = 3


def diagonal_onehot():
    jj = np.arange(DIAG_W)
    diff = np.where(jj < A_WIN, jj, jj - DIAG_W)
    out = np.zeros((N_VARIANTS, DIAG_W, REL_TABLE), np.float32)
    for v in range(N_VARIANTS):
        rel = np.clip(ATT_Q * v - diff, -(CHUNK - 1), MAX_REL) + (CHUNK - 1)
        out[v, jj, rel] = 1.0
    return out.reshape(N_VARIANTS * DIAG_W, REL_TABLE)


def exact_dot(a, b, *, name):
    def body(a_ref, b_ref, o_ref):
        o_ref[...] = jnp.dot(a_ref[...], b_ref[...], precision=lax.Precision.HIGHEST, preferred_element_type=F32)

    return pl.pallas_call(body, out_shape=jax.ShapeDtypeStruct((a.shape[0], b.shape[1]), F32),
                          compiler_params=_cp(), name=name)(a, b)


def _band_valid(v):
    qc = (lax.broadcasted_iota(jnp.int32, (ATT_Q, A_WIN), 0) + ATT_Q * v) // CHUNK
    kc = lax.broadcasted_iota(jnp.int32, (ATT_Q, A_WIN), 1) // CHUNK
    return (kc <= qc) & (kc >= qc - N_LEFT)


def bias_expand(diag, *, name):
    def body(d_ref, o_ref):
        rows = jnp.broadcast_to(d_ref[0, 0], (ATT_Q, DIAG_W))
        skew = pltpu.roll(rows, 0, 1, stride=1, stride_axis=0)
        o_ref[0, 0] = jnp.where(_band_valid(pl.program_id(0)), skew[:, :A_WIN], NEG)

    return pl.pallas_call(
        body,
        out_shape=jax.ShapeDtypeStruct((N_VARIANTS, N_HEADS, ATT_Q, A_WIN), F32),
        grid=(N_VARIANTS, N_HEADS),
        in_specs=[pl.BlockSpec((1, 1, 1, DIAG_W), lambda v, h: (v, h, 0, 0))],
        out_specs=pl.BlockSpec((1, 1, ATT_Q, A_WIN), lambda v, h: (v, h, 0, 0)),
        compiler_params=_cp(("parallel", "parallel")),
        name=name,
    )(diag)


def relbias_reduce(dbias, *, name):
    def body(db_ref, o_ref):
        x = jnp.concatenate([db_ref[0, 0], jnp.zeros((ATT_Q, DIAG_W - A_WIN), F32)], axis=1)
        row = lax.broadcasted_iota(jnp.int32, (ATT_Q, DIAG_W), 0)
        for b in range(8):
            x = jnp.where((row >> b) & 1 == 1, pltpu.roll(x, DIAG_W - (1 << b), 1), x)
        o_ref[0, 0] = jnp.sum(x, axis=0, keepdims=True)

    return pl.pallas_call(
        body,
        out_shape=jax.ShapeDtypeStruct((N_VARIANTS, N_HEADS, 1, DIAG_W), F32),
        grid=(N_VARIANTS, N_HEADS),
        in_specs=[pl.BlockSpec((1, 1, ATT_Q, A_WIN), lambda v, h: (v, h, 0, 0))],
        out_specs=pl.BlockSpec((1, 1, 1, DIAG_W), lambda v, h: (v, h, 0, 0)),
        compiler_params=_cp(("parallel", "parallel")),
        name=name,
    )(dbias)


def _a_window_start(qb):
    return pl.multiple_of(jnp.maximum(qb * ATT_Q - N_LEFT * CHUNK, 0), ATT_Q)


def attn_a_fwd(q, k, v, biasm, *, name):
    S = q.shape[0]
    nq = S // ATT_Q

    def body(q_ref, k_ref, v_ref, b_ref, o_ref):
        qb = pl.program_id(1)
        start = _a_window_start(qb)
        outs = []
        for h in range(2):
            lanes = slice(h * HEAD_DIM, (h + 1) * HEAD_DIM)
            qh = q_ref[:, lanes]
            kw = k_ref[pl.ds(start, A_WIN), lanes]
            vw = v_ref[pl.ds(start, A_WIN), lanes]
            s = _dot(qh, kw, 1, 1) * (1.0 / math.sqrt(HEAD_DIM)) + b_ref[0, h]
            m = jnp.max(s, axis=-1, keepdims=True)
            e = jnp.exp(s - m)
            p = e / jnp.sum(e, axis=-1, keepdims=True)
            outs.append(_dot(p.astype(BF16), vw, 1, 0))
        o_ref[...] = jnp.concatenate(outs, axis=1).astype(BF16)

    qspec = pl.BlockSpec((ATT_Q, 2 * HEAD_DIM), lambda hp, qb: (qb, hp))
    kvspec = pl.BlockSpec((S, 2 * HEAD_DIM), lambda hp, qb: (0, hp))
    bspec = pl.BlockSpec((1, 2, ATT_Q, A_WIN), lambda hp, qb: (jnp.minimum(qb, 2), hp, 0, 0))
    return pl.pallas_call(
        body,
        out_shape=jax.ShapeDtypeStruct((S, WIDTH), BF16),
        grid=(N_HEADS // 2, nq),
        in_specs=[qspec, kvspec, kvspec, bspec],
        out_specs=qspec,
        compiler_params=_cp(("parallel", "arbitrary")),
        name=name,
    )(q, k, v, biasm)


def attn_a_bwd(q, k, v, biasm, do, *, name):
    S = q.shape[0]
    nq = S // ATT_Q
    scale = 1.0 / math.sqrt(HEAD_DIM)

    def body(q_ref, k_ref, v_ref, b_ref, do_ref, dq_ref, dk_ref, dv_ref, db_ref):
        qb = pl.program_id(1)
        start = _a_window_start(qb)

        @pl.when(qb == 0)
        def _():
            dk_ref[...] = jnp.zeros_like(dk_ref)
            dv_ref[...] = jnp.zeros_like(dv_ref)

        @pl.when(qb <= 2)
        def _():
            db_ref[...] = jnp.zeros_like(db_ref)

        dqs = []
        for h in range(2):
            lanes = slice(h * HEAD_DIM, (h + 1) * HEAD_DIM)
            qh = q_ref[:, lanes]
            doh = do_ref[:, lanes]
            kw = k_ref[pl.ds(start, A_WIN), lanes]
            vw = v_ref[pl.ds(start, A_WIN), lanes]
            s = _dot(qh, kw, 1, 1) * scale + b_ref[0, h]
            m = jnp.max(s, axis=-1, keepdims=True)
            e = jnp.exp(s - m)
            p = e / jnp.sum(e, axis=-1, keepdims=True)
            dp = _dot(doh, vw, 1, 1)
            delta = jnp.sum(p * dp, axis=-1, keepdims=True)
            ds = p * (dp - delta)
            db_ref[0, h] += ds
            dsb = ds.astype(BF16)
            dqs.append(_dot(dsb, kw, 1, 0) * scale)
            dk_ref[pl.ds(start, A_WIN), lanes] += _dot(dsb, qh, 0, 0) * scale
            dv_ref[pl.ds(start, A_WIN), lanes] += _dot(p.astype(BF16), doh, 0, 0)
        dq_ref[...] = jnp.concatenate(dqs, axis=1)

    qspec = pl.BlockSpec((ATT_Q, 2 * HEAD_DIM), lambda hp, qb: (qb, hp))
    kvspec = pl.BlockSpec((S, 2 * HEAD_DIM), lambda hp, qb: (0, hp))
    bspec = pl.BlockSpec((1, 2, ATT_Q, A_WIN), lambda hp, qb: (jnp.minimum(qb, 2), hp, 0, 0))
    return pl.pallas_call(
        body,
        out_shape=(jax.ShapeDtypeStruct((S, WIDTH), F32), jax.ShapeDtypeStruct((S, WIDTH), F32),
                   jax.ShapeDtypeStruct((S, WIDTH), F32), jax.ShapeDtypeStruct((3, N_HEADS, ATT_Q, A_WIN), F32)),
        grid=(N_HEADS // 2, nq),
        in_specs=[qspec, kvspec, kvspec, bspec, qspec],
        out_specs=(qspec, kvspec, kvspec, bspec),
        compiler_params=_cp(("parallel", "arbitrary")),
        name=name,
    )(q, k, v, biasm, do)


def _tri(kind):
    j = lax.broadcasted_iota(jnp.int32, (ATT_Q, ATT_Q), 0)
    s = lax.broadcasted_iota(jnp.int32, (ATT_Q, ATT_Q), 1)
    if kind == "gt":
        m = j > s
    elif kind == "le":
        m = j <= s
    else:
        m = j < s
    return jnp.where(m, 1.0, 0.0).astype(BF16)


def _cum(v, tri):
    hi, lo = _split_hi_lo(v)
    return _dot(hi, tri, 1, 0) + _dot(lo, tri, 1, 0)


def _log_keep(z, mask):
    sp = jnp.maximum(z, 0.0) + jnp.log(1.0 + jnp.exp(-jnp.abs(z)))
    return jnp.where(mask, -sp, 0.0)


def _before(qb, kb):
    row = lax.broadcasted_iota(jnp.int32, (ATT_Q, ATT_Q), 0)
    col = lax.broadcasted_iota(jnp.int32, (ATT_Q, ATT_Q), 1)
    return (col - row) < (qb - kb) * ATT_Q


EXIT_LOG = -104.0


def attn_b_fwd(q, k, v, *, name):
    S = q.shape[0]
    nq = S // ATT_Q
    scale = 1.0 / math.sqrt(HEAD_DIM)

    def body(q_ref, k_ref, v_ref, o_ref, t_ref, n_ref):
        hp = pl.program_id(0)
        qb = pl.program_id(1)
        tri = _tri("gt")

        def cond(state):
            it, cmax, _ = state
            return jnp.logical_and(it <= qb, cmax >= EXIT_LOG)

        def step(state):
            it, _, carry = state
            kb = qb - it
            ks = pl.multiple_of(kb * ATT_Q, ATT_Q)
            mask = _before(qb, kb)
            new = []
            for h in range(2):
                lanes = slice(h * HEAD_DIM, (h + 1) * HEAD_DIM)
                c, acc = carry[h]
                z = _dot(q_ref[:, lanes], k_ref[pl.ds(ks, ATT_Q), lanes], 1, 1) * scale
                L = _log_keep(z, mask)
                tail = _cum(L, tri) + c
                w = jnp.where(mask, jnp.exp(z + L + tail), 0.0)
                acc = acc + _dot(w.astype(BF16), v_ref[pl.ds(ks, ATT_Q), lanes], 1, 0)
                c = c + jnp.sum(L, axis=-1, keepdims=True)
                new.append((c, acc))
            cmax = jnp.maximum(jnp.max(new[0][0]), jnp.max(new[1][0]))
            return it + 1, cmax, tuple(new)

        init = tuple((jnp.zeros((ATT_Q, 1), F32), jnp.zeros((ATT_Q, HEAD_DIM), F32)) for _ in range(2))
        visited, _, res = lax.while_loop(cond, step, (jnp.int32(0), jnp.float32(0.0), init))
        o_ref[...] = jnp.concatenate([res[0][1], res[1][1]], axis=1).astype(BF16)
        t_ref[...] = jnp.concatenate([jnp.broadcast_to(res[h][0], (ATT_Q, HEAD_DIM)) for h in range(2)], axis=1)
        n_ref[hp, qb] = visited.astype(F32)

    qspec = pl.BlockSpec((ATT_Q, 2 * HEAD_DIM), lambda hp, qb: (qb, hp))
    kvspec = pl.BlockSpec((S, 2 * HEAD_DIM), lambda hp, qb: (0, hp))
    return pl.pallas_call(
        body,
        out_shape=(jax.ShapeDtypeStruct((S, WIDTH), BF16), jax.ShapeDtypeStruct((S, WIDTH), F32),
                   jax.ShapeDtypeStruct((N_HEADS // 2, nq), F32)),
        grid=(N_HEADS // 2, nq),
        in_specs=[qspec, kvspec, kvspec],
        out_specs=(qspec, qspec, pl.BlockSpec(memory_space=pltpu.SMEM)),
        compiler_params=_cp(("arbitrary", "arbitrary")),
        name=name,
    )(q, k, v)


def attn_b_bwd(q, k, v, tot, nblk, do, *, name):
    S = q.shape[0]
    nq = S // ATT_Q
    scale = 1.0 / math.sqrt(HEAD_DIM)

    def body(q_ref, k_ref, v_ref, t_ref, n_ref, do_ref, dq_ref, dk_ref, dv_ref):
        hp = pl.program_id(0)
        qb = pl.program_id(1)
        first = jnp.clip(qb + 1 - n_ref[hp, qb].astype(jnp.int32), 0, qb + 1)
        tri_le = _tri("le")
        tri_lt = _tri("lt")

        @pl.when(qb == 0)
        def _():
            dk_ref[...] = jnp.zeros_like(dk_ref)
            dv_ref[...] = jnp.zeros_like(dv_ref)

        def step(kb, carry):
            ks = pl.multiple_of(kb * ATT_Q, ATT_Q)
            mask = _before(qb, kb)
            new = []
            for h in range(2):
                lanes = slice(h * HEAD_DIM, (h + 1) * HEAD_DIM)
                cl, cg, dq = carry[h]
                qh = q_ref[:, lanes]
                doh = do_ref[:, lanes]
                kh = k_ref[pl.ds(ks, ATT_Q), lanes]
                vh = v_ref[pl.ds(ks, ATT_Q), lanes]
                totl = t_ref[:, h * HEAD_DIM:h * HEAD_DIM + 1]
                z = _dot(qh, kh, 1, 1) * scale
                L = _log_keep(z, mask)
                tail = totl - (_cum(L, tri_le) + cl)
                sig = jnp.exp(z + L)
                w = jnp.where(mask, sig * jnp.exp(tail), 0.0)
                g = w * _dot(doh, vh, 1, 1)
                G = _cum(g, tri_lt) + cg
                dz = jnp.where(mask, g * (1.0 - sig) - sig * G, 0.0).astype(BF16)
                dq = dq + _dot(dz, kh, 1, 0)
                dk_ref[pl.ds(ks, ATT_Q), lanes] += _dot(dz, qh, 0, 0) * scale
                dv_ref[pl.ds(ks, ATT_Q), lanes] += _dot(w.astype(BF16), doh, 0, 0)
                cl = cl + jnp.sum(L, axis=-1, keepdims=True)
                cg = cg + jnp.sum(g, axis=-1, keepdims=True)
                new.append((cl, cg, dq))
            return tuple(new)

        init = tuple((jnp.zeros((ATT_Q, 1), F32), jnp.zeros((ATT_Q, 1), F32), jnp.zeros((ATT_Q, HEAD_DIM), F32))
                     for _ in range(2))
        res = lax.fori_loop(first, qb + 1, step, init)
        dq_ref[...] = (jnp.concatenate([res[0][2], res[1][2]], axis=1) * scale).astype(BF16)

    qspec = pl.BlockSpec((ATT_Q, 2 * HEAD_DIM), lambda hp, qb: (qb, hp))
    kvspec = pl.BlockSpec((S, 2 * HEAD_DIM), lambda hp, qb: (0, hp))
    return pl.pallas_call(
        body,
        out_shape=(jax.ShapeDtypeStruct((S, WIDTH), BF16), jax.ShapeDtypeStruct((S, WIDTH), F32),
                   jax.ShapeDtypeStruct((S, WIDTH), F32)),
        grid=(N_HEADS // 2, nq),
        in_specs=[qspec, kvspec, kvspec, qspec, pl.BlockSpec(memory_space=pltpu.SMEM), qspec],
        out_specs=(qspec, kvspec, kvspec),
        compiler_params=_cp(("parallel", "arbitrary")),
        name=name,
    )(q, k, v, tot, nblk, do)


U_COLBLK = 6


def _pool_counts(t0, rows):
    t = t0 + lax.broadcasted_iota(jnp.int32, (rows, WIDTH), 0)
    lane_grp = lax.broadcasted_iota(jnp.int32, (rows, WIDTH), 1) // GROUP_DIM
    win = jnp.where(lane_grp == 0, 2, jnp.where(lane_grp == 1, 4, jnp.where(lane_grp == 2, 8, 16)))
    cnt = jnp.minimum(t + 1, win)
    return 1.0 / cnt.astype(F32), lane_grp


def _window_sums(ext, shift_fn):
    s2 = ext + shift_fn(ext, 1)
    s4 = s2 + shift_fn(s2, 2)
    s8 = s4 + shift_fn(s4, 4)
    s16 = s8 + shift_fn(s8, 8)
    return s2, s4, s8, s16


def _select_group(lane_grp, s2, s4, s8, s16):
    return jnp.where(lane_grp == 0, s2, jnp.where(lane_grp == 1, s4, jnp.where(lane_grp == 2, s8, s16)))


def _pooled_tile(u_ref, h_ref, i, T):
    halo = jnp.where(i > 0, h_ref[...], 0.0)
    ext = jnp.concatenate([halo, u_ref[...]], axis=0)
    n = T + HALO
    sums = _window_sums(ext, lambda v, k: pltpu.roll(v, k, 0))
    inv, lane_grp = _pool_counts(i * T - HALO, n)
    pooled = _select_group(lane_grp, *sums) * inv - ext
    return pooled[HALO:, :]


def pool_fwd(proj, w_pool, scale, *, name):
    S = proj.shape[0]
    T = _tile(S)
    hb = T // HALO

    def body(u_ref, h_ref, w_ref, s_ref, o_ref):
        i = pl.program_id(0)
        pooled = _pooled_tile(u_ref, h_ref, i, T).astype(BF16)
        outs = [_dot(pooled[:, g * GROUP_DIM:(g + 1) * GROUP_DIM], w_ref[g], 1, 0) for g in range(4)]
        o_ref[...] = (jnp.concatenate(outs, axis=1) * s_ref[...]).astype(BF16)

    return pl.pallas_call(
        body,
        out_shape=jax.ShapeDtypeStruct((S, WIDTH), BF16),
        grid=(S // T,),
        in_specs=[pl.BlockSpec((T, WIDTH), lambda i: (i, U_COLBLK)),
                  pl.BlockSpec((HALO, WIDTH), lambda i: (jnp.maximum(i * hb - 1, 0), U_COLBLK)),
                  pl.BlockSpec((4, GROUP_DIM, GROUP_DIM), lambda i: (0, 0, 0)),
                  pl.BlockSpec((1, WIDTH), lambda i: (0, 0))],
        out_specs=pl.BlockSpec((T, WIDTH), lambda i: (i, 0)),
        compiler_params=_cp(("parallel",)),
        name=name,
    )(proj, proj, w_pool, scale)


def pool_bwd(proj, w_pool, scale, do, *, name):
    S = proj.shape[0]
    T = _tile(S)
    hb = T // HALO
    nt = S // T

    def body(u_ref, h_ref, w_ref, s_ref, do_ref, dof_ref, du_ref, dw_ref, ds_ref):
        i = pl.program_id(0)

        @pl.when(i == 0)
        def _():
            dw_ref[...] = jnp.zeros_like(dw_ref)
            ds_ref[...] = jnp.zeros_like(ds_ref)

        pooled = _pooled_tile(u_ref, h_ref, i, T).astype(BF16)
        dov = do_ref[...].astype(F32)
        fut = jnp.where(i < nt - 1, dof_ref[...].astype(F32), 0.0)
        dmix = (jnp.concatenate([dov, fut], axis=0) * s_ref[...]).astype(BF16)
        mixed, dpool = [], []
        for g in range(4):
            lanes = slice(g * GROUP_DIM, (g + 1) * GROUP_DIM)
            mixed.append(_dot(pooled[:, lanes], w_ref[g], 1, 0))
            dw_ref[g] += _dot(pooled[:, lanes], dmix[:T, lanes], 0, 0)
            dpool.append(_dot(dmix[:, lanes], w_ref[g], 1, 1))
        ds_ref[...] += jnp.sum(dov * jnp.concatenate(mixed, axis=1), axis=0, keepdims=True)
        dp = jnp.concatenate(dpool, axis=1)
        n = T + HALO
        inv, lane_grp = _pool_counts(i * T, n)
        sums = _window_sums(dp * inv, lambda v, k: pltpu.roll(v, n - k, 0))
        du = _select_group(lane_grp, *sums) - dp
        du_ref[...] = du[:T, :].astype(BF16)

    row = pl.BlockSpec((T, WIDTH), lambda i: (i, 0))
    return pl.pallas_call(
        body,
        out_shape=(jax.ShapeDtypeStruct((S, WIDTH), BF16), jax.ShapeDtypeStruct((4, GROUP_DIM, GROUP_DIM), F32),
                   jax.ShapeDtypeStruct((1, WIDTH), F32)),
        grid=(nt,),
        in_specs=[pl.BlockSpec((T, WIDTH), lambda i: (i, U_COLBLK)),
                  pl.BlockSpec((HALO, WIDTH), lambda i: (jnp.maximum(i * hb - 1, 0), U_COLBLK)),
                  pl.BlockSpec((4, GROUP_DIM, GROUP_DIM), lambda i: (0, 0, 0)),
                  pl.BlockSpec((1, WIDTH), lambda i: (0, 0)),
                  row,
                  pl.BlockSpec((HALO, WIDTH), lambda i: (jnp.minimum((i + 1) * hb, S // HALO - 1), 0))],
        out_specs=(row, pl.BlockSpec((4, GROUP_DIM, GROUP_DIM), lambda i: (0, 0, 0)),
                   pl.BlockSpec((1, WIDTH), lambda i: (0, 0))),
        compiler_params=_cp(("arbitrary",)),
        name=name,
    )(proj, proj, w_pool, scale, do, do)


GATE_BLK0 = GATE_COL0 // WIDTH


def merge_fwd(oa, ob, oc, proj, b_gate, wa, wb, wc, *, name):
    S = oa.shape[0]
    T = _tile(S)

    def body(oa_ref, ob_ref, oc_ref, ga, gb, gc, ba, bb, bc, wa_ref, wb_ref, wc_ref, m_ref):
        acc = None
        for o_ref, g_ref, b_ref, w_ref in ((oa_ref, ga, ba, wa_ref), (ob_ref, gb, bb, wb_ref), (oc_ref, gc, bc, wc_ref)):
            y = _dot(o_ref[...], w_ref[...], 1, 0)
            t = jax.nn.sigmoid(g_ref[...] + b_ref[...]) * y
            acc = t if acc is None else acc + t
        m_ref[...] = acc.astype(BF16)

    row = pl.BlockSpec((T, WIDTH), lambda i, n: (i, 0))
    gate = lambda b: pl.BlockSpec((T, WIDTH), lambda i, n, b=b: (i, GATE_BLK0 + 2 * b + n))
    bias = lambda b: pl.BlockSpec((1, WIDTH), lambda i, n, b=b: (0, 2 * b + n))
    wspec = pl.BlockSpec((WIDTH, WIDTH), lambda i, n: (0, n))
    return pl.pallas_call(
        body,
        out_shape=jax.ShapeDtypeStruct((S, D_MODEL), BF16),
        grid=(S // T, 2),
        in_specs=[row, row, row, gate(0), gate(1), gate(2), bias(0), bias(1), bias(2), wspec, wspec, wspec],
        out_specs=pl.BlockSpec((T, WIDTH), lambda i, n: (i, n)),
        compiler_params=_cp(("parallel", "parallel")),
        name=name,
    )(oa, ob, oc, proj, proj, proj, b_gate, b_gate, b_gate, wa, wb, wc)


def merge_bwd(dm, oa, ob, oc, proj, b_gate, wa, wb, wc, *, name):
    S = oa.shape[0]
    T = _tile(S)

    def body(dm_ref, oa_ref, ob_ref, oc_ref, ga, gb, gc, ba, bb, bc, wa_ref, wb_ref, wc_ref,
             ta, tb, tc, dga, dgb, dgc, dba, dbb, dbc):
        i = pl.program_id(1)
        dmv = dm_ref[...].astype(F32)
        for o_ref, g_ref, b_ref, w_ref, t_ref, dg_ref, db_ref in (
                (oa_ref, ga, ba, wa_ref, ta, dga, dba), (ob_ref, gb, bb, wb_ref, tb, dgb, dbb),
                (oc_ref, gc, bc, wc_ref, tc, dgc, dbc)):
            y = _dot(o_ref[...], w_ref[...], 1, 0)
            gate = jax.nn.sigmoid(g_ref[...] + b_ref[...])
            t_ref[...] = (gate * dmv).astype(BF16)
            dgl = dmv * y * gate * (1.0 - gate)
            dg_ref[...] = dgl.astype(BF16)

            @pl.when(i == 0)
            def _():
                db_ref[...] = jnp.zeros_like(db_ref)

            db_ref[...] += jnp.sum(dgl, axis=0, keepdims=True)

    row = pl.BlockSpec((T, WIDTH), lambda n, i: (i, 0))
    half = pl.BlockSpec((T, WIDTH), lambda n, i: (i, n))
    gate = lambda b: pl.BlockSpec((T, WIDTH), lambda n, i, b=b: (i, GATE_BLK0 + 2 * b + n))
    bias = lambda b: pl.BlockSpec((1, WIDTH), lambda n, i, b=b: (0, 2 * b + n))
    wspec = pl.BlockSpec((WIDTH, WIDTH), lambda n, i: (0, n))
    bvec = pl.BlockSpec((1, WIDTH), lambda n, i: (0, n))
    act = jax.ShapeDtypeStruct((S, D_MODEL), BF16)
    vec = jax.ShapeDtypeStruct((1, D_MODEL), F32)
    return pl.pallas_call(
        body,
        out_shape=(act, act, act, act, act, act, vec, vec, vec),
        grid=(2, S // T),
        in_specs=[half, row, row, row, gate(0), gate(1), gate(2), bias(0), bias(1), bias(2), wspec, wspec, wspec],
        out_specs=(half, half, half, half, half, half, bvec, bvec, bvec),
        compiler_params=_cp(("parallel", "arbitrary")),
        name=name,
    )(dm, oa, ob, oc, proj, proj, proj, b_gate, b_gate, b_gate, wa, wb, wc)


FF_T = 256
FF_BLKS = D_FF // FF_T


def _silu_parts(x):
    s = jax.nn.sigmoid(x)
    return x * s, s


def _conv3(ext, w_ref, b_ref):
    return (b_ref[...] + w_ref[0:1, :] * pltpu.roll(ext, 2, 0) + w_ref[1:2, :] * pltpu.roll(ext, 1, 0)
            + w_ref[2:3, :] * ext)


def conv_glu_fwd(u, conv_w, conv_b, *, name):
    S = u.shape[0]
    T = _tile(S)
    hb = T // CONV_HALO

    def body(ug, ugh, uv, uvh, wg, wv, bg, bv, a_ref):
        i = pl.program_id(1)
        cs = []
        for m_ref, h_ref, w_ref, b_ref in ((ug, ugh, wg, bg), (uv, uvh, wv, bv)):
            halo = jnp.where(i > 0, h_ref[...], 0.0)
            ext = jnp.concatenate([halo, m_ref[...]], axis=0)
            cs.append(_conv3(ext, w_ref, b_ref)[CONV_HALO:, :])
        act, _ = _silu_parts(cs[0])
        a_ref[...] = (act * cs[1]).astype(BF16)

    main = lambda o: pl.BlockSpec((T, FF_T), lambda c, i, o=o: (i, c + o))
    halo = lambda o: pl.BlockSpec((CONV_HALO, FF_T), lambda c, i, o=o: (jnp.maximum(i * hb - 1, 0), c + o))
    wsp = lambda o: pl.BlockSpec((3, FF_T), lambda c, i, o=o: (0, c + o))
    bsp = lambda o: pl.BlockSpec((1, FF_T), lambda c, i, o=o: (0, c + o))
    return pl.pallas_call(
        body,
        out_shape=jax.ShapeDtypeStruct((S, D_FF), BF16),
        grid=(FF_BLKS, S // T),
        in_specs=[main(0), halo(0), main(FF_BLKS), halo(FF_BLKS), wsp(0), wsp(FF_BLKS), bsp(0), bsp(FF_BLKS)],
        out_specs=pl.BlockSpec((T, FF_T), lambda c, i: (i, c)),
        compiler_params=_cp(("parallel", "parallel")),
        name=name,
    )(u, u, u, u, conv_w, conv_w, conv_b, conv_b)


def conv_glu_bwd(u, conv_w, conv_b, da, *, name):
    S = u.shape[0]
    T = _tile(S)
    hb = T // CONV_HALO
    nt = S // T
    n = T + 2 * CONV_HALO

    def body(ug, ugp, ugf, uv, uvp, uvf, wg, wv, bg, bv, da_ref, daf_ref,
             dug, duv, dwg, dwv, dbg, dbv):
        i = pl.program_id(1)
        first, last = i == 0, i == nt - 1
        exts, cs = [], []
        for m_ref, p_ref, f_ref, w_ref, b_ref in ((ug, ugp, ugf, wg, bg), (uv, uvp, uvf, wv, bv)):
            ext = jnp.concatenate([jnp.where(first, 0.0, p_ref[...]), m_ref[...], jnp.where(last, 0.0, f_ref[...])], axis=0)
            exts.append(ext)
            cs.append(_conv3(ext, w_ref, b_ref))
        dae = jnp.concatenate([jnp.zeros((CONV_HALO, FF_T), F32), da_ref[...].astype(F32),
                               jnp.where(last, 0.0, daf_ref[...].astype(F32))], axis=0)
        act, sg = _silu_parts(cs[0])
        dcs = (dae * cs[1] * (sg * (1.0 + cs[0] * (1.0 - sg))), dae * act)
        main = slice(CONV_HALO, CONV_HALO + T)
        for ext, dc, w_ref, du_ref, dw_ref, db_ref in ((exts[0], dcs[0], wg, dug, dwg, dbg),
                                                       (exts[1], dcs[1], wv, duv, dwv, dbv)):
            du = (w_ref[2:3, :] * dc + w_ref[1:2, :] * pltpu.roll(dc, n - 1, 0) + w_ref[0:1, :] * pltpu.roll(dc, n - 2, 0))
            du_ref[...] = du[main, :].astype(BF16)
            dcm = dc[main, :]
            rows = [jnp.sum(dcm * pltpu.roll(ext, 2 - j, 0)[main, :], axis=0, keepdims=True) if j < 2
                    else jnp.sum(dcm * ext[main, :], axis=0, keepdims=True) for j in range(3)]

            @pl.when(first)
            def _():
                dw_ref[...] = jnp.zeros_like(dw_ref)
                db_ref[...] = jnp.zeros_like(db_ref)

            dw_ref[...] += jnp.concatenate(rows, axis=0)
            db_ref[...] += jnp.sum(dcm, axis=0, keepdims=True)

    main = lambda o: pl.BlockSpec((T, FF_T), lambda c, i, o=o: (i, c + o))
    past = lambda o: pl.BlockSpec((CONV_HALO, FF_T), lambda c, i, o=o: (jnp.maximum(i * hb - 1, 0), c + o))
    fut = lambda o: pl.BlockSpec((CONV_HALO, FF_T), lambda c, i, o=o: (jnp.minimum((i + 1) * hb, S // CONV_HALO - 1), c + o))
    wsp = lambda o: pl.BlockSpec((3, FF_T), lambda c, i, o=o: (0, c + o))
    bsp = lambda o: pl.BlockSpec((1, FF_T), lambda c, i, o=o: (0, c + o))
    return pl.pallas_call(
        body,
        out_shape=(jax.ShapeDtypeStruct((S, D_FF), BF16), jax.ShapeDtypeStruct((S, D_FF), BF16),
                   jax.ShapeDtypeStruct((3, D_FF), F32), jax.ShapeDtypeStruct((3, D_FF), F32),
                   jax.ShapeDtypeStruct((1, D_FF), F32), jax.ShapeDtypeStruct((1, D_FF), F32)),
        grid=(FF_BLKS, nt),
        in_specs=[main(0), past(0), fut(0), main(FF_BLKS), past(FF_BLKS), fut(FF_BLKS),
                  wsp(0), wsp(FF_BLKS), bsp(0), bsp(FF_BLKS), main(0), fut(0)],
        out_specs=(main(0), main(0), wsp(0), wsp(0), bsp(0), bsp(0)),
        compiler_params=_cp(("parallel", "arbitrary")),
        name=name,
    )(u, u, u, u, u, u, conv_w, conv_w, conv_b, conv_b, da, da)


def loss_head(y, target, *, name):
    S, D = y.shape
    T = _tile(S)

    def body(y_ref, t_ref, dy_ref, l_ref):
        i = pl.program_id(0)
        err = y_ref[...] - t_ref[...]
        dy_ref[...] = err * (1.0 / D)

        @pl.when(i == 0)
        def _():
            l_ref[...] = jnp.zeros_like(l_ref)

        l_ref[...] += 0.5 * jnp.sum(jnp.mean(err * err, axis=-1, keepdims=True))

    row = pl.BlockSpec((T, D), lambda i: (i, 0))
    return pl.pallas_call(
        body,
        out_shape=(jax.ShapeDtypeStruct((S, D), F32), jax.ShapeDtypeStruct((8, 128), F32)),
        grid=(S // T,),
        in_specs=[row, row],
        out_specs=(row, pl.BlockSpec((8, 128), lambda i: (0, 0))),
        compiler_params=_cp(("arbitrary",)),
        name=name,
    )(y, target)


ELEMS_PER_BLOCK = 256 * 1024


def _rows_tile(rows, cols):
    if rows * cols <= ELEMS_PER_BLOCK or rows % 8:
        return rows
    best = 8
    for tr in range(8, rows + 1, 8):
        if rows % tr == 0 and tr * cols <= ELEMS_PER_BLOCK:
            best = tr
    return best


def _adamw_math(g, w_ref, m_ref, v_ref, g_out, d_out, m_out, v_out):
    mn = ADAM_B1 * m_ref[...] + (1.0 - ADAM_B1) * g
    vn = ADAM_B2 * v_ref[...] + (1.0 - ADAM_B2) * (g * g)
    m_hat = mn / (1.0 - ADAM_B1 ** ADAM_STEP)
    v_hat = vn / (1.0 - ADAM_B2 ** ADAM_STEP)
    g_out[...] = g
    d_out[...] = -ADAM_LR * (m_hat / (jnp.sqrt(v_hat) + ADAM_EPS) + ADAM_WD * w_ref[...])
    m_out[...] = mn
    v_out[...] = vn


def adamw(w, m, v, g, *, name):
    rows, cols = w.shape
    tr = _rows_tile(rows, cols)

    def body(w_ref, m_ref, v_ref, g_ref, g_out, d_out, m_out, v_out):
        _adamw_math(g_ref[...], w_ref, m_ref, v_ref, g_out, d_out, m_out, v_out)

    spec = pl.BlockSpec((tr, cols), lambda i: (i, 0))
    shp = jax.ShapeDtypeStruct((rows, cols), F32)
    return pl.pallas_call(
        body,
        out_shape=(shp, shp, shp, shp),
        grid=(rows // tr,),
        in_specs=[spec] * 4,
        out_specs=(spec, spec, spec, spec),
        compiler_params=_cp(("parallel",)),
        name=name,
    )(w, m, v, g)


ANY = pl.BlockSpec(memory_space=pl.ANY)


def _mesh_pos():
    return lax.axis_index("x"), lax.axis_index("y"), lax.axis_index("c")


def _chip_peers(x, y):
    return [(1 - x, y), (x, 1 - y), (1 - x, 1 - y)]


def _all_peers(x, y, c):
    return [((1 - x) if (r >> 2) & 1 else x, (1 - y) if (r >> 1) & 1 else y, (1 - c) if r & 1 else c)
            for r in range(1, 8)]


def _shard_slice(ref, axis, j, size, layer=None):
    idx = [slice(None)] * 3
    idx[axis] = pl.ds(pl.multiple_of(j * size, 128 if axis == 2 else 16), size)
    if layer is not None:
        idx[0] = pl.ds(layer, 1)
    return ref.at[tuple(idx)]


def all_gather_shards(shards, axes, *, name):
    nt = len(shards)
    sizes = [s.shape[a] for s, a in zip(shards, axes)]

    def body(*refs):
        ins, outs = refs[:nt], refs[nt:2 * nt]
        ici_send, ici_recv, d2d_send, d2d_recv = refs[2 * nt:]
        x, y, c = _mesh_pos()
        mine = 2 * x + y
        peers = _chip_peers(x, y)

        def ici(t, k, blk):
            px, py = peers[k]
            return pltpu.make_async_remote_copy(
                src_ref=ins[t].at[pl.ds(c, 1)], dst_ref=_shard_slice(outs[t], axes[t], blk, sizes[t], c),
                send_sem=ici_send.at[3 * t + k], recv_sem=ici_recv.at[3 * t + k],
                device_id=(px, py, c), device_id_type=MESH_T)

        def d2d(t, k, layer):
            px, py = peers[k]
            piece = _shard_slice(outs[t], axes[t], 2 * px + py, sizes[t], layer)
            return pltpu.make_async_remote_copy(
                src_ref=piece, dst_ref=piece, send_sem=d2d_send.at[3 * t + k], recv_sem=d2d_recv.at[3 * t + k],
                device_id=(x, y, 1 - c), device_id_type=MESH_T)

        sends = []
        for t in range(nt):
            for k in range(3):
                snd = ici(t, k, mine)
                snd.start()
                sends.append(snd)
        for t in range(nt):
            for k, (px, py) in enumerate(peers):
                ici(t, k, 2 * px + py).wait_recv()
                fwd = d2d(t, k, c)
                fwd.start()
                sends.append(fwd)
        for t in range(nt):
            for k in range(3):
                d2d(t, k, 1 - c).wait_recv()
        for s in sends:
            s.wait_send()

    out_shape = []
    for s, a, sz in zip(shards, axes, sizes):
        shp = list(s.shape)
        shp[a] = 4 * sz
        out_shape.append(jax.ShapeDtypeStruct(tuple(shp), s.dtype))
    gathered = pl.pallas_call(
        body,
        out_shape=tuple(out_shape),
        in_specs=[ANY] * nt,
        out_specs=tuple([ANY] * nt),
        scratch_shapes=[pltpu.SemaphoreType.DMA((3 * nt,)) for _ in range(4)],
        name=name,
    )(*shards)
    mine = 2 * lax.axis_index("x") + lax.axis_index("y")
    full = []
    for o, s, a, sz in zip(gathered, shards, axes, sizes):
        start = [0, 0, 0]
        start[a] = mine * sz
        full.append(lax.dynamic_update_slice(o, s, start))
    return full


def sibling_layer_exchange(grads, *, name):
    nt = len(grads)

    def body(*refs):
        ins, outs = refs[:nt], refs[nt:2 * nt]
        send_sems, recv_sems = refs[2 * nt:]
        x, y, c = _mesh_pos()
        cps = []
        for t in range(nt):
            cp = pltpu.make_async_remote_copy(src_ref=ins[t].at[pl.ds(1 - c, 1)], dst_ref=outs[t],
                                              send_sem=send_sems.at[t], recv_sem=recv_sems.at[t],
                                              device_id=(x, y, 1 - c), device_id_type=MESH_T)
            cp.start()
            cps.append(cp)
        for cp in cps:
            cp.wait_recv()
        for cp in cps:
            cp.wait_send()

    return pl.pallas_call(
        body,
        out_shape=tuple(jax.ShapeDtypeStruct((1,) + g.shape[1:], F32) for g in grads),
        in_specs=[ANY] * nt,
        out_specs=tuple([ANY] * nt),
        scratch_shapes=[pltpu.SemaphoreType.DMA((nt,)), pltpu.SemaphoreType.DMA((nt,))],
        name=name,
    )(*grads)


def pair_sum(grad, other, layer, *, name):
    _, R, C = grad.shape
    tr = _rows_tile(R, C)

    def body(l_ref, g_ref, o_ref, s32_ref, s16_ref):
        s = g_ref[...] + o_ref[...]
        s32_ref[...] = s
        s16_ref[...] = s.astype(BF16)

    blk = pl.BlockSpec((1, tr, C), lambda i, l: (0, i, 0))
    return pl.pallas_call(
        body,
        out_shape=(jax.ShapeDtypeStruct((1, R, C), F32), jax.ShapeDtypeStruct((1, R, C), BF16)),
        grid_spec=pltpu.PrefetchScalarGridSpec(
            num_scalar_prefetch=1, grid=(R // tr,),
            in_specs=[pl.BlockSpec((1, tr, C), lambda i, l: (l[0], i, 0)), blk],
            out_specs=(blk, blk)),
        compiler_params=_cp(("parallel",)),
        name=name,
    )(layer, grad, other)


def scatter_pairs(pairs16, axes, small, *, name):
    nt = len(pairs16)
    sizes = [g.shape[a] // 4 for g, a in zip(pairs16, axes)]

    def body(*refs):
        in16, small_in = refs[:nt], refs[nt]
        recv, small_out = refs[nt + 1:2 * nt + 1], refs[2 * nt + 1]
        send_sems, recv_sems, ssend, srecv = refs[2 * nt + 2:]
        x, y, c = _mesh_pos()
        me = 4 * x + 2 * y + c
        sends, recvs = [], []
        for t in range(nt):
            for k, (px, py) in enumerate(_chip_peers(x, y)):
                snd = pltpu.make_async_remote_copy(
                    src_ref=_shard_slice(in16[t], axes[t], 2 * px + py, sizes[t]), dst_ref=recv[t].at[k],
                    send_sem=send_sems.at[3 * t + k], recv_sem=recv_sems.at[3 * t + k],
                    device_id=(px, py, c), device_id_type=MESH_T)
                snd.start()
                sends.append(snd)
                recvs.append(snd)
        for r, (px, py, pc) in enumerate(_all_peers(x, y, c)):
            def mk(slot, r=r, px=px, py=py, pc=pc):
                return pltpu.make_async_remote_copy(
                    src_ref=small_in, dst_ref=small_out.at[slot], send_sem=ssend.at[r], recv_sem=srecv.at[r],
                    device_id=(px, py, pc), device_id_type=MESH_T)
            snd = mk(me)
            snd.start()
            sends.append(snd)
            recvs.append(mk(4 * px + 2 * py + pc))
        for r in recvs:
            r.wait_recv()
        for s in sends:
            s.wait_send()

    out_shape = []
    for g, a, sz in zip(pairs16, axes, sizes):
        shp = list(g.shape)
        shp[a] = sz
        out_shape.append(jax.ShapeDtypeStruct((3,) + tuple(shp), BF16))
    out_shape.append(jax.ShapeDtypeStruct((8,) + small.shape, F32))
    return pl.pallas_call(
        body,
        out_shape=tuple(out_shape),
        in_specs=[ANY] * (nt + 1),
        out_specs=tuple([ANY] * (nt + 1)),
        scratch_shapes=[pltpu.SemaphoreType.DMA((3 * nt,)), pltpu.SemaphoreType.DMA((3 * nt,)),
                        pltpu.SemaphoreType.DMA((7,)), pltpu.SemaphoreType.DMA((7,))],
        name=name,
    )(*pairs16, small)


def sum_chips(pair32, recv, axis, chip, *, name):
    _, _, r, c = recv.shape
    tr = _rows_tile(r, c)
    nr = r // tr

    def body(j_ref, p_ref, r_ref, s_ref):
        acc = p_ref[...]
        for k in range(3):
            acc = acc + r_ref[k].astype(F32)
        s_ref[...] = acc

    if axis == 2:
        own = pl.BlockSpec((1, tr, c), lambda i, j: (0, i, j[0]))
    else:
        own = pl.BlockSpec((1, tr, c), lambda i, j: (0, j[0] * nr + i, 0))
    blk = pl.BlockSpec((1, tr, c), lambda i, j: (0, i, 0))
    return pl.pallas_call(
        body,
        out_shape=jax.ShapeDtypeStruct((1, r, c), F32),
        grid_spec=pltpu.PrefetchScalarGridSpec(
            num_scalar_prefetch=1, grid=(nr,),
            in_specs=[own, pl.BlockSpec((3, 1, tr, c), lambda i, j: (0, 0, i, 0))],
            out_specs=blk),
        compiler_params=_cp(("parallel",)),
        name=name,
    )(chip, pair32, recv)


def sum_devices(gathered, own, me, *, name):
    _, R, C = gathered.shape

    def body(me_ref, g_ref, o_ref, s_ref):
        acc = None
        for k in range(8):
            part = jnp.where(me_ref[0] == k, o_ref[...], g_ref[k])
            acc = part if acc is None else acc + part
        s_ref[...] = acc

    return pl.pallas_call(
        body,
        out_shape=jax.ShapeDtypeStruct((R, C), F32),
        grid_spec=pltpu.PrefetchScalarGridSpec(
            num_scalar_prefetch=1, grid=(1,),
            in_specs=[pl.BlockSpec((8, R, C), lambda i, m: (0, 0, 0)), pl.BlockSpec((R, C), lambda i, m: (0, 0))],
            out_specs=pl.BlockSpec((R, C), lambda i, m: (0, 0))),
        compiler_params=_cp(("arbitrary",)),
        name=name,
    )(me, gathered, own)


def sibling_swap(parts, *, name):
    nt = len(parts)

    def body(*refs):
        ins, outs = refs[:nt], refs[nt:2 * nt]
        send_sems, recv_sems = refs[2 * nt:]
        x, y, c = _mesh_pos()
        cps = []
        for t in range(nt):
            cp = pltpu.make_async_remote_copy(src_ref=ins[t], dst_ref=outs[t], send_sem=send_sems.at[t],
                                              recv_sem=recv_sems.at[t], device_id=(x, y, 1 - c), device_id_type=MESH_T)
            cp.start()
            cps.append(cp)
        for cp in cps:
            cp.wait_recv()
        for cp in cps:
            cp.wait_send()

    return pl.pallas_call(
        body,
        out_shape=tuple(jax.ShapeDtypeStruct(p.shape, p.dtype) for p in parts),
        in_specs=[ANY] * nt,
        out_specs=tuple([ANY] * nt),
        scratch_shapes=[pltpu.SemaphoreType.DMA((nt,)), pltpu.SemaphoreType.DMA((nt,))],
        name=name,
    )(*parts)


def adamw_layers(w, m, v, g_mine, g_other, layer, *, name):
    _, r, c = w.shape
    tr = _rows_tile(r, c)

    def body(l_ref, w_ref, m_ref, v_ref, a_ref, b_ref, g_out, d_out, m_out, v_out):
        g = jnp.where(pl.program_id(0) == l_ref[0], a_ref[...], b_ref[...])
        _adamw_math(g, w_ref, m_ref, v_ref, g_out, d_out, m_out, v_out)

    full = pl.BlockSpec((1, tr, c), lambda l, i, s: (l, i, 0))
    half = pl.BlockSpec((1, tr, c), lambda l, i, s: (0, i, 0))
    shp = jax.ShapeDtypeStruct((2, r, c), F32)
    return pl.pallas_call(
        body,
        out_shape=(shp, shp, shp, shp),
        grid_spec=pltpu.PrefetchScalarGridSpec(
            num_scalar_prefetch=1, grid=(2, r // tr),
            in_specs=[full, full, full, half, half],
            out_specs=(full, full, full, full)),
        compiler_params=_cp(("parallel", "parallel")),
        name=name,
    )(layer, w, m, v, g_mine, g_other)


WEIGHTS = ("norm_mix", "w_in", "b_gate", "q_norm_a", "k_norm_a", "rel_bias_a", "w_pool", "pool_scale",
           "w_branch_a", "w_branch_b", "w_branch_c", "w_out", "norm_ffn", "w_up", "conv_w", "conv_b", "w_down")
SHARDED = {"w_in": 2, "w_branch_a": 2, "w_branch_b": 2, "w_branch_c": 2, "w_out": 1, "w_up": 2, "conv_w": 2,
           "w_down": 1}
REPLICATED = tuple(n for n in WEIGHTS if n not in SHARDED)
SMALL_ROWS = 1232


def _layer_fwd(x, p, tables):
    diag = exact_dot(p["rel_bias_a"], tables["onehot_t"], name="bias_diagonals")
    diag = diag.reshape(N_HEADS, N_VARIANTS, 1, DIAG_W).transpose(1, 0, 2, 3)
    biasm = bias_expand(diag, name="bias_expand")
    gq8 = jnp.tile(p["q_norm_a"], N_HEADS)[None]
    gk8 = jnp.tile(p["k_norm_a"], N_HEADS)[None]
    h = rmsnorm_fwd(x, p["norm_mix"][None], name="rmsnorm_fwd")
    proj = matmul(h, p["w_in"], name="mm_in")
    qa, ka, va, qb, kb, vb = qkv_prep(proj, gq8, gk8, name="qkv_prep")
    oa = attn_a_fwd(qa, ka, va, biasm, name="attn_a_fwd")
    ob, tot, nblk = attn_b_fwd(qb, kb, vb, name="attn_b_fwd")
    wpool = p["w_pool"].astype(BF16)
    oc = pool_fwd(proj, wpool, p["pool_scale"][None], name="pool_fwd")
    merged = merge_fwd(oa, ob, oc, proj, p["b_gate"][None], p["w_branch_a"], p["w_branch_b"], p["w_branch_c"],
                       name="merge_fwd")
    x1 = matmul(merged, p["w_out"], add=x, name="mm_out")
    h2 = rmsnorm_fwd(x1, p["norm_ffn"][None], name="rmsnorm_fwd")
    u = matmul(h2, p["w_up"], name="mm_up")
    a = conv_glu_fwd(u, p["conv_w"], p["conv_b"][None], name="conv_glu_fwd")
    x2 = matmul(a, p["w_down"], add=x1, name="mm_down")
    saved = dict(x=x, h=h, proj=proj, qa=qa, ka=ka, va=va, qb=qb, kb=kb, vb=vb, oa=oa, ob=ob, tot=tot, nblk=nblk, oc=oc,
                 merged=merged, x1=x1, h2=h2, u=u, a=a, biasm=biasm, gq8=gq8, gk8=gk8, wpool=wpool)
    return x2, saved


def _layer_bwd(dx2, s, p, tables):
    g = {}
    da = matmul(dx2, p["w_down"], tb=True, name="mm_down_dx")
    g["w_down"] = matmul(s["a"], dx2, ta=True, name="mm_down_dw")
    dug, duv, dcwg, dcwv, dcbg, dcbv = conv_glu_bwd(s["u"], p["conv_w"], p["conv_b"][None], da, name="conv_glu_bwd")
    du = jnp.concatenate([dug, duv], axis=1)
    g["conv_w"] = jnp.concatenate([dcwg, dcwv], axis=1)
    g["conv_b"] = jnp.concatenate([dcbg, dcbv], axis=1)[0]
    g["w_up"] = matmul(s["h2"], du, ta=True, name="mm_up_dw")
    dh2 = matmul(du, p["w_up"], tb=True, name="mm_up_dx")
    dx1, dg2 = rmsnorm_bwd(s["x1"], p["norm_ffn"][None], dh2, dx2, name="rmsnorm_bwd")
    g["norm_ffn"] = dg2[0]
    dmerged = matmul(dx1, p["w_out"], tb=True, name="mm_out_dx")
    g["w_out"] = matmul(s["merged"], dx1, ta=True, name="mm_out_dw")
    t_a, t_b, t_c, dga, dgb, dgc, dba, dbb, dbc = merge_bwd(
        dmerged, s["oa"], s["ob"], s["oc"], s["proj"], p["b_gate"][None], p["w_branch_a"], p["w_branch_b"],
        p["w_branch_c"], name="merge_bwd")
    g["b_gate"] = jnp.concatenate([dba, dbb, dbc], axis=1)[0]
    g["w_branch_a"] = matmul(s["oa"], t_a, ta=True, name="mm_branch_dw")
    g["w_branch_b"] = matmul(s["ob"], t_b, ta=True, name="mm_branch_dw")
    g["w_branch_c"] = matmul(s["oc"], t_c, ta=True, name="mm_branch_dw")
    doa = matmul(t_a, p["w_branch_a"], tb=True, out_dtype=BF16, name="mm_branch_dx")
    dob = matmul(t_b, p["w_branch_b"], tb=True, out_dtype=BF16, name="mm_branch_dx")
    doc = matmul(t_c, p["w_branch_c"], tb=True, name="mm_branch_dx_f32")
    dqh, dkh, dva, dbias = attn_a_bwd(s["qa"], s["ka"], s["va"], s["biasm"], doa, name="attn_a_bwd")
    ddiag = relbias_reduce(dbias, name="relbias_reduce")
    ddiag = ddiag.transpose(1, 0, 2, 3).reshape(N_HEADS, N_VARIANTS * DIAG_W)
    g["rel_bias_a"] = exact_dot(ddiag, tables["onehot"], name="relbias_table")
    dqa, dka, dgq8, dgk8 = qknorm_bwd(s["proj"], s["gq8"], s["gk8"], dqh, dkh, name="qknorm_bwd")
    g["q_norm_a"] = dgq8.reshape(N_HEADS, HEAD_DIM).sum(axis=0)
    g["k_norm_a"] = dgk8.reshape(N_HEADS, HEAD_DIM).sum(axis=0)
    dqb, dkb, dvb = attn_b_bwd(s["qb"], s["kb"], s["vb"], s["tot"], s["nblk"], dob, name="attn_b_bwd")
    duc, dwp, dsc = pool_bwd(s["proj"], s["wpool"], p["pool_scale"][None], doc, name="pool_bwd")
    g["w_pool"] = dwp
    g["pool_scale"] = dsc[0]
    dproj = jnp.concatenate([dqa, dka, dva.astype(BF16), dqb, dkb.astype(BF16), dvb.astype(BF16), duc,
                             dga, dgb, dgc], axis=1)
    g["w_in"] = matmul(s["h"], dproj, ta=True, name="mm_in_dw")
    dh = matmul(dproj, p["w_in"], tb=True, name="mm_in_dx")
    dx, dg1 = rmsnorm_bwd(s["x"], p["norm_mix"][None], dh, dx1, name="rmsnorm_bwd")
    g["norm_mix"] = dg1[0]
    return dx, g


def kernel(x, norm_mix, w_in, b_gate, q_norm_a, k_norm_a, rel_bias_a, w_pool, pool_scale, w_branch_a, w_branch_b, w_branch_c, w_out, norm_ffn, w_up, conv_w, conv_b, w_down, loss_target, m_norm_mix, m_w_in, m_b_gate, m_q_norm_a, m_k_norm_a, m_rel_bias_a, m_w_pool, m_pool_scale, m_w_branch_a, m_w_branch_b, m_w_branch_c, m_w_out, m_norm_ffn, m_w_up, m_conv_w, m_conv_b, m_w_down, v_norm_mix, v_w_in, v_b_gate, v_q_norm_a, v_k_norm_a, v_rel_bias_a, v_w_pool, v_pool_scale, v_w_branch_a, v_w_branch_b, v_w_branch_c, v_w_out, v_norm_ffn, v_w_up, v_conv_w, v_conv_b, v_w_down):
    w = dict(zip(WEIGHTS, (norm_mix, w_in, b_gate, q_norm_a, k_norm_a, rel_bias_a, w_pool, pool_scale, w_branch_a,
                           w_branch_b, w_branch_c, w_out, norm_ffn, w_up, conv_w, conv_b, w_down)))
    m = dict(zip(WEIGHTS, (m_norm_mix, m_w_in, m_b_gate, m_q_norm_a, m_k_norm_a, m_rel_bias_a, m_w_pool, m_pool_scale,
                           m_w_branch_a, m_w_branch_b, m_w_branch_c, m_w_out, m_norm_ffn, m_w_up, m_conv_w, m_conv_b,
                           m_w_down)))
    v = dict(zip(WEIGHTS, (v_norm_mix, v_w_in, v_b_gate, v_q_norm_a, v_k_norm_a, v_rel_bias_a, v_w_pool, v_pool_scale,
                           v_w_branch_a, v_w_branch_b, v_w_branch_c, v_w_out, v_norm_ffn, v_w_up, v_conv_w, v_conv_b,
                           v_w_down)))
    onehot = diagonal_onehot()
    tables = dict(onehot=jnp.asarray(onehot), onehot_t=jnp.asarray(np.ascontiguousarray(onehot.T)))

    names = tuple(SHARDED)
    shards = [w[n] if n == "conv_w" else w[n].astype(BF16) for n in names]
    full = dict(zip(names, all_gather_shards(shards, [SHARDED[n] for n in names], name="all_gather_weights")))

    def layer_params(l):
        p = {n: full[n][l] for n in names}
        p.update({n: w[n][l] for n in REPLICATED})
        return p

    xs = x[0]
    saved = []
    for l in range(DEPTH):
        xs, s = _layer_fwd(xs, layer_params(l), tables)
        saved.append(s)
    dx, lpart = loss_head(xs, loss_target[0], name="loss_head")
    loss = lax.psum(lpart[0, 0], MESH_AXES)
    grads = [None] * DEPTH
    for l in reversed(range(DEPTH)):
        dx, grads[l] = _layer_bwd(dx, saved[l], layer_params(l), tables)
    g = {n: jnp.stack([grads[l][n] for l in range(DEPTH)]) for n in WEIGHTS}

    flat = jnp.concatenate([g[n].reshape(-1) for n in REPLICATED])
    small = jnp.pad(flat, (0, SMALL_ROWS * 128 - flat.shape[0])).reshape(SMALL_ROWS, 128)
    axes = [SHARDED[n] for n in names]
    nt = len(names)
    as_index = lambda i: jnp.reshape(i, (1,)).astype(jnp.int32)
    cx, cy, cc = _mesh_pos()
    layer, chip, me = as_index(cc), as_index(2 * cx + cy), as_index(4 * cx + 2 * cy + cc)
    others = sibling_layer_exchange([g[n] for n in names], name="sibling_layer_exchange")
    pairs = [pair_sum(g[n], o, layer, name="pair_sum") for n, o in zip(names, others)]
    outs = scatter_pairs([p[1] for p in pairs], axes, small, name="scatter_pairs")
    finals = [sum_chips(p[0], recv, a, chip, name="sum_chips") for p, recv, a in zip(pairs, outs[:nt], axes)]
    finals_other = sibling_swap(finals, name="sibling_swap")
    small_sum = sum_devices(outs[-1], small, me, name="sum_devices").reshape(-1)

    res = {}
    for n, mine, other in zip(names, finals, finals_other):
        res[n] = adamw_layers(w[n], m[n], v[n], mine, other, layer, name="adamw_layers")
    off = 0
    for n in REPLICATED:
        shp = w[n].shape
        size = int(np.prod(shp))
        gn = small_sum[off:off + size]
        off += size
        cols = shp[-1]
        two_d = lambda t: t.reshape(size // cols, cols)
        res[n] = [t.reshape(shp) for t in adamw(two_d(w[n]), two_d(m[n]), two_d(v[n]), two_d(gn), name="adamw")]

    out = [loss, dx[None]]
    for k in range(4):
        out.extend(res[n][k] for n in WEIGHTS)
    return tuple(out)
```

```python
import jax
import jax.numpy as jnp
import numpy as np
from jax import lax
from jax.experimental import pallas as pl
from jax.experimental.pallas import tpu as pltpu

F32 = jnp.float32
BF16 = jnp.bfloat16

D_MODEL = 1024
DEPTH = 2
CHUNK = 64
N_LEFT = 8
HEAD_DIM = 64
N_HEADS = 8
WIDTH = 512
POOL_WINDOWS = (2, 4, 8, 16)
GROUP_DIM = 128
MAX_REL = 2 * CHUNK
REL_TABLE = MAX_REL + CHUNK
D_FF = 2816
EPS = 1e-6
QK_SCALE = 0.125
IN_COLS = 7 * WIDTH + 3 * D_MODEL
GATE_COL0 = 7 * WIDTH

ADAM_LR = 0.001
ADAM_B1 = 0.9
ADAM_B2 = 0.999
ADAM_EPS = 1e-08
ADAM_WD = 0.01
ADAM_STEP = 10

VMEM_LIMIT = 56 * 1024 * 1024
ATT_Q = 256
A_WIN = ATT_Q + N_LEFT * CHUNK
HALO = 16
CONV_HALO = 8
NEG = -1e30

MESH_AXES = ("x", "y", "c")
MESH_T = pl.DeviceIdType.MESH


def _cp(sem=None, vmem=VMEM_LIMIT):
    return pltpu.CompilerParams(dimension_semantics=sem, vmem_limit_bytes=vmem)


def _dot(a, b, ca, cb):
    return lax.dot_general(a, b, (((ca,), (cb,)), ((), ())), preferred_element_type=F32)


def _tile(n, cands=(512, 256, 128)):
    for c in cands:
        if n % c == 0:
            return c
    return n


def _split_hi_lo(v):
    hi = v.astype(BF16)
    lo = (v - hi.astype(F32)).astype(BF16)
    return hi, lo


def matmul(a, b, *, ta=False, tb=False, add=None, out_dtype=F32, b_layer=None, out_layer=None, out_stack=None,
           name):
    if ta:
        K, M = a.shape
    else:
        M, K = a.shape
    if tb:
        N, K2 = b.shape[-2:]
    else:
        K2, N = b.shape[-2:]
    assert K == K2, (a.shape, b.shape, ta, tb)
    big = (1024, 1408, 512, 256, 128)
    tm, tn, tk = _tile(M, big), _tile(N, big), _tile(K, big)
    nk = K // tk

    def body(*refs):
        a_ref, b_ref = refs[:2]
        r_ref = refs[2] if add is not None else None
        o_ref, acc = refs[-2:]
        k = pl.program_id(2)

        @pl.when(k == 0)
        def _():
            acc[...] = jnp.zeros_like(acc)

        av = a_ref[...].astype(BF16)
        bv = b_ref[...].astype(BF16)
        acc[...] += _dot(av, bv, 0 if ta else 1, 1 if tb else 0)

        @pl.when(k == nk - 1)
        def _():
            r = acc[...]
            if add is not None:
                r = r + r_ref[...].astype(F32)
            o_ref[...] = r.astype(out_dtype)

    a_spec = pl.BlockSpec((tk, tm), lambda i, j, k: (k, i)) if ta else pl.BlockSpec((tm, tk), lambda i, j, k: (i, k))
    if b_layer is None:
        b_spec = pl.BlockSpec((tn, tk), lambda i, j, k: (j, k)) if tb else pl.BlockSpec((tk, tn), lambda i, j, k: (k, j))
    elif tb:
        b_spec = pl.BlockSpec((None, tn, tk), lambda i, j, k: (b_layer, j, k))
    else:
        b_spec = pl.BlockSpec((None, tk, tn), lambda i, j, k: (b_layer, k, j))
    in_specs = [a_spec, b_spec]
    args = [a, b]
    if add is not None:
        in_specs.append(pl.BlockSpec((tm, tn), lambda i, j, k: (i, j)))
        args.append(add)
    aliases = {}
    if out_layer is None:
        o_spec = pl.BlockSpec((tm, tn), lambda i, j, k: (i, j))
        out_shape = jax.ShapeDtypeStruct((M, N), out_dtype)
    else:
        o_spec = pl.BlockSpec((None, tm, tn), lambda i, j, k: (out_layer, i, j))
        out_shape = jax.ShapeDtypeStruct((DEPTH, M, N), out_dtype)
        if out_stack is not None:
            aliases = {len(args): 0}
            in_specs.append(pl.BlockSpec(memory_space=pl.ANY))
            args.append(out_stack)
    return pl.pallas_call(
        body,
        out_shape=out_shape,
        grid=(M // tm, N // tn, nk),
        in_specs=in_specs,
        out_specs=o_spec,
        scratch_shapes=[pltpu.VMEM((tm, tn), F32)],
        input_output_aliases=aliases,
        compiler_params=_cp(("parallel", "parallel", "arbitrary")),
        name=name,
    )(*args)


def rmsnorm_fwd(x, g, *, name):
    S, D = x.shape
    T = _tile(S)

    def body(x_ref, g_ref, h_ref):
        xv = x_ref[...]
        r = lax.rsqrt(jnp.mean(xv * xv, axis=-1, keepdims=True) + EPS)
        h_ref[...] = (xv * r * g_ref[...]).astype(BF16)

    return pl.pallas_call(
        body,
        out_shape=jax.ShapeDtypeStruct((S, D), BF16),
        grid=(S // T,),
        in_specs=[pl.BlockSpec((T, D), lambda i: (i, 0)), pl.BlockSpec((1, D), lambda i: (0, 0))],
        out_specs=pl.BlockSpec((T, D), lambda i: (i, 0)),
        compiler_params=_cp(("parallel",)),
        name=name,
    )(x, g)


def rmsnorm_bwd(x, g, dh, dres, *, name):
    S, D = x.shape
    T = _tile(S)

    def body(x_ref, g_ref, dh_ref, dres_ref, dx_ref, dg_ref):
        i = pl.program_id(0)
        xv = x_ref[...]
        dhv = dh_ref[...].astype(F32)
        r = lax.rsqrt(jnp.mean(xv * xv, axis=-1, keepdims=True) + EPS)
        gd = dhv * g_ref[...]
        m = jnp.mean(xv * gd, axis=-1, keepdims=True)
        dx_ref[...] = dres_ref[...] + r * gd - xv * (r * r * r * m)

        @pl.when(i == 0)
        def _():
            dg_ref[...] = jnp.zeros_like(dg_ref)

        dg_ref[...] += jnp.sum(dhv * xv * r, axis=0, keepdims=True)

    row = pl.BlockSpec((T, D), lambda i: (i, 0))
    vec = pl.BlockSpec((1, D), lambda i: (0, 0))
    return pl.pallas_call(
        body,
        out_shape=(jax.ShapeDtypeStruct((S, D), F32), jax.ShapeDtypeStruct((1, D), F32)),
        grid=(S // T,),
        in_specs=[row, vec, row, row],
        out_specs=(row, vec),
        compiler_params=_cp(("arbitrary",)),
        name=name,
    )(x, g, dh, dres)


def _head_mean_matrix():
    r = lax.broadcasted_iota(jnp.int32, (WIDTH, WIDTH), 0) // HEAD_DIM
    c = lax.broadcasted_iota(jnp.int32, (WIDTH, WIDTH), 1) // HEAD_DIM
    return jnp.where(r == c, 1.0 / HEAD_DIM, 0.0).astype(BF16)


def _head_mean(v, mm):
    hi, lo = _split_hi_lo(v)
    return _dot(hi, mm, 1, 0) + _dot(lo, mm, 1, 0)


def qkv_prep(proj, gq, gk, *, name):
    S = proj.shape[0]
    T = _tile(S)

    def body(qa, ka, va, qb, kb, vb, gq_ref, gk_ref, oqa, oka, ova, oqb, okb, ovb):
        mm = _head_mean_matrix()
        for src, gref, dst, scale in ((qa, gq_ref, oqa, QK_SCALE), (ka, gk_ref, oka, 1.0)):
            v = src[...]
            r = lax.rsqrt(_head_mean(v * v, mm) + EPS)
            dst[...] = (v * r * gref[...] * scale).astype(BF16)
        oqb[...] = (qb[...] * QK_SCALE).astype(BF16)
        for src, dst in ((va, ova), (kb, okb), (vb, ovb)):
            dst[...] = src[...].astype(BF16)

    col = lambda j: pl.BlockSpec((T, WIDTH), lambda i, j=j: (i, j))
    vec = pl.BlockSpec((1, WIDTH), lambda i: (0, 0))
    out = pl.BlockSpec((T, WIDTH), lambda i: (i, 0))
    return pl.pallas_call(
        body,
        out_shape=tuple(jax.ShapeDtypeStruct((S, WIDTH), BF16) for _ in range(6)),
        grid=(S // T,),
        in_specs=[col(0), col(1), col(2), col(3), col(4), col(5), vec, vec],
        out_specs=tuple(out for _ in range(6)),
        compiler_params=_cp(("parallel",)),
        name=name,
    )(proj, proj, proj, proj, proj, proj, gq, gk)


def qknorm_bwd(proj, gq, gk, dqh, dkh, *, name):
    S = proj.shape[0]
    T = _tile(S)

    def body(qa, ka, gq_ref, gk_ref, dq_ref, dk_ref, oq, ok, ogq, ogk):
        i = pl.program_id(0)
        mm = _head_mean_matrix()

        @pl.when(i == 0)
        def _():
            ogq[...] = jnp.zeros_like(ogq)
            ogk[...] = jnp.zeros_like(ogk)

        for src, gref, dref, dst, gdst in ((qa, gq_ref, dq_ref, oq, ogq), (ka, gk_ref, dk_ref, ok, ogk)):
            v = src[...]
            dy = dref[...]
            r = lax.rsqrt(_head_mean(v * v, mm) + EPS)
            gd = dy * gref[...]
            m = _head_mean(v * gd, mm)
            dst[...] = (r * gd - v * (r * r * r * m)).astype(BF16)
            gdst[...] += jnp.sum(dy * v * r, axis=0, keepdims=True)

    col = lambda j: pl.BlockSpec((T, WIDTH), lambda i, j=j: (i, j))
    vec = pl.BlockSpec((1, WIDTH), lambda i: (0, 0))
    row = pl.BlockSpec((T, WIDTH), lambda i: (i, 0))
    return pl.pallas_call(
        body,
        out_shape=(jax.ShapeDtypeStruct((S, WIDTH), BF16), jax.ShapeDtypeStruct((S, WIDTH), BF16),
                   jax.ShapeDtypeStruct((1, WIDTH), F32), jax.ShapeDtypeStruct((1, WIDTH), F32)),
        grid=(S // T,),
        in_specs=[col(0), col(1), vec, vec, row, row],
        out_specs=(row, row, vec, vec),
        compiler_params=_cp(("arbitrary",)),
        name=name,
    )(proj, proj, gq, gk, dqh, dkh)


DIAG_W = 1024
N_VARIANTS = 3


def diagonal_onehot():
    jj = np.arange(DIAG_W)
    diff = np.where(jj < A_WIN, jj, jj - DIAG_W)
    out = np.zeros((N_VARIANTS, DIAG_W, REL_TABLE), np.float32)
    for v in range(N_VARIANTS):
        rel = np.clip(ATT_Q * v - diff, -(CHUNK - 1), MAX_REL) + (CHUNK - 1)
        out[v, jj, rel] = 1.0
    return out.reshape(N_VARIANTS * DIAG_W, REL_TABLE)


def exact_dot(a, b, *, name):
    def body(a_ref, b_ref, o_ref):
        o_ref[...] = jnp.dot(a_ref[...], b_ref[...], precision=lax.Precision.HIGHEST, preferred_element_type=F32)

    return pl.pallas_call(body, out_shape=jax.ShapeDtypeStruct((a.shape[0], b.shape[1]), F32),
                          compiler_params=_cp(), name=name)(a, b)


def _band_valid(v):
    qc = (lax.broadcasted_iota(jnp.int32, (ATT_Q, A_WIN), 0) + ATT_Q * v) // CHUNK
    kc = lax.broadcasted_iota(jnp.int32, (ATT_Q, A_WIN), 1) // CHUNK
    return (kc <= qc) & (kc >= qc - N_LEFT)


def bias_expand(diag, *, name):
    def body(d_ref, o_ref):
        rows = jnp.broadcast_to(d_ref[0, 0], (ATT_Q, DIAG_W))
        skew = pltpu.roll(rows, 0, 1, stride=1, stride_axis=0)
        o_ref[0, 0] = jnp.where(_band_valid(pl.program_id(0)), skew[:, :A_WIN], NEG)

    return pl.pallas_call(
        body,
        out_shape=jax.ShapeDtypeStruct((N_VARIANTS, N_HEADS, ATT_Q, A_WIN), F32),
        grid=(N_VARIANTS, N_HEADS),
        in_specs=[pl.BlockSpec((1, 1, 1, DIAG_W), lambda v, h: (v, h, 0, 0))],
        out_specs=pl.BlockSpec((1, 1, ATT_Q, A_WIN), lambda v, h: (v, h, 0, 0)),
        compiler_params=_cp(("parallel", "parallel")),
        name=name,
    )(diag)


def relbias_reduce(dbias, *, name):
    def body(db_ref, o_ref):
        x = jnp.concatenate([db_ref[0, 0], jnp.zeros((ATT_Q, DIAG_W - A_WIN), F32)], axis=1)
        row = lax.broadcasted_iota(jnp.int32, (ATT_Q, DIAG_W), 0)
        for b in range(8):
            x = jnp.where((row >> b) & 1 == 1, pltpu.roll(x, DIAG_W - (1 << b), 1), x)
        o_ref[0, 0] = jnp.sum(x, axis=0, keepdims=True)

    return pl.pallas_call(
        body,
        out_shape=jax.ShapeDtypeStruct((N_VARIANTS, N_HEADS, 1, DIAG_W), F32),
        grid=(N_VARIANTS, N_HEADS),
        in_specs=[pl.BlockSpec((1, 1, ATT_Q, A_WIN), lambda v, h: (v, h, 0, 0))],
        out_specs=pl.BlockSpec((1, 1, 1, DIAG_W), lambda v, h: (v, h, 0, 0)),
        compiler_params=_cp(("parallel", "parallel")),
        name=name,
    )(dbias)


def _a_window_start(qb):
    return pl.multiple_of(jnp.maximum(qb * ATT_Q - N_LEFT * CHUNK, 0), ATT_Q)


def attn_a_fwd(q, k, v, biasm, *, name):
    S = q.shape[0]
    nq = S // ATT_Q

    def body(q_ref, k_ref, v_ref, b_ref, o_ref):
        qb = pl.program_id(1)
        start = _a_window_start(qb)
        outs = []
        for h in range(2):
            lanes = slice(h * HEAD_DIM, (h + 1) * HEAD_DIM)
            qh = q_ref[:, lanes]
            kw = k_ref[pl.ds(start, A_WIN), lanes]
            vw = v_ref[pl.ds(start, A_WIN), lanes]
            s = _dot(qh, kw, 1, 1) + b_ref[0, h]
            m = jnp.max(s, axis=-1, keepdims=True)
            e = jnp.exp(s - m)
            p = e * (1.0 / jnp.sum(e, axis=-1, keepdims=True))
            outs.append(_dot(p.astype(BF16), vw, 1, 0))
        o_ref[...] = jnp.concatenate(outs, axis=1).astype(BF16)

    qspec = pl.BlockSpec((ATT_Q, 2 * HEAD_DIM), lambda hp, qb: (qb, hp))
    kvspec = pl.BlockSpec((S, 2 * HEAD_DIM), lambda hp, qb: (0, hp))
    bspec = pl.BlockSpec((1, 2, ATT_Q, A_WIN), lambda hp, qb: (jnp.minimum(qb, 2), hp, 0, 0))
    return pl.pallas_call(
        body,
        out_shape=jax.ShapeDtypeStruct((S, WIDTH), BF16),
        grid=(N_HEADS // 2, nq),
        in_specs=[qspec, kvspec, kvspec, bspec],
        out_specs=qspec,
        compiler_params=_cp(("parallel", "arbitrary")),
        name=name,
    )(q, k, v, biasm)


def attn_a_bwd(q, k, v, biasm, do, *, name):
    S = q.shape[0]
    nq = S // ATT_Q

    def body(q_ref, k_ref, v_ref, b_ref, do_ref, dq_ref, dk_ref, dv_ref, db_ref):
        qb = pl.program_id(1)
        start = _a_window_start(qb)

        @pl.when(qb == 0)
        def _():
            dk_ref[...] = jnp.zeros_like(dk_ref)
            dv_ref[...] = jnp.zeros_like(dv_ref)

        @pl.when(qb <= 2)
        def _():
            db_ref[...] = jnp.zeros_like(db_ref)

        dqs = []
        for h in range(2):
            lanes = slice(h * HEAD_DIM, (h + 1) * HEAD_DIM)
            qh = q_ref[:, lanes]
            doh = do_ref[:, lanes]
            kw = k_ref[pl.ds(start, A_WIN), lanes]
            vw = v_ref[pl.ds(start, A_WIN), lanes]
            s = _dot(qh, kw, 1, 1) + b_ref[0, h]
            m = jnp.max(s, axis=-1, keepdims=True)
            e = jnp.exp(s - m)
            p = e * (1.0 / jnp.sum(e, axis=-1, keepdims=True))
            dp = _dot(doh, vw, 1, 1)
            delta = jnp.sum(p * dp, axis=-1, keepdims=True)
            ds = p * (dp - delta)
            db_ref[0, h] += ds
            dsb = ds.astype(BF16)
            dqs.append(_dot(dsb, kw, 1, 0) * QK_SCALE)
            dk_ref[pl.ds(start, A_WIN), lanes] += _dot(dsb, qh, 0, 0)
            dv_ref[pl.ds(start, A_WIN), lanes] += _dot(p.astype(BF16), doh, 0, 0)
        dq_ref[...] = jnp.concatenate(dqs, axis=1)

    qspec = pl.BlockSpec((ATT_Q, 2 * HEAD_DIM), lambda hp, qb: (qb, hp))
    kvspec = pl.BlockSpec((S, 2 * HEAD_DIM), lambda hp, qb: (0, hp))
    bspec = pl.BlockSpec((1, 2, ATT_Q, A_WIN), lambda hp, qb: (jnp.minimum(qb, 2), hp, 0, 0))
    return pl.pallas_call(
        body,
        out_shape=(jax.ShapeDtypeStruct((S, WIDTH), F32), jax.ShapeDtypeStruct((S, WIDTH), F32),
                   jax.ShapeDtypeStruct((S, WIDTH), F32), jax.ShapeDtypeStruct((3, N_HEADS, ATT_Q, A_WIN), F32)),
        grid=(N_HEADS // 2, nq),
        in_specs=[qspec, kvspec, kvspec, bspec, qspec],
        out_specs=(qspec, kvspec, kvspec, bspec),
        compiler_params=_cp(("parallel", "arbitrary")),
        name=name,
    )(q, k, v, biasm, do)


def _tri(kind):
    j = lax.broadcasted_iota(jnp.int32, (ATT_Q, ATT_Q), 0)
    s = lax.broadcasted_iota(jnp.int32, (ATT_Q, ATT_Q), 1)
    if kind == "gt":
        m = j > s
    elif kind == "le":
        m = j <= s
    else:
        m = j < s
    return jnp.where(m, 1.0, 0.0).astype(BF16)


def _cum(v, tri):
    hi, lo = _split_hi_lo(v)
    return _dot(hi, tri, 1, 0) + _dot(lo, tri, 1, 0)


def _log_sigmoids(z, mask):
    t = jnp.log(1.0 + jnp.exp(-jnp.abs(z)))
    keep = -(jnp.maximum(z, 0.0) + t)
    take = jnp.minimum(z, 0.0) - t
    return (keep if mask is None else jnp.where(mask, keep, 0.0)), take


def _strictly_before():
    row = lax.broadcasted_iota(jnp.int32, (ATT_Q, ATT_Q), 0)
    col = lax.broadcasted_iota(jnp.int32, (ATT_Q, ATT_Q), 1)
    return col < row


EXIT_LOG = -104.0


def attn_b_fwd(q, k, v, *, name):
    S = q.shape[0]
    nq = S // ATT_Q

    def body(q_ref, k_ref, v_ref, o_ref, t_ref, n_ref):
        hp = pl.program_id(0)
        qb = pl.program_id(1)
        tri = _tri("gt")

        def block(kb, carry, mask):
            ks = pl.multiple_of(kb * ATT_Q, ATT_Q)
            new = []
            for h in range(2):
                lanes = slice(h * HEAD_DIM, (h + 1) * HEAD_DIM)
                c, acc = carry[h]
                z = _dot(q_ref[:, lanes], k_ref[pl.ds(ks, ATT_Q), lanes], 1, 1)
                keep, take = _log_sigmoids(z, mask)
                w = jnp.exp(take + (_cum(keep, tri) + c))
                if mask is not None:
                    w = jnp.where(mask, w, 0.0)
                acc = acc + _dot(w.astype(BF16), v_ref[pl.ds(ks, ATT_Q), lanes], 1, 0)
                c = c + jnp.sum(keep, axis=-1, keepdims=True)
                new.append((c, acc))
            return jnp.maximum(jnp.max(new[0][0]), jnp.max(new[1][0])), tuple(new)

        def cond(state):
            it, cmax, _ = state
            return jnp.logical_and(it <= qb, cmax >= EXIT_LOG)

        def step(state):
            it, _, carry = state
            cmax, carry = block(qb - it, carry, None)
            return it + 1, cmax, carry

        init = tuple((jnp.zeros((ATT_Q, 1), F32), jnp.zeros((ATT_Q, HEAD_DIM), F32)) for _ in range(2))
        cmax, diag = block(qb, init, _strictly_before())
        visited, _, res = lax.while_loop(cond, step, (jnp.int32(1), cmax, diag))
        o_ref[...] = jnp.concatenate([res[0][1], res[1][1]], axis=1).astype(BF16)
        t_ref[...] = jnp.concatenate([jnp.broadcast_to(res[h][0], (ATT_Q, HEAD_DIM)) for h in range(2)], axis=1)
        n_ref[hp, qb] = visited.astype(F32)

    qspec = pl.BlockSpec((ATT_Q, 2 * HEAD_DIM), lambda hp, qb: (qb, hp))
    kvspec = pl.BlockSpec((S, 2 * HEAD_DIM), lambda hp, qb: (0, hp))
    return pl.pallas_call(
        body,
        out_shape=(jax.ShapeDtypeStruct((S, WIDTH), BF16), jax.ShapeDtypeStruct((S, WIDTH), F32),
                   jax.ShapeDtypeStruct((N_HEADS // 2, nq), F32)),
        grid=(N_HEADS // 2, nq),
        in_specs=[qspec, kvspec, kvspec],
        out_specs=(qspec, qspec, pl.BlockSpec(memory_space=pltpu.SMEM)),
        compiler_params=_cp(("arbitrary", "arbitrary")),
        name=name,
    )(q, k, v)


def attn_b_bwd(q, k, v, tot, nblk, do, *, name):
    S = q.shape[0]
    nq = S // ATT_Q

    def body(q_ref, k_ref, v_ref, t_ref, n_ref, do_ref, dq_ref, dk_ref, dv_ref):
        hp = pl.program_id(0)
        qb = pl.program_id(1)
        first = jnp.clip(qb + 1 - n_ref[hp, qb].astype(jnp.int32), 0, qb + 1)
        tri_le = _tri("le")
        tri_lt = _tri("lt")

        @pl.when(qb == 0)
        def _():
            dk_ref[...] = jnp.zeros_like(dk_ref)
            dv_ref[...] = jnp.zeros_like(dv_ref)

        def block(kb, carry, mask):
            ks = pl.multiple_of(kb * ATT_Q, ATT_Q)
            new = []
            for h in range(2):
                lanes = slice(h * HEAD_DIM, (h + 1) * HEAD_DIM)
                cl, cg, dq = carry[h]
                qh = q_ref[:, lanes]
                doh = do_ref[:, lanes]
                kh = k_ref[pl.ds(ks, ATT_Q), lanes]
                vh = v_ref[pl.ds(ks, ATT_Q), lanes]
                totl = t_ref[:, h * HEAD_DIM:h * HEAD_DIM + 1]
                z = _dot(qh, kh, 1, 1)
                keep, take = _log_sigmoids(z, mask)
                sig = jnp.exp(take)
                w = sig * jnp.exp((totl - cl) - _cum(keep, tri_le))
                if mask is not None:
                    w = jnp.where(mask, w, 0.0)
                g = w * _dot(doh, vh, 1, 1)
                G = _dot(g.astype(BF16), tri_lt, 1, 0) + cg
                dz = g * (1.0 - sig) - sig * G
                if mask is not None:
                    dz = jnp.where(mask, dz, 0.0)
                dz = dz.astype(BF16)
                dq = dq + _dot(dz, kh, 1, 0)
                dk_ref[pl.ds(ks, ATT_Q), lanes] += _dot(dz, qh, 0, 0)
                dv_ref[pl.ds(ks, ATT_Q), lanes] += _dot(w.astype(BF16), doh, 0, 0)
                cl = cl + jnp.sum(keep, axis=-1, keepdims=True)
                cg = cg + jnp.sum(g, axis=-1, keepdims=True)
                new.append((cl, cg, dq))
            return tuple(new)

        init = tuple((jnp.zeros((ATT_Q, 1), F32), jnp.zeros((ATT_Q, 1), F32), jnp.zeros((ATT_Q, HEAD_DIM), F32))
                     for _ in range(2))
        res = lax.fori_loop(jnp.minimum(first, qb), qb, lambda kb, carry: block(kb, carry, None), init)
        res = block(qb, res, _strictly_before())
        dq_ref[...] = (jnp.concatenate([res[0][2], res[1][2]], axis=1) * QK_SCALE).astype(BF16)

    qspec = pl.BlockSpec((ATT_Q, 2 * HEAD_DIM), lambda hp, qb: (qb, hp))
    kvspec = pl.BlockSpec((S, 2 * HEAD_DIM), lambda hp, qb: (0, hp))
    return pl.pallas_call(
        body,
        out_shape=(jax.ShapeDtypeStruct((S, WIDTH), BF16), jax.ShapeDtypeStruct((S, WIDTH), F32),
                   jax.ShapeDtypeStruct((S, WIDTH), F32)),
        grid=(N_HEADS // 2, nq),
        in_specs=[qspec, kvspec, kvspec, qspec, pl.BlockSpec(memory_space=pltpu.SMEM), qspec],
        out_specs=(qspec, kvspec, kvspec),
        compiler_params=_cp(("parallel", "arbitrary")),
        name=name,
    )(q, k, v, tot, nblk, do)


U_COLBLK = 6


def _pool_counts(t0, rows):
    t = t0 + lax.broadcasted_iota(jnp.int32, (rows, WIDTH), 0)
    lane_grp = lax.broadcasted_iota(jnp.int32, (rows, WIDTH), 1) // GROUP_DIM
    win = jnp.where(lane_grp == 0, 2, jnp.where(lane_grp == 1, 4, jnp.where(lane_grp == 2, 8, 16)))
    cnt = jnp.minimum(t + 1, win)
    return 1.0 / cnt.astype(F32), lane_grp


def _window_sums(ext, shift_fn):
    s2 = ext + shift_fn(ext, 1)
    s4 = s2 + shift_fn(s2, 2)
    s8 = s4 + shift_fn(s4, 4)
    s16 = s8 + shift_fn(s8, 8)
    return s2, s4, s8, s16


def _select_group(lane_grp, s2, s4, s8, s16):
    return jnp.where(lane_grp == 0, s2, jnp.where(lane_grp == 1, s4, jnp.where(lane_grp == 2, s8, s16)))


def _pooled_tile(u_ref, h_ref, i, T):
    halo = jnp.where(i > 0, h_ref[...], 0.0)
    ext = jnp.concatenate([halo, u_ref[...]], axis=0)
    n = T + HALO
    sums = _window_sums(ext, lambda v, k: pltpu.roll(v, k, 0))
    inv, lane_grp = _pool_counts(i * T - HALO, n)
    pooled = _select_group(lane_grp, *sums) * inv - ext
    return pooled[HALO:, :]


def pool_fwd(proj, w_pool, scale, *, name):
    S = proj.shape[0]
    T = _tile(S)
    hb = T // HALO

    def body(u_ref, h_ref, w_ref, s_ref, o_ref):
        i = pl.program_id(0)
        pooled = _pooled_tile(u_ref, h_ref, i, T).astype(BF16)
        outs = [_dot(pooled[:, g * GROUP_DIM:(g + 1) * GROUP_DIM], w_ref[g], 1, 0) for g in range(4)]
        o_ref[...] = (jnp.concatenate(outs, axis=1) * s_ref[...]).astype(BF16)

    return pl.pallas_call(
        body,
        out_shape=jax.ShapeDtypeStruct((S, WIDTH), BF16),
        grid=(S // T,),
        in_specs=[pl.BlockSpec((T, WIDTH), lambda i: (i, U_COLBLK)),
                  pl.BlockSpec((HALO, WIDTH), lambda i: (jnp.maximum(i * hb - 1, 0), U_COLBLK)),
                  pl.BlockSpec((4, GROUP_DIM, GROUP_DIM), lambda i: (0, 0, 0)),
                  pl.BlockSpec((1, WIDTH), lambda i: (0, 0))],
        out_specs=pl.BlockSpec((T, WIDTH), lambda i: (i, 0)),
        compiler_params=_cp(("parallel",)),
        name=name,
    )(proj, proj, w_pool, scale)


def pool_bwd(proj, w_pool, scale, do, *, name):
    S = proj.shape[0]
    T = _tile(S)
    hb = T // HALO
    nt = S // T

    def body(u_ref, h_ref, w_ref, s_ref, do_ref, dof_ref, du_ref, dw_ref, ds_ref):
        i = pl.program_id(0)

        @pl.when(i == 0)
        def _():
            dw_ref[...] = jnp.zeros_like(dw_ref)
            ds_ref[...] = jnp.zeros_like(ds_ref)

        pooled = _pooled_tile(u_ref, h_ref, i, T).astype(BF16)
        dov = do_ref[...].astype(F32)
        fut = jnp.where(i < nt - 1, dof_ref[...].astype(F32), 0.0)
        dmix = (jnp.concatenate([dov, fut], axis=0) * s_ref[...]).astype(BF16)
        mixed, dpool = [], []
        for g in range(4):
            lanes = slice(g * GROUP_DIM, (g + 1) * GROUP_DIM)
            mixed.append(_dot(pooled[:, lanes], w_ref[g], 1, 0))
            dw_ref[g] += _dot(pooled[:, lanes], dmix[:T, lanes], 0, 0)
            dpool.append(_dot(dmix[:, lanes], w_ref[g], 1, 1))
        ds_ref[...] += jnp.sum(dov * jnp.concatenate(mixed, axis=1), axis=0, keepdims=True)
        dp = jnp.concatenate(dpool, axis=1)
        n = T + HALO
        inv, lane_grp = _pool_counts(i * T, n)
        sums = _window_sums(dp * inv, lambda v, k: pltpu.roll(v, n - k, 0))
        du = _select_group(lane_grp, *sums) - dp
        du_ref[...] = du[:T, :].astype(BF16)

    row = pl.BlockSpec((T, WIDTH), lambda i: (i, 0))
    return pl.pallas_call(
        body,
        out_shape=(jax.ShapeDtypeStruct((S, WIDTH), BF16), jax.ShapeDtypeStruct((4, GROUP_DIM, GROUP_DIM), F32),
                   jax.ShapeDtypeStruct((1, WIDTH), F32)),
        grid=(nt,),
        in_specs=[pl.BlockSpec((T, WIDTH), lambda i: (i, U_COLBLK)),
                  pl.BlockSpec((HALO, WIDTH), lambda i: (jnp.maximum(i * hb - 1, 0), U_COLBLK)),
                  pl.BlockSpec((4, GROUP_DIM, GROUP_DIM), lambda i: (0, 0, 0)),
                  pl.BlockSpec((1, WIDTH), lambda i: (0, 0)),
                  row,
                  pl.BlockSpec((HALO, WIDTH), lambda i: (jnp.minimum((i + 1) * hb, S // HALO - 1), 0))],
        out_specs=(row, pl.BlockSpec((4, GROUP_DIM, GROUP_DIM), lambda i: (0, 0, 0)),
                   pl.BlockSpec((1, WIDTH), lambda i: (0, 0))),
        compiler_params=_cp(("arbitrary",)),
        name=name,
    )(proj, proj, w_pool, scale, do, do)


GATE_BLK0 = GATE_COL0 // WIDTH


def merge_fwd(oa, ob, oc, proj, b_gate, wa, wb, wc, *, name):
    S = oa.shape[0]
    T = _tile(S)

    def body(oa_ref, ob_ref, oc_ref, ga, gb, gc, ba, bb, bc, wa_ref, wb_ref, wc_ref, m_ref):
        acc = None
        for o_ref, g_ref, b_ref, w_ref in ((oa_ref, ga, ba, wa_ref), (ob_ref, gb, bb, wb_ref), (oc_ref, gc, bc, wc_ref)):
            y = _dot(o_ref[...], w_ref[...], 1, 0)
            t = jax.nn.sigmoid(g_ref[...] + b_ref[...]) * y
            acc = t if acc is None else acc + t
        m_ref[...] = acc.astype(BF16)

    row = pl.BlockSpec((T, WIDTH), lambda i, n: (i, 0))
    gate = lambda b: pl.BlockSpec((T, WIDTH), lambda i, n, b=b: (i, GATE_BLK0 + 2 * b + n))
    bias = lambda b: pl.BlockSpec((1, WIDTH), lambda i, n, b=b: (0, 2 * b + n))
    wspec = pl.BlockSpec((WIDTH, WIDTH), lambda i, n: (0, n))
    return pl.pallas_call(
        body,
        out_shape=jax.ShapeDtypeStruct((S, D_MODEL), BF16),
        grid=(S // T, 2),
        in_specs=[row, row, row, gate(0), gate(1), gate(2), bias(0), bias(1), bias(2), wspec, wspec, wspec],
        out_specs=pl.BlockSpec((T, WIDTH), lambda i, n: (i, n)),
        compiler_params=_cp(("parallel", "parallel")),
        name=name,
    )(oa, ob, oc, proj, proj, proj, b_gate, b_gate, b_gate, wa, wb, wc)


def merge_bwd(dm, oa, ob, oc, proj, b_gate, wa, wb, wc, *, name):
    S = oa.shape[0]
    T = _tile(S)

    def body(dm_ref, oa_ref, ob_ref, oc_ref, ga, gb, gc, ba, bb, bc, wa_ref, wb_ref, wc_ref,
             ta, tb, tc, dga, dgb, dgc, dba, dbb, dbc):
        i = pl.program_id(1)
        dmv = dm_ref[...].astype(F32)
        for o_ref, g_ref, b_ref, w_ref, t_ref, dg_ref, db_ref in (
                (oa_ref, ga, ba, wa_ref, ta, dga, dba), (ob_ref, gb, bb, wb_ref, tb, dgb, dbb),
                (oc_ref, gc, bc, wc_ref, tc, dgc, dbc)):
            y = _dot(o_ref[...], w_ref[...], 1, 0)
            gate = jax.nn.sigmoid(g_ref[...] + b_ref[...])
            t_ref[...] = (gate * dmv).astype(BF16)
            dgl = dmv * y * gate * (1.0 - gate)
            dg_ref[...] = dgl.astype(BF16)

            @pl.when(i == 0)
            def _():
                db_ref[...] = jnp.zeros_like(db_ref)

            db_ref[...] += jnp.sum(dgl, axis=0, keepdims=True)

    row = pl.BlockSpec((T, WIDTH), lambda n, i: (i, 0))
    half = pl.BlockSpec((T, WIDTH), lambda n, i: (i, n))
    gate = lambda b: pl.BlockSpec((T, WIDTH), lambda n, i, b=b: (i, GATE_BLK0 + 2 * b + n))
    bias = lambda b: pl.BlockSpec((1, WIDTH), lambda n, i, b=b: (0, 2 * b + n))
    wspec = pl.BlockSpec((WIDTH, WIDTH), lambda n, i: (0, n))
    bvec = pl.BlockSpec((1, WIDTH), lambda n, i: (0, n))
    act = jax.ShapeDtypeStruct((S, D_MODEL), BF16)
    vec = jax.ShapeDtypeStruct((1, D_MODEL), F32)
    return pl.pallas_call(
        body,
        out_shape=(act, act, act, act, act, act, vec, vec, vec),
        grid=(2, S // T),
        in_specs=[half, row, row, row, gate(0), gate(1), gate(2), bias(0), bias(1), bias(2), wspec, wspec, wspec],
        out_specs=(half, half, half, half, half, half, bvec, bvec, bvec),
        compiler_params=_cp(("parallel", "arbitrary")),
        name=name,
    )(dm, oa, ob, oc, proj, proj, proj, b_gate, b_gate, b_gate, wa, wb, wc)


FF_T = 256
FF_BLKS = D_FF // FF_T


def _silu_parts(x):
    s = jax.nn.sigmoid(x)
    return x * s, s


def _conv3(ext, w_ref, b_ref):
    return (b_ref[...] + w_ref[0:1, :] * pltpu.roll(ext, 2, 0) + w_ref[1:2, :] * pltpu.roll(ext, 1, 0)
            + w_ref[2:3, :] * ext)


def conv_glu_fwd(u, conv_w, conv_b, *, name):
    S = u.shape[0]
    T = _tile(S)
    hb = T // CONV_HALO

    def body(ug, ugh, uv, uvh, wg, wv, bg, bv, a_ref):
        i = pl.program_id(1)
        cs = []
        for m_ref, h_ref, w_ref, b_ref in ((ug, ugh, wg, bg), (uv, uvh, wv, bv)):
            halo = jnp.where(i > 0, h_ref[...], 0.0)
            ext = jnp.concatenate([halo, m_ref[...]], axis=0)
            cs.append(_conv3(ext, w_ref, b_ref)[CONV_HALO:, :])
        act, _ = _silu_parts(cs[0])
        a_ref[...] = (act * cs[1]).astype(BF16)

    main = lambda o: pl.BlockSpec((T, FF_T), lambda c, i, o=o: (i, c + o))
    halo = lambda o: pl.BlockSpec((CONV_HALO, FF_T), lambda c, i, o=o: (jnp.maximum(i * hb - 1, 0), c + o))
    wsp = lambda o: pl.BlockSpec((3, FF_T), lambda c, i, o=o: (0, c + o))
    bsp = lambda o: pl.BlockSpec((1, FF_T), lambda c, i, o=o: (0, c + o))
    return pl.pallas_call(
        body,
        out_shape=jax.ShapeDtypeStruct((S, D_FF), BF16),
        grid=(FF_BLKS, S // T),
        in_specs=[main(0), halo(0), main(FF_BLKS), halo(FF_BLKS), wsp(0), wsp(FF_BLKS), bsp(0), bsp(FF_BLKS)],
        out_specs=pl.BlockSpec((T, FF_T), lambda c, i: (i, c)),
        compiler_params=_cp(("parallel", "parallel")),
        name=name,
    )(u, u, u, u, conv_w, conv_w, conv_b, conv_b)


def conv_glu_bwd(u, conv_w, conv_b, da, *, name):
    S = u.shape[0]
    T = _tile(S)
    hb = T // CONV_HALO
    nt = S // T
    n = T + 2 * CONV_HALO

    def body(ug, ugp, ugf, uv, uvp, uvf, wg, wv, bg, bv, da_ref, daf_ref,
             dug, duv, dwg, dwv, dbg, dbv):
        i = pl.program_id(1)
        first, last = i == 0, i == nt - 1
        exts, cs = [], []
        for m_ref, p_ref, f_ref, w_ref, b_ref in ((ug, ugp, ugf, wg, bg), (uv, uvp, uvf, wv, bv)):
            ext = jnp.concatenate([jnp.where(first, 0.0, p_ref[...]), m_ref[...], jnp.where(last, 0.0, f_ref[...])], axis=0)
            exts.append(ext)
            cs.append(_conv3(ext, w_ref, b_ref))
        dae = jnp.concatenate([jnp.zeros((CONV_HALO, FF_T), F32), da_ref[...].astype(F32),
                               jnp.where(last, 0.0, daf_ref[...].astype(F32))], axis=0)
        act, sg = _silu_parts(cs[0])
        dcs = (dae * cs[1] * (sg * (1.0 + cs[0] * (1.0 - sg))), dae * act)
        main = slice(CONV_HALO, CONV_HALO + T)
        for ext, dc, w_ref, du_ref, dw_ref, db_ref in ((exts[0], dcs[0], wg, dug, dwg, dbg),
                                                       (exts[1], dcs[1], wv, duv, dwv, dbv)):
            du = (w_ref[2:3, :] * dc + w_ref[1:2, :] * pltpu.roll(dc, n - 1, 0) + w_ref[0:1, :] * pltpu.roll(dc, n - 2, 0))
            du_ref[...] = du[main, :].astype(BF16)
            dcm = dc[main, :]
            rows = [jnp.sum(dcm * pltpu.roll(ext, 2 - j, 0)[main, :], axis=0, keepdims=True) if j < 2
                    else jnp.sum(dcm * ext[main, :], axis=0, keepdims=True) for j in range(3)]

            @pl.when(first)
            def _():
                dw_ref[...] = jnp.zeros_like(dw_ref)
                db_ref[...] = jnp.zeros_like(db_ref)

            dw_ref[...] += jnp.concatenate(rows, axis=0)
            db_ref[...] += jnp.sum(dcm, axis=0, keepdims=True)

    main = lambda o: pl.BlockSpec((T, FF_T), lambda c, i, o=o: (i, c + o))
    past = lambda o: pl.BlockSpec((CONV_HALO, FF_T), lambda c, i, o=o: (jnp.maximum(i * hb - 1, 0), c + o))
    fut = lambda o: pl.BlockSpec((CONV_HALO, FF_T), lambda c, i, o=o: (jnp.minimum((i + 1) * hb, S // CONV_HALO - 1), c + o))
    wsp = lambda o: pl.BlockSpec((3, FF_T), lambda c, i, o=o: (0, c + o))
    bsp = lambda o: pl.BlockSpec((1, FF_T), lambda c, i, o=o: (0, c + o))
    return pl.pallas_call(
        body,
        out_shape=(jax.ShapeDtypeStruct((S, D_FF), BF16), jax.ShapeDtypeStruct((S, D_FF), BF16),
                   jax.ShapeDtypeStruct((3, D_FF), F32), jax.ShapeDtypeStruct((3, D_FF), F32),
                   jax.ShapeDtypeStruct((1, D_FF), F32), jax.ShapeDtypeStruct((1, D_FF), F32)),
        grid=(FF_BLKS, nt),
        in_specs=[main(0), past(0), fut(0), main(FF_BLKS), past(FF_BLKS), fut(FF_BLKS),
                  wsp(0), wsp(FF_BLKS), bsp(0), bsp(FF_BLKS), main(0), fut(0)],
        out_specs=(main(0), main(0), wsp(0), wsp(0), bsp(0), bsp(0)),
        compiler_params=_cp(("parallel", "arbitrary")),
        name=name,
    )(u, u, u, u, u, u, conv_w, conv_w, conv_b, conv_b, da, da)


def loss_head(y, target, *, name):
    S, D = y.shape
    T = _tile(S)

    def body(y_ref, t_ref, dy_ref, l_ref):
        i = pl.program_id(0)
        err = y_ref[...] - t_ref[...]
        dy_ref[...] = err * (1.0 / D)

        @pl.when(i == 0)
        def _():
            l_ref[...] = jnp.zeros_like(l_ref)

        l_ref[...] += 0.5 * jnp.sum(jnp.mean(err * err, axis=-1, keepdims=True))

    row = pl.BlockSpec((T, D), lambda i: (i, 0))
    return pl.pallas_call(
        body,
        out_shape=(jax.ShapeDtypeStruct((S, D), F32), jax.ShapeDtypeStruct((8, 128), F32)),
        grid=(S // T,),
        in_specs=[row, row],
        out_specs=(row, pl.BlockSpec((8, 128), lambda i: (0, 0))),
        compiler_params=_cp(("arbitrary",)),
        name=name,
    )(y, target)


ELEMS_PER_BLOCK = 256 * 1024


def _rows_tile(rows, cols):
    if rows * cols <= ELEMS_PER_BLOCK or rows % 8:
        return rows
    best = 8
    for tr in range(8, rows + 1, 8):
        if rows % tr == 0 and tr * cols <= ELEMS_PER_BLOCK:
            best = tr
    return best


def _adamw_math(g, w_ref, m_ref, v_ref, g_out, d_out, m_out, v_out):
    mn = ADAM_B1 * m_ref[...] + (1.0 - ADAM_B1) * g
    vn = ADAM_B2 * v_ref[...] + (1.0 - ADAM_B2) * (g * g)
    m_hat = mn / (1.0 - ADAM_B1 ** ADAM_STEP)
    v_hat = vn / (1.0 - ADAM_B2 ** ADAM_STEP)
    g_out[...] = g
    d_out[...] = -ADAM_LR * (m_hat / (jnp.sqrt(v_hat) + ADAM_EPS) + ADAM_WD * w_ref[...])
    m_out[...] = mn
    v_out[...] = vn


def adamw(w, m, v, g, *, name):
    rows, cols = w.shape
    tr = _rows_tile(rows, cols)

    def body(w_ref, m_ref, v_ref, g_ref, g_out, d_out, m_out, v_out):
        _adamw_math(g_ref[...], w_ref, m_ref, v_ref, g_out, d_out, m_out, v_out)

    spec = pl.BlockSpec((tr, cols), lambda i: (i, 0))
    shp = jax.ShapeDtypeStruct((rows, cols), F32)
    return pl.pallas_call(
        body,
        out_shape=(shp, shp, shp, shp),
        grid=(rows // tr,),
        in_specs=[spec] * 4,
        out_specs=(spec, spec, spec, spec),
        compiler_params=_cp(("parallel",)),
        name=name,
    )(w, m, v, g)


ANY = pl.BlockSpec(memory_space=pl.ANY)


def _mesh_pos():
    return lax.axis_index("x"), lax.axis_index("y"), lax.axis_index("c")


def _chip_peers(x, y):
    return [(1 - x, y), (x, 1 - y), (1 - x, 1 - y)]


def _all_peers(x, y, c):
    return [((1 - x) if (r >> 2) & 1 else x, (1 - y) if (r >> 1) & 1 else y, (1 - c) if r & 1 else c)
            for r in range(1, 8)]


def _shard_slice(ref, axis, j, size, layer=None):
    idx = [slice(None)] * 3
    idx[axis] = pl.ds(pl.multiple_of(j * size, 128 if axis == 2 else 16), size)
    if layer is not None:
        idx[0] = pl.ds(layer, 1)
    return ref.at[tuple(idx)]


def all_gather_shards(shards, axes, *, name):
    nt = len(shards)
    sizes = [s.shape[a] for s, a in zip(shards, axes)]

    def body(*refs):
        ins, outs = refs[:nt], refs[nt:2 * nt]
        ici_send, ici_recv, d2d_send, d2d_recv = refs[2 * nt:]
        x, y, c = _mesh_pos()
        mine = 2 * x + y
        peers = _chip_peers(x, y)

        def ici(t, k, blk):
            px, py = peers[k]
            return pltpu.make_async_remote_copy(
                src_ref=ins[t].at[pl.ds(c, 1)], dst_ref=_shard_slice(outs[t], axes[t], blk, sizes[t], c),
                send_sem=ici_send.at[3 * t + k], recv_sem=ici_recv.at[3 * t + k],
                device_id=(px, py, c), device_id_type=MESH_T)

        def d2d(t, k, layer):
            px, py = peers[k]
            piece = _shard_slice(outs[t], axes[t], 2 * px + py, sizes[t], layer)
            return pltpu.make_async_remote_copy(
                src_ref=piece, dst_ref=piece, send_sem=d2d_send.at[3 * t + k], recv_sem=d2d_recv.at[3 * t + k],
                device_id=(x, y, 1 - c), device_id_type=MESH_T)

        sends = []
        for t in range(nt):
            for k in range(3):
                snd = ici(t, k, mine)
                snd.start()
                sends.append(snd)
        for t in range(nt):
            for k, (px, py) in enumerate(peers):
                ici(t, k, 2 * px + py).wait_recv()
                fwd = d2d(t, k, c)
                fwd.start()
                sends.append(fwd)
        for t in range(nt):
            for k in range(3):
                d2d(t, k, 1 - c).wait_recv()
        for s in sends:
            s.wait_send()

    out_shape = []
    for s, a, sz in zip(shards, axes, sizes):
        shp = list(s.shape)
        shp[a] = 4 * sz
        out_shape.append(jax.ShapeDtypeStruct(tuple(shp), s.dtype))
    gathered = pl.pallas_call(
        body,
        out_shape=tuple(out_shape),
        in_specs=[ANY] * nt,
        out_specs=tuple([ANY] * nt),
        scratch_shapes=[pltpu.SemaphoreType.DMA((3 * nt,)) for _ in range(4)],
        name=name,
    )(*shards)
    mine = 2 * lax.axis_index("x") + lax.axis_index("y")
    full = []
    for o, s, a, sz in zip(gathered, shards, axes, sizes):
        start = [0, 0, 0]
        start[a] = mine * sz
        full.append(lax.dynamic_update_slice(o, s, start))
    return full


def sibling_layer_exchange(grads, *, name):
    nt = len(grads)

    def body(*refs):
        ins, outs = refs[:nt], refs[nt:2 * nt]
        send_sems, recv_sems = refs[2 * nt:]
        x, y, c = _mesh_pos()
        cps = []
        for t in range(nt):
            cp = pltpu.make_async_remote_copy(src_ref=ins[t].at[pl.ds(1 - c, 1)], dst_ref=outs[t],
                                              send_sem=send_sems.at[t], recv_sem=recv_sems.at[t],
                                              device_id=(x, y, 1 - c), device_id_type=MESH_T)
            cp.start()
            cps.append(cp)
        for cp in cps:
            cp.wait_recv()
        for cp in cps:
            cp.wait_send()

    return pl.pallas_call(
        body,
        out_shape=tuple(jax.ShapeDtypeStruct((1,) + g.shape[1:], F32) for g in grads),
        in_specs=[ANY] * nt,
        out_specs=tuple([ANY] * nt),
        scratch_shapes=[pltpu.SemaphoreType.DMA((nt,)), pltpu.SemaphoreType.DMA((nt,))],
        name=name,
    )(*grads)


def pair_sum(grad, other, layer, *, name):
    _, R, C = grad.shape
    tr = _rows_tile(R, C)

    def body(l_ref, g_ref, o_ref, s32_ref, s16_ref):
        s = g_ref[...] + o_ref[...]
        s32_ref[...] = s
        s16_ref[...] = s.astype(BF16)

    blk = pl.BlockSpec((1, tr, C), lambda i, l: (0, i, 0))
    return pl.pallas_call(
        body,
        out_shape=(jax.ShapeDtypeStruct((1, R, C), F32), jax.ShapeDtypeStruct((1, R, C), BF16)),
        grid_spec=pltpu.PrefetchScalarGridSpec(
            num_scalar_prefetch=1, grid=(R // tr,),
            in_specs=[pl.BlockSpec((1, tr, C), lambda i, l: (l[0], i, 0)), blk],
            out_specs=(blk, blk)),
        compiler_params=_cp(("parallel",)),
        name=name,
    )(layer, grad, other)


def scatter_pairs(pairs16, axes, small, *, name):
    nt = len(pairs16)
    sizes = [g.shape[a] // 4 for g, a in zip(pairs16, axes)]

    def body(*refs):
        in16, small_in = refs[:nt], refs[nt]
        recv, small_out = refs[nt + 1:2 * nt + 1], refs[2 * nt + 1]
        send_sems, recv_sems, ssend, srecv = refs[2 * nt + 2:]
        x, y, c = _mesh_pos()
        me = 4 * x + 2 * y + c
        sends, recvs = [], []
        for t in range(nt):
            for k, (px, py) in enumerate(_chip_peers(x, y)):
                snd = pltpu.make_async_remote_copy(
                    src_ref=_shard_slice(in16[t], axes[t], 2 * px + py, sizes[t]), dst_ref=recv[t].at[k],
                    send_sem=send_sems.at[3 * t + k], recv_sem=recv_sems.at[3 * t + k],
                    device_id=(px, py, c), device_id_type=MESH_T)
                snd.start()
                sends.append(snd)
                recvs.append(snd)
        for r, (px, py, pc) in enumerate(_all_peers(x, y, c)):
            def mk(slot, r=r, px=px, py=py, pc=pc):
                return pltpu.make_async_remote_copy(
                    src_ref=small_in, dst_ref=small_out.at[slot], send_sem=ssend.at[r], recv_sem=srecv.at[r],
                    device_id=(px, py, pc), device_id_type=MESH_T)
            snd = mk(me)
            snd.start()
            sends.append(snd)
            recvs.append(mk(4 * px + 2 * py + pc))
        for r in recvs:
            r.wait_recv()
        for s in sends:
            s.wait_send()

    out_shape = []
    for g, a, sz in zip(pairs16, axes, sizes):
        shp = list(g.shape)
        shp[a] = sz
        out_shape.append(jax.ShapeDtypeStruct((3,) + tuple(shp), BF16))
    out_shape.append(jax.ShapeDtypeStruct((8,) + small.shape, F32))
    return pl.pallas_call(
        body,
        out_shape=tuple(out_shape),
        in_specs=[ANY] * (nt + 1),
        out_specs=tuple([ANY] * (nt + 1)),
        scratch_shapes=[pltpu.SemaphoreType.DMA((3 * nt,)), pltpu.SemaphoreType.DMA((3 * nt,)),
                        pltpu.SemaphoreType.DMA((7,)), pltpu.SemaphoreType.DMA((7,))],
        name=name,
    )(*pairs16, small)


def sum_chips(pair32, recv, axis, chip, *, name):
    _, _, r, c = recv.shape
    tr = _rows_tile(r, c)
    nr = r // tr

    def body(j_ref, p_ref, r_ref, s_ref):
        acc = p_ref[...]
        for k in range(3):
            acc = acc + r_ref[k].astype(F32)
        s_ref[...] = acc

    if axis == 2:
        own = pl.BlockSpec((1, tr, c), lambda i, j: (0, i, j[0]))
    else:
        own = pl.BlockSpec((1, tr, c), lambda i, j: (0, j[0] * nr + i, 0))
    blk = pl.BlockSpec((1, tr, c), lambda i, j: (0, i, 0))
    return pl.pallas_call(
        body,
        out_shape=jax.ShapeDtypeStruct((1, r, c), F32),
        grid_spec=pltpu.PrefetchScalarGridSpec(
            num_scalar_prefetch=1, grid=(nr,),
            in_specs=[own, pl.BlockSpec((3, 1, tr, c), lambda i, j: (0, 0, i, 0))],
            out_specs=blk),
        compiler_params=_cp(("parallel",)),
        name=name,
    )(chip, pair32, recv)


def sum_devices(gathered, own, me, *, name):
    _, R, C = gathered.shape

    def body(me_ref, g_ref, o_ref, s_ref):
        acc = None
        for k in range(8):
            part = jnp.where(me_ref[0] == k, o_ref[...], g_ref[k])
            acc = part if acc is None else acc + part
        s_ref[...] = acc

    return pl.pallas_call(
        body,
        out_shape=jax.ShapeDtypeStruct((R, C), F32),
        grid_spec=pltpu.PrefetchScalarGridSpec(
            num_scalar_prefetch=1, grid=(1,),
            in_specs=[pl.BlockSpec((8, R, C), lambda i, m: (0, 0, 0)), pl.BlockSpec((R, C), lambda i, m: (0, 0))],
            out_specs=pl.BlockSpec((R, C), lambda i, m: (0, 0))),
        compiler_params=_cp(("arbitrary",)),
        name=name,
    )(me, gathered, own)


def sibling_swap(parts, *, name):
    nt = len(parts)

    def body(*refs):
        ins, outs = refs[:nt], refs[nt:2 * nt]
        send_sems, recv_sems = refs[2 * nt:]
        x, y, c = _mesh_pos()
        cps = []
        for t in range(nt):
            cp = pltpu.make_async_remote_copy(src_ref=ins[t], dst_ref=outs[t], send_sem=send_sems.at[t],
                                              recv_sem=recv_sems.at[t], device_id=(x, y, 1 - c), device_id_type=MESH_T)
            cp.start()
            cps.append(cp)
        for cp in cps:
            cp.wait_recv()
        for cp in cps:
            cp.wait_send()

    return pl.pallas_call(
        body,
        out_shape=tuple(jax.ShapeDtypeStruct(p.shape, p.dtype) for p in parts),
        in_specs=[ANY] * nt,
        out_specs=tuple([ANY] * nt),
        scratch_shapes=[pltpu.SemaphoreType.DMA((nt,)), pltpu.SemaphoreType.DMA((nt,))],
        name=name,
    )(*parts)


def adamw_layers(w, m, v, g_mine, g_other, layer, *, name):
    _, r, c = w.shape
    tr = _rows_tile(r, c)

    def body(l_ref, w_ref, m_ref, v_ref, a_ref, b_ref, g_out, d_out, m_out, v_out):
        g = jnp.where(pl.program_id(0) == l_ref[0], a_ref[...], b_ref[...])
        _adamw_math(g, w_ref, m_ref, v_ref, g_out, d_out, m_out, v_out)

    full = pl.BlockSpec((1, tr, c), lambda l, i, s: (l, i, 0))
    half = pl.BlockSpec((1, tr, c), lambda l, i, s: (0, i, 0))
    shp = jax.ShapeDtypeStruct((2, r, c), F32)
    return pl.pallas_call(
        body,
        out_shape=(shp, shp, shp, shp),
        grid_spec=pltpu.PrefetchScalarGridSpec(
            num_scalar_prefetch=1, grid=(2, r // tr),
            in_specs=[full, full, full, half, half],
            out_specs=(full, full, full, full)),
        compiler_params=_cp(("parallel", "parallel")),
        name=name,
    )(layer, w, m, v, g_mine, g_other)


WEIGHTS = ("norm_mix", "w_in", "b_gate", "q_norm_a", "k_norm_a", "rel_bias_a", "w_pool", "pool_scale",
           "w_branch_a", "w_branch_b", "w_branch_c", "w_out", "norm_ffn", "w_up", "conv_w", "conv_b", "w_down")
SHARDED = {"w_in": 2, "w_branch_a": 2, "w_branch_b": 2, "w_branch_c": 2, "w_out": 1, "w_up": 2, "conv_w": 2,
           "w_down": 1}
REPLICATED = tuple(n for n in WEIGHTS if n not in SHARDED)
MATMUL_WEIGHTS = tuple(n for n in SHARDED if n != "conv_w")
SMALL_ROWS = 1232


def _layer_fwd(x, p, tables):
    diag = exact_dot(p["rel_bias_a"], tables["onehot_t"], name="bias_diagonals")
    diag = diag.reshape(N_HEADS, N_VARIANTS, 1, DIAG_W).transpose(1, 0, 2, 3)
    biasm = bias_expand(diag, name="bias_expand")
    gq8 = jnp.tile(p["q_norm_a"], N_HEADS)[None]
    gk8 = jnp.tile(p["k_norm_a"], N_HEADS)[None]
    h = rmsnorm_fwd(x, p["norm_mix"][None], name="rmsnorm_fwd")
    proj = matmul(h, p["full"]["w_in"], b_layer=p["l"], name="mm_in")
    qa, ka, va, qb, kb, vb = qkv_prep(proj, gq8, gk8, name="qkv_prep")
    oa = attn_a_fwd(qa, ka, va, biasm, name="attn_a_fwd")
    ob, tot, nblk = attn_b_fwd(qb, kb, vb, name="attn_b_fwd")
    wpool = p["w_pool"].astype(BF16)
    oc = pool_fwd(proj, wpool, p["pool_scale"][None], name="pool_fwd")
    merged = merge_fwd(oa, ob, oc, proj, p["b_gate"][None], p["w_branch_a"], p["w_branch_b"], p["w_branch_c"],
                       name="merge_fwd")
    x1 = matmul(merged, p["full"]["w_out"], b_layer=p["l"], add=x, name="mm_out")
    h2 = rmsnorm_fwd(x1, p["norm_ffn"][None], name="rmsnorm_fwd")
    u = matmul(h2, p["full"]["w_up"], b_layer=p["l"], name="mm_up")
    a = conv_glu_fwd(u, p["conv_w"], p["conv_b"][None], name="conv_glu_fwd")
    x2 = matmul(a, p["full"]["w_down"], b_layer=p["l"], add=x1, name="mm_down")
    saved = dict(x=x, h=h, proj=proj, qa=qa, ka=ka, va=va, qb=qb, kb=kb, vb=vb, oa=oa, ob=ob, tot=tot, nblk=nblk, oc=oc,
                 merged=merged, x1=x1, h2=h2, u=u, a=a, biasm=biasm, gq8=gq8, gk8=gk8, wpool=wpool)
    return x2, saved


def _layer_bwd(dx2, s, p, tables, stacks):
    g = {}
    full, l = p["full"], p["l"]
    into = lambda n: dict(out_layer=l, out_stack=stacks.get(n))
    da = matmul(dx2, full["w_down"], b_layer=l, tb=True, name="mm_down_dx")
    g["w_down"] = matmul(s["a"], dx2, ta=True, name="mm_down_dw", **into("w_down"))
    dug, duv, dcwg, dcwv, dcbg, dcbv = conv_glu_bwd(s["u"], p["conv_w"], p["conv_b"][None], da, name="conv_glu_bwd")
    du = jnp.concatenate([dug, duv], axis=1)
    g["conv_w"] = jnp.concatenate([dcwg, dcwv], axis=1)
    g["conv_b"] = jnp.concatenate([dcbg, dcbv], axis=1)[0]
    g["w_up"] = matmul(s["h2"], du, ta=True, name="mm_up_dw", **into("w_up"))
    dh2 = matmul(du, full["w_up"], b_layer=l, tb=True, name="mm_up_dx")
    dx1, dg2 = rmsnorm_bwd(s["x1"], p["norm_ffn"][None], dh2, dx2, name="rmsnorm_bwd")
    g["norm_ffn"] = dg2[0]
    dmerged = matmul(dx1, full["w_out"], b_layer=l, tb=True, name="mm_out_dx")
    g["w_out"] = matmul(s["merged"], dx1, ta=True, name="mm_out_dw", **into("w_out"))
    t_a, t_b, t_c, dga, dgb, dgc, dba, dbb, dbc = merge_bwd(
        dmerged, s["oa"], s["ob"], s["oc"], s["proj"], p["b_gate"][None], p["w_branch_a"], p["w_branch_b"],
        p["w_branch_c"], name="merge_bwd")
    g["b_gate"] = jnp.concatenate([dba, dbb, dbc], axis=1)[0]
    g["w_branch_a"] = matmul(s["oa"], t_a, ta=True, name="mm_branch_dw", **into("w_branch_a"))
    g["w_branch_b"] = matmul(s["ob"], t_b, ta=True, name="mm_branch_dw", **into("w_branch_b"))
    g["w_branch_c"] = matmul(s["oc"], t_c, ta=True, name="mm_branch_dw", **into("w_branch_c"))
    doa = matmul(t_a, full["w_branch_a"], b_layer=l, tb=True, out_dtype=BF16, name="mm_branch_dx")
    dob = matmul(t_b, full["w_branch_b"], b_layer=l, tb=True, out_dtype=BF16, name="mm_branch_dx")
    doc = matmul(t_c, full["w_branch_c"], b_layer=l, tb=True, name="mm_branch_dx_f32")
    dqh, dkh, dva, dbias = attn_a_bwd(s["qa"], s["ka"], s["va"], s["biasm"], doa, name="attn_a_bwd")
    ddiag = relbias_reduce(dbias, name="relbias_reduce")
    ddiag = ddiag.transpose(1, 0, 2, 3).reshape(N_HEADS, N_VARIANTS * DIAG_W)
    g["rel_bias_a"] = exact_dot(ddiag, tables["onehot"], name="relbias_table")
    dqa, dka, dgq8, dgk8 = qknorm_bwd(s["proj"], s["gq8"], s["gk8"], dqh, dkh, name="qknorm_bwd")
    g["q_norm_a"] = dgq8.reshape(N_HEADS, HEAD_DIM).sum(axis=0)
    g["k_norm_a"] = dgk8.reshape(N_HEADS, HEAD_DIM).sum(axis=0)
    dqb, dkb, dvb = attn_b_bwd(s["qb"], s["kb"], s["vb"], s["tot"], s["nblk"], dob, name="attn_b_bwd")
    duc, dwp, dsc = pool_bwd(s["proj"], s["wpool"], p["pool_scale"][None], doc, name="pool_bwd")
    g["w_pool"] = dwp
    g["pool_scale"] = dsc[0]
    dproj = jnp.concatenate([dqa, dka, dva.astype(BF16), dqb, dkb.astype(BF16), dvb.astype(BF16), duc,
                             dga, dgb, dgc], axis=1)
    g["w_in"] = matmul(s["h"], dproj, ta=True, name="mm_in_dw", **into("w_in"))
    dh = matmul(dproj, full["w_in"], b_layer=l, tb=True, name="mm_in_dx")
    dx, dg1 = rmsnorm_bwd(s["x"], p["norm_mix"][None], dh, dx1, name="rmsnorm_bwd")
    g["norm_mix"] = dg1[0]
    return dx, g


def kernel(x, norm_mix, w_in, b_gate, q_norm_a, k_norm_a, rel_bias_a, w_pool, pool_scale, w_branch_a, w_branch_b, w_branch_c, w_out, norm_ffn, w_up, conv_w, conv_b, w_down, loss_target, m_norm_mix, m_w_in, m_b_gate, m_q_norm_a, m_k_norm_a, m_rel_bias_a, m_w_pool, m_pool_scale, m_w_branch_a, m_w_branch_b, m_w_branch_c, m_w_out, m_norm_ffn, m_w_up, m_conv_w, m_conv_b, m_w_down, v_norm_mix, v_w_in, v_b_gate, v_q_norm_a, v_k_norm_a, v_rel_bias_a, v_w_pool, v_pool_scale, v_w_branch_a, v_w_branch_b, v_w_branch_c, v_w_out, v_norm_ffn, v_w_up, v_conv_w, v_conv_b, v_w_down):
    w = dict(zip(WEIGHTS, (norm_mix, w_in, b_gate, q_norm_a, k_norm_a, rel_bias_a, w_pool, pool_scale, w_branch_a,
                           w_branch_b, w_branch_c, w_out, norm_ffn, w_up, conv_w, conv_b, w_down)))
    m = dict(zip(WEIGHTS, (m_norm_mix, m_w_in, m_b_gate, m_q_norm_a, m_k_norm_a, m_rel_bias_a, m_w_pool, m_pool_scale,
                           m_w_branch_a, m_w_branch_b, m_w_branch_c, m_w_out, m_norm_ffn, m_w_up, m_conv_w, m_conv_b,
                           m_w_down)))
    v = dict(zip(WEIGHTS, (v_norm_mix, v_w_in, v_b_gate, v_q_norm_a, v_k_norm_a, v_rel_bias_a, v_w_pool, v_pool_scale,
                           v_w_branch_a, v_w_branch_b, v_w_branch_c, v_w_out, v_norm_ffn, v_w_up, v_conv_w, v_conv_b,
                           v_w_down)))
    onehot = diagonal_onehot()
    tables = dict(onehot=jnp.asarray(onehot), onehot_t=jnp.asarray(np.ascontiguousarray(onehot.T)))

    names = tuple(SHARDED)
    shards = [w[n] if n == "conv_w" else w[n].astype(BF16) for n in names]
    full = dict(zip(names, all_gather_shards(shards, [SHARDED[n] for n in names], name="all_gather_weights")))

    def layer_params(l):
        p = {n: full[n][l] for n in ("w_branch_a", "w_branch_b", "w_branch_c", "conv_w")}
        p.update({n: w[n][l] for n in REPLICATED})
        p.update(full=full, l=l)
        return p

    xs = x[0]
    saved = []
    for l in range(DEPTH):
        xs, s = _layer_fwd(xs, layer_params(l), tables)
        saved.append(s)
    dx, lpart = loss_head(xs, loss_target[0], name="loss_head")
    loss = lax.psum(lpart[0, 0], MESH_AXES)
    grads = [None] * DEPTH
    stacks = {}
    for l in reversed(range(DEPTH)):
        dx, grads[l] = _layer_bwd(dx, saved[l], layer_params(l), tables, stacks)
        stacks = {n: grads[l][n] for n in MATMUL_WEIGHTS}
    g = {n: stacks[n] if n in stacks else jnp.stack([grads[l][n] for l in range(DEPTH)]) for n in WEIGHTS}

    flat = jnp.concatenate([g[n].reshape(-1) for n in REPLICATED])
    small = jnp.pad(flat, (0, SMALL_ROWS * 128 - flat.shape[0])).reshape(SMALL_ROWS, 128)
    axes = [SHARDED[n] for n in names]
    nt = len(names)
    as_index = lambda i: jnp.reshape(i, (1,)).astype(jnp.int32)
    cx, cy, cc = _mesh_pos()
    layer, chip, me = as_index(cc), as_index(2 * cx + cy), as_index(4 * cx + 2 * cy + cc)
    others = sibling_layer_exchange([g[n] for n in names], name="sibling_layer_exchange")
    pairs = [pair_sum(g[n], o, layer, name="pair_sum") for n, o in zip(names, others)]
    outs = scatter_pairs([p[1] for p in pairs], axes, small, name="scatter_pairs")
    finals = [sum_chips(p[0], recv, a, chip, name="sum_chips") for p, recv, a in zip(pairs, outs[:nt], axes)]
    finals_other = sibling_swap(finals, name="sibling_swap")
    small_sum = sum_devices(outs[-1], small, me, name="sum_devices").reshape(-1)

    res = {}
    for n, mine, other in zip(names, finals, finals_other):
        res[n] = adamw_layers(w[n], m[n], v[n], mine, other, layer, name="adamw_layers")
    off = 0
    for n in REPLICATED:
        shp = w[n].shape
        size = int(np.prod(shp))
        gn = small_sum[off:off + size]
        off += size
        cols = shp[-1]
        two_d = lambda t: t.reshape(size // cols, cols)
        res[n] = [t.reshape(shp) for t in adamw(two_d(w[n]), two_d(m[n]), two_d(v[n]), two_d(gn), name="adamw")]

    out = [loss, dx[None]]
    for k in range(4):
        out.extend(res[n][k] for n in WEIGHTS)
    return tuple(out)
```

```python
import jax
import jax.numpy as jnp
import numpy as np
from jax import lax
from jax.experimental import pallas as pl
from jax.experimental.pallas import tpu as pltpu

F32 = jnp.float32
BF16 = jnp.bfloat16

D_MODEL = 1024
DEPTH = 2
CHUNK = 64
N_LEFT = 8
HEAD_DIM = 64
N_HEADS = 8
WIDTH = 512
POOL_WINDOWS = (2, 4, 8, 16)
GROUP_DIM = 128
MAX_REL = 2 * CHUNK
REL_TABLE = MAX_REL + CHUNK
D_FF = 2816
EPS = 1e-6
QK_SCALE = 0.125
IN_COLS = 7 * WIDTH + 3 * D_MODEL
GATE_COL0 = 7 * WIDTH

ADAM_LR = 0.001
ADAM_B1 = 0.9
ADAM_B2 = 0.999
ADAM_EPS = 1e-08
ADAM_WD = 0.01
ADAM_STEP = 10

VMEM_LIMIT = 56 * 1024 * 1024
ATT_Q = 256
A_WIN = ATT_Q + N_LEFT * CHUNK
HALO = 16
CONV_HALO = 8
NEG = -1e30

MESH_AXES = ("x", "y", "c")
MESH_T = pl.DeviceIdType.MESH


def _cp(sem=None, vmem=VMEM_LIMIT):
    return pltpu.CompilerParams(dimension_semantics=sem, vmem_limit_bytes=vmem)


def _dot(a, b, ca, cb):
    return lax.dot_general(a, b, (((ca,), (cb,)), ((), ())), preferred_element_type=F32)


def _tile(n, cands=(512, 256, 128)):
    for c in cands:
        if n % c == 0:
            return c
    return n


def _split_hi_lo(v):
    hi = v.astype(BF16)
    lo = (v - hi.astype(F32)).astype(BF16)
    return hi, lo


def matmul(a, b, *, ta=False, tb=False, add=None, out_dtype=F32, b_layer=None, out_layer=None, out_stack=None,
           name):
    if ta:
        K, M = a.shape
    else:
        M, K = a.shape
    if tb:
        N, K2 = b.shape[-2:]
    else:
        K2, N = b.shape[-2:]
    assert K == K2, (a.shape, b.shape, ta, tb)
    big = (1024, 1408, 512, 256, 128)
    tm, tn, tk = _tile(M, big), _tile(N, big), _tile(K, big)
    nk = K // tk

    def body(*refs):
        a_ref, b_ref = refs[:2]
        r_ref = refs[2] if add is not None else None
        o_ref, acc = refs[-2:]
        k = pl.program_id(2)

        @pl.when(k == 0)
        def _():
            acc[...] = jnp.zeros_like(acc)

        av = a_ref[...].astype(BF16)
        bv = b_ref[...].astype(BF16)
        acc[...] += _dot(av, bv, 0 if ta else 1, 1 if tb else 0)

        @pl.when(k == nk - 1)
        def _():
            r = acc[...]
            if add is not None:
                r = r + r_ref[...].astype(F32)
            o_ref[...] = r.astype(out_dtype)

    a_spec = pl.BlockSpec((tk, tm), lambda i, j, k: (k, i)) if ta else pl.BlockSpec((tm, tk), lambda i, j, k: (i, k))
    if b_layer is None:
        b_spec = pl.BlockSpec((tn, tk), lambda i, j, k: (j, k)) if tb else pl.BlockSpec((tk, tn), lambda i, j, k: (k, j))
    elif tb:
        b_spec = pl.BlockSpec((None, tn, tk), lambda i, j, k: (b_layer, j, k))
    else:
        b_spec = pl.BlockSpec((None, tk, tn), lambda i, j, k: (b_layer, k, j))
    in_specs = [a_spec, b_spec]
    args = [a, b]
    if add is not None:
        in_specs.append(pl.BlockSpec((tm, tn), lambda i, j, k: (i, j)))
        args.append(add)
    aliases = {}
    if out_layer is None:
        o_spec = pl.BlockSpec((tm, tn), lambda i, j, k: (i, j))
        out_shape = jax.ShapeDtypeStruct((M, N), out_dtype)
    else:
        o_spec = pl.BlockSpec((None, tm, tn), lambda i, j, k: (out_layer, i, j))
        out_shape = jax.ShapeDtypeStruct((DEPTH, M, N), out_dtype)
        if out_stack is not None:
            aliases = {len(args): 0}
            in_specs.append(pl.BlockSpec(memory_space=pl.ANY))
            args.append(out_stack)
    return pl.pallas_call(
        body,
        out_shape=out_shape,
        grid=(M // tm, N // tn, nk),
        in_specs=in_specs,
        out_specs=o_spec,
        scratch_shapes=[pltpu.VMEM((tm, tn), F32)],
        input_output_aliases=aliases,
        compiler_params=_cp(("parallel", "parallel", "arbitrary")),
        name=name,
    )(*args)


def rmsnorm_fwd(x, g, *, name):
    S, D = x.shape
    T = _tile(S)

    def body(x_ref, g_ref, h_ref):
        xv = x_ref[...]
        r = lax.rsqrt(jnp.mean(xv * xv, axis=-1, keepdims=True) + EPS)
        h_ref[...] = (xv * r * g_ref[...]).astype(BF16)

    return pl.pallas_call(
        body,
        out_shape=jax.ShapeDtypeStruct((S, D), BF16),
        grid=(S // T,),
        in_specs=[pl.BlockSpec((T, D), lambda i: (i, 0)), pl.BlockSpec((1, D), lambda i: (0, 0))],
        out_specs=pl.BlockSpec((T, D), lambda i: (i, 0)),
        compiler_params=_cp(("parallel",)),
        name=name,
    )(x, g)


def rmsnorm_bwd(x, g, dh, dres, *, name):
    S, D = x.shape
    T = _tile(S)

    def body(x_ref, g_ref, dh_ref, dres_ref, dx_ref, dg_ref):
        i = pl.program_id(0)
        xv = x_ref[...]
        dhv = dh_ref[...].astype(F32)
        r = lax.rsqrt(jnp.mean(xv * xv, axis=-1, keepdims=True) + EPS)
        gd = dhv * g_ref[...]
        m = jnp.mean(xv * gd, axis=-1, keepdims=True)
        dx_ref[...] = dres_ref[...] + r * gd - xv * (r * r * r * m)

        @pl.when(i == 0)
        def _():
            dg_ref[...] = jnp.zeros_like(dg_ref)

        dg_ref[...] += jnp.sum(dhv * xv * r, axis=0, keepdims=True)

    row = pl.BlockSpec((T, D), lambda i: (i, 0))
    vec = pl.BlockSpec((1, D), lambda i: (0, 0))
    return pl.pallas_call(
        body,
        out_shape=(jax.ShapeDtypeStruct((S, D), F32), jax.ShapeDtypeStruct((1, D), F32)),
        grid=(S // T,),
        in_specs=[row, vec, row, row],
        out_specs=(row, vec),
        compiler_params=_cp(("arbitrary",)),
        name=name,
    )(x, g, dh, dres)


def _head_mean_matrix():
    r = lax.broadcasted_iota(jnp.int32, (WIDTH, WIDTH), 0) // HEAD_DIM
    c = lax.broadcasted_iota(jnp.int32, (WIDTH, WIDTH), 1) // HEAD_DIM
    return jnp.where(r == c, 1.0 / HEAD_DIM, 0.0).astype(BF16)


def _head_mean(v, mm):
    hi, lo = _split_hi_lo(v)
    return _dot(hi, mm, 1, 0) + _dot(lo, mm, 1, 0)


def qkv_prep(proj, gq, gk, *, name):
    S = proj.shape[0]
    T = _tile(S)

    def body(qa, ka, va, qb, kb, vb, gq_ref, gk_ref, oqa, oka, ova, oqb, okb, ovb):
        mm = _head_mean_matrix()
        for src, gref, dst, scale in ((qa, gq_ref, oqa, QK_SCALE), (ka, gk_ref, oka, 1.0)):
            v = src[...]
            r = lax.rsqrt(_head_mean(v * v, mm) + EPS)
            dst[...] = (v * r * gref[...] * scale).astype(BF16)
        oqb[...] = (qb[...] * QK_SCALE).astype(BF16)
        for src, dst in ((va, ova), (kb, okb), (vb, ovb)):
            dst[...] = src[...].astype(BF16)

    col = lambda j: pl.BlockSpec((T, WIDTH), lambda i, j=j: (i, j))
    vec = pl.BlockSpec((1, WIDTH), lambda i: (0, 0))
    out = pl.BlockSpec((T, WIDTH), lambda i: (i, 0))
    return pl.pallas_call(
        body,
        out_shape=tuple(jax.ShapeDtypeStruct((S, WIDTH), BF16) for _ in range(6)),
        grid=(S // T,),
        in_specs=[col(0), col(1), col(2), col(3), col(4), col(5), vec, vec],
        out_specs=tuple(out for _ in range(6)),
        compiler_params=_cp(("parallel",)),
        name=name,
    )(proj, proj, proj, proj, proj, proj, gq, gk)


def qknorm_bwd(proj, gq, gk, dqh, dkh, *, name):
    S = proj.shape[0]
    T = _tile(S)

    def body(qa, ka, gq_ref, gk_ref, dq_ref, dk_ref, oq, ok, ogq, ogk):
        i = pl.program_id(0)
        mm = _head_mean_matrix()

        @pl.when(i == 0)
        def _():
            ogq[...] = jnp.zeros_like(ogq)
            ogk[...] = jnp.zeros_like(ogk)

        for src, gref, dref, dst, gdst in ((qa, gq_ref, dq_ref, oq, ogq), (ka, gk_ref, dk_ref, ok, ogk)):
            v = src[...]
            dy = dref[...]
            r = lax.rsqrt(_head_mean(v * v, mm) + EPS)
            gd = dy * gref[...]
            m = _head_mean(v * gd, mm)
            dst[...] = (r * gd - v * (r * r * r * m)).astype(BF16)
            gdst[...] += jnp.sum(dy * v * r, axis=0, keepdims=True)

    col = lambda j: pl.BlockSpec((T, WIDTH), lambda i, j=j: (i, j))
    vec = pl.BlockSpec((1, WIDTH), lambda i: (0, 0))
    row = pl.BlockSpec((T, WIDTH), lambda i: (i, 0))
    return pl.pallas_call(
        body,
        out_shape=(jax.ShapeDtypeStruct((S, WIDTH), BF16), jax.ShapeDtypeStruct((S, WIDTH), BF16),
                   jax.ShapeDtypeStruct((1, WIDTH), F32), jax.ShapeDtypeStruct((1, WIDTH), F32)),
        grid=(S // T,),
        in_specs=[col(0), col(1), vec, vec, row, row],
        out_specs=(row, row, vec, vec),
        compiler_params=_cp(("arbitrary",)),
        name=name,
    )(proj, proj, gq, gk, dqh, dkh)


DIAG_W = 1024
N_VARIANTS = 3


def diagonal_onehot():
    jj = np.arange(DIAG_W)
    diff = np.where(jj < A_WIN, jj, jj - DIAG_W)
    out = np.zeros((N_VARIANTS, DIAG_W, REL_TABLE), np.float32)
    for v in range(N_VARIANTS):
        rel = np.clip(ATT_Q * v - diff, -(CHUNK - 1), MAX_REL) + (CHUNK - 1)
        out[v, jj, rel] = 1.0
    return out.reshape(N_VARIANTS * DIAG_W, REL_TABLE)


def exact_dot(a, b, *, name):
    def body(a_ref, b_ref, o_ref):
        o_ref[...] = jnp.dot(a_ref[...], b_ref[...], precision=lax.Precision.HIGHEST, preferred_element_type=F32)

    return pl.pallas_call(body, out_shape=jax.ShapeDtypeStruct((a.shape[0], b.shape[1]), F32),
                          compiler_params=_cp(), name=name)(a, b)


def _band_valid(v):
    qc = (lax.broadcasted_iota(jnp.int32, (ATT_Q, A_WIN), 0) + ATT_Q * v) // CHUNK
    kc = lax.broadcasted_iota(jnp.int32, (ATT_Q, A_WIN), 1) // CHUNK
    return (kc <= qc) & (kc >= qc - N_LEFT)


def bias_expand(diag, *, name):
    def body(d_ref, o_ref):
        rows = jnp.broadcast_to(d_ref[0, 0], (ATT_Q, DIAG_W))
        skew = pltpu.roll(rows, 0, 1, stride=1, stride_axis=0)
        o_ref[0, 0] = jnp.where(_band_valid(pl.program_id(0)), skew[:, :A_WIN], NEG)

    return pl.pallas_call(
        body,
        out_shape=jax.ShapeDtypeStruct((N_VARIANTS, N_HEADS, ATT_Q, A_WIN), F32),
        grid=(N_VARIANTS, N_HEADS),
        in_specs=[pl.BlockSpec((1, 1, 1, DIAG_W), lambda v, h: (v, h, 0, 0))],
        out_specs=pl.BlockSpec((1, 1, ATT_Q, A_WIN), lambda v, h: (v, h, 0, 0)),
        compiler_params=_cp(("parallel", "parallel")),
        name=name,
    )(diag)


def relbias_reduce(dbias, *, name):
    def body(db_ref, o_ref):
        x = jnp.concatenate([db_ref[0, 0], jnp.zeros((ATT_Q, DIAG_W - A_WIN), F32)], axis=1)
        row = lax.broadcasted_iota(jnp.int32, (ATT_Q, DIAG_W), 0)
        for b in range(8):
            x = jnp.where((row >> b) & 1 == 1, pltpu.roll(x, DIAG_W - (1 << b), 1), x)
        o_ref[0, 0] = jnp.sum(x, axis=0, keepdims=True)

    return pl.pallas_call(
        body,
        out_shape=jax.ShapeDtypeStruct((N_VARIANTS, N_HEADS, 1, DIAG_W), F32),
        grid=(N_VARIANTS, N_HEADS),
        in_specs=[pl.BlockSpec((1, 1, ATT_Q, A_WIN), lambda v, h: (v, h, 0, 0))],
        out_specs=pl.BlockSpec((1, 1, 1, DIAG_W), lambda v, h: (v, h, 0, 0)),
        compiler_params=_cp(("parallel", "parallel")),
        name=name,
    )(dbias)


def _a_window_start(qb):
    return pl.multiple_of(jnp.maximum(qb * ATT_Q - N_LEFT * CHUNK, 0), ATT_Q)


def attn_a_fwd(q, k, v, biasm, *, name):
    S = q.shape[0]
    nq = S // ATT_Q

    def body(q_ref, k_ref, v_ref, b_ref, o_ref):
        qb = pl.program_id(1)
        start = _a_window_start(qb)
        outs = []
        for h in range(2):
            lanes = slice(h * HEAD_DIM, (h + 1) * HEAD_DIM)
            qh = q_ref[:, lanes]
            kw = k_ref[pl.ds(start, A_WIN), lanes]
            vw = v_ref[pl.ds(start, A_WIN), lanes]
            s = _dot(qh, kw, 1, 1) + b_ref[0, h]
            m = jnp.max(s, axis=-1, keepdims=True)
            e = jnp.exp(s - m)
            p = e * (1.0 / jnp.sum(e, axis=-1, keepdims=True))
            outs.append(_dot(p.astype(BF16), vw, 1, 0))
        o_ref[...] = jnp.concatenate(outs, axis=1).astype(BF16)

    qspec = pl.BlockSpec((ATT_Q, 2 * HEAD_DIM), lambda hp, qb: (qb, hp))
    kvspec = pl.BlockSpec((S, 2 * HEAD_DIM), lambda hp, qb: (0, hp))
    bspec = pl.BlockSpec((1, 2, ATT_Q, A_WIN), lambda hp, qb: (jnp.minimum(qb, 2), hp, 0, 0))
    return pl.pallas_call(
        body,
        out_shape=jax.ShapeDtypeStruct((S, WIDTH), BF16),
        grid=(N_HEADS // 2, nq),
        in_specs=[qspec, kvspec, kvspec, bspec],
        out_specs=qspec,
        compiler_params=_cp(("parallel", "arbitrary")),
        name=name,
    )(q, k, v, biasm)


def attn_a_bwd(q, k, v, biasm, do, *, name):
    S = q.shape[0]
    nq = S // ATT_Q

    def body(q_ref, k_ref, v_ref, b_ref, do_ref, dq_ref, dk_ref, dv_ref, db_ref):
        qb = pl.program_id(1)
        start = _a_window_start(qb)

        @pl.when(qb == 0)
        def _():
            dk_ref[...] = jnp.zeros_like(dk_ref)
            dv_ref[...] = jnp.zeros_like(dv_ref)

        @pl.when(qb <= 2)
        def _():
            db_ref[...] = jnp.zeros_like(db_ref)

        dqs = []
        for h in range(2):
            lanes = slice(h * HEAD_DIM, (h + 1) * HEAD_DIM)
            qh = q_ref[:, lanes]
            doh = do_ref[:, lanes]
            kw = k_ref[pl.ds(start, A_WIN), lanes]
            vw = v_ref[pl.ds(start, A_WIN), lanes]
            s = _dot(qh, kw, 1, 1) + b_ref[0, h]
            m = jnp.max(s, axis=-1, keepdims=True)
            e = jnp.exp(s - m)
            p = e * (1.0 / jnp.sum(e, axis=-1, keepdims=True))
            dp = _dot(doh, vw, 1, 1)
            delta = jnp.sum(p * dp, axis=-1, keepdims=True)
            ds = p * (dp - delta)
            db_ref[0, h] += ds
            dsb = ds.astype(BF16)
            dqs.append(_dot(dsb, kw, 1, 0) * QK_SCALE)
            dk_ref[pl.ds(start, A_WIN), lanes] += _dot(dsb, qh, 0, 0)
            dv_ref[pl.ds(start, A_WIN), lanes] += _dot(p.astype(BF16), doh, 0, 0)
        dq_ref[...] = jnp.concatenate(dqs, axis=1)

    qspec = pl.BlockSpec((ATT_Q, 2 * HEAD_DIM), lambda hp, qb: (qb, hp))
    kvspec = pl.BlockSpec((S, 2 * HEAD_DIM), lambda hp, qb: (0, hp))
    bspec = pl.BlockSpec((1, 2, ATT_Q, A_WIN), lambda hp, qb: (jnp.minimum(qb, 2), hp, 0, 0))
    return pl.pallas_call(
        body,
        out_shape=(jax.ShapeDtypeStruct((S, WIDTH), F32), jax.ShapeDtypeStruct((S, WIDTH), F32),
                   jax.ShapeDtypeStruct((S, WIDTH), F32), jax.ShapeDtypeStruct((3, N_HEADS, ATT_Q, A_WIN), F32)),
        grid=(N_HEADS // 2, nq),
        in_specs=[qspec, kvspec, kvspec, bspec, qspec],
        out_specs=(qspec, kvspec, kvspec, bspec),
        compiler_params=_cp(("parallel", "arbitrary")),
        name=name,
    )(q, k, v, biasm, do)


def _tri(kind):
    j = lax.broadcasted_iota(jnp.int32, (ATT_Q, ATT_Q), 0)
    s = lax.broadcasted_iota(jnp.int32, (ATT_Q, ATT_Q), 1)
    if kind == "gt":
        m = j > s
    elif kind == "le":
        m = j <= s
    else:
        m = j < s
    return jnp.where(m, 1.0, 0.0).astype(BF16)


def _cum(v, tri):
    hi, lo = _split_hi_lo(v)
    return _dot(hi, tri, 1, 0) + _dot(lo, tri, 1, 0)


def _log_sigmoids(z, mask):
    t = jnp.log(1.0 + jnp.exp(-jnp.abs(z)))
    keep = -(jnp.maximum(z, 0.0) + t)
    take = jnp.minimum(z, 0.0) - t
    return (keep if mask is None else jnp.where(mask, keep, 0.0)), take


def _strictly_before():
    row = lax.broadcasted_iota(jnp.int32, (ATT_Q, ATT_Q), 0)
    col = lax.broadcasted_iota(jnp.int32, (ATT_Q, ATT_Q), 1)
    return col < row


EXIT_LOG = -104.0


def attn_b_fwd(q, k, v, *, gather=None, name):
    S = q.shape[0]
    nq = S // ATT_Q
    plan, shards, fulls = gather if gather is not None else (None, [], [])
    ng = len(shards)

    def body(q_ref, k_ref, v_ref, *rest):
        hp = pl.program_id(0)
        qb = pl.program_id(1)
        o_ref, t_ref, n_ref = rest[2 * ng:2 * ng + 3]
        if plan is not None:
            comm = (rest[:ng], rest[2 * ng + 3:3 * ng + 3], rest[3 * ng + 3:])

            @pl.when(jnp.logical_and(hp == 0, qb == 0))
            def _():
                plan.start(*comm)

        tri = _tri("gt")

        def block(kb, carry, mask):
            ks = pl.multiple_of(kb * ATT_Q, ATT_Q)
            new = []
            for h in range(2):
                lanes = slice(h * HEAD_DIM, (h + 1) * HEAD_DIM)
                c, acc = carry[h]
                z = _dot(q_ref[:, lanes], k_ref[pl.ds(ks, ATT_Q), lanes], 1, 1)
                keep, take = _log_sigmoids(z, mask)
                w = jnp.exp(take + (_cum(keep, tri) + c))
                if mask is not None:
                    w = jnp.where(mask, w, 0.0)
                acc = acc + _dot(w.astype(BF16), v_ref[pl.ds(ks, ATT_Q), lanes], 1, 0)
                c = c + jnp.sum(keep, axis=-1, keepdims=True)
                new.append((c, acc))
            return jnp.maximum(jnp.max(new[0][0]), jnp.max(new[1][0])), tuple(new)

        def cond(state):
            it, cmax, _ = state
            return jnp.logical_and(it <= qb, cmax >= EXIT_LOG)

        def step(state):
            it, _, carry = state
            cmax, carry = block(qb - it, carry, None)
            return it + 1, cmax, carry

        init = tuple((jnp.zeros((ATT_Q, 1), F32), jnp.zeros((ATT_Q, HEAD_DIM), F32)) for _ in range(2))
        cmax, diag = block(qb, init, _strictly_before())
        visited, _, res = lax.while_loop(cond, step, (jnp.int32(1), cmax, diag))
        o_ref[...] = jnp.concatenate([res[0][1], res[1][1]], axis=1).astype(BF16)
        t_ref[...] = jnp.concatenate([jnp.broadcast_to(res[h][0], (ATT_Q, HEAD_DIM)) for h in range(2)], axis=1)
        n_ref[hp, qb] = visited.astype(F32)
        if plan is not None:
            @pl.when(jnp.logical_and(hp == N_HEADS // 2 - 1, qb == nq - 1))
            def _():
                plan.finish(*comm)

    qspec = pl.BlockSpec((ATT_Q, 2 * HEAD_DIM), lambda hp, qb: (qb, hp))
    kvspec = pl.BlockSpec((S, 2 * HEAD_DIM), lambda hp, qb: (0, hp))
    outs = pl.pallas_call(
        body,
        out_shape=(jax.ShapeDtypeStruct((S, WIDTH), BF16), jax.ShapeDtypeStruct((S, WIDTH), F32),
                   jax.ShapeDtypeStruct((N_HEADS // 2, nq), F32))
        + tuple(jax.ShapeDtypeStruct(f.shape, f.dtype) for f in fulls),
        grid=(N_HEADS // 2, nq),
        in_specs=[qspec, kvspec, kvspec] + [ANY] * (2 * ng),
        out_specs=(qspec, qspec, pl.BlockSpec(memory_space=pltpu.SMEM)) + tuple([ANY] * ng),
        scratch_shapes=plan.scratch_shapes() if plan is not None else (),
        input_output_aliases={3 + ng + i: 3 + i for i in range(ng)},
        compiler_params=_cp(("arbitrary", "arbitrary")),
        name=name,
    )(q, k, v, *shards, *fulls)
    return outs if plan is None else (outs[0], outs[1], outs[2], list(outs[3:]))


def attn_b_bwd(q, k, v, tot, nblk, do, *, name):
    S = q.shape[0]
    nq = S // ATT_Q

    def body(q_ref, k_ref, v_ref, t_ref, n_ref, do_ref, dq_ref, dk_ref, dv_ref):
        hp = pl.program_id(0)
        qb = pl.program_id(1)
        first = jnp.clip(qb + 1 - n_ref[hp, qb].astype(jnp.int32), 0, qb + 1)
        tri_le = _tri("le")
        tri_lt = _tri("lt")

        @pl.when(qb == 0)
        def _():
            dk_ref[...] = jnp.zeros_like(dk_ref)
            dv_ref[...] = jnp.zeros_like(dv_ref)

        def block(kb, carry, mask):
            ks = pl.multiple_of(kb * ATT_Q, ATT_Q)
            new = []
            for h in range(2):
                lanes = slice(h * HEAD_DIM, (h + 1) * HEAD_DIM)
                cl, cg, dq = carry[h]
                qh = q_ref[:, lanes]
                doh = do_ref[:, lanes]
                kh = k_ref[pl.ds(ks, ATT_Q), lanes]
                vh = v_ref[pl.ds(ks, ATT_Q), lanes]
                totl = t_ref[:, h * HEAD_DIM:h * HEAD_DIM + 1]
                z = _dot(qh, kh, 1, 1)
                keep, take = _log_sigmoids(z, mask)
                sig = jnp.exp(take)
                w = sig * jnp.exp((totl - cl) - _cum(keep, tri_le))
                if mask is not None:
                    w = jnp.where(mask, w, 0.0)
                g = w * _dot(doh, vh, 1, 1)
                G = _dot(g.astype(BF16), tri_lt, 1, 0) + cg
                dz = g * (1.0 - sig) - sig * G
                if mask is not None:
                    dz = jnp.where(mask, dz, 0.0)
                dz = dz.astype(BF16)
                dq = dq + _dot(dz, kh, 1, 0)
                dk_ref[pl.ds(ks, ATT_Q), lanes] += _dot(dz, qh, 0, 0)
                dv_ref[pl.ds(ks, ATT_Q), lanes] += _dot(w.astype(BF16), doh, 0, 0)
                cl = cl + jnp.sum(keep, axis=-1, keepdims=True)
                cg = cg + jnp.sum(g, axis=-1, keepdims=True)
                new.append((cl, cg, dq))
            return tuple(new)

        init = tuple((jnp.zeros((ATT_Q, 1), F32), jnp.zeros((ATT_Q, 1), F32), jnp.zeros((ATT_Q, HEAD_DIM), F32))
                     for _ in range(2))
        res = lax.fori_loop(jnp.minimum(first, qb), qb, lambda kb, carry: block(kb, carry, None), init)
        res = block(qb, res, _strictly_before())
        dq_ref[...] = (jnp.concatenate([res[0][2], res[1][2]], axis=1) * QK_SCALE).astype(BF16)

    qspec = pl.BlockSpec((ATT_Q, 2 * HEAD_DIM), lambda hp, qb: (qb, hp))
    kvspec = pl.BlockSpec((S, 2 * HEAD_DIM), lambda hp, qb: (0, hp))
    return pl.pallas_call(
        body,
        out_shape=(jax.ShapeDtypeStruct((S, WIDTH), BF16), jax.ShapeDtypeStruct((S, WIDTH), F32),
                   jax.ShapeDtypeStruct((S, WIDTH), F32)),
        grid=(N_HEADS // 2, nq),
        in_specs=[qspec, kvspec, kvspec, qspec, pl.BlockSpec(memory_space=pltpu.SMEM), qspec],
        out_specs=(qspec, kvspec, kvspec),
        compiler_params=_cp(("parallel", "arbitrary")),
        name=name,
    )(q, k, v, tot, nblk, do)


U_COLBLK = 6


def _pool_counts(t0, rows):
    t = t0 + lax.broadcasted_iota(jnp.int32, (rows, WIDTH), 0)
    lane_grp = lax.broadcasted_iota(jnp.int32, (rows, WIDTH), 1) // GROUP_DIM
    win = jnp.where(lane_grp == 0, 2, jnp.where(lane_grp == 1, 4, jnp.where(lane_grp == 2, 8, 16)))
    cnt = jnp.minimum(t + 1, win)
    return 1.0 / cnt.astype(F32), lane_grp


def _window_sums(ext, shift_fn):
    s2 = ext + shift_fn(ext, 1)
    s4 = s2 + shift_fn(s2, 2)
    s8 = s4 + shift_fn(s4, 4)
    s16 = s8 + shift_fn(s8, 8)
    return s2, s4, s8, s16


def _select_group(lane_grp, s2, s4, s8, s16):
    return jnp.where(lane_grp == 0, s2, jnp.where(lane_grp == 1, s4, jnp.where(lane_grp == 2, s8, s16)))


def _pooled_tile(u_ref, h_ref, i, T):
    halo = jnp.where(i > 0, h_ref[...], 0.0)
    ext = jnp.concatenate([halo, u_ref[...]], axis=0)
    n = T + HALO
    sums = _window_sums(ext, lambda v, k: pltpu.roll(v, k, 0))
    inv, lane_grp = _pool_counts(i * T - HALO, n)
    pooled = _select_group(lane_grp, *sums) * inv - ext
    return pooled[HALO:, :]


def pool_fwd(proj, w_pool, scale, *, name):
    S = proj.shape[0]
    T = _tile(S)
    hb = T // HALO

    def body(u_ref, h_ref, w_ref, s_ref, o_ref):
        i = pl.program_id(0)
        pooled = _pooled_tile(u_ref, h_ref, i, T).astype(BF16)
        outs = [_dot(pooled[:, g * GROUP_DIM:(g + 1) * GROUP_DIM], w_ref[g], 1, 0) for g in range(4)]
        o_ref[...] = (jnp.concatenate(outs, axis=1) * s_ref[...]).astype(BF16)

    return pl.pallas_call(
        body,
        out_shape=jax.ShapeDtypeStruct((S, WIDTH), BF16),
        grid=(S // T,),
        in_specs=[pl.BlockSpec((T, WIDTH), lambda i: (i, U_COLBLK)),
                  pl.BlockSpec((HALO, WIDTH), lambda i: (jnp.maximum(i * hb - 1, 0), U_COLBLK)),
                  pl.BlockSpec((4, GROUP_DIM, GROUP_DIM), lambda i: (0, 0, 0)),
                  pl.BlockSpec((1, WIDTH), lambda i: (0, 0))],
        out_specs=pl.BlockSpec((T, WIDTH), lambda i: (i, 0)),
        compiler_params=_cp(("parallel",)),
        name=name,
    )(proj, proj, w_pool, scale)


def pool_bwd(proj, w_pool, scale, do, *, name):
    S = proj.shape[0]
    T = _tile(S)
    hb = T // HALO
    nt = S // T

    def body(u_ref, h_ref, w_ref, s_ref, do_ref, dof_ref, du_ref, dw_ref, ds_ref):
        i = pl.program_id(0)

        @pl.when(i == 0)
        def _():
            dw_ref[...] = jnp.zeros_like(dw_ref)
            ds_ref[...] = jnp.zeros_like(ds_ref)

        pooled = _pooled_tile(u_ref, h_ref, i, T).astype(BF16)
        dov = do_ref[...].astype(F32)
        fut = jnp.where(i < nt - 1, dof_ref[...].astype(F32), 0.0)
        dmix = (jnp.concatenate([dov, fut], axis=0) * s_ref[...]).astype(BF16)
        mixed, dpool = [], []
        for g in range(4):
            lanes = slice(g * GROUP_DIM, (g + 1) * GROUP_DIM)
            mixed.append(_dot(pooled[:, lanes], w_ref[g], 1, 0))
            dw_ref[g] += _dot(pooled[:, lanes], dmix[:T, lanes], 0, 0)
            dpool.append(_dot(dmix[:, lanes], w_ref[g], 1, 1))
        ds_ref[...] += jnp.sum(dov * jnp.concatenate(mixed, axis=1), axis=0, keepdims=True)
        dp = jnp.concatenate(dpool, axis=1)
        n = T + HALO
        inv, lane_grp = _pool_counts(i * T, n)
        sums = _window_sums(dp * inv, lambda v, k: pltpu.roll(v, n - k, 0))
        du = _select_group(lane_grp, *sums) - dp
        du_ref[...] = du[:T, :].astype(BF16)

    row = pl.BlockSpec((T, WIDTH), lambda i: (i, 0))
    return pl.pallas_call(
        body,
        out_shape=(jax.ShapeDtypeStruct((S, WIDTH), BF16), jax.ShapeDtypeStruct((4, GROUP_DIM, GROUP_DIM), F32),
                   jax.ShapeDtypeStruct((1, WIDTH), F32)),
        grid=(nt,),
        in_specs=[pl.BlockSpec((T, WIDTH), lambda i: (i, U_COLBLK)),
                  pl.BlockSpec((HALO, WIDTH), lambda i: (jnp.maximum(i * hb - 1, 0), U_COLBLK)),
                  pl.BlockSpec((4, GROUP_DIM, GROUP_DIM), lambda i: (0, 0, 0)),
                  pl.BlockSpec((1, WIDTH), lambda i: (0, 0)),
                  row,
                  pl.BlockSpec((HALO, WIDTH), lambda i: (jnp.minimum((i + 1) * hb, S // HALO - 1), 0))],
        out_specs=(row, pl.BlockSpec((4, GROUP_DIM, GROUP_DIM), lambda i: (0, 0, 0)),
                   pl.BlockSpec((1, WIDTH), lambda i: (0, 0))),
        compiler_params=_cp(("arbitrary",)),
        name=name,
    )(proj, proj, w_pool, scale, do, do)


GATE_BLK0 = GATE_COL0 // WIDTH


def merge_fwd(oa, ob, oc, proj, b_gate, wa, wb, wc, *, name):
    S = oa.shape[0]
    T = _tile(S)

    def body(oa_ref, ob_ref, oc_ref, ga, gb, gc, ba, bb, bc, wa_ref, wb_ref, wc_ref, m_ref):
        acc = None
        for o_ref, g_ref, b_ref, w_ref in ((oa_ref, ga, ba, wa_ref), (ob_ref, gb, bb, wb_ref), (oc_ref, gc, bc, wc_ref)):
            y = _dot(o_ref[...], w_ref[...], 1, 0)
            t = jax.nn.sigmoid(g_ref[...] + b_ref[...]) * y
            acc = t if acc is None else acc + t
        m_ref[...] = acc.astype(BF16)

    row = pl.BlockSpec((T, WIDTH), lambda i, n: (i, 0))
    gate = lambda b: pl.BlockSpec((T, WIDTH), lambda i, n, b=b: (i, GATE_BLK0 + 2 * b + n))
    bias = lambda b: pl.BlockSpec((1, WIDTH), lambda i, n, b=b: (0, 2 * b + n))
    wspec = pl.BlockSpec((WIDTH, WIDTH), lambda i, n: (0, n))
    return pl.pallas_call(
        body,
        out_shape=jax.ShapeDtypeStruct((S, D_MODEL), BF16),
        grid=(S // T, 2),
        in_specs=[row, row, row, gate(0), gate(1), gate(2), bias(0), bias(1), bias(2), wspec, wspec, wspec],
        out_specs=pl.BlockSpec((T, WIDTH), lambda i, n: (i, n)),
        compiler_params=_cp(("parallel", "parallel")),
        name=name,
    )(oa, ob, oc, proj, proj, proj, b_gate, b_gate, b_gate, wa, wb, wc)


def merge_bwd(dm, oa, ob, oc, proj, b_gate, wa, wb, wc, *, name):
    S = oa.shape[0]
    T = _tile(S)

    def body(dm_ref, oa_ref, ob_ref, oc_ref, ga, gb, gc, ba, bb, bc, wa_ref, wb_ref, wc_ref,
             ta, tb, tc, dga, dgb, dgc, dba, dbb, dbc):
        i = pl.program_id(1)
        dmv = dm_ref[...].astype(F32)
        for o_ref, g_ref, b_ref, w_ref, t_ref, dg_ref, db_ref in (
                (oa_ref, ga, ba, wa_ref, ta, dga, dba), (ob_ref, gb, bb, wb_ref, tb, dgb, dbb),
                (oc_ref, gc, bc, wc_ref, tc, dgc, dbc)):
            y = _dot(o_ref[...], w_ref[...], 1, 0)
            gate = jax.nn.sigmoid(g_ref[...] + b_ref[...])
            t_ref[...] = (gate * dmv).astype(BF16)
            dgl = dmv * y * gate * (1.0 - gate)
            dg_ref[...] = dgl.astype(BF16)

            @pl.when(i == 0)
            def _():
                db_ref[...] = jnp.zeros_like(db_ref)

            db_ref[...] += jnp.sum(dgl, axis=0, keepdims=True)

    row = pl.BlockSpec((T, WIDTH), lambda n, i: (i, 0))
    half = pl.BlockSpec((T, WIDTH), lambda n, i: (i, n))
    gate = lambda b: pl.BlockSpec((T, WIDTH), lambda n, i, b=b: (i, GATE_BLK0 + 2 * b + n))
    bias = lambda b: pl.BlockSpec((1, WIDTH), lambda n, i, b=b: (0, 2 * b + n))
    wspec = pl.BlockSpec((WIDTH, WIDTH), lambda n, i: (0, n))
    bvec = pl.BlockSpec((1, WIDTH), lambda n, i: (0, n))
    act = jax.ShapeDtypeStruct((S, D_MODEL), BF16)
    vec = jax.ShapeDtypeStruct((1, D_MODEL), F32)
    return pl.pallas_call(
        body,
        out_shape=(act, act, act, act, act, act, vec, vec, vec),
        grid=(2, S // T),
        in_specs=[half, row, row, row, gate(0), gate(1), gate(2), bias(0), bias(1), bias(2), wspec, wspec, wspec],
        out_specs=(half, half, half, half, half, half, bvec, bvec, bvec),
        compiler_params=_cp(("parallel", "arbitrary")),
        name=name,
    )(dm, oa, ob, oc, proj, proj, proj, b_gate, b_gate, b_gate, wa, wb, wc)


FF_T = 256
FF_BLKS = D_FF // FF_T


def _silu_parts(x):
    s = jax.nn.sigmoid(x)
    return x * s, s


def _conv3(ext, w_ref, b_ref):
    return (b_ref[...] + w_ref[0:1, :] * pltpu.roll(ext, 2, 0) + w_ref[1:2, :] * pltpu.roll(ext, 1, 0)
            + w_ref[2:3, :] * ext)


def conv_glu_fwd(u, conv_w, conv_b, *, name):
    S = u.shape[0]
    T = _tile(S)
    hb = T // CONV_HALO

    def body(ug, ugh, uv, uvh, wg, wv, bg, bv, a_ref):
        i = pl.program_id(1)
        cs = []
        for m_ref, h_ref, w_ref, b_ref in ((ug, ugh, wg, bg), (uv, uvh, wv, bv)):
            halo = jnp.where(i > 0, h_ref[...], 0.0)
            ext = jnp.concatenate([halo, m_ref[...]], axis=0)
            cs.append(_conv3(ext, w_ref, b_ref)[CONV_HALO:, :])
        act, _ = _silu_parts(cs[0])
        a_ref[...] = (act * cs[1]).astype(BF16)

    main = lambda o: pl.BlockSpec((T, FF_T), lambda c, i, o=o: (i, c + o))
    halo = lambda o: pl.BlockSpec((CONV_HALO, FF_T), lambda c, i, o=o: (jnp.maximum(i * hb - 1, 0), c + o))
    wsp = lambda o: pl.BlockSpec((3, FF_T), lambda c, i, o=o: (0, c + o))
    bsp = lambda o: pl.BlockSpec((1, FF_T), lambda c, i, o=o: (0, c + o))
    return pl.pallas_call(
        body,
        out_shape=jax.ShapeDtypeStruct((S, D_FF), BF16),
        grid=(FF_BLKS, S // T),
        in_specs=[main(0), halo(0), main(FF_BLKS), halo(FF_BLKS), wsp(0), wsp(FF_BLKS), bsp(0), bsp(FF_BLKS)],
        out_specs=pl.BlockSpec((T, FF_T), lambda c, i: (i, c)),
        compiler_params=_cp(("parallel", "parallel")),
        name=name,
    )(u, u, u, u, conv_w, conv_w, conv_b, conv_b)


def conv_glu_bwd(u, conv_w, conv_b, da, *, name):
    S = u.shape[0]
    T = _tile(S)
    hb = T // CONV_HALO
    nt = S // T
    n = T + 2 * CONV_HALO

    def body(ug, ugp, ugf, uv, uvp, uvf, wg, wv, bg, bv, da_ref, daf_ref,
             dug, duv, dwg, dwv, dbg, dbv):
        i = pl.program_id(1)
        first, last = i == 0, i == nt - 1
        exts, cs = [], []
        for m_ref, p_ref, f_ref, w_ref, b_ref in ((ug, ugp, ugf, wg, bg), (uv, uvp, uvf, wv, bv)):
            ext = jnp.concatenate([jnp.where(first, 0.0, p_ref[...]), m_ref[...], jnp.where(last, 0.0, f_ref[...])], axis=0)
            exts.append(ext)
            cs.append(_conv3(ext, w_ref, b_ref))
        dae = jnp.concatenate([jnp.zeros((CONV_HALO, FF_T), F32), da_ref[...].astype(F32),
                               jnp.where(last, 0.0, daf_ref[...].astype(F32))], axis=0)
        act, sg = _silu_parts(cs[0])
        dcs = (dae * cs[1] * (sg * (1.0 + cs[0] * (1.0 - sg))), dae * act)
        main = slice(CONV_HALO, CONV_HALO + T)
        for ext, dc, w_ref, du_ref, dw_ref, db_ref in ((exts[0], dcs[0], wg, dug, dwg, dbg),
                                                       (exts[1], dcs[1], wv, duv, dwv, dbv)):
            du = (w_ref[2:3, :] * dc + w_ref[1:2, :] * pltpu.roll(dc, n - 1, 0) + w_ref[0:1, :] * pltpu.roll(dc, n - 2, 0))
            du_ref[...] = du[main, :].astype(BF16)
            dcm = dc[main, :]
            rows = [jnp.sum(dcm * pltpu.roll(ext, 2 - j, 0)[main, :], axis=0, keepdims=True) if j < 2
                    else jnp.sum(dcm * ext[main, :], axis=0, keepdims=True) for j in range(3)]

            @pl.when(first)
            def _():
                dw_ref[...] = jnp.zeros_like(dw_ref)
                db_ref[...] = jnp.zeros_like(db_ref)

            dw_ref[...] += jnp.concatenate(rows, axis=0)
            db_ref[...] += jnp.sum(dcm, axis=0, keepdims=True)

    main = lambda o: pl.BlockSpec((T, FF_T), lambda c, i, o=o: (i, c + o))
    past = lambda o: pl.BlockSpec((CONV_HALO, FF_T), lambda c, i, o=o: (jnp.maximum(i * hb - 1, 0), c + o))
    fut = lambda o: pl.BlockSpec((CONV_HALO, FF_T), lambda c, i, o=o: (jnp.minimum((i + 1) * hb, S // CONV_HALO - 1), c + o))
    wsp = lambda o: pl.BlockSpec((3, FF_T), lambda c, i, o=o: (0, c + o))
    bsp = lambda o: pl.BlockSpec((1, FF_T), lambda c, i, o=o: (0, c + o))
    return pl.pallas_call(
        body,
        out_shape=(jax.ShapeDtypeStruct((S, D_FF), BF16), jax.ShapeDtypeStruct((S, D_FF), BF16),
                   jax.ShapeDtypeStruct((3, D_FF), F32), jax.ShapeDtypeStruct((3, D_FF), F32),
                   jax.ShapeDtypeStruct((1, D_FF), F32), jax.ShapeDtypeStruct((1, D_FF), F32)),
        grid=(FF_BLKS, nt),
        in_specs=[main(0), past(0), fut(0), main(FF_BLKS), past(FF_BLKS), fut(FF_BLKS),
                  wsp(0), wsp(FF_BLKS), bsp(0), bsp(FF_BLKS), main(0), fut(0)],
        out_specs=(main(0), main(0), wsp(0), wsp(0), bsp(0), bsp(0)),
        compiler_params=_cp(("parallel", "arbitrary")),
        name=name,
    )(u, u, u, u, u, u, conv_w, conv_w, conv_b, conv_b, da, da)


def loss_head(y, target, *, name):
    S, D = y.shape
    T = _tile(S)

    def body(y_ref, t_ref, dy_ref, l_ref):
        i = pl.program_id(0)
        err = y_ref[...] - t_ref[...]
        dy_ref[...] = err * (1.0 / D)

        @pl.when(i == 0)
        def _():
            l_ref[...] = jnp.zeros_like(l_ref)

        l_ref[...] += 0.5 * jnp.sum(jnp.mean(err * err, axis=-1, keepdims=True))

    row = pl.BlockSpec((T, D), lambda i: (i, 0))
    return pl.pallas_call(
        body,
        out_shape=(jax.ShapeDtypeStruct((S, D), F32), jax.ShapeDtypeStruct((8, 128), F32)),
        grid=(S // T,),
        in_specs=[row, row],
        out_specs=(row, pl.BlockSpec((8, 128), lambda i: (0, 0))),
        compiler_params=_cp(("arbitrary",)),
        name=name,
    )(y, target)


ELEMS_PER_BLOCK = 256 * 1024


def _rows_tile(rows, cols):
    if rows * cols <= ELEMS_PER_BLOCK or rows % 8:
        return rows
    best = 8
    for tr in range(8, rows + 1, 8):
        if rows % tr == 0 and tr * cols <= ELEMS_PER_BLOCK:
            best = tr
    return best


def _adamw_math(g, w_ref, m_ref, v_ref, g_out, d_out, m_out, v_out):
    mn = ADAM_B1 * m_ref[...] + (1.0 - ADAM_B1) * g
    vn = ADAM_B2 * v_ref[...] + (1.0 - ADAM_B2) * (g * g)
    m_hat = mn / (1.0 - ADAM_B1 ** ADAM_STEP)
    v_hat = vn / (1.0 - ADAM_B2 ** ADAM_STEP)
    g_out[...] = g
    d_out[...] = -ADAM_LR * (m_hat / (jnp.sqrt(v_hat) + ADAM_EPS) + ADAM_WD * w_ref[...])
    m_out[...] = mn
    v_out[...] = vn


def adamw(w, m, v, g, *, name):
    rows, cols = w.shape
    tr = _rows_tile(rows, cols)

    def body(w_ref, m_ref, v_ref, g_ref, g_out, d_out, m_out, v_out):
        _adamw_math(g_ref[...], w_ref, m_ref, v_ref, g_out, d_out, m_out, v_out)

    spec = pl.BlockSpec((tr, cols), lambda i: (i, 0))
    shp = jax.ShapeDtypeStruct((rows, cols), F32)
    return pl.pallas_call(
        body,
        out_shape=(shp, shp, shp, shp),
        grid=(rows // tr,),
        in_specs=[spec] * 4,
        out_specs=(spec, spec, spec, spec),
        compiler_params=_cp(("parallel",)),
        name=name,
    )(w, m, v, g)


ANY = pl.BlockSpec(memory_space=pl.ANY)
STAGE_BYTES = 2 * 1024 * 1024


def _mesh_pos():
    return lax.axis_index("x"), lax.axis_index("y"), lax.axis_index("c")


def _chip_peers(x, y):
    return [(1 - x, y), (x, 1 - y), (1 - x, 1 - y)]


def _all_peers(x, y, c):
    return [((1 - x) if (r >> 2) & 1 else x, (1 - y) if (r >> 1) & 1 else y, (1 - c) if r & 1 else c)
            for r in range(1, 8)]


def _shard_slice(ref, axis, j, size, layer=None):
    idx = [slice(None)] * 3
    idx[axis] = pl.ds(pl.multiple_of(j * size, 128 if axis == 2 else 16), size)
    if layer is not None:
        idx[0] = pl.ds(layer, 1)
    return ref.at[tuple(idx)]


class LayerGather:
    def __init__(self, shards, axes, layer):
        self.nt = len(shards)
        self.axes = list(axes)
        self.layer = layer
        self.shapes = [s.shape for s in shards]
        self.dtypes = [s.dtype for s in shards]
        self.sizes = [s.shape[a] for s, a in zip(shards, axes)]
        self.split = [s.shape[1] % 32 == 0 for s in shards]
        self.half_rows = [s.shape[1] // 2 if sp else s.shape[1] for s, sp in zip(shards, self.split)]
        self.chunk_rows = []
        for s in shards:
            rt = s.shape[1]
            while rt % 32 == 0 and rt * s.shape[2] * s.dtype.itemsize > STAGE_BYTES:
                rt //= 2
            self.chunk_rows.append(rt)

    def full_shapes(self):
        out = []
        for shp, a, sz, dt in zip(self.shapes, self.axes, self.sizes, self.dtypes):
            shp = list(shp)
            shp[a] = 4 * sz
            out.append(jax.ShapeDtypeStruct(tuple(shp), dt))
        return out

    def scratch_shapes(self):
        return ([pltpu.VMEM((1, rt, shp[2]), dt) for shp, rt, dt in zip(self.shapes, self.chunk_rows, self.dtypes)]
                + [pltpu.SemaphoreType.DMA((2,))] + [pltpu.SemaphoreType.DMA((3 * self.nt,)) for _ in range(4)])

    def _views(self, ins, outs, scratch):
        nt = self.nt
        stage, stage_sems = scratch[:nt], scratch[nt]
        ici_send, ici_recv, d2d_send, d2d_recv = scratch[nt + 1:]
        x, y, c = _mesh_pos()
        mine = 2 * x + y
        peers = _chip_peers(x, y)
        layer = pl.ds(self.layer, 1)

        def rows(t, half, r0=0, n=None):
            hr = self.half_rows[t]
            if n is None:
                return pl.ds(pl.multiple_of(half * hr, 16), hr) if self.split[t] else pl.ds(0, hr)
            return pl.ds(r0, n)

        def placed(t, blk, row_sel, row_len):
            sz = self.sizes[t]
            if self.axes[t] == 2:
                return outs[t].at[layer, row_sel, pl.ds(pl.multiple_of(blk * sz, 128), sz)]
            return outs[t].at[layer, pl.ds(pl.multiple_of(blk * sz, 16) + row_sel.start, row_len), :]

        def ici(t, k, blk):
            px, py = peers[k]
            sel = rows(t, c)
            return pltpu.make_async_remote_copy(
                src_ref=ins[t].at[layer, sel, :], dst_ref=placed(t, blk, sel, self.half_rows[t]),
                send_sem=ici_send.at[3 * t + k], recv_sem=ici_recv.at[3 * t + k],
                device_id=(px, py, c), device_id_type=MESH_T)

        def d2d(t, k, half):
            px, py = peers[k]
            piece = placed(t, 2 * px + py, rows(t, half), self.half_rows[t])
            return pltpu.make_async_remote_copy(
                src_ref=piece, dst_ref=piece, send_sem=d2d_send.at[3 * t + k], recv_sem=d2d_recv.at[3 * t + k],
                device_id=(x, y, 1 - c), device_id_type=MESH_T)

        def own_chunk(t, r0):
            rt = self.chunk_rows[t]
            sel = pl.ds(r0, rt)
            return ins[t].at[layer, sel, :], placed(t, mine, sel, rt), stage[t], stage_sems

        return c, mine, peers, ici, d2d, own_chunk

    def start(self, ins, outs, scratch):
        c, mine, peers, ici, d2d, own_chunk = self._views(ins, outs, scratch)
        for t in range(self.nt):
            for k in range(3):
                ici(t, k, mine).start()
        for t in range(self.nt):
            for r0 in range(0, self.shapes[t][1], self.chunk_rows[t]):
                src, dst, buf, sems = own_chunk(t, r0)
                load = pltpu.make_async_copy(src, buf, sems.at[0])
                load.start()
                load.wait()
                store = pltpu.make_async_copy(buf, dst, sems.at[1])
                store.start()
                store.wait()

    def finish(self, ins, outs, scratch):
        c, mine, peers, ici, d2d, own_chunk = self._views(ins, outs, scratch)
        for t in range(self.nt):
            for k, (px, py) in enumerate(peers):
                ici(t, k, 2 * px + py).wait_recv()
                if self.split[t]:
                    d2d(t, k, c).start()
        for t in range(self.nt):
            for k in range(3):
                if self.split[t]:
                    d2d(t, k, 1 - c).wait_recv()
        for t in range(self.nt):
            for k in range(3):
                ici(t, k, mine).wait_send()
                if self.split[t]:
                    d2d(t, k, c).wait_send()


def all_gather_layer(shards, axes, layer, *, name):
    plan = LayerGather(shards, axes, layer)
    nt = plan.nt

    def body(*refs):
        ins, outs, scratch = refs[:nt], refs[nt:2 * nt], refs[2 * nt:]
        plan.start(ins, outs, scratch)
        plan.finish(ins, outs, scratch)

    return pl.pallas_call(
        body,
        out_shape=tuple(plan.full_shapes()),
        in_specs=[ANY] * nt,
        out_specs=tuple([ANY] * nt),
        scratch_shapes=plan.scratch_shapes(),
        name=name,
    )(*shards)


def sibling_layer_exchange(grads, *, name):
    nt = len(grads)

    def body(*refs):
        ins, outs = refs[:nt], refs[nt:2 * nt]
        send_sems, recv_sems = refs[2 * nt:]
        x, y, c = _mesh_pos()
        cps = []
        for t in range(nt):
            cp = pltpu.make_async_remote_copy(src_ref=ins[t].at[pl.ds(1 - c, 1)], dst_ref=outs[t],
                                              send_sem=send_sems.at[t], recv_sem=recv_sems.at[t],
                                              device_id=(x, y, 1 - c), device_id_type=MESH_T)
            cp.start()
            cps.append(cp)
        for cp in cps:
            cp.wait_recv()
        for cp in cps:
            cp.wait_send()

    return pl.pallas_call(
        body,
        out_shape=tuple(jax.ShapeDtypeStruct((1,) + g.shape[1:], F32) for g in grads),
        in_specs=[ANY] * nt,
        out_specs=tuple([ANY] * nt),
        scratch_shapes=[pltpu.SemaphoreType.DMA((nt,)), pltpu.SemaphoreType.DMA((nt,))],
        name=name,
    )(*grads)


def pair_sum(grad, other, layer, *, name):
    _, R, C = grad.shape
    tr = _rows_tile(R, C)

    def body(l_ref, g_ref, o_ref, s32_ref, s16_ref):
        s = g_ref[...] + o_ref[...]
        s32_ref[...] = s
        s16_ref[...] = s.astype(BF16)

    blk = pl.BlockSpec((1, tr, C), lambda i, l: (0, i, 0))
    return pl.pallas_call(
        body,
        out_shape=(jax.ShapeDtypeStruct((1, R, C), F32), jax.ShapeDtypeStruct((1, R, C), BF16)),
        grid_spec=pltpu.PrefetchScalarGridSpec(
            num_scalar_prefetch=1, grid=(R // tr,),
            in_specs=[pl.BlockSpec((1, tr, C), lambda i, l: (l[0], i, 0)), blk],
            out_specs=(blk, blk)),
        compiler_params=_cp(("parallel",)),
        name=name,
    )(layer, grad, other)


def scatter_pairs(pairs16, axes, small, *, name):
    nt = len(pairs16)
    sizes = [g.shape[a] // 4 for g, a in zip(pairs16, axes)]

    def body(*refs):
        in16, small_in = refs[:nt], refs[nt]
        recv, small_out = refs[nt + 1:2 * nt + 1], refs[2 * nt + 1]
        send_sems, recv_sems, ssend, srecv = refs[2 * nt + 2:]
        x, y, c = _mesh_pos()
        me = 4 * x + 2 * y + c
        sends, recvs = [], []
        for t in range(nt):
            for k, (px, py) in enumerate(_chip_peers(x, y)):
                snd = pltpu.make_async_remote_copy(
                    src_ref=_shard_slice(in16[t], axes[t], 2 * px + py, sizes[t]), dst_ref=recv[t].at[k],
                    send_sem=send_sems.at[3 * t + k], recv_sem=recv_sems.at[3 * t + k],
                    device_id=(px, py, c), device_id_type=MESH_T)
                snd.start()
                sends.append(snd)
                recvs.append(snd)
        for r, (px, py, pc) in enumerate(_all_peers(x, y, c)):
            def mk(slot, r=r, px=px, py=py, pc=pc):
                return pltpu.make_async_remote_copy(
                    src_ref=small_in, dst_ref=small_out.at[slot], send_sem=ssend.at[r], recv_sem=srecv.at[r],
                    device_id=(px, py, pc), device_id_type=MESH_T)
            snd = mk(me)
            snd.start()
            sends.append(snd)
            recvs.append(mk(4 * px + 2 * py + pc))
        for r in recvs:
            r.wait_recv()
        for s in sends:
            s.wait_send()

    out_shape = []
    for g, a, sz in zip(pairs16, axes, sizes):
        shp = list(g.shape)
        shp[a] = sz
        out_shape.append(jax.ShapeDtypeStruct((3,) + tuple(shp), BF16))
    out_shape.append(jax.ShapeDtypeStruct((8,) + small.shape, F32))
    return pl.pallas_call(
        body,
        out_shape=tuple(out_shape),
        in_specs=[ANY] * (nt + 1),
        out_specs=tuple([ANY] * (nt + 1)),
        scratch_shapes=[pltpu.SemaphoreType.DMA((3 * nt,)), pltpu.SemaphoreType.DMA((3 * nt,)),
                        pltpu.SemaphoreType.DMA((7,)), pltpu.SemaphoreType.DMA((7,))],
        name=name,
    )(*pairs16, small)


def sum_chips(pair32, recv, axis, chip, *, name):
    _, _, r, c = recv.shape
    tr = _rows_tile(r, c)
    nr = r // tr

    def body(j_ref, p_ref, r_ref, s_ref):
        acc = p_ref[...]
        for k in range(3):
            acc = acc + r_ref[k].astype(F32)
        s_ref[...] = acc

    if axis == 2:
        own = pl.BlockSpec((1, tr, c), lambda i, j: (0, i, j[0]))
    else:
        own = pl.BlockSpec((1, tr, c), lambda i, j: (0, j[0] * nr + i, 0))
    blk = pl.BlockSpec((1, tr, c), lambda i, j: (0, i, 0))
    return pl.pallas_call(
        body,
        out_shape=jax.ShapeDtypeStruct((1, r, c), F32),
        grid_spec=pltpu.PrefetchScalarGridSpec(
            num_scalar_prefetch=1, grid=(nr,),
            in_specs=[own, pl.BlockSpec((3, 1, tr, c), lambda i, j: (0, 0, i, 0))],
            out_specs=blk),
        compiler_params=_cp(("parallel",)),
        name=name,
    )(chip, pair32, recv)


def sum_devices(gathered, own, me, *, name):
    _, R, C = gathered.shape

    def body(me_ref, g_ref, o_ref, s_ref):
        acc = None
        for k in range(8):
            part = jnp.where(me_ref[0] == k, o_ref[...], g_ref[k])
            acc = part if acc is None else acc + part
        s_ref[...] = acc

    return pl.pallas_call(
        body,
        out_shape=jax.ShapeDtypeStruct((R, C), F32),
        grid_spec=pltpu.PrefetchScalarGridSpec(
            num_scalar_prefetch=1, grid=(1,),
            in_specs=[pl.BlockSpec((8, R, C), lambda i, m: (0, 0, 0)), pl.BlockSpec((R, C), lambda i, m: (0, 0))],
            out_specs=pl.BlockSpec((R, C), lambda i, m: (0, 0))),
        compiler_params=_cp(("arbitrary",)),
        name=name,
    )(me, gathered, own)


def sibling_swap(parts, *, name):
    nt = len(parts)

    def body(*refs):
        ins, outs = refs[:nt], refs[nt:2 * nt]
        send_sems, recv_sems = refs[2 * nt:]
        x, y, c = _mesh_pos()
        cps = []
        for t in range(nt):
            cp = pltpu.make_async_remote_copy(src_ref=ins[t], dst_ref=outs[t], send_sem=send_sems.at[t],
                                              recv_sem=recv_sems.at[t], device_id=(x, y, 1 - c), device_id_type=MESH_T)
            cp.start()
            cps.append(cp)
        for cp in cps:
            cp.wait_recv()
        for cp in cps:
            cp.wait_send()

    return pl.pallas_call(
        body,
        out_shape=tuple(jax.ShapeDtypeStruct(p.shape, p.dtype) for p in parts),
        in_specs=[ANY] * nt,
        out_specs=tuple([ANY] * nt),
        scratch_shapes=[pltpu.SemaphoreType.DMA((nt,)), pltpu.SemaphoreType.DMA((nt,))],
        name=name,
    )(*parts)


def adamw_layers(w, m, v, g_mine, g_other, layer, *, name):
    _, r, c = w.shape
    tr = _rows_tile(r, c)

    def body(l_ref, w_ref, m_ref, v_ref, a_ref, b_ref, g_out, d_out, m_out, v_out):
        g = jnp.where(pl.program_id(0) == l_ref[0], a_ref[...], b_ref[...])
        _adamw_math(g, w_ref, m_ref, v_ref, g_out, d_out, m_out, v_out)

    full = pl.BlockSpec((1, tr, c), lambda l, i, s: (l, i, 0))
    half = pl.BlockSpec((1, tr, c), lambda l, i, s: (0, i, 0))
    shp = jax.ShapeDtypeStruct((2, r, c), F32)
    return pl.pallas_call(
        body,
        out_shape=(shp, shp, shp, shp),
        grid_spec=pltpu.PrefetchScalarGridSpec(
            num_scalar_prefetch=1, grid=(2, r // tr),
            in_specs=[full, full, full, half, half],
            out_specs=(full, full, full, full)),
        compiler_params=_cp(("parallel", "parallel")),
        name=name,
    )(layer, w, m, v, g_mine, g_other)


WEIGHTS = ("norm_mix", "w_in", "b_gate", "q_norm_a", "k_norm_a", "rel_bias_a", "w_pool", "pool_scale",
           "w_branch_a", "w_branch_b", "w_branch_c", "w_out", "norm_ffn", "w_up", "conv_w", "conv_b", "w_down")
SHARDED = {"w_in": 2, "w_branch_a": 2, "w_branch_b": 2, "w_branch_c": 2, "w_out": 1, "w_up": 2, "conv_w": 2,
           "w_down": 1}
REPLICATED = tuple(n for n in WEIGHTS if n not in SHARDED)
MATMUL_WEIGHTS = tuple(n for n in SHARDED if n != "conv_w")
SMALL_ROWS = 1232


def _layer_fwd(x, p, tables, prefetch=None):
    full = p["full"]
    diag = exact_dot(p["rel_bias_a"], tables["onehot_t"], name="bias_diagonals")
    diag = diag.reshape(N_HEADS, N_VARIANTS, 1, DIAG_W).transpose(1, 0, 2, 3)
    biasm = bias_expand(diag, name="bias_expand")
    gq8 = jnp.tile(p["q_norm_a"], N_HEADS)[None]
    gk8 = jnp.tile(p["k_norm_a"], N_HEADS)[None]
    h = rmsnorm_fwd(x, p["norm_mix"][None], name="rmsnorm_fwd")
    proj = matmul(h, full["w_in"], b_layer=p["l"], name="mm_in")
    qa, ka, va, qb, kb, vb = qkv_prep(proj, gq8, gk8, name="qkv_prep")
    oa = attn_a_fwd(qa, ka, va, biasm, name="attn_a_fwd")
    if prefetch is None:
        ob, tot, nblk = attn_b_fwd(qb, kb, vb, name="attn_b_fwd")
    else:
        plan, shards, names = prefetch
        ob, tot, nblk, filled = attn_b_fwd(qb, kb, vb, gather=(plan, shards, [full[n] for n in names]),
                                           name="attn_b_fwd_gather")
        full = dict(zip(names, filled))
    wpool = p["w_pool"].astype(BF16)
    oc = pool_fwd(proj, wpool, p["pool_scale"][None], name="pool_fwd")
    merged = merge_fwd(oa, ob, oc, proj, p["b_gate"][None], p["w_branch_a"], p["w_branch_b"], p["w_branch_c"],
                       name="merge_fwd")
    x1 = matmul(merged, full["w_out"], b_layer=p["l"], add=x, name="mm_out")
    h2 = rmsnorm_fwd(x1, p["norm_ffn"][None], name="rmsnorm_fwd")
    u = matmul(h2, full["w_up"], b_layer=p["l"], name="mm_up")
    a = conv_glu_fwd(u, p["conv_w"], p["conv_b"][None], name="conv_glu_fwd")
    x2 = matmul(a, full["w_down"], b_layer=p["l"], add=x1, name="mm_down")
    saved = dict(x=x, h=h, proj=proj, qa=qa, ka=ka, va=va, qb=qb, kb=kb, vb=vb, oa=oa, ob=ob, tot=tot, nblk=nblk, oc=oc,
                 merged=merged, x1=x1, h2=h2, u=u, a=a, biasm=biasm, gq8=gq8, gk8=gk8, wpool=wpool)
    return x2, saved, full


def _layer_bwd(dx2, s, p, tables, stacks):
    g = {}
    full, l = p["full"], p["l"]
    into = lambda n: dict(out_layer=l, out_stack=stacks.get(n))
    da = matmul(dx2, full["w_down"], b_layer=l, tb=True, name="mm_down_dx")
    g["w_down"] = matmul(s["a"], dx2, ta=True, name="mm_down_dw", **into("w_down"))
    dug, duv, dcwg, dcwv, dcbg, dcbv = conv_glu_bwd(s["u"], p["conv_w"], p["conv_b"][None], da, name="conv_glu_bwd")
    du = jnp.concatenate([dug, duv], axis=1)
    g["conv_w"] = jnp.concatenate([dcwg, dcwv], axis=1)
    g["conv_b"] = jnp.concatenate([dcbg, dcbv], axis=1)[0]
    g["w_up"] = matmul(s["h2"], du, ta=True, name="mm_up_dw", **into("w_up"))
    dh2 = matmul(du, full["w_up"], b_layer=l, tb=True, name="mm_up_dx")
    dx1, dg2 = rmsnorm_bwd(s["x1"], p["norm_ffn"][None], dh2, dx2, name="rmsnorm_bwd")
    g["norm_ffn"] = dg2[0]
    dmerged = matmul(dx1, full["w_out"], b_layer=l, tb=True, name="mm_out_dx")
    g["w_out"] = matmul(s["merged"], dx1, ta=True, name="mm_out_dw", **into("w_out"))
    t_a, t_b, t_c, dga, dgb, dgc, dba, dbb, dbc = merge_bwd(
        dmerged, s["oa"], s["ob"], s["oc"], s["proj"], p["b_gate"][None], p["w_branch_a"], p["w_branch_b"],
        p["w_branch_c"], name="merge_bwd")
    g["b_gate"] = jnp.concatenate([dba, dbb, dbc], axis=1)[0]
    g["w_branch_a"] = matmul(s["oa"], t_a, ta=True, name="mm_branch_dw", **into("w_branch_a"))
    g["w_branch_b"] = matmul(s["ob"], t_b, ta=True, name="mm_branch_dw", **into("w_branch_b"))
    g["w_branch_c"] = matmul(s["oc"], t_c, ta=True, name="mm_branch_dw", **into("w_branch_c"))
    doa = matmul(t_a, full["w_branch_a"], b_layer=l, tb=True, out_dtype=BF16, name="mm_branch_dx")
    dob = matmul(t_b, full["w_branch_b"], b_layer=l, tb=True, out_dtype=BF16, name="mm_branch_dx")
    doc = matmul(t_c, full["w_branch_c"], b_layer=l, tb=True, name="mm_branch_dx_f32")
    dqh, dkh, dva, dbias = attn_a_bwd(s["qa"], s["ka"], s["va"], s["biasm"], doa, name="attn_a_bwd")
    ddiag = relbias_reduce(dbias, name="relbias_reduce")
    ddiag = ddiag.transpose(1, 0, 2, 3).reshape(N_HEADS, N_VARIANTS * DIAG_W)
    g["rel_bias_a"] = exact_dot(ddiag, tables["onehot"], name="relbias_table")
    dqa, dka, dgq8, dgk8 = qknorm_bwd(s["proj"], s["gq8"], s["gk8"], dqh, dkh, name="qknorm_bwd")
    g["q_norm_a"] = dgq8.reshape(N_HEADS, HEAD_DIM).sum(axis=0)
    g["k_norm_a"] = dgk8.reshape(N_HEADS, HEAD_DIM).sum(axis=0)
    dqb, dkb, dvb = attn_b_bwd(s["qb"], s["kb"], s["vb"], s["tot"], s["nblk"], dob, name="attn_b_bwd")
    duc, dwp, dsc = pool_bwd(s["proj"], s["wpool"], p["pool_scale"][None], doc, name="pool_bwd")
    g["w_pool"] = dwp
    g["pool_scale"] = dsc[0]
    dproj = jnp.concatenate([dqa, dka, dva.astype(BF16), dqb, dkb.astype(BF16), dvb.astype(BF16), duc,
                             dga, dgb, dgc], axis=1)
    g["w_in"] = matmul(s["h"], dproj, ta=True, name="mm_in_dw", **into("w_in"))
    dh = matmul(dproj, full["w_in"], b_layer=l, tb=True, name="mm_in_dx")
    dx, dg1 = rmsnorm_bwd(s["x"], p["norm_mix"][None], dh, dx1, name="rmsnorm_bwd")
    g["norm_mix"] = dg1[0]
    return dx, g


def kernel(x, norm_mix, w_in, b_gate, q_norm_a, k_norm_a, rel_bias_a, w_pool, pool_scale, w_branch_a, w_branch_b, w_branch_c, w_out, norm_ffn, w_up, conv_w, conv_b, w_down, loss_target, m_norm_mix, m_w_in, m_b_gate, m_q_norm_a, m_k_norm_a, m_rel_bias_a, m_w_pool, m_pool_scale, m_w_branch_a, m_w_branch_b, m_w_branch_c, m_w_out, m_norm_ffn, m_w_up, m_conv_w, m_conv_b, m_w_down, v_norm_mix, v_w_in, v_b_gate, v_q_norm_a, v_k_norm_a, v_rel_bias_a, v_w_pool, v_pool_scale, v_w_branch_a, v_w_branch_b, v_w_branch_c, v_w_out, v_norm_ffn, v_w_up, v_conv_w, v_conv_b, v_w_down):
    w = dict(zip(WEIGHTS, (norm_mix, w_in, b_gate, q_norm_a, k_norm_a, rel_bias_a, w_pool, pool_scale, w_branch_a,
                           w_branch_b, w_branch_c, w_out, norm_ffn, w_up, conv_w, conv_b, w_down)))
    m = dict(zip(WEIGHTS, (m_norm_mix, m_w_in, m_b_gate, m_q_norm_a, m_k_norm_a, m_rel_bias_a, m_w_pool, m_pool_scale,
                           m_w_branch_a, m_w_branch_b, m_w_branch_c, m_w_out, m_norm_ffn, m_w_up, m_conv_w, m_conv_b,
                           m_w_down)))
    v = dict(zip(WEIGHTS, (v_norm_mix, v_w_in, v_b_gate, v_q_norm_a, v_k_norm_a, v_rel_bias_a, v_w_pool, v_pool_scale,
                           v_w_branch_a, v_w_branch_b, v_w_branch_c, v_w_out, v_norm_ffn, v_w_up, v_conv_w, v_conv_b,
                           v_w_down)))
    onehot = diagonal_onehot()
    tables = dict(onehot=jnp.asarray(onehot), onehot_t=jnp.asarray(np.ascontiguousarray(onehot.T)))

    names = tuple(SHARDED)
    shards = [w[n] if n == "conv_w" else w[n].astype(BF16) for n in names]
    axes = [SHARDED[n] for n in names]
    full = dict(zip(names, all_gather_layer(shards, axes, 0, name="all_gather_layer")))

    def layer_params(l):
        p = {n: full[n][l] for n in ("w_branch_a", "w_branch_b", "w_branch_c", "conv_w")}
        p.update({n: w[n][l] for n in REPLICATED})
        p.update(full=full, l=l)
        return p

    xs = x[0]
    saved = []
    for l in range(DEPTH):
        prefetch = (LayerGather(shards, axes, l + 1), shards, names) if l + 1 < DEPTH else None
        xs, s, full = _layer_fwd(xs, layer_params(l), tables, prefetch)
        saved.append(s)
    dx, lpart = loss_head(xs, loss_target[0], name="loss_head")
    loss = lax.psum(lpart[0, 0], MESH_AXES)
    grads = [None] * DEPTH
    stacks = {}
    for l in reversed(range(DEPTH)):
        dx, grads[l] = _layer_bwd(dx, saved[l], layer_params(l), tables, stacks)
        stacks = {n: grads[l][n] for n in MATMUL_WEIGHTS}
    g = {n: stacks[n] if n in stacks else jnp.stack([grads[l][n] for l in range(DEPTH)]) for n in WEIGHTS}

    flat = jnp.concatenate([g[n].reshape(-1) for n in REPLICATED])
    small = jnp.pad(flat, (0, SMALL_ROWS * 128 - flat.shape[0])).reshape(SMALL_ROWS, 128)
    nt = len(names)
    as_index = lambda i: jnp.reshape(i, (1,)).astype(jnp.int32)
    cx, cy, cc = _mesh_pos()
    layer, chip, me = as_index(cc), as_index(2 * cx + cy), as_index(4 * cx + 2 * cy + cc)
    others = sibling_layer_exchange([g[n] for n in names], name="sibling_layer_exchange")
    pairs = [pair_sum(g[n], o, layer, name="pair_sum") for n, o in zip(names, others)]
    outs = scatter_pairs([p[1] for p in pairs], axes, small, name="scatter_pairs")
    finals = [sum_chips(p[0], recv, a, chip, name="sum_chips") for p, recv, a in zip(pairs, outs[:nt], axes)]
    finals_other = sibling_swap(finals, name="sibling_swap")
    small_sum = sum_devices(outs[-1], small, me, name="sum_devices").reshape(-1)

    res = {}
    for n, mine, other in zip(names, finals, finals_other):
        res[n] = adamw_layers(w[n], m[n], v[n], mine, other, layer, name="adamw_layers")
    off = 0
    for n in REPLICATED:
        shp = w[n].shape
        size = int(np.prod(shp))
        gn = small_sum[off:off + size]
        off += size
        cols = shp[-1]
        two_d = lambda t: t.reshape(size // cols, cols)
        res[n] = [t.reshape(shp) for t in adamw(two_d(w[n]), two_d(m[n]), two_d(v[n]), two_d(gn), name="adamw")]

    out = [loss, dx[None]]
    for k in range(4):
        out.extend(res[n][k] for n in WEIGHTS)
    return tuple(out)
```

```python
import jax
import jax.numpy as jnp
import numpy as np
from jax import lax
from jax.experimental import pallas as pl
from jax.experimental.pallas import tpu as pltpu

F32 = jnp.float32
BF16 = jnp.bfloat16

D_MODEL = 1024
DEPTH = 2
CHUNK = 64
N_LEFT = 8
HEAD_DIM = 64
N_HEADS = 8
WIDTH = 512
POOL_WINDOWS = (2, 4, 8, 16)
GROUP_DIM = 128
MAX_REL = 2 * CHUNK
REL_TABLE = MAX_REL + CHUNK
D_FF = 2816
EPS = 1e-6
QK_SCALE = 0.125
IN_COLS = 7 * WIDTH + 3 * D_MODEL
GATE_COL0 = 7 * WIDTH

ADAM_LR = 0.001
ADAM_B1 = 0.9
ADAM_B2 = 0.999
ADAM_EPS = 1e-08
ADAM_WD = 0.01
ADAM_STEP = 10

VMEM_LIMIT = 56 * 1024 * 1024
ATT_Q = 256
A_Q = 256
A_WIN = A_Q + N_LEFT * CHUNK
HALO = 16
CONV_HALO = 8
NEG = -1e30

MESH_AXES = ("x", "y", "c")
MESH_T = pl.DeviceIdType.MESH


def _cp(sem=None, vmem=VMEM_LIMIT):
    return pltpu.CompilerParams(dimension_semantics=sem, vmem_limit_bytes=vmem)


def _dot(a, b, ca, cb):
    return lax.dot_general(a, b, (((ca,), (cb,)), ((), ())), preferred_element_type=F32)


def _tile(n, cands=(512, 256, 128)):
    for c in cands:
        if n % c == 0:
            return c
    return n


def _split_hi_lo(v):
    hi = v.astype(BF16)
    lo = (v - hi.astype(F32)).astype(BF16)
    return hi, lo


def matmul(a, b, *, ta=False, tb=False, add=None, out_dtype=F32, b_layer=None, out_layer=None, out_stack=None,
           name):
    if ta:
        K, M = a.shape
    else:
        M, K = a.shape
    if tb:
        N, K2 = b.shape[-2:]
    else:
        K2, N = b.shape[-2:]
    assert K == K2, (a.shape, b.shape, ta, tb)
    big = (1024, 1408, 512, 256, 128)
    tm, tn, tk = _tile(M, big), _tile(N, big), _tile(K, big)
    nk = K // tk

    def body(*refs):
        a_ref, b_ref = refs[:2]
        r_ref = refs[2] if add is not None else None
        o_ref, acc = refs[-2:]
        k = pl.program_id(2)

        @pl.when(k == 0)
        def _():
            acc[...] = jnp.zeros_like(acc)

        av = a_ref[...].astype(BF16)
        bv = b_ref[...].astype(BF16)
        acc[...] += _dot(av, bv, 0 if ta else 1, 1 if tb else 0)

        @pl.when(k == nk - 1)
        def _():
            r = acc[...]
            if add is not None:
                r = r + r_ref[...].astype(F32)
            o_ref[...] = r.astype(out_dtype)

    a_spec = pl.BlockSpec((tk, tm), lambda i, j, k: (k, i)) if ta else pl.BlockSpec((tm, tk), lambda i, j, k: (i, k))
    if b_layer is None:
        b_spec = pl.BlockSpec((tn, tk), lambda i, j, k: (j, k)) if tb else pl.BlockSpec((tk, tn), lambda i, j, k: (k, j))
    elif tb:
        b_spec = pl.BlockSpec((None, tn, tk), lambda i, j, k: (b_layer, j, k))
    else:
        b_spec = pl.BlockSpec((None, tk, tn), lambda i, j, k: (b_layer, k, j))
    in_specs = [a_spec, b_spec]
    args = [a, b]
    if add is not None:
        in_specs.append(pl.BlockSpec((tm, tn), lambda i, j, k: (i, j)))
        args.append(add)
    aliases = {}
    if out_layer is None:
        o_spec = pl.BlockSpec((tm, tn), lambda i, j, k: (i, j))
        out_shape = jax.ShapeDtypeStruct((M, N), out_dtype)
    else:
        o_spec = pl.BlockSpec((None, tm, tn), lambda i, j, k: (out_layer, i, j))
        out_shape = jax.ShapeDtypeStruct((DEPTH, M, N), out_dtype)
        if out_stack is not None:
            aliases = {len(args): 0}
            in_specs.append(pl.BlockSpec(memory_space=pl.ANY))
            args.append(out_stack)
    return pl.pallas_call(
        body,
        out_shape=out_shape,
        grid=(M // tm, N // tn, nk),
        in_specs=in_specs,
        out_specs=o_spec,
        scratch_shapes=[pltpu.VMEM((tm, tn), F32)],
        input_output_aliases=aliases,
        compiler_params=_cp(("parallel", "parallel", "arbitrary")),
        name=name,
    )(*args)


def rmsnorm_fwd(x, g, *, name):
    S, D = x.shape
    T = _tile(S)

    def body(x_ref, g_ref, h_ref):
        xv = x_ref[...]
        r = lax.rsqrt(jnp.mean(xv * xv, axis=-1, keepdims=True) + EPS)
        h_ref[...] = (xv * r * g_ref[...]).astype(BF16)

    return pl.pallas_call(
        body,
        out_shape=jax.ShapeDtypeStruct((S, D), BF16),
        grid=(S // T,),
        in_specs=[pl.BlockSpec((T, D), lambda i: (i, 0)), pl.BlockSpec((1, D), lambda i: (0, 0))],
        out_specs=pl.BlockSpec((T, D), lambda i: (i, 0)),
        compiler_params=_cp(("parallel",)),
        name=name,
    )(x, g)


def rmsnorm_bwd(x, g, dh, dres, *, name):
    S, D = x.shape
    T = _tile(S)

    def body(x_ref, g_ref, dh_ref, dres_ref, dx_ref, dg_ref):
        i = pl.program_id(0)
        xv = x_ref[...]
        dhv = dh_ref[...].astype(F32)
        r = lax.rsqrt(jnp.mean(xv * xv, axis=-1, keepdims=True) + EPS)
        gd = dhv * g_ref[...]
        m = jnp.mean(xv * gd, axis=-1, keepdims=True)
        dx_ref[...] = dres_ref[...] + r * gd - xv * (r * r * r * m)

        @pl.when(i == 0)
        def _():
            dg_ref[...] = jnp.zeros_like(dg_ref)

        dg_ref[...] += jnp.sum(dhv * xv * r, axis=0, keepdims=True)

    row = pl.BlockSpec((T, D), lambda i: (i, 0))
    vec = pl.BlockSpec((1, D), lambda i: (0, 0))
    return pl.pallas_call(
        body,
        out_shape=(jax.ShapeDtypeStruct((S, D), F32), jax.ShapeDtypeStruct((1, D), F32)),
        grid=(S // T,),
        in_specs=[row, vec, row, row],
        out_specs=(row, vec),
        compiler_params=_cp(("arbitrary",)),
        name=name,
    )(x, g, dh, dres)


def _head_mean_matrix():
    r = lax.broadcasted_iota(jnp.int32, (WIDTH, WIDTH), 0) // HEAD_DIM
    c = lax.broadcasted_iota(jnp.int32, (WIDTH, WIDTH), 1) // HEAD_DIM
    return jnp.where(r == c, 1.0 / HEAD_DIM, 0.0).astype(BF16)


def _head_mean(v, mm):
    hi, lo = _split_hi_lo(v)
    return _dot(hi, mm, 1, 0) + _dot(lo, mm, 1, 0)


def qkv_prep(proj, gq, gk, *, name):
    S = proj.shape[0]
    T = _tile(S)

    def body(qa, ka, va, qb, kb, vb, gq_ref, gk_ref, oqa, oka, ova, oqb, okb, ovb):
        mm = _head_mean_matrix()
        for src, gref, dst, scale in ((qa, gq_ref, oqa, QK_SCALE), (ka, gk_ref, oka, 1.0)):
            v = src[...]
            r = lax.rsqrt(_head_mean(v * v, mm) + EPS)
            dst[...] = (v * r * gref[...] * scale).astype(BF16)
        oqb[...] = (qb[...] * QK_SCALE).astype(BF16)
        for src, dst in ((va, ova), (kb, okb), (vb, ovb)):
            dst[...] = src[...].astype(BF16)

    col = lambda j: pl.BlockSpec((T, WIDTH), lambda i, j=j: (i, j))
    vec = pl.BlockSpec((1, WIDTH), lambda i: (0, 0))
    out = pl.BlockSpec((T, WIDTH), lambda i: (i, 0))
    return pl.pallas_call(
        body,
        out_shape=tuple(jax.ShapeDtypeStruct((S, WIDTH), BF16) for _ in range(6)),
        grid=(S // T,),
        in_specs=[col(0), col(1), col(2), col(3), col(4), col(5), vec, vec],
        out_specs=tuple(out for _ in range(6)),
        compiler_params=_cp(("parallel",)),
        name=name,
    )(proj, proj, proj, proj, proj, proj, gq, gk)


def qknorm_bwd(proj, gq, gk, dqh, dkh, *, name):
    S = proj.shape[0]
    T = _tile(S)

    def body(qa, ka, gq_ref, gk_ref, dq_ref, dk_ref, oq, ok, ogq, ogk):
        i = pl.program_id(0)
        mm = _head_mean_matrix()

        @pl.when(i == 0)
        def _():
            ogq[...] = jnp.zeros_like(ogq)
            ogk[...] = jnp.zeros_like(ogk)

        for src, gref, dref, dst, gdst in ((qa, gq_ref, dq_ref, oq, ogq), (ka, gk_ref, dk_ref, ok, ogk)):
            v = src[...]
            dy = dref[...]
            r = lax.rsqrt(_head_mean(v * v, mm) + EPS)
            gd = dy * gref[...]
            m = _head_mean(v * gd, mm)
            dst[...] = (r * gd - v * (r * r * r * m)).astype(BF16)
            gdst[...] += jnp.sum(dy * v * r, axis=0, keepdims=True)

    col = lambda j: pl.BlockSpec((T, WIDTH), lambda i, j=j: (i, j))
    vec = pl.BlockSpec((1, WIDTH), lambda i: (0, 0))
    row = pl.BlockSpec((T, WIDTH), lambda i: (i, 0))
    return pl.pallas_call(
        body,
        out_shape=(jax.ShapeDtypeStruct((S, WIDTH), BF16), jax.ShapeDtypeStruct((S, WIDTH), BF16),
                   jax.ShapeDtypeStruct((1, WIDTH), F32), jax.ShapeDtypeStruct((1, WIDTH), F32)),
        grid=(S // T,),
        in_specs=[col(0), col(1), vec, vec, row, row],
        out_specs=(row, row, vec, vec),
        compiler_params=_cp(("arbitrary",)),
        name=name,
    )(proj, proj, gq, gk, dqh, dkh)


DIAG_W = 1024
N_VARIANTS = N_LEFT * CHUNK // A_Q + 1


def diagonal_onehot():
    jj = np.arange(DIAG_W)
    diff = np.where(jj < A_WIN, jj, jj - DIAG_W)
    out = np.zeros((N_VARIANTS, DIAG_W, REL_TABLE), np.float32)
    for v in range(N_VARIANTS):
        rel = np.clip(A_Q * v - diff, -(CHUNK - 1), MAX_REL) + (CHUNK - 1)
        out[v, jj, rel] = 1.0
    return out.reshape(N_VARIANTS * DIAG_W, REL_TABLE)


def exact_dot(a, b, *, name):
    def body(a_ref, b_ref, o_ref):
        o_ref[...] = jnp.dot(a_ref[...], b_ref[...], precision=lax.Precision.HIGHEST, preferred_element_type=F32)

    return pl.pallas_call(body, out_shape=jax.ShapeDtypeStruct((a.shape[0], b.shape[1]), F32),
                          compiler_params=_cp(), name=name)(a, b)


def _band_valid(v):
    qc = (lax.broadcasted_iota(jnp.int32, (A_Q, A_WIN), 0) + A_Q * v) // CHUNK
    kc = lax.broadcasted_iota(jnp.int32, (A_Q, A_WIN), 1) // CHUNK
    return (kc <= qc) & (kc >= qc - N_LEFT)


def bias_expand(diag, *, name):
    def body(d_ref, o_ref):
        rows = jnp.broadcast_to(d_ref[0, 0], (A_Q, DIAG_W))
        skew = pltpu.roll(rows, 0, 1, stride=1, stride_axis=0)
        o_ref[0, 0] = jnp.where(_band_valid(pl.program_id(0)), skew[:, :A_WIN], NEG)

    return pl.pallas_call(
        body,
        out_shape=jax.ShapeDtypeStruct((N_VARIANTS, N_HEADS, A_Q, A_WIN), F32),
        grid=(N_VARIANTS, N_HEADS),
        in_specs=[pl.BlockSpec((1, 1, 1, DIAG_W), lambda v, h: (v, h, 0, 0))],
        out_specs=pl.BlockSpec((1, 1, A_Q, A_WIN), lambda v, h: (v, h, 0, 0)),
        compiler_params=_cp(("parallel", "parallel")),
        name=name,
    )(diag)


def relbias_reduce(dbias, *, name):
    def body(db_ref, o_ref):
        acc = None
        for a in range(A_Q // 8):
            x = jnp.concatenate([db_ref[0, 0, 8 * a:8 * a + 8, :], jnp.zeros((8, DIAG_W - A_WIN), F32)], axis=1)
            x = pltpu.roll(x, DIAG_W - 8 * a, 1) if a else x
            acc = x if acc is None else acc + x
        row = lax.broadcasted_iota(jnp.int32, (8, DIAG_W), 0)
        for b in range(3):
            acc = jnp.where((row >> b) & 1 == 1, pltpu.roll(acc, DIAG_W - (1 << b), 1), acc)
        o_ref[0, 0] = jnp.sum(acc, axis=0, keepdims=True)

    return pl.pallas_call(
        body,
        out_shape=jax.ShapeDtypeStruct((N_VARIANTS, N_HEADS, 1, DIAG_W), F32),
        grid=(N_VARIANTS, N_HEADS),
        in_specs=[pl.BlockSpec((1, 1, A_Q, A_WIN), lambda v, h: (v, h, 0, 0))],
        out_specs=pl.BlockSpec((1, 1, 1, DIAG_W), lambda v, h: (v, h, 0, 0)),
        compiler_params=_cp(("parallel", "parallel")),
        name=name,
    )(dbias)


def _a_window_start(qb):
    return pl.multiple_of(jnp.maximum(qb * A_Q - N_LEFT * CHUNK, 0), A_Q)


def attn_a_fwd(q, k, v, biasm, *, name):
    S = q.shape[0]
    nq = S // A_Q

    def body(q_ref, k_ref, v_ref, b_ref, o_ref):
        qb = pl.program_id(1)
        start = _a_window_start(qb)
        outs = []
        for h in range(2):
            lanes = slice(h * HEAD_DIM, (h + 1) * HEAD_DIM)
            qh = q_ref[:, lanes]
            kw = k_ref[pl.ds(start, A_WIN), lanes]
            vw = v_ref[pl.ds(start, A_WIN), lanes]
            s = _dot(qh, kw, 1, 1) + b_ref[0, h]
            m = jnp.max(s, axis=-1, keepdims=True)
            e = jnp.exp(s - m)
            outs.append(_dot(e.astype(BF16), vw, 1, 0) * (1.0 / jnp.sum(e, axis=-1, keepdims=True)))
        o_ref[...] = jnp.concatenate(outs, axis=1).astype(BF16)

    qspec = pl.BlockSpec((A_Q, 2 * HEAD_DIM), lambda hp, qb: (qb, hp))
    kvspec = pl.BlockSpec((S, 2 * HEAD_DIM), lambda hp, qb: (0, hp))
    bspec = pl.BlockSpec((1, 2, A_Q, A_WIN), lambda hp, qb: (jnp.minimum(qb, N_VARIANTS - 1), hp, 0, 0))
    return pl.pallas_call(
        body,
        out_shape=jax.ShapeDtypeStruct((S, WIDTH), BF16),
        grid=(N_HEADS // 2, nq),
        in_specs=[qspec, kvspec, kvspec, bspec],
        out_specs=qspec,
        compiler_params=_cp(("parallel", "arbitrary")),
        name=name,
    )(q, k, v, biasm)


def attn_a_bwd(q, k, v, biasm, do, *, name):
    S = q.shape[0]
    nq = S // A_Q

    def body(q_ref, k_ref, v_ref, b_ref, do_ref, dq_ref, dk_ref, dv_ref, db_ref):
        qb = pl.program_id(1)
        start = _a_window_start(qb)

        @pl.when(qb == 0)
        def _():
            dk_ref[...] = jnp.zeros_like(dk_ref)
            dv_ref[...] = jnp.zeros_like(dv_ref)

        @pl.when(qb < N_VARIANTS)
        def _():
            db_ref[...] = jnp.zeros_like(db_ref)

        dqs = []
        for h in range(2):
            lanes = slice(h * HEAD_DIM, (h + 1) * HEAD_DIM)
            qh = q_ref[:, lanes]
            doh = do_ref[:, lanes]
            kw = k_ref[pl.ds(start, A_WIN), lanes]
            vw = v_ref[pl.ds(start, A_WIN), lanes]
            s = _dot(qh, kw, 1, 1) + b_ref[0, h]
            m = jnp.max(s, axis=-1, keepdims=True)
            e = jnp.exp(s - m)
            p = e * (1.0 / jnp.sum(e, axis=-1, keepdims=True))
            dp = _dot(doh, vw, 1, 1)
            delta = jnp.sum(p * dp, axis=-1, keepdims=True)
            ds = p * (dp - delta)
            db_ref[0, h] += ds
            dsb = ds.astype(BF16)
            dqs.append(_dot(dsb, kw, 1, 0) * QK_SCALE)
            dk_ref[pl.ds(start, A_WIN), lanes] += _dot(dsb, qh, 0, 0)
            dv_ref[pl.ds(start, A_WIN), lanes] += _dot(p.astype(BF16), doh, 0, 0)
        dq_ref[...] = jnp.concatenate(dqs, axis=1)

    qspec = pl.BlockSpec((A_Q, 2 * HEAD_DIM), lambda hp, qb: (qb, hp))
    kvspec = pl.BlockSpec((S, 2 * HEAD_DIM), lambda hp, qb: (0, hp))
    bspec = pl.BlockSpec((1, 2, A_Q, A_WIN), lambda hp, qb: (jnp.minimum(qb, N_VARIANTS - 1), hp, 0, 0))
    return pl.pallas_call(
        body,
        out_shape=(jax.ShapeDtypeStruct((S, WIDTH), F32), jax.ShapeDtypeStruct((S, WIDTH), F32),
                   jax.ShapeDtypeStruct((S, WIDTH), F32), jax.ShapeDtypeStruct((N_VARIANTS, N_HEADS, A_Q, A_WIN), F32)),
        grid=(N_HEADS // 2, nq),
        in_specs=[qspec, kvspec, kvspec, bspec, qspec],
        out_specs=(qspec, kvspec, kvspec, bspec),
        compiler_params=_cp(("parallel", "arbitrary")),
        name=name,
    )(q, k, v, biasm, do)


def _tri(kind):
    j = lax.broadcasted_iota(jnp.int32, (ATT_Q, ATT_Q), 0)
    s = lax.broadcasted_iota(jnp.int32, (ATT_Q, ATT_Q), 1)
    if kind == "gt":
        m = j > s
    elif kind == "le":
        m = j <= s
    else:
        m = j < s
    return jnp.where(m, 1.0, 0.0).astype(BF16)


def _cum(v, tri):
    hi, lo = _split_hi_lo(v)
    return _dot(hi, tri, 1, 0) + _dot(lo, tri, 1, 0)


def _log_sigmoids(z, mask):
    t = jnp.log(1.0 + jnp.exp(-jnp.abs(z)))
    keep = -(jnp.maximum(z, 0.0) + t)
    take = jnp.minimum(z, 0.0) - t
    return (keep if mask is None else jnp.where(mask, keep, 0.0)), take


def _strictly_before():
    row = lax.broadcasted_iota(jnp.int32, (ATT_Q, ATT_Q), 0)
    col = lax.broadcasted_iota(jnp.int32, (ATT_Q, ATT_Q), 1)
    return col < row


EXIT_LOG = -104.0


def attn_b_fwd(q, k, v, *, gather=None, name):
    S = q.shape[0]
    nq = S // ATT_Q
    plan, shards, fulls = gather if gather is not None else (None, [], [])
    ng = len(shards)

    def body(q_ref, k_ref, v_ref, *rest):
        hp = pl.program_id(0)
        qb = pl.program_id(1)
        o_ref, t_ref, n_ref = rest[2 * ng:2 * ng + 3]
        if plan is not None:
            comm = (rest[:ng], rest[2 * ng + 3:3 * ng + 3], rest[3 * ng + 3:])

            @pl.when(jnp.logical_and(hp == 0, qb == 0))
            def _():
                plan.start(*comm)

        tri = _tri("gt")

        def block(kb, carry, mask):
            ks = pl.multiple_of(kb * ATT_Q, ATT_Q)
            new = []
            for h in range(2):
                lanes = slice(h * HEAD_DIM, (h + 1) * HEAD_DIM)
                c, acc = carry[h]
                z = _dot(q_ref[:, lanes], k_ref[pl.ds(ks, ATT_Q), lanes], 1, 1)
                keep, take = _log_sigmoids(z, mask)
                w = jnp.exp(take + (_cum(keep, tri) + c))
                if mask is not None:
                    w = jnp.where(mask, w, 0.0)
                acc = acc + _dot(w.astype(BF16), v_ref[pl.ds(ks, ATT_Q), lanes], 1, 0)
                c = c + jnp.sum(keep, axis=-1, keepdims=True)
                new.append((c, acc))
            return jnp.maximum(jnp.max(new[0][0]), jnp.max(new[1][0])), tuple(new)

        def cond(state):
            it, cmax, _ = state
            return jnp.logical_and(it <= qb, cmax >= EXIT_LOG)

        def step(state):
            it, _, carry = state
            cmax, carry = block(qb - it, carry, None)
            return it + 1, cmax, carry

        init = tuple((jnp.zeros((ATT_Q, 1), F32), jnp.zeros((ATT_Q, HEAD_DIM), F32)) for _ in range(2))
        cmax, diag = block(qb, init, _strictly_before())
        visited, _, res = lax.while_loop(cond, step, (jnp.int32(1), cmax, diag))
        o_ref[...] = jnp.concatenate([res[0][1], res[1][1]], axis=1).astype(BF16)
        t_ref[...] = jnp.concatenate([jnp.broadcast_to(res[h][0], (ATT_Q, HEAD_DIM)) for h in range(2)], axis=1)
        n_ref[hp, qb] = visited.astype(F32)
        if plan is not None:
            @pl.when(jnp.logical_and(hp == N_HEADS // 2 - 1, qb == nq - 1))
            def _():
                plan.finish(*comm)

    qspec = pl.BlockSpec((ATT_Q, 2 * HEAD_DIM), lambda hp, qb: (qb, hp))
    kvspec = pl.BlockSpec((S, 2 * HEAD_DIM), lambda hp, qb: (0, hp))
    outs = pl.pallas_call(
        body,
        out_shape=(jax.ShapeDtypeStruct((S, WIDTH), BF16), jax.ShapeDtypeStruct((S, WIDTH), F32),
                   jax.ShapeDtypeStruct((N_HEADS // 2, nq), F32))
        + tuple(jax.ShapeDtypeStruct(f.shape, f.dtype) for f in fulls),
        grid=(N_HEADS // 2, nq),
        in_specs=[qspec, kvspec, kvspec] + [ANY] * (2 * ng),
        out_specs=(qspec, qspec, pl.BlockSpec(memory_space=pltpu.SMEM)) + tuple([ANY] * ng),
        scratch_shapes=plan.scratch_shapes() if plan is not None else (),
        input_output_aliases={3 + ng + i: 3 + i for i in range(ng)},
        compiler_params=_cp(("arbitrary", "arbitrary")),
        name=name,
    )(q, k, v, *shards, *fulls)
    return outs if plan is None else (outs[0], outs[1], outs[2], list(outs[3:]))


def attn_b_bwd(q, k, v, tot, nblk, do, *, name):
    S = q.shape[0]
    nq = S // ATT_Q

    def body(q_ref, k_ref, v_ref, t_ref, n_ref, do_ref, dq_ref, dk_ref, dv_ref):
        hp = pl.program_id(0)
        qb = pl.program_id(1)
        first = jnp.clip(qb + 1 - n_ref[hp, qb].astype(jnp.int32), 0, qb + 1)
        tri_le = _tri("le")
        tri_lt = _tri("lt")

        @pl.when(qb == 0)
        def _():
            dk_ref[...] = jnp.zeros_like(dk_ref)
            dv_ref[...] = jnp.zeros_like(dv_ref)

        def block(kb, carry, mask):
            ks = pl.multiple_of(kb * ATT_Q, ATT_Q)
            new = []
            for h in range(2):
                lanes = slice(h * HEAD_DIM, (h + 1) * HEAD_DIM)
                cl, cg, dq = carry[h]
                qh = q_ref[:, lanes]
                doh = do_ref[:, lanes]
                kh = k_ref[pl.ds(ks, ATT_Q), lanes]
                vh = v_ref[pl.ds(ks, ATT_Q), lanes]
                totl = t_ref[:, h * HEAD_DIM:h * HEAD_DIM + 1]
                z = _dot(qh, kh, 1, 1)
                keep, take = _log_sigmoids(z, mask)
                sig = jnp.exp(take)
                w = sig * jnp.exp((totl - cl) - _cum(keep, tri_le))
                if mask is not None:
                    w = jnp.where(mask, w, 0.0)
                g = w * _dot(doh, vh, 1, 1)
                G = _dot(g.astype(BF16), tri_lt, 1, 0) + cg
                dz = g * (1.0 - sig) - sig * G
                if mask is not None:
                    dz = jnp.where(mask, dz, 0.0)
                dz = dz.astype(BF16)
                dq = dq + _dot(dz, kh, 1, 0)
                dk_ref[pl.ds(ks, ATT_Q), lanes] += _dot(dz, qh, 0, 0)
                dv_ref[pl.ds(ks, ATT_Q), lanes] += _dot(w.astype(BF16), doh, 0, 0)
                cl = cl + jnp.sum(keep, axis=-1, keepdims=True)
                cg = cg + jnp.sum(g, axis=-1, keepdims=True)
                new.append((cl, cg, dq))
            return tuple(new)

        init = tuple((jnp.zeros((ATT_Q, 1), F32), jnp.zeros((ATT_Q, 1), F32), jnp.zeros((ATT_Q, HEAD_DIM), F32))
                     for _ in range(2))
        res = lax.fori_loop(jnp.minimum(first, qb), qb, lambda kb, carry: block(kb, carry, None), init)
        res = block(qb, res, _strictly_before())
        dq_ref[...] = (jnp.concatenate([res[0][2], res[1][2]], axis=1) * QK_SCALE).astype(BF16)

    qspec = pl.BlockSpec((ATT_Q, 2 * HEAD_DIM), lambda hp, qb: (qb, hp))
    kvspec = pl.BlockSpec((S, 2 * HEAD_DIM), lambda hp, qb: (0, hp))
    return pl.pallas_call(
        body,
        out_shape=(jax.ShapeDtypeStruct((S, WIDTH), BF16), jax.ShapeDtypeStruct((S, WIDTH), F32),
                   jax.ShapeDtypeStruct((S, WIDTH), F32)),
        grid=(N_HEADS // 2, nq),
        in_specs=[qspec, kvspec, kvspec, qspec, pl.BlockSpec(memory_space=pltpu.SMEM), qspec],
        out_specs=(qspec, kvspec, kvspec),
        compiler_params=_cp(("parallel", "arbitrary")),
        name=name,
    )(q, k, v, tot, nblk, do)


U_COLBLK = 6


def _pool_counts(t0, rows):
    t = t0 + lax.broadcasted_iota(jnp.int32, (rows, WIDTH), 0)
    lane_grp = lax.broadcasted_iota(jnp.int32, (rows, WIDTH), 1) // GROUP_DIM
    win = jnp.where(lane_grp == 0, 2, jnp.where(lane_grp == 1, 4, jnp.where(lane_grp == 2, 8, 16)))
    cnt = jnp.minimum(t + 1, win)
    return 1.0 / cnt.astype(F32), lane_grp


def _window_sums(ext, shift_fn):
    s2 = ext + shift_fn(ext, 1)
    s4 = s2 + shift_fn(s2, 2)
    s8 = s4 + shift_fn(s4, 4)
    s16 = s8 + shift_fn(s8, 8)
    return s2, s4, s8, s16


def _select_group(lane_grp, s2, s4, s8, s16):
    return jnp.where(lane_grp == 0, s2, jnp.where(lane_grp == 1, s4, jnp.where(lane_grp == 2, s8, s16)))


def _pooled_tile(u_ref, h_ref, i, T):
    halo = jnp.where(i > 0, h_ref[...], 0.0)
    ext = jnp.concatenate([halo, u_ref[...]], axis=0)
    n = T + HALO
    sums = _window_sums(ext, lambda v, k: pltpu.roll(v, k, 0))
    inv, lane_grp = _pool_counts(i * T - HALO, n)
    pooled = _select_group(lane_grp, *sums) * inv - ext
    return pooled[HALO:, :]


def pool_fwd(proj, w_pool, scale, *, name):
    S = proj.shape[0]
    T = _tile(S)
    hb = T // HALO

    def body(u_ref, h_ref, w_ref, s_ref, o_ref):
        i = pl.program_id(0)
        pooled = _pooled_tile(u_ref, h_ref, i, T).astype(BF16)
        outs = [_dot(pooled[:, g * GROUP_DIM:(g + 1) * GROUP_DIM], w_ref[g], 1, 0) for g in range(4)]
        o_ref[...] = (jnp.concatenate(outs, axis=1) * s_ref[...]).astype(BF16)

    return pl.pallas_call(
        body,
        out_shape=jax.ShapeDtypeStruct((S, WIDTH), BF16),
        grid=(S // T,),
        in_specs=[pl.BlockSpec((T, WIDTH), lambda i: (i, U_COLBLK)),
                  pl.BlockSpec((HALO, WIDTH), lambda i: (jnp.maximum(i * hb - 1, 0), U_COLBLK)),
                  pl.BlockSpec((4, GROUP_DIM, GROUP_DIM), lambda i: (0, 0, 0)),
                  pl.BlockSpec((1, WIDTH), lambda i: (0, 0))],
        out_specs=pl.BlockSpec((T, WIDTH), lambda i: (i, 0)),
        compiler_params=_cp(("parallel",)),
        name=name,
    )(proj, proj, w_pool, scale)


def pool_bwd(proj, w_pool, scale, do, *, name):
    S = proj.shape[0]
    T = _tile(S)
    hb = T // HALO
    nt = S // T

    def body(u_ref, h_ref, w_ref, s_ref, do_ref, dof_ref, du_ref, dw_ref, ds_ref):
        i = pl.program_id(0)

        @pl.when(i == 0)
        def _():
            dw_ref[...] = jnp.zeros_like(dw_ref)
            ds_ref[...] = jnp.zeros_like(ds_ref)

        pooled = _pooled_tile(u_ref, h_ref, i, T).astype(BF16)
        dov = do_ref[...].astype(F32)
        fut = jnp.where(i < nt - 1, dof_ref[...].astype(F32), 0.0)
        dmix = (jnp.concatenate([dov, fut], axis=0) * s_ref[...]).astype(BF16)
        mixed, dpool = [], []
        for g in range(4):
            lanes = slice(g * GROUP_DIM, (g + 1) * GROUP_DIM)
            mixed.append(_dot(pooled[:, lanes], w_ref[g], 1, 0))
            dw_ref[g] += _dot(pooled[:, lanes], dmix[:T, lanes], 0, 0)
            dpool.append(_dot(dmix[:, lanes], w_ref[g], 1, 1))
        ds_ref[...] += jnp.sum(dov * jnp.concatenate(mixed, axis=1), axis=0, keepdims=True)
        dp = jnp.concatenate(dpool, axis=1)
        n = T + HALO
        inv, lane_grp = _pool_counts(i * T, n)
        sums = _window_sums(dp * inv, lambda v, k: pltpu.roll(v, n - k, 0))
        du = _select_group(lane_grp, *sums) - dp
        du_ref[...] = du[:T, :].astype(BF16)

    row = pl.BlockSpec((T, WIDTH), lambda i: (i, 0))
    return pl.pallas_call(
        body,
        out_shape=(jax.ShapeDtypeStruct((S, WIDTH), BF16), jax.ShapeDtypeStruct((4, GROUP_DIM, GROUP_DIM), F32),
                   jax.ShapeDtypeStruct((1, WIDTH), F32)),
        grid=(nt,),
        in_specs=[pl.BlockSpec((T, WIDTH), lambda i: (i, U_COLBLK)),
                  pl.BlockSpec((HALO, WIDTH), lambda i: (jnp.maximum(i * hb - 1, 0), U_COLBLK)),
                  pl.BlockSpec((4, GROUP_DIM, GROUP_DIM), lambda i: (0, 0, 0)),
                  pl.BlockSpec((1, WIDTH), lambda i: (0, 0)),
                  row,
                  pl.BlockSpec((HALO, WIDTH), lambda i: (jnp.minimum((i + 1) * hb, S // HALO - 1), 0))],
        out_specs=(row, pl.BlockSpec((4, GROUP_DIM, GROUP_DIM), lambda i: (0, 0, 0)),
                   pl.BlockSpec((1, WIDTH), lambda i: (0, 0))),
        compiler_params=_cp(("arbitrary",)),
        name=name,
    )(proj, proj, w_pool, scale, do, do)


GATE_BLK0 = GATE_COL0 // WIDTH


def merge_fwd(oa, ob, oc, proj, b_gate, wa, wb, wc, *, name):
    S = oa.shape[0]
    T = _tile(S)

    def body(oa_ref, ob_ref, oc_ref, ga, gb, gc, ba, bb, bc, wa_ref, wb_ref, wc_ref, m_ref):
        acc = None
        for o_ref, g_ref, b_ref, w_ref in ((oa_ref, ga, ba, wa_ref), (ob_ref, gb, bb, wb_ref), (oc_ref, gc, bc, wc_ref)):
            y = _dot(o_ref[...], w_ref[...], 1, 0)
            t = jax.nn.sigmoid(g_ref[...] + b_ref[...]) * y
            acc = t if acc is None else acc + t
        m_ref[...] = acc.astype(BF16)

    row = pl.BlockSpec((T, WIDTH), lambda i, n: (i, 0))
    gate = lambda b: pl.BlockSpec((T, WIDTH), lambda i, n, b=b: (i, GATE_BLK0 + 2 * b + n))
    bias = lambda b: pl.BlockSpec((1, WIDTH), lambda i, n, b=b: (0, 2 * b + n))
    wspec = pl.BlockSpec((WIDTH, WIDTH), lambda i, n: (0, n))
    return pl.pallas_call(
        body,
        out_shape=jax.ShapeDtypeStruct((S, D_MODEL), BF16),
        grid=(S // T, 2),
        in_specs=[row, row, row, gate(0), gate(1), gate(2), bias(0), bias(1), bias(2), wspec, wspec, wspec],
        out_specs=pl.BlockSpec((T, WIDTH), lambda i, n: (i, n)),
        compiler_params=_cp(("parallel", "parallel")),
        name=name,
    )(oa, ob, oc, proj, proj, proj, b_gate, b_gate, b_gate, wa, wb, wc)


def merge_bwd(dm, oa, ob, oc, proj, b_gate, wa, wb, wc, *, name):
    S = oa.shape[0]
    T = _tile(S)

    def body(dm_ref, oa_ref, ob_ref, oc_ref, ga, gb, gc, ba, bb, bc, wa_ref, wb_ref, wc_ref,
             ta, tb, tc, dga, dgb, dgc, dba, dbb, dbc):
        i = pl.program_id(1)
        dmv = dm_ref[...].astype(F32)
        for o_ref, g_ref, b_ref, w_ref, t_ref, dg_ref, db_ref in (
                (oa_ref, ga, ba, wa_ref, ta, dga, dba), (ob_ref, gb, bb, wb_ref, tb, dgb, dbb),
                (oc_ref, gc, bc, wc_ref, tc, dgc, dbc)):
            y = _dot(o_ref[...], w_ref[...], 1, 0)
            gate = jax.nn.sigmoid(g_ref[...] + b_ref[...])
            t_ref[...] = (gate * dmv).astype(BF16)
            dgl = dmv * y * gate * (1.0 - gate)
            dg_ref[...] = dgl.astype(BF16)

            @pl.when(i == 0)
            def _():
                db_ref[...] = jnp.zeros_like(db_ref)

            db_ref[...] += jnp.sum(dgl, axis=0, keepdims=True)

    row = pl.BlockSpec((T, WIDTH), lambda n, i: (i, 0))
    half = pl.BlockSpec((T, WIDTH), lambda n, i: (i, n))
    gate = lambda b: pl.BlockSpec((T, WIDTH), lambda n, i, b=b: (i, GATE_BLK0 + 2 * b + n))
    bias = lambda b: pl.BlockSpec((1, WIDTH), lambda n, i, b=b: (0, 2 * b + n))
    wspec = pl.BlockSpec((WIDTH, WIDTH), lambda n, i: (0, n))
    bvec = pl.BlockSpec((1, WIDTH), lambda n, i: (0, n))
    act = jax.ShapeDtypeStruct((S, D_MODEL), BF16)
    vec = jax.ShapeDtypeStruct((1, D_MODEL), F32)
    return pl.pallas_call(
        body,
        out_shape=(act, act, act, act, act, act, vec, vec, vec),
        grid=(2, S // T),
        in_specs=[half, row, row, row, gate(0), gate(1), gate(2), bias(0), bias(1), bias(2), wspec, wspec, wspec],
        out_specs=(half, half, half, half, half, half, bvec, bvec, bvec),
        compiler_params=_cp(("parallel", "arbitrary")),
        name=name,
    )(dm, oa, ob, oc, proj, proj, proj, b_gate, b_gate, b_gate, wa, wb, wc)


FF_T = 256
FF_BLKS = D_FF // FF_T


def _silu_parts(x):
    s = jax.nn.sigmoid(x)
    return x * s, s


def _conv3(ext, w_ref, b_ref):
    return (b_ref[...] + w_ref[0:1, :] * pltpu.roll(ext, 2, 0) + w_ref[1:2, :] * pltpu.roll(ext, 1, 0)
            + w_ref[2:3, :] * ext)


def conv_glu_fwd(u, conv_w, conv_b, *, name):
    S = u.shape[0]
    T = _tile(S)
    hb = T // CONV_HALO

    def body(ug, ugh, uv, uvh, wg, wv, bg, bv, a_ref):
        i = pl.program_id(1)
        cs = []
        for m_ref, h_ref, w_ref, b_ref in ((ug, ugh, wg, bg), (uv, uvh, wv, bv)):
            halo = jnp.where(i > 0, h_ref[...], 0.0)
            ext = jnp.concatenate([halo, m_ref[...]], axis=0)
            cs.append(_conv3(ext, w_ref, b_ref)[CONV_HALO:, :])
        act, _ = _silu_parts(cs[0])
        a_ref[...] = (act * cs[1]).astype(BF16)

    main = lambda o: pl.BlockSpec((T, FF_T), lambda c, i, o=o: (i, c + o))
    halo = lambda o: pl.BlockSpec((CONV_HALO, FF_T), lambda c, i, o=o: (jnp.maximum(i * hb - 1, 0), c + o))
    wsp = lambda o: pl.BlockSpec((3, FF_T), lambda c, i, o=o: (0, c + o))
    bsp = lambda o: pl.BlockSpec((1, FF_T), lambda c, i, o=o: (0, c + o))
    return pl.pallas_call(
        body,
        out_shape=jax.ShapeDtypeStruct((S, D_FF), BF16),
        grid=(FF_BLKS, S // T),
        in_specs=[main(0), halo(0), main(FF_BLKS), halo(FF_BLKS), wsp(0), wsp(FF_BLKS), bsp(0), bsp(FF_BLKS)],
        out_specs=pl.BlockSpec((T, FF_T), lambda c, i: (i, c)),
        compiler_params=_cp(("parallel", "parallel")),
        name=name,
    )(u, u, u, u, conv_w, conv_w, conv_b, conv_b)


def conv_glu_bwd(u, conv_w, conv_b, da, *, name):
    S = u.shape[0]
    T = _tile(S)
    hb = T // CONV_HALO
    nt = S // T
    n = T + 2 * CONV_HALO

    def body(ug, ugp, ugf, uv, uvp, uvf, wg, wv, bg, bv, da_ref, daf_ref,
             dug, duv, dwg, dwv, dbg, dbv):
        i = pl.program_id(1)
        first, last = i == 0, i == nt - 1
        exts, cs = [], []
        for m_ref, p_ref, f_ref, w_ref, b_ref in ((ug, ugp, ugf, wg, bg), (uv, uvp, uvf, wv, bv)):
            ext = jnp.concatenate([jnp.where(first, 0.0, p_ref[...]), m_ref[...], jnp.where(last, 0.0, f_ref[...])], axis=0)
            exts.append(ext)
            cs.append(_conv3(ext, w_ref, b_ref))
        dae = jnp.concatenate([jnp.zeros((CONV_HALO, FF_T), F32), da_ref[...].astype(F32),
                               jnp.where(last, 0.0, daf_ref[...].astype(F32))], axis=0)
        act, sg = _silu_parts(cs[0])
        dcs = (dae * cs[1] * (sg * (1.0 + cs[0] * (1.0 - sg))), dae * act)
        main = slice(CONV_HALO, CONV_HALO + T)
        for ext, dc, w_ref, du_ref, dw_ref, db_ref in ((exts[0], dcs[0], wg, dug, dwg, dbg),
                                                       (exts[1], dcs[1], wv, duv, dwv, dbv)):
            du = (w_ref[2:3, :] * dc + w_ref[1:2, :] * pltpu.roll(dc, n - 1, 0) + w_ref[0:1, :] * pltpu.roll(dc, n - 2, 0))
            du_ref[...] = du[main, :].astype(BF16)
            dcm = dc[main, :]
            rows = [jnp.sum(dcm * pltpu.roll(ext, 2 - j, 0)[main, :], axis=0, keepdims=True) if j < 2
                    else jnp.sum(dcm * ext[main, :], axis=0, keepdims=True) for j in range(3)]

            @pl.when(first)
            def _():
                dw_ref[...] = jnp.zeros_like(dw_ref)
                db_ref[...] = jnp.zeros_like(db_ref)

            dw_ref[...] += jnp.concatenate(rows, axis=0)
            db_ref[...] += jnp.sum(dcm, axis=0, keepdims=True)

    main = lambda o: pl.BlockSpec((T, FF_T), lambda c, i, o=o: (i, c + o))
    past = lambda o: pl.BlockSpec((CONV_HALO, FF_T), lambda c, i, o=o: (jnp.maximum(i * hb - 1, 0), c + o))
    fut = lambda o: pl.BlockSpec((CONV_HALO, FF_T), lambda c, i, o=o: (jnp.minimum((i + 1) * hb, S // CONV_HALO - 1), c + o))
    wsp = lambda o: pl.BlockSpec((3, FF_T), lambda c, i, o=o: (0, c + o))
    bsp = lambda o: pl.BlockSpec((1, FF_T), lambda c, i, o=o: (0, c + o))
    return pl.pallas_call(
        body,
        out_shape=(jax.ShapeDtypeStruct((S, D_FF), BF16), jax.ShapeDtypeStruct((S, D_FF), BF16),
                   jax.ShapeDtypeStruct((3, D_FF), F32), jax.ShapeDtypeStruct((3, D_FF), F32),
                   jax.ShapeDtypeStruct((1, D_FF), F32), jax.ShapeDtypeStruct((1, D_FF), F32)),
        grid=(FF_BLKS, nt),
        in_specs=[main(0), past(0), fut(0), main(FF_BLKS), past(FF_BLKS), fut(FF_BLKS),
                  wsp(0), wsp(FF_BLKS), bsp(0), bsp(FF_BLKS), main(0), fut(0)],
        out_specs=(main(0), main(0), wsp(0), wsp(0), bsp(0), bsp(0)),
        compiler_params=_cp(("parallel", "arbitrary")),
        name=name,
    )(u, u, u, u, u, u, conv_w, conv_w, conv_b, conv_b, da, da)


def loss_head(y, target, *, name):
    S, D = y.shape
    T = _tile(S)

    def body(y_ref, t_ref, dy_ref, l_ref):
        i = pl.program_id(0)
        err = y_ref[...] - t_ref[...]
        dy_ref[...] = err * (1.0 / D)

        @pl.when(i == 0)
        def _():
            l_ref[...] = jnp.zeros_like(l_ref)

        l_ref[...] += 0.5 * jnp.sum(jnp.mean(err * err, axis=-1, keepdims=True))

    row = pl.BlockSpec((T, D), lambda i: (i, 0))
    return pl.pallas_call(
        body,
        out_shape=(jax.ShapeDtypeStruct((S, D), F32), jax.ShapeDtypeStruct((8, 128), F32)),
        grid=(S // T,),
        in_specs=[row, row],
        out_specs=(row, pl.BlockSpec((8, 128), lambda i: (0, 0))),
        compiler_params=_cp(("arbitrary",)),
        name=name,
    )(y, target)


ELEMS_PER_BLOCK = 256 * 1024


def _rows_tile(rows, cols):
    if rows * cols <= ELEMS_PER_BLOCK or rows % 8:
        return rows
    best = 8
    for tr in range(8, rows + 1, 8):
        if rows % tr == 0 and tr * cols <= ELEMS_PER_BLOCK:
            best = tr
    return best


def _adamw_math(g, w_ref, m_ref, v_ref, g_out, d_out, m_out, v_out):
    mn = ADAM_B1 * m_ref[...] + (1.0 - ADAM_B1) * g
    vn = ADAM_B2 * v_ref[...] + (1.0 - ADAM_B2) * (g * g)
    m_hat = mn / (1.0 - ADAM_B1 ** ADAM_STEP)
    v_hat = vn / (1.0 - ADAM_B2 ** ADAM_STEP)
    g_out[...] = g
    d_out[...] = -ADAM_LR * (m_hat / (jnp.sqrt(v_hat) + ADAM_EPS) + ADAM_WD * w_ref[...])
    m_out[...] = mn
    v_out[...] = vn


def adamw(w, m, v, g, *, name):
    rows, cols = w.shape
    tr = _rows_tile(rows, cols)

    def body(w_ref, m_ref, v_ref, g_ref, g_out, d_out, m_out, v_out):
        _adamw_math(g_ref[...], w_ref, m_ref, v_ref, g_out, d_out, m_out, v_out)

    spec = pl.BlockSpec((tr, cols), lambda i: (i, 0))
    shp = jax.ShapeDtypeStruct((rows, cols), F32)
    return pl.pallas_call(
        body,
        out_shape=(shp, shp, shp, shp),
        grid=(rows // tr,),
        in_specs=[spec] * 4,
        out_specs=(spec, spec, spec, spec),
        compiler_params=_cp(("parallel",)),
        name=name,
    )(w, m, v, g)


ANY = pl.BlockSpec(memory_space=pl.ANY)
STAGE_BYTES = 2 * 1024 * 1024


def _mesh_pos():
    return lax.axis_index("x"), lax.axis_index("y"), lax.axis_index("c")


def _chip_peers(x, y):
    return [(1 - x, y), (x, 1 - y), (1 - x, 1 - y)]


def _all_peers(x, y, c):
    return [((1 - x) if (r >> 2) & 1 else x, (1 - y) if (r >> 1) & 1 else y, (1 - c) if r & 1 else c)
            for r in range(1, 8)]


def _shard_slice(ref, axis, j, size, layer=None):
    idx = [slice(None)] * 3
    idx[axis] = pl.ds(pl.multiple_of(j * size, 128 if axis == 2 else 16), size)
    if layer is not None:
        idx[0] = pl.ds(layer, 1)
    return ref.at[tuple(idx)]


class LayerGather:
    def __init__(self, shards, axes, layer):
        self.nt = len(shards)
        self.axes = list(axes)
        self.layer = layer
        self.shapes = [s.shape for s in shards]
        self.dtypes = [s.dtype for s in shards]
        self.sizes = [s.shape[a] for s, a in zip(shards, axes)]
        self.split = [s.shape[1] % 32 == 0 for s in shards]
        self.half_rows = [s.shape[1] // 2 if sp else s.shape[1] for s, sp in zip(shards, self.split)]
        self.chunk_rows = []
        for s in shards:
            rt = s.shape[1]
            while rt % 32 == 0 and rt * s.shape[2] * s.dtype.itemsize > STAGE_BYTES:
                rt //= 2
            self.chunk_rows.append(rt)

    def full_shapes(self):
        out = []
        for shp, a, sz, dt in zip(self.shapes, self.axes, self.sizes, self.dtypes):
            shp = list(shp)
            shp[a] = 4 * sz
            out.append(jax.ShapeDtypeStruct(tuple(shp), dt))
        return out

    def scratch_shapes(self):
        return ([pltpu.VMEM((1, rt, shp[2]), dt) for shp, rt, dt in zip(self.shapes, self.chunk_rows, self.dtypes)]
                + [pltpu.SemaphoreType.DMA((2,))] + [pltpu.SemaphoreType.DMA((3 * self.nt,)) for _ in range(4)])

    def _views(self, ins, outs, scratch):
        nt = self.nt
        stage, stage_sems = scratch[:nt], scratch[nt]
        ici_send, ici_recv, d2d_send, d2d_recv = scratch[nt + 1:]
        x, y, c = _mesh_pos()
        mine = 2 * x + y
        peers = _chip_peers(x, y)
        layer = pl.ds(self.layer, 1)

        def rows(t, half, r0=0, n=None):
            hr = self.half_rows[t]
            if n is None:
                return pl.ds(pl.multiple_of(half * hr, 16), hr) if self.split[t] else pl.ds(0, hr)
            return pl.ds(r0, n)

        def placed(t, blk, row_sel, row_len):
            sz = self.sizes[t]
            if self.axes[t] == 2:
                return outs[t].at[layer, row_sel, pl.ds(pl.multiple_of(blk * sz, 128), sz)]
            return outs[t].at[layer, pl.ds(pl.multiple_of(blk * sz, 16) + row_sel.start, row_len), :]

        def ici(t, k, blk):
            px, py = peers[k]
            sel = rows(t, c)
            return pltpu.make_async_remote_copy(
                src_ref=ins[t].at[layer, sel, :], dst_ref=placed(t, blk, sel, self.half_rows[t]),
                send_sem=ici_send.at[3 * t + k], recv_sem=ici_recv.at[3 * t + k],
                device_id=(px, py, c), device_id_type=MESH_T)

        def d2d(t, k, half):
            px, py = peers[k]
            piece = placed(t, 2 * px + py, rows(t, half), self.half_rows[t])
            return pltpu.make_async_remote_copy(
                src_ref=piece, dst_ref=piece, send_sem=d2d_send.at[3 * t + k], recv_sem=d2d_recv.at[3 * t + k],
                device_id=(x, y, 1 - c), device_id_type=MESH_T)

        def own_chunk(t, r0):
            rt = self.chunk_rows[t]
            sel = pl.ds(r0, rt)
            return ins[t].at[layer, sel, :], placed(t, mine, sel, rt), stage[t], stage_sems

        return c, mine, peers, ici, d2d, own_chunk

    def start(self, ins, outs, scratch):
        c, mine, peers, ici, d2d, own_chunk = self._views(ins, outs, scratch)
        for t in range(self.nt):
            for k in range(3):
                ici(t, k, mine).start()
        for t in range(self.nt):
            for r0 in range(0, self.shapes[t][1], self.chunk_rows[t]):
                src, dst, buf, sems = own_chunk(t, r0)
                load = pltpu.make_async_copy(src, buf, sems.at[0])
                load.start()
                load.wait()
                store = pltpu.make_async_copy(buf, dst, sems.at[1])
                store.start()
                store.wait()

    def finish(self, ins, outs, scratch):
        c, mine, peers, ici, d2d, own_chunk = self._views(ins, outs, scratch)
        for t in range(self.nt):
            for k, (px, py) in enumerate(peers):
                ici(t, k, 2 * px + py).wait_recv()
                if self.split[t]:
                    d2d(t, k, c).start()
        for t in range(self.nt):
            for k in range(3):
                if self.split[t]:
                    d2d(t, k, 1 - c).wait_recv()
        for t in range(self.nt):
            for k in range(3):
                ici(t, k, mine).wait_send()
                if self.split[t]:
                    d2d(t, k, c).wait_send()


def all_gather_layer(shards, axes, layer, *, name):
    plan = LayerGather(shards, axes, layer)
    nt = plan.nt

    def body(*refs):
        ins, outs, scratch = refs[:nt], refs[nt:2 * nt], refs[2 * nt:]
        plan.start(ins, outs, scratch)
        plan.finish(ins, outs, scratch)

    return pl.pallas_call(
        body,
        out_shape=tuple(plan.full_shapes()),
        in_specs=[ANY] * nt,
        out_specs=tuple([ANY] * nt),
        scratch_shapes=plan.scratch_shapes(),
        name=name,
    )(*shards)


def sibling_layer_exchange(grads, *, name):
    nt = len(grads)

    def body(*refs):
        ins, outs = refs[:nt], refs[nt:2 * nt]
        send_sems, recv_sems = refs[2 * nt:]
        x, y, c = _mesh_pos()
        cps = []
        for t in range(nt):
            cp = pltpu.make_async_remote_copy(src_ref=ins[t].at[pl.ds(1 - c, 1)], dst_ref=outs[t],
                                              send_sem=send_sems.at[t], recv_sem=recv_sems.at[t],
                                              device_id=(x, y, 1 - c), device_id_type=MESH_T)
            cp.start()
            cps.append(cp)
        for cp in cps:
            cp.wait_recv()
        for cp in cps:
            cp.wait_send()

    return pl.pallas_call(
        body,
        out_shape=tuple(jax.ShapeDtypeStruct((1,) + g.shape[1:], F32) for g in grads),
        in_specs=[ANY] * nt,
        out_specs=tuple([ANY] * nt),
        scratch_shapes=[pltpu.SemaphoreType.DMA((nt,)), pltpu.SemaphoreType.DMA((nt,))],
        name=name,
    )(*grads)


def pair_sum(grad, other, layer, *, name):
    _, R, C = grad.shape
    tr = _rows_tile(R, C)

    def body(l_ref, g_ref, o_ref, s32_ref, s16_ref):
        s = g_ref[...] + o_ref[...]
        s32_ref[...] = s
        s16_ref[...] = s.astype(BF16)

    blk = pl.BlockSpec((1, tr, C), lambda i, l: (0, i, 0))
    return pl.pallas_call(
        body,
        out_shape=(jax.ShapeDtypeStruct((1, R, C), F32), jax.ShapeDtypeStruct((1, R, C), BF16)),
        grid_spec=pltpu.PrefetchScalarGridSpec(
            num_scalar_prefetch=1, grid=(R // tr,),
            in_specs=[pl.BlockSpec((1, tr, C), lambda i, l: (l[0], i, 0)), blk],
            out_specs=(blk, blk)),
        compiler_params=_cp(("parallel",)),
        name=name,
    )(layer, grad, other)


def scatter_pairs(pairs16, axes, small, *, name):
    nt = len(pairs16)
    sizes = [g.shape[a] // 4 for g, a in zip(pairs16, axes)]

    def body(*refs):
        in16, small_in = refs[:nt], refs[nt]
        recv, small_out = refs[nt + 1:2 * nt + 1], refs[2 * nt + 1]
        send_sems, recv_sems, ssend, srecv = refs[2 * nt + 2:]
        x, y, c = _mesh_pos()
        me = 4 * x + 2 * y + c
        sends, recvs = [], []
        for t in range(nt):
            for k, (px, py) in enumerate(_chip_peers(x, y)):
                snd = pltpu.make_async_remote_copy(
                    src_ref=_shard_slice(in16[t], axes[t], 2 * px + py, sizes[t]), dst_ref=recv[t].at[k],
                    send_sem=send_sems.at[3 * t + k], recv_sem=recv_sems.at[3 * t + k],
                    device_id=(px, py, c), device_id_type=MESH_T)
                snd.start()
                sends.append(snd)
                recvs.append(snd)
        for r, (px, py, pc) in enumerate(_all_peers(x, y, c)):
            def mk(slot, r=r, px=px, py=py, pc=pc):
                return pltpu.make_async_remote_copy(
                    src_ref=small_in, dst_ref=small_out.at[slot], send_sem=ssend.at[r], recv_sem=srecv.at[r],
                    device_id=(px, py, pc), device_id_type=MESH_T)
            snd = mk(me)
            snd.start()
            sends.append(snd)
            recvs.append(mk(4 * px + 2 * py + pc))
        for r in recvs:
            r.wait_recv()
        for s in sends:
            s.wait_send()

    out_shape = []
    for g, a, sz in zip(pairs16, axes, sizes):
        shp = list(g.shape)
        shp[a] = sz
        out_shape.append(jax.ShapeDtypeStruct((3,) + tuple(shp), BF16))
    out_shape.append(jax.ShapeDtypeStruct((8,) + small.shape, F32))
    return pl.pallas_call(
        body,
        out_shape=tuple(out_shape),
        in_specs=[ANY] * (nt + 1),
        out_specs=tuple([ANY] * (nt + 1)),
        scratch_shapes=[pltpu.SemaphoreType.DMA((3 * nt,)), pltpu.SemaphoreType.DMA((3 * nt,)),
                        pltpu.SemaphoreType.DMA((7,)), pltpu.SemaphoreType.DMA((7,))],
        name=name,
    )(*pairs16, small)


def sum_chips(pair32, recv, axis, chip, *, name):
    _, _, r, c = recv.shape
    tr = _rows_tile(r, c)
    nr = r // tr

    def body(j_ref, p_ref, r_ref, s_ref):
        acc = p_ref[...]
        for k in range(3):
            acc = acc + r_ref[k].astype(F32)
        s_ref[...] = acc

    if axis == 2:
        own = pl.BlockSpec((1, tr, c), lambda i, j: (0, i, j[0]))
    else:
        own = pl.BlockSpec((1, tr, c), lambda i, j: (0, j[0] * nr + i, 0))
    blk = pl.BlockSpec((1, tr, c), lambda i, j: (0, i, 0))
    return pl.pallas_call(
        body,
        out_shape=jax.ShapeDtypeStruct((1, r, c), F32),
        grid_spec=pltpu.PrefetchScalarGridSpec(
            num_scalar_prefetch=1, grid=(nr,),
            in_specs=[own, pl.BlockSpec((3, 1, tr, c), lambda i, j: (0, 0, i, 0))],
            out_specs=blk),
        compiler_params=_cp(("parallel",)),
        name=name,
    )(chip, pair32, recv)


def sum_devices(gathered, own, me, *, name):
    _, R, C = gathered.shape

    def body(me_ref, g_ref, o_ref, s_ref):
        acc = None
        for k in range(8):
            part = jnp.where(me_ref[0] == k, o_ref[...], g_ref[k])
            acc = part if acc is None else acc + part
        s_ref[...] = acc

    return pl.pallas_call(
        body,
        out_shape=jax.ShapeDtypeStruct((R, C), F32),
        grid_spec=pltpu.PrefetchScalarGridSpec(
            num_scalar_prefetch=1, grid=(1,),
            in_specs=[pl.BlockSpec((8, R, C), lambda i, m: (0, 0, 0)), pl.BlockSpec((R, C), lambda i, m: (0, 0))],
            out_specs=pl.BlockSpec((R, C), lambda i, m: (0, 0))),
        compiler_params=_cp(("arbitrary",)),
        name=name,
    )(me, gathered, own)


def sibling_swap(parts, *, name):
    nt = len(parts)

    def body(*refs):
        ins, outs = refs[:nt], refs[nt:2 * nt]
        send_sems, recv_sems = refs[2 * nt:]
        x, y, c = _mesh_pos()
        cps = []
        for t in range(nt):
            cp = pltpu.make_async_remote_copy(src_ref=ins[t], dst_ref=outs[t], send_sem=send_sems.at[t],
                                              recv_sem=recv_sems.at[t], device_id=(x, y, 1 - c), device_id_type=MESH_T)
            cp.start()
            cps.append(cp)
        for cp in cps:
            cp.wait_recv()
        for cp in cps:
            cp.wait_send()

    return pl.pallas_call(
        body,
        out_shape=tuple(jax.ShapeDtypeStruct(p.shape, p.dtype) for p in parts),
        in_specs=[ANY] * nt,
        out_specs=tuple([ANY] * nt),
        scratch_shapes=[pltpu.SemaphoreType.DMA((nt,)), pltpu.SemaphoreType.DMA((nt,))],
        name=name,
    )(*parts)


def adamw_layers(w, m, v, g_mine, g_other, layer, *, name):
    _, r, c = w.shape
    tr = _rows_tile(r, c)

    def body(l_ref, w_ref, m_ref, v_ref, a_ref, b_ref, g_out, d_out, m_out, v_out):
        g = jnp.where(pl.program_id(0) == l_ref[0], a_ref[...], b_ref[...])
        _adamw_math(g, w_ref, m_ref, v_ref, g_out, d_out, m_out, v_out)

    full = pl.BlockSpec((1, tr, c), lambda l, i, s: (l, i, 0))
    half = pl.BlockSpec((1, tr, c), lambda l, i, s: (0, i, 0))
    shp = jax.ShapeDtypeStruct((2, r, c), F32)
    return pl.pallas_call(
        body,
        out_shape=(shp, shp, shp, shp),
        grid_spec=pltpu.PrefetchScalarGridSpec(
            num_scalar_prefetch=1, grid=(2, r // tr),
            in_specs=[full, full, full, half, half],
            out_specs=(full, full, full, full)),
        compiler_params=_cp(("parallel", "parallel")),
        name=name,
    )(layer, w, m, v, g_mine, g_other)


WEIGHTS = ("norm_mix", "w_in", "b_gate", "q_norm_a", "k_norm_a", "rel_bias_a", "w_pool", "pool_scale",
           "w_branch_a", "w_branch_b", "w_branch_c", "w_out", "norm_ffn", "w_up", "conv_w", "conv_b", "w_down")
SHARDED = {"w_in": 2, "w_branch_a": 2, "w_branch_b": 2, "w_branch_c": 2, "w_out": 1, "w_up": 2, "conv_w": 2,
           "w_down": 1}
REPLICATED = tuple(n for n in WEIGHTS if n not in SHARDED)
MATMUL_WEIGHTS = tuple(n for n in SHARDED if n != "conv_w")
SMALL_ROWS = 1232


def _layer_fwd(x, p, tables, prefetch=None):
    full = p["full"]
    diag = exact_dot(p["rel_bias_a"], tables["onehot_t"], name="bias_diagonals")
    diag = diag.reshape(N_HEADS, N_VARIANTS, 1, DIAG_W).transpose(1, 0, 2, 3)
    biasm = bias_expand(diag, name="bias_expand")
    gq8 = jnp.tile(p["q_norm_a"], N_HEADS)[None]
    gk8 = jnp.tile(p["k_norm_a"], N_HEADS)[None]
    h = rmsnorm_fwd(x, p["norm_mix"][None], name="rmsnorm_fwd")
    proj = matmul(h, full["w_in"], b_layer=p["l"], name="mm_in")
    qa, ka, va, qb, kb, vb = qkv_prep(proj, gq8, gk8, name="qkv_prep")
    oa = attn_a_fwd(qa, ka, va, biasm, name="attn_a_fwd")
    if prefetch is None:
        ob, tot, nblk = attn_b_fwd(qb, kb, vb, name="attn_b_fwd")
    else:
        plan, shards, names = prefetch
        ob, tot, nblk, filled = attn_b_fwd(qb, kb, vb, gather=(plan, shards, [full[n] for n in names]),
                                           name="attn_b_fwd_gather")
        full = dict(zip(names, filled))
    wpool = p["w_pool"].astype(BF16)
    oc = pool_fwd(proj, wpool, p["pool_scale"][None], name="pool_fwd")
    merged = merge_fwd(oa, ob, oc, proj, p["b_gate"][None], p["w_branch_a"], p["w_branch_b"], p["w_branch_c"],
                       name="merge_fwd")
    x1 = matmul(merged, full["w_out"], b_layer=p["l"], add=x, name="mm_out")
    h2 = rmsnorm_fwd(x1, p["norm_ffn"][None], name="rmsnorm_fwd")
    u = matmul(h2, full["w_up"], b_layer=p["l"], name="mm_up")
    a = conv_glu_fwd(u, p["conv_w"], p["conv_b"][None], name="conv_glu_fwd")
    x2 = matmul(a, full["w_down"], b_layer=p["l"], add=x1, name="mm_down")
    saved = dict(x=x, h=h, proj=proj, qa=qa, ka=ka, va=va, qb=qb, kb=kb, vb=vb, oa=oa, ob=ob, tot=tot, nblk=nblk, oc=oc,
                 merged=merged, x1=x1, h2=h2, u=u, a=a, biasm=biasm, gq8=gq8, gk8=gk8, wpool=wpool)
    return x2, saved, full


def _layer_bwd(dx2, s, p, tables, stacks):
    g = {}
    full, l = p["full"], p["l"]
    into = lambda n: dict(out_layer=l, out_stack=stacks.get(n))
    da = matmul(dx2, full["w_down"], b_layer=l, tb=True, name="mm_down_dx")
    g["w_down"] = matmul(s["a"], dx2, ta=True, name="mm_down_dw", **into("w_down"))
    dug, duv, dcwg, dcwv, dcbg, dcbv = conv_glu_bwd(s["u"], p["conv_w"], p["conv_b"][None], da, name="conv_glu_bwd")
    du = jnp.concatenate([dug, duv], axis=1)
    g["conv_w"] = jnp.concatenate([dcwg, dcwv], axis=1)
    g["conv_b"] = jnp.concatenate([dcbg, dcbv], axis=1)[0]
    g["w_up"] = matmul(s["h2"], du, ta=True, name="mm_up_dw", **into("w_up"))
    dh2 = matmul(du, full["w_up"], b_layer=l, tb=True, name="mm_up_dx")
    dx1, dg2 = rmsnorm_bwd(s["x1"], p["norm_ffn"][None], dh2, dx2, name="rmsnorm_bwd")
    g["norm_ffn"] = dg2[0]
    dmerged = matmul(dx1, full["w_out"], b_layer=l, tb=True, name="mm_out_dx")
    g["w_out"] = matmul(s["merged"], dx1, ta=True, name="mm_out_dw", **into("w_out"))
    t_a, t_b, t_c, dga, dgb, dgc, dba, dbb, dbc = merge_bwd(
        dmerged, s["oa"], s["ob"], s["oc"], s["proj"], p["b_gate"][None], p["w_branch_a"], p["w_branch_b"],
        p["w_branch_c"], name="merge_bwd")
    g["b_gate"] = jnp.concatenate([dba, dbb, dbc], axis=1)[0]
    g["w_branch_a"] = matmul(s["oa"], t_a, ta=True, name="mm_branch_dw", **into("w_branch_a"))
    g["w_branch_b"] = matmul(s["ob"], t_b, ta=True, name="mm_branch_dw", **into("w_branch_b"))
    g["w_branch_c"] = matmul(s["oc"], t_c, ta=True, name="mm_branch_dw", **into("w_branch_c"))
    doa = matmul(t_a, full["w_branch_a"], b_layer=l, tb=True, out_dtype=BF16, name="mm_branch_dx")
    dob = matmul(t_b, full["w_branch_b"], b_layer=l, tb=True, out_dtype=BF16, name="mm_branch_dx")
    doc = matmul(t_c, full["w_branch_c"], b_layer=l, tb=True, name="mm_branch_dx_f32")
    dqh, dkh, dva, dbias = attn_a_bwd(s["qa"], s["ka"], s["va"], s["biasm"], doa, name="attn_a_bwd")
    ddiag = relbias_reduce(dbias, name="relbias_reduce")
    ddiag = ddiag.transpose(1, 0, 2, 3).reshape(N_HEADS, N_VARIANTS * DIAG_W)
    g["rel_bias_a"] = exact_dot(ddiag, tables["onehot"], name="relbias_table")
    dqa, dka, dgq8, dgk8 = qknorm_bwd(s["proj"], s["gq8"], s["gk8"], dqh, dkh, name="qknorm_bwd")
    g["q_norm_a"] = dgq8.reshape(N_HEADS, HEAD_DIM).sum(axis=0)
    g["k_norm_a"] = dgk8.reshape(N_HEADS, HEAD_DIM).sum(axis=0)
    dqb, dkb, dvb = attn_b_bwd(s["qb"], s["kb"], s["vb"], s["tot"], s["nblk"], dob, name="attn_b_bwd")
    duc, dwp, dsc = pool_bwd(s["proj"], s["wpool"], p["pool_scale"][None], doc, name="pool_bwd")
    g["w_pool"] = dwp
    g["pool_scale"] = dsc[0]
    dproj = jnp.concatenate([dqa, dka, dva.astype(BF16), dqb, dkb.astype(BF16), dvb.astype(BF16), duc,
                             dga, dgb, dgc], axis=1)
    g["w_in"] = matmul(s["h"], dproj, ta=True, name="mm_in_dw", **into("w_in"))
    dh = matmul(dproj, full["w_in"], b_layer=l, tb=True, name="mm_in_dx")
    dx, dg1 = rmsnorm_bwd(s["x"], p["norm_mix"][None], dh, dx1, name="rmsnorm_bwd")
    g["norm_mix"] = dg1[0]
    return dx, g


def kernel(x, norm_mix, w_in, b_gate, q_norm_a, k_norm_a, rel_bias_a, w_pool, pool_scale, w_branch_a, w_branch_b, w_branch_c, w_out, norm_ffn, w_up, conv_w, conv_b, w_down, loss_target, m_norm_mix, m_w_in, m_b_gate, m_q_norm_a, m_k_norm_a, m_rel_bias_a, m_w_pool, m_pool_scale, m_w_branch_a, m_w_branch_b, m_w_branch_c, m_w_out, m_norm_ffn, m_w_up, m_conv_w, m_conv_b, m_w_down, v_norm_mix, v_w_in, v_b_gate, v_q_norm_a, v_k_norm_a, v_rel_bias_a, v_w_pool, v_pool_scale, v_w_branch_a, v_w_branch_b, v_w_branch_c, v_w_out, v_norm_ffn, v_w_up, v_conv_w, v_conv_b, v_w_down):
    w = dict(zip(WEIGHTS, (norm_mix, w_in, b_gate, q_norm_a, k_norm_a, rel_bias_a, w_pool, pool_scale, w_branch_a,
                           w_branch_b, w_branch_c, w_out, norm_ffn, w_up, conv_w, conv_b, w_down)))
    m = dict(zip(WEIGHTS, (m_norm_mix, m_w_in, m_b_gate, m_q_norm_a, m_k_norm_a, m_rel_bias_a, m_w_pool, m_pool_scale,
                           m_w_branch_a, m_w_branch_b, m_w_branch_c, m_w_out, m_norm_ffn, m_w_up, m_conv_w, m_conv_b,
                           m_w_down)))
    v = dict(zip(WEIGHTS, (v_norm_mix, v_w_in, v_b_gate, v_q_norm_a, v_k_norm_a, v_rel_bias_a, v_w_pool, v_pool_scale,
                           v_w_branch_a, v_w_branch_b, v_w_branch_c, v_w_out, v_norm_ffn, v_w_up, v_conv_w, v_conv_b,
                           v_w_down)))
    onehot = diagonal_onehot()
    tables = dict(onehot=jnp.asarray(onehot), onehot_t=jnp.asarray(np.ascontiguousarray(onehot.T)))

    names = tuple(SHARDED)
    shards = [w[n] if n == "conv_w" else w[n].astype(BF16) for n in names]
    axes = [SHARDED[n] for n in names]
    full = dict(zip(names, all_gather_layer(shards, axes, 0, name="all_gather_layer")))

    def layer_params(l):
        p = {n: full[n][l] for n in ("w_branch_a", "w_branch_b", "w_branch_c", "conv_w")}
        p.update({n: w[n][l] for n in REPLICATED})
        p.update(full=full, l=l)
        return p

    xs = x[0]
    saved = []
    for l in range(DEPTH):
        prefetch = (LayerGather(shards, axes, l + 1), shards, names) if l + 1 < DEPTH else None
        xs, s, full = _layer_fwd(xs, layer_params(l), tables, prefetch)
        saved.append(s)
    dx, lpart = loss_head(xs, loss_target[0], name="loss_head")
    loss = lax.psum(lpart[0, 0], MESH_AXES)
    grads = [None] * DEPTH
    stacks = {}
    for l in reversed(range(DEPTH)):
        dx, grads[l] = _layer_bwd(dx, saved[l], layer_params(l), tables, stacks)
        stacks = {n: grads[l][n] for n in MATMUL_WEIGHTS}
    g = {n: stacks[n] if n in stacks else jnp.stack([grads[l][n] for l in range(DEPTH)]) for n in WEIGHTS}

    flat = jnp.concatenate([g[n].reshape(-1) for n in REPLICATED])
    small = jnp.pad(flat, (0, SMALL_ROWS * 128 - flat.shape[0])).reshape(SMALL_ROWS, 128)
    nt = len(names)
    as_index = lambda i: jnp.reshape(i, (1,)).astype(jnp.int32)
    cx, cy, cc = _mesh_pos()
    layer, chip, me = as_index(cc), as_index(2 * cx + cy), as_index(4 * cx + 2 * cy + cc)
    others = sibling_layer_exchange([g[n] for n in names], name="sibling_layer_exchange")
    pairs = [pair_sum(g[n], o, layer, name="pair_sum") for n, o in zip(names, others)]
    outs = scatter_pairs([p[1] for p in pairs], axes, small, name="scatter_pairs")
    finals = [sum_chips(p[0], recv, a, chip, name="sum_chips") for p, recv, a in zip(pairs, outs[:nt], axes)]
    finals_other = sibling_swap(finals, name="sibling_swap")
    small_sum = sum_devices(outs[-1], small, me, name="sum_devices").reshape(-1)

    res = {}
    for n, mine, other in zip(names, finals, finals_other):
        res[n] = adamw_layers(w[n], m[n], v[n], mine, other, layer, name="adamw_layers")
    off = 0
    for n in REPLICATED:
        shp = w[n].shape
        size = int(np.prod(shp))
        gn = small_sum[off:off + size]
        off += size
        cols = shp[-1]
        two_d = lambda t: t.reshape(size // cols, cols)
        res[n] = [t.reshape(shp) for t in adamw(two_d(w[n]), two_d(m[n]), two_d(v[n]), two_d(gn), name="adamw")]

    out = [loss, dx[None]]
    for k in range(4):
        out.extend(res[n][k] for n in WEIGHTS)
    return tuple(out)
```

```python
import jax
import jax.numpy as jnp
import numpy as np
from jax import lax
from jax.experimental import pallas as pl
from jax.experimental.pallas import tpu as pltpu

F32 = jnp.float32
BF16 = jnp.bfloat16

D_MODEL = 1024
DEPTH = 2
CHUNK = 64
N_LEFT = 8
HEAD_DIM = 64
N_HEADS = 8
WIDTH = 512
POOL_WINDOWS = (2, 4, 8, 16)
GROUP_DIM = 128
MAX_REL = 2 * CHUNK
REL_TABLE = MAX_REL + CHUNK
D_FF = 2816
EPS = 1e-6
QK_SCALE = 0.125
IN_COLS = 7 * WIDTH + 3 * D_MODEL
GATE_COL0 = 7 * WIDTH

ADAM_LR = 0.001
ADAM_B1 = 0.9
ADAM_B2 = 0.999
ADAM_EPS = 1e-08
ADAM_WD = 0.01
ADAM_STEP = 10

VMEM_LIMIT = 56 * 1024 * 1024
ATT_Q = 256
A_Q = 256
A_WIN = A_Q + N_LEFT * CHUNK
HALO = 16
CONV_HALO = 8
NEG = -1e30

MESH_AXES = ("x", "y", "c")
MESH_T = pl.DeviceIdType.MESH


def _cp(sem=None, vmem=VMEM_LIMIT):
    return pltpu.CompilerParams(dimension_semantics=sem, vmem_limit_bytes=vmem)


def _dot(a, b, ca, cb):
    return lax.dot_general(a, b, (((ca,), (cb,)), ((), ())), preferred_element_type=F32)


def _tile(n, cands=(512, 256, 128)):
    for c in cands:
        if n % c == 0:
            return c
    return n


def _split_hi_lo(v):
    hi = v.astype(BF16)
    lo = (v - hi.astype(F32)).astype(BF16)
    return hi, lo


def matmul(a, b, *, ta=False, tb=False, add=None, out_dtype=F32, b_layer=None, name):
    if ta:
        K, M = a.shape
    else:
        M, K = a.shape
    if tb:
        N, K2 = b.shape[-2:]
    else:
        K2, N = b.shape[-2:]
    assert K == K2, (a.shape, b.shape, ta, tb)
    big = (1024, 1408, 512, 256, 128)
    tm, tn, tk = _tile(M, big), _tile(N, big), _tile(K, big)
    nk = K // tk

    def body(*refs):
        a_ref, b_ref = refs[:2]
        r_ref = refs[2] if add is not None else None
        o_ref, acc = refs[-2:]
        k = pl.program_id(2)

        @pl.when(k == 0)
        def _():
            acc[...] = jnp.zeros_like(acc)

        av = a_ref[...].astype(BF16)
        bv = b_ref[...].astype(BF16)
        acc[...] += _dot(av, bv, 0 if ta else 1, 1 if tb else 0)

        @pl.when(k == nk - 1)
        def _():
            r = acc[...]
            if add is not None:
                r = r + r_ref[...].astype(F32)
            o_ref[...] = r.astype(out_dtype)

    a_spec = pl.BlockSpec((tk, tm), lambda i, j, k: (k, i)) if ta else pl.BlockSpec((tm, tk), lambda i, j, k: (i, k))
    if b_layer is None:
        b_spec = pl.BlockSpec((tn, tk), lambda i, j, k: (j, k)) if tb else pl.BlockSpec((tk, tn), lambda i, j, k: (k, j))
    elif tb:
        b_spec = pl.BlockSpec((None, tn, tk), lambda i, j, k: (b_layer, j, k))
    else:
        b_spec = pl.BlockSpec((None, tk, tn), lambda i, j, k: (b_layer, k, j))
    in_specs = [a_spec, b_spec]
    args = [a, b]
    if add is not None:
        in_specs.append(pl.BlockSpec((tm, tn), lambda i, j, k: (i, j)))
        args.append(add)
    return pl.pallas_call(
        body,
        out_shape=jax.ShapeDtypeStruct((M, N), out_dtype),
        grid=(M // tm, N // tn, nk),
        in_specs=in_specs,
        out_specs=pl.BlockSpec((tm, tn), lambda i, j, k: (i, j)),
        scratch_shapes=[pltpu.VMEM((tm, tn), F32)],
        compiler_params=_cp(("parallel", "parallel", "arbitrary")),
        name=name,
    )(*args)


def rmsnorm_fwd(x, g, *, name):
    S, D = x.shape
    T = _tile(S)

    def body(x_ref, g_ref, h_ref):
        xv = x_ref[...]
        r = lax.rsqrt(jnp.mean(xv * xv, axis=-1, keepdims=True) + EPS)
        h_ref[...] = (xv * r * g_ref[...]).astype(BF16)

    return pl.pallas_call(
        body,
        out_shape=jax.ShapeDtypeStruct((S, D), BF16),
        grid=(S // T,),
        in_specs=[pl.BlockSpec((T, D), lambda i: (i, 0)), pl.BlockSpec((1, D), lambda i: (0, 0))],
        out_specs=pl.BlockSpec((T, D), lambda i: (i, 0)),
        compiler_params=_cp(("parallel",)),
        name=name,
    )(x, g)


def rmsnorm_bwd(x, g, dh, dres, *, name):
    S, D = x.shape
    T = _tile(S)

    def body(x_ref, g_ref, dh_ref, dres_ref, dx_ref, dg_ref):
        i = pl.program_id(0)
        xv = x_ref[...]
        dhv = dh_ref[...].astype(F32)
        r = lax.rsqrt(jnp.mean(xv * xv, axis=-1, keepdims=True) + EPS)
        gd = dhv * g_ref[...]
        m = jnp.mean(xv * gd, axis=-1, keepdims=True)
        dx_ref[...] = dres_ref[...] + r * gd - xv * (r * r * r * m)

        @pl.when(i == 0)
        def _():
            dg_ref[...] = jnp.zeros_like(dg_ref)

        dg_ref[...] += jnp.sum(dhv * xv * r, axis=0, keepdims=True)

    row = pl.BlockSpec((T, D), lambda i: (i, 0))
    vec = pl.BlockSpec((1, D), lambda i: (0, 0))
    return pl.pallas_call(
        body,
        out_shape=(jax.ShapeDtypeStruct((S, D), F32), jax.ShapeDtypeStruct((1, D), F32)),
        grid=(S // T,),
        in_specs=[row, vec, row, row],
        out_specs=(row, vec),
        compiler_params=_cp(("arbitrary",)),
        name=name,
    )(x, g, dh, dres)


def _head_mean_matrix():
    r = lax.broadcasted_iota(jnp.int32, (WIDTH, WIDTH), 0) // HEAD_DIM
    c = lax.broadcasted_iota(jnp.int32, (WIDTH, WIDTH), 1) // HEAD_DIM
    return jnp.where(r == c, 1.0 / HEAD_DIM, 0.0).astype(BF16)


def _head_mean(v, mm):
    hi, lo = _split_hi_lo(v)
    return _dot(hi, mm, 1, 0) + _dot(lo, mm, 1, 0)


def qkv_prep(proj, gq, gk, *, name):
    S = proj.shape[0]
    T = _tile(S)

    def body(qa, ka, va, qb, kb, vb, gq_ref, gk_ref, oqa, oka, ova, oqb, okb, ovb):
        mm = _head_mean_matrix()
        for src, gref, dst, scale in ((qa, gq_ref, oqa, QK_SCALE), (ka, gk_ref, oka, 1.0)):
            v = src[...]
            r = lax.rsqrt(_head_mean(v * v, mm) + EPS)
            dst[...] = (v * r * gref[...] * scale).astype(BF16)
        oqb[...] = (qb[...] * QK_SCALE).astype(BF16)
        for src, dst in ((va, ova), (kb, okb), (vb, ovb)):
            dst[...] = src[...].astype(BF16)

    col = lambda j: pl.BlockSpec((T, WIDTH), lambda i, j=j: (i, j))
    vec = pl.BlockSpec((1, WIDTH), lambda i: (0, 0))
    out = pl.BlockSpec((T, WIDTH), lambda i: (i, 0))
    return pl.pallas_call(
        body,
        out_shape=tuple(jax.ShapeDtypeStruct((S, WIDTH), BF16) for _ in range(6)),
        grid=(S // T,),
        in_specs=[col(0), col(1), col(2), col(3), col(4), col(5), vec, vec],
        out_specs=tuple(out for _ in range(6)),
        compiler_params=_cp(("parallel",)),
        name=name,
    )(proj, proj, proj, proj, proj, proj, gq, gk)


def qknorm_bwd(proj, gq, gk, dqh, dkh, *, name):
    S = proj.shape[0]
    T = _tile(S)

    def body(qa, ka, gq_ref, gk_ref, dq_ref, dk_ref, oq, ok, ogq, ogk):
        i = pl.program_id(0)
        mm = _head_mean_matrix()

        @pl.when(i == 0)
        def _():
            ogq[...] = jnp.zeros_like(ogq)
            ogk[...] = jnp.zeros_like(ogk)

        for src, gref, dref, dst, gdst in ((qa, gq_ref, dq_ref, oq, ogq), (ka, gk_ref, dk_ref, ok, ogk)):
            v = src[...]
            dy = dref[...]
            r = lax.rsqrt(_head_mean(v * v, mm) + EPS)
            gd = dy * gref[...]
            m = _head_mean(v * gd, mm)
            dst[...] = (r * gd - v * (r * r * r * m)).astype(BF16)
            gdst[...] += jnp.sum(dy * v * r, axis=0, keepdims=True)

    col = lambda j: pl.BlockSpec((T, WIDTH), lambda i, j=j: (i, j))
    vec = pl.BlockSpec((1, WIDTH), lambda i: (0, 0))
    row = pl.BlockSpec((T, WIDTH), lambda i: (i, 0))
    return pl.pallas_call(
        body,
        out_shape=(jax.ShapeDtypeStruct((S, WIDTH), BF16), jax.ShapeDtypeStruct((S, WIDTH), BF16),
                   jax.ShapeDtypeStruct((1, WIDTH), F32), jax.ShapeDtypeStruct((1, WIDTH), F32)),
        grid=(S // T,),
        in_specs=[col(0), col(1), vec, vec, row, row],
        out_specs=(row, row, vec, vec),
        compiler_params=_cp(("arbitrary",)),
        name=name,
    )(proj, proj, gq, gk, dqh, dkh)


DIAG_W = 1024
N_VARIANTS = N_LEFT * CHUNK // A_Q + 1


def diagonal_onehot():
    jj = np.arange(DIAG_W)
    diff = np.where(jj < A_WIN, jj, jj - DIAG_W)
    out = np.zeros((N_VARIANTS, DIAG_W, REL_TABLE), np.float32)
    for v in range(N_VARIANTS):
        rel = np.clip(A_Q * v - diff, -(CHUNK - 1), MAX_REL) + (CHUNK - 1)
        out[v, jj, rel] = 1.0
    return out.reshape(N_VARIANTS * DIAG_W, REL_TABLE)


def exact_dot(a, b, *, name):
    def body(a_ref, b_ref, o_ref):
        o_ref[...] = jnp.dot(a_ref[...], b_ref[...], precision=lax.Precision.HIGHEST, preferred_element_type=F32)

    return pl.pallas_call(body, out_shape=jax.ShapeDtypeStruct((a.shape[0], b.shape[1]), F32),
                          compiler_params=_cp(), name=name)(a, b)


def _band_valid(v):
    qc = (lax.broadcasted_iota(jnp.int32, (A_Q, A_WIN), 0) + A_Q * v) // CHUNK
    kc = lax.broadcasted_iota(jnp.int32, (A_Q, A_WIN), 1) // CHUNK
    return (kc <= qc) & (kc >= qc - N_LEFT)


def bias_expand(diag, *, name):
    def body(d_ref, o_ref):
        rows = jnp.broadcast_to(d_ref[0, 0], (A_Q, DIAG_W))
        skew = pltpu.roll(rows, 0, 1, stride=1, stride_axis=0)
        o_ref[0, 0] = jnp.where(_band_valid(pl.program_id(0)), skew[:, :A_WIN], NEG)

    return pl.pallas_call(
        body,
        out_shape=jax.ShapeDtypeStruct((N_VARIANTS, N_HEADS, A_Q, A_WIN), F32),
        grid=(N_VARIANTS, N_HEADS),
        in_specs=[pl.BlockSpec((1, 1, 1, DIAG_W), lambda v, h: (v, h, 0, 0))],
        out_specs=pl.BlockSpec((1, 1, A_Q, A_WIN), lambda v, h: (v, h, 0, 0)),
        compiler_params=_cp(("parallel", "parallel")),
        name=name,
    )(diag)


def relbias_reduce(dbias, *, name):
    def body(db_ref, o_ref):
        acc = None
        for a in range(A_Q // 8):
            x = jnp.concatenate([db_ref[0, 0, 8 * a:8 * a + 8, :], jnp.zeros((8, DIAG_W - A_WIN), F32)], axis=1)
            x = pltpu.roll(x, DIAG_W - 8 * a, 1) if a else x
            acc = x if acc is None else acc + x
        row = lax.broadcasted_iota(jnp.int32, (8, DIAG_W), 0)
        for b in range(3):
            acc = jnp.where((row >> b) & 1 == 1, pltpu.roll(acc, DIAG_W - (1 << b), 1), acc)
        o_ref[0, 0] = jnp.sum(acc, axis=0, keepdims=True)

    return pl.pallas_call(
        body,
        out_shape=jax.ShapeDtypeStruct((N_VARIANTS, N_HEADS, 1, DIAG_W), F32),
        grid=(N_VARIANTS, N_HEADS),
        in_specs=[pl.BlockSpec((1, 1, A_Q, A_WIN), lambda v, h: (v, h, 0, 0))],
        out_specs=pl.BlockSpec((1, 1, 1, DIAG_W), lambda v, h: (v, h, 0, 0)),
        compiler_params=_cp(("parallel", "parallel")),
        name=name,
    )(dbias)


def _a_window_start(qb):
    return pl.multiple_of(jnp.maximum(qb * A_Q - N_LEFT * CHUNK, 0), A_Q)


def attn_a_fwd(q, k, v, biasm, *, name):
    S = q.shape[0]
    nq = S // A_Q

    def body(q_ref, k_ref, v_ref, b_ref, o_ref):
        qb = pl.program_id(1)
        start = _a_window_start(qb)
        outs = []
        for h in range(2):
            lanes = slice(h * HEAD_DIM, (h + 1) * HEAD_DIM)
            qh = q_ref[:, lanes]
            kw = k_ref[pl.ds(start, A_WIN), lanes]
            vw = v_ref[pl.ds(start, A_WIN), lanes]
            s = _dot(qh, kw, 1, 1) + b_ref[0, h]
            m = jnp.max(s, axis=-1, keepdims=True)
            e = jnp.exp(s - m)
            outs.append(_dot(e.astype(BF16), vw, 1, 0) * (1.0 / jnp.sum(e, axis=-1, keepdims=True)))
        o_ref[...] = jnp.concatenate(outs, axis=1).astype(BF16)

    qspec = pl.BlockSpec((A_Q, 2 * HEAD_DIM), lambda hp, qb: (qb, hp))
    kvspec = pl.BlockSpec((S, 2 * HEAD_DIM), lambda hp, qb: (0, hp))
    bspec = pl.BlockSpec((1, 2, A_Q, A_WIN), lambda hp, qb: (jnp.minimum(qb, N_VARIANTS - 1), hp, 0, 0))
    return pl.pallas_call(
        body,
        out_shape=jax.ShapeDtypeStruct((S, WIDTH), BF16),
        grid=(N_HEADS // 2, nq),
        in_specs=[qspec, kvspec, kvspec, bspec],
        out_specs=qspec,
        compiler_params=_cp(("parallel", "arbitrary")),
        name=name,
    )(q, k, v, biasm)


def attn_a_bwd(q, k, v, biasm, do, *, name):
    S = q.shape[0]
    nq = S // A_Q

    def body(q_ref, k_ref, v_ref, b_ref, do_ref, dq_ref, dk_ref, dv_ref, db_ref):
        qb = pl.program_id(1)
        start = _a_window_start(qb)

        @pl.when(qb == 0)
        def _():
            dk_ref[...] = jnp.zeros_like(dk_ref)
            dv_ref[...] = jnp.zeros_like(dv_ref)

        @pl.when(qb < N_VARIANTS)
        def _():
            db_ref[...] = jnp.zeros_like(db_ref)

        dqs = []
        for h in range(2):
            lanes = slice(h * HEAD_DIM, (h + 1) * HEAD_DIM)
            qh = q_ref[:, lanes]
            doh = do_ref[:, lanes]
            kw = k_ref[pl.ds(start, A_WIN), lanes]
            vw = v_ref[pl.ds(start, A_WIN), lanes]
            s = _dot(qh, kw, 1, 1) + b_ref[0, h]
            m = jnp.max(s, axis=-1, keepdims=True)
            e = jnp.exp(s - m)
            p = e * (1.0 / jnp.sum(e, axis=-1, keepdims=True))
            dp = _dot(doh, vw, 1, 1)
            delta = jnp.sum(p * dp, axis=-1, keepdims=True)
            ds = p * (dp - delta)
            db_ref[0, h] += ds
            dsb = ds.astype(BF16)
            dqs.append(_dot(dsb, kw, 1, 0) * QK_SCALE)
            dk_ref[pl.ds(start, A_WIN), lanes] += _dot(dsb, qh, 0, 0)
            dv_ref[pl.ds(start, A_WIN), lanes] += _dot(p.astype(BF16), doh, 0, 0)
        dq_ref[...] = jnp.concatenate(dqs, axis=1)

    qspec = pl.BlockSpec((A_Q, 2 * HEAD_DIM), lambda hp, qb: (qb, hp))
    kvspec = pl.BlockSpec((S, 2 * HEAD_DIM), lambda hp, qb: (0, hp))
    bspec = pl.BlockSpec((1, 2, A_Q, A_WIN), lambda hp, qb: (jnp.minimum(qb, N_VARIANTS - 1), hp, 0, 0))
    return pl.pallas_call(
        body,
        out_shape=(jax.ShapeDtypeStruct((S, WIDTH), F32), jax.ShapeDtypeStruct((S, WIDTH), F32),
                   jax.ShapeDtypeStruct((S, WIDTH), F32), jax.ShapeDtypeStruct((N_VARIANTS, N_HEADS, A_Q, A_WIN), F32)),
        grid=(N_HEADS // 2, nq),
        in_specs=[qspec, kvspec, kvspec, bspec, qspec],
        out_specs=(qspec, kvspec, kvspec, bspec),
        compiler_params=_cp(("parallel", "arbitrary")),
        name=name,
    )(q, k, v, biasm, do)


def _tri(kind):
    j = lax.broadcasted_iota(jnp.int32, (ATT_Q, ATT_Q), 0)
    s = lax.broadcasted_iota(jnp.int32, (ATT_Q, ATT_Q), 1)
    if kind == "gt":
        m = j > s
    elif kind == "le":
        m = j <= s
    else:
        m = j < s
    return jnp.where(m, 1.0, 0.0).astype(BF16)


def _cum(v, tri):
    hi, lo = _split_hi_lo(v)
    return _dot(hi, tri, 1, 0) + _dot(lo, tri, 1, 0)


def _log_sigmoids(z, mask):
    t = jnp.log(1.0 + jnp.exp(-jnp.abs(z)))
    keep = -(jnp.maximum(z, 0.0) + t)
    take = jnp.minimum(z, 0.0) - t
    return (keep if mask is None else jnp.where(mask, keep, 0.0)), take


def _strictly_before():
    row = lax.broadcasted_iota(jnp.int32, (ATT_Q, ATT_Q), 0)
    col = lax.broadcasted_iota(jnp.int32, (ATT_Q, ATT_Q), 1)
    return col < row


EXIT_LOG = -104.0


def attn_b_fwd(q, k, v, *, gather=None, name):
    S = q.shape[0]
    nq = S // ATT_Q
    plan, shards, fulls = gather if gather is not None else (None, [], [])
    ng = len(shards)

    def body(q_ref, k_ref, v_ref, *rest):
        hp = pl.program_id(0)
        qb = pl.program_id(1)
        o_ref, t_ref, n_ref = rest[2 * ng:2 * ng + 3]
        if plan is not None:
            comm = (rest[:ng], rest[2 * ng + 3:3 * ng + 3], rest[3 * ng + 3:])

            @pl.when(jnp.logical_and(hp == 0, qb == 0))
            def _():
                plan.start(*comm)

        tri = _tri("gt")

        def block(kb, carry, mask):
            ks = pl.multiple_of(kb * ATT_Q, ATT_Q)
            new = []
            for h in range(2):
                lanes = slice(h * HEAD_DIM, (h + 1) * HEAD_DIM)
                c, acc = carry[h]
                z = _dot(q_ref[:, lanes], k_ref[pl.ds(ks, ATT_Q), lanes], 1, 1)
                keep, take = _log_sigmoids(z, mask)
                w = jnp.exp(take + (_cum(keep, tri) + c))
                if mask is not None:
                    w = jnp.where(mask, w, 0.0)
                acc = acc + _dot(w.astype(BF16), v_ref[pl.ds(ks, ATT_Q), lanes], 1, 0)
                c = c + jnp.sum(keep, axis=-1, keepdims=True)
                new.append((c, acc))
            return jnp.maximum(jnp.max(new[0][0]), jnp.max(new[1][0])), tuple(new)

        def cond(state):
            it, cmax, _ = state
            return jnp.logical_and(it <= qb, cmax >= EXIT_LOG)

        def step(state):
            it, _, carry = state
            cmax, carry = block(qb - it, carry, None)
            return it + 1, cmax, carry

        init = tuple((jnp.zeros((ATT_Q, 1), F32), jnp.zeros((ATT_Q, HEAD_DIM), F32)) for _ in range(2))
        cmax, diag = block(qb, init, _strictly_before())
        visited, _, res = lax.while_loop(cond, step, (jnp.int32(1), cmax, diag))
        o_ref[...] = jnp.concatenate([res[0][1], res[1][1]], axis=1).astype(BF16)
        t_ref[...] = jnp.concatenate([jnp.broadcast_to(res[h][0], (ATT_Q, HEAD_DIM)) for h in range(2)], axis=1)
        n_ref[hp, qb] = visited.astype(F32)
        if plan is not None:
            @pl.when(jnp.logical_and(hp == N_HEADS // 2 - 1, qb == nq - 1))
            def _():
                plan.finish(*comm)

    qspec = pl.BlockSpec((ATT_Q, 2 * HEAD_DIM), lambda hp, qb: (qb, hp))
    kvspec = pl.BlockSpec((S, 2 * HEAD_DIM), lambda hp, qb: (0, hp))
    outs = pl.pallas_call(
        body,
        out_shape=(jax.ShapeDtypeStruct((S, WIDTH), BF16), jax.ShapeDtypeStruct((S, WIDTH), F32),
                   jax.ShapeDtypeStruct((N_HEADS // 2, nq), F32))
        + tuple(jax.ShapeDtypeStruct(f.shape, f.dtype) for f in fulls),
        grid=(N_HEADS // 2, nq),
        in_specs=[qspec, kvspec, kvspec] + [ANY] * (2 * ng),
        out_specs=(qspec, qspec, pl.BlockSpec(memory_space=pltpu.SMEM)) + tuple([ANY] * ng),
        scratch_shapes=plan.scratch_shapes() if plan is not None else (),
        input_output_aliases={3 + ng + i: 3 + i for i in range(ng)},
        compiler_params=_cp(("arbitrary", "arbitrary")),
        name=name,
    )(q, k, v, *shards, *fulls)
    return outs if plan is None else (outs[0], outs[1], outs[2], list(outs[3:]))


def attn_b_bwd(q, k, v, tot, nblk, do, *, scatter=None, name):
    S = q.shape[0]
    nq = S // ATT_Q
    plan, pairs16 = scatter if scatter is not None else (None, [])
    ns = len(pairs16)

    def body(q_ref, k_ref, v_ref, t_ref, n_ref, do_ref, *rest):
        hp = pl.program_id(0)
        qb = pl.program_id(1)
        dq_ref, dk_ref, dv_ref = rest[ns:ns + 3]
        if plan is not None:
            comm = (rest[:ns], rest[ns + 3:2 * ns + 3], rest[2 * ns + 3:])

            @pl.when(jnp.logical_and(hp == 0, qb == 0))
            def _():
                plan.start(*comm)

        first = jnp.clip(qb + 1 - n_ref[hp, qb].astype(jnp.int32), 0, qb + 1)
        tri_le = _tri("le")
        tri_lt = _tri("lt")

        @pl.when(qb == 0)
        def _():
            dk_ref[...] = jnp.zeros_like(dk_ref)
            dv_ref[...] = jnp.zeros_like(dv_ref)

        def block(kb, carry, mask):
            ks = pl.multiple_of(kb * ATT_Q, ATT_Q)
            new = []
            for h in range(2):
                lanes = slice(h * HEAD_DIM, (h + 1) * HEAD_DIM)
                cl, cg, dq = carry[h]
                qh = q_ref[:, lanes]
                doh = do_ref[:, lanes]
                kh = k_ref[pl.ds(ks, ATT_Q), lanes]
                vh = v_ref[pl.ds(ks, ATT_Q), lanes]
                totl = t_ref[:, h * HEAD_DIM:h * HEAD_DIM + 1]
                z = _dot(qh, kh, 1, 1)
                keep, take = _log_sigmoids(z, mask)
                sig = jnp.exp(take)
                w = sig * jnp.exp((totl - cl) - _cum(keep, tri_le))
                if mask is not None:
                    w = jnp.where(mask, w, 0.0)
                g = w * _dot(doh, vh, 1, 1)
                G = _dot(g.astype(BF16), tri_lt, 1, 0) + cg
                dz = g * (1.0 - sig) - sig * G
                if mask is not None:
                    dz = jnp.where(mask, dz, 0.0)
                dz = dz.astype(BF16)
                dq = dq + _dot(dz, kh, 1, 0)
                dk_ref[pl.ds(ks, ATT_Q), lanes] += _dot(dz, qh, 0, 0)
                dv_ref[pl.ds(ks, ATT_Q), lanes] += _dot(w.astype(BF16), doh, 0, 0)
                cl = cl + jnp.sum(keep, axis=-1, keepdims=True)
                cg = cg + jnp.sum(g, axis=-1, keepdims=True)
                new.append((cl, cg, dq))
            return tuple(new)

        init = tuple((jnp.zeros((ATT_Q, 1), F32), jnp.zeros((ATT_Q, 1), F32), jnp.zeros((ATT_Q, HEAD_DIM), F32))
                     for _ in range(2))
        res = lax.fori_loop(jnp.minimum(first, qb), qb, lambda kb, carry: block(kb, carry, None), init)
        res = block(qb, res, _strictly_before())
        dq_ref[...] = (jnp.concatenate([res[0][2], res[1][2]], axis=1) * QK_SCALE).astype(BF16)
        if plan is not None:
            @pl.when(jnp.logical_and(hp == N_HEADS // 2 - 1, qb == nq - 1))
            def _():
                plan.finish(*comm)

    qspec = pl.BlockSpec((ATT_Q, 2 * HEAD_DIM), lambda hp, qb: (qb, hp))
    kvspec = pl.BlockSpec((S, 2 * HEAD_DIM), lambda hp, qb: (0, hp))
    outs = pl.pallas_call(
        body,
        out_shape=(jax.ShapeDtypeStruct((S, WIDTH), BF16), jax.ShapeDtypeStruct((S, WIDTH), F32),
                   jax.ShapeDtypeStruct((S, WIDTH), F32)) + (tuple(plan.recv_shapes()) if plan is not None else ()),
        grid=(N_HEADS // 2, nq),
        in_specs=[qspec, kvspec, kvspec, qspec, pl.BlockSpec(memory_space=pltpu.SMEM), qspec] + [ANY] * ns,
        out_specs=(qspec, kvspec, kvspec) + tuple([ANY] * ns),
        scratch_shapes=plan.scratch_shapes() if plan is not None else (),
        compiler_params=_cp(("arbitrary", "arbitrary")),
        name=name,
    )(q, k, v, tot, nblk, do, *pairs16)
    return outs if plan is None else (outs[0], outs[1], outs[2], list(outs[3:]))


U_COLBLK = 6


def _pool_counts(t0, rows):
    t = t0 + lax.broadcasted_iota(jnp.int32, (rows, WIDTH), 0)
    lane_grp = lax.broadcasted_iota(jnp.int32, (rows, WIDTH), 1) // GROUP_DIM
    win = jnp.where(lane_grp == 0, 2, jnp.where(lane_grp == 1, 4, jnp.where(lane_grp == 2, 8, 16)))
    cnt = jnp.minimum(t + 1, win)
    return 1.0 / cnt.astype(F32), lane_grp


def _window_sums(ext, shift_fn):
    s2 = ext + shift_fn(ext, 1)
    s4 = s2 + shift_fn(s2, 2)
    s8 = s4 + shift_fn(s4, 4)
    s16 = s8 + shift_fn(s8, 8)
    return s2, s4, s8, s16


def _select_group(lane_grp, s2, s4, s8, s16):
    return jnp.where(lane_grp == 0, s2, jnp.where(lane_grp == 1, s4, jnp.where(lane_grp == 2, s8, s16)))


def _pooled_tile(u_ref, h_ref, i, T):
    halo = jnp.where(i > 0, h_ref[...], 0.0)
    ext = jnp.concatenate([halo, u_ref[...]], axis=0)
    n = T + HALO
    sums = _window_sums(ext, lambda v, k: pltpu.roll(v, k, 0))
    inv, lane_grp = _pool_counts(i * T - HALO, n)
    pooled = _select_group(lane_grp, *sums) * inv - ext
    return pooled[HALO:, :]


def pool_fwd(proj, w_pool, scale, *, name):
    S = proj.shape[0]
    T = _tile(S)
    hb = T // HALO

    def body(u_ref, h_ref, w_ref, s_ref, o_ref):
        i = pl.program_id(0)
        pooled = _pooled_tile(u_ref, h_ref, i, T).astype(BF16)
        outs = [_dot(pooled[:, g * GROUP_DIM:(g + 1) * GROUP_DIM], w_ref[g], 1, 0) for g in range(4)]
        o_ref[...] = (jnp.concatenate(outs, axis=1) * s_ref[...]).astype(BF16)

    return pl.pallas_call(
        body,
        out_shape=jax.ShapeDtypeStruct((S, WIDTH), BF16),
        grid=(S // T,),
        in_specs=[pl.BlockSpec((T, WIDTH), lambda i: (i, U_COLBLK)),
                  pl.BlockSpec((HALO, WIDTH), lambda i: (jnp.maximum(i * hb - 1, 0), U_COLBLK)),
                  pl.BlockSpec((4, GROUP_DIM, GROUP_DIM), lambda i: (0, 0, 0)),
                  pl.BlockSpec((1, WIDTH), lambda i: (0, 0))],
        out_specs=pl.BlockSpec((T, WIDTH), lambda i: (i, 0)),
        compiler_params=_cp(("parallel",)),
        name=name,
    )(proj, proj, w_pool, scale)


def pool_bwd(proj, w_pool, scale, do, *, name):
    S = proj.shape[0]
    T = _tile(S)
    hb = T // HALO
    nt = S // T

    def body(u_ref, h_ref, w_ref, s_ref, do_ref, dof_ref, du_ref, dw_ref, ds_ref):
        i = pl.program_id(0)

        @pl.when(i == 0)
        def _():
            dw_ref[...] = jnp.zeros_like(dw_ref)
            ds_ref[...] = jnp.zeros_like(ds_ref)

        pooled = _pooled_tile(u_ref, h_ref, i, T).astype(BF16)
        dov = do_ref[...].astype(F32)
        fut = jnp.where(i < nt - 1, dof_ref[...].astype(F32), 0.0)
        dmix = (jnp.concatenate([dov, fut], axis=0) * s_ref[...]).astype(BF16)
        mixed, dpool = [], []
        for g in range(4):
            lanes = slice(g * GROUP_DIM, (g + 1) * GROUP_DIM)
            mixed.append(_dot(pooled[:, lanes], w_ref[g], 1, 0))
            dw_ref[g] += _dot(pooled[:, lanes], dmix[:T, lanes], 0, 0)
            dpool.append(_dot(dmix[:, lanes], w_ref[g], 1, 1))
        ds_ref[...] += jnp.sum(dov * jnp.concatenate(mixed, axis=1), axis=0, keepdims=True)
        dp = jnp.concatenate(dpool, axis=1)
        n = T + HALO
        inv, lane_grp = _pool_counts(i * T, n)
        sums = _window_sums(dp * inv, lambda v, k: pltpu.roll(v, n - k, 0))
        du = _select_group(lane_grp, *sums) - dp
        du_ref[...] = du[:T, :].astype(BF16)

    row = pl.BlockSpec((T, WIDTH), lambda i: (i, 0))
    return pl.pallas_call(
        body,
        out_shape=(jax.ShapeDtypeStruct((S, WIDTH), BF16), jax.ShapeDtypeStruct((4, GROUP_DIM, GROUP_DIM), F32),
                   jax.ShapeDtypeStruct((1, WIDTH), F32)),
        grid=(nt,),
        in_specs=[pl.BlockSpec((T, WIDTH), lambda i: (i, U_COLBLK)),
                  pl.BlockSpec((HALO, WIDTH), lambda i: (jnp.maximum(i * hb - 1, 0), U_COLBLK)),
                  pl.BlockSpec((4, GROUP_DIM, GROUP_DIM), lambda i: (0, 0, 0)),
                  pl.BlockSpec((1, WIDTH), lambda i: (0, 0)),
                  row,
                  pl.BlockSpec((HALO, WIDTH), lambda i: (jnp.minimum((i + 1) * hb, S // HALO - 1), 0))],
        out_specs=(row, pl.BlockSpec((4, GROUP_DIM, GROUP_DIM), lambda i: (0, 0, 0)),
                   pl.BlockSpec((1, WIDTH), lambda i: (0, 0))),
        compiler_params=_cp(("arbitrary",)),
        name=name,
    )(proj, proj, w_pool, scale, do, do)


GATE_BLK0 = GATE_COL0 // WIDTH


def merge_fwd(oa, ob, oc, proj, b_gate, wa, wb, wc, *, name):
    S = oa.shape[0]
    T = _tile(S)

    def body(oa_ref, ob_ref, oc_ref, ga, gb, gc, ba, bb, bc, wa_ref, wb_ref, wc_ref, m_ref):
        acc = None
        for o_ref, g_ref, b_ref, w_ref in ((oa_ref, ga, ba, wa_ref), (ob_ref, gb, bb, wb_ref), (oc_ref, gc, bc, wc_ref)):
            y = _dot(o_ref[...], w_ref[...], 1, 0)
            t = jax.nn.sigmoid(g_ref[...] + b_ref[...]) * y
            acc = t if acc is None else acc + t
        m_ref[...] = acc.astype(BF16)

    row = pl.BlockSpec((T, WIDTH), lambda i, n: (i, 0))
    gate = lambda b: pl.BlockSpec((T, WIDTH), lambda i, n, b=b: (i, GATE_BLK0 + 2 * b + n))
    bias = lambda b: pl.BlockSpec((1, WIDTH), lambda i, n, b=b: (0, 2 * b + n))
    wspec = pl.BlockSpec((WIDTH, WIDTH), lambda i, n: (0, n))
    return pl.pallas_call(
        body,
        out_shape=jax.ShapeDtypeStruct((S, D_MODEL), BF16),
        grid=(S // T, 2),
        in_specs=[row, row, row, gate(0), gate(1), gate(2), bias(0), bias(1), bias(2), wspec, wspec, wspec],
        out_specs=pl.BlockSpec((T, WIDTH), lambda i, n: (i, n)),
        compiler_params=_cp(("parallel", "parallel")),
        name=name,
    )(oa, ob, oc, proj, proj, proj, b_gate, b_gate, b_gate, wa, wb, wc)


def merge_bwd(dm, oa, ob, oc, proj, b_gate, wa, wb, wc, *, name):
    S = oa.shape[0]
    T = _tile(S)

    def body(dm_ref, oa_ref, ob_ref, oc_ref, ga, gb, gc, ba, bb, bc, wa_ref, wb_ref, wc_ref,
             ta, tb, tc, dga, dgb, dgc, dba, dbb, dbc):
        i = pl.program_id(1)
        dmv = dm_ref[...].astype(F32)
        for o_ref, g_ref, b_ref, w_ref, t_ref, dg_ref, db_ref in (
                (oa_ref, ga, ba, wa_ref, ta, dga, dba), (ob_ref, gb, bb, wb_ref, tb, dgb, dbb),
                (oc_ref, gc, bc, wc_ref, tc, dgc, dbc)):
            y = _dot(o_ref[...], w_ref[...], 1, 0)
            gate = jax.nn.sigmoid(g_ref[...] + b_ref[...])
            t_ref[...] = (gate * dmv).astype(BF16)
            dgl = dmv * y * gate * (1.0 - gate)
            dg_ref[...] = dgl.astype(BF16)

            @pl.when(i == 0)
            def _():
                db_ref[...] = jnp.zeros_like(db_ref)

            db_ref[...] += jnp.sum(dgl, axis=0, keepdims=True)

    row = pl.BlockSpec((T, WIDTH), lambda n, i: (i, 0))
    half = pl.BlockSpec((T, WIDTH), lambda n, i: (i, n))
    gate = lambda b: pl.BlockSpec((T, WIDTH), lambda n, i, b=b: (i, GATE_BLK0 + 2 * b + n))
    bias = lambda b: pl.BlockSpec((1, WIDTH), lambda n, i, b=b: (0, 2 * b + n))
    wspec = pl.BlockSpec((WIDTH, WIDTH), lambda n, i: (0, n))
    bvec = pl.BlockSpec((1, WIDTH), lambda n, i: (0, n))
    act = jax.ShapeDtypeStruct((S, D_MODEL), BF16)
    vec = jax.ShapeDtypeStruct((1, D_MODEL), F32)
    return pl.pallas_call(
        body,
        out_shape=(act, act, act, act, act, act, vec, vec, vec),
        grid=(2, S // T),
        in_specs=[half, row, row, row, gate(0), gate(1), gate(2), bias(0), bias(1), bias(2), wspec, wspec, wspec],
        out_specs=(half, half, half, half, half, half, bvec, bvec, bvec),
        compiler_params=_cp(("parallel", "arbitrary")),
        name=name,
    )(dm, oa, ob, oc, proj, proj, proj, b_gate, b_gate, b_gate, wa, wb, wc)


FF_T = 256
FF_BLKS = D_FF // FF_T


def _silu_parts(x):
    s = jax.nn.sigmoid(x)
    return x * s, s


def _conv3(ext, w_ref, b_ref):
    return (b_ref[...] + w_ref[0:1, :] * pltpu.roll(ext, 2, 0) + w_ref[1:2, :] * pltpu.roll(ext, 1, 0)
            + w_ref[2:3, :] * ext)


def conv_glu_fwd(u, conv_w, conv_b, *, name):
    S = u.shape[0]
    T = _tile(S)
    hb = T // CONV_HALO

    def body(ug, ugh, uv, uvh, wg, wv, bg, bv, a_ref):
        i = pl.program_id(1)
        cs = []
        for m_ref, h_ref, w_ref, b_ref in ((ug, ugh, wg, bg), (uv, uvh, wv, bv)):
            halo = jnp.where(i > 0, h_ref[...], 0.0)
            ext = jnp.concatenate([halo, m_ref[...]], axis=0)
            cs.append(_conv3(ext, w_ref, b_ref)[CONV_HALO:, :])
        act, _ = _silu_parts(cs[0])
        a_ref[...] = (act * cs[1]).astype(BF16)

    main = lambda o: pl.BlockSpec((T, FF_T), lambda c, i, o=o: (i, c + o))
    halo = lambda o: pl.BlockSpec((CONV_HALO, FF_T), lambda c, i, o=o: (jnp.maximum(i * hb - 1, 0), c + o))
    wsp = lambda o: pl.BlockSpec((3, FF_T), lambda c, i, o=o: (0, c + o))
    bsp = lambda o: pl.BlockSpec((1, FF_T), lambda c, i, o=o: (0, c + o))
    return pl.pallas_call(
        body,
        out_shape=jax.ShapeDtypeStruct((S, D_FF), BF16),
        grid=(FF_BLKS, S // T),
        in_specs=[main(0), halo(0), main(FF_BLKS), halo(FF_BLKS), wsp(0), wsp(FF_BLKS), bsp(0), bsp(FF_BLKS)],
        out_specs=pl.BlockSpec((T, FF_T), lambda c, i: (i, c)),
        compiler_params=_cp(("parallel", "parallel")),
        name=name,
    )(u, u, u, u, conv_w, conv_w, conv_b, conv_b)


def conv_glu_bwd(u, conv_w, conv_b, da, *, name):
    S = u.shape[0]
    T = _tile(S)
    hb = T // CONV_HALO
    nt = S // T
    n = T + 2 * CONV_HALO

    def body(ug, ugp, ugf, uv, uvp, uvf, wg, wv, bg, bv, da_ref, daf_ref,
             dug, duv, dwg, dwv, dbg, dbv):
        i = pl.program_id(1)
        first, last = i == 0, i == nt - 1
        exts, cs = [], []
        for m_ref, p_ref, f_ref, w_ref, b_ref in ((ug, ugp, ugf, wg, bg), (uv, uvp, uvf, wv, bv)):
            ext = jnp.concatenate([jnp.where(first, 0.0, p_ref[...]), m_ref[...], jnp.where(last, 0.0, f_ref[...])], axis=0)
            exts.append(ext)
            cs.append(_conv3(ext, w_ref, b_ref))
        dae = jnp.concatenate([jnp.zeros((CONV_HALO, FF_T), F32), da_ref[...].astype(F32),
                               jnp.where(last, 0.0, daf_ref[...].astype(F32))], axis=0)
        act, sg = _silu_parts(cs[0])
        dcs = (dae * cs[1] * (sg * (1.0 + cs[0] * (1.0 - sg))), dae * act)
        main = slice(CONV_HALO, CONV_HALO + T)
        for ext, dc, w_ref, du_ref, dw_ref, db_ref in ((exts[0], dcs[0], wg, dug, dwg, dbg),
                                                       (exts[1], dcs[1], wv, duv, dwv, dbv)):
            du = (w_ref[2:3, :] * dc + w_ref[1:2, :] * pltpu.roll(dc, n - 1, 0) + w_ref[0:1, :] * pltpu.roll(dc, n - 2, 0))
            du_ref[...] = du[main, :].astype(BF16)
            dcm = dc[main, :]
            rows = [jnp.sum(dcm * pltpu.roll(ext, 2 - j, 0)[main, :], axis=0, keepdims=True) if j < 2
                    else jnp.sum(dcm * ext[main, :], axis=0, keepdims=True) for j in range(3)]

            @pl.when(first)
            def _():
                dw_ref[...] = jnp.zeros_like(dw_ref)
                db_ref[...] = jnp.zeros_like(db_ref)

            dw_ref[...] += jnp.concatenate(rows, axis=0)
            db_ref[...] += jnp.sum(dcm, axis=0, keepdims=True)

    main = lambda o: pl.BlockSpec((T, FF_T), lambda c, i, o=o: (i, c + o))
    past = lambda o: pl.BlockSpec((CONV_HALO, FF_T), lambda c, i, o=o: (jnp.maximum(i * hb - 1, 0), c + o))
    fut = lambda o: pl.BlockSpec((CONV_HALO, FF_T), lambda c, i, o=o: (jnp.minimum((i + 1) * hb, S // CONV_HALO - 1), c + o))
    wsp = lambda o: pl.BlockSpec((3, FF_T), lambda c, i, o=o: (0, c + o))
    bsp = lambda o: pl.BlockSpec((1, FF_T), lambda c, i, o=o: (0, c + o))
    return pl.pallas_call(
        body,
        out_shape=(jax.ShapeDtypeStruct((S, D_FF), BF16), jax.ShapeDtypeStruct((S, D_FF), BF16),
                   jax.ShapeDtypeStruct((3, D_FF), F32), jax.ShapeDtypeStruct((3, D_FF), F32),
                   jax.ShapeDtypeStruct((1, D_FF), F32), jax.ShapeDtypeStruct((1, D_FF), F32)),
        grid=(FF_BLKS, nt),
        in_specs=[main(0), past(0), fut(0), main(FF_BLKS), past(FF_BLKS), fut(FF_BLKS),
                  wsp(0), wsp(FF_BLKS), bsp(0), bsp(FF_BLKS), main(0), fut(0)],
        out_specs=(main(0), main(0), wsp(0), wsp(0), bsp(0), bsp(0)),
        compiler_params=_cp(("parallel", "arbitrary")),
        name=name,
    )(u, u, u, u, u, u, conv_w, conv_w, conv_b, conv_b, da, da)


def loss_head(y, target, *, name):
    S, D = y.shape
    T = _tile(S)

    def body(y_ref, t_ref, dy_ref, l_ref):
        i = pl.program_id(0)
        err = y_ref[...] - t_ref[...]
        dy_ref[...] = err * (1.0 / D)

        @pl.when(i == 0)
        def _():
            l_ref[...] = jnp.zeros_like(l_ref)

        l_ref[...] += 0.5 * jnp.sum(jnp.mean(err * err, axis=-1, keepdims=True))

    row = pl.BlockSpec((T, D), lambda i: (i, 0))
    return pl.pallas_call(
        body,
        out_shape=(jax.ShapeDtypeStruct((S, D), F32), jax.ShapeDtypeStruct((8, 128), F32)),
        grid=(S // T,),
        in_specs=[row, row],
        out_specs=(row, pl.BlockSpec((8, 128), lambda i: (0, 0))),
        compiler_params=_cp(("arbitrary",)),
        name=name,
    )(y, target)


ELEMS_PER_BLOCK = 256 * 1024


def _rows_tile(rows, cols):
    if rows * cols <= ELEMS_PER_BLOCK or rows % 8:
        return rows
    best = 8
    for tr in range(8, rows + 1, 8):
        if rows % tr == 0 and tr * cols <= ELEMS_PER_BLOCK:
            best = tr
    return best


def _adamw_math(g, w_ref, m_ref, v_ref, g_out, d_out, m_out, v_out):
    mn = ADAM_B1 * m_ref[...] + (1.0 - ADAM_B1) * g
    vn = ADAM_B2 * v_ref[...] + (1.0 - ADAM_B2) * (g * g)
    m_hat = mn / (1.0 - ADAM_B1 ** ADAM_STEP)
    v_hat = vn / (1.0 - ADAM_B2 ** ADAM_STEP)
    g_out[...] = g
    d_out[...] = -ADAM_LR * (m_hat / (jnp.sqrt(v_hat) + ADAM_EPS) + ADAM_WD * w_ref[...])
    m_out[...] = mn
    v_out[...] = vn


def adamw(w, m, v, g, *, name):
    rows, cols = w.shape
    tr = _rows_tile(rows, cols)

    def body(w_ref, m_ref, v_ref, g_ref, g_out, d_out, m_out, v_out):
        _adamw_math(g_ref[...], w_ref, m_ref, v_ref, g_out, d_out, m_out, v_out)

    spec = pl.BlockSpec((tr, cols), lambda i: (i, 0))
    shp = jax.ShapeDtypeStruct((rows, cols), F32)
    return pl.pallas_call(
        body,
        out_shape=(shp, shp, shp, shp),
        grid=(rows // tr,),
        in_specs=[spec] * 4,
        out_specs=(spec, spec, spec, spec),
        compiler_params=_cp(("parallel",)),
        name=name,
    )(w, m, v, g)


ANY = pl.BlockSpec(memory_space=pl.ANY)
STAGE_BYTES = 2 * 1024 * 1024


def _mesh_pos():
    return lax.axis_index("x"), lax.axis_index("y"), lax.axis_index("c")


def _chip_peers(x, y):
    return [(1 - x, y), (x, 1 - y), (1 - x, 1 - y)]


def _all_peers(x, y, c):
    return [((1 - x) if (r >> 2) & 1 else x, (1 - y) if (r >> 1) & 1 else y, (1 - c) if r & 1 else c)
            for r in range(1, 8)]


def _shard_slice(ref, axis, j, size, layer=None):
    idx = [slice(None)] * 3
    idx[axis] = pl.ds(pl.multiple_of(j * size, 128 if axis == 2 else 16), size)
    if layer is not None:
        idx[0] = pl.ds(layer, 1)
    return ref.at[tuple(idx)]


class LayerGather:
    def __init__(self, shards, axes, layer):
        self.nt = len(shards)
        self.axes = list(axes)
        self.layer = layer
        self.shapes = [s.shape for s in shards]
        self.dtypes = [s.dtype for s in shards]
        self.sizes = [s.shape[a] for s, a in zip(shards, axes)]
        self.split = [s.shape[1] % 32 == 0 for s in shards]
        self.half_rows = [s.shape[1] // 2 if sp else s.shape[1] for s, sp in zip(shards, self.split)]
        self.chunk_rows = []
        for s in shards:
            rt = s.shape[1]
            while rt % 32 == 0 and rt * s.shape[2] * s.dtype.itemsize > STAGE_BYTES:
                rt //= 2
            self.chunk_rows.append(rt)

    def full_shapes(self):
        out = []
        for shp, a, sz, dt in zip(self.shapes, self.axes, self.sizes, self.dtypes):
            shp = list(shp)
            shp[a] = 4 * sz
            out.append(jax.ShapeDtypeStruct(tuple(shp), dt))
        return out

    def scratch_shapes(self):
        return ([pltpu.VMEM((1, rt, shp[2]), dt) for shp, rt, dt in zip(self.shapes, self.chunk_rows, self.dtypes)]
                + [pltpu.SemaphoreType.DMA((2,))] + [pltpu.SemaphoreType.DMA((3 * self.nt,)) for _ in range(4)])

    def _views(self, ins, outs, scratch):
        nt = self.nt
        stage, stage_sems = scratch[:nt], scratch[nt]
        ici_send, ici_recv, d2d_send, d2d_recv = scratch[nt + 1:]
        x, y, c = _mesh_pos()
        mine = 2 * x + y
        peers = _chip_peers(x, y)
        layer = pl.ds(self.layer, 1)

        def rows(t, half, r0=0, n=None):
            hr = self.half_rows[t]
            if n is None:
                return pl.ds(pl.multiple_of(half * hr, 16), hr) if self.split[t] else pl.ds(0, hr)
            return pl.ds(r0, n)

        def placed(t, blk, row_sel, row_len):
            sz = self.sizes[t]
            if self.axes[t] == 2:
                return outs[t].at[layer, row_sel, pl.ds(pl.multiple_of(blk * sz, 128), sz)]
            return outs[t].at[layer, pl.ds(pl.multiple_of(blk * sz, 16) + row_sel.start, row_len), :]

        def ici(t, k, blk):
            px, py = peers[k]
            sel = rows(t, c)
            return pltpu.make_async_remote_copy(
                src_ref=ins[t].at[layer, sel, :], dst_ref=placed(t, blk, sel, self.half_rows[t]),
                send_sem=ici_send.at[3 * t + k], recv_sem=ici_recv.at[3 * t + k],
                device_id=(px, py, c), device_id_type=MESH_T)

        def d2d(t, k, half):
            px, py = peers[k]
            piece = placed(t, 2 * px + py, rows(t, half), self.half_rows[t])
            return pltpu.make_async_remote_copy(
                src_ref=piece, dst_ref=piece, send_sem=d2d_send.at[3 * t + k], recv_sem=d2d_recv.at[3 * t + k],
                device_id=(x, y, 1 - c), device_id_type=MESH_T)

        def own_chunk(t, r0):
            rt = self.chunk_rows[t]
            sel = pl.ds(r0, rt)
            return ins[t].at[layer, sel, :], placed(t, mine, sel, rt), stage[t], stage_sems

        return c, mine, peers, ici, d2d, own_chunk

    def start(self, ins, outs, scratch):
        c, mine, peers, ici, d2d, own_chunk = self._views(ins, outs, scratch)
        for t in range(self.nt):
            for k in range(3):
                ici(t, k, mine).start()
        for t in range(self.nt):
            for r0 in range(0, self.shapes[t][1], self.chunk_rows[t]):
                src, dst, buf, sems = own_chunk(t, r0)
                load = pltpu.make_async_copy(src, buf, sems.at[0])
                load.start()
                load.wait()
                store = pltpu.make_async_copy(buf, dst, sems.at[1])
                store.start()
                store.wait()

    def finish(self, ins, outs, scratch):
        c, mine, peers, ici, d2d, own_chunk = self._views(ins, outs, scratch)
        for t in range(self.nt):
            for k, (px, py) in enumerate(peers):
                ici(t, k, 2 * px + py).wait_recv()
                if self.split[t]:
                    d2d(t, k, c).start()
        for t in range(self.nt):
            for k in range(3):
                if self.split[t]:
                    d2d(t, k, 1 - c).wait_recv()
        for t in range(self.nt):
            for k in range(3):
                ici(t, k, mine).wait_send()
                if self.split[t]:
                    d2d(t, k, c).wait_send()


def all_gather_layer(shards, axes, layer, *, name):
    plan = LayerGather(shards, axes, layer)
    nt = plan.nt

    def body(*refs):
        ins, outs, scratch = refs[:nt], refs[nt:2 * nt], refs[2 * nt:]
        plan.start(ins, outs, scratch)
        plan.finish(ins, outs, scratch)

    return pl.pallas_call(
        body,
        out_shape=tuple(plan.full_shapes()),
        in_specs=[ANY] * nt,
        out_specs=tuple([ANY] * nt),
        scratch_shapes=plan.scratch_shapes(),
        name=name,
    )(*shards)


class HalfLayout:
    def __init__(self, shape, axis):
        self.R, self.C = shape
        self.axis = axis
        if axis == 1:
            self.hr, self.pw = self.R // 2, self.C // 4
            self.half_shape = (self.hr, self.C)
        else:
            self.hr, self.pw = self.R // 8, self.C
            self.half_shape = (4 * self.hr, self.C)
        self.tr = _rows_tile(self.hr, self.pw)
        self.nr = self.hr // self.tr

    def in_grad(self, ref, blk, half):
        if self.axis == 1:
            return ref.at[pl.ds(pl.multiple_of(half * self.hr, 16), self.hr), pl.ds(pl.multiple_of(blk * self.pw, 128), self.pw)]
        return ref.at[pl.ds(pl.multiple_of((2 * blk + half) * self.hr, 16), self.hr), :]

    def in_half(self, ref, blk):
        if self.axis == 1:
            return ref.at[:, pl.ds(pl.multiple_of(blk * self.pw, 128), self.pw)]
        return ref.at[pl.ds(pl.multiple_of(blk * self.hr, 16), self.hr), :]

    def grad_spec(self):
        if self.axis == 1:
            return pl.BlockSpec((self.tr, self.pw), lambda j, i, s: (s[0] * self.nr + i, j))
        return pl.BlockSpec((self.tr, self.pw), lambda j, i, s: ((2 * j + s[0]) * self.nr + i, 0))

    def half_spec(self):
        if self.axis == 1:
            return pl.BlockSpec((self.tr, self.pw), lambda j, i, s: (i, j))
        return pl.BlockSpec((self.tr, self.pw), lambda j, i, s: (j * self.nr + i, 0))


def half_exchange(grads, layouts, *, name):
    nt = len(grads)
    pieces = [(t, j) for t in range(nt) for j in (range(4) if layouts[t].axis == 0 else range(1))]

    def body(*refs):
        ins, outs = refs[:nt], refs[nt:2 * nt]
        send_sems, recv_sems = refs[2 * nt:]
        x, y, c = _mesh_pos()
        cps = []
        for n, (t, j) in enumerate(pieces):
            lay = layouts[t]
            if lay.axis == 1:
                src = ins[t].at[pl.ds(pl.multiple_of((1 - c) * lay.hr, 16), lay.hr), :]
                dst = outs[t]
            else:
                src = lay.in_grad(ins[t], j, 1 - c)
                dst = lay.in_half(outs[t], j)
            cp = pltpu.make_async_remote_copy(src_ref=src, dst_ref=dst, send_sem=send_sems.at[n], recv_sem=recv_sems.at[n],
                                              device_id=(x, y, 1 - c), device_id_type=MESH_T)
            cp.start()
            cps.append(cp)
        for cp in cps:
            cp.wait_recv()
        for cp in cps:
            cp.wait_send()

    return pl.pallas_call(
        body,
        out_shape=tuple(jax.ShapeDtypeStruct(lay.half_shape, F32) for lay in layouts),
        in_specs=[ANY] * nt,
        out_specs=tuple([ANY] * nt),
        scratch_shapes=[pltpu.SemaphoreType.DMA((len(pieces),)), pltpu.SemaphoreType.DMA((len(pieces),))],
        name=name,
    )(*grads)


def pair_sum(grad, other, lay, core, *, name):
    def body(c_ref, g_ref, o_ref, s32_ref, s16_ref):
        s = g_ref[...] + o_ref[...]
        s32_ref[...] = s
        s16_ref[...] = s.astype(BF16)

    return pl.pallas_call(
        body,
        out_shape=(jax.ShapeDtypeStruct(lay.half_shape, F32), jax.ShapeDtypeStruct(lay.half_shape, BF16)),
        grid_spec=pltpu.PrefetchScalarGridSpec(
            num_scalar_prefetch=1, grid=(4, lay.nr),
            in_specs=[lay.grad_spec(), lay.half_spec()],
            out_specs=(lay.half_spec(), lay.half_spec())),
        compiler_params=_cp(("parallel", "parallel")),
        name=name,
    )(core, grad, other)


class BlockScatter:
    def __init__(self, layouts):
        self.layouts = layouts
        self.nt = len(layouts)

    def recv_shapes(self):
        return [jax.ShapeDtypeStruct((3, lay.hr, lay.pw), BF16) for lay in self.layouts]

    def scratch_shapes(self):
        return [pltpu.SemaphoreType.DMA((3 * self.nt,)), pltpu.SemaphoreType.DMA((3 * self.nt,))]

    def _copies(self, pairs16, recv, scratch):
        send_sems, recv_sems = scratch
        x, y, c = _mesh_pos()
        return [pltpu.make_async_remote_copy(
            src_ref=lay.in_half(pairs16[t], 2 * px + py), dst_ref=recv[t].at[k],
            send_sem=send_sems.at[3 * t + k], recv_sem=recv_sems.at[3 * t + k],
            device_id=(px, py, c), device_id_type=MESH_T)
            for t, lay in enumerate(self.layouts) for k, (px, py) in enumerate(_chip_peers(x, y))]

    def start(self, pairs16, recv, scratch):
        for cp in self._copies(pairs16, recv, scratch):
            cp.start()

    def finish(self, pairs16, recv, scratch):
        copies = self._copies(pairs16, recv, scratch)
        for cp in copies:
            cp.wait_recv()
        for cp in copies:
            cp.wait_send()


def scatter_blocks(pairs16, layouts, small, *, name):
    plan = BlockScatter(layouts)
    nt = plan.nt

    def body(*refs):
        in16, small_in = refs[:nt], refs[nt]
        recv, small_out = refs[nt + 1:2 * nt + 1], refs[2 * nt + 1]
        scratch, ssend, srecv = refs[2 * nt + 2:2 * nt + 4], refs[2 * nt + 4], refs[2 * nt + 5]
        x, y, c = _mesh_pos()
        me = 4 * x + 2 * y + c
        plan.start(in16, recv, scratch)
        sends, recvs = [], []
        for r, (px, py, pc) in enumerate(_all_peers(x, y, c)):
            def mk(slot, r=r, px=px, py=py, pc=pc):
                return pltpu.make_async_remote_copy(
                    src_ref=small_in, dst_ref=small_out.at[slot], send_sem=ssend.at[r], recv_sem=srecv.at[r],
                    device_id=(px, py, pc), device_id_type=MESH_T)
            snd = mk(me)
            snd.start()
            sends.append(snd)
            recvs.append(mk(4 * px + 2 * py + pc))
        for r in recvs:
            r.wait_recv()
        for s in sends:
            s.wait_send()
        plan.finish(in16, recv, scratch)

    return pl.pallas_call(
        body,
        out_shape=tuple(plan.recv_shapes() + [jax.ShapeDtypeStruct((8,) + small.shape, F32)]),
        in_specs=[ANY] * (nt + 1),
        out_specs=tuple([ANY] * (nt + 1)),
        scratch_shapes=plan.scratch_shapes() + [pltpu.SemaphoreType.DMA((7,)), pltpu.SemaphoreType.DMA((7,))],
        name=name,
    )(*pairs16, small)


def sum_chips(pair32, recv, lay, chip, *, name):
    def body(j_ref, p_ref, r_ref, s_ref):
        acc = p_ref[...]
        for k in range(3):
            acc = acc + r_ref[k].astype(F32)
        s_ref[...] = acc

    if lay.axis == 1:
        own = pl.BlockSpec((lay.tr, lay.pw), lambda i, j: (i, j[0]))
    else:
        own = pl.BlockSpec((lay.tr, lay.pw), lambda i, j: (j[0] * lay.nr + i, 0))
    return pl.pallas_call(
        body,
        out_shape=jax.ShapeDtypeStruct((lay.hr, lay.pw), F32),
        grid_spec=pltpu.PrefetchScalarGridSpec(
            num_scalar_prefetch=1, grid=(lay.nr,),
            in_specs=[own, pl.BlockSpec((3, lay.tr, lay.pw), lambda i, j: (0, i, 0))],
            out_specs=pl.BlockSpec((lay.tr, lay.pw), lambda i, j: (i, 0))),
        compiler_params=_cp(("parallel",)),
        name=name,
    )(chip, pair32, recv)


def sum_devices(gathered, own, me, *, name):
    _, R, C = gathered.shape

    def body(me_ref, g_ref, o_ref, s_ref):
        acc = None
        for k in range(8):
            part = jnp.where(me_ref[0] == k, o_ref[...], g_ref[k])
            acc = part if acc is None else acc + part
        s_ref[...] = acc

    return pl.pallas_call(
        body,
        out_shape=jax.ShapeDtypeStruct((R, C), F32),
        grid_spec=pltpu.PrefetchScalarGridSpec(
            num_scalar_prefetch=1, grid=(1,),
            in_specs=[pl.BlockSpec((8, R, C), lambda i, m: (0, 0, 0)), pl.BlockSpec((R, C), lambda i, m: (0, 0))],
            out_specs=pl.BlockSpec((R, C), lambda i, m: (0, 0))),
        compiler_params=_cp(("arbitrary",)),
        name=name,
    )(me, gathered, own)


def sibling_swap(parts, *, name):
    nt = len(parts)

    def body(*refs):
        ins, outs = refs[:nt], refs[nt:2 * nt]
        send_sems, recv_sems = refs[2 * nt:]
        x, y, c = _mesh_pos()
        cps = []
        for t in range(nt):
            cp = pltpu.make_async_remote_copy(src_ref=ins[t], dst_ref=outs[t], send_sem=send_sems.at[t],
                                              recv_sem=recv_sems.at[t], device_id=(x, y, 1 - c), device_id_type=MESH_T)
            cp.start()
            cps.append(cp)
        for cp in cps:
            cp.wait_recv()
        for cp in cps:
            cp.wait_send()

    return pl.pallas_call(
        body,
        out_shape=tuple(jax.ShapeDtypeStruct(p.shape, p.dtype) for p in parts),
        in_specs=[ANY] * nt,
        out_specs=tuple([ANY] * nt),
        scratch_shapes=[pltpu.SemaphoreType.DMA((nt,)), pltpu.SemaphoreType.DMA((nt,))],
        name=name,
    )(*parts)


def adamw_halves(w, m, v, mine, other, lay, core, *, name):
    _, r, c = w.shape
    tr, nr = lay.tr, lay.nr
    assert (r, c) == (2 * lay.hr, lay.pw), (w.shape, lay.hr, lay.pw)

    def body(c_ref, w_ref, m_ref, v_ref, *rest):
        g_refs, outs = rest[:2 * DEPTH], rest[2 * DEPTH:]
        l, h = pl.program_id(0), pl.program_id(1)
        g = None
        for d in range(DEPTH):
            gd = jnp.where(h == c_ref[0], g_refs[d][...], g_refs[DEPTH + d][...])
            g = gd if g is None else jnp.where(l == d, gd, g)
        _adamw_math(g, w_ref, m_ref, v_ref, *outs)

    full = pl.BlockSpec((None, tr, c), lambda l, h, i, s: (l, h * nr + i, 0))

    def part(d, is_mine):
        def index(l, h, i, s):
            used = jnp.logical_and(l == d, (h == s[0]) == is_mine)
            return jnp.where(used, i, 0), 0
        return pl.BlockSpec((tr, c), index)

    shp = jax.ShapeDtypeStruct((DEPTH, r, c), F32)
    return pl.pallas_call(
        body,
        out_shape=(shp, shp, shp, shp),
        grid_spec=pltpu.PrefetchScalarGridSpec(
            num_scalar_prefetch=1, grid=(DEPTH, 2, nr),
            in_specs=[full, full, full] + [part(d, True) for d in range(DEPTH)] + [part(d, False) for d in range(DEPTH)],
            out_specs=(full, full, full, full)),
        compiler_params=_cp(("arbitrary", "arbitrary", "arbitrary")),
        name=name,
    )(core, w, m, v, *mine, *other)


WEIGHTS = ("norm_mix", "w_in", "b_gate", "q_norm_a", "k_norm_a", "rel_bias_a", "w_pool", "pool_scale",
           "w_branch_a", "w_branch_b", "w_branch_c", "w_out", "norm_ffn", "w_up", "conv_w", "conv_b", "w_down")
SHARDED = {"w_in": 2, "w_branch_a": 2, "w_branch_b": 2, "w_branch_c": 2, "w_out": 1, "w_up": 2, "conv_w": 2,
           "w_down": 1}
REPLICATED = tuple(n for n in WEIGHTS if n not in SHARDED)
MATMUL_WEIGHTS = tuple(n for n in SHARDED if n != "conv_w")
SMALL_WEIGHTS = tuple(n for n in WEIGHTS if n not in MATMUL_WEIGHTS)
SMALL_ROWS = 1496


def _layer_fwd(x, p, tables, prefetch=None):
    full = p["full"]
    diag = exact_dot(p["rel_bias_a"], tables["onehot_t"], name="bias_diagonals")
    diag = diag.reshape(N_HEADS, N_VARIANTS, 1, DIAG_W).transpose(1, 0, 2, 3)
    biasm = bias_expand(diag, name="bias_expand")
    gq8 = jnp.tile(p["q_norm_a"], N_HEADS)[None]
    gk8 = jnp.tile(p["k_norm_a"], N_HEADS)[None]
    h = rmsnorm_fwd(x, p["norm_mix"][None], name="rmsnorm_fwd")
    proj = matmul(h, full["w_in"], b_layer=p["l"], name="mm_in")
    qa, ka, va, qb, kb, vb = qkv_prep(proj, gq8, gk8, name="qkv_prep")
    oa = attn_a_fwd(qa, ka, va, biasm, name="attn_a_fwd")
    if prefetch is None:
        ob, tot, nblk = attn_b_fwd(qb, kb, vb, name="attn_b_fwd")
    else:
        plan, shards, names = prefetch
        ob, tot, nblk, filled = attn_b_fwd(qb, kb, vb, gather=(plan, shards, [full[n] for n in names]),
                                           name="attn_b_fwd_gather")
        full = dict(zip(names, filled))
    wpool = p["w_pool"].astype(BF16)
    oc = pool_fwd(proj, wpool, p["pool_scale"][None], name="pool_fwd")
    merged = merge_fwd(oa, ob, oc, proj, p["b_gate"][None], p["w_branch_a"], p["w_branch_b"], p["w_branch_c"],
                       name="merge_fwd")
    x1 = matmul(merged, full["w_out"], b_layer=p["l"], add=x, name="mm_out")
    h2 = rmsnorm_fwd(x1, p["norm_ffn"][None], name="rmsnorm_fwd")
    u = matmul(h2, full["w_up"], b_layer=p["l"], name="mm_up")
    a = conv_glu_fwd(u, p["conv_w"], p["conv_b"][None], name="conv_glu_fwd")
    x2 = matmul(a, full["w_down"], b_layer=p["l"], add=x1, name="mm_down")
    saved = dict(x=x, h=h, proj=proj, qa=qa, ka=ka, va=va, qb=qb, kb=kb, vb=vb, oa=oa, ob=ob, tot=tot, nblk=nblk, oc=oc,
                 merged=merged, x1=x1, h2=h2, u=u, a=a, biasm=biasm, gq8=gq8, gk8=gk8, wpool=wpool)
    return x2, saved, full


def _layer_bwd(dx2, s, p, tables, scatter=None):
    g = {}
    full, l = p["full"], p["l"]
    da = matmul(dx2, full["w_down"], b_layer=l, tb=True, name="mm_down_dx")
    g["w_down"] = matmul(s["a"], dx2, ta=True, name="mm_down_dw")
    dug, duv, dcwg, dcwv, dcbg, dcbv = conv_glu_bwd(s["u"], p["conv_w"], p["conv_b"][None], da, name="conv_glu_bwd")
    du = jnp.concatenate([dug, duv], axis=1)
    g["conv_w"] = jnp.concatenate([dcwg, dcwv], axis=1)
    g["conv_b"] = jnp.concatenate([dcbg, dcbv], axis=1)[0]
    g["w_up"] = matmul(s["h2"], du, ta=True, name="mm_up_dw")
    dh2 = matmul(du, full["w_up"], b_layer=l, tb=True, name="mm_up_dx")
    dx1, dg2 = rmsnorm_bwd(s["x1"], p["norm_ffn"][None], dh2, dx2, name="rmsnorm_bwd")
    g["norm_ffn"] = dg2[0]
    dmerged = matmul(dx1, full["w_out"], b_layer=l, tb=True, name="mm_out_dx")
    g["w_out"] = matmul(s["merged"], dx1, ta=True, name="mm_out_dw")
    t_a, t_b, t_c, dga, dgb, dgc, dba, dbb, dbc = merge_bwd(
        dmerged, s["oa"], s["ob"], s["oc"], s["proj"], p["b_gate"][None], p["w_branch_a"], p["w_branch_b"],
        p["w_branch_c"], name="merge_bwd")
    g["b_gate"] = jnp.concatenate([dba, dbb, dbc], axis=1)[0]
    g["w_branch_a"] = matmul(s["oa"], t_a, ta=True, name="mm_branch_dw")
    g["w_branch_b"] = matmul(s["ob"], t_b, ta=True, name="mm_branch_dw")
    g["w_branch_c"] = matmul(s["oc"], t_c, ta=True, name="mm_branch_dw")
    doa = matmul(t_a, full["w_branch_a"], b_layer=l, tb=True, out_dtype=BF16, name="mm_branch_dx")
    dob = matmul(t_b, full["w_branch_b"], b_layer=l, tb=True, out_dtype=BF16, name="mm_branch_dx")
    doc = matmul(t_c, full["w_branch_c"], b_layer=l, tb=True, name="mm_branch_dx_f32")
    dqh, dkh, dva, dbias = attn_a_bwd(s["qa"], s["ka"], s["va"], s["biasm"], doa, name="attn_a_bwd")
    ddiag = relbias_reduce(dbias, name="relbias_reduce")
    ddiag = ddiag.transpose(1, 0, 2, 3).reshape(N_HEADS, N_VARIANTS * DIAG_W)
    g["rel_bias_a"] = exact_dot(ddiag, tables["onehot"], name="relbias_table")
    dqa, dka, dgq8, dgk8 = qknorm_bwd(s["proj"], s["gq8"], s["gk8"], dqh, dkh, name="qknorm_bwd")
    g["q_norm_a"] = dgq8.reshape(N_HEADS, HEAD_DIM).sum(axis=0)
    g["k_norm_a"] = dgk8.reshape(N_HEADS, HEAD_DIM).sum(axis=0)
    received = None
    if scatter is None:
        dqb, dkb, dvb = attn_b_bwd(s["qb"], s["kb"], s["vb"], s["tot"], s["nblk"], dob, name="attn_b_bwd")
    else:
        dqb, dkb, dvb, received = attn_b_bwd(s["qb"], s["kb"], s["vb"], s["tot"], s["nblk"], dob, scatter=scatter,
                                             name="attn_b_bwd_scatter")
    duc, dwp, dsc = pool_bwd(s["proj"], s["wpool"], p["pool_scale"][None], doc, name="pool_bwd")
    g["w_pool"] = dwp
    g["pool_scale"] = dsc[0]
    dproj = jnp.concatenate([dqa, dka, dva.astype(BF16), dqb, dkb.astype(BF16), dvb.astype(BF16), duc,
                             dga, dgb, dgc], axis=1)
    g["w_in"] = matmul(s["h"], dproj, ta=True, name="mm_in_dw")
    dh = matmul(dproj, full["w_in"], b_layer=l, tb=True, name="mm_in_dx")
    dx, dg1 = rmsnorm_bwd(s["x"], p["norm_mix"][None], dh, dx1, name="rmsnorm_bwd")
    g["norm_mix"] = dg1[0]
    return dx, g, received


def kernel(x, norm_mix, w_in, b_gate, q_norm_a, k_norm_a, rel_bias_a, w_pool, pool_scale, w_branch_a, w_branch_b, w_branch_c, w_out, norm_ffn, w_up, conv_w, conv_b, w_down, loss_target, m_norm_mix, m_w_in, m_b_gate, m_q_norm_a, m_k_norm_a, m_rel_bias_a, m_w_pool, m_pool_scale, m_w_branch_a, m_w_branch_b, m_w_branch_c, m_w_out, m_norm_ffn, m_w_up, m_conv_w, m_conv_b, m_w_down, v_norm_mix, v_w_in, v_b_gate, v_q_norm_a, v_k_norm_a, v_rel_bias_a, v_w_pool, v_pool_scale, v_w_branch_a, v_w_branch_b, v_w_branch_c, v_w_out, v_norm_ffn, v_w_up, v_conv_w, v_conv_b, v_w_down):
    w = dict(zip(WEIGHTS, (norm_mix, w_in, b_gate, q_norm_a, k_norm_a, rel_bias_a, w_pool, pool_scale, w_branch_a,
                           w_branch_b, w_branch_c, w_out, norm_ffn, w_up, conv_w, conv_b, w_down)))
    m = dict(zip(WEIGHTS, (m_norm_mix, m_w_in, m_b_gate, m_q_norm_a, m_k_norm_a, m_rel_bias_a, m_w_pool, m_pool_scale,
                           m_w_branch_a, m_w_branch_b, m_w_branch_c, m_w_out, m_norm_ffn, m_w_up, m_conv_w, m_conv_b,
                           m_w_down)))
    v = dict(zip(WEIGHTS, (v_norm_mix, v_w_in, v_b_gate, v_q_norm_a, v_k_norm_a, v_rel_bias_a, v_w_pool, v_pool_scale,
                           v_w_branch_a, v_w_branch_b, v_w_branch_c, v_w_out, v_norm_ffn, v_w_up, v_conv_w, v_conv_b,
                           v_w_down)))
    onehot = diagonal_onehot()
    tables = dict(onehot=jnp.asarray(onehot), onehot_t=jnp.asarray(np.ascontiguousarray(onehot.T)))

    names = tuple(SHARDED)
    shards = [w[n] if n == "conv_w" else w[n].astype(BF16) for n in names]
    axes = [SHARDED[n] for n in names]
    full = dict(zip(names, all_gather_layer(shards, axes, 0, name="all_gather_layer")))

    def layer_params(l):
        p = {n: full[n][l] for n in ("w_branch_a", "w_branch_b", "w_branch_c", "conv_w")}
        p.update({n: w[n][l] for n in REPLICATED})
        p.update(full=full, l=l)
        return p

    xs = x[0]
    saved = []
    for l in range(DEPTH):
        prefetch = (LayerGather(shards, axes, l + 1), shards, names) if l + 1 < DEPTH else None
        xs, s, full = _layer_fwd(xs, layer_params(l), tables, prefetch)
        saved.append(s)
    dx, lpart = loss_head(xs, loss_target[0], name="loss_head")
    loss = lax.psum(lpart[0, 0], MESH_AXES)
    as_index = lambda i: jnp.reshape(i, (1,)).astype(jnp.int32)
    cx, cy, cc = _mesh_pos()
    core, chip, me = as_index(cc), as_index(2 * cx + cy), as_index(4 * cx + 2 * cy + cc)
    layouts = [HalfLayout((full[n].shape[1], full[n].shape[2]), SHARDED[n] - 1) for n in MATMUL_WEIGHTS]
    nt = len(layouts)
    grads, pairs, received = [None] * DEPTH, [None] * DEPTH, [None] * DEPTH
    pending = None
    for l in reversed(range(DEPTH)):
        dx, grads[l], got = _layer_bwd(dx, saved[l], layer_params(l), tables, pending)
        if pending is not None:
            received[l + 1] = got
        mine = [grads[l][n] for n in MATMUL_WEIGHTS]
        others = half_exchange(mine, layouts, name="half_exchange")
        pairs[l] = [pair_sum(gm, go, lay, core, name="pair_sum") for gm, go, lay in zip(mine, others, layouts)]
        pending = (BlockScatter(layouts), [p[1] for p in pairs[l]]) if l > 0 else None

    g = {n: jnp.stack([grads[l][n] for l in range(DEPTH)]) for n in SMALL_WEIGHTS}
    flat = jnp.concatenate([g[n].reshape(-1) for n in SMALL_WEIGHTS])
    small = jnp.pad(flat, (0, SMALL_ROWS * 128 - flat.shape[0])).reshape(SMALL_ROWS, 128)
    outs = scatter_blocks([p[1] for p in pairs[0]], layouts, small, name="scatter_blocks")
    received[0] = list(outs[:nt])
    small_sum = sum_devices(outs[-1], small, me, name="sum_devices").reshape(-1)
    finals = [[sum_chips(p[0], r, lay, chip, name="sum_chips") for p, r, lay in zip(pairs[l], received[l], layouts)]
              for l in range(DEPTH)]
    swapped = sibling_swap([f for fl in finals for f in fl], name="sibling_swap")
    finals_other = [swapped[l * nt:(l + 1) * nt] for l in range(DEPTH)]

    res = {}
    for t, (n, lay) in enumerate(zip(MATMUL_WEIGHTS, layouts)):
        res[n] = adamw_halves(w[n], m[n], v[n], [finals[l][t] for l in range(DEPTH)],
                              [finals_other[l][t] for l in range(DEPTH)], lay, core, name="adamw_halves")
    off = 0
    for n in SMALL_WEIGHTS:
        shp = g[n].shape
        size = int(np.prod(shp))
        gn = small_sum[off:off + size].reshape(shp)
        off += size
        if n in SHARDED:
            gn = lax.dynamic_slice_in_dim(gn, (2 * cx + cy) * w[n].shape[-1], w[n].shape[-1], axis=len(shp) - 1)
        shp = w[n].shape
        cols = shp[-1]
        two_d = lambda t: t.reshape(int(np.prod(shp)) // cols, cols)
        res[n] = [t.reshape(shp) for t in adamw(two_d(w[n]), two_d(m[n]), two_d(v[n]), two_d(gn), name="adamw")]

    out = [loss, dx[None]]
    for k in range(4):
        out.extend(res[n][k] for n in WEIGHTS)
    return tuple(out)
```

```python
import jax
import jax.numpy as jnp
import numpy as np
from jax import lax
from jax.experimental import pallas as pl
from jax.experimental.pallas import tpu as pltpu

F32 = jnp.float32
BF16 = jnp.bfloat16

D_MODEL = 1024
DEPTH = 2
CHUNK = 64
N_LEFT = 8
HEAD_DIM = 64
N_HEADS = 8
WIDTH = 512
POOL_WINDOWS = (2, 4, 8, 16)
GROUP_DIM = 128
MAX_REL = 2 * CHUNK
REL_TABLE = MAX_REL + CHUNK
D_FF = 2816
EPS = 1e-6
QK_SCALE = 0.125
IN_COLS = 7 * WIDTH + 3 * D_MODEL
GATE_COL0 = 7 * WIDTH

ADAM_LR = 0.001
ADAM_B1 = 0.9
ADAM_B2 = 0.999
ADAM_EPS = 1e-08
ADAM_WD = 0.01
ADAM_STEP = 10

VMEM_LIMIT = 56 * 1024 * 1024
ATT_Q = 256
A_Q = 256
A_WIN = A_Q + N_LEFT * CHUNK
HALO = 16
CONV_HALO = 8
NEG = -1e30

MESH_AXES = ("x", "y", "c")
MESH_T = pl.DeviceIdType.MESH


def _cp(sem=None, vmem=VMEM_LIMIT):
    return pltpu.CompilerParams(dimension_semantics=sem, vmem_limit_bytes=vmem)


def _dot(a, b, ca, cb):
    return lax.dot_general(a, b, (((ca,), (cb,)), ((), ())), preferred_element_type=F32)


def _tile(n, cands=(512, 256, 128)):
    for c in cands:
        if n % c == 0:
            return c
    return n


def _split_hi_lo(v):
    hi = v.astype(BF16)
    lo = (v - hi.astype(F32)).astype(BF16)
    return hi, lo


def matmul(a, b, *, ta=False, tb=False, add=None, out_dtype=F32, b_layer=None, name):
    if ta:
        K, M = a.shape
    else:
        M, K = a.shape
    if tb:
        N, K2 = b.shape[-2:]
    else:
        K2, N = b.shape[-2:]
    assert K == K2, (a.shape, b.shape, ta, tb)
    big = (1024, 1408, 512, 256, 128)
    tm, tn, tk = _tile(M, big), _tile(N, big), _tile(K, big)
    nk = K // tk

    def body(*refs):
        a_ref, b_ref = refs[:2]
        r_ref = refs[2] if add is not None else None
        o_ref, acc = refs[-2:]
        k = pl.program_id(2)

        @pl.when(k == 0)
        def _():
            acc[...] = jnp.zeros_like(acc)

        av = a_ref[...].astype(BF16)
        bv = b_ref[...].astype(BF16)
        acc[...] += _dot(av, bv, 0 if ta else 1, 1 if tb else 0)

        @pl.when(k == nk - 1)
        def _():
            r = acc[...]
            if add is not None:
                r = r + r_ref[...].astype(F32)
            o_ref[...] = r.astype(out_dtype)

    a_spec = pl.BlockSpec((tk, tm), lambda i, j, k: (k, i)) if ta else pl.BlockSpec((tm, tk), lambda i, j, k: (i, k))
    if b_layer is None:
        b_spec = pl.BlockSpec((tn, tk), lambda i, j, k: (j, k)) if tb else pl.BlockSpec((tk, tn), lambda i, j, k: (k, j))
    elif tb:
        b_spec = pl.BlockSpec((None, tn, tk), lambda i, j, k: (b_layer, j, k))
    else:
        b_spec = pl.BlockSpec((None, tk, tn), lambda i, j, k: (b_layer, k, j))
    in_specs = [a_spec, b_spec]
    args = [a, b]
    if add is not None:
        in_specs.append(pl.BlockSpec((tm, tn), lambda i, j, k: (i, j)))
        args.append(add)
    return pl.pallas_call(
        body,
        out_shape=jax.ShapeDtypeStruct((M, N), out_dtype),
        grid=(M // tm, N // tn, nk),
        in_specs=in_specs,
        out_specs=pl.BlockSpec((tm, tn), lambda i, j, k: (i, j)),
        scratch_shapes=[pltpu.VMEM((tm, tn), F32)],
        compiler_params=_cp(("parallel", "parallel", "arbitrary")),
        name=name,
    )(*args)


def rmsnorm_fwd(x, g, *, name):
    S, D = x.shape
    T = _tile(S)

    def body(x_ref, g_ref, h_ref):
        xv = x_ref[...]
        r = lax.rsqrt(jnp.mean(xv * xv, axis=-1, keepdims=True) + EPS)
        h_ref[...] = (xv * r * g_ref[...]).astype(BF16)

    return pl.pallas_call(
        body,
        out_shape=jax.ShapeDtypeStruct((S, D), BF16),
        grid=(S // T,),
        in_specs=[pl.BlockSpec((T, D), lambda i: (i, 0)), pl.BlockSpec((1, D), lambda i: (0, 0))],
        out_specs=pl.BlockSpec((T, D), lambda i: (i, 0)),
        compiler_params=_cp(("parallel",)),
        name=name,
    )(x, g)


def rmsnorm_bwd(x, g, dh, dres, *, name):
    S, D = x.shape
    T = _tile(S)

    def body(x_ref, g_ref, dh_ref, dres_ref, dx_ref, dg_ref):
        i = pl.program_id(0)
        xv = x_ref[...]
        dhv = dh_ref[...].astype(F32)
        r = lax.rsqrt(jnp.mean(xv * xv, axis=-1, keepdims=True) + EPS)
        gd = dhv * g_ref[...]
        m = jnp.mean(xv * gd, axis=-1, keepdims=True)
        dx_ref[...] = dres_ref[...] + r * gd - xv * (r * r * r * m)

        @pl.when(i == 0)
        def _():
            dg_ref[...] = jnp.zeros_like(dg_ref)

        dg_ref[...] += jnp.sum(dhv * xv * r, axis=0, keepdims=True)

    row = pl.BlockSpec((T, D), lambda i: (i, 0))
    vec = pl.BlockSpec((1, D), lambda i: (0, 0))
    return pl.pallas_call(
        body,
        out_shape=(jax.ShapeDtypeStruct((S, D), F32), jax.ShapeDtypeStruct((1, D), F32)),
        grid=(S // T,),
        in_specs=[row, vec, row, row],
        out_specs=(row, vec),
        compiler_params=_cp(("arbitrary",)),
        name=name,
    )(x, g, dh, dres)


def _head_mean_matrix():
    r = lax.broadcasted_iota(jnp.int32, (WIDTH, WIDTH), 0) // HEAD_DIM
    c = lax.broadcasted_iota(jnp.int32, (WIDTH, WIDTH), 1) // HEAD_DIM
    return jnp.where(r == c, 1.0 / HEAD_DIM, 0.0).astype(BF16)


def _head_mean(v, mm):
    hi, lo = _split_hi_lo(v)
    return _dot(hi, mm, 1, 0) + _dot(lo, mm, 1, 0)


def qkv_prep(proj, gq, gk, *, name):
    S = proj.shape[0]
    T = _tile(S)

    def body(qa, ka, va, qb, kb, vb, gq_ref, gk_ref, oqa, oka, ova, oqb, okb, ovb):
        mm = _head_mean_matrix()
        for src, gref, dst, scale in ((qa, gq_ref, oqa, QK_SCALE), (ka, gk_ref, oka, 1.0)):
            v = src[...]
            r = lax.rsqrt(_head_mean(v * v, mm) + EPS)
            dst[...] = (v * r * gref[...] * scale).astype(BF16)
        oqb[...] = (qb[...] * QK_SCALE).astype(BF16)
        for src, dst in ((va, ova), (kb, okb), (vb, ovb)):
            dst[...] = src[...].astype(BF16)

    col = lambda j: pl.BlockSpec((T, WIDTH), lambda i, j=j: (i, j))
    vec = pl.BlockSpec((1, WIDTH), lambda i: (0, 0))
    out = pl.BlockSpec((T, WIDTH), lambda i: (i, 0))
    return pl.pallas_call(
        body,
        out_shape=tuple(jax.ShapeDtypeStruct((S, WIDTH), BF16) for _ in range(6)),
        grid=(S // T,),
        in_specs=[col(0), col(1), col(2), col(3), col(4), col(5), vec, vec],
        out_specs=tuple(out for _ in range(6)),
        compiler_params=_cp(("parallel",)),
        name=name,
    )(proj, proj, proj, proj, proj, proj, gq, gk)


def qknorm_bwd(proj, gq, gk, dqh, dkh, *, name):
    S = proj.shape[0]
    T = _tile(S)

    def body(qa, ka, gq_ref, gk_ref, dq_ref, dk_ref, oq, ok, ogq, ogk):
        i = pl.program_id(0)
        mm = _head_mean_matrix()

        @pl.when(i == 0)
        def _():
            ogq[...] = jnp.zeros_like(ogq)
            ogk[...] = jnp.zeros_like(ogk)

        for src, gref, dref, dst, gdst in ((qa, gq_ref, dq_ref, oq, ogq), (ka, gk_ref, dk_ref, ok, ogk)):
            v = src[...]
            dy = dref[...]
            r = lax.rsqrt(_head_mean(v * v, mm) + EPS)
            gd = dy * gref[...]
            m = _head_mean(v * gd, mm)
            dst[...] = (r * gd - v * (r * r * r * m)).astype(BF16)
            gdst[...] += jnp.sum(dy * v * r, axis=0, keepdims=True)

    col = lambda j: pl.BlockSpec((T, WIDTH), lambda i, j=j: (i, j))
    vec = pl.BlockSpec((1, WIDTH), lambda i: (0, 0))
    row = pl.BlockSpec((T, WIDTH), lambda i: (i, 0))
    return pl.pallas_call(
        body,
        out_shape=(jax.ShapeDtypeStruct((S, WIDTH), BF16), jax.ShapeDtypeStruct((S, WIDTH), BF16),
                   jax.ShapeDtypeStruct((1, WIDTH), F32), jax.ShapeDtypeStruct((1, WIDTH), F32)),
        grid=(S // T,),
        in_specs=[col(0), col(1), vec, vec, row, row],
        out_specs=(row, row, vec, vec),
        compiler_params=_cp(("arbitrary",)),
        name=name,
    )(proj, proj, gq, gk, dqh, dkh)


DIAG_W = 1024
N_VARIANTS = N_LEFT * CHUNK // A_Q + 1


def diagonal_onehot():
    jj = np.arange(DIAG_W)
    diff = np.where(jj < A_WIN, jj, jj - DIAG_W)
    out = np.zeros((N_VARIANTS, DIAG_W, REL_TABLE), np.float32)
    for v in range(N_VARIANTS):
        rel = np.clip(A_Q * v - diff, -(CHUNK - 1), MAX_REL) + (CHUNK - 1)
        out[v, jj, rel] = 1.0
    return out.reshape(N_VARIANTS * DIAG_W, REL_TABLE)


def exact_dot(a, b, *, name):
    def body(a_ref, b_ref, o_ref):
        o_ref[...] = jnp.dot(a_ref[...], b_ref[...], precision=lax.Precision.HIGHEST, preferred_element_type=F32)

    return pl.pallas_call(body, out_shape=jax.ShapeDtypeStruct((a.shape[0], b.shape[1]), F32),
                          compiler_params=_cp(), name=name)(a, b)


def _band_valid(v):
    qc = (lax.broadcasted_iota(jnp.int32, (A_Q, A_WIN), 0) + A_Q * v) // CHUNK
    kc = lax.broadcasted_iota(jnp.int32, (A_Q, A_WIN), 1) // CHUNK
    return (kc <= qc) & (kc >= qc - N_LEFT)


def bias_expand(diag, *, name):
    def body(d_ref, o_ref):
        rows = jnp.broadcast_to(d_ref[0, 0], (A_Q, DIAG_W))
        skew = pltpu.roll(rows, 0, 1, stride=1, stride_axis=0)
        o_ref[0, 0] = jnp.where(_band_valid(pl.program_id(0)), skew[:, :A_WIN], NEG)

    return pl.pallas_call(
        body,
        out_shape=jax.ShapeDtypeStruct((N_VARIANTS, N_HEADS, A_Q, A_WIN), F32),
        grid=(N_VARIANTS, N_HEADS),
        in_specs=[pl.BlockSpec((1, 1, 1, DIAG_W), lambda v, h: (v, h, 0, 0))],
        out_specs=pl.BlockSpec((1, 1, A_Q, A_WIN), lambda v, h: (v, h, 0, 0)),
        compiler_params=_cp(("parallel", "parallel")),
        name=name,
    )(diag)


def relbias_reduce(dbias, *, name):
    def body(db_ref, o_ref):
        acc = None
        for a in range(A_Q // 8):
            x = jnp.concatenate([db_ref[0, 0, 8 * a:8 * a + 8, :], jnp.zeros((8, DIAG_W - A_WIN), F32)], axis=1)
            x = pltpu.roll(x, DIAG_W - 8 * a, 1) if a else x
            acc = x if acc is None else acc + x
        row = lax.broadcasted_iota(jnp.int32, (8, DIAG_W), 0)
        for b in range(3):
            acc = jnp.where((row >> b) & 1 == 1, pltpu.roll(acc, DIAG_W - (1 << b), 1), acc)
        o_ref[0, 0] = jnp.sum(acc, axis=0, keepdims=True)

    return pl.pallas_call(
        body,
        out_shape=jax.ShapeDtypeStruct((N_VARIANTS, N_HEADS, 1, DIAG_W), F32),
        grid=(N_VARIANTS, N_HEADS),
        in_specs=[pl.BlockSpec((1, 1, A_Q, A_WIN), lambda v, h: (v, h, 0, 0))],
        out_specs=pl.BlockSpec((1, 1, 1, DIAG_W), lambda v, h: (v, h, 0, 0)),
        compiler_params=_cp(("parallel", "parallel")),
        name=name,
    )(dbias)


def _a_window_start(qb):
    return pl.multiple_of(jnp.maximum(qb * A_Q - N_LEFT * CHUNK, 0), A_Q)


def attn_a_fwd(q, k, v, biasm, *, name):
    S = q.shape[0]
    nq = S // A_Q

    def body(q_ref, k_ref, v_ref, b_ref, o_ref):
        qb = pl.program_id(1)
        start = _a_window_start(qb)
        outs = []
        for h in range(2):
            lanes = slice(h * HEAD_DIM, (h + 1) * HEAD_DIM)
            qh = q_ref[:, lanes]
            kw = k_ref[pl.ds(start, A_WIN), lanes]
            vw = v_ref[pl.ds(start, A_WIN), lanes]
            s = _dot(qh, kw, 1, 1) + b_ref[0, h]
            m = jnp.max(s, axis=-1, keepdims=True)
            e = jnp.exp(s - m)
            outs.append(_dot(e.astype(BF16), vw, 1, 0) * (1.0 / jnp.sum(e, axis=-1, keepdims=True)))
        o_ref[...] = jnp.concatenate(outs, axis=1).astype(BF16)

    qspec = pl.BlockSpec((A_Q, 2 * HEAD_DIM), lambda hp, qb: (qb, hp))
    kvspec = pl.BlockSpec((S, 2 * HEAD_DIM), lambda hp, qb: (0, hp))
    bspec = pl.BlockSpec((1, 2, A_Q, A_WIN), lambda hp, qb: (jnp.minimum(qb, N_VARIANTS - 1), hp, 0, 0))
    return pl.pallas_call(
        body,
        out_shape=jax.ShapeDtypeStruct((S, WIDTH), BF16),
        grid=(N_HEADS // 2, nq),
        in_specs=[qspec, kvspec, kvspec, bspec],
        out_specs=qspec,
        compiler_params=_cp(("parallel", "arbitrary")),
        name=name,
    )(q, k, v, biasm)


def attn_a_bwd(q, k, v, biasm, do, *, name):
    S = q.shape[0]
    nq = S // A_Q

    def body(q_ref, k_ref, v_ref, b_ref, do_ref, dq_ref, dk_ref, dv_ref, db_ref):
        qb = pl.program_id(1)
        start = _a_window_start(qb)

        @pl.when(qb == 0)
        def _():
            dk_ref[...] = jnp.zeros_like(dk_ref)
            dv_ref[...] = jnp.zeros_like(dv_ref)

        @pl.when(qb < N_VARIANTS)
        def _():
            db_ref[...] = jnp.zeros_like(db_ref)

        dqs = []
        for h in range(2):
            lanes = slice(h * HEAD_DIM, (h + 1) * HEAD_DIM)
            qh = q_ref[:, lanes]
            doh = do_ref[:, lanes]
            kw = k_ref[pl.ds(start, A_WIN), lanes]
            vw = v_ref[pl.ds(start, A_WIN), lanes]
            s = _dot(qh, kw, 1, 1) + b_ref[0, h]
            m = jnp.max(s, axis=-1, keepdims=True)
            e = jnp.exp(s - m)
            p = e * (1.0 / jnp.sum(e, axis=-1, keepdims=True))
            dp = _dot(doh, vw, 1, 1)
            delta = jnp.sum(p * dp, axis=-1, keepdims=True)
            ds = p * (dp - delta)
            db_ref[0, h] += ds
            dsb = ds.astype(BF16)
            dqs.append(_dot(dsb, kw, 1, 0) * QK_SCALE)
            dk_ref[pl.ds(start, A_WIN), lanes] += _dot(dsb, qh, 0, 0)
            dv_ref[pl.ds(start, A_WIN), lanes] += _dot(p.astype(BF16), doh, 0, 0)
        dq_ref[...] = jnp.concatenate(dqs, axis=1)

    qspec = pl.BlockSpec((A_Q, 2 * HEAD_DIM), lambda hp, qb: (qb, hp))
    kvspec = pl.BlockSpec((S, 2 * HEAD_DIM), lambda hp, qb: (0, hp))
    bspec = pl.BlockSpec((1, 2, A_Q, A_WIN), lambda hp, qb: (jnp.minimum(qb, N_VARIANTS - 1), hp, 0, 0))
    return pl.pallas_call(
        body,
        out_shape=(jax.ShapeDtypeStruct((S, WIDTH), F32), jax.ShapeDtypeStruct((S, WIDTH), F32),
                   jax.ShapeDtypeStruct((S, WIDTH), F32), jax.ShapeDtypeStruct((N_VARIANTS, N_HEADS, A_Q, A_WIN), F32)),
        grid=(N_HEADS // 2, nq),
        in_specs=[qspec, kvspec, kvspec, bspec, qspec],
        out_specs=(qspec, kvspec, kvspec, bspec),
        compiler_params=_cp(("parallel", "arbitrary")),
        name=name,
    )(q, k, v, biasm, do)


def _tri(kind):
    j = lax.broadcasted_iota(jnp.int32, (ATT_Q, ATT_Q), 0)
    s = lax.broadcasted_iota(jnp.int32, (ATT_Q, ATT_Q), 1)
    if kind == "gt":
        m = j > s
    elif kind == "le":
        m = j <= s
    else:
        m = j < s
    return jnp.where(m, 1.0, 0.0).astype(BF16)


def _cum(v, tri):
    hi, lo = _split_hi_lo(v)
    return _dot(hi, tri, 1, 0) + _dot(lo, tri, 1, 0)


def _log_sigmoids(z, mask):
    t = jnp.log(1.0 + jnp.exp(-jnp.abs(z)))
    keep = -(jnp.maximum(z, 0.0) + t)
    take = jnp.minimum(z, 0.0) - t
    return (keep if mask is None else jnp.where(mask, keep, 0.0)), take


def _strictly_before():
    row = lax.broadcasted_iota(jnp.int32, (ATT_Q, ATT_Q), 0)
    col = lax.broadcasted_iota(jnp.int32, (ATT_Q, ATT_Q), 1)
    return col < row


EXIT_LOG = -104.0


def attn_b_fwd(q, k, v, *, gather=None, name):
    S = q.shape[0]
    nq = S // ATT_Q
    plan, shards, fulls = gather if gather is not None else (None, [], [])
    ng = len(shards)

    def body(q_ref, k_ref, v_ref, *rest):
        hp = pl.program_id(0)
        qb = pl.program_id(1)
        o_ref, t_ref, n_ref = rest[2 * ng:2 * ng + 3]
        if plan is not None:
            comm = (rest[:ng], rest[2 * ng + 3:3 * ng + 3], rest[3 * ng + 3:])

            @pl.when(jnp.logical_and(hp == 0, qb == 0))
            def _():
                plan.start(*comm)

        tri = _tri("gt")

        def block(kb, carry, mask):
            ks = pl.multiple_of(kb * ATT_Q, ATT_Q)
            new = []
            for h in range(2):
                lanes = slice(h * HEAD_DIM, (h + 1) * HEAD_DIM)
                c, acc = carry[h]
                z = _dot(q_ref[:, lanes], k_ref[pl.ds(ks, ATT_Q), lanes], 1, 1)
                keep, take = _log_sigmoids(z, mask)
                w = jnp.exp(take + (_cum(keep, tri) + c))
                if mask is not None:
                    w = jnp.where(mask, w, 0.0)
                acc = acc + _dot(w.astype(BF16), v_ref[pl.ds(ks, ATT_Q), lanes], 1, 0)
                c = c + jnp.sum(keep, axis=-1, keepdims=True)
                new.append((c, acc))
            return jnp.maximum(jnp.max(new[0][0]), jnp.max(new[1][0])), tuple(new)

        def cond(state):
            it, cmax, _ = state
            return jnp.logical_and(it <= qb, cmax >= EXIT_LOG)

        def step(state):
            it, _, carry = state
            cmax, carry = block(qb - it, carry, None)
            return it + 1, cmax, carry

        init = tuple((jnp.zeros((ATT_Q, 1), F32), jnp.zeros((ATT_Q, HEAD_DIM), F32)) for _ in range(2))
        cmax, diag = block(qb, init, _strictly_before())
        visited, _, res = lax.while_loop(cond, step, (jnp.int32(1), cmax, diag))
        o_ref[...] = jnp.concatenate([res[0][1], res[1][1]], axis=1).astype(BF16)
        t_ref[...] = jnp.concatenate([jnp.broadcast_to(res[h][0], (ATT_Q, HEAD_DIM)) for h in range(2)], axis=1)
        n_ref[hp, qb] = visited.astype(F32)
        if plan is not None:
            @pl.when(jnp.logical_and(hp == N_HEADS // 2 - 1, qb == nq - 1))
            def _():
                plan.finish(*comm)

    qspec = pl.BlockSpec((ATT_Q, 2 * HEAD_DIM), lambda hp, qb: (qb, hp))
    kvspec = pl.BlockSpec((S, 2 * HEAD_DIM), lambda hp, qb: (0, hp))
    outs = pl.pallas_call(
        body,
        out_shape=(jax.ShapeDtypeStruct((S, WIDTH), BF16), jax.ShapeDtypeStruct((S, WIDTH), F32),
                   jax.ShapeDtypeStruct((N_HEADS // 2, nq), F32))
        + tuple(jax.ShapeDtypeStruct(f.shape, f.dtype) for f in fulls),
        grid=(N_HEADS // 2, nq),
        in_specs=[qspec, kvspec, kvspec] + [ANY] * (2 * ng),
        out_specs=(qspec, qspec, pl.BlockSpec(memory_space=pltpu.SMEM)) + tuple([ANY] * ng),
        scratch_shapes=plan.scratch_shapes() if plan is not None else (),
        input_output_aliases={3 + ng + i: 3 + i for i in range(ng)},
        compiler_params=_cp(("arbitrary", "arbitrary")),
        name=name,
    )(q, k, v, *shards, *fulls)
    return outs if plan is None else (outs[0], outs[1], outs[2], list(outs[3:]))


def attn_b_bwd(q, k, v, tot, nblk, do, *, scatter=None, name):
    S = q.shape[0]
    nq = S // ATT_Q
    plan, pairs16 = scatter if scatter is not None else (None, [])
    ns = len(pairs16)

    def body(q_ref, k_ref, v_ref, t_ref, n_ref, do_ref, *rest):
        hp = pl.program_id(0)
        qb = pl.program_id(1)
        dq_ref, dk_ref, dv_ref = rest[ns:ns + 3]
        if plan is not None:
            comm = (rest[:ns], rest[ns + 3:2 * ns + 3], rest[2 * ns + 3:])

            @pl.when(jnp.logical_and(hp == 0, qb == 0))
            def _():
                plan.start(*comm)

        first = jnp.clip(qb + 1 - n_ref[hp, qb].astype(jnp.int32), 0, qb + 1)
        tri_le = _tri("le")
        tri_lt = _tri("lt")

        @pl.when(qb == 0)
        def _():
            dk_ref[...] = jnp.zeros_like(dk_ref)
            dv_ref[...] = jnp.zeros_like(dv_ref)

        def block(kb, carry, mask):
            ks = pl.multiple_of(kb * ATT_Q, ATT_Q)
            new = []
            for h in range(2):
                lanes = slice(h * HEAD_DIM, (h + 1) * HEAD_DIM)
                cl, cg, dq = carry[h]
                qh = q_ref[:, lanes]
                doh = do_ref[:, lanes]
                kh = k_ref[pl.ds(ks, ATT_Q), lanes]
                vh = v_ref[pl.ds(ks, ATT_Q), lanes]
                totl = t_ref[:, h * HEAD_DIM:h * HEAD_DIM + 1]
                z = _dot(qh, kh, 1, 1)
                keep, take = _log_sigmoids(z, mask)
                sig = jnp.exp(take)
                w = sig * jnp.exp((totl - cl) - _cum(keep, tri_le))
                if mask is not None:
                    w = jnp.where(mask, w, 0.0)
                g = w * _dot(doh, vh, 1, 1)
                G = _dot(g.astype(BF16), tri_lt, 1, 0) + cg
                dz = g * (1.0 - sig) - sig * G
                if mask is not None:
                    dz = jnp.where(mask, dz, 0.0)
                dz = dz.astype(BF16)
                dq = dq + _dot(dz, kh, 1, 0)
                dk_ref[pl.ds(ks, ATT_Q), lanes] += _dot(dz, qh, 0, 0)
                dv_ref[pl.ds(ks, ATT_Q), lanes] += _dot(w.astype(BF16), doh, 0, 0)
                cl = cl + jnp.sum(keep, axis=-1, keepdims=True)
                cg = cg + jnp.sum(g, axis=-1, keepdims=True)
                new.append((cl, cg, dq))
            return tuple(new)

        init = tuple((jnp.zeros((ATT_Q, 1), F32), jnp.zeros((ATT_Q, 1), F32), jnp.zeros((ATT_Q, HEAD_DIM), F32))
                     for _ in range(2))
        res = lax.fori_loop(jnp.minimum(first, qb), qb, lambda kb, carry: block(kb, carry, None), init)
        res = block(qb, res, _strictly_before())
        dq_ref[...] = (jnp.concatenate([res[0][2], res[1][2]], axis=1) * QK_SCALE).astype(BF16)
        if plan is not None:
            @pl.when(jnp.logical_and(hp == N_HEADS // 2 - 1, qb == nq - 1))
            def _():
                plan.finish(*comm)

    qspec = pl.BlockSpec((ATT_Q, 2 * HEAD_DIM), lambda hp, qb: (qb, hp))
    kvspec = pl.BlockSpec((S, 2 * HEAD_DIM), lambda hp, qb: (0, hp))
    outs = pl.pallas_call(
        body,
        out_shape=(jax.ShapeDtypeStruct((S, WIDTH), BF16), jax.ShapeDtypeStruct((S, WIDTH), F32),
                   jax.ShapeDtypeStruct((S, WIDTH), F32)) + (tuple(plan.recv_shapes()) if plan is not None else ()),
        grid=(N_HEADS // 2, nq),
        in_specs=[qspec, kvspec, kvspec, qspec, pl.BlockSpec(memory_space=pltpu.SMEM), qspec] + [ANY] * ns,
        out_specs=(qspec, kvspec, kvspec) + tuple([ANY] * ns),
        scratch_shapes=plan.scratch_shapes() if plan is not None else (),
        compiler_params=_cp(("arbitrary", "arbitrary")),
        name=name,
    )(q, k, v, tot, nblk, do, *pairs16)
    return outs if plan is None else (outs[0], outs[1], outs[2], list(outs[3:]))


U_COLBLK = 6


def _pool_counts(t0, rows):
    t = t0 + lax.broadcasted_iota(jnp.int32, (rows, WIDTH), 0)
    lane_grp = lax.broadcasted_iota(jnp.int32, (rows, WIDTH), 1) // GROUP_DIM
    win = jnp.where(lane_grp == 0, 2, jnp.where(lane_grp == 1, 4, jnp.where(lane_grp == 2, 8, 16)))
    cnt = jnp.minimum(t + 1, win)
    return 1.0 / cnt.astype(F32), lane_grp


def _window_sums(ext, shift_fn):
    s2 = ext + shift_fn(ext, 1)
    s4 = s2 + shift_fn(s2, 2)
    s8 = s4 + shift_fn(s4, 4)
    s16 = s8 + shift_fn(s8, 8)
    return s2, s4, s8, s16


def _select_group(lane_grp, s2, s4, s8, s16):
    return jnp.where(lane_grp == 0, s2, jnp.where(lane_grp == 1, s4, jnp.where(lane_grp == 2, s8, s16)))


def _pooled_tile(u_ref, h_ref, i, T):
    halo = jnp.where(i > 0, h_ref[...], 0.0)
    ext = jnp.concatenate([halo, u_ref[...]], axis=0)
    n = T + HALO
    sums = _window_sums(ext, lambda v, k: pltpu.roll(v, k, 0))
    inv, lane_grp = _pool_counts(i * T - HALO, n)
    pooled = _select_group(lane_grp, *sums) * inv - ext
    return pooled[HALO:, :]


def pool_fwd(proj, w_pool, scale, *, name):
    S = proj.shape[0]
    T = _tile(S)
    hb = T // HALO

    def body(u_ref, h_ref, w_ref, s_ref, o_ref):
        i = pl.program_id(0)
        pooled = _pooled_tile(u_ref, h_ref, i, T).astype(BF16)
        outs = [_dot(pooled[:, g * GROUP_DIM:(g + 1) * GROUP_DIM], w_ref[g], 1, 0) for g in range(4)]
        o_ref[...] = (jnp.concatenate(outs, axis=1) * s_ref[...]).astype(BF16)

    return pl.pallas_call(
        body,
        out_shape=jax.ShapeDtypeStruct((S, WIDTH), BF16),
        grid=(S // T,),
        in_specs=[pl.BlockSpec((T, WIDTH), lambda i: (i, U_COLBLK)),
                  pl.BlockSpec((HALO, WIDTH), lambda i: (jnp.maximum(i * hb - 1, 0), U_COLBLK)),
                  pl.BlockSpec((4, GROUP_DIM, GROUP_DIM), lambda i: (0, 0, 0)),
                  pl.BlockSpec((1, WIDTH), lambda i: (0, 0))],
        out_specs=pl.BlockSpec((T, WIDTH), lambda i: (i, 0)),
        compiler_params=_cp(("parallel",)),
        name=name,
    )(proj, proj, w_pool, scale)


def pool_bwd(proj, w_pool, scale, do, *, name):
    S = proj.shape[0]
    T = _tile(S)
    hb = T // HALO
    nt = S // T

    def body(u_ref, h_ref, w_ref, s_ref, do_ref, dof_ref, du_ref, dw_ref, ds_ref):
        i = pl.program_id(0)

        @pl.when(i == 0)
        def _():
            dw_ref[...] = jnp.zeros_like(dw_ref)
            ds_ref[...] = jnp.zeros_like(ds_ref)

        pooled = _pooled_tile(u_ref, h_ref, i, T).astype(BF16)
        dov = do_ref[...].astype(F32)
        fut = jnp.where(i < nt - 1, dof_ref[...].astype(F32), 0.0)
        dmix = (jnp.concatenate([dov, fut], axis=0) * s_ref[...]).astype(BF16)
        mixed, dpool = [], []
        for g in range(4):
            lanes = slice(g * GROUP_DIM, (g + 1) * GROUP_DIM)
            mixed.append(_dot(pooled[:, lanes], w_ref[g], 1, 0))
            dw_ref[g] += _dot(pooled[:, lanes], dmix[:T, lanes], 0, 0)
            dpool.append(_dot(dmix[:, lanes], w_ref[g], 1, 1))
        ds_ref[...] += jnp.sum(dov * jnp.concatenate(mixed, axis=1), axis=0, keepdims=True)
        dp = jnp.concatenate(dpool, axis=1)
        n = T + HALO
        inv, lane_grp = _pool_counts(i * T, n)
        sums = _window_sums(dp * inv, lambda v, k: pltpu.roll(v, n - k, 0))
        du = _select_group(lane_grp, *sums) - dp
        du_ref[...] = du[:T, :].astype(BF16)

    row = pl.BlockSpec((T, WIDTH), lambda i: (i, 0))
    return pl.pallas_call(
        body,
        out_shape=(jax.ShapeDtypeStruct((S, WIDTH), BF16), jax.ShapeDtypeStruct((4, GROUP_DIM, GROUP_DIM), F32),
                   jax.ShapeDtypeStruct((1, WIDTH), F32)),
        grid=(nt,),
        in_specs=[pl.BlockSpec((T, WIDTH), lambda i: (i, U_COLBLK)),
                  pl.BlockSpec((HALO, WIDTH), lambda i: (jnp.maximum(i * hb - 1, 0), U_COLBLK)),
                  pl.BlockSpec((4, GROUP_DIM, GROUP_DIM), lambda i: (0, 0, 0)),
                  pl.BlockSpec((1, WIDTH), lambda i: (0, 0)),
                  row,
                  pl.BlockSpec((HALO, WIDTH), lambda i: (jnp.minimum((i + 1) * hb, S // HALO - 1), 0))],
        out_specs=(row, pl.BlockSpec((4, GROUP_DIM, GROUP_DIM), lambda i: (0, 0, 0)),
                   pl.BlockSpec((1, WIDTH), lambda i: (0, 0))),
        compiler_params=_cp(("arbitrary",)),
        name=name,
    )(proj, proj, w_pool, scale, do, do)


GATE_BLK0 = GATE_COL0 // WIDTH


def merge_fwd(oa, ob, oc, proj, b_gate, wa, wb, wc, *, name):
    S = oa.shape[0]
    T = _tile(S)

    def body(oa_ref, ob_ref, oc_ref, ga, gb, gc, ba, bb, bc, wa_ref, wb_ref, wc_ref, m_ref):
        acc = None
        for o_ref, g_ref, b_ref, w_ref in ((oa_ref, ga, ba, wa_ref), (ob_ref, gb, bb, wb_ref), (oc_ref, gc, bc, wc_ref)):
            y = _dot(o_ref[...], w_ref[...], 1, 0)
            t = jax.nn.sigmoid(g_ref[...] + b_ref[...]) * y
            acc = t if acc is None else acc + t
        m_ref[...] = acc.astype(BF16)

    row = pl.BlockSpec((T, WIDTH), lambda i, n: (i, 0))
    gate = lambda b: pl.BlockSpec((T, WIDTH), lambda i, n, b=b: (i, GATE_BLK0 + 2 * b + n))
    bias = lambda b: pl.BlockSpec((1, WIDTH), lambda i, n, b=b: (0, 2 * b + n))
    wspec = pl.BlockSpec((WIDTH, WIDTH), lambda i, n: (0, n))
    return pl.pallas_call(
        body,
        out_shape=jax.ShapeDtypeStruct((S, D_MODEL), BF16),
        grid=(S // T, 2),
        in_specs=[row, row, row, gate(0), gate(1), gate(2), bias(0), bias(1), bias(2), wspec, wspec, wspec],
        out_specs=pl.BlockSpec((T, WIDTH), lambda i, n: (i, n)),
        compiler_params=_cp(("parallel", "parallel")),
        name=name,
    )(oa, ob, oc, proj, proj, proj, b_gate, b_gate, b_gate, wa, wb, wc)


def merge_bwd(dm, oa, ob, oc, proj, b_gate, wa, wb, wc, *, name):
    S = oa.shape[0]
    T = _tile(S)

    def body(dm_ref, oa_ref, ob_ref, oc_ref, ga, gb, gc, ba, bb, bc, wa_ref, wb_ref, wc_ref,
             ta, tb, tc, dga, dgb, dgc, dba, dbb, dbc):
        i = pl.program_id(1)
        dmv = dm_ref[...].astype(F32)
        for o_ref, g_ref, b_ref, w_ref, t_ref, dg_ref, db_ref in (
                (oa_ref, ga, ba, wa_ref, ta, dga, dba), (ob_ref, gb, bb, wb_ref, tb, dgb, dbb),
                (oc_ref, gc, bc, wc_ref, tc, dgc, dbc)):
            y = _dot(o_ref[...], w_ref[...], 1, 0)
            gate = jax.nn.sigmoid(g_ref[...] + b_ref[...])
            t_ref[...] = (gate * dmv).astype(BF16)
            dgl = dmv * y * gate * (1.0 - gate)
            dg_ref[...] = dgl.astype(BF16)

            @pl.when(i == 0)
            def _():
                db_ref[...] = jnp.zeros_like(db_ref)

            db_ref[...] += jnp.sum(dgl, axis=0, keepdims=True)

    row = pl.BlockSpec((T, WIDTH), lambda n, i: (i, 0))
    half = pl.BlockSpec((T, WIDTH), lambda n, i: (i, n))
    gate = lambda b: pl.BlockSpec((T, WIDTH), lambda n, i, b=b: (i, GATE_BLK0 + 2 * b + n))
    bias = lambda b: pl.BlockSpec((1, WIDTH), lambda n, i, b=b: (0, 2 * b + n))
    wspec = pl.BlockSpec((WIDTH, WIDTH), lambda n, i: (0, n))
    bvec = pl.BlockSpec((1, WIDTH), lambda n, i: (0, n))
    act = jax.ShapeDtypeStruct((S, D_MODEL), BF16)
    vec = jax.ShapeDtypeStruct((1, D_MODEL), F32)
    return pl.pallas_call(
        body,
        out_shape=(act, act, act, act, act, act, vec, vec, vec),
        grid=(2, S // T),
        in_specs=[half, row, row, row, gate(0), gate(1), gate(2), bias(0), bias(1), bias(2), wspec, wspec, wspec],
        out_specs=(half, half, half, half, half, half, bvec, bvec, bvec),
        compiler_params=_cp(("parallel", "arbitrary")),
        name=name,
    )(dm, oa, ob, oc, proj, proj, proj, b_gate, b_gate, b_gate, wa, wb, wc)


FF_T = 256
FF_BLKS = D_FF // FF_T


def _silu_parts(x):
    s = jax.nn.sigmoid(x)
    return x * s, s


def _conv3(ext, w_ref, b_ref):
    return (b_ref[...] + w_ref[0:1, :] * pltpu.roll(ext, 2, 0) + w_ref[1:2, :] * pltpu.roll(ext, 1, 0)
            + w_ref[2:3, :] * ext)


def conv_glu_fwd(u, conv_w, conv_b, *, name):
    S = u.shape[0]
    T = _tile(S)
    hb = T // CONV_HALO

    def body(ug, ugh, uv, uvh, wg, wv, bg, bv, a_ref):
        i = pl.program_id(1)
        cs = []
        for m_ref, h_ref, w_ref, b_ref in ((ug, ugh, wg, bg), (uv, uvh, wv, bv)):
            halo = jnp.where(i > 0, h_ref[...], 0.0)
            ext = jnp.concatenate([halo, m_ref[...]], axis=0)
            cs.append(_conv3(ext, w_ref, b_ref)[CONV_HALO:, :])
        act, _ = _silu_parts(cs[0])
        a_ref[...] = (act * cs[1]).astype(BF16)

    main = lambda o: pl.BlockSpec((T, FF_T), lambda c, i, o=o: (i, c + o))
    halo = lambda o: pl.BlockSpec((CONV_HALO, FF_T), lambda c, i, o=o: (jnp.maximum(i * hb - 1, 0), c + o))
    wsp = lambda o: pl.BlockSpec((3, FF_T), lambda c, i, o=o: (0, c + o))
    bsp = lambda o: pl.BlockSpec((1, FF_T), lambda c, i, o=o: (0, c + o))
    return pl.pallas_call(
        body,
        out_shape=jax.ShapeDtypeStruct((S, D_FF), BF16),
        grid=(FF_BLKS, S // T),
        in_specs=[main(0), halo(0), main(FF_BLKS), halo(FF_BLKS), wsp(0), wsp(FF_BLKS), bsp(0), bsp(FF_BLKS)],
        out_specs=pl.BlockSpec((T, FF_T), lambda c, i: (i, c)),
        compiler_params=_cp(("parallel", "parallel")),
        name=name,
    )(u, u, u, u, conv_w, conv_w, conv_b, conv_b)


def conv_glu_bwd(u, conv_w, conv_b, da, *, name):
    S = u.shape[0]
    T = _tile(S)
    hb = T // CONV_HALO
    nt = S // T
    n = T + 2 * CONV_HALO

    def body(ug, ugp, ugf, uv, uvp, uvf, wg, wv, bg, bv, da_ref, daf_ref,
             dug, duv, dwg, dwv, dbg, dbv):
        i = pl.program_id(1)
        first, last = i == 0, i == nt - 1
        exts, cs = [], []
        for m_ref, p_ref, f_ref, w_ref, b_ref in ((ug, ugp, ugf, wg, bg), (uv, uvp, uvf, wv, bv)):
            ext = jnp.concatenate([jnp.where(first, 0.0, p_ref[...]), m_ref[...], jnp.where(last, 0.0, f_ref[...])], axis=0)
            exts.append(ext)
            cs.append(_conv3(ext, w_ref, b_ref))
        dae = jnp.concatenate([jnp.zeros((CONV_HALO, FF_T), F32), da_ref[...].astype(F32),
                               jnp.where(last, 0.0, daf_ref[...].astype(F32))], axis=0)
        act, sg = _silu_parts(cs[0])
        dcs = (dae * cs[1] * (sg * (1.0 + cs[0] * (1.0 - sg))), dae * act)
        main = slice(CONV_HALO, CONV_HALO + T)
        for ext, dc, w_ref, du_ref, dw_ref, db_ref in ((exts[0], dcs[0], wg, dug, dwg, dbg),
                                                       (exts[1], dcs[1], wv, duv, dwv, dbv)):
            du = (w_ref[2:3, :] * dc + w_ref[1:2, :] * pltpu.roll(dc, n - 1, 0) + w_ref[0:1, :] * pltpu.roll(dc, n - 2, 0))
            du_ref[...] = du[main, :].astype(BF16)
            dcm = dc[main, :]
            rows = [jnp.sum(dcm * pltpu.roll(ext, 2 - j, 0)[main, :], axis=0, keepdims=True) if j < 2
                    else jnp.sum(dcm * ext[main, :], axis=0, keepdims=True) for j in range(3)]

            @pl.when(first)
            def _():
                dw_ref[...] = jnp.zeros_like(dw_ref)
                db_ref[...] = jnp.zeros_like(db_ref)

            dw_ref[...] += jnp.concatenate(rows, axis=0)
            db_ref[...] += jnp.sum(dcm, axis=0, keepdims=True)

    main = lambda o: pl.BlockSpec((T, FF_T), lambda c, i, o=o: (i, c + o))
    past = lambda o: pl.BlockSpec((CONV_HALO, FF_T), lambda c, i, o=o: (jnp.maximum(i * hb - 1, 0), c + o))
    fut = lambda o: pl.BlockSpec((CONV_HALO, FF_T), lambda c, i, o=o: (jnp.minimum((i + 1) * hb, S // CONV_HALO - 1), c + o))
    wsp = lambda o: pl.BlockSpec((3, FF_T), lambda c, i, o=o: (0, c + o))
    bsp = lambda o: pl.BlockSpec((1, FF_T), lambda c, i, o=o: (0, c + o))
    return pl.pallas_call(
        body,
        out_shape=(jax.ShapeDtypeStruct((S, D_FF), BF16), jax.ShapeDtypeStruct((S, D_FF), BF16),
                   jax.ShapeDtypeStruct((3, D_FF), F32), jax.ShapeDtypeStruct((3, D_FF), F32),
                   jax.ShapeDtypeStruct((1, D_FF), F32), jax.ShapeDtypeStruct((1, D_FF), F32)),
        grid=(FF_BLKS, nt),
        in_specs=[main(0), past(0), fut(0), main(FF_BLKS), past(FF_BLKS), fut(FF_BLKS),
                  wsp(0), wsp(FF_BLKS), bsp(0), bsp(FF_BLKS), main(0), fut(0)],
        out_specs=(main(0), main(0), wsp(0), wsp(0), bsp(0), bsp(0)),
        compiler_params=_cp(("parallel", "arbitrary")),
        name=name,
    )(u, u, u, u, u, u, conv_w, conv_w, conv_b, conv_b, da, da)


def loss_head(y, target, *, name):
    S, D = y.shape
    T = _tile(S)

    def body(y_ref, t_ref, dy_ref, l_ref):
        i = pl.program_id(0)
        err = y_ref[...] - t_ref[...]
        dy_ref[...] = err * (1.0 / D)

        @pl.when(i == 0)
        def _():
            l_ref[...] = jnp.zeros_like(l_ref)

        l_ref[...] += 0.5 * jnp.sum(jnp.mean(err * err, axis=-1, keepdims=True))

    row = pl.BlockSpec((T, D), lambda i: (i, 0))
    return pl.pallas_call(
        body,
        out_shape=(jax.ShapeDtypeStruct((S, D), F32), jax.ShapeDtypeStruct((8, 128), F32)),
        grid=(S // T,),
        in_specs=[row, row],
        out_specs=(row, pl.BlockSpec((8, 128), lambda i: (0, 0))),
        compiler_params=_cp(("arbitrary",)),
        name=name,
    )(y, target)


ELEMS_PER_BLOCK = 256 * 1024


def _rows_tile(rows, cols):
    if rows * cols <= ELEMS_PER_BLOCK or rows % 8:
        return rows
    best = 8
    for tr in range(8, rows + 1, 8):
        if rows % tr == 0 and tr * cols <= ELEMS_PER_BLOCK:
            best = tr
    return best


def _adamw_math(g, w_ref, m_ref, v_ref, g_out, d_out, m_out, v_out):
    mn = ADAM_B1 * m_ref[...] + (1.0 - ADAM_B1) * g
    vn = ADAM_B2 * v_ref[...] + (1.0 - ADAM_B2) * (g * g)
    m_hat = mn / (1.0 - ADAM_B1 ** ADAM_STEP)
    v_hat = vn / (1.0 - ADAM_B2 ** ADAM_STEP)
    g_out[...] = g
    d_out[...] = -ADAM_LR * (m_hat / (jnp.sqrt(v_hat) + ADAM_EPS) + ADAM_WD * w_ref[...])
    m_out[...] = mn
    v_out[...] = vn


def adamw(w, m, v, g, *, name):
    rows, cols = w.shape
    tr = _rows_tile(rows, cols)

    def body(w_ref, m_ref, v_ref, g_ref, g_out, d_out, m_out, v_out):
        _adamw_math(g_ref[...], w_ref, m_ref, v_ref, g_out, d_out, m_out, v_out)

    spec = pl.BlockSpec((tr, cols), lambda i: (i, 0))
    shp = jax.ShapeDtypeStruct((rows, cols), F32)
    return pl.pallas_call(
        body,
        out_shape=(shp, shp, shp, shp),
        grid=(rows // tr,),
        in_specs=[spec] * 4,
        out_specs=(spec, spec, spec, spec),
        compiler_params=_cp(("parallel",)),
        name=name,
    )(w, m, v, g)


ANY = pl.BlockSpec(memory_space=pl.ANY)
STAGE_BYTES = 2 * 1024 * 1024


def _mesh_pos():
    return lax.axis_index("x"), lax.axis_index("y"), lax.axis_index("c")


def _chip_peers(x, y):
    return [(1 - x, y), (x, 1 - y), (1 - x, 1 - y)]


def _all_peers(x, y, c):
    return [((1 - x) if (r >> 2) & 1 else x, (1 - y) if (r >> 1) & 1 else y, (1 - c) if r & 1 else c)
            for r in range(1, 8)]


def _shard_slice(ref, axis, j, size, layer=None):
    idx = [slice(None)] * 3
    idx[axis] = pl.ds(pl.multiple_of(j * size, 128 if axis == 2 else 16), size)
    if layer is not None:
        idx[0] = pl.ds(layer, 1)
    return ref.at[tuple(idx)]


class LayerGather:
    def __init__(self, shards, axes, layer):
        self.nt = len(shards)
        self.axes = list(axes)
        self.layer = layer
        self.shapes = [s.shape for s in shards]
        self.dtypes = [s.dtype for s in shards]
        self.sizes = [s.shape[a] for s, a in zip(shards, axes)]
        self.split = [s.shape[1] % 32 == 0 for s in shards]
        self.half_rows = [s.shape[1] // 2 if sp else s.shape[1] for s, sp in zip(shards, self.split)]
        self.chunk_rows = []
        for s in shards:
            rt = s.shape[1]
            while rt % 32 == 0 and rt * s.shape[2] * s.dtype.itemsize > STAGE_BYTES:
                rt //= 2
            self.chunk_rows.append(rt)

    def full_shapes(self):
        out = []
        for shp, a, sz, dt in zip(self.shapes, self.axes, self.sizes, self.dtypes):
            shp = list(shp)
            shp[a] = 4 * sz
            out.append(jax.ShapeDtypeStruct(tuple(shp), dt))
        return out

    def scratch_shapes(self):
        return ([pltpu.VMEM((1, rt, shp[2]), dt) for shp, rt, dt in zip(self.shapes, self.chunk_rows, self.dtypes)]
                + [pltpu.SemaphoreType.DMA((2,))] + [pltpu.SemaphoreType.DMA((3 * self.nt,)) for _ in range(4)])

    def _views(self, ins, outs, scratch):
        nt = self.nt
        stage, stage_sems = scratch[:nt], scratch[nt]
        ici_send, ici_recv, d2d_send, d2d_recv = scratch[nt + 1:]
        x, y, c = _mesh_pos()
        mine = 2 * x + y
        peers = _chip_peers(x, y)
        layer = pl.ds(self.layer, 1)

        def rows(t, half, r0=0, n=None):
            hr = self.half_rows[t]
            if n is None:
                return pl.ds(pl.multiple_of(half * hr, 16), hr) if self.split[t] else pl.ds(0, hr)
            return pl.ds(r0, n)

        def placed(t, blk, row_sel, row_len):
            sz = self.sizes[t]
            if self.axes[t] == 2:
                return outs[t].at[layer, row_sel, pl.ds(pl.multiple_of(blk * sz, 128), sz)]
            return outs[t].at[layer, pl.ds(pl.multiple_of(blk * sz, 16) + row_sel.start, row_len), :]

        def ici(t, k, blk):
            px, py = peers[k]
            sel = rows(t, c)
            return pltpu.make_async_remote_copy(
                src_ref=ins[t].at[layer, sel, :], dst_ref=placed(t, blk, sel, self.half_rows[t]),
                send_sem=ici_send.at[3 * t + k], recv_sem=ici_recv.at[3 * t + k],
                device_id=(px, py, c), device_id_type=MESH_T)

        def d2d(t, k, half):
            px, py = peers[k]
            piece = placed(t, 2 * px + py, rows(t, half), self.half_rows[t])
            return pltpu.make_async_remote_copy(
                src_ref=piece, dst_ref=piece, send_sem=d2d_send.at[3 * t + k], recv_sem=d2d_recv.at[3 * t + k],
                device_id=(x, y, 1 - c), device_id_type=MESH_T)

        def own_chunk(t, r0):
            rt = self.chunk_rows[t]
            sel = pl.ds(r0, rt)
            return ins[t].at[layer, sel, :], placed(t, mine, sel, rt), stage[t], stage_sems

        return c, mine, peers, ici, d2d, own_chunk

    def start(self, ins, outs, scratch):
        c, mine, peers, ici, d2d, own_chunk = self._views(ins, outs, scratch)
        for t in range(self.nt):
            for k in range(3):
                ici(t, k, mine).start()
        for t in range(self.nt):
            for r0 in range(0, self.shapes[t][1], self.chunk_rows[t]):
                src, dst, buf, sems = own_chunk(t, r0)
                load = pltpu.make_async_copy(src, buf, sems.at[0])
                load.start()
                load.wait()
                store = pltpu.make_async_copy(buf, dst, sems.at[1])
                store.start()
                store.wait()

    def finish(self, ins, outs, scratch):
        c, mine, peers, ici, d2d, own_chunk = self._views(ins, outs, scratch)
        for t in range(self.nt):
            for k, (px, py) in enumerate(peers):
                ici(t, k, 2 * px + py).wait_recv()
                if self.split[t]:
                    d2d(t, k, c).start()
        for t in range(self.nt):
            for k in range(3):
                if self.split[t]:
                    d2d(t, k, 1 - c).wait_recv()
        for t in range(self.nt):
            for k in range(3):
                ici(t, k, mine).wait_send()
                if self.split[t]:
                    d2d(t, k, c).wait_send()


def all_gather_layer(shards, axes, layer, *, name):
    plan = LayerGather(shards, axes, layer)
    nt = plan.nt

    def body(*refs):
        ins, outs, scratch = refs[:nt], refs[nt:2 * nt], refs[2 * nt:]
        plan.start(ins, outs, scratch)
        plan.finish(ins, outs, scratch)

    return pl.pallas_call(
        body,
        out_shape=tuple(plan.full_shapes()),
        in_specs=[ANY] * nt,
        out_specs=tuple([ANY] * nt),
        scratch_shapes=plan.scratch_shapes(),
        name=name,
    )(*shards)


class HalfLayout:
    def __init__(self, shape, axis):
        self.R, self.C = shape
        self.axis = axis
        if axis == 1:
            self.hr, self.pw = self.R // 2, self.C // 4
            self.half_shape = (self.hr, self.C)
        else:
            self.hr, self.pw = self.R // 8, self.C
            self.half_shape = (4 * self.hr, self.C)
        self.tr = _rows_tile(self.hr, self.pw)
        self.nr = self.hr // self.tr

    def in_grad(self, ref, blk, half):
        if self.axis == 1:
            return ref.at[pl.ds(pl.multiple_of(half * self.hr, 16), self.hr), pl.ds(pl.multiple_of(blk * self.pw, 128), self.pw)]
        return ref.at[pl.ds(pl.multiple_of((2 * blk + half) * self.hr, 16), self.hr), :]

    def in_half(self, ref, blk):
        if self.axis == 1:
            return ref.at[:, pl.ds(pl.multiple_of(blk * self.pw, 128), self.pw)]
        return ref.at[pl.ds(pl.multiple_of(blk * self.hr, 16), self.hr), :]

    def grad_spec(self):
        if self.axis == 1:
            return pl.BlockSpec((self.tr, self.pw), lambda j, i, s: (s[0] * self.nr + i, j))
        return pl.BlockSpec((self.tr, self.pw), lambda j, i, s: ((2 * j + s[0]) * self.nr + i, 0))

    def half_spec(self):
        if self.axis == 1:
            return pl.BlockSpec((self.tr, self.pw), lambda j, i, s: (i, j))
        return pl.BlockSpec((self.tr, self.pw), lambda j, i, s: (j * self.nr + i, 0))


def half_exchange(grads, layouts, *, name):
    nt = len(grads)
    pieces = [(t, j) for t in range(nt) for j in (range(4) if layouts[t].axis == 0 else range(1))]

    def body(*refs):
        ins, outs = refs[:nt], refs[nt:2 * nt]
        send_sems, recv_sems = refs[2 * nt:]
        x, y, c = _mesh_pos()
        cps = []
        for n, (t, j) in enumerate(pieces):
            lay = layouts[t]
            if lay.axis == 1:
                src = ins[t].at[pl.ds(pl.multiple_of((1 - c) * lay.hr, 16), lay.hr), :]
                dst = outs[t]
            else:
                src = lay.in_grad(ins[t], j, 1 - c)
                dst = lay.in_half(outs[t], j)
            cp = pltpu.make_async_remote_copy(src_ref=src, dst_ref=dst, send_sem=send_sems.at[n], recv_sem=recv_sems.at[n],
                                              device_id=(x, y, 1 - c), device_id_type=MESH_T)
            cp.start()
            cps.append(cp)
        for cp in cps:
            cp.wait_recv()
        for cp in cps:
            cp.wait_send()

    return pl.pallas_call(
        body,
        out_shape=tuple(jax.ShapeDtypeStruct(lay.half_shape, F32) for lay in layouts),
        in_specs=[ANY] * nt,
        out_specs=tuple([ANY] * nt),
        scratch_shapes=[pltpu.SemaphoreType.DMA((len(pieces),)), pltpu.SemaphoreType.DMA((len(pieces),))],
        name=name,
    )(*grads)


def pair_sum(grad, other, lay, core, *, name):
    def body(c_ref, g_ref, o_ref, s32_ref, s16_ref):
        s = g_ref[...] + o_ref[...]
        s32_ref[...] = s
        s16_ref[...] = s.astype(BF16)

    return pl.pallas_call(
        body,
        out_shape=(jax.ShapeDtypeStruct(lay.half_shape, F32), jax.ShapeDtypeStruct(lay.half_shape, BF16)),
        grid_spec=pltpu.PrefetchScalarGridSpec(
            num_scalar_prefetch=1, grid=(4, lay.nr),
            in_specs=[lay.grad_spec(), lay.half_spec()],
            out_specs=(lay.half_spec(), lay.half_spec())),
        compiler_params=_cp(("parallel", "parallel")),
        name=name,
    )(core, grad, other)


class BlockScatter:
    def __init__(self, layouts):
        self.layouts = layouts
        self.nt = len(layouts)

    def recv_shapes(self):
        return [jax.ShapeDtypeStruct((3, lay.hr, lay.pw), BF16) for lay in self.layouts]

    def scratch_shapes(self):
        return [pltpu.SemaphoreType.DMA((3 * self.nt,)), pltpu.SemaphoreType.DMA((3 * self.nt,))]

    def _copies(self, pairs16, recv, scratch):
        send_sems, recv_sems = scratch
        x, y, c = _mesh_pos()
        return [pltpu.make_async_remote_copy(
            src_ref=lay.in_half(pairs16[t], 2 * px + py), dst_ref=recv[t].at[k],
            send_sem=send_sems.at[3 * t + k], recv_sem=recv_sems.at[3 * t + k],
            device_id=(px, py, c), device_id_type=MESH_T)
            for t, lay in enumerate(self.layouts) for k, (px, py) in enumerate(_chip_peers(x, y))]

    def start(self, pairs16, recv, scratch):
        for cp in self._copies(pairs16, recv, scratch):
            cp.start()

    def finish(self, pairs16, recv, scratch):
        copies = self._copies(pairs16, recv, scratch)
        for cp in copies:
            cp.wait_recv()
        for cp in copies:
            cp.wait_send()


def scatter_blocks(pairs16, layouts, small, *, name):
    plan = BlockScatter(layouts)
    nt = plan.nt

    def body(*refs):
        in16, small_in = refs[:nt], refs[nt]
        recv, small_out = refs[nt + 1:2 * nt + 1], refs[2 * nt + 1]
        scratch, ssend, srecv = refs[2 * nt + 2:2 * nt + 4], refs[2 * nt + 4], refs[2 * nt + 5]
        x, y, c = _mesh_pos()
        me = 4 * x + 2 * y + c
        plan.start(in16, recv, scratch)
        sends, recvs = [], []
        for r, (px, py, pc) in enumerate(_all_peers(x, y, c)):
            def mk(slot, r=r, px=px, py=py, pc=pc):
                return pltpu.make_async_remote_copy(
                    src_ref=small_in, dst_ref=small_out.at[slot], send_sem=ssend.at[r], recv_sem=srecv.at[r],
                    device_id=(px, py, pc), device_id_type=MESH_T)
            snd = mk(me)
            snd.start()
            sends.append(snd)
            recvs.append(mk(4 * px + 2 * py + pc))
        for r in recvs:
            r.wait_recv()
        for s in sends:
            s.wait_send()
        plan.finish(in16, recv, scratch)

    return pl.pallas_call(
        body,
        out_shape=tuple(plan.recv_shapes() + [jax.ShapeDtypeStruct((8,) + small.shape, F32)]),
        in_specs=[ANY] * (nt + 1),
        out_specs=tuple([ANY] * (nt + 1)),
        scratch_shapes=plan.scratch_shapes() + [pltpu.SemaphoreType.DMA((7,)), pltpu.SemaphoreType.DMA((7,))],
        name=name,
    )(*pairs16, small)


def sum_chips(pair32, recv, lay, chip, *, name):
    def body(j_ref, p_ref, r_ref, s_ref):
        acc = p_ref[...]
        for k in range(3):
            acc = acc + r_ref[k].astype(F32)
        s_ref[...] = acc

    if lay.axis == 1:
        own = pl.BlockSpec((lay.tr, lay.pw), lambda i, j: (i, j[0]))
    else:
        own = pl.BlockSpec((lay.tr, lay.pw), lambda i, j: (j[0] * lay.nr + i, 0))
    return pl.pallas_call(
        body,
        out_shape=jax.ShapeDtypeStruct((lay.hr, lay.pw), F32),
        grid_spec=pltpu.PrefetchScalarGridSpec(
            num_scalar_prefetch=1, grid=(lay.nr,),
            in_specs=[own, pl.BlockSpec((3, lay.tr, lay.pw), lambda i, j: (0, i, 0))],
            out_specs=pl.BlockSpec((lay.tr, lay.pw), lambda i, j: (i, 0))),
        compiler_params=_cp(("parallel",)),
        name=name,
    )(chip, pair32, recv)


def sum_devices(gathered, own, me, *, name):
    _, R, C = gathered.shape

    def body(me_ref, g_ref, o_ref, s_ref):
        acc = None
        for k in range(8):
            part = jnp.where(me_ref[0] == k, o_ref[...], g_ref[k])
            acc = part if acc is None else acc + part
        s_ref[...] = acc

    return pl.pallas_call(
        body,
        out_shape=jax.ShapeDtypeStruct((R, C), F32),
        grid_spec=pltpu.PrefetchScalarGridSpec(
            num_scalar_prefetch=1, grid=(1,),
            in_specs=[pl.BlockSpec((8, R, C), lambda i, m: (0, 0, 0)), pl.BlockSpec((R, C), lambda i, m: (0, 0))],
            out_specs=pl.BlockSpec((R, C), lambda i, m: (0, 0))),
        compiler_params=_cp(("arbitrary",)),
        name=name,
    )(me, gathered, own)


def sibling_swap(parts, *, name):
    nt = len(parts)

    def body(*refs):
        ins, outs = refs[:nt], refs[nt:2 * nt]
        send_sems, recv_sems = refs[2 * nt:]
        x, y, c = _mesh_pos()
        cps = []
        for t in range(nt):
            cp = pltpu.make_async_remote_copy(src_ref=ins[t], dst_ref=outs[t], send_sem=send_sems.at[t],
                                              recv_sem=recv_sems.at[t], device_id=(x, y, 1 - c), device_id_type=MESH_T)
            cp.start()
            cps.append(cp)
        for cp in cps:
            cp.wait_recv()
        for cp in cps:
            cp.wait_send()

    return pl.pallas_call(
        body,
        out_shape=tuple(jax.ShapeDtypeStruct(p.shape, p.dtype) for p in parts),
        in_specs=[ANY] * nt,
        out_specs=tuple([ANY] * nt),
        scratch_shapes=[pltpu.SemaphoreType.DMA((nt,)), pltpu.SemaphoreType.DMA((nt,))],
        name=name,
    )(*parts)


def adamw_halves(w, m, v, mine, other, lay, core, *, name):
    _, r, c = w.shape
    tr, nr = lay.tr, lay.nr
    assert (r, c) == (2 * lay.hr, lay.pw), (w.shape, lay.hr, lay.pw)

    def body(c_ref, w_ref, m_ref, v_ref, *rest):
        g_refs, outs = rest[:2 * DEPTH], rest[2 * DEPTH:]
        l, h = pl.program_id(0), pl.program_id(1)
        g = None
        for d in range(DEPTH):
            gd = jnp.where(h == c_ref[0], g_refs[d][...], g_refs[DEPTH + d][...])
            g = gd if g is None else jnp.where(l == d, gd, g)
        _adamw_math(g, w_ref, m_ref, v_ref, *outs)

    full = pl.BlockSpec((None, tr, c), lambda l, h, i, s: (l, h * nr + i, 0))

    def part(d, is_mine):
        def index(l, h, i, s):
            used = jnp.logical_and(l == d, (h == s[0]) == is_mine)
            return jnp.where(used, i, 0), 0
        return pl.BlockSpec((tr, c), index)

    shp = jax.ShapeDtypeStruct((DEPTH, r, c), F32)
    return pl.pallas_call(
        body,
        out_shape=(shp, shp, shp, shp),
        grid_spec=pltpu.PrefetchScalarGridSpec(
            num_scalar_prefetch=1, grid=(DEPTH, 2, nr),
            in_specs=[full, full, full] + [part(d, True) for d in range(DEPTH)] + [part(d, False) for d in range(DEPTH)],
            out_specs=(full, full, full, full)),
        compiler_params=_cp(("arbitrary", "arbitrary", "arbitrary")),
        name=name,
    )(core, w, m, v, *mine, *other)


WEIGHTS = ("norm_mix", "w_in", "b_gate", "q_norm_a", "k_norm_a", "rel_bias_a", "w_pool", "pool_scale",
           "w_branch_a", "w_branch_b", "w_branch_c", "w_out", "norm_ffn", "w_up", "conv_w", "conv_b", "w_down")
SHARDED = {"w_in": 2, "w_branch_a": 2, "w_branch_b": 2, "w_branch_c": 2, "w_out": 1, "w_up": 2, "conv_w": 2,
           "w_down": 1}
REPLICATED = tuple(n for n in WEIGHTS if n not in SHARDED)
MATMUL_WEIGHTS = tuple(n for n in SHARDED if n != "conv_w")
SMALL_WEIGHTS = tuple(n for n in WEIGHTS if n not in MATMUL_WEIGHTS)
SMALL_ROWS = 1496


def _layer_fwd(x, p, tables, prefetch=None):
    full = p["full"]
    diag = exact_dot(p["rel_bias_a"], tables["onehot_t"], name="bias_diagonals")
    diag = diag.reshape(N_HEADS, N_VARIANTS, 1, DIAG_W).transpose(1, 0, 2, 3)
    biasm = bias_expand(diag, name="bias_expand")
    gq8 = jnp.tile(p["q_norm_a"], N_HEADS)[None]
    gk8 = jnp.tile(p["k_norm_a"], N_HEADS)[None]
    h = rmsnorm_fwd(x, p["norm_mix"][None], name="rmsnorm_fwd")
    proj = matmul(h, full["w_in"], b_layer=p["l"], name="mm_in")
    qa, ka, va, qb, kb, vb = qkv_prep(proj, gq8, gk8, name="qkv_prep")
    oa = attn_a_fwd(qa, ka, va, biasm, name="attn_a_fwd")
    if prefetch is None:
        ob, tot, nblk = attn_b_fwd(qb, kb, vb, name="attn_b_fwd")
    else:
        plan, shards, names = prefetch
        ob, tot, nblk, filled = attn_b_fwd(qb, kb, vb, gather=(plan, shards, [full[n] for n in names]),
                                           name="attn_b_fwd_gather")
        full = dict(zip(names, filled))
    wpool = p["w_pool"].astype(BF16)
    oc = pool_fwd(proj, wpool, p["pool_scale"][None], name="pool_fwd")
    merged = merge_fwd(oa, ob, oc, proj, p["b_gate"][None], p["w_branch_a"], p["w_branch_b"], p["w_branch_c"],
                       name="merge_fwd")
    x1 = matmul(merged, full["w_out"], b_layer=p["l"], add=x, name="mm_out")
    h2 = rmsnorm_fwd(x1, p["norm_ffn"][None], name="rmsnorm_fwd")
    u = matmul(h2, full["w_up"], b_layer=p["l"], name="mm_up")
    a = conv_glu_fwd(u, p["conv_w"], p["conv_b"][None], name="conv_glu_fwd")
    x2 = matmul(a, full["w_down"], b_layer=p["l"], add=x1, name="mm_down")
    saved = dict(x=x, h=h, proj=proj, qa=qa, ka=ka, va=va, qb=qb, kb=kb, vb=vb, oa=oa, ob=ob, tot=tot, nblk=nblk, oc=oc,
                 merged=merged, x1=x1, h2=h2, u=u, a=a, biasm=biasm, gq8=gq8, gk8=gk8, wpool=wpool)
    return x2, saved, full


class GradReducer:
    def __init__(self, layouts, core, chip):
        self.layouts, self.core, self.chip = layouts, core, chip
        self.pairs32, self.received, self.pending = {}, {}, []

    def prepare(self, layer, grads):
        names = list(grads)
        lays = [self.layouts[n] for n in names]
        others = half_exchange([grads[n] for n in names], lays, name="half_exchange")
        for n, lay, other in zip(names, lays, others):
            p32, p16 = pair_sum(grads[n], other, lay, self.core, name="pair_sum")
            self.pairs32[(layer, n)] = p32
            self.pending.append(((layer, n), p16))

    def take(self):
        keys = [k for k, _ in self.pending]
        pairs16 = [p for _, p in self.pending]
        self.pending = []
        return keys, BlockScatter([self.layouts[n] for _, n in keys]), pairs16

    def store(self, keys, received):
        self.received.update(zip(keys, received))

    def finish(self):
        keys = list(self.pairs32)
        mine = [sum_chips(self.pairs32[k], self.received[k], self.layouts[k[1]], self.chip, name="sum_chips") for k in keys]
        other = sibling_swap(mine, name="sibling_swap")
        return dict(zip(keys, mine)), dict(zip(keys, other))


EARLY_WEIGHTS = ("w_down", "w_up", "w_out", "w_branch_a", "w_branch_b", "w_branch_c")


def _layer_bwd(dx2, s, p, tables, reducer):
    g = {}
    full, l = p["full"], p["l"]
    da = matmul(dx2, full["w_down"], b_layer=l, tb=True, name="mm_down_dx")
    g["w_down"] = matmul(s["a"], dx2, ta=True, name="mm_down_dw")
    dug, duv, dcwg, dcwv, dcbg, dcbv = conv_glu_bwd(s["u"], p["conv_w"], p["conv_b"][None], da, name="conv_glu_bwd")
    du = jnp.concatenate([dug, duv], axis=1)
    g["conv_w"] = jnp.concatenate([dcwg, dcwv], axis=1)
    g["conv_b"] = jnp.concatenate([dcbg, dcbv], axis=1)[0]
    g["w_up"] = matmul(s["h2"], du, ta=True, name="mm_up_dw")
    dh2 = matmul(du, full["w_up"], b_layer=l, tb=True, name="mm_up_dx")
    dx1, dg2 = rmsnorm_bwd(s["x1"], p["norm_ffn"][None], dh2, dx2, name="rmsnorm_bwd")
    g["norm_ffn"] = dg2[0]
    dmerged = matmul(dx1, full["w_out"], b_layer=l, tb=True, name="mm_out_dx")
    g["w_out"] = matmul(s["merged"], dx1, ta=True, name="mm_out_dw")
    t_a, t_b, t_c, dga, dgb, dgc, dba, dbb, dbc = merge_bwd(
        dmerged, s["oa"], s["ob"], s["oc"], s["proj"], p["b_gate"][None], p["w_branch_a"], p["w_branch_b"],
        p["w_branch_c"], name="merge_bwd")
    g["b_gate"] = jnp.concatenate([dba, dbb, dbc], axis=1)[0]
    g["w_branch_a"] = matmul(s["oa"], t_a, ta=True, name="mm_branch_dw")
    g["w_branch_b"] = matmul(s["ob"], t_b, ta=True, name="mm_branch_dw")
    g["w_branch_c"] = matmul(s["oc"], t_c, ta=True, name="mm_branch_dw")
    doa = matmul(t_a, full["w_branch_a"], b_layer=l, tb=True, out_dtype=BF16, name="mm_branch_dx")
    dob = matmul(t_b, full["w_branch_b"], b_layer=l, tb=True, out_dtype=BF16, name="mm_branch_dx")
    doc = matmul(t_c, full["w_branch_c"], b_layer=l, tb=True, name="mm_branch_dx_f32")
    dqh, dkh, dva, dbias = attn_a_bwd(s["qa"], s["ka"], s["va"], s["biasm"], doa, name="attn_a_bwd")
    ddiag = relbias_reduce(dbias, name="relbias_reduce")
    ddiag = ddiag.transpose(1, 0, 2, 3).reshape(N_HEADS, N_VARIANTS * DIAG_W)
    g["rel_bias_a"] = exact_dot(ddiag, tables["onehot"], name="relbias_table")
    dqa, dka, dgq8, dgk8 = qknorm_bwd(s["proj"], s["gq8"], s["gk8"], dqh, dkh, name="qknorm_bwd")
    g["q_norm_a"] = dgq8.reshape(N_HEADS, HEAD_DIM).sum(axis=0)
    g["k_norm_a"] = dgk8.reshape(N_HEADS, HEAD_DIM).sum(axis=0)
    reducer.prepare(l, {n: g[n] for n in EARLY_WEIGHTS})
    keys, plan, pairs16 = reducer.take()
    dqb, dkb, dvb, received = attn_b_bwd(s["qb"], s["kb"], s["vb"], s["tot"], s["nblk"], dob, scatter=(plan, pairs16),
                                         name="attn_b_bwd_scatter")
    reducer.store(keys, received)
    duc, dwp, dsc = pool_bwd(s["proj"], s["wpool"], p["pool_scale"][None], doc, name="pool_bwd")
    g["w_pool"] = dwp
    g["pool_scale"] = dsc[0]
    dproj = jnp.concatenate([dqa, dka, dva.astype(BF16), dqb, dkb.astype(BF16), dvb.astype(BF16), duc,
                             dga, dgb, dgc], axis=1)
    g["w_in"] = matmul(s["h"], dproj, ta=True, name="mm_in_dw")
    reducer.prepare(l, {"w_in": g["w_in"]})
    dh = matmul(dproj, full["w_in"], b_layer=l, tb=True, name="mm_in_dx")
    dx, dg1 = rmsnorm_bwd(s["x"], p["norm_mix"][None], dh, dx1, name="rmsnorm_bwd")
    g["norm_mix"] = dg1[0]
    return dx, g


def kernel(x, norm_mix, w_in, b_gate, q_norm_a, k_norm_a, rel_bias_a, w_pool, pool_scale, w_branch_a, w_branch_b, w_branch_c, w_out, norm_ffn, w_up, conv_w, conv_b, w_down, loss_target, m_norm_mix, m_w_in, m_b_gate, m_q_norm_a, m_k_norm_a, m_rel_bias_a, m_w_pool, m_pool_scale, m_w_branch_a, m_w_branch_b, m_w_branch_c, m_w_out, m_norm_ffn, m_w_up, m_conv_w, m_conv_b, m_w_down, v_norm_mix, v_w_in, v_b_gate, v_q_norm_a, v_k_norm_a, v_rel_bias_a, v_w_pool, v_pool_scale, v_w_branch_a, v_w_branch_b, v_w_branch_c, v_w_out, v_norm_ffn, v_w_up, v_conv_w, v_conv_b, v_w_down):
    w = dict(zip(WEIGHTS, (norm_mix, w_in, b_gate, q_norm_a, k_norm_a, rel_bias_a, w_pool, pool_scale, w_branch_a,
                           w_branch_b, w_branch_c, w_out, norm_ffn, w_up, conv_w, conv_b, w_down)))
    m = dict(zip(WEIGHTS, (m_norm_mix, m_w_in, m_b_gate, m_q_norm_a, m_k_norm_a, m_rel_bias_a, m_w_pool, m_pool_scale,
                           m_w_branch_a, m_w_branch_b, m_w_branch_c, m_w_out, m_norm_ffn, m_w_up, m_conv_w, m_conv_b,
                           m_w_down)))
    v = dict(zip(WEIGHTS, (v_norm_mix, v_w_in, v_b_gate, v_q_norm_a, v_k_norm_a, v_rel_bias_a, v_w_pool, v_pool_scale,
                           v_w_branch_a, v_w_branch_b, v_w_branch_c, v_w_out, v_norm_ffn, v_w_up, v_conv_w, v_conv_b,
                           v_w_down)))
    onehot = diagonal_onehot()
    tables = dict(onehot=jnp.asarray(onehot), onehot_t=jnp.asarray(np.ascontiguousarray(onehot.T)))

    names = tuple(SHARDED)
    shards = [w[n] if n == "conv_w" else w[n].astype(BF16) for n in names]
    axes = [SHARDED[n] for n in names]
    full = dict(zip(names, all_gather_layer(shards, axes, 0, name="all_gather_layer")))

    def layer_params(l):
        p = {n: full[n][l] for n in ("w_branch_a", "w_branch_b", "w_branch_c", "conv_w")}
        p.update({n: w[n][l] for n in REPLICATED})
        p.update(full=full, l=l)
        return p

    xs = x[0]
    saved = []
    for l in range(DEPTH):
        prefetch = (LayerGather(shards, axes, l + 1), shards, names) if l + 1 < DEPTH else None
        xs, s, full = _layer_fwd(xs, layer_params(l), tables, prefetch)
        saved.append(s)
    dx, lpart = loss_head(xs, loss_target[0], name="loss_head")
    loss = lax.psum(lpart[0, 0], MESH_AXES)
    as_index = lambda i: jnp.reshape(i, (1,)).astype(jnp.int32)
    cx, cy, cc = _mesh_pos()
    core, chip, me = as_index(cc), as_index(2 * cx + cy), as_index(4 * cx + 2 * cy + cc)
    layouts = {n: HalfLayout((full[n].shape[1], full[n].shape[2]), SHARDED[n] - 1) for n in MATMUL_WEIGHTS}
    reducer = GradReducer(layouts, core, chip)
    grads = [None] * DEPTH
    for l in reversed(range(DEPTH)):
        dx, grads[l] = _layer_bwd(dx, saved[l], layer_params(l), tables, reducer)

    g = {n: jnp.stack([grads[l][n] for l in range(DEPTH)]) for n in SMALL_WEIGHTS}
    flat = jnp.concatenate([g[n].reshape(-1) for n in SMALL_WEIGHTS])
    small = jnp.pad(flat, (0, SMALL_ROWS * 128 - flat.shape[0])).reshape(SMALL_ROWS, 128)
    keys, plan, pairs16 = reducer.take()
    outs = scatter_blocks(pairs16, plan.layouts, small, name="scatter_blocks")
    reducer.store(keys, outs[:-1])
    small_sum = sum_devices(outs[-1], small, me, name="sum_devices").reshape(-1)
    mine, other = reducer.finish()

    res = {}
    for n in MATMUL_WEIGHTS:
        res[n] = adamw_halves(w[n], m[n], v[n], [mine[(l, n)] for l in range(DEPTH)],
                              [other[(l, n)] for l in range(DEPTH)], layouts[n], core, name="adamw_halves")
    off = 0
    for n in SMALL_WEIGHTS:
        shp = g[n].shape
        size = int(np.prod(shp))
        gn = small_sum[off:off + size].reshape(shp)
        off += size
        if n in SHARDED:
            gn = lax.dynamic_slice_in_dim(gn, (2 * cx + cy) * w[n].shape[-1], w[n].shape[-1], axis=len(shp) - 1)
        shp = w[n].shape
        cols = shp[-1]
        two_d = lambda t: t.reshape(int(np.prod(shp)) // cols, cols)
        res[n] = [t.reshape(shp) for t in adamw(two_d(w[n]), two_d(m[n]), two_d(v[n]), two_d(gn), name="adamw")]

    out = [loss, dx[None]]
    for k in range(4):
        out.extend(res[n][k] for n in WEIGHTS)
    return tuple(out)
```

```python
import jax
import jax.numpy as jnp
import numpy as np
from jax import lax
from jax.experimental import pallas as pl
from jax.experimental.pallas import tpu as pltpu

F32 = jnp.float32
BF16 = jnp.bfloat16

D_MODEL = 1024
DEPTH = 2
CHUNK = 64
N_LEFT = 8
HEAD_DIM = 64
N_HEADS = 8
WIDTH = 512
POOL_WINDOWS = (2, 4, 8, 16)
GROUP_DIM = 128
MAX_REL = 2 * CHUNK
REL_TABLE = MAX_REL + CHUNK
D_FF = 2816
EPS = 1e-6
QK_SCALE = 0.125
IN_COLS = 7 * WIDTH + 3 * D_MODEL
GATE_COL0 = 7 * WIDTH

ADAM_LR = 0.001
ADAM_B1 = 0.9
ADAM_B2 = 0.999
ADAM_EPS = 1e-08
ADAM_WD = 0.01
ADAM_STEP = 10

VMEM_LIMIT = 56 * 1024 * 1024
ATT_Q = 256
A_Q = 256
A_WIN = A_Q + N_LEFT * CHUNK
HALO = 16
CONV_HALO = 8
NEG = -1e30

MESH_AXES = ("x", "y", "c")
MESH_T = pl.DeviceIdType.MESH


def _cp(sem=None, vmem=VMEM_LIMIT):
    return pltpu.CompilerParams(dimension_semantics=sem, vmem_limit_bytes=vmem)


def _dot(a, b, ca, cb):
    return lax.dot_general(a, b, (((ca,), (cb,)), ((), ())), preferred_element_type=F32)


def _tile(n, cands=(512, 256, 128)):
    for c in cands:
        if n % c == 0:
            return c
    return n


def _split_hi_lo(v):
    hi = v.astype(BF16)
    lo = (v - hi.astype(F32)).astype(BF16)
    return hi, lo


def matmul(a, b, *, ta=False, tb=False, add=None, out_dtype=F32, b_layer=None, behind=None, name):
    plan, sources = behind if behind is not None else (None, [])
    n_src = len(sources)
    n_dst = len(plan.out_shapes()) if plan is not None else 0
    if ta:
        K, M = a.shape
    else:
        M, K = a.shape
    if tb:
        N, K2 = b.shape[-2:]
    else:
        K2, N = b.shape[-2:]
    assert K == K2, (a.shape, b.shape, ta, tb)
    big = (1024, 1408, 512, 256, 128)
    tm, tn, tk = _tile(M, big), _tile(N, big), _tile(K, big)
    nk = K // tk

    n_in = (3 if add is not None else 2) + n_src
    grid = (M // tm, N // tn, nk)

    def body(*refs):
        a_ref, b_ref = refs[:2]
        r_ref = refs[2] if add is not None else None
        o_ref, acc = refs[n_in], refs[n_in + 1 + n_dst]
        i, j, k = pl.program_id(0), pl.program_id(1), pl.program_id(2)
        if plan is not None:
            comm = (refs[n_in - n_src:n_in], refs[n_in + 1:n_in + 1 + n_dst], refs[n_in + 2 + n_dst:])

            @pl.when((i == 0) & (j == 0) & (k == 0))
            def _():
                plan.start(*comm)

        @pl.when(k == 0)
        def _():
            acc[...] = jnp.zeros_like(acc)

        av = a_ref[...].astype(BF16)
        bv = b_ref[...].astype(BF16)
        acc[...] += _dot(av, bv, 0 if ta else 1, 1 if tb else 0)

        @pl.when(k == nk - 1)
        def _():
            r = acc[...]
            if add is not None:
                r = r + r_ref[...].astype(F32)
            o_ref[...] = r.astype(out_dtype)

        if plan is not None:
            @pl.when((i == grid[0] - 1) & (j == grid[1] - 1) & (k == nk - 1))
            def _():
                plan.finish(*comm)

    a_spec = pl.BlockSpec((tk, tm), lambda i, j, k: (k, i)) if ta else pl.BlockSpec((tm, tk), lambda i, j, k: (i, k))
    if b_layer is None:
        b_spec = pl.BlockSpec((tn, tk), lambda i, j, k: (j, k)) if tb else pl.BlockSpec((tk, tn), lambda i, j, k: (k, j))
    elif tb:
        b_spec = pl.BlockSpec((None, tn, tk), lambda i, j, k: (b_layer, j, k))
    else:
        b_spec = pl.BlockSpec((None, tk, tn), lambda i, j, k: (b_layer, k, j))
    in_specs = [a_spec, b_spec]
    args = [a, b]
    if add is not None:
        in_specs.append(pl.BlockSpec((tm, tn), lambda i, j, k: (i, j)))
        args.append(add)
    o_spec = pl.BlockSpec((tm, tn), lambda i, j, k: (i, j))
    if plan is None:
        return pl.pallas_call(
            body,
            out_shape=jax.ShapeDtypeStruct((M, N), out_dtype),
            grid=grid,
            in_specs=in_specs,
            out_specs=o_spec,
            scratch_shapes=[pltpu.VMEM((tm, tn), F32)],
            compiler_params=_cp(("parallel", "parallel", "arbitrary")),
            name=name,
        )(*args)
    any_space = pl.BlockSpec(memory_space=pl.ANY)
    outs = pl.pallas_call(
        body,
        out_shape=(jax.ShapeDtypeStruct((M, N), out_dtype),) + tuple(plan.out_shapes()),
        grid=grid,
        in_specs=in_specs + [any_space] * n_src,
        out_specs=(o_spec,) + tuple([any_space] * n_dst),
        scratch_shapes=[pltpu.VMEM((tm, tn), F32)] + list(plan.scratch_shapes()),
        compiler_params=_cp(("arbitrary", "arbitrary", "arbitrary")),
        name=name,
    )(*args, *sources)
    return outs[0], list(outs[1:])


def rmsnorm_fwd(x, g, *, name):
    S, D = x.shape
    T = _tile(S)

    def body(x_ref, g_ref, h_ref):
        xv = x_ref[...]
        r = lax.rsqrt(jnp.mean(xv * xv, axis=-1, keepdims=True) + EPS)
        h_ref[...] = (xv * r * g_ref[...]).astype(BF16)

    return pl.pallas_call(
        body,
        out_shape=jax.ShapeDtypeStruct((S, D), BF16),
        grid=(S // T,),
        in_specs=[pl.BlockSpec((T, D), lambda i: (i, 0)), pl.BlockSpec((1, D), lambda i: (0, 0))],
        out_specs=pl.BlockSpec((T, D), lambda i: (i, 0)),
        compiler_params=_cp(("parallel",)),
        name=name,
    )(x, g)


def rmsnorm_bwd(x, g, dh, dres, *, name):
    S, D = x.shape
    T = _tile(S)

    def body(x_ref, g_ref, dh_ref, dres_ref, dx_ref, dg_ref):
        i = pl.program_id(0)
        xv = x_ref[...]
        dhv = dh_ref[...].astype(F32)
        r = lax.rsqrt(jnp.mean(xv * xv, axis=-1, keepdims=True) + EPS)
        gd = dhv * g_ref[...]
        m = jnp.mean(xv * gd, axis=-1, keepdims=True)
        dx_ref[...] = dres_ref[...] + r * gd - xv * (r * r * r * m)

        @pl.when(i == 0)
        def _():
            dg_ref[...] = jnp.zeros_like(dg_ref)

        dg_ref[...] += jnp.sum(dhv * xv * r, axis=0, keepdims=True)

    row = pl.BlockSpec((T, D), lambda i: (i, 0))
    vec = pl.BlockSpec((1, D), lambda i: (0, 0))
    return pl.pallas_call(
        body,
        out_shape=(jax.ShapeDtypeStruct((S, D), F32), jax.ShapeDtypeStruct((1, D), F32)),
        grid=(S // T,),
        in_specs=[row, vec, row, row],
        out_specs=(row, vec),
        compiler_params=_cp(("arbitrary",)),
        name=name,
    )(x, g, dh, dres)


def _head_mean_matrix():
    r = lax.broadcasted_iota(jnp.int32, (WIDTH, WIDTH), 0) // HEAD_DIM
    c = lax.broadcasted_iota(jnp.int32, (WIDTH, WIDTH), 1) // HEAD_DIM
    return jnp.where(r == c, 1.0 / HEAD_DIM, 0.0).astype(BF16)


def _head_mean(v, mm):
    hi, lo = _split_hi_lo(v)
    return _dot(hi, mm, 1, 0) + _dot(lo, mm, 1, 0)


def qkv_prep(proj, gq, gk, *, name):
    S = proj.shape[0]
    T = _tile(S)

    def body(qa, ka, va, qb, kb, vb, gq_ref, gk_ref, oqa, oka, ova, oqb, okb, ovb):
        mm = _head_mean_matrix()
        for src, gref, dst, scale in ((qa, gq_ref, oqa, QK_SCALE), (ka, gk_ref, oka, 1.0)):
            v = src[...]
            r = lax.rsqrt(_head_mean(v * v, mm) + EPS)
            dst[...] = (v * r * gref[...] * scale).astype(BF16)
        oqb[...] = (qb[...] * QK_SCALE).astype(BF16)
        for src, dst in ((va, ova), (kb, okb), (vb, ovb)):
            dst[...] = src[...].astype(BF16)

    col = lambda j: pl.BlockSpec((T, WIDTH), lambda i, j=j: (i, j))
    vec = pl.BlockSpec((1, WIDTH), lambda i: (0, 0))
    out = pl.BlockSpec((T, WIDTH), lambda i: (i, 0))
    return pl.pallas_call(
        body,
        out_shape=tuple(jax.ShapeDtypeStruct((S, WIDTH), BF16) for _ in range(6)),
        grid=(S // T,),
        in_specs=[col(0), col(1), col(2), col(3), col(4), col(5), vec, vec],
        out_specs=tuple(out for _ in range(6)),
        compiler_params=_cp(("parallel",)),
        name=name,
    )(proj, proj, proj, proj, proj, proj, gq, gk)


def qknorm_bwd(proj, gq, gk, dqh, dkh, *, name):
    S = proj.shape[0]
    T = _tile(S)

    def body(qa, ka, gq_ref, gk_ref, dq_ref, dk_ref, oq, ok, ogq, ogk):
        i = pl.program_id(0)
        mm = _head_mean_matrix()

        @pl.when(i == 0)
        def _():
            ogq[...] = jnp.zeros_like(ogq)
            ogk[...] = jnp.zeros_like(ogk)

        for src, gref, dref, dst, gdst in ((qa, gq_ref, dq_ref, oq, ogq), (ka, gk_ref, dk_ref, ok, ogk)):
            v = src[...]
            dy = dref[...]
            r = lax.rsqrt(_head_mean(v * v, mm) + EPS)
            gd = dy * gref[...]
            m = _head_mean(v * gd, mm)
            dst[...] = (r * gd - v * (r * r * r * m)).astype(BF16)
            gdst[...] += jnp.sum(dy * v * r, axis=0, keepdims=True)

    col = lambda j: pl.BlockSpec((T, WIDTH), lambda i, j=j: (i, j))
    vec = pl.BlockSpec((1, WIDTH), lambda i: (0, 0))
    row = pl.BlockSpec((T, WIDTH), lambda i: (i, 0))
    return pl.pallas_call(
        body,
        out_shape=(jax.ShapeDtypeStruct((S, WIDTH), BF16), jax.ShapeDtypeStruct((S, WIDTH), BF16),
                   jax.ShapeDtypeStruct((1, WIDTH), F32), jax.ShapeDtypeStruct((1, WIDTH), F32)),
        grid=(S // T,),
        in_specs=[col(0), col(1), vec, vec, row, row],
        out_specs=(row, row, vec, vec),
        compiler_params=_cp(("arbitrary",)),
        name=name,
    )(proj, proj, gq, gk, dqh, dkh)


DIAG_W = 1024
N_VARIANTS = N_LEFT * CHUNK // A_Q + 1


def diagonal_onehot():
    jj = np.arange(DIAG_W)
    diff = np.where(jj < A_WIN, jj, jj - DIAG_W)
    out = np.zeros((N_VARIANTS, DIAG_W, REL_TABLE), np.float32)
    for v in range(N_VARIANTS):
        rel = np.clip(A_Q * v - diff, -(CHUNK - 1), MAX_REL) + (CHUNK - 1)
        out[v, jj, rel] = 1.0
    return out.reshape(N_VARIANTS * DIAG_W, REL_TABLE)


def exact_dot(a, b, *, name):
    def body(a_ref, b_ref, o_ref):
        o_ref[...] = jnp.dot(a_ref[...], b_ref[...], precision=lax.Precision.HIGHEST, preferred_element_type=F32)

    return pl.pallas_call(body, out_shape=jax.ShapeDtypeStruct((a.shape[0], b.shape[1]), F32),
                          compiler_params=_cp(), name=name)(a, b)


def _band_valid(v):
    qc = (lax.broadcasted_iota(jnp.int32, (A_Q, A_WIN), 0) + A_Q * v) // CHUNK
    kc = lax.broadcasted_iota(jnp.int32, (A_Q, A_WIN), 1) // CHUNK
    return (kc <= qc) & (kc >= qc - N_LEFT)


def bias_expand(diag, *, name):
    def body(d_ref, o_ref):
        rows = jnp.broadcast_to(d_ref[0, 0], (A_Q, DIAG_W))
        skew = pltpu.roll(rows, 0, 1, stride=1, stride_axis=0)
        o_ref[0, 0] = jnp.where(_band_valid(pl.program_id(0)), skew[:, :A_WIN], NEG)

    return pl.pallas_call(
        body,
        out_shape=jax.ShapeDtypeStruct((N_VARIANTS, N_HEADS, A_Q, A_WIN), F32),
        grid=(N_VARIANTS, N_HEADS),
        in_specs=[pl.BlockSpec((1, 1, 1, DIAG_W), lambda v, h: (v, h, 0, 0))],
        out_specs=pl.BlockSpec((1, 1, A_Q, A_WIN), lambda v, h: (v, h, 0, 0)),
        compiler_params=_cp(("parallel", "parallel")),
        name=name,
    )(diag)


def relbias_reduce(dbias, *, name):
    def body(db_ref, o_ref):
        acc = None
        for a in range(A_Q // 8):
            x = jnp.concatenate([db_ref[0, 0, 8 * a:8 * a + 8, :], jnp.zeros((8, DIAG_W - A_WIN), F32)], axis=1)
            x = pltpu.roll(x, DIAG_W - 8 * a, 1) if a else x
            acc = x if acc is None else acc + x
        row = lax.broadcasted_iota(jnp.int32, (8, DIAG_W), 0)
        for b in range(3):
            acc = jnp.where((row >> b) & 1 == 1, pltpu.roll(acc, DIAG_W - (1 << b), 1), acc)
        o_ref[0, 0] = jnp.sum(acc, axis=0, keepdims=True)

    return pl.pallas_call(
        body,
        out_shape=jax.ShapeDtypeStruct((N_VARIANTS, N_HEADS, 1, DIAG_W), F32),
        grid=(N_VARIANTS, N_HEADS),
        in_specs=[pl.BlockSpec((1, 1, A_Q, A_WIN), lambda v, h: (v, h, 0, 0))],
        out_specs=pl.BlockSpec((1, 1, 1, DIAG_W), lambda v, h: (v, h, 0, 0)),
        compiler_params=_cp(("parallel", "parallel")),
        name=name,
    )(dbias)


def _a_window_start(qb):
    return pl.multiple_of(jnp.maximum(qb * A_Q - N_LEFT * CHUNK, 0), A_Q)


def attn_a_fwd(q, k, v, biasm, *, name):
    S = q.shape[0]
    nq = S // A_Q

    def body(q_ref, k_ref, v_ref, b_ref, o_ref):
        qb = pl.program_id(1)
        start = _a_window_start(qb)
        outs = []
        for h in range(2):
            lanes = slice(h * HEAD_DIM, (h + 1) * HEAD_DIM)
            qh = q_ref[:, lanes]
            kw = k_ref[pl.ds(start, A_WIN), lanes]
            vw = v_ref[pl.ds(start, A_WIN), lanes]
            s = _dot(qh, kw, 1, 1) + b_ref[0, h]
            m = jnp.max(s, axis=-1, keepdims=True)
            e = jnp.exp(s - m)
            outs.append(_dot(e.astype(BF16), vw, 1, 0) * (1.0 / jnp.sum(e, axis=-1, keepdims=True)))
        o_ref[...] = jnp.concatenate(outs, axis=1).astype(BF16)

    qspec = pl.BlockSpec((A_Q, 2 * HEAD_DIM), lambda hp, qb: (qb, hp))
    kvspec = pl.BlockSpec((S, 2 * HEAD_DIM), lambda hp, qb: (0, hp))
    bspec = pl.BlockSpec((1, 2, A_Q, A_WIN), lambda hp, qb: (jnp.minimum(qb, N_VARIANTS - 1), hp, 0, 0))
    return pl.pallas_call(
        body,
        out_shape=jax.ShapeDtypeStruct((S, WIDTH), BF16),
        grid=(N_HEADS // 2, nq),
        in_specs=[qspec, kvspec, kvspec, bspec],
        out_specs=qspec,
        compiler_params=_cp(("parallel", "arbitrary")),
        name=name,
    )(q, k, v, biasm)


def attn_a_bwd(q, k, v, biasm, do, *, name):
    S = q.shape[0]
    nq = S // A_Q

    def body(q_ref, k_ref, v_ref, b_ref, do_ref, dq_ref, dk_ref, dv_ref, db_ref):
        qb = pl.program_id(1)
        start = _a_window_start(qb)

        @pl.when(qb == 0)
        def _():
            dk_ref[...] = jnp.zeros_like(dk_ref)
            dv_ref[...] = jnp.zeros_like(dv_ref)

        @pl.when(qb < N_VARIANTS)
        def _():
            db_ref[...] = jnp.zeros_like(db_ref)

        dqs = []
        for h in range(2):
            lanes = slice(h * HEAD_DIM, (h + 1) * HEAD_DIM)
            qh = q_ref[:, lanes]
            doh = do_ref[:, lanes]
            kw = k_ref[pl.ds(start, A_WIN), lanes]
            vw = v_ref[pl.ds(start, A_WIN), lanes]
            s = _dot(qh, kw, 1, 1) + b_ref[0, h]
            m = jnp.max(s, axis=-1, keepdims=True)
            e = jnp.exp(s - m)
            p = e * (1.0 / jnp.sum(e, axis=-1, keepdims=True))
            dp = _dot(doh, vw, 1, 1)
            delta = jnp.sum(p * dp, axis=-1, keepdims=True)
            ds = p * (dp - delta)
            db_ref[0, h] += ds
            dsb = ds.astype(BF16)
            dqs.append(_dot(dsb, kw, 1, 0) * QK_SCALE)
            dk_ref[pl.ds(start, A_WIN), lanes] += _dot(dsb, qh, 0, 0)
            dv_ref[pl.ds(start, A_WIN), lanes] += _dot(p.astype(BF16), doh, 0, 0)
        dq_ref[...] = jnp.concatenate(dqs, axis=1)

    qspec = pl.BlockSpec((A_Q, 2 * HEAD_DIM), lambda hp, qb: (qb, hp))
    kvspec = pl.BlockSpec((S, 2 * HEAD_DIM), lambda hp, qb: (0, hp))
    bspec = pl.BlockSpec((1, 2, A_Q, A_WIN), lambda hp, qb: (jnp.minimum(qb, N_VARIANTS - 1), hp, 0, 0))
    return pl.pallas_call(
        body,
        out_shape=(jax.ShapeDtypeStruct((S, WIDTH), F32), jax.ShapeDtypeStruct((S, WIDTH), F32),
                   jax.ShapeDtypeStruct((S, WIDTH), F32), jax.ShapeDtypeStruct((N_VARIANTS, N_HEADS, A_Q, A_WIN), F32)),
        grid=(N_HEADS // 2, nq),
        in_specs=[qspec, kvspec, kvspec, bspec, qspec],
        out_specs=(qspec, kvspec, kvspec, bspec),
        compiler_params=_cp(("parallel", "arbitrary")),
        name=name,
    )(q, k, v, biasm, do)


def _tri(kind):
    j = lax.broadcasted_iota(jnp.int32, (ATT_Q, ATT_Q), 0)
    s = lax.broadcasted_iota(jnp.int32, (ATT_Q, ATT_Q), 1)
    if kind == "gt":
        m = j > s
    elif kind == "le":
        m = j <= s
    else:
        m = j < s
    return jnp.where(m, 1.0, 0.0).astype(BF16)


def _cum(v, tri):
    hi, lo = _split_hi_lo(v)
    return _dot(hi, tri, 1, 0) + _dot(lo, tri, 1, 0)


def _log_sigmoids(z, mask):
    t = jnp.log(1.0 + jnp.exp(-jnp.abs(z)))
    keep = -(jnp.maximum(z, 0.0) + t)
    take = jnp.minimum(z, 0.0) - t
    return (keep if mask is None else jnp.where(mask, keep, 0.0)), take


def _strictly_before():
    row = lax.broadcasted_iota(jnp.int32, (ATT_Q, ATT_Q), 0)
    col = lax.broadcasted_iota(jnp.int32, (ATT_Q, ATT_Q), 1)
    return col < row


EXIT_LOG = -104.0


def attn_b_fwd(q, k, v, *, gather=None, name):
    S = q.shape[0]
    nq = S // ATT_Q
    plan, shards, fulls = gather if gather is not None else (None, [], [])
    ng = len(shards)

    def body(q_ref, k_ref, v_ref, *rest):
        hp = pl.program_id(0)
        qb = pl.program_id(1)
        o_ref, t_ref, n_ref = rest[2 * ng:2 * ng + 3]
        if plan is not None:
            comm = (rest[:ng], rest[2 * ng + 3:3 * ng + 3], rest[3 * ng + 3:])

            @pl.when(jnp.logical_and(hp == 0, qb == 0))
            def _():
                plan.start(*comm)

        tri = _tri("gt")

        def block(kb, carry, mask):
            ks = pl.multiple_of(kb * ATT_Q, ATT_Q)
            new = []
            for h in range(2):
                lanes = slice(h * HEAD_DIM, (h + 1) * HEAD_DIM)
                c, acc = carry[h]
                z = _dot(q_ref[:, lanes], k_ref[pl.ds(ks, ATT_Q), lanes], 1, 1)
                keep, take = _log_sigmoids(z, mask)
                w = jnp.exp(take + (_cum(keep, tri) + c))
                if mask is not None:
                    w = jnp.where(mask, w, 0.0)
                acc = acc + _dot(w.astype(BF16), v_ref[pl.ds(ks, ATT_Q), lanes], 1, 0)
                c = c + jnp.sum(keep, axis=-1, keepdims=True)
                new.append((c, acc))
            return jnp.maximum(jnp.max(new[0][0]), jnp.max(new[1][0])), tuple(new)

        def cond(state):
            it, cmax, _ = state
            return jnp.logical_and(it <= qb, cmax >= EXIT_LOG)

        def step(state):
            it, _, carry = state
            cmax, carry = block(qb - it, carry, None)
            return it + 1, cmax, carry

        init = tuple((jnp.zeros((ATT_Q, 1), F32), jnp.zeros((ATT_Q, HEAD_DIM), F32)) for _ in range(2))
        cmax, diag = block(qb, init, _strictly_before())
        visited, _, res = lax.while_loop(cond, step, (jnp.int32(1), cmax, diag))
        o_ref[...] = jnp.concatenate([res[0][1], res[1][1]], axis=1).astype(BF16)
        t_ref[...] = jnp.concatenate([jnp.broadcast_to(res[h][0], (ATT_Q, HEAD_DIM)) for h in range(2)], axis=1)
        n_ref[hp, qb] = visited.astype(F32)
        if plan is not None:
            @pl.when(jnp.logical_and(hp == N_HEADS // 2 - 1, qb == nq - 1))
            def _():
                plan.finish(*comm)

    qspec = pl.BlockSpec((ATT_Q, 2 * HEAD_DIM), lambda hp, qb: (qb, hp))
    kvspec = pl.BlockSpec((S, 2 * HEAD_DIM), lambda hp, qb: (0, hp))
    outs = pl.pallas_call(
        body,
        out_shape=(jax.ShapeDtypeStruct((S, WIDTH), BF16), jax.ShapeDtypeStruct((S, WIDTH), F32),
                   jax.ShapeDtypeStruct((N_HEADS // 2, nq), F32))
        + tuple(jax.ShapeDtypeStruct(f.shape, f.dtype) for f in fulls),
        grid=(N_HEADS // 2, nq),
        in_specs=[qspec, kvspec, kvspec] + [ANY] * (2 * ng),
        out_specs=(qspec, qspec, pl.BlockSpec(memory_space=pltpu.SMEM)) + tuple([ANY] * ng),
        scratch_shapes=plan.scratch_shapes() if plan is not None else (),
        input_output_aliases={3 + ng + i: 3 + i for i in range(ng)},
        compiler_params=_cp(("arbitrary", "arbitrary")),
        name=name,
    )(q, k, v, *shards, *fulls)
    return outs if plan is None else (outs[0], outs[1], outs[2], list(outs[3:]))


def attn_b_bwd(q, k, v, tot, nblk, do, *, scatter=None, name):
    S = q.shape[0]
    nq = S // ATT_Q
    plan, pairs16 = scatter if scatter is not None else (None, [])
    ns = len(pairs16)

    def body(q_ref, k_ref, v_ref, t_ref, n_ref, do_ref, *rest):
        hp = pl.program_id(0)
        qb = pl.program_id(1)
        dq_ref, dk_ref, dv_ref = rest[ns:ns + 3]
        if plan is not None:
            comm = (rest[:ns], rest[ns + 3:2 * ns + 3], rest[2 * ns + 3:])

            @pl.when(jnp.logical_and(hp == 0, qb == 0))
            def _():
                plan.start(*comm)

        first = jnp.clip(qb + 1 - n_ref[hp, qb].astype(jnp.int32), 0, qb + 1)
        tri_le = _tri("le")
        tri_lt = _tri("lt")

        @pl.when(qb == 0)
        def _():
            dk_ref[...] = jnp.zeros_like(dk_ref)
            dv_ref[...] = jnp.zeros_like(dv_ref)

        def block(kb, carry, mask):
            ks = pl.multiple_of(kb * ATT_Q, ATT_Q)
            new = []
            for h in range(2):
                lanes = slice(h * HEAD_DIM, (h + 1) * HEAD_DIM)
                cl, cg, dq = carry[h]
                qh = q_ref[:, lanes]
                doh = do_ref[:, lanes]
                kh = k_ref[pl.ds(ks, ATT_Q), lanes]
                vh = v_ref[pl.ds(ks, ATT_Q), lanes]
                totl = t_ref[:, h * HEAD_DIM:h * HEAD_DIM + 1]
                z = _dot(qh, kh, 1, 1)
                keep, take = _log_sigmoids(z, mask)
                sig = jnp.exp(take)
                w = sig * jnp.exp((totl - cl) - _cum(keep, tri_le))
                if mask is not None:
                    w = jnp.where(mask, w, 0.0)
                g = w * _dot(doh, vh, 1, 1)
                G = _dot(g.astype(BF16), tri_lt, 1, 0) + cg
                dz = g * (1.0 - sig) - sig * G
                if mask is not None:
                    dz = jnp.where(mask, dz, 0.0)
                dz = dz.astype(BF16)
                dq = dq + _dot(dz, kh, 1, 0)
                dk_ref[pl.ds(ks, ATT_Q), lanes] += _dot(dz, qh, 0, 0)
                dv_ref[pl.ds(ks, ATT_Q), lanes] += _dot(w.astype(BF16), doh, 0, 0)
                cl = cl + jnp.sum(keep, axis=-1, keepdims=True)
                cg = cg + jnp.sum(g, axis=-1, keepdims=True)
                new.append((cl, cg, dq))
            return tuple(new)

        init = tuple((jnp.zeros((ATT_Q, 1), F32), jnp.zeros((ATT_Q, 1), F32), jnp.zeros((ATT_Q, HEAD_DIM), F32))
                     for _ in range(2))
        res = lax.fori_loop(jnp.minimum(first, qb), qb, lambda kb, carry: block(kb, carry, None), init)
        res = block(qb, res, _strictly_before())
        dq_ref[...] = (jnp.concatenate([res[0][2], res[1][2]], axis=1) * QK_SCALE).astype(BF16)
        if plan is not None:
            @pl.when(jnp.logical_and(hp == N_HEADS // 2 - 1, qb == nq - 1))
            def _():
                plan.finish(*comm)

    qspec = pl.BlockSpec((ATT_Q, 2 * HEAD_DIM), lambda hp, qb: (qb, hp))
    kvspec = pl.BlockSpec((S, 2 * HEAD_DIM), lambda hp, qb: (0, hp))
    outs = pl.pallas_call(
        body,
        out_shape=(jax.ShapeDtypeStruct((S, WIDTH), BF16), jax.ShapeDtypeStruct((S, WIDTH), F32),
                   jax.ShapeDtypeStruct((S, WIDTH), F32)) + (tuple(plan.out_shapes()) if plan is not None else ()),
        grid=(N_HEADS // 2, nq),
        in_specs=[qspec, kvspec, kvspec, qspec, pl.BlockSpec(memory_space=pltpu.SMEM), qspec] + [ANY] * ns,
        out_specs=(qspec, kvspec, kvspec) + tuple([ANY] * ns),
        scratch_shapes=plan.scratch_shapes() if plan is not None else (),
        compiler_params=_cp(("arbitrary", "arbitrary")),
        name=name,
    )(q, k, v, tot, nblk, do, *pairs16)
    return outs if plan is None else (outs[0], outs[1], outs[2], list(outs[3:]))


U_COLBLK = 6


def _pool_counts(t0, rows):
    t = t0 + lax.broadcasted_iota(jnp.int32, (rows, WIDTH), 0)
    lane_grp = lax.broadcasted_iota(jnp.int32, (rows, WIDTH), 1) // GROUP_DIM
    win = jnp.where(lane_grp == 0, 2, jnp.where(lane_grp == 1, 4, jnp.where(lane_grp == 2, 8, 16)))
    cnt = jnp.minimum(t + 1, win)
    return 1.0 / cnt.astype(F32), lane_grp


def _window_sums(ext, shift_fn):
    s2 = ext + shift_fn(ext, 1)
    s4 = s2 + shift_fn(s2, 2)
    s8 = s4 + shift_fn(s4, 4)
    s16 = s8 + shift_fn(s8, 8)
    return s2, s4, s8, s16


def _select_group(lane_grp, s2, s4, s8, s16):
    return jnp.where(lane_grp == 0, s2, jnp.where(lane_grp == 1, s4, jnp.where(lane_grp == 2, s8, s16)))


def _pooled_tile(u_ref, h_ref, i, T):
    halo = jnp.where(i > 0, h_ref[...], 0.0)
    ext = jnp.concatenate([halo, u_ref[...]], axis=0)
    n = T + HALO
    sums = _window_sums(ext, lambda v, k: pltpu.roll(v, k, 0))
    inv, lane_grp = _pool_counts(i * T - HALO, n)
    pooled = _select_group(lane_grp, *sums) * inv - ext
    return pooled[HALO:, :]


def pool_fwd(proj, w_pool, scale, *, name):
    S = proj.shape[0]
    T = _tile(S)
    hb = T // HALO

    def body(u_ref, h_ref, w_ref, s_ref, o_ref):
        i = pl.program_id(0)
        pooled = _pooled_tile(u_ref, h_ref, i, T).astype(BF16)
        outs = [_dot(pooled[:, g * GROUP_DIM:(g + 1) * GROUP_DIM], w_ref[g], 1, 0) for g in range(4)]
        o_ref[...] = (jnp.concatenate(outs, axis=1) * s_ref[...]).astype(BF16)

    return pl.pallas_call(
        body,
        out_shape=jax.ShapeDtypeStruct((S, WIDTH), BF16),
        grid=(S // T,),
        in_specs=[pl.BlockSpec((T, WIDTH), lambda i: (i, U_COLBLK)),
                  pl.BlockSpec((HALO, WIDTH), lambda i: (jnp.maximum(i * hb - 1, 0), U_COLBLK)),
                  pl.BlockSpec((4, GROUP_DIM, GROUP_DIM), lambda i: (0, 0, 0)),
                  pl.BlockSpec((1, WIDTH), lambda i: (0, 0))],
        out_specs=pl.BlockSpec((T, WIDTH), lambda i: (i, 0)),
        compiler_params=_cp(("parallel",)),
        name=name,
    )(proj, proj, w_pool, scale)


def pool_bwd(proj, w_pool, scale, do, *, name):
    S = proj.shape[0]
    T = _tile(S)
    hb = T // HALO
    nt = S // T

    def body(u_ref, h_ref, w_ref, s_ref, do_ref, dof_ref, du_ref, dw_ref, ds_ref):
        i = pl.program_id(0)

        @pl.when(i == 0)
        def _():
            dw_ref[...] = jnp.zeros_like(dw_ref)
            ds_ref[...] = jnp.zeros_like(ds_ref)

        pooled = _pooled_tile(u_ref, h_ref, i, T).astype(BF16)
        dov = do_ref[...].astype(F32)
        fut = jnp.where(i < nt - 1, dof_ref[...].astype(F32), 0.0)
        dmix = (jnp.concatenate([dov, fut], axis=0) * s_ref[...]).astype(BF16)
        mixed, dpool = [], []
        for g in range(4):
            lanes = slice(g * GROUP_DIM, (g + 1) * GROUP_DIM)
            mixed.append(_dot(pooled[:, lanes], w_ref[g], 1, 0))
            dw_ref[g] += _dot(pooled[:, lanes], dmix[:T, lanes], 0, 0)
            dpool.append(_dot(dmix[:, lanes], w_ref[g], 1, 1))
        ds_ref[...] += jnp.sum(dov * jnp.concatenate(mixed, axis=1), axis=0, keepdims=True)
        dp = jnp.concatenate(dpool, axis=1)
        n = T + HALO
        inv, lane_grp = _pool_counts(i * T, n)
        sums = _window_sums(dp * inv, lambda v, k: pltpu.roll(v, n - k, 0))
        du = _select_group(lane_grp, *sums) - dp
        du_ref[...] = du[:T, :].astype(BF16)

    row = pl.BlockSpec((T, WIDTH), lambda i: (i, 0))
    return pl.pallas_call(
        body,
        out_shape=(jax.ShapeDtypeStruct((S, WIDTH), BF16), jax.ShapeDtypeStruct((4, GROUP_DIM, GROUP_DIM), F32),
                   jax.ShapeDtypeStruct((1, WIDTH), F32)),
        grid=(nt,),
        in_specs=[pl.BlockSpec((T, WIDTH), lambda i: (i, U_COLBLK)),
                  pl.BlockSpec((HALO, WIDTH), lambda i: (jnp.maximum(i * hb - 1, 0), U_COLBLK)),
                  pl.BlockSpec((4, GROUP_DIM, GROUP_DIM), lambda i: (0, 0, 0)),
                  pl.BlockSpec((1, WIDTH), lambda i: (0, 0)),
                  row,
                  pl.BlockSpec((HALO, WIDTH), lambda i: (jnp.minimum((i + 1) * hb, S // HALO - 1), 0))],
        out_specs=(row, pl.BlockSpec((4, GROUP_DIM, GROUP_DIM), lambda i: (0, 0, 0)),
                   pl.BlockSpec((1, WIDTH), lambda i: (0, 0))),
        compiler_params=_cp(("arbitrary",)),
        name=name,
    )(proj, proj, w_pool, scale, do, do)


GATE_BLK0 = GATE_COL0 // WIDTH


def merge_fwd(oa, ob, oc, proj, b_gate, wa, wb, wc, *, name):
    S = oa.shape[0]
    T = _tile(S)

    def body(oa_ref, ob_ref, oc_ref, ga, gb, gc, ba, bb, bc, wa_ref, wb_ref, wc_ref, m_ref):
        acc = None
        for o_ref, g_ref, b_ref, w_ref in ((oa_ref, ga, ba, wa_ref), (ob_ref, gb, bb, wb_ref), (oc_ref, gc, bc, wc_ref)):
            y = _dot(o_ref[...], w_ref[...], 1, 0)
            t = jax.nn.sigmoid(g_ref[...] + b_ref[...]) * y
            acc = t if acc is None else acc + t
        m_ref[...] = acc.astype(BF16)

    row = pl.BlockSpec((T, WIDTH), lambda i, n: (i, 0))
    gate = lambda b: pl.BlockSpec((T, WIDTH), lambda i, n, b=b: (i, GATE_BLK0 + 2 * b + n))
    bias = lambda b: pl.BlockSpec((1, WIDTH), lambda i, n, b=b: (0, 2 * b + n))
    wspec = pl.BlockSpec((WIDTH, WIDTH), lambda i, n: (0, n))
    return pl.pallas_call(
        body,
        out_shape=jax.ShapeDtypeStruct((S, D_MODEL), BF16),
        grid=(S // T, 2),
        in_specs=[row, row, row, gate(0), gate(1), gate(2), bias(0), bias(1), bias(2), wspec, wspec, wspec],
        out_specs=pl.BlockSpec((T, WIDTH), lambda i, n: (i, n)),
        compiler_params=_cp(("parallel", "parallel")),
        name=name,
    )(oa, ob, oc, proj, proj, proj, b_gate, b_gate, b_gate, wa, wb, wc)


def merge_bwd(dm, oa, ob, oc, proj, b_gate, wa, wb, wc, *, name):
    S = oa.shape[0]
    T = _tile(S)

    def body(dm_ref, oa_ref, ob_ref, oc_ref, ga, gb, gc, ba, bb, bc, wa_ref, wb_ref, wc_ref,
             ta, tb, tc, dga, dgb, dgc, dba, dbb, dbc):
        i = pl.program_id(1)
        dmv = dm_ref[...].astype(F32)
        for o_ref, g_ref, b_ref, w_ref, t_ref, dg_ref, db_ref in (
                (oa_ref, ga, ba, wa_ref, ta, dga, dba), (ob_ref, gb, bb, wb_ref, tb, dgb, dbb),
                (oc_ref, gc, bc, wc_ref, tc, dgc, dbc)):
            y = _dot(o_ref[...], w_ref[...], 1, 0)
            gate = jax.nn.sigmoid(g_ref[...] + b_ref[...])
            t_ref[...] = (gate * dmv).astype(BF16)
            dgl = dmv * y * gate * (1.0 - gate)
            dg_ref[...] = dgl.astype(BF16)

            @pl.when(i == 0)
            def _():
                db_ref[...] = jnp.zeros_like(db_ref)

            db_ref[...] += jnp.sum(dgl, axis=0, keepdims=True)

    row = pl.BlockSpec((T, WIDTH), lambda n, i: (i, 0))
    half = pl.BlockSpec((T, WIDTH), lambda n, i: (i, n))
    gate = lambda b: pl.BlockSpec((T, WIDTH), lambda n, i, b=b: (i, GATE_BLK0 + 2 * b + n))
    bias = lambda b: pl.BlockSpec((1, WIDTH), lambda n, i, b=b: (0, 2 * b + n))
    wspec = pl.BlockSpec((WIDTH, WIDTH), lambda n, i: (0, n))
    bvec = pl.BlockSpec((1, WIDTH), lambda n, i: (0, n))
    act = jax.ShapeDtypeStruct((S, D_MODEL), BF16)
    vec = jax.ShapeDtypeStruct((1, D_MODEL), F32)
    return pl.pallas_call(
        body,
        out_shape=(act, act, act, act, act, act, vec, vec, vec),
        grid=(2, S // T),
        in_specs=[half, row, row, row, gate(0), gate(1), gate(2), bias(0), bias(1), bias(2), wspec, wspec, wspec],
        out_specs=(half, half, half, half, half, half, bvec, bvec, bvec),
        compiler_params=_cp(("parallel", "arbitrary")),
        name=name,
    )(dm, oa, ob, oc, proj, proj, proj, b_gate, b_gate, b_gate, wa, wb, wc)


FF_T = 256
FF_BLKS = D_FF // FF_T


def _silu_parts(x):
    s = jax.nn.sigmoid(x)
    return x * s, s


def _conv3(ext, w_ref, b_ref):
    return (b_ref[...] + w_ref[0:1, :] * pltpu.roll(ext, 2, 0) + w_ref[1:2, :] * pltpu.roll(ext, 1, 0)
            + w_ref[2:3, :] * ext)


def conv_glu_fwd(u, conv_w, conv_b, *, name):
    S = u.shape[0]
    T = _tile(S)
    hb = T // CONV_HALO

    def body(ug, ugh, uv, uvh, wg, wv, bg, bv, a_ref):
        i = pl.program_id(1)
        cs = []
        for m_ref, h_ref, w_ref, b_ref in ((ug, ugh, wg, bg), (uv, uvh, wv, bv)):
            halo = jnp.where(i > 0, h_ref[...], 0.0)
            ext = jnp.concatenate([halo, m_ref[...]], axis=0)
            cs.append(_conv3(ext, w_ref, b_ref)[CONV_HALO:, :])
        act, _ = _silu_parts(cs[0])
        a_ref[...] = (act * cs[1]).astype(BF16)

    main = lambda o: pl.BlockSpec((T, FF_T), lambda c, i, o=o: (i, c + o))
    halo = lambda o: pl.BlockSpec((CONV_HALO, FF_T), lambda c, i, o=o: (jnp.maximum(i * hb - 1, 0), c + o))
    wsp = lambda o: pl.BlockSpec((3, FF_T), lambda c, i, o=o: (0, c + o))
    bsp = lambda o: pl.BlockSpec((1, FF_T), lambda c, i, o=o: (0, c + o))
    return pl.pallas_call(
        body,
        out_shape=jax.ShapeDtypeStruct((S, D_FF), BF16),
        grid=(FF_BLKS, S // T),
        in_specs=[main(0), halo(0), main(FF_BLKS), halo(FF_BLKS), wsp(0), wsp(FF_BLKS), bsp(0), bsp(FF_BLKS)],
        out_specs=pl.BlockSpec((T, FF_T), lambda c, i: (i, c)),
        compiler_params=_cp(("parallel", "parallel")),
        name=name,
    )(u, u, u, u, conv_w, conv_w, conv_b, conv_b)


def conv_glu_bwd(u, conv_w, conv_b, da, *, name):
    S = u.shape[0]
    T = _tile(S)
    hb = T // CONV_HALO
    nt = S // T
    n = T + 2 * CONV_HALO

    def body(ug, ugp, ugf, uv, uvp, uvf, wg, wv, bg, bv, da_ref, daf_ref,
             dug, duv, dwg, dwv, dbg, dbv):
        i = pl.program_id(1)
        first, last = i == 0, i == nt - 1
        exts, cs = [], []
        for m_ref, p_ref, f_ref, w_ref, b_ref in ((ug, ugp, ugf, wg, bg), (uv, uvp, uvf, wv, bv)):
            ext = jnp.concatenate([jnp.where(first, 0.0, p_ref[...]), m_ref[...], jnp.where(last, 0.0, f_ref[...])], axis=0)
            exts.append(ext)
            cs.append(_conv3(ext, w_ref, b_ref))
        dae = jnp.concatenate([jnp.zeros((CONV_HALO, FF_T), F32), da_ref[...].astype(F32),
                               jnp.where(last, 0.0, daf_ref[...].astype(F32))], axis=0)
        act, sg = _silu_parts(cs[0])
        dcs = (dae * cs[1] * (sg * (1.0 + cs[0] * (1.0 - sg))), dae * act)
        main = slice(CONV_HALO, CONV_HALO + T)
        for ext, dc, w_ref, du_ref, dw_ref, db_ref in ((exts[0], dcs[0], wg, dug, dwg, dbg),
                                                       (exts[1], dcs[1], wv, duv, dwv, dbv)):
            du = (w_ref[2:3, :] * dc + w_ref[1:2, :] * pltpu.roll(dc, n - 1, 0) + w_ref[0:1, :] * pltpu.roll(dc, n - 2, 0))
            du_ref[...] = du[main, :].astype(BF16)
            dcm = dc[main, :]
            rows = [jnp.sum(dcm * pltpu.roll(ext, 2 - j, 0)[main, :], axis=0, keepdims=True) if j < 2
                    else jnp.sum(dcm * ext[main, :], axis=0, keepdims=True) for j in range(3)]

            @pl.when(first)
            def _():
                dw_ref[...] = jnp.zeros_like(dw_ref)
                db_ref[...] = jnp.zeros_like(db_ref)

            dw_ref[...] += jnp.concatenate(rows, axis=0)
            db_ref[...] += jnp.sum(dcm, axis=0, keepdims=True)

    main = lambda o: pl.BlockSpec((T, FF_T), lambda c, i, o=o: (i, c + o))
    past = lambda o: pl.BlockSpec((CONV_HALO, FF_T), lambda c, i, o=o: (jnp.maximum(i * hb - 1, 0), c + o))
    fut = lambda o: pl.BlockSpec((CONV_HALO, FF_T), lambda c, i, o=o: (jnp.minimum((i + 1) * hb, S // CONV_HALO - 1), c + o))
    wsp = lambda o: pl.BlockSpec((3, FF_T), lambda c, i, o=o: (0, c + o))
    bsp = lambda o: pl.BlockSpec((1, FF_T), lambda c, i, o=o: (0, c + o))
    return pl.pallas_call(
        body,
        out_shape=(jax.ShapeDtypeStruct((S, D_FF), BF16), jax.ShapeDtypeStruct((S, D_FF), BF16),
                   jax.ShapeDtypeStruct((3, D_FF), F32), jax.ShapeDtypeStruct((3, D_FF), F32),
                   jax.ShapeDtypeStruct((1, D_FF), F32), jax.ShapeDtypeStruct((1, D_FF), F32)),
        grid=(FF_BLKS, nt),
        in_specs=[main(0), past(0), fut(0), main(FF_BLKS), past(FF_BLKS), fut(FF_BLKS),
                  wsp(0), wsp(FF_BLKS), bsp(0), bsp(FF_BLKS), main(0), fut(0)],
        out_specs=(main(0), main(0), wsp(0), wsp(0), bsp(0), bsp(0)),
        compiler_params=_cp(("parallel", "arbitrary")),
        name=name,
    )(u, u, u, u, u, u, conv_w, conv_w, conv_b, conv_b, da, da)


def loss_head(y, target, *, name):
    S, D = y.shape
    T = _tile(S)

    def body(y_ref, t_ref, dy_ref, l_ref):
        i = pl.program_id(0)
        err = y_ref[...] - t_ref[...]
        dy_ref[...] = err * (1.0 / D)

        @pl.when(i == 0)
        def _():
            l_ref[...] = jnp.zeros_like(l_ref)

        l_ref[...] += 0.5 * jnp.sum(jnp.mean(err * err, axis=-1, keepdims=True))

    row = pl.BlockSpec((T, D), lambda i: (i, 0))
    return pl.pallas_call(
        body,
        out_shape=(jax.ShapeDtypeStruct((S, D), F32), jax.ShapeDtypeStruct((8, 128), F32)),
        grid=(S // T,),
        in_specs=[row, row],
        out_specs=(row, pl.BlockSpec((8, 128), lambda i: (0, 0))),
        compiler_params=_cp(("arbitrary",)),
        name=name,
    )(y, target)


ELEMS_PER_BLOCK = 256 * 1024


def _rows_tile(rows, cols):
    if rows * cols <= ELEMS_PER_BLOCK or rows % 8:
        return rows
    best = 8
    for tr in range(8, rows + 1, 8):
        if rows % tr == 0 and tr * cols <= ELEMS_PER_BLOCK:
            best = tr
    return best


def _adamw_math(g, w_ref, m_ref, v_ref, g_out, d_out, m_out, v_out):
    mn = ADAM_B1 * m_ref[...] + (1.0 - ADAM_B1) * g
    vn = ADAM_B2 * v_ref[...] + (1.0 - ADAM_B2) * (g * g)
    m_hat = mn / (1.0 - ADAM_B1 ** ADAM_STEP)
    v_hat = vn / (1.0 - ADAM_B2 ** ADAM_STEP)
    g_out[...] = g
    d_out[...] = -ADAM_LR * (m_hat / (jnp.sqrt(v_hat) + ADAM_EPS) + ADAM_WD * w_ref[...])
    m_out[...] = mn
    v_out[...] = vn


def adamw(w, m, v, g, *, name):
    rows, cols = w.shape
    tr = _rows_tile(rows, cols)

    def body(w_ref, m_ref, v_ref, g_ref, g_out, d_out, m_out, v_out):
        _adamw_math(g_ref[...], w_ref, m_ref, v_ref, g_out, d_out, m_out, v_out)

    spec = pl.BlockSpec((tr, cols), lambda i: (i, 0))
    shp = jax.ShapeDtypeStruct((rows, cols), F32)
    return pl.pallas_call(
        body,
        out_shape=(shp, shp, shp, shp),
        grid=(rows // tr,),
        in_specs=[spec] * 4,
        out_specs=(spec, spec, spec, spec),
        compiler_params=_cp(("parallel",)),
        name=name,
    )(w, m, v, g)


ANY = pl.BlockSpec(memory_space=pl.ANY)
STAGE_BYTES = 2 * 1024 * 1024


def _mesh_pos():
    return lax.axis_index("x"), lax.axis_index("y"), lax.axis_index("c")


def _chip_peers(x, y):
    return [(1 - x, y), (x, 1 - y), (1 - x, 1 - y)]


def _all_peers(x, y, c):
    return [((1 - x) if (r >> 2) & 1 else x, (1 - y) if (r >> 1) & 1 else y, (1 - c) if r & 1 else c)
            for r in range(1, 8)]


def _shard_slice(ref, axis, j, size, layer=None):
    idx = [slice(None)] * 3
    idx[axis] = pl.ds(pl.multiple_of(j * size, 128 if axis == 2 else 16), size)
    if layer is not None:
        idx[0] = pl.ds(layer, 1)
    return ref.at[tuple(idx)]


class LayerGather:
    def __init__(self, shards, axes, layer):
        self.nt = len(shards)
        self.axes = list(axes)
        self.layer = layer
        self.shapes = [s.shape for s in shards]
        self.dtypes = [s.dtype for s in shards]
        self.sizes = [s.shape[a] for s, a in zip(shards, axes)]
        self.split = [s.shape[1] % 32 == 0 for s in shards]
        self.half_rows = [s.shape[1] // 2 if sp else s.shape[1] for s, sp in zip(shards, self.split)]
        self.chunk_rows = []
        for s in shards:
            rt = s.shape[1]
            while rt % 32 == 0 and rt * s.shape[2] * s.dtype.itemsize > STAGE_BYTES:
                rt //= 2
            self.chunk_rows.append(rt)

    def out_shapes(self):
        out = []
        for shp, a, sz, dt in zip(self.shapes, self.axes, self.sizes, self.dtypes):
            shp = list(shp)
            shp[a] = 4 * sz
            out.append(jax.ShapeDtypeStruct(tuple(shp), dt))
        return out

    def scratch_shapes(self):
        return ([pltpu.VMEM((1, rt, shp[2]), dt) for shp, rt, dt in zip(self.shapes, self.chunk_rows, self.dtypes)]
                + [pltpu.SemaphoreType.DMA((2,))] + [pltpu.SemaphoreType.DMA((3 * self.nt,)) for _ in range(4)])

    def _views(self, ins, outs, scratch):
        nt = self.nt
        stage, stage_sems = scratch[:nt], scratch[nt]
        ici_send, ici_recv, d2d_send, d2d_recv = scratch[nt + 1:]
        x, y, c = _mesh_pos()
        mine = 2 * x + y
        peers = _chip_peers(x, y)
        layer = pl.ds(self.layer, 1)

        def rows(t, half, r0=0, n=None):
            hr = self.half_rows[t]
            if n is None:
                return pl.ds(pl.multiple_of(half * hr, 16), hr) if self.split[t] else pl.ds(0, hr)
            return pl.ds(r0, n)

        def placed(t, blk, row_sel, row_len):
            sz = self.sizes[t]
            if self.axes[t] == 2:
                return outs[t].at[layer, row_sel, pl.ds(pl.multiple_of(blk * sz, 128), sz)]
            return outs[t].at[layer, pl.ds(pl.multiple_of(blk * sz, 16) + row_sel.start, row_len), :]

        def ici(t, k, blk):
            px, py = peers[k]
            sel = rows(t, c)
            return pltpu.make_async_remote_copy(
                src_ref=ins[t].at[layer, sel, :], dst_ref=placed(t, blk, sel, self.half_rows[t]),
                send_sem=ici_send.at[3 * t + k], recv_sem=ici_recv.at[3 * t + k],
                device_id=(px, py, c), device_id_type=MESH_T)

        def d2d(t, k, half):
            px, py = peers[k]
            piece = placed(t, 2 * px + py, rows(t, half), self.half_rows[t])
            return pltpu.make_async_remote_copy(
                src_ref=piece, dst_ref=piece, send_sem=d2d_send.at[3 * t + k], recv_sem=d2d_recv.at[3 * t + k],
                device_id=(x, y, 1 - c), device_id_type=MESH_T)

        def own_chunk(t, r0):
            rt = self.chunk_rows[t]
            sel = pl.ds(r0, rt)
            return ins[t].at[layer, sel, :], placed(t, mine, sel, rt), stage[t], stage_sems

        return c, mine, peers, ici, d2d, own_chunk

    def start(self, ins, outs, scratch):
        c, mine, peers, ici, d2d, own_chunk = self._views(ins, outs, scratch)
        for t in range(self.nt):
            for k in range(3):
                ici(t, k, mine).start()
        for t in range(self.nt):
            for r0 in range(0, self.shapes[t][1], self.chunk_rows[t]):
                src, dst, buf, sems = own_chunk(t, r0)
                load = pltpu.make_async_copy(src, buf, sems.at[0])
                load.start()
                load.wait()
                store = pltpu.make_async_copy(buf, dst, sems.at[1])
                store.start()
                store.wait()

    def finish(self, ins, outs, scratch):
        c, mine, peers, ici, d2d, own_chunk = self._views(ins, outs, scratch)
        for t in range(self.nt):
            for k, (px, py) in enumerate(peers):
                ici(t, k, 2 * px + py).wait_recv()
                if self.split[t]:
                    d2d(t, k, c).start()
        for t in range(self.nt):
            for k in range(3):
                if self.split[t]:
                    d2d(t, k, 1 - c).wait_recv()
        for t in range(self.nt):
            for k in range(3):
                ici(t, k, mine).wait_send()
                if self.split[t]:
                    d2d(t, k, c).wait_send()


def all_gather_layer(shards, axes, layer, *, name):
    plan = LayerGather(shards, axes, layer)
    nt = plan.nt

    def body(*refs):
        ins, outs, scratch = refs[:nt], refs[nt:2 * nt], refs[2 * nt:]
        plan.start(ins, outs, scratch)
        plan.finish(ins, outs, scratch)

    return pl.pallas_call(
        body,
        out_shape=tuple(plan.out_shapes()),
        in_specs=[ANY] * nt,
        out_specs=tuple([ANY] * nt),
        scratch_shapes=plan.scratch_shapes(),
        name=name,
    )(*shards)


class HalfLayout:
    def __init__(self, shape, axis):
        self.R, self.C = shape
        self.axis = axis
        if axis == 1:
            self.hr, self.pw = self.R // 2, self.C // 4
            self.half_shape = (self.hr, self.C)
        else:
            self.hr, self.pw = self.R // 8, self.C
            self.half_shape = (4 * self.hr, self.C)
        self.tr = _rows_tile(self.hr, self.pw)
        self.nr = self.hr // self.tr

    def in_grad(self, ref, blk, half):
        if self.axis == 1:
            return ref.at[pl.ds(pl.multiple_of(half * self.hr, 16), self.hr), pl.ds(pl.multiple_of(blk * self.pw, 128), self.pw)]
        return ref.at[pl.ds(pl.multiple_of((2 * blk + half) * self.hr, 16), self.hr), :]

    def in_half(self, ref, blk):
        if self.axis == 1:
            return ref.at[:, pl.ds(pl.multiple_of(blk * self.pw, 128), self.pw)]
        return ref.at[pl.ds(pl.multiple_of(blk * self.hr, 16), self.hr), :]

    def grad_spec(self):
        if self.axis == 1:
            return pl.BlockSpec((self.tr, self.pw), lambda j, i, s: (s[0] * self.nr + i, j))
        return pl.BlockSpec((self.tr, self.pw), lambda j, i, s: ((2 * j + s[0]) * self.nr + i, 0))

    def half_spec(self):
        if self.axis == 1:
            return pl.BlockSpec((self.tr, self.pw), lambda j, i, s: (i, j))
        return pl.BlockSpec((self.tr, self.pw), lambda j, i, s: (j * self.nr + i, 0))


def half_exchange(grads, layouts, *, name):
    nt = len(grads)
    pieces = [(t, j) for t in range(nt) for j in (range(4) if layouts[t].axis == 0 else range(1))]

    def body(*refs):
        ins, outs = refs[:nt], refs[nt:2 * nt]
        send_sems, recv_sems = refs[2 * nt:]
        x, y, c = _mesh_pos()
        cps = []
        for n, (t, j) in enumerate(pieces):
            lay = layouts[t]
            if lay.axis == 1:
                src = ins[t].at[pl.ds(pl.multiple_of((1 - c) * lay.hr, 16), lay.hr), :]
                dst = outs[t]
            else:
                src = lay.in_grad(ins[t], j, 1 - c)
                dst = lay.in_half(outs[t], j)
            cp = pltpu.make_async_remote_copy(src_ref=src, dst_ref=dst, send_sem=send_sems.at[n], recv_sem=recv_sems.at[n],
                                              device_id=(x, y, 1 - c), device_id_type=MESH_T)
            cp.start()
            cps.append(cp)
        for cp in cps:
            cp.wait_recv()
        for cp in cps:
            cp.wait_send()

    return pl.pallas_call(
        body,
        out_shape=tuple(jax.ShapeDtypeStruct(lay.half_shape, F32) for lay in layouts),
        in_specs=[ANY] * nt,
        out_specs=tuple([ANY] * nt),
        scratch_shapes=[pltpu.SemaphoreType.DMA((len(pieces),)), pltpu.SemaphoreType.DMA((len(pieces),))],
        name=name,
    )(*grads)


def pair_sum(grad, other, lay, core, *, name):
    def body(c_ref, g_ref, o_ref, s32_ref, s16_ref):
        s = g_ref[...] + o_ref[...]
        s32_ref[...] = s
        s16_ref[...] = s.astype(BF16)

    return pl.pallas_call(
        body,
        out_shape=(jax.ShapeDtypeStruct(lay.half_shape, F32), jax.ShapeDtypeStruct(lay.half_shape, BF16)),
        grid_spec=pltpu.PrefetchScalarGridSpec(
            num_scalar_prefetch=1, grid=(4, lay.nr),
            in_specs=[lay.grad_spec(), lay.half_spec()],
            out_specs=(lay.half_spec(), lay.half_spec())),
        compiler_params=_cp(("parallel", "parallel")),
        name=name,
    )(core, grad, other)


class BlockScatter:
    def __init__(self, layouts):
        self.layouts = layouts
        self.nt = len(layouts)

    def out_shapes(self):
        return [jax.ShapeDtypeStruct((3, lay.hr, lay.pw), BF16) for lay in self.layouts]

    def scratch_shapes(self):
        return [pltpu.SemaphoreType.DMA((3 * self.nt,)), pltpu.SemaphoreType.DMA((3 * self.nt,))]

    def _copies(self, pairs16, recv, scratch):
        send_sems, recv_sems = scratch
        x, y, c = _mesh_pos()
        return [pltpu.make_async_remote_copy(
            src_ref=lay.in_half(pairs16[t], 2 * px + py), dst_ref=recv[t].at[k],
            send_sem=send_sems.at[3 * t + k], recv_sem=recv_sems.at[3 * t + k],
            device_id=(px, py, c), device_id_type=MESH_T)
            for t, lay in enumerate(self.layouts) for k, (px, py) in enumerate(_chip_peers(x, y))]

    def start(self, pairs16, recv, scratch):
        for cp in self._copies(pairs16, recv, scratch):
            cp.start()

    def finish(self, pairs16, recv, scratch):
        copies = self._copies(pairs16, recv, scratch)
        for cp in copies:
            cp.wait_recv()
        for cp in copies:
            cp.wait_send()


def gather_small(small, *, name):
    def body(small_in, small_out, ssend, srecv):
        x, y, c = _mesh_pos()
        me = 4 * x + 2 * y + c
        sends, recvs = [], []
        for r, (px, py, pc) in enumerate(_all_peers(x, y, c)):
            def mk(slot, r=r, px=px, py=py, pc=pc):
                return pltpu.make_async_remote_copy(
                    src_ref=small_in, dst_ref=small_out.at[slot], send_sem=ssend.at[r], recv_sem=srecv.at[r],
                    device_id=(px, py, pc), device_id_type=MESH_T)
            snd = mk(me)
            snd.start()
            sends.append(snd)
            recvs.append(mk(4 * px + 2 * py + pc))
        for r in recvs:
            r.wait_recv()
        for s in sends:
            s.wait_send()

    return pl.pallas_call(
        body,
        out_shape=jax.ShapeDtypeStruct((8,) + small.shape, F32),
        in_specs=[ANY],
        out_specs=ANY,
        scratch_shapes=[pltpu.SemaphoreType.DMA((7,)), pltpu.SemaphoreType.DMA((7,))],
        name=name,
    )(small)


def sum_chips(pair32, recv, lay, chip, *, name):
    def body(j_ref, p_ref, r_ref, s_ref):
        acc = p_ref[...]
        for k in range(3):
            acc = acc + r_ref[k].astype(F32)
        s_ref[...] = acc

    if lay.axis == 1:
        own = pl.BlockSpec((lay.tr, lay.pw), lambda i, j: (i, j[0]))
    else:
        own = pl.BlockSpec((lay.tr, lay.pw), lambda i, j: (j[0] * lay.nr + i, 0))
    return pl.pallas_call(
        body,
        out_shape=jax.ShapeDtypeStruct((lay.hr, lay.pw), F32),
        grid_spec=pltpu.PrefetchScalarGridSpec(
            num_scalar_prefetch=1, grid=(lay.nr,),
            in_specs=[own, pl.BlockSpec((3, lay.tr, lay.pw), lambda i, j: (0, i, 0))],
            out_specs=pl.BlockSpec((lay.tr, lay.pw), lambda i, j: (i, 0))),
        compiler_params=_cp(("parallel",)),
        name=name,
    )(chip, pair32, recv)


def sum_devices(gathered, own, me, *, name):
    _, R, C = gathered.shape

    def body(me_ref, g_ref, o_ref, s_ref):
        acc = None
        for k in range(8):
            part = jnp.where(me_ref[0] == k, o_ref[...], g_ref[k])
            acc = part if acc is None else acc + part
        s_ref[...] = acc

    return pl.pallas_call(
        body,
        out_shape=jax.ShapeDtypeStruct((R, C), F32),
        grid_spec=pltpu.PrefetchScalarGridSpec(
            num_scalar_prefetch=1, grid=(1,),
            in_specs=[pl.BlockSpec((8, R, C), lambda i, m: (0, 0, 0)), pl.BlockSpec((R, C), lambda i, m: (0, 0))],
            out_specs=pl.BlockSpec((R, C), lambda i, m: (0, 0))),
        compiler_params=_cp(("arbitrary",)),
        name=name,
    )(me, gathered, own)


def sibling_swap(parts, *, name):
    nt = len(parts)

    def body(*refs):
        ins, outs = refs[:nt], refs[nt:2 * nt]
        send_sems, recv_sems = refs[2 * nt:]
        x, y, c = _mesh_pos()
        cps = []
        for t in range(nt):
            cp = pltpu.make_async_remote_copy(src_ref=ins[t], dst_ref=outs[t], send_sem=send_sems.at[t],
                                              recv_sem=recv_sems.at[t], device_id=(x, y, 1 - c), device_id_type=MESH_T)
            cp.start()
            cps.append(cp)
        for cp in cps:
            cp.wait_recv()
        for cp in cps:
            cp.wait_send()

    return pl.pallas_call(
        body,
        out_shape=tuple(jax.ShapeDtypeStruct(p.shape, p.dtype) for p in parts),
        in_specs=[ANY] * nt,
        out_specs=tuple([ANY] * nt),
        scratch_shapes=[pltpu.SemaphoreType.DMA((nt,)), pltpu.SemaphoreType.DMA((nt,))],
        name=name,
    )(*parts)


def adamw_halves(w, m, v, mine, other, lay, core, *, name):
    _, r, c = w.shape
    tr, nr = lay.tr, lay.nr
    assert (r, c) == (2 * lay.hr, lay.pw), (w.shape, lay.hr, lay.pw)

    def body(c_ref, w_ref, m_ref, v_ref, *rest):
        g_refs, outs = rest[:2 * DEPTH], rest[2 * DEPTH:]
        l, h = pl.program_id(0), pl.program_id(1)
        g = None
        for d in range(DEPTH):
            gd = jnp.where(h == c_ref[0], g_refs[d][...], g_refs[DEPTH + d][...])
            g = gd if g is None else jnp.where(l == d, gd, g)
        _adamw_math(g, w_ref, m_ref, v_ref, *outs)

    full = pl.BlockSpec((None, tr, c), lambda l, h, i, s: (l, h * nr + i, 0))

    def part(d, is_mine):
        def index(l, h, i, s):
            used = jnp.logical_and(l == d, (h == s[0]) == is_mine)
            return jnp.where(used, i, 0), 0
        return pl.BlockSpec((tr, c), index)

    shp = jax.ShapeDtypeStruct((DEPTH, r, c), F32)
    return pl.pallas_call(
        body,
        out_shape=(shp, shp, shp, shp),
        grid_spec=pltpu.PrefetchScalarGridSpec(
            num_scalar_prefetch=1, grid=(DEPTH, 2, nr),
            in_specs=[full, full, full] + [part(d, True) for d in range(DEPTH)] + [part(d, False) for d in range(DEPTH)],
            out_specs=(full, full, full, full)),
        compiler_params=_cp(("arbitrary", "arbitrary", "arbitrary")),
        name=name,
    )(core, w, m, v, *mine, *other)


WEIGHTS = ("norm_mix", "w_in", "b_gate", "q_norm_a", "k_norm_a", "rel_bias_a", "w_pool", "pool_scale",
           "w_branch_a", "w_branch_b", "w_branch_c", "w_out", "norm_ffn", "w_up", "conv_w", "conv_b", "w_down")
SHARDED = {"w_in": 2, "w_branch_a": 2, "w_branch_b": 2, "w_branch_c": 2, "w_out": 1, "w_up": 2, "conv_w": 2,
           "w_down": 1}
REPLICATED = tuple(n for n in WEIGHTS if n not in SHARDED)
MATMUL_WEIGHTS = tuple(n for n in SHARDED if n != "conv_w")
SMALL_WEIGHTS = tuple(n for n in WEIGHTS if n not in MATMUL_WEIGHTS)
SMALL_ROWS = 1496


def _layer_fwd(x, p, full, tables, rest=None, prefetch=None):
    l = p["l"]
    diag = exact_dot(p["rel_bias_a"], tables["onehot_t"], name="bias_diagonals")
    diag = diag.reshape(N_HEADS, N_VARIANTS, 1, DIAG_W).transpose(1, 0, 2, 3)
    biasm = bias_expand(diag, name="bias_expand")
    gq8 = jnp.tile(p["q_norm_a"], N_HEADS)[None]
    gk8 = jnp.tile(p["k_norm_a"], N_HEADS)[None]
    h = rmsnorm_fwd(x, p["norm_mix"][None], name="rmsnorm_fwd")
    if rest is None:
        proj = matmul(h, full["w_in"], b_layer=l, name="mm_in")
    else:
        plan, shards, names = rest
        proj, gathered = matmul(h, full["w_in"], b_layer=l, behind=(plan, shards), name="mm_in_gather")
        full = {**full, **dict(zip(names, gathered))}
    qa, ka, va, qb, kb, vb = qkv_prep(proj, gq8, gk8, name="qkv_prep")
    oa = attn_a_fwd(qa, ka, va, biasm, name="attn_a_fwd")
    if prefetch is None:
        ob, tot, nblk = attn_b_fwd(qb, kb, vb, name="attn_b_fwd")
    else:
        plan, shards, names = prefetch
        ob, tot, nblk, filled = attn_b_fwd(qb, kb, vb, gather=(plan, shards, [full[n] for n in names]),
                                           name="attn_b_fwd_gather")
        full = dict(zip(names, filled))
    wpool = p["w_pool"].astype(BF16)
    oc = pool_fwd(proj, wpool, p["pool_scale"][None], name="pool_fwd")
    merged = merge_fwd(oa, ob, oc, proj, p["b_gate"][None], full["w_branch_a"][l], full["w_branch_b"][l],
                       full["w_branch_c"][l], name="merge_fwd")
    x1 = matmul(merged, full["w_out"], b_layer=l, add=x, name="mm_out")
    h2 = rmsnorm_fwd(x1, p["norm_ffn"][None], name="rmsnorm_fwd")
    u = matmul(h2, full["w_up"], b_layer=l, name="mm_up")
    a = conv_glu_fwd(u, full["conv_w"][l], p["conv_b"][None], name="conv_glu_fwd")
    x2 = matmul(a, full["w_down"], b_layer=l, add=x1, name="mm_down")
    saved = dict(x=x, h=h, proj=proj, qa=qa, ka=ka, va=va, qb=qb, kb=kb, vb=vb, oa=oa, ob=ob, tot=tot, nblk=nblk, oc=oc,
                 merged=merged, x1=x1, h2=h2, u=u, a=a, biasm=biasm, gq8=gq8, gk8=gk8, wpool=wpool)
    return x2, saved, full


class GradReducer:
    def __init__(self, layouts, core, chip):
        self.layouts, self.core, self.chip = layouts, core, chip
        self.pairs32, self.received, self.pending = {}, {}, []

    def prepare(self, layer, grads):
        names = list(grads)
        lays = [self.layouts[n] for n in names]
        others = half_exchange([grads[n] for n in names], lays, name="half_exchange")
        for n, lay, other in zip(names, lays, others):
            p32, p16 = pair_sum(grads[n], other, lay, self.core, name="pair_sum")
            self.pairs32[(layer, n)] = p32
            self.pending.append(((layer, n), p16))

    def take(self):
        keys = [k for k, _ in self.pending]
        pairs16 = [p for _, p in self.pending]
        self.pending = []
        return keys, BlockScatter([self.layouts[n] for _, n in keys]), pairs16

    def store(self, keys, received):
        self.received.update(zip(keys, received))

    def finish(self):
        keys = list(self.pairs32)
        mine = [sum_chips(self.pairs32[k], self.received[k], self.layouts[k[1]], self.chip, name="sum_chips") for k in keys]
        other = sibling_swap(mine, name="sibling_swap")
        return dict(zip(keys, mine)), dict(zip(keys, other))


EARLY_WEIGHTS = ("w_down", "w_up", "w_out", "w_branch_a", "w_branch_b", "w_branch_c")


def _layer_bwd(dx2, s, p, tables, reducer):
    g = {}
    full, l = p["full"], p["l"]
    da = matmul(dx2, full["w_down"], b_layer=l, tb=True, name="mm_down_dx")
    g["w_down"] = matmul(s["a"], dx2, ta=True, name="mm_down_dw")
    dug, duv, dcwg, dcwv, dcbg, dcbv = conv_glu_bwd(s["u"], p["conv_w"], p["conv_b"][None], da, name="conv_glu_bwd")
    du = jnp.concatenate([dug, duv], axis=1)
    g["conv_w"] = jnp.concatenate([dcwg, dcwv], axis=1)
    g["conv_b"] = jnp.concatenate([dcbg, dcbv], axis=1)[0]
    g["w_up"] = matmul(s["h2"], du, ta=True, name="mm_up_dw")
    dh2 = matmul(du, full["w_up"], b_layer=l, tb=True, name="mm_up_dx")
    dx1, dg2 = rmsnorm_bwd(s["x1"], p["norm_ffn"][None], dh2, dx2, name="rmsnorm_bwd")
    g["norm_ffn"] = dg2[0]
    dmerged = matmul(dx1, full["w_out"], b_layer=l, tb=True, name="mm_out_dx")
    g["w_out"] = matmul(s["merged"], dx1, ta=True, name="mm_out_dw")
    t_a, t_b, t_c, dga, dgb, dgc, dba, dbb, dbc = merge_bwd(
        dmerged, s["oa"], s["ob"], s["oc"], s["proj"], p["b_gate"][None], p["w_branch_a"], p["w_branch_b"],
        p["w_branch_c"], name="merge_bwd")
    g["b_gate"] = jnp.concatenate([dba, dbb, dbc], axis=1)[0]
    g["w_branch_a"] = matmul(s["oa"], t_a, ta=True, name="mm_branch_dw")
    g["w_branch_b"] = matmul(s["ob"], t_b, ta=True, name="mm_branch_dw")
    g["w_branch_c"] = matmul(s["oc"], t_c, ta=True, name="mm_branch_dw")
    doa = matmul(t_a, full["w_branch_a"], b_layer=l, tb=True, out_dtype=BF16, name="mm_branch_dx")
    dob = matmul(t_b, full["w_branch_b"], b_layer=l, tb=True, out_dtype=BF16, name="mm_branch_dx")
    doc = matmul(t_c, full["w_branch_c"], b_layer=l, tb=True, name="mm_branch_dx_f32")
    dqh, dkh, dva, dbias = attn_a_bwd(s["qa"], s["ka"], s["va"], s["biasm"], doa, name="attn_a_bwd")
    ddiag = relbias_reduce(dbias, name="relbias_reduce")
    ddiag = ddiag.transpose(1, 0, 2, 3).reshape(N_HEADS, N_VARIANTS * DIAG_W)
    g["rel_bias_a"] = exact_dot(ddiag, tables["onehot"], name="relbias_table")
    dqa, dka, dgq8, dgk8 = qknorm_bwd(s["proj"], s["gq8"], s["gk8"], dqh, dkh, name="qknorm_bwd")
    g["q_norm_a"] = dgq8.reshape(N_HEADS, HEAD_DIM).sum(axis=0)
    g["k_norm_a"] = dgk8.reshape(N_HEADS, HEAD_DIM).sum(axis=0)
    reducer.prepare(l, {n: g[n] for n in EARLY_WEIGHTS})
    keys, plan, pairs16 = reducer.take()
    dqb, dkb, dvb, received = attn_b_bwd(s["qb"], s["kb"], s["vb"], s["tot"], s["nblk"], dob, scatter=(plan, pairs16),
                                         name="attn_b_bwd_scatter")
    reducer.store(keys, received)
    duc, dwp, dsc = pool_bwd(s["proj"], s["wpool"], p["pool_scale"][None], doc, name="pool_bwd")
    g["w_pool"] = dwp
    g["pool_scale"] = dsc[0]
    dproj = jnp.concatenate([dqa, dka, dva.astype(BF16), dqb, dkb.astype(BF16), dvb.astype(BF16), duc,
                             dga, dgb, dgc], axis=1)
    g["w_in"] = matmul(s["h"], dproj, ta=True, name="mm_in_dw")
    reducer.prepare(l, {"w_in": g["w_in"]})
    if l > 0:
        dh = matmul(dproj, full["w_in"], b_layer=l, tb=True, name="mm_in_dx")
    else:
        keys, plan, pairs16 = reducer.take()
        dh, received = matmul(dproj, full["w_in"], b_layer=l, tb=True, behind=(plan, pairs16), name="mm_in_dx_scatter")
        reducer.store(keys, received)
    dx, dg1 = rmsnorm_bwd(s["x"], p["norm_mix"][None], dh, dx1, name="rmsnorm_bwd")
    g["norm_mix"] = dg1[0]
    return dx, g


def kernel(x, norm_mix, w_in, b_gate, q_norm_a, k_norm_a, rel_bias_a, w_pool, pool_scale, w_branch_a, w_branch_b, w_branch_c, w_out, norm_ffn, w_up, conv_w, conv_b, w_down, loss_target, m_norm_mix, m_w_in, m_b_gate, m_q_norm_a, m_k_norm_a, m_rel_bias_a, m_w_pool, m_pool_scale, m_w_branch_a, m_w_branch_b, m_w_branch_c, m_w_out, m_norm_ffn, m_w_up, m_conv_w, m_conv_b, m_w_down, v_norm_mix, v_w_in, v_b_gate, v_q_norm_a, v_k_norm_a, v_rel_bias_a, v_w_pool, v_pool_scale, v_w_branch_a, v_w_branch_b, v_w_branch_c, v_w_out, v_norm_ffn, v_w_up, v_conv_w, v_conv_b, v_w_down):
    w = dict(zip(WEIGHTS, (norm_mix, w_in, b_gate, q_norm_a, k_norm_a, rel_bias_a, w_pool, pool_scale, w_branch_a,
                           w_branch_b, w_branch_c, w_out, norm_ffn, w_up, conv_w, conv_b, w_down)))
    m = dict(zip(WEIGHTS, (m_norm_mix, m_w_in, m_b_gate, m_q_norm_a, m_k_norm_a, m_rel_bias_a, m_w_pool, m_pool_scale,
                           m_w_branch_a, m_w_branch_b, m_w_branch_c, m_w_out, m_norm_ffn, m_w_up, m_conv_w, m_conv_b,
                           m_w_down)))
    v = dict(zip(WEIGHTS, (v_norm_mix, v_w_in, v_b_gate, v_q_norm_a, v_k_norm_a, v_rel_bias_a, v_w_pool, v_pool_scale,
                           v_w_branch_a, v_w_branch_b, v_w_branch_c, v_w_out, v_norm_ffn, v_w_up, v_conv_w, v_conv_b,
                           v_w_down)))
    onehot = diagonal_onehot()
    tables = dict(onehot=jnp.asarray(onehot), onehot_t=jnp.asarray(np.ascontiguousarray(onehot.T)))

    names = tuple(SHARDED)
    shards = [w[n] if n == "conv_w" else w[n].astype(BF16) for n in names]
    axes = [SHARDED[n] for n in names]
    later = [i for i, n in enumerate(names) if n != "w_in"]
    rest = (LayerGather([shards[i] for i in later], [axes[i] for i in later], 0), [shards[i] for i in later],
            [names[i] for i in later])
    first = names.index("w_in")
    full = {"w_in": all_gather_layer([shards[first]], [axes[first]], 0, name="all_gather_layer")[0]}

    def layer_params(l):
        p = {n: full[n][l] for n in ("w_branch_a", "w_branch_b", "w_branch_c", "conv_w")}
        p.update({n: w[n][l] for n in REPLICATED})
        p.update(full=full, l=l)
        return p

    xs = x[0]
    saved = []
    for l in range(DEPTH):
        prefetch = (LayerGather(shards, axes, l + 1), shards, names) if l + 1 < DEPTH else None
        replicated = {n: w[n][l] for n in REPLICATED}
        xs, s, full = _layer_fwd(xs, dict(replicated, l=l), full, tables, rest if l == 0 else None, prefetch)
        saved.append(s)
    dx, lpart = loss_head(xs, loss_target[0], name="loss_head")
    loss = lax.psum(lpart[0, 0], MESH_AXES)
    as_index = lambda i: jnp.reshape(i, (1,)).astype(jnp.int32)
    cx, cy, cc = _mesh_pos()
    core, chip, me = as_index(cc), as_index(2 * cx + cy), as_index(4 * cx + 2 * cy + cc)
    layouts = {n: HalfLayout((full[n].shape[1], full[n].shape[2]), SHARDED[n] - 1) for n in MATMUL_WEIGHTS}
    reducer = GradReducer(layouts, core, chip)
    grads = [None] * DEPTH
    for l in reversed(range(DEPTH)):
        dx, grads[l] = _layer_bwd(dx, saved[l], layer_params(l), tables, reducer)

    g = {n: jnp.stack([grads[l][n] for l in range(DEPTH)]) for n in SMALL_WEIGHTS}
    flat = jnp.concatenate([g[n].reshape(-1) for n in SMALL_WEIGHTS])
    small = jnp.pad(flat, (0, SMALL_ROWS * 128 - flat.shape[0])).reshape(SMALL_ROWS, 128)
    small_sum = sum_devices(gather_small(small, name="gather_small"), small, me, name="sum_devices").reshape(-1)
    mine, other = reducer.finish()

    res = {}
    for n in MATMUL_WEIGHTS:
        res[n] = adamw_halves(w[n], m[n], v[n], [mine[(l, n)] for l in range(DEPTH)],
                              [other[(l, n)] for l in range(DEPTH)], layouts[n], core, name="adamw_halves")
    off = 0
    for n in SMALL_WEIGHTS:
        shp = g[n].shape
        size = int(np.prod(shp))
        gn = small_sum[off:off + size].reshape(shp)
        off += size
        if n in SHARDED:
            gn = lax.dynamic_slice_in_dim(gn, (2 * cx + cy) * w[n].shape[-1], w[n].shape[-1], axis=len(shp) - 1)
        shp = w[n].shape
        cols = shp[-1]
        two_d = lambda t: t.reshape(int(np.prod(shp)) // cols, cols)
        res[n] = [t.reshape(shp) for t in adamw(two_d(w[n]), two_d(m[n]), two_d(v[n]), two_d(gn), name="adamw")]

    out = [loss, dx[None]]
    for k in range(4):
        out.extend(res[n][k] for n in WEIGHTS)
    return tuple(out)
```

```python
import jax
import jax.numpy as jnp
import numpy as np
from jax import lax
from jax.experimental import pallas as pl
from jax.experimental.pallas import tpu as pltpu

F32 = jnp.float32
BF16 = jnp.bfloat16

D_MODEL = 1024
DEPTH = 2
CHUNK = 64
N_LEFT = 8
HEAD_DIM = 64
N_HEADS = 8
WIDTH = 512
POOL_WINDOWS = (2, 4, 8, 16)
GROUP_DIM = 128
MAX_REL = 2 * CHUNK
REL_TABLE = MAX_REL + CHUNK
D_FF = 2816
EPS = 1e-6
QK_SCALE = 0.125
IN_COLS = 7 * WIDTH + 3 * D_MODEL
GATE_COL0 = 7 * WIDTH

ADAM_LR = 0.001
ADAM_B1 = 0.9
ADAM_B2 = 0.999
ADAM_EPS = 1e-08
ADAM_WD = 0.01
ADAM_STEP = 10

VMEM_LIMIT = 56 * 1024 * 1024
ATT_Q = 256
A_Q = 256
A_WIN = A_Q + N_LEFT * CHUNK
HALO = 16
CONV_HALO = 8
NEG = -1e30

MESH_AXES = ("x", "y", "c")
MESH_T = pl.DeviceIdType.MESH


def _cp(sem=None, vmem=VMEM_LIMIT):
    return pltpu.CompilerParams(dimension_semantics=sem, vmem_limit_bytes=vmem)


def _dot(a, b, ca, cb):
    return lax.dot_general(a, b, (((ca,), (cb,)), ((), ())), preferred_element_type=F32)


def _tile(n, cands=(512, 256, 128)):
    for c in cands:
        if n % c == 0:
            return c
    return n


def _split_hi_lo(v):
    hi = v.astype(BF16)
    lo = (v - hi.astype(F32)).astype(BF16)
    return hi, lo


def matmul(a, b, *, ta=False, tb=False, add=None, out_dtype=F32, b_layer=None, behind=None, name):
    plan, sources = behind if behind is not None else (None, [])
    n_src = len(sources)
    n_dst = len(plan.out_shapes()) if plan is not None else 0
    if ta:
        K, M = a.shape
    else:
        M, K = a.shape
    if tb:
        N, K2 = b.shape[-2:]
    else:
        K2, N = b.shape[-2:]
    assert K == K2, (a.shape, b.shape, ta, tb)
    big = (1024, 1408, 512, 256, 128)
    tm = _tile(M, big)
    tn = _tile(N, (1664,) + big if ta else big)
    tk = _tile(K, big if ta else (3328, 2816) + big)
    nk = K // tk

    n_in = (3 if add is not None else 2) + n_src
    grid = (M // tm, N // tn, nk)

    def body(*refs):
        a_ref, b_ref = refs[:2]
        r_ref = refs[2] if add is not None else None
        o_ref, acc = refs[n_in], refs[n_in + 1 + n_dst]
        i, j, k = pl.program_id(0), pl.program_id(1), pl.program_id(2)
        if plan is not None:
            comm = (refs[n_in - n_src:n_in], refs[n_in + 1:n_in + 1 + n_dst], refs[n_in + 2 + n_dst:])

            @pl.when((i == 0) & (j == 0) & (k == 0))
            def _():
                plan.start(*comm)

        @pl.when(k == 0)
        def _():
            acc[...] = jnp.zeros_like(acc)

        av = a_ref[...].astype(BF16)
        bv = b_ref[...].astype(BF16)
        acc[...] += _dot(av, bv, 0 if ta else 1, 1 if tb else 0)

        @pl.when(k == nk - 1)
        def _():
            r = acc[...]
            if add is not None:
                r = r + r_ref[...].astype(F32)
            o_ref[...] = r.astype(out_dtype)

        if plan is not None:
            @pl.when((i == grid[0] - 1) & (j == grid[1] - 1) & (k == nk - 1))
            def _():
                plan.finish(*comm)

    a_spec = pl.BlockSpec((tk, tm), lambda i, j, k: (k, i)) if ta else pl.BlockSpec((tm, tk), lambda i, j, k: (i, k))
    if b_layer is None:
        b_spec = pl.BlockSpec((tn, tk), lambda i, j, k: (j, k)) if tb else pl.BlockSpec((tk, tn), lambda i, j, k: (k, j))
    elif tb:
        b_spec = pl.BlockSpec((None, tn, tk), lambda i, j, k: (b_layer, j, k))
    else:
        b_spec = pl.BlockSpec((None, tk, tn), lambda i, j, k: (b_layer, k, j))
    in_specs = [a_spec, b_spec]
    args = [a, b]
    if add is not None:
        in_specs.append(pl.BlockSpec((tm, tn), lambda i, j, k: (i, j)))
        args.append(add)
    o_spec = pl.BlockSpec((tm, tn), lambda i, j, k: (i, j))
    if plan is None:
        return pl.pallas_call(
            body,
            out_shape=jax.ShapeDtypeStruct((M, N), out_dtype),
            grid=grid,
            in_specs=in_specs,
            out_specs=o_spec,
            scratch_shapes=[pltpu.VMEM((tm, tn), F32)],
            compiler_params=_cp(("parallel", "parallel", "arbitrary")),
            name=name,
        )(*args)
    any_space = pl.BlockSpec(memory_space=pl.ANY)
    outs = pl.pallas_call(
        body,
        out_shape=(jax.ShapeDtypeStruct((M, N), out_dtype),) + tuple(plan.out_shapes()),
        grid=grid,
        in_specs=in_specs + [any_space] * n_src,
        out_specs=(o_spec,) + tuple([any_space] * n_dst),
        scratch_shapes=[pltpu.VMEM((tm, tn), F32)] + list(plan.scratch_shapes()),
        compiler_params=_cp(("arbitrary", "arbitrary", "arbitrary")),
        name=name,
    )(*args, *sources)
    return outs[0], list(outs[1:])


def rmsnorm_fwd(x, g, *, name):
    S, D = x.shape
    T = _tile(S)

    def body(x_ref, g_ref, h_ref):
        xv = x_ref[...]
        r = lax.rsqrt(jnp.mean(xv * xv, axis=-1, keepdims=True) + EPS)
        h_ref[...] = (xv * r * g_ref[...]).astype(BF16)

    return pl.pallas_call(
        body,
        out_shape=jax.ShapeDtypeStruct((S, D), BF16),
        grid=(S // T,),
        in_specs=[pl.BlockSpec((T, D), lambda i: (i, 0)), pl.BlockSpec((1, D), lambda i: (0, 0))],
        out_specs=pl.BlockSpec((T, D), lambda i: (i, 0)),
        compiler_params=_cp(("parallel",)),
        name=name,
    )(x, g)


def rmsnorm_bwd(x, g, dh, dres, *, name):
    S, D = x.shape
    T = _tile(S)

    def body(x_ref, g_ref, dh_ref, dres_ref, dx_ref, dg_ref):
        i = pl.program_id(0)
        xv = x_ref[...]
        dhv = dh_ref[...].astype(F32)
        r = lax.rsqrt(jnp.mean(xv * xv, axis=-1, keepdims=True) + EPS)
        gd = dhv * g_ref[...]
        m = jnp.mean(xv * gd, axis=-1, keepdims=True)
        dx_ref[...] = dres_ref[...] + r * gd - xv * (r * r * r * m)

        @pl.when(i == 0)
        def _():
            dg_ref[...] = jnp.zeros_like(dg_ref)

        dg_ref[...] += jnp.sum(dhv * xv * r, axis=0, keepdims=True)

    row = pl.BlockSpec((T, D), lambda i: (i, 0))
    vec = pl.BlockSpec((1, D), lambda i: (0, 0))
    return pl.pallas_call(
        body,
        out_shape=(jax.ShapeDtypeStruct((S, D), F32), jax.ShapeDtypeStruct((1, D), F32)),
        grid=(S // T,),
        in_specs=[row, vec, row, row],
        out_specs=(row, vec),
        compiler_params=_cp(("arbitrary",)),
        name=name,
    )(x, g, dh, dres)


def _head_mean_matrix():
    r = lax.broadcasted_iota(jnp.int32, (WIDTH, WIDTH), 0) // HEAD_DIM
    c = lax.broadcasted_iota(jnp.int32, (WIDTH, WIDTH), 1) // HEAD_DIM
    return jnp.where(r == c, 1.0 / HEAD_DIM, 0.0).astype(BF16)


def _head_mean(v, mm):
    hi, lo = _split_hi_lo(v)
    return _dot(hi, mm, 1, 0) + _dot(lo, mm, 1, 0)


def qkv_prep(proj, gq, gk, *, name):
    S = proj.shape[0]
    T = _tile(S)

    def body(qa, ka, va, qb, kb, vb, gq_ref, gk_ref, oqa, oka, ova, oqb, okb, ovb):
        mm = _head_mean_matrix()
        for src, gref, dst, scale in ((qa, gq_ref, oqa, QK_SCALE), (ka, gk_ref, oka, 1.0)):
            v = src[...]
            r = lax.rsqrt(_head_mean(v * v, mm) + EPS)
            dst[...] = (v * r * gref[...] * scale).astype(BF16)
        oqb[...] = (qb[...] * QK_SCALE).astype(BF16)
        for src, dst in ((va, ova), (kb, okb), (vb, ovb)):
            dst[...] = src[...].astype(BF16)

    col = lambda j: pl.BlockSpec((T, WIDTH), lambda i, j=j: (i, j))
    vec = pl.BlockSpec((1, WIDTH), lambda i: (0, 0))
    out = pl.BlockSpec((T, WIDTH), lambda i: (i, 0))
    return pl.pallas_call(
        body,
        out_shape=tuple(jax.ShapeDtypeStruct((S, WIDTH), BF16) for _ in range(6)),
        grid=(S // T,),
        in_specs=[col(0), col(1), col(2), col(3), col(4), col(5), vec, vec],
        out_specs=tuple(out for _ in range(6)),
        compiler_params=_cp(("parallel",)),
        name=name,
    )(proj, proj, proj, proj, proj, proj, gq, gk)


def qknorm_bwd(proj, gq, gk, dqh, dkh, *, name):
    S = proj.shape[0]
    T = _tile(S)

    def body(qa, ka, gq_ref, gk_ref, dq_ref, dk_ref, oq, ok, ogq, ogk):
        i = pl.program_id(0)
        mm = _head_mean_matrix()

        @pl.when(i == 0)
        def _():
            ogq[...] = jnp.zeros_like(ogq)
            ogk[...] = jnp.zeros_like(ogk)

        for src, gref, dref, dst, gdst in ((qa, gq_ref, dq_ref, oq, ogq), (ka, gk_ref, dk_ref, ok, ogk)):
            v = src[...]
            dy = dref[...]
            r = lax.rsqrt(_head_mean(v * v, mm) + EPS)
            gd = dy * gref[...]
            m = _head_mean(v * gd, mm)
            dst[...] = (r * gd - v * (r * r * r * m)).astype(BF16)
            gdst[...] += jnp.sum(dy * v * r, axis=0, keepdims=True)

    col = lambda j: pl.BlockSpec((T, WIDTH), lambda i, j=j: (i, j))
    vec = pl.BlockSpec((1, WIDTH), lambda i: (0, 0))
    row = pl.BlockSpec((T, WIDTH), lambda i: (i, 0))
    return pl.pallas_call(
        body,
        out_shape=(jax.ShapeDtypeStruct((S, WIDTH), BF16), jax.ShapeDtypeStruct((S, WIDTH), BF16),
                   jax.ShapeDtypeStruct((1, WIDTH), F32), jax.ShapeDtypeStruct((1, WIDTH), F32)),
        grid=(S // T,),
        in_specs=[col(0), col(1), vec, vec, row, row],
        out_specs=(row, row, vec, vec),
        compiler_params=_cp(("arbitrary",)),
        name=name,
    )(proj, proj, gq, gk, dqh, dkh)


DIAG_W = 1024
N_VARIANTS = N_LEFT * CHUNK // A_Q + 1


def diagonal_onehot():
    jj = np.arange(DIAG_W)
    diff = np.where(jj < A_WIN, jj, jj - DIAG_W)
    out = np.zeros((N_VARIANTS, DIAG_W, REL_TABLE), np.float32)
    for v in range(N_VARIANTS):
        rel = np.clip(A_Q * v - diff, -(CHUNK - 1), MAX_REL) + (CHUNK - 1)
        out[v, jj, rel] = 1.0
    return out.reshape(N_VARIANTS * DIAG_W, REL_TABLE)


def exact_dot(a, b, *, name):
    def body(a_ref, b_ref, o_ref):
        o_ref[...] = jnp.dot(a_ref[...], b_ref[...], precision=lax.Precision.HIGHEST, preferred_element_type=F32)

    return pl.pallas_call(body, out_shape=jax.ShapeDtypeStruct((a.shape[0], b.shape[1]), F32),
                          compiler_params=_cp(), name=name)(a, b)


def _band_valid(v):
    qc = (lax.broadcasted_iota(jnp.int32, (A_Q, A_WIN), 0) + A_Q * v) // CHUNK
    kc = lax.broadcasted_iota(jnp.int32, (A_Q, A_WIN), 1) // CHUNK
    return (kc <= qc) & (kc >= qc - N_LEFT)


def bias_expand(diag, *, name):
    def body(d_ref, o_ref):
        rows = jnp.broadcast_to(d_ref[0, 0], (A_Q, DIAG_W))
        skew = pltpu.roll(rows, 0, 1, stride=1, stride_axis=0)
        o_ref[0, 0] = jnp.where(_band_valid(pl.program_id(0)), skew[:, :A_WIN], NEG)

    return pl.pallas_call(
        body,
        out_shape=jax.ShapeDtypeStruct((N_VARIANTS, N_HEADS, A_Q, A_WIN), F32),
        grid=(N_VARIANTS, N_HEADS),
        in_specs=[pl.BlockSpec((1, 1, 1, DIAG_W), lambda v, h: (v, h, 0, 0))],
        out_specs=pl.BlockSpec((1, 1, A_Q, A_WIN), lambda v, h: (v, h, 0, 0)),
        compiler_params=_cp(("parallel", "parallel")),
        name=name,
    )(diag)


def relbias_reduce(dbias, *, name):
    def body(db_ref, o_ref):
        acc = None
        for a in range(A_Q // 8):
            x = jnp.concatenate([db_ref[0, 0, 8 * a:8 * a + 8, :], jnp.zeros((8, DIAG_W - A_WIN), F32)], axis=1)
            x = pltpu.roll(x, DIAG_W - 8 * a, 1) if a else x
            acc = x if acc is None else acc + x
        row = lax.broadcasted_iota(jnp.int32, (8, DIAG_W), 0)
        for b in range(3):
            acc = jnp.where((row >> b) & 1 == 1, pltpu.roll(acc, DIAG_W - (1 << b), 1), acc)
        o_ref[0, 0] = jnp.sum(acc, axis=0, keepdims=True)

    return pl.pallas_call(
        body,
        out_shape=jax.ShapeDtypeStruct((N_VARIANTS, N_HEADS, 1, DIAG_W), F32),
        grid=(N_VARIANTS, N_HEADS),
        in_specs=[pl.BlockSpec((1, 1, A_Q, A_WIN), lambda v, h: (v, h, 0, 0))],
        out_specs=pl.BlockSpec((1, 1, 1, DIAG_W), lambda v, h: (v, h, 0, 0)),
        compiler_params=_cp(("parallel", "parallel")),
        name=name,
    )(dbias)


def _a_window_start(qb):
    return pl.multiple_of(jnp.maximum(qb * A_Q - N_LEFT * CHUNK, 0), A_Q)


def attn_a_fwd(q, k, v, biasm, *, name):
    S = q.shape[0]
    nq = S // A_Q

    def body(q_ref, k_ref, v_ref, b_ref, o_ref):
        qb = pl.program_id(1)
        start = _a_window_start(qb)
        outs = []
        for h in range(2):
            lanes = slice(h * HEAD_DIM, (h + 1) * HEAD_DIM)
            qh = q_ref[:, lanes]
            kw = k_ref[pl.ds(start, A_WIN), lanes]
            vw = v_ref[pl.ds(start, A_WIN), lanes]
            s = _dot(qh, kw, 1, 1) + b_ref[0, h]
            m = jnp.max(s, axis=-1, keepdims=True)
            e = jnp.exp(s - m)
            outs.append(_dot(e.astype(BF16), vw, 1, 0) * (1.0 / jnp.sum(e, axis=-1, keepdims=True)))
        o_ref[...] = jnp.concatenate(outs, axis=1).astype(BF16)

    qspec = pl.BlockSpec((A_Q, 2 * HEAD_DIM), lambda hp, qb: (qb, hp))
    kvspec = pl.BlockSpec((S, 2 * HEAD_DIM), lambda hp, qb: (0, hp))
    bspec = pl.BlockSpec((1, 2, A_Q, A_WIN), lambda hp, qb: (jnp.minimum(qb, N_VARIANTS - 1), hp, 0, 0))
    return pl.pallas_call(
        body,
        out_shape=jax.ShapeDtypeStruct((S, WIDTH), BF16),
        grid=(N_HEADS // 2, nq),
        in_specs=[qspec, kvspec, kvspec, bspec],
        out_specs=qspec,
        compiler_params=_cp(("parallel", "arbitrary")),
        name=name,
    )(q, k, v, biasm)


def attn_a_bwd(q, k, v, biasm, do, *, name):
    S = q.shape[0]
    nq = S // A_Q

    def body(q_ref, k_ref, v_ref, b_ref, do_ref, dq_ref, dk_ref, dv_ref, db_ref):
        qb = pl.program_id(1)
        start = _a_window_start(qb)

        @pl.when(qb == 0)
        def _():
            dk_ref[...] = jnp.zeros_like(dk_ref)
            dv_ref[...] = jnp.zeros_like(dv_ref)

        @pl.when(qb < N_VARIANTS)
        def _():
            db_ref[...] = jnp.zeros_like(db_ref)

        dqs = []
        for h in range(2):
            lanes = slice(h * HEAD_DIM, (h + 1) * HEAD_DIM)
            qh = q_ref[:, lanes]
            doh = do_ref[:, lanes]
            kw = k_ref[pl.ds(start, A_WIN), lanes]
            vw = v_ref[pl.ds(start, A_WIN), lanes]
            s = _dot(qh, kw, 1, 1) + b_ref[0, h]
            m = jnp.max(s, axis=-1, keepdims=True)
            e = jnp.exp(s - m)
            p = e * (1.0 / jnp.sum(e, axis=-1, keepdims=True))
            dp = _dot(doh, vw, 1, 1)
            delta = jnp.sum(p * dp, axis=-1, keepdims=True)
            ds = p * (dp - delta)
            db_ref[0, h] += ds
            dsb = ds.astype(BF16)
            dqs.append(_dot(dsb, kw, 1, 0) * QK_SCALE)
            dk_ref[pl.ds(start, A_WIN), lanes] += _dot(dsb, qh, 0, 0)
            dv_ref[pl.ds(start, A_WIN), lanes] += _dot(p.astype(BF16), doh, 0, 0)
        dq_ref[...] = jnp.concatenate(dqs, axis=1)

    qspec = pl.BlockSpec((A_Q, 2 * HEAD_DIM), lambda hp, qb: (qb, hp))
    kvspec = pl.BlockSpec((S, 2 * HEAD_DIM), lambda hp, qb: (0, hp))
    bspec = pl.BlockSpec((1, 2, A_Q, A_WIN), lambda hp, qb: (jnp.minimum(qb, N_VARIANTS - 1), hp, 0, 0))
    return pl.pallas_call(
        body,
        out_shape=(jax.ShapeDtypeStruct((S, WIDTH), F32), jax.ShapeDtypeStruct((S, WIDTH), F32),
                   jax.ShapeDtypeStruct((S, WIDTH), F32), jax.ShapeDtypeStruct((N_VARIANTS, N_HEADS, A_Q, A_WIN), F32)),
        grid=(N_HEADS // 2, nq),
        in_specs=[qspec, kvspec, kvspec, bspec, qspec],
        out_specs=(qspec, kvspec, kvspec, bspec),
        compiler_params=_cp(("parallel", "arbitrary")),
        name=name,
    )(q, k, v, biasm, do)


def _tri(kind):
    j = lax.broadcasted_iota(jnp.int32, (ATT_Q, ATT_Q), 0)
    s = lax.broadcasted_iota(jnp.int32, (ATT_Q, ATT_Q), 1)
    if kind == "gt":
        m = j > s
    elif kind == "le":
        m = j <= s
    else:
        m = j < s
    return jnp.where(m, 1.0, 0.0).astype(BF16)


def _cum(v, tri):
    hi, lo = _split_hi_lo(v)
    return _dot(hi, tri, 1, 0) + _dot(lo, tri, 1, 0)


def _log_sigmoids(z, mask):
    t = jnp.log(1.0 + jnp.exp(-jnp.abs(z)))
    keep = -(jnp.maximum(z, 0.0) + t)
    take = jnp.minimum(z, 0.0) - t
    return (keep if mask is None else jnp.where(mask, keep, 0.0)), take


def _strictly_before():
    row = lax.broadcasted_iota(jnp.int32, (ATT_Q, ATT_Q), 0)
    col = lax.broadcasted_iota(jnp.int32, (ATT_Q, ATT_Q), 1)
    return col < row


EXIT_LOG = -104.0


def attn_b_fwd(q, k, v, *, gather=None, name):
    S = q.shape[0]
    nq = S // ATT_Q
    plan, shards, fulls = gather if gather is not None else (None, [], [])
    ng = len(shards)

    def body(q_ref, k_ref, v_ref, *rest):
        hp = pl.program_id(0)
        qb = pl.program_id(1)
        o_ref, t_ref, n_ref = rest[2 * ng:2 * ng + 3]
        if plan is not None:
            comm = (rest[:ng], rest[2 * ng + 3:3 * ng + 3], rest[3 * ng + 3:])

            @pl.when(jnp.logical_and(hp == 0, qb == 0))
            def _():
                plan.start(*comm)

        tri = _tri("gt")

        def block(kb, carry, mask):
            ks = pl.multiple_of(kb * ATT_Q, ATT_Q)
            new = []
            for h in range(2):
                lanes = slice(h * HEAD_DIM, (h + 1) * HEAD_DIM)
                c, acc = carry[h]
                z = _dot(q_ref[:, lanes], k_ref[pl.ds(ks, ATT_Q), lanes], 1, 1)
                keep, take = _log_sigmoids(z, mask)
                w = jnp.exp(take + (_cum(keep, tri) + c))
                if mask is not None:
                    w = jnp.where(mask, w, 0.0)
                acc = acc + _dot(w.astype(BF16), v_ref[pl.ds(ks, ATT_Q), lanes], 1, 0)
                c = c + jnp.sum(keep, axis=-1, keepdims=True)
                new.append((c, acc))
            return jnp.maximum(jnp.max(new[0][0]), jnp.max(new[1][0])), tuple(new)

        def cond(state):
            it, cmax, _ = state
            return jnp.logical_and(it <= qb, cmax >= EXIT_LOG)

        def step(state):
            it, _, carry = state
            cmax, carry = block(qb - it, carry, None)
            return it + 1, cmax, carry

        init = tuple((jnp.zeros((ATT_Q, 1), F32), jnp.zeros((ATT_Q, HEAD_DIM), F32)) for _ in range(2))
        cmax, diag = block(qb, init, _strictly_before())
        visited, _, res = lax.while_loop(cond, step, (jnp.int32(1), cmax, diag))
        o_ref[...] = jnp.concatenate([res[0][1], res[1][1]], axis=1).astype(BF16)
        t_ref[...] = jnp.concatenate([jnp.broadcast_to(res[h][0], (ATT_Q, HEAD_DIM)) for h in range(2)], axis=1)
        n_ref[hp, qb] = visited.astype(F32)
        if plan is not None:
            @pl.when(jnp.logical_and(hp == N_HEADS // 2 - 1, qb == nq - 1))
            def _():
                plan.finish(*comm)

    qspec = pl.BlockSpec((ATT_Q, 2 * HEAD_DIM), lambda hp, qb: (qb, hp))
    kvspec = pl.BlockSpec((S, 2 * HEAD_DIM), lambda hp, qb: (0, hp))
    outs = pl.pallas_call(
        body,
        out_shape=(jax.ShapeDtypeStruct((S, WIDTH), BF16), jax.ShapeDtypeStruct((S, WIDTH), F32),
                   jax.ShapeDtypeStruct((N_HEADS // 2, nq), F32))
        + tuple(jax.ShapeDtypeStruct(f.shape, f.dtype) for f in fulls),
        grid=(N_HEADS // 2, nq),
        in_specs=[qspec, kvspec, kvspec] + [ANY] * (2 * ng),
        out_specs=(qspec, qspec, pl.BlockSpec(memory_space=pltpu.SMEM)) + tuple([ANY] * ng),
        scratch_shapes=plan.scratch_shapes() if plan is not None else (),
        input_output_aliases={3 + ng + i: 3 + i for i in range(ng)},
        compiler_params=_cp(("arbitrary", "arbitrary")),
        name=name,
    )(q, k, v, *shards, *fulls)
    return outs if plan is None else (outs[0], outs[1], outs[2], list(outs[3:]))


def attn_b_bwd(q, k, v, tot, nblk, do, *, scatter=None, name):
    S = q.shape[0]
    nq = S // ATT_Q
    plan, pairs16 = scatter if scatter is not None else (None, [])
    ns = len(pairs16)

    def body(q_ref, k_ref, v_ref, t_ref, n_ref, do_ref, *rest):
        hp = pl.program_id(0)
        qb = pl.program_id(1)
        dq_ref, dk_ref, dv_ref = rest[ns:ns + 3]
        if plan is not None:
            comm = (rest[:ns], rest[ns + 3:2 * ns + 3], rest[2 * ns + 3:])

            @pl.when(jnp.logical_and(hp == 0, qb == 0))
            def _():
                plan.start(*comm)

        first = jnp.clip(qb + 1 - n_ref[hp, qb].astype(jnp.int32), 0, qb + 1)
        tri_le = _tri("le")
        tri_lt = _tri("lt")

        @pl.when(qb == 0)
        def _():
            dk_ref[...] = jnp.zeros_like(dk_ref)
            dv_ref[...] = jnp.zeros_like(dv_ref)

        def block(kb, carry, mask):
            ks = pl.multiple_of(kb * ATT_Q, ATT_Q)
            new = []
            for h in range(2):
                lanes = slice(h * HEAD_DIM, (h + 1) * HEAD_DIM)
                cl, cg, dq = carry[h]
                qh = q_ref[:, lanes]
                doh = do_ref[:, lanes]
                kh = k_ref[pl.ds(ks, ATT_Q), lanes]
                vh = v_ref[pl.ds(ks, ATT_Q), lanes]
                totl = t_ref[:, h * HEAD_DIM:h * HEAD_DIM + 1]
                z = _dot(qh, kh, 1, 1)
                keep, take = _log_sigmoids(z, mask)
                sig = jnp.exp(take)
                w = sig * jnp.exp((totl - cl) - _cum(keep, tri_le))
                if mask is not None:
                    w = jnp.where(mask, w, 0.0)
                g = w * _dot(doh, vh, 1, 1)
                G = _dot(g.astype(BF16), tri_lt, 1, 0) + cg
                dz = g * (1.0 - sig) - sig * G
                if mask is not None:
                    dz = jnp.where(mask, dz, 0.0)
                dz = dz.astype(BF16)
                dq = dq + _dot(dz, kh, 1, 0)
                dk_ref[pl.ds(ks, ATT_Q), lanes] += _dot(dz, qh, 0, 0)
                dv_ref[pl.ds(ks, ATT_Q), lanes] += _dot(w.astype(BF16), doh, 0, 0)
                cl = cl + jnp.sum(keep, axis=-1, keepdims=True)
                cg = cg + jnp.sum(g, axis=-1, keepdims=True)
                new.append((cl, cg, dq))
            return tuple(new)

        init = tuple((jnp.zeros((ATT_Q, 1), F32), jnp.zeros((ATT_Q, 1), F32), jnp.zeros((ATT_Q, HEAD_DIM), F32))
                     for _ in range(2))
        res = lax.fori_loop(jnp.minimum(first, qb), qb, lambda kb, carry: block(kb, carry, None), init)
        res = block(qb, res, _strictly_before())
        dq_ref[...] = (jnp.concatenate([res[0][2], res[1][2]], axis=1) * QK_SCALE).astype(BF16)
        if plan is not None:
            @pl.when(jnp.logical_and(hp == N_HEADS // 2 - 1, qb == nq - 1))
            def _():
                plan.finish(*comm)

    qspec = pl.BlockSpec((ATT_Q, 2 * HEAD_DIM), lambda hp, qb: (qb, hp))
    kvspec = pl.BlockSpec((S, 2 * HEAD_DIM), lambda hp, qb: (0, hp))
    outs = pl.pallas_call(
        body,
        out_shape=(jax.ShapeDtypeStruct((S, WIDTH), BF16), jax.ShapeDtypeStruct((S, WIDTH), F32),
                   jax.ShapeDtypeStruct((S, WIDTH), F32)) + (tuple(plan.out_shapes()) if plan is not None else ()),
        grid=(N_HEADS // 2, nq),
        in_specs=[qspec, kvspec, kvspec, qspec, pl.BlockSpec(memory_space=pltpu.SMEM), qspec] + [ANY] * ns,
        out_specs=(qspec, kvspec, kvspec) + tuple([ANY] * ns),
        scratch_shapes=plan.scratch_shapes() if plan is not None else (),
        compiler_params=_cp(("arbitrary", "arbitrary")),
        name=name,
    )(q, k, v, tot, nblk, do, *pairs16)
    return outs if plan is None else (outs[0], outs[1], outs[2], list(outs[3:]))


U_COLBLK = 6


def _pool_counts(t0, rows):
    t = t0 + lax.broadcasted_iota(jnp.int32, (rows, WIDTH), 0)
    lane_grp = lax.broadcasted_iota(jnp.int32, (rows, WIDTH), 1) // GROUP_DIM
    win = jnp.where(lane_grp == 0, 2, jnp.where(lane_grp == 1, 4, jnp.where(lane_grp == 2, 8, 16)))
    cnt = jnp.minimum(t + 1, win)
    return 1.0 / cnt.astype(F32), lane_grp


def _window_sums(ext, shift_fn):
    s2 = ext + shift_fn(ext, 1)
    s4 = s2 + shift_fn(s2, 2)
    s8 = s4 + shift_fn(s4, 4)
    s16 = s8 + shift_fn(s8, 8)
    return s2, s4, s8, s16


def _select_group(lane_grp, s2, s4, s8, s16):
    return jnp.where(lane_grp == 0, s2, jnp.where(lane_grp == 1, s4, jnp.where(lane_grp == 2, s8, s16)))


def _pooled_tile(u_ref, h_ref, i, T):
    halo = jnp.where(i > 0, h_ref[...], 0.0)
    ext = jnp.concatenate([halo, u_ref[...]], axis=0)
    n = T + HALO
    sums = _window_sums(ext, lambda v, k: pltpu.roll(v, k, 0))
    inv, lane_grp = _pool_counts(i * T - HALO, n)
    pooled = _select_group(lane_grp, *sums) * inv - ext
    return pooled[HALO:, :]


def pool_fwd(proj, w_pool, scale, *, name):
    S = proj.shape[0]
    T = _tile(S)
    hb = T // HALO

    def body(u_ref, h_ref, w_ref, s_ref, o_ref):
        i = pl.program_id(0)
        pooled = _pooled_tile(u_ref, h_ref, i, T).astype(BF16)
        outs = [_dot(pooled[:, g * GROUP_DIM:(g + 1) * GROUP_DIM], w_ref[g], 1, 0) for g in range(4)]
        o_ref[...] = (jnp.concatenate(outs, axis=1) * s_ref[...]).astype(BF16)

    return pl.pallas_call(
        body,
        out_shape=jax.ShapeDtypeStruct((S, WIDTH), BF16),
        grid=(S // T,),
        in_specs=[pl.BlockSpec((T, WIDTH), lambda i: (i, U_COLBLK)),
                  pl.BlockSpec((HALO, WIDTH), lambda i: (jnp.maximum(i * hb - 1, 0), U_COLBLK)),
                  pl.BlockSpec((4, GROUP_DIM, GROUP_DIM), lambda i: (0, 0, 0)),
                  pl.BlockSpec((1, WIDTH), lambda i: (0, 0))],
        out_specs=pl.BlockSpec((T, WIDTH), lambda i: (i, 0)),
        compiler_params=_cp(("parallel",)),
        name=name,
    )(proj, proj, w_pool, scale)


def pool_bwd(proj, w_pool, scale, do, *, name):
    S = proj.shape[0]
    T = _tile(S)
    hb = T // HALO
    nt = S // T

    def body(u_ref, h_ref, w_ref, s_ref, do_ref, dof_ref, du_ref, dw_ref, ds_ref):
        i = pl.program_id(0)

        @pl.when(i == 0)
        def _():
            dw_ref[...] = jnp.zeros_like(dw_ref)
            ds_ref[...] = jnp.zeros_like(ds_ref)

        pooled = _pooled_tile(u_ref, h_ref, i, T).astype(BF16)
        dov = do_ref[...].astype(F32)
        fut = jnp.where(i < nt - 1, dof_ref[...].astype(F32), 0.0)
        dmix = (jnp.concatenate([dov, fut], axis=0) * s_ref[...]).astype(BF16)
        mixed, dpool = [], []
        for g in range(4):
            lanes = slice(g * GROUP_DIM, (g + 1) * GROUP_DIM)
            mixed.append(_dot(pooled[:, lanes], w_ref[g], 1, 0))
            dw_ref[g] += _dot(pooled[:, lanes], dmix[:T, lanes], 0, 0)
            dpool.append(_dot(dmix[:, lanes], w_ref[g], 1, 1))
        ds_ref[...] += jnp.sum(dov * jnp.concatenate(mixed, axis=1), axis=0, keepdims=True)
        dp = jnp.concatenate(dpool, axis=1)
        n = T + HALO
        inv, lane_grp = _pool_counts(i * T, n)
        sums = _window_sums(dp * inv, lambda v, k: pltpu.roll(v, n - k, 0))
        du = _select_group(lane_grp, *sums) - dp
        du_ref[...] = du[:T, :].astype(BF16)

    row = pl.BlockSpec((T, WIDTH), lambda i: (i, 0))
    return pl.pallas_call(
        body,
        out_shape=(jax.ShapeDtypeStruct((S, WIDTH), BF16), jax.ShapeDtypeStruct((4, GROUP_DIM, GROUP_DIM), F32),
                   jax.ShapeDtypeStruct((1, WIDTH), F32)),
        grid=(nt,),
        in_specs=[pl.BlockSpec((T, WIDTH), lambda i: (i, U_COLBLK)),
                  pl.BlockSpec((HALO, WIDTH), lambda i: (jnp.maximum(i * hb - 1, 0), U_COLBLK)),
                  pl.BlockSpec((4, GROUP_DIM, GROUP_DIM), lambda i: (0, 0, 0)),
                  pl.BlockSpec((1, WIDTH), lambda i: (0, 0)),
                  row,
                  pl.BlockSpec((HALO, WIDTH), lambda i: (jnp.minimum((i + 1) * hb, S // HALO - 1), 0))],
        out_specs=(row, pl.BlockSpec((4, GROUP_DIM, GROUP_DIM), lambda i: (0, 0, 0)),
                   pl.BlockSpec((1, WIDTH), lambda i: (0, 0))),
        compiler_params=_cp(("arbitrary",)),
        name=name,
    )(proj, proj, w_pool, scale, do, do)


GATE_BLK0 = GATE_COL0 // WIDTH


def merge_fwd(oa, ob, oc, proj, b_gate, wa, wb, wc, *, name):
    S = oa.shape[0]
    T = _tile(S)

    def body(oa_ref, ob_ref, oc_ref, ga, gb, gc, ba, bb, bc, wa_ref, wb_ref, wc_ref, m_ref):
        acc = None
        for o_ref, g_ref, b_ref, w_ref in ((oa_ref, ga, ba, wa_ref), (ob_ref, gb, bb, wb_ref), (oc_ref, gc, bc, wc_ref)):
            y = _dot(o_ref[...], w_ref[...], 1, 0)
            t = jax.nn.sigmoid(g_ref[...] + b_ref[...]) * y
            acc = t if acc is None else acc + t
        m_ref[...] = acc.astype(BF16)

    row = pl.BlockSpec((T, WIDTH), lambda i, n: (i, 0))
    gate = lambda b: pl.BlockSpec((T, WIDTH), lambda i, n, b=b: (i, GATE_BLK0 + 2 * b + n))
    bias = lambda b: pl.BlockSpec((1, WIDTH), lambda i, n, b=b: (0, 2 * b + n))
    wspec = pl.BlockSpec((WIDTH, WIDTH), lambda i, n: (0, n))
    return pl.pallas_call(
        body,
        out_shape=jax.ShapeDtypeStruct((S, D_MODEL), BF16),
        grid=(S // T, 2),
        in_specs=[row, row, row, gate(0), gate(1), gate(2), bias(0), bias(1), bias(2), wspec, wspec, wspec],
        out_specs=pl.BlockSpec((T, WIDTH), lambda i, n: (i, n)),
        compiler_params=_cp(("parallel", "parallel")),
        name=name,
    )(oa, ob, oc, proj, proj, proj, b_gate, b_gate, b_gate, wa, wb, wc)


def merge_bwd(dm, oa, ob, oc, proj, b_gate, wa, wb, wc, *, name):
    S = oa.shape[0]
    T = _tile(S)

    def body(dm_ref, oa_ref, ob_ref, oc_ref, ga, gb, gc, ba, bb, bc, wa_ref, wb_ref, wc_ref,
             ta, tb, tc, dga, dgb, dgc, dba, dbb, dbc):
        i = pl.program_id(1)
        dmv = dm_ref[...].astype(F32)
        for o_ref, g_ref, b_ref, w_ref, t_ref, dg_ref, db_ref in (
                (oa_ref, ga, ba, wa_ref, ta, dga, dba), (ob_ref, gb, bb, wb_ref, tb, dgb, dbb),
                (oc_ref, gc, bc, wc_ref, tc, dgc, dbc)):
            y = _dot(o_ref[...], w_ref[...], 1, 0)
            gate = jax.nn.sigmoid(g_ref[...] + b_ref[...])
            t_ref[...] = (gate * dmv).astype(BF16)
            dgl = dmv * y * gate * (1.0 - gate)
            dg_ref[...] = dgl.astype(BF16)

            @pl.when(i == 0)
            def _():
                db_ref[...] = jnp.zeros_like(db_ref)

            db_ref[...] += jnp.sum(dgl, axis=0, keepdims=True)

    row = pl.BlockSpec((T, WIDTH), lambda n, i: (i, 0))
    half = pl.BlockSpec((T, WIDTH), lambda n, i: (i, n))
    gate = lambda b: pl.BlockSpec((T, WIDTH), lambda n, i, b=b: (i, GATE_BLK0 + 2 * b + n))
    bias = lambda b: pl.BlockSpec((1, WIDTH), lambda n, i, b=b: (0, 2 * b + n))
    wspec = pl.BlockSpec((WIDTH, WIDTH), lambda n, i: (0, n))
    bvec = pl.BlockSpec((1, WIDTH), lambda n, i: (0, n))
    act = jax.ShapeDtypeStruct((S, D_MODEL), BF16)
    vec = jax.ShapeDtypeStruct((1, D_MODEL), F32)
    return pl.pallas_call(
        body,
        out_shape=(act, act, act, act, act, act, vec, vec, vec),
        grid=(2, S // T),
        in_specs=[half, row, row, row, gate(0), gate(1), gate(2), bias(0), bias(1), bias(2), wspec, wspec, wspec],
        out_specs=(half, half, half, half, half, half, bvec, bvec, bvec),
        compiler_params=_cp(("parallel", "arbitrary")),
        name=name,
    )(dm, oa, ob, oc, proj, proj, proj, b_gate, b_gate, b_gate, wa, wb, wc)


FF_T = 256
FF_BLKS = D_FF // FF_T


def _silu_parts(x):
    s = jax.nn.sigmoid(x)
    return x * s, s


def _conv3(ext, w_ref, b_ref):
    taps = (pltpu.roll(ext, 2, 0), pltpu.roll(ext, 1, 0), ext)
    return b_ref[...] + w_ref[0:1, :] * taps[0] + w_ref[1:2, :] * taps[1] + w_ref[2:3, :] * taps[2], taps


def conv_glu_fwd(u, conv_w, conv_b, *, name):
    S = u.shape[0]
    T = _tile(S)
    hb = T // CONV_HALO

    def body(ug, ugh, uv, uvh, wg, wv, bg, bv, a_ref):
        i = pl.program_id(1)
        cs = []
        for m_ref, h_ref, w_ref, b_ref in ((ug, ugh, wg, bg), (uv, uvh, wv, bv)):
            halo = jnp.where(i > 0, h_ref[...], 0.0)
            ext = jnp.concatenate([halo, m_ref[...]], axis=0)
            cs.append(_conv3(ext, w_ref, b_ref)[0][CONV_HALO:, :])
        act, _ = _silu_parts(cs[0])
        a_ref[...] = (act * cs[1]).astype(BF16)

    main = lambda o: pl.BlockSpec((T, FF_T), lambda c, i, o=o: (i, c + o))
    halo = lambda o: pl.BlockSpec((CONV_HALO, FF_T), lambda c, i, o=o: (jnp.maximum(i * hb - 1, 0), c + o))
    wsp = lambda o: pl.BlockSpec((3, FF_T), lambda c, i, o=o: (0, c + o))
    bsp = lambda o: pl.BlockSpec((1, FF_T), lambda c, i, o=o: (0, c + o))
    return pl.pallas_call(
        body,
        out_shape=jax.ShapeDtypeStruct((S, D_FF), BF16),
        grid=(FF_BLKS, S // T),
        in_specs=[main(0), halo(0), main(FF_BLKS), halo(FF_BLKS), wsp(0), wsp(FF_BLKS), bsp(0), bsp(FF_BLKS)],
        out_specs=pl.BlockSpec((T, FF_T), lambda c, i: (i, c)),
        compiler_params=_cp(("parallel", "parallel")),
        name=name,
    )(u, u, u, u, conv_w, conv_w, conv_b, conv_b)


def conv_glu_bwd(u, conv_w, conv_b, da, *, name):
    S = u.shape[0]
    T = _tile(S)
    hb = T // CONV_HALO
    nt = S // T
    n = T + 2 * CONV_HALO

    def body(ug, ugp, ugf, uv, uvp, uvf, wg, wv, bg, bv, da_ref, daf_ref,
             dug, duv, dwg, dwv, dbg, dbv):
        i = pl.program_id(1)
        first, last = i == 0, i == nt - 1
        taps, cs = [], []
        for m_ref, p_ref, f_ref, w_ref, b_ref in ((ug, ugp, ugf, wg, bg), (uv, uvp, uvf, wv, bv)):
            ext = jnp.concatenate([jnp.where(first, 0.0, p_ref[...]), m_ref[...], jnp.where(last, 0.0, f_ref[...])], axis=0)
            c, tp = _conv3(ext, w_ref, b_ref)
            cs.append(c)
            taps.append(tp)
        dae = jnp.concatenate([jnp.zeros((CONV_HALO, FF_T), F32), da_ref[...].astype(F32),
                               jnp.where(last, 0.0, daf_ref[...].astype(F32))], axis=0)
        act, sg = _silu_parts(cs[0])
        dcs = (dae * cs[1] * (sg * (1.0 + cs[0] * (1.0 - sg))), dae * act)
        main = slice(CONV_HALO, CONV_HALO + T)
        for tp, dc, w_ref, du_ref, dw_ref, db_ref in ((taps[0], dcs[0], wg, dug, dwg, dbg),
                                                      (taps[1], dcs[1], wv, duv, dwv, dbv)):
            du = (w_ref[2:3, :] * dc + w_ref[1:2, :] * pltpu.roll(dc, n - 1, 0) + w_ref[0:1, :] * pltpu.roll(dc, n - 2, 0))
            du_ref[...] = du[main, :].astype(BF16)
            dcm = dc[main, :]
            rows = [jnp.sum(dcm * tp[j][main, :], axis=0, keepdims=True) for j in range(3)]

            @pl.when(first)
            def _():
                dw_ref[...] = jnp.zeros_like(dw_ref)
                db_ref[...] = jnp.zeros_like(db_ref)

            dw_ref[...] += jnp.concatenate(rows, axis=0)
            db_ref[...] += jnp.sum(dcm, axis=0, keepdims=True)

    main = lambda o: pl.BlockSpec((T, FF_T), lambda c, i, o=o: (i, c + o))
    past = lambda o: pl.BlockSpec((CONV_HALO, FF_T), lambda c, i, o=o: (jnp.maximum(i * hb - 1, 0), c + o))
    fut = lambda o: pl.BlockSpec((CONV_HALO, FF_T), lambda c, i, o=o: (jnp.minimum((i + 1) * hb, S // CONV_HALO - 1), c + o))
    wsp = lambda o: pl.BlockSpec((3, FF_T), lambda c, i, o=o: (0, c + o))
    bsp = lambda o: pl.BlockSpec((1, FF_T), lambda c, i, o=o: (0, c + o))
    return pl.pallas_call(
        body,
        out_shape=(jax.ShapeDtypeStruct((S, D_FF), BF16), jax.ShapeDtypeStruct((S, D_FF), BF16),
                   jax.ShapeDtypeStruct((3, D_FF), F32), jax.ShapeDtypeStruct((3, D_FF), F32),
                   jax.ShapeDtypeStruct((1, D_FF), F32), jax.ShapeDtypeStruct((1, D_FF), F32)),
        grid=(FF_BLKS, nt),
        in_specs=[main(0), past(0), fut(0), main(FF_BLKS), past(FF_BLKS), fut(FF_BLKS),
                  wsp(0), wsp(FF_BLKS), bsp(0), bsp(FF_BLKS), main(0), fut(0)],
        out_specs=(main(0), main(0), wsp(0), wsp(0), bsp(0), bsp(0)),
        compiler_params=_cp(("parallel", "arbitrary")),
        name=name,
    )(u, u, u, u, u, u, conv_w, conv_w, conv_b, conv_b, da, da)


def loss_head(y, target, *, name):
    S, D = y.shape
    T = _tile(S)

    def body(y_ref, t_ref, dy_ref, l_ref):
        i = pl.program_id(0)
        err = y_ref[...] - t_ref[...]
        dy_ref[...] = err * (1.0 / D)

        @pl.when(i == 0)
        def _():
            l_ref[...] = jnp.zeros_like(l_ref)

        l_ref[...] += 0.5 * jnp.sum(jnp.mean(err * err, axis=-1, keepdims=True))

    row = pl.BlockSpec((T, D), lambda i: (i, 0))
    return pl.pallas_call(
        body,
        out_shape=(jax.ShapeDtypeStruct((S, D), F32), jax.ShapeDtypeStruct((8, 128), F32)),
        grid=(S // T,),
        in_specs=[row, row],
        out_specs=(row, pl.BlockSpec((8, 128), lambda i: (0, 0))),
        compiler_params=_cp(("arbitrary",)),
        name=name,
    )(y, target)


ELEMS_PER_BLOCK = 256 * 1024


def _rows_tile(rows, cols):
    if rows * cols <= ELEMS_PER_BLOCK or rows % 8:
        return rows
    best = 8
    for tr in range(8, rows + 1, 8):
        if rows % tr == 0 and tr * cols <= ELEMS_PER_BLOCK:
            best = tr
    return best


def _adamw_math(g, w_ref, m_ref, v_ref, g_out, d_out, m_out, v_out):
    mn = ADAM_B1 * m_ref[...] + (1.0 - ADAM_B1) * g
    vn = ADAM_B2 * v_ref[...] + (1.0 - ADAM_B2) * (g * g)
    m_hat = mn / (1.0 - ADAM_B1 ** ADAM_STEP)
    v_hat = vn / (1.0 - ADAM_B2 ** ADAM_STEP)
    g_out[...] = g
    d_out[...] = -ADAM_LR * (m_hat / (jnp.sqrt(v_hat) + ADAM_EPS) + ADAM_WD * w_ref[...])
    m_out[...] = mn
    v_out[...] = vn


def adamw(w, m, v, g, *, name):
    rows, cols = w.shape
    tr = _rows_tile(rows, cols)

    def body(w_ref, m_ref, v_ref, g_ref, g_out, d_out, m_out, v_out):
        _adamw_math(g_ref[...], w_ref, m_ref, v_ref, g_out, d_out, m_out, v_out)

    spec = pl.BlockSpec((tr, cols), lambda i: (i, 0))
    shp = jax.ShapeDtypeStruct((rows, cols), F32)
    return pl.pallas_call(
        body,
        out_shape=(shp, shp, shp, shp),
        grid=(rows // tr,),
        in_specs=[spec] * 4,
        out_specs=(spec, spec, spec, spec),
        compiler_params=_cp(("parallel",)),
        name=name,
    )(w, m, v, g)


ANY = pl.BlockSpec(memory_space=pl.ANY)
STAGE_BYTES = 2 * 1024 * 1024


def _mesh_pos():
    return lax.axis_index("x"), lax.axis_index("y"), lax.axis_index("c")


def _chip_peers(x, y):
    return [(1 - x, y), (x, 1 - y), (1 - x, 1 - y)]


def _all_peers(x, y, c):
    return [((1 - x) if (r >> 2) & 1 else x, (1 - y) if (r >> 1) & 1 else y, (1 - c) if r & 1 else c)
            for r in range(1, 8)]


def _shard_slice(ref, axis, j, size, layer=None):
    idx = [slice(None)] * 3
    idx[axis] = pl.ds(pl.multiple_of(j * size, 128 if axis == 2 else 16), size)
    if layer is not None:
        idx[0] = pl.ds(layer, 1)
    return ref.at[tuple(idx)]


class LayerGather:
    def __init__(self, shards, axes, layer):
        self.nt = len(shards)
        self.axes = list(axes)
        self.layer = layer
        self.shapes = [s.shape for s in shards]
        self.dtypes = [s.dtype for s in shards]
        self.sizes = [s.shape[a] for s, a in zip(shards, axes)]
        self.split = [s.shape[1] % 32 == 0 for s in shards]
        self.half_rows = [s.shape[1] // 2 if sp else s.shape[1] for s, sp in zip(shards, self.split)]
        self.chunk_rows = []
        for s in shards:
            rt = s.shape[1]
            while rt % 32 == 0 and rt * s.shape[2] * s.dtype.itemsize > STAGE_BYTES:
                rt //= 2
            self.chunk_rows.append(rt)

    def out_shapes(self):
        out = []
        for shp, a, sz, dt in zip(self.shapes, self.axes, self.sizes, self.dtypes):
            shp = list(shp)
            shp[a] = 4 * sz
            out.append(jax.ShapeDtypeStruct(tuple(shp), dt))
        return out

    def scratch_shapes(self):
        return ([pltpu.VMEM((1, rt, shp[2]), dt) for shp, rt, dt in zip(self.shapes, self.chunk_rows, self.dtypes)]
                + [pltpu.SemaphoreType.DMA((2 * self.nt,))] + [pltpu.SemaphoreType.DMA((3 * self.nt,)) for _ in range(4)])

    def _views(self, ins, outs, scratch):
        nt = self.nt
        stage, stage_sems = scratch[:nt], scratch[nt]
        ici_send, ici_recv, d2d_send, d2d_recv = scratch[nt + 1:]
        x, y, c = _mesh_pos()
        mine = 2 * x + y
        peers = _chip_peers(x, y)
        layer = pl.ds(self.layer, 1)

        def rows(t, half, r0=0, n=None):
            hr = self.half_rows[t]
            if n is None:
                return pl.ds(pl.multiple_of(half * hr, 16), hr) if self.split[t] else pl.ds(0, hr)
            return pl.ds(r0, n)

        def placed(t, blk, row_sel, row_len):
            sz = self.sizes[t]
            if self.axes[t] == 2:
                return outs[t].at[layer, row_sel, pl.ds(pl.multiple_of(blk * sz, 128), sz)]
            return outs[t].at[layer, pl.ds(pl.multiple_of(blk * sz, 16) + row_sel.start, row_len), :]

        def ici(t, k, blk):
            px, py = peers[k]
            sel = rows(t, c)
            return pltpu.make_async_remote_copy(
                src_ref=ins[t].at[layer, sel, :], dst_ref=placed(t, blk, sel, self.half_rows[t]),
                send_sem=ici_send.at[3 * t + k], recv_sem=ici_recv.at[3 * t + k],
                device_id=(px, py, c), device_id_type=MESH_T)

        def d2d(t, k, half):
            px, py = peers[k]
            piece = placed(t, 2 * px + py, rows(t, half), self.half_rows[t])
            return pltpu.make_async_remote_copy(
                src_ref=piece, dst_ref=piece, send_sem=d2d_send.at[3 * t + k], recv_sem=d2d_recv.at[3 * t + k],
                device_id=(x, y, 1 - c), device_id_type=MESH_T)

        def own_chunk(t, r0):
            rt = self.chunk_rows[t]
            sel = pl.ds(r0, rt)
            return ins[t].at[layer, sel, :], placed(t, mine, sel, rt), stage[t], stage_sems

        return c, mine, peers, ici, d2d, own_chunk

    def start(self, ins, outs, scratch):
        c, mine, peers, ici, d2d, own_chunk = self._views(ins, outs, scratch)
        for t in range(self.nt):
            for k in range(3):
                ici(t, k, mine).start()
        starts = [list(range(0, self.shapes[t][1], self.chunk_rows[t])) for t in range(self.nt)]
        for r in range(max(len(s) for s in starts)):
            active = [(t, *own_chunk(t, starts[t][r])) for t in range(self.nt) if r < len(starts[t])]
            loads = [pltpu.make_async_copy(src, buf, sems.at[2 * t]) for t, src, dst, buf, sems in active]
            for cp in loads:
                cp.start()
            for cp in loads:
                cp.wait()
            stores = [pltpu.make_async_copy(buf, dst, sems.at[2 * t + 1]) for t, src, dst, buf, sems in active]
            for cp in stores:
                cp.start()
            for cp in stores:
                cp.wait()

    def finish(self, ins, outs, scratch):
        c, mine, peers, ici, d2d, own_chunk = self._views(ins, outs, scratch)
        for t in range(self.nt):
            for k, (px, py) in enumerate(peers):
                ici(t, k, 2 * px + py).wait_recv()
                if self.split[t]:
                    d2d(t, k, c).start()
        for t in range(self.nt):
            for k in range(3):
                if self.split[t]:
                    d2d(t, k, 1 - c).wait_recv()
        for t in range(self.nt):
            for k in range(3):
                ici(t, k, mine).wait_send()
                if self.split[t]:
                    d2d(t, k, c).wait_send()


def all_gather_layer(shards, axes, layer, *, name):
    plan = LayerGather(shards, axes, layer)
    nt = plan.nt

    def body(*refs):
        ins, outs, scratch = refs[:nt], refs[nt:2 * nt], refs[2 * nt:]
        plan.start(ins, outs, scratch)
        plan.finish(ins, outs, scratch)

    return pl.pallas_call(
        body,
        out_shape=tuple(plan.out_shapes()),
        in_specs=[ANY] * nt,
        out_specs=tuple([ANY] * nt),
        scratch_shapes=plan.scratch_shapes(),
        name=name,
    )(*shards)


class HalfLayout:
    def __init__(self, shape, axis):
        self.R, self.C = shape
        self.axis = axis
        if axis == 1:
            self.hr, self.pw = self.R // 2, self.C // 4
            self.half_shape = (self.hr, self.C)
        else:
            self.hr, self.pw = self.R // 8, self.C
            self.half_shape = (4 * self.hr, self.C)
        self.tr = _rows_tile(self.hr, self.pw)
        self.nr = self.hr // self.tr

    def in_grad(self, ref, blk, half):
        if self.axis == 1:
            return ref.at[pl.ds(pl.multiple_of(half * self.hr, 16), self.hr), pl.ds(pl.multiple_of(blk * self.pw, 128), self.pw)]
        return ref.at[pl.ds(pl.multiple_of((2 * blk + half) * self.hr, 16), self.hr), :]

    def in_half(self, ref, blk):
        if self.axis == 1:
            return ref.at[:, pl.ds(pl.multiple_of(blk * self.pw, 128), self.pw)]
        return ref.at[pl.ds(pl.multiple_of(blk * self.hr, 16), self.hr), :]

    def grad_spec(self):
        if self.axis == 1:
            return pl.BlockSpec((self.tr, self.pw), lambda j, i, s: (s[0] * self.nr + i, j))
        return pl.BlockSpec((self.tr, self.pw), lambda j, i, s: ((2 * j + s[0]) * self.nr + i, 0))

    def half_spec(self):
        if self.axis == 1:
            return pl.BlockSpec((self.tr, self.pw), lambda j, i, s: (i, j))
        return pl.BlockSpec((self.tr, self.pw), lambda j, i, s: (j * self.nr + i, 0))


def half_exchange(grads, layouts, *, name):
    nt = len(grads)
    pieces = [(t, j) for t in range(nt) for j in (range(4) if layouts[t].axis == 0 else range(1))]

    def body(*refs):
        ins, outs = refs[:nt], refs[nt:2 * nt]
        send_sems, recv_sems = refs[2 * nt:]
        x, y, c = _mesh_pos()
        cps = []
        for n, (t, j) in enumerate(pieces):
            lay = layouts[t]
            if lay.axis == 1:
                src = ins[t].at[pl.ds(pl.multiple_of((1 - c) * lay.hr, 16), lay.hr), :]
                dst = outs[t]
            else:
                src = lay.in_grad(ins[t], j, 1 - c)
                dst = lay.in_half(outs[t], j)
            cp = pltpu.make_async_remote_copy(src_ref=src, dst_ref=dst, send_sem=send_sems.at[n], recv_sem=recv_sems.at[n],
                                              device_id=(x, y, 1 - c), device_id_type=MESH_T)
            cp.start()
            cps.append(cp)
        for cp in cps:
            cp.wait_recv()
        for cp in cps:
            cp.wait_send()

    return pl.pallas_call(
        body,
        out_shape=tuple(jax.ShapeDtypeStruct(lay.half_shape, F32) for lay in layouts),
        in_specs=[ANY] * nt,
        out_specs=tuple([ANY] * nt),
        scratch_shapes=[pltpu.SemaphoreType.DMA((len(pieces),)), pltpu.SemaphoreType.DMA((len(pieces),))],
        name=name,
    )(*grads)


def pair_sum(grad, other, lay, core, *, name):
    def body(c_ref, g_ref, o_ref, s32_ref, s16_ref):
        s = g_ref[...] + o_ref[...]
        s32_ref[...] = s
        s16_ref[...] = s.astype(BF16)

    return pl.pallas_call(
        body,
        out_shape=(jax.ShapeDtypeStruct(lay.half_shape, F32), jax.ShapeDtypeStruct(lay.half_shape, BF16)),
        grid_spec=pltpu.PrefetchScalarGridSpec(
            num_scalar_prefetch=1, grid=(4, lay.nr),
            in_specs=[lay.grad_spec(), lay.half_spec()],
            out_specs=(lay.half_spec(), lay.half_spec())),
        compiler_params=_cp(("parallel", "parallel")),
        name=name,
    )(core, grad, other)


class BlockScatter:
    def __init__(self, layouts):
        self.layouts = layouts
        self.nt = len(layouts)

    def out_shapes(self):
        return [jax.ShapeDtypeStruct((3, lay.hr, lay.pw), BF16) for lay in self.layouts]

    def scratch_shapes(self):
        return [pltpu.SemaphoreType.DMA((3 * self.nt,)), pltpu.SemaphoreType.DMA((3 * self.nt,))]

    def _copies(self, pairs16, recv, scratch):
        send_sems, recv_sems = scratch
        x, y, c = _mesh_pos()
        return [pltpu.make_async_remote_copy(
            src_ref=lay.in_half(pairs16[t], 2 * px + py), dst_ref=recv[t].at[k],
            send_sem=send_sems.at[3 * t + k], recv_sem=recv_sems.at[3 * t + k],
            device_id=(px, py, c), device_id_type=MESH_T)
            for t, lay in enumerate(self.layouts) for k, (px, py) in enumerate(_chip_peers(x, y))]

    def start(self, pairs16, recv, scratch):
        for cp in self._copies(pairs16, recv, scratch):
            cp.start()

    def finish(self, pairs16, recv, scratch):
        copies = self._copies(pairs16, recv, scratch)
        for cp in copies:
            cp.wait_recv()
        for cp in copies:
            cp.wait_send()


def gather_small(small, *, name):
    def body(small_in, small_out, ssend, srecv):
        x, y, c = _mesh_pos()
        me = 4 * x + 2 * y + c
        sends, recvs = [], []
        for r, (px, py, pc) in enumerate(_all_peers(x, y, c)):
            def mk(slot, r=r, px=px, py=py, pc=pc):
                return pltpu.make_async_remote_copy(
                    src_ref=small_in, dst_ref=small_out.at[slot], send_sem=ssend.at[r], recv_sem=srecv.at[r],
                    device_id=(px, py, pc), device_id_type=MESH_T)
            snd = mk(me)
            snd.start()
            sends.append(snd)
            recvs.append(mk(4 * px + 2 * py + pc))
        for r in recvs:
            r.wait_recv()
        for s in sends:
            s.wait_send()

    return pl.pallas_call(
        body,
        out_shape=jax.ShapeDtypeStruct((8,) + small.shape, F32),
        in_specs=[ANY],
        out_specs=ANY,
        scratch_shapes=[pltpu.SemaphoreType.DMA((7,)), pltpu.SemaphoreType.DMA((7,))],
        name=name,
    )(small)


def sum_chips(pair32, recv, lay, chip, *, name):
    def body(j_ref, p_ref, r_ref, s_ref):
        acc = p_ref[...]
        for k in range(3):
            acc = acc + r_ref[k].astype(F32)
        s_ref[...] = acc

    if lay.axis == 1:
        own = pl.BlockSpec((lay.tr, lay.pw), lambda i, j: (i, j[0]))
    else:
        own = pl.BlockSpec((lay.tr, lay.pw), lambda i, j: (j[0] * lay.nr + i, 0))
    return pl.pallas_call(
        body,
        out_shape=jax.ShapeDtypeStruct((lay.hr, lay.pw), F32),
        grid_spec=pltpu.PrefetchScalarGridSpec(
            num_scalar_prefetch=1, grid=(lay.nr,),
            in_specs=[own, pl.BlockSpec((3, lay.tr, lay.pw), lambda i, j: (0, i, 0))],
            out_specs=pl.BlockSpec((lay.tr, lay.pw), lambda i, j: (i, 0))),
        compiler_params=_cp(("parallel",)),
        name=name,
    )(chip, pair32, recv)


def sum_devices(gathered, own, me, *, name):
    _, R, C = gathered.shape

    def body(me_ref, g_ref, o_ref, s_ref):
        acc = None
        for k in range(8):
            part = jnp.where(me_ref[0] == k, o_ref[...], g_ref[k])
            acc = part if acc is None else acc + part
        s_ref[...] = acc

    return pl.pallas_call(
        body,
        out_shape=jax.ShapeDtypeStruct((R, C), F32),
        grid_spec=pltpu.PrefetchScalarGridSpec(
            num_scalar_prefetch=1, grid=(1,),
            in_specs=[pl.BlockSpec((8, R, C), lambda i, m: (0, 0, 0)), pl.BlockSpec((R, C), lambda i, m: (0, 0))],
            out_specs=pl.BlockSpec((R, C), lambda i, m: (0, 0))),
        compiler_params=_cp(("arbitrary",)),
        name=name,
    )(me, gathered, own)


def sibling_swap(parts, *, name):
    nt = len(parts)

    def body(*refs):
        ins, outs = refs[:nt], refs[nt:2 * nt]
        send_sems, recv_sems = refs[2 * nt:]
        x, y, c = _mesh_pos()
        cps = []
        for t in range(nt):
            cp = pltpu.make_async_remote_copy(src_ref=ins[t], dst_ref=outs[t], send_sem=send_sems.at[t],
                                              recv_sem=recv_sems.at[t], device_id=(x, y, 1 - c), device_id_type=MESH_T)
            cp.start()
            cps.append(cp)
        for cp in cps:
            cp.wait_recv()
        for cp in cps:
            cp.wait_send()

    return pl.pallas_call(
        body,
        out_shape=tuple(jax.ShapeDtypeStruct(p.shape, p.dtype) for p in parts),
        in_specs=[ANY] * nt,
        out_specs=tuple([ANY] * nt),
        scratch_shapes=[pltpu.SemaphoreType.DMA((nt,)), pltpu.SemaphoreType.DMA((nt,))],
        name=name,
    )(*parts)


def adamw_halves(w, m, v, mine, other, lay, core, *, name):
    _, r, c = w.shape
    tr, nr = lay.tr, lay.nr
    assert (r, c) == (2 * lay.hr, lay.pw), (w.shape, lay.hr, lay.pw)

    def body(c_ref, w_ref, m_ref, v_ref, *rest):
        g_refs, outs = rest[:2 * DEPTH], rest[2 * DEPTH:]
        l, h = pl.program_id(0), pl.program_id(1)
        g = None
        for d in range(DEPTH):
            gd = jnp.where(h == c_ref[0], g_refs[d][...], g_refs[DEPTH + d][...])
            g = gd if g is None else jnp.where(l == d, gd, g)
        _adamw_math(g, w_ref, m_ref, v_ref, *outs)

    full = pl.BlockSpec((None, tr, c), lambda l, h, i, s: (l, h * nr + i, 0))

    def part(d, is_mine):
        def index(l, h, i, s):
            used = jnp.logical_and(l == d, (h == s[0]) == is_mine)
            return jnp.where(used, i, 0), 0
        return pl.BlockSpec((tr, c), index)

    shp = jax.ShapeDtypeStruct((DEPTH, r, c), F32)
    return pl.pallas_call(
        body,
        out_shape=(shp, shp, shp, shp),
        grid_spec=pltpu.PrefetchScalarGridSpec(
            num_scalar_prefetch=1, grid=(DEPTH, 2, nr),
            in_specs=[full, full, full] + [part(d, True) for d in range(DEPTH)] + [part(d, False) for d in range(DEPTH)],
            out_specs=(full, full, full, full)),
        compiler_params=_cp(("arbitrary", "arbitrary", "arbitrary")),
        name=name,
    )(core, w, m, v, *mine, *other)


WEIGHTS = ("norm_mix", "w_in", "b_gate", "q_norm_a", "k_norm_a", "rel_bias_a", "w_pool", "pool_scale",
           "w_branch_a", "w_branch_b", "w_branch_c", "w_out", "norm_ffn", "w_up", "conv_w", "conv_b", "w_down")
SHARDED = {"w_in": 2, "w_branch_a": 2, "w_branch_b": 2, "w_branch_c": 2, "w_out": 1, "w_up": 2, "conv_w": 2,
           "w_down": 1}
REPLICATED = tuple(n for n in WEIGHTS if n not in SHARDED)
MATMUL_WEIGHTS = tuple(n for n in SHARDED if n != "conv_w")
SMALL_WEIGHTS = tuple(n for n in WEIGHTS if n not in MATMUL_WEIGHTS)
SMALL_ROWS = 1496


def _layer_fwd(x, p, full, tables, rest=None, prefetch=None):
    l = p["l"]
    diag = exact_dot(p["rel_bias_a"], tables["onehot_t"], name="bias_diagonals")
    diag = diag.reshape(N_HEADS, N_VARIANTS, 1, DIAG_W).transpose(1, 0, 2, 3)
    biasm = bias_expand(diag, name="bias_expand")
    gq8 = jnp.tile(p["q_norm_a"], N_HEADS)[None]
    gk8 = jnp.tile(p["k_norm_a"], N_HEADS)[None]
    h = rmsnorm_fwd(x, p["norm_mix"][None], name="rmsnorm_fwd")
    if rest is None:
        proj = matmul(h, full["w_in"], b_layer=l, name="mm_in")
    else:
        plan, shards, names = rest
        proj, gathered = matmul(h, full["w_in"], b_layer=l, behind=(plan, shards), name="mm_in_gather")
        full = {**full, **dict(zip(names, gathered))}
    qa, ka, va, qb, kb, vb = qkv_prep(proj, gq8, gk8, name="qkv_prep")
    oa = attn_a_fwd(qa, ka, va, biasm, name="attn_a_fwd")
    if prefetch is None:
        ob, tot, nblk = attn_b_fwd(qb, kb, vb, name="attn_b_fwd")
    else:
        plan, shards, names = prefetch
        ob, tot, nblk, filled = attn_b_fwd(qb, kb, vb, gather=(plan, shards, [full[n] for n in names]),
                                           name="attn_b_fwd_gather")
        full = dict(zip(names, filled))
    wpool = p["w_pool"].astype(BF16)
    oc = pool_fwd(proj, wpool, p["pool_scale"][None], name="pool_fwd")
    merged = merge_fwd(oa, ob, oc, proj, p["b_gate"][None], full["w_branch_a"][l], full["w_branch_b"][l],
                       full["w_branch_c"][l], name="merge_fwd")
    x1 = matmul(merged, full["w_out"], b_layer=l, add=x, name="mm_out")
    h2 = rmsnorm_fwd(x1, p["norm_ffn"][None], name="rmsnorm_fwd")
    u = matmul(h2, full["w_up"], b_layer=l, name="mm_up")
    a = conv_glu_fwd(u, full["conv_w"][l], p["conv_b"][None], name="conv_glu_fwd")
    x2 = matmul(a, full["w_down"], b_layer=l, add=x1, name="mm_down")
    saved = dict(x=x, h=h, proj=proj, qa=qa, ka=ka, va=va, qb=qb, kb=kb, vb=vb, oa=oa, ob=ob, tot=tot, nblk=nblk, oc=oc,
                 merged=merged, x1=x1, h2=h2, u=u, a=a, biasm=biasm, gq8=gq8, gk8=gk8, wpool=wpool)
    return x2, saved, full


class GradReducer:
    def __init__(self, layouts, core, chip):
        self.layouts, self.core, self.chip = layouts, core, chip
        self.pairs32, self.received, self.pending = {}, {}, []

    def prepare(self, layer, grads):
        names = list(grads)
        lays = [self.layouts[n] for n in names]
        others = half_exchange([grads[n] for n in names], lays, name="half_exchange")
        for n, lay, other in zip(names, lays, others):
            p32, p16 = pair_sum(grads[n], other, lay, self.core, name="pair_sum")
            self.pairs32[(layer, n)] = p32
            self.pending.append(((layer, n), p16))

    def take(self):
        keys = [k for k, _ in self.pending]
        pairs16 = [p for _, p in self.pending]
        self.pending = []
        return keys, BlockScatter([self.layouts[n] for _, n in keys]), pairs16

    def store(self, keys, received):
        self.received.update(zip(keys, received))

    def finish(self):
        keys = list(self.pairs32)
        mine = [sum_chips(self.pairs32[k], self.received[k], self.layouts[k[1]], self.chip, name="sum_chips") for k in keys]
        other = sibling_swap(mine, name="sibling_swap")
        return dict(zip(keys, mine)), dict(zip(keys, other))


EARLY_WEIGHTS = ("w_down", "w_up", "w_out", "w_branch_a", "w_branch_b", "w_branch_c")


def _layer_bwd(dx2, s, p, tables, reducer):
    g = {}
    full, l = p["full"], p["l"]
    da = matmul(dx2, full["w_down"], b_layer=l, tb=True, name="mm_down_dx")
    g["w_down"] = matmul(s["a"], dx2, ta=True, name="mm_down_dw")
    dug, duv, dcwg, dcwv, dcbg, dcbv = conv_glu_bwd(s["u"], p["conv_w"], p["conv_b"][None], da, name="conv_glu_bwd")
    du = jnp.concatenate([dug, duv], axis=1)
    g["conv_w"] = jnp.concatenate([dcwg, dcwv], axis=1)
    g["conv_b"] = jnp.concatenate([dcbg, dcbv], axis=1)[0]
    g["w_up"] = matmul(s["h2"], du, ta=True, name="mm_up_dw")
    dh2 = matmul(du, full["w_up"], b_layer=l, tb=True, name="mm_up_dx")
    dx1, dg2 = rmsnorm_bwd(s["x1"], p["norm_ffn"][None], dh2, dx2, name="rmsnorm_bwd")
    g["norm_ffn"] = dg2[0]
    dmerged = matmul(dx1, full["w_out"], b_layer=l, tb=True, name="mm_out_dx")
    g["w_out"] = matmul(s["merged"], dx1, ta=True, name="mm_out_dw")
    t_a, t_b, t_c, dga, dgb, dgc, dba, dbb, dbc = merge_bwd(
        dmerged, s["oa"], s["ob"], s["oc"], s["proj"], p["b_gate"][None], p["w_branch_a"], p["w_branch_b"],
        p["w_branch_c"], name="merge_bwd")
    g["b_gate"] = jnp.concatenate([dba, dbb, dbc], axis=1)[0]
    g["w_branch_a"] = matmul(s["oa"], t_a, ta=True, name="mm_branch_dw")
    g["w_branch_b"] = matmul(s["ob"], t_b, ta=True, name="mm_branch_dw")
    g["w_branch_c"] = matmul(s["oc"], t_c, ta=True, name="mm_branch_dw")
    doa = matmul(t_a, full["w_branch_a"], b_layer=l, tb=True, out_dtype=BF16, name="mm_branch_dx")
    dob = matmul(t_b, full["w_branch_b"], b_layer=l, tb=True, out_dtype=BF16, name="mm_branch_dx")
    doc = matmul(t_c, full["w_branch_c"], b_layer=l, tb=True, name="mm_branch_dx_f32")
    dqh, dkh, dva, dbias = attn_a_bwd(s["qa"], s["ka"], s["va"], s["biasm"], doa, name="attn_a_bwd")
    ddiag = relbias_reduce(dbias, name="relbias_reduce")
    ddiag = ddiag.transpose(1, 0, 2, 3).reshape(N_HEADS, N_VARIANTS * DIAG_W)
    g["rel_bias_a"] = exact_dot(ddiag, tables["onehot"], name="relbias_table")
    dqa, dka, dgq8, dgk8 = qknorm_bwd(s["proj"], s["gq8"], s["gk8"], dqh, dkh, name="qknorm_bwd")
    g["q_norm_a"] = dgq8.reshape(N_HEADS, HEAD_DIM).sum(axis=0)
    g["k_norm_a"] = dgk8.reshape(N_HEADS, HEAD_DIM).sum(axis=0)
    reducer.prepare(l, {n: g[n] for n in EARLY_WEIGHTS})
    keys, plan, pairs16 = reducer.take()
    dqb, dkb, dvb, received = attn_b_bwd(s["qb"], s["kb"], s["vb"], s["tot"], s["nblk"], dob, scatter=(plan, pairs16),
                                         name="attn_b_bwd_scatter")
    reducer.store(keys, received)
    duc, dwp, dsc = pool_bwd(s["proj"], s["wpool"], p["pool_scale"][None], doc, name="pool_bwd")
    g["w_pool"] = dwp
    g["pool_scale"] = dsc[0]
    dproj = jnp.concatenate([dqa, dka, dva.astype(BF16), dqb, dkb.astype(BF16), dvb.astype(BF16), duc,
                             dga, dgb, dgc], axis=1)
    g["w_in"] = matmul(s["h"], dproj, ta=True, name="mm_in_dw")
    reducer.prepare(l, {"w_in": g["w_in"]})
    if l > 0:
        dh = matmul(dproj, full["w_in"], b_layer=l, tb=True, name="mm_in_dx")
    else:
        keys, plan, pairs16 = reducer.take()
        dh, received = matmul(dproj, full["w_in"], b_layer=l, tb=True, behind=(plan, pairs16), name="mm_in_dx_scatter")
        reducer.store(keys, received)
    dx, dg1 = rmsnorm_bwd(s["x"], p["norm_mix"][None], dh, dx1, name="rmsnorm_bwd")
    g["norm_mix"] = dg1[0]
    return dx, g


def kernel(x, norm_mix, w_in, b_gate, q_norm_a, k_norm_a, rel_bias_a, w_pool, pool_scale, w_branch_a, w_branch_b, w_branch_c, w_out, norm_ffn, w_up, conv_w, conv_b, w_down, loss_target, m_norm_mix, m_w_in, m_b_gate, m_q_norm_a, m_k_norm_a, m_rel_bias_a, m_w_pool, m_pool_scale, m_w_branch_a, m_w_branch_b, m_w_branch_c, m_w_out, m_norm_ffn, m_w_up, m_conv_w, m_conv_b, m_w_down, v_norm_mix, v_w_in, v_b_gate, v_q_norm_a, v_k_norm_a, v_rel_bias_a, v_w_pool, v_pool_scale, v_w_branch_a, v_w_branch_b, v_w_branch_c, v_w_out, v_norm_ffn, v_w_up, v_conv_w, v_conv_b, v_w_down):
    w = dict(zip(WEIGHTS, (norm_mix, w_in, b_gate, q_norm_a, k_norm_a, rel_bias_a, w_pool, pool_scale, w_branch_a,
                           w_branch_b, w_branch_c, w_out, norm_ffn, w_up, conv_w, conv_b, w_down)))
    m = dict(zip(WEIGHTS, (m_norm_mix, m_w_in, m_b_gate, m_q_norm_a, m_k_norm_a, m_rel_bias_a, m_w_pool, m_pool_scale,
                           m_w_branch_a, m_w_branch_b, m_w_branch_c, m_w_out, m_norm_ffn, m_w_up, m_conv_w, m_conv_b,
                           m_w_down)))
    v = dict(zip(WEIGHTS, (v_norm_mix, v_w_in, v_b_gate, v_q_norm_a, v_k_norm_a, v_rel_bias_a, v_w_pool, v_pool_scale,
                           v_w_branch_a, v_w_branch_b, v_w_branch_c, v_w_out, v_norm_ffn, v_w_up, v_conv_w, v_conv_b,
                           v_w_down)))
    onehot = diagonal_onehot()
    tables = dict(onehot=jnp.asarray(onehot), onehot_t=jnp.asarray(np.ascontiguousarray(onehot.T)))

    names = tuple(SHARDED)
    shards = [w[n] if n == "conv_w" else w[n].astype(BF16) for n in names]
    axes = [SHARDED[n] for n in names]
    later = [i for i, n in enumerate(names) if n != "w_in"]
    rest = (LayerGather([shards[i] for i in later], [axes[i] for i in later], 0), [shards[i] for i in later],
            [names[i] for i in later])
    first = names.index("w_in")
    full = {"w_in": all_gather_layer([shards[first]], [axes[first]], 0, name="all_gather_layer")[0]}

    def layer_params(l):
        p = {n: full[n][l] for n in ("w_branch_a", "w_branch_b", "w_branch_c", "conv_w")}
        p.update({n: w[n][l] for n in REPLICATED})
        p.update(full=full, l=l)
        return p

    xs = x[0]
    saved = []
    for l in range(DEPTH):
        prefetch = (LayerGather(shards, axes, l + 1), shards, names) if l + 1 < DEPTH else None
        replicated = {n: w[n][l] for n in REPLICATED}
        xs, s, full = _layer_fwd(xs, dict(replicated, l=l), full, tables, rest if l == 0 else None, prefetch)
        saved.append(s)
    dx, lpart = loss_head(xs, loss_target[0], name="loss_head")
    loss = lax.psum(lpart[0, 0], MESH_AXES)
    as_index = lambda i: jnp.reshape(i, (1,)).astype(jnp.int32)
    cx, cy, cc = _mesh_pos()
    core, chip, me = as_index(cc), as_index(2 * cx + cy), as_index(4 * cx + 2 * cy + cc)
    layouts = {n: HalfLayout((full[n].shape[1], full[n].shape[2]), SHARDED[n] - 1) for n in MATMUL_WEIGHTS}
    reducer = GradReducer(layouts, core, chip)
    grads = [None] * DEPTH
    for l in reversed(range(DEPTH)):
        dx, grads[l] = _layer_bwd(dx, saved[l], layer_params(l), tables, reducer)

    g = {n: jnp.stack([grads[l][n] for l in range(DEPTH)]) for n in SMALL_WEIGHTS}
    flat = jnp.concatenate([g[n].reshape(-1) for n in SMALL_WEIGHTS])
    small = jnp.pad(flat, (0, SMALL_ROWS * 128 - flat.shape[0])).reshape(SMALL_ROWS, 128)
    small_sum = sum_devices(gather_small(small, name="gather_small"), small, me, name="sum_devices").reshape(-1)
    mine, other = reducer.finish()

    res = {}
    for n in MATMUL_WEIGHTS:
        res[n] = adamw_halves(w[n], m[n], v[n], [mine[(l, n)] for l in range(DEPTH)],
                              [other[(l, n)] for l in range(DEPTH)], layouts[n], core, name="adamw_halves")
    off = 0
    for n in SMALL_WEIGHTS:
        shp = g[n].shape
        size = int(np.prod(shp))
        gn = small_sum[off:off + size].reshape(shp)
        off += size
        if n in SHARDED:
            gn = lax.dynamic_slice_in_dim(gn, (2 * cx + cy) * w[n].shape[-1], w[n].shape[-1], axis=len(shp) - 1)
        shp = w[n].shape
        cols = shp[-1]
        two_d = lambda t: t.reshape(int(np.prod(shp)) // cols, cols)
        res[n] = [t.reshape(shp) for t in adamw(two_d(w[n]), two_d(m[n]), two_d(v[n]), two_d(gn), name="adamw")]

    out = [loss, dx[None]]
    for k in range(4):
        out.extend(res[n][k] for n in WEIGHTS)
    return tuple(out)
```

```python
import jax
import jax.numpy as jnp
import numpy as np
from jax import lax
from jax.experimental import pallas as pl
from jax.experimental.pallas import tpu as pltpu

F32 = jnp.float32
BF16 = jnp.bfloat16

D_MODEL = 1024
DEPTH = 2
CHUNK = 64
N_LEFT = 8
HEAD_DIM = 64
N_HEADS = 8
WIDTH = 512
POOL_WINDOWS = (2, 4, 8, 16)
GROUP_DIM = 128
MAX_REL = 2 * CHUNK
REL_TABLE = MAX_REL + CHUNK
D_FF = 2816
EPS = 1e-6
QK_SCALE = 0.125
IN_COLS = 7 * WIDTH + 3 * D_MODEL
GATE_COL0 = 7 * WIDTH

ADAM_LR = 0.001
ADAM_B1 = 0.9
ADAM_B2 = 0.999
ADAM_EPS = 1e-08
ADAM_WD = 0.01
ADAM_STEP = 10

VMEM_LIMIT = 56 * 1024 * 1024
ATT_Q = 256
A_Q = 256
A_WIN = A_Q + N_LEFT * CHUNK
HALO = 16
CONV_HALO = 8
NEG = -1e30

MESH_AXES = ("x", "y", "c")
MESH_T = pl.DeviceIdType.MESH


def _cp(sem=None, vmem=VMEM_LIMIT):
    return pltpu.CompilerParams(dimension_semantics=sem, vmem_limit_bytes=vmem)


def _dot(a, b, ca, cb):
    return lax.dot_general(a, b, (((ca,), (cb,)), ((), ())), preferred_element_type=F32)


def _tile(n, cands=(512, 256, 128)):
    for c in cands:
        if n % c == 0:
            return c
    return n


def _split_hi_lo(v):
    hi = v.astype(BF16)
    lo = (v - hi.astype(F32)).astype(BF16)
    return hi, lo


def matmul(a, b, *, ta=False, tb=False, add=None, norm_bwd=None, out_dtype=F32, b_layer=None, behind=None, name):
    plan, sources = behind if behind is not None else (None, [])
    n_src = len(sources)
    n_dst = len(plan.out_shapes()) if plan is not None else 0
    assert add is None or norm_bwd is None
    if ta:
        K, M = a.shape
    else:
        M, K = a.shape
    if tb:
        N, K2 = b.shape[-2:]
    else:
        K2, N = b.shape[-2:]
    assert K == K2, (a.shape, b.shape, ta, tb)
    big = (1024, 1408, 512, 256, 128)
    tm = _tile(M, big)
    tn = _tile(N, (1664,) + big)
    tk = _tile(K, big if ta else (1664, 1408) + big if norm_bwd is not None else (3328, 2816) + big)
    nk = K // tk

    n_extra = 1 if add is not None else 3 if norm_bwd is not None else 0
    n_in = 2 + n_extra + n_src
    n_out = 2 if norm_bwd is not None else 1
    grid = (M // tm, N // tn, nk)
    assert norm_bwd is None or tn == N, "the norm gradient needs whole rows in one output tile"

    def body(*refs):
        a_ref, b_ref = refs[:2]
        extra = refs[2:2 + n_extra]
        o_ref, acc = refs[n_in], refs[n_in + n_out + n_dst]
        i, j, k = pl.program_id(0), pl.program_id(1), pl.program_id(2)
        if plan is not None:
            comm = (refs[n_in - n_src:n_in], refs[n_in + n_out:n_in + n_out + n_dst], refs[n_in + n_out + n_dst + 1:])

            @pl.when((i == 0) & (j == 0) & (k == 0))
            def _():
                plan.start(*comm)

        @pl.when(k == 0)
        def _():
            acc[...] = jnp.zeros_like(acc)

        av = a_ref[...].astype(BF16)
        bv = b_ref[...].astype(BF16)
        acc[...] += _dot(av, bv, 0 if ta else 1, 1 if tb else 0)

        @pl.when(k == nk - 1)
        def _():
            r = acc[...]
            if add is not None:
                r = r + extra[0][...].astype(F32)
            if norm_bwd is not None:
                x_ref, g_ref, dres_ref = extra
                dg_ref = refs[n_in + 1]
                xv = x_ref[...]
                inv = lax.rsqrt(jnp.mean(xv * xv, axis=-1, keepdims=True) + EPS)
                gd = r * g_ref[...]
                mean = jnp.mean(xv * gd, axis=-1, keepdims=True)

                @pl.when(i == 0)
                def _():
                    dg_ref[...] = jnp.zeros_like(dg_ref)

                dg_ref[...] += jnp.sum(r * xv * inv, axis=0, keepdims=True)
                r = dres_ref[...] + inv * gd - xv * (inv * inv * inv * mean)
            o_ref[...] = r.astype(out_dtype)

        if plan is not None:
            @pl.when((i == grid[0] - 1) & (j == grid[1] - 1) & (k == nk - 1))
            def _():
                plan.finish(*comm)

    a_spec = pl.BlockSpec((tk, tm), lambda i, j, k: (k, i)) if ta else pl.BlockSpec((tm, tk), lambda i, j, k: (i, k))
    if b_layer is None:
        b_spec = pl.BlockSpec((tn, tk), lambda i, j, k: (j, k)) if tb else pl.BlockSpec((tk, tn), lambda i, j, k: (k, j))
    elif tb:
        b_spec = pl.BlockSpec((None, tn, tk), lambda i, j, k: (b_layer, j, k))
    else:
        b_spec = pl.BlockSpec((None, tk, tn), lambda i, j, k: (b_layer, k, j))
    o_spec = pl.BlockSpec((tm, tn), lambda i, j, k: (i, j))
    vec_spec = pl.BlockSpec((1, tn), lambda i, j, k: (0, j))
    in_specs = [a_spec, b_spec]
    args = [a, b]
    out_shape = [jax.ShapeDtypeStruct((M, N), out_dtype)]
    out_specs = [o_spec]
    if add is not None:
        in_specs.append(o_spec)
        args.append(add)
    if norm_bwd is not None:
        in_specs += [o_spec, vec_spec, o_spec]
        args += list(norm_bwd)
        out_shape.append(jax.ShapeDtypeStruct((1, N), F32))
        out_specs.append(vec_spec)
    sequential = plan is not None or norm_bwd is not None
    any_space = pl.BlockSpec(memory_space=pl.ANY)
    outs = pl.pallas_call(
        body,
        out_shape=tuple(out_shape) + (tuple(plan.out_shapes()) if plan is not None else ()),
        grid=grid,
        in_specs=in_specs + [any_space] * n_src,
        out_specs=tuple(out_specs) + tuple([any_space] * n_dst),
        scratch_shapes=[pltpu.VMEM((tm, tn), F32)] + (list(plan.scratch_shapes()) if plan is not None else []),
        compiler_params=_cp(("arbitrary" if sequential else "parallel",) * 2 + ("arbitrary",)),
        name=name,
    )(*args, *sources)
    if plan is None:
        return outs[0] if n_out == 1 else tuple(outs)
    return tuple(outs[:n_out]) + (list(outs[n_out:]),) if n_out > 1 else (outs[0], list(outs[1:]))


def rmsnorm_fwd(x, g, *, name):
    S, D = x.shape
    T = _tile(S)

    def body(x_ref, g_ref, h_ref):
        xv = x_ref[...]
        r = lax.rsqrt(jnp.mean(xv * xv, axis=-1, keepdims=True) + EPS)
        h_ref[...] = (xv * r * g_ref[...]).astype(BF16)

    return pl.pallas_call(
        body,
        out_shape=jax.ShapeDtypeStruct((S, D), BF16),
        grid=(S // T,),
        in_specs=[pl.BlockSpec((T, D), lambda i: (i, 0)), pl.BlockSpec((1, D), lambda i: (0, 0))],
        out_specs=pl.BlockSpec((T, D), lambda i: (i, 0)),
        compiler_params=_cp(("parallel",)),
        name=name,
    )(x, g)


def _head_mean_matrix():
    r = lax.broadcasted_iota(jnp.int32, (WIDTH, WIDTH), 0) // HEAD_DIM
    c = lax.broadcasted_iota(jnp.int32, (WIDTH, WIDTH), 1) // HEAD_DIM
    return jnp.where(r == c, 1.0 / HEAD_DIM, 0.0).astype(BF16)


def _head_mean(v, mm):
    hi, lo = _split_hi_lo(v)
    return _dot(hi, mm, 1, 0) + _dot(lo, mm, 1, 0)


def qkv_prep(proj, gq, gk, *, name):
    S = proj.shape[0]
    T = _tile(S)

    def body(qa, ka, va, qb, kb, vb, gq_ref, gk_ref, oqa, oka, ova, oqb, okb, ovb):
        mm = _head_mean_matrix()
        for src, gref, dst, scale in ((qa, gq_ref, oqa, QK_SCALE), (ka, gk_ref, oka, 1.0)):
            v = src[...]
            r = lax.rsqrt(_head_mean(v * v, mm) + EPS)
            dst[...] = (v * r * gref[...] * scale).astype(BF16)
        oqb[...] = (qb[...] * QK_SCALE).astype(BF16)
        for src, dst in ((va, ova), (kb, okb), (vb, ovb)):
            dst[...] = src[...].astype(BF16)

    col = lambda j: pl.BlockSpec((T, WIDTH), lambda i, j=j: (i, j))
    vec = pl.BlockSpec((1, WIDTH), lambda i: (0, 0))
    out = pl.BlockSpec((T, WIDTH), lambda i: (i, 0))
    return pl.pallas_call(
        body,
        out_shape=tuple(jax.ShapeDtypeStruct((S, WIDTH), BF16) for _ in range(6)),
        grid=(S // T,),
        in_specs=[col(0), col(1), col(2), col(3), col(4), col(5), vec, vec],
        out_specs=tuple(out for _ in range(6)),
        compiler_params=_cp(("parallel",)),
        name=name,
    )(proj, proj, proj, proj, proj, proj, gq, gk)


def qknorm_bwd(proj, gq, gk, dqh, dkh, *, name):
    S = proj.shape[0]
    T = _tile(S)

    def body(qa, ka, gq_ref, gk_ref, dq_ref, dk_ref, oq, ok, ogq, ogk):
        i = pl.program_id(0)
        mm = _head_mean_matrix()

        @pl.when(i == 0)
        def _():
            ogq[...] = jnp.zeros_like(ogq)
            ogk[...] = jnp.zeros_like(ogk)

        for src, gref, dref, dst, gdst in ((qa, gq_ref, dq_ref, oq, ogq), (ka, gk_ref, dk_ref, ok, ogk)):
            v = src[...]
            dy = dref[...]
            r = lax.rsqrt(_head_mean(v * v, mm) + EPS)
            gd = dy * gref[...]
            m = _head_mean(v * gd, mm)
            dst[...] = (r * gd - v * (r * r * r * m)).astype(BF16)
            gdst[...] += jnp.sum(dy * v * r, axis=0, keepdims=True)

    col = lambda j: pl.BlockSpec((T, WIDTH), lambda i, j=j: (i, j))
    vec = pl.BlockSpec((1, WIDTH), lambda i: (0, 0))
    row = pl.BlockSpec((T, WIDTH), lambda i: (i, 0))
    return pl.pallas_call(
        body,
        out_shape=(jax.ShapeDtypeStruct((S, WIDTH), BF16), jax.ShapeDtypeStruct((S, WIDTH), BF16),
                   jax.ShapeDtypeStruct((1, WIDTH), F32), jax.ShapeDtypeStruct((1, WIDTH), F32)),
        grid=(S // T,),
        in_specs=[col(0), col(1), vec, vec, row, row],
        out_specs=(row, row, vec, vec),
        compiler_params=_cp(("arbitrary",)),
        name=name,
    )(proj, proj, gq, gk, dqh, dkh)


DIAG_W = 1024
N_VARIANTS = N_LEFT * CHUNK // A_Q + 1


def diagonal_onehot():
    jj = np.arange(DIAG_W)
    diff = np.where(jj < A_WIN, jj, jj - DIAG_W)
    out = np.zeros((N_VARIANTS, DIAG_W, REL_TABLE), np.float32)
    for v in range(N_VARIANTS):
        rel = np.clip(A_Q * v - diff, -(CHUNK - 1), MAX_REL) + (CHUNK - 1)
        out[v, jj, rel] = 1.0
    return out.reshape(N_VARIANTS * DIAG_W, REL_TABLE)


def exact_dot(a, b, *, name):
    def body(a_ref, b_ref, o_ref):
        o_ref[...] = jnp.dot(a_ref[...], b_ref[...], precision=lax.Precision.HIGHEST, preferred_element_type=F32)

    return pl.pallas_call(body, out_shape=jax.ShapeDtypeStruct((a.shape[0], b.shape[1]), F32),
                          compiler_params=_cp(), name=name)(a, b)


def _band_valid(v):
    qc = (lax.broadcasted_iota(jnp.int32, (A_Q, A_WIN), 0) + A_Q * v) // CHUNK
    kc = lax.broadcasted_iota(jnp.int32, (A_Q, A_WIN), 1) // CHUNK
    return (kc <= qc) & (kc >= qc - N_LEFT)


def bias_expand(diag, *, name):
    def body(d_ref, o_ref):
        rows = jnp.broadcast_to(d_ref[0, 0], (A_Q, DIAG_W))
        skew = pltpu.roll(rows, 0, 1, stride=1, stride_axis=0)
        o_ref[0, 0] = jnp.where(_band_valid(pl.program_id(0)), skew[:, :A_WIN], NEG)

    return pl.pallas_call(
        body,
        out_shape=jax.ShapeDtypeStruct((N_VARIANTS, N_HEADS, A_Q, A_WIN), F32),
        grid=(N_VARIANTS, N_HEADS),
        in_specs=[pl.BlockSpec((1, 1, 1, DIAG_W), lambda v, h: (v, h, 0, 0))],
        out_specs=pl.BlockSpec((1, 1, A_Q, A_WIN), lambda v, h: (v, h, 0, 0)),
        compiler_params=_cp(("parallel", "parallel")),
        name=name,
    )(diag)


def relbias_reduce(dbias, *, name):
    def body(db_ref, o_ref):
        acc = None
        for a in range(A_Q // 8):
            x = jnp.concatenate([db_ref[0, 0, 8 * a:8 * a + 8, :], jnp.zeros((8, DIAG_W - A_WIN), F32)], axis=1)
            x = pltpu.roll(x, DIAG_W - 8 * a, 1) if a else x
            acc = x if acc is None else acc + x
        row = lax.broadcasted_iota(jnp.int32, (8, DIAG_W), 0)
        for b in range(3):
            acc = jnp.where((row >> b) & 1 == 1, pltpu.roll(acc, DIAG_W - (1 << b), 1), acc)
        o_ref[0, 0] = jnp.sum(acc, axis=0, keepdims=True)

    return pl.pallas_call(
        body,
        out_shape=jax.ShapeDtypeStruct((N_VARIANTS, N_HEADS, 1, DIAG_W), F32),
        grid=(N_VARIANTS, N_HEADS),
        in_specs=[pl.BlockSpec((1, 1, A_Q, A_WIN), lambda v, h: (v, h, 0, 0))],
        out_specs=pl.BlockSpec((1, 1, 1, DIAG_W), lambda v, h: (v, h, 0, 0)),
        compiler_params=_cp(("parallel", "parallel")),
        name=name,
    )(dbias)


def _a_window_start(qb):
    return pl.multiple_of(jnp.maximum(qb * A_Q - N_LEFT * CHUNK, 0), A_Q)


def attn_a_fwd(q, k, v, biasm, *, name):
    S = q.shape[0]
    nq = S // A_Q

    def body(q_ref, k_ref, v_ref, b_ref, o_ref):
        qb = pl.program_id(1)
        start = _a_window_start(qb)
        outs = []
        for h in range(2):
            lanes = slice(h * HEAD_DIM, (h + 1) * HEAD_DIM)
            qh = q_ref[:, lanes]
            kw = k_ref[pl.ds(start, A_WIN), lanes]
            vw = v_ref[pl.ds(start, A_WIN), lanes]
            s = _dot(qh, kw, 1, 1) + b_ref[0, h]
            m = jnp.max(s, axis=-1, keepdims=True)
            e = jnp.exp(s - m)
            outs.append(_dot(e.astype(BF16), vw, 1, 0) * (1.0 / jnp.sum(e, axis=-1, keepdims=True)))
        o_ref[...] = jnp.concatenate(outs, axis=1).astype(BF16)

    qspec = pl.BlockSpec((A_Q, 2 * HEAD_DIM), lambda hp, qb: (qb, hp))
    kvspec = pl.BlockSpec((S, 2 * HEAD_DIM), lambda hp, qb: (0, hp))
    bspec = pl.BlockSpec((1, 2, A_Q, A_WIN), lambda hp, qb: (jnp.minimum(qb, N_VARIANTS - 1), hp, 0, 0))
    return pl.pallas_call(
        body,
        out_shape=jax.ShapeDtypeStruct((S, WIDTH), BF16),
        grid=(N_HEADS // 2, nq),
        in_specs=[qspec, kvspec, kvspec, bspec],
        out_specs=qspec,
        compiler_params=_cp(("parallel", "arbitrary")),
        name=name,
    )(q, k, v, biasm)


def attn_a_bwd(q, k, v, biasm, do, *, name):
    S = q.shape[0]
    nq = S // A_Q

    def body(q_ref, k_ref, v_ref, b_ref, do_ref, dq_ref, dk_ref, dv_ref, db_ref):
        qb = pl.program_id(1)
        start = _a_window_start(qb)

        @pl.when(qb == 0)
        def _():
            dk_ref[...] = jnp.zeros_like(dk_ref)
            dv_ref[...] = jnp.zeros_like(dv_ref)

        @pl.when(qb < N_VARIANTS)
        def _():
            db_ref[...] = jnp.zeros_like(db_ref)

        dqs = []
        for h in range(2):
            lanes = slice(h * HEAD_DIM, (h + 1) * HEAD_DIM)
            qh = q_ref[:, lanes]
            doh = do_ref[:, lanes]
            kw = k_ref[pl.ds(start, A_WIN), lanes]
            vw = v_ref[pl.ds(start, A_WIN), lanes]
            s = _dot(qh, kw, 1, 1) + b_ref[0, h]
            m = jnp.max(s, axis=-1, keepdims=True)
            e = jnp.exp(s - m)
            p = e * (1.0 / jnp.sum(e, axis=-1, keepdims=True))
            dp = _dot(doh, vw, 1, 1)
            delta = jnp.sum(p * dp, axis=-1, keepdims=True)
            ds = p * (dp - delta)
            db_ref[0, h] += ds
            dsb = ds.astype(BF16)
            dqs.append(_dot(dsb, kw, 1, 0) * QK_SCALE)
            dk_ref[pl.ds(start, A_WIN), lanes] += _dot(dsb, qh, 0, 0)
            dv_ref[pl.ds(start, A_WIN), lanes] += _dot(p.astype(BF16), doh, 0, 0)
        dq_ref[...] = jnp.concatenate(dqs, axis=1)

    qspec = pl.BlockSpec((A_Q, 2 * HEAD_DIM), lambda hp, qb: (qb, hp))
    kvspec = pl.BlockSpec((S, 2 * HEAD_DIM), lambda hp, qb: (0, hp))
    bspec = pl.BlockSpec((1, 2, A_Q, A_WIN), lambda hp, qb: (jnp.minimum(qb, N_VARIANTS - 1), hp, 0, 0))
    return pl.pallas_call(
        body,
        out_shape=(jax.ShapeDtypeStruct((S, WIDTH), F32), jax.ShapeDtypeStruct((S, WIDTH), F32),
                   jax.ShapeDtypeStruct((S, WIDTH), F32), jax.ShapeDtypeStruct((N_VARIANTS, N_HEADS, A_Q, A_WIN), F32)),
        grid=(N_HEADS // 2, nq),
        in_specs=[qspec, kvspec, kvspec, bspec, qspec],
        out_specs=(qspec, kvspec, kvspec, bspec),
        compiler_params=_cp(("parallel", "arbitrary")),
        name=name,
    )(q, k, v, biasm, do)


def _tri(kind):
    j = lax.broadcasted_iota(jnp.int32, (ATT_Q, ATT_Q), 0)
    s = lax.broadcasted_iota(jnp.int32, (ATT_Q, ATT_Q), 1)
    if kind == "gt":
        m = j > s
    elif kind == "le":
        m = j <= s
    else:
        m = j < s
    return jnp.where(m, 1.0, 0.0).astype(BF16)


def _cum(v, tri):
    hi, lo = _split_hi_lo(v)
    return _dot(hi, tri, 1, 0) + _dot(lo, tri, 1, 0)


def _log_sigmoids(z, mask):
    t = jnp.log(1.0 + jnp.exp(-jnp.abs(z)))
    keep = -(jnp.maximum(z, 0.0) + t)
    take = jnp.minimum(z, 0.0) - t
    return (keep if mask is None else jnp.where(mask, keep, 0.0)), take


def _strictly_before():
    row = lax.broadcasted_iota(jnp.int32, (ATT_Q, ATT_Q), 0)
    col = lax.broadcasted_iota(jnp.int32, (ATT_Q, ATT_Q), 1)
    return col < row


EXIT_LOG = -104.0


def attn_b_fwd(q, k, v, *, gather=None, name):
    S = q.shape[0]
    nq = S // ATT_Q
    plan, shards, fulls = gather if gather is not None else (None, [], [])
    ng = len(shards)

    def body(q_ref, k_ref, v_ref, *rest):
        hp = pl.program_id(0)
        qb = pl.program_id(1)
        o_ref, t_ref, n_ref = rest[2 * ng:2 * ng + 3]
        if plan is not None:
            comm = (rest[:ng], rest[2 * ng + 3:3 * ng + 3], rest[3 * ng + 3:])

            @pl.when(jnp.logical_and(hp == 0, qb == 0))
            def _():
                plan.start(*comm)

        tri = _tri("gt")

        def block(kb, carry, mask):
            ks = pl.multiple_of(kb * ATT_Q, ATT_Q)
            new = []
            for h in range(2):
                lanes = slice(h * HEAD_DIM, (h + 1) * HEAD_DIM)
                c, acc = carry[h]
                z = _dot(q_ref[:, lanes], k_ref[pl.ds(ks, ATT_Q), lanes], 1, 1)
                keep, take = _log_sigmoids(z, mask)
                w = jnp.exp(take + (_cum(keep, tri) + c))
                if mask is not None:
                    w = jnp.where(mask, w, 0.0)
                acc = acc + _dot(w.astype(BF16), v_ref[pl.ds(ks, ATT_Q), lanes], 1, 0)
                c = c + jnp.sum(keep, axis=-1, keepdims=True)
                new.append((c, acc))
            return jnp.maximum(jnp.max(new[0][0]), jnp.max(new[1][0])), tuple(new)

        def cond(state):
            it, cmax, _ = state
            return jnp.logical_and(it <= qb, cmax >= EXIT_LOG)

        def step(state):
            it, _, carry = state
            cmax, carry = block(qb - it, carry, None)
            return it + 1, cmax, carry

        init = tuple((jnp.zeros((ATT_Q, 1), F32), jnp.zeros((ATT_Q, HEAD_DIM), F32)) for _ in range(2))
        cmax, diag = block(qb, init, _strictly_before())
        visited, _, res = lax.while_loop(cond, step, (jnp.int32(1), cmax, diag))
        o_ref[...] = jnp.concatenate([res[0][1], res[1][1]], axis=1).astype(BF16)
        t_ref[...] = jnp.concatenate([jnp.broadcast_to(res[h][0], (ATT_Q, HEAD_DIM)) for h in range(2)], axis=1)
        n_ref[hp, qb] = visited.astype(F32)
        if plan is not None:
            @pl.when(jnp.logical_and(hp == N_HEADS // 2 - 1, qb == nq - 1))
            def _():
                plan.finish(*comm)

    qspec = pl.BlockSpec((ATT_Q, 2 * HEAD_DIM), lambda hp, qb: (qb, hp))
    kvspec = pl.BlockSpec((S, 2 * HEAD_DIM), lambda hp, qb: (0, hp))
    outs = pl.pallas_call(
        body,
        out_shape=(jax.ShapeDtypeStruct((S, WIDTH), BF16), jax.ShapeDtypeStruct((S, WIDTH), F32),
                   jax.ShapeDtypeStruct((N_HEADS // 2, nq), F32))
        + tuple(jax.ShapeDtypeStruct(f.shape, f.dtype) for f in fulls),
        grid=(N_HEADS // 2, nq),
        in_specs=[qspec, kvspec, kvspec] + [ANY] * (2 * ng),
        out_specs=(qspec, qspec, pl.BlockSpec(memory_space=pltpu.SMEM)) + tuple([ANY] * ng),
        scratch_shapes=plan.scratch_shapes() if plan is not None else (),
        input_output_aliases={3 + ng + i: 3 + i for i in range(ng)},
        compiler_params=_cp(("arbitrary", "arbitrary")),
        name=name,
    )(q, k, v, *shards, *fulls)
    return outs if plan is None else (outs[0], outs[1], outs[2], list(outs[3:]))


def attn_b_bwd(q, k, v, tot, nblk, do, *, scatter=None, name):
    S = q.shape[0]
    nq = S // ATT_Q
    plan, pairs16 = scatter if scatter is not None else (None, [])
    ns = len(pairs16)

    def body(q_ref, k_ref, v_ref, t_ref, n_ref, do_ref, *rest):
        hp = pl.program_id(0)
        qb = pl.program_id(1)
        dq_ref, dk_ref, dv_ref = rest[ns:ns + 3]
        if plan is not None:
            comm = (rest[:ns], rest[ns + 3:2 * ns + 3], rest[2 * ns + 3:])

            @pl.when(jnp.logical_and(hp == 0, qb == 0))
            def _():
                plan.start(*comm)

        first = jnp.clip(qb + 1 - n_ref[hp, qb].astype(jnp.int32), 0, qb + 1)
        tri_le = _tri("le")
        tri_lt = _tri("lt")

        @pl.when(qb == 0)
        def _():
            dk_ref[...] = jnp.zeros_like(dk_ref)
            dv_ref[...] = jnp.zeros_like(dv_ref)

        def block(kb, carry, mask):
            ks = pl.multiple_of(kb * ATT_Q, ATT_Q)
            new = []
            for h in range(2):
                lanes = slice(h * HEAD_DIM, (h + 1) * HEAD_DIM)
                cl, cg, dq = carry[h]
                qh = q_ref[:, lanes]
                doh = do_ref[:, lanes]
                kh = k_ref[pl.ds(ks, ATT_Q), lanes]
                vh = v_ref[pl.ds(ks, ATT_Q), lanes]
                totl = t_ref[:, h * HEAD_DIM:h * HEAD_DIM + 1]
                z = _dot(qh, kh, 1, 1)
                keep, take = _log_sigmoids(z, mask)
                sig = jnp.exp(take)
                w = sig * jnp.exp((totl - cl) - _cum(keep, tri_le))
                if mask is not None:
                    w = jnp.where(mask, w, 0.0)
                g = w * _dot(doh, vh, 1, 1)
                G = _dot(g.astype(BF16), tri_lt, 1, 0) + cg
                dz = g * (1.0 - sig) - sig * G
                if mask is not None:
                    dz = jnp.where(mask, dz, 0.0)
                dz = dz.astype(BF16)
                dq = dq + _dot(dz, kh, 1, 0)
                dk_ref[pl.ds(ks, ATT_Q), lanes] += _dot(dz, qh, 0, 0)
                dv_ref[pl.ds(ks, ATT_Q), lanes] += _dot(w.astype(BF16), doh, 0, 0)
                cl = cl + jnp.sum(keep, axis=-1, keepdims=True)
                cg = cg + jnp.sum(g, axis=-1, keepdims=True)
                new.append((cl, cg, dq))
            return tuple(new)

        init = tuple((jnp.zeros((ATT_Q, 1), F32), jnp.zeros((ATT_Q, 1), F32), jnp.zeros((ATT_Q, HEAD_DIM), F32))
                     for _ in range(2))
        res = lax.fori_loop(jnp.minimum(first, qb), qb, lambda kb, carry: block(kb, carry, None), init)
        res = block(qb, res, _strictly_before())
        dq_ref[...] = (jnp.concatenate([res[0][2], res[1][2]], axis=1) * QK_SCALE).astype(BF16)
        if plan is not None:
            @pl.when(jnp.logical_and(hp == N_HEADS // 2 - 1, qb == nq - 1))
            def _():
                plan.finish(*comm)

    qspec = pl.BlockSpec((ATT_Q, 2 * HEAD_DIM), lambda hp, qb: (qb, hp))
    kvspec = pl.BlockSpec((S, 2 * HEAD_DIM), lambda hp, qb: (0, hp))
    outs = pl.pallas_call(
        body,
        out_shape=(jax.ShapeDtypeStruct((S, WIDTH), BF16), jax.ShapeDtypeStruct((S, WIDTH), F32),
                   jax.ShapeDtypeStruct((S, WIDTH), F32)) + (tuple(plan.out_shapes()) if plan is not None else ()),
        grid=(N_HEADS // 2, nq),
        in_specs=[qspec, kvspec, kvspec, qspec, pl.BlockSpec(memory_space=pltpu.SMEM), qspec] + [ANY] * ns,
        out_specs=(qspec, kvspec, kvspec) + tuple([ANY] * ns),
        scratch_shapes=plan.scratch_shapes() if plan is not None else (),
        compiler_params=_cp(("arbitrary", "arbitrary")),
        name=name,
    )(q, k, v, tot, nblk, do, *pairs16)
    return outs if plan is None else (outs[0], outs[1], outs[2], list(outs[3:]))


U_COLBLK = 6


def _pool_counts(t0, rows):
    t = t0 + lax.broadcasted_iota(jnp.int32, (rows, WIDTH), 0)
    lane_grp = lax.broadcasted_iota(jnp.int32, (rows, WIDTH), 1) // GROUP_DIM
    win = jnp.where(lane_grp == 0, 2, jnp.where(lane_grp == 1, 4, jnp.where(lane_grp == 2, 8, 16)))
    cnt = jnp.minimum(t + 1, win)
    return 1.0 / cnt.astype(F32), lane_grp


def _window_sums(ext, shift_fn):
    s2 = ext + shift_fn(ext, 1)
    s4 = s2 + shift_fn(s2, 2)
    s8 = s4 + shift_fn(s4, 4)
    s16 = s8 + shift_fn(s8, 8)
    return s2, s4, s8, s16


def _select_group(lane_grp, s2, s4, s8, s16):
    return jnp.where(lane_grp == 0, s2, jnp.where(lane_grp == 1, s4, jnp.where(lane_grp == 2, s8, s16)))


def _pooled_tile(u_ref, h_ref, i, T):
    halo = jnp.where(i > 0, h_ref[...], 0.0)
    ext = jnp.concatenate([halo, u_ref[...]], axis=0)
    n = T + HALO
    sums = _window_sums(ext, lambda v, k: pltpu.roll(v, k, 0))
    inv, lane_grp = _pool_counts(i * T - HALO, n)
    pooled = _select_group(lane_grp, *sums) * inv - ext
    return pooled[HALO:, :]


def pool_fwd(proj, w_pool, scale, *, name):
    S = proj.shape[0]
    T = _tile(S)
    hb = T // HALO

    def body(u_ref, h_ref, w_ref, s_ref, o_ref):
        i = pl.program_id(0)
        pooled = _pooled_tile(u_ref, h_ref, i, T).astype(BF16)
        outs = [_dot(pooled[:, g * GROUP_DIM:(g + 1) * GROUP_DIM], w_ref[g], 1, 0) for g in range(4)]
        o_ref[...] = (jnp.concatenate(outs, axis=1) * s_ref[...]).astype(BF16)

    return pl.pallas_call(
        body,
        out_shape=jax.ShapeDtypeStruct((S, WIDTH), BF16),
        grid=(S // T,),
        in_specs=[pl.BlockSpec((T, WIDTH), lambda i: (i, U_COLBLK)),
                  pl.BlockSpec((HALO, WIDTH), lambda i: (jnp.maximum(i * hb - 1, 0), U_COLBLK)),
                  pl.BlockSpec((4, GROUP_DIM, GROUP_DIM), lambda i: (0, 0, 0)),
                  pl.BlockSpec((1, WIDTH), lambda i: (0, 0))],
        out_specs=pl.BlockSpec((T, WIDTH), lambda i: (i, 0)),
        compiler_params=_cp(("parallel",)),
        name=name,
    )(proj, proj, w_pool, scale)


def pool_bwd(proj, w_pool, scale, do, *, name):
    S = proj.shape[0]
    T = _tile(S)
    hb = T // HALO
    nt = S // T

    def body(u_ref, h_ref, w_ref, s_ref, do_ref, dof_ref, du_ref, dw_ref, ds_ref):
        i = pl.program_id(0)

        @pl.when(i == 0)
        def _():
            dw_ref[...] = jnp.zeros_like(dw_ref)
            ds_ref[...] = jnp.zeros_like(ds_ref)

        pooled = _pooled_tile(u_ref, h_ref, i, T).astype(BF16)
        dov = do_ref[...].astype(F32)
        fut = jnp.where(i < nt - 1, dof_ref[...].astype(F32), 0.0)
        dmix = (jnp.concatenate([dov, fut], axis=0) * s_ref[...]).astype(BF16)
        mixed, dpool = [], []
        for g in range(4):
            lanes = slice(g * GROUP_DIM, (g + 1) * GROUP_DIM)
            mixed.append(_dot(pooled[:, lanes], w_ref[g], 1, 0))
            dw_ref[g] += _dot(pooled[:, lanes], dmix[:T, lanes], 0, 0)
            dpool.append(_dot(dmix[:, lanes], w_ref[g], 1, 1))
        ds_ref[...] += jnp.sum(dov * jnp.concatenate(mixed, axis=1), axis=0, keepdims=True)
        dp = jnp.concatenate(dpool, axis=1)
        n = T + HALO
        inv, lane_grp = _pool_counts(i * T, n)
        sums = _window_sums(dp * inv, lambda v, k: pltpu.roll(v, n - k, 0))
        du = _select_group(lane_grp, *sums) - dp
        du_ref[...] = du[:T, :].astype(BF16)

    row = pl.BlockSpec((T, WIDTH), lambda i: (i, 0))
    return pl.pallas_call(
        body,
        out_shape=(jax.ShapeDtypeStruct((S, WIDTH), BF16), jax.ShapeDtypeStruct((4, GROUP_DIM, GROUP_DIM), F32),
                   jax.ShapeDtypeStruct((1, WIDTH), F32)),
        grid=(nt,),
        in_specs=[pl.BlockSpec((T, WIDTH), lambda i: (i, U_COLBLK)),
                  pl.BlockSpec((HALO, WIDTH), lambda i: (jnp.maximum(i * hb - 1, 0), U_COLBLK)),
                  pl.BlockSpec((4, GROUP_DIM, GROUP_DIM), lambda i: (0, 0, 0)),
                  pl.BlockSpec((1, WIDTH), lambda i: (0, 0)),
                  row,
                  pl.BlockSpec((HALO, WIDTH), lambda i: (jnp.minimum((i + 1) * hb, S // HALO - 1), 0))],
        out_specs=(row, pl.BlockSpec((4, GROUP_DIM, GROUP_DIM), lambda i: (0, 0, 0)),
                   pl.BlockSpec((1, WIDTH), lambda i: (0, 0))),
        compiler_params=_cp(("arbitrary",)),
        name=name,
    )(proj, proj, w_pool, scale, do, do)


GATE_BLK0 = GATE_COL0 // WIDTH


def merge_fwd(oa, ob, oc, proj, b_gate, wa, wb, wc, *, name):
    S = oa.shape[0]
    T = _tile(S)

    def body(oa_ref, ob_ref, oc_ref, ga, gb, gc, ba, bb, bc, wa_ref, wb_ref, wc_ref, m_ref):
        acc = None
        for o_ref, g_ref, b_ref, w_ref in ((oa_ref, ga, ba, wa_ref), (ob_ref, gb, bb, wb_ref), (oc_ref, gc, bc, wc_ref)):
            y = _dot(o_ref[...], w_ref[...], 1, 0)
            t = jax.nn.sigmoid(g_ref[...] + b_ref[...]) * y
            acc = t if acc is None else acc + t
        m_ref[...] = acc.astype(BF16)

    row = pl.BlockSpec((T, WIDTH), lambda i, n: (i, 0))
    gate = lambda b: pl.BlockSpec((T, WIDTH), lambda i, n, b=b: (i, GATE_BLK0 + 2 * b + n))
    bias = lambda b: pl.BlockSpec((1, WIDTH), lambda i, n, b=b: (0, 2 * b + n))
    wspec = pl.BlockSpec((WIDTH, WIDTH), lambda i, n: (0, n))
    return pl.pallas_call(
        body,
        out_shape=jax.ShapeDtypeStruct((S, D_MODEL), BF16),
        grid=(S // T, 2),
        in_specs=[row, row, row, gate(0), gate(1), gate(2), bias(0), bias(1), bias(2), wspec, wspec, wspec],
        out_specs=pl.BlockSpec((T, WIDTH), lambda i, n: (i, n)),
        compiler_params=_cp(("parallel", "parallel")),
        name=name,
    )(oa, ob, oc, proj, proj, proj, b_gate, b_gate, b_gate, wa, wb, wc)


def merge_bwd(dm, oa, ob, oc, proj, b_gate, wa, wb, wc, *, name):
    S = oa.shape[0]
    T = _tile(S)

    def body(dm_ref, oa_ref, ob_ref, oc_ref, ga, gb, gc, ba, bb, bc, wa_ref, wb_ref, wc_ref,
             ta, tb, tc, dga, dgb, dgc, dba, dbb, dbc):
        i = pl.program_id(1)
        dmv = dm_ref[...].astype(F32)
        for o_ref, g_ref, b_ref, w_ref, t_ref, dg_ref, db_ref in (
                (oa_ref, ga, ba, wa_ref, ta, dga, dba), (ob_ref, gb, bb, wb_ref, tb, dgb, dbb),
                (oc_ref, gc, bc, wc_ref, tc, dgc, dbc)):
            y = _dot(o_ref[...], w_ref[...], 1, 0)
            gate = jax.nn.sigmoid(g_ref[...] + b_ref[...])
            t_ref[...] = (gate * dmv).astype(BF16)
            dgl = dmv * y * gate * (1.0 - gate)
            dg_ref[...] = dgl.astype(BF16)

            @pl.when(i == 0)
            def _():
                db_ref[...] = jnp.zeros_like(db_ref)

            db_ref[...] += jnp.sum(dgl, axis=0, keepdims=True)

    row = pl.BlockSpec((T, WIDTH), lambda n, i: (i, 0))
    half = pl.BlockSpec((T, WIDTH), lambda n, i: (i, n))
    gate = lambda b: pl.BlockSpec((T, WIDTH), lambda n, i, b=b: (i, GATE_BLK0 + 2 * b + n))
    bias = lambda b: pl.BlockSpec((1, WIDTH), lambda n, i, b=b: (0, 2 * b + n))
    wspec = pl.BlockSpec((WIDTH, WIDTH), lambda n, i: (0, n))
    bvec = pl.BlockSpec((1, WIDTH), lambda n, i: (0, n))
    act = jax.ShapeDtypeStruct((S, D_MODEL), BF16)
    vec = jax.ShapeDtypeStruct((1, D_MODEL), F32)
    return pl.pallas_call(
        body,
        out_shape=(act, act, act, act, act, act, vec, vec, vec),
        grid=(2, S // T),
        in_specs=[half, row, row, row, gate(0), gate(1), gate(2), bias(0), bias(1), bias(2), wspec, wspec, wspec],
        out_specs=(half, half, half, half, half, half, bvec, bvec, bvec),
        compiler_params=_cp(("parallel", "arbitrary")),
        name=name,
    )(dm, oa, ob, oc, proj, proj, proj, b_gate, b_gate, b_gate, wa, wb, wc)


FF_T = 256
FF_BLKS = D_FF // FF_T


def _silu_parts(x):
    s = jax.nn.sigmoid(x)
    return x * s, s


def _conv3(ext, w_ref, b_ref):
    taps = (pltpu.roll(ext, 2, 0), pltpu.roll(ext, 1, 0), ext)
    return b_ref[...] + w_ref[0:1, :] * taps[0] + w_ref[1:2, :] * taps[1] + w_ref[2:3, :] * taps[2], taps


def conv_glu_fwd(u, conv_w, conv_b, *, name):
    S = u.shape[0]
    T = _tile(S)
    hb = T // CONV_HALO

    def body(ug, ugh, uv, uvh, wg, wv, bg, bv, a_ref):
        i = pl.program_id(1)
        cs = []
        for m_ref, h_ref, w_ref, b_ref in ((ug, ugh, wg, bg), (uv, uvh, wv, bv)):
            halo = jnp.where(i > 0, h_ref[...], 0.0)
            ext = jnp.concatenate([halo, m_ref[...]], axis=0)
            cs.append(_conv3(ext, w_ref, b_ref)[0][CONV_HALO:, :])
        act, _ = _silu_parts(cs[0])
        a_ref[...] = (act * cs[1]).astype(BF16)

    main = lambda o: pl.BlockSpec((T, FF_T), lambda c, i, o=o: (i, c + o))
    halo = lambda o: pl.BlockSpec((CONV_HALO, FF_T), lambda c, i, o=o: (jnp.maximum(i * hb - 1, 0), c + o))
    wsp = lambda o: pl.BlockSpec((3, FF_T), lambda c, i, o=o: (0, c + o))
    bsp = lambda o: pl.BlockSpec((1, FF_T), lambda c, i, o=o: (0, c + o))
    return pl.pallas_call(
        body,
        out_shape=jax.ShapeDtypeStruct((S, D_FF), BF16),
        grid=(FF_BLKS, S // T),
        in_specs=[main(0), halo(0), main(FF_BLKS), halo(FF_BLKS), wsp(0), wsp(FF_BLKS), bsp(0), bsp(FF_BLKS)],
        out_specs=pl.BlockSpec((T, FF_T), lambda c, i: (i, c)),
        compiler_params=_cp(("parallel", "parallel")),
        name=name,
    )(u, u, u, u, conv_w, conv_w, conv_b, conv_b)


def conv_glu_bwd(u, conv_w, conv_b, da, *, name):
    S = u.shape[0]
    T = _tile(S)
    hb = T // CONV_HALO
    nt = S // T
    n = T + 2 * CONV_HALO

    def body(ug, ugp, ugf, uv, uvp, uvf, wg, wv, bg, bv, da_ref, daf_ref,
             dug, duv, dwg, dwv, dbg, dbv):
        i = pl.program_id(1)
        first, last = i == 0, i == nt - 1
        taps, cs = [], []
        for m_ref, p_ref, f_ref, w_ref, b_ref in ((ug, ugp, ugf, wg, bg), (uv, uvp, uvf, wv, bv)):
            ext = jnp.concatenate([jnp.where(first, 0.0, p_ref[...]), m_ref[...], jnp.where(last, 0.0, f_ref[...])], axis=0)
            c, tp = _conv3(ext, w_ref, b_ref)
            cs.append(c)
            taps.append(tp)
        dae = jnp.concatenate([jnp.zeros((CONV_HALO, FF_T), F32), da_ref[...].astype(F32),
                               jnp.where(last, 0.0, daf_ref[...].astype(F32))], axis=0)
        act, sg = _silu_parts(cs[0])
        dcs = (dae * cs[1] * (sg * (1.0 + cs[0] * (1.0 - sg))), dae * act)
        main = slice(CONV_HALO, CONV_HALO + T)
        for tp, dc, w_ref, du_ref, dw_ref, db_ref in ((taps[0], dcs[0], wg, dug, dwg, dbg),
                                                      (taps[1], dcs[1], wv, duv, dwv, dbv)):
            du = (w_ref[2:3, :] * dc + w_ref[1:2, :] * pltpu.roll(dc, n - 1, 0) + w_ref[0:1, :] * pltpu.roll(dc, n - 2, 0))
            du_ref[...] = du[main, :].astype(BF16)
            dcm = dc[main, :]
            rows = [jnp.sum(dcm * tp[j][main, :], axis=0, keepdims=True) for j in range(3)]

            @pl.when(first)
            def _():
                dw_ref[...] = jnp.zeros_like(dw_ref)
                db_ref[...] = jnp.zeros_like(db_ref)

            dw_ref[...] += jnp.concatenate(rows, axis=0)
            db_ref[...] += jnp.sum(dcm, axis=0, keepdims=True)

    main = lambda o: pl.BlockSpec((T, FF_T), lambda c, i, o=o: (i, c + o))
    past = lambda o: pl.BlockSpec((CONV_HALO, FF_T), lambda c, i, o=o: (jnp.maximum(i * hb - 1, 0), c + o))
    fut = lambda o: pl.BlockSpec((CONV_HALO, FF_T), lambda c, i, o=o: (jnp.minimum((i + 1) * hb, S // CONV_HALO - 1), c + o))
    wsp = lambda o: pl.BlockSpec((3, FF_T), lambda c, i, o=o: (0, c + o))
    bsp = lambda o: pl.BlockSpec((1, FF_T), lambda c, i, o=o: (0, c + o))
    return pl.pallas_call(
        body,
        out_shape=(jax.ShapeDtypeStruct((S, D_FF), BF16), jax.ShapeDtypeStruct((S, D_FF), BF16),
                   jax.ShapeDtypeStruct((3, D_FF), F32), jax.ShapeDtypeStruct((3, D_FF), F32),
                   jax.ShapeDtypeStruct((1, D_FF), F32), jax.ShapeDtypeStruct((1, D_FF), F32)),
        grid=(FF_BLKS, nt),
        in_specs=[main(0), past(0), fut(0), main(FF_BLKS), past(FF_BLKS), fut(FF_BLKS),
                  wsp(0), wsp(FF_BLKS), bsp(0), bsp(FF_BLKS), main(0), fut(0)],
        out_specs=(main(0), main(0), wsp(0), wsp(0), bsp(0), bsp(0)),
        compiler_params=_cp(("parallel", "arbitrary")),
        name=name,
    )(u, u, u, u, u, u, conv_w, conv_w, conv_b, conv_b, da, da)


def loss_head(y, target, *, name):
    S, D = y.shape
    T = _tile(S)

    def body(y_ref, t_ref, dy_ref, l_ref):
        i = pl.program_id(0)
        err = y_ref[...] - t_ref[...]
        dy_ref[...] = err * (1.0 / D)

        @pl.when(i == 0)
        def _():
            l_ref[...] = jnp.zeros_like(l_ref)

        l_ref[...] += 0.5 * jnp.sum(jnp.mean(err * err, axis=-1, keepdims=True))

    row = pl.BlockSpec((T, D), lambda i: (i, 0))
    return pl.pallas_call(
        body,
        out_shape=(jax.ShapeDtypeStruct((S, D), F32), jax.ShapeDtypeStruct((8, 128), F32)),
        grid=(S // T,),
        in_specs=[row, row],
        out_specs=(row, pl.BlockSpec((8, 128), lambda i: (0, 0))),
        compiler_params=_cp(("arbitrary",)),
        name=name,
    )(y, target)


ELEMS_PER_BLOCK = 256 * 1024


def _rows_tile(rows, cols):
    if rows * cols <= ELEMS_PER_BLOCK or rows % 8:
        return rows
    best = 8
    for tr in range(8, rows + 1, 8):
        if rows % tr == 0 and tr * cols <= ELEMS_PER_BLOCK:
            best = tr
    return best


def _adamw_math(g, w_ref, m_ref, v_ref, g_out, d_out, m_out, v_out):
    mn = ADAM_B1 * m_ref[...] + (1.0 - ADAM_B1) * g
    vn = ADAM_B2 * v_ref[...] + (1.0 - ADAM_B2) * (g * g)
    m_hat = mn / (1.0 - ADAM_B1 ** ADAM_STEP)
    v_hat = vn / (1.0 - ADAM_B2 ** ADAM_STEP)
    g_out[...] = g
    d_out[...] = -ADAM_LR * (m_hat / (jnp.sqrt(v_hat) + ADAM_EPS) + ADAM_WD * w_ref[...])
    m_out[...] = mn
    v_out[...] = vn


def adamw(w, m, v, g, *, name):
    rows, cols = w.shape
    tr = _rows_tile(rows, cols)

    def body(w_ref, m_ref, v_ref, g_ref, g_out, d_out, m_out, v_out):
        _adamw_math(g_ref[...], w_ref, m_ref, v_ref, g_out, d_out, m_out, v_out)

    spec = pl.BlockSpec((tr, cols), lambda i: (i, 0))
    shp = jax.ShapeDtypeStruct((rows, cols), F32)
    return pl.pallas_call(
        body,
        out_shape=(shp, shp, shp, shp),
        grid=(rows // tr,),
        in_specs=[spec] * 4,
        out_specs=(spec, spec, spec, spec),
        compiler_params=_cp(("parallel",)),
        name=name,
    )(w, m, v, g)


ANY = pl.BlockSpec(memory_space=pl.ANY)
STAGE_BYTES = 2 * 1024 * 1024


def _mesh_pos():
    return lax.axis_index("x"), lax.axis_index("y"), lax.axis_index("c")


def _chip_peers(x, y):
    return [(1 - x, y), (x, 1 - y), (1 - x, 1 - y)]


def _all_peers(x, y, c):
    return [((1 - x) if (r >> 2) & 1 else x, (1 - y) if (r >> 1) & 1 else y, (1 - c) if r & 1 else c)
            for r in range(1, 8)]


def _shard_slice(ref, axis, j, size, layer=None):
    idx = [slice(None)] * 3
    idx[axis] = pl.ds(pl.multiple_of(j * size, 128 if axis == 2 else 16), size)
    if layer is not None:
        idx[0] = pl.ds(layer, 1)
    return ref.at[tuple(idx)]


class LayerGather:
    def __init__(self, shards, axes, layer):
        self.nt = len(shards)
        self.axes = list(axes)
        self.layer = layer
        self.shapes = [s.shape for s in shards]
        self.dtypes = [s.dtype for s in shards]
        self.sizes = [s.shape[a] for s, a in zip(shards, axes)]
        self.split = [s.shape[1] % 32 == 0 for s in shards]
        self.half_rows = [s.shape[1] // 2 if sp else s.shape[1] for s, sp in zip(shards, self.split)]
        self.chunk_rows = []
        for s in shards:
            rt = s.shape[1]
            while rt % 32 == 0 and rt * s.shape[2] * s.dtype.itemsize > STAGE_BYTES:
                rt //= 2
            self.chunk_rows.append(rt)

    def out_shapes(self):
        out = []
        for shp, a, sz, dt in zip(self.shapes, self.axes, self.sizes, self.dtypes):
            shp = list(shp)
            shp[a] = 4 * sz
            out.append(jax.ShapeDtypeStruct(tuple(shp), dt))
        return out

    def scratch_shapes(self):
        return ([pltpu.VMEM((1, rt, shp[2]), dt) for shp, rt, dt in zip(self.shapes, self.chunk_rows, self.dtypes)]
                + [pltpu.SemaphoreType.DMA((2 * self.nt,))] + [pltpu.SemaphoreType.DMA((3 * self.nt,)) for _ in range(4)])

    def _views(self, ins, outs, scratch):
        nt = self.nt
        stage, stage_sems = scratch[:nt], scratch[nt]
        ici_send, ici_recv, d2d_send, d2d_recv = scratch[nt + 1:]
        x, y, c = _mesh_pos()
        mine = 2 * x + y
        peers = _chip_peers(x, y)
        layer = pl.ds(self.layer, 1)

        def rows(t, half, r0=0, n=None):
            hr = self.half_rows[t]
            if n is None:
                return pl.ds(pl.multiple_of(half * hr, 16), hr) if self.split[t] else pl.ds(0, hr)
            return pl.ds(r0, n)

        def placed(t, blk, row_sel, row_len):
            sz = self.sizes[t]
            if self.axes[t] == 2:
                return outs[t].at[layer, row_sel, pl.ds(pl.multiple_of(blk * sz, 128), sz)]
            return outs[t].at[layer, pl.ds(pl.multiple_of(blk * sz, 16) + row_sel.start, row_len), :]

        def ici(t, k, blk):
            px, py = peers[k]
            sel = rows(t, c)
            return pltpu.make_async_remote_copy(
                src_ref=ins[t].at[layer, sel, :], dst_ref=placed(t, blk, sel, self.half_rows[t]),
                send_sem=ici_send.at[3 * t + k], recv_sem=ici_recv.at[3 * t + k],
                device_id=(px, py, c), device_id_type=MESH_T)

        def d2d(t, k, half):
            px, py = peers[k]
            piece = placed(t, 2 * px + py, rows(t, half), self.half_rows[t])
            return pltpu.make_async_remote_copy(
                src_ref=piece, dst_ref=piece, send_sem=d2d_send.at[3 * t + k], recv_sem=d2d_recv.at[3 * t + k],
                device_id=(x, y, 1 - c), device_id_type=MESH_T)

        def own_chunk(t, r0):
            rt = self.chunk_rows[t]
            sel = pl.ds(r0, rt)
            return ins[t].at[layer, sel, :], placed(t, mine, sel, rt), stage[t], stage_sems

        return c, mine, peers, ici, d2d, own_chunk

    def start(self, ins, outs, scratch):
        c, mine, peers, ici, d2d, own_chunk = self._views(ins, outs, scratch)
        for t in range(self.nt):
            for k in range(3):
                ici(t, k, mine).start()
        starts = [list(range(0, self.shapes[t][1], self.chunk_rows[t])) for t in range(self.nt)]
        for r in range(max(len(s) for s in starts)):
            active = [(t, *own_chunk(t, starts[t][r])) for t in range(self.nt) if r < len(starts[t])]
            loads = [pltpu.make_async_copy(src, buf, sems.at[2 * t]) for t, src, dst, buf, sems in active]
            for cp in loads:
                cp.start()
            for cp in loads:
                cp.wait()
            stores = [pltpu.make_async_copy(buf, dst, sems.at[2 * t + 1]) for t, src, dst, buf, sems in active]
            for cp in stores:
                cp.start()
            for cp in stores:
                cp.wait()

    def finish(self, ins, outs, scratch):
        c, mine, peers, ici, d2d, own_chunk = self._views(ins, outs, scratch)
        for t in range(self.nt):
            for k, (px, py) in enumerate(peers):
                ici(t, k, 2 * px + py).wait_recv()
                if self.split[t]:
                    d2d(t, k, c).start()
        for t in range(self.nt):
            for k in range(3):
                if self.split[t]:
                    d2d(t, k, 1 - c).wait_recv()
        for t in range(self.nt):
            for k in range(3):
                ici(t, k, mine).wait_send()
                if self.split[t]:
                    d2d(t, k, c).wait_send()


def all_gather_layer(shards, axes, layer, *, name):
    plan = LayerGather(shards, axes, layer)
    nt = plan.nt

    def body(*refs):
        ins, outs, scratch = refs[:nt], refs[nt:2 * nt], refs[2 * nt:]
        plan.start(ins, outs, scratch)
        plan.finish(ins, outs, scratch)

    return pl.pallas_call(
        body,
        out_shape=tuple(plan.out_shapes()),
        in_specs=[ANY] * nt,
        out_specs=tuple([ANY] * nt),
        scratch_shapes=plan.scratch_shapes(),
        name=name,
    )(*shards)


class HalfLayout:
    def __init__(self, shape, axis):
        self.R, self.C = shape
        self.axis = axis
        if axis == 1:
            self.hr, self.pw = self.R // 2, self.C // 4
            self.half_shape = (self.hr, self.C)
        else:
            self.hr, self.pw = self.R // 8, self.C
            self.half_shape = (4 * self.hr, self.C)
        self.tr = _rows_tile(self.hr, self.pw)
        self.nr = self.hr // self.tr

    def in_grad(self, ref, blk, half):
        if self.axis == 1:
            return ref.at[pl.ds(pl.multiple_of(half * self.hr, 16), self.hr), pl.ds(pl.multiple_of(blk * self.pw, 128), self.pw)]
        return ref.at[pl.ds(pl.multiple_of((2 * blk + half) * self.hr, 16), self.hr), :]

    def in_half(self, ref, blk):
        if self.axis == 1:
            return ref.at[:, pl.ds(pl.multiple_of(blk * self.pw, 128), self.pw)]
        return ref.at[pl.ds(pl.multiple_of(blk * self.hr, 16), self.hr), :]

    def grad_spec(self):
        if self.axis == 1:
            return pl.BlockSpec((self.tr, self.pw), lambda j, i, s: (s[0] * self.nr + i, j))
        return pl.BlockSpec((self.tr, self.pw), lambda j, i, s: ((2 * j + s[0]) * self.nr + i, 0))

    def half_spec(self):
        if self.axis == 1:
            return pl.BlockSpec((self.tr, self.pw), lambda j, i, s: (i, j))
        return pl.BlockSpec((self.tr, self.pw), lambda j, i, s: (j * self.nr + i, 0))


def half_exchange(grads, layouts, *, name):
    nt = len(grads)
    pieces = [(t, j) for t in range(nt) for j in (range(4) if layouts[t].axis == 0 else range(1))]

    def body(*refs):
        ins, outs = refs[:nt], refs[nt:2 * nt]
        send_sems, recv_sems = refs[2 * nt:]
        x, y, c = _mesh_pos()
        cps = []
        for n, (t, j) in enumerate(pieces):
            lay = layouts[t]
            if lay.axis == 1:
                src = ins[t].at[pl.ds(pl.multiple_of((1 - c) * lay.hr, 16), lay.hr), :]
                dst = outs[t]
            else:
                src = lay.in_grad(ins[t], j, 1 - c)
                dst = lay.in_half(outs[t], j)
            cp = pltpu.make_async_remote_copy(src_ref=src, dst_ref=dst, send_sem=send_sems.at[n], recv_sem=recv_sems.at[n],
                                              device_id=(x, y, 1 - c), device_id_type=MESH_T)
            cp.start()
            cps.append(cp)
        for cp in cps:
            cp.wait_recv()
        for cp in cps:
            cp.wait_send()

    return pl.pallas_call(
        body,
        out_shape=tuple(jax.ShapeDtypeStruct(lay.half_shape, F32) for lay in layouts),
        in_specs=[ANY] * nt,
        out_specs=tuple([ANY] * nt),
        scratch_shapes=[pltpu.SemaphoreType.DMA((len(pieces),)), pltpu.SemaphoreType.DMA((len(pieces),))],
        name=name,
    )(*grads)


def pair_sum(grad, other, lay, core, *, name):
    def body(c_ref, g_ref, o_ref, s32_ref, s16_ref):
        s = g_ref[...] + o_ref[...]
        s32_ref[...] = s
        s16_ref[...] = s.astype(BF16)

    return pl.pallas_call(
        body,
        out_shape=(jax.ShapeDtypeStruct(lay.half_shape, F32), jax.ShapeDtypeStruct(lay.half_shape, BF16)),
        grid_spec=pltpu.PrefetchScalarGridSpec(
            num_scalar_prefetch=1, grid=(4, lay.nr),
            in_specs=[lay.grad_spec(), lay.half_spec()],
            out_specs=(lay.half_spec(), lay.half_spec())),
        compiler_params=_cp(("parallel", "parallel")),
        name=name,
    )(core, grad, other)


class BlockScatter:
    def __init__(self, layouts):
        self.layouts = layouts
        self.nt = len(layouts)

    def out_shapes(self):
        return [jax.ShapeDtypeStruct((3, lay.hr, lay.pw), BF16) for lay in self.layouts]

    def scratch_shapes(self):
        return [pltpu.SemaphoreType.DMA((3 * self.nt,)), pltpu.SemaphoreType.DMA((3 * self.nt,))]

    def _copies(self, pairs16, recv, scratch):
        send_sems, recv_sems = scratch
        x, y, c = _mesh_pos()
        return [pltpu.make_async_remote_copy(
            src_ref=lay.in_half(pairs16[t], 2 * px + py), dst_ref=recv[t].at[k],
            send_sem=send_sems.at[3 * t + k], recv_sem=recv_sems.at[3 * t + k],
            device_id=(px, py, c), device_id_type=MESH_T)
            for t, lay in enumerate(self.layouts) for k, (px, py) in enumerate(_chip_peers(x, y))]

    def start(self, pairs16, recv, scratch):
        for cp in self._copies(pairs16, recv, scratch):
            cp.start()

    def finish(self, pairs16, recv, scratch):
        copies = self._copies(pairs16, recv, scratch)
        for cp in copies:
            cp.wait_recv()
        for cp in copies:
            cp.wait_send()


def gather_small(small, *, name):
    def body(small_in, small_out, ssend, srecv):
        x, y, c = _mesh_pos()
        me = 4 * x + 2 * y + c
        sends, recvs = [], []
        for r, (px, py, pc) in enumerate(_all_peers(x, y, c)):
            def mk(slot, r=r, px=px, py=py, pc=pc):
                return pltpu.make_async_remote_copy(
                    src_ref=small_in, dst_ref=small_out.at[slot], send_sem=ssend.at[r], recv_sem=srecv.at[r],
                    device_id=(px, py, pc), device_id_type=MESH_T)
            snd = mk(me)
            snd.start()
            sends.append(snd)
            recvs.append(mk(4 * px + 2 * py + pc))
        for r in recvs:
            r.wait_recv()
        for s in sends:
            s.wait_send()

    return pl.pallas_call(
        body,
        out_shape=jax.ShapeDtypeStruct((8,) + small.shape, F32),
        in_specs=[ANY],
        out_specs=ANY,
        scratch_shapes=[pltpu.SemaphoreType.DMA((7,)), pltpu.SemaphoreType.DMA((7,))],
        name=name,
    )(small)


def sum_chips(pair32, recv, lay, chip, *, name):
    def body(j_ref, p_ref, r_ref, s_ref):
        acc = p_ref[...]
        for k in range(3):
            acc = acc + r_ref[k].astype(F32)
        s_ref[...] = acc

    if lay.axis == 1:
        own = pl.BlockSpec((lay.tr, lay.pw), lambda i, j: (i, j[0]))
    else:
        own = pl.BlockSpec((lay.tr, lay.pw), lambda i, j: (j[0] * lay.nr + i, 0))
    return pl.pallas_call(
        body,
        out_shape=jax.ShapeDtypeStruct((lay.hr, lay.pw), F32),
        grid_spec=pltpu.PrefetchScalarGridSpec(
            num_scalar_prefetch=1, grid=(lay.nr,),
            in_specs=[own, pl.BlockSpec((3, lay.tr, lay.pw), lambda i, j: (0, i, 0))],
            out_specs=pl.BlockSpec((lay.tr, lay.pw), lambda i, j: (i, 0))),
        compiler_params=_cp(("parallel",)),
        name=name,
    )(chip, pair32, recv)


def sum_devices(gathered, own, me, *, name):
    _, R, C = gathered.shape

    def body(me_ref, g_ref, o_ref, s_ref):
        acc = None
        for k in range(8):
            part = jnp.where(me_ref[0] == k, o_ref[...], g_ref[k])
            acc = part if acc is None else acc + part
        s_ref[...] = acc

    return pl.pallas_call(
        body,
        out_shape=jax.ShapeDtypeStruct((R, C), F32),
        grid_spec=pltpu.PrefetchScalarGridSpec(
            num_scalar_prefetch=1, grid=(1,),
            in_specs=[pl.BlockSpec((8, R, C), lambda i, m: (0, 0, 0)), pl.BlockSpec((R, C), lambda i, m: (0, 0))],
            out_specs=pl.BlockSpec((R, C), lambda i, m: (0, 0))),
        compiler_params=_cp(("arbitrary",)),
        name=name,
    )(me, gathered, own)


def sibling_swap(parts, *, name):
    nt = len(parts)

    def body(*refs):
        ins, outs = refs[:nt], refs[nt:2 * nt]
        send_sems, recv_sems = refs[2 * nt:]
        x, y, c = _mesh_pos()
        cps = []
        for t in range(nt):
            cp = pltpu.make_async_remote_copy(src_ref=ins[t], dst_ref=outs[t], send_sem=send_sems.at[t],
                                              recv_sem=recv_sems.at[t], device_id=(x, y, 1 - c), device_id_type=MESH_T)
            cp.start()
            cps.append(cp)
        for cp in cps:
            cp.wait_recv()
        for cp in cps:
            cp.wait_send()

    return pl.pallas_call(
        body,
        out_shape=tuple(jax.ShapeDtypeStruct(p.shape, p.dtype) for p in parts),
        in_specs=[ANY] * nt,
        out_specs=tuple([ANY] * nt),
        scratch_shapes=[pltpu.SemaphoreType.DMA((nt,)), pltpu.SemaphoreType.DMA((nt,))],
        name=name,
    )(*parts)


def adamw_halves(w, m, v, mine, other, lay, core, *, name):
    _, r, c = w.shape
    tr, nr = lay.tr, lay.nr
    assert (r, c) == (2 * lay.hr, lay.pw), (w.shape, lay.hr, lay.pw)

    def body(c_ref, w_ref, m_ref, v_ref, *rest):
        g_refs, outs = rest[:2 * DEPTH], rest[2 * DEPTH:]
        l, h = pl.program_id(0), pl.program_id(1)
        g = None
        for d in range(DEPTH):
            gd = jnp.where(h == c_ref[0], g_refs[d][...], g_refs[DEPTH + d][...])
            g = gd if g is None else jnp.where(l == d, gd, g)
        _adamw_math(g, w_ref, m_ref, v_ref, *outs)

    full = pl.BlockSpec((None, tr, c), lambda l, h, i, s: (l, h * nr + i, 0))

    def part(d, is_mine):
        def index(l, h, i, s):
            used = jnp.logical_and(l == d, (h == s[0]) == is_mine)
            return jnp.where(used, i, 0), 0
        return pl.BlockSpec((tr, c), index)

    shp = jax.ShapeDtypeStruct((DEPTH, r, c), F32)
    return pl.pallas_call(
        body,
        out_shape=(shp, shp, shp, shp),
        grid_spec=pltpu.PrefetchScalarGridSpec(
            num_scalar_prefetch=1, grid=(DEPTH, 2, nr),
            in_specs=[full, full, full] + [part(d, True) for d in range(DEPTH)] + [part(d, False) for d in range(DEPTH)],
            out_specs=(full, full, full, full)),
        compiler_params=_cp(("arbitrary", "arbitrary", "arbitrary")),
        name=name,
    )(core, w, m, v, *mine, *other)


WEIGHTS = ("norm_mix", "w_in", "b_gate", "q_norm_a", "k_norm_a", "rel_bias_a", "w_pool", "pool_scale",
           "w_branch_a", "w_branch_b", "w_branch_c", "w_out", "norm_ffn", "w_up", "conv_w", "conv_b", "w_down")
SHARDED = {"w_in": 2, "w_branch_a": 2, "w_branch_b": 2, "w_branch_c": 2, "w_out": 1, "w_up": 2, "conv_w": 2,
           "w_down": 1}
REPLICATED = tuple(n for n in WEIGHTS if n not in SHARDED)
MATMUL_WEIGHTS = tuple(n for n in SHARDED if n != "conv_w")
SMALL_WEIGHTS = tuple(n for n in WEIGHTS if n not in MATMUL_WEIGHTS)
SMALL_ROWS = 1496


def _layer_fwd(x, p, full, tables, rest=None, prefetch=None):
    l = p["l"]
    diag = exact_dot(p["rel_bias_a"], tables["onehot_t"], name="bias_diagonals")
    diag = diag.reshape(N_HEADS, N_VARIANTS, 1, DIAG_W).transpose(1, 0, 2, 3)
    biasm = bias_expand(diag, name="bias_expand")
    gq8 = jnp.tile(p["q_norm_a"], N_HEADS)[None]
    gk8 = jnp.tile(p["k_norm_a"], N_HEADS)[None]
    h = rmsnorm_fwd(x, p["norm_mix"][None], name="rmsnorm_fwd")
    if rest is None:
        proj = matmul(h, full["w_in"], b_layer=l, name="mm_in")
    else:
        plan, shards, names = rest
        proj, gathered = matmul(h, full["w_in"], b_layer=l, behind=(plan, shards), name="mm_in_gather")
        full = {**full, **dict(zip(names, gathered))}
    qa, ka, va, qb, kb, vb = qkv_prep(proj, gq8, gk8, name="qkv_prep")
    oa = attn_a_fwd(qa, ka, va, biasm, name="attn_a_fwd")
    if prefetch is None:
        ob, tot, nblk = attn_b_fwd(qb, kb, vb, name="attn_b_fwd")
    else:
        plan, shards, names = prefetch
        ob, tot, nblk, filled = attn_b_fwd(qb, kb, vb, gather=(plan, shards, [full[n] for n in names]),
                                           name="attn_b_fwd_gather")
        full = dict(zip(names, filled))
    wpool = p["w_pool"].astype(BF16)
    oc = pool_fwd(proj, wpool, p["pool_scale"][None], name="pool_fwd")
    merged = merge_fwd(oa, ob, oc, proj, p["b_gate"][None], full["w_branch_a"][l], full["w_branch_b"][l],
                       full["w_branch_c"][l], name="merge_fwd")
    x1 = matmul(merged, full["w_out"], b_layer=l, add=x, name="mm_out")
    h2 = rmsnorm_fwd(x1, p["norm_ffn"][None], name="rmsnorm_fwd")
    u = matmul(h2, full["w_up"], b_layer=l, name="mm_up")
    a = conv_glu_fwd(u, full["conv_w"][l], p["conv_b"][None], name="conv_glu_fwd")
    x2 = matmul(a, full["w_down"], b_layer=l, add=x1, name="mm_down")
    saved = dict(x=x, h=h, proj=proj, qa=qa, ka=ka, va=va, qb=qb, kb=kb, vb=vb, oa=oa, ob=ob, tot=tot, nblk=nblk, oc=oc,
                 merged=merged, x1=x1, h2=h2, u=u, a=a, biasm=biasm, gq8=gq8, gk8=gk8, wpool=wpool)
    return x2, saved, full


class GradReducer:
    def __init__(self, layouts, core, chip):
        self.layouts, self.core, self.chip = layouts, core, chip
        self.pairs32, self.received, self.pending = {}, {}, []

    def prepare(self, layer, grads):
        names = list(grads)
        lays = [self.layouts[n] for n in names]
        others = half_exchange([grads[n] for n in names], lays, name="half_exchange")
        for n, lay, other in zip(names, lays, others):
            p32, p16 = pair_sum(grads[n], other, lay, self.core, name="pair_sum")
            self.pairs32[(layer, n)] = p32
            self.pending.append(((layer, n), p16))

    def take(self):
        keys = [k for k, _ in self.pending]
        pairs16 = [p for _, p in self.pending]
        self.pending = []
        return keys, BlockScatter([self.layouts[n] for _, n in keys]), pairs16

    def store(self, keys, received):
        self.received.update(zip(keys, received))

    def finish(self):
        keys = list(self.pairs32)
        mine = [sum_chips(self.pairs32[k], self.received[k], self.layouts[k[1]], self.chip, name="sum_chips") for k in keys]
        other = sibling_swap(mine, name="sibling_swap")
        return dict(zip(keys, mine)), dict(zip(keys, other))


EARLY_WEIGHTS = ("w_down", "w_up", "w_out", "w_branch_a", "w_branch_b", "w_branch_c")


def _layer_bwd(dx2, s, p, tables, reducer):
    g = {}
    full, l = p["full"], p["l"]
    da = matmul(dx2, full["w_down"], b_layer=l, tb=True, name="mm_down_dx")
    g["w_down"] = matmul(s["a"], dx2, ta=True, name="mm_down_dw")
    dug, duv, dcwg, dcwv, dcbg, dcbv = conv_glu_bwd(s["u"], p["conv_w"], p["conv_b"][None], da, name="conv_glu_bwd")
    du = jnp.concatenate([dug, duv], axis=1)
    g["conv_w"] = jnp.concatenate([dcwg, dcwv], axis=1)
    g["conv_b"] = jnp.concatenate([dcbg, dcbv], axis=1)[0]
    g["w_up"] = matmul(s["h2"], du, ta=True, name="mm_up_dw")
    dx1, dg2 = matmul(du, full["w_up"], b_layer=l, tb=True, norm_bwd=(s["x1"], p["norm_ffn"][None], dx2),
                      name="mm_up_dx_norm")
    g["norm_ffn"] = dg2[0]
    dmerged = matmul(dx1, full["w_out"], b_layer=l, tb=True, name="mm_out_dx")
    g["w_out"] = matmul(s["merged"], dx1, ta=True, name="mm_out_dw")
    t_a, t_b, t_c, dga, dgb, dgc, dba, dbb, dbc = merge_bwd(
        dmerged, s["oa"], s["ob"], s["oc"], s["proj"], p["b_gate"][None], p["w_branch_a"], p["w_branch_b"],
        p["w_branch_c"], name="merge_bwd")
    g["b_gate"] = jnp.concatenate([dba, dbb, dbc], axis=1)[0]
    g["w_branch_a"] = matmul(s["oa"], t_a, ta=True, name="mm_branch_dw")
    g["w_branch_b"] = matmul(s["ob"], t_b, ta=True, name="mm_branch_dw")
    g["w_branch_c"] = matmul(s["oc"], t_c, ta=True, name="mm_branch_dw")
    doa = matmul(t_a, full["w_branch_a"], b_layer=l, tb=True, out_dtype=BF16, name="mm_branch_dx")
    dob = matmul(t_b, full["w_branch_b"], b_layer=l, tb=True, out_dtype=BF16, name="mm_branch_dx")
    doc = matmul(t_c, full["w_branch_c"], b_layer=l, tb=True, name="mm_branch_dx_f32")
    dqh, dkh, dva, dbias = attn_a_bwd(s["qa"], s["ka"], s["va"], s["biasm"], doa, name="attn_a_bwd")
    ddiag = relbias_reduce(dbias, name="relbias_reduce")
    ddiag = ddiag.transpose(1, 0, 2, 3).reshape(N_HEADS, N_VARIANTS * DIAG_W)
    g["rel_bias_a"] = exact_dot(ddiag, tables["onehot"], name="relbias_table")
    dqa, dka, dgq8, dgk8 = qknorm_bwd(s["proj"], s["gq8"], s["gk8"], dqh, dkh, name="qknorm_bwd")
    g["q_norm_a"] = dgq8.reshape(N_HEADS, HEAD_DIM).sum(axis=0)
    g["k_norm_a"] = dgk8.reshape(N_HEADS, HEAD_DIM).sum(axis=0)
    reducer.prepare(l, {n: g[n] for n in EARLY_WEIGHTS})
    keys, plan, pairs16 = reducer.take()
    dqb, dkb, dvb, received = attn_b_bwd(s["qb"], s["kb"], s["vb"], s["tot"], s["nblk"], dob, scatter=(plan, pairs16),
                                         name="attn_b_bwd_scatter")
    reducer.store(keys, received)
    duc, dwp, dsc = pool_bwd(s["proj"], s["wpool"], p["pool_scale"][None], doc, name="pool_bwd")
    g["w_pool"] = dwp
    g["pool_scale"] = dsc[0]
    dproj = jnp.concatenate([dqa, dka, dva.astype(BF16), dqb, dkb.astype(BF16), dvb.astype(BF16), duc,
                             dga, dgb, dgc], axis=1)
    g["w_in"] = matmul(s["h"], dproj, ta=True, name="mm_in_dw")
    reducer.prepare(l, {"w_in": g["w_in"]})
    norm = (s["x"], p["norm_mix"][None], dx1)
    if l > 0:
        dx, dg1 = matmul(dproj, full["w_in"], b_layer=l, tb=True, norm_bwd=norm, name="mm_in_dx_norm")
    else:
        keys, plan, pairs16 = reducer.take()
        dx, dg1, received = matmul(dproj, full["w_in"], b_layer=l, tb=True, norm_bwd=norm, behind=(plan, pairs16),
                                   name="mm_in_dx_norm_scatter")
        reducer.store(keys, received)
    g["norm_mix"] = dg1[0]
    return dx, g


def kernel(x, norm_mix, w_in, b_gate, q_norm_a, k_norm_a, rel_bias_a, w_pool, pool_scale, w_branch_a, w_branch_b, w_branch_c, w_out, norm_ffn, w_up, conv_w, conv_b, w_down, loss_target, m_norm_mix, m_w_in, m_b_gate, m_q_norm_a, m_k_norm_a, m_rel_bias_a, m_w_pool, m_pool_scale, m_w_branch_a, m_w_branch_b, m_w_branch_c, m_w_out, m_norm_ffn, m_w_up, m_conv_w, m_conv_b, m_w_down, v_norm_mix, v_w_in, v_b_gate, v_q_norm_a, v_k_norm_a, v_rel_bias_a, v_w_pool, v_pool_scale, v_w_branch_a, v_w_branch_b, v_w_branch_c, v_w_out, v_norm_ffn, v_w_up, v_conv_w, v_conv_b, v_w_down):
    w = dict(zip(WEIGHTS, (norm_mix, w_in, b_gate, q_norm_a, k_norm_a, rel_bias_a, w_pool, pool_scale, w_branch_a,
                           w_branch_b, w_branch_c, w_out, norm_ffn, w_up, conv_w, conv_b, w_down)))
    m = dict(zip(WEIGHTS, (m_norm_mix, m_w_in, m_b_gate, m_q_norm_a, m_k_norm_a, m_rel_bias_a, m_w_pool, m_pool_scale,
                           m_w_branch_a, m_w_branch_b, m_w_branch_c, m_w_out, m_norm_ffn, m_w_up, m_conv_w, m_conv_b,
                           m_w_down)))
    v = dict(zip(WEIGHTS, (v_norm_mix, v_w_in, v_b_gate, v_q_norm_a, v_k_norm_a, v_rel_bias_a, v_w_pool, v_pool_scale,
                           v_w_branch_a, v_w_branch_b, v_w_branch_c, v_w_out, v_norm_ffn, v_w_up, v_conv_w, v_conv_b,
                           v_w_down)))
    onehot = diagonal_onehot()
    tables = dict(onehot=jnp.asarray(onehot), onehot_t=jnp.asarray(np.ascontiguousarray(onehot.T)))

    names = tuple(SHARDED)
    shards = [w[n] if n == "conv_w" else w[n].astype(BF16) for n in names]
    axes = [SHARDED[n] for n in names]
    later = [i for i, n in enumerate(names) if n != "w_in"]
    rest = (LayerGather([shards[i] for i in later], [axes[i] for i in later], 0), [shards[i] for i in later],
            [names[i] for i in later])
    first = names.index("w_in")
    full = {"w_in": all_gather_layer([shards[first]], [axes[first]], 0, name="all_gather_layer")[0]}

    def layer_params(l):
        p = {n: full[n][l] for n in ("w_branch_a", "w_branch_b", "w_branch_c", "conv_w")}
        p.update({n: w[n][l] for n in REPLICATED})
        p.update(full=full, l=l)
        return p

    xs = x[0]
    saved = []
    for l in range(DEPTH):
        prefetch = (LayerGather(shards, axes, l + 1), shards, names) if l + 1 < DEPTH else None
        replicated = {n: w[n][l] for n in REPLICATED}
        xs, s, full = _layer_fwd(xs, dict(replicated, l=l), full, tables, rest if l == 0 else None, prefetch)
        saved.append(s)
    dx, lpart = loss_head(xs, loss_target[0], name="loss_head")
    loss = lax.psum(lpart[0, 0], MESH_AXES)
    as_index = lambda i: jnp.reshape(i, (1,)).astype(jnp.int32)
    cx, cy, cc = _mesh_pos()
    core, chip, me = as_index(cc), as_index(2 * cx + cy), as_index(4 * cx + 2 * cy + cc)
    layouts = {n: HalfLayout((full[n].shape[1], full[n].shape[2]), SHARDED[n] - 1) for n in MATMUL_WEIGHTS}
    reducer = GradReducer(layouts, core, chip)
    grads = [None] * DEPTH
    for l in reversed(range(DEPTH)):
        dx, grads[l] = _layer_bwd(dx, saved[l], layer_params(l), tables, reducer)

    g = {n: jnp.stack([grads[l][n] for l in range(DEPTH)]) for n in SMALL_WEIGHTS}
    flat = jnp.concatenate([g[n].reshape(-1) for n in SMALL_WEIGHTS])
    small = jnp.pad(flat, (0, SMALL_ROWS * 128 - flat.shape[0])).reshape(SMALL_ROWS, 128)
    small_sum = sum_devices(gather_small(small, name="gather_small"), small, me, name="sum_devices").reshape(-1)
    mine, other = reducer.finish()

    res = {}
    for n in MATMUL_WEIGHTS:
        res[n] = adamw_halves(w[n], m[n], v[n], [mine[(l, n)] for l in range(DEPTH)],
                              [other[(l, n)] for l in range(DEPTH)], layouts[n], core, name="adamw_halves")
    off = 0
    for n in SMALL_WEIGHTS:
        shp = g[n].shape
        size = int(np.prod(shp))
        gn = small_sum[off:off + size].reshape(shp)
        off += size
        if n in SHARDED:
            gn = lax.dynamic_slice_in_dim(gn, (2 * cx + cy) * w[n].shape[-1], w[n].shape[-1], axis=len(shp) - 1)
        shp = w[n].shape
        cols = shp[-1]
        two_d = lambda t: t.reshape(int(np.prod(shp)) // cols, cols)
        res[n] = [t.reshape(shp) for t in adamw(two_d(w[n]), two_d(m[n]), two_d(v[n]), two_d(gn), name="adamw")]

    out = [loss, dx[None]]
    for k in range(4):
        out.extend(res[n][k] for n in WEIGHTS)
    return tuple(out)
```

```python
import jax
import jax.numpy as jnp
import numpy as np
from jax import lax
from jax.experimental import pallas as pl
from jax.experimental.pallas import tpu as pltpu

F32 = jnp.float32
BF16 = jnp.bfloat16

D_MODEL = 1024
DEPTH = 2
CHUNK = 64
N_LEFT = 8
HEAD_DIM = 64
N_HEADS = 8
WIDTH = 512
POOL_WINDOWS = (2, 4, 8, 16)
GROUP_DIM = 128
MAX_REL = 2 * CHUNK
REL_TABLE = MAX_REL + CHUNK
D_FF = 2816
EPS = 1e-6
QK_SCALE = 0.125
GATE_COL0 = 7 * WIDTH

ADAM_LR = 0.001
ADAM_B1 = 0.9
ADAM_B2 = 0.999
ADAM_EPS = 1e-08
ADAM_WD = 0.01
ADAM_STEP = 10

VMEM_LIMIT = 56 * 1024 * 1024
ATT_Q = 256
A_Q = 256
A_WIN = A_Q + N_LEFT * CHUNK
HALO = 16
CONV_HALO = 8
NEG = -1e30

MESH_AXES = ("x", "y", "c")
MESH_T = pl.DeviceIdType.MESH


def _cp(sem=None, vmem=VMEM_LIMIT):
    return pltpu.CompilerParams(dimension_semantics=sem, vmem_limit_bytes=vmem)


def _dot(a, b, ca, cb):
    return lax.dot_general(a, b, (((ca,), (cb,)), ((), ())), preferred_element_type=F32)


def _tile(n, cands=(512, 256, 128)):
    for c in cands:
        if n % c == 0:
            return c
    return n


def _split_hi_lo(v):
    hi = v.astype(BF16)
    lo = (v - hi.astype(F32)).astype(BF16)
    return hi, lo


def matmul(a, b, *, ta=False, tb=False, add=None, norm_bwd=None, out_dtype=F32, b_layer=None, behind=None, name):
    plan, sources = behind if behind is not None else (None, [])
    n_src = len(sources)
    n_dst = len(plan.out_shapes()) if plan is not None else 0
    assert add is None or norm_bwd is None
    if ta:
        K, M = a.shape
    else:
        M, K = a.shape
    if tb:
        N, K2 = b.shape[-2:]
    else:
        K2, N = b.shape[-2:]
    assert K == K2, (a.shape, b.shape, ta, tb)
    big = (1024, 1408, 512, 256, 128)
    tm = _tile(M, big)
    tn = _tile(N, (1664,) + big)
    tk = _tile(K, big if ta else (1664, 1408) + big if norm_bwd is not None else (3328, 2816) + big)
    nk = K // tk

    n_extra = 1 if add is not None else 3 if norm_bwd is not None else 0
    n_in = 2 + n_extra + n_src
    n_out = 2 if norm_bwd is not None else 1
    grid = (M // tm, N // tn, nk)
    assert norm_bwd is None or tn == N, "the norm gradient needs whole rows in one output tile"

    def body(*refs):
        a_ref, b_ref = refs[:2]
        extra = refs[2:2 + n_extra]
        o_ref, acc = refs[n_in], refs[n_in + n_out + n_dst]
        i, j, k = pl.program_id(0), pl.program_id(1), pl.program_id(2)
        if plan is not None:
            comm = (refs[n_in - n_src:n_in], refs[n_in + n_out:n_in + n_out + n_dst], refs[n_in + n_out + n_dst + 1:])

            @pl.when((i == 0) & (j == 0) & (k == 0))
            def _():
                plan.start(*comm)

        @pl.when(k == 0)
        def _():
            acc[...] = jnp.zeros_like(acc)

        av = a_ref[...].astype(BF16)
        bv = b_ref[...].astype(BF16)
        acc[...] += _dot(av, bv, 0 if ta else 1, 1 if tb else 0)

        @pl.when(k == nk - 1)
        def _():
            r = acc[...]
            if add is not None:
                r = r + extra[0][...].astype(F32)
            if norm_bwd is not None:
                x_ref, g_ref, dres_ref = extra
                dg_ref = refs[n_in + 1]
                xv = x_ref[...]
                inv = lax.rsqrt(jnp.mean(xv * xv, axis=-1, keepdims=True) + EPS)
                gd = r * g_ref[...]
                mean = jnp.mean(xv * gd, axis=-1, keepdims=True)

                @pl.when(i == 0)
                def _():
                    dg_ref[...] = jnp.zeros_like(dg_ref)

                dg_ref[...] += jnp.sum(r * xv * inv, axis=0, keepdims=True)
                r = dres_ref[...] + inv * gd - xv * (inv * inv * inv * mean)
            o_ref[...] = r.astype(out_dtype)

        if plan is not None:
            @pl.when((i == grid[0] - 1) & (j == grid[1] - 1) & (k == nk - 1))
            def _():
                plan.finish(*comm)

    a_spec = pl.BlockSpec((tk, tm), lambda i, j, k: (k, i)) if ta else pl.BlockSpec((tm, tk), lambda i, j, k: (i, k))
    if b_layer is None:
        b_spec = pl.BlockSpec((tn, tk), lambda i, j, k: (j, k)) if tb else pl.BlockSpec((tk, tn), lambda i, j, k: (k, j))
    elif tb:
        b_spec = pl.BlockSpec((None, tn, tk), lambda i, j, k: (b_layer, j, k))
    else:
        b_spec = pl.BlockSpec((None, tk, tn), lambda i, j, k: (b_layer, k, j))
    o_spec = pl.BlockSpec((tm, tn), lambda i, j, k: (i, j))
    vec_spec = pl.BlockSpec((1, tn), lambda i, j, k: (0, j))
    in_specs = [a_spec, b_spec]
    args = [a, b]
    out_shape = [jax.ShapeDtypeStruct((M, N), out_dtype)]
    out_specs = [o_spec]
    if add is not None:
        in_specs.append(o_spec)
        args.append(add)
    if norm_bwd is not None:
        in_specs += [o_spec, vec_spec, o_spec]
        args += list(norm_bwd)
        out_shape.append(jax.ShapeDtypeStruct((1, N), F32))
        out_specs.append(vec_spec)
    sequential = plan is not None or norm_bwd is not None
    any_space = pl.BlockSpec(memory_space=pl.ANY)
    outs = pl.pallas_call(
        body,
        out_shape=tuple(out_shape) + (tuple(plan.out_shapes()) if plan is not None else ()),
        grid=grid,
        in_specs=in_specs + [any_space] * n_src,
        out_specs=tuple(out_specs) + tuple([any_space] * n_dst),
        scratch_shapes=[pltpu.VMEM((tm, tn), F32)] + (list(plan.scratch_shapes()) if plan is not None else []),
        compiler_params=_cp(("arbitrary" if sequential else "parallel",) * 2 + ("arbitrary",)),
        name=name,
    )(*args, *sources)
    if plan is None:
        return outs[0] if n_out == 1 else tuple(outs)
    return tuple(outs[:n_out]) + (list(outs[n_out:]),) if n_out > 1 else (outs[0], list(outs[1:]))


def rmsnorm_fwd(x, g, *, name):
    S, D = x.shape
    T = _tile(S)

    def body(x_ref, g_ref, h_ref):
        xv = x_ref[...]
        r = lax.rsqrt(jnp.mean(xv * xv, axis=-1, keepdims=True) + EPS)
        h_ref[...] = (xv * r * g_ref[...]).astype(BF16)

    return pl.pallas_call(
        body,
        out_shape=jax.ShapeDtypeStruct((S, D), BF16),
        grid=(S // T,),
        in_specs=[pl.BlockSpec((T, D), lambda i: (i, 0)), pl.BlockSpec((1, D), lambda i: (0, 0))],
        out_specs=pl.BlockSpec((T, D), lambda i: (i, 0)),
        compiler_params=_cp(("parallel",)),
        name=name,
    )(x, g)


def _head_mean_matrix():
    r = lax.broadcasted_iota(jnp.int32, (WIDTH, WIDTH), 0) // HEAD_DIM
    c = lax.broadcasted_iota(jnp.int32, (WIDTH, WIDTH), 1) // HEAD_DIM
    return jnp.where(r == c, 1.0 / HEAD_DIM, 0.0).astype(BF16)


def _head_mean(v, mm):
    hi, lo = _split_hi_lo(v)
    return _dot(hi, mm, 1, 0) + _dot(lo, mm, 1, 0)


def qkv_prep(proj, gq, gk, *, name):
    S = proj.shape[0]
    T = _tile(S)

    def body(qa, ka, va, qb, kb, vb, gq_ref, gk_ref, oqa, oka, ova, oqb, okb, ovb):
        mm = _head_mean_matrix()
        for src, gref, dst, scale in ((qa, gq_ref, oqa, QK_SCALE), (ka, gk_ref, oka, 1.0)):
            v = src[...]
            r = lax.rsqrt(_head_mean(v * v, mm) + EPS)
            dst[...] = (v * r * gref[...] * scale).astype(BF16)
        oqb[...] = (qb[...] * QK_SCALE).astype(BF16)
        for src, dst in ((va, ova), (kb, okb), (vb, ovb)):
            dst[...] = src[...].astype(BF16)

    col = lambda j: pl.BlockSpec((T, WIDTH), lambda i, j=j: (i, j))
    vec = pl.BlockSpec((1, WIDTH), lambda i: (0, 0))
    out = pl.BlockSpec((T, WIDTH), lambda i: (i, 0))
    return pl.pallas_call(
        body,
        out_shape=tuple(jax.ShapeDtypeStruct((S, WIDTH), BF16) for _ in range(6)),
        grid=(S // T,),
        in_specs=[col(0), col(1), col(2), col(3), col(4), col(5), vec, vec],
        out_specs=tuple(out for _ in range(6)),
        compiler_params=_cp(("parallel",)),
        name=name,
    )(proj, proj, proj, proj, proj, proj, gq, gk)


def qknorm_bwd(proj, gq, gk, dqh, dkh, *, name):
    S = proj.shape[0]
    T = _tile(S)

    def body(qa, ka, gq_ref, gk_ref, dq_ref, dk_ref, oq, ok, ogq, ogk):
        i = pl.program_id(0)
        mm = _head_mean_matrix()

        @pl.when(i == 0)
        def _():
            ogq[...] = jnp.zeros_like(ogq)
            ogk[...] = jnp.zeros_like(ogk)

        for src, gref, dref, dst, gdst in ((qa, gq_ref, dq_ref, oq, ogq), (ka, gk_ref, dk_ref, ok, ogk)):
            v = src[...]
            dy = dref[...]
            r = lax.rsqrt(_head_mean(v * v, mm) + EPS)
            gd = dy * gref[...]
            m = _head_mean(v * gd, mm)
            dst[...] = (r * gd - v * (r * r * r * m)).astype(BF16)
            gdst[...] += jnp.sum(dy * v * r, axis=0, keepdims=True)

    col = lambda j: pl.BlockSpec((T, WIDTH), lambda i, j=j: (i, j))
    vec = pl.BlockSpec((1, WIDTH), lambda i: (0, 0))
    row = pl.BlockSpec((T, WIDTH), lambda i: (i, 0))
    return pl.pallas_call(
        body,
        out_shape=(jax.ShapeDtypeStruct((S, WIDTH), BF16), jax.ShapeDtypeStruct((S, WIDTH), BF16),
                   jax.ShapeDtypeStruct((1, WIDTH), F32), jax.ShapeDtypeStruct((1, WIDTH), F32)),
        grid=(S // T,),
        in_specs=[col(0), col(1), vec, vec, row, row],
        out_specs=(row, row, vec, vec),
        compiler_params=_cp(("arbitrary",)),
        name=name,
    )(proj, proj, gq, gk, dqh, dkh)


DIAG_W = 1024
N_VARIANTS = N_LEFT * CHUNK // A_Q + 1


def diagonal_onehot():
    jj = np.arange(DIAG_W)
    diff = np.where(jj < A_WIN, jj, jj - DIAG_W)
    out = np.zeros((N_VARIANTS, DIAG_W, REL_TABLE), np.float32)
    for v in range(N_VARIANTS):
        rel = np.clip(A_Q * v - diff, -(CHUNK - 1), MAX_REL) + (CHUNK - 1)
        out[v, jj, rel] = 1.0
    return out.reshape(N_VARIANTS * DIAG_W, REL_TABLE)


def exact_dot(a, b, *, name):
    def body(a_ref, b_ref, o_ref):
        o_ref[...] = jnp.dot(a_ref[...], b_ref[...], precision=lax.Precision.HIGHEST, preferred_element_type=F32)

    return pl.pallas_call(body, out_shape=jax.ShapeDtypeStruct((a.shape[0], b.shape[1]), F32),
                          compiler_params=_cp(), name=name)(a, b)


def _band_valid(v):
    qc = (lax.broadcasted_iota(jnp.int32, (A_Q, A_WIN), 0) + A_Q * v) // CHUNK
    kc = lax.broadcasted_iota(jnp.int32, (A_Q, A_WIN), 1) // CHUNK
    return (kc <= qc) & (kc >= qc - N_LEFT)


def bias_expand(diag, *, name):
    def body(d_ref, o_ref):
        rows = jnp.broadcast_to(d_ref[0, 0], (A_Q, DIAG_W))
        skew = pltpu.roll(rows, 0, 1, stride=1, stride_axis=0)
        o_ref[0, 0] = jnp.where(_band_valid(pl.program_id(0)), skew[:, :A_WIN], NEG)

    return pl.pallas_call(
        body,
        out_shape=jax.ShapeDtypeStruct((N_VARIANTS, N_HEADS, A_Q, A_WIN), F32),
        grid=(N_VARIANTS, N_HEADS),
        in_specs=[pl.BlockSpec((1, 1, 1, DIAG_W), lambda v, h: (v, h, 0, 0))],
        out_specs=pl.BlockSpec((1, 1, A_Q, A_WIN), lambda v, h: (v, h, 0, 0)),
        compiler_params=_cp(("parallel", "parallel")),
        name=name,
    )(diag)


def relbias_reduce(dbias, *, name):
    def body(db_ref, o_ref):
        acc = None
        for a in range(A_Q // 8):
            x = jnp.concatenate([db_ref[0, 0, 8 * a:8 * a + 8, :], jnp.zeros((8, DIAG_W - A_WIN), F32)], axis=1)
            x = pltpu.roll(x, DIAG_W - 8 * a, 1) if a else x
            acc = x if acc is None else acc + x
        row = lax.broadcasted_iota(jnp.int32, (8, DIAG_W), 0)
        for b in range(3):
            acc = jnp.where((row >> b) & 1 == 1, pltpu.roll(acc, DIAG_W - (1 << b), 1), acc)
        o_ref[0, 0] = jnp.sum(acc, axis=0, keepdims=True)

    return pl.pallas_call(
        body,
        out_shape=jax.ShapeDtypeStruct((N_VARIANTS, N_HEADS, 1, DIAG_W), F32),
        grid=(N_VARIANTS, N_HEADS),
        in_specs=[pl.BlockSpec((1, 1, A_Q, A_WIN), lambda v, h: (v, h, 0, 0))],
        out_specs=pl.BlockSpec((1, 1, 1, DIAG_W), lambda v, h: (v, h, 0, 0)),
        compiler_params=_cp(("parallel", "parallel")),
        name=name,
    )(dbias)


def _a_window_start(qb):
    return pl.multiple_of(jnp.maximum(qb * A_Q - N_LEFT * CHUNK, 0), A_Q)


def attn_a_fwd(q, k, v, biasm, *, name):
    S = q.shape[0]
    nq = S // A_Q

    def body(q_ref, k_ref, v_ref, b_ref, o_ref):
        qb = pl.program_id(1)
        start = _a_window_start(qb)
        outs = []
        for h in range(2):
            lanes = slice(h * HEAD_DIM, (h + 1) * HEAD_DIM)
            qh = q_ref[:, lanes]
            kw = k_ref[pl.ds(start, A_WIN), lanes]
            vw = v_ref[pl.ds(start, A_WIN), lanes]
            s = _dot(qh, kw, 1, 1) + b_ref[0, h]
            m = jnp.max(s, axis=-1, keepdims=True)
            e = jnp.exp(s - m)
            outs.append(_dot(e.astype(BF16), vw, 1, 0) * (1.0 / jnp.sum(e, axis=-1, keepdims=True)))
        o_ref[...] = jnp.concatenate(outs, axis=1).astype(BF16)

    qspec = pl.BlockSpec((A_Q, 2 * HEAD_DIM), lambda hp, qb: (qb, hp))
    kvspec = pl.BlockSpec((S, 2 * HEAD_DIM), lambda hp, qb: (0, hp))
    bspec = pl.BlockSpec((1, 2, A_Q, A_WIN), lambda hp, qb: (jnp.minimum(qb, N_VARIANTS - 1), hp, 0, 0))
    return pl.pallas_call(
        body,
        out_shape=jax.ShapeDtypeStruct((S, WIDTH), BF16),
        grid=(N_HEADS // 2, nq),
        in_specs=[qspec, kvspec, kvspec, bspec],
        out_specs=qspec,
        compiler_params=_cp(("parallel", "arbitrary")),
        name=name,
    )(q, k, v, biasm)


def attn_a_bwd(q, k, v, biasm, do, *, name):
    S = q.shape[0]
    nq = S // A_Q

    def body(q_ref, k_ref, v_ref, b_ref, do_ref, dq_ref, dk_ref, dv_ref, db_ref):
        qb = pl.program_id(1)
        start = _a_window_start(qb)

        @pl.when(qb == 0)
        def _():
            dk_ref[...] = jnp.zeros_like(dk_ref)
            dv_ref[...] = jnp.zeros_like(dv_ref)

        @pl.when(qb < N_VARIANTS)
        def _():
            db_ref[...] = jnp.zeros_like(db_ref)

        dqs = []
        for h in range(2):
            lanes = slice(h * HEAD_DIM, (h + 1) * HEAD_DIM)
            qh = q_ref[:, lanes]
            doh = do_ref[:, lanes]
            kw = k_ref[pl.ds(start, A_WIN), lanes]
            vw = v_ref[pl.ds(start, A_WIN), lanes]
            s = _dot(qh, kw, 1, 1) + b_ref[0, h]
            m = jnp.max(s, axis=-1, keepdims=True)
            e = jnp.exp(s - m)
            p = e * (1.0 / jnp.sum(e, axis=-1, keepdims=True))
            dp = _dot(doh, vw, 1, 1)
            delta = jnp.sum(p * dp, axis=-1, keepdims=True)
            ds = p * (dp - delta)
            db_ref[0, h] += ds
            dsb = ds.astype(BF16)
            dqs.append(_dot(dsb, kw, 1, 0) * QK_SCALE)
            dk_ref[pl.ds(start, A_WIN), lanes] += _dot(dsb, qh, 0, 0)
            dv_ref[pl.ds(start, A_WIN), lanes] += _dot(p.astype(BF16), doh, 0, 0)
        dq_ref[...] = jnp.concatenate(dqs, axis=1)

    qspec = pl.BlockSpec((A_Q, 2 * HEAD_DIM), lambda hp, qb: (qb, hp))
    kvspec = pl.BlockSpec((S, 2 * HEAD_DIM), lambda hp, qb: (0, hp))
    bspec = pl.BlockSpec((1, 2, A_Q, A_WIN), lambda hp, qb: (jnp.minimum(qb, N_VARIANTS - 1), hp, 0, 0))
    return pl.pallas_call(
        body,
        out_shape=(jax.ShapeDtypeStruct((S, WIDTH), F32), jax.ShapeDtypeStruct((S, WIDTH), F32),
                   jax.ShapeDtypeStruct((S, WIDTH), F32), jax.ShapeDtypeStruct((N_VARIANTS, N_HEADS, A_Q, A_WIN), F32)),
        grid=(N_HEADS // 2, nq),
        in_specs=[qspec, kvspec, kvspec, bspec, qspec],
        out_specs=(qspec, kvspec, kvspec, bspec),
        compiler_params=_cp(("parallel", "arbitrary")),
        name=name,
    )(q, k, v, biasm, do)


def _tri(kind):
    j = lax.broadcasted_iota(jnp.int32, (ATT_Q, ATT_Q), 0)
    s = lax.broadcasted_iota(jnp.int32, (ATT_Q, ATT_Q), 1)
    if kind == "gt":
        m = j > s
    elif kind == "le":
        m = j <= s
    else:
        m = j < s
    return jnp.where(m, 1.0, 0.0).astype(BF16)


def _cum(v, tri):
    hi, lo = _split_hi_lo(v)
    return _dot(hi, tri, 1, 0) + _dot(lo, tri, 1, 0)


def _log_sigmoids(z):
    t = jnp.log(1.0 + jnp.exp(-jnp.abs(z)))
    return -(jnp.maximum(z, 0.0) + t), jnp.minimum(z, 0.0) - t


def _not_before_bias():
    row = lax.broadcasted_iota(jnp.int32, (ATT_Q, ATT_Q), 0)
    col = lax.broadcasted_iota(jnp.int32, (ATT_Q, ATT_Q), 1)
    return jnp.where(col < row, 0.0, NEG)


EXIT_LOG = -104.0


def attn_b_fwd(q, k, v, *, gather=None, name):
    S = q.shape[0]
    nq = S // ATT_Q
    plan, shards, fulls = gather if gather is not None else (None, [], [])
    ng = len(shards)

    def body(q_ref, k_ref, v_ref, *rest):
        hp = pl.program_id(0)
        qb = pl.program_id(1)
        o_ref, t_ref, n_ref = rest[2 * ng:2 * ng + 3]
        if plan is not None:
            comm = (rest[:ng], rest[2 * ng + 3:3 * ng + 3], rest[3 * ng + 3:])

            @pl.when(jnp.logical_and(hp == 0, qb == 0))
            def _():
                plan.start(*comm)

        tri = _tri("gt")

        def block(kb, carry, bias):
            ks = pl.multiple_of(kb * ATT_Q, ATT_Q)
            new = []
            for h in range(2):
                lanes = slice(h * HEAD_DIM, (h + 1) * HEAD_DIM)
                c, acc = carry[h]
                z = _dot(q_ref[:, lanes], k_ref[pl.ds(ks, ATT_Q), lanes], 1, 1)
                keep, take = _log_sigmoids(z if bias is None else z + bias)
                w = jnp.exp(take + (_cum(keep, tri) + c))
                acc = acc + _dot(w.astype(BF16), v_ref[pl.ds(ks, ATT_Q), lanes], 1, 0)
                c = c + jnp.sum(keep, axis=-1, keepdims=True)
                new.append((c, acc))
            return tuple(new)

        def cond(state):
            it, cmax, _ = state
            return jnp.logical_and(it <= qb, cmax >= EXIT_LOG)

        def step(state):
            it, _, carry = state
            carry = block(qb - it, carry, None)
            return it + 1, jnp.max(jnp.maximum(carry[0][0], carry[1][0])), carry

        init = tuple((jnp.zeros((ATT_Q, 1), F32), jnp.zeros((ATT_Q, HEAD_DIM), F32)) for _ in range(2))
        diag = block(qb, init, _not_before_bias())
        visited, _, res = lax.while_loop(cond, step, (jnp.int32(1), jnp.float32(0.0), diag))
        o_ref[...] = jnp.concatenate([res[0][1], res[1][1]], axis=1).astype(BF16)
        t_ref[...] = jnp.concatenate([jnp.broadcast_to(res[h][0], (ATT_Q, HEAD_DIM)) for h in range(2)], axis=1)
        n_ref[hp, qb] = visited.astype(F32)
        if plan is not None:
            @pl.when(jnp.logical_and(hp == N_HEADS // 2 - 1, qb == nq - 1))
            def _():
                plan.finish(*comm)

    qspec = pl.BlockSpec((ATT_Q, 2 * HEAD_DIM), lambda hp, qb: (qb, hp))
    kvspec = pl.BlockSpec((S, 2 * HEAD_DIM), lambda hp, qb: (0, hp))
    outs = pl.pallas_call(
        body,
        out_shape=(jax.ShapeDtypeStruct((S, WIDTH), BF16), jax.ShapeDtypeStruct((S, WIDTH), F32),
                   jax.ShapeDtypeStruct((N_HEADS // 2, nq), F32))
        + tuple(jax.ShapeDtypeStruct(f.shape, f.dtype) for f in fulls),
        grid=(N_HEADS // 2, nq),
        in_specs=[qspec, kvspec, kvspec] + [ANY] * (2 * ng),
        out_specs=(qspec, qspec, pl.BlockSpec(memory_space=pltpu.SMEM)) + tuple([ANY] * ng),
        scratch_shapes=plan.scratch_shapes() if plan is not None else (),
        input_output_aliases={3 + ng + i: 3 + i for i in range(ng)},
        compiler_params=_cp(("arbitrary", "arbitrary")),
        name=name,
    )(q, k, v, *shards, *fulls)
    return outs if plan is None else (outs[0], outs[1], outs[2], list(outs[3:]))


def attn_b_bwd(q, k, v, tot, nblk, do, *, scatter=None, name):
    S = q.shape[0]
    nq = S // ATT_Q
    plan, pairs16 = scatter if scatter is not None else (None, [])
    ns = len(pairs16)

    def body(q_ref, k_ref, v_ref, t_ref, n_ref, do_ref, *rest):
        hp = pl.program_id(0)
        qb = pl.program_id(1)
        dq_ref, dk_ref, dv_ref = rest[ns:ns + 3]
        if plan is not None:
            comm = (rest[:ns], rest[ns + 3:2 * ns + 3], rest[2 * ns + 3:])

            @pl.when(jnp.logical_and(hp == 0, qb == 0))
            def _():
                plan.start(*comm)

        first = jnp.clip(qb + 1 - n_ref[hp, qb].astype(jnp.int32), 0, qb + 1)
        tri_le = _tri("le")
        tri_lt = _tri("lt")

        @pl.when(qb == 0)
        def _():
            dk_ref[...] = jnp.zeros_like(dk_ref)
            dv_ref[...] = jnp.zeros_like(dv_ref)

        def block(kb, carry, bias):
            ks = pl.multiple_of(kb * ATT_Q, ATT_Q)
            new = []
            for h in range(2):
                lanes = slice(h * HEAD_DIM, (h + 1) * HEAD_DIM)
                cl, cg, dq = carry[h]
                qh = q_ref[:, lanes]
                doh = do_ref[:, lanes]
                kh = k_ref[pl.ds(ks, ATT_Q), lanes]
                vh = v_ref[pl.ds(ks, ATT_Q), lanes]
                totl = t_ref[:, h * HEAD_DIM:h * HEAD_DIM + 1]
                z = _dot(qh, kh, 1, 1)
                keep, take = _log_sigmoids(z if bias is None else z + bias)
                sig = jnp.exp(take)
                w = sig * jnp.exp((totl - cl) - _cum(keep, tri_le))
                g = w * _dot(doh, vh, 1, 1)
                G = _dot(g.astype(BF16), tri_lt, 1, 0) + cg
                dz = (g * (1.0 - sig) - sig * G).astype(BF16)
                dq = dq + _dot(dz, kh, 1, 0)
                dk_ref[pl.ds(ks, ATT_Q), lanes] += _dot(dz, qh, 0, 0)
                dv_ref[pl.ds(ks, ATT_Q), lanes] += _dot(w.astype(BF16), doh, 0, 0)
                cl = cl + jnp.sum(keep, axis=-1, keepdims=True)
                cg = cg + jnp.sum(g, axis=-1, keepdims=True)
                new.append((cl, cg, dq))
            return tuple(new)

        init = tuple((jnp.zeros((ATT_Q, 1), F32), jnp.zeros((ATT_Q, 1), F32), jnp.zeros((ATT_Q, HEAD_DIM), F32))
                     for _ in range(2))
        res = lax.fori_loop(jnp.minimum(first, qb), qb, lambda kb, carry: block(kb, carry, None), init)
        res = block(qb, res, _not_before_bias())
        dq_ref[...] = (jnp.concatenate([res[0][2], res[1][2]], axis=1) * QK_SCALE).astype(BF16)
        if plan is not None:
            @pl.when(jnp.logical_and(hp == N_HEADS // 2 - 1, qb == nq - 1))
            def _():
                plan.finish(*comm)

    qspec = pl.BlockSpec((ATT_Q, 2 * HEAD_DIM), lambda hp, qb: (qb, hp))
    kvspec = pl.BlockSpec((S, 2 * HEAD_DIM), lambda hp, qb: (0, hp))
    outs = pl.pallas_call(
        body,
        out_shape=(jax.ShapeDtypeStruct((S, WIDTH), BF16), jax.ShapeDtypeStruct((S, WIDTH), F32),
                   jax.ShapeDtypeStruct((S, WIDTH), F32)) + (tuple(plan.out_shapes()) if plan is not None else ()),
        grid=(N_HEADS // 2, nq),
        in_specs=[qspec, kvspec, kvspec, qspec, pl.BlockSpec(memory_space=pltpu.SMEM), qspec] + [ANY] * ns,
        out_specs=(qspec, kvspec, kvspec) + tuple([ANY] * ns),
        scratch_shapes=plan.scratch_shapes() if plan is not None else (),
        compiler_params=_cp(("arbitrary", "arbitrary")),
        name=name,
    )(q, k, v, tot, nblk, do, *pairs16)
    return outs if plan is None else (outs[0], outs[1], outs[2], list(outs[3:]))


U_COLBLK = 6


def _pool_counts(t0, rows):
    t = t0 + lax.broadcasted_iota(jnp.int32, (rows, WIDTH), 0)
    lane_grp = lax.broadcasted_iota(jnp.int32, (rows, WIDTH), 1) // GROUP_DIM
    w2, w4, w8, w16 = POOL_WINDOWS
    win = jnp.where(lane_grp == 0, w2, jnp.where(lane_grp == 1, w4, jnp.where(lane_grp == 2, w8, w16)))
    cnt = jnp.minimum(t + 1, win)
    return 1.0 / cnt.astype(F32), lane_grp


def _window_sums(ext, shift_fn):
    s2 = ext + shift_fn(ext, 1)
    s4 = s2 + shift_fn(s2, 2)
    s8 = s4 + shift_fn(s4, 4)
    s16 = s8 + shift_fn(s8, 8)
    return s2, s4, s8, s16


def _select_group(lane_grp, s2, s4, s8, s16):
    return jnp.where(lane_grp == 0, s2, jnp.where(lane_grp == 1, s4, jnp.where(lane_grp == 2, s8, s16)))


def _pooled_tile(u_ref, h_ref, i, T):
    halo = jnp.where(i > 0, h_ref[...], 0.0)
    ext = jnp.concatenate([halo, u_ref[...]], axis=0)
    n = T + HALO
    sums = _window_sums(ext, lambda v, k: pltpu.roll(v, k, 0))
    inv, lane_grp = _pool_counts(i * T - HALO, n)
    pooled = _select_group(lane_grp, *sums) * inv - ext
    return pooled[HALO:, :]


def pool_fwd(proj, w_pool, scale, *, name):
    S = proj.shape[0]
    T = _tile(S)
    hb = T // HALO

    def body(u_ref, h_ref, w_ref, s_ref, o_ref):
        i = pl.program_id(0)
        pooled = _pooled_tile(u_ref, h_ref, i, T).astype(BF16)
        outs = [_dot(pooled[:, g * GROUP_DIM:(g + 1) * GROUP_DIM], w_ref[g], 1, 0) for g in range(4)]
        o_ref[...] = (jnp.concatenate(outs, axis=1) * s_ref[...]).astype(BF16)

    return pl.pallas_call(
        body,
        out_shape=jax.ShapeDtypeStruct((S, WIDTH), BF16),
        grid=(S // T,),
        in_specs=[pl.BlockSpec((T, WIDTH), lambda i: (i, U_COLBLK)),
                  pl.BlockSpec((HALO, WIDTH), lambda i: (jnp.maximum(i * hb - 1, 0), U_COLBLK)),
                  pl.BlockSpec((4, GROUP_DIM, GROUP_DIM), lambda i: (0, 0, 0)),
                  pl.BlockSpec((1, WIDTH), lambda i: (0, 0))],
        out_specs=pl.BlockSpec((T, WIDTH), lambda i: (i, 0)),
        compiler_params=_cp(("parallel",)),
        name=name,
    )(proj, proj, w_pool, scale)


def pool_bwd(proj, w_pool, scale, do, *, name):
    S = proj.shape[0]
    T = _tile(S)
    hb = T // HALO
    nt = S // T

    def body(u_ref, h_ref, w_ref, s_ref, do_ref, dof_ref, du_ref, dw_ref, ds_ref):
        i = pl.program_id(0)

        @pl.when(i == 0)
        def _():
            dw_ref[...] = jnp.zeros_like(dw_ref)
            ds_ref[...] = jnp.zeros_like(ds_ref)

        pooled = _pooled_tile(u_ref, h_ref, i, T).astype(BF16)
        dov = do_ref[...].astype(F32)
        fut = jnp.where(i < nt - 1, dof_ref[...].astype(F32), 0.0)
        dmix = (jnp.concatenate([dov, fut], axis=0) * s_ref[...]).astype(BF16)
        mixed, dpool = [], []
        for g in range(4):
            lanes = slice(g * GROUP_DIM, (g + 1) * GROUP_DIM)
            mixed.append(_dot(pooled[:, lanes], w_ref[g], 1, 0))
            dw_ref[g] += _dot(pooled[:, lanes], dmix[:T, lanes], 0, 0)
            dpool.append(_dot(dmix[:, lanes], w_ref[g], 1, 1))
        ds_ref[...] += jnp.sum(dov * jnp.concatenate(mixed, axis=1), axis=0, keepdims=True)
        dp = jnp.concatenate(dpool, axis=1)
        n = T + HALO
        inv, lane_grp = _pool_counts(i * T, n)
        sums = _window_sums(dp * inv, lambda v, k: pltpu.roll(v, n - k, 0))
        du = _select_group(lane_grp, *sums) - dp
        du_ref[...] = du[:T, :].astype(BF16)

    row = pl.BlockSpec((T, WIDTH), lambda i: (i, 0))
    return pl.pallas_call(
        body,
        out_shape=(jax.ShapeDtypeStruct((S, WIDTH), BF16), jax.ShapeDtypeStruct((4, GROUP_DIM, GROUP_DIM), F32),
                   jax.ShapeDtypeStruct((1, WIDTH), F32)),
        grid=(nt,),
        in_specs=[pl.BlockSpec((T, WIDTH), lambda i: (i, U_COLBLK)),
                  pl.BlockSpec((HALO, WIDTH), lambda i: (jnp.maximum(i * hb - 1, 0), U_COLBLK)),
                  pl.BlockSpec((4, GROUP_DIM, GROUP_DIM), lambda i: (0, 0, 0)),
                  pl.BlockSpec((1, WIDTH), lambda i: (0, 0)),
                  row,
                  pl.BlockSpec((HALO, WIDTH), lambda i: (jnp.minimum((i + 1) * hb, S // HALO - 1), 0))],
        out_specs=(row, pl.BlockSpec((4, GROUP_DIM, GROUP_DIM), lambda i: (0, 0, 0)),
                   pl.BlockSpec((1, WIDTH), lambda i: (0, 0))),
        compiler_params=_cp(("arbitrary",)),
        name=name,
    )(proj, proj, w_pool, scale, do, do)


GATE_BLK0 = GATE_COL0 // WIDTH


def merge_fwd(oa, ob, oc, proj, b_gate, wa, wb, wc, *, name):
    S = oa.shape[0]
    T = _tile(S)

    def body(oa_ref, ob_ref, oc_ref, ga, gb, gc, ba, bb, bc, wa_ref, wb_ref, wc_ref, m_ref):
        acc = None
        for o_ref, g_ref, b_ref, w_ref in ((oa_ref, ga, ba, wa_ref), (ob_ref, gb, bb, wb_ref), (oc_ref, gc, bc, wc_ref)):
            y = _dot(o_ref[...], w_ref[...], 1, 0)
            t = jax.nn.sigmoid(g_ref[...] + b_ref[...]) * y
            acc = t if acc is None else acc + t
        m_ref[...] = acc.astype(BF16)

    row = pl.BlockSpec((T, WIDTH), lambda i, n: (i, 0))
    gate = lambda b: pl.BlockSpec((T, WIDTH), lambda i, n, b=b: (i, GATE_BLK0 + 2 * b + n))
    bias = lambda b: pl.BlockSpec((1, WIDTH), lambda i, n, b=b: (0, 2 * b + n))
    wspec = pl.BlockSpec((WIDTH, WIDTH), lambda i, n: (0, n))
    return pl.pallas_call(
        body,
        out_shape=jax.ShapeDtypeStruct((S, D_MODEL), BF16),
        grid=(S // T, 2),
        in_specs=[row, row, row, gate(0), gate(1), gate(2), bias(0), bias(1), bias(2), wspec, wspec, wspec],
        out_specs=pl.BlockSpec((T, WIDTH), lambda i, n: (i, n)),
        compiler_params=_cp(("parallel", "parallel")),
        name=name,
    )(oa, ob, oc, proj, proj, proj, b_gate, b_gate, b_gate, wa, wb, wc)


def merge_bwd(dm, oa, ob, oc, proj, b_gate, wa, wb, wc, *, name):
    S = oa.shape[0]
    T = _tile(S)

    def body(dm_ref, oa_ref, ob_ref, oc_ref, ga, gb, gc, ba, bb, bc, wa_ref, wb_ref, wc_ref,
             ta, tb, tc, dga, dgb, dgc, dba, dbb, dbc):
        i = pl.program_id(1)
        dmv = dm_ref[...].astype(F32)
        for o_ref, g_ref, b_ref, w_ref, t_ref, dg_ref, db_ref in (
                (oa_ref, ga, ba, wa_ref, ta, dga, dba), (ob_ref, gb, bb, wb_ref, tb, dgb, dbb),
                (oc_ref, gc, bc, wc_ref, tc, dgc, dbc)):
            y = _dot(o_ref[...], w_ref[...], 1, 0)
            gate = jax.nn.sigmoid(g_ref[...] + b_ref[...])
            t_ref[...] = (gate * dmv).astype(BF16)
            dgl = dmv * y * gate * (1.0 - gate)
            dg_ref[...] = dgl.astype(BF16)

            @pl.when(i == 0)
            def _():
                db_ref[...] = jnp.zeros_like(db_ref)

            db_ref[...] += jnp.sum(dgl, axis=0, keepdims=True)

    row = pl.BlockSpec((T, WIDTH), lambda n, i: (i, 0))
    half = pl.BlockSpec((T, WIDTH), lambda n, i: (i, n))
    gate = lambda b: pl.BlockSpec((T, WIDTH), lambda n, i, b=b: (i, GATE_BLK0 + 2 * b + n))
    bias = lambda b: pl.BlockSpec((1, WIDTH), lambda n, i, b=b: (0, 2 * b + n))
    wspec = pl.BlockSpec((WIDTH, WIDTH), lambda n, i: (0, n))
    bvec = pl.BlockSpec((1, WIDTH), lambda n, i: (0, n))
    act = jax.ShapeDtypeStruct((S, D_MODEL), BF16)
    vec = jax.ShapeDtypeStruct((1, D_MODEL), F32)
    return pl.pallas_call(
        body,
        out_shape=(act, act, act, act, act, act, vec, vec, vec),
        grid=(2, S // T),
        in_specs=[half, row, row, row, gate(0), gate(1), gate(2), bias(0), bias(1), bias(2), wspec, wspec, wspec],
        out_specs=(half, half, half, half, half, half, bvec, bvec, bvec),
        compiler_params=_cp(("parallel", "arbitrary")),
        name=name,
    )(dm, oa, ob, oc, proj, proj, proj, b_gate, b_gate, b_gate, wa, wb, wc)


FF_T = 256
FF_BLKS = D_FF // FF_T


def _silu_parts(x):
    s = jax.nn.sigmoid(x)
    return x * s, s


def _conv3(ext, w_ref, b_ref):
    taps = (pltpu.roll(ext, 2, 0), pltpu.roll(ext, 1, 0), ext)
    return b_ref[...] + w_ref[0:1, :] * taps[0] + w_ref[1:2, :] * taps[1] + w_ref[2:3, :] * taps[2], taps


def conv_glu_fwd(u, conv_w, conv_b, *, name):
    S = u.shape[0]
    T = _tile(S)
    hb = T // CONV_HALO

    def body(ug, ugh, uv, uvh, wg, wv, bg, bv, a_ref):
        i = pl.program_id(1)
        cs = []
        for m_ref, h_ref, w_ref, b_ref in ((ug, ugh, wg, bg), (uv, uvh, wv, bv)):
            halo = jnp.where(i > 0, h_ref[...], 0.0)
            ext = jnp.concatenate([halo, m_ref[...]], axis=0)
            cs.append(_conv3(ext, w_ref, b_ref)[0][CONV_HALO:, :])
        act, _ = _silu_parts(cs[0])
        a_ref[...] = (act * cs[1]).astype(BF16)

    main = lambda o: pl.BlockSpec((T, FF_T), lambda c, i, o=o: (i, c + o))
    halo = lambda o: pl.BlockSpec((CONV_HALO, FF_T), lambda c, i, o=o: (jnp.maximum(i * hb - 1, 0), c + o))
    wsp = lambda o: pl.BlockSpec((3, FF_T), lambda c, i, o=o: (0, c + o))
    bsp = lambda o: pl.BlockSpec((1, FF_T), lambda c, i, o=o: (0, c + o))
    return pl.pallas_call(
        body,
        out_shape=jax.ShapeDtypeStruct((S, D_FF), BF16),
        grid=(FF_BLKS, S // T),
        in_specs=[main(0), halo(0), main(FF_BLKS), halo(FF_BLKS), wsp(0), wsp(FF_BLKS), bsp(0), bsp(FF_BLKS)],
        out_specs=pl.BlockSpec((T, FF_T), lambda c, i: (i, c)),
        compiler_params=_cp(("parallel", "parallel")),
        name=name,
    )(u, u, u, u, conv_w, conv_w, conv_b, conv_b)


def conv_glu_bwd(u, conv_w, conv_b, da, *, name):
    S = u.shape[0]
    T = _tile(S)
    hb = T // CONV_HALO
    nt = S // T
    n = T + 2 * CONV_HALO

    def body(ug, ugp, ugf, uv, uvp, uvf, wg, wv, bg, bv, da_ref, daf_ref,
             dug, duv, dwg, dwv, dbg, dbv):
        i = pl.program_id(1)
        first, last = i == 0, i == nt - 1
        taps, cs = [], []
        for m_ref, p_ref, f_ref, w_ref, b_ref in ((ug, ugp, ugf, wg, bg), (uv, uvp, uvf, wv, bv)):
            ext = jnp.concatenate([jnp.where(first, 0.0, p_ref[...]), m_ref[...], jnp.where(last, 0.0, f_ref[...])], axis=0)
            c, tp = _conv3(ext, w_ref, b_ref)
            cs.append(c)
            taps.append(tp)
        dae = jnp.concatenate([jnp.zeros((CONV_HALO, FF_T), F32), da_ref[...].astype(F32),
                               jnp.where(last, 0.0, daf_ref[...].astype(F32))], axis=0)
        act, sg = _silu_parts(cs[0])
        dcs = (dae * cs[1] * (sg * (1.0 + cs[0] * (1.0 - sg))), dae * act)
        main = slice(CONV_HALO, CONV_HALO + T)
        for tp, dc, w_ref, du_ref, dw_ref, db_ref in ((taps[0], dcs[0], wg, dug, dwg, dbg),
                                                      (taps[1], dcs[1], wv, duv, dwv, dbv)):
            du = (w_ref[2:3, :] * dc + w_ref[1:2, :] * pltpu.roll(dc, n - 1, 0) + w_ref[0:1, :] * pltpu.roll(dc, n - 2, 0))
            du_ref[...] = du[main, :].astype(BF16)
            dcm = dc[main, :]
            rows = [jnp.sum(dcm * tp[j][main, :], axis=0, keepdims=True) for j in range(3)]

            @pl.when(first)
            def _():
                dw_ref[...] = jnp.zeros_like(dw_ref)
                db_ref[...] = jnp.zeros_like(db_ref)

            dw_ref[...] += jnp.concatenate(rows, axis=0)
            db_ref[...] += jnp.sum(dcm, axis=0, keepdims=True)

    main = lambda o: pl.BlockSpec((T, FF_T), lambda c, i, o=o: (i, c + o))
    past = lambda o: pl.BlockSpec((CONV_HALO, FF_T), lambda c, i, o=o: (jnp.maximum(i * hb - 1, 0), c + o))
    fut = lambda o: pl.BlockSpec((CONV_HALO, FF_T), lambda c, i, o=o: (jnp.minimum((i + 1) * hb, S // CONV_HALO - 1), c + o))
    wsp = lambda o: pl.BlockSpec((3, FF_T), lambda c, i, o=o: (0, c + o))
    bsp = lambda o: pl.BlockSpec((1, FF_T), lambda c, i, o=o: (0, c + o))
    return pl.pallas_call(
        body,
        out_shape=(jax.ShapeDtypeStruct((S, D_FF), BF16), jax.ShapeDtypeStruct((S, D_FF), BF16),
                   jax.ShapeDtypeStruct((3, D_FF), F32), jax.ShapeDtypeStruct((3, D_FF), F32),
                   jax.ShapeDtypeStruct((1, D_FF), F32), jax.ShapeDtypeStruct((1, D_FF), F32)),
        grid=(FF_BLKS, nt),
        in_specs=[main(0), past(0), fut(0), main(FF_BLKS), past(FF_BLKS), fut(FF_BLKS),
                  wsp(0), wsp(FF_BLKS), bsp(0), bsp(FF_BLKS), main(0), fut(0)],
        out_specs=(main(0), main(0), wsp(0), wsp(0), bsp(0), bsp(0)),
        compiler_params=_cp(("parallel", "arbitrary")),
        name=name,
    )(u, u, u, u, u, u, conv_w, conv_w, conv_b, conv_b, da, da)


def loss_head(y, target, *, name):
    S, D = y.shape
    T = _tile(S)

    def body(y_ref, t_ref, dy_ref, l_ref):
        i = pl.program_id(0)
        err = y_ref[...] - t_ref[...]
        dy_ref[...] = err * (1.0 / D)

        @pl.when(i == 0)
        def _():
            l_ref[...] = jnp.zeros_like(l_ref)

        l_ref[...] += 0.5 * jnp.sum(jnp.mean(err * err, axis=-1, keepdims=True))

    row = pl.BlockSpec((T, D), lambda i: (i, 0))
    return pl.pallas_call(
        body,
        out_shape=(jax.ShapeDtypeStruct((S, D), F32), jax.ShapeDtypeStruct((8, 128), F32)),
        grid=(S // T,),
        in_specs=[row, row],
        out_specs=(row, pl.BlockSpec((8, 128), lambda i: (0, 0))),
        compiler_params=_cp(("arbitrary",)),
        name=name,
    )(y, target)


ELEMS_PER_BLOCK = 256 * 1024


def _rows_tile(rows, cols):
    if rows * cols <= ELEMS_PER_BLOCK or rows % 8:
        return rows
    best = 8
    for tr in range(8, rows + 1, 8):
        if rows % tr == 0 and tr * cols <= ELEMS_PER_BLOCK:
            best = tr
    return best


def _adamw_math(g, w_ref, m_ref, v_ref, g_out, d_out, m_out, v_out):
    mn = ADAM_B1 * m_ref[...] + (1.0 - ADAM_B1) * g
    vn = ADAM_B2 * v_ref[...] + (1.0 - ADAM_B2) * (g * g)
    m_hat = mn / (1.0 - ADAM_B1 ** ADAM_STEP)
    v_hat = vn / (1.0 - ADAM_B2 ** ADAM_STEP)
    g_out[...] = g
    d_out[...] = -ADAM_LR * (m_hat / (jnp.sqrt(v_hat) + ADAM_EPS) + ADAM_WD * w_ref[...])
    m_out[...] = mn
    v_out[...] = vn


def adamw(w, m, v, g, *, name):
    rows, cols = w.shape
    tr = _rows_tile(rows, cols)

    def body(w_ref, m_ref, v_ref, g_ref, g_out, d_out, m_out, v_out):
        _adamw_math(g_ref[...], w_ref, m_ref, v_ref, g_out, d_out, m_out, v_out)

    spec = pl.BlockSpec((tr, cols), lambda i: (i, 0))
    shp = jax.ShapeDtypeStruct((rows, cols), F32)
    return pl.pallas_call(
        body,
        out_shape=(shp, shp, shp, shp),
        grid=(rows // tr,),
        in_specs=[spec] * 4,
        out_specs=(spec, spec, spec, spec),
        compiler_params=_cp(("parallel",)),
        name=name,
    )(w, m, v, g)


ANY = pl.BlockSpec(memory_space=pl.ANY)
STAGE_BYTES = 2 * 1024 * 1024


def _mesh_pos():
    return lax.axis_index("x"), lax.axis_index("y"), lax.axis_index("c")


def _chip_peers(x, y):
    return [(1 - x, y), (x, 1 - y), (1 - x, 1 - y)]


def _all_peers(x, y, c):
    return [((1 - x) if (r >> 2) & 1 else x, (1 - y) if (r >> 1) & 1 else y, (1 - c) if r & 1 else c)
            for r in range(1, 8)]


class LayerGather:
    def __init__(self, shards, axes, layer):
        self.nt = len(shards)
        self.axes = list(axes)
        self.layer = layer
        self.shapes = [s.shape for s in shards]
        self.dtypes = [s.dtype for s in shards]
        self.sizes = [s.shape[a] for s, a in zip(shards, axes)]
        self.split = [s.shape[1] % 32 == 0 for s in shards]
        self.half_rows = [s.shape[1] // 2 if sp else s.shape[1] for s, sp in zip(shards, self.split)]
        self.chunk_rows = []
        for s in shards:
            rt = s.shape[1]
            while rt % 32 == 0 and rt * s.shape[2] * s.dtype.itemsize > STAGE_BYTES:
                rt //= 2
            self.chunk_rows.append(rt)

    def out_shapes(self):
        out = []
        for shp, a, sz, dt in zip(self.shapes, self.axes, self.sizes, self.dtypes):
            shp = list(shp)
            shp[a] = 4 * sz
            out.append(jax.ShapeDtypeStruct(tuple(shp), dt))
        return out

    def scratch_shapes(self):
        return ([pltpu.VMEM((1, rt, shp[2]), dt) for shp, rt, dt in zip(self.shapes, self.chunk_rows, self.dtypes)]
                + [pltpu.SemaphoreType.DMA((2 * self.nt,))] + [pltpu.SemaphoreType.DMA((3 * self.nt,)) for _ in range(4)])

    def _views(self, ins, outs, scratch):
        nt = self.nt
        stage, stage_sems = scratch[:nt], scratch[nt]
        ici_send, ici_recv, d2d_send, d2d_recv = scratch[nt + 1:]
        x, y, c = _mesh_pos()
        mine = 2 * x + y
        peers = _chip_peers(x, y)
        layer = pl.ds(self.layer, 1)

        def rows(t, half, r0=0, n=None):
            hr = self.half_rows[t]
            if n is None:
                return pl.ds(pl.multiple_of(half * hr, 16), hr) if self.split[t] else pl.ds(0, hr)
            return pl.ds(r0, n)

        def placed(t, blk, row_sel, row_len):
            sz = self.sizes[t]
            if self.axes[t] == 2:
                return outs[t].at[layer, row_sel, pl.ds(pl.multiple_of(blk * sz, 128), sz)]
            return outs[t].at[layer, pl.ds(pl.multiple_of(blk * sz, 16) + row_sel.start, row_len), :]

        def ici(t, k, blk):
            px, py = peers[k]
            sel = rows(t, c)
            return pltpu.make_async_remote_copy(
                src_ref=ins[t].at[layer, sel, :], dst_ref=placed(t, blk, sel, self.half_rows[t]),
                send_sem=ici_send.at[3 * t + k], recv_sem=ici_recv.at[3 * t + k],
                device_id=(px, py, c), device_id_type=MESH_T)

        def d2d(t, k, half):
            px, py = peers[k]
            piece = placed(t, 2 * px + py, rows(t, half), self.half_rows[t])
            return pltpu.make_async_remote_copy(
                src_ref=piece, dst_ref=piece, send_sem=d2d_send.at[3 * t + k], recv_sem=d2d_recv.at[3 * t + k],
                device_id=(x, y, 1 - c), device_id_type=MESH_T)

        def own_chunk(t, r0):
            rt = self.chunk_rows[t]
            sel = pl.ds(r0, rt)
            return ins[t].at[layer, sel, :], placed(t, mine, sel, rt), stage[t], stage_sems

        return c, mine, peers, ici, d2d, own_chunk

    def start(self, ins, outs, scratch):
        c, mine, peers, ici, d2d, own_chunk = self._views(ins, outs, scratch)
        for t in range(self.nt):
            for k in range(3):
                ici(t, k, mine).start()
        starts = [list(range(0, self.shapes[t][1], self.chunk_rows[t])) for t in range(self.nt)]
        for r in range(max(len(s) for s in starts)):
            active = [(t, *own_chunk(t, starts[t][r])) for t in range(self.nt) if r < len(starts[t])]
            loads = [pltpu.make_async_copy(src, buf, sems.at[2 * t]) for t, src, dst, buf, sems in active]
            for cp in loads:
                cp.start()
            for cp in loads:
                cp.wait()
            stores = [pltpu.make_async_copy(buf, dst, sems.at[2 * t + 1]) for t, src, dst, buf, sems in active]
            for cp in stores:
                cp.start()
            for cp in stores:
                cp.wait()

    def finish(self, ins, outs, scratch):
        c, mine, peers, ici, d2d, own_chunk = self._views(ins, outs, scratch)
        for t in range(self.nt):
            for k, (px, py) in enumerate(peers):
                ici(t, k, 2 * px + py).wait_recv()
                if self.split[t]:
                    d2d(t, k, c).start()
        for t in range(self.nt):
            for k in range(3):
                if self.split[t]:
                    d2d(t, k, 1 - c).wait_recv()
        for t in range(self.nt):
            for k in range(3):
                ici(t, k, mine).wait_send()
                if self.split[t]:
                    d2d(t, k, c).wait_send()


def all_gather_layer(shards, axes, layer, *, name):
    plan = LayerGather(shards, axes, layer)
    nt = plan.nt

    def body(*refs):
        ins, outs, scratch = refs[:nt], refs[nt:2 * nt], refs[2 * nt:]
        plan.start(ins, outs, scratch)
        plan.finish(ins, outs, scratch)

    return pl.pallas_call(
        body,
        out_shape=tuple(plan.out_shapes()),
        in_specs=[ANY] * nt,
        out_specs=tuple([ANY] * nt),
        scratch_shapes=plan.scratch_shapes(),
        name=name,
    )(*shards)


class HalfLayout:
    def __init__(self, shape, axis):
        self.R, self.C = shape
        self.axis = axis
        if axis == 1:
            self.hr, self.pw = self.R // 2, self.C // 4
            self.half_shape = (self.hr, self.C)
        else:
            self.hr, self.pw = self.R // 8, self.C
            self.half_shape = (4 * self.hr, self.C)
        self.tr = _rows_tile(self.hr, self.pw)
        self.nr = self.hr // self.tr

    def in_grad(self, ref, blk, half):
        if self.axis == 1:
            return ref.at[pl.ds(pl.multiple_of(half * self.hr, 16), self.hr), pl.ds(pl.multiple_of(blk * self.pw, 128), self.pw)]
        return ref.at[pl.ds(pl.multiple_of((2 * blk + half) * self.hr, 16), self.hr), :]

    def in_half(self, ref, blk):
        if self.axis == 1:
            return ref.at[:, pl.ds(pl.multiple_of(blk * self.pw, 128), self.pw)]
        return ref.at[pl.ds(pl.multiple_of(blk * self.hr, 16), self.hr), :]

    def grad_spec(self):
        if self.axis == 1:
            return pl.BlockSpec((self.tr, self.pw), lambda j, i, s: (s[0] * self.nr + i, j))
        return pl.BlockSpec((self.tr, self.pw), lambda j, i, s: ((2 * j + s[0]) * self.nr + i, 0))

    def half_spec(self):
        if self.axis == 1:
            return pl.BlockSpec((self.tr, self.pw), lambda j, i, s: (i, j))
        return pl.BlockSpec((self.tr, self.pw), lambda j, i, s: (j * self.nr + i, 0))


def half_exchange(grads, layouts, *, name):
    nt = len(grads)
    pieces = [(t, j) for t in range(nt) for j in (range(4) if layouts[t].axis == 0 else range(1))]

    def body(*refs):
        ins, outs = refs[:nt], refs[nt:2 * nt]
        send_sems, recv_sems = refs[2 * nt:]
        x, y, c = _mesh_pos()
        cps = []
        for n, (t, j) in enumerate(pieces):
            lay = layouts[t]
            if lay.axis == 1:
                src = ins[t].at[pl.ds(pl.multiple_of((1 - c) * lay.hr, 16), lay.hr), :]
                dst = outs[t]
            else:
                src = lay.in_grad(ins[t], j, 1 - c)
                dst = lay.in_half(outs[t], j)
            cp = pltpu.make_async_remote_copy(src_ref=src, dst_ref=dst, send_sem=send_sems.at[n], recv_sem=recv_sems.at[n],
                                              device_id=(x, y, 1 - c), device_id_type=MESH_T)
            cp.start()
            cps.append(cp)
        for cp in cps:
            cp.wait_recv()
        for cp in cps:
            cp.wait_send()

    return pl.pallas_call(
        body,
        out_shape=tuple(jax.ShapeDtypeStruct(lay.half_shape, F32) for lay in layouts),
        in_specs=[ANY] * nt,
        out_specs=tuple([ANY] * nt),
        scratch_shapes=[pltpu.SemaphoreType.DMA((len(pieces),)), pltpu.SemaphoreType.DMA((len(pieces),))],
        name=name,
    )(*grads)


def pair_sum(grad, other, lay, core, *, name):
    def body(c_ref, g_ref, o_ref, s32_ref, s16_ref):
        s = g_ref[...] + o_ref[...]
        s32_ref[...] = s
        s16_ref[...] = s.astype(BF16)

    return pl.pallas_call(
        body,
        out_shape=(jax.ShapeDtypeStruct(lay.half_shape, F32), jax.ShapeDtypeStruct(lay.half_shape, BF16)),
        grid_spec=pltpu.PrefetchScalarGridSpec(
            num_scalar_prefetch=1, grid=(4, lay.nr),
            in_specs=[lay.grad_spec(), lay.half_spec()],
            out_specs=(lay.half_spec(), lay.half_spec())),
        compiler_params=_cp(("parallel", "parallel")),
        name=name,
    )(core, grad, other)


class BlockScatter:
    def __init__(self, layouts):
        self.layouts = layouts
        self.nt = len(layouts)

    def out_shapes(self):
        return [jax.ShapeDtypeStruct((3, lay.hr, lay.pw), BF16) for lay in self.layouts]

    def scratch_shapes(self):
        return [pltpu.SemaphoreType.DMA((3 * self.nt,)), pltpu.SemaphoreType.DMA((3 * self.nt,))]

    def _copies(self, pairs16, recv, scratch):
        send_sems, recv_sems = scratch
        x, y, c = _mesh_pos()
        return [pltpu.make_async_remote_copy(
            src_ref=lay.in_half(pairs16[t], 2 * px + py), dst_ref=recv[t].at[k],
            send_sem=send_sems.at[3 * t + k], recv_sem=recv_sems.at[3 * t + k],
            device_id=(px, py, c), device_id_type=MESH_T)
            for t, lay in enumerate(self.layouts) for k, (px, py) in enumerate(_chip_peers(x, y))]

    def start(self, pairs16, recv, scratch):
        for cp in self._copies(pairs16, recv, scratch):
            cp.start()

    def finish(self, pairs16, recv, scratch):
        copies = self._copies(pairs16, recv, scratch)
        for cp in copies:
            cp.wait_recv()
        for cp in copies:
            cp.wait_send()


def gather_small(small, *, name):
    def body(small_in, small_out, ssend, srecv):
        x, y, c = _mesh_pos()
        me = 4 * x + 2 * y + c
        sends, recvs = [], []
        for r, (px, py, pc) in enumerate(_all_peers(x, y, c)):
            def mk(slot, r=r, px=px, py=py, pc=pc):
                return pltpu.make_async_remote_copy(
                    src_ref=small_in, dst_ref=small_out.at[slot], send_sem=ssend.at[r], recv_sem=srecv.at[r],
                    device_id=(px, py, pc), device_id_type=MESH_T)
            snd = mk(me)
            snd.start()
            sends.append(snd)
            recvs.append(mk(4 * px + 2 * py + pc))
        for r in recvs:
            r.wait_recv()
        for s in sends:
            s.wait_send()

    return pl.pallas_call(
        body,
        out_shape=jax.ShapeDtypeStruct((8,) + small.shape, F32),
        in_specs=[ANY],
        out_specs=ANY,
        scratch_shapes=[pltpu.SemaphoreType.DMA((7,)), pltpu.SemaphoreType.DMA((7,))],
        name=name,
    )(small)


def sum_chips(pair32, recv, lay, chip, *, name):
    def body(j_ref, p_ref, r_ref, s_ref):
        acc = p_ref[...]
        for k in range(3):
            acc = acc + r_ref[k].astype(F32)
        s_ref[...] = acc

    if lay.axis == 1:
        own = pl.BlockSpec((lay.tr, lay.pw), lambda i, j: (i, j[0]))
    else:
        own = pl.BlockSpec((lay.tr, lay.pw), lambda i, j: (j[0] * lay.nr + i, 0))
    return pl.pallas_call(
        body,
        out_shape=jax.ShapeDtypeStruct((lay.hr, lay.pw), F32),
        grid_spec=pltpu.PrefetchScalarGridSpec(
            num_scalar_prefetch=1, grid=(lay.nr,),
            in_specs=[own, pl.BlockSpec((3, lay.tr, lay.pw), lambda i, j: (0, i, 0))],
            out_specs=pl.BlockSpec((lay.tr, lay.pw), lambda i, j: (i, 0))),
        compiler_params=_cp(("parallel",)),
        name=name,
    )(chip, pair32, recv)


def sum_devices(gathered, own, me, *, name):
    _, R, C = gathered.shape

    def body(me_ref, g_ref, o_ref, s_ref):
        acc = None
        for k in range(8):
            part = jnp.where(me_ref[0] == k, o_ref[...], g_ref[k])
            acc = part if acc is None else acc + part
        s_ref[...] = acc

    return pl.pallas_call(
        body,
        out_shape=jax.ShapeDtypeStruct((R, C), F32),
        grid_spec=pltpu.PrefetchScalarGridSpec(
            num_scalar_prefetch=1, grid=(1,),
            in_specs=[pl.BlockSpec((8, R, C), lambda i, m: (0, 0, 0)), pl.BlockSpec((R, C), lambda i, m: (0, 0))],
            out_specs=pl.BlockSpec((R, C), lambda i, m: (0, 0))),
        compiler_params=_cp(("arbitrary",)),
        name=name,
    )(me, gathered, own)


def sibling_swap(parts, *, name):
    nt = len(parts)

    def body(*refs):
        ins, outs = refs[:nt], refs[nt:2 * nt]
        send_sems, recv_sems = refs[2 * nt:]
        x, y, c = _mesh_pos()
        cps = []
        for t in range(nt):
            cp = pltpu.make_async_remote_copy(src_ref=ins[t], dst_ref=outs[t], send_sem=send_sems.at[t],
                                              recv_sem=recv_sems.at[t], device_id=(x, y, 1 - c), device_id_type=MESH_T)
            cp.start()
            cps.append(cp)
        for cp in cps:
            cp.wait_recv()
        for cp in cps:
            cp.wait_send()

    return pl.pallas_call(
        body,
        out_shape=tuple(jax.ShapeDtypeStruct(p.shape, p.dtype) for p in parts),
        in_specs=[ANY] * nt,
        out_specs=tuple([ANY] * nt),
        scratch_shapes=[pltpu.SemaphoreType.DMA((nt,)), pltpu.SemaphoreType.DMA((nt,))],
        name=name,
    )(*parts)


def adamw_halves(w, m, v, mine, other, lay, core, *, name):
    _, r, c = w.shape
    tr, nr = lay.tr, lay.nr
    assert (r, c) == (2 * lay.hr, lay.pw), (w.shape, lay.hr, lay.pw)

    def body(c_ref, w_ref, m_ref, v_ref, *rest):
        g_refs, outs = rest[:2 * DEPTH], rest[2 * DEPTH:]
        l, h = pl.program_id(0), pl.program_id(1)
        g = None
        for d in range(DEPTH):
            gd = jnp.where(h == c_ref[0], g_refs[d][...], g_refs[DEPTH + d][...])
            g = gd if g is None else jnp.where(l == d, gd, g)
        _adamw_math(g, w_ref, m_ref, v_ref, *outs)

    full = pl.BlockSpec((None, tr, c), lambda l, h, i, s: (l, h * nr + i, 0))

    def part(d, is_mine):
        def index(l, h, i, s):
            used = jnp.logical_and(l == d, (h == s[0]) == is_mine)
            return jnp.where(used, i, 0), 0
        return pl.BlockSpec((tr, c), index)

    shp = jax.ShapeDtypeStruct((DEPTH, r, c), F32)
    return pl.pallas_call(
        body,
        out_shape=(shp, shp, shp, shp),
        grid_spec=pltpu.PrefetchScalarGridSpec(
            num_scalar_prefetch=1, grid=(DEPTH, 2, nr),
            in_specs=[full, full, full] + [part(d, True) for d in range(DEPTH)] + [part(d, False) for d in range(DEPTH)],
            out_specs=(full, full, full, full)),
        compiler_params=_cp(("arbitrary", "arbitrary", "arbitrary")),
        name=name,
    )(core, w, m, v, *mine, *other)


WEIGHTS = ("norm_mix", "w_in", "b_gate", "q_norm_a", "k_norm_a", "rel_bias_a", "w_pool", "pool_scale",
           "w_branch_a", "w_branch_b", "w_branch_c", "w_out", "norm_ffn", "w_up", "conv_w", "conv_b", "w_down")
SHARDED = {"w_in": 2, "w_branch_a": 2, "w_branch_b": 2, "w_branch_c": 2, "w_out": 1, "w_up": 2, "conv_w": 2,
           "w_down": 1}
REPLICATED = tuple(n for n in WEIGHTS if n not in SHARDED)
MATMUL_WEIGHTS = tuple(n for n in SHARDED if n != "conv_w")
SMALL_WEIGHTS = tuple(n for n in WEIGHTS if n not in MATMUL_WEIGHTS)
SMALL_ROWS = 1496


def _layer_fwd(x, p, full, tables, rest=None, prefetch=None):
    l = p["l"]
    diag = exact_dot(p["rel_bias_a"], tables["onehot_t"], name="bias_diagonals")
    diag = diag.reshape(N_HEADS, N_VARIANTS, 1, DIAG_W).transpose(1, 0, 2, 3)
    biasm = bias_expand(diag, name="bias_expand")
    gq8 = jnp.tile(p["q_norm_a"], N_HEADS)[None]
    gk8 = jnp.tile(p["k_norm_a"], N_HEADS)[None]
    h = rmsnorm_fwd(x, p["norm_mix"][None], name="rmsnorm_fwd")
    if rest is None:
        proj = matmul(h, full["w_in"], b_layer=l, name="mm_in")
    else:
        plan, shards, names = rest
        proj, gathered = matmul(h, full["w_in"], b_layer=l, behind=(plan, shards), name="mm_in_gather")
        full = {**full, **dict(zip(names, gathered))}
    qa, ka, va, qb, kb, vb = qkv_prep(proj, gq8, gk8, name="qkv_prep")
    oa = attn_a_fwd(qa, ka, va, biasm, name="attn_a_fwd")
    if prefetch is None:
        ob, tot, nblk = attn_b_fwd(qb, kb, vb, name="attn_b_fwd")
    else:
        plan, shards, names = prefetch
        ob, tot, nblk, filled = attn_b_fwd(qb, kb, vb, gather=(plan, shards, [full[n] for n in names]),
                                           name="attn_b_fwd_gather")
        full = dict(zip(names, filled))
    wpool = p["w_pool"].astype(BF16)
    oc = pool_fwd(proj, wpool, p["pool_scale"][None], name="pool_fwd")
    merged = merge_fwd(oa, ob, oc, proj, p["b_gate"][None], full["w_branch_a"][l], full["w_branch_b"][l],
                       full["w_branch_c"][l], name="merge_fwd")
    x1 = matmul(merged, full["w_out"], b_layer=l, add=x, name="mm_out")
    h2 = rmsnorm_fwd(x1, p["norm_ffn"][None], name="rmsnorm_fwd")
    u = matmul(h2, full["w_up"], b_layer=l, name="mm_up")
    a = conv_glu_fwd(u, full["conv_w"][l], p["conv_b"][None], name="conv_glu_fwd")
    x2 = matmul(a, full["w_down"], b_layer=l, add=x1, name="mm_down")
    saved = dict(x=x, h=h, proj=proj, qa=qa, ka=ka, va=va, qb=qb, kb=kb, vb=vb, oa=oa, ob=ob, tot=tot, nblk=nblk, oc=oc,
                 merged=merged, x1=x1, h2=h2, u=u, a=a, biasm=biasm, gq8=gq8, gk8=gk8, wpool=wpool)
    return x2, saved, full


class GradReducer:
    def __init__(self, layouts, core, chip):
        self.layouts, self.core, self.chip = layouts, core, chip
        self.pairs32, self.received, self.pending = {}, {}, []

    def prepare(self, layer, grads):
        names = list(grads)
        lays = [self.layouts[n] for n in names]
        others = half_exchange([grads[n] for n in names], lays, name="half_exchange")
        for n, lay, other in zip(names, lays, others):
            p32, p16 = pair_sum(grads[n], other, lay, self.core, name="pair_sum")
            self.pairs32[(layer, n)] = p32
            self.pending.append(((layer, n), p16))

    def take(self):
        keys = [k for k, _ in self.pending]
        pairs16 = [p for _, p in self.pending]
        self.pending = []
        return keys, BlockScatter([self.layouts[n] for _, n in keys]), pairs16

    def store(self, keys, received):
        self.received.update(zip(keys, received))

    def finish(self):
        keys = list(self.pairs32)
        mine = [sum_chips(self.pairs32[k], self.received[k], self.layouts[k[1]], self.chip, name="sum_chips") for k in keys]
        other = sibling_swap(mine, name="sibling_swap")
        return dict(zip(keys, mine)), dict(zip(keys, other))


EARLY_WEIGHTS = ("w_down", "w_up", "w_out", "w_branch_a", "w_branch_b", "w_branch_c")


def _layer_bwd(dx2, s, p, tables, reducer):
    g = {}
    full, l = p["full"], p["l"]
    da = matmul(dx2, full["w_down"], b_layer=l, tb=True, name="mm_down_dx")
    g["w_down"] = matmul(s["a"], dx2, ta=True, name="mm_down_dw")
    dug, duv, dcwg, dcwv, dcbg, dcbv = conv_glu_bwd(s["u"], p["conv_w"], p["conv_b"][None], da, name="conv_glu_bwd")
    du = jnp.concatenate([dug, duv], axis=1)
    g["conv_w"] = jnp.concatenate([dcwg, dcwv], axis=1)
    g["conv_b"] = jnp.concatenate([dcbg, dcbv], axis=1)[0]
    g["w_up"] = matmul(s["h2"], du, ta=True, name="mm_up_dw")
    dx1, dg2 = matmul(du, full["w_up"], b_layer=l, tb=True, norm_bwd=(s["x1"], p["norm_ffn"][None], dx2),
                      name="mm_up_dx_norm")
    g["norm_ffn"] = dg2[0]
    dmerged = matmul(dx1, full["w_out"], b_layer=l, tb=True, name="mm_out_dx")
    g["w_out"] = matmul(s["merged"], dx1, ta=True, name="mm_out_dw")
    t_a, t_b, t_c, dga, dgb, dgc, dba, dbb, dbc = merge_bwd(
        dmerged, s["oa"], s["ob"], s["oc"], s["proj"], p["b_gate"][None], p["w_branch_a"], p["w_branch_b"],
        p["w_branch_c"], name="merge_bwd")
    g["b_gate"] = jnp.concatenate([dba, dbb, dbc], axis=1)[0]
    g["w_branch_a"] = matmul(s["oa"], t_a, ta=True, name="mm_branch_dw")
    g["w_branch_b"] = matmul(s["ob"], t_b, ta=True, name="mm_branch_dw")
    g["w_branch_c"] = matmul(s["oc"], t_c, ta=True, name="mm_branch_dw")
    doa = matmul(t_a, full["w_branch_a"], b_layer=l, tb=True, out_dtype=BF16, name="mm_branch_dx")
    dob = matmul(t_b, full["w_branch_b"], b_layer=l, tb=True, out_dtype=BF16, name="mm_branch_dx")
    doc = matmul(t_c, full["w_branch_c"], b_layer=l, tb=True, name="mm_branch_dx_f32")
    dqh, dkh, dva, dbias = attn_a_bwd(s["qa"], s["ka"], s["va"], s["biasm"], doa, name="attn_a_bwd")
    ddiag = relbias_reduce(dbias, name="relbias_reduce")
    ddiag = ddiag.transpose(1, 0, 2, 3).reshape(N_HEADS, N_VARIANTS * DIAG_W)
    g["rel_bias_a"] = exact_dot(ddiag, tables["onehot"], name="relbias_table")
    dqa, dka, dgq8, dgk8 = qknorm_bwd(s["proj"], s["gq8"], s["gk8"], dqh, dkh, name="qknorm_bwd")
    g["q_norm_a"] = dgq8.reshape(N_HEADS, HEAD_DIM).sum(axis=0)
    g["k_norm_a"] = dgk8.reshape(N_HEADS, HEAD_DIM).sum(axis=0)
    reducer.prepare(l, {n: g[n] for n in EARLY_WEIGHTS})
    keys, plan, pairs16 = reducer.take()
    dqb, dkb, dvb, received = attn_b_bwd(s["qb"], s["kb"], s["vb"], s["tot"], s["nblk"], dob, scatter=(plan, pairs16),
                                         name="attn_b_bwd_scatter")
    reducer.store(keys, received)
    duc, dwp, dsc = pool_bwd(s["proj"], s["wpool"], p["pool_scale"][None], doc, name="pool_bwd")
    g["w_pool"] = dwp
    g["pool_scale"] = dsc[0]
    dproj = jnp.concatenate([dqa, dka, dva.astype(BF16), dqb, dkb.astype(BF16), dvb.astype(BF16), duc,
                             dga, dgb, dgc], axis=1)
    g["w_in"] = matmul(s["h"], dproj, ta=True, name="mm_in_dw")
    reducer.prepare(l, {"w_in": g["w_in"]})
    norm = (s["x"], p["norm_mix"][None], dx1)
    if l > 0:
        dx, dg1 = matmul(dproj, full["w_in"], b_layer=l, tb=True, norm_bwd=norm, name="mm_in_dx_norm")
    else:
        keys, plan, pairs16 = reducer.take()
        dx, dg1, received = matmul(dproj, full["w_in"], b_layer=l, tb=True, norm_bwd=norm, behind=(plan, pairs16),
                                   name="mm_in_dx_norm_scatter")
        reducer.store(keys, received)
    g["norm_mix"] = dg1[0]
    return dx, g


def kernel(x, norm_mix, w_in, b_gate, q_norm_a, k_norm_a, rel_bias_a, w_pool, pool_scale, w_branch_a, w_branch_b, w_branch_c, w_out, norm_ffn, w_up, conv_w, conv_b, w_down, loss_target, m_norm_mix, m_w_in, m_b_gate, m_q_norm_a, m_k_norm_a, m_rel_bias_a, m_w_pool, m_pool_scale, m_w_branch_a, m_w_branch_b, m_w_branch_c, m_w_out, m_norm_ffn, m_w_up, m_conv_w, m_conv_b, m_w_down, v_norm_mix, v_w_in, v_b_gate, v_q_norm_a, v_k_norm_a, v_rel_bias_a, v_w_pool, v_pool_scale, v_w_branch_a, v_w_branch_b, v_w_branch_c, v_w_out, v_norm_ffn, v_w_up, v_conv_w, v_conv_b, v_w_down):
    w = dict(zip(WEIGHTS, (norm_mix, w_in, b_gate, q_norm_a, k_norm_a, rel_bias_a, w_pool, pool_scale, w_branch_a,
                           w_branch_b, w_branch_c, w_out, norm_ffn, w_up, conv_w, conv_b, w_down)))
    m = dict(zip(WEIGHTS, (m_norm_mix, m_w_in, m_b_gate, m_q_norm_a, m_k_norm_a, m_rel_bias_a, m_w_pool, m_pool_scale,
                           m_w_branch_a, m_w_branch_b, m_w_branch_c, m_w_out, m_norm_ffn, m_w_up, m_conv_w, m_conv_b,
                           m_w_down)))
    v = dict(zip(WEIGHTS, (v_norm_mix, v_w_in, v_b_gate, v_q_norm_a, v_k_norm_a, v_rel_bias_a, v_w_pool, v_pool_scale,
                           v_w_branch_a, v_w_branch_b, v_w_branch_c, v_w_out, v_norm_ffn, v_w_up, v_conv_w, v_conv_b,
                           v_w_down)))
    onehot = diagonal_onehot()
    tables = dict(onehot=jnp.asarray(onehot), onehot_t=jnp.asarray(np.ascontiguousarray(onehot.T)))

    names = tuple(SHARDED)
    shards = [w[n] if n == "conv_w" else w[n].astype(BF16) for n in names]
    axes = [SHARDED[n] for n in names]
    later = [i for i, n in enumerate(names) if n != "w_in"]
    rest = (LayerGather([shards[i] for i in later], [axes[i] for i in later], 0), [shards[i] for i in later],
            [names[i] for i in later])
    first = names.index("w_in")
    full = {"w_in": all_gather_layer([shards[first]], [axes[first]], 0, name="all_gather_layer")[0]}

    def layer_params(l):
        p = {n: full[n][l] for n in ("w_branch_a", "w_branch_b", "w_branch_c", "conv_w")}
        p.update({n: w[n][l] for n in REPLICATED})
        p.update(full=full, l=l)
        return p

    xs = x[0]
    saved = []
    for l in range(DEPTH):
        prefetch = (LayerGather(shards, axes, l + 1), shards, names) if l + 1 < DEPTH else None
        replicated = {n: w[n][l] for n in REPLICATED}
        xs, s, full = _layer_fwd(xs, dict(replicated, l=l), full, tables, rest if l == 0 else None, prefetch)
        saved.append(s)
    dx, lpart = loss_head(xs, loss_target[0], name="loss_head")
    loss = lax.psum(lpart[0, 0], MESH_AXES)
    as_index = lambda i: jnp.reshape(i, (1,)).astype(jnp.int32)
    cx, cy, cc = _mesh_pos()
    core, chip, me = as_index(cc), as_index(2 * cx + cy), as_index(4 * cx + 2 * cy + cc)
    layouts = {n: HalfLayout((full[n].shape[1], full[n].shape[2]), SHARDED[n] - 1) for n in MATMUL_WEIGHTS}
    reducer = GradReducer(layouts, core, chip)
    grads = [None] * DEPTH
    for l in reversed(range(DEPTH)):
        dx, grads[l] = _layer_bwd(dx, saved[l], layer_params(l), tables, reducer)

    g = {n: jnp.stack([grads[l][n] for l in range(DEPTH)]) for n in SMALL_WEIGHTS}
    flat = jnp.concatenate([g[n].reshape(-1) for n in SMALL_WEIGHTS])
    small = jnp.pad(flat, (0, SMALL_ROWS * 128 - flat.shape[0])).reshape(SMALL_ROWS, 128)
    small_sum = sum_devices(gather_small(small, name="gather_small"), small, me, name="sum_devices").reshape(-1)
    mine, other = reducer.finish()

    res = {}
    for n in MATMUL_WEIGHTS:
        res[n] = adamw_halves(w[n], m[n], v[n], [mine[(l, n)] for l in range(DEPTH)],
                              [other[(l, n)] for l in range(DEPTH)], layouts[n], core, name="adamw_halves")
    off = 0
    for n in SMALL_WEIGHTS:
        shp = g[n].shape
        size = int(np.prod(shp))
        gn = small_sum[off:off + size].reshape(shp)
        off += size
        if n in SHARDED:
            gn = lax.dynamic_slice_in_dim(gn, (2 * cx + cy) * w[n].shape[-1], w[n].shape[-1], axis=len(shp) - 1)
        shp = w[n].shape
        cols = shp[-1]
        two_d = lambda t: t.reshape(int(np.prod(shp)) // cols, cols)
        res[n] = [t.reshape(shp) for t in adamw(two_d(w[n]), two_d(m[n]), two_d(v[n]), two_d(gn), name="adamw")]

    out = [loss, dx[None]]
    for k in range(4):
        out.extend(res[n][k] for n in WEIGHTS)
    return tuple(out)
```

```python
import jax
import jax.numpy as jnp
import numpy as np
from jax import lax
from jax.experimental import pallas as pl
from jax.experimental.pallas import tpu as pltpu

F32 = jnp.float32
BF16 = jnp.bfloat16

D_MODEL = 1024
DEPTH = 2
CHUNK = 64
N_LEFT = 8
HEAD_DIM = 64
N_HEADS = 8
WIDTH = 512
POOL_WINDOWS = (2, 4, 8, 16)
GROUP_DIM = 128
MAX_REL = 2 * CHUNK
REL_TABLE = MAX_REL + CHUNK
D_FF = 2816
EPS = 1e-6
QK_SCALE = 0.125
GATE_COL0 = 7 * WIDTH

ADAM_LR = 0.001
ADAM_B1 = 0.9
ADAM_B2 = 0.999
ADAM_EPS = 1e-08
ADAM_WD = 0.01
ADAM_STEP = 10

VMEM_LIMIT = 56 * 1024 * 1024
ATT_Q = 256
A_Q = 256
A_WIN = A_Q + N_LEFT * CHUNK
A_FWD_HEADS = 4
HALO = 16
CONV_HALO = 8
NEG = -1e30

MESH_AXES = ("x", "y", "c")
MESH_T = pl.DeviceIdType.MESH


def _cp(sem=None, vmem=VMEM_LIMIT):
    return pltpu.CompilerParams(dimension_semantics=sem, vmem_limit_bytes=vmem)


def _dot(a, b, ca, cb):
    return lax.dot_general(a, b, (((ca,), (cb,)), ((), ())), preferred_element_type=F32)


def _tile(n, cands=(512, 256, 128)):
    for c in cands:
        if n % c == 0:
            return c
    return n


def _split_hi_lo(v):
    hi = v.astype(BF16)
    lo = (v - hi.astype(F32)).astype(BF16)
    return hi, lo


def matmul(a, b, *, ta=False, tb=False, add=None, norm_bwd=None, out_dtype=F32, b_layer=None, behind=None, name):
    plan, sources = behind if behind is not None else (None, [])
    n_src = len(sources)
    n_dst = len(plan.out_shapes()) if plan is not None else 0
    assert add is None or norm_bwd is None
    if ta:
        K, M = a.shape
    else:
        M, K = a.shape
    if tb:
        N, K2 = b.shape[-2:]
    else:
        K2, N = b.shape[-2:]
    assert K == K2, (a.shape, b.shape, ta, tb)
    big = (1024, 1408, 512, 256, 128)
    tm = _tile(M, big)
    tn = _tile(N, (1664,) + big)
    tk = _tile(K, big if ta else (1664, 1408) + big if norm_bwd is not None else (3328, 2816) + big)
    nk = K // tk

    n_extra = 1 if add is not None else 3 if norm_bwd is not None else 0
    n_in = 2 + n_extra + n_src
    n_out = 2 if norm_bwd is not None else 1
    grid = (M // tm, N // tn, nk)
    assert norm_bwd is None or tn == N, "the norm gradient needs whole rows in one output tile"

    def body(*refs):
        a_ref, b_ref = refs[:2]
        extra = refs[2:2 + n_extra]
        o_ref, acc = refs[n_in], refs[n_in + n_out + n_dst]
        i, j, k = pl.program_id(0), pl.program_id(1), pl.program_id(2)
        if plan is not None:
            comm = (refs[n_in - n_src:n_in], refs[n_in + n_out:n_in + n_out + n_dst], refs[n_in + n_out + n_dst + 1:])

            @pl.when((i == 0) & (j == 0) & (k == 0))
            def _():
                plan.start(*comm)

        @pl.when(k == 0)
        def _():
            acc[...] = jnp.zeros_like(acc)

        av = a_ref[...].astype(BF16)
        bv = b_ref[...].astype(BF16)
        acc[...] += _dot(av, bv, 0 if ta else 1, 1 if tb else 0)

        @pl.when(k == nk - 1)
        def _():
            r = acc[...]
            if add is not None:
                r = r + extra[0][...].astype(F32)
            if norm_bwd is not None:
                x_ref, g_ref, dres_ref = extra
                dg_ref = refs[n_in + 1]
                xv = x_ref[...]
                inv = lax.rsqrt(jnp.mean(xv * xv, axis=-1, keepdims=True) + EPS)
                gd = r * g_ref[...]
                mean = jnp.mean(xv * gd, axis=-1, keepdims=True)

                @pl.when(i == 0)
                def _():
                    dg_ref[...] = jnp.zeros_like(dg_ref)

                dg_ref[...] += jnp.sum(r * xv * inv, axis=0, keepdims=True)
                r = dres_ref[...] + inv * gd - xv * (inv * inv * inv * mean)
            o_ref[...] = r.astype(out_dtype)

        if plan is not None:
            @pl.when((i == grid[0] - 1) & (j == grid[1] - 1) & (k == nk - 1))
            def _():
                plan.finish(*comm)

    a_spec = pl.BlockSpec((tk, tm), lambda i, j, k: (k, i)) if ta else pl.BlockSpec((tm, tk), lambda i, j, k: (i, k))
    if b_layer is None:
        b_spec = pl.BlockSpec((tn, tk), lambda i, j, k: (j, k)) if tb else pl.BlockSpec((tk, tn), lambda i, j, k: (k, j))
    elif tb:
        b_spec = pl.BlockSpec((None, tn, tk), lambda i, j, k: (b_layer, j, k))
    else:
        b_spec = pl.BlockSpec((None, tk, tn), lambda i, j, k: (b_layer, k, j))
    o_spec = pl.BlockSpec((tm, tn), lambda i, j, k: (i, j))
    vec_spec = pl.BlockSpec((1, tn), lambda i, j, k: (0, j))
    in_specs = [a_spec, b_spec]
    args = [a, b]
    out_shape = [jax.ShapeDtypeStruct((M, N), out_dtype)]
    out_specs = [o_spec]
    if add is not None:
        in_specs.append(o_spec)
        args.append(add)
    if norm_bwd is not None:
        in_specs += [o_spec, vec_spec, o_spec]
        args += list(norm_bwd)
        out_shape.append(jax.ShapeDtypeStruct((1, N), F32))
        out_specs.append(vec_spec)
    sequential = plan is not None or norm_bwd is not None
    any_space = pl.BlockSpec(memory_space=pl.ANY)
    outs = pl.pallas_call(
        body,
        out_shape=tuple(out_shape) + (tuple(plan.out_shapes()) if plan is not None else ()),
        grid=grid,
        in_specs=in_specs + [any_space] * n_src,
        out_specs=tuple(out_specs) + tuple([any_space] * n_dst),
        scratch_shapes=[pltpu.VMEM((tm, tn), F32)] + (list(plan.scratch_shapes()) if plan is not None else []),
        compiler_params=_cp(("arbitrary" if sequential else "parallel",) * 2 + ("arbitrary",)),
        name=name,
    )(*args, *sources)
    if plan is None:
        return outs[0] if n_out == 1 else tuple(outs)
    return tuple(outs[:n_out]) + (list(outs[n_out:]),) if n_out > 1 else (outs[0], list(outs[1:]))


def rmsnorm_fwd(x, g, *, name):
    S, D = x.shape
    T = _tile(S)

    def body(x_ref, g_ref, h_ref):
        xv = x_ref[...]
        r = lax.rsqrt(jnp.mean(xv * xv, axis=-1, keepdims=True) + EPS)
        h_ref[...] = (xv * r * g_ref[...]).astype(BF16)

    return pl.pallas_call(
        body,
        out_shape=jax.ShapeDtypeStruct((S, D), BF16),
        grid=(S // T,),
        in_specs=[pl.BlockSpec((T, D), lambda i: (i, 0)), pl.BlockSpec((1, D), lambda i: (0, 0))],
        out_specs=pl.BlockSpec((T, D), lambda i: (i, 0)),
        compiler_params=_cp(("parallel",)),
        name=name,
    )(x, g)


def _head_mean_matrix():
    r = lax.broadcasted_iota(jnp.int32, (WIDTH, WIDTH), 0) // HEAD_DIM
    c = lax.broadcasted_iota(jnp.int32, (WIDTH, WIDTH), 1) // HEAD_DIM
    return jnp.where(r == c, 1.0 / HEAD_DIM, 0.0).astype(BF16)


def _head_mean(v, mm):
    hi, lo = _split_hi_lo(v)
    return _dot(hi, mm, 1, 0) + _dot(lo, mm, 1, 0)


def qkv_prep(proj, gq, gk, *, name):
    S = proj.shape[0]
    T = _tile(S)

    def body(qa, ka, va, qb, kb, vb, gq_ref, gk_ref, oqa, oka, ova, oqb, okb, ovb):
        mm = _head_mean_matrix()
        for src, gref, dst, scale in ((qa, gq_ref, oqa, QK_SCALE), (ka, gk_ref, oka, 1.0)):
            v = src[...]
            r = lax.rsqrt(_head_mean(v * v, mm) + EPS)
            dst[...] = (v * r * gref[...] * scale).astype(BF16)
        oqb[...] = (qb[...] * QK_SCALE).astype(BF16)
        for src, dst in ((va, ova), (kb, okb), (vb, ovb)):
            dst[...] = src[...].astype(BF16)

    col = lambda j: pl.BlockSpec((T, WIDTH), lambda i, j=j: (i, j))
    vec = pl.BlockSpec((1, WIDTH), lambda i: (0, 0))
    out = pl.BlockSpec((T, WIDTH), lambda i: (i, 0))
    return pl.pallas_call(
        body,
        out_shape=tuple(jax.ShapeDtypeStruct((S, WIDTH), BF16) for _ in range(6)),
        grid=(S // T,),
        in_specs=[col(0), col(1), col(2), col(3), col(4), col(5), vec, vec],
        out_specs=tuple(out for _ in range(6)),
        compiler_params=_cp(("parallel",)),
        name=name,
    )(proj, proj, proj, proj, proj, proj, gq, gk)


def qknorm_bwd(proj, gq, gk, dqh, dkh, *, name):
    S = proj.shape[0]
    T = _tile(S)

    def body(qa, ka, gq_ref, gk_ref, dq_ref, dk_ref, oq, ok, ogq, ogk):
        i = pl.program_id(0)
        mm = _head_mean_matrix()

        @pl.when(i == 0)
        def _():
            ogq[...] = jnp.zeros_like(ogq)
            ogk[...] = jnp.zeros_like(ogk)

        for src, gref, dref, dst, gdst in ((qa, gq_ref, dq_ref, oq, ogq), (ka, gk_ref, dk_ref, ok, ogk)):
            v = src[...]
            dy = dref[...]
            r = lax.rsqrt(_head_mean(v * v, mm) + EPS)
            gd = dy * gref[...]
            m = _head_mean(v * gd, mm)
            dst[...] = (r * gd - v * (r * r * r * m)).astype(BF16)
            gdst[...] += jnp.sum(dy * v * r, axis=0, keepdims=True)

    col = lambda j: pl.BlockSpec((T, WIDTH), lambda i, j=j: (i, j))
    vec = pl.BlockSpec((1, WIDTH), lambda i: (0, 0))
    row = pl.BlockSpec((T, WIDTH), lambda i: (i, 0))
    return pl.pallas_call(
        body,
        out_shape=(jax.ShapeDtypeStruct((S, WIDTH), BF16), jax.ShapeDtypeStruct((S, WIDTH), BF16),
                   jax.ShapeDtypeStruct((1, WIDTH), F32), jax.ShapeDtypeStruct((1, WIDTH), F32)),
        grid=(S // T,),
        in_specs=[col(0), col(1), vec, vec, row, row],
        out_specs=(row, row, vec, vec),
        compiler_params=_cp(("arbitrary",)),
        name=name,
    )(proj, proj, gq, gk, dqh, dkh)


DIAG_W = 1024
N_VARIANTS = N_LEFT * CHUNK // A_Q + 1


def diagonal_onehot():
    jj = np.arange(DIAG_W)
    diff = np.where(jj < A_WIN, jj, jj - DIAG_W)
    out = np.zeros((N_VARIANTS, DIAG_W, REL_TABLE), np.float32)
    for v in range(N_VARIANTS):
        rel = np.clip(A_Q * v - diff, -(CHUNK - 1), MAX_REL) + (CHUNK - 1)
        out[v, jj, rel] = 1.0
    return out.reshape(N_VARIANTS * DIAG_W, REL_TABLE)


def exact_dot(a, b, *, name):
    def body(a_ref, b_ref, o_ref):
        o_ref[...] = jnp.dot(a_ref[...], b_ref[...], precision=lax.Precision.HIGHEST, preferred_element_type=F32)

    return pl.pallas_call(body, out_shape=jax.ShapeDtypeStruct((a.shape[0], b.shape[1]), F32),
                          compiler_params=_cp(), name=name)(a, b)


def _band_valid(v):
    qc = (lax.broadcasted_iota(jnp.int32, (A_Q, A_WIN), 0) + A_Q * v) // CHUNK
    kc = lax.broadcasted_iota(jnp.int32, (A_Q, A_WIN), 1) // CHUNK
    return (kc <= qc) & (kc >= qc - N_LEFT)


def bias_expand(diag, *, name):
    def body(d_ref, o_ref):
        rows = jnp.broadcast_to(d_ref[0, 0], (A_Q, DIAG_W))
        skew = pltpu.roll(rows, 0, 1, stride=1, stride_axis=0)
        o_ref[0, 0] = jnp.where(_band_valid(pl.program_id(0)), skew[:, :A_WIN], NEG)

    return pl.pallas_call(
        body,
        out_shape=jax.ShapeDtypeStruct((N_VARIANTS, N_HEADS, A_Q, A_WIN), F32),
        grid=(N_VARIANTS, N_HEADS),
        in_specs=[pl.BlockSpec((1, 1, 1, DIAG_W), lambda v, h: (v, h, 0, 0))],
        out_specs=pl.BlockSpec((1, 1, A_Q, A_WIN), lambda v, h: (v, h, 0, 0)),
        compiler_params=_cp(("parallel", "parallel")),
        name=name,
    )(diag)


def relbias_reduce(dbias, *, name):
    def body(db_ref, o_ref):
        acc = None
        for a in range(A_Q // 8):
            x = jnp.concatenate([db_ref[0, 0, 8 * a:8 * a + 8, :], jnp.zeros((8, DIAG_W - A_WIN), F32)], axis=1)
            x = pltpu.roll(x, DIAG_W - 8 * a, 1) if a else x
            acc = x if acc is None else acc + x
        row = lax.broadcasted_iota(jnp.int32, (8, DIAG_W), 0)
        for b in range(3):
            acc = jnp.where((row >> b) & 1 == 1, pltpu.roll(acc, DIAG_W - (1 << b), 1), acc)
        o_ref[0, 0] = jnp.sum(acc, axis=0, keepdims=True)

    return pl.pallas_call(
        body,
        out_shape=jax.ShapeDtypeStruct((N_VARIANTS, N_HEADS, 1, DIAG_W), F32),
        grid=(N_VARIANTS, N_HEADS),
        in_specs=[pl.BlockSpec((1, 1, A_Q, A_WIN), lambda v, h: (v, h, 0, 0))],
        out_specs=pl.BlockSpec((1, 1, 1, DIAG_W), lambda v, h: (v, h, 0, 0)),
        compiler_params=_cp(("parallel", "parallel")),
        name=name,
    )(dbias)


def _a_window_start(qb):
    return pl.multiple_of(jnp.maximum(qb * A_Q - N_LEFT * CHUNK, 0), A_Q)


def attn_a_fwd(q, k, v, biasm, *, name):
    S = q.shape[0]
    nq = S // A_Q

    def body(q_ref, k_ref, v_ref, b_ref, o_ref):
        qb = pl.program_id(1)
        start = _a_window_start(qb)
        outs = []
        for h in range(A_FWD_HEADS):
            lanes = slice(h * HEAD_DIM, (h + 1) * HEAD_DIM)
            qh = q_ref[:, lanes]
            kw = k_ref[pl.ds(start, A_WIN), lanes]
            vw = v_ref[pl.ds(start, A_WIN), lanes]
            s = _dot(qh, kw, 1, 1) + b_ref[0, h]
            m = jnp.max(s, axis=-1, keepdims=True)
            e = jnp.exp(s - m)
            outs.append(_dot(e.astype(BF16), vw, 1, 0) * (1.0 / jnp.sum(e, axis=-1, keepdims=True)))
        o_ref[...] = jnp.concatenate(outs, axis=1).astype(BF16)

    qspec = pl.BlockSpec((A_Q, A_FWD_HEADS * HEAD_DIM), lambda hp, qb: (qb, hp))
    kvspec = pl.BlockSpec((S, A_FWD_HEADS * HEAD_DIM), lambda hp, qb: (0, hp))
    bspec = pl.BlockSpec((1, A_FWD_HEADS, A_Q, A_WIN), lambda hp, qb: (jnp.minimum(qb, N_VARIANTS - 1), hp, 0, 0))
    return pl.pallas_call(
        body,
        out_shape=jax.ShapeDtypeStruct((S, WIDTH), BF16),
        grid=(N_HEADS // A_FWD_HEADS, nq),
        in_specs=[qspec, kvspec, kvspec, bspec],
        out_specs=qspec,
        compiler_params=_cp(("parallel", "arbitrary")),
        name=name,
    )(q, k, v, biasm)


def attn_a_bwd(q, k, v, biasm, do, *, name):
    S = q.shape[0]
    nq = S // A_Q

    def body(q_ref, k_ref, v_ref, b_ref, do_ref, dq_ref, dk_ref, dv_ref, db_ref):
        qb = pl.program_id(1)
        start = _a_window_start(qb)

        @pl.when(qb == 0)
        def _():
            dk_ref[...] = jnp.zeros_like(dk_ref)
            dv_ref[...] = jnp.zeros_like(dv_ref)

        @pl.when(qb < N_VARIANTS)
        def _():
            db_ref[...] = jnp.zeros_like(db_ref)

        dqs = []
        for h in range(2):
            lanes = slice(h * HEAD_DIM, (h + 1) * HEAD_DIM)
            qh = q_ref[:, lanes]
            doh = do_ref[:, lanes]
            kw = k_ref[pl.ds(start, A_WIN), lanes]
            vw = v_ref[pl.ds(start, A_WIN), lanes]
            s = _dot(qh, kw, 1, 1) + b_ref[0, h]
            m = jnp.max(s, axis=-1, keepdims=True)
            e = jnp.exp(s - m)
            p = e * (1.0 / jnp.sum(e, axis=-1, keepdims=True))
            dp = _dot(doh, vw, 1, 1)
            delta = jnp.sum(p * dp, axis=-1, keepdims=True)
            ds = p * (dp - delta)
            db_ref[0, h] += ds
            dsb = ds.astype(BF16)
            dqs.append(_dot(dsb, kw, 1, 0) * QK_SCALE)
            dk_ref[pl.ds(start, A_WIN), lanes] += _dot(dsb, qh, 0, 0)
            dv_ref[pl.ds(start, A_WIN), lanes] += _dot(p.astype(BF16), doh, 0, 0)
        dq_ref[...] = jnp.concatenate(dqs, axis=1)

    qspec = pl.BlockSpec((A_Q, 2 * HEAD_DIM), lambda hp, qb: (qb, hp))
    kvspec = pl.BlockSpec((S, 2 * HEAD_DIM), lambda hp, qb: (0, hp))
    bspec = pl.BlockSpec((1, 2, A_Q, A_WIN), lambda hp, qb: (jnp.minimum(qb, N_VARIANTS - 1), hp, 0, 0))
    return pl.pallas_call(
        body,
        out_shape=(jax.ShapeDtypeStruct((S, WIDTH), F32), jax.ShapeDtypeStruct((S, WIDTH), F32),
                   jax.ShapeDtypeStruct((S, WIDTH), F32), jax.ShapeDtypeStruct((N_VARIANTS, N_HEADS, A_Q, A_WIN), F32)),
        grid=(N_HEADS // 2, nq),
        in_specs=[qspec, kvspec, kvspec, bspec, qspec],
        out_specs=(qspec, kvspec, kvspec, bspec),
        compiler_params=_cp(("parallel", "arbitrary")),
        name=name,
    )(q, k, v, biasm, do)


def _tri(kind):
    j = lax.broadcasted_iota(jnp.int32, (ATT_Q, ATT_Q), 0)
    s = lax.broadcasted_iota(jnp.int32, (ATT_Q, ATT_Q), 1)
    if kind == "gt":
        m = j > s
    elif kind == "le":
        m = j <= s
    else:
        m = j < s
    return jnp.where(m, 1.0, 0.0).astype(BF16)


def _cum(v, tri):
    hi, lo = _split_hi_lo(v)
    return _dot(hi, tri, 1, 0) + _dot(lo, tri, 1, 0)


def _log_sigmoids(z, mask):
    t = jnp.log(1.0 + jnp.exp(-jnp.abs(z)))
    keep = -(jnp.maximum(z, 0.0) + t)
    take = jnp.minimum(z, 0.0) - t
    return (keep if mask is None else jnp.where(mask, keep, 0.0)), take


def _strictly_before():
    row = lax.broadcasted_iota(jnp.int32, (ATT_Q, ATT_Q), 0)
    col = lax.broadcasted_iota(jnp.int32, (ATT_Q, ATT_Q), 1)
    return col < row


EXIT_LOG = -104.0


def attn_b_fwd(q, k, v, *, gather=None, name):
    S = q.shape[0]
    nq = S // ATT_Q
    plan, shards, fulls = gather if gather is not None else (None, [], [])
    ng = len(shards)

    def body(q_ref, k_ref, v_ref, *rest):
        hp = pl.program_id(0)
        qb = pl.program_id(1)
        o_ref, t_ref, n_ref = rest[2 * ng:2 * ng + 3]
        if plan is not None:
            comm = (rest[:ng], rest[2 * ng + 3:3 * ng + 3], rest[3 * ng + 3:])

            @pl.when(jnp.logical_and(hp == 0, qb == 0))
            def _():
                plan.start(*comm)

        tri = _tri("gt")

        def block(kb, carry, mask):
            ks = pl.multiple_of(kb * ATT_Q, ATT_Q)
            new = []
            for h in range(2):
                lanes = slice(h * HEAD_DIM, (h + 1) * HEAD_DIM)
                c, acc = carry[h]
                z = _dot(q_ref[:, lanes], k_ref[pl.ds(ks, ATT_Q), lanes], 1, 1)
                keep, take = _log_sigmoids(z, mask)
                w = jnp.exp(take + (_cum(keep, tri) + c))
                if mask is not None:
                    w = jnp.where(mask, w, 0.0)
                acc = acc + _dot(w.astype(BF16), v_ref[pl.ds(ks, ATT_Q), lanes], 1, 0)
                c = c + jnp.sum(keep, axis=-1, keepdims=True)
                new.append((c, acc))
            return tuple(new)

        def cond(state):
            it, cmax, _ = state
            return jnp.logical_and(it <= qb, cmax >= EXIT_LOG)

        def step(state):
            it, _, carry = state
            carry = block(qb - it, carry, None)
            return it + 1, jnp.max(jnp.maximum(carry[0][0], carry[1][0])), carry

        init = tuple((jnp.zeros((ATT_Q, 1), F32), jnp.zeros((ATT_Q, HEAD_DIM), F32)) for _ in range(2))
        diag = block(qb, init, _strictly_before())
        visited, _, res = lax.while_loop(cond, step, (jnp.int32(1), jnp.float32(0.0), diag))
        o_ref[...] = jnp.concatenate([res[0][1], res[1][1]], axis=1).astype(BF16)
        t_ref[...] = jnp.concatenate([jnp.broadcast_to(res[h][0], (ATT_Q, HEAD_DIM)) for h in range(2)], axis=1)
        n_ref[hp, qb] = visited.astype(F32)
        if plan is not None:
            @pl.when(jnp.logical_and(hp == N_HEADS // 2 - 1, qb == nq - 1))
            def _():
                plan.finish(*comm)

    qspec = pl.BlockSpec((ATT_Q, 2 * HEAD_DIM), lambda hp, qb: (qb, hp))
    kvspec = pl.BlockSpec((S, 2 * HEAD_DIM), lambda hp, qb: (0, hp))
    outs = pl.pallas_call(
        body,
        out_shape=(jax.ShapeDtypeStruct((S, WIDTH), BF16), jax.ShapeDtypeStruct((S, WIDTH), F32),
                   jax.ShapeDtypeStruct((N_HEADS // 2, nq), F32))
        + tuple(jax.ShapeDtypeStruct(f.shape, f.dtype) for f in fulls),
        grid=(N_HEADS // 2, nq),
        in_specs=[qspec, kvspec, kvspec] + [ANY] * (2 * ng),
        out_specs=(qspec, qspec, pl.BlockSpec(memory_space=pltpu.SMEM)) + tuple([ANY] * ng),
        scratch_shapes=plan.scratch_shapes() if plan is not None else (),
        input_output_aliases={3 + ng + i: 3 + i for i in range(ng)},
        compiler_params=_cp(("arbitrary", "arbitrary")),
        name=name,
    )(q, k, v, *shards, *fulls)
    return outs if plan is None else (outs[0], outs[1], outs[2], list(outs[3:]))


def attn_b_bwd(q, k, v, tot, nblk, do, *, scatter=None, name):
    S = q.shape[0]
    nq = S // ATT_Q
    plan, pairs16 = scatter if scatter is not None else (None, [])
    ns = len(pairs16)

    def body(q_ref, k_ref, v_ref, t_ref, n_ref, do_ref, *rest):
        hp = pl.program_id(0)
        qb = pl.program_id(1)
        dq_ref, dk_ref, dv_ref = rest[ns:ns + 3]
        if plan is not None:
            comm = (rest[:ns], rest[ns + 3:2 * ns + 3], rest[2 * ns + 3:])

            @pl.when(jnp.logical_and(hp == 0, qb == 0))
            def _():
                plan.start(*comm)

        first = jnp.clip(qb + 1 - n_ref[hp, qb].astype(jnp.int32), 0, qb + 1)
        tri_le = _tri("le")
        tri_lt = _tri("lt")

        @pl.when(qb == 0)
        def _():
            dk_ref[...] = jnp.zeros_like(dk_ref)
            dv_ref[...] = jnp.zeros_like(dv_ref)

        def block(kb, carry, mask):
            ks = pl.multiple_of(kb * ATT_Q, ATT_Q)
            new = []
            for h in range(2):
                lanes = slice(h * HEAD_DIM, (h + 1) * HEAD_DIM)
                cl, cg, dq = carry[h]
                qh = q_ref[:, lanes]
                doh = do_ref[:, lanes]
                kh = k_ref[pl.ds(ks, ATT_Q), lanes]
                vh = v_ref[pl.ds(ks, ATT_Q), lanes]
                totl = t_ref[:, h * HEAD_DIM:h * HEAD_DIM + 1]
                z = _dot(qh, kh, 1, 1)
                keep, take = _log_sigmoids(z, mask)
                sig = jnp.exp(take)
                w = sig * jnp.exp((totl - cl) - _cum(keep, tri_le))
                if mask is not None:
                    w = jnp.where(mask, w, 0.0)
                g = w * _dot(doh, vh, 1, 1)
                G = _dot(g.astype(BF16), tri_lt, 1, 0) + cg
                dz = g * (1.0 - sig) - sig * G
                if mask is not None:
                    dz = jnp.where(mask, dz, 0.0)
                dz = dz.astype(BF16)
                dq = dq + _dot(dz, kh, 1, 0)
                dk_ref[pl.ds(ks, ATT_Q), lanes] += _dot(dz, qh, 0, 0)
                dv_ref[pl.ds(ks, ATT_Q), lanes] += _dot(w.astype(BF16), doh, 0, 0)
                cl = cl + jnp.sum(keep, axis=-1, keepdims=True)
                cg = cg + jnp.sum(g, axis=-1, keepdims=True)
                new.append((cl, cg, dq))
            return tuple(new)

        init = tuple((jnp.zeros((ATT_Q, 1), F32), jnp.zeros((ATT_Q, 1), F32), jnp.zeros((ATT_Q, HEAD_DIM), F32))
                     for _ in range(2))
        res = lax.fori_loop(jnp.minimum(first, qb), qb, lambda kb, carry: block(kb, carry, None), init)
        res = block(qb, res, _strictly_before())
        dq_ref[...] = (jnp.concatenate([res[0][2], res[1][2]], axis=1) * QK_SCALE).astype(BF16)
        if plan is not None:
            @pl.when(jnp.logical_and(hp == N_HEADS // 2 - 1, qb == nq - 1))
            def _():
                plan.finish(*comm)

    qspec = pl.BlockSpec((ATT_Q, 2 * HEAD_DIM), lambda hp, qb: (qb, hp))
    kvspec = pl.BlockSpec((S, 2 * HEAD_DIM), lambda hp, qb: (0, hp))
    outs = pl.pallas_call(
        body,
        out_shape=(jax.ShapeDtypeStruct((S, WIDTH), BF16), jax.ShapeDtypeStruct((S, WIDTH), F32),
                   jax.ShapeDtypeStruct((S, WIDTH), F32)) + (tuple(plan.out_shapes()) if plan is not None else ()),
        grid=(N_HEADS // 2, nq),
        in_specs=[qspec, kvspec, kvspec, qspec, pl.BlockSpec(memory_space=pltpu.SMEM), qspec] + [ANY] * ns,
        out_specs=(qspec, kvspec, kvspec) + tuple([ANY] * ns),
        scratch_shapes=plan.scratch_shapes() if plan is not None else (),
        compiler_params=_cp(("arbitrary", "arbitrary")),
        name=name,
    )(q, k, v, tot, nblk, do, *pairs16)
    return outs if plan is None else (outs[0], outs[1], outs[2], list(outs[3:]))


U_COLBLK = 6


def _pool_counts(t0, rows):
    t = t0 + lax.broadcasted_iota(jnp.int32, (rows, WIDTH), 0)
    lane_grp = lax.broadcasted_iota(jnp.int32, (rows, WIDTH), 1) // GROUP_DIM
    w2, w4, w8, w16 = POOL_WINDOWS
    win = jnp.where(lane_grp == 0, w2, jnp.where(lane_grp == 1, w4, jnp.where(lane_grp == 2, w8, w16)))
    cnt = jnp.minimum(t + 1, win)
    return 1.0 / cnt.astype(F32), lane_grp


def _window_sums(ext, shift_fn):
    s2 = ext + shift_fn(ext, 1)
    s4 = s2 + shift_fn(s2, 2)
    s8 = s4 + shift_fn(s4, 4)
    s16 = s8 + shift_fn(s8, 8)
    return s2, s4, s8, s16


def _select_group(lane_grp, s2, s4, s8, s16):
    return jnp.where(lane_grp == 0, s2, jnp.where(lane_grp == 1, s4, jnp.where(lane_grp == 2, s8, s16)))


def _pooled_tile(u_ref, h_ref, i, T):
    halo = jnp.where(i > 0, h_ref[...], 0.0)
    ext = jnp.concatenate([halo, u_ref[...]], axis=0)
    n = T + HALO
    sums = _window_sums(ext, lambda v, k: pltpu.roll(v, k, 0))
    inv, lane_grp = _pool_counts(i * T - HALO, n)
    pooled = _select_group(lane_grp, *sums) * inv - ext
    return pooled[HALO:, :]


def pool_fwd(proj, w_pool, scale, *, name):
    S = proj.shape[0]
    T = _tile(S)
    hb = T // HALO

    def body(u_ref, h_ref, w_ref, s_ref, o_ref):
        i = pl.program_id(0)
        pooled = _pooled_tile(u_ref, h_ref, i, T).astype(BF16)
        outs = [_dot(pooled[:, g * GROUP_DIM:(g + 1) * GROUP_DIM], w_ref[g], 1, 0) for g in range(4)]
        o_ref[...] = (jnp.concatenate(outs, axis=1) * s_ref[...]).astype(BF16)

    return pl.pallas_call(
        body,
        out_shape=jax.ShapeDtypeStruct((S, WIDTH), BF16),
        grid=(S // T,),
        in_specs=[pl.BlockSpec((T, WIDTH), lambda i: (i, U_COLBLK)),
                  pl.BlockSpec((HALO, WIDTH), lambda i: (jnp.maximum(i * hb - 1, 0), U_COLBLK)),
                  pl.BlockSpec((4, GROUP_DIM, GROUP_DIM), lambda i: (0, 0, 0)),
                  pl.BlockSpec((1, WIDTH), lambda i: (0, 0))],
        out_specs=pl.BlockSpec((T, WIDTH), lambda i: (i, 0)),
        compiler_params=_cp(("parallel",)),
        name=name,
    )(proj, proj, w_pool, scale)


def pool_bwd(proj, w_pool, scale, do, *, name):
    S = proj.shape[0]
    T = _tile(S)
    hb = T // HALO
    nt = S // T

    def body(u_ref, h_ref, w_ref, s_ref, do_ref, dof_ref, du_ref, dw_ref, ds_ref):
        i = pl.program_id(0)

        @pl.when(i == 0)
        def _():
            dw_ref[...] = jnp.zeros_like(dw_ref)
            ds_ref[...] = jnp.zeros_like(ds_ref)

        pooled = _pooled_tile(u_ref, h_ref, i, T).astype(BF16)
        dov = do_ref[...].astype(F32)
        fut = jnp.where(i < nt - 1, dof_ref[...].astype(F32), 0.0)
        dmix = (jnp.concatenate([dov, fut], axis=0) * s_ref[...]).astype(BF16)
        mixed, dpool = [], []
        for g in range(4):
            lanes = slice(g * GROUP_DIM, (g + 1) * GROUP_DIM)
            mixed.append(_dot(pooled[:, lanes], w_ref[g], 1, 0))
            dw_ref[g] += _dot(pooled[:, lanes], dmix[:T, lanes], 0, 0)
            dpool.append(_dot(dmix[:, lanes], w_ref[g], 1, 1))
        ds_ref[...] += jnp.sum(dov * jnp.concatenate(mixed, axis=1), axis=0, keepdims=True)
        dp = jnp.concatenate(dpool, axis=1)
        n = T + HALO
        inv, lane_grp = _pool_counts(i * T, n)
        sums = _window_sums(dp * inv, lambda v, k: pltpu.roll(v, n - k, 0))
        du = _select_group(lane_grp, *sums) - dp
        du_ref[...] = du[:T, :].astype(BF16)

    row = pl.BlockSpec((T, WIDTH), lambda i: (i, 0))
    return pl.pallas_call(
        body,
        out_shape=(jax.ShapeDtypeStruct((S, WIDTH), BF16), jax.ShapeDtypeStruct((4, GROUP_DIM, GROUP_DIM), F32),
                   jax.ShapeDtypeStruct((1, WIDTH), F32)),
        grid=(nt,),
        in_specs=[pl.BlockSpec((T, WIDTH), lambda i: (i, U_COLBLK)),
                  pl.BlockSpec((HALO, WIDTH), lambda i: (jnp.maximum(i * hb - 1, 0), U_COLBLK)),
                  pl.BlockSpec((4, GROUP_DIM, GROUP_DIM), lambda i: (0, 0, 0)),
                  pl.BlockSpec((1, WIDTH), lambda i: (0, 0)),
                  row,
                  pl.BlockSpec((HALO, WIDTH), lambda i: (jnp.minimum((i + 1) * hb, S // HALO - 1), 0))],
        out_specs=(row, pl.BlockSpec((4, GROUP_DIM, GROUP_DIM), lambda i: (0, 0, 0)),
                   pl.BlockSpec((1, WIDTH), lambda i: (0, 0))),
        compiler_params=_cp(("arbitrary",)),
        name=name,
    )(proj, proj, w_pool, scale, do, do)


GATE_BLK0 = GATE_COL0 // WIDTH


def merge_fwd(oa, ob, oc, proj, b_gate, wa, wb, wc, *, name):
    S = oa.shape[0]
    T = _tile(S)

    def body(oa_ref, ob_ref, oc_ref, ga, gb, gc, ba, bb, bc, wa_ref, wb_ref, wc_ref, m_ref):
        acc = None
        for o_ref, g_ref, b_ref, w_ref in ((oa_ref, ga, ba, wa_ref), (ob_ref, gb, bb, wb_ref), (oc_ref, gc, bc, wc_ref)):
            y = _dot(o_ref[...], w_ref[...], 1, 0)
            t = jax.nn.sigmoid(g_ref[...] + b_ref[...]) * y
            acc = t if acc is None else acc + t
        m_ref[...] = acc.astype(BF16)

    row = pl.BlockSpec((T, WIDTH), lambda i, n: (i, 0))
    gate = lambda b: pl.BlockSpec((T, WIDTH), lambda i, n, b=b: (i, GATE_BLK0 + 2 * b + n))
    bias = lambda b: pl.BlockSpec((1, WIDTH), lambda i, n, b=b: (0, 2 * b + n))
    wspec = pl.BlockSpec((WIDTH, WIDTH), lambda i, n: (0, n))
    return pl.pallas_call(
        body,
        out_shape=jax.ShapeDtypeStruct((S, D_MODEL), BF16),
        grid=(S // T, 2),
        in_specs=[row, row, row, gate(0), gate(1), gate(2), bias(0), bias(1), bias(2), wspec, wspec, wspec],
        out_specs=pl.BlockSpec((T, WIDTH), lambda i, n: (i, n)),
        compiler_params=_cp(("parallel", "parallel")),
        name=name,
    )(oa, ob, oc, proj, proj, proj, b_gate, b_gate, b_gate, wa, wb, wc)


def merge_bwd(dm, oa, ob, oc, proj, b_gate, wa, wb, wc, *, name):
    S = oa.shape[0]
    T = _tile(S)

    def body(dm_ref, oa_ref, ob_ref, oc_ref, ga, gb, gc, ba, bb, bc, wa_ref, wb_ref, wc_ref,
             ta, tb, tc, dga, dgb, dgc, dba, dbb, dbc):
        i = pl.program_id(1)
        dmv = dm_ref[...].astype(F32)
        for o_ref, g_ref, b_ref, w_ref, t_ref, dg_ref, db_ref in (
                (oa_ref, ga, ba, wa_ref, ta, dga, dba), (ob_ref, gb, bb, wb_ref, tb, dgb, dbb),
                (oc_ref, gc, bc, wc_ref, tc, dgc, dbc)):
            y = _dot(o_ref[...], w_ref[...], 1, 0)
            gate = jax.nn.sigmoid(g_ref[...] + b_ref[...])
            t_ref[...] = (gate * dmv).astype(BF16)
            dgl = dmv * y * gate * (1.0 - gate)
            dg_ref[...] = dgl.astype(BF16)

            @pl.when(i == 0)
            def _():
                db_ref[...] = jnp.zeros_like(db_ref)

            db_ref[...] += jnp.sum(dgl, axis=0, keepdims=True)

    row = pl.BlockSpec((T, WIDTH), lambda n, i: (i, 0))
    half = pl.BlockSpec((T, WIDTH), lambda n, i: (i, n))
    gate = lambda b: pl.BlockSpec((T, WIDTH), lambda n, i, b=b: (i, GATE_BLK0 + 2 * b + n))
    bias = lambda b: pl.BlockSpec((1, WIDTH), lambda n, i, b=b: (0, 2 * b + n))
    wspec = pl.BlockSpec((WIDTH, WIDTH), lambda n, i: (0, n))
    bvec = pl.BlockSpec((1, WIDTH), lambda n, i: (0, n))
    act = jax.ShapeDtypeStruct((S, D_MODEL), BF16)
    vec = jax.ShapeDtypeStruct((1, D_MODEL), F32)
    return pl.pallas_call(
        body,
        out_shape=(act, act, act, act, act, act, vec, vec, vec),
        grid=(2, S // T),
        in_specs=[half, row, row, row, gate(0), gate(1), gate(2), bias(0), bias(1), bias(2), wspec, wspec, wspec],
        out_specs=(half, half, half, half, half, half, bvec, bvec, bvec),
        compiler_params=_cp(("parallel", "arbitrary")),
        name=name,
    )(dm, oa, ob, oc, proj, proj, proj, b_gate, b_gate, b_gate, wa, wb, wc)


FF_T = 256
FF_BLKS = D_FF // FF_T


def _silu_parts(x):
    s = jax.nn.sigmoid(x)
    return x * s, s


def _conv3(ext, w_ref, b_ref):
    taps = (pltpu.roll(ext, 2, 0), pltpu.roll(ext, 1, 0), ext)
    return b_ref[...] + w_ref[0:1, :] * taps[0] + w_ref[1:2, :] * taps[1] + w_ref[2:3, :] * taps[2], taps


def conv_glu_fwd(u, conv_w, conv_b, *, name):
    S = u.shape[0]
    T = _tile(S)
    hb = T // CONV_HALO

    def body(ug, ugh, uv, uvh, wg, wv, bg, bv, a_ref):
        i = pl.program_id(1)
        cs = []
        for m_ref, h_ref, w_ref, b_ref in ((ug, ugh, wg, bg), (uv, uvh, wv, bv)):
            halo = jnp.where(i > 0, h_ref[...], 0.0)
            ext = jnp.concatenate([halo, m_ref[...]], axis=0)
            cs.append(_conv3(ext, w_ref, b_ref)[0][CONV_HALO:, :])
        act, _ = _silu_parts(cs[0])
        a_ref[...] = (act * cs[1]).astype(BF16)

    main = lambda o: pl.BlockSpec((T, FF_T), lambda c, i, o=o: (i, c + o))
    halo = lambda o: pl.BlockSpec((CONV_HALO, FF_T), lambda c, i, o=o: (jnp.maximum(i * hb - 1, 0), c + o))
    wsp = lambda o: pl.BlockSpec((3, FF_T), lambda c, i, o=o: (0, c + o))
    bsp = lambda o: pl.BlockSpec((1, FF_T), lambda c, i, o=o: (0, c + o))
    return pl.pallas_call(
        body,
        out_shape=jax.ShapeDtypeStruct((S, D_FF), BF16),
        grid=(FF_BLKS, S // T),
        in_specs=[main(0), halo(0), main(FF_BLKS), halo(FF_BLKS), wsp(0), wsp(FF_BLKS), bsp(0), bsp(FF_BLKS)],
        out_specs=pl.BlockSpec((T, FF_T), lambda c, i: (i, c)),
        compiler_params=_cp(("parallel", "parallel")),
        name=name,
    )(u, u, u, u, conv_w, conv_w, conv_b, conv_b)


def conv_glu_bwd(u, conv_w, conv_b, da, *, name):
    S = u.shape[0]
    T = _tile(S)
    hb = T // CONV_HALO
    nt = S // T
    n = T + 2 * CONV_HALO

    def body(ug, ugp, ugf, uv, uvp, uvf, wg, wv, bg, bv, da_ref, daf_ref,
             dug, duv, dwg, dwv, dbg, dbv):
        i = pl.program_id(1)
        first, last = i == 0, i == nt - 1
        taps, cs = [], []
        for m_ref, p_ref, f_ref, w_ref, b_ref in ((ug, ugp, ugf, wg, bg), (uv, uvp, uvf, wv, bv)):
            ext = jnp.concatenate([jnp.where(first, 0.0, p_ref[...]), m_ref[...], jnp.where(last, 0.0, f_ref[...])], axis=0)
            c, tp = _conv3(ext, w_ref, b_ref)
            cs.append(c)
            taps.append(tp)
        dae = jnp.concatenate([jnp.zeros((CONV_HALO, FF_T), F32), da_ref[...].astype(F32),
                               jnp.where(last, 0.0, daf_ref[...].astype(F32))], axis=0)
        act, sg = _silu_parts(cs[0])
        dcs = (dae * cs[1] * (sg * (1.0 + cs[0] * (1.0 - sg))), dae * act)
        main = slice(CONV_HALO, CONV_HALO + T)
        for tp, dc, w_ref, du_ref, dw_ref, db_ref in ((taps[0], dcs[0], wg, dug, dwg, dbg),
                                                      (taps[1], dcs[1], wv, duv, dwv, dbv)):
            du = (w_ref[2:3, :] * dc + w_ref[1:2, :] * pltpu.roll(dc, n - 1, 0) + w_ref[0:1, :] * pltpu.roll(dc, n - 2, 0))
            du_ref[...] = du[main, :].astype(BF16)
            dcm = dc[main, :]
            rows = [jnp.sum(dcm * tp[j][main, :], axis=0, keepdims=True) for j in range(3)]

            @pl.when(first)
            def _():
                dw_ref[...] = jnp.zeros_like(dw_ref)
                db_ref[...] = jnp.zeros_like(db_ref)

            dw_ref[...] += jnp.concatenate(rows, axis=0)
            db_ref[...] += jnp.sum(dcm, axis=0, keepdims=True)

    main = lambda o: pl.BlockSpec((T, FF_T), lambda c, i, o=o: (i, c + o))
    past = lambda o: pl.BlockSpec((CONV_HALO, FF_T), lambda c, i, o=o: (jnp.maximum(i * hb - 1, 0), c + o))
    fut = lambda o: pl.BlockSpec((CONV_HALO, FF_T), lambda c, i, o=o: (jnp.minimum((i + 1) * hb, S // CONV_HALO - 1), c + o))
    wsp = lambda o: pl.BlockSpec((3, FF_T), lambda c, i, o=o: (0, c + o))
    bsp = lambda o: pl.BlockSpec((1, FF_T), lambda c, i, o=o: (0, c + o))
    return pl.pallas_call(
        body,
        out_shape=(jax.ShapeDtypeStruct((S, D_FF), BF16), jax.ShapeDtypeStruct((S, D_FF), BF16),
                   jax.ShapeDtypeStruct((3, D_FF), F32), jax.ShapeDtypeStruct((3, D_FF), F32),
                   jax.ShapeDtypeStruct((1, D_FF), F32), jax.ShapeDtypeStruct((1, D_FF), F32)),
        grid=(FF_BLKS, nt),
        in_specs=[main(0), past(0), fut(0), main(FF_BLKS), past(FF_BLKS), fut(FF_BLKS),
                  wsp(0), wsp(FF_BLKS), bsp(0), bsp(FF_BLKS), main(0), fut(0)],
        out_specs=(main(0), main(0), wsp(0), wsp(0), bsp(0), bsp(0)),
        compiler_params=_cp(("parallel", "arbitrary")),
        name=name,
    )(u, u, u, u, u, u, conv_w, conv_w, conv_b, conv_b, da, da)


def loss_head(y, target, *, name):
    S, D = y.shape
    T = _tile(S)

    def body(y_ref, t_ref, dy_ref, l_ref):
        i = pl.program_id(0)
        err = y_ref[...] - t_ref[...]
        dy_ref[...] = err * (1.0 / D)

        @pl.when(i == 0)
        def _():
            l_ref[...] = jnp.zeros_like(l_ref)

        l_ref[...] += 0.5 * jnp.sum(jnp.mean(err * err, axis=-1, keepdims=True))

    row = pl.BlockSpec((T, D), lambda i: (i, 0))
    return pl.pallas_call(
        body,
        out_shape=(jax.ShapeDtypeStruct((S, D), F32), jax.ShapeDtypeStruct((8, 128), F32)),
        grid=(S // T,),
        in_specs=[row, row],
        out_specs=(row, pl.BlockSpec((8, 128), lambda i: (0, 0))),
        compiler_params=_cp(("arbitrary",)),
        name=name,
    )(y, target)


ELEMS_PER_BLOCK = 256 * 1024


def _rows_tile(rows, cols):
    if rows * cols <= ELEMS_PER_BLOCK or rows % 8:
        return rows
    best = 8
    for tr in range(8, rows + 1, 8):
        if rows % tr == 0 and tr * cols <= ELEMS_PER_BLOCK:
            best = tr
    return best


def _adamw_math(g, w_ref, m_ref, v_ref, g_out, d_out, m_out, v_out):
    mn = ADAM_B1 * m_ref[...] + (1.0 - ADAM_B1) * g
    vn = ADAM_B2 * v_ref[...] + (1.0 - ADAM_B2) * (g * g)
    m_hat = mn / (1.0 - ADAM_B1 ** ADAM_STEP)
    v_hat = vn / (1.0 - ADAM_B2 ** ADAM_STEP)
    g_out[...] = g
    d_out[...] = -ADAM_LR * (m_hat / (jnp.sqrt(v_hat) + ADAM_EPS) + ADAM_WD * w_ref[...])
    m_out[...] = mn
    v_out[...] = vn


def adamw(w, m, v, g, *, name):
    rows, cols = w.shape
    tr = _rows_tile(rows, cols)

    def body(w_ref, m_ref, v_ref, g_ref, g_out, d_out, m_out, v_out):
        _adamw_math(g_ref[...], w_ref, m_ref, v_ref, g_out, d_out, m_out, v_out)

    spec = pl.BlockSpec((tr, cols), lambda i: (i, 0))
    shp = jax.ShapeDtypeStruct((rows, cols), F32)
    return pl.pallas_call(
        body,
        out_shape=(shp, shp, shp, shp),
        grid=(rows // tr,),
        in_specs=[spec] * 4,
        out_specs=(spec, spec, spec, spec),
        compiler_params=_cp(("parallel",)),
        name=name,
    )(w, m, v, g)


ANY = pl.BlockSpec(memory_space=pl.ANY)
STAGE_BYTES = 2 * 1024 * 1024


def _mesh_pos():
    return lax.axis_index("x"), lax.axis_index("y"), lax.axis_index("c")


def _chip_peers(x, y):
    return [(1 - x, y), (x, 1 - y), (1 - x, 1 - y)]


def _all_peers(x, y, c):
    return [((1 - x) if (r >> 2) & 1 else x, (1 - y) if (r >> 1) & 1 else y, (1 - c) if r & 1 else c)
            for r in range(1, 8)]


class LayerGather:
    def __init__(self, shards, axes, layer):
        self.nt = len(shards)
        self.axes = list(axes)
        self.layer = layer
        self.shapes = [s.shape for s in shards]
        self.dtypes = [s.dtype for s in shards]
        self.sizes = [s.shape[a] for s, a in zip(shards, axes)]
        self.split = [s.shape[1] % 32 == 0 for s in shards]
        self.half_rows = [s.shape[1] // 2 if sp else s.shape[1] for s, sp in zip(shards, self.split)]
        self.chunk_rows = []
        for s in shards:
            rt = s.shape[1]
            while rt % 32 == 0 and rt * s.shape[2] * s.dtype.itemsize > STAGE_BYTES:
                rt //= 2
            self.chunk_rows.append(rt)

    def out_shapes(self):
        out = []
        for shp, a, sz, dt in zip(self.shapes, self.axes, self.sizes, self.dtypes):
            shp = list(shp)
            shp[a] = 4 * sz
            out.append(jax.ShapeDtypeStruct(tuple(shp), dt))
        return out

    def scratch_shapes(self):
        return ([pltpu.VMEM((1, rt, shp[2]), dt) for shp, rt, dt in zip(self.shapes, self.chunk_rows, self.dtypes)]
                + [pltpu.SemaphoreType.DMA((2 * self.nt,))] + [pltpu.SemaphoreType.DMA((3 * self.nt,)) for _ in range(4)])

    def _views(self, ins, outs, scratch):
        nt = self.nt
        stage, stage_sems = scratch[:nt], scratch[nt]
        ici_send, ici_recv, d2d_send, d2d_recv = scratch[nt + 1:]
        x, y, c = _mesh_pos()
        mine = 2 * x + y
        peers = _chip_peers(x, y)
        layer = pl.ds(self.layer, 1)

        def rows(t, half, r0=0, n=None):
            hr = self.half_rows[t]
            if n is None:
                return pl.ds(pl.multiple_of(half * hr, 16), hr) if self.split[t] else pl.ds(0, hr)
            return pl.ds(r0, n)

        def placed(t, blk, row_sel, row_len):
            sz = self.sizes[t]
            if self.axes[t] == 2:
                return outs[t].at[layer, row_sel, pl.ds(pl.multiple_of(blk * sz, 128), sz)]
            return outs[t].at[layer, pl.ds(pl.multiple_of(blk * sz, 16) + row_sel.start, row_len), :]

        def ici(t, k, blk):
            px, py = peers[k]
            sel = rows(t, c)
            return pltpu.make_async_remote_copy(
                src_ref=ins[t].at[layer, sel, :], dst_ref=placed(t, blk, sel, self.half_rows[t]),
                send_sem=ici_send.at[3 * t + k], recv_sem=ici_recv.at[3 * t + k],
                device_id=(px, py, c), device_id_type=MESH_T)

        def d2d(t, k, half):
            px, py = peers[k]
            piece = placed(t, 2 * px + py, rows(t, half), self.half_rows[t])
            return pltpu.make_async_remote_copy(
                src_ref=piece, dst_ref=piece, send_sem=d2d_send.at[3 * t + k], recv_sem=d2d_recv.at[3 * t + k],
                device_id=(x, y, 1 - c), device_id_type=MESH_T)

        def own_chunk(t, r0):
            rt = self.chunk_rows[t]
            sel = pl.ds(r0, rt)
            return ins[t].at[layer, sel, :], placed(t, mine, sel, rt), stage[t], stage_sems

        return c, mine, peers, ici, d2d, own_chunk

    def start(self, ins, outs, scratch):
        c, mine, peers, ici, d2d, own_chunk = self._views(ins, outs, scratch)
        for t in range(self.nt):
            for k in range(3):
                ici(t, k, mine).start()
        starts = [list(range(0, self.shapes[t][1], self.chunk_rows[t])) for t in range(self.nt)]
        for r in range(max(len(s) for s in starts)):
            active = [(t, *own_chunk(t, starts[t][r])) for t in range(self.nt) if r < len(starts[t])]
            loads = [pltpu.make_async_copy(src, buf, sems.at[2 * t]) for t, src, dst, buf, sems in active]
            for cp in loads:
                cp.start()
            for cp in loads:
                cp.wait()
            stores = [pltpu.make_async_copy(buf, dst, sems.at[2 * t + 1]) for t, src, dst, buf, sems in active]
            for cp in stores:
                cp.start()
            for cp in stores:
                cp.wait()

    def finish(self, ins, outs, scratch):
        c, mine, peers, ici, d2d, own_chunk = self._views(ins, outs, scratch)
        for t in range(self.nt):
            for k, (px, py) in enumerate(peers):
                ici(t, k, 2 * px + py).wait_recv()
                if self.split[t]:
                    d2d(t, k, c).start()
        for t in range(self.nt):
            for k in range(3):
                if self.split[t]:
                    d2d(t, k, 1 - c).wait_recv()
        for t in range(self.nt):
            for k in range(3):
                ici(t, k, mine).wait_send()
                if self.split[t]:
                    d2d(t, k, c).wait_send()


def all_gather_layer(shards, axes, layer, *, name):
    plan = LayerGather(shards, axes, layer)
    nt = plan.nt

    def body(*refs):
        ins, outs, scratch = refs[:nt], refs[nt:2 * nt], refs[2 * nt:]
        plan.start(ins, outs, scratch)
        plan.finish(ins, outs, scratch)

    return pl.pallas_call(
        body,
        out_shape=tuple(plan.out_shapes()),
        in_specs=[ANY] * nt,
        out_specs=tuple([ANY] * nt),
        scratch_shapes=plan.scratch_shapes(),
        name=name,
    )(*shards)


class HalfLayout:
    def __init__(self, shape, axis):
        self.R, self.C = shape
        self.axis = axis
        if axis == 1:
            self.hr, self.pw = self.R // 2, self.C // 4
            self.half_shape = (self.hr, self.C)
        else:
            self.hr, self.pw = self.R // 8, self.C
            self.half_shape = (4 * self.hr, self.C)
        self.tr = _rows_tile(self.hr, self.pw)
        self.nr = self.hr // self.tr

    def in_grad(self, ref, blk, half):
        if self.axis == 1:
            return ref.at[pl.ds(pl.multiple_of(half * self.hr, 16), self.hr), pl.ds(pl.multiple_of(blk * self.pw, 128), self.pw)]
        return ref.at[pl.ds(pl.multiple_of((2 * blk + half) * self.hr, 16), self.hr), :]

    def in_half(self, ref, blk):
        if self.axis == 1:
            return ref.at[:, pl.ds(pl.multiple_of(blk * self.pw, 128), self.pw)]
        return ref.at[pl.ds(pl.multiple_of(blk * self.hr, 16), self.hr), :]

    def grad_spec(self):
        if self.axis == 1:
            return pl.BlockSpec((self.tr, self.pw), lambda j, i, s: (s[0] * self.nr + i, j))
        return pl.BlockSpec((self.tr, self.pw), lambda j, i, s: ((2 * j + s[0]) * self.nr + i, 0))

    def half_spec(self):
        if self.axis == 1:
            return pl.BlockSpec((self.tr, self.pw), lambda j, i, s: (i, j))
        return pl.BlockSpec((self.tr, self.pw), lambda j, i, s: (j * self.nr + i, 0))


def half_exchange(grads, layouts, *, name):
    nt = len(grads)
    pieces = [(t, j) for t in range(nt) for j in (range(4) if layouts[t].axis == 0 else range(1))]

    def body(*refs):
        ins, outs = refs[:nt], refs[nt:2 * nt]
        send_sems, recv_sems = refs[2 * nt:]
        x, y, c = _mesh_pos()
        cps = []
        for n, (t, j) in enumerate(pieces):
            lay = layouts[t]
            if lay.axis == 1:
                src = ins[t].at[pl.ds(pl.multiple_of((1 - c) * lay.hr, 16), lay.hr), :]
                dst = outs[t]
            else:
                src = lay.in_grad(ins[t], j, 1 - c)
                dst = lay.in_half(outs[t], j)
            cp = pltpu.make_async_remote_copy(src_ref=src, dst_ref=dst, send_sem=send_sems.at[n], recv_sem=recv_sems.at[n],
                                              device_id=(x, y, 1 - c), device_id_type=MESH_T)
            cp.start()
            cps.append(cp)
        for cp in cps:
            cp.wait_recv()
        for cp in cps:
            cp.wait_send()

    return pl.pallas_call(
        body,
        out_shape=tuple(jax.ShapeDtypeStruct(lay.half_shape, F32) for lay in layouts),
        in_specs=[ANY] * nt,
        out_specs=tuple([ANY] * nt),
        scratch_shapes=[pltpu.SemaphoreType.DMA((len(pieces),)), pltpu.SemaphoreType.DMA((len(pieces),))],
        name=name,
    )(*grads)


def pair_sum(grad, other, lay, core, *, name):
    def body(c_ref, g_ref, o_ref, s32_ref, s16_ref):
        s = g_ref[...] + o_ref[...]
        s32_ref[...] = s
        s16_ref[...] = s.astype(BF16)

    return pl.pallas_call(
        body,
        out_shape=(jax.ShapeDtypeStruct(lay.half_shape, F32), jax.ShapeDtypeStruct(lay.half_shape, BF16)),
        grid_spec=pltpu.PrefetchScalarGridSpec(
            num_scalar_prefetch=1, grid=(4, lay.nr),
            in_specs=[lay.grad_spec(), lay.half_spec()],
            out_specs=(lay.half_spec(), lay.half_spec())),
        compiler_params=_cp(("parallel", "parallel")),
        name=name,
    )(core, grad, other)


class BlockScatter:
    def __init__(self, layouts):
        self.layouts = layouts
        self.nt = len(layouts)

    def out_shapes(self):
        return [jax.ShapeDtypeStruct((3, lay.hr, lay.pw), BF16) for lay in self.layouts]

    def scratch_shapes(self):
        return [pltpu.SemaphoreType.DMA((3 * self.nt,)), pltpu.SemaphoreType.DMA((3 * self.nt,))]

    def _copies(self, pairs16, recv, scratch):
        send_sems, recv_sems = scratch
        x, y, c = _mesh_pos()
        return [pltpu.make_async_remote_copy(
            src_ref=lay.in_half(pairs16[t], 2 * px + py), dst_ref=recv[t].at[k],
            send_sem=send_sems.at[3 * t + k], recv_sem=recv_sems.at[3 * t + k],
            device_id=(px, py, c), device_id_type=MESH_T)
            for t, lay in enumerate(self.layouts) for k, (px, py) in enumerate(_chip_peers(x, y))]

    def start(self, pairs16, recv, scratch):
        for cp in self._copies(pairs16, recv, scratch):
            cp.start()

    def finish(self, pairs16, recv, scratch):
        copies = self._copies(pairs16, recv, scratch)
        for cp in copies:
            cp.wait_recv()
        for cp in copies:
            cp.wait_send()


def gather_small(small, *, name):
    def body(small_in, small_out, ssend, srecv):
        x, y, c = _mesh_pos()
        me = 4 * x + 2 * y + c
        sends, recvs = [], []
        for r, (px, py, pc) in enumerate(_all_peers(x, y, c)):
            def mk(slot, r=r, px=px, py=py, pc=pc):
                return pltpu.make_async_remote_copy(
                    src_ref=small_in, dst_ref=small_out.at[slot], send_sem=ssend.at[r], recv_sem=srecv.at[r],
                    device_id=(px, py, pc), device_id_type=MESH_T)
            snd = mk(me)
            snd.start()
            sends.append(snd)
            recvs.append(mk(4 * px + 2 * py + pc))
        for r in recvs:
            r.wait_recv()
        for s in sends:
            s.wait_send()

    return pl.pallas_call(
        body,
        out_shape=jax.ShapeDtypeStruct((8,) + small.shape, F32),
        in_specs=[ANY],
        out_specs=ANY,
        scratch_shapes=[pltpu.SemaphoreType.DMA((7,)), pltpu.SemaphoreType.DMA((7,))],
        name=name,
    )(small)


def sum_chips(pair32, recv, lay, chip, *, name):
    def body(j_ref, p_ref, r_ref, s_ref):
        acc = p_ref[...]
        for k in range(3):
            acc = acc + r_ref[k].astype(F32)
        s_ref[...] = acc

    if lay.axis == 1:
        own = pl.BlockSpec((lay.tr, lay.pw), lambda i, j: (i, j[0]))
    else:
        own = pl.BlockSpec((lay.tr, lay.pw), lambda i, j: (j[0] * lay.nr + i, 0))
    return pl.pallas_call(
        body,
        out_shape=jax.ShapeDtypeStruct((lay.hr, lay.pw), F32),
        grid_spec=pltpu.PrefetchScalarGridSpec(
            num_scalar_prefetch=1, grid=(lay.nr,),
            in_specs=[own, pl.BlockSpec((3, lay.tr, lay.pw), lambda i, j: (0, i, 0))],
            out_specs=pl.BlockSpec((lay.tr, lay.pw), lambda i, j: (i, 0))),
        compiler_params=_cp(("parallel",)),
        name=name,
    )(chip, pair32, recv)


def sum_devices(gathered, own, me, *, name):
    _, R, C = gathered.shape

    def body(me_ref, g_ref, o_ref, s_ref):
        acc = None
        for k in range(8):
            part = jnp.where(me_ref[0] == k, o_ref[...], g_ref[k])
            acc = part if acc is None else acc + part
        s_ref[...] = acc

    return pl.pallas_call(
        body,
        out_shape=jax.ShapeDtypeStruct((R, C), F32),
        grid_spec=pltpu.PrefetchScalarGridSpec(
            num_scalar_prefetch=1, grid=(1,),
            in_specs=[pl.BlockSpec((8, R, C), lambda i, m: (0, 0, 0)), pl.BlockSpec((R, C), lambda i, m: (0, 0))],
            out_specs=pl.BlockSpec((R, C), lambda i, m: (0, 0))),
        compiler_params=_cp(("arbitrary",)),
        name=name,
    )(me, gathered, own)


def sibling_swap(parts, *, name):
    nt = len(parts)

    def body(*refs):
        ins, outs = refs[:nt], refs[nt:2 * nt]
        send_sems, recv_sems = refs[2 * nt:]
        x, y, c = _mesh_pos()
        cps = []
        for t in range(nt):
            cp = pltpu.make_async_remote_copy(src_ref=ins[t], dst_ref=outs[t], send_sem=send_sems.at[t],
                                              recv_sem=recv_sems.at[t], device_id=(x, y, 1 - c), device_id_type=MESH_T)
            cp.start()
            cps.append(cp)
        for cp in cps:
            cp.wait_recv()
        for cp in cps:
            cp.wait_send()

    return pl.pallas_call(
        body,
        out_shape=tuple(jax.ShapeDtypeStruct(p.shape, p.dtype) for p in parts),
        in_specs=[ANY] * nt,
        out_specs=tuple([ANY] * nt),
        scratch_shapes=[pltpu.SemaphoreType.DMA((nt,)), pltpu.SemaphoreType.DMA((nt,))],
        name=name,
    )(*parts)


def adamw_halves(w, m, v, mine, other, lay, core, *, name):
    _, r, c = w.shape
    tr, nr = lay.tr, lay.nr
    assert (r, c) == (2 * lay.hr, lay.pw), (w.shape, lay.hr, lay.pw)

    def body(c_ref, w_ref, m_ref, v_ref, *rest):
        g_refs, outs = rest[:2 * DEPTH], rest[2 * DEPTH:]
        l, h = pl.program_id(0), pl.program_id(1)
        g = None
        for d in range(DEPTH):
            gd = jnp.where(h == c_ref[0], g_refs[d][...], g_refs[DEPTH + d][...])
            g = gd if g is None else jnp.where(l == d, gd, g)
        _adamw_math(g, w_ref, m_ref, v_ref, *outs)

    full = pl.BlockSpec((None, tr, c), lambda l, h, i, s: (l, h * nr + i, 0))

    def part(d, is_mine):
        def index(l, h, i, s):
            used = jnp.logical_and(l == d, (h == s[0]) == is_mine)
            return jnp.where(used, i, 0), 0
        return pl.BlockSpec((tr, c), index)

    shp = jax.ShapeDtypeStruct((DEPTH, r, c), F32)
    return pl.pallas_call(
        body,
        out_shape=(shp, shp, shp, shp),
        grid_spec=pltpu.PrefetchScalarGridSpec(
            num_scalar_prefetch=1, grid=(DEPTH, 2, nr),
            in_specs=[full, full, full] + [part(d, True) for d in range(DEPTH)] + [part(d, False) for d in range(DEPTH)],
            out_specs=(full, full, full, full)),
        compiler_params=_cp(("arbitrary", "arbitrary", "arbitrary")),
        name=name,
    )(core, w, m, v, *mine, *other)


WEIGHTS = ("norm_mix", "w_in", "b_gate", "q_norm_a", "k_norm_a", "rel_bias_a", "w_pool", "pool_scale",
           "w_branch_a", "w_branch_b", "w_branch_c", "w_out", "norm_ffn", "w_up", "conv_w", "conv_b", "w_down")
SHARDED = {"w_in": 2, "w_branch_a": 2, "w_branch_b": 2, "w_branch_c": 2, "w_out": 1, "w_up": 2, "conv_w": 2,
           "w_down": 1}
REPLICATED = tuple(n for n in WEIGHTS if n not in SHARDED)
MATMUL_WEIGHTS = tuple(n for n in SHARDED if n != "conv_w")
SMALL_WEIGHTS = tuple(n for n in WEIGHTS if n not in MATMUL_WEIGHTS)
SMALL_ROWS = 1496


def _layer_fwd(x, p, full, tables, rest=None, prefetch=None):
    l = p["l"]
    diag = exact_dot(p["rel_bias_a"], tables["onehot_t"], name="bias_diagonals")
    diag = diag.reshape(N_HEADS, N_VARIANTS, 1, DIAG_W).transpose(1, 0, 2, 3)
    biasm = bias_expand(diag, name="bias_expand")
    gq8 = jnp.tile(p["q_norm_a"], N_HEADS)[None]
    gk8 = jnp.tile(p["k_norm_a"], N_HEADS)[None]
    h = rmsnorm_fwd(x, p["norm_mix"][None], name="rmsnorm_fwd")
    if rest is None:
        proj = matmul(h, full["w_in"], b_layer=l, name="mm_in")
    else:
        plan, shards, names = rest
        proj, gathered = matmul(h, full["w_in"], b_layer=l, behind=(plan, shards), name="mm_in_gather")
        full = {**full, **dict(zip(names, gathered))}
    qa, ka, va, qb, kb, vb = qkv_prep(proj, gq8, gk8, name="qkv_prep")
    oa = attn_a_fwd(qa, ka, va, biasm, name="attn_a_fwd")
    if prefetch is None:
        ob, tot, nblk = attn_b_fwd(qb, kb, vb, name="attn_b_fwd")
    else:
        plan, shards, names = prefetch
        ob, tot, nblk, filled = attn_b_fwd(qb, kb, vb, gather=(plan, shards, [full[n] for n in names]),
                                           name="attn_b_fwd_gather")
        full = dict(zip(names, filled))
    wpool = p["w_pool"].astype(BF16)
    oc = pool_fwd(proj, wpool, p["pool_scale"][None], name="pool_fwd")
    merged = merge_fwd(oa, ob, oc, proj, p["b_gate"][None], full["w_branch_a"][l], full["w_branch_b"][l],
                       full["w_branch_c"][l], name="merge_fwd")
    x1 = matmul(merged, full["w_out"], b_layer=l, add=x, name="mm_out")
    h2 = rmsnorm_fwd(x1, p["norm_ffn"][None], name="rmsnorm_fwd")
    u = matmul(h2, full["w_up"], b_layer=l, name="mm_up")
    a = conv_glu_fwd(u, full["conv_w"][l], p["conv_b"][None], name="conv_glu_fwd")
    x2 = matmul(a, full["w_down"], b_layer=l, add=x1, name="mm_down")
    saved = dict(x=x, h=h, proj=proj, qa=qa, ka=ka, va=va, qb=qb, kb=kb, vb=vb, oa=oa, ob=ob, tot=tot, nblk=nblk, oc=oc,
                 merged=merged, x1=x1, h2=h2, u=u, a=a, biasm=biasm, gq8=gq8, gk8=gk8, wpool=wpool)
    return x2, saved, full


class GradReducer:
    def __init__(self, layouts, core, chip):
        self.layouts, self.core, self.chip = layouts, core, chip
        self.pairs32, self.received, self.pending = {}, {}, []

    def prepare(self, layer, grads):
        names = list(grads)
        lays = [self.layouts[n] for n in names]
        others = half_exchange([grads[n] for n in names], lays, name="half_exchange")
        for n, lay, other in zip(names, lays, others):
            p32, p16 = pair_sum(grads[n], other, lay, self.core, name="pair_sum")
            self.pairs32[(layer, n)] = p32
            self.pending.append(((layer, n), p16))

    def take(self):
        keys = [k for k, _ in self.pending]
        pairs16 = [p for _, p in self.pending]
        self.pending = []
        return keys, BlockScatter([self.layouts[n] for _, n in keys]), pairs16

    def store(self, keys, received):
        self.received.update(zip(keys, received))

    def finish(self):
        keys = list(self.pairs32)
        mine = [sum_chips(self.pairs32[k], self.received[k], self.layouts[k[1]], self.chip, name="sum_chips") for k in keys]
        other = sibling_swap(mine, name="sibling_swap")
        return dict(zip(keys, mine)), dict(zip(keys, other))


EARLY_WEIGHTS = ("w_down", "w_up", "w_out", "w_branch_a", "w_branch_b", "w_branch_c")


def _layer_bwd(dx2, s, p, tables, reducer):
    g = {}
    full, l = p["full"], p["l"]
    da = matmul(dx2, full["w_down"], b_layer=l, tb=True, name="mm_down_dx")
    g["w_down"] = matmul(s["a"], dx2, ta=True, name="mm_down_dw")
    dug, duv, dcwg, dcwv, dcbg, dcbv = conv_glu_bwd(s["u"], p["conv_w"], p["conv_b"][None], da, name="conv_glu_bwd")
    du = jnp.concatenate([dug, duv], axis=1)
    g["conv_w"] = jnp.concatenate([dcwg, dcwv], axis=1)
    g["conv_b"] = jnp.concatenate([dcbg, dcbv], axis=1)[0]
    g["w_up"] = matmul(s["h2"], du, ta=True, name="mm_up_dw")
    dx1, dg2 = matmul(du, full["w_up"], b_layer=l, tb=True, norm_bwd=(s["x1"], p["norm_ffn"][None], dx2),
                      name="mm_up_dx_norm")
    g["norm_ffn"] = dg2[0]
    dmerged = matmul(dx1, full["w_out"], b_layer=l, tb=True, name="mm_out_dx")
    g["w_out"] = matmul(s["merged"], dx1, ta=True, name="mm_out_dw")
    t_a, t_b, t_c, dga, dgb, dgc, dba, dbb, dbc = merge_bwd(
        dmerged, s["oa"], s["ob"], s["oc"], s["proj"], p["b_gate"][None], p["w_branch_a"], p["w_branch_b"],
        p["w_branch_c"], name="merge_bwd")
    g["b_gate"] = jnp.concatenate([dba, dbb, dbc], axis=1)[0]
    g["w_branch_a"] = matmul(s["oa"], t_a, ta=True, name="mm_branch_dw")
    g["w_branch_b"] = matmul(s["ob"], t_b, ta=True, name="mm_branch_dw")
    g["w_branch_c"] = matmul(s["oc"], t_c, ta=True, name="mm_branch_dw")
    doa = matmul(t_a, full["w_branch_a"], b_layer=l, tb=True, out_dtype=BF16, name="mm_branch_dx")
    dob = matmul(t_b, full["w_branch_b"], b_layer=l, tb=True, out_dtype=BF16, name="mm_branch_dx")
    doc = matmul(t_c, full["w_branch_c"], b_layer=l, tb=True, name="mm_branch_dx_f32")
    dqh, dkh, dva, dbias = attn_a_bwd(s["qa"], s["ka"], s["va"], s["biasm"], doa, name="attn_a_bwd")
    ddiag = relbias_reduce(dbias, name="relbias_reduce")
    ddiag = ddiag.transpose(1, 0, 2, 3).reshape(N_HEADS, N_VARIANTS * DIAG_W)
    g["rel_bias_a"] = exact_dot(ddiag, tables["onehot"], name="relbias_table")
    dqa, dka, dgq8, dgk8 = qknorm_bwd(s["proj"], s["gq8"], s["gk8"], dqh, dkh, name="qknorm_bwd")
    g["q_norm_a"] = dgq8.reshape(N_HEADS, HEAD_DIM).sum(axis=0)
    g["k_norm_a"] = dgk8.reshape(N_HEADS, HEAD_DIM).sum(axis=0)
    reducer.prepare(l, {n: g[n] for n in EARLY_WEIGHTS})
    keys, plan, pairs16 = reducer.take()
    dqb, dkb, dvb, received = attn_b_bwd(s["qb"], s["kb"], s["vb"], s["tot"], s["nblk"], dob, scatter=(plan, pairs16),
                                         name="attn_b_bwd_scatter")
    reducer.store(keys, received)
    duc, dwp, dsc = pool_bwd(s["proj"], s["wpool"], p["pool_scale"][None], doc, name="pool_bwd")
    g["w_pool"] = dwp
    g["pool_scale"] = dsc[0]
    dproj = jnp.concatenate([dqa, dka, dva.astype(BF16), dqb, dkb.astype(BF16), dvb.astype(BF16), duc,
                             dga, dgb, dgc], axis=1)
    g["w_in"] = matmul(s["h"], dproj, ta=True, name="mm_in_dw")
    reducer.prepare(l, {"w_in": g["w_in"]})
    norm = (s["x"], p["norm_mix"][None], dx1)
    if l > 0:
        dx, dg1 = matmul(dproj, full["w_in"], b_layer=l, tb=True, norm_bwd=norm, name="mm_in_dx_norm")
    else:
        keys, plan, pairs16 = reducer.take()
        dx, dg1, received = matmul(dproj, full["w_in"], b_layer=l, tb=True, norm_bwd=norm, behind=(plan, pairs16),
                                   name="mm_in_dx_norm_scatter")
        reducer.store(keys, received)
    g["norm_mix"] = dg1[0]
    return dx, g


def kernel(x, norm_mix, w_in, b_gate, q_norm_a, k_norm_a, rel_bias_a, w_pool, pool_scale, w_branch_a, w_branch_b, w_branch_c, w_out, norm_ffn, w_up, conv_w, conv_b, w_down, loss_target, m_norm_mix, m_w_in, m_b_gate, m_q_norm_a, m_k_norm_a, m_rel_bias_a, m_w_pool, m_pool_scale, m_w_branch_a, m_w_branch_b, m_w_branch_c, m_w_out, m_norm_ffn, m_w_up, m_conv_w, m_conv_b, m_w_down, v_norm_mix, v_w_in, v_b_gate, v_q_norm_a, v_k_norm_a, v_rel_bias_a, v_w_pool, v_pool_scale, v_w_branch_a, v_w_branch_b, v_w_branch_c, v_w_out, v_norm_ffn, v_w_up, v_conv_w, v_conv_b, v_w_down):
    w = dict(zip(WEIGHTS, (norm_mix, w_in, b_gate, q_norm_a, k_norm_a, rel_bias_a, w_pool, pool_scale, w_branch_a,
                           w_branch_b, w_branch_c, w_out, norm_ffn, w_up, conv_w, conv_b, w_down)))
    m = dict(zip(WEIGHTS, (m_norm_mix, m_w_in, m_b_gate, m_q_norm_a, m_k_norm_a, m_rel_bias_a, m_w_pool, m_pool_scale,
                           m_w_branch_a, m_w_branch_b, m_w_branch_c, m_w_out, m_norm_ffn, m_w_up, m_conv_w, m_conv_b,
                           m_w_down)))
    v = dict(zip(WEIGHTS, (v_norm_mix, v_w_in, v_b_gate, v_q_norm_a, v_k_norm_a, v_rel_bias_a, v_w_pool, v_pool_scale,
                           v_w_branch_a, v_w_branch_b, v_w_branch_c, v_w_out, v_norm_ffn, v_w_up, v_conv_w, v_conv_b,
                           v_w_down)))
    onehot = diagonal_onehot()
    tables = dict(onehot=jnp.asarray(onehot), onehot_t=jnp.asarray(np.ascontiguousarray(onehot.T)))

    names = tuple(SHARDED)
    shards = [w[n] if n == "conv_w" else w[n].astype(BF16) for n in names]
    axes = [SHARDED[n] for n in names]
    later = [i for i, n in enumerate(names) if n != "w_in"]
    rest = (LayerGather([shards[i] for i in later], [axes[i] for i in later], 0), [shards[i] for i in later],
            [names[i] for i in later])
    first = names.index("w_in")
    full = {"w_in": all_gather_layer([shards[first]], [axes[first]], 0, name="all_gather_layer")[0]}

    def layer_params(l):
        p = {n: full[n][l] for n in ("w_branch_a", "w_branch_b", "w_branch_c", "conv_w")}
        p.update({n: w[n][l] for n in REPLICATED})
        p.update(full=full, l=l)
        return p

    xs = x[0]
    saved = []
    for l in range(DEPTH):
        prefetch = (LayerGather(shards, axes, l + 1), shards, names) if l + 1 < DEPTH else None
        replicated = {n: w[n][l] for n in REPLICATED}
        xs, s, full = _layer_fwd(xs, dict(replicated, l=l), full, tables, rest if l == 0 else None, prefetch)
        saved.append(s)
    dx, lpart = loss_head(xs, loss_target[0], name="loss_head")
    loss = lax.psum(lpart[0, 0], MESH_AXES)
    as_index = lambda i: jnp.reshape(i, (1,)).astype(jnp.int32)
    cx, cy, cc = _mesh_pos()
    core, chip, me = as_index(cc), as_index(2 * cx + cy), as_index(4 * cx + 2 * cy + cc)
    layouts = {n: HalfLayout((full[n].shape[1], full[n].shape[2]), SHARDED[n] - 1) for n in MATMUL_WEIGHTS}
    reducer = GradReducer(layouts, core, chip)
    grads = [None] * DEPTH
    for l in reversed(range(DEPTH)):
        dx, grads[l] = _layer_bwd(dx, saved[l], layer_params(l), tables, reducer)

    g = {n: jnp.stack([grads[l][n] for l in range(DEPTH)]) for n in SMALL_WEIGHTS}
    flat = jnp.concatenate([g[n].reshape(-1) for n in SMALL_WEIGHTS])
    small = jnp.pad(flat, (0, SMALL_ROWS * 128 - flat.shape[0])).reshape(SMALL_ROWS, 128)
    small_sum = sum_devices(gather_small(small, name="gather_small"), small, me, name="sum_devices").reshape(-1)
    mine, other = reducer.finish()

    res = {}
    for n in MATMUL_WEIGHTS:
        res[n] = adamw_halves(w[n], m[n], v[n], [mine[(l, n)] for l in range(DEPTH)],
                              [other[(l, n)] for l in range(DEPTH)], layouts[n], core, name="adamw_halves")
    off = 0
    for n in SMALL_WEIGHTS:
        shp = g[n].shape
        size = int(np.prod(shp))
        gn = small_sum[off:off + size].reshape(shp)
        off += size
        if n in SHARDED:
            gn = lax.dynamic_slice_in_dim(gn, (2 * cx + cy) * w[n].shape[-1], w[n].shape[-1], axis=len(shp) - 1)
        shp = w[n].shape
        cols = shp[-1]
        two_d = lambda t: t.reshape(int(np.prod(shp)) // cols, cols)
        res[n] = [t.reshape(shp) for t in adamw(two_d(w[n]), two_d(m[n]), two_d(v[n]), two_d(gn), name="adamw")]

    out = [loss, dx[None]]
    for k in range(4):
        out.extend(res[n][k] for n in WEIGHTS)
    return tuple(out)
```

```python
import jax
import jax.numpy as jnp
import numpy as np
from jax import lax
from jax.experimental import pallas as pl
from jax.experimental.pallas import tpu as pltpu

F32 = jnp.float32
BF16 = jnp.bfloat16

D_MODEL = 1024
DEPTH = 2
CHUNK = 64
N_LEFT = 8
HEAD_DIM = 64
N_HEADS = 8
WIDTH = 512
POOL_WINDOWS = (2, 4, 8, 16)
GROUP_DIM = 128
MAX_REL = 2 * CHUNK
REL_TABLE = MAX_REL + CHUNK
D_FF = 2816
EPS = 1e-6
QK_SCALE = 0.125
GATE_COL0 = 7 * WIDTH

ADAM_LR = 0.001
ADAM_B1 = 0.9
ADAM_B2 = 0.999
ADAM_EPS = 1e-08
ADAM_WD = 0.01
ADAM_STEP = 10

VMEM_LIMIT = 56 * 1024 * 1024
ATT_Q = 256
A_Q = 256
A_WIN = A_Q + N_LEFT * CHUNK
A_FWD_HEADS = 8
A_BWD_HEADS = 4
B_HEADS = 4
HALO = 16
CONV_HALO = 8
NEG = -1e30

MESH_AXES = ("x", "y", "c")
MESH_T = pl.DeviceIdType.MESH


def _cp(sem=None, vmem=VMEM_LIMIT):
    return pltpu.CompilerParams(dimension_semantics=sem, vmem_limit_bytes=vmem)


def _dot(a, b, ca, cb):
    return lax.dot_general(a, b, (((ca,), (cb,)), ((), ())), preferred_element_type=F32)


def _tile(n, cands=(512, 256, 128)):
    for c in cands:
        if n % c == 0:
            return c
    return n


def _split_hi_lo(v):
    hi = v.astype(BF16)
    lo = (v - hi.astype(F32)).astype(BF16)
    return hi, lo


def matmul(a, b, *, ta=False, tb=False, add=None, norm_bwd=None, out_dtype=F32, b_layer=None, behind=None, name):
    plan, sources = behind if behind is not None else (None, [])
    n_src = len(sources)
    n_dst = len(plan.out_shapes()) if plan is not None else 0
    assert add is None or norm_bwd is None
    if ta:
        K, M = a.shape
    else:
        M, K = a.shape
    if tb:
        N, K2 = b.shape[-2:]
    else:
        K2, N = b.shape[-2:]
    assert K == K2, (a.shape, b.shape, ta, tb)
    big = (1024, 1408, 512, 256, 128)
    tm = _tile(M, big)
    tn = _tile(N, (1664,) + big)
    tk = _tile(K, big if ta else (1664, 1408) + big if norm_bwd is not None else (3328, 2816) + big)
    nk = K // tk

    n_extra = 1 if add is not None else 3 if norm_bwd is not None else 0
    n_in = 2 + n_extra + n_src
    n_out = 2 if norm_bwd is not None else 1
    grid = (M // tm, N // tn, nk)
    assert norm_bwd is None or tn == N, "the norm gradient needs whole rows in one output tile"

    def body(*refs):
        a_ref, b_ref = refs[:2]
        extra = refs[2:2 + n_extra]
        o_ref, acc = refs[n_in], refs[n_in + n_out + n_dst]
        i, j, k = pl.program_id(0), pl.program_id(1), pl.program_id(2)
        if plan is not None:
            comm = (refs[n_in - n_src:n_in], refs[n_in + n_out:n_in + n_out + n_dst], refs[n_in + n_out + n_dst + 1:])

            @pl.when((i == 0) & (j == 0) & (k == 0))
            def _():
                plan.start(*comm)

        @pl.when(k == 0)
        def _():
            acc[...] = jnp.zeros_like(acc)

        av = a_ref[...].astype(BF16)
        bv = b_ref[...].astype(BF16)
        acc[...] += _dot(av, bv, 0 if ta else 1, 1 if tb else 0)

        @pl.when(k == nk - 1)
        def _():
            r = acc[...]
            if add is not None:
                r = r + extra[0][...].astype(F32)
            if norm_bwd is not None:
                x_ref, g_ref, dres_ref = extra
                dg_ref = refs[n_in + 1]
                xv = x_ref[...]
                inv = lax.rsqrt(jnp.mean(xv * xv, axis=-1, keepdims=True) + EPS)
                gd = r * g_ref[...]
                mean = jnp.mean(xv * gd, axis=-1, keepdims=True)

                @pl.when(i == 0)
                def _():
                    dg_ref[...] = jnp.zeros_like(dg_ref)

                dg_ref[...] += jnp.sum(r * xv * inv, axis=0, keepdims=True)
                r = dres_ref[...] + inv * gd - xv * (inv * inv * inv * mean)
            o_ref[...] = r.astype(out_dtype)

        if plan is not None:
            @pl.when((i == grid[0] - 1) & (j == grid[1] - 1) & (k == nk - 1))
            def _():
                plan.finish(*comm)

    a_spec = pl.BlockSpec((tk, tm), lambda i, j, k: (k, i)) if ta else pl.BlockSpec((tm, tk), lambda i, j, k: (i, k))
    if b_layer is None:
        b_spec = pl.BlockSpec((tn, tk), lambda i, j, k: (j, k)) if tb else pl.BlockSpec((tk, tn), lambda i, j, k: (k, j))
    elif tb:
        b_spec = pl.BlockSpec((None, tn, tk), lambda i, j, k: (b_layer, j, k))
    else:
        b_spec = pl.BlockSpec((None, tk, tn), lambda i, j, k: (b_layer, k, j))
    o_spec = pl.BlockSpec((tm, tn), lambda i, j, k: (i, j))
    vec_spec = pl.BlockSpec((1, tn), lambda i, j, k: (0, j))
    in_specs = [a_spec, b_spec]
    args = [a, b]
    out_shape = [jax.ShapeDtypeStruct((M, N), out_dtype)]
    out_specs = [o_spec]
    if add is not None:
        in_specs.append(o_spec)
        args.append(add)
    if norm_bwd is not None:
        in_specs += [o_spec, vec_spec, o_spec]
        args += list(norm_bwd)
        out_shape.append(jax.ShapeDtypeStruct((1, N), F32))
        out_specs.append(vec_spec)
    sequential = plan is not None or norm_bwd is not None
    any_space = pl.BlockSpec(memory_space=pl.ANY)
    outs = pl.pallas_call(
        body,
        out_shape=tuple(out_shape) + (tuple(plan.out_shapes()) if plan is not None else ()),
        grid=grid,
        in_specs=in_specs + [any_space] * n_src,
        out_specs=tuple(out_specs) + tuple([any_space] * n_dst),
        scratch_shapes=[pltpu.VMEM((tm, tn), F32)] + (list(plan.scratch_shapes()) if plan is not None else []),
        compiler_params=_cp(("arbitrary" if sequential else "parallel",) * 2 + ("arbitrary",)),
        name=name,
    )(*args, *sources)
    if plan is None:
        return outs[0] if n_out == 1 else tuple(outs)
    return tuple(outs[:n_out]) + (list(outs[n_out:]),) if n_out > 1 else (outs[0], list(outs[1:]))


def rmsnorm_fwd(x, g, *, name):
    S, D = x.shape
    T = _tile(S)

    def body(x_ref, g_ref, h_ref):
        xv = x_ref[...]
        r = lax.rsqrt(jnp.mean(xv * xv, axis=-1, keepdims=True) + EPS)
        h_ref[...] = (xv * r * g_ref[...]).astype(BF16)

    return pl.pallas_call(
        body,
        out_shape=jax.ShapeDtypeStruct((S, D), BF16),
        grid=(S // T,),
        in_specs=[pl.BlockSpec((T, D), lambda i: (i, 0)), pl.BlockSpec((1, D), lambda i: (0, 0))],
        out_specs=pl.BlockSpec((T, D), lambda i: (i, 0)),
        compiler_params=_cp(("parallel",)),
        name=name,
    )(x, g)


def _head_mean_matrix():
    r = lax.broadcasted_iota(jnp.int32, (WIDTH, WIDTH), 0) // HEAD_DIM
    c = lax.broadcasted_iota(jnp.int32, (WIDTH, WIDTH), 1) // HEAD_DIM
    return jnp.where(r == c, 1.0 / HEAD_DIM, 0.0).astype(BF16)


def _head_mean(v, mm):
    hi, lo = _split_hi_lo(v)
    return _dot(hi, mm, 1, 0) + _dot(lo, mm, 1, 0)


def qkv_prep(proj, gq, gk, *, name):
    S = proj.shape[0]
    T = _tile(S)

    def body(qa, ka, va, qb, kb, vb, gq_ref, gk_ref, oqa, oka, ova, oqb, okb, ovb):
        mm = _head_mean_matrix()
        for src, gref, dst, scale in ((qa, gq_ref, oqa, QK_SCALE), (ka, gk_ref, oka, 1.0)):
            v = src[...]
            r = lax.rsqrt(_head_mean(v * v, mm) + EPS)
            dst[...] = (v * r * gref[...] * scale).astype(BF16)
        oqb[...] = (qb[...] * QK_SCALE).astype(BF16)
        for src, dst in ((va, ova), (kb, okb), (vb, ovb)):
            dst[...] = src[...].astype(BF16)

    col = lambda j: pl.BlockSpec((T, WIDTH), lambda i, j=j: (i, j))
    vec = pl.BlockSpec((1, WIDTH), lambda i: (0, 0))
    out = pl.BlockSpec((T, WIDTH), lambda i: (i, 0))
    return pl.pallas_call(
        body,
        out_shape=tuple(jax.ShapeDtypeStruct((S, WIDTH), BF16) for _ in range(6)),
        grid=(S // T,),
        in_specs=[col(0), col(1), col(2), col(3), col(4), col(5), vec, vec],
        out_specs=tuple(out for _ in range(6)),
        compiler_params=_cp(("parallel",)),
        name=name,
    )(proj, proj, proj, proj, proj, proj, gq, gk)


def qknorm_bwd(proj, gq, gk, dqh, dkh, *, name):
    S = proj.shape[0]
    T = _tile(S)

    def body(qa, ka, gq_ref, gk_ref, dq_ref, dk_ref, oq, ok, ogq, ogk):
        i = pl.program_id(0)
        mm = _head_mean_matrix()

        @pl.when(i == 0)
        def _():
            ogq[...] = jnp.zeros_like(ogq)
            ogk[...] = jnp.zeros_like(ogk)

        for src, gref, dref, dst, gdst in ((qa, gq_ref, dq_ref, oq, ogq), (ka, gk_ref, dk_ref, ok, ogk)):
            v = src[...]
            dy = dref[...]
            r = lax.rsqrt(_head_mean(v * v, mm) + EPS)
            gd = dy * gref[...]
            m = _head_mean(v * gd, mm)
            dst[...] = (r * gd - v * (r * r * r * m)).astype(BF16)
            gdst[...] += jnp.sum(dy * v * r, axis=0, keepdims=True)

    col = lambda j: pl.BlockSpec((T, WIDTH), lambda i, j=j: (i, j))
    vec = pl.BlockSpec((1, WIDTH), lambda i: (0, 0))
    row = pl.BlockSpec((T, WIDTH), lambda i: (i, 0))
    return pl.pallas_call(
        body,
        out_shape=(jax.ShapeDtypeStruct((S, WIDTH), BF16), jax.ShapeDtypeStruct((S, WIDTH), BF16),
                   jax.ShapeDtypeStruct((1, WIDTH), F32), jax.ShapeDtypeStruct((1, WIDTH), F32)),
        grid=(S // T,),
        in_specs=[col(0), col(1), vec, vec, row, row],
        out_specs=(row, row, vec, vec),
        compiler_params=_cp(("arbitrary",)),
        name=name,
    )(proj, proj, gq, gk, dqh, dkh)


DIAG_W = 1024
N_VARIANTS = N_LEFT * CHUNK // A_Q + 1


def diagonal_onehot():
    jj = np.arange(DIAG_W)
    diff = np.where(jj < A_WIN, jj, jj - DIAG_W)
    out = np.zeros((N_VARIANTS, DIAG_W, REL_TABLE), np.float32)
    for v in range(N_VARIANTS):
        rel = np.clip(A_Q * v - diff, -(CHUNK - 1), MAX_REL) + (CHUNK - 1)
        out[v, jj, rel] = 1.0
    return out.reshape(N_VARIANTS * DIAG_W, REL_TABLE)


def exact_dot(a, b, *, name):
    def body(a_ref, b_ref, o_ref):
        o_ref[...] = jnp.dot(a_ref[...], b_ref[...], precision=lax.Precision.HIGHEST, preferred_element_type=F32)

    return pl.pallas_call(body, out_shape=jax.ShapeDtypeStruct((a.shape[0], b.shape[1]), F32),
                          compiler_params=_cp(), name=name)(a, b)


def _band_valid(v):
    qc = (lax.broadcasted_iota(jnp.int32, (A_Q, A_WIN), 0) + A_Q * v) // CHUNK
    kc = lax.broadcasted_iota(jnp.int32, (A_Q, A_WIN), 1) // CHUNK
    return (kc <= qc) & (kc >= qc - N_LEFT)


def bias_expand(diag, *, name):
    def body(d_ref, o_ref):
        rows = jnp.broadcast_to(d_ref[0, 0], (A_Q, DIAG_W))
        skew = pltpu.roll(rows, 0, 1, stride=1, stride_axis=0)
        o_ref[0, 0] = jnp.where(_band_valid(pl.program_id(0)), skew[:, :A_WIN], NEG)

    return pl.pallas_call(
        body,
        out_shape=jax.ShapeDtypeStruct((N_VARIANTS, N_HEADS, A_Q, A_WIN), F32),
        grid=(N_VARIANTS, N_HEADS),
        in_specs=[pl.BlockSpec((1, 1, 1, DIAG_W), lambda v, h: (v, h, 0, 0))],
        out_specs=pl.BlockSpec((1, 1, A_Q, A_WIN), lambda v, h: (v, h, 0, 0)),
        compiler_params=_cp(("parallel", "parallel")),
        name=name,
    )(diag)


def relbias_reduce(dbias, *, name):
    def body(db_ref, o_ref):
        acc = None
        for a in range(A_Q // 8):
            x = jnp.concatenate([db_ref[0, 0, 8 * a:8 * a + 8, :], jnp.zeros((8, DIAG_W - A_WIN), F32)], axis=1)
            x = pltpu.roll(x, DIAG_W - 8 * a, 1) if a else x
            acc = x if acc is None else acc + x
        row = lax.broadcasted_iota(jnp.int32, (8, DIAG_W), 0)
        for b in range(3):
            acc = jnp.where((row >> b) & 1 == 1, pltpu.roll(acc, DIAG_W - (1 << b), 1), acc)
        o_ref[0, 0] = jnp.sum(acc, axis=0, keepdims=True)

    return pl.pallas_call(
        body,
        out_shape=jax.ShapeDtypeStruct((N_VARIANTS, N_HEADS, 1, DIAG_W), F32),
        grid=(N_VARIANTS, N_HEADS),
        in_specs=[pl.BlockSpec((1, 1, A_Q, A_WIN), lambda v, h: (v, h, 0, 0))],
        out_specs=pl.BlockSpec((1, 1, 1, DIAG_W), lambda v, h: (v, h, 0, 0)),
        compiler_params=_cp(("parallel", "parallel")),
        name=name,
    )(dbias)


def _a_window_start(qb):
    return pl.multiple_of(jnp.maximum(qb * A_Q - N_LEFT * CHUNK, 0), A_Q)


def attn_a_fwd(q, k, v, biasm, *, name):
    S = q.shape[0]
    nq = S // A_Q

    def body(q_ref, k_ref, v_ref, b_ref, o_ref):
        qb = pl.program_id(1)
        start = _a_window_start(qb)
        outs = []
        for h in range(A_FWD_HEADS):
            lanes = slice(h * HEAD_DIM, (h + 1) * HEAD_DIM)
            qh = q_ref[:, lanes]
            kw = k_ref[pl.ds(start, A_WIN), lanes]
            vw = v_ref[pl.ds(start, A_WIN), lanes]
            s = _dot(qh, kw, 1, 1) + b_ref[0, h]
            m = jnp.max(s, axis=-1, keepdims=True)
            e = jnp.exp(s - m)
            outs.append(_dot(e.astype(BF16), vw, 1, 0) * (1.0 / jnp.sum(e, axis=-1, keepdims=True)))
        o_ref[...] = jnp.concatenate(outs, axis=1).astype(BF16)

    qspec = pl.BlockSpec((A_Q, A_FWD_HEADS * HEAD_DIM), lambda hp, qb: (qb, hp))
    kvspec = pl.BlockSpec((S, A_FWD_HEADS * HEAD_DIM), lambda hp, qb: (0, hp))
    bspec = pl.BlockSpec((1, A_FWD_HEADS, A_Q, A_WIN), lambda hp, qb: (jnp.minimum(qb, N_VARIANTS - 1), hp, 0, 0))
    return pl.pallas_call(
        body,
        out_shape=jax.ShapeDtypeStruct((S, WIDTH), BF16),
        grid=(N_HEADS // A_FWD_HEADS, nq),
        in_specs=[qspec, kvspec, kvspec, bspec],
        out_specs=qspec,
        compiler_params=_cp(("parallel", "arbitrary")),
        name=name,
    )(q, k, v, biasm)


def attn_a_bwd(q, k, v, biasm, do, *, name):
    S = q.shape[0]
    nq = S // A_Q

    def body(q_ref, k_ref, v_ref, b_ref, do_ref, dq_ref, dk_ref, dv_ref, db_ref):
        qb = pl.program_id(1)
        start = _a_window_start(qb)

        @pl.when(qb == 0)
        def _():
            dk_ref[...] = jnp.zeros_like(dk_ref)
            dv_ref[...] = jnp.zeros_like(dv_ref)

        @pl.when(qb < N_VARIANTS)
        def _():
            db_ref[...] = jnp.zeros_like(db_ref)

        dqs = []
        for h in range(A_BWD_HEADS):
            lanes = slice(h * HEAD_DIM, (h + 1) * HEAD_DIM)
            qh = q_ref[:, lanes]
            doh = do_ref[:, lanes]
            kw = k_ref[pl.ds(start, A_WIN), lanes]
            vw = v_ref[pl.ds(start, A_WIN), lanes]
            s = _dot(qh, kw, 1, 1) + b_ref[0, h]
            m = jnp.max(s, axis=-1, keepdims=True)
            e = jnp.exp(s - m)
            p = e * (1.0 / jnp.sum(e, axis=-1, keepdims=True))
            dp = _dot(doh, vw, 1, 1)
            delta = jnp.sum(p * dp, axis=-1, keepdims=True)
            ds = p * (dp - delta)
            db_ref[0, h] += ds
            dsb = ds.astype(BF16)
            dqs.append(_dot(dsb, kw, 1, 0) * QK_SCALE)
            dk_ref[pl.ds(start, A_WIN), lanes] += _dot(dsb, qh, 0, 0)
            dv_ref[pl.ds(start, A_WIN), lanes] += _dot(p.astype(BF16), doh, 0, 0)
        dq_ref[...] = jnp.concatenate(dqs, axis=1)

    once = pl.Buffered(1)
    qspec = pl.BlockSpec((A_Q, A_BWD_HEADS * HEAD_DIM), lambda hp, qb: (qb, hp))
    kvspec = pl.BlockSpec((S, A_BWD_HEADS * HEAD_DIM), lambda hp, qb: (0, hp), pipeline_mode=once)
    bspec = pl.BlockSpec((1, A_BWD_HEADS, A_Q, A_WIN), lambda hp, qb: (jnp.minimum(qb, N_VARIANTS - 1), hp, 0, 0))
    return pl.pallas_call(
        body,
        out_shape=(jax.ShapeDtypeStruct((S, WIDTH), F32), jax.ShapeDtypeStruct((S, WIDTH), F32),
                   jax.ShapeDtypeStruct((S, WIDTH), F32), jax.ShapeDtypeStruct((N_VARIANTS, N_HEADS, A_Q, A_WIN), F32)),
        grid=(N_HEADS // A_BWD_HEADS, nq),
        in_specs=[qspec, kvspec, kvspec, bspec, qspec],
        out_specs=(qspec, kvspec, kvspec, bspec),
        compiler_params=_cp(("parallel", "arbitrary")),
        name=name,
    )(q, k, v, biasm, do)


def _tri(kind):
    j = lax.broadcasted_iota(jnp.int32, (ATT_Q, ATT_Q), 0)
    s = lax.broadcasted_iota(jnp.int32, (ATT_Q, ATT_Q), 1)
    if kind == "gt":
        m = j > s
    elif kind == "le":
        m = j <= s
    else:
        m = j < s
    return jnp.where(m, 1.0, 0.0).astype(BF16)


def _cum(v, tri):
    hi, lo = _split_hi_lo(v)
    return _dot(hi, tri, 1, 0) + _dot(lo, tri, 1, 0)


def _log_sigmoids(z, mask):
    t = jnp.log(1.0 + jnp.exp(-jnp.abs(z)))
    keep = -(jnp.maximum(z, 0.0) + t)
    take = jnp.minimum(z, 0.0) - t
    return (keep if mask is None else jnp.where(mask, keep, 0.0)), take


def _strictly_before():
    row = lax.broadcasted_iota(jnp.int32, (ATT_Q, ATT_Q), 0)
    col = lax.broadcasted_iota(jnp.int32, (ATT_Q, ATT_Q), 1)
    return col < row


EXIT_LOG = -104.0


def attn_b_fwd(q, k, v, *, gather=None, name):
    S = q.shape[0]
    nq = S // ATT_Q
    plan, shards, fulls = gather if gather is not None else (None, [], [])
    ng = len(shards)

    def body(q_ref, k_ref, v_ref, *rest):
        hp = pl.program_id(0)
        qb = pl.program_id(1)
        o_ref, t_ref, n_ref = rest[2 * ng:2 * ng + 3]
        if plan is not None:
            comm = (rest[:ng], rest[2 * ng + 3:3 * ng + 3], rest[3 * ng + 3:])

            @pl.when(jnp.logical_and(hp == 0, qb == 0))
            def _():
                plan.start(*comm)

        tri = _tri("gt")

        def block(kb, carry, mask):
            ks = pl.multiple_of(kb * ATT_Q, ATT_Q)
            new = []
            for h in range(B_HEADS):
                lanes = slice(h * HEAD_DIM, (h + 1) * HEAD_DIM)
                c, acc = carry[h]
                z = _dot(q_ref[:, lanes], k_ref[pl.ds(ks, ATT_Q), lanes], 1, 1)
                keep, take = _log_sigmoids(z, mask)
                w = jnp.exp(take + (_cum(keep, tri) + c))
                if mask is not None:
                    w = jnp.where(mask, w, 0.0)
                acc = acc + _dot(w.astype(BF16), v_ref[pl.ds(ks, ATT_Q), lanes], 1, 0)
                c = c + jnp.sum(keep, axis=-1, keepdims=True)
                new.append((c, acc))
            return tuple(new)

        def cond(state):
            it, cmax, _ = state
            return jnp.logical_and(it <= qb, cmax >= EXIT_LOG)

        def step(state):
            it, _, carry = state
            carry = block(qb - it, carry, None)
            worst = carry[0][0]
            for h in range(1, B_HEADS):
                worst = jnp.maximum(worst, carry[h][0])
            return it + 1, jnp.max(worst), carry

        init = tuple((jnp.zeros((ATT_Q, 1), F32), jnp.zeros((ATT_Q, HEAD_DIM), F32)) for _ in range(B_HEADS))
        diag = block(qb, init, _strictly_before())
        visited, _, res = lax.while_loop(cond, step, (jnp.int32(1), jnp.float32(0.0), diag))
        o_ref[...] = jnp.concatenate([res[h][1] for h in range(B_HEADS)], axis=1).astype(BF16)
        t_ref[...] = jnp.concatenate([jnp.broadcast_to(res[h][0], (ATT_Q, HEAD_DIM)) for h in range(B_HEADS)], axis=1)
        n_ref[hp, qb] = visited.astype(F32)
        if plan is not None:
            @pl.when(jnp.logical_and(hp == N_HEADS // B_HEADS - 1, qb == nq - 1))
            def _():
                plan.finish(*comm)

    qspec = pl.BlockSpec((ATT_Q, B_HEADS * HEAD_DIM), lambda hp, qb: (qb, hp))
    kvspec = pl.BlockSpec((S, B_HEADS * HEAD_DIM), lambda hp, qb: (0, hp), pipeline_mode=pl.Buffered(1))
    outs = pl.pallas_call(
        body,
        out_shape=(jax.ShapeDtypeStruct((S, WIDTH), BF16), jax.ShapeDtypeStruct((S, WIDTH), F32),
                   jax.ShapeDtypeStruct((N_HEADS // B_HEADS, nq), F32))
        + tuple(jax.ShapeDtypeStruct(f.shape, f.dtype) for f in fulls),
        grid=(N_HEADS // B_HEADS, nq),
        in_specs=[qspec, kvspec, kvspec] + [ANY] * (2 * ng),
        out_specs=(qspec, qspec, pl.BlockSpec(memory_space=pltpu.SMEM)) + tuple([ANY] * ng),
        scratch_shapes=plan.scratch_shapes() if plan is not None else (),
        input_output_aliases={3 + ng + i: 3 + i for i in range(ng)},
        compiler_params=_cp(("arbitrary", "arbitrary")),
        name=name,
    )(q, k, v, *shards, *fulls)
    return outs if plan is None else (outs[0], outs[1], outs[2], list(outs[3:]))


def attn_b_bwd(q, k, v, tot, nblk, do, *, scatter=None, name):
    S = q.shape[0]
    nq = S // ATT_Q
    plan, pairs16 = scatter if scatter is not None else (None, [])
    ns = len(pairs16)

    def body(q_ref, k_ref, v_ref, t_ref, n_ref, do_ref, *rest):
        hp = pl.program_id(0)
        qb = pl.program_id(1)
        dq_ref, dk_ref, dv_ref = rest[ns:ns + 3]
        if plan is not None:
            comm = (rest[:ns], rest[ns + 3:2 * ns + 3], rest[2 * ns + 3:])

            @pl.when(jnp.logical_and(hp == 0, qb == 0))
            def _():
                plan.start(*comm)

        first = jnp.clip(qb + 1 - n_ref[hp, qb].astype(jnp.int32), 0, qb + 1)
        tri_le = _tri("le")
        tri_lt = _tri("lt")

        @pl.when(qb == 0)
        def _():
            dk_ref[...] = jnp.zeros_like(dk_ref)
            dv_ref[...] = jnp.zeros_like(dv_ref)

        def block(kb, carry, mask):
            ks = pl.multiple_of(kb * ATT_Q, ATT_Q)
            new = []
            for h in range(B_HEADS):
                lanes = slice(h * HEAD_DIM, (h + 1) * HEAD_DIM)
                cl, cg, dq = carry[h]
                qh = q_ref[:, lanes]
                doh = do_ref[:, lanes]
                kh = k_ref[pl.ds(ks, ATT_Q), lanes]
                vh = v_ref[pl.ds(ks, ATT_Q), lanes]
                totl = t_ref[:, h * HEAD_DIM:h * HEAD_DIM + 1]
                z = _dot(qh, kh, 1, 1)
                keep, take = _log_sigmoids(z, mask)
                sig = jnp.exp(take)
                w = sig * jnp.exp((totl - cl) - _cum(keep, tri_le))
                if mask is not None:
                    w = jnp.where(mask, w, 0.0)
                g = w * _dot(doh, vh, 1, 1)
                G = _dot(g.astype(BF16), tri_lt, 1, 0) + cg
                dz = g * (1.0 - sig) - sig * G
                if mask is not None:
                    dz = jnp.where(mask, dz, 0.0)
                dz = dz.astype(BF16)
                dq = dq + _dot(dz, kh, 1, 0)
                dk_ref[pl.ds(ks, ATT_Q), lanes] += _dot(dz, qh, 0, 0)
                dv_ref[pl.ds(ks, ATT_Q), lanes] += _dot(w.astype(BF16), doh, 0, 0)
                cl = cl + jnp.sum(keep, axis=-1, keepdims=True)
                cg = cg + jnp.sum(g, axis=-1, keepdims=True)
                new.append((cl, cg, dq))
            return tuple(new)

        init = tuple((jnp.zeros((ATT_Q, 1), F32), jnp.zeros((ATT_Q, 1), F32), jnp.zeros((ATT_Q, HEAD_DIM), F32))
                     for _ in range(B_HEADS))
        res = lax.fori_loop(jnp.minimum(first, qb), qb, lambda kb, carry: block(kb, carry, None), init)
        res = block(qb, res, _strictly_before())
        dq_ref[...] = (jnp.concatenate([res[h][2] for h in range(B_HEADS)], axis=1) * QK_SCALE).astype(BF16)
        if plan is not None:
            @pl.when(jnp.logical_and(hp == N_HEADS // B_HEADS - 1, qb == nq - 1))
            def _():
                plan.finish(*comm)

    qspec = pl.BlockSpec((ATT_Q, B_HEADS * HEAD_DIM), lambda hp, qb: (qb, hp))
    kvspec = pl.BlockSpec((S, B_HEADS * HEAD_DIM), lambda hp, qb: (0, hp), pipeline_mode=pl.Buffered(1))
    outs = pl.pallas_call(
        body,
        out_shape=(jax.ShapeDtypeStruct((S, WIDTH), BF16), jax.ShapeDtypeStruct((S, WIDTH), F32),
                   jax.ShapeDtypeStruct((S, WIDTH), F32)) + (tuple(plan.out_shapes()) if plan is not None else ()),
        grid=(N_HEADS // B_HEADS, nq),
        in_specs=[qspec, kvspec, kvspec, qspec, pl.BlockSpec(memory_space=pltpu.SMEM), qspec] + [ANY] * ns,
        out_specs=(qspec, kvspec, kvspec) + tuple([ANY] * ns),
        scratch_shapes=plan.scratch_shapes() if plan is not None else (),
        compiler_params=_cp(("arbitrary", "arbitrary")),
        name=name,
    )(q, k, v, tot, nblk, do, *pairs16)
    return outs if plan is None else (outs[0], outs[1], outs[2], list(outs[3:]))


U_COLBLK = 6


def _pool_counts(t0, rows):
    t = t0 + lax.broadcasted_iota(jnp.int32, (rows, WIDTH), 0)
    lane_grp = lax.broadcasted_iota(jnp.int32, (rows, WIDTH), 1) // GROUP_DIM
    w2, w4, w8, w16 = POOL_WINDOWS
    win = jnp.where(lane_grp == 0, w2, jnp.where(lane_grp == 1, w4, jnp.where(lane_grp == 2, w8, w16)))
    cnt = jnp.minimum(t + 1, win)
    return 1.0 / cnt.astype(F32), lane_grp


def _window_sums(ext, shift_fn):
    s2 = ext + shift_fn(ext, 1)
    s4 = s2 + shift_fn(s2, 2)
    s8 = s4 + shift_fn(s4, 4)
    s16 = s8 + shift_fn(s8, 8)
    return s2, s4, s8, s16


def _select_group(lane_grp, s2, s4, s8, s16):
    return jnp.where(lane_grp == 0, s2, jnp.where(lane_grp == 1, s4, jnp.where(lane_grp == 2, s8, s16)))


def _pooled_tile(u_ref, h_ref, i, T):
    halo = jnp.where(i > 0, h_ref[...], 0.0)
    ext = jnp.concatenate([halo, u_ref[...]], axis=0)
    n = T + HALO
    sums = _window_sums(ext, lambda v, k: pltpu.roll(v, k, 0))
    inv, lane_grp = _pool_counts(i * T - HALO, n)
    pooled = _select_group(lane_grp, *sums) * inv - ext
    return pooled[HALO:, :]


def pool_fwd(proj, w_pool, scale, *, name):
    S = proj.shape[0]
    T = _tile(S)
    hb = T // HALO

    def body(u_ref, h_ref, w_ref, s_ref, o_ref):
        i = pl.program_id(0)
        pooled = _pooled_tile(u_ref, h_ref, i, T).astype(BF16)
        outs = [_dot(pooled[:, g * GROUP_DIM:(g + 1) * GROUP_DIM], w_ref[g], 1, 0) for g in range(4)]
        o_ref[...] = (jnp.concatenate(outs, axis=1) * s_ref[...]).astype(BF16)

    return pl.pallas_call(
        body,
        out_shape=jax.ShapeDtypeStruct((S, WIDTH), BF16),
        grid=(S // T,),
        in_specs=[pl.BlockSpec((T, WIDTH), lambda i: (i, U_COLBLK)),
                  pl.BlockSpec((HALO, WIDTH), lambda i: (jnp.maximum(i * hb - 1, 0), U_COLBLK)),
                  pl.BlockSpec((4, GROUP_DIM, GROUP_DIM), lambda i: (0, 0, 0)),
                  pl.BlockSpec((1, WIDTH), lambda i: (0, 0))],
        out_specs=pl.BlockSpec((T, WIDTH), lambda i: (i, 0)),
        compiler_params=_cp(("parallel",)),
        name=name,
    )(proj, proj, w_pool, scale)


def pool_bwd(proj, w_pool, scale, do, *, name):
    S = proj.shape[0]
    T = _tile(S)
    hb = T // HALO
    nt = S // T

    def body(u_ref, h_ref, w_ref, s_ref, do_ref, dof_ref, du_ref, dw_ref, ds_ref):
        i = pl.program_id(0)

        @pl.when(i == 0)
        def _():
            dw_ref[...] = jnp.zeros_like(dw_ref)
            ds_ref[...] = jnp.zeros_like(ds_ref)

        pooled = _pooled_tile(u_ref, h_ref, i, T).astype(BF16)
        dov = do_ref[...].astype(F32)
        fut = jnp.where(i < nt - 1, dof_ref[...].astype(F32), 0.0)
        dmix = (jnp.concatenate([dov, fut], axis=0) * s_ref[...]).astype(BF16)
        mixed, dpool = [], []
        for g in range(4):
            lanes = slice(g * GROUP_DIM, (g + 1) * GROUP_DIM)
            mixed.append(_dot(pooled[:, lanes], w_ref[g], 1, 0))
            dw_ref[g] += _dot(pooled[:, lanes], dmix[:T, lanes], 0, 0)
            dpool.append(_dot(dmix[:, lanes], w_ref[g], 1, 1))
        ds_ref[...] += jnp.sum(dov * jnp.concatenate(mixed, axis=1), axis=0, keepdims=True)
        dp = jnp.concatenate(dpool, axis=1)
        n = T + HALO
        inv, lane_grp = _pool_counts(i * T, n)
        sums = _window_sums(dp * inv, lambda v, k: pltpu.roll(v, n - k, 0))
        du = _select_group(lane_grp, *sums) - dp
        du_ref[...] = du[:T, :].astype(BF16)

    row = pl.BlockSpec((T, WIDTH), lambda i: (i, 0))
    return pl.pallas_call(
        body,
        out_shape=(jax.ShapeDtypeStruct((S, WIDTH), BF16), jax.ShapeDtypeStruct((4, GROUP_DIM, GROUP_DIM), F32),
                   jax.ShapeDtypeStruct((1, WIDTH), F32)),
        grid=(nt,),
        in_specs=[pl.BlockSpec((T, WIDTH), lambda i: (i, U_COLBLK)),
                  pl.BlockSpec((HALO, WIDTH), lambda i: (jnp.maximum(i * hb - 1, 0), U_COLBLK)),
                  pl.BlockSpec((4, GROUP_DIM, GROUP_DIM), lambda i: (0, 0, 0)),
                  pl.BlockSpec((1, WIDTH), lambda i: (0, 0)),
                  row,
                  pl.BlockSpec((HALO, WIDTH), lambda i: (jnp.minimum((i + 1) * hb, S // HALO - 1), 0))],
        out_specs=(row, pl.BlockSpec((4, GROUP_DIM, GROUP_DIM), lambda i: (0, 0, 0)),
                   pl.BlockSpec((1, WIDTH), lambda i: (0, 0))),
        compiler_params=_cp(("arbitrary",)),
        name=name,
    )(proj, proj, w_pool, scale, do, do)


GATE_BLK0 = GATE_COL0 // WIDTH


def merge_fwd(oa, ob, oc, proj, b_gate, wa, wb, wc, *, name):
    S = oa.shape[0]
    T = _tile(S)

    def body(oa_ref, ob_ref, oc_ref, ga, gb, gc, ba, bb, bc, wa_ref, wb_ref, wc_ref, m_ref):
        acc = None
        for o_ref, g_ref, b_ref, w_ref in ((oa_ref, ga, ba, wa_ref), (ob_ref, gb, bb, wb_ref), (oc_ref, gc, bc, wc_ref)):
            y = _dot(o_ref[...], w_ref[...], 1, 0)
            t = jax.nn.sigmoid(g_ref[...] + b_ref[...]) * y
            acc = t if acc is None else acc + t
        m_ref[...] = acc.astype(BF16)

    row = pl.BlockSpec((T, WIDTH), lambda i, n: (i, 0))
    gate = lambda b: pl.BlockSpec((T, WIDTH), lambda i, n, b=b: (i, GATE_BLK0 + 2 * b + n))
    bias = lambda b: pl.BlockSpec((1, WIDTH), lambda i, n, b=b: (0, 2 * b + n))
    wspec = pl.BlockSpec((WIDTH, WIDTH), lambda i, n: (0, n))
    return pl.pallas_call(
        body,
        out_shape=jax.ShapeDtypeStruct((S, D_MODEL), BF16),
        grid=(S // T, 2),
        in_specs=[row, row, row, gate(0), gate(1), gate(2), bias(0), bias(1), bias(2), wspec, wspec, wspec],
        out_specs=pl.BlockSpec((T, WIDTH), lambda i, n: (i, n)),
        compiler_params=_cp(("parallel", "parallel")),
        name=name,
    )(oa, ob, oc, proj, proj, proj, b_gate, b_gate, b_gate, wa, wb, wc)


def merge_bwd(dm, oa, ob, oc, proj, b_gate, wa, wb, wc, *, name):
    S = oa.shape[0]
    T = _tile(S)

    def body(dm_ref, oa_ref, ob_ref, oc_ref, ga, gb, gc, ba, bb, bc, wa_ref, wb_ref, wc_ref,
             ta, tb, tc, dga, dgb, dgc, dba, dbb, dbc):
        i = pl.program_id(1)
        dmv = dm_ref[...].astype(F32)
        for o_ref, g_ref, b_ref, w_ref, t_ref, dg_ref, db_ref in (
                (oa_ref, ga, ba, wa_ref, ta, dga, dba), (ob_ref, gb, bb, wb_ref, tb, dgb, dbb),
                (oc_ref, gc, bc, wc_ref, tc, dgc, dbc)):
            y = _dot(o_ref[...], w_ref[...], 1, 0)
            gate = jax.nn.sigmoid(g_ref[...] + b_ref[...])
            t_ref[...] = (gate * dmv).astype(BF16)
            dgl = dmv * y * gate * (1.0 - gate)
            dg_ref[...] = dgl.astype(BF16)

            @pl.when(i == 0)
            def _():
                db_ref[...] = jnp.zeros_like(db_ref)

            db_ref[...] += jnp.sum(dgl, axis=0, keepdims=True)

    row = pl.BlockSpec((T, WIDTH), lambda n, i: (i, 0))
    half = pl.BlockSpec((T, WIDTH), lambda n, i: (i, n))
    gate = lambda b: pl.BlockSpec((T, WIDTH), lambda n, i, b=b: (i, GATE_BLK0 + 2 * b + n))
    bias = lambda b: pl.BlockSpec((1, WIDTH), lambda n, i, b=b: (0, 2 * b + n))
    wspec = pl.BlockSpec((WIDTH, WIDTH), lambda n, i: (0, n))
    bvec = pl.BlockSpec((1, WIDTH), lambda n, i: (0, n))
    act = jax.ShapeDtypeStruct((S, D_MODEL), BF16)
    vec = jax.ShapeDtypeStruct((1, D_MODEL), F32)
    return pl.pallas_call(
        body,
        out_shape=(act, act, act, act, act, act, vec, vec, vec),
        grid=(2, S // T),
        in_specs=[half, row, row, row, gate(0), gate(1), gate(2), bias(0), bias(1), bias(2), wspec, wspec, wspec],
        out_specs=(half, half, half, half, half, half, bvec, bvec, bvec),
        compiler_params=_cp(("parallel", "arbitrary")),
        name=name,
    )(dm, oa, ob, oc, proj, proj, proj, b_gate, b_gate, b_gate, wa, wb, wc)


FF_T = 256
FF_BLKS = D_FF // FF_T


def _silu_parts(x):
    s = jax.nn.sigmoid(x)
    return x * s, s


def _conv3(ext, w_ref, b_ref):
    taps = (pltpu.roll(ext, 2, 0), pltpu.roll(ext, 1, 0), ext)
    return b_ref[...] + w_ref[0:1, :] * taps[0] + w_ref[1:2, :] * taps[1] + w_ref[2:3, :] * taps[2], taps


def conv_glu_fwd(u, conv_w, conv_b, *, name):
    S = u.shape[0]
    T = _tile(S)
    hb = T // CONV_HALO

    def body(ug, ugh, uv, uvh, wg, wv, bg, bv, a_ref):
        i = pl.program_id(1)
        cs = []
        for m_ref, h_ref, w_ref, b_ref in ((ug, ugh, wg, bg), (uv, uvh, wv, bv)):
            halo = jnp.where(i > 0, h_ref[...], 0.0)
            ext = jnp.concatenate([halo, m_ref[...]], axis=0)
            cs.append(_conv3(ext, w_ref, b_ref)[0][CONV_HALO:, :])
        act, _ = _silu_parts(cs[0])
        a_ref[...] = (act * cs[1]).astype(BF16)

    main = lambda o: pl.BlockSpec((T, FF_T), lambda c, i, o=o: (i, c + o))
    halo = lambda o: pl.BlockSpec((CONV_HALO, FF_T), lambda c, i, o=o: (jnp.maximum(i * hb - 1, 0), c + o))
    wsp = lambda o: pl.BlockSpec((3, FF_T), lambda c, i, o=o: (0, c + o))
    bsp = lambda o: pl.BlockSpec((1, FF_T), lambda c, i, o=o: (0, c + o))
    return pl.pallas_call(
        body,
        out_shape=jax.ShapeDtypeStruct((S, D_FF), BF16),
        grid=(FF_BLKS, S // T),
        in_specs=[main(0), halo(0), main(FF_BLKS), halo(FF_BLKS), wsp(0), wsp(FF_BLKS), bsp(0), bsp(FF_BLKS)],
        out_specs=pl.BlockSpec((T, FF_T), lambda c, i: (i, c)),
        compiler_params=_cp(("parallel", "parallel")),
        name=name,
    )(u, u, u, u, conv_w, conv_w, conv_b, conv_b)


def conv_glu_bwd(u, conv_w, conv_b, da, *, name):
    S = u.shape[0]
    T = _tile(S)
    hb = T // CONV_HALO
    nt = S // T
    n = T + 2 * CONV_HALO

    def body(ug, ugp, ugf, uv, uvp, uvf, wg, wv, bg, bv, da_ref, daf_ref,
             dug, duv, dwg, dwv, dbg, dbv):
        i = pl.program_id(1)
        first, last = i == 0, i == nt - 1
        taps, cs = [], []
        for m_ref, p_ref, f_ref, w_ref, b_ref in ((ug, ugp, ugf, wg, bg), (uv, uvp, uvf, wv, bv)):
            ext = jnp.concatenate([jnp.where(first, 0.0, p_ref[...]), m_ref[...], jnp.where(last, 0.0, f_ref[...])], axis=0)
            c, tp = _conv3(ext, w_ref, b_ref)
            cs.append(c)
            taps.append(tp)
        dae = jnp.concatenate([jnp.zeros((CONV_HALO, FF_T), F32), da_ref[...].astype(F32),
                               jnp.where(last, 0.0, daf_ref[...].astype(F32))], axis=0)
        act, sg = _silu_parts(cs[0])
        dcs = (dae * cs[1] * (sg * (1.0 + cs[0] * (1.0 - sg))), dae * act)
        main = slice(CONV_HALO, CONV_HALO + T)
        for tp, dc, w_ref, du_ref, dw_ref, db_ref in ((taps[0], dcs[0], wg, dug, dwg, dbg),
                                                      (taps[1], dcs[1], wv, duv, dwv, dbv)):
            du = (w_ref[2:3, :] * dc + w_ref[1:2, :] * pltpu.roll(dc, n - 1, 0) + w_ref[0:1, :] * pltpu.roll(dc, n - 2, 0))
            du_ref[...] = du[main, :].astype(BF16)
            dcm = dc[main, :]
            rows = [jnp.sum(dcm * tp[j][main, :], axis=0, keepdims=True) for j in range(3)]

            @pl.when(first)
            def _():
                dw_ref[...] = jnp.zeros_like(dw_ref)
                db_ref[...] = jnp.zeros_like(db_ref)

            dw_ref[...] += jnp.concatenate(rows, axis=0)
            db_ref[...] += jnp.sum(dcm, axis=0, keepdims=True)

    main = lambda o: pl.BlockSpec((T, FF_T), lambda c, i, o=o: (i, c + o))
    past = lambda o: pl.BlockSpec((CONV_HALO, FF_T), lambda c, i, o=o: (jnp.maximum(i * hb - 1, 0), c + o))
    fut = lambda o: pl.BlockSpec((CONV_HALO, FF_T), lambda c, i, o=o: (jnp.minimum((i + 1) * hb, S // CONV_HALO - 1), c + o))
    wsp = lambda o: pl.BlockSpec((3, FF_T), lambda c, i, o=o: (0, c + o))
    bsp = lambda o: pl.BlockSpec((1, FF_T), lambda c, i, o=o: (0, c + o))
    return pl.pallas_call(
        body,
        out_shape=(jax.ShapeDtypeStruct((S, D_FF), BF16), jax.ShapeDtypeStruct((S, D_FF), BF16),
                   jax.ShapeDtypeStruct((3, D_FF), F32), jax.ShapeDtypeStruct((3, D_FF), F32),
                   jax.ShapeDtypeStruct((1, D_FF), F32), jax.ShapeDtypeStruct((1, D_FF), F32)),
        grid=(FF_BLKS, nt),
        in_specs=[main(0), past(0), fut(0), main(FF_BLKS), past(FF_BLKS), fut(FF_BLKS),
                  wsp(0), wsp(FF_BLKS), bsp(0), bsp(FF_BLKS), main(0), fut(0)],
        out_specs=(main(0), main(0), wsp(0), wsp(0), bsp(0), bsp(0)),
        compiler_params=_cp(("parallel", "arbitrary")),
        name=name,
    )(u, u, u, u, u, u, conv_w, conv_w, conv_b, conv_b, da, da)


def loss_head(y, target, *, name):
    S, D = y.shape
    T = _tile(S)

    def body(y_ref, t_ref, dy_ref, l_ref):
        i = pl.program_id(0)
        err = y_ref[...] - t_ref[...]
        dy_ref[...] = err * (1.0 / D)

        @pl.when(i == 0)
        def _():
            l_ref[...] = jnp.zeros_like(l_ref)

        l_ref[...] += 0.5 * jnp.sum(jnp.mean(err * err, axis=-1, keepdims=True))

    row = pl.BlockSpec((T, D), lambda i: (i, 0))
    return pl.pallas_call(
        body,
        out_shape=(jax.ShapeDtypeStruct((S, D), F32), jax.ShapeDtypeStruct((8, 128), F32)),
        grid=(S // T,),
        in_specs=[row, row],
        out_specs=(row, pl.BlockSpec((8, 128), lambda i: (0, 0))),
        compiler_params=_cp(("arbitrary",)),
        name=name,
    )(y, target)


ELEMS_PER_BLOCK = 256 * 1024


def _rows_tile(rows, cols):
    if rows * cols <= ELEMS_PER_BLOCK or rows % 8:
        return rows
    best = 8
    for tr in range(8, rows + 1, 8):
        if rows % tr == 0 and tr * cols <= ELEMS_PER_BLOCK:
            best = tr
    return best


def _adamw_math(g, w_ref, m_ref, v_ref, g_out, d_out, m_out, v_out):
    mn = ADAM_B1 * m_ref[...] + (1.0 - ADAM_B1) * g
    vn = ADAM_B2 * v_ref[...] + (1.0 - ADAM_B2) * (g * g)
    m_hat = mn / (1.0 - ADAM_B1 ** ADAM_STEP)
    v_hat = vn / (1.0 - ADAM_B2 ** ADAM_STEP)
    g_out[...] = g
    d_out[...] = -ADAM_LR * (m_hat / (jnp.sqrt(v_hat) + ADAM_EPS) + ADAM_WD * w_ref[...])
    m_out[...] = mn
    v_out[...] = vn


def adamw(w, m, v, g, *, name):
    rows, cols = w.shape
    tr = _rows_tile(rows, cols)

    def body(w_ref, m_ref, v_ref, g_ref, g_out, d_out, m_out, v_out):
        _adamw_math(g_ref[...], w_ref, m_ref, v_ref, g_out, d_out, m_out, v_out)

    spec = pl.BlockSpec((tr, cols), lambda i: (i, 0))
    shp = jax.ShapeDtypeStruct((rows, cols), F32)
    return pl.pallas_call(
        body,
        out_shape=(shp, shp, shp, shp),
        grid=(rows // tr,),
        in_specs=[spec] * 4,
        out_specs=(spec, spec, spec, spec),
        compiler_params=_cp(("parallel",)),
        name=name,
    )(w, m, v, g)


ANY = pl.BlockSpec(memory_space=pl.ANY)
STAGE_BYTES = 2 * 1024 * 1024


def _mesh_pos():
    return lax.axis_index("x"), lax.axis_index("y"), lax.axis_index("c")


def _chip_peers(x, y):
    return [(1 - x, y), (x, 1 - y), (1 - x, 1 - y)]


def _all_peers(x, y, c):
    return [((1 - x) if (r >> 2) & 1 else x, (1 - y) if (r >> 1) & 1 else y, (1 - c) if r & 1 else c)
            for r in range(1, 8)]


class LayerGather:
    def __init__(self, shards, axes, layer):
        self.nt = len(shards)
        self.axes = list(axes)
        self.layer = layer
        self.shapes = [s.shape for s in shards]
        self.dtypes = [s.dtype for s in shards]
        self.sizes = [s.shape[a] for s, a in zip(shards, axes)]
        self.split = [s.shape[1] % 32 == 0 for s in shards]
        self.half_rows = [s.shape[1] // 2 if sp else s.shape[1] for s, sp in zip(shards, self.split)]
        self.chunk_rows = []
        for s in shards:
            rt = s.shape[1]
            while rt % 32 == 0 and rt * s.shape[2] * s.dtype.itemsize > STAGE_BYTES:
                rt //= 2
            self.chunk_rows.append(rt)

    def out_shapes(self):
        out = []
        for shp, a, sz, dt in zip(self.shapes, self.axes, self.sizes, self.dtypes):
            shp = list(shp)
            shp[a] = 4 * sz
            out.append(jax.ShapeDtypeStruct(tuple(shp), dt))
        return out

    def scratch_shapes(self):
        return ([pltpu.VMEM((1, rt, shp[2]), dt) for shp, rt, dt in zip(self.shapes, self.chunk_rows, self.dtypes)]
                + [pltpu.SemaphoreType.DMA((2 * self.nt,))] + [pltpu.SemaphoreType.DMA((3 * self.nt,)) for _ in range(4)])

    def _views(self, ins, outs, scratch):
        nt = self.nt
        stage, stage_sems = scratch[:nt], scratch[nt]
        ici_send, ici_recv, d2d_send, d2d_recv = scratch[nt + 1:]
        x, y, c = _mesh_pos()
        mine = 2 * x + y
        peers = _chip_peers(x, y)
        layer = pl.ds(self.layer, 1)

        def rows(t, half, r0=0, n=None):
            hr = self.half_rows[t]
            if n is None:
                return pl.ds(pl.multiple_of(half * hr, 16), hr) if self.split[t] else pl.ds(0, hr)
            return pl.ds(r0, n)

        def placed(t, blk, row_sel, row_len):
            sz = self.sizes[t]
            if self.axes[t] == 2:
                return outs[t].at[layer, row_sel, pl.ds(pl.multiple_of(blk * sz, 128), sz)]
            return outs[t].at[layer, pl.ds(pl.multiple_of(blk * sz, 16) + row_sel.start, row_len), :]

        def ici(t, k, blk):
            px, py = peers[k]
            sel = rows(t, c)
            return pltpu.make_async_remote_copy(
                src_ref=ins[t].at[layer, sel, :], dst_ref=placed(t, blk, sel, self.half_rows[t]),
                send_sem=ici_send.at[3 * t + k], recv_sem=ici_recv.at[3 * t + k],
                device_id=(px, py, c), device_id_type=MESH_T)

        def d2d(t, k, half):
            px, py = peers[k]
            piece = placed(t, 2 * px + py, rows(t, half), self.half_rows[t])
            return pltpu.make_async_remote_copy(
                src_ref=piece, dst_ref=piece, send_sem=d2d_send.at[3 * t + k], recv_sem=d2d_recv.at[3 * t + k],
                device_id=(x, y, 1 - c), device_id_type=MESH_T)

        def own_chunk(t, r0):
            rt = self.chunk_rows[t]
            sel = pl.ds(r0, rt)
            return ins[t].at[layer, sel, :], placed(t, mine, sel, rt), stage[t], stage_sems

        return c, mine, peers, ici, d2d, own_chunk

    def start(self, ins, outs, scratch):
        c, mine, peers, ici, d2d, own_chunk = self._views(ins, outs, scratch)
        for t in range(self.nt):
            for k in range(3):
                ici(t, k, mine).start()
        starts = [list(range(0, self.shapes[t][1], self.chunk_rows[t])) for t in range(self.nt)]
        for r in range(max(len(s) for s in starts)):
            active = [(t, *own_chunk(t, starts[t][r])) for t in range(self.nt) if r < len(starts[t])]
            loads = [pltpu.make_async_copy(src, buf, sems.at[2 * t]) for t, src, dst, buf, sems in active]
            for cp in loads:
                cp.start()
            for cp in loads:
                cp.wait()
            stores = [pltpu.make_async_copy(buf, dst, sems.at[2 * t + 1]) for t, src, dst, buf, sems in active]
            for cp in stores:
                cp.start()
            for cp in stores:
                cp.wait()

    def finish(self, ins, outs, scratch):
        c, mine, peers, ici, d2d, own_chunk = self._views(ins, outs, scratch)
        for t in range(self.nt):
            for k, (px, py) in enumerate(peers):
                ici(t, k, 2 * px + py).wait_recv()
                if self.split[t]:
                    d2d(t, k, c).start()
        for t in range(self.nt):
            for k in range(3):
                if self.split[t]:
                    d2d(t, k, 1 - c).wait_recv()
        for t in range(self.nt):
            for k in range(3):
                ici(t, k, mine).wait_send()
                if self.split[t]:
                    d2d(t, k, c).wait_send()


def all_gather_layer(shards, axes, layer, *, name):
    plan = LayerGather(shards, axes, layer)
    nt = plan.nt

    def body(*refs):
        ins, outs, scratch = refs[:nt], refs[nt:2 * nt], refs[2 * nt:]
        plan.start(ins, outs, scratch)
        plan.finish(ins, outs, scratch)

    return pl.pallas_call(
        body,
        out_shape=tuple(plan.out_shapes()),
        in_specs=[ANY] * nt,
        out_specs=tuple([ANY] * nt),
        scratch_shapes=plan.scratch_shapes(),
        name=name,
    )(*shards)


class HalfLayout:
    def __init__(self, shape, axis):
        self.R, self.C = shape
        self.axis = axis
        if axis == 1:
            self.hr, self.pw = self.R // 2, self.C // 4
            self.half_shape = (self.hr, self.C)
        else:
            self.hr, self.pw = self.R // 8, self.C
            self.half_shape = (4 * self.hr, self.C)
        self.tr = _rows_tile(self.hr, self.pw)
        self.nr = self.hr // self.tr

    def in_grad(self, ref, blk, half):
        if self.axis == 1:
            return ref.at[pl.ds(pl.multiple_of(half * self.hr, 16), self.hr), pl.ds(pl.multiple_of(blk * self.pw, 128), self.pw)]
        return ref.at[pl.ds(pl.multiple_of((2 * blk + half) * self.hr, 16), self.hr), :]

    def in_half(self, ref, blk):
        if self.axis == 1:
            return ref.at[:, pl.ds(pl.multiple_of(blk * self.pw, 128), self.pw)]
        return ref.at[pl.ds(pl.multiple_of(blk * self.hr, 16), self.hr), :]

    def grad_spec(self):
        if self.axis == 1:
            return pl.BlockSpec((self.tr, self.pw), lambda j, i, s: (s[0] * self.nr + i, j))
        return pl.BlockSpec((self.tr, self.pw), lambda j, i, s: ((2 * j + s[0]) * self.nr + i, 0))

    def half_spec(self):
        if self.axis == 1:
            return pl.BlockSpec((self.tr, self.pw), lambda j, i, s: (i, j))
        return pl.BlockSpec((self.tr, self.pw), lambda j, i, s: (j * self.nr + i, 0))


def half_exchange(grads, layouts, *, name):
    nt = len(grads)
    pieces = [(t, j) for t in range(nt) for j in (range(4) if layouts[t].axis == 0 else range(1))]

    def body(*refs):
        ins, outs = refs[:nt], refs[nt:2 * nt]
        send_sems, recv_sems = refs[2 * nt:]
        x, y, c = _mesh_pos()
        cps = []
        for n, (t, j) in enumerate(pieces):
            lay = layouts[t]
            if lay.axis == 1:
                src = ins[t].at[pl.ds(pl.multiple_of((1 - c) * lay.hr, 16), lay.hr), :]
                dst = outs[t]
            else:
                src = lay.in_grad(ins[t], j, 1 - c)
                dst = lay.in_half(outs[t], j)
            cp = pltpu.make_async_remote_copy(src_ref=src, dst_ref=dst, send_sem=send_sems.at[n], recv_sem=recv_sems.at[n],
                                              device_id=(x, y, 1 - c), device_id_type=MESH_T)
            cp.start()
            cps.append(cp)
        for cp in cps:
            cp.wait_recv()
        for cp in cps:
            cp.wait_send()

    return pl.pallas_call(
        body,
        out_shape=tuple(jax.ShapeDtypeStruct(lay.half_shape, F32) for lay in layouts),
        in_specs=[ANY] * nt,
        out_specs=tuple([ANY] * nt),
        scratch_shapes=[pltpu.SemaphoreType.DMA((len(pieces),)), pltpu.SemaphoreType.DMA((len(pieces),))],
        name=name,
    )(*grads)


def pair_sum(grad, other, lay, core, *, name):
    def body(c_ref, g_ref, o_ref, s32_ref, s16_ref):
        s = g_ref[...] + o_ref[...]
        s32_ref[...] = s
        s16_ref[...] = s.astype(BF16)

    return pl.pallas_call(
        body,
        out_shape=(jax.ShapeDtypeStruct(lay.half_shape, F32), jax.ShapeDtypeStruct(lay.half_shape, BF16)),
        grid_spec=pltpu.PrefetchScalarGridSpec(
            num_scalar_prefetch=1, grid=(4, lay.nr),
            in_specs=[lay.grad_spec(), lay.half_spec()],
            out_specs=(lay.half_spec(), lay.half_spec())),
        compiler_params=_cp(("parallel", "parallel")),
        name=name,
    )(core, grad, other)


class BlockScatter:
    def __init__(self, layouts):
        self.layouts = layouts
        self.nt = len(layouts)

    def out_shapes(self):
        return [jax.ShapeDtypeStruct((3, lay.hr, lay.pw), BF16) for lay in self.layouts]

    def scratch_shapes(self):
        return [pltpu.SemaphoreType.DMA((3 * self.nt,)), pltpu.SemaphoreType.DMA((3 * self.nt,))]

    def _copies(self, pairs16, recv, scratch):
        send_sems, recv_sems = scratch
        x, y, c = _mesh_pos()
        return [pltpu.make_async_remote_copy(
            src_ref=lay.in_half(pairs16[t], 2 * px + py), dst_ref=recv[t].at[k],
            send_sem=send_sems.at[3 * t + k], recv_sem=recv_sems.at[3 * t + k],
            device_id=(px, py, c), device_id_type=MESH_T)
            for t, lay in enumerate(self.layouts) for k, (px, py) in enumerate(_chip_peers(x, y))]

    def start(self, pairs16, recv, scratch):
        for cp in self._copies(pairs16, recv, scratch):
            cp.start()

    def finish(self, pairs16, recv, scratch):
        copies = self._copies(pairs16, recv, scratch)
        for cp in copies:
            cp.wait_recv()
        for cp in copies:
            cp.wait_send()


def gather_small(small, *, name):
    def body(small_in, small_out, ssend, srecv):
        x, y, c = _mesh_pos()
        me = 4 * x + 2 * y + c
        sends, recvs = [], []
        for r, (px, py, pc) in enumerate(_all_peers(x, y, c)):
            def mk(slot, r=r, px=px, py=py, pc=pc):
                return pltpu.make_async_remote_copy(
                    src_ref=small_in, dst_ref=small_out.at[slot], send_sem=ssend.at[r], recv_sem=srecv.at[r],
                    device_id=(px, py, pc), device_id_type=MESH_T)
            snd = mk(me)
            snd.start()
            sends.append(snd)
            recvs.append(mk(4 * px + 2 * py + pc))
        for r in recvs:
            r.wait_recv()
        for s in sends:
            s.wait_send()

    return pl.pallas_call(
        body,
        out_shape=jax.ShapeDtypeStruct((8,) + small.shape, F32),
        in_specs=[ANY],
        out_specs=ANY,
        scratch_shapes=[pltpu.SemaphoreType.DMA((7,)), pltpu.SemaphoreType.DMA((7,))],
        name=name,
    )(small)


def sum_chips(pair32, recv, lay, chip, *, name):
    def body(j_ref, p_ref, r_ref, s_ref):
        acc = p_ref[...]
        for k in range(3):
            acc = acc + r_ref[k].astype(F32)
        s_ref[...] = acc

    if lay.axis == 1:
        own = pl.BlockSpec((lay.tr, lay.pw), lambda i, j: (i, j[0]))
    else:
        own = pl.BlockSpec((lay.tr, lay.pw), lambda i, j: (j[0] * lay.nr + i, 0))
    return pl.pallas_call(
        body,
        out_shape=jax.ShapeDtypeStruct((lay.hr, lay.pw), F32),
        grid_spec=pltpu.PrefetchScalarGridSpec(
            num_scalar_prefetch=1, grid=(lay.nr,),
            in_specs=[own, pl.BlockSpec((3, lay.tr, lay.pw), lambda i, j: (0, i, 0))],
            out_specs=pl.BlockSpec((lay.tr, lay.pw), lambda i, j: (i, 0))),
        compiler_params=_cp(("parallel",)),
        name=name,
    )(chip, pair32, recv)


def sum_devices(gathered, own, me, *, name):
    _, R, C = gathered.shape

    def body(me_ref, g_ref, o_ref, s_ref):
        acc = None
        for k in range(8):
            part = jnp.where(me_ref[0] == k, o_ref[...], g_ref[k])
            acc = part if acc is None else acc + part
        s_ref[...] = acc

    return pl.pallas_call(
        body,
        out_shape=jax.ShapeDtypeStruct((R, C), F32),
        grid_spec=pltpu.PrefetchScalarGridSpec(
            num_scalar_prefetch=1, grid=(1,),
            in_specs=[pl.BlockSpec((8, R, C), lambda i, m: (0, 0, 0)), pl.BlockSpec((R, C), lambda i, m: (0, 0))],
            out_specs=pl.BlockSpec((R, C), lambda i, m: (0, 0))),
        compiler_params=_cp(("arbitrary",)),
        name=name,
    )(me, gathered, own)


def sibling_swap(parts, *, name):
    nt = len(parts)

    def body(*refs):
        ins, outs = refs[:nt], refs[nt:2 * nt]
        send_sems, recv_sems = refs[2 * nt:]
        x, y, c = _mesh_pos()
        cps = []
        for t in range(nt):
            cp = pltpu.make_async_remote_copy(src_ref=ins[t], dst_ref=outs[t], send_sem=send_sems.at[t],
                                              recv_sem=recv_sems.at[t], device_id=(x, y, 1 - c), device_id_type=MESH_T)
            cp.start()
            cps.append(cp)
        for cp in cps:
            cp.wait_recv()
        for cp in cps:
            cp.wait_send()

    return pl.pallas_call(
        body,
        out_shape=tuple(jax.ShapeDtypeStruct(p.shape, p.dtype) for p in parts),
        in_specs=[ANY] * nt,
        out_specs=tuple([ANY] * nt),
        scratch_shapes=[pltpu.SemaphoreType.DMA((nt,)), pltpu.SemaphoreType.DMA((nt,))],
        name=name,
    )(*parts)


def adamw_halves(w, m, v, mine, other, lay, core, *, name):
    _, r, c = w.shape
    tr, nr = lay.tr, lay.nr
    assert (r, c) == (2 * lay.hr, lay.pw), (w.shape, lay.hr, lay.pw)

    def body(c_ref, w_ref, m_ref, v_ref, *rest):
        g_refs, outs = rest[:2 * DEPTH], rest[2 * DEPTH:]
        l, h = pl.program_id(0), pl.program_id(1)
        g = None
        for d in range(DEPTH):
            gd = jnp.where(h == c_ref[0], g_refs[d][...], g_refs[DEPTH + d][...])
            g = gd if g is None else jnp.where(l == d, gd, g)
        _adamw_math(g, w_ref, m_ref, v_ref, *outs)

    full = pl.BlockSpec((None, tr, c), lambda l, h, i, s: (l, h * nr + i, 0))

    def part(d, is_mine):
        def index(l, h, i, s):
            used = jnp.logical_and(l == d, (h == s[0]) == is_mine)
            return jnp.where(used, i, 0), 0
        return pl.BlockSpec((tr, c), index)

    shp = jax.ShapeDtypeStruct((DEPTH, r, c), F32)
    return pl.pallas_call(
        body,
        out_shape=(shp, shp, shp, shp),
        grid_spec=pltpu.PrefetchScalarGridSpec(
            num_scalar_prefetch=1, grid=(DEPTH, 2, nr),
            in_specs=[full, full, full] + [part(d, True) for d in range(DEPTH)] + [part(d, False) for d in range(DEPTH)],
            out_specs=(full, full, full, full)),
        compiler_params=_cp(("arbitrary", "arbitrary", "arbitrary")),
        name=name,
    )(core, w, m, v, *mine, *other)


WEIGHTS = ("norm_mix", "w_in", "b_gate", "q_norm_a", "k_norm_a", "rel_bias_a", "w_pool", "pool_scale",
           "w_branch_a", "w_branch_b", "w_branch_c", "w_out", "norm_ffn", "w_up", "conv_w", "conv_b", "w_down")
SHARDED = {"w_in": 2, "w_branch_a": 2, "w_branch_b": 2, "w_branch_c": 2, "w_out": 1, "w_up": 2, "conv_w": 2,
           "w_down": 1}
REPLICATED = tuple(n for n in WEIGHTS if n not in SHARDED)
MATMUL_WEIGHTS = tuple(n for n in SHARDED if n != "conv_w")
SMALL_WEIGHTS = tuple(n for n in WEIGHTS if n not in MATMUL_WEIGHTS)
SMALL_ROWS = 1496


def _layer_fwd(x, p, full, tables, rest=None, prefetch=None):
    l = p["l"]
    diag = exact_dot(p["rel_bias_a"], tables["onehot_t"], name="bias_diagonals")
    diag = diag.reshape(N_HEADS, N_VARIANTS, 1, DIAG_W).transpose(1, 0, 2, 3)
    biasm = bias_expand(diag, name="bias_expand")
    gq8 = jnp.tile(p["q_norm_a"], N_HEADS)[None]
    gk8 = jnp.tile(p["k_norm_a"], N_HEADS)[None]
    h = rmsnorm_fwd(x, p["norm_mix"][None], name="rmsnorm_fwd")
    if rest is None:
        proj = matmul(h, full["w_in"], b_layer=l, name="mm_in")
    else:
        plan, shards, names = rest
        proj, gathered = matmul(h, full["w_in"], b_layer=l, behind=(plan, shards), name="mm_in_gather")
        full = {**full, **dict(zip(names, gathered))}
    qa, ka, va, qb, kb, vb = qkv_prep(proj, gq8, gk8, name="qkv_prep")
    oa = attn_a_fwd(qa, ka, va, biasm, name="attn_a_fwd")
    if prefetch is None:
        ob, tot, nblk = attn_b_fwd(qb, kb, vb, name="attn_b_fwd")
    else:
        plan, shards, names = prefetch
        ob, tot, nblk, filled = attn_b_fwd(qb, kb, vb, gather=(plan, shards, [full[n] for n in names]),
                                           name="attn_b_fwd_gather")
        full = dict(zip(names, filled))
    wpool = p["w_pool"].astype(BF16)
    oc = pool_fwd(proj, wpool, p["pool_scale"][None], name="pool_fwd")
    merged = merge_fwd(oa, ob, oc, proj, p["b_gate"][None], full["w_branch_a"][l], full["w_branch_b"][l],
                       full["w_branch_c"][l], name="merge_fwd")
    x1 = matmul(merged, full["w_out"], b_layer=l, add=x, name="mm_out")
    h2 = rmsnorm_fwd(x1, p["norm_ffn"][None], name="rmsnorm_fwd")
    u = matmul(h2, full["w_up"], b_layer=l, name="mm_up")
    a = conv_glu_fwd(u, full["conv_w"][l], p["conv_b"][None], name="conv_glu_fwd")
    x2 = matmul(a, full["w_down"], b_layer=l, add=x1, name="mm_down")
    saved = dict(x=x, h=h, proj=proj, qa=qa, ka=ka, va=va, qb=qb, kb=kb, vb=vb, oa=oa, ob=ob, tot=tot, nblk=nblk, oc=oc,
                 merged=merged, x1=x1, h2=h2, u=u, a=a, biasm=biasm, gq8=gq8, gk8=gk8, wpool=wpool)
    return x2, saved, full


class GradReducer:
    def __init__(self, layouts, core, chip):
        self.layouts, self.core, self.chip = layouts, core, chip
        self.pairs32, self.received, self.pending = {}, {}, []

    def prepare(self, layer, grads):
        names = list(grads)
        lays = [self.layouts[n] for n in names]
        others = half_exchange([grads[n] for n in names], lays, name="half_exchange")
        for n, lay, other in zip(names, lays, others):
            p32, p16 = pair_sum(grads[n], other, lay, self.core, name="pair_sum")
            self.pairs32[(layer, n)] = p32
            self.pending.append(((layer, n), p16))

    def take(self):
        keys = [k for k, _ in self.pending]
        pairs16 = [p for _, p in self.pending]
        self.pending = []
        return keys, BlockScatter([self.layouts[n] for _, n in keys]), pairs16

    def store(self, keys, received):
        self.received.update(zip(keys, received))

    def finish(self):
        keys = list(self.pairs32)
        mine = [sum_chips(self.pairs32[k], self.received[k], self.layouts[k[1]], self.chip, name="sum_chips") for k in keys]
        other = sibling_swap(mine, name="sibling_swap")
        return dict(zip(keys, mine)), dict(zip(keys, other))


EARLY_WEIGHTS = ("w_down", "w_up", "w_out", "w_branch_a", "w_branch_b", "w_branch_c")


def _layer_bwd(dx2, s, p, tables, reducer):
    g = {}
    full, l = p["full"], p["l"]
    da = matmul(dx2, full["w_down"], b_layer=l, tb=True, name="mm_down_dx")
    g["w_down"] = matmul(s["a"], dx2, ta=True, name="mm_down_dw")
    dug, duv, dcwg, dcwv, dcbg, dcbv = conv_glu_bwd(s["u"], p["conv_w"], p["conv_b"][None], da, name="conv_glu_bwd")
    du = jnp.concatenate([dug, duv], axis=1)
    g["conv_w"] = jnp.concatenate([dcwg, dcwv], axis=1)
    g["conv_b"] = jnp.concatenate([dcbg, dcbv], axis=1)[0]
    g["w_up"] = matmul(s["h2"], du, ta=True, name="mm_up_dw")
    dx1, dg2 = matmul(du, full["w_up"], b_layer=l, tb=True, norm_bwd=(s["x1"], p["norm_ffn"][None], dx2),
                      name="mm_up_dx_norm")
    g["norm_ffn"] = dg2[0]
    dmerged = matmul(dx1, full["w_out"], b_layer=l, tb=True, name="mm_out_dx")
    g["w_out"] = matmul(s["merged"], dx1, ta=True, name="mm_out_dw")
    t_a, t_b, t_c, dga, dgb, dgc, dba, dbb, dbc = merge_bwd(
        dmerged, s["oa"], s["ob"], s["oc"], s["proj"], p["b_gate"][None], p["w_branch_a"], p["w_branch_b"],
        p["w_branch_c"], name="merge_bwd")
    g["b_gate"] = jnp.concatenate([dba, dbb, dbc], axis=1)[0]
    g["w_branch_a"] = matmul(s["oa"], t_a, ta=True, name="mm_branch_dw")
    g["w_branch_b"] = matmul(s["ob"], t_b, ta=True, name="mm_branch_dw")
    g["w_branch_c"] = matmul(s["oc"], t_c, ta=True, name="mm_branch_dw")
    doa = matmul(t_a, full["w_branch_a"], b_layer=l, tb=True, out_dtype=BF16, name="mm_branch_dx")
    dob = matmul(t_b, full["w_branch_b"], b_layer=l, tb=True, out_dtype=BF16, name="mm_branch_dx")
    doc = matmul(t_c, full["w_branch_c"], b_layer=l, tb=True, name="mm_branch_dx_f32")
    dqh, dkh, dva, dbias = attn_a_bwd(s["qa"], s["ka"], s["va"], s["biasm"], doa, name="attn_a_bwd")
    ddiag = relbias_reduce(dbias, name="relbias_reduce")
    ddiag = ddiag.transpose(1, 0, 2, 3).reshape(N_HEADS, N_VARIANTS * DIAG_W)
    g["rel_bias_a"] = exact_dot(ddiag, tables["onehot"], name="relbias_table")
    dqa, dka, dgq8, dgk8 = qknorm_bwd(s["proj"], s["gq8"], s["gk8"], dqh, dkh, name="qknorm_bwd")
    g["q_norm_a"] = dgq8.reshape(N_HEADS, HEAD_DIM).sum(axis=0)
    g["k_norm_a"] = dgk8.reshape(N_HEADS, HEAD_DIM).sum(axis=0)
    reducer.prepare(l, {n: g[n] for n in EARLY_WEIGHTS})
    keys, plan, pairs16 = reducer.take()
    dqb, dkb, dvb, received = attn_b_bwd(s["qb"], s["kb"], s["vb"], s["tot"], s["nblk"], dob, scatter=(plan, pairs16),
                                         name="attn_b_bwd_scatter")
    reducer.store(keys, received)
    duc, dwp, dsc = pool_bwd(s["proj"], s["wpool"], p["pool_scale"][None], doc, name="pool_bwd")
    g["w_pool"] = dwp
    g["pool_scale"] = dsc[0]
    dproj = jnp.concatenate([dqa, dka, dva.astype(BF16), dqb, dkb.astype(BF16), dvb.astype(BF16), duc,
                             dga, dgb, dgc], axis=1)
    g["w_in"] = matmul(s["h"], dproj, ta=True, name="mm_in_dw")
    reducer.prepare(l, {"w_in": g["w_in"]})
    norm = (s["x"], p["norm_mix"][None], dx1)
    if l > 0:
        dx, dg1 = matmul(dproj, full["w_in"], b_layer=l, tb=True, norm_bwd=norm, name="mm_in_dx_norm")
    else:
        keys, plan, pairs16 = reducer.take()
        dx, dg1, received = matmul(dproj, full["w_in"], b_layer=l, tb=True, norm_bwd=norm, behind=(plan, pairs16),
                                   name="mm_in_dx_norm_scatter")
        reducer.store(keys, received)
    g["norm_mix"] = dg1[0]
    return dx, g


def kernel(x, norm_mix, w_in, b_gate, q_norm_a, k_norm_a, rel_bias_a, w_pool, pool_scale, w_branch_a, w_branch_b, w_branch_c, w_out, norm_ffn, w_up, conv_w, conv_b, w_down, loss_target, m_norm_mix, m_w_in, m_b_gate, m_q_norm_a, m_k_norm_a, m_rel_bias_a, m_w_pool, m_pool_scale, m_w_branch_a, m_w_branch_b, m_w_branch_c, m_w_out, m_norm_ffn, m_w_up, m_conv_w, m_conv_b, m_w_down, v_norm_mix, v_w_in, v_b_gate, v_q_norm_a, v_k_norm_a, v_rel_bias_a, v_w_pool, v_pool_scale, v_w_branch_a, v_w_branch_b, v_w_branch_c, v_w_out, v_norm_ffn, v_w_up, v_conv_w, v_conv_b, v_w_down):
    w = dict(zip(WEIGHTS, (norm_mix, w_in, b_gate, q_norm_a, k_norm_a, rel_bias_a, w_pool, pool_scale, w_branch_a,
                           w_branch_b, w_branch_c, w_out, norm_ffn, w_up, conv_w, conv_b, w_down)))
    m = dict(zip(WEIGHTS, (m_norm_mix, m_w_in, m_b_gate, m_q_norm_a, m_k_norm_a, m_rel_bias_a, m_w_pool, m_pool_scale,
                           m_w_branch_a, m_w_branch_b, m_w_branch_c, m_w_out, m_norm_ffn, m_w_up, m_conv_w, m_conv_b,
                           m_w_down)))
    v = dict(zip(WEIGHTS, (v_norm_mix, v_w_in, v_b_gate, v_q_norm_a, v_k_norm_a, v_rel_bias_a, v_w_pool, v_pool_scale,
                           v_w_branch_a, v_w_branch_b, v_w_branch_c, v_w_out, v_norm_ffn, v_w_up, v_conv_w, v_conv_b,
                           v_w_down)))
    onehot = diagonal_onehot()
    tables = dict(onehot=jnp.asarray(onehot), onehot_t=jnp.asarray(np.ascontiguousarray(onehot.T)))

    names = tuple(SHARDED)
    shards = [w[n] if n == "conv_w" else w[n].astype(BF16) for n in names]
    axes = [SHARDED[n] for n in names]
    later = [i for i, n in enumerate(names) if n != "w_in"]
    rest = (LayerGather([shards[i] for i in later], [axes[i] for i in later], 0), [shards[i] for i in later],
            [names[i] for i in later])
    first = names.index("w_in")
    full = {"w_in": all_gather_layer([shards[first]], [axes[first]], 0, name="all_gather_layer")[0]}

    def layer_params(l):
        p = {n: full[n][l] for n in ("w_branch_a", "w_branch_b", "w_branch_c", "conv_w")}
        p.update({n: w[n][l] for n in REPLICATED})
        p.update(full=full, l=l)
        return p

    xs = x[0]
    saved = []
    for l in range(DEPTH):
        prefetch = (LayerGather(shards, axes, l + 1), shards, names) if l + 1 < DEPTH else None
        replicated = {n: w[n][l] for n in REPLICATED}
        xs, s, full = _layer_fwd(xs, dict(replicated, l=l), full, tables, rest if l == 0 else None, prefetch)
        saved.append(s)
    dx, lpart = loss_head(xs, loss_target[0], name="loss_head")
    loss = lax.psum(lpart[0, 0], MESH_AXES)
    as_index = lambda i: jnp.reshape(i, (1,)).astype(jnp.int32)
    cx, cy, cc = _mesh_pos()
    core, chip, me = as_index(cc), as_index(2 * cx + cy), as_index(4 * cx + 2 * cy + cc)
    layouts = {n: HalfLayout((full[n].shape[1], full[n].shape[2]), SHARDED[n] - 1) for n in MATMUL_WEIGHTS}
    reducer = GradReducer(layouts, core, chip)
    grads = [None] * DEPTH
    for l in reversed(range(DEPTH)):
        dx, grads[l] = _layer_bwd(dx, saved[l], layer_params(l), tables, reducer)

    g = {n: jnp.stack([grads[l][n] for l in range(DEPTH)]) for n in SMALL_WEIGHTS}
    flat = jnp.concatenate([g[n].reshape(-1) for n in SMALL_WEIGHTS])
    small = jnp.pad(flat, (0, SMALL_ROWS * 128 - flat.shape[0])).reshape(SMALL_ROWS, 128)
    small_sum = sum_devices(gather_small(small, name="gather_small"), small, me, name="sum_devices").reshape(-1)
    mine, other = reducer.finish()

    res = {}
    for n in MATMUL_WEIGHTS:
        res[n] = adamw_halves(w[n], m[n], v[n], [mine[(l, n)] for l in range(DEPTH)],
                              [other[(l, n)] for l in range(DEPTH)], layouts[n], core, name="adamw_halves")
    off = 0
    for n in SMALL_WEIGHTS:
        shp = g[n].shape
        size = int(np.prod(shp))
        gn = small_sum[off:off + size].reshape(shp)
        off += size
        if n in SHARDED:
            gn = lax.dynamic_slice_in_dim(gn, (2 * cx + cy) * w[n].shape[-1], w[n].shape[-1], axis=len(shp) - 1)
        shp = w[n].shape
        cols = shp[-1]
        two_d = lambda t: t.reshape(int(np.prod(shp)) // cols, cols)
        res[n] = [t.reshape(shp) for t in adamw(two_d(w[n]), two_d(m[n]), two_d(v[n]), two_d(gn), name="adamw")]

    out = [loss, dx[None]]
    for k in range(4):
        out.extend(res[n][k] for n in WEIGHTS)
    return tuple(out)
```

```python
import jax
import jax.numpy as jnp
import numpy as np
from jax import lax
from jax.experimental import pallas as pl
from jax.experimental.pallas import tpu as pltpu

F32 = jnp.float32
BF16 = jnp.bfloat16

D_MODEL = 1024
DEPTH = 2
CHUNK = 64
N_LEFT = 8
HEAD_DIM = 64
N_HEADS = 8
WIDTH = 512
POOL_WINDOWS = (2, 4, 8, 16)
GROUP_DIM = 128
MAX_REL = 2 * CHUNK
REL_TABLE = MAX_REL + CHUNK
D_FF = 2816
EPS = 1e-6
QK_SCALE = 0.125
GATE_COL0 = 7 * WIDTH

ADAM_LR = 0.001
ADAM_B1 = 0.9
ADAM_B2 = 0.999
ADAM_EPS = 1e-08
ADAM_WD = 0.01
ADAM_STEP = 10

VMEM_LIMIT = 56 * 1024 * 1024
ATT_Q = 256
A_Q = 256
A_WIN = A_Q + N_LEFT * CHUNK
A_FWD_HEADS = 8
A_BWD_HEADS = 4
B_FWD_HEADS = 8
B_BWD_HEADS = 4
HALO = 16
CONV_HALO = 8
NEG = -1e30

MESH_AXES = ("x", "y", "c")
MESH_T = pl.DeviceIdType.MESH


def _cp(sem=None, vmem=VMEM_LIMIT):
    return pltpu.CompilerParams(dimension_semantics=sem, vmem_limit_bytes=vmem)


def _dot(a, b, ca, cb):
    return lax.dot_general(a, b, (((ca,), (cb,)), ((), ())), preferred_element_type=F32)


def _tile(n, cands=(512, 256, 128)):
    for c in cands:
        if n % c == 0:
            return c
    return n


def _split_hi_lo(v):
    hi = v.astype(BF16)
    lo = (v - hi.astype(F32)).astype(BF16)
    return hi, lo


def matmul(a, b, *, ta=False, tb=False, add=None, norm_bwd=None, out_dtype=F32, b_layer=None, behind=None, name):
    plan, sources = behind if behind is not None else (None, [])
    n_src = len(sources)
    n_dst = len(plan.out_shapes()) if plan is not None else 0
    assert add is None or norm_bwd is None
    if ta:
        K, M = a.shape
    else:
        M, K = a.shape
    if tb:
        N, K2 = b.shape[-2:]
    else:
        K2, N = b.shape[-2:]
    assert K == K2, (a.shape, b.shape, ta, tb)
    big = (1024, 1408, 512, 256, 128)
    tm = _tile(M, big)
    tn = _tile(N, (1664,) + big)
    tk = _tile(K, big if ta else (1664, 1408) + big if norm_bwd is not None else (3328, 2816) + big)
    nk = K // tk

    n_extra = 1 if add is not None else 3 if norm_bwd is not None else 0
    n_in = 2 + n_extra + n_src
    n_out = 2 if norm_bwd is not None else 1
    grid = (M // tm, N // tn, nk)
    assert norm_bwd is None or tn == N, "the norm gradient needs whole rows in one output tile"

    def body(*refs):
        a_ref, b_ref = refs[:2]
        extra = refs[2:2 + n_extra]
        o_ref, acc = refs[n_in], refs[n_in + n_out + n_dst]
        i, j, k = pl.program_id(0), pl.program_id(1), pl.program_id(2)
        if plan is not None:
            comm = (refs[n_in - n_src:n_in], refs[n_in + n_out:n_in + n_out + n_dst], refs[n_in + n_out + n_dst + 1:])

            @pl.when((i == 0) & (j == 0) & (k == 0))
            def _():
                plan.start(*comm)

        @pl.when(k == 0)
        def _():
            acc[...] = jnp.zeros_like(acc)

        av = a_ref[...].astype(BF16)
        bv = b_ref[...].astype(BF16)
        acc[...] += _dot(av, bv, 0 if ta else 1, 1 if tb else 0)

        @pl.when(k == nk - 1)
        def _():
            r = acc[...]
            if add is not None:
                r = r + extra[0][...].astype(F32)
            if norm_bwd is not None:
                x_ref, g_ref, dres_ref = extra
                dg_ref = refs[n_in + 1]
                xv = x_ref[...]
                inv = lax.rsqrt(jnp.mean(xv * xv, axis=-1, keepdims=True) + EPS)
                gd = r * g_ref[...]
                mean = jnp.mean(xv * gd, axis=-1, keepdims=True)

                @pl.when(i == 0)
                def _():
                    dg_ref[...] = jnp.zeros_like(dg_ref)

                dg_ref[...] += jnp.sum(r * xv * inv, axis=0, keepdims=True)
                r = dres_ref[...] + inv * gd - xv * (inv * inv * inv * mean)
            o_ref[...] = r.astype(out_dtype)

        if plan is not None:
            @pl.when((i == grid[0] - 1) & (j == grid[1] - 1) & (k == nk - 1))
            def _():
                plan.finish(*comm)

    a_spec = pl.BlockSpec((tk, tm), lambda i, j, k: (k, i)) if ta else pl.BlockSpec((tm, tk), lambda i, j, k: (i, k))
    if b_layer is None:
        b_spec = pl.BlockSpec((tn, tk), lambda i, j, k: (j, k)) if tb else pl.BlockSpec((tk, tn), lambda i, j, k: (k, j))
    elif tb:
        b_spec = pl.BlockSpec((None, tn, tk), lambda i, j, k: (b_layer, j, k))
    else:
        b_spec = pl.BlockSpec((None, tk, tn), lambda i, j, k: (b_layer, k, j))
    o_spec = pl.BlockSpec((tm, tn), lambda i, j, k: (i, j))
    vec_spec = pl.BlockSpec((1, tn), lambda i, j, k: (0, j))
    in_specs = [a_spec, b_spec]
    args = [a, b]
    out_shape = [jax.ShapeDtypeStruct((M, N), out_dtype)]
    out_specs = [o_spec]
    if add is not None:
        in_specs.append(o_spec)
        args.append(add)
    if norm_bwd is not None:
        in_specs += [o_spec, vec_spec, o_spec]
        args += list(norm_bwd)
        out_shape.append(jax.ShapeDtypeStruct((1, N), F32))
        out_specs.append(vec_spec)
    sequential = plan is not None or norm_bwd is not None
    any_space = pl.BlockSpec(memory_space=pl.ANY)
    outs = pl.pallas_call(
        body,
        out_shape=tuple(out_shape) + (tuple(plan.out_shapes()) if plan is not None else ()),
        grid=grid,
        in_specs=in_specs + [any_space] * n_src,
        out_specs=tuple(out_specs) + tuple([any_space] * n_dst),
        scratch_shapes=[pltpu.VMEM((tm, tn), F32)] + (list(plan.scratch_shapes()) if plan is not None else []),
        compiler_params=_cp(("arbitrary" if sequential else "parallel",) * 2 + ("arbitrary",)),
        name=name,
    )(*args, *sources)
    if plan is None:
        return outs[0] if n_out == 1 else tuple(outs)
    return tuple(outs[:n_out]) + (list(outs[n_out:]),) if n_out > 1 else (outs[0], list(outs[1:]))


def rmsnorm_fwd(x, g, *, name):
    S, D = x.shape
    T = _tile(S)

    def body(x_ref, g_ref, h_ref):
        xv = x_ref[...]
        r = lax.rsqrt(jnp.mean(xv * xv, axis=-1, keepdims=True) + EPS)
        h_ref[...] = (xv * r * g_ref[...]).astype(BF16)

    return pl.pallas_call(
        body,
        out_shape=jax.ShapeDtypeStruct((S, D), BF16),
        grid=(S // T,),
        in_specs=[pl.BlockSpec((T, D), lambda i: (i, 0)), pl.BlockSpec((1, D), lambda i: (0, 0))],
        out_specs=pl.BlockSpec((T, D), lambda i: (i, 0)),
        compiler_params=_cp(("parallel",)),
        name=name,
    )(x, g)


def _head_mean_matrix():
    r = lax.broadcasted_iota(jnp.int32, (WIDTH, WIDTH), 0) // HEAD_DIM
    c = lax.broadcasted_iota(jnp.int32, (WIDTH, WIDTH), 1) // HEAD_DIM
    return jnp.where(r == c, 1.0 / HEAD_DIM, 0.0).astype(BF16)


def _head_mean(v, mm):
    hi, lo = _split_hi_lo(v)
    return _dot(hi, mm, 1, 0) + _dot(lo, mm, 1, 0)


def qkv_prep(proj, gq, gk, *, name):
    S = proj.shape[0]
    T = _tile(S)

    def body(qa, ka, va, qb, kb, vb, gq_ref, gk_ref, oqa, oka, ova, oqb, okb, ovb):
        mm = _head_mean_matrix()
        for src, gref, dst, scale in ((qa, gq_ref, oqa, QK_SCALE), (ka, gk_ref, oka, 1.0)):
            v = src[...]
            r = lax.rsqrt(_head_mean(v * v, mm) + EPS)
            dst[...] = (v * r * gref[...] * scale).astype(BF16)
        oqb[...] = (qb[...] * QK_SCALE).astype(BF16)
        for src, dst in ((va, ova), (kb, okb), (vb, ovb)):
            dst[...] = src[...].astype(BF16)

    col = lambda j: pl.BlockSpec((T, WIDTH), lambda i, j=j: (i, j))
    vec = pl.BlockSpec((1, WIDTH), lambda i: (0, 0))
    out = pl.BlockSpec((T, WIDTH), lambda i: (i, 0))
    return pl.pallas_call(
        body,
        out_shape=tuple(jax.ShapeDtypeStruct((S, WIDTH), BF16) for _ in range(6)),
        grid=(S // T,),
        in_specs=[col(0), col(1), col(2), col(3), col(4), col(5), vec, vec],
        out_specs=tuple(out for _ in range(6)),
        compiler_params=_cp(("parallel",)),
        name=name,
    )(proj, proj, proj, proj, proj, proj, gq, gk)


def qknorm_bwd(proj, gq, gk, dqh, dkh, *, name):
    S = proj.shape[0]
    T = _tile(S)

    def body(qa, ka, gq_ref, gk_ref, dq_ref, dk_ref, oq, ok, ogq, ogk):
        i = pl.program_id(0)
        mm = _head_mean_matrix()

        @pl.when(i == 0)
        def _():
            ogq[...] = jnp.zeros_like(ogq)
            ogk[...] = jnp.zeros_like(ogk)

        for src, gref, dref, dst, gdst in ((qa, gq_ref, dq_ref, oq, ogq), (ka, gk_ref, dk_ref, ok, ogk)):
            v = src[...]
            dy = dref[...]
            r = lax.rsqrt(_head_mean(v * v, mm) + EPS)
            gd = dy * gref[...]
            m = _head_mean(v * gd, mm)
            dst[...] = (r * gd - v * (r * r * r * m)).astype(BF16)
            gdst[...] += jnp.sum(dy * v * r, axis=0, keepdims=True)

    col = lambda j: pl.BlockSpec((T, WIDTH), lambda i, j=j: (i, j))
    vec = pl.BlockSpec((1, WIDTH), lambda i: (0, 0))
    row = pl.BlockSpec((T, WIDTH), lambda i: (i, 0))
    return pl.pallas_call(
        body,
        out_shape=(jax.ShapeDtypeStruct((S, WIDTH), BF16), jax.ShapeDtypeStruct((S, WIDTH), BF16),
                   jax.ShapeDtypeStruct((1, WIDTH), F32), jax.ShapeDtypeStruct((1, WIDTH), F32)),
        grid=(S // T,),
        in_specs=[col(0), col(1), vec, vec, row, row],
        out_specs=(row, row, vec, vec),
        compiler_params=_cp(("arbitrary",)),
        name=name,
    )(proj, proj, gq, gk, dqh, dkh)


DIAG_W = 1024
N_VARIANTS = N_LEFT * CHUNK // A_Q + 1


def diagonal_onehot():
    jj = np.arange(DIAG_W)
    diff = np.where(jj < A_WIN, jj, jj - DIAG_W)
    out = np.zeros((N_VARIANTS, DIAG_W, REL_TABLE), np.float32)
    for v in range(N_VARIANTS):
        rel = np.clip(A_Q * v - diff, -(CHUNK - 1), MAX_REL) + (CHUNK - 1)
        out[v, jj, rel] = 1.0
    return out.reshape(N_VARIANTS * DIAG_W, REL_TABLE)


def exact_dot(a, b, *, name):
    def body(a_ref, b_ref, o_ref):
        o_ref[...] = jnp.dot(a_ref[...], b_ref[...], precision=lax.Precision.HIGHEST, preferred_element_type=F32)

    return pl.pallas_call(body, out_shape=jax.ShapeDtypeStruct((a.shape[0], b.shape[1]), F32),
                          compiler_params=_cp(), name=name)(a, b)


def _band_valid(v):
    qc = (lax.broadcasted_iota(jnp.int32, (A_Q, A_WIN), 0) + A_Q * v) // CHUNK
    kc = lax.broadcasted_iota(jnp.int32, (A_Q, A_WIN), 1) // CHUNK
    return (kc <= qc) & (kc >= qc - N_LEFT)


def bias_expand(diag, *, name):
    def body(d_ref, o_ref):
        rows = jnp.broadcast_to(d_ref[0, 0], (A_Q, DIAG_W))
        skew = pltpu.roll(rows, 0, 1, stride=1, stride_axis=0)
        o_ref[0, 0] = jnp.where(_band_valid(pl.program_id(0)), skew[:, :A_WIN], NEG)

    return pl.pallas_call(
        body,
        out_shape=jax.ShapeDtypeStruct((N_VARIANTS, N_HEADS, A_Q, A_WIN), F32),
        grid=(N_VARIANTS, N_HEADS),
        in_specs=[pl.BlockSpec((1, 1, 1, DIAG_W), lambda v, h: (v, h, 0, 0))],
        out_specs=pl.BlockSpec((1, 1, A_Q, A_WIN), lambda v, h: (v, h, 0, 0)),
        compiler_params=_cp(("parallel", "parallel")),
        name=name,
    )(diag)


def relbias_reduce(dbias, *, name):
    def body(db_ref, o_ref):
        acc = None
        for a in range(A_Q // 8):
            x = jnp.concatenate([db_ref[0, 0, 8 * a:8 * a + 8, :], jnp.zeros((8, DIAG_W - A_WIN), F32)], axis=1)
            x = pltpu.roll(x, DIAG_W - 8 * a, 1) if a else x
            acc = x if acc is None else acc + x
        row = lax.broadcasted_iota(jnp.int32, (8, DIAG_W), 0)
        for b in range(3):
            acc = jnp.where((row >> b) & 1 == 1, pltpu.roll(acc, DIAG_W - (1 << b), 1), acc)
        o_ref[0, 0] = jnp.sum(acc, axis=0, keepdims=True)

    return pl.pallas_call(
        body,
        out_shape=jax.ShapeDtypeStruct((N_VARIANTS, N_HEADS, 1, DIAG_W), F32),
        grid=(N_VARIANTS, N_HEADS),
        in_specs=[pl.BlockSpec((1, 1, A_Q, A_WIN), lambda v, h: (v, h, 0, 0))],
        out_specs=pl.BlockSpec((1, 1, 1, DIAG_W), lambda v, h: (v, h, 0, 0)),
        compiler_params=_cp(("parallel", "parallel")),
        name=name,
    )(dbias)


def _a_window_start(qb):
    return pl.multiple_of(jnp.maximum(qb * A_Q - N_LEFT * CHUNK, 0), A_Q)


def attn_a_fwd(q, k, v, biasm, *, name):
    S = q.shape[0]
    nq = S // A_Q

    def body(q_ref, k_ref, v_ref, b_ref, o_ref):
        qb = pl.program_id(1)
        start = _a_window_start(qb)
        outs = []
        for h in range(A_FWD_HEADS):
            lanes = slice(h * HEAD_DIM, (h + 1) * HEAD_DIM)
            qh = q_ref[:, lanes]
            kw = k_ref[pl.ds(start, A_WIN), lanes]
            vw = v_ref[pl.ds(start, A_WIN), lanes]
            s = _dot(qh, kw, 1, 1) + b_ref[0, h]
            m = jnp.max(s, axis=-1, keepdims=True)
            e = jnp.exp(s - m)
            outs.append(_dot(e.astype(BF16), vw, 1, 0) * (1.0 / jnp.sum(e, axis=-1, keepdims=True)))
        o_ref[...] = jnp.concatenate(outs, axis=1).astype(BF16)

    qspec = pl.BlockSpec((A_Q, A_FWD_HEADS * HEAD_DIM), lambda hp, qb: (qb, hp))
    kvspec = pl.BlockSpec((S, A_FWD_HEADS * HEAD_DIM), lambda hp, qb: (0, hp))
    bspec = pl.BlockSpec((1, A_FWD_HEADS, A_Q, A_WIN), lambda hp, qb: (jnp.minimum(qb, N_VARIANTS - 1), hp, 0, 0))
    return pl.pallas_call(
        body,
        out_shape=jax.ShapeDtypeStruct((S, WIDTH), BF16),
        grid=(N_HEADS // A_FWD_HEADS, nq),
        in_specs=[qspec, kvspec, kvspec, bspec],
        out_specs=qspec,
        compiler_params=_cp(("parallel", "arbitrary")),
        name=name,
    )(q, k, v, biasm)


def attn_a_bwd(q, k, v, biasm, do, *, name):
    S = q.shape[0]
    nq = S // A_Q

    def body(q_ref, k_ref, v_ref, b_ref, do_ref, dq_ref, dk_ref, dv_ref, db_ref):
        qb = pl.program_id(1)
        start = _a_window_start(qb)

        @pl.when(qb == 0)
        def _():
            dk_ref[...] = jnp.zeros_like(dk_ref)
            dv_ref[...] = jnp.zeros_like(dv_ref)

        @pl.when(qb < N_VARIANTS)
        def _():
            db_ref[...] = jnp.zeros_like(db_ref)

        dqs = []
        for h in range(A_BWD_HEADS):
            lanes = slice(h * HEAD_DIM, (h + 1) * HEAD_DIM)
            qh = q_ref[:, lanes]
            doh = do_ref[:, lanes]
            kw = k_ref[pl.ds(start, A_WIN), lanes]
            vw = v_ref[pl.ds(start, A_WIN), lanes]
            s = _dot(qh, kw, 1, 1) + b_ref[0, h]
            m = jnp.max(s, axis=-1, keepdims=True)
            e = jnp.exp(s - m)
            p = e * (1.0 / jnp.sum(e, axis=-1, keepdims=True))
            dp = _dot(doh, vw, 1, 1)
            delta = jnp.sum(p * dp, axis=-1, keepdims=True)
            ds = p * (dp - delta)
            db_ref[0, h] += ds
            dsb = ds.astype(BF16)
            dqs.append(_dot(dsb, kw, 1, 0) * QK_SCALE)
            dk_ref[pl.ds(start, A_WIN), lanes] += _dot(dsb, qh, 0, 0)
            dv_ref[pl.ds(start, A_WIN), lanes] += _dot(p.astype(BF16), doh, 0, 0)
        dq_ref[...] = jnp.concatenate(dqs, axis=1)

    once = pl.Buffered(1)
    qspec = pl.BlockSpec((A_Q, A_BWD_HEADS * HEAD_DIM), lambda hp, qb: (qb, hp))
    kvspec = pl.BlockSpec((S, A_BWD_HEADS * HEAD_DIM), lambda hp, qb: (0, hp), pipeline_mode=once)
    bspec = pl.BlockSpec((1, A_BWD_HEADS, A_Q, A_WIN), lambda hp, qb: (jnp.minimum(qb, N_VARIANTS - 1), hp, 0, 0))
    return pl.pallas_call(
        body,
        out_shape=(jax.ShapeDtypeStruct((S, WIDTH), F32), jax.ShapeDtypeStruct((S, WIDTH), F32),
                   jax.ShapeDtypeStruct((S, WIDTH), F32), jax.ShapeDtypeStruct((N_VARIANTS, N_HEADS, A_Q, A_WIN), F32)),
        grid=(N_HEADS // A_BWD_HEADS, nq),
        in_specs=[qspec, kvspec, kvspec, bspec, qspec],
        out_specs=(qspec, kvspec, kvspec, bspec),
        compiler_params=_cp(("parallel", "arbitrary")),
        name=name,
    )(q, k, v, biasm, do)


def _tri(kind):
    j = lax.broadcasted_iota(jnp.int32, (ATT_Q, ATT_Q), 0)
    s = lax.broadcasted_iota(jnp.int32, (ATT_Q, ATT_Q), 1)
    if kind == "gt":
        m = j > s
    elif kind == "le":
        m = j <= s
    else:
        m = j < s
    return jnp.where(m, 1.0, 0.0).astype(BF16)


def _cum(v, tri):
    hi, lo = _split_hi_lo(v)
    return _dot(hi, tri, 1, 0) + _dot(lo, tri, 1, 0)


def _log_sigmoids(z, mask):
    t = jnp.log(1.0 + jnp.exp(-jnp.abs(z)))
    keep = -(jnp.maximum(z, 0.0) + t)
    take = jnp.minimum(z, 0.0) - t
    return (keep if mask is None else jnp.where(mask, keep, 0.0)), take


def _strictly_before():
    row = lax.broadcasted_iota(jnp.int32, (ATT_Q, ATT_Q), 0)
    col = lax.broadcasted_iota(jnp.int32, (ATT_Q, ATT_Q), 1)
    return col < row


EXIT_LOG = -104.0


def attn_b_fwd(q, k, v, *, gather=None, name):
    S = q.shape[0]
    nq = S // ATT_Q
    heads = B_FWD_HEADS
    plan, shards, fulls = gather if gather is not None else (None, [], [])
    ng = len(shards)

    def body(q_ref, k_ref, v_ref, *rest):
        hp = pl.program_id(0)
        qb = pl.program_id(1)
        o_ref, t_ref, n_ref = rest[2 * ng:2 * ng + 3]
        if plan is not None:
            comm = (rest[:ng], rest[2 * ng + 3:3 * ng + 3], rest[3 * ng + 3:])

            @pl.when(jnp.logical_and(hp == 0, qb == 0))
            def _():
                plan.start(*comm)

        tri = _tri("gt")

        def block(kb, carry, mask):
            ks = pl.multiple_of(kb * ATT_Q, ATT_Q)
            new = []
            for h in range(heads):
                lanes = slice(h * HEAD_DIM, (h + 1) * HEAD_DIM)
                c, acc = carry[h]
                z = _dot(q_ref[:, lanes], k_ref[pl.ds(ks, ATT_Q), lanes], 1, 1)
                keep, take = _log_sigmoids(z, mask)
                w = jnp.exp(take + (_cum(keep, tri) + c))
                if mask is not None:
                    w = jnp.where(mask, w, 0.0)
                acc = acc + _dot(w.astype(BF16), v_ref[pl.ds(ks, ATT_Q), lanes], 1, 0)
                c = c + jnp.sum(keep, axis=-1, keepdims=True)
                new.append((c, acc))
            return tuple(new)

        def cond(state):
            it, cmax, _ = state
            return jnp.logical_and(it <= qb, cmax >= EXIT_LOG)

        def step(state):
            it, _, carry = state
            carry = block(qb - it, carry, None)
            worst = carry[0][0]
            for h in range(1, heads):
                worst = jnp.maximum(worst, carry[h][0])
            return it + 1, jnp.max(worst), carry

        init = tuple((jnp.zeros((ATT_Q, 1), F32), jnp.zeros((ATT_Q, HEAD_DIM), F32)) for _ in range(heads))
        diag = block(qb, init, _strictly_before())
        visited, _, res = lax.while_loop(cond, step, (jnp.int32(1), jnp.float32(0.0), diag))
        o_ref[...] = jnp.concatenate([res[h][1] for h in range(heads)], axis=1).astype(BF16)
        t_ref[...] = jnp.concatenate([jnp.broadcast_to(res[h][0], (ATT_Q, HEAD_DIM)) for h in range(heads)], axis=1)
        n_ref[hp, qb] = visited.astype(F32)
        if plan is not None:
            @pl.when(jnp.logical_and(hp == N_HEADS // heads - 1, qb == nq - 1))
            def _():
                plan.finish(*comm)

    qspec = pl.BlockSpec((ATT_Q, heads * HEAD_DIM), lambda hp, qb: (qb, hp))
    kvspec = pl.BlockSpec((S, heads * HEAD_DIM), lambda hp, qb: (0, hp), pipeline_mode=pl.Buffered(1))
    outs = pl.pallas_call(
        body,
        out_shape=(jax.ShapeDtypeStruct((S, WIDTH), BF16), jax.ShapeDtypeStruct((S, WIDTH), F32),
                   jax.ShapeDtypeStruct((N_HEADS // heads, nq), F32))
        + tuple(jax.ShapeDtypeStruct(f.shape, f.dtype) for f in fulls),
        grid=(N_HEADS // heads, nq),
        in_specs=[qspec, kvspec, kvspec] + [ANY] * (2 * ng),
        out_specs=(qspec, qspec, pl.BlockSpec(memory_space=pltpu.SMEM)) + tuple([ANY] * ng),
        scratch_shapes=plan.scratch_shapes() if plan is not None else (),
        input_output_aliases={3 + ng + i: 3 + i for i in range(ng)},
        compiler_params=_cp(("arbitrary", "arbitrary")),
        name=name,
    )(q, k, v, *shards, *fulls)
    return outs if plan is None else (outs[0], outs[1], outs[2], list(outs[3:]))


def attn_b_bwd(q, k, v, tot, nblk, do, *, scatter=None, name):
    S = q.shape[0]
    nq = S // ATT_Q
    heads = B_BWD_HEADS
    plan, pairs16 = scatter if scatter is not None else (None, [])
    ns = len(pairs16)

    def body(q_ref, k_ref, v_ref, t_ref, n_ref, do_ref, *rest):
        hp = pl.program_id(0)
        qb = pl.program_id(1)
        dq_ref, dk_ref, dv_ref = rest[ns:ns + 3]
        if plan is not None:
            comm = (rest[:ns], rest[ns + 3:2 * ns + 3], rest[2 * ns + 3:])

            @pl.when(jnp.logical_and(hp == 0, qb == 0))
            def _():
                plan.start(*comm)

        visited = n_ref[(hp * heads) // B_FWD_HEADS, qb].astype(jnp.int32)
        first = jnp.clip(qb + 1 - visited, 0, qb + 1)
        tri_le = _tri("le")
        tri_lt = _tri("lt")

        @pl.when(qb == 0)
        def _():
            dk_ref[...] = jnp.zeros_like(dk_ref)
            dv_ref[...] = jnp.zeros_like(dv_ref)

        def block(kb, carry, mask):
            ks = pl.multiple_of(kb * ATT_Q, ATT_Q)
            new = []
            for h in range(heads):
                lanes = slice(h * HEAD_DIM, (h + 1) * HEAD_DIM)
                cl, cg, dq = carry[h]
                qh = q_ref[:, lanes]
                doh = do_ref[:, lanes]
                kh = k_ref[pl.ds(ks, ATT_Q), lanes]
                vh = v_ref[pl.ds(ks, ATT_Q), lanes]
                totl = t_ref[:, h * HEAD_DIM:h * HEAD_DIM + 1]
                z = _dot(qh, kh, 1, 1)
                keep, take = _log_sigmoids(z, mask)
                sig = jnp.exp(take)
                w = sig * jnp.exp((totl - cl) - _cum(keep, tri_le))
                if mask is not None:
                    w = jnp.where(mask, w, 0.0)
                g = w * _dot(doh, vh, 1, 1)
                G = _dot(g.astype(BF16), tri_lt, 1, 0) + cg
                dz = g * (1.0 - sig) - sig * G
                if mask is not None:
                    dz = jnp.where(mask, dz, 0.0)
                dz = dz.astype(BF16)
                dq = dq + _dot(dz, kh, 1, 0)
                dk_ref[pl.ds(ks, ATT_Q), lanes] += _dot(dz, qh, 0, 0)
                dv_ref[pl.ds(ks, ATT_Q), lanes] += _dot(w.astype(BF16), doh, 0, 0)
                cl = cl + jnp.sum(keep, axis=-1, keepdims=True)
                cg = cg + jnp.sum(g, axis=-1, keepdims=True)
                new.append((cl, cg, dq))
            return tuple(new)

        init = tuple((jnp.zeros((ATT_Q, 1), F32), jnp.zeros((ATT_Q, 1), F32), jnp.zeros((ATT_Q, HEAD_DIM), F32))
                     for _ in range(heads))
        res = lax.fori_loop(jnp.minimum(first, qb), qb, lambda kb, carry: block(kb, carry, None), init)
        res = block(qb, res, _strictly_before())
        dq_ref[...] = (jnp.concatenate([res[h][2] for h in range(heads)], axis=1) * QK_SCALE).astype(BF16)
        if plan is not None:
            @pl.when(jnp.logical_and(hp == N_HEADS // heads - 1, qb == nq - 1))
            def _():
                plan.finish(*comm)

    qspec = pl.BlockSpec((ATT_Q, heads * HEAD_DIM), lambda hp, qb: (qb, hp))
    kvspec = pl.BlockSpec((S, heads * HEAD_DIM), lambda hp, qb: (0, hp), pipeline_mode=pl.Buffered(1))
    outs = pl.pallas_call(
        body,
        out_shape=(jax.ShapeDtypeStruct((S, WIDTH), BF16), jax.ShapeDtypeStruct((S, WIDTH), F32),
                   jax.ShapeDtypeStruct((S, WIDTH), F32)) + (tuple(plan.out_shapes()) if plan is not None else ()),
        grid=(N_HEADS // heads, nq),
        in_specs=[qspec, kvspec, kvspec, qspec, pl.BlockSpec(memory_space=pltpu.SMEM), qspec] + [ANY] * ns,
        out_specs=(qspec, kvspec, kvspec) + tuple([ANY] * ns),
        scratch_shapes=plan.scratch_shapes() if plan is not None else (),
        compiler_params=_cp(("arbitrary", "arbitrary")),
        name=name,
    )(q, k, v, tot, nblk, do, *pairs16)
    return outs if plan is None else (outs[0], outs[1], outs[2], list(outs[3:]))


U_COLBLK = 6


def _pool_counts(t0, rows):
    t = t0 + lax.broadcasted_iota(jnp.int32, (rows, WIDTH), 0)
    lane_grp = lax.broadcasted_iota(jnp.int32, (rows, WIDTH), 1) // GROUP_DIM
    w2, w4, w8, w16 = POOL_WINDOWS
    win = jnp.where(lane_grp == 0, w2, jnp.where(lane_grp == 1, w4, jnp.where(lane_grp == 2, w8, w16)))
    cnt = jnp.minimum(t + 1, win)
    return 1.0 / cnt.astype(F32), lane_grp


def _window_sums(ext, shift_fn):
    s2 = ext + shift_fn(ext, 1)
    s4 = s2 + shift_fn(s2, 2)
    s8 = s4 + shift_fn(s4, 4)
    s16 = s8 + shift_fn(s8, 8)
    return s2, s4, s8, s16


def _select_group(lane_grp, s2, s4, s8, s16):
    return jnp.where(lane_grp == 0, s2, jnp.where(lane_grp == 1, s4, jnp.where(lane_grp == 2, s8, s16)))


def _pooled_tile(u_ref, h_ref, i, T):
    halo = jnp.where(i > 0, h_ref[...], 0.0)
    ext = jnp.concatenate([halo, u_ref[...]], axis=0)
    n = T + HALO
    sums = _window_sums(ext, lambda v, k: pltpu.roll(v, k, 0))
    inv, lane_grp = _pool_counts(i * T - HALO, n)
    pooled = _select_group(lane_grp, *sums) * inv - ext
    return pooled[HALO:, :]


def pool_fwd(proj, w_pool, scale, *, name):
    S = proj.shape[0]
    T = _tile(S)
    hb = T // HALO

    def body(u_ref, h_ref, w_ref, s_ref, o_ref):
        i = pl.program_id(0)
        pooled = _pooled_tile(u_ref, h_ref, i, T).astype(BF16)
        outs = [_dot(pooled[:, g * GROUP_DIM:(g + 1) * GROUP_DIM], w_ref[g], 1, 0) for g in range(4)]
        o_ref[...] = (jnp.concatenate(outs, axis=1) * s_ref[...]).astype(BF16)

    return pl.pallas_call(
        body,
        out_shape=jax.ShapeDtypeStruct((S, WIDTH), BF16),
        grid=(S // T,),
        in_specs=[pl.BlockSpec((T, WIDTH), lambda i: (i, U_COLBLK)),
                  pl.BlockSpec((HALO, WIDTH), lambda i: (jnp.maximum(i * hb - 1, 0), U_COLBLK)),
                  pl.BlockSpec((4, GROUP_DIM, GROUP_DIM), lambda i: (0, 0, 0)),
                  pl.BlockSpec((1, WIDTH), lambda i: (0, 0))],
        out_specs=pl.BlockSpec((T, WIDTH), lambda i: (i, 0)),
        compiler_params=_cp(("parallel",)),
        name=name,
    )(proj, proj, w_pool, scale)


def pool_bwd(proj, w_pool, scale, do, *, name):
    S = proj.shape[0]
    T = _tile(S)
    hb = T // HALO
    nt = S // T

    def body(u_ref, h_ref, w_ref, s_ref, do_ref, dof_ref, du_ref, dw_ref, ds_ref):
        i = pl.program_id(0)

        @pl.when(i == 0)
        def _():
            dw_ref[...] = jnp.zeros_like(dw_ref)
            ds_ref[...] = jnp.zeros_like(ds_ref)

        pooled = _pooled_tile(u_ref, h_ref, i, T).astype(BF16)
        dov = do_ref[...].astype(F32)
        fut = jnp.where(i < nt - 1, dof_ref[...].astype(F32), 0.0)
        dmix = (jnp.concatenate([dov, fut], axis=0) * s_ref[...]).astype(BF16)
        mixed, dpool = [], []
        for g in range(4):
            lanes = slice(g * GROUP_DIM, (g + 1) * GROUP_DIM)
            mixed.append(_dot(pooled[:, lanes], w_ref[g], 1, 0))
            dw_ref[g] += _dot(pooled[:, lanes], dmix[:T, lanes], 0, 0)
            dpool.append(_dot(dmix[:, lanes], w_ref[g], 1, 1))
        ds_ref[...] += jnp.sum(dov * jnp.concatenate(mixed, axis=1), axis=0, keepdims=True)
        dp = jnp.concatenate(dpool, axis=1)
        n = T + HALO
        inv, lane_grp = _pool_counts(i * T, n)
        sums = _window_sums(dp * inv, lambda v, k: pltpu.roll(v, n - k, 0))
        du = _select_group(lane_grp, *sums) - dp
        du_ref[...] = du[:T, :].astype(BF16)

    row = pl.BlockSpec((T, WIDTH), lambda i: (i, 0))
    return pl.pallas_call(
        body,
        out_shape=(jax.ShapeDtypeStruct((S, WIDTH), BF16), jax.ShapeDtypeStruct((4, GROUP_DIM, GROUP_DIM), F32),
                   jax.ShapeDtypeStruct((1, WIDTH), F32)),
        grid=(nt,),
        in_specs=[pl.BlockSpec((T, WIDTH), lambda i: (i, U_COLBLK)),
                  pl.BlockSpec((HALO, WIDTH), lambda i: (jnp.maximum(i * hb - 1, 0), U_COLBLK)),
                  pl.BlockSpec((4, GROUP_DIM, GROUP_DIM), lambda i: (0, 0, 0)),
                  pl.BlockSpec((1, WIDTH), lambda i: (0, 0)),
                  row,
                  pl.BlockSpec((HALO, WIDTH), lambda i: (jnp.minimum((i + 1) * hb, S // HALO - 1), 0))],
        out_specs=(row, pl.BlockSpec((4, GROUP_DIM, GROUP_DIM), lambda i: (0, 0, 0)),
                   pl.BlockSpec((1, WIDTH), lambda i: (0, 0))),
        compiler_params=_cp(("arbitrary",)),
        name=name,
    )(proj, proj, w_pool, scale, do, do)


GATE_BLK0 = GATE_COL0 // WIDTH


def merge_fwd(oa, ob, oc, proj, b_gate, wa, wb, wc, *, name):
    S = oa.shape[0]
    T = _tile(S)

    def body(oa_ref, ob_ref, oc_ref, ga, gb, gc, ba, bb, bc, wa_ref, wb_ref, wc_ref, m_ref):
        acc = None
        for o_ref, g_ref, b_ref, w_ref in ((oa_ref, ga, ba, wa_ref), (ob_ref, gb, bb, wb_ref), (oc_ref, gc, bc, wc_ref)):
            y = _dot(o_ref[...], w_ref[...], 1, 0)
            t = jax.nn.sigmoid(g_ref[...] + b_ref[...]) * y
            acc = t if acc is None else acc + t
        m_ref[...] = acc.astype(BF16)

    row = pl.BlockSpec((T, WIDTH), lambda i, n: (i, 0))
    gate = lambda b: pl.BlockSpec((T, WIDTH), lambda i, n, b=b: (i, GATE_BLK0 + 2 * b + n))
    bias = lambda b: pl.BlockSpec((1, WIDTH), lambda i, n, b=b: (0, 2 * b + n))
    wspec = pl.BlockSpec((WIDTH, WIDTH), lambda i, n: (0, n))
    return pl.pallas_call(
        body,
        out_shape=jax.ShapeDtypeStruct((S, D_MODEL), BF16),
        grid=(S // T, 2),
        in_specs=[row, row, row, gate(0), gate(1), gate(2), bias(0), bias(1), bias(2), wspec, wspec, wspec],
        out_specs=pl.BlockSpec((T, WIDTH), lambda i, n: (i, n)),
        compiler_params=_cp(("parallel", "parallel")),
        name=name,
    )(oa, ob, oc, proj, proj, proj, b_gate, b_gate, b_gate, wa, wb, wc)


def merge_bwd(dm, oa, ob, oc, proj, b_gate, wa, wb, wc, *, name):
    S = oa.shape[0]
    T = _tile(S)

    def body(dm_ref, oa_ref, ob_ref, oc_ref, ga, gb, gc, ba, bb, bc, wa_ref, wb_ref, wc_ref,
             ta, tb, tc, dga, dgb, dgc, dba, dbb, dbc):
        i = pl.program_id(1)
        dmv = dm_ref[...].astype(F32)
        for o_ref, g_ref, b_ref, w_ref, t_ref, dg_ref, db_ref in (
                (oa_ref, ga, ba, wa_ref, ta, dga, dba), (ob_ref, gb, bb, wb_ref, tb, dgb, dbb),
                (oc_ref, gc, bc, wc_ref, tc, dgc, dbc)):
            y = _dot(o_ref[...], w_ref[...], 1, 0)
            gate = jax.nn.sigmoid(g_ref[...] + b_ref[...])
            t_ref[...] = (gate * dmv).astype(BF16)
            dgl = dmv * y * gate * (1.0 - gate)
            dg_ref[...] = dgl.astype(BF16)

            @pl.when(i == 0)
            def _():
                db_ref[...] = jnp.zeros_like(db_ref)

            db_ref[...] += jnp.sum(dgl, axis=0, keepdims=True)

    row = pl.BlockSpec((T, WIDTH), lambda n, i: (i, 0))
    half = pl.BlockSpec((T, WIDTH), lambda n, i: (i, n))
    gate = lambda b: pl.BlockSpec((T, WIDTH), lambda n, i, b=b: (i, GATE_BLK0 + 2 * b + n))
    bias = lambda b: pl.BlockSpec((1, WIDTH), lambda n, i, b=b: (0, 2 * b + n))
    wspec = pl.BlockSpec((WIDTH, WIDTH), lambda n, i: (0, n))
    bvec = pl.BlockSpec((1, WIDTH), lambda n, i: (0, n))
    act = jax.ShapeDtypeStruct((S, D_MODEL), BF16)
    vec = jax.ShapeDtypeStruct((1, D_MODEL), F32)
    return pl.pallas_call(
        body,
        out_shape=(act, act, act, act, act, act, vec, vec, vec),
        grid=(2, S // T),
        in_specs=[half, row, row, row, gate(0), gate(1), gate(2), bias(0), bias(1), bias(2), wspec, wspec, wspec],
        out_specs=(half, half, half, half, half, half, bvec, bvec, bvec),
        compiler_params=_cp(("parallel", "arbitrary")),
        name=name,
    )(dm, oa, ob, oc, proj, proj, proj, b_gate, b_gate, b_gate, wa, wb, wc)


FF_T = 256
FF_BLKS = D_FF // FF_T


def _silu_parts(x):
    s = jax.nn.sigmoid(x)
    return x * s, s


def _conv3(ext, w_ref, b_ref):
    taps = (pltpu.roll(ext, 2, 0), pltpu.roll(ext, 1, 0), ext)
    return b_ref[...] + w_ref[0:1, :] * taps[0] + w_ref[1:2, :] * taps[1] + w_ref[2:3, :] * taps[2], taps


def conv_glu_fwd(u, conv_w, conv_b, *, name):
    S = u.shape[0]
    T = _tile(S)
    hb = T // CONV_HALO

    def body(ug, ugh, uv, uvh, wg, wv, bg, bv, a_ref):
        i = pl.program_id(1)
        cs = []
        for m_ref, h_ref, w_ref, b_ref in ((ug, ugh, wg, bg), (uv, uvh, wv, bv)):
            halo = jnp.where(i > 0, h_ref[...], 0.0)
            ext = jnp.concatenate([halo, m_ref[...]], axis=0)
            cs.append(_conv3(ext, w_ref, b_ref)[0][CONV_HALO:, :])
        act, _ = _silu_parts(cs[0])
        a_ref[...] = (act * cs[1]).astype(BF16)

    main = lambda o: pl.BlockSpec((T, FF_T), lambda c, i, o=o: (i, c + o))
    halo = lambda o: pl.BlockSpec((CONV_HALO, FF_T), lambda c, i, o=o: (jnp.maximum(i * hb - 1, 0), c + o))
    wsp = lambda o: pl.BlockSpec((3, FF_T), lambda c, i, o=o: (0, c + o))
    bsp = lambda o: pl.BlockSpec((1, FF_T), lambda c, i, o=o: (0, c + o))
    return pl.pallas_call(
        body,
        out_shape=jax.ShapeDtypeStruct((S, D_FF), BF16),
        grid=(FF_BLKS, S // T),
        in_specs=[main(0), halo(0), main(FF_BLKS), halo(FF_BLKS), wsp(0), wsp(FF_BLKS), bsp(0), bsp(FF_BLKS)],
        out_specs=pl.BlockSpec((T, FF_T), lambda c, i: (i, c)),
        compiler_params=_cp(("parallel", "parallel")),
        name=name,
    )(u, u, u, u, conv_w, conv_w, conv_b, conv_b)


def conv_glu_bwd(u, conv_w, conv_b, da, *, name):
    S = u.shape[0]
    T = _tile(S)
    hb = T // CONV_HALO
    nt = S // T
    n = T + 2 * CONV_HALO

    def body(ug, ugp, ugf, uv, uvp, uvf, wg, wv, bg, bv, da_ref, daf_ref,
             dug, duv, dwg, dwv, dbg, dbv):
        i = pl.program_id(1)
        first, last = i == 0, i == nt - 1
        taps, cs = [], []
        for m_ref, p_ref, f_ref, w_ref, b_ref in ((ug, ugp, ugf, wg, bg), (uv, uvp, uvf, wv, bv)):
            ext = jnp.concatenate([jnp.where(first, 0.0, p_ref[...]), m_ref[...], jnp.where(last, 0.0, f_ref[...])], axis=0)
            c, tp = _conv3(ext, w_ref, b_ref)
            cs.append(c)
            taps.append(tp)
        dae = jnp.concatenate([jnp.zeros((CONV_HALO, FF_T), F32), da_ref[...].astype(F32),
                               jnp.where(last, 0.0, daf_ref[...].astype(F32))], axis=0)
        act, sg = _silu_parts(cs[0])
        dcs = (dae * cs[1] * (sg * (1.0 + cs[0] * (1.0 - sg))), dae * act)
        main = slice(CONV_HALO, CONV_HALO + T)
        for tp, dc, w_ref, du_ref, dw_ref, db_ref in ((taps[0], dcs[0], wg, dug, dwg, dbg),
                                                      (taps[1], dcs[1], wv, duv, dwv, dbv)):
            du = (w_ref[2:3, :] * dc + w_ref[1:2, :] * pltpu.roll(dc, n - 1, 0) + w_ref[0:1, :] * pltpu.roll(dc, n - 2, 0))
            du_ref[...] = du[main, :].astype(BF16)
            dcm = dc[main, :]
            rows = [jnp.sum(dcm * tp[j][main, :], axis=0, keepdims=True) for j in range(3)]

            @pl.when(first)
            def _():
                dw_ref[...] = jnp.zeros_like(dw_ref)
                db_ref[...] = jnp.zeros_like(db_ref)

            dw_ref[...] += jnp.concatenate(rows, axis=0)
            db_ref[...] += jnp.sum(dcm, axis=0, keepdims=True)

    main = lambda o: pl.BlockSpec((T, FF_T), lambda c, i, o=o: (i, c + o))
    past = lambda o: pl.BlockSpec((CONV_HALO, FF_T), lambda c, i, o=o: (jnp.maximum(i * hb - 1, 0), c + o))
    fut = lambda o: pl.BlockSpec((CONV_HALO, FF_T), lambda c, i, o=o: (jnp.minimum((i + 1) * hb, S // CONV_HALO - 1), c + o))
    wsp = lambda o: pl.BlockSpec((3, FF_T), lambda c, i, o=o: (0, c + o))
    bsp = lambda o: pl.BlockSpec((1, FF_T), lambda c, i, o=o: (0, c + o))
    return pl.pallas_call(
        body,
        out_shape=(jax.ShapeDtypeStruct((S, D_FF), BF16), jax.ShapeDtypeStruct((S, D_FF), BF16),
                   jax.ShapeDtypeStruct((3, D_FF), F32), jax.ShapeDtypeStruct((3, D_FF), F32),
                   jax.ShapeDtypeStruct((1, D_FF), F32), jax.ShapeDtypeStruct((1, D_FF), F32)),
        grid=(FF_BLKS, nt),
        in_specs=[main(0), past(0), fut(0), main(FF_BLKS), past(FF_BLKS), fut(FF_BLKS),
                  wsp(0), wsp(FF_BLKS), bsp(0), bsp(FF_BLKS), main(0), fut(0)],
        out_specs=(main(0), main(0), wsp(0), wsp(0), bsp(0), bsp(0)),
        compiler_params=_cp(("parallel", "arbitrary")),
        name=name,
    )(u, u, u, u, u, u, conv_w, conv_w, conv_b, conv_b, da, da)


def loss_head(y, target, *, name):
    S, D = y.shape
    T = _tile(S)

    def body(y_ref, t_ref, dy_ref, l_ref):
        i = pl.program_id(0)
        err = y_ref[...] - t_ref[...]
        dy_ref[...] = err * (1.0 / D)

        @pl.when(i == 0)
        def _():
            l_ref[...] = jnp.zeros_like(l_ref)

        l_ref[...] += 0.5 * jnp.sum(jnp.mean(err * err, axis=-1, keepdims=True))

    row = pl.BlockSpec((T, D), lambda i: (i, 0))
    return pl.pallas_call(
        body,
        out_shape=(jax.ShapeDtypeStruct((S, D), F32), jax.ShapeDtypeStruct((8, 128), F32)),
        grid=(S // T,),
        in_specs=[row, row],
        out_specs=(row, pl.BlockSpec((8, 128), lambda i: (0, 0))),
        compiler_params=_cp(("arbitrary",)),
        name=name,
    )(y, target)


ELEMS_PER_BLOCK = 256 * 1024


def _rows_tile(rows, cols):
    if rows * cols <= ELEMS_PER_BLOCK or rows % 8:
        return rows
    best = 8
    for tr in range(8, rows + 1, 8):
        if rows % tr == 0 and tr * cols <= ELEMS_PER_BLOCK:
            best = tr
    return best


def _adamw_math(g, w_ref, m_ref, v_ref, g_out, d_out, m_out, v_out):
    mn = ADAM_B1 * m_ref[...] + (1.0 - ADAM_B1) * g
    vn = ADAM_B2 * v_ref[...] + (1.0 - ADAM_B2) * (g * g)
    m_hat = mn / (1.0 - ADAM_B1 ** ADAM_STEP)
    v_hat = vn / (1.0 - ADAM_B2 ** ADAM_STEP)
    g_out[...] = g
    d_out[...] = -ADAM_LR * (m_hat / (jnp.sqrt(v_hat) + ADAM_EPS) + ADAM_WD * w_ref[...])
    m_out[...] = mn
    v_out[...] = vn


def adamw(w, m, v, g, *, name):
    rows, cols = w.shape
    tr = _rows_tile(rows, cols)

    def body(w_ref, m_ref, v_ref, g_ref, g_out, d_out, m_out, v_out):
        _adamw_math(g_ref[...], w_ref, m_ref, v_ref, g_out, d_out, m_out, v_out)

    spec = pl.BlockSpec((tr, cols), lambda i: (i, 0))
    shp = jax.ShapeDtypeStruct((rows, cols), F32)
    return pl.pallas_call(
        body,
        out_shape=(shp, shp, shp, shp),
        grid=(rows // tr,),
        in_specs=[spec] * 4,
        out_specs=(spec, spec, spec, spec),
        compiler_params=_cp(("parallel",)),
        name=name,
    )(w, m, v, g)


ANY = pl.BlockSpec(memory_space=pl.ANY)
STAGE_BYTES = 2 * 1024 * 1024


def _mesh_pos():
    return lax.axis_index("x"), lax.axis_index("y"), lax.axis_index("c")


def _chip_peers(x, y):
    return [(1 - x, y), (x, 1 - y), (1 - x, 1 - y)]


def _all_peers(x, y, c):
    return [((1 - x) if (r >> 2) & 1 else x, (1 - y) if (r >> 1) & 1 else y, (1 - c) if r & 1 else c)
            for r in range(1, 8)]


class LayerGather:
    def __init__(self, shards, axes, layer):
        self.nt = len(shards)
        self.axes = list(axes)
        self.layer = layer
        self.shapes = [s.shape for s in shards]
        self.dtypes = [s.dtype for s in shards]
        self.sizes = [s.shape[a] for s, a in zip(shards, axes)]
        self.split = [s.shape[1] % 32 == 0 for s in shards]
        self.half_rows = [s.shape[1] // 2 if sp else s.shape[1] for s, sp in zip(shards, self.split)]
        self.chunk_rows = []
        for s in shards:
            rt = s.shape[1]
            while rt % 32 == 0 and rt * s.shape[2] * s.dtype.itemsize > STAGE_BYTES:
                rt //= 2
            self.chunk_rows.append(rt)

    def out_shapes(self):
        out = []
        for shp, a, sz, dt in zip(self.shapes, self.axes, self.sizes, self.dtypes):
            shp = list(shp)
            shp[a] = 4 * sz
            out.append(jax.ShapeDtypeStruct(tuple(shp), dt))
        return out

    def scratch_shapes(self):
        return ([pltpu.VMEM((1, rt, shp[2]), dt) for shp, rt, dt in zip(self.shapes, self.chunk_rows, self.dtypes)]
                + [pltpu.SemaphoreType.DMA((2 * self.nt,))] + [pltpu.SemaphoreType.DMA((3 * self.nt,)) for _ in range(4)])

    def _views(self, ins, outs, scratch):
        nt = self.nt
        stage, stage_sems = scratch[:nt], scratch[nt]
        ici_send, ici_recv, d2d_send, d2d_recv = scratch[nt + 1:]
        x, y, c = _mesh_pos()
        mine = 2 * x + y
        peers = _chip_peers(x, y)
        layer = pl.ds(self.layer, 1)

        def rows(t, half, r0=0, n=None):
            hr = self.half_rows[t]
            if n is None:
                return pl.ds(pl.multiple_of(half * hr, 16), hr) if self.split[t] else pl.ds(0, hr)
            return pl.ds(r0, n)

        def placed(t, blk, row_sel, row_len):
            sz = self.sizes[t]
            if self.axes[t] == 2:
                return outs[t].at[layer, row_sel, pl.ds(pl.multiple_of(blk * sz, 128), sz)]
            return outs[t].at[layer, pl.ds(pl.multiple_of(blk * sz, 16) + row_sel.start, row_len), :]

        def ici(t, k, blk):
            px, py = peers[k]
            sel = rows(t, c)
            return pltpu.make_async_remote_copy(
                src_ref=ins[t].at[layer, sel, :], dst_ref=placed(t, blk, sel, self.half_rows[t]),
                send_sem=ici_send.at[3 * t + k], recv_sem=ici_recv.at[3 * t + k],
                device_id=(px, py, c), device_id_type=MESH_T)

        def d2d(t, k, half):
            px, py = peers[k]
            piece = placed(t, 2 * px + py, rows(t, half), self.half_rows[t])
            return pltpu.make_async_remote_copy(
                src_ref=piece, dst_ref=piece, send_sem=d2d_send.at[3 * t + k], recv_sem=d2d_recv.at[3 * t + k],
                device_id=(x, y, 1 - c), device_id_type=MESH_T)

        def own_chunk(t, r0):
            rt = self.chunk_rows[t]
            sel = pl.ds(r0, rt)
            return ins[t].at[layer, sel, :], placed(t, mine, sel, rt), stage[t], stage_sems

        return c, mine, peers, ici, d2d, own_chunk

    def start(self, ins, outs, scratch):
        c, mine, peers, ici, d2d, own_chunk = self._views(ins, outs, scratch)
        for t in range(self.nt):
            for k in range(3):
                ici(t, k, mine).start()
        starts = [list(range(0, self.shapes[t][1], self.chunk_rows[t])) for t in range(self.nt)]
        for r in range(max(len(s) for s in starts)):
            active = [(t, *own_chunk(t, starts[t][r])) for t in range(self.nt) if r < len(starts[t])]
            loads = [pltpu.make_async_copy(src, buf, sems.at[2 * t]) for t, src, dst, buf, sems in active]
            for cp in loads:
                cp.start()
            for cp in loads:
                cp.wait()
            stores = [pltpu.make_async_copy(buf, dst, sems.at[2 * t + 1]) for t, src, dst, buf, sems in active]
            for cp in stores:
                cp.start()
            for cp in stores:
                cp.wait()

    def finish(self, ins, outs, scratch):
        c, mine, peers, ici, d2d, own_chunk = self._views(ins, outs, scratch)
        for t in range(self.nt):
            for k, (px, py) in enumerate(peers):
                ici(t, k, 2 * px + py).wait_recv()
                if self.split[t]:
                    d2d(t, k, c).start()
        for t in range(self.nt):
            for k in range(3):
                if self.split[t]:
                    d2d(t, k, 1 - c).wait_recv()
        for t in range(self.nt):
            for k in range(3):
                ici(t, k, mine).wait_send()
                if self.split[t]:
                    d2d(t, k, c).wait_send()


def all_gather_layer(shards, axes, layer, *, name):
    plan = LayerGather(shards, axes, layer)
    nt = plan.nt

    def body(*refs):
        ins, outs, scratch = refs[:nt], refs[nt:2 * nt], refs[2 * nt:]
        plan.start(ins, outs, scratch)
        plan.finish(ins, outs, scratch)

    return pl.pallas_call(
        body,
        out_shape=tuple(plan.out_shapes()),
        in_specs=[ANY] * nt,
        out_specs=tuple([ANY] * nt),
        scratch_shapes=plan.scratch_shapes(),
        name=name,
    )(*shards)


class HalfLayout:
    def __init__(self, shape, axis):
        self.R, self.C = shape
        self.axis = axis
        if axis == 1:
            self.hr, self.pw = self.R // 2, self.C // 4
            self.half_shape = (self.hr, self.C)
        else:
            self.hr, self.pw = self.R // 8, self.C
            self.half_shape = (4 * self.hr, self.C)
        self.tr = _rows_tile(self.hr, self.pw)
        self.nr = self.hr // self.tr

    def in_grad(self, ref, blk, half):
        if self.axis == 1:
            return ref.at[pl.ds(pl.multiple_of(half * self.hr, 16), self.hr), pl.ds(pl.multiple_of(blk * self.pw, 128), self.pw)]
        return ref.at[pl.ds(pl.multiple_of((2 * blk + half) * self.hr, 16), self.hr), :]

    def in_half(self, ref, blk):
        if self.axis == 1:
            return ref.at[:, pl.ds(pl.multiple_of(blk * self.pw, 128), self.pw)]
        return ref.at[pl.ds(pl.multiple_of(blk * self.hr, 16), self.hr), :]

    def grad_spec(self):
        if self.axis == 1:
            return pl.BlockSpec((self.tr, self.pw), lambda j, i, s: (s[0] * self.nr + i, j))
        return pl.BlockSpec((self.tr, self.pw), lambda j, i, s: ((2 * j + s[0]) * self.nr + i, 0))

    def half_spec(self):
        if self.axis == 1:
            return pl.BlockSpec((self.tr, self.pw), lambda j, i, s: (i, j))
        return pl.BlockSpec((self.tr, self.pw), lambda j, i, s: (j * self.nr + i, 0))


def half_exchange(grads, layouts, *, name):
    nt = len(grads)
    pieces = [(t, j) for t in range(nt) for j in (range(4) if layouts[t].axis == 0 else range(1))]

    def body(*refs):
        ins, outs = refs[:nt], refs[nt:2 * nt]
        send_sems, recv_sems = refs[2 * nt:]
        x, y, c = _mesh_pos()
        cps = []
        for n, (t, j) in enumerate(pieces):
            lay = layouts[t]
            if lay.axis == 1:
                src = ins[t].at[pl.ds(pl.multiple_of((1 - c) * lay.hr, 16), lay.hr), :]
                dst = outs[t]
            else:
                src = lay.in_grad(ins[t], j, 1 - c)
                dst = lay.in_half(outs[t], j)
            cp = pltpu.make_async_remote_copy(src_ref=src, dst_ref=dst, send_sem=send_sems.at[n], recv_sem=recv_sems.at[n],
                                              device_id=(x, y, 1 - c), device_id_type=MESH_T)
            cp.start()
            cps.append(cp)
        for cp in cps:
            cp.wait_recv()
        for cp in cps:
            cp.wait_send()

    return pl.pallas_call(
        body,
        out_shape=tuple(jax.ShapeDtypeStruct(lay.half_shape, F32) for lay in layouts),
        in_specs=[ANY] * nt,
        out_specs=tuple([ANY] * nt),
        scratch_shapes=[pltpu.SemaphoreType.DMA((len(pieces),)), pltpu.SemaphoreType.DMA((len(pieces),))],
        name=name,
    )(*grads)


def pair_sum(grad, other, lay, core, *, name):
    def body(c_ref, g_ref, o_ref, s32_ref, s16_ref):
        s = g_ref[...] + o_ref[...]
        s32_ref[...] = s
        s16_ref[...] = s.astype(BF16)

    return pl.pallas_call(
        body,
        out_shape=(jax.ShapeDtypeStruct(lay.half_shape, F32), jax.ShapeDtypeStruct(lay.half_shape, BF16)),
        grid_spec=pltpu.PrefetchScalarGridSpec(
            num_scalar_prefetch=1, grid=(4, lay.nr),
            in_specs=[lay.grad_spec(), lay.half_spec()],
            out_specs=(lay.half_spec(), lay.half_spec())),
        compiler_params=_cp(("parallel", "parallel")),
        name=name,
    )(core, grad, other)


class BlockScatter:
    def __init__(self, layouts):
        self.layouts = layouts
        self.nt = len(layouts)

    def out_shapes(self):
        return [jax.ShapeDtypeStruct((3, lay.hr, lay.pw), BF16) for lay in self.layouts]

    def scratch_shapes(self):
        return [pltpu.SemaphoreType.DMA((3 * self.nt,)), pltpu.SemaphoreType.DMA((3 * self.nt,))]

    def _copies(self, pairs16, recv, scratch):
        send_sems, recv_sems = scratch
        x, y, c = _mesh_pos()
        return [pltpu.make_async_remote_copy(
            src_ref=lay.in_half(pairs16[t], 2 * px + py), dst_ref=recv[t].at[k],
            send_sem=send_sems.at[3 * t + k], recv_sem=recv_sems.at[3 * t + k],
            device_id=(px, py, c), device_id_type=MESH_T)
            for t, lay in enumerate(self.layouts) for k, (px, py) in enumerate(_chip_peers(x, y))]

    def start(self, pairs16, recv, scratch):
        for cp in self._copies(pairs16, recv, scratch):
            cp.start()

    def finish(self, pairs16, recv, scratch):
        copies = self._copies(pairs16, recv, scratch)
        for cp in copies:
            cp.wait_recv()
        for cp in copies:
            cp.wait_send()


def gather_small(small, *, name):
    def body(small_in, small_out, ssend, srecv):
        x, y, c = _mesh_pos()
        me = 4 * x + 2 * y + c
        sends, recvs = [], []
        for r, (px, py, pc) in enumerate(_all_peers(x, y, c)):
            def mk(slot, r=r, px=px, py=py, pc=pc):
                return pltpu.make_async_remote_copy(
                    src_ref=small_in, dst_ref=small_out.at[slot], send_sem=ssend.at[r], recv_sem=srecv.at[r],
                    device_id=(px, py, pc), device_id_type=MESH_T)
            snd = mk(me)
            snd.start()
            sends.append(snd)
            recvs.append(mk(4 * px + 2 * py + pc))
        for r in recvs:
            r.wait_recv()
        for s in sends:
            s.wait_send()

    return pl.pallas_call(
        body,
        out_shape=jax.ShapeDtypeStruct((8,) + small.shape, F32),
        in_specs=[ANY],
        out_specs=ANY,
        scratch_shapes=[pltpu.SemaphoreType.DMA((7,)), pltpu.SemaphoreType.DMA((7,))],
        name=name,
    )(small)


def sum_chips(pair32, recv, lay, chip, *, name):
    def body(j_ref, p_ref, r_ref, s_ref):
        acc = p_ref[...]
        for k in range(3):
            acc = acc + r_ref[k].astype(F32)
        s_ref[...] = acc

    if lay.axis == 1:
        own = pl.BlockSpec((lay.tr, lay.pw), lambda i, j: (i, j[0]))
    else:
        own = pl.BlockSpec((lay.tr, lay.pw), lambda i, j: (j[0] * lay.nr + i, 0))
    return pl.pallas_call(
        body,
        out_shape=jax.ShapeDtypeStruct((lay.hr, lay.pw), F32),
        grid_spec=pltpu.PrefetchScalarGridSpec(
            num_scalar_prefetch=1, grid=(lay.nr,),
            in_specs=[own, pl.BlockSpec((3, lay.tr, lay.pw), lambda i, j: (0, i, 0))],
            out_specs=pl.BlockSpec((lay.tr, lay.pw), lambda i, j: (i, 0))),
        compiler_params=_cp(("parallel",)),
        name=name,
    )(chip, pair32, recv)


def sum_devices(gathered, own, me, *, name):
    _, R, C = gathered.shape

    def body(me_ref, g_ref, o_ref, s_ref):
        acc = None
        for k in range(8):
            part = jnp.where(me_ref[0] == k, o_ref[...], g_ref[k])
            acc = part if acc is None else acc + part
        s_ref[...] = acc

    return pl.pallas_call(
        body,
        out_shape=jax.ShapeDtypeStruct((R, C), F32),
        grid_spec=pltpu.PrefetchScalarGridSpec(
            num_scalar_prefetch=1, grid=(1,),
            in_specs=[pl.BlockSpec((8, R, C), lambda i, m: (0, 0, 0)), pl.BlockSpec((R, C), lambda i, m: (0, 0))],
            out_specs=pl.BlockSpec((R, C), lambda i, m: (0, 0))),
        compiler_params=_cp(("arbitrary",)),
        name=name,
    )(me, gathered, own)


def sibling_swap(parts, *, name):
    nt = len(parts)

    def body(*refs):
        ins, outs = refs[:nt], refs[nt:2 * nt]
        send_sems, recv_sems = refs[2 * nt:]
        x, y, c = _mesh_pos()
        cps = []
        for t in range(nt):
            cp = pltpu.make_async_remote_copy(src_ref=ins[t], dst_ref=outs[t], send_sem=send_sems.at[t],
                                              recv_sem=recv_sems.at[t], device_id=(x, y, 1 - c), device_id_type=MESH_T)
            cp.start()
            cps.append(cp)
        for cp in cps:
            cp.wait_recv()
        for cp in cps:
            cp.wait_send()

    return pl.pallas_call(
        body,
        out_shape=tuple(jax.ShapeDtypeStruct(p.shape, p.dtype) for p in parts),
        in_specs=[ANY] * nt,
        out_specs=tuple([ANY] * nt),
        scratch_shapes=[pltpu.SemaphoreType.DMA((nt,)), pltpu.SemaphoreType.DMA((nt,))],
        name=name,
    )(*parts)


def adamw_halves(w, m, v, mine, other, lay, core, *, name):
    _, r, c = w.shape
    tr, nr = lay.tr, lay.nr
    assert (r, c) == (2 * lay.hr, lay.pw), (w.shape, lay.hr, lay.pw)

    def body(c_ref, w_ref, m_ref, v_ref, *rest):
        g_refs, outs = rest[:2 * DEPTH], rest[2 * DEPTH:]
        l, h = pl.program_id(0), pl.program_id(1)
        g = None
        for d in range(DEPTH):
            gd = jnp.where(h == c_ref[0], g_refs[d][...], g_refs[DEPTH + d][...])
            g = gd if g is None else jnp.where(l == d, gd, g)
        _adamw_math(g, w_ref, m_ref, v_ref, *outs)

    full = pl.BlockSpec((None, tr, c), lambda l, h, i, s: (l, h * nr + i, 0))

    def part(d, is_mine):
        def index(l, h, i, s):
            used = jnp.logical_and(l == d, (h == s[0]) == is_mine)
            return jnp.where(used, i, 0), 0
        return pl.BlockSpec((tr, c), index)

    shp = jax.ShapeDtypeStruct((DEPTH, r, c), F32)
    return pl.pallas_call(
        body,
        out_shape=(shp, shp, shp, shp),
        grid_spec=pltpu.PrefetchScalarGridSpec(
            num_scalar_prefetch=1, grid=(DEPTH, 2, nr),
            in_specs=[full, full, full] + [part(d, True) for d in range(DEPTH)] + [part(d, False) for d in range(DEPTH)],
            out_specs=(full, full, full, full)),
        compiler_params=_cp(("arbitrary", "arbitrary", "arbitrary")),
        name=name,
    )(core, w, m, v, *mine, *other)


WEIGHTS = ("norm_mix", "w_in", "b_gate", "q_norm_a", "k_norm_a", "rel_bias_a", "w_pool", "pool_scale",
           "w_branch_a", "w_branch_b", "w_branch_c", "w_out", "norm_ffn", "w_up", "conv_w", "conv_b", "w_down")
SHARDED = {"w_in": 2, "w_branch_a": 2, "w_branch_b": 2, "w_branch_c": 2, "w_out": 1, "w_up": 2, "conv_w": 2,
           "w_down": 1}
REPLICATED = tuple(n for n in WEIGHTS if n not in SHARDED)
MATMUL_WEIGHTS = tuple(n for n in SHARDED if n != "conv_w")
SMALL_WEIGHTS = tuple(n for n in WEIGHTS if n not in MATMUL_WEIGHTS)
SMALL_ROWS = 1496


def _layer_fwd(x, p, full, tables, rest=None, prefetch=None):
    l = p["l"]
    diag = exact_dot(p["rel_bias_a"], tables["onehot_t"], name="bias_diagonals")
    diag = diag.reshape(N_HEADS, N_VARIANTS, 1, DIAG_W).transpose(1, 0, 2, 3)
    biasm = bias_expand(diag, name="bias_expand")
    gq8 = jnp.tile(p["q_norm_a"], N_HEADS)[None]
    gk8 = jnp.tile(p["k_norm_a"], N_HEADS)[None]
    h = rmsnorm_fwd(x, p["norm_mix"][None], name="rmsnorm_fwd")
    if rest is None:
        proj = matmul(h, full["w_in"], b_layer=l, name="mm_in")
    else:
        plan, shards, names = rest
        proj, gathered = matmul(h, full["w_in"], b_layer=l, behind=(plan, shards), name="mm_in_gather")
        full = {**full, **dict(zip(names, gathered))}
    qa, ka, va, qb, kb, vb = qkv_prep(proj, gq8, gk8, name="qkv_prep")
    oa = attn_a_fwd(qa, ka, va, biasm, name="attn_a_fwd")
    if prefetch is None:
        ob, tot, nblk = attn_b_fwd(qb, kb, vb, name="attn_b_fwd")
    else:
        plan, shards, names = prefetch
        ob, tot, nblk, filled = attn_b_fwd(qb, kb, vb, gather=(plan, shards, [full[n] for n in names]),
                                           name="attn_b_fwd_gather")
        full = dict(zip(names, filled))
    wpool = p["w_pool"].astype(BF16)
    oc = pool_fwd(proj, wpool, p["pool_scale"][None], name="pool_fwd")
    merged = merge_fwd(oa, ob, oc, proj, p["b_gate"][None], full["w_branch_a"][l], full["w_branch_b"][l],
                       full["w_branch_c"][l], name="merge_fwd")
    x1 = matmul(merged, full["w_out"], b_layer=l, add=x, name="mm_out")
    h2 = rmsnorm_fwd(x1, p["norm_ffn"][None], name="rmsnorm_fwd")
    u = matmul(h2, full["w_up"], b_layer=l, name="mm_up")
    a = conv_glu_fwd(u, full["conv_w"][l], p["conv_b"][None], name="conv_glu_fwd")
    x2 = matmul(a, full["w_down"], b_layer=l, add=x1, name="mm_down")
    saved = dict(x=x, h=h, proj=proj, qa=qa, ka=ka, va=va, qb=qb, kb=kb, vb=vb, oa=oa, ob=ob, tot=tot, nblk=nblk, oc=oc,
                 merged=merged, x1=x1, h2=h2, u=u, a=a, biasm=biasm, gq8=gq8, gk8=gk8, wpool=wpool)
    return x2, saved, full


class GradReducer:
    def __init__(self, layouts, core, chip):
        self.layouts, self.core, self.chip = layouts, core, chip
        self.pairs32, self.received, self.pending = {}, {}, []

    def prepare(self, layer, grads):
        names = list(grads)
        lays = [self.layouts[n] for n in names]
        others = half_exchange([grads[n] for n in names], lays, name="half_exchange")
        for n, lay, other in zip(names, lays, others):
            p32, p16 = pair_sum(grads[n], other, lay, self.core, name="pair_sum")
            self.pairs32[(layer, n)] = p32
            self.pending.append(((layer, n), p16))

    def take(self):
        keys = [k for k, _ in self.pending]
        pairs16 = [p for _, p in self.pending]
        self.pending = []
        return keys, BlockScatter([self.layouts[n] for _, n in keys]), pairs16

    def store(self, keys, received):
        self.received.update(zip(keys, received))

    def finish(self):
        keys = list(self.pairs32)
        mine = [sum_chips(self.pairs32[k], self.received[k], self.layouts[k[1]], self.chip, name="sum_chips") for k in keys]
        other = sibling_swap(mine, name="sibling_swap")
        return dict(zip(keys, mine)), dict(zip(keys, other))


EARLY_WEIGHTS = ("w_down", "w_up", "w_out", "w_branch_a", "w_branch_b", "w_branch_c")


def _layer_bwd(dx2, s, p, tables, reducer):
    g = {}
    full, l = p["full"], p["l"]
    da = matmul(dx2, full["w_down"], b_layer=l, tb=True, name="mm_down_dx")
    g["w_down"] = matmul(s["a"], dx2, ta=True, name="mm_down_dw")
    dug, duv, dcwg, dcwv, dcbg, dcbv = conv_glu_bwd(s["u"], p["conv_w"], p["conv_b"][None], da, name="conv_glu_bwd")
    du = jnp.concatenate([dug, duv], axis=1)
    g["conv_w"] = jnp.concatenate([dcwg, dcwv], axis=1)
    g["conv_b"] = jnp.concatenate([dcbg, dcbv], axis=1)[0]
    g["w_up"] = matmul(s["h2"], du, ta=True, name="mm_up_dw")
    dx1, dg2 = matmul(du, full["w_up"], b_layer=l, tb=True, norm_bwd=(s["x1"], p["norm_ffn"][None], dx2),
                      name="mm_up_dx_norm")
    g["norm_ffn"] = dg2[0]
    dmerged = matmul(dx1, full["w_out"], b_layer=l, tb=True, name="mm_out_dx")
    g["w_out"] = matmul(s["merged"], dx1, ta=True, name="mm_out_dw")
    t_a, t_b, t_c, dga, dgb, dgc, dba, dbb, dbc = merge_bwd(
        dmerged, s["oa"], s["ob"], s["oc"], s["proj"], p["b_gate"][None], p["w_branch_a"], p["w_branch_b"],
        p["w_branch_c"], name="merge_bwd")
    g["b_gate"] = jnp.concatenate([dba, dbb, dbc], axis=1)[0]
    g["w_branch_a"] = matmul(s["oa"], t_a, ta=True, name="mm_branch_dw")
    g["w_branch_b"] = matmul(s["ob"], t_b, ta=True, name="mm_branch_dw")
    g["w_branch_c"] = matmul(s["oc"], t_c, ta=True, name="mm_branch_dw")
    doa = matmul(t_a, full["w_branch_a"], b_layer=l, tb=True, out_dtype=BF16, name="mm_branch_dx")
    dob = matmul(t_b, full["w_branch_b"], b_layer=l, tb=True, out_dtype=BF16, name="mm_branch_dx")
    doc = matmul(t_c, full["w_branch_c"], b_layer=l, tb=True, name="mm_branch_dx_f32")
    dqh, dkh, dva, dbias = attn_a_bwd(s["qa"], s["ka"], s["va"], s["biasm"], doa, name="attn_a_bwd")
    ddiag = relbias_reduce(dbias, name="relbias_reduce")
    ddiag = ddiag.transpose(1, 0, 2, 3).reshape(N_HEADS, N_VARIANTS * DIAG_W)
    g["rel_bias_a"] = exact_dot(ddiag, tables["onehot"], name="relbias_table")
    dqa, dka, dgq8, dgk8 = qknorm_bwd(s["proj"], s["gq8"], s["gk8"], dqh, dkh, name="qknorm_bwd")
    g["q_norm_a"] = dgq8.reshape(N_HEADS, HEAD_DIM).sum(axis=0)
    g["k_norm_a"] = dgk8.reshape(N_HEADS, HEAD_DIM).sum(axis=0)
    reducer.prepare(l, {n: g[n] for n in EARLY_WEIGHTS})
    keys, plan, pairs16 = reducer.take()
    dqb, dkb, dvb, received = attn_b_bwd(s["qb"], s["kb"], s["vb"], s["tot"], s["nblk"], dob, scatter=(plan, pairs16),
                                         name="attn_b_bwd_scatter")
    reducer.store(keys, received)
    duc, dwp, dsc = pool_bwd(s["proj"], s["wpool"], p["pool_scale"][None], doc, name="pool_bwd")
    g["w_pool"] = dwp
    g["pool_scale"] = dsc[0]
    dproj = jnp.concatenate([dqa, dka, dva.astype(BF16), dqb, dkb.astype(BF16), dvb.astype(BF16), duc,
                             dga, dgb, dgc], axis=1)
    g["w_in"] = matmul(s["h"], dproj, ta=True, name="mm_in_dw")
    reducer.prepare(l, {"w_in": g["w_in"]})
    norm = (s["x"], p["norm_mix"][None], dx1)
    if l > 0:
        dx, dg1 = matmul(dproj, full["w_in"], b_layer=l, tb=True, norm_bwd=norm, name="mm_in_dx_norm")
    else:
        keys, plan, pairs16 = reducer.take()
        dx, dg1, received = matmul(dproj, full["w_in"], b_layer=l, tb=True, norm_bwd=norm, behind=(plan, pairs16),
                                   name="mm_in_dx_norm_scatter")
        reducer.store(keys, received)
    g["norm_mix"] = dg1[0]
    return dx, g


def kernel(x, norm_mix, w_in, b_gate, q_norm_a, k_norm_a, rel_bias_a, w_pool, pool_scale, w_branch_a, w_branch_b, w_branch_c, w_out, norm_ffn, w_up, conv_w, conv_b, w_down, loss_target, m_norm_mix, m_w_in, m_b_gate, m_q_norm_a, m_k_norm_a, m_rel_bias_a, m_w_pool, m_pool_scale, m_w_branch_a, m_w_branch_b, m_w_branch_c, m_w_out, m_norm_ffn, m_w_up, m_conv_w, m_conv_b, m_w_down, v_norm_mix, v_w_in, v_b_gate, v_q_norm_a, v_k_norm_a, v_rel_bias_a, v_w_pool, v_pool_scale, v_w_branch_a, v_w_branch_b, v_w_branch_c, v_w_out, v_norm_ffn, v_w_up, v_conv_w, v_conv_b, v_w_down):
    w = dict(zip(WEIGHTS, (norm_mix, w_in, b_gate, q_norm_a, k_norm_a, rel_bias_a, w_pool, pool_scale, w_branch_a,
                           w_branch_b, w_branch_c, w_out, norm_ffn, w_up, conv_w, conv_b, w_down)))
    m = dict(zip(WEIGHTS, (m_norm_mix, m_w_in, m_b_gate, m_q_norm_a, m_k_norm_a, m_rel_bias_a, m_w_pool, m_pool_scale,
                           m_w_branch_a, m_w_branch_b, m_w_branch_c, m_w_out, m_norm_ffn, m_w_up, m_conv_w, m_conv_b,
                           m_w_down)))
    v = dict(zip(WEIGHTS, (v_norm_mix, v_w_in, v_b_gate, v_q_norm_a, v_k_norm_a, v_rel_bias_a, v_w_pool, v_pool_scale,
                           v_w_branch_a, v_w_branch_b, v_w_branch_c, v_w_out, v_norm_ffn, v_w_up, v_conv_w, v_conv_b,
                           v_w_down)))
    onehot = diagonal_onehot()
    tables = dict(onehot=jnp.asarray(onehot), onehot_t=jnp.asarray(np.ascontiguousarray(onehot.T)))

    names = tuple(SHARDED)
    shards = [w[n] if n == "conv_w" else w[n].astype(BF16) for n in names]
    axes = [SHARDED[n] for n in names]
    later = [i for i, n in enumerate(names) if n != "w_in"]
    rest = (LayerGather([shards[i] for i in later], [axes[i] for i in later], 0), [shards[i] for i in later],
            [names[i] for i in later])
    first = names.index("w_in")
    full = {"w_in": all_gather_layer([shards[first]], [axes[first]], 0, name="all_gather_layer")[0]}

    def layer_params(l):
        p = {n: full[n][l] for n in ("w_branch_a", "w_branch_b", "w_branch_c", "conv_w")}
        p.update({n: w[n][l] for n in REPLICATED})
        p.update(full=full, l=l)
        return p

    xs = x[0]
    saved = []
    for l in range(DEPTH):
        prefetch = (LayerGather(shards, axes, l + 1), shards, names) if l + 1 < DEPTH else None
        replicated = {n: w[n][l] for n in REPLICATED}
        xs, s, full = _layer_fwd(xs, dict(replicated, l=l), full, tables, rest if l == 0 else None, prefetch)
        saved.append(s)
    dx, lpart = loss_head(xs, loss_target[0], name="loss_head")
    loss = lax.psum(lpart[0, 0], MESH_AXES)
    as_index = lambda i: jnp.reshape(i, (1,)).astype(jnp.int32)
    cx, cy, cc = _mesh_pos()
    core, chip, me = as_index(cc), as_index(2 * cx + cy), as_index(4 * cx + 2 * cy + cc)
    layouts = {n: HalfLayout((full[n].shape[1], full[n].shape[2]), SHARDED[n] - 1) for n in MATMUL_WEIGHTS}
    reducer = GradReducer(layouts, core, chip)
    grads = [None] * DEPTH
    for l in reversed(range(DEPTH)):
        dx, grads[l] = _layer_bwd(dx, saved[l], layer_params(l), tables, reducer)

    g = {n: jnp.stack([grads[l][n] for l in range(DEPTH)]) for n in SMALL_WEIGHTS}
    flat = jnp.concatenate([g[n].reshape(-1) for n in SMALL_WEIGHTS])
    small = jnp.pad(flat, (0, SMALL_ROWS * 128 - flat.shape[0])).reshape(SMALL_ROWS, 128)
    small_sum = sum_devices(gather_small(small, name="gather_small"), small, me, name="sum_devices").reshape(-1)
    mine, other = reducer.finish()

    res = {}
    for n in MATMUL_WEIGHTS:
        res[n] = adamw_halves(w[n], m[n], v[n], [mine[(l, n)] for l in range(DEPTH)],
                              [other[(l, n)] for l in range(DEPTH)], layouts[n], core, name="adamw_halves")
    off = 0
    for n in SMALL_WEIGHTS:
        shp = g[n].shape
        size = int(np.prod(shp))
        gn = small_sum[off:off + size].reshape(shp)
        off += size
        if n in SHARDED:
            gn = lax.dynamic_slice_in_dim(gn, (2 * cx + cy) * w[n].shape[-1], w[n].shape[-1], axis=len(shp) - 1)
        shp = w[n].shape
        cols = shp[-1]
        two_d = lambda t: t.reshape(int(np.prod(shp)) // cols, cols)
        res[n] = [t.reshape(shp) for t in adamw(two_d(w[n]), two_d(m[n]), two_d(v[n]), two_d(gn), name="adamw")]

    out = [loss, dx[None]]
    for k in range(4):
        out.extend(res[n][k] for n in WEIGHTS)
    return tuple(out)
```

```python
import jax
import jax.numpy as jnp
import numpy as np
from jax import lax
from jax.experimental import pallas as pl
from jax.experimental.pallas import tpu as pltpu

F32 = jnp.float32
BF16 = jnp.bfloat16

D_MODEL = 1024
DEPTH = 2
CHUNK = 64
N_LEFT = 8
HEAD_DIM = 64
N_HEADS = 8
WIDTH = 512
POOL_WINDOWS = (2, 4, 8, 16)
GROUP_DIM = 128
MAX_REL = 2 * CHUNK
REL_TABLE = MAX_REL + CHUNK
D_FF = 2816
EPS = 1e-6
QK_SCALE = 0.125
GATE_COL0 = 7 * WIDTH

ADAM_LR = 0.001
ADAM_B1 = 0.9
ADAM_B2 = 0.999
ADAM_EPS = 1e-08
ADAM_WD = 0.01
ADAM_STEP = 10

VMEM_LIMIT = 56 * 1024 * 1024
ATT_Q = 256
A_Q = 256
A_WIN = A_Q + N_LEFT * CHUNK
A_FWD_HEADS = 8
A_BWD_HEADS = 4
B_FWD_HEADS = 8
B_BWD_HEADS = 4
HALO = 16
CONV_HALO = 8
NEG = -1e30

MESH_AXES = ("x", "y", "c")
MESH_T = pl.DeviceIdType.MESH


def _cp(sem=None, vmem=VMEM_LIMIT):
    return pltpu.CompilerParams(dimension_semantics=sem, vmem_limit_bytes=vmem)


def _dot(a, b, ca, cb):
    return lax.dot_general(a, b, (((ca,), (cb,)), ((), ())), preferred_element_type=F32)


def _tile(n, cands=(512, 256, 128)):
    for c in cands:
        if n % c == 0:
            return c
    return n


def _split_hi_lo(v):
    hi = v.astype(BF16)
    lo = (v - hi.astype(F32)).astype(BF16)
    return hi, lo


def matmul(a, b, *, ta=False, tb=False, add=None, norm_bwd=None, out_dtype=F32, b_layer=None, behind=None, name):
    plan, sources = behind if behind is not None else (None, [])
    n_src = len(sources)
    n_dst = len(plan.out_shapes()) if plan is not None else 0
    assert add is None or norm_bwd is None
    if ta:
        K, M = a.shape
    else:
        M, K = a.shape
    if tb:
        N, K2 = b.shape[-2:]
    else:
        K2, N = b.shape[-2:]
    assert K == K2, (a.shape, b.shape, ta, tb)
    big = (1024, 1408, 512, 256, 128)
    tm = _tile(M, big)
    tn = _tile(N, (1664,) + big)
    tk = _tile(K, big if ta else (1664, 1408) + big if norm_bwd is not None else (3328, 2816) + big)
    nk = K // tk

    n_extra = 1 if add is not None else 3 if norm_bwd is not None else 0
    n_in = 2 + n_extra + n_src
    n_out = 2 if norm_bwd is not None else 1
    grid = (M // tm, N // tn, nk)
    assert norm_bwd is None or tn == N, "the norm gradient needs whole rows in one output tile"

    def body(*refs):
        a_ref, b_ref = refs[:2]
        extra = refs[2:2 + n_extra]
        o_ref, acc = refs[n_in], refs[n_in + n_out + n_dst]
        i, j, k = pl.program_id(0), pl.program_id(1), pl.program_id(2)
        if plan is not None:
            comm = (refs[n_in - n_src:n_in], refs[n_in + n_out:n_in + n_out + n_dst], refs[n_in + n_out + n_dst + 1:])

            @pl.when((i == 0) & (j == 0) & (k == 0))
            def _():
                plan.start(*comm)

        @pl.when(k == 0)
        def _():
            acc[...] = jnp.zeros_like(acc)

        av = a_ref[...].astype(BF16)
        bv = b_ref[...].astype(BF16)
        acc[...] += _dot(av, bv, 0 if ta else 1, 1 if tb else 0)

        @pl.when(k == nk - 1)
        def _():
            r = acc[...]
            if add is not None:
                r = r + extra[0][...].astype(F32)
            if norm_bwd is not None:
                x_ref, g_ref, dres_ref = extra
                dg_ref = refs[n_in + 1]
                xv = x_ref[...]
                inv = lax.rsqrt(jnp.mean(xv * xv, axis=-1, keepdims=True) + EPS)
                gd = r * g_ref[...]
                mean = jnp.mean(xv * gd, axis=-1, keepdims=True)

                @pl.when(i == 0)
                def _():
                    dg_ref[...] = jnp.zeros_like(dg_ref)

                dg_ref[...] += jnp.sum(r * xv * inv, axis=0, keepdims=True)
                r = dres_ref[...] + inv * gd - xv * (inv * inv * inv * mean)
            o_ref[...] = r.astype(out_dtype)

        if plan is not None:
            @pl.when((i == grid[0] - 1) & (j == grid[1] - 1) & (k == nk - 1))
            def _():
                plan.finish(*comm)

    a_spec = pl.BlockSpec((tk, tm), lambda i, j, k: (k, i)) if ta else pl.BlockSpec((tm, tk), lambda i, j, k: (i, k))
    if b_layer is None:
        b_spec = pl.BlockSpec((tn, tk), lambda i, j, k: (j, k)) if tb else pl.BlockSpec((tk, tn), lambda i, j, k: (k, j))
    elif tb:
        b_spec = pl.BlockSpec((None, tn, tk), lambda i, j, k: (b_layer, j, k))
    else:
        b_spec = pl.BlockSpec((None, tk, tn), lambda i, j, k: (b_layer, k, j))
    o_spec = pl.BlockSpec((tm, tn), lambda i, j, k: (i, j))
    vec_spec = pl.BlockSpec((1, tn), lambda i, j, k: (0, j))
    in_specs = [a_spec, b_spec]
    args = [a, b]
    out_shape = [jax.ShapeDtypeStruct((M, N), out_dtype)]
    out_specs = [o_spec]
    if add is not None:
        in_specs.append(o_spec)
        args.append(add)
    if norm_bwd is not None:
        in_specs += [o_spec, vec_spec, o_spec]
        args += list(norm_bwd)
        out_shape.append(jax.ShapeDtypeStruct((1, N), F32))
        out_specs.append(vec_spec)
    sequential = plan is not None or norm_bwd is not None
    any_space = pl.BlockSpec(memory_space=pl.ANY)
    outs = pl.pallas_call(
        body,
        out_shape=tuple(out_shape) + (tuple(plan.out_shapes()) if plan is not None else ()),
        grid=grid,
        in_specs=in_specs + [any_space] * n_src,
        out_specs=tuple(out_specs) + tuple([any_space] * n_dst),
        scratch_shapes=[pltpu.VMEM((tm, tn), F32)] + (list(plan.scratch_shapes()) if plan is not None else []),
        compiler_params=_cp(("arbitrary" if sequential else "parallel",) * 2 + ("arbitrary",)),
        name=name,
    )(*args, *sources)
    if plan is None:
        return outs[0] if n_out == 1 else tuple(outs)
    return tuple(outs[:n_out]) + (list(outs[n_out:]),) if n_out > 1 else (outs[0], list(outs[1:]))


def rmsnorm_fwd(x, g, *, name):
    S, D = x.shape
    T = _tile(S)

    def body(x_ref, g_ref, h_ref):
        xv = x_ref[...]
        r = lax.rsqrt(jnp.mean(xv * xv, axis=-1, keepdims=True) + EPS)
        h_ref[...] = (xv * r * g_ref[...]).astype(BF16)

    return pl.pallas_call(
        body,
        out_shape=jax.ShapeDtypeStruct((S, D), BF16),
        grid=(S // T,),
        in_specs=[pl.BlockSpec((T, D), lambda i: (i, 0)), pl.BlockSpec((1, D), lambda i: (0, 0))],
        out_specs=pl.BlockSpec((T, D), lambda i: (i, 0)),
        compiler_params=_cp(("parallel",)),
        name=name,
    )(x, g)


def _head_mean_matrix():
    r = lax.broadcasted_iota(jnp.int32, (WIDTH, WIDTH), 0) // HEAD_DIM
    c = lax.broadcasted_iota(jnp.int32, (WIDTH, WIDTH), 1) // HEAD_DIM
    return jnp.where(r == c, 1.0 / HEAD_DIM, 0.0).astype(BF16)


def _head_mean(v, mm):
    hi, lo = _split_hi_lo(v)
    return _dot(hi, mm, 1, 0) + _dot(lo, mm, 1, 0)


def qkv_prep(proj, gq, gk, *, name):
    S = proj.shape[0]
    T = _tile(S)

    def body(qa, ka, va, qb, kb, vb, gq_ref, gk_ref, oqa, oka, ova, oqb, okb, ovb):
        mm = _head_mean_matrix()
        for src, gref, dst, scale in ((qa, gq_ref, oqa, QK_SCALE), (ka, gk_ref, oka, 1.0)):
            v = src[...]
            r = lax.rsqrt(_head_mean(v * v, mm) + EPS)
            dst[...] = (v * r * gref[...] * scale).astype(BF16)
        oqb[...] = (qb[...] * QK_SCALE).astype(BF16)
        for src, dst in ((va, ova), (kb, okb), (vb, ovb)):
            dst[...] = src[...].astype(BF16)

    col = lambda j: pl.BlockSpec((T, WIDTH), lambda i, j=j: (i, j))
    vec = pl.BlockSpec((1, WIDTH), lambda i: (0, 0))
    out = pl.BlockSpec((T, WIDTH), lambda i: (i, 0))
    return pl.pallas_call(
        body,
        out_shape=tuple(jax.ShapeDtypeStruct((S, WIDTH), BF16) for _ in range(6)),
        grid=(S // T,),
        in_specs=[col(0), col(1), col(2), col(3), col(4), col(5), vec, vec],
        out_specs=tuple(out for _ in range(6)),
        compiler_params=_cp(("parallel",)),
        name=name,
    )(proj, proj, proj, proj, proj, proj, gq, gk)


def qknorm_bwd(proj, gq, gk, dqh, dkh, *, name):
    S = proj.shape[0]
    T = _tile(S)

    def body(qa, ka, gq_ref, gk_ref, dq_ref, dk_ref, oq, ok, ogq, ogk):
        i = pl.program_id(0)
        mm = _head_mean_matrix()

        @pl.when(i == 0)
        def _():
            ogq[...] = jnp.zeros_like(ogq)
            ogk[...] = jnp.zeros_like(ogk)

        for src, gref, dref, dst, gdst in ((qa, gq_ref, dq_ref, oq, ogq), (ka, gk_ref, dk_ref, ok, ogk)):
            v = src[...]
            dy = dref[...]
            r = lax.rsqrt(_head_mean(v * v, mm) + EPS)
            gd = dy * gref[...]
            m = _head_mean(v * gd, mm)
            dst[...] = (r * gd - v * (r * r * r * m)).astype(BF16)
            gdst[...] += jnp.sum(dy * v * r, axis=0, keepdims=True)

    col = lambda j: pl.BlockSpec((T, WIDTH), lambda i, j=j: (i, j))
    vec = pl.BlockSpec((1, WIDTH), lambda i: (0, 0))
    row = pl.BlockSpec((T, WIDTH), lambda i: (i, 0))
    return pl.pallas_call(
        body,
        out_shape=(jax.ShapeDtypeStruct((S, WIDTH), BF16), jax.ShapeDtypeStruct((S, WIDTH), BF16),
                   jax.ShapeDtypeStruct((1, WIDTH), F32), jax.ShapeDtypeStruct((1, WIDTH), F32)),
        grid=(S // T,),
        in_specs=[col(0), col(1), vec, vec, row, row],
        out_specs=(row, row, vec, vec),
        compiler_params=_cp(("arbitrary",)),
        name=name,
    )(proj, proj, gq, gk, dqh, dkh)


DIAG_W = 1024
N_VARIANTS = N_LEFT * CHUNK // A_Q + 1


def diagonal_onehot():
    jj = np.arange(DIAG_W)
    diff = np.where(jj < A_WIN, jj, jj - DIAG_W)
    out = np.zeros((N_VARIANTS, DIAG_W, REL_TABLE), np.float32)
    for v in range(N_VARIANTS):
        rel = np.clip(A_Q * v - diff, -(CHUNK - 1), MAX_REL) + (CHUNK - 1)
        out[v, jj, rel] = 1.0
    return out.reshape(N_VARIANTS * DIAG_W, REL_TABLE)


def exact_dot(a, b, *, name):
    def body(a_ref, b_ref, o_ref):
        o_ref[...] = jnp.dot(a_ref[...], b_ref[...], precision=lax.Precision.HIGHEST, preferred_element_type=F32)

    return pl.pallas_call(body, out_shape=jax.ShapeDtypeStruct((a.shape[0], b.shape[1]), F32),
                          compiler_params=_cp(), name=name)(a, b)


def _band_valid(v):
    qc = (lax.broadcasted_iota(jnp.int32, (A_Q, A_WIN), 0) + A_Q * v) // CHUNK
    kc = lax.broadcasted_iota(jnp.int32, (A_Q, A_WIN), 1) // CHUNK
    return (kc <= qc) & (kc >= qc - N_LEFT)


def bias_expand(diag, *, name):
    def body(d_ref, o_ref):
        rows = jnp.broadcast_to(d_ref[0, 0], (A_Q, DIAG_W))
        skew = pltpu.roll(rows, 0, 1, stride=1, stride_axis=0)
        o_ref[0, 0] = jnp.where(_band_valid(pl.program_id(0)), skew[:, :A_WIN], NEG)

    return pl.pallas_call(
        body,
        out_shape=jax.ShapeDtypeStruct((N_VARIANTS, N_HEADS, A_Q, A_WIN), F32),
        grid=(N_VARIANTS, N_HEADS),
        in_specs=[pl.BlockSpec((1, 1, 1, DIAG_W), lambda v, h: (v, h, 0, 0))],
        out_specs=pl.BlockSpec((1, 1, A_Q, A_WIN), lambda v, h: (v, h, 0, 0)),
        compiler_params=_cp(("parallel", "parallel")),
        name=name,
    )(diag)


def relbias_reduce(dbias, *, name):
    def body(db_ref, o_ref):
        acc = None
        for a in range(A_Q // 8):
            x = jnp.concatenate([db_ref[0, 0, 8 * a:8 * a + 8, :], jnp.zeros((8, DIAG_W - A_WIN), F32)], axis=1)
            x = pltpu.roll(x, DIAG_W - 8 * a, 1) if a else x
            acc = x if acc is None else acc + x
        row = lax.broadcasted_iota(jnp.int32, (8, DIAG_W), 0)
        for b in range(3):
            acc = jnp.where((row >> b) & 1 == 1, pltpu.roll(acc, DIAG_W - (1 << b), 1), acc)
        o_ref[0, 0] = jnp.sum(acc, axis=0, keepdims=True)

    return pl.pallas_call(
        body,
        out_shape=jax.ShapeDtypeStruct((N_VARIANTS, N_HEADS, 1, DIAG_W), F32),
        grid=(N_VARIANTS, N_HEADS),
        in_specs=[pl.BlockSpec((1, 1, A_Q, A_WIN), lambda v, h: (v, h, 0, 0))],
        out_specs=pl.BlockSpec((1, 1, 1, DIAG_W), lambda v, h: (v, h, 0, 0)),
        compiler_params=_cp(("parallel", "parallel")),
        name=name,
    )(dbias)


def _a_window_start(qb):
    return pl.multiple_of(jnp.maximum(qb * A_Q - N_LEFT * CHUNK, 0), A_Q)


def attn_a_fwd(q, k, v, biasm, *, name):
    S = q.shape[0]
    nq = S // A_Q

    def body(q_ref, k_ref, v_ref, b_ref, o_ref):
        qb = pl.program_id(1)
        start = _a_window_start(qb)
        outs = []
        for h in range(A_FWD_HEADS):
            lanes = slice(h * HEAD_DIM, (h + 1) * HEAD_DIM)
            qh = q_ref[:, lanes]
            kw = k_ref[pl.ds(start, A_WIN), lanes]
            vw = v_ref[pl.ds(start, A_WIN), lanes]
            s = _dot(qh, kw, 1, 1) + b_ref[0, h]
            m = jnp.max(s, axis=-1, keepdims=True)
            e = jnp.exp(s - m)
            outs.append(_dot(e.astype(BF16), vw, 1, 0) * (1.0 / jnp.sum(e, axis=-1, keepdims=True)))
        o_ref[...] = jnp.concatenate(outs, axis=1).astype(BF16)

    qspec = pl.BlockSpec((A_Q, A_FWD_HEADS * HEAD_DIM), lambda hp, qb: (qb, hp))
    kvspec = pl.BlockSpec((S, A_FWD_HEADS * HEAD_DIM), lambda hp, qb: (0, hp))
    bspec = pl.BlockSpec((1, A_FWD_HEADS, A_Q, A_WIN), lambda hp, qb: (jnp.minimum(qb, N_VARIANTS - 1), hp, 0, 0))
    return pl.pallas_call(
        body,
        out_shape=jax.ShapeDtypeStruct((S, WIDTH), BF16),
        grid=(N_HEADS // A_FWD_HEADS, nq),
        in_specs=[qspec, kvspec, kvspec, bspec],
        out_specs=qspec,
        compiler_params=_cp(("parallel", "arbitrary")),
        name=name,
    )(q, k, v, biasm)


def attn_a_bwd(q, k, v, biasm, do, *, name):
    S = q.shape[0]
    nq = S // A_Q

    def body(q_ref, k_ref, v_ref, b_ref, do_ref, dq_ref, dk_ref, dv_ref, db_ref):
        qb = pl.program_id(1)
        start = _a_window_start(qb)

        @pl.when(qb == 0)
        def _():
            dk_ref[...] = jnp.zeros_like(dk_ref)
            dv_ref[...] = jnp.zeros_like(dv_ref)

        @pl.when(qb < N_VARIANTS)
        def _():
            db_ref[...] = jnp.zeros_like(db_ref)

        dqs = []
        for h in range(A_BWD_HEADS):
            lanes = slice(h * HEAD_DIM, (h + 1) * HEAD_DIM)
            qh = q_ref[:, lanes]
            doh = do_ref[:, lanes]
            kw = k_ref[pl.ds(start, A_WIN), lanes]
            vw = v_ref[pl.ds(start, A_WIN), lanes]
            s = _dot(qh, kw, 1, 1) + b_ref[0, h]
            m = jnp.max(s, axis=-1, keepdims=True)
            e = jnp.exp(s - m)
            p = e * (1.0 / jnp.sum(e, axis=-1, keepdims=True))
            dp = _dot(doh, vw, 1, 1)
            delta = jnp.sum(p * dp, axis=-1, keepdims=True)
            ds = p * (dp - delta)
            db_ref[0, h] += ds
            dsb = ds.astype(BF16)
            dqs.append(_dot(dsb, kw, 1, 0) * QK_SCALE)
            dk_ref[pl.ds(start, A_WIN), lanes] += _dot(dsb, qh, 0, 0)
            dv_ref[pl.ds(start, A_WIN), lanes] += _dot(p.astype(BF16), doh, 0, 0)
        dq_ref[...] = jnp.concatenate(dqs, axis=1)

    once = pl.Buffered(1)
    qspec = pl.BlockSpec((A_Q, A_BWD_HEADS * HEAD_DIM), lambda hp, qb: (qb, hp))
    kvspec = pl.BlockSpec((S, A_BWD_HEADS * HEAD_DIM), lambda hp, qb: (0, hp), pipeline_mode=once)
    bspec = pl.BlockSpec((1, A_BWD_HEADS, A_Q, A_WIN), lambda hp, qb: (jnp.minimum(qb, N_VARIANTS - 1), hp, 0, 0))
    return pl.pallas_call(
        body,
        out_shape=(jax.ShapeDtypeStruct((S, WIDTH), F32), jax.ShapeDtypeStruct((S, WIDTH), F32),
                   jax.ShapeDtypeStruct((S, WIDTH), F32), jax.ShapeDtypeStruct((N_VARIANTS, N_HEADS, A_Q, A_WIN), F32)),
        grid=(N_HEADS // A_BWD_HEADS, nq),
        in_specs=[qspec, kvspec, kvspec, bspec, qspec],
        out_specs=(qspec, kvspec, kvspec, bspec),
        compiler_params=_cp(("parallel", "arbitrary")),
        name=name,
    )(q, k, v, biasm, do)


def _tri(kind):
    j = lax.broadcasted_iota(jnp.int32, (ATT_Q, ATT_Q), 0)
    s = lax.broadcasted_iota(jnp.int32, (ATT_Q, ATT_Q), 1)
    if kind == "gt":
        m = j > s
    elif kind == "le":
        m = j <= s
    else:
        m = j < s
    return jnp.where(m, 1.0, 0.0).astype(BF16)


def _cum(v, tri):
    hi, lo = _split_hi_lo(v)
    return _dot(hi, tri, 1, 0) + _dot(lo, tri, 1, 0)


def _log_sigmoids(z, mask):
    t = jnp.log(1.0 + jnp.exp(-jnp.abs(z)))
    keep = -(jnp.maximum(z, 0.0) + t)
    take = jnp.minimum(z, 0.0) - t
    return (keep if mask is None else jnp.where(mask, keep, 0.0)), take


def _strictly_before():
    row = lax.broadcasted_iota(jnp.int32, (ATT_Q, ATT_Q), 0)
    col = lax.broadcasted_iota(jnp.int32, (ATT_Q, ATT_Q), 1)
    return col < row


EXIT_LOG = -104.0


def attn_b_fwd(q, k, v, *, gather=None, name):
    S = q.shape[0]
    nq = S // ATT_Q
    heads = B_FWD_HEADS
    plan, shards, fulls = gather if gather is not None else (None, [], [])
    ng = len(shards)

    def body(q_ref, k_ref, v_ref, *rest):
        hp = pl.program_id(0)
        qb = pl.program_id(1)
        o_ref, t_ref, n_ref = rest[2 * ng:2 * ng + 3]
        if plan is not None:
            comm = (rest[:ng], rest[2 * ng + 3:3 * ng + 3], rest[3 * ng + 3:])

            @pl.when(jnp.logical_and(hp == 0, qb == 0))
            def _():
                plan.start(*comm)

        tri = _tri("gt")

        def block(kb, carry, mask):
            ks = pl.multiple_of(kb * ATT_Q, ATT_Q)
            new = []
            for h in range(heads):
                lanes = slice(h * HEAD_DIM, (h + 1) * HEAD_DIM)
                c, acc = carry[h]
                z = _dot(q_ref[:, lanes], k_ref[pl.ds(ks, ATT_Q), lanes], 1, 1)
                keep, take = _log_sigmoids(z, mask)
                w = jnp.exp(take + (_cum(keep, tri) + c))
                if mask is not None:
                    w = jnp.where(mask, w, 0.0)
                acc = acc + _dot(w.astype(BF16), v_ref[pl.ds(ks, ATT_Q), lanes], 1, 0)
                c = c + jnp.sum(keep, axis=-1, keepdims=True)
                new.append((c, acc))
            return tuple(new)

        def cond(state):
            it, cmax, _ = state
            return jnp.logical_and(it <= qb, cmax >= EXIT_LOG)

        def step(state):
            it, _, carry = state
            carry = block(qb - it, carry, None)
            worst = carry[0][0]
            for h in range(1, heads):
                worst = jnp.maximum(worst, carry[h][0])
            return it + 1, jnp.max(worst), carry

        init = tuple((jnp.zeros((ATT_Q, 1), F32), jnp.zeros((ATT_Q, HEAD_DIM), F32)) for _ in range(heads))
        diag = block(qb, init, _strictly_before())
        visited, _, res = lax.while_loop(cond, step, (jnp.int32(1), jnp.float32(0.0), diag))
        o_ref[...] = jnp.concatenate([res[h][1] for h in range(heads)], axis=1).astype(BF16)
        t_ref[...] = jnp.concatenate([jnp.broadcast_to(res[h][0], (ATT_Q, HEAD_DIM)) for h in range(heads)], axis=1)
        n_ref[hp, qb] = visited.astype(F32)
        if plan is not None:
            @pl.when(jnp.logical_and(hp == N_HEADS // heads - 1, qb == nq - 1))
            def _():
                plan.finish(*comm)

    qspec = pl.BlockSpec((ATT_Q, heads * HEAD_DIM), lambda hp, qb: (qb, hp))
    kvspec = pl.BlockSpec((S, heads * HEAD_DIM), lambda hp, qb: (0, hp), pipeline_mode=pl.Buffered(1))
    outs = pl.pallas_call(
        body,
        out_shape=(jax.ShapeDtypeStruct((S, WIDTH), BF16), jax.ShapeDtypeStruct((S, WIDTH), F32),
                   jax.ShapeDtypeStruct((N_HEADS // heads, nq), F32))
        + tuple(jax.ShapeDtypeStruct(f.shape, f.dtype) for f in fulls),
        grid=(N_HEADS // heads, nq),
        in_specs=[qspec, kvspec, kvspec] + [ANY] * (2 * ng),
        out_specs=(qspec, qspec, pl.BlockSpec(memory_space=pltpu.SMEM)) + tuple([ANY] * ng),
        scratch_shapes=plan.scratch_shapes() if plan is not None else (),
        input_output_aliases={3 + ng + i: 3 + i for i in range(ng)},
        compiler_params=_cp(("arbitrary", "arbitrary")),
        name=name,
    )(q, k, v, *shards, *fulls)
    return outs if plan is None else (outs[0], outs[1], outs[2], list(outs[3:]))


def attn_b_bwd(q, k, v, tot, nblk, do, *, scatter=None, name):
    S = q.shape[0]
    nq = S // ATT_Q
    heads = B_BWD_HEADS
    plan, pairs16 = scatter if scatter is not None else (None, [])
    ns = len(pairs16)

    def body(q_ref, k_ref, v_ref, t_ref, n_ref, do_ref, *rest):
        hp = pl.program_id(0)
        qb = pl.program_id(1)
        dq_ref, dk_ref, dv_ref = rest[ns:ns + 3]
        if plan is not None:
            comm = (rest[:ns], rest[ns + 3:2 * ns + 3], rest[2 * ns + 3:])

            @pl.when(jnp.logical_and(hp == 0, qb == 0))
            def _():
                plan.start(*comm)

        visited = n_ref[(hp * heads) // B_FWD_HEADS, qb].astype(jnp.int32)
        first = jnp.clip(qb + 1 - visited, 0, qb + 1)
        tri_le = _tri("le")
        tri_lt = _tri("lt")

        @pl.when(qb == 0)
        def _():
            dk_ref[...] = jnp.zeros_like(dk_ref)
            dv_ref[...] = jnp.zeros_like(dv_ref)

        def block(kb, carry, mask):
            ks = pl.multiple_of(kb * ATT_Q, ATT_Q)
            new = []
            for h in range(heads):
                lanes = slice(h * HEAD_DIM, (h + 1) * HEAD_DIM)
                cl, cg, dq = carry[h]
                qh = q_ref[:, lanes]
                doh = do_ref[:, lanes]
                kh = k_ref[pl.ds(ks, ATT_Q), lanes]
                vh = v_ref[pl.ds(ks, ATT_Q), lanes]
                totl = t_ref[:, h * HEAD_DIM:h * HEAD_DIM + 1]
                z = _dot(qh, kh, 1, 1)
                keep, take = _log_sigmoids(z, mask)
                sig = jnp.exp(take)
                w = sig * jnp.exp((totl - cl) - _cum(keep, tri_le))
                if mask is not None:
                    w = jnp.where(mask, w, 0.0)
                g = w * _dot(doh, vh, 1, 1)
                G = _dot(g.astype(BF16), tri_lt, 1, 0) + cg
                dz = g * (1.0 - sig) - sig * G
                if mask is not None:
                    dz = jnp.where(mask, dz, 0.0)
                dz = dz.astype(BF16)
                dq = dq + _dot(dz, kh, 1, 0)
                dk_ref[pl.ds(ks, ATT_Q), lanes] += _dot(dz, qh, 0, 0)
                dv_ref[pl.ds(ks, ATT_Q), lanes] += _dot(w.astype(BF16), doh, 0, 0)
                cl = cl + jnp.sum(keep, axis=-1, keepdims=True)
                cg = cg + jnp.sum(g, axis=-1, keepdims=True)
                new.append((cl, cg, dq))
            return tuple(new)

        init = tuple((jnp.zeros((ATT_Q, 1), F32), jnp.zeros((ATT_Q, 1), F32), jnp.zeros((ATT_Q, HEAD_DIM), F32))
                     for _ in range(heads))
        res = lax.fori_loop(jnp.minimum(first, qb), qb, lambda kb, carry: block(kb, carry, None), init)
        res = block(qb, res, _strictly_before())
        dq_ref[...] = (jnp.concatenate([res[h][2] for h in range(heads)], axis=1) * QK_SCALE).astype(BF16)
        if plan is not None:
            @pl.when(jnp.logical_and(hp == N_HEADS // heads - 1, qb == nq - 1))
            def _():
                plan.finish(*comm)

    qspec = pl.BlockSpec((ATT_Q, heads * HEAD_DIM), lambda hp, qb: (qb, hp))
    kvspec = pl.BlockSpec((S, heads * HEAD_DIM), lambda hp, qb: (0, hp), pipeline_mode=pl.Buffered(1))
    outs = pl.pallas_call(
        body,
        out_shape=(jax.ShapeDtypeStruct((S, WIDTH), BF16), jax.ShapeDtypeStruct((S, WIDTH), F32),
                   jax.ShapeDtypeStruct((S, WIDTH), F32)) + (tuple(plan.out_shapes()) if plan is not None else ()),
        grid=(N_HEADS // heads, nq),
        in_specs=[qspec, kvspec, kvspec, qspec, pl.BlockSpec(memory_space=pltpu.SMEM), qspec] + [ANY] * ns,
        out_specs=(qspec, kvspec, kvspec) + tuple([ANY] * ns),
        scratch_shapes=plan.scratch_shapes() if plan is not None else (),
        compiler_params=_cp(("arbitrary", "arbitrary")),
        name=name,
    )(q, k, v, tot, nblk, do, *pairs16)
    return outs if plan is None else (outs[0], outs[1], outs[2], list(outs[3:]))


U_COLBLK = 6


def _pool_counts(t0, rows):
    t = t0 + lax.broadcasted_iota(jnp.int32, (rows, WIDTH), 0)
    lane_grp = lax.broadcasted_iota(jnp.int32, (rows, WIDTH), 1) // GROUP_DIM
    w2, w4, w8, w16 = POOL_WINDOWS
    win = jnp.where(lane_grp == 0, w2, jnp.where(lane_grp == 1, w4, jnp.where(lane_grp == 2, w8, w16)))
    cnt = jnp.minimum(t + 1, win)
    return 1.0 / cnt.astype(F32), lane_grp


def _window_sums(ext, shift_fn):
    s2 = ext + shift_fn(ext, 1)
    s4 = s2 + shift_fn(s2, 2)
    s8 = s4 + shift_fn(s4, 4)
    s16 = s8 + shift_fn(s8, 8)
    return s2, s4, s8, s16


def _select_group(lane_grp, s2, s4, s8, s16):
    return jnp.where(lane_grp == 0, s2, jnp.where(lane_grp == 1, s4, jnp.where(lane_grp == 2, s8, s16)))


def _pooled_tile(u_ref, h_ref, i, T):
    halo = jnp.where(i > 0, h_ref[...], 0.0)
    ext = jnp.concatenate([halo, u_ref[...]], axis=0)
    n = T + HALO
    sums = _window_sums(ext, lambda v, k: pltpu.roll(v, k, 0))
    inv, lane_grp = _pool_counts(i * T - HALO, n)
    pooled = _select_group(lane_grp, *sums) * inv - ext
    return pooled[HALO:, :]


def pool_fwd(proj, w_pool, scale, *, name):
    S = proj.shape[0]
    T = _tile(S)
    hb = T // HALO

    def body(u_ref, h_ref, w_ref, s_ref, o_ref):
        i = pl.program_id(0)
        pooled = _pooled_tile(u_ref, h_ref, i, T).astype(BF16)
        outs = [_dot(pooled[:, g * GROUP_DIM:(g + 1) * GROUP_DIM], w_ref[g], 1, 0) for g in range(4)]
        o_ref[...] = (jnp.concatenate(outs, axis=1) * s_ref[...]).astype(BF16)

    return pl.pallas_call(
        body,
        out_shape=jax.ShapeDtypeStruct((S, WIDTH), BF16),
        grid=(S // T,),
        in_specs=[pl.BlockSpec((T, WIDTH), lambda i: (i, U_COLBLK)),
                  pl.BlockSpec((HALO, WIDTH), lambda i: (jnp.maximum(i * hb - 1, 0), U_COLBLK)),
                  pl.BlockSpec((4, GROUP_DIM, GROUP_DIM), lambda i: (0, 0, 0)),
                  pl.BlockSpec((1, WIDTH), lambda i: (0, 0))],
        out_specs=pl.BlockSpec((T, WIDTH), lambda i: (i, 0)),
        compiler_params=_cp(("parallel",)),
        name=name,
    )(proj, proj, w_pool, scale)


def pool_bwd(proj, w_pool, scale, do, *, name):
    S = proj.shape[0]
    T = _tile(S)
    hb = T // HALO
    nt = S // T

    def body(u_ref, h_ref, w_ref, s_ref, do_ref, dof_ref, du_ref, dw_ref, ds_ref):
        i = pl.program_id(0)

        @pl.when(i == 0)
        def _():
            dw_ref[...] = jnp.zeros_like(dw_ref)
            ds_ref[...] = jnp.zeros_like(ds_ref)

        pooled = _pooled_tile(u_ref, h_ref, i, T).astype(BF16)
        dov = do_ref[...].astype(F32)
        fut = jnp.where(i < nt - 1, dof_ref[...].astype(F32), 0.0)
        dmix = (jnp.concatenate([dov, fut], axis=0) * s_ref[...]).astype(BF16)
        mixed, dpool = [], []
        for g in range(4):
            lanes = slice(g * GROUP_DIM, (g + 1) * GROUP_DIM)
            mixed.append(_dot(pooled[:, lanes], w_ref[g], 1, 0))
            dw_ref[g] += _dot(pooled[:, lanes], dmix[:T, lanes], 0, 0)
            dpool.append(_dot(dmix[:, lanes], w_ref[g], 1, 1))
        ds_ref[...] += jnp.sum(dov * jnp.concatenate(mixed, axis=1), axis=0, keepdims=True)
        dp = jnp.concatenate(dpool, axis=1)
        n = T + HALO
        inv, lane_grp = _pool_counts(i * T, n)
        sums = _window_sums(dp * inv, lambda v, k: pltpu.roll(v, n - k, 0))
        du = _select_group(lane_grp, *sums) - dp
        du_ref[...] = du[:T, :].astype(BF16)

    row = pl.BlockSpec((T, WIDTH), lambda i: (i, 0))
    return pl.pallas_call(
        body,
        out_shape=(jax.ShapeDtypeStruct((S, WIDTH), BF16), jax.ShapeDtypeStruct((4, GROUP_DIM, GROUP_DIM), F32),
                   jax.ShapeDtypeStruct((1, WIDTH), F32)),
        grid=(nt,),
        in_specs=[pl.BlockSpec((T, WIDTH), lambda i: (i, U_COLBLK)),
                  pl.BlockSpec((HALO, WIDTH), lambda i: (jnp.maximum(i * hb - 1, 0), U_COLBLK)),
                  pl.BlockSpec((4, GROUP_DIM, GROUP_DIM), lambda i: (0, 0, 0)),
                  pl.BlockSpec((1, WIDTH), lambda i: (0, 0)),
                  row,
                  pl.BlockSpec((HALO, WIDTH), lambda i: (jnp.minimum((i + 1) * hb, S // HALO - 1), 0))],
        out_specs=(row, pl.BlockSpec((4, GROUP_DIM, GROUP_DIM), lambda i: (0, 0, 0)),
                   pl.BlockSpec((1, WIDTH), lambda i: (0, 0))),
        compiler_params=_cp(("arbitrary",)),
        name=name,
    )(proj, proj, w_pool, scale, do, do)


GATE_BLK0 = GATE_COL0 // WIDTH


def merge_fwd(oa, ob, oc, proj, b_gate, wa, wb, wc, *, name):
    S = oa.shape[0]
    T = _tile(S)

    def body(oa_ref, ob_ref, oc_ref, ga, gb, gc, ba, bb, bc, wa_ref, wb_ref, wc_ref, m_ref):
        acc = None
        for o_ref, g_ref, b_ref, w_ref in ((oa_ref, ga, ba, wa_ref), (ob_ref, gb, bb, wb_ref), (oc_ref, gc, bc, wc_ref)):
            y = _dot(o_ref[...], w_ref[...], 1, 0)
            t = jax.nn.sigmoid(g_ref[...] + b_ref[...]) * y
            acc = t if acc is None else acc + t
        m_ref[...] = acc.astype(BF16)

    row = pl.BlockSpec((T, WIDTH), lambda i, n: (i, 0))
    gate = lambda b: pl.BlockSpec((T, WIDTH), lambda i, n, b=b: (i, GATE_BLK0 + 2 * b + n))
    bias = lambda b: pl.BlockSpec((1, WIDTH), lambda i, n, b=b: (0, 2 * b + n))
    wspec = pl.BlockSpec((WIDTH, WIDTH), lambda i, n: (0, n))
    return pl.pallas_call(
        body,
        out_shape=jax.ShapeDtypeStruct((S, D_MODEL), BF16),
        grid=(S // T, 2),
        in_specs=[row, row, row, gate(0), gate(1), gate(2), bias(0), bias(1), bias(2), wspec, wspec, wspec],
        out_specs=pl.BlockSpec((T, WIDTH), lambda i, n: (i, n)),
        compiler_params=_cp(("parallel", "parallel")),
        name=name,
    )(oa, ob, oc, proj, proj, proj, b_gate, b_gate, b_gate, wa, wb, wc)


def merge_bwd(dm, oa, ob, oc, proj, b_gate, wa, wb, wc, *, name):
    S = oa.shape[0]
    T = _tile(S)

    def body(dm_ref, oa_ref, ob_ref, oc_ref, ga, gb, gc, ba, bb, bc, wa_ref, wb_ref, wc_ref,
             ta, tb, tc, dga, dgb, dgc, dba, dbb, dbc):
        i = pl.program_id(1)
        dmv = dm_ref[...].astype(F32)
        for o_ref, g_ref, b_ref, w_ref, t_ref, dg_ref, db_ref in (
                (oa_ref, ga, ba, wa_ref, ta, dga, dba), (ob_ref, gb, bb, wb_ref, tb, dgb, dbb),
                (oc_ref, gc, bc, wc_ref, tc, dgc, dbc)):
            y = _dot(o_ref[...], w_ref[...], 1, 0)
            gate = jax.nn.sigmoid(g_ref[...] + b_ref[...])
            t_ref[...] = (gate * dmv).astype(BF16)
            dgl = dmv * y * gate * (1.0 - gate)
            dg_ref[...] = dgl.astype(BF16)

            @pl.when(i == 0)
            def _():
                db_ref[...] = jnp.zeros_like(db_ref)

            db_ref[...] += jnp.sum(dgl, axis=0, keepdims=True)

    row = pl.BlockSpec((T, WIDTH), lambda n, i: (i, 0))
    half = pl.BlockSpec((T, WIDTH), lambda n, i: (i, n))
    gate = lambda b: pl.BlockSpec((T, WIDTH), lambda n, i, b=b: (i, GATE_BLK0 + 2 * b + n))
    bias = lambda b: pl.BlockSpec((1, WIDTH), lambda n, i, b=b: (0, 2 * b + n))
    wspec = pl.BlockSpec((WIDTH, WIDTH), lambda n, i: (0, n))
    bvec = pl.BlockSpec((1, WIDTH), lambda n, i: (0, n))
    act = jax.ShapeDtypeStruct((S, D_MODEL), BF16)
    vec = jax.ShapeDtypeStruct((1, D_MODEL), F32)
    return pl.pallas_call(
        body,
        out_shape=(act, act, act, act, act, act, vec, vec, vec),
        grid=(2, S // T),
        in_specs=[half, row, row, row, gate(0), gate(1), gate(2), bias(0), bias(1), bias(2), wspec, wspec, wspec],
        out_specs=(half, half, half, half, half, half, bvec, bvec, bvec),
        compiler_params=_cp(("parallel", "arbitrary")),
        name=name,
    )(dm, oa, ob, oc, proj, proj, proj, b_gate, b_gate, b_gate, wa, wb, wc)


FF_T = 256
FF_BLKS = D_FF // FF_T


def _silu_parts(x):
    s = jax.nn.sigmoid(x)
    return x * s, s


def _conv3(ext, w_ref, b_ref):
    taps = (pltpu.roll(ext, 2, 0), pltpu.roll(ext, 1, 0), ext)
    return b_ref[...] + w_ref[0:1, :] * taps[0] + w_ref[1:2, :] * taps[1] + w_ref[2:3, :] * taps[2], taps


def conv_glu_fwd(u, conv_w, conv_b, *, name):
    S = u.shape[0]
    T = _tile(S)
    hb = T // CONV_HALO

    def body(ug, ugh, uv, uvh, wg, wv, bg, bv, a_ref):
        i = pl.program_id(1)
        cs = []
        for m_ref, h_ref, w_ref, b_ref in ((ug, ugh, wg, bg), (uv, uvh, wv, bv)):
            halo = jnp.where(i > 0, h_ref[...], 0.0)
            ext = jnp.concatenate([halo, m_ref[...]], axis=0)
            cs.append(_conv3(ext, w_ref, b_ref)[0][CONV_HALO:, :])
        act, _ = _silu_parts(cs[0])
        a_ref[...] = (act * cs[1]).astype(BF16)

    main = lambda o: pl.BlockSpec((T, FF_T), lambda c, i, o=o: (i, c + o))
    halo = lambda o: pl.BlockSpec((CONV_HALO, FF_T), lambda c, i, o=o: (jnp.maximum(i * hb - 1, 0), c + o))
    wsp = lambda o: pl.BlockSpec((3, FF_T), lambda c, i, o=o: (0, c + o))
    bsp = lambda o: pl.BlockSpec((1, FF_T), lambda c, i, o=o: (0, c + o))
    return pl.pallas_call(
        body,
        out_shape=jax.ShapeDtypeStruct((S, D_FF), BF16),
        grid=(FF_BLKS, S // T),
        in_specs=[main(0), halo(0), main(FF_BLKS), halo(FF_BLKS), wsp(0), wsp(FF_BLKS), bsp(0), bsp(FF_BLKS)],
        out_specs=pl.BlockSpec((T, FF_T), lambda c, i: (i, c)),
        compiler_params=_cp(("parallel", "parallel")),
        name=name,
    )(u, u, u, u, conv_w, conv_w, conv_b, conv_b)


def conv_glu_bwd(u, conv_w, conv_b, da, *, name):
    S = u.shape[0]
    T = _tile(S)
    hb = T // CONV_HALO
    nt = S // T
    n = T + 2 * CONV_HALO

    def body(ug, ugp, ugf, uv, uvp, uvf, wg, wv, bg, bv, da_ref, daf_ref,
             dug, duv, dwg, dwv, dbg, dbv):
        i = pl.program_id(1)
        first, last = i == 0, i == nt - 1
        taps, cs = [], []
        for m_ref, p_ref, f_ref, w_ref, b_ref in ((ug, ugp, ugf, wg, bg), (uv, uvp, uvf, wv, bv)):
            ext = jnp.concatenate([jnp.where(first, 0.0, p_ref[...]), m_ref[...], jnp.where(last, 0.0, f_ref[...])], axis=0)
            c, tp = _conv3(ext, w_ref, b_ref)
            cs.append(c)
            taps.append(tp)
        dae = jnp.concatenate([jnp.zeros((CONV_HALO, FF_T), F32), da_ref[...].astype(F32),
                               jnp.where(last, 0.0, daf_ref[...].astype(F32))], axis=0)
        act, sg = _silu_parts(cs[0])
        dcs = (dae * cs[1] * (sg * (1.0 + cs[0] * (1.0 - sg))), dae * act)
        main = slice(CONV_HALO, CONV_HALO + T)
        for tp, dc, w_ref, du_ref, dw_ref, db_ref in ((taps[0], dcs[0], wg, dug, dwg, dbg),
                                                      (taps[1], dcs[1], wv, duv, dwv, dbv)):
            du = (w_ref[2:3, :] * dc + w_ref[1:2, :] * pltpu.roll(dc, n - 1, 0) + w_ref[0:1, :] * pltpu.roll(dc, n - 2, 0))
            du_ref[...] = du[main, :].astype(BF16)
            dcm = dc[main, :]
            rows = [jnp.sum(dcm * tp[j][main, :], axis=0, keepdims=True) for j in range(3)]

            @pl.when(first)
            def _():
                dw_ref[...] = jnp.zeros_like(dw_ref)
                db_ref[...] = jnp.zeros_like(db_ref)

            dw_ref[...] += jnp.concatenate(rows, axis=0)
            db_ref[...] += jnp.sum(dcm, axis=0, keepdims=True)

    main = lambda o: pl.BlockSpec((T, FF_T), lambda c, i, o=o: (i, c + o))
    past = lambda o: pl.BlockSpec((CONV_HALO, FF_T), lambda c, i, o=o: (jnp.maximum(i * hb - 1, 0), c + o))
    fut = lambda o: pl.BlockSpec((CONV_HALO, FF_T), lambda c, i, o=o: (jnp.minimum((i + 1) * hb, S // CONV_HALO - 1), c + o))
    wsp = lambda o: pl.BlockSpec((3, FF_T), lambda c, i, o=o: (0, c + o))
    bsp = lambda o: pl.BlockSpec((1, FF_T), lambda c, i, o=o: (0, c + o))
    return pl.pallas_call(
        body,
        out_shape=(jax.ShapeDtypeStruct((S, D_FF), BF16), jax.ShapeDtypeStruct((S, D_FF), BF16),
                   jax.ShapeDtypeStruct((3, D_FF), F32), jax.ShapeDtypeStruct((3, D_FF), F32),
                   jax.ShapeDtypeStruct((1, D_FF), F32), jax.ShapeDtypeStruct((1, D_FF), F32)),
        grid=(FF_BLKS, nt),
        in_specs=[main(0), past(0), fut(0), main(FF_BLKS), past(FF_BLKS), fut(FF_BLKS),
                  wsp(0), wsp(FF_BLKS), bsp(0), bsp(FF_BLKS), main(0), fut(0)],
        out_specs=(main(0), main(0), wsp(0), wsp(0), bsp(0), bsp(0)),
        compiler_params=_cp(("parallel", "arbitrary")),
        name=name,
    )(u, u, u, u, u, u, conv_w, conv_w, conv_b, conv_b, da, da)


def loss_head(y, target, *, name):
    S, D = y.shape
    T = _tile(S)

    def body(y_ref, t_ref, dy_ref, l_ref):
        i = pl.program_id(0)
        err = y_ref[...] - t_ref[...]
        dy_ref[...] = err * (1.0 / D)

        @pl.when(i == 0)
        def _():
            l_ref[...] = jnp.zeros_like(l_ref)

        l_ref[...] += 0.5 * jnp.sum(jnp.mean(err * err, axis=-1, keepdims=True))

    row = pl.BlockSpec((T, D), lambda i: (i, 0))
    return pl.pallas_call(
        body,
        out_shape=(jax.ShapeDtypeStruct((S, D), F32), jax.ShapeDtypeStruct((8, 128), F32)),
        grid=(S // T,),
        in_specs=[row, row],
        out_specs=(row, pl.BlockSpec((8, 128), lambda i: (0, 0))),
        compiler_params=_cp(("arbitrary",)),
        name=name,
    )(y, target)


ELEMS_PER_BLOCK = 512 * 1024


def _rows_tile(rows, cols):
    if rows * cols <= ELEMS_PER_BLOCK or rows % 8:
        return rows
    best = 8
    for tr in range(8, rows + 1, 8):
        if rows % tr == 0 and tr * cols <= ELEMS_PER_BLOCK:
            best = tr
    return best


def _adamw_math(g, w_ref, m_ref, v_ref, g_out, d_out, m_out, v_out):
    mn = ADAM_B1 * m_ref[...] + (1.0 - ADAM_B1) * g
    vn = ADAM_B2 * v_ref[...] + (1.0 - ADAM_B2) * (g * g)
    m_hat = mn / (1.0 - ADAM_B1 ** ADAM_STEP)
    v_hat = vn / (1.0 - ADAM_B2 ** ADAM_STEP)
    g_out[...] = g
    d_out[...] = -ADAM_LR * (m_hat / (jnp.sqrt(v_hat) + ADAM_EPS) + ADAM_WD * w_ref[...])
    m_out[...] = mn
    v_out[...] = vn


def adamw(w, m, v, g, *, name):
    rows, cols = w.shape
    tr = _rows_tile(rows, cols)

    def body(w_ref, m_ref, v_ref, g_ref, g_out, d_out, m_out, v_out):
        _adamw_math(g_ref[...], w_ref, m_ref, v_ref, g_out, d_out, m_out, v_out)

    spec = pl.BlockSpec((tr, cols), lambda i: (i, 0))
    shp = jax.ShapeDtypeStruct((rows, cols), F32)
    return pl.pallas_call(
        body,
        out_shape=(shp, shp, shp, shp),
        grid=(rows // tr,),
        in_specs=[spec] * 4,
        out_specs=(spec, spec, spec, spec),
        compiler_params=_cp(("parallel",)),
        name=name,
    )(w, m, v, g)


ANY = pl.BlockSpec(memory_space=pl.ANY)
STAGE_BYTES = 2 * 1024 * 1024


def _mesh_pos():
    return lax.axis_index("x"), lax.axis_index("y"), lax.axis_index("c")


def _chip_peers(x, y):
    return [(1 - x, y), (x, 1 - y), (1 - x, 1 - y)]


def _all_peers(x, y, c):
    return [((1 - x) if (r >> 2) & 1 else x, (1 - y) if (r >> 1) & 1 else y, (1 - c) if r & 1 else c)
            for r in range(1, 8)]


class LayerGather:
    def __init__(self, shards, axes, layer):
        self.nt = len(shards)
        self.axes = list(axes)
        self.layer = layer
        self.shapes = [s.shape for s in shards]
        self.dtypes = [s.dtype for s in shards]
        self.sizes = [s.shape[a] for s, a in zip(shards, axes)]
        self.split = [s.shape[1] % 32 == 0 for s in shards]
        self.half_rows = [s.shape[1] // 2 if sp else s.shape[1] for s, sp in zip(shards, self.split)]
        self.chunk_rows = []
        for s in shards:
            rt = s.shape[1]
            while rt % 32 == 0 and rt * s.shape[2] * s.dtype.itemsize > STAGE_BYTES:
                rt //= 2
            self.chunk_rows.append(rt)

    def out_shapes(self):
        out = []
        for shp, a, sz, dt in zip(self.shapes, self.axes, self.sizes, self.dtypes):
            shp = list(shp)
            shp[a] = 4 * sz
            out.append(jax.ShapeDtypeStruct(tuple(shp), dt))
        return out

    def scratch_shapes(self):
        return ([pltpu.VMEM((1, rt, shp[2]), dt) for shp, rt, dt in zip(self.shapes, self.chunk_rows, self.dtypes)]
                + [pltpu.SemaphoreType.DMA((2 * self.nt,))] + [pltpu.SemaphoreType.DMA((3 * self.nt,)) for _ in range(4)])

    def _views(self, ins, outs, scratch):
        nt = self.nt
        stage, stage_sems = scratch[:nt], scratch[nt]
        ici_send, ici_recv, d2d_send, d2d_recv = scratch[nt + 1:]
        x, y, c = _mesh_pos()
        mine = 2 * x + y
        peers = _chip_peers(x, y)
        layer = pl.ds(self.layer, 1)

        def rows(t, half, r0=0, n=None):
            hr = self.half_rows[t]
            if n is None:
                return pl.ds(pl.multiple_of(half * hr, 16), hr) if self.split[t] else pl.ds(0, hr)
            return pl.ds(r0, n)

        def placed(t, blk, row_sel, row_len):
            sz = self.sizes[t]
            if self.axes[t] == 2:
                return outs[t].at[layer, row_sel, pl.ds(pl.multiple_of(blk * sz, 128), sz)]
            return outs[t].at[layer, pl.ds(pl.multiple_of(blk * sz, 16) + row_sel.start, row_len), :]

        def ici(t, k, blk):
            px, py = peers[k]
            sel = rows(t, c)
            return pltpu.make_async_remote_copy(
                src_ref=ins[t].at[layer, sel, :], dst_ref=placed(t, blk, sel, self.half_rows[t]),
                send_sem=ici_send.at[3 * t + k], recv_sem=ici_recv.at[3 * t + k],
                device_id=(px, py, c), device_id_type=MESH_T)

        def d2d(t, k, half):
            px, py = peers[k]
            piece = placed(t, 2 * px + py, rows(t, half), self.half_rows[t])
            return pltpu.make_async_remote_copy(
                src_ref=piece, dst_ref=piece, send_sem=d2d_send.at[3 * t + k], recv_sem=d2d_recv.at[3 * t + k],
                device_id=(x, y, 1 - c), device_id_type=MESH_T)

        def own_chunk(t, r0):
            rt = self.chunk_rows[t]
            sel = pl.ds(r0, rt)
            return ins[t].at[layer, sel, :], placed(t, mine, sel, rt), stage[t], stage_sems

        return c, mine, peers, ici, d2d, own_chunk

    def start(self, ins, outs, scratch):
        c, mine, peers, ici, d2d, own_chunk = self._views(ins, outs, scratch)
        for t in range(self.nt):
            for k in range(3):
                ici(t, k, mine).start()
        starts = [list(range(0, self.shapes[t][1], self.chunk_rows[t])) for t in range(self.nt)]
        for r in range(max(len(s) for s in starts)):
            active = [(t, *own_chunk(t, starts[t][r])) for t in range(self.nt) if r < len(starts[t])]
            loads = [pltpu.make_async_copy(src, buf, sems.at[2 * t]) for t, src, dst, buf, sems in active]
            for cp in loads:
                cp.start()
            for cp in loads:
                cp.wait()
            stores = [pltpu.make_async_copy(buf, dst, sems.at[2 * t + 1]) for t, src, dst, buf, sems in active]
            for cp in stores:
                cp.start()
            for cp in stores:
                cp.wait()

    def finish(self, ins, outs, scratch):
        c, mine, peers, ici, d2d, own_chunk = self._views(ins, outs, scratch)
        for t in range(self.nt):
            for k, (px, py) in enumerate(peers):
                ici(t, k, 2 * px + py).wait_recv()
                if self.split[t]:
                    d2d(t, k, c).start()
        for t in range(self.nt):
            for k in range(3):
                if self.split[t]:
                    d2d(t, k, 1 - c).wait_recv()
        for t in range(self.nt):
            for k in range(3):
                ici(t, k, mine).wait_send()
                if self.split[t]:
                    d2d(t, k, c).wait_send()


def all_gather_layer(shards, axes, layer, *, name):
    plan = LayerGather(shards, axes, layer)
    nt = plan.nt

    def body(*refs):
        ins, outs, scratch = refs[:nt], refs[nt:2 * nt], refs[2 * nt:]
        plan.start(ins, outs, scratch)
        plan.finish(ins, outs, scratch)

    return pl.pallas_call(
        body,
        out_shape=tuple(plan.out_shapes()),
        in_specs=[ANY] * nt,
        out_specs=tuple([ANY] * nt),
        scratch_shapes=plan.scratch_shapes(),
        name=name,
    )(*shards)


class HalfLayout:
    def __init__(self, shape, axis):
        self.R, self.C = shape
        self.axis = axis
        if axis == 1:
            self.hr, self.pw = self.R // 2, self.C // 4
            self.half_shape = (self.hr, self.C)
        else:
            self.hr, self.pw = self.R // 8, self.C
            self.half_shape = (4 * self.hr, self.C)
        self.tr = _rows_tile(self.hr, self.pw)
        self.nr = self.hr // self.tr

    def in_grad(self, ref, blk, half):
        if self.axis == 1:
            return ref.at[pl.ds(pl.multiple_of(half * self.hr, 16), self.hr), pl.ds(pl.multiple_of(blk * self.pw, 128), self.pw)]
        return ref.at[pl.ds(pl.multiple_of((2 * blk + half) * self.hr, 16), self.hr), :]

    def in_half(self, ref, blk):
        if self.axis == 1:
            return ref.at[:, pl.ds(pl.multiple_of(blk * self.pw, 128), self.pw)]
        return ref.at[pl.ds(pl.multiple_of(blk * self.hr, 16), self.hr), :]

    def grad_spec(self):
        if self.axis == 1:
            return pl.BlockSpec((self.tr, self.pw), lambda j, i, s: (s[0] * self.nr + i, j))
        return pl.BlockSpec((self.tr, self.pw), lambda j, i, s: ((2 * j + s[0]) * self.nr + i, 0))

    def half_spec(self):
        if self.axis == 1:
            return pl.BlockSpec((self.tr, self.pw), lambda j, i, s: (i, j))
        return pl.BlockSpec((self.tr, self.pw), lambda j, i, s: (j * self.nr + i, 0))


def half_exchange(grads, layouts, *, name):
    nt = len(grads)
    pieces = [(t, j) for t in range(nt) for j in (range(4) if layouts[t].axis == 0 else range(1))]

    def body(*refs):
        ins, outs = refs[:nt], refs[nt:2 * nt]
        send_sems, recv_sems = refs[2 * nt:]
        x, y, c = _mesh_pos()
        cps = []
        for n, (t, j) in enumerate(pieces):
            lay = layouts[t]
            if lay.axis == 1:
                src = ins[t].at[pl.ds(pl.multiple_of((1 - c) * lay.hr, 16), lay.hr), :]
                dst = outs[t]
            else:
                src = lay.in_grad(ins[t], j, 1 - c)
                dst = lay.in_half(outs[t], j)
            cp = pltpu.make_async_remote_copy(src_ref=src, dst_ref=dst, send_sem=send_sems.at[n], recv_sem=recv_sems.at[n],
                                              device_id=(x, y, 1 - c), device_id_type=MESH_T)
            cp.start()
            cps.append(cp)
        for cp in cps:
            cp.wait_recv()
        for cp in cps:
            cp.wait_send()

    return pl.pallas_call(
        body,
        out_shape=tuple(jax.ShapeDtypeStruct(lay.half_shape, F32) for lay in layouts),
        in_specs=[ANY] * nt,
        out_specs=tuple([ANY] * nt),
        scratch_shapes=[pltpu.SemaphoreType.DMA((len(pieces),)), pltpu.SemaphoreType.DMA((len(pieces),))],
        name=name,
    )(*grads)


def pair_sum(grad, other, lay, core, *, name):
    def body(c_ref, g_ref, o_ref, s32_ref, s16_ref):
        s = g_ref[...] + o_ref[...]
        s32_ref[...] = s
        s16_ref[...] = s.astype(BF16)

    return pl.pallas_call(
        body,
        out_shape=(jax.ShapeDtypeStruct(lay.half_shape, F32), jax.ShapeDtypeStruct(lay.half_shape, BF16)),
        grid_spec=pltpu.PrefetchScalarGridSpec(
            num_scalar_prefetch=1, grid=(4, lay.nr),
            in_specs=[lay.grad_spec(), lay.half_spec()],
            out_specs=(lay.half_spec(), lay.half_spec())),
        compiler_params=_cp(("parallel", "parallel")),
        name=name,
    )(core, grad, other)


class BlockScatter:
    def __init__(self, layouts):
        self.layouts = layouts
        self.nt = len(layouts)

    def out_shapes(self):
        return [jax.ShapeDtypeStruct((3, lay.hr, lay.pw), BF16) for lay in self.layouts]

    def scratch_shapes(self):
        return [pltpu.SemaphoreType.DMA((3 * self.nt,)), pltpu.SemaphoreType.DMA((3 * self.nt,))]

    def _copies(self, pairs16, recv, scratch):
        send_sems, recv_sems = scratch
        x, y, c = _mesh_pos()
        return [pltpu.make_async_remote_copy(
            src_ref=lay.in_half(pairs16[t], 2 * px + py), dst_ref=recv[t].at[k],
            send_sem=send_sems.at[3 * t + k], recv_sem=recv_sems.at[3 * t + k],
            device_id=(px, py, c), device_id_type=MESH_T)
            for t, lay in enumerate(self.layouts) for k, (px, py) in enumerate(_chip_peers(x, y))]

    def start(self, pairs16, recv, scratch):
        for cp in self._copies(pairs16, recv, scratch):
            cp.start()

    def finish(self, pairs16, recv, scratch):
        copies = self._copies(pairs16, recv, scratch)
        for cp in copies:
            cp.wait_recv()
        for cp in copies:
            cp.wait_send()


def gather_small(small, *, name):
    def body(small_in, small_out, ssend, srecv):
        x, y, c = _mesh_pos()
        me = 4 * x + 2 * y + c
        sends, recvs = [], []
        for r, (px, py, pc) in enumerate(_all_peers(x, y, c)):
            def mk(slot, r=r, px=px, py=py, pc=pc):
                return pltpu.make_async_remote_copy(
                    src_ref=small_in, dst_ref=small_out.at[slot], send_sem=ssend.at[r], recv_sem=srecv.at[r],
                    device_id=(px, py, pc), device_id_type=MESH_T)
            snd = mk(me)
            snd.start()
            sends.append(snd)
            recvs.append(mk(4 * px + 2 * py + pc))
        for r in recvs:
            r.wait_recv()
        for s in sends:
            s.wait_send()

    return pl.pallas_call(
        body,
        out_shape=jax.ShapeDtypeStruct((8,) + small.shape, F32),
        in_specs=[ANY],
        out_specs=ANY,
        scratch_shapes=[pltpu.SemaphoreType.DMA((7,)), pltpu.SemaphoreType.DMA((7,))],
        name=name,
    )(small)


def sum_chips(pair32, recv, lay, chip, *, name):
    def body(j_ref, p_ref, r_ref, s_ref):
        acc = p_ref[...]
        for k in range(3):
            acc = acc + r_ref[k].astype(F32)
        s_ref[...] = acc

    if lay.axis == 1:
        own = pl.BlockSpec((lay.tr, lay.pw), lambda i, j: (i, j[0]))
    else:
        own = pl.BlockSpec((lay.tr, lay.pw), lambda i, j: (j[0] * lay.nr + i, 0))
    return pl.pallas_call(
        body,
        out_shape=jax.ShapeDtypeStruct((lay.hr, lay.pw), F32),
        grid_spec=pltpu.PrefetchScalarGridSpec(
            num_scalar_prefetch=1, grid=(lay.nr,),
            in_specs=[own, pl.BlockSpec((3, lay.tr, lay.pw), lambda i, j: (0, i, 0))],
            out_specs=pl.BlockSpec((lay.tr, lay.pw), lambda i, j: (i, 0))),
        compiler_params=_cp(("parallel",)),
        name=name,
    )(chip, pair32, recv)


def sum_devices(gathered, own, me, *, name):
    _, R, C = gathered.shape

    def body(me_ref, g_ref, o_ref, s_ref):
        acc = None
        for k in range(8):
            part = jnp.where(me_ref[0] == k, o_ref[...], g_ref[k])
            acc = part if acc is None else acc + part
        s_ref[...] = acc

    return pl.pallas_call(
        body,
        out_shape=jax.ShapeDtypeStruct((R, C), F32),
        grid_spec=pltpu.PrefetchScalarGridSpec(
            num_scalar_prefetch=1, grid=(1,),
            in_specs=[pl.BlockSpec((8, R, C), lambda i, m: (0, 0, 0)), pl.BlockSpec((R, C), lambda i, m: (0, 0))],
            out_specs=pl.BlockSpec((R, C), lambda i, m: (0, 0))),
        compiler_params=_cp(("arbitrary",)),
        name=name,
    )(me, gathered, own)


def sibling_swap(parts, *, name):
    nt = len(parts)

    def body(*refs):
        ins, outs = refs[:nt], refs[nt:2 * nt]
        send_sems, recv_sems = refs[2 * nt:]
        x, y, c = _mesh_pos()
        cps = []
        for t in range(nt):
            cp = pltpu.make_async_remote_copy(src_ref=ins[t], dst_ref=outs[t], send_sem=send_sems.at[t],
                                              recv_sem=recv_sems.at[t], device_id=(x, y, 1 - c), device_id_type=MESH_T)
            cp.start()
            cps.append(cp)
        for cp in cps:
            cp.wait_recv()
        for cp in cps:
            cp.wait_send()

    return pl.pallas_call(
        body,
        out_shape=tuple(jax.ShapeDtypeStruct(p.shape, p.dtype) for p in parts),
        in_specs=[ANY] * nt,
        out_specs=tuple([ANY] * nt),
        scratch_shapes=[pltpu.SemaphoreType.DMA((nt,)), pltpu.SemaphoreType.DMA((nt,))],
        name=name,
    )(*parts)


def adamw_halves(w, m, v, mine, other, lay, core, *, name):
    _, r, c = w.shape
    tr, nr = lay.tr, lay.nr
    assert (r, c) == (2 * lay.hr, lay.pw), (w.shape, lay.hr, lay.pw)

    def body(c_ref, w_ref, m_ref, v_ref, *rest):
        g_refs, outs = rest[:2 * DEPTH], rest[2 * DEPTH:]
        l, h = pl.program_id(0), pl.program_id(1)
        g = None
        for d in range(DEPTH):
            gd = jnp.where(h == c_ref[0], g_refs[d][...], g_refs[DEPTH + d][...])
            g = gd if g is None else jnp.where(l == d, gd, g)
        _adamw_math(g, w_ref, m_ref, v_ref, *outs)

    full = pl.BlockSpec((None, tr, c), lambda l, h, i, s: (l, h * nr + i, 0))

    def part(d, is_mine):
        def index(l, h, i, s):
            used = jnp.logical_and(l == d, (h == s[0]) == is_mine)
            return jnp.where(used, i, 0), 0
        return pl.BlockSpec((tr, c), index)

    shp = jax.ShapeDtypeStruct((DEPTH, r, c), F32)
    return pl.pallas_call(
        body,
        out_shape=(shp, shp, shp, shp),
        grid_spec=pltpu.PrefetchScalarGridSpec(
            num_scalar_prefetch=1, grid=(DEPTH, 2, nr),
            in_specs=[full, full, full] + [part(d, True) for d in range(DEPTH)] + [part(d, False) for d in range(DEPTH)],
            out_specs=(full, full, full, full)),
        compiler_params=_cp(("arbitrary", "arbitrary", "arbitrary")),
        name=name,
    )(core, w, m, v, *mine, *other)


WEIGHTS = ("norm_mix", "w_in", "b_gate", "q_norm_a", "k_norm_a", "rel_bias_a", "w_pool", "pool_scale",
           "w_branch_a", "w_branch_b", "w_branch_c", "w_out", "norm_ffn", "w_up", "conv_w", "conv_b", "w_down")
SHARDED = {"w_in": 2, "w_branch_a": 2, "w_branch_b": 2, "w_branch_c": 2, "w_out": 1, "w_up": 2, "conv_w": 2,
           "w_down": 1}
REPLICATED = tuple(n for n in WEIGHTS if n not in SHARDED)
MATMUL_WEIGHTS = tuple(n for n in SHARDED if n != "conv_w")
SMALL_WEIGHTS = tuple(n for n in WEIGHTS if n not in MATMUL_WEIGHTS)
SMALL_ROWS = 1496


def _layer_fwd(x, p, full, tables, rest=None, prefetch=None):
    l = p["l"]
    diag = exact_dot(p["rel_bias_a"], tables["onehot_t"], name="bias_diagonals")
    diag = diag.reshape(N_HEADS, N_VARIANTS, 1, DIAG_W).transpose(1, 0, 2, 3)
    biasm = bias_expand(diag, name="bias_expand")
    gq8 = jnp.tile(p["q_norm_a"], N_HEADS)[None]
    gk8 = jnp.tile(p["k_norm_a"], N_HEADS)[None]
    h = rmsnorm_fwd(x, p["norm_mix"][None], name="rmsnorm_fwd")
    if rest is None:
        proj = matmul(h, full["w_in"], b_layer=l, name="mm_in")
    else:
        plan, shards, names = rest
        proj, gathered = matmul(h, full["w_in"], b_layer=l, behind=(plan, shards), name="mm_in_gather")
        full = {**full, **dict(zip(names, gathered))}
    qa, ka, va, qb, kb, vb = qkv_prep(proj, gq8, gk8, name="qkv_prep")
    oa = attn_a_fwd(qa, ka, va, biasm, name="attn_a_fwd")
    if prefetch is None:
        ob, tot, nblk = attn_b_fwd(qb, kb, vb, name="attn_b_fwd")
    else:
        plan, shards, names = prefetch
        ob, tot, nblk, filled = attn_b_fwd(qb, kb, vb, gather=(plan, shards, [full[n] for n in names]),
                                           name="attn_b_fwd_gather")
        full = dict(zip(names, filled))
    wpool = p["w_pool"].astype(BF16)
    oc = pool_fwd(proj, wpool, p["pool_scale"][None], name="pool_fwd")
    merged = merge_fwd(oa, ob, oc, proj, p["b_gate"][None], full["w_branch_a"][l], full["w_branch_b"][l],
                       full["w_branch_c"][l], name="merge_fwd")
    x1 = matmul(merged, full["w_out"], b_layer=l, add=x, name="mm_out")
    h2 = rmsnorm_fwd(x1, p["norm_ffn"][None], name="rmsnorm_fwd")
    u = matmul(h2, full["w_up"], b_layer=l, name="mm_up")
    a = conv_glu_fwd(u, full["conv_w"][l], p["conv_b"][None], name="conv_glu_fwd")
    x2 = matmul(a, full["w_down"], b_layer=l, add=x1, name="mm_down")
    saved = dict(x=x, h=h, proj=proj, qa=qa, ka=ka, va=va, qb=qb, kb=kb, vb=vb, oa=oa, ob=ob, tot=tot, nblk=nblk, oc=oc,
                 merged=merged, x1=x1, h2=h2, u=u, a=a, biasm=biasm, gq8=gq8, gk8=gk8, wpool=wpool)
    return x2, saved, full


class GradReducer:
    def __init__(self, layouts, core, chip):
        self.layouts, self.core, self.chip = layouts, core, chip
        self.pairs32, self.received, self.pending = {}, {}, []

    def prepare(self, layer, grads):
        names = list(grads)
        lays = [self.layouts[n] for n in names]
        others = half_exchange([grads[n] for n in names], lays, name="half_exchange")
        for n, lay, other in zip(names, lays, others):
            p32, p16 = pair_sum(grads[n], other, lay, self.core, name="pair_sum")
            self.pairs32[(layer, n)] = p32
            self.pending.append(((layer, n), p16))

    def take(self):
        keys = [k for k, _ in self.pending]
        pairs16 = [p for _, p in self.pending]
        self.pending = []
        return keys, BlockScatter([self.layouts[n] for _, n in keys]), pairs16

    def store(self, keys, received):
        self.received.update(zip(keys, received))

    def finish(self):
        keys = list(self.pairs32)
        mine = [sum_chips(self.pairs32[k], self.received[k], self.layouts[k[1]], self.chip, name="sum_chips") for k in keys]
        other = sibling_swap(mine, name="sibling_swap")
        return dict(zip(keys, mine)), dict(zip(keys, other))


EARLY_WEIGHTS = ("w_down", "w_up", "w_out", "w_branch_a", "w_branch_b", "w_branch_c")


def _layer_bwd(dx2, s, p, tables, reducer):
    g = {}
    full, l = p["full"], p["l"]
    da = matmul(dx2, full["w_down"], b_layer=l, tb=True, name="mm_down_dx")
    g["w_down"] = matmul(s["a"], dx2, ta=True, name="mm_down_dw")
    dug, duv, dcwg, dcwv, dcbg, dcbv = conv_glu_bwd(s["u"], p["conv_w"], p["conv_b"][None], da, name="conv_glu_bwd")
    du = jnp.concatenate([dug, duv], axis=1)
    g["conv_w"] = jnp.concatenate([dcwg, dcwv], axis=1)
    g["conv_b"] = jnp.concatenate([dcbg, dcbv], axis=1)[0]
    g["w_up"] = matmul(s["h2"], du, ta=True, name="mm_up_dw")
    dx1, dg2 = matmul(du, full["w_up"], b_layer=l, tb=True, norm_bwd=(s["x1"], p["norm_ffn"][None], dx2),
                      name="mm_up_dx_norm")
    g["norm_ffn"] = dg2[0]
    dmerged = matmul(dx1, full["w_out"], b_layer=l, tb=True, name="mm_out_dx")
    g["w_out"] = matmul(s["merged"], dx1, ta=True, name="mm_out_dw")
    t_a, t_b, t_c, dga, dgb, dgc, dba, dbb, dbc = merge_bwd(
        dmerged, s["oa"], s["ob"], s["oc"], s["proj"], p["b_gate"][None], p["w_branch_a"], p["w_branch_b"],
        p["w_branch_c"], name="merge_bwd")
    g["b_gate"] = jnp.concatenate([dba, dbb, dbc], axis=1)[0]
    g["w_branch_a"] = matmul(s["oa"], t_a, ta=True, name="mm_branch_dw")
    g["w_branch_b"] = matmul(s["ob"], t_b, ta=True, name="mm_branch_dw")
    g["w_branch_c"] = matmul(s["oc"], t_c, ta=True, name="mm_branch_dw")
    doa = matmul(t_a, full["w_branch_a"], b_layer=l, tb=True, out_dtype=BF16, name="mm_branch_dx")
    dob = matmul(t_b, full["w_branch_b"], b_layer=l, tb=True, out_dtype=BF16, name="mm_branch_dx")
    doc = matmul(t_c, full["w_branch_c"], b_layer=l, tb=True, name="mm_branch_dx_f32")
    dqh, dkh, dva, dbias = attn_a_bwd(s["qa"], s["ka"], s["va"], s["biasm"], doa, name="attn_a_bwd")
    ddiag = relbias_reduce(dbias, name="relbias_reduce")
    ddiag = ddiag.transpose(1, 0, 2, 3).reshape(N_HEADS, N_VARIANTS * DIAG_W)
    g["rel_bias_a"] = exact_dot(ddiag, tables["onehot"], name="relbias_table")
    dqa, dka, dgq8, dgk8 = qknorm_bwd(s["proj"], s["gq8"], s["gk8"], dqh, dkh, name="qknorm_bwd")
    g["q_norm_a"] = dgq8.reshape(N_HEADS, HEAD_DIM).sum(axis=0)
    g["k_norm_a"] = dgk8.reshape(N_HEADS, HEAD_DIM).sum(axis=0)
    reducer.prepare(l, {n: g[n] for n in EARLY_WEIGHTS})
    keys, plan, pairs16 = reducer.take()
    dqb, dkb, dvb, received = attn_b_bwd(s["qb"], s["kb"], s["vb"], s["tot"], s["nblk"], dob, scatter=(plan, pairs16),
                                         name="attn_b_bwd_scatter")
    reducer.store(keys, received)
    duc, dwp, dsc = pool_bwd(s["proj"], s["wpool"], p["pool_scale"][None], doc, name="pool_bwd")
    g["w_pool"] = dwp
    g["pool_scale"] = dsc[0]
    dproj = jnp.concatenate([dqa, dka, dva.astype(BF16), dqb, dkb.astype(BF16), dvb.astype(BF16), duc,
                             dga, dgb, dgc], axis=1)
    g["w_in"] = matmul(s["h"], dproj, ta=True, name="mm_in_dw")
    reducer.prepare(l, {"w_in": g["w_in"]})
    norm = (s["x"], p["norm_mix"][None], dx1)
    if l > 0:
        dx, dg1 = matmul(dproj, full["w_in"], b_layer=l, tb=True, norm_bwd=norm, name="mm_in_dx_norm")
    else:
        keys, plan, pairs16 = reducer.take()
        dx, dg1, received = matmul(dproj, full["w_in"], b_layer=l, tb=True, norm_bwd=norm, behind=(plan, pairs16),
                                   name="mm_in_dx_norm_scatter")
        reducer.store(keys, received)
    g["norm_mix"] = dg1[0]
    return dx, g


def kernel(x, norm_mix, w_in, b_gate, q_norm_a, k_norm_a, rel_bias_a, w_pool, pool_scale, w_branch_a, w_branch_b, w_branch_c, w_out, norm_ffn, w_up, conv_w, conv_b, w_down, loss_target, m_norm_mix, m_w_in, m_b_gate, m_q_norm_a, m_k_norm_a, m_rel_bias_a, m_w_pool, m_pool_scale, m_w_branch_a, m_w_branch_b, m_w_branch_c, m_w_out, m_norm_ffn, m_w_up, m_conv_w, m_conv_b, m_w_down, v_norm_mix, v_w_in, v_b_gate, v_q_norm_a, v_k_norm_a, v_rel_bias_a, v_w_pool, v_pool_scale, v_w_branch_a, v_w_branch_b, v_w_branch_c, v_w_out, v_norm_ffn, v_w_up, v_conv_w, v_conv_b, v_w_down):
    w = dict(zip(WEIGHTS, (norm_mix, w_in, b_gate, q_norm_a, k_norm_a, rel_bias_a, w_pool, pool_scale, w_branch_a,
                           w_branch_b, w_branch_c, w_out, norm_ffn, w_up, conv_w, conv_b, w_down)))
    m = dict(zip(WEIGHTS, (m_norm_mix, m_w_in, m_b_gate, m_q_norm_a, m_k_norm_a, m_rel_bias_a, m_w_pool, m_pool_scale,
                           m_w_branch_a, m_w_branch_b, m_w_branch_c, m_w_out, m_norm_ffn, m_w_up, m_conv_w, m_conv_b,
                           m_w_down)))
    v = dict(zip(WEIGHTS, (v_norm_mix, v_w_in, v_b_gate, v_q_norm_a, v_k_norm_a, v_rel_bias_a, v_w_pool, v_pool_scale,
                           v_w_branch_a, v_w_branch_b, v_w_branch_c, v_w_out, v_norm_ffn, v_w_up, v_conv_w, v_conv_b,
                           v_w_down)))
    onehot = diagonal_onehot()
    tables = dict(onehot=jnp.asarray(onehot), onehot_t=jnp.asarray(np.ascontiguousarray(onehot.T)))

    names = tuple(SHARDED)
    shards = [w[n] if n == "conv_w" else w[n].astype(BF16) for n in names]
    axes = [SHARDED[n] for n in names]
    later = [i for i, n in enumerate(names) if n != "w_in"]
    rest = (LayerGather([shards[i] for i in later], [axes[i] for i in later], 0), [shards[i] for i in later],
            [names[i] for i in later])
    first = names.index("w_in")
    full = {"w_in": all_gather_layer([shards[first]], [axes[first]], 0, name="all_gather_layer")[0]}

    def layer_params(l):
        p = {n: full[n][l] for n in ("w_branch_a", "w_branch_b", "w_branch_c", "conv_w")}
        p.update({n: w[n][l] for n in REPLICATED})
        p.update(full=full, l=l)
        return p

    xs = x[0]
    saved = []
    for l in range(DEPTH):
        prefetch = (LayerGather(shards, axes, l + 1), shards, names) if l + 1 < DEPTH else None
        replicated = {n: w[n][l] for n in REPLICATED}
        xs, s, full = _layer_fwd(xs, dict(replicated, l=l), full, tables, rest if l == 0 else None, prefetch)
        saved.append(s)
    dx, lpart = loss_head(xs, loss_target[0], name="loss_head")
    loss = lax.psum(lpart[0, 0], MESH_AXES)
    as_index = lambda i: jnp.reshape(i, (1,)).astype(jnp.int32)
    cx, cy, cc = _mesh_pos()
    core, chip, me = as_index(cc), as_index(2 * cx + cy), as_index(4 * cx + 2 * cy + cc)
    layouts = {n: HalfLayout((full[n].shape[1], full[n].shape[2]), SHARDED[n] - 1) for n in MATMUL_WEIGHTS}
    reducer = GradReducer(layouts, core, chip)
    grads = [None] * DEPTH
    for l in reversed(range(DEPTH)):
        dx, grads[l] = _layer_bwd(dx, saved[l], layer_params(l), tables, reducer)

    g = {n: jnp.stack([grads[l][n] for l in range(DEPTH)]) for n in SMALL_WEIGHTS}
    flat = jnp.concatenate([g[n].reshape(-1) for n in SMALL_WEIGHTS])
    small = jnp.pad(flat, (0, SMALL_ROWS * 128 - flat.shape[0])).reshape(SMALL_ROWS, 128)
    small_sum = sum_devices(gather_small(small, name="gather_small"), small, me, name="sum_devices").reshape(-1)
    mine, other = reducer.finish()

    res = {}
    for n in MATMUL_WEIGHTS:
        res[n] = adamw_halves(w[n], m[n], v[n], [mine[(l, n)] for l in range(DEPTH)],
                              [other[(l, n)] for l in range(DEPTH)], layouts[n], core, name="adamw_halves")
    off = 0
    for n in SMALL_WEIGHTS:
        shp = g[n].shape
        size = int(np.prod(shp))
        gn = small_sum[off:off + size].reshape(shp)
        off += size
        if n in SHARDED:
            gn = lax.dynamic_slice_in_dim(gn, (2 * cx + cy) * w[n].shape[-1], w[n].shape[-1], axis=len(shp) - 1)
        shp = w[n].shape
        cols = shp[-1]
        two_d = lambda t: t.reshape(int(np.prod(shp)) // cols, cols)
        res[n] = [t.reshape(shp) for t in adamw(two_d(w[n]), two_d(m[n]), two_d(v[n]), two_d(gn), name="adamw")]

    out = [loss, dx[None]]
    for k in range(4):
        out.extend(res[n][k] for n in WEIGHTS)
    return tuple(out)
```

```python
import jax
import jax.numpy as jnp
import numpy as np
from jax import lax
from jax.experimental import pallas as pl
from jax.experimental.pallas import tpu as pltpu

F32 = jnp.float32
BF16 = jnp.bfloat16

D_MODEL = 1024
DEPTH = 2
CHUNK = 64
N_LEFT = 8
HEAD_DIM = 64
N_HEADS = 8
WIDTH = 512
POOL_WINDOWS = (2, 4, 8, 16)
GROUP_DIM = 128
MAX_REL = 2 * CHUNK
REL_TABLE = MAX_REL + CHUNK
D_FF = 2816
EPS = 1e-6
QK_SCALE = 0.125
GATE_COL0 = 7 * WIDTH

ADAM_LR = 0.001
ADAM_B1 = 0.9
ADAM_B2 = 0.999
ADAM_EPS = 1e-08
ADAM_WD = 0.01
ADAM_STEP = 10

VMEM_LIMIT = 56 * 1024 * 1024
ATT_Q = 256
A_Q = 256
A_WIN = A_Q + N_LEFT * CHUNK
A_FWD_HEADS = 8
A_BWD_HEADS = 4
B_FWD_HEADS = 8
B_BWD_HEADS = 4
HALO = 16
CONV_HALO = 8
NEG = -1e30

MESH_AXES = ("x", "y", "c")
MESH_T = pl.DeviceIdType.MESH


def _cp(sem=None, vmem=VMEM_LIMIT):
    return pltpu.CompilerParams(dimension_semantics=sem, vmem_limit_bytes=vmem)


def _dot(a, b, ca, cb):
    return lax.dot_general(a, b, (((ca,), (cb,)), ((), ())), preferred_element_type=F32)


def _tile(n, cands=(1024, 512, 256, 128)):
    for c in cands:
        if n % c == 0:
            return c
    return n


def _split_hi_lo(v):
    hi = v.astype(BF16)
    lo = (v - hi.astype(F32)).astype(BF16)
    return hi, lo


def matmul(a, b, *, ta=False, tb=False, add=None, norm_bwd=None, out_dtype=F32, b_layer=None, behind=None, name):
    plan, sources = behind if behind is not None else (None, [])
    n_src = len(sources)
    n_dst = len(plan.out_shapes()) if plan is not None else 0
    assert add is None or norm_bwd is None
    if ta:
        K, M = a.shape
    else:
        M, K = a.shape
    if tb:
        N, K2 = b.shape[-2:]
    else:
        K2, N = b.shape[-2:]
    assert K == K2, (a.shape, b.shape, ta, tb)
    big = (1024, 1408, 512, 256, 128)
    tm = _tile(M, big)
    tn = _tile(N, (1664,) + big)
    tk = _tile(K, big if ta else (1664, 1408) + big if norm_bwd is not None else (3328, 2816) + big)
    nk = K // tk

    n_extra = 1 if add is not None else 3 if norm_bwd is not None else 0
    n_in = 2 + n_extra + n_src
    n_out = 2 if norm_bwd is not None else 1
    grid = (M // tm, N // tn, nk)
    assert norm_bwd is None or tn == N, "the norm gradient needs whole rows in one output tile"

    def body(*refs):
        a_ref, b_ref = refs[:2]
        extra = refs[2:2 + n_extra]
        o_ref, acc = refs[n_in], refs[n_in + n_out + n_dst]
        i, j, k = pl.program_id(0), pl.program_id(1), pl.program_id(2)
        if plan is not None:
            comm = (refs[n_in - n_src:n_in], refs[n_in + n_out:n_in + n_out + n_dst], refs[n_in + n_out + n_dst + 1:])

            @pl.when((i == 0) & (j == 0) & (k == 0))
            def _():
                plan.start(*comm)

        @pl.when(k == 0)
        def _():
            acc[...] = jnp.zeros_like(acc)

        av = a_ref[...].astype(BF16)
        bv = b_ref[...].astype(BF16)
        acc[...] += _dot(av, bv, 0 if ta else 1, 1 if tb else 0)

        @pl.when(k == nk - 1)
        def _():
            r = acc[...]
            if add is not None:
                r = r + extra[0][...].astype(F32)
            if norm_bwd is not None:
                x_ref, g_ref, dres_ref = extra
                dg_ref = refs[n_in + 1]
                xv = x_ref[...]
                inv = lax.rsqrt(jnp.mean(xv * xv, axis=-1, keepdims=True) + EPS)
                gd = r * g_ref[...]
                mean = jnp.mean(xv * gd, axis=-1, keepdims=True)

                @pl.when(i == 0)
                def _():
                    dg_ref[...] = jnp.zeros_like(dg_ref)

                dg_ref[...] += jnp.sum(r * xv * inv, axis=0, keepdims=True)
                r = dres_ref[...] + inv * gd - xv * (inv * inv * inv * mean)
            o_ref[...] = r.astype(out_dtype)

        if plan is not None:
            @pl.when((i == grid[0] - 1) & (j == grid[1] - 1) & (k == nk - 1))
            def _():
                plan.finish(*comm)

    a_spec = pl.BlockSpec((tk, tm), lambda i, j, k: (k, i)) if ta else pl.BlockSpec((tm, tk), lambda i, j, k: (i, k))
    if b_layer is None:
        b_spec = pl.BlockSpec((tn, tk), lambda i, j, k: (j, k)) if tb else pl.BlockSpec((tk, tn), lambda i, j, k: (k, j))
    elif tb:
        b_spec = pl.BlockSpec((None, tn, tk), lambda i, j, k: (b_layer, j, k))
    else:
        b_spec = pl.BlockSpec((None, tk, tn), lambda i, j, k: (b_layer, k, j))
    o_spec = pl.BlockSpec((tm, tn), lambda i, j, k: (i, j))
    vec_spec = pl.BlockSpec((1, tn), lambda i, j, k: (0, j))
    in_specs = [a_spec, b_spec]
    args = [a, b]
    out_shape = [jax.ShapeDtypeStruct((M, N), out_dtype)]
    out_specs = [o_spec]
    if add is not None:
        in_specs.append(o_spec)
        args.append(add)
    if norm_bwd is not None:
        in_specs += [o_spec, vec_spec, o_spec]
        args += list(norm_bwd)
        out_shape.append(jax.ShapeDtypeStruct((1, N), F32))
        out_specs.append(vec_spec)
    sequential = plan is not None or norm_bwd is not None
    any_space = pl.BlockSpec(memory_space=pl.ANY)
    outs = pl.pallas_call(
        body,
        out_shape=tuple(out_shape) + (tuple(plan.out_shapes()) if plan is not None else ()),
        grid=grid,
        in_specs=in_specs + [any_space] * n_src,
        out_specs=tuple(out_specs) + tuple([any_space] * n_dst),
        scratch_shapes=[pltpu.VMEM((tm, tn), F32)] + (list(plan.scratch_shapes()) if plan is not None else []),
        compiler_params=_cp(("arbitrary" if sequential else "parallel",) * 2 + ("arbitrary",)),
        name=name,
    )(*args, *sources)
    if plan is None:
        return outs[0] if n_out == 1 else tuple(outs)
    return tuple(outs[:n_out]) + (list(outs[n_out:]),) if n_out > 1 else (outs[0], list(outs[1:]))


def rmsnorm_fwd(x, g, *, name):
    S, D = x.shape
    T = _tile(S)

    def body(x_ref, g_ref, h_ref):
        xv = x_ref[...]
        r = lax.rsqrt(jnp.mean(xv * xv, axis=-1, keepdims=True) + EPS)
        h_ref[...] = (xv * r * g_ref[...]).astype(BF16)

    return pl.pallas_call(
        body,
        out_shape=jax.ShapeDtypeStruct((S, D), BF16),
        grid=(S // T,),
        in_specs=[pl.BlockSpec((T, D), lambda i: (i, 0)), pl.BlockSpec((1, D), lambda i: (0, 0))],
        out_specs=pl.BlockSpec((T, D), lambda i: (i, 0)),
        compiler_params=_cp(("parallel",)),
        name=name,
    )(x, g)


def _head_mean_matrix():
    r = lax.broadcasted_iota(jnp.int32, (WIDTH, WIDTH), 0) // HEAD_DIM
    c = lax.broadcasted_iota(jnp.int32, (WIDTH, WIDTH), 1) // HEAD_DIM
    return jnp.where(r == c, 1.0 / HEAD_DIM, 0.0).astype(BF16)


def _head_mean(v, mm):
    hi, lo = _split_hi_lo(v)
    return _dot(hi, mm, 1, 0) + _dot(lo, mm, 1, 0)


def qkv_prep(proj, gq, gk, *, name):
    S = proj.shape[0]
    T = _tile(S)

    def body(qa, ka, va, qb, kb, vb, gq_ref, gk_ref, oqa, oka, ova, oqb, okb, ovb):
        mm = _head_mean_matrix()
        for src, gref, dst, scale in ((qa, gq_ref, oqa, QK_SCALE), (ka, gk_ref, oka, 1.0)):
            v = src[...]
            r = lax.rsqrt(_head_mean(v * v, mm) + EPS)
            dst[...] = (v * r * gref[...] * scale).astype(BF16)
        oqb[...] = (qb[...] * QK_SCALE).astype(BF16)
        for src, dst in ((va, ova), (kb, okb), (vb, ovb)):
            dst[...] = src[...].astype(BF16)

    col = lambda j: pl.BlockSpec((T, WIDTH), lambda i, j=j: (i, j))
    vec = pl.BlockSpec((1, WIDTH), lambda i: (0, 0))
    out = pl.BlockSpec((T, WIDTH), lambda i: (i, 0))
    return pl.pallas_call(
        body,
        out_shape=tuple(jax.ShapeDtypeStruct((S, WIDTH), BF16) for _ in range(6)),
        grid=(S // T,),
        in_specs=[col(0), col(1), col(2), col(3), col(4), col(5), vec, vec],
        out_specs=tuple(out for _ in range(6)),
        compiler_params=_cp(("parallel",)),
        name=name,
    )(proj, proj, proj, proj, proj, proj, gq, gk)


def qknorm_bwd(proj, gq, gk, dqh, dkh, *, name):
    S = proj.shape[0]
    T = _tile(S)

    def body(qa, ka, gq_ref, gk_ref, dq_ref, dk_ref, oq, ok, ogq, ogk):
        i = pl.program_id(0)
        mm = _head_mean_matrix()

        @pl.when(i == 0)
        def _():
            ogq[...] = jnp.zeros_like(ogq)
            ogk[...] = jnp.zeros_like(ogk)

        for src, gref, dref, dst, gdst in ((qa, gq_ref, dq_ref, oq, ogq), (ka, gk_ref, dk_ref, ok, ogk)):
            v = src[...]
            dy = dref[...]
            r = lax.rsqrt(_head_mean(v * v, mm) + EPS)
            gd = dy * gref[...]
            m = _head_mean(v * gd, mm)
            dst[...] = (r * gd - v * (r * r * r * m)).astype(BF16)
            gdst[...] += jnp.sum(dy * v * r, axis=0, keepdims=True)

    col = lambda j: pl.BlockSpec((T, WIDTH), lambda i, j=j: (i, j))
    vec = pl.BlockSpec((1, WIDTH), lambda i: (0, 0))
    row = pl.BlockSpec((T, WIDTH), lambda i: (i, 0))
    return pl.pallas_call(
        body,
        out_shape=(jax.ShapeDtypeStruct((S, WIDTH), BF16), jax.ShapeDtypeStruct((S, WIDTH), BF16),
                   jax.ShapeDtypeStruct((1, WIDTH), F32), jax.ShapeDtypeStruct((1, WIDTH), F32)),
        grid=(S // T,),
        in_specs=[col(0), col(1), vec, vec, row, row],
        out_specs=(row, row, vec, vec),
        compiler_params=_cp(("arbitrary",)),
        name=name,
    )(proj, proj, gq, gk, dqh, dkh)


DIAG_W = 1024
N_VARIANTS = N_LEFT * CHUNK // A_Q + 1


def diagonal_onehot():
    jj = np.arange(DIAG_W)
    diff = np.where(jj < A_WIN, jj, jj - DIAG_W)
    out = np.zeros((N_VARIANTS, DIAG_W, REL_TABLE), np.float32)
    for v in range(N_VARIANTS):
        rel = np.clip(A_Q * v - diff, -(CHUNK - 1), MAX_REL) + (CHUNK - 1)
        out[v, jj, rel] = 1.0
    return out.reshape(N_VARIANTS * DIAG_W, REL_TABLE)


def exact_dot(a, b, *, name):
    def body(a_ref, b_ref, o_ref):
        o_ref[...] = jnp.dot(a_ref[...], b_ref[...], precision=lax.Precision.HIGHEST, preferred_element_type=F32)

    return pl.pallas_call(body, out_shape=jax.ShapeDtypeStruct((a.shape[0], b.shape[1]), F32),
                          compiler_params=_cp(), name=name)(a, b)


def _band_valid(v):
    qc = (lax.broadcasted_iota(jnp.int32, (A_Q, A_WIN), 0) + A_Q * v) // CHUNK
    kc = lax.broadcasted_iota(jnp.int32, (A_Q, A_WIN), 1) // CHUNK
    return (kc <= qc) & (kc >= qc - N_LEFT)


def bias_expand(diag, *, name):
    def body(d_ref, o_ref):
        rows = jnp.broadcast_to(d_ref[0, 0], (A_Q, DIAG_W))
        skew = pltpu.roll(rows, 0, 1, stride=1, stride_axis=0)
        o_ref[0, 0] = jnp.where(_band_valid(pl.program_id(0)), skew[:, :A_WIN], NEG)

    return pl.pallas_call(
        body,
        out_shape=jax.ShapeDtypeStruct((N_VARIANTS, N_HEADS, A_Q, A_WIN), F32),
        grid=(N_VARIANTS, N_HEADS),
        in_specs=[pl.BlockSpec((1, 1, 1, DIAG_W), lambda v, h: (v, h, 0, 0))],
        out_specs=pl.BlockSpec((1, 1, A_Q, A_WIN), lambda v, h: (v, h, 0, 0)),
        compiler_params=_cp(("parallel", "parallel")),
        name=name,
    )(diag)


def relbias_reduce(dbias, *, name):
    def body(db_ref, o_ref):
        acc = None
        for a in range(A_Q // 8):
            x = jnp.concatenate([db_ref[0, 0, 8 * a:8 * a + 8, :], jnp.zeros((8, DIAG_W - A_WIN), F32)], axis=1)
            x = pltpu.roll(x, DIAG_W - 8 * a, 1) if a else x
            acc = x if acc is None else acc + x
        row = lax.broadcasted_iota(jnp.int32, (8, DIAG_W), 0)
        for b in range(3):
            acc = jnp.where((row >> b) & 1 == 1, pltpu.roll(acc, DIAG_W - (1 << b), 1), acc)
        o_ref[0, 0] = jnp.sum(acc, axis=0, keepdims=True)

    return pl.pallas_call(
        body,
        out_shape=jax.ShapeDtypeStruct((N_VARIANTS, N_HEADS, 1, DIAG_W), F32),
        grid=(N_VARIANTS, N_HEADS),
        in_specs=[pl.BlockSpec((1, 1, A_Q, A_WIN), lambda v, h: (v, h, 0, 0))],
        out_specs=pl.BlockSpec((1, 1, 1, DIAG_W), lambda v, h: (v, h, 0, 0)),
        compiler_params=_cp(("parallel", "parallel")),
        name=name,
    )(dbias)


def _a_window_start(qb):
    return pl.multiple_of(jnp.maximum(qb * A_Q - N_LEFT * CHUNK, 0), A_Q)


def attn_a_fwd(q, k, v, biasm, *, name):
    S = q.shape[0]
    nq = S // A_Q

    def body(q_ref, k_ref, v_ref, b_ref, o_ref):
        qb = pl.program_id(1)
        start = _a_window_start(qb)
        outs = []
        for h in range(A_FWD_HEADS):
            lanes = slice(h * HEAD_DIM, (h + 1) * HEAD_DIM)
            qh = q_ref[:, lanes]
            kw = k_ref[pl.ds(start, A_WIN), lanes]
            vw = v_ref[pl.ds(start, A_WIN), lanes]
            s = _dot(qh, kw, 1, 1) + b_ref[0, h]
            m = jnp.max(s, axis=-1, keepdims=True)
            e = jnp.exp(s - m)
            outs.append(_dot(e.astype(BF16), vw, 1, 0) * (1.0 / jnp.sum(e, axis=-1, keepdims=True)))
        o_ref[...] = jnp.concatenate(outs, axis=1).astype(BF16)

    qspec = pl.BlockSpec((A_Q, A_FWD_HEADS * HEAD_DIM), lambda hp, qb: (qb, hp))
    kvspec = pl.BlockSpec((S, A_FWD_HEADS * HEAD_DIM), lambda hp, qb: (0, hp))
    bspec = pl.BlockSpec((1, A_FWD_HEADS, A_Q, A_WIN), lambda hp, qb: (jnp.minimum(qb, N_VARIANTS - 1), hp, 0, 0))
    return pl.pallas_call(
        body,
        out_shape=jax.ShapeDtypeStruct((S, WIDTH), BF16),
        grid=(N_HEADS // A_FWD_HEADS, nq),
        in_specs=[qspec, kvspec, kvspec, bspec],
        out_specs=qspec,
        compiler_params=_cp(("parallel", "arbitrary")),
        name=name,
    )(q, k, v, biasm)


def attn_a_bwd(q, k, v, biasm, do, *, name):
    S = q.shape[0]
    nq = S // A_Q

    def body(q_ref, k_ref, v_ref, b_ref, do_ref, dq_ref, dk_ref, dv_ref, db_ref):
        qb = pl.program_id(1)
        start = _a_window_start(qb)

        @pl.when(qb == 0)
        def _():
            dk_ref[...] = jnp.zeros_like(dk_ref)
            dv_ref[...] = jnp.zeros_like(dv_ref)

        @pl.when(qb < N_VARIANTS)
        def _():
            db_ref[...] = jnp.zeros_like(db_ref)

        dqs = []
        for h in range(A_BWD_HEADS):
            lanes = slice(h * HEAD_DIM, (h + 1) * HEAD_DIM)
            qh = q_ref[:, lanes]
            doh = do_ref[:, lanes]
            kw = k_ref[pl.ds(start, A_WIN), lanes]
            vw = v_ref[pl.ds(start, A_WIN), lanes]
            s = _dot(qh, kw, 1, 1) + b_ref[0, h]
            m = jnp.max(s, axis=-1, keepdims=True)
            e = jnp.exp(s - m)
            p = e * (1.0 / jnp.sum(e, axis=-1, keepdims=True))
            dp = _dot(doh, vw, 1, 1)
            delta = jnp.sum(p * dp, axis=-1, keepdims=True)
            ds = p * (dp - delta)
            db_ref[0, h] += ds
            dsb = ds.astype(BF16)
            dqs.append(_dot(dsb, kw, 1, 0) * QK_SCALE)
            dk_ref[pl.ds(start, A_WIN), lanes] += _dot(dsb, qh, 0, 0)
            dv_ref[pl.ds(start, A_WIN), lanes] += _dot(p.astype(BF16), doh, 0, 0)
        dq_ref[...] = jnp.concatenate(dqs, axis=1)

    once = pl.Buffered(1)
    qspec = pl.BlockSpec((A_Q, A_BWD_HEADS * HEAD_DIM), lambda hp, qb: (qb, hp))
    kvspec = pl.BlockSpec((S, A_BWD_HEADS * HEAD_DIM), lambda hp, qb: (0, hp), pipeline_mode=once)
    bspec = pl.BlockSpec((1, A_BWD_HEADS, A_Q, A_WIN), lambda hp, qb: (jnp.minimum(qb, N_VARIANTS - 1), hp, 0, 0))
    return pl.pallas_call(
        body,
        out_shape=(jax.ShapeDtypeStruct((S, WIDTH), F32), jax.ShapeDtypeStruct((S, WIDTH), F32),
                   jax.ShapeDtypeStruct((S, WIDTH), F32), jax.ShapeDtypeStruct((N_VARIANTS, N_HEADS, A_Q, A_WIN), F32)),
        grid=(N_HEADS // A_BWD_HEADS, nq),
        in_specs=[qspec, kvspec, kvspec, bspec, qspec],
        out_specs=(qspec, kvspec, kvspec, bspec),
        compiler_params=_cp(("parallel", "arbitrary")),
        name=name,
    )(q, k, v, biasm, do)


def _tri(kind):
    j = lax.broadcasted_iota(jnp.int32, (ATT_Q, ATT_Q), 0)
    s = lax.broadcasted_iota(jnp.int32, (ATT_Q, ATT_Q), 1)
    if kind == "gt":
        m = j > s
    elif kind == "le":
        m = j <= s
    else:
        m = j < s
    return jnp.where(m, 1.0, 0.0).astype(BF16)


def _cum(v, tri):
    hi, lo = _split_hi_lo(v)
    return _dot(hi, tri, 1, 0) + _dot(lo, tri, 1, 0)


def _log_sigmoids(z, mask):
    t = jnp.log(1.0 + jnp.exp(-jnp.abs(z)))
    keep = -(jnp.maximum(z, 0.0) + t)
    take = jnp.minimum(z, 0.0) - t
    return (keep if mask is None else jnp.where(mask, keep, 0.0)), take


def _strictly_before():
    row = lax.broadcasted_iota(jnp.int32, (ATT_Q, ATT_Q), 0)
    col = lax.broadcasted_iota(jnp.int32, (ATT_Q, ATT_Q), 1)
    return col < row


EXIT_LOG = -104.0


def attn_b_fwd(q, k, v, *, gather=None, name):
    S = q.shape[0]
    nq = S // ATT_Q
    heads = B_FWD_HEADS
    plan, shards, fulls = gather if gather is not None else (None, [], [])
    ng = len(shards)

    def body(q_ref, k_ref, v_ref, *rest):
        hp = pl.program_id(0)
        qb = pl.program_id(1)
        o_ref, t_ref, n_ref = rest[2 * ng:2 * ng + 3]
        if plan is not None:
            comm = (rest[:ng], rest[2 * ng + 3:3 * ng + 3], rest[3 * ng + 3:])

            @pl.when(jnp.logical_and(hp == 0, qb == 0))
            def _():
                plan.start(*comm)

        tri = _tri("gt")

        def block(kb, carry, mask):
            ks = pl.multiple_of(kb * ATT_Q, ATT_Q)
            new = []
            for h in range(heads):
                lanes = slice(h * HEAD_DIM, (h + 1) * HEAD_DIM)
                c, acc = carry[h]
                z = _dot(q_ref[:, lanes], k_ref[pl.ds(ks, ATT_Q), lanes], 1, 1)
                keep, take = _log_sigmoids(z, mask)
                w = jnp.exp(take + (_cum(keep, tri) + c))
                if mask is not None:
                    w = jnp.where(mask, w, 0.0)
                acc = acc + _dot(w.astype(BF16), v_ref[pl.ds(ks, ATT_Q), lanes], 1, 0)
                c = c + jnp.sum(keep, axis=-1, keepdims=True)
                new.append((c, acc))
            return tuple(new)

        def cond(state):
            it, cmax, _ = state
            return jnp.logical_and(it <= qb, cmax >= EXIT_LOG)

        def step(state):
            it, _, carry = state
            carry = block(qb - it, carry, None)
            worst = carry[0][0]
            for h in range(1, heads):
                worst = jnp.maximum(worst, carry[h][0])
            return it + 1, jnp.max(worst), carry

        init = tuple((jnp.zeros((ATT_Q, 1), F32), jnp.zeros((ATT_Q, HEAD_DIM), F32)) for _ in range(heads))
        diag = block(qb, init, _strictly_before())
        visited, _, res = lax.while_loop(cond, step, (jnp.int32(1), jnp.float32(0.0), diag))
        o_ref[...] = jnp.concatenate([res[h][1] for h in range(heads)], axis=1).astype(BF16)
        t_ref[...] = jnp.concatenate([jnp.broadcast_to(res[h][0], (ATT_Q, HEAD_DIM)) for h in range(heads)], axis=1)
        n_ref[hp, qb] = visited.astype(F32)
        if plan is not None:
            @pl.when(jnp.logical_and(hp == N_HEADS // heads - 1, qb == nq - 1))
            def _():
                plan.finish(*comm)

    qspec = pl.BlockSpec((ATT_Q, heads * HEAD_DIM), lambda hp, qb: (qb, hp))
    kvspec = pl.BlockSpec((S, heads * HEAD_DIM), lambda hp, qb: (0, hp), pipeline_mode=pl.Buffered(1))
    outs = pl.pallas_call(
        body,
        out_shape=(jax.ShapeDtypeStruct((S, WIDTH), BF16), jax.ShapeDtypeStruct((S, WIDTH), F32),
                   jax.ShapeDtypeStruct((N_HEADS // heads, nq), F32))
        + tuple(jax.ShapeDtypeStruct(f.shape, f.dtype) for f in fulls),
        grid=(N_HEADS // heads, nq),
        in_specs=[qspec, kvspec, kvspec] + [ANY] * (2 * ng),
        out_specs=(qspec, qspec, pl.BlockSpec(memory_space=pltpu.SMEM)) + tuple([ANY] * ng),
        scratch_shapes=plan.scratch_shapes() if plan is not None else (),
        input_output_aliases={3 + ng + i: 3 + i for i in range(ng)},
        compiler_params=_cp(("arbitrary", "arbitrary")),
        name=name,
    )(q, k, v, *shards, *fulls)
    return outs if plan is None else (outs[0], outs[1], outs[2], list(outs[3:]))


def attn_b_bwd(q, k, v, tot, nblk, do, *, scatter=None, name):
    S = q.shape[0]
    nq = S // ATT_Q
    heads = B_BWD_HEADS
    plan, pairs16 = scatter if scatter is not None else (None, [])
    ns = len(pairs16)

    def body(q_ref, k_ref, v_ref, t_ref, n_ref, do_ref, *rest):
        hp = pl.program_id(0)
        qb = pl.program_id(1)
        dq_ref, dk_ref, dv_ref = rest[ns:ns + 3]
        if plan is not None:
            comm = (rest[:ns], rest[ns + 3:2 * ns + 3], rest[2 * ns + 3:])

            @pl.when(jnp.logical_and(hp == 0, qb == 0))
            def _():
                plan.start(*comm)

        visited = n_ref[(hp * heads) // B_FWD_HEADS, qb].astype(jnp.int32)
        first = jnp.clip(qb + 1 - visited, 0, qb + 1)
        tri_le = _tri("le")
        tri_lt = _tri("lt")

        @pl.when(qb == 0)
        def _():
            dk_ref[...] = jnp.zeros_like(dk_ref)
            dv_ref[...] = jnp.zeros_like(dv_ref)

        def block(kb, carry, mask):
            ks = pl.multiple_of(kb * ATT_Q, ATT_Q)
            new = []
            for h in range(heads):
                lanes = slice(h * HEAD_DIM, (h + 1) * HEAD_DIM)
                cl, cg, dq = carry[h]
                qh = q_ref[:, lanes]
                doh = do_ref[:, lanes]
                kh = k_ref[pl.ds(ks, ATT_Q), lanes]
                vh = v_ref[pl.ds(ks, ATT_Q), lanes]
                totl = t_ref[:, h * HEAD_DIM:h * HEAD_DIM + 1]
                z = _dot(qh, kh, 1, 1)
                keep, take = _log_sigmoids(z, mask)
                sig = jnp.exp(take)
                w = sig * jnp.exp((totl - cl) - _cum(keep, tri_le))
                if mask is not None:
                    w = jnp.where(mask, w, 0.0)
                g = w * _dot(doh, vh, 1, 1)
                G = _dot(g.astype(BF16), tri_lt, 1, 0) + cg
                dz = g * (1.0 - sig) - sig * G
                if mask is not None:
                    dz = jnp.where(mask, dz, 0.0)
                dz = dz.astype(BF16)
                dq = dq + _dot(dz, kh, 1, 0)
                dk_ref[pl.ds(ks, ATT_Q), lanes] += _dot(dz, qh, 0, 0)
                dv_ref[pl.ds(ks, ATT_Q), lanes] += _dot(w.astype(BF16), doh, 0, 0)
                cl = cl + jnp.sum(keep, axis=-1, keepdims=True)
                cg = cg + jnp.sum(g, axis=-1, keepdims=True)
                new.append((cl, cg, dq))
            return tuple(new)

        init = tuple((jnp.zeros((ATT_Q, 1), F32), jnp.zeros((ATT_Q, 1), F32), jnp.zeros((ATT_Q, HEAD_DIM), F32))
                     for _ in range(heads))
        res = lax.fori_loop(jnp.minimum(first, qb), qb, lambda kb, carry: block(kb, carry, None), init)
        res = block(qb, res, _strictly_before())
        dq_ref[...] = (jnp.concatenate([res[h][2] for h in range(heads)], axis=1) * QK_SCALE).astype(BF16)
        if plan is not None:
            @pl.when(jnp.logical_and(hp == N_HEADS // heads - 1, qb == nq - 1))
            def _():
                plan.finish(*comm)

    qspec = pl.BlockSpec((ATT_Q, heads * HEAD_DIM), lambda hp, qb: (qb, hp))
    kvspec = pl.BlockSpec((S, heads * HEAD_DIM), lambda hp, qb: (0, hp), pipeline_mode=pl.Buffered(1))
    outs = pl.pallas_call(
        body,
        out_shape=(jax.ShapeDtypeStruct((S, WIDTH), BF16), jax.ShapeDtypeStruct((S, WIDTH), F32),
                   jax.ShapeDtypeStruct((S, WIDTH), F32)) + (tuple(plan.out_shapes()) if plan is not None else ()),
        grid=(N_HEADS // heads, nq),
        in_specs=[qspec, kvspec, kvspec, qspec, pl.BlockSpec(memory_space=pltpu.SMEM), qspec] + [ANY] * ns,
        out_specs=(qspec, kvspec, kvspec) + tuple([ANY] * ns),
        scratch_shapes=plan.scratch_shapes() if plan is not None else (),
        compiler_params=_cp(("arbitrary", "arbitrary")),
        name=name,
    )(q, k, v, tot, nblk, do, *pairs16)
    return outs if plan is None else (outs[0], outs[1], outs[2], list(outs[3:]))


U_COLBLK = 6


def _pool_counts(t0, rows):
    t = t0 + lax.broadcasted_iota(jnp.int32, (rows, WIDTH), 0)
    lane_grp = lax.broadcasted_iota(jnp.int32, (rows, WIDTH), 1) // GROUP_DIM
    w2, w4, w8, w16 = POOL_WINDOWS
    win = jnp.where(lane_grp == 0, w2, jnp.where(lane_grp == 1, w4, jnp.where(lane_grp == 2, w8, w16)))
    cnt = jnp.minimum(t + 1, win)
    return 1.0 / cnt.astype(F32), lane_grp


def _window_sums(ext, shift_fn):
    s2 = ext + shift_fn(ext, 1)
    s4 = s2 + shift_fn(s2, 2)
    s8 = s4 + shift_fn(s4, 4)
    s16 = s8 + shift_fn(s8, 8)
    return s2, s4, s8, s16


def _select_group(lane_grp, s2, s4, s8, s16):
    return jnp.where(lane_grp == 0, s2, jnp.where(lane_grp == 1, s4, jnp.where(lane_grp == 2, s8, s16)))


def _pooled_tile(u_ref, h_ref, i, T):
    halo = jnp.where(i > 0, h_ref[...], 0.0)
    ext = jnp.concatenate([halo, u_ref[...]], axis=0)
    n = T + HALO
    sums = _window_sums(ext, lambda v, k: pltpu.roll(v, k, 0))
    inv, lane_grp = _pool_counts(i * T - HALO, n)
    pooled = _select_group(lane_grp, *sums) * inv - ext
    return pooled[HALO:, :]


def pool_fwd(proj, w_pool, scale, *, name):
    S = proj.shape[0]
    T = _tile(S)
    hb = T // HALO

    def body(u_ref, h_ref, w_ref, s_ref, o_ref):
        i = pl.program_id(0)
        pooled = _pooled_tile(u_ref, h_ref, i, T).astype(BF16)
        outs = [_dot(pooled[:, g * GROUP_DIM:(g + 1) * GROUP_DIM], w_ref[g], 1, 0) for g in range(4)]
        o_ref[...] = (jnp.concatenate(outs, axis=1) * s_ref[...]).astype(BF16)

    return pl.pallas_call(
        body,
        out_shape=jax.ShapeDtypeStruct((S, WIDTH), BF16),
        grid=(S // T,),
        in_specs=[pl.BlockSpec((T, WIDTH), lambda i: (i, U_COLBLK)),
                  pl.BlockSpec((HALO, WIDTH), lambda i: (jnp.maximum(i * hb - 1, 0), U_COLBLK)),
                  pl.BlockSpec((4, GROUP_DIM, GROUP_DIM), lambda i: (0, 0, 0)),
                  pl.BlockSpec((1, WIDTH), lambda i: (0, 0))],
        out_specs=pl.BlockSpec((T, WIDTH), lambda i: (i, 0)),
        compiler_params=_cp(("parallel",)),
        name=name,
    )(proj, proj, w_pool, scale)


def pool_bwd(proj, w_pool, scale, do, *, name):
    S = proj.shape[0]
    T = _tile(S)
    hb = T // HALO
    nt = S // T

    def body(u_ref, h_ref, w_ref, s_ref, do_ref, dof_ref, du_ref, dw_ref, ds_ref):
        i = pl.program_id(0)

        @pl.when(i == 0)
        def _():
            dw_ref[...] = jnp.zeros_like(dw_ref)
            ds_ref[...] = jnp.zeros_like(ds_ref)

        pooled = _pooled_tile(u_ref, h_ref, i, T).astype(BF16)
        dov = do_ref[...].astype(F32)
        fut = jnp.where(i < nt - 1, dof_ref[...].astype(F32), 0.0)
        dmix = (jnp.concatenate([dov, fut], axis=0) * s_ref[...]).astype(BF16)
        mixed, dpool = [], []
        for g in range(4):
            lanes = slice(g * GROUP_DIM, (g + 1) * GROUP_DIM)
            mixed.append(_dot(pooled[:, lanes], w_ref[g], 1, 0))
            dw_ref[g] += _dot(pooled[:, lanes], dmix[:T, lanes], 0, 0)
            dpool.append(_dot(dmix[:, lanes], w_ref[g], 1, 1))
        ds_ref[...] += jnp.sum(dov * jnp.concatenate(mixed, axis=1), axis=0, keepdims=True)
        dp = jnp.concatenate(dpool, axis=1)
        n = T + HALO
        inv, lane_grp = _pool_counts(i * T, n)
        sums = _window_sums(dp * inv, lambda v, k: pltpu.roll(v, n - k, 0))
        du = _select_group(lane_grp, *sums) - dp
        du_ref[...] = du[:T, :].astype(BF16)

    row = pl.BlockSpec((T, WIDTH), lambda i: (i, 0))
    return pl.pallas_call(
        body,
        out_shape=(jax.ShapeDtypeStruct((S, WIDTH), BF16), jax.ShapeDtypeStruct((4, GROUP_DIM, GROUP_DIM), F32),
                   jax.ShapeDtypeStruct((1, WIDTH), F32)),
        grid=(nt,),
        in_specs=[pl.BlockSpec((T, WIDTH), lambda i: (i, U_COLBLK)),
                  pl.BlockSpec((HALO, WIDTH), lambda i: (jnp.maximum(i * hb - 1, 0), U_COLBLK)),
                  pl.BlockSpec((4, GROUP_DIM, GROUP_DIM), lambda i: (0, 0, 0)),
                  pl.BlockSpec((1, WIDTH), lambda i: (0, 0)),
                  row,
                  pl.BlockSpec((HALO, WIDTH), lambda i: (jnp.minimum((i + 1) * hb, S // HALO - 1), 0))],
        out_specs=(row, pl.BlockSpec((4, GROUP_DIM, GROUP_DIM), lambda i: (0, 0, 0)),
                   pl.BlockSpec((1, WIDTH), lambda i: (0, 0))),
        compiler_params=_cp(("arbitrary",)),
        name=name,
    )(proj, proj, w_pool, scale, do, do)


GATE_BLK0 = GATE_COL0 // WIDTH


def merge_fwd(oa, ob, oc, proj, b_gate, wa, wb, wc, *, name):
    S = oa.shape[0]
    T = _tile(S)

    def body(oa_ref, ob_ref, oc_ref, ga, gb, gc, ba, bb, bc, wa_ref, wb_ref, wc_ref, m_ref):
        acc = None
        for o_ref, g_ref, b_ref, w_ref in ((oa_ref, ga, ba, wa_ref), (ob_ref, gb, bb, wb_ref), (oc_ref, gc, bc, wc_ref)):
            y = _dot(o_ref[...], w_ref[...], 1, 0)
            t = jax.nn.sigmoid(g_ref[...] + b_ref[...]) * y
            acc = t if acc is None else acc + t
        m_ref[...] = acc.astype(BF16)

    row = pl.BlockSpec((T, WIDTH), lambda i, n: (i, 0))
    gate = lambda b: pl.BlockSpec((T, WIDTH), lambda i, n, b=b: (i, GATE_BLK0 + 2 * b + n))
    bias = lambda b: pl.BlockSpec((1, WIDTH), lambda i, n, b=b: (0, 2 * b + n))
    wspec = pl.BlockSpec((WIDTH, WIDTH), lambda i, n: (0, n))
    return pl.pallas_call(
        body,
        out_shape=jax.ShapeDtypeStruct((S, D_MODEL), BF16),
        grid=(S // T, 2),
        in_specs=[row, row, row, gate(0), gate(1), gate(2), bias(0), bias(1), bias(2), wspec, wspec, wspec],
        out_specs=pl.BlockSpec((T, WIDTH), lambda i, n: (i, n)),
        compiler_params=_cp(("parallel", "parallel")),
        name=name,
    )(oa, ob, oc, proj, proj, proj, b_gate, b_gate, b_gate, wa, wb, wc)


def merge_bwd(dm, oa, ob, oc, proj, b_gate, wa, wb, wc, *, name):
    S = oa.shape[0]
    T = _tile(S)

    def body(dm_ref, oa_ref, ob_ref, oc_ref, ga, gb, gc, ba, bb, bc, wa_ref, wb_ref, wc_ref,
             ta, tb, tc, dga, dgb, dgc, dba, dbb, dbc):
        i = pl.program_id(1)
        dmv = dm_ref[...].astype(F32)
        for o_ref, g_ref, b_ref, w_ref, t_ref, dg_ref, db_ref in (
                (oa_ref, ga, ba, wa_ref, ta, dga, dba), (ob_ref, gb, bb, wb_ref, tb, dgb, dbb),
                (oc_ref, gc, bc, wc_ref, tc, dgc, dbc)):
            y = _dot(o_ref[...], w_ref[...], 1, 0)
            gate = jax.nn.sigmoid(g_ref[...] + b_ref[...])
            t_ref[...] = (gate * dmv).astype(BF16)
            dgl = dmv * y * gate * (1.0 - gate)
            dg_ref[...] = dgl.astype(BF16)

            @pl.when(i == 0)
            def _():
                db_ref[...] = jnp.zeros_like(db_ref)

            db_ref[...] += jnp.sum(dgl, axis=0, keepdims=True)

    row = pl.BlockSpec((T, WIDTH), lambda n, i: (i, 0))
    half = pl.BlockSpec((T, WIDTH), lambda n, i: (i, n))
    gate = lambda b: pl.BlockSpec((T, WIDTH), lambda n, i, b=b: (i, GATE_BLK0 + 2 * b + n))
    bias = lambda b: pl.BlockSpec((1, WIDTH), lambda n, i, b=b: (0, 2 * b + n))
    wspec = pl.BlockSpec((WIDTH, WIDTH), lambda n, i: (0, n))
    bvec = pl.BlockSpec((1, WIDTH), lambda n, i: (0, n))
    act = jax.ShapeDtypeStruct((S, D_MODEL), BF16)
    vec = jax.ShapeDtypeStruct((1, D_MODEL), F32)
    return pl.pallas_call(
        body,
        out_shape=(act, act, act, act, act, act, vec, vec, vec),
        grid=(2, S // T),
        in_specs=[half, row, row, row, gate(0), gate(1), gate(2), bias(0), bias(1), bias(2), wspec, wspec, wspec],
        out_specs=(half, half, half, half, half, half, bvec, bvec, bvec),
        compiler_params=_cp(("parallel", "arbitrary")),
        name=name,
    )(dm, oa, ob, oc, proj, proj, proj, b_gate, b_gate, b_gate, wa, wb, wc)


FF_T = 256
FF_BLKS = D_FF // FF_T


def _silu_parts(x):
    s = jax.nn.sigmoid(x)
    return x * s, s


def _conv3(ext, w_ref, b_ref):
    taps = (pltpu.roll(ext, 2, 0), pltpu.roll(ext, 1, 0), ext)
    return b_ref[...] + w_ref[0:1, :] * taps[0] + w_ref[1:2, :] * taps[1] + w_ref[2:3, :] * taps[2], taps


def conv_glu_fwd(u, conv_w, conv_b, *, name):
    S = u.shape[0]
    T = _tile(S)
    hb = T // CONV_HALO

    def body(ug, ugh, uv, uvh, wg, wv, bg, bv, a_ref):
        i = pl.program_id(1)
        cs = []
        for m_ref, h_ref, w_ref, b_ref in ((ug, ugh, wg, bg), (uv, uvh, wv, bv)):
            halo = jnp.where(i > 0, h_ref[...], 0.0)
            ext = jnp.concatenate([halo, m_ref[...]], axis=0)
            cs.append(_conv3(ext, w_ref, b_ref)[0][CONV_HALO:, :])
        act, _ = _silu_parts(cs[0])
        a_ref[...] = (act * cs[1]).astype(BF16)

    main = lambda o: pl.BlockSpec((T, FF_T), lambda c, i, o=o: (i, c + o))
    halo = lambda o: pl.BlockSpec((CONV_HALO, FF_T), lambda c, i, o=o: (jnp.maximum(i * hb - 1, 0), c + o))
    wsp = lambda o: pl.BlockSpec((3, FF_T), lambda c, i, o=o: (0, c + o))
    bsp = lambda o: pl.BlockSpec((1, FF_T), lambda c, i, o=o: (0, c + o))
    return pl.pallas_call(
        body,
        out_shape=jax.ShapeDtypeStruct((S, D_FF), BF16),
        grid=(FF_BLKS, S // T),
        in_specs=[main(0), halo(0), main(FF_BLKS), halo(FF_BLKS), wsp(0), wsp(FF_BLKS), bsp(0), bsp(FF_BLKS)],
        out_specs=pl.BlockSpec((T, FF_T), lambda c, i: (i, c)),
        compiler_params=_cp(("parallel", "parallel")),
        name=name,
    )(u, u, u, u, conv_w, conv_w, conv_b, conv_b)


def conv_glu_bwd(u, conv_w, conv_b, da, *, name):
    S = u.shape[0]
    T = _tile(S)
    hb = T // CONV_HALO
    nt = S // T
    n = T + 2 * CONV_HALO

    def body(ug, ugp, ugf, uv, uvp, uvf, wg, wv, bg, bv, da_ref, daf_ref,
             dug, duv, dwg, dwv, dbg, dbv):
        i = pl.program_id(1)
        first, last = i == 0, i == nt - 1
        taps, cs = [], []
        for m_ref, p_ref, f_ref, w_ref, b_ref in ((ug, ugp, ugf, wg, bg), (uv, uvp, uvf, wv, bv)):
            ext = jnp.concatenate([jnp.where(first, 0.0, p_ref[...]), m_ref[...], jnp.where(last, 0.0, f_ref[...])], axis=0)
            c, tp = _conv3(ext, w_ref, b_ref)
            cs.append(c)
            taps.append(tp)
        dae = jnp.concatenate([jnp.zeros((CONV_HALO, FF_T), F32), da_ref[...].astype(F32),
                               jnp.where(last, 0.0, daf_ref[...].astype(F32))], axis=0)
        act, sg = _silu_parts(cs[0])
        dcs = (dae * cs[1] * (sg * (1.0 + cs[0] * (1.0 - sg))), dae * act)
        main = slice(CONV_HALO, CONV_HALO + T)
        for tp, dc, w_ref, du_ref, dw_ref, db_ref in ((taps[0], dcs[0], wg, dug, dwg, dbg),
                                                      (taps[1], dcs[1], wv, duv, dwv, dbv)):
            du = (w_ref[2:3, :] * dc + w_ref[1:2, :] * pltpu.roll(dc, n - 1, 0) + w_ref[0:1, :] * pltpu.roll(dc, n - 2, 0))
            du_ref[...] = du[main, :].astype(BF16)
            dcm = dc[main, :]
            rows = [jnp.sum(dcm * tp[j][main, :], axis=0, keepdims=True) for j in range(3)]

            @pl.when(first)
            def _():
                dw_ref[...] = jnp.zeros_like(dw_ref)
                db_ref[...] = jnp.zeros_like(db_ref)

            dw_ref[...] += jnp.concatenate(rows, axis=0)
            db_ref[...] += jnp.sum(dcm, axis=0, keepdims=True)

    main = lambda o: pl.BlockSpec((T, FF_T), lambda c, i, o=o: (i, c + o))
    past = lambda o: pl.BlockSpec((CONV_HALO, FF_T), lambda c, i, o=o: (jnp.maximum(i * hb - 1, 0), c + o))
    fut = lambda o: pl.BlockSpec((CONV_HALO, FF_T), lambda c, i, o=o: (jnp.minimum((i + 1) * hb, S // CONV_HALO - 1), c + o))
    wsp = lambda o: pl.BlockSpec((3, FF_T), lambda c, i, o=o: (0, c + o))
    bsp = lambda o: pl.BlockSpec((1, FF_T), lambda c, i, o=o: (0, c + o))
    return pl.pallas_call(
        body,
        out_shape=(jax.ShapeDtypeStruct((S, D_FF), BF16), jax.ShapeDtypeStruct((S, D_FF), BF16),
                   jax.ShapeDtypeStruct((3, D_FF), F32), jax.ShapeDtypeStruct((3, D_FF), F32),
                   jax.ShapeDtypeStruct((1, D_FF), F32), jax.ShapeDtypeStruct((1, D_FF), F32)),
        grid=(FF_BLKS, nt),
        in_specs=[main(0), past(0), fut(0), main(FF_BLKS), past(FF_BLKS), fut(FF_BLKS),
                  wsp(0), wsp(FF_BLKS), bsp(0), bsp(FF_BLKS), main(0), fut(0)],
        out_specs=(main(0), main(0), wsp(0), wsp(0), bsp(0), bsp(0)),
        compiler_params=_cp(("parallel", "arbitrary")),
        name=name,
    )(u, u, u, u, u, u, conv_w, conv_w, conv_b, conv_b, da, da)


def loss_head(y, target, *, name):
    S, D = y.shape
    T = _tile(S)

    def body(y_ref, t_ref, dy_ref, l_ref):
        i = pl.program_id(0)
        err = y_ref[...] - t_ref[...]
        dy_ref[...] = err * (1.0 / D)

        @pl.when(i == 0)
        def _():
            l_ref[...] = jnp.zeros_like(l_ref)

        l_ref[...] += 0.5 * jnp.sum(jnp.mean(err * err, axis=-1, keepdims=True))

    row = pl.BlockSpec((T, D), lambda i: (i, 0))
    return pl.pallas_call(
        body,
        out_shape=(jax.ShapeDtypeStruct((S, D), F32), jax.ShapeDtypeStruct((8, 128), F32)),
        grid=(S // T,),
        in_specs=[row, row],
        out_specs=(row, pl.BlockSpec((8, 128), lambda i: (0, 0))),
        compiler_params=_cp(("arbitrary",)),
        name=name,
    )(y, target)


ELEMS_PER_BLOCK = 512 * 1024


def _rows_tile(rows, cols):
    if rows * cols <= ELEMS_PER_BLOCK or rows % 8:
        return rows
    best = 8
    for tr in range(8, rows + 1, 8):
        if rows % tr == 0 and tr * cols <= ELEMS_PER_BLOCK:
            best = tr
    return best


def _adamw_math(g, w_ref, m_ref, v_ref, g_out, d_out, m_out, v_out):
    mn = ADAM_B1 * m_ref[...] + (1.0 - ADAM_B1) * g
    vn = ADAM_B2 * v_ref[...] + (1.0 - ADAM_B2) * (g * g)
    m_hat = mn / (1.0 - ADAM_B1 ** ADAM_STEP)
    v_hat = vn / (1.0 - ADAM_B2 ** ADAM_STEP)
    g_out[...] = g
    d_out[...] = -ADAM_LR * (m_hat / (jnp.sqrt(v_hat) + ADAM_EPS) + ADAM_WD * w_ref[...])
    m_out[...] = mn
    v_out[...] = vn


def adamw(w, m, v, g, *, name):
    rows, cols = w.shape
    tr = _rows_tile(rows, cols)

    def body(w_ref, m_ref, v_ref, g_ref, g_out, d_out, m_out, v_out):
        _adamw_math(g_ref[...], w_ref, m_ref, v_ref, g_out, d_out, m_out, v_out)

    spec = pl.BlockSpec((tr, cols), lambda i: (i, 0))
    shp = jax.ShapeDtypeStruct((rows, cols), F32)
    return pl.pallas_call(
        body,
        out_shape=(shp, shp, shp, shp),
        grid=(rows // tr,),
        in_specs=[spec] * 4,
        out_specs=(spec, spec, spec, spec),
        compiler_params=_cp(("parallel",)),
        name=name,
    )(w, m, v, g)


ANY = pl.BlockSpec(memory_space=pl.ANY)
STAGE_BYTES = 2 * 1024 * 1024


def _mesh_pos():
    return lax.axis_index("x"), lax.axis_index("y"), lax.axis_index("c")


def _chip_peers(x, y):
    return [(1 - x, y), (x, 1 - y), (1 - x, 1 - y)]


def _all_peers(x, y, c):
    return [((1 - x) if (r >> 2) & 1 else x, (1 - y) if (r >> 1) & 1 else y, (1 - c) if r & 1 else c)
            for r in range(1, 8)]


class LayerGather:
    def __init__(self, shards, axes, layer):
        self.nt = len(shards)
        self.axes = list(axes)
        self.layer = layer
        self.shapes = [s.shape for s in shards]
        self.dtypes = [s.dtype for s in shards]
        self.sizes = [s.shape[a] for s, a in zip(shards, axes)]
        self.split = [s.shape[1] % 32 == 0 for s in shards]
        self.half_rows = [s.shape[1] // 2 if sp else s.shape[1] for s, sp in zip(shards, self.split)]
        self.chunk_rows = []
        for s in shards:
            rt = s.shape[1]
            while rt % 32 == 0 and rt * s.shape[2] * s.dtype.itemsize > STAGE_BYTES:
                rt //= 2
            self.chunk_rows.append(rt)

    def out_shapes(self):
        out = []
        for shp, a, sz, dt in zip(self.shapes, self.axes, self.sizes, self.dtypes):
            shp = list(shp)
            shp[a] = 4 * sz
            out.append(jax.ShapeDtypeStruct(tuple(shp), dt))
        return out

    def scratch_shapes(self):
        return ([pltpu.VMEM((1, rt, shp[2]), dt) for shp, rt, dt in zip(self.shapes, self.chunk_rows, self.dtypes)]
                + [pltpu.SemaphoreType.DMA((2 * self.nt,))] + [pltpu.SemaphoreType.DMA((3 * self.nt,)) for _ in range(4)])

    def _views(self, ins, outs, scratch):
        nt = self.nt
        stage, stage_sems = scratch[:nt], scratch[nt]
        ici_send, ici_recv, d2d_send, d2d_recv = scratch[nt + 1:]
        x, y, c = _mesh_pos()
        mine = 2 * x + y
        peers = _chip_peers(x, y)
        layer = pl.ds(self.layer, 1)

        def rows(t, half, r0=0, n=None):
            hr = self.half_rows[t]
            if n is None:
                return pl.ds(pl.multiple_of(half * hr, 16), hr) if self.split[t] else pl.ds(0, hr)
            return pl.ds(r0, n)

        def placed(t, blk, row_sel, row_len):
            sz = self.sizes[t]
            if self.axes[t] == 2:
                return outs[t].at[layer, row_sel, pl.ds(pl.multiple_of(blk * sz, 128), sz)]
            return outs[t].at[layer, pl.ds(pl.multiple_of(blk * sz, 16) + row_sel.start, row_len), :]

        def ici(t, k, blk):
            px, py = peers[k]
            sel = rows(t, c)
            return pltpu.make_async_remote_copy(
                src_ref=ins[t].at[layer, sel, :], dst_ref=placed(t, blk, sel, self.half_rows[t]),
                send_sem=ici_send.at[3 * t + k], recv_sem=ici_recv.at[3 * t + k],
                device_id=(px, py, c), device_id_type=MESH_T)

        def d2d(t, k, half):
            px, py = peers[k]
            piece = placed(t, 2 * px + py, rows(t, half), self.half_rows[t])
            return pltpu.make_async_remote_copy(
                src_ref=piece, dst_ref=piece, send_sem=d2d_send.at[3 * t + k], recv_sem=d2d_recv.at[3 * t + k],
                device_id=(x, y, 1 - c), device_id_type=MESH_T)

        def own_chunk(t, r0):
            rt = self.chunk_rows[t]
            sel = pl.ds(r0, rt)
            return ins[t].at[layer, sel, :], placed(t, mine, sel, rt), stage[t], stage_sems

        return c, mine, peers, ici, d2d, own_chunk

    def start(self, ins, outs, scratch):
        c, mine, peers, ici, d2d, own_chunk = self._views(ins, outs, scratch)
        for t in range(self.nt):
            for k in range(3):
                ici(t, k, mine).start()
        starts = [list(range(0, self.shapes[t][1], self.chunk_rows[t])) for t in range(self.nt)]
        for r in range(max(len(s) for s in starts)):
            active = [(t, *own_chunk(t, starts[t][r])) for t in range(self.nt) if r < len(starts[t])]
            loads = [pltpu.make_async_copy(src, buf, sems.at[2 * t]) for t, src, dst, buf, sems in active]
            for cp in loads:
                cp.start()
            for cp in loads:
                cp.wait()
            stores = [pltpu.make_async_copy(buf, dst, sems.at[2 * t + 1]) for t, src, dst, buf, sems in active]
            for cp in stores:
                cp.start()
            for cp in stores:
                cp.wait()

    def finish(self, ins, outs, scratch):
        c, mine, peers, ici, d2d, own_chunk = self._views(ins, outs, scratch)
        for t in range(self.nt):
            for k, (px, py) in enumerate(peers):
                ici(t, k, 2 * px + py).wait_recv()
                if self.split[t]:
                    d2d(t, k, c).start()
        for t in range(self.nt):
            for k in range(3):
                if self.split[t]:
                    d2d(t, k, 1 - c).wait_recv()
        for t in range(self.nt):
            for k in range(3):
                ici(t, k, mine).wait_send()
                if self.split[t]:
                    d2d(t, k, c).wait_send()


def all_gather_layer(shards, axes, layer, *, name):
    plan = LayerGather(shards, axes, layer)
    nt = plan.nt

    def body(*refs):
        ins, outs, scratch = refs[:nt], refs[nt:2 * nt], refs[2 * nt:]
        plan.start(ins, outs, scratch)
        plan.finish(ins, outs, scratch)

    return pl.pallas_call(
        body,
        out_shape=tuple(plan.out_shapes()),
        in_specs=[ANY] * nt,
        out_specs=tuple([ANY] * nt),
        scratch_shapes=plan.scratch_shapes(),
        name=name,
    )(*shards)


class HalfLayout:
    def __init__(self, shape, axis):
        self.R, self.C = shape
        self.axis = axis
        if axis == 1:
            self.hr, self.pw = self.R // 2, self.C // 4
            self.half_shape = (self.hr, self.C)
        else:
            self.hr, self.pw = self.R // 8, self.C
            self.half_shape = (4 * self.hr, self.C)
        self.tr = _rows_tile(self.hr, self.pw)
        self.nr = self.hr // self.tr

    def in_grad(self, ref, blk, half):
        if self.axis == 1:
            return ref.at[pl.ds(pl.multiple_of(half * self.hr, 16), self.hr), pl.ds(pl.multiple_of(blk * self.pw, 128), self.pw)]
        return ref.at[pl.ds(pl.multiple_of((2 * blk + half) * self.hr, 16), self.hr), :]

    def in_half(self, ref, blk):
        if self.axis == 1:
            return ref.at[:, pl.ds(pl.multiple_of(blk * self.pw, 128), self.pw)]
        return ref.at[pl.ds(pl.multiple_of(blk * self.hr, 16), self.hr), :]

    def grad_spec(self):
        if self.axis == 1:
            return pl.BlockSpec((self.tr, self.pw), lambda j, i, s: (s[0] * self.nr + i, j))
        return pl.BlockSpec((self.tr, self.pw), lambda j, i, s: ((2 * j + s[0]) * self.nr + i, 0))

    def half_spec(self):
        if self.axis == 1:
            return pl.BlockSpec((self.tr, self.pw), lambda j, i, s: (i, j))
        return pl.BlockSpec((self.tr, self.pw), lambda j, i, s: (j * self.nr + i, 0))


def half_exchange(grads, layouts, *, name):
    nt = len(grads)
    pieces = [(t, j) for t in range(nt) for j in (range(4) if layouts[t].axis == 0 else range(1))]

    def body(*refs):
        ins, outs = refs[:nt], refs[nt:2 * nt]
        send_sems, recv_sems = refs[2 * nt:]
        x, y, c = _mesh_pos()
        cps = []
        for n, (t, j) in enumerate(pieces):
            lay = layouts[t]
            if lay.axis == 1:
                src = ins[t].at[pl.ds(pl.multiple_of((1 - c) * lay.hr, 16), lay.hr), :]
                dst = outs[t]
            else:
                src = lay.in_grad(ins[t], j, 1 - c)
                dst = lay.in_half(outs[t], j)
            cp = pltpu.make_async_remote_copy(src_ref=src, dst_ref=dst, send_sem=send_sems.at[n], recv_sem=recv_sems.at[n],
                                              device_id=(x, y, 1 - c), device_id_type=MESH_T)
            cp.start()
            cps.append(cp)
        for cp in cps:
            cp.wait_recv()
        for cp in cps:
            cp.wait_send()

    return pl.pallas_call(
        body,
        out_shape=tuple(jax.ShapeDtypeStruct(lay.half_shape, F32) for lay in layouts),
        in_specs=[ANY] * nt,
        out_specs=tuple([ANY] * nt),
        scratch_shapes=[pltpu.SemaphoreType.DMA((len(pieces),)), pltpu.SemaphoreType.DMA((len(pieces),))],
        name=name,
    )(*grads)


def pair_sum(grad, other, lay, core, *, name):
    def body(c_ref, g_ref, o_ref, s32_ref, s16_ref):
        s = g_ref[...] + o_ref[...]
        s32_ref[...] = s
        s16_ref[...] = s.astype(BF16)

    return pl.pallas_call(
        body,
        out_shape=(jax.ShapeDtypeStruct(lay.half_shape, F32), jax.ShapeDtypeStruct(lay.half_shape, BF16)),
        grid_spec=pltpu.PrefetchScalarGridSpec(
            num_scalar_prefetch=1, grid=(4, lay.nr),
            in_specs=[lay.grad_spec(), lay.half_spec()],
            out_specs=(lay.half_spec(), lay.half_spec())),
        compiler_params=_cp(("parallel", "parallel")),
        name=name,
    )(core, grad, other)


class BlockScatter:
    def __init__(self, layouts):
        self.layouts = layouts
        self.nt = len(layouts)

    def out_shapes(self):
        return [jax.ShapeDtypeStruct((3, lay.hr, lay.pw), BF16) for lay in self.layouts]

    def scratch_shapes(self):
        return [pltpu.SemaphoreType.DMA((3 * self.nt,)), pltpu.SemaphoreType.DMA((3 * self.nt,))]

    def _copies(self, pairs16, recv, scratch):
        send_sems, recv_sems = scratch
        x, y, c = _mesh_pos()
        return [pltpu.make_async_remote_copy(
            src_ref=lay.in_half(pairs16[t], 2 * px + py), dst_ref=recv[t].at[k],
            send_sem=send_sems.at[3 * t + k], recv_sem=recv_sems.at[3 * t + k],
            device_id=(px, py, c), device_id_type=MESH_T)
            for t, lay in enumerate(self.layouts) for k, (px, py) in enumerate(_chip_peers(x, y))]

    def start(self, pairs16, recv, scratch):
        for cp in self._copies(pairs16, recv, scratch):
            cp.start()

    def finish(self, pairs16, recv, scratch):
        copies = self._copies(pairs16, recv, scratch)
        for cp in copies:
            cp.wait_recv()
        for cp in copies:
            cp.wait_send()


def gather_small(small, *, name):
    def body(small_in, small_out, ssend, srecv):
        x, y, c = _mesh_pos()
        me = 4 * x + 2 * y + c
        sends, recvs = [], []
        for r, (px, py, pc) in enumerate(_all_peers(x, y, c)):
            def mk(slot, r=r, px=px, py=py, pc=pc):
                return pltpu.make_async_remote_copy(
                    src_ref=small_in, dst_ref=small_out.at[slot], send_sem=ssend.at[r], recv_sem=srecv.at[r],
                    device_id=(px, py, pc), device_id_type=MESH_T)
            snd = mk(me)
            snd.start()
            sends.append(snd)
            recvs.append(mk(4 * px + 2 * py + pc))
        for r in recvs:
            r.wait_recv()
        for s in sends:
            s.wait_send()

    return pl.pallas_call(
        body,
        out_shape=jax.ShapeDtypeStruct((8,) + small.shape, F32),
        in_specs=[ANY],
        out_specs=ANY,
        scratch_shapes=[pltpu.SemaphoreType.DMA((7,)), pltpu.SemaphoreType.DMA((7,))],
        name=name,
    )(small)


def sum_chips(pair32, recv, lay, chip, *, name):
    def body(j_ref, p_ref, r_ref, s_ref):
        acc = p_ref[...]
        for k in range(3):
            acc = acc + r_ref[k].astype(F32)
        s_ref[...] = acc

    if lay.axis == 1:
        own = pl.BlockSpec((lay.tr, lay.pw), lambda i, j: (i, j[0]))
    else:
        own = pl.BlockSpec((lay.tr, lay.pw), lambda i, j: (j[0] * lay.nr + i, 0))
    return pl.pallas_call(
        body,
        out_shape=jax.ShapeDtypeStruct((lay.hr, lay.pw), F32),
        grid_spec=pltpu.PrefetchScalarGridSpec(
            num_scalar_prefetch=1, grid=(lay.nr,),
            in_specs=[own, pl.BlockSpec((3, lay.tr, lay.pw), lambda i, j: (0, i, 0))],
            out_specs=pl.BlockSpec((lay.tr, lay.pw), lambda i, j: (i, 0))),
        compiler_params=_cp(("parallel",)),
        name=name,
    )(chip, pair32, recv)


def sum_devices(gathered, own, me, *, name):
    _, R, C = gathered.shape

    def body(me_ref, g_ref, o_ref, s_ref):
        acc = None
        for k in range(8):
            part = jnp.where(me_ref[0] == k, o_ref[...], g_ref[k])
            acc = part if acc is None else acc + part
        s_ref[...] = acc

    return pl.pallas_call(
        body,
        out_shape=jax.ShapeDtypeStruct((R, C), F32),
        grid_spec=pltpu.PrefetchScalarGridSpec(
            num_scalar_prefetch=1, grid=(1,),
            in_specs=[pl.BlockSpec((8, R, C), lambda i, m: (0, 0, 0)), pl.BlockSpec((R, C), lambda i, m: (0, 0))],
            out_specs=pl.BlockSpec((R, C), lambda i, m: (0, 0))),
        compiler_params=_cp(("arbitrary",)),
        name=name,
    )(me, gathered, own)


def sibling_swap(parts, *, name):
    nt = len(parts)

    def body(*refs):
        ins, outs = refs[:nt], refs[nt:2 * nt]
        send_sems, recv_sems = refs[2 * nt:]
        x, y, c = _mesh_pos()
        cps = []
        for t in range(nt):
            cp = pltpu.make_async_remote_copy(src_ref=ins[t], dst_ref=outs[t], send_sem=send_sems.at[t],
                                              recv_sem=recv_sems.at[t], device_id=(x, y, 1 - c), device_id_type=MESH_T)
            cp.start()
            cps.append(cp)
        for cp in cps:
            cp.wait_recv()
        for cp in cps:
            cp.wait_send()

    return pl.pallas_call(
        body,
        out_shape=tuple(jax.ShapeDtypeStruct(p.shape, p.dtype) for p in parts),
        in_specs=[ANY] * nt,
        out_specs=tuple([ANY] * nt),
        scratch_shapes=[pltpu.SemaphoreType.DMA((nt,)), pltpu.SemaphoreType.DMA((nt,))],
        name=name,
    )(*parts)


def adamw_halves(w, m, v, mine, other, lay, core, *, name):
    _, r, c = w.shape
    tr, nr = lay.tr, lay.nr
    assert (r, c) == (2 * lay.hr, lay.pw), (w.shape, lay.hr, lay.pw)

    def body(c_ref, w_ref, m_ref, v_ref, *rest):
        g_refs, outs = rest[:2 * DEPTH], rest[2 * DEPTH:]
        l, h = pl.program_id(0), pl.program_id(1)
        g = None
        for d in range(DEPTH):
            gd = jnp.where(h == c_ref[0], g_refs[d][...], g_refs[DEPTH + d][...])
            g = gd if g is None else jnp.where(l == d, gd, g)
        _adamw_math(g, w_ref, m_ref, v_ref, *outs)

    full = pl.BlockSpec((None, tr, c), lambda l, h, i, s: (l, h * nr + i, 0))

    def part(d, is_mine):
        def index(l, h, i, s):
            used = jnp.logical_and(l == d, (h == s[0]) == is_mine)
            return jnp.where(used, i, 0), 0
        return pl.BlockSpec((tr, c), index)

    shp = jax.ShapeDtypeStruct((DEPTH, r, c), F32)
    return pl.pallas_call(
        body,
        out_shape=(shp, shp, shp, shp),
        grid_spec=pltpu.PrefetchScalarGridSpec(
            num_scalar_prefetch=1, grid=(DEPTH, 2, nr),
            in_specs=[full, full, full] + [part(d, True) for d in range(DEPTH)] + [part(d, False) for d in range(DEPTH)],
            out_specs=(full, full, full, full)),
        compiler_params=_cp(("arbitrary", "arbitrary", "arbitrary")),
        name=name,
    )(core, w, m, v, *mine, *other)


WEIGHTS = ("norm_mix", "w_in", "b_gate", "q_norm_a", "k_norm_a", "rel_bias_a", "w_pool", "pool_scale",
           "w_branch_a", "w_branch_b", "w_branch_c", "w_out", "norm_ffn", "w_up", "conv_w", "conv_b", "w_down")
SHARDED = {"w_in": 2, "w_branch_a": 2, "w_branch_b": 2, "w_branch_c": 2, "w_out": 1, "w_up": 2, "conv_w": 2,
           "w_down": 1}
REPLICATED = tuple(n for n in WEIGHTS if n not in SHARDED)
MATMUL_WEIGHTS = tuple(n for n in SHARDED if n != "conv_w")
SMALL_WEIGHTS = tuple(n for n in WEIGHTS if n not in MATMUL_WEIGHTS)
SMALL_ROWS = 1496


def _layer_fwd(x, p, full, tables, rest=None, prefetch=None):
    l = p["l"]
    diag = exact_dot(p["rel_bias_a"], tables["onehot_t"], name="bias_diagonals")
    diag = diag.reshape(N_HEADS, N_VARIANTS, 1, DIAG_W).transpose(1, 0, 2, 3)
    biasm = bias_expand(diag, name="bias_expand")
    gq8 = jnp.tile(p["q_norm_a"], N_HEADS)[None]
    gk8 = jnp.tile(p["k_norm_a"], N_HEADS)[None]
    h = rmsnorm_fwd(x, p["norm_mix"][None], name="rmsnorm_fwd")
    if rest is None:
        proj = matmul(h, full["w_in"], b_layer=l, name="mm_in")
    else:
        plan, shards, names = rest
        proj, gathered = matmul(h, full["w_in"], b_layer=l, behind=(plan, shards), name="mm_in_gather")
        full = {**full, **dict(zip(names, gathered))}
    qa, ka, va, qb, kb, vb = qkv_prep(proj, gq8, gk8, name="qkv_prep")
    oa = attn_a_fwd(qa, ka, va, biasm, name="attn_a_fwd")
    if prefetch is None:
        ob, tot, nblk = attn_b_fwd(qb, kb, vb, name="attn_b_fwd")
    else:
        plan, shards, names = prefetch
        ob, tot, nblk, filled = attn_b_fwd(qb, kb, vb, gather=(plan, shards, [full[n] for n in names]),
                                           name="attn_b_fwd_gather")
        full = dict(zip(names, filled))
    wpool = p["w_pool"].astype(BF16)
    oc = pool_fwd(proj, wpool, p["pool_scale"][None], name="pool_fwd")
    merged = merge_fwd(oa, ob, oc, proj, p["b_gate"][None], full["w_branch_a"][l], full["w_branch_b"][l],
                       full["w_branch_c"][l], name="merge_fwd")
    x1 = matmul(merged, full["w_out"], b_layer=l, add=x, name="mm_out")
    h2 = rmsnorm_fwd(x1, p["norm_ffn"][None], name="rmsnorm_fwd")
    u = matmul(h2, full["w_up"], b_layer=l, name="mm_up")
    a = conv_glu_fwd(u, full["conv_w"][l], p["conv_b"][None], name="conv_glu_fwd")
    x2 = matmul(a, full["w_down"], b_layer=l, add=x1, name="mm_down")
    saved = dict(x=x, h=h, proj=proj, qa=qa, ka=ka, va=va, qb=qb, kb=kb, vb=vb, oa=oa, ob=ob, tot=tot, nblk=nblk, oc=oc,
                 merged=merged, x1=x1, h2=h2, u=u, a=a, biasm=biasm, gq8=gq8, gk8=gk8, wpool=wpool)
    return x2, saved, full


class GradReducer:
    def __init__(self, layouts, core, chip):
        self.layouts, self.core, self.chip = layouts, core, chip
        self.pairs32, self.received, self.pending = {}, {}, []

    def prepare(self, layer, grads):
        names = list(grads)
        lays = [self.layouts[n] for n in names]
        others = half_exchange([grads[n] for n in names], lays, name="half_exchange")
        for n, lay, other in zip(names, lays, others):
            p32, p16 = pair_sum(grads[n], other, lay, self.core, name="pair_sum")
            self.pairs32[(layer, n)] = p32
            self.pending.append(((layer, n), p16))

    def take(self):
        keys = [k for k, _ in self.pending]
        pairs16 = [p for _, p in self.pending]
        self.pending = []
        return keys, BlockScatter([self.layouts[n] for _, n in keys]), pairs16

    def store(self, keys, received):
        self.received.update(zip(keys, received))

    def finish(self):
        keys = list(self.pairs32)
        mine = [sum_chips(self.pairs32[k], self.received[k], self.layouts[k[1]], self.chip, name="sum_chips") for k in keys]
        other = sibling_swap(mine, name="sibling_swap")
        return dict(zip(keys, mine)), dict(zip(keys, other))


EARLY_WEIGHTS = ("w_down", "w_up", "w_out", "w_branch_a", "w_branch_b", "w_branch_c")


def _layer_bwd(dx2, s, p, tables, reducer):
    g = {}
    full, l = p["full"], p["l"]
    da = matmul(dx2, full["w_down"], b_layer=l, tb=True, name="mm_down_dx")
    g["w_down"] = matmul(s["a"], dx2, ta=True, name="mm_down_dw")
    dug, duv, dcwg, dcwv, dcbg, dcbv = conv_glu_bwd(s["u"], p["conv_w"], p["conv_b"][None], da, name="conv_glu_bwd")
    du = jnp.concatenate([dug, duv], axis=1)
    g["conv_w"] = jnp.concatenate([dcwg, dcwv], axis=1)
    g["conv_b"] = jnp.concatenate([dcbg, dcbv], axis=1)[0]
    g["w_up"] = matmul(s["h2"], du, ta=True, name="mm_up_dw")
    dx1, dg2 = matmul(du, full["w_up"], b_layer=l, tb=True, norm_bwd=(s["x1"], p["norm_ffn"][None], dx2),
                      name="mm_up_dx_norm")
    g["norm_ffn"] = dg2[0]
    dmerged = matmul(dx1, full["w_out"], b_layer=l, tb=True, name="mm_out_dx")
    g["w_out"] = matmul(s["merged"], dx1, ta=True, name="mm_out_dw")
    t_a, t_b, t_c, dga, dgb, dgc, dba, dbb, dbc = merge_bwd(
        dmerged, s["oa"], s["ob"], s["oc"], s["proj"], p["b_gate"][None], p["w_branch_a"], p["w_branch_b"],
        p["w_branch_c"], name="merge_bwd")
    g["b_gate"] = jnp.concatenate([dba, dbb, dbc], axis=1)[0]
    g["w_branch_a"] = matmul(s["oa"], t_a, ta=True, name="mm_branch_dw")
    g["w_branch_b"] = matmul(s["ob"], t_b, ta=True, name="mm_branch_dw")
    g["w_branch_c"] = matmul(s["oc"], t_c, ta=True, name="mm_branch_dw")
    doa = matmul(t_a, full["w_branch_a"], b_layer=l, tb=True, out_dtype=BF16, name="mm_branch_dx")
    dob = matmul(t_b, full["w_branch_b"], b_layer=l, tb=True, out_dtype=BF16, name="mm_branch_dx")
    doc = matmul(t_c, full["w_branch_c"], b_layer=l, tb=True, name="mm_branch_dx_f32")
    dqh, dkh, dva, dbias = attn_a_bwd(s["qa"], s["ka"], s["va"], s["biasm"], doa, name="attn_a_bwd")
    ddiag = relbias_reduce(dbias, name="relbias_reduce")
    ddiag = ddiag.transpose(1, 0, 2, 3).reshape(N_HEADS, N_VARIANTS * DIAG_W)
    g["rel_bias_a"] = exact_dot(ddiag, tables["onehot"], name="relbias_table")
    dqa, dka, dgq8, dgk8 = qknorm_bwd(s["proj"], s["gq8"], s["gk8"], dqh, dkh, name="qknorm_bwd")
    g["q_norm_a"] = dgq8.reshape(N_HEADS, HEAD_DIM).sum(axis=0)
    g["k_norm_a"] = dgk8.reshape(N_HEADS, HEAD_DIM).sum(axis=0)
    reducer.prepare(l, {n: g[n] for n in EARLY_WEIGHTS})
    keys, plan, pairs16 = reducer.take()
    dqb, dkb, dvb, received = attn_b_bwd(s["qb"], s["kb"], s["vb"], s["tot"], s["nblk"], dob, scatter=(plan, pairs16),
                                         name="attn_b_bwd_scatter")
    reducer.store(keys, received)
    duc, dwp, dsc = pool_bwd(s["proj"], s["wpool"], p["pool_scale"][None], doc, name="pool_bwd")
    g["w_pool"] = dwp
    g["pool_scale"] = dsc[0]
    dproj = jnp.concatenate([dqa, dka, dva.astype(BF16), dqb, dkb.astype(BF16), dvb.astype(BF16), duc,
                             dga, dgb, dgc], axis=1)
    g["w_in"] = matmul(s["h"], dproj, ta=True, name="mm_in_dw")
    reducer.prepare(l, {"w_in": g["w_in"]})
    norm = (s["x"], p["norm_mix"][None], dx1)
    if l > 0:
        dx, dg1 = matmul(dproj, full["w_in"], b_layer=l, tb=True, norm_bwd=norm, name="mm_in_dx_norm")
    else:
        keys, plan, pairs16 = reducer.take()
        dx, dg1, received = matmul(dproj, full["w_in"], b_layer=l, tb=True, norm_bwd=norm, behind=(plan, pairs16),
                                   name="mm_in_dx_norm_scatter")
        reducer.store(keys, received)
    g["norm_mix"] = dg1[0]
    return dx, g


def kernel(x, norm_mix, w_in, b_gate, q_norm_a, k_norm_a, rel_bias_a, w_pool, pool_scale, w_branch_a, w_branch_b, w_branch_c, w_out, norm_ffn, w_up, conv_w, conv_b, w_down, loss_target, m_norm_mix, m_w_in, m_b_gate, m_q_norm_a, m_k_norm_a, m_rel_bias_a, m_w_pool, m_pool_scale, m_w_branch_a, m_w_branch_b, m_w_branch_c, m_w_out, m_norm_ffn, m_w_up, m_conv_w, m_conv_b, m_w_down, v_norm_mix, v_w_in, v_b_gate, v_q_norm_a, v_k_norm_a, v_rel_bias_a, v_w_pool, v_pool_scale, v_w_branch_a, v_w_branch_b, v_w_branch_c, v_w_out, v_norm_ffn, v_w_up, v_conv_w, v_conv_b, v_w_down):
    w = dict(zip(WEIGHTS, (norm_mix, w_in, b_gate, q_norm_a, k_norm_a, rel_bias_a, w_pool, pool_scale, w_branch_a,
                           w_branch_b, w_branch_c, w_out, norm_ffn, w_up, conv_w, conv_b, w_down)))
    m = dict(zip(WEIGHTS, (m_norm_mix, m_w_in, m_b_gate, m_q_norm_a, m_k_norm_a, m_rel_bias_a, m_w_pool, m_pool_scale,
                           m_w_branch_a, m_w_branch_b, m_w_branch_c, m_w_out, m_norm_ffn, m_w_up, m_conv_w, m_conv_b,
                           m_w_down)))
    v = dict(zip(WEIGHTS, (v_norm_mix, v_w_in, v_b_gate, v_q_norm_a, v_k_norm_a, v_rel_bias_a, v_w_pool, v_pool_scale,
                           v_w_branch_a, v_w_branch_b, v_w_branch_c, v_w_out, v_norm_ffn, v_w_up, v_conv_w, v_conv_b,
                           v_w_down)))
    onehot = diagonal_onehot()
    tables = dict(onehot=jnp.asarray(onehot), onehot_t=jnp.asarray(np.ascontiguousarray(onehot.T)))

    names = tuple(SHARDED)
    shards = [w[n] if n == "conv_w" else w[n].astype(BF16) for n in names]
    axes = [SHARDED[n] for n in names]
    later = [i for i, n in enumerate(names) if n != "w_in"]
    rest = (LayerGather([shards[i] for i in later], [axes[i] for i in later], 0), [shards[i] for i in later],
            [names[i] for i in later])
    first = names.index("w_in")
    full = {"w_in": all_gather_layer([shards[first]], [axes[first]], 0, name="all_gather_layer")[0]}

    def layer_params(l):
        p = {n: full[n][l] for n in ("w_branch_a", "w_branch_b", "w_branch_c", "conv_w")}
        p.update({n: w[n][l] for n in REPLICATED})
        p.update(full=full, l=l)
        return p

    xs = x[0]
    saved = []
    for l in range(DEPTH):
        prefetch = (LayerGather(shards, axes, l + 1), shards, names) if l + 1 < DEPTH else None
        replicated = {n: w[n][l] for n in REPLICATED}
        xs, s, full = _layer_fwd(xs, dict(replicated, l=l), full, tables, rest if l == 0 else None, prefetch)
        saved.append(s)
    dx, lpart = loss_head(xs, loss_target[0], name="loss_head")
    loss = lax.psum(lpart[0, 0], MESH_AXES)
    as_index = lambda i: jnp.reshape(i, (1,)).astype(jnp.int32)
    cx, cy, cc = _mesh_pos()
    core, chip, me = as_index(cc), as_index(2 * cx + cy), as_index(4 * cx + 2 * cy + cc)
    layouts = {n: HalfLayout((full[n].shape[1], full[n].shape[2]), SHARDED[n] - 1) for n in MATMUL_WEIGHTS}
    reducer = GradReducer(layouts, core, chip)
    grads = [None] * DEPTH
    for l in reversed(range(DEPTH)):
        dx, grads[l] = _layer_bwd(dx, saved[l], layer_params(l), tables, reducer)

    g = {n: jnp.stack([grads[l][n] for l in range(DEPTH)]) for n in SMALL_WEIGHTS}
    flat = jnp.concatenate([g[n].reshape(-1) for n in SMALL_WEIGHTS])
    small = jnp.pad(flat, (0, SMALL_ROWS * 128 - flat.shape[0])).reshape(SMALL_ROWS, 128)
    small_sum = sum_devices(gather_small(small, name="gather_small"), small, me, name="sum_devices").reshape(-1)
    mine, other = reducer.finish()

    res = {}
    for n in MATMUL_WEIGHTS:
        res[n] = adamw_halves(w[n], m[n], v[n], [mine[(l, n)] for l in range(DEPTH)],
                              [other[(l, n)] for l in range(DEPTH)], layouts[n], core, name="adamw_halves")
    off = 0
    for n in SMALL_WEIGHTS:
        shp = g[n].shape
        size = int(np.prod(shp))
        gn = small_sum[off:off + size].reshape(shp)
        off += size
        if n in SHARDED:
            gn = lax.dynamic_slice_in_dim(gn, (2 * cx + cy) * w[n].shape[-1], w[n].shape[-1], axis=len(shp) - 1)
        shp = w[n].shape
        cols = shp[-1]
        two_d = lambda t: t.reshape(int(np.prod(shp)) // cols, cols)
        res[n] = [t.reshape(shp) for t in adamw(two_d(w[n]), two_d(m[n]), two_d(v[n]), two_d(gn), name="adamw")]

    out = [loss, dx[None]]
    for k in range(4):
        out.extend(res[n][k] for n in WEIGHTS)
    return tuple(out)
```

```python
import jax
import jax.numpy as jnp
import numpy as np
from jax import lax
from jax.experimental import pallas as pl
from jax.experimental.pallas import tpu as pltpu

F32 = jnp.float32
BF16 = jnp.bfloat16

D_MODEL = 1024
DEPTH = 2
CHUNK = 64
N_LEFT = 8
HEAD_DIM = 64
N_HEADS = 8
WIDTH = 512
POOL_WINDOWS = (2, 4, 8, 16)
GROUP_DIM = 128
MAX_REL = 2 * CHUNK
REL_TABLE = MAX_REL + CHUNK
D_FF = 2816
EPS = 1e-6
QK_SCALE = 0.125
GATE_COL0 = 7 * WIDTH

ADAM_LR = 0.001
ADAM_B1 = 0.9
ADAM_B2 = 0.999
ADAM_EPS = 1e-08
ADAM_WD = 0.01
ADAM_STEP = 10

VMEM_LIMIT = 56 * 1024 * 1024
ATT_Q = 256
A_Q = 256
A_WIN = A_Q + N_LEFT * CHUNK
A_FWD_HEADS = 8
A_BWD_HEADS = 4
B_FWD_HEADS = 8
B_BWD_HEADS = 4
HALO = 16
CONV_HALO = 8
NEG = -1e30

MESH_AXES = ("x", "y", "c")
MESH_T = pl.DeviceIdType.MESH


def _cp(sem=None, vmem=VMEM_LIMIT):
    return pltpu.CompilerParams(dimension_semantics=sem, vmem_limit_bytes=vmem)


def _dot(a, b, ca, cb):
    return lax.dot_general(a, b, (((ca,), (cb,)), ((), ())), preferred_element_type=F32)


def _tile(n, cands=(1024, 512, 256, 128)):
    for c in cands:
        if n % c == 0:
            return c
    return n


def _split_hi_lo(v):
    hi = v.astype(BF16)
    lo = (v - hi.astype(F32)).astype(BF16)
    return hi, lo


def matmul(a, b, *, ta=False, tb=False, add=None, norm_bwd=None, out_dtype=F32, b_layer=None, behind=None, name):
    plan, sources = behind if behind is not None else (None, [])
    n_src = len(sources)
    n_dst = len(plan.out_shapes()) if plan is not None else 0
    assert add is None or norm_bwd is None
    if ta:
        K, M = a.shape
    else:
        M, K = a.shape
    if tb:
        N, K2 = b.shape[-2:]
    else:
        K2, N = b.shape[-2:]
    assert K == K2, (a.shape, b.shape, ta, tb)
    big = (1024, 1408, 512, 256, 128)
    tm = _tile(M, big)
    tn = _tile(N, (1664,) + big)
    tk = _tile(K, big if ta else (1664, 1408) + big if norm_bwd is not None else (3328, 2816) + big)
    nk = K // tk

    n_extra = 1 if add is not None else 3 if norm_bwd is not None else 0
    n_in = 2 + n_extra + n_src
    n_out = 2 if norm_bwd is not None else 1
    grid = (M // tm, N // tn, nk)
    assert norm_bwd is None or tn == N, "the norm gradient needs whole rows in one output tile"

    def body(*refs):
        a_ref, b_ref = refs[:2]
        extra = refs[2:2 + n_extra]
        o_ref, acc = refs[n_in], refs[n_in + n_out + n_dst]
        i, j, k = pl.program_id(0), pl.program_id(1), pl.program_id(2)
        if plan is not None:
            comm = (refs[n_in - n_src:n_in], refs[n_in + n_out:n_in + n_out + n_dst], refs[n_in + n_out + n_dst + 1:])

            @pl.when((i == 0) & (j == 0) & (k == 0))
            def _():
                plan.start(*comm)

        @pl.when(k == 0)
        def _():
            acc[...] = jnp.zeros_like(acc)

        av = a_ref[...].astype(BF16)
        bv = b_ref[...].astype(BF16)
        acc[...] += _dot(av, bv, 0 if ta else 1, 1 if tb else 0)

        @pl.when(k == nk - 1)
        def _():
            r = acc[...]
            if add is not None:
                r = r + extra[0][...].astype(F32)
            if norm_bwd is not None:
                x_ref, g_ref, dres_ref = extra
                dg_ref = refs[n_in + 1]
                xv = x_ref[...]
                inv = lax.rsqrt(jnp.mean(xv * xv, axis=-1, keepdims=True) + EPS)
                gd = r * g_ref[...]
                mean = jnp.mean(xv * gd, axis=-1, keepdims=True)

                @pl.when(i == 0)
                def _():
                    dg_ref[...] = jnp.zeros_like(dg_ref)

                dg_ref[...] += jnp.sum(r * xv * inv, axis=0, keepdims=True)
                r = dres_ref[...] + inv * gd - xv * (inv * inv * inv * mean)
            o_ref[...] = r.astype(out_dtype)

        if plan is not None:
            @pl.when((i == grid[0] - 1) & (j == grid[1] - 1) & (k == nk - 1))
            def _():
                plan.finish(*comm)

    a_spec = pl.BlockSpec((tk, tm), lambda i, j, k: (k, i)) if ta else pl.BlockSpec((tm, tk), lambda i, j, k: (i, k))
    if b_layer is None:
        b_spec = pl.BlockSpec((tn, tk), lambda i, j, k: (j, k)) if tb else pl.BlockSpec((tk, tn), lambda i, j, k: (k, j))
    elif tb:
        b_spec = pl.BlockSpec((None, tn, tk), lambda i, j, k: (b_layer, j, k))
    else:
        b_spec = pl.BlockSpec((None, tk, tn), lambda i, j, k: (b_layer, k, j))
    o_spec = pl.BlockSpec((tm, tn), lambda i, j, k: (i, j))
    vec_spec = pl.BlockSpec((1, tn), lambda i, j, k: (0, j))
    in_specs = [a_spec, b_spec]
    args = [a, b]
    out_shape = [jax.ShapeDtypeStruct((M, N), out_dtype)]
    out_specs = [o_spec]
    if add is not None:
        in_specs.append(o_spec)
        args.append(add)
    if norm_bwd is not None:
        in_specs += [o_spec, vec_spec, o_spec]
        args += list(norm_bwd)
        out_shape.append(jax.ShapeDtypeStruct((1, N), F32))
        out_specs.append(vec_spec)
    sequential = plan is not None or norm_bwd is not None
    any_space = pl.BlockSpec(memory_space=pl.ANY)
    outs = pl.pallas_call(
        body,
        out_shape=tuple(out_shape) + (tuple(plan.out_shapes()) if plan is not None else ()),
        grid=grid,
        in_specs=in_specs + [any_space] * n_src,
        out_specs=tuple(out_specs) + tuple([any_space] * n_dst),
        scratch_shapes=[pltpu.VMEM((tm, tn), F32)] + (list(plan.scratch_shapes()) if plan is not None else []),
        compiler_params=_cp(("arbitrary" if sequential else "parallel",) * 2 + ("arbitrary",)),
        name=name,
    )(*args, *sources)
    if plan is None:
        return outs[0] if n_out == 1 else tuple(outs)
    return tuple(outs[:n_out]) + (list(outs[n_out:]),) if n_out > 1 else (outs[0], list(outs[1:]))


def rmsnorm_fwd(x, g, *, name):
    S, D = x.shape
    T = _tile(S)

    def body(x_ref, g_ref, h_ref):
        xv = x_ref[...]
        r = lax.rsqrt(jnp.mean(xv * xv, axis=-1, keepdims=True) + EPS)
        h_ref[...] = (xv * r * g_ref[...]).astype(BF16)

    return pl.pallas_call(
        body,
        out_shape=jax.ShapeDtypeStruct((S, D), BF16),
        grid=(S // T,),
        in_specs=[pl.BlockSpec((T, D), lambda i: (i, 0)), pl.BlockSpec((1, D), lambda i: (0, 0))],
        out_specs=pl.BlockSpec((T, D), lambda i: (i, 0)),
        compiler_params=_cp(("parallel",)),
        name=name,
    )(x, g)


def _head_mean_matrix():
    r = lax.broadcasted_iota(jnp.int32, (WIDTH, WIDTH), 0) // HEAD_DIM
    c = lax.broadcasted_iota(jnp.int32, (WIDTH, WIDTH), 1) // HEAD_DIM
    return jnp.where(r == c, 1.0 / HEAD_DIM, 0.0).astype(BF16)


def _head_mean(v, mm):
    hi, lo = _split_hi_lo(v)
    return _dot(hi, mm, 1, 0) + _dot(lo, mm, 1, 0)


def qkv_prep(proj, gq, gk, *, name):
    S = proj.shape[0]
    T = _tile(S)

    def body(qa, ka, va, qb, kb, vb, gq_ref, gk_ref, oqa, oka, ova, oqb, okb, ovb):
        mm = _head_mean_matrix()
        for src, gref, dst, scale in ((qa, gq_ref, oqa, QK_SCALE), (ka, gk_ref, oka, 1.0)):
            v = src[...]
            r = lax.rsqrt(_head_mean(v * v, mm) + EPS)
            dst[...] = (v * r * gref[...] * scale).astype(BF16)
        oqb[...] = (qb[...] * QK_SCALE).astype(BF16)
        for src, dst in ((va, ova), (kb, okb), (vb, ovb)):
            dst[...] = src[...].astype(BF16)

    col = lambda j: pl.BlockSpec((T, WIDTH), lambda i, j=j: (i, j))
    vec = pl.BlockSpec((1, WIDTH), lambda i: (0, 0))
    out = pl.BlockSpec((T, WIDTH), lambda i: (i, 0))
    return pl.pallas_call(
        body,
        out_shape=tuple(jax.ShapeDtypeStruct((S, WIDTH), BF16) for _ in range(6)),
        grid=(S // T,),
        in_specs=[col(0), col(1), col(2), col(3), col(4), col(5), vec, vec],
        out_specs=tuple(out for _ in range(6)),
        compiler_params=_cp(("parallel",)),
        name=name,
    )(proj, proj, proj, proj, proj, proj, gq, gk)


def qknorm_bwd(proj, gq, gk, dqh, dkh, *, name):
    S = proj.shape[0]
    T = _tile(S)

    def body(qa, ka, gq_ref, gk_ref, dq_ref, dk_ref, oq, ok, ogq, ogk):
        i = pl.program_id(0)
        mm = _head_mean_matrix()

        @pl.when(i == 0)
        def _():
            ogq[...] = jnp.zeros_like(ogq)
            ogk[...] = jnp.zeros_like(ogk)

        for src, gref, dref, dst, gdst in ((qa, gq_ref, dq_ref, oq, ogq), (ka, gk_ref, dk_ref, ok, ogk)):
            v = src[...]
            dy = dref[...]
            r = lax.rsqrt(_head_mean(v * v, mm) + EPS)
            gd = dy * gref[...]
            m = _head_mean(v * gd, mm)
            dst[...] = (r * gd - v * (r * r * r * m)).astype(BF16)
            gdst[...] += jnp.sum(dy * v * r, axis=0, keepdims=True)

    col = lambda j: pl.BlockSpec((T, WIDTH), lambda i, j=j: (i, j))
    vec = pl.BlockSpec((1, WIDTH), lambda i: (0, 0))
    row = pl.BlockSpec((T, WIDTH), lambda i: (i, 0))
    return pl.pallas_call(
        body,
        out_shape=(jax.ShapeDtypeStruct((S, WIDTH), BF16), jax.ShapeDtypeStruct((S, WIDTH), BF16),
                   jax.ShapeDtypeStruct((1, WIDTH), F32), jax.ShapeDtypeStruct((1, WIDTH), F32)),
        grid=(S // T,),
        in_specs=[col(0), col(1), vec, vec, row, row],
        out_specs=(row, row, vec, vec),
        compiler_params=_cp(("arbitrary",)),
        name=name,
    )(proj, proj, gq, gk, dqh, dkh)


DIAG_W = 1024
N_VARIANTS = N_LEFT * CHUNK // A_Q + 1


def diagonal_onehot():
    jj = np.arange(DIAG_W)
    diff = np.where(jj < A_WIN, jj, jj - DIAG_W)
    out = np.zeros((N_VARIANTS, DIAG_W, REL_TABLE), np.float32)
    for v in range(N_VARIANTS):
        rel = np.clip(A_Q * v - diff, -(CHUNK - 1), MAX_REL) + (CHUNK - 1)
        out[v, jj, rel] = 1.0
    return out.reshape(N_VARIANTS * DIAG_W, REL_TABLE)


def exact_dot(a, b, *, name):
    def body(a_ref, b_ref, o_ref):
        o_ref[...] = jnp.dot(a_ref[...], b_ref[...], precision=lax.Precision.HIGHEST, preferred_element_type=F32)

    return pl.pallas_call(body, out_shape=jax.ShapeDtypeStruct((a.shape[0], b.shape[1]), F32),
                          compiler_params=_cp(), name=name)(a, b)


def _band_valid(v):
    qc = (lax.broadcasted_iota(jnp.int32, (A_Q, A_WIN), 0) + A_Q * v) // CHUNK
    kc = lax.broadcasted_iota(jnp.int32, (A_Q, A_WIN), 1) // CHUNK
    return (kc <= qc) & (kc >= qc - N_LEFT)


def bias_expand(diag, *, name):
    def body(d_ref, o_ref):
        rows = jnp.broadcast_to(d_ref[0, 0], (A_Q, DIAG_W))
        skew = pltpu.roll(rows, 0, 1, stride=1, stride_axis=0)
        o_ref[0, 0] = jnp.where(_band_valid(pl.program_id(0)), skew[:, :A_WIN], NEG)

    return pl.pallas_call(
        body,
        out_shape=jax.ShapeDtypeStruct((N_VARIANTS, N_HEADS, A_Q, A_WIN), F32),
        grid=(N_VARIANTS, N_HEADS),
        in_specs=[pl.BlockSpec((1, 1, 1, DIAG_W), lambda v, h: (v, h, 0, 0))],
        out_specs=pl.BlockSpec((1, 1, A_Q, A_WIN), lambda v, h: (v, h, 0, 0)),
        compiler_params=_cp(("parallel", "parallel")),
        name=name,
    )(diag)


def relbias_reduce(dbias, *, name):
    def body(db_ref, o_ref):
        acc = None
        for a in range(A_Q // 8):
            x = jnp.concatenate([db_ref[0, 0, 8 * a:8 * a + 8, :], jnp.zeros((8, DIAG_W - A_WIN), F32)], axis=1)
            x = pltpu.roll(x, DIAG_W - 8 * a, 1) if a else x
            acc = x if acc is None else acc + x
        row = lax.broadcasted_iota(jnp.int32, (8, DIAG_W), 0)
        for b in range(3):
            acc = jnp.where((row >> b) & 1 == 1, pltpu.roll(acc, DIAG_W - (1 << b), 1), acc)
        o_ref[0, 0] = jnp.sum(acc, axis=0, keepdims=True)

    return pl.pallas_call(
        body,
        out_shape=jax.ShapeDtypeStruct((N_VARIANTS, N_HEADS, 1, DIAG_W), F32),
        grid=(N_VARIANTS, N_HEADS),
        in_specs=[pl.BlockSpec((1, 1, A_Q, A_WIN), lambda v, h: (v, h, 0, 0))],
        out_specs=pl.BlockSpec((1, 1, 1, DIAG_W), lambda v, h: (v, h, 0, 0)),
        compiler_params=_cp(("parallel", "parallel")),
        name=name,
    )(dbias)


def _a_window_start(qb):
    return pl.multiple_of(jnp.maximum(qb * A_Q - N_LEFT * CHUNK, 0), A_Q)


def attn_a_fwd(q, k, v, biasm, *, name):
    S = q.shape[0]
    nq = S // A_Q

    def body(q_ref, k_ref, v_ref, b_ref, o_ref):
        qb = pl.program_id(1)
        start = _a_window_start(qb)
        outs = []
        for h in range(A_FWD_HEADS):
            lanes = slice(h * HEAD_DIM, (h + 1) * HEAD_DIM)
            qh = q_ref[:, lanes]
            kw = k_ref[pl.ds(start, A_WIN), lanes]
            vw = v_ref[pl.ds(start, A_WIN), lanes]
            s = _dot(qh, kw, 1, 1) + b_ref[0, h]
            m = jnp.max(s, axis=-1, keepdims=True)
            e = jnp.exp(s - m)
            outs.append(_dot(e.astype(BF16), vw, 1, 0) * (1.0 / jnp.sum(e, axis=-1, keepdims=True)))
        o_ref[...] = jnp.concatenate(outs, axis=1).astype(BF16)

    qspec = pl.BlockSpec((A_Q, A_FWD_HEADS * HEAD_DIM), lambda hp, qb: (qb, hp))
    kvspec = pl.BlockSpec((S, A_FWD_HEADS * HEAD_DIM), lambda hp, qb: (0, hp))
    bspec = pl.BlockSpec((1, A_FWD_HEADS, A_Q, A_WIN), lambda hp, qb: (jnp.minimum(qb, N_VARIANTS - 1), hp, 0, 0))
    return pl.pallas_call(
        body,
        out_shape=jax.ShapeDtypeStruct((S, WIDTH), BF16),
        grid=(N_HEADS // A_FWD_HEADS, nq),
        in_specs=[qspec, kvspec, kvspec, bspec],
        out_specs=qspec,
        compiler_params=_cp(("parallel", "arbitrary")),
        name=name,
    )(q, k, v, biasm)


def attn_a_bwd(q, k, v, biasm, do, *, name):
    S = q.shape[0]
    nq = S // A_Q

    def body(q_ref, k_ref, v_ref, b_ref, do_ref, dq_ref, dk_ref, dv_ref, db_ref):
        qb = pl.program_id(1)
        start = _a_window_start(qb)

        @pl.when(qb == 0)
        def _():
            dk_ref[...] = jnp.zeros_like(dk_ref)
            dv_ref[...] = jnp.zeros_like(dv_ref)

        @pl.when(qb < N_VARIANTS)
        def _():
            db_ref[...] = jnp.zeros_like(db_ref)

        dqs = []
        for h in range(A_BWD_HEADS):
            lanes = slice(h * HEAD_DIM, (h + 1) * HEAD_DIM)
            qh = q_ref[:, lanes]
            doh = do_ref[:, lanes]
            kw = k_ref[pl.ds(start, A_WIN), lanes]
            vw = v_ref[pl.ds(start, A_WIN), lanes]
            s = _dot(qh, kw, 1, 1) + b_ref[0, h]
            m = jnp.max(s, axis=-1, keepdims=True)
            e = jnp.exp(s - m)
            p = e * (1.0 / jnp.sum(e, axis=-1, keepdims=True))
            dp = _dot(doh, vw, 1, 1)
            delta = jnp.sum(p * dp, axis=-1, keepdims=True)
            ds = p * (dp - delta)
            db_ref[0, h] += ds
            dsb = ds.astype(BF16)
            dqs.append(_dot(dsb, kw, 1, 0) * QK_SCALE)
            dk_ref[pl.ds(start, A_WIN), lanes] += _dot(dsb, qh, 0, 0)
            dv_ref[pl.ds(start, A_WIN), lanes] += _dot(p.astype(BF16), doh, 0, 0)
        dq_ref[...] = jnp.concatenate(dqs, axis=1)

    once = pl.Buffered(1)
    qspec = pl.BlockSpec((A_Q, A_BWD_HEADS * HEAD_DIM), lambda hp, qb: (qb, hp))
    kvspec = pl.BlockSpec((S, A_BWD_HEADS * HEAD_DIM), lambda hp, qb: (0, hp), pipeline_mode=once)
    bspec = pl.BlockSpec((1, A_BWD_HEADS, A_Q, A_WIN), lambda hp, qb: (jnp.minimum(qb, N_VARIANTS - 1), hp, 0, 0))
    return pl.pallas_call(
        body,
        out_shape=(jax.ShapeDtypeStruct((S, WIDTH), F32), jax.ShapeDtypeStruct((S, WIDTH), F32),
                   jax.ShapeDtypeStruct((S, WIDTH), F32), jax.ShapeDtypeStruct((N_VARIANTS, N_HEADS, A_Q, A_WIN), F32)),
        grid=(N_HEADS // A_BWD_HEADS, nq),
        in_specs=[qspec, kvspec, kvspec, bspec, qspec],
        out_specs=(qspec, kvspec, kvspec, bspec),
        compiler_params=_cp(("parallel", "arbitrary")),
        name=name,
    )(q, k, v, biasm, do)


def _tri(kind):
    j = lax.broadcasted_iota(jnp.int32, (ATT_Q, ATT_Q), 0)
    s = lax.broadcasted_iota(jnp.int32, (ATT_Q, ATT_Q), 1)
    if kind == "gt":
        m = j > s
    elif kind == "le":
        m = j <= s
    else:
        m = j < s
    return jnp.where(m, 1.0, 0.0).astype(BF16)


def _cum(v, tri):
    hi, lo = _split_hi_lo(v)
    return _dot(hi, tri, 1, 0) + _dot(lo, tri, 1, 0)


def _log_sigmoids(z, mask):
    t = jnp.log(1.0 + jnp.exp(-jnp.abs(z)))
    keep = -(jnp.maximum(z, 0.0) + t)
    take = jnp.minimum(z, 0.0) - t
    return (keep if mask is None else jnp.where(mask, keep, 0.0)), take


def _strictly_before():
    row = lax.broadcasted_iota(jnp.int32, (ATT_Q, ATT_Q), 0)
    col = lax.broadcasted_iota(jnp.int32, (ATT_Q, ATT_Q), 1)
    return col < row


EXIT_LOG = -104.0


def attn_b_fwd(q, k, v, *, gather=None, name):
    S = q.shape[0]
    nq = S // ATT_Q
    heads = B_FWD_HEADS
    plan, shards, fulls = gather if gather is not None else (None, [], [])
    ng = len(shards)

    def body(q_ref, k_ref, v_ref, *rest):
        hp = pl.program_id(0)
        qb = pl.program_id(1)
        o_ref, t_ref, n_ref = rest[2 * ng:2 * ng + 3]
        if plan is not None:
            comm = (rest[:ng], rest[2 * ng + 3:3 * ng + 3], rest[3 * ng + 3:])

            @pl.when(jnp.logical_and(hp == 0, qb == 0))
            def _():
                plan.start(*comm)

        tri = _tri("gt")

        def block(kb, carry, mask):
            ks = pl.multiple_of(kb * ATT_Q, ATT_Q)
            new = []
            for h in range(heads):
                lanes = slice(h * HEAD_DIM, (h + 1) * HEAD_DIM)
                c, acc = carry[h]
                z = _dot(q_ref[:, lanes], k_ref[pl.ds(ks, ATT_Q), lanes], 1, 1)
                keep, take = _log_sigmoids(z, mask)
                w = jnp.exp(take + (_cum(keep, tri) + c))
                if mask is not None:
                    w = jnp.where(mask, w, 0.0)
                acc = acc + _dot(w.astype(BF16), v_ref[pl.ds(ks, ATT_Q), lanes], 1, 0)
                c = c + jnp.sum(keep, axis=-1, keepdims=True)
                new.append((c, acc))
            return tuple(new)

        def cond(state):
            it, cmax, _ = state
            return jnp.logical_and(it <= qb, cmax >= EXIT_LOG)

        def step(state):
            it, _, carry = state
            carry = block(qb - it, carry, None)
            worst = carry[0][0]
            for h in range(1, heads):
                worst = jnp.maximum(worst, carry[h][0])
            return it + 1, jnp.max(worst), carry

        init = tuple((jnp.zeros((ATT_Q, 1), F32), jnp.zeros((ATT_Q, HEAD_DIM), F32)) for _ in range(heads))
        diag = block(qb, init, _strictly_before())
        visited, _, res = lax.while_loop(cond, step, (jnp.int32(1), jnp.float32(0.0), diag))
        o_ref[...] = jnp.concatenate([res[h][1] for h in range(heads)], axis=1).astype(BF16)
        t_ref[...] = jnp.concatenate([jnp.broadcast_to(res[h][0], (ATT_Q, HEAD_DIM)) for h in range(heads)], axis=1)
        n_ref[hp, qb] = visited.astype(F32)
        if plan is not None:
            @pl.when(jnp.logical_and(hp == N_HEADS // heads - 1, qb == nq - 1))
            def _():
                plan.finish(*comm)

    qspec = pl.BlockSpec((ATT_Q, heads * HEAD_DIM), lambda hp, qb: (qb, hp))
    kvspec = pl.BlockSpec((S, heads * HEAD_DIM), lambda hp, qb: (0, hp), pipeline_mode=pl.Buffered(1))
    outs = pl.pallas_call(
        body,
        out_shape=(jax.ShapeDtypeStruct((S, WIDTH), BF16), jax.ShapeDtypeStruct((S, WIDTH), F32),
                   jax.ShapeDtypeStruct((N_HEADS // heads, nq), F32))
        + tuple(jax.ShapeDtypeStruct(f.shape, f.dtype) for f in fulls),
        grid=(N_HEADS // heads, nq),
        in_specs=[qspec, kvspec, kvspec] + [ANY] * (2 * ng),
        out_specs=(qspec, qspec, pl.BlockSpec(memory_space=pltpu.SMEM)) + tuple([ANY] * ng),
        scratch_shapes=plan.scratch_shapes() if plan is not None else (),
        input_output_aliases={3 + ng + i: 3 + i for i in range(ng)},
        compiler_params=_cp(("arbitrary", "arbitrary")),
        name=name,
    )(q, k, v, *shards, *fulls)
    return outs if plan is None else (outs[0], outs[1], outs[2], list(outs[3:]))


def attn_b_bwd(q, k, v, tot, nblk, do, *, scatter=None, name):
    S = q.shape[0]
    nq = S // ATT_Q
    heads = B_BWD_HEADS
    plan, pairs16 = scatter if scatter is not None else (None, [])
    ns = len(pairs16)

    def body(q_ref, k_ref, v_ref, t_ref, n_ref, do_ref, *rest):
        hp = pl.program_id(0)
        qb = pl.program_id(1)
        dq_ref, dk_ref, dv_ref = rest[ns:ns + 3]
        if plan is not None:
            comm = (rest[:ns], rest[ns + 3:2 * ns + 3], rest[2 * ns + 3:])

            @pl.when(jnp.logical_and(hp == 0, qb == 0))
            def _():
                plan.start(*comm)

        visited = n_ref[(hp * heads) // B_FWD_HEADS, qb].astype(jnp.int32)
        first = jnp.clip(qb + 1 - visited, 0, qb + 1)
        tri_le = _tri("le")
        tri_lt = _tri("lt")

        @pl.when(qb == 0)
        def _():
            dk_ref[...] = jnp.zeros_like(dk_ref)
            dv_ref[...] = jnp.zeros_like(dv_ref)

        def block(kb, carry, mask):
            ks = pl.multiple_of(kb * ATT_Q, ATT_Q)
            new = []
            for h in range(heads):
                lanes = slice(h * HEAD_DIM, (h + 1) * HEAD_DIM)
                cl, cg, dq = carry[h]
                qh = q_ref[:, lanes]
                doh = do_ref[:, lanes]
                kh = k_ref[pl.ds(ks, ATT_Q), lanes]
                vh = v_ref[pl.ds(ks, ATT_Q), lanes]
                totl = t_ref[:, h * HEAD_DIM:h * HEAD_DIM + 1]
                z = _dot(qh, kh, 1, 1)
                keep, take = _log_sigmoids(z, mask)
                sig = jnp.exp(take)
                w = sig * jnp.exp((totl - cl) - _cum(keep, tri_le))
                if mask is not None:
                    w = jnp.where(mask, w, 0.0)
                g = w * _dot(doh, vh, 1, 1)
                G = _dot(g.astype(BF16), tri_lt, 1, 0) + cg
                dz = g * (1.0 - sig) - sig * G
                if mask is not None:
                    dz = jnp.where(mask, dz, 0.0)
                dz = dz.astype(BF16)
                dq = dq + _dot(dz, kh, 1, 0)
                dk_ref[pl.ds(ks, ATT_Q), lanes] += _dot(dz, qh, 0, 0)
                dv_ref[pl.ds(ks, ATT_Q), lanes] += _dot(w.astype(BF16), doh, 0, 0)
                cl = cl + jnp.sum(keep, axis=-1, keepdims=True)
                cg = cg + jnp.sum(g, axis=-1, keepdims=True)
                new.append((cl, cg, dq))
            return tuple(new)

        init = tuple((jnp.zeros((ATT_Q, 1), F32), jnp.zeros((ATT_Q, 1), F32), jnp.zeros((ATT_Q, HEAD_DIM), F32))
                     for _ in range(heads))
        res = lax.fori_loop(jnp.minimum(first, qb), qb, lambda kb, carry: block(kb, carry, None), init)
        res = block(qb, res, _strictly_before())
        dq_ref[...] = (jnp.concatenate([res[h][2] for h in range(heads)], axis=1) * QK_SCALE).astype(BF16)
        if plan is not None:
            @pl.when(jnp.logical_and(hp == N_HEADS // heads - 1, qb == nq - 1))
            def _():
                plan.finish(*comm)

    qspec = pl.BlockSpec((ATT_Q, heads * HEAD_DIM), lambda hp, qb: (qb, hp))
    kvspec = pl.BlockSpec((S, heads * HEAD_DIM), lambda hp, qb: (0, hp), pipeline_mode=pl.Buffered(1))
    outs = pl.pallas_call(
        body,
        out_shape=(jax.ShapeDtypeStruct((S, WIDTH), BF16), jax.ShapeDtypeStruct((S, WIDTH), F32),
                   jax.ShapeDtypeStruct((S, WIDTH), F32)) + (tuple(plan.out_shapes()) if plan is not None else ()),
        grid=(N_HEADS // heads, nq),
        in_specs=[qspec, kvspec, kvspec, qspec, pl.BlockSpec(memory_space=pltpu.SMEM), qspec] + [ANY] * ns,
        out_specs=(qspec, kvspec, kvspec) + tuple([ANY] * ns),
        scratch_shapes=plan.scratch_shapes() if plan is not None else (),
        compiler_params=_cp(("arbitrary", "arbitrary")),
        name=name,
    )(q, k, v, tot, nblk, do, *pairs16)
    return outs if plan is None else (outs[0], outs[1], outs[2], list(outs[3:]))


U_COLBLK = 6


def _pool_counts(t0, rows):
    t = t0 + lax.broadcasted_iota(jnp.int32, (rows, WIDTH), 0)
    lane_grp = lax.broadcasted_iota(jnp.int32, (rows, WIDTH), 1) // GROUP_DIM
    w2, w4, w8, w16 = POOL_WINDOWS
    win = jnp.where(lane_grp == 0, w2, jnp.where(lane_grp == 1, w4, jnp.where(lane_grp == 2, w8, w16)))
    cnt = jnp.minimum(t + 1, win)
    return 1.0 / cnt.astype(F32), lane_grp


def _window_sums(ext, shift_fn):
    s2 = ext + shift_fn(ext, 1)
    s4 = s2 + shift_fn(s2, 2)
    s8 = s4 + shift_fn(s4, 4)
    s16 = s8 + shift_fn(s8, 8)
    return s2, s4, s8, s16


def _select_group(lane_grp, s2, s4, s8, s16):
    return jnp.where(lane_grp == 0, s2, jnp.where(lane_grp == 1, s4, jnp.where(lane_grp == 2, s8, s16)))


def _pooled_tile(u_ref, h_ref, i, T):
    halo = jnp.where(i > 0, h_ref[...], 0.0)
    ext = jnp.concatenate([halo, u_ref[...]], axis=0)
    n = T + HALO
    sums = _window_sums(ext, lambda v, k: pltpu.roll(v, k, 0))
    inv, lane_grp = _pool_counts(i * T - HALO, n)
    pooled = _select_group(lane_grp, *sums) * inv - ext
    return pooled[HALO:, :]


def pool_fwd(proj, w_pool, scale, *, name):
    S = proj.shape[0]
    T = _tile(S)
    hb = T // HALO

    def body(u_ref, h_ref, w_ref, s_ref, o_ref):
        i = pl.program_id(0)
        pooled = _pooled_tile(u_ref, h_ref, i, T).astype(BF16)
        outs = [_dot(pooled[:, g * GROUP_DIM:(g + 1) * GROUP_DIM], w_ref[g], 1, 0) for g in range(4)]
        o_ref[...] = (jnp.concatenate(outs, axis=1) * s_ref[...]).astype(BF16)

    return pl.pallas_call(
        body,
        out_shape=jax.ShapeDtypeStruct((S, WIDTH), BF16),
        grid=(S // T,),
        in_specs=[pl.BlockSpec((T, WIDTH), lambda i: (i, U_COLBLK)),
                  pl.BlockSpec((HALO, WIDTH), lambda i: (jnp.maximum(i * hb - 1, 0), U_COLBLK)),
                  pl.BlockSpec((4, GROUP_DIM, GROUP_DIM), lambda i: (0, 0, 0)),
                  pl.BlockSpec((1, WIDTH), lambda i: (0, 0))],
        out_specs=pl.BlockSpec((T, WIDTH), lambda i: (i, 0)),
        compiler_params=_cp(("parallel",)),
        name=name,
    )(proj, proj, w_pool, scale)


def pool_bwd(proj, w_pool, scale, do, *, name):
    S = proj.shape[0]
    T = _tile(S)
    hb = T // HALO
    nt = S // T

    def body(u_ref, h_ref, w_ref, s_ref, do_ref, dof_ref, du_ref, dw_ref, ds_ref):
        i = pl.program_id(0)

        @pl.when(i == 0)
        def _():
            dw_ref[...] = jnp.zeros_like(dw_ref)
            ds_ref[...] = jnp.zeros_like(ds_ref)

        pooled = _pooled_tile(u_ref, h_ref, i, T).astype(BF16)
        dov = do_ref[...].astype(F32)
        fut = jnp.where(i < nt - 1, dof_ref[...].astype(F32), 0.0)
        dmix = (jnp.concatenate([dov, fut], axis=0) * s_ref[...]).astype(BF16)
        mixed, dpool = [], []
        for g in range(4):
            lanes = slice(g * GROUP_DIM, (g + 1) * GROUP_DIM)
            mixed.append(_dot(pooled[:, lanes], w_ref[g], 1, 0))
            dw_ref[g] += _dot(pooled[:, lanes], dmix[:T, lanes], 0, 0)
            dpool.append(_dot(dmix[:, lanes], w_ref[g], 1, 1))
        ds_ref[...] += jnp.sum(dov * jnp.concatenate(mixed, axis=1), axis=0, keepdims=True)
        dp = jnp.concatenate(dpool, axis=1)
        n = T + HALO
        inv, lane_grp = _pool_counts(i * T, n)
        sums = _window_sums(dp * inv, lambda v, k: pltpu.roll(v, n - k, 0))
        du = _select_group(lane_grp, *sums) - dp
        du_ref[...] = du[:T, :].astype(BF16)

    row = pl.BlockSpec((T, WIDTH), lambda i: (i, 0))
    return pl.pallas_call(
        body,
        out_shape=(jax.ShapeDtypeStruct((S, WIDTH), BF16), jax.ShapeDtypeStruct((4, GROUP_DIM, GROUP_DIM), F32),
                   jax.ShapeDtypeStruct((1, WIDTH), F32)),
        grid=(nt,),
        in_specs=[pl.BlockSpec((T, WIDTH), lambda i: (i, U_COLBLK)),
                  pl.BlockSpec((HALO, WIDTH), lambda i: (jnp.maximum(i * hb - 1, 0), U_COLBLK)),
                  pl.BlockSpec((4, GROUP_DIM, GROUP_DIM), lambda i: (0, 0, 0)),
                  pl.BlockSpec((1, WIDTH), lambda i: (0, 0)),
                  row,
                  pl.BlockSpec((HALO, WIDTH), lambda i: (jnp.minimum((i + 1) * hb, S // HALO - 1), 0))],
        out_specs=(row, pl.BlockSpec((4, GROUP_DIM, GROUP_DIM), lambda i: (0, 0, 0)),
                   pl.BlockSpec((1, WIDTH), lambda i: (0, 0))),
        compiler_params=_cp(("arbitrary",)),
        name=name,
    )(proj, proj, w_pool, scale, do, do)


GATE_BLK0 = GATE_COL0 // WIDTH


def merge_fwd(oa, ob, oc, proj, b_gate, wa, wb, wc, *, name):
    S = oa.shape[0]
    T = _tile(S)

    def body(oa_ref, ob_ref, oc_ref, ga, gb, gc, ba, bb, bc, wa_ref, wb_ref, wc_ref, m_ref):
        acc = None
        for o_ref, g_ref, b_ref, w_ref in ((oa_ref, ga, ba, wa_ref), (ob_ref, gb, bb, wb_ref), (oc_ref, gc, bc, wc_ref)):
            y = _dot(o_ref[...], w_ref[...], 1, 0)
            t = jax.nn.sigmoid(g_ref[...] + b_ref[...]) * y
            acc = t if acc is None else acc + t
        m_ref[...] = acc.astype(BF16)

    row = pl.BlockSpec((T, WIDTH), lambda i, n: (i, 0))
    gate = lambda b: pl.BlockSpec((T, WIDTH), lambda i, n, b=b: (i, GATE_BLK0 + 2 * b + n))
    bias = lambda b: pl.BlockSpec((1, WIDTH), lambda i, n, b=b: (0, 2 * b + n))
    wspec = pl.BlockSpec((WIDTH, WIDTH), lambda i, n: (0, n))
    return pl.pallas_call(
        body,
        out_shape=jax.ShapeDtypeStruct((S, D_MODEL), BF16),
        grid=(S // T, 2),
        in_specs=[row, row, row, gate(0), gate(1), gate(2), bias(0), bias(1), bias(2), wspec, wspec, wspec],
        out_specs=pl.BlockSpec((T, WIDTH), lambda i, n: (i, n)),
        compiler_params=_cp(("parallel", "parallel")),
        name=name,
    )(oa, ob, oc, proj, proj, proj, b_gate, b_gate, b_gate, wa, wb, wc)


def merge_bwd(dm, oa, ob, oc, proj, b_gate, wa, wb, wc, *, name):
    S = oa.shape[0]
    T = _tile(S)

    def body(dm_ref, oa_ref, ob_ref, oc_ref, ga, gb, gc, ba, bb, bc, wa_ref, wb_ref, wc_ref,
             ta, tb, tc, dga, dgb, dgc, dba, dbb, dbc):
        i = pl.program_id(1)
        dmv = dm_ref[...].astype(F32)
        for o_ref, g_ref, b_ref, w_ref, t_ref, dg_ref, db_ref in (
                (oa_ref, ga, ba, wa_ref, ta, dga, dba), (ob_ref, gb, bb, wb_ref, tb, dgb, dbb),
                (oc_ref, gc, bc, wc_ref, tc, dgc, dbc)):
            y = _dot(o_ref[...], w_ref[...], 1, 0)
            gate = jax.nn.sigmoid(g_ref[...] + b_ref[...])
            t_ref[...] = (gate * dmv).astype(BF16)
            dgl = dmv * y * gate * (1.0 - gate)
            dg_ref[...] = dgl.astype(BF16)

            @pl.when(i == 0)
            def _():
                db_ref[...] = jnp.zeros_like(db_ref)

            db_ref[...] += jnp.sum(dgl, axis=0, keepdims=True)

    row = pl.BlockSpec((T, WIDTH), lambda n, i: (i, 0))
    half = pl.BlockSpec((T, WIDTH), lambda n, i: (i, n))
    gate = lambda b: pl.BlockSpec((T, WIDTH), lambda n, i, b=b: (i, GATE_BLK0 + 2 * b + n))
    bias = lambda b: pl.BlockSpec((1, WIDTH), lambda n, i, b=b: (0, 2 * b + n))
    wspec = pl.BlockSpec((WIDTH, WIDTH), lambda n, i: (0, n))
    bvec = pl.BlockSpec((1, WIDTH), lambda n, i: (0, n))
    act = jax.ShapeDtypeStruct((S, D_MODEL), BF16)
    vec = jax.ShapeDtypeStruct((1, D_MODEL), F32)
    return pl.pallas_call(
        body,
        out_shape=(act, act, act, act, act, act, vec, vec, vec),
        grid=(2, S // T),
        in_specs=[half, row, row, row, gate(0), gate(1), gate(2), bias(0), bias(1), bias(2), wspec, wspec, wspec],
        out_specs=(half, half, half, half, half, half, bvec, bvec, bvec),
        compiler_params=_cp(("parallel", "arbitrary")),
        name=name,
    )(dm, oa, ob, oc, proj, proj, proj, b_gate, b_gate, b_gate, wa, wb, wc)


FF_T = 256
FF_BLKS = D_FF // FF_T
CONV_ROWS = (2048, 1024, 512, 256, 128)


def _silu_parts(x):
    s = jax.nn.sigmoid(x)
    return x * s, s


def _conv3(ext, w_ref, b_ref):
    taps = (pltpu.roll(ext, 2, 0), pltpu.roll(ext, 1, 0), ext)
    return b_ref[...] + w_ref[0:1, :] * taps[0] + w_ref[1:2, :] * taps[1] + w_ref[2:3, :] * taps[2], taps


def conv_glu_fwd(u, conv_w, conv_b, *, name):
    S = u.shape[0]
    T = _tile(S, CONV_ROWS)
    hb = T // CONV_HALO

    def body(ug, ugh, uv, uvh, wg, wv, bg, bv, a_ref):
        i = pl.program_id(1)
        cs = []
        for m_ref, h_ref, w_ref, b_ref in ((ug, ugh, wg, bg), (uv, uvh, wv, bv)):
            halo = jnp.where(i > 0, h_ref[...], 0.0)
            ext = jnp.concatenate([halo, m_ref[...]], axis=0)
            cs.append(_conv3(ext, w_ref, b_ref)[0][CONV_HALO:, :])
        act, _ = _silu_parts(cs[0])
        a_ref[...] = (act * cs[1]).astype(BF16)

    main = lambda o: pl.BlockSpec((T, FF_T), lambda c, i, o=o: (i, c + o))
    halo = lambda o: pl.BlockSpec((CONV_HALO, FF_T), lambda c, i, o=o: (jnp.maximum(i * hb - 1, 0), c + o))
    wsp = lambda o: pl.BlockSpec((3, FF_T), lambda c, i, o=o: (0, c + o))
    bsp = lambda o: pl.BlockSpec((1, FF_T), lambda c, i, o=o: (0, c + o))
    return pl.pallas_call(
        body,
        out_shape=jax.ShapeDtypeStruct((S, D_FF), BF16),
        grid=(FF_BLKS, S // T),
        in_specs=[main(0), halo(0), main(FF_BLKS), halo(FF_BLKS), wsp(0), wsp(FF_BLKS), bsp(0), bsp(FF_BLKS)],
        out_specs=pl.BlockSpec((T, FF_T), lambda c, i: (i, c)),
        compiler_params=_cp(("parallel", "parallel")),
        name=name,
    )(u, u, u, u, conv_w, conv_w, conv_b, conv_b)


def conv_glu_bwd(u, conv_w, conv_b, da, *, name):
    S = u.shape[0]
    T = _tile(S, CONV_ROWS)
    hb = T // CONV_HALO
    nt = S // T
    n = T + 2 * CONV_HALO

    def body(ug, ugp, ugf, uv, uvp, uvf, wg, wv, bg, bv, da_ref, daf_ref,
             dug, duv, dwg, dwv, dbg, dbv):
        i = pl.program_id(1)
        first, last = i == 0, i == nt - 1
        taps, cs = [], []
        for m_ref, p_ref, f_ref, w_ref, b_ref in ((ug, ugp, ugf, wg, bg), (uv, uvp, uvf, wv, bv)):
            ext = jnp.concatenate([jnp.where(first, 0.0, p_ref[...]), m_ref[...], jnp.where(last, 0.0, f_ref[...])], axis=0)
            c, tp = _conv3(ext, w_ref, b_ref)
            cs.append(c)
            taps.append(tp)
        dae = jnp.concatenate([jnp.zeros((CONV_HALO, FF_T), F32), da_ref[...].astype(F32),
                               jnp.where(last, 0.0, daf_ref[...].astype(F32))], axis=0)
        act, sg = _silu_parts(cs[0])
        dcs = (dae * cs[1] * (sg * (1.0 + cs[0] * (1.0 - sg))), dae * act)
        main = slice(CONV_HALO, CONV_HALO + T)
        for tp, dc, w_ref, du_ref, dw_ref, db_ref in ((taps[0], dcs[0], wg, dug, dwg, dbg),
                                                      (taps[1], dcs[1], wv, duv, dwv, dbv)):
            du = (w_ref[2:3, :] * dc + w_ref[1:2, :] * pltpu.roll(dc, n - 1, 0) + w_ref[0:1, :] * pltpu.roll(dc, n - 2, 0))
            du_ref[...] = du[main, :].astype(BF16)
            dcm = dc[main, :]
            rows = [jnp.sum(dcm * tp[j][main, :], axis=0, keepdims=True) for j in range(3)]

            @pl.when(first)
            def _():
                dw_ref[...] = jnp.zeros_like(dw_ref)
                db_ref[...] = jnp.zeros_like(db_ref)

            dw_ref[...] += jnp.concatenate(rows, axis=0)
            db_ref[...] += jnp.sum(dcm, axis=0, keepdims=True)

    main = lambda o: pl.BlockSpec((T, FF_T), lambda c, i, o=o: (i, c + o))
    past = lambda o: pl.BlockSpec((CONV_HALO, FF_T), lambda c, i, o=o: (jnp.maximum(i * hb - 1, 0), c + o))
    fut = lambda o: pl.BlockSpec((CONV_HALO, FF_T), lambda c, i, o=o: (jnp.minimum((i + 1) * hb, S // CONV_HALO - 1), c + o))
    wsp = lambda o: pl.BlockSpec((3, FF_T), lambda c, i, o=o: (0, c + o))
    bsp = lambda o: pl.BlockSpec((1, FF_T), lambda c, i, o=o: (0, c + o))
    return pl.pallas_call(
        body,
        out_shape=(jax.ShapeDtypeStruct((S, D_FF), BF16), jax.ShapeDtypeStruct((S, D_FF), BF16),
                   jax.ShapeDtypeStruct((3, D_FF), F32), jax.ShapeDtypeStruct((3, D_FF), F32),
                   jax.ShapeDtypeStruct((1, D_FF), F32), jax.ShapeDtypeStruct((1, D_FF), F32)),
        grid=(FF_BLKS, nt),
        in_specs=[main(0), past(0), fut(0), main(FF_BLKS), past(FF_BLKS), fut(FF_BLKS),
                  wsp(0), wsp(FF_BLKS), bsp(0), bsp(FF_BLKS), main(0), fut(0)],
        out_specs=(main(0), main(0), wsp(0), wsp(0), bsp(0), bsp(0)),
        compiler_params=_cp(("parallel", "arbitrary")),
        name=name,
    )(u, u, u, u, u, u, conv_w, conv_w, conv_b, conv_b, da, da)


def loss_head(y, target, *, name):
    S, D = y.shape
    T = _tile(S)

    def body(y_ref, t_ref, dy_ref, l_ref):
        i = pl.program_id(0)
        err = y_ref[...] - t_ref[...]
        dy_ref[...] = err * (1.0 / D)

        @pl.when(i == 0)
        def _():
            l_ref[...] = jnp.zeros_like(l_ref)

        l_ref[...] += 0.5 * jnp.sum(jnp.mean(err * err, axis=-1, keepdims=True))

    row = pl.BlockSpec((T, D), lambda i: (i, 0))
    return pl.pallas_call(
        body,
        out_shape=(jax.ShapeDtypeStruct((S, D), F32), jax.ShapeDtypeStruct((8, 128), F32)),
        grid=(S // T,),
        in_specs=[row, row],
        out_specs=(row, pl.BlockSpec((8, 128), lambda i: (0, 0))),
        compiler_params=_cp(("arbitrary",)),
        name=name,
    )(y, target)


ELEMS_PER_BLOCK = 512 * 1024


def _rows_tile(rows, cols):
    if rows * cols <= ELEMS_PER_BLOCK or rows % 8:
        return rows
    best = 8
    for tr in range(8, rows + 1, 8):
        if rows % tr == 0 and tr * cols <= ELEMS_PER_BLOCK:
            best = tr
    return best


def _adamw_math(g, w_ref, m_ref, v_ref, g_out, d_out, m_out, v_out):
    mn = ADAM_B1 * m_ref[...] + (1.0 - ADAM_B1) * g
    vn = ADAM_B2 * v_ref[...] + (1.0 - ADAM_B2) * (g * g)
    m_hat = mn / (1.0 - ADAM_B1 ** ADAM_STEP)
    v_hat = vn / (1.0 - ADAM_B2 ** ADAM_STEP)
    g_out[...] = g
    d_out[...] = -ADAM_LR * (m_hat / (jnp.sqrt(v_hat) + ADAM_EPS) + ADAM_WD * w_ref[...])
    m_out[...] = mn
    v_out[...] = vn


def adamw(w, m, v, g, *, name):
    rows, cols = w.shape
    tr = _rows_tile(rows, cols)

    def body(w_ref, m_ref, v_ref, g_ref, g_out, d_out, m_out, v_out):
        _adamw_math(g_ref[...], w_ref, m_ref, v_ref, g_out, d_out, m_out, v_out)

    spec = pl.BlockSpec((tr, cols), lambda i: (i, 0))
    shp = jax.ShapeDtypeStruct((rows, cols), F32)
    return pl.pallas_call(
        body,
        out_shape=(shp, shp, shp, shp),
        grid=(rows // tr,),
        in_specs=[spec] * 4,
        out_specs=(spec, spec, spec, spec),
        compiler_params=_cp(("parallel",)),
        name=name,
    )(w, m, v, g)


ANY = pl.BlockSpec(memory_space=pl.ANY)
STAGE_BYTES = 2 * 1024 * 1024


def _mesh_pos():
    return lax.axis_index("x"), lax.axis_index("y"), lax.axis_index("c")


def _chip_peers(x, y):
    return [(1 - x, y), (x, 1 - y), (1 - x, 1 - y)]


def _all_peers(x, y, c):
    return [((1 - x) if (r >> 2) & 1 else x, (1 - y) if (r >> 1) & 1 else y, (1 - c) if r & 1 else c)
            for r in range(1, 8)]


class LayerGather:
    def __init__(self, shards, axes, layer):
        self.nt = len(shards)
        self.axes = list(axes)
        self.layer = layer
        self.shapes = [s.shape for s in shards]
        self.dtypes = [s.dtype for s in shards]
        self.sizes = [s.shape[a] for s, a in zip(shards, axes)]
        self.split = [s.shape[1] % 32 == 0 for s in shards]
        self.half_rows = [s.shape[1] // 2 if sp else s.shape[1] for s, sp in zip(shards, self.split)]
        self.chunk_rows = []
        for s in shards:
            rt = s.shape[1]
            while rt % 32 == 0 and rt * s.shape[2] * s.dtype.itemsize > STAGE_BYTES:
                rt //= 2
            self.chunk_rows.append(rt)

    def out_shapes(self):
        out = []
        for shp, a, sz, dt in zip(self.shapes, self.axes, self.sizes, self.dtypes):
            shp = list(shp)
            shp[a] = 4 * sz
            out.append(jax.ShapeDtypeStruct(tuple(shp), dt))
        return out

    def scratch_shapes(self):
        return ([pltpu.VMEM((1, rt, shp[2]), dt) for shp, rt, dt in zip(self.shapes, self.chunk_rows, self.dtypes)]
                + [pltpu.SemaphoreType.DMA((2 * self.nt,))] + [pltpu.SemaphoreType.DMA((3 * self.nt,)) for _ in range(4)])

    def _views(self, ins, outs, scratch):
        nt = self.nt
        stage, stage_sems = scratch[:nt], scratch[nt]
        ici_send, ici_recv, d2d_send, d2d_recv = scratch[nt + 1:]
        x, y, c = _mesh_pos()
        mine = 2 * x + y
        peers = _chip_peers(x, y)
        layer = pl.ds(self.layer, 1)

        def rows(t, half, r0=0, n=None):
            hr = self.half_rows[t]
            if n is None:
                return pl.ds(pl.multiple_of(half * hr, 16), hr) if self.split[t] else pl.ds(0, hr)
            return pl.ds(r0, n)

        def placed(t, blk, row_sel, row_len):
            sz = self.sizes[t]
            if self.axes[t] == 2:
                return outs[t].at[layer, row_sel, pl.ds(pl.multiple_of(blk * sz, 128), sz)]
            return outs[t].at[layer, pl.ds(pl.multiple_of(blk * sz, 16) + row_sel.start, row_len), :]

        def ici(t, k, blk):
            px, py = peers[k]
            sel = rows(t, c)
            return pltpu.make_async_remote_copy(
                src_ref=ins[t].at[layer, sel, :], dst_ref=placed(t, blk, sel, self.half_rows[t]),
                send_sem=ici_send.at[3 * t + k], recv_sem=ici_recv.at[3 * t + k],
                device_id=(px, py, c), device_id_type=MESH_T)

        def d2d(t, k, half):
            px, py = peers[k]
            piece = placed(t, 2 * px + py, rows(t, half), self.half_rows[t])
            return pltpu.make_async_remote_copy(
                src_ref=piece, dst_ref=piece, send_sem=d2d_send.at[3 * t + k], recv_sem=d2d_recv.at[3 * t + k],
                device_id=(x, y, 1 - c), device_id_type=MESH_T)

        def own_chunk(t, r0):
            rt = self.chunk_rows[t]
            sel = pl.ds(r0, rt)
            return ins[t].at[layer, sel, :], placed(t, mine, sel, rt), stage[t], stage_sems

        return c, mine, peers, ici, d2d, own_chunk

    def start(self, ins, outs, scratch):
        c, mine, peers, ici, d2d, own_chunk = self._views(ins, outs, scratch)
        for t in range(self.nt):
            for k in range(3):
                ici(t, k, mine).start()
        starts = [list(range(0, self.shapes[t][1], self.chunk_rows[t])) for t in range(self.nt)]
        for r in range(max(len(s) for s in starts)):
            active = [(t, *own_chunk(t, starts[t][r])) for t in range(self.nt) if r < len(starts[t])]
            loads = [pltpu.make_async_copy(src, buf, sems.at[2 * t]) for t, src, dst, buf, sems in active]
            for cp in loads:
                cp.start()
            for cp in loads:
                cp.wait()
            stores = [pltpu.make_async_copy(buf, dst, sems.at[2 * t + 1]) for t, src, dst, buf, sems in active]
            for cp in stores:
                cp.start()
            for cp in stores:
                cp.wait()

    def finish(self, ins, outs, scratch):
        c, mine, peers, ici, d2d, own_chunk = self._views(ins, outs, scratch)
        for t in range(self.nt):
            for k, (px, py) in enumerate(peers):
                ici(t, k, 2 * px + py).wait_recv()
                if self.split[t]:
                    d2d(t, k, c).start()
        for t in range(self.nt):
            for k in range(3):
                if self.split[t]:
                    d2d(t, k, 1 - c).wait_recv()
        for t in range(self.nt):
            for k in range(3):
                ici(t, k, mine).wait_send()
                if self.split[t]:
                    d2d(t, k, c).wait_send()


def all_gather_layer(shards, axes, layer, *, name):
    plan = LayerGather(shards, axes, layer)
    nt = plan.nt

    def body(*refs):
        ins, outs, scratch = refs[:nt], refs[nt:2 * nt], refs[2 * nt:]
        plan.start(ins, outs, scratch)
        plan.finish(ins, outs, scratch)

    return pl.pallas_call(
        body,
        out_shape=tuple(plan.out_shapes()),
        in_specs=[ANY] * nt,
        out_specs=tuple([ANY] * nt),
        scratch_shapes=plan.scratch_shapes(),
        name=name,
    )(*shards)


class HalfLayout:
    def __init__(self, shape, axis):
        self.R, self.C = shape
        self.axis = axis
        if axis == 1:
            self.hr, self.pw = self.R // 2, self.C // 4
            self.half_shape = (self.hr, self.C)
        else:
            self.hr, self.pw = self.R // 8, self.C
            self.half_shape = (4 * self.hr, self.C)
        self.tr = _rows_tile(self.hr, self.pw)
        self.nr = self.hr // self.tr

    def in_grad(self, ref, blk, half):
        if self.axis == 1:
            return ref.at[pl.ds(pl.multiple_of(half * self.hr, 16), self.hr), pl.ds(pl.multiple_of(blk * self.pw, 128), self.pw)]
        return ref.at[pl.ds(pl.multiple_of((2 * blk + half) * self.hr, 16), self.hr), :]

    def in_half(self, ref, blk):
        if self.axis == 1:
            return ref.at[:, pl.ds(pl.multiple_of(blk * self.pw, 128), self.pw)]
        return ref.at[pl.ds(pl.multiple_of(blk * self.hr, 16), self.hr), :]

    def grad_spec(self):
        if self.axis == 1:
            return pl.BlockSpec((self.tr, self.pw), lambda j, i, s: (s[0] * self.nr + i, j))
        return pl.BlockSpec((self.tr, self.pw), lambda j, i, s: ((2 * j + s[0]) * self.nr + i, 0))

    def half_spec(self):
        if self.axis == 1:
            return pl.BlockSpec((self.tr, self.pw), lambda j, i, s: (i, j))
        return pl.BlockSpec((self.tr, self.pw), lambda j, i, s: (j * self.nr + i, 0))


def half_exchange(grads, layouts, *, name):
    nt = len(grads)
    pieces = [(t, j) for t in range(nt) for j in (range(4) if layouts[t].axis == 0 else range(1))]

    def body(*refs):
        ins, outs = refs[:nt], refs[nt:2 * nt]
        send_sems, recv_sems = refs[2 * nt:]
        x, y, c = _mesh_pos()
        cps = []
        for n, (t, j) in enumerate(pieces):
            lay = layouts[t]
            if lay.axis == 1:
                src = ins[t].at[pl.ds(pl.multiple_of((1 - c) * lay.hr, 16), lay.hr), :]
                dst = outs[t]
            else:
                src = lay.in_grad(ins[t], j, 1 - c)
                dst = lay.in_half(outs[t], j)
            cp = pltpu.make_async_remote_copy(src_ref=src, dst_ref=dst, send_sem=send_sems.at[n], recv_sem=recv_sems.at[n],
                                              device_id=(x, y, 1 - c), device_id_type=MESH_T)
            cp.start()
            cps.append(cp)
        for cp in cps:
            cp.wait_recv()
        for cp in cps:
            cp.wait_send()

    return pl.pallas_call(
        body,
        out_shape=tuple(jax.ShapeDtypeStruct(lay.half_shape, F32) for lay in layouts),
        in_specs=[ANY] * nt,
        out_specs=tuple([ANY] * nt),
        scratch_shapes=[pltpu.SemaphoreType.DMA((len(pieces),)), pltpu.SemaphoreType.DMA((len(pieces),))],
        name=name,
    )(*grads)


def pair_sum(grad, other, lay, core, *, name):
    def body(c_ref, g_ref, o_ref, s32_ref, s16_ref):
        s = g_ref[...] + o_ref[...]
        s32_ref[...] = s
        s16_ref[...] = s.astype(BF16)

    return pl.pallas_call(
        body,
        out_shape=(jax.ShapeDtypeStruct(lay.half_shape, F32), jax.ShapeDtypeStruct(lay.half_shape, BF16)),
        grid_spec=pltpu.PrefetchScalarGridSpec(
            num_scalar_prefetch=1, grid=(4, lay.nr),
            in_specs=[lay.grad_spec(), lay.half_spec()],
            out_specs=(lay.half_spec(), lay.half_spec())),
        compiler_params=_cp(("parallel", "parallel")),
        name=name,
    )(core, grad, other)


class BlockScatter:
    def __init__(self, layouts):
        self.layouts = layouts
        self.nt = len(layouts)

    def out_shapes(self):
        return [jax.ShapeDtypeStruct((3, lay.hr, lay.pw), BF16) for lay in self.layouts]

    def scratch_shapes(self):
        return [pltpu.SemaphoreType.DMA((3 * self.nt,)), pltpu.SemaphoreType.DMA((3 * self.nt,))]

    def _copies(self, pairs16, recv, scratch):
        send_sems, recv_sems = scratch
        x, y, c = _mesh_pos()
        return [pltpu.make_async_remote_copy(
            src_ref=lay.in_half(pairs16[t], 2 * px + py), dst_ref=recv[t].at[k],
            send_sem=send_sems.at[3 * t + k], recv_sem=recv_sems.at[3 * t + k],
            device_id=(px, py, c), device_id_type=MESH_T)
            for t, lay in enumerate(self.layouts) for k, (px, py) in enumerate(_chip_peers(x, y))]

    def start(self, pairs16, recv, scratch):
        for cp in self._copies(pairs16, recv, scratch):
            cp.start()

    def finish(self, pairs16, recv, scratch):
        copies = self._copies(pairs16, recv, scratch)
        for cp in copies:
            cp.wait_recv()
        for cp in copies:
            cp.wait_send()


def gather_small(small, *, name):
    def body(small_in, small_out, ssend, srecv):
        x, y, c = _mesh_pos()
        me = 4 * x + 2 * y + c
        sends, recvs = [], []
        for r, (px, py, pc) in enumerate(_all_peers(x, y, c)):
            def mk(slot, r=r, px=px, py=py, pc=pc):
                return pltpu.make_async_remote_copy(
                    src_ref=small_in, dst_ref=small_out.at[slot], send_sem=ssend.at[r], recv_sem=srecv.at[r],
                    device_id=(px, py, pc), device_id_type=MESH_T)
            snd = mk(me)
            snd.start()
            sends.append(snd)
            recvs.append(mk(4 * px + 2 * py + pc))
        for r in recvs:
            r.wait_recv()
        for s in sends:
            s.wait_send()

    return pl.pallas_call(
        body,
        out_shape=jax.ShapeDtypeStruct((8,) + small.shape, F32),
        in_specs=[ANY],
        out_specs=ANY,
        scratch_shapes=[pltpu.SemaphoreType.DMA((7,)), pltpu.SemaphoreType.DMA((7,))],
        name=name,
    )(small)


def sum_chips(pair32, recv, lay, chip, *, name):
    def body(j_ref, p_ref, r_ref, s_ref):
        acc = p_ref[...]
        for k in range(3):
            acc = acc + r_ref[k].astype(F32)
        s_ref[...] = acc

    if lay.axis == 1:
        own = pl.BlockSpec((lay.tr, lay.pw), lambda i, j: (i, j[0]))
    else:
        own = pl.BlockSpec((lay.tr, lay.pw), lambda i, j: (j[0] * lay.nr + i, 0))
    return pl.pallas_call(
        body,
        out_shape=jax.ShapeDtypeStruct((lay.hr, lay.pw), F32),
        grid_spec=pltpu.PrefetchScalarGridSpec(
            num_scalar_prefetch=1, grid=(lay.nr,),
            in_specs=[own, pl.BlockSpec((3, lay.tr, lay.pw), lambda i, j: (0, i, 0))],
            out_specs=pl.BlockSpec((lay.tr, lay.pw), lambda i, j: (i, 0))),
        compiler_params=_cp(("parallel",)),
        name=name,
    )(chip, pair32, recv)


def sum_devices(gathered, own, me, *, name):
    _, R, C = gathered.shape

    def body(me_ref, g_ref, o_ref, s_ref):
        acc = None
        for k in range(8):
            part = jnp.where(me_ref[0] == k, o_ref[...], g_ref[k])
            acc = part if acc is None else acc + part
        s_ref[...] = acc

    return pl.pallas_call(
        body,
        out_shape=jax.ShapeDtypeStruct((R, C), F32),
        grid_spec=pltpu.PrefetchScalarGridSpec(
            num_scalar_prefetch=1, grid=(1,),
            in_specs=[pl.BlockSpec((8, R, C), lambda i, m: (0, 0, 0)), pl.BlockSpec((R, C), lambda i, m: (0, 0))],
            out_specs=pl.BlockSpec((R, C), lambda i, m: (0, 0))),
        compiler_params=_cp(("arbitrary",)),
        name=name,
    )(me, gathered, own)


def sibling_swap(parts, *, name):
    nt = len(parts)

    def body(*refs):
        ins, outs = refs[:nt], refs[nt:2 * nt]
        send_sems, recv_sems = refs[2 * nt:]
        x, y, c = _mesh_pos()
        cps = []
        for t in range(nt):
            cp = pltpu.make_async_remote_copy(src_ref=ins[t], dst_ref=outs[t], send_sem=send_sems.at[t],
                                              recv_sem=recv_sems.at[t], device_id=(x, y, 1 - c), device_id_type=MESH_T)
            cp.start()
            cps.append(cp)
        for cp in cps:
            cp.wait_recv()
        for cp in cps:
            cp.wait_send()

    return pl.pallas_call(
        body,
        out_shape=tuple(jax.ShapeDtypeStruct(p.shape, p.dtype) for p in parts),
        in_specs=[ANY] * nt,
        out_specs=tuple([ANY] * nt),
        scratch_shapes=[pltpu.SemaphoreType.DMA((nt,)), pltpu.SemaphoreType.DMA((nt,))],
        name=name,
    )(*parts)


def adamw_halves(w, m, v, mine, other, lay, core, *, name):
    _, r, c = w.shape
    tr, nr = lay.tr, lay.nr
    assert (r, c) == (2 * lay.hr, lay.pw), (w.shape, lay.hr, lay.pw)

    def body(c_ref, w_ref, m_ref, v_ref, *rest):
        g_refs, outs = rest[:2 * DEPTH], rest[2 * DEPTH:]
        l, h = pl.program_id(0), pl.program_id(1)
        g = None
        for d in range(DEPTH):
            gd = jnp.where(h == c_ref[0], g_refs[d][...], g_refs[DEPTH + d][...])
            g = gd if g is None else jnp.where(l == d, gd, g)
        _adamw_math(g, w_ref, m_ref, v_ref, *outs)

    full = pl.BlockSpec((None, tr, c), lambda l, h, i, s: (l, h * nr + i, 0))

    def part(d, is_mine):
        def index(l, h, i, s):
            used = jnp.logical_and(l == d, (h == s[0]) == is_mine)
            return jnp.where(used, i, 0), 0
        return pl.BlockSpec((tr, c), index)

    shp = jax.ShapeDtypeStruct((DEPTH, r, c), F32)
    return pl.pallas_call(
        body,
        out_shape=(shp, shp, shp, shp),
        grid_spec=pltpu.PrefetchScalarGridSpec(
            num_scalar_prefetch=1, grid=(DEPTH, 2, nr),
            in_specs=[full, full, full] + [part(d, True) for d in range(DEPTH)] + [part(d, False) for d in range(DEPTH)],
            out_specs=(full, full, full, full)),
        compiler_params=_cp(("arbitrary", "arbitrary", "arbitrary")),
        name=name,
    )(core, w, m, v, *mine, *other)


WEIGHTS = ("norm_mix", "w_in", "b_gate", "q_norm_a", "k_norm_a", "rel_bias_a", "w_pool", "pool_scale",
           "w_branch_a", "w_branch_b", "w_branch_c", "w_out", "norm_ffn", "w_up", "conv_w", "conv_b", "w_down")
SHARDED = {"w_in": 2, "w_branch_a": 2, "w_branch_b": 2, "w_branch_c": 2, "w_out": 1, "w_up": 2, "conv_w": 2,
           "w_down": 1}
REPLICATED = tuple(n for n in WEIGHTS if n not in SHARDED)
MATMUL_WEIGHTS = tuple(n for n in SHARDED if n != "conv_w")
SMALL_WEIGHTS = tuple(n for n in WEIGHTS if n not in MATMUL_WEIGHTS)
SMALL_ROWS = 1496


def _layer_fwd(x, p, full, tables, rest=None, prefetch=None):
    l = p["l"]
    diag = exact_dot(p["rel_bias_a"], tables["onehot_t"], name="bias_diagonals")
    diag = diag.reshape(N_HEADS, N_VARIANTS, 1, DIAG_W).transpose(1, 0, 2, 3)
    biasm = bias_expand(diag, name="bias_expand")
    gq8 = jnp.tile(p["q_norm_a"], N_HEADS)[None]
    gk8 = jnp.tile(p["k_norm_a"], N_HEADS)[None]
    h = rmsnorm_fwd(x, p["norm_mix"][None], name="rmsnorm_fwd")
    if rest is None:
        proj = matmul(h, full["w_in"], b_layer=l, name="mm_in")
    else:
        plan, shards, names = rest
        proj, gathered = matmul(h, full["w_in"], b_layer=l, behind=(plan, shards), name="mm_in_gather")
        full = {**full, **dict(zip(names, gathered))}
    qa, ka, va, qb, kb, vb = qkv_prep(proj, gq8, gk8, name="qkv_prep")
    oa = attn_a_fwd(qa, ka, va, biasm, name="attn_a_fwd")
    if prefetch is None:
        ob, tot, nblk = attn_b_fwd(qb, kb, vb, name="attn_b_fwd")
    else:
        plan, shards, names = prefetch
        ob, tot, nblk, filled = attn_b_fwd(qb, kb, vb, gather=(plan, shards, [full[n] for n in names]),
                                           name="attn_b_fwd_gather")
        full = dict(zip(names, filled))
    wpool = p["w_pool"].astype(BF16)
    oc = pool_fwd(proj, wpool, p["pool_scale"][None], name="pool_fwd")
    merged = merge_fwd(oa, ob, oc, proj, p["b_gate"][None], full["w_branch_a"][l], full["w_branch_b"][l],
                       full["w_branch_c"][l], name="merge_fwd")
    x1 = matmul(merged, full["w_out"], b_layer=l, add=x, name="mm_out")
    h2 = rmsnorm_fwd(x1, p["norm_ffn"][None], name="rmsnorm_fwd")
    u = matmul(h2, full["w_up"], b_layer=l, name="mm_up")
    a = conv_glu_fwd(u, full["conv_w"][l], p["conv_b"][None], name="conv_glu_fwd")
    x2 = matmul(a, full["w_down"], b_layer=l, add=x1, name="mm_down")
    saved = dict(x=x, h=h, proj=proj, qa=qa, ka=ka, va=va, qb=qb, kb=kb, vb=vb, oa=oa, ob=ob, tot=tot, nblk=nblk, oc=oc,
                 merged=merged, x1=x1, h2=h2, u=u, a=a, biasm=biasm, gq8=gq8, gk8=gk8, wpool=wpool)
    return x2, saved, full


class GradReducer:
    def __init__(self, layouts, core, chip):
        self.layouts, self.core, self.chip = layouts, core, chip
        self.pairs32, self.received, self.pending = {}, {}, []

    def prepare(self, layer, grads):
        names = list(grads)
        lays = [self.layouts[n] for n in names]
        others = half_exchange([grads[n] for n in names], lays, name="half_exchange")
        for n, lay, other in zip(names, lays, others):
            p32, p16 = pair_sum(grads[n], other, lay, self.core, name="pair_sum")
            self.pairs32[(layer, n)] = p32
            self.pending.append(((layer, n), p16))

    def take(self):
        keys = [k for k, _ in self.pending]
        pairs16 = [p for _, p in self.pending]
        self.pending = []
        return keys, BlockScatter([self.layouts[n] for _, n in keys]), pairs16

    def store(self, keys, received):
        self.received.update(zip(keys, received))

    def finish(self):
        keys = list(self.pairs32)
        mine = [sum_chips(self.pairs32[k], self.received[k], self.layouts[k[1]], self.chip, name="sum_chips") for k in keys]
        other = sibling_swap(mine, name="sibling_swap")
        return dict(zip(keys, mine)), dict(zip(keys, other))


EARLY_WEIGHTS = ("w_down", "w_up", "w_out", "w_branch_a", "w_branch_b", "w_branch_c")


def _layer_bwd(dx2, s, p, tables, reducer):
    g = {}
    full, l = p["full"], p["l"]
    da = matmul(dx2, full["w_down"], b_layer=l, tb=True, name="mm_down_dx")
    g["w_down"] = matmul(s["a"], dx2, ta=True, name="mm_down_dw")
    dug, duv, dcwg, dcwv, dcbg, dcbv = conv_glu_bwd(s["u"], p["conv_w"], p["conv_b"][None], da, name="conv_glu_bwd")
    du = jnp.concatenate([dug, duv], axis=1)
    g["conv_w"] = jnp.concatenate([dcwg, dcwv], axis=1)
    g["conv_b"] = jnp.concatenate([dcbg, dcbv], axis=1)[0]
    g["w_up"] = matmul(s["h2"], du, ta=True, name="mm_up_dw")
    dx1, dg2 = matmul(du, full["w_up"], b_layer=l, tb=True, norm_bwd=(s["x1"], p["norm_ffn"][None], dx2),
                      name="mm_up_dx_norm")
    g["norm_ffn"] = dg2[0]
    dmerged = matmul(dx1, full["w_out"], b_layer=l, tb=True, name="mm_out_dx")
    g["w_out"] = matmul(s["merged"], dx1, ta=True, name="mm_out_dw")
    t_a, t_b, t_c, dga, dgb, dgc, dba, dbb, dbc = merge_bwd(
        dmerged, s["oa"], s["ob"], s["oc"], s["proj"], p["b_gate"][None], p["w_branch_a"], p["w_branch_b"],
        p["w_branch_c"], name="merge_bwd")
    g["b_gate"] = jnp.concatenate([dba, dbb, dbc], axis=1)[0]
    g["w_branch_a"] = matmul(s["oa"], t_a, ta=True, name="mm_branch_dw")
    g["w_branch_b"] = matmul(s["ob"], t_b, ta=True, name="mm_branch_dw")
    g["w_branch_c"] = matmul(s["oc"], t_c, ta=True, name="mm_branch_dw")
    doa = matmul(t_a, full["w_branch_a"], b_layer=l, tb=True, out_dtype=BF16, name="mm_branch_dx")
    dob = matmul(t_b, full["w_branch_b"], b_layer=l, tb=True, out_dtype=BF16, name="mm_branch_dx")
    doc = matmul(t_c, full["w_branch_c"], b_layer=l, tb=True, name="mm_branch_dx_f32")
    dqh, dkh, dva, dbias = attn_a_bwd(s["qa"], s["ka"], s["va"], s["biasm"], doa, name="attn_a_bwd")
    ddiag = relbias_reduce(dbias, name="relbias_reduce")
    ddiag = ddiag.transpose(1, 0, 2, 3).reshape(N_HEADS, N_VARIANTS * DIAG_W)
    g["rel_bias_a"] = exact_dot(ddiag, tables["onehot"], name="relbias_table")
    dqa, dka, dgq8, dgk8 = qknorm_bwd(s["proj"], s["gq8"], s["gk8"], dqh, dkh, name="qknorm_bwd")
    g["q_norm_a"] = dgq8.reshape(N_HEADS, HEAD_DIM).sum(axis=0)
    g["k_norm_a"] = dgk8.reshape(N_HEADS, HEAD_DIM).sum(axis=0)
    reducer.prepare(l, {n: g[n] for n in EARLY_WEIGHTS})
    keys, plan, pairs16 = reducer.take()
    dqb, dkb, dvb, received = attn_b_bwd(s["qb"], s["kb"], s["vb"], s["tot"], s["nblk"], dob, scatter=(plan, pairs16),
                                         name="attn_b_bwd_scatter")
    reducer.store(keys, received)
    duc, dwp, dsc = pool_bwd(s["proj"], s["wpool"], p["pool_scale"][None], doc, name="pool_bwd")
    g["w_pool"] = dwp
    g["pool_scale"] = dsc[0]
    dproj = jnp.concatenate([dqa, dka, dva.astype(BF16), dqb, dkb.astype(BF16), dvb.astype(BF16), duc,
                             dga, dgb, dgc], axis=1)
    g["w_in"] = matmul(s["h"], dproj, ta=True, name="mm_in_dw")
    reducer.prepare(l, {"w_in": g["w_in"]})
    norm = (s["x"], p["norm_mix"][None], dx1)
    if l > 0:
        dx, dg1 = matmul(dproj, full["w_in"], b_layer=l, tb=True, norm_bwd=norm, name="mm_in_dx_norm")
    else:
        keys, plan, pairs16 = reducer.take()
        dx, dg1, received = matmul(dproj, full["w_in"], b_layer=l, tb=True, norm_bwd=norm, behind=(plan, pairs16),
                                   name="mm_in_dx_norm_scatter")
        reducer.store(keys, received)
    g["norm_mix"] = dg1[0]
    return dx, g


def kernel(x, norm_mix, w_in, b_gate, q_norm_a, k_norm_a, rel_bias_a, w_pool, pool_scale, w_branch_a, w_branch_b, w_branch_c, w_out, norm_ffn, w_up, conv_w, conv_b, w_down, loss_target, m_norm_mix, m_w_in, m_b_gate, m_q_norm_a, m_k_norm_a, m_rel_bias_a, m_w_pool, m_pool_scale, m_w_branch_a, m_w_branch_b, m_w_branch_c, m_w_out, m_norm_ffn, m_w_up, m_conv_w, m_conv_b, m_w_down, v_norm_mix, v_w_in, v_b_gate, v_q_norm_a, v_k_norm_a, v_rel_bias_a, v_w_pool, v_pool_scale, v_w_branch_a, v_w_branch_b, v_w_branch_c, v_w_out, v_norm_ffn, v_w_up, v_conv_w, v_conv_b, v_w_down):
    w = dict(zip(WEIGHTS, (norm_mix, w_in, b_gate, q_norm_a, k_norm_a, rel_bias_a, w_pool, pool_scale, w_branch_a,
                           w_branch_b, w_branch_c, w_out, norm_ffn, w_up, conv_w, conv_b, w_down)))
    m = dict(zip(WEIGHTS, (m_norm_mix, m_w_in, m_b_gate, m_q_norm_a, m_k_norm_a, m_rel_bias_a, m_w_pool, m_pool_scale,
                           m_w_branch_a, m_w_branch_b, m_w_branch_c, m_w_out, m_norm_ffn, m_w_up, m_conv_w, m_conv_b,
                           m_w_down)))
    v = dict(zip(WEIGHTS, (v_norm_mix, v_w_in, v_b_gate, v_q_norm_a, v_k_norm_a, v_rel_bias_a, v_w_pool, v_pool_scale,
                           v_w_branch_a, v_w_branch_b, v_w_branch_c, v_w_out, v_norm_ffn, v_w_up, v_conv_w, v_conv_b,
                           v_w_down)))
    onehot = diagonal_onehot()
    tables = dict(onehot=jnp.asarray(onehot), onehot_t=jnp.asarray(np.ascontiguousarray(onehot.T)))

    names = tuple(SHARDED)
    shards = [w[n] if n == "conv_w" else w[n].astype(BF16) for n in names]
    axes = [SHARDED[n] for n in names]
    later = [i for i, n in enumerate(names) if n != "w_in"]
    rest = (LayerGather([shards[i] for i in later], [axes[i] for i in later], 0), [shards[i] for i in later],
            [names[i] for i in later])
    first = names.index("w_in")
    full = {"w_in": all_gather_layer([shards[first]], [axes[first]], 0, name="all_gather_layer")[0]}

    def layer_params(l):
        p = {n: full[n][l] for n in ("w_branch_a", "w_branch_b", "w_branch_c", "conv_w")}
        p.update({n: w[n][l] for n in REPLICATED})
        p.update(full=full, l=l)
        return p

    xs = x[0]
    saved = []
    for l in range(DEPTH):
        prefetch = (LayerGather(shards, axes, l + 1), shards, names) if l + 1 < DEPTH else None
        replicated = {n: w[n][l] for n in REPLICATED}
        xs, s, full = _layer_fwd(xs, dict(replicated, l=l), full, tables, rest if l == 0 else None, prefetch)
        saved.append(s)
    dx, lpart = loss_head(xs, loss_target[0], name="loss_head")
    loss = lax.psum(lpart[0, 0], MESH_AXES)
    as_index = lambda i: jnp.reshape(i, (1,)).astype(jnp.int32)
    cx, cy, cc = _mesh_pos()
    core, chip, me = as_index(cc), as_index(2 * cx + cy), as_index(4 * cx + 2 * cy + cc)
    layouts = {n: HalfLayout((full[n].shape[1], full[n].shape[2]), SHARDED[n] - 1) for n in MATMUL_WEIGHTS}
    reducer = GradReducer(layouts, core, chip)
    grads = [None] * DEPTH
    for l in reversed(range(DEPTH)):
        dx, grads[l] = _layer_bwd(dx, saved[l], layer_params(l), tables, reducer)

    g = {n: jnp.stack([grads[l][n] for l in range(DEPTH)]) for n in SMALL_WEIGHTS}
    flat = jnp.concatenate([g[n].reshape(-1) for n in SMALL_WEIGHTS])
    small = jnp.pad(flat, (0, SMALL_ROWS * 128 - flat.shape[0])).reshape(SMALL_ROWS, 128)
    small_sum = sum_devices(gather_small(small, name="gather_small"), small, me, name="sum_devices").reshape(-1)
    mine, other = reducer.finish()

    res = {}
    for n in MATMUL_WEIGHTS:
        res[n] = adamw_halves(w[n], m[n], v[n], [mine[(l, n)] for l in range(DEPTH)],
                              [other[(l, n)] for l in range(DEPTH)], layouts[n], core, name="adamw_halves")
    off = 0
    for n in SMALL_WEIGHTS:
        shp = g[n].shape
        size = int(np.prod(shp))
        gn = small_sum[off:off + size].reshape(shp)
        off += size
        if n in SHARDED:
            gn = lax.dynamic_slice_in_dim(gn, (2 * cx + cy) * w[n].shape[-1], w[n].shape[-1], axis=len(shp) - 1)
        shp = w[n].shape
        cols = shp[-1]
        two_d = lambda t: t.reshape(int(np.prod(shp)) // cols, cols)
        res[n] = [t.reshape(shp) for t in adamw(two_d(w[n]), two_d(m[n]), two_d(v[n]), two_d(gn), name="adamw")]

    out = [loss, dx[None]]
    for k in range(4):
        out.extend(res[n][k] for n in WEIGHTS)
    return tuple(out)
```

```python
import jax
import jax.numpy as jnp
import numpy as np
from jax import lax
from jax.experimental import pallas as pl
from jax.experimental.pallas import tpu as pltpu

F32 = jnp.float32
BF16 = jnp.bfloat16

D_MODEL = 1024
DEPTH = 2
CHUNK = 64
N_LEFT = 8
HEAD_DIM = 64
N_HEADS = 8
WIDTH = 512
POOL_WINDOWS = (2, 4, 8, 16)
GROUP_DIM = 128
MAX_REL = 2 * CHUNK
REL_TABLE = MAX_REL + CHUNK
D_FF = 2816
EPS = 1e-6
QK_SCALE = 0.125
GATE_COL0 = 7 * WIDTH

ADAM_LR = 0.001
ADAM_B1 = 0.9
ADAM_B2 = 0.999
ADAM_EPS = 1e-08
ADAM_WD = 0.01
ADAM_STEP = 10

VMEM_LIMIT = 56 * 1024 * 1024
ATT_Q = 256
A_Q = 256
A_WIN = A_Q + N_LEFT * CHUNK
A_FWD_HEADS = 8
A_BWD_HEADS = 4
B_FWD_HEADS = 8
B_BWD_HEADS = 4
HALO = 16
CONV_HALO = 8
NEG = -1e30

MESH_AXES = ("x", "y", "c")
MESH_T = pl.DeviceIdType.MESH


def _cp(sem=None, vmem=VMEM_LIMIT):
    return pltpu.CompilerParams(dimension_semantics=sem, vmem_limit_bytes=vmem)


def _dot(a, b, ca, cb):
    return lax.dot_general(a, b, (((ca,), (cb,)), ((), ())), preferred_element_type=F32)


def _tile(n, cands=(1024, 512, 256, 128)):
    for c in cands:
        if n % c == 0:
            return c
    return n


def _split_hi_lo(v):
    hi = v.astype(BF16)
    lo = (v - hi.astype(F32)).astype(BF16)
    return hi, lo


def matmul(a, b, *, ta=False, tb=False, add=None, norm_bwd=None, out_dtype=F32, b_layer=None, behind=None, name):
    plan, sources = behind if behind is not None else (None, [])
    n_src = len(sources)
    n_dst = len(plan.out_shapes()) if plan is not None else 0
    assert add is None or norm_bwd is None
    if ta:
        K, M = a.shape
    else:
        M, K = a.shape
    if tb:
        N, K2 = b.shape[-2:]
    else:
        K2, N = b.shape[-2:]
    assert K == K2, (a.shape, b.shape, ta, tb)
    big = (1024, 1408, 512, 256, 128)
    tm = _tile(M, big)
    tn = _tile(N, (1664,) + big)
    tk = _tile(K, big if ta else (1664, 1408) + big if norm_bwd is not None else (3328, 2816) + big)
    nk = K // tk

    n_extra = 1 if add is not None else 3 if norm_bwd is not None else 0
    n_in = 2 + n_extra + n_src
    n_out = 2 if norm_bwd is not None else 1
    grid = (M // tm, N // tn, nk)
    assert norm_bwd is None or tn == N, "the norm gradient needs whole rows in one output tile"

    def body(*refs):
        a_ref, b_ref = refs[:2]
        extra = refs[2:2 + n_extra]
        o_ref, acc = refs[n_in], refs[n_in + n_out + n_dst]
        i, j, k = pl.program_id(0), pl.program_id(1), pl.program_id(2)
        if plan is not None:
            comm = (refs[n_in - n_src:n_in], refs[n_in + n_out:n_in + n_out + n_dst], refs[n_in + n_out + n_dst + 1:])

            @pl.when((i == 0) & (j == 0) & (k == 0))
            def _():
                plan.start(*comm)

        @pl.when(k == 0)
        def _():
            acc[...] = jnp.zeros_like(acc)

        av = a_ref[...].astype(BF16)
        bv = b_ref[...].astype(BF16)
        acc[...] += _dot(av, bv, 0 if ta else 1, 1 if tb else 0)

        @pl.when(k == nk - 1)
        def _():
            r = acc[...]
            if add is not None:
                r = r + extra[0][...].astype(F32)
            if norm_bwd is not None:
                x_ref, g_ref, dres_ref = extra
                dg_ref = refs[n_in + 1]
                xv = x_ref[...]
                inv = lax.rsqrt(jnp.mean(xv * xv, axis=-1, keepdims=True) + EPS)
                gd = r * g_ref[...]
                mean = jnp.mean(xv * gd, axis=-1, keepdims=True)

                @pl.when(i == 0)
                def _():
                    dg_ref[...] = jnp.zeros_like(dg_ref)

                dg_ref[...] += jnp.sum(r * xv * inv, axis=0, keepdims=True)
                r = dres_ref[...] + inv * gd - xv * (inv * inv * inv * mean)
            o_ref[...] = r.astype(out_dtype)

        if plan is not None:
            @pl.when((i == grid[0] - 1) & (j == grid[1] - 1) & (k == nk - 1))
            def _():
                plan.finish(*comm)

    a_spec = pl.BlockSpec((tk, tm), lambda i, j, k: (k, i)) if ta else pl.BlockSpec((tm, tk), lambda i, j, k: (i, k))
    if b_layer is None:
        b_spec = pl.BlockSpec((tn, tk), lambda i, j, k: (j, k)) if tb else pl.BlockSpec((tk, tn), lambda i, j, k: (k, j))
    elif tb:
        b_spec = pl.BlockSpec((None, tn, tk), lambda i, j, k: (b_layer, j, k))
    else:
        b_spec = pl.BlockSpec((None, tk, tn), lambda i, j, k: (b_layer, k, j))
    o_spec = pl.BlockSpec((tm, tn), lambda i, j, k: (i, j))
    vec_spec = pl.BlockSpec((1, tn), lambda i, j, k: (0, j))
    in_specs = [a_spec, b_spec]
    args = [a, b]
    out_shape = [jax.ShapeDtypeStruct((M, N), out_dtype)]
    out_specs = [o_spec]
    if add is not None:
        in_specs.append(o_spec)
        args.append(add)
    if norm_bwd is not None:
        in_specs += [o_spec, vec_spec, o_spec]
        args += list(norm_bwd)
        out_shape.append(jax.ShapeDtypeStruct((1, N), F32))
        out_specs.append(vec_spec)
    sequential = plan is not None or norm_bwd is not None
    any_space = pl.BlockSpec(memory_space=pl.ANY)
    outs = pl.pallas_call(
        body,
        out_shape=tuple(out_shape) + (tuple(plan.out_shapes()) if plan is not None else ()),
        grid=grid,
        in_specs=in_specs + [any_space] * n_src,
        out_specs=tuple(out_specs) + tuple([any_space] * n_dst),
        scratch_shapes=[pltpu.VMEM((tm, tn), F32)] + (list(plan.scratch_shapes()) if plan is not None else []),
        compiler_params=_cp(("arbitrary" if sequential else "parallel",) * 2 + ("arbitrary",)),
        name=name,
    )(*args, *sources)
    if plan is None:
        return outs[0] if n_out == 1 else tuple(outs)
    return tuple(outs[:n_out]) + (list(outs[n_out:]),) if n_out > 1 else (outs[0], list(outs[1:]))


def rmsnorm_fwd(x, g, *, name):
    S, D = x.shape
    T = _tile(S)

    def body(x_ref, g_ref, h_ref):
        xv = x_ref[...]
        r = lax.rsqrt(jnp.mean(xv * xv, axis=-1, keepdims=True) + EPS)
        h_ref[...] = (xv * r * g_ref[...]).astype(BF16)

    return pl.pallas_call(
        body,
        out_shape=jax.ShapeDtypeStruct((S, D), BF16),
        grid=(S // T,),
        in_specs=[pl.BlockSpec((T, D), lambda i: (i, 0)), pl.BlockSpec((1, D), lambda i: (0, 0))],
        out_specs=pl.BlockSpec((T, D), lambda i: (i, 0)),
        compiler_params=_cp(("parallel",)),
        name=name,
    )(x, g)


def _head_mean_matrix():
    r = lax.broadcasted_iota(jnp.int32, (WIDTH, WIDTH), 0) // HEAD_DIM
    c = lax.broadcasted_iota(jnp.int32, (WIDTH, WIDTH), 1) // HEAD_DIM
    return jnp.where(r == c, 1.0 / HEAD_DIM, 0.0).astype(BF16)


def _head_mean(v, mm):
    hi, lo = _split_hi_lo(v)
    return _dot(hi, mm, 1, 0) + _dot(lo, mm, 1, 0)


def qkv_prep(proj, gq, gk, *, name):
    S = proj.shape[0]
    T = _tile(S)

    def body(qa, ka, va, qb, kb, vb, gq_ref, gk_ref, oqa, oka, ova, oqb, okb, ovb):
        mm = _head_mean_matrix()
        for src, gref, dst, scale in ((qa, gq_ref, oqa, QK_SCALE), (ka, gk_ref, oka, 1.0)):
            v = src[...]
            r = lax.rsqrt(_head_mean(v * v, mm) + EPS)
            dst[...] = (v * r * gref[...] * scale).astype(BF16)
        oqb[...] = (qb[...] * QK_SCALE).astype(BF16)
        for src, dst in ((va, ova), (kb, okb), (vb, ovb)):
            dst[...] = src[...].astype(BF16)

    col = lambda j: pl.BlockSpec((T, WIDTH), lambda i, j=j: (i, j))
    vec = pl.BlockSpec((1, WIDTH), lambda i: (0, 0))
    out = pl.BlockSpec((T, WIDTH), lambda i: (i, 0))
    return pl.pallas_call(
        body,
        out_shape=tuple(jax.ShapeDtypeStruct((S, WIDTH), BF16) for _ in range(6)),
        grid=(S // T,),
        in_specs=[col(0), col(1), col(2), col(3), col(4), col(5), vec, vec],
        out_specs=tuple(out for _ in range(6)),
        compiler_params=_cp(("parallel",)),
        name=name,
    )(proj, proj, proj, proj, proj, proj, gq, gk)


def qknorm_bwd(proj, gq, gk, dqh, dkh, *, name):
    S = proj.shape[0]
    T = _tile(S)

    def body(qa, ka, gq_ref, gk_ref, dq_ref, dk_ref, oq, ok, ogq, ogk):
        i = pl.program_id(0)
        mm = _head_mean_matrix()

        @pl.when(i == 0)
        def _():
            ogq[...] = jnp.zeros_like(ogq)
            ogk[...] = jnp.zeros_like(ogk)

        for src, gref, dref, dst, gdst in ((qa, gq_ref, dq_ref, oq, ogq), (ka, gk_ref, dk_ref, ok, ogk)):
            v = src[...]
            dy = dref[...]
            r = lax.rsqrt(_head_mean(v * v, mm) + EPS)
            gd = dy * gref[...]
            m = _head_mean(v * gd, mm)
            dst[...] = (r * gd - v * (r * r * r * m)).astype(BF16)
            gdst[...] += jnp.sum(dy * v * r, axis=0, keepdims=True)

    col = lambda j: pl.BlockSpec((T, WIDTH), lambda i, j=j: (i, j))
    vec = pl.BlockSpec((1, WIDTH), lambda i: (0, 0))
    row = pl.BlockSpec((T, WIDTH), lambda i: (i, 0))
    return pl.pallas_call(
        body,
        out_shape=(jax.ShapeDtypeStruct((S, WIDTH), BF16), jax.ShapeDtypeStruct((S, WIDTH), BF16),
                   jax.ShapeDtypeStruct((1, WIDTH), F32), jax.ShapeDtypeStruct((1, WIDTH), F32)),
        grid=(S // T,),
        in_specs=[col(0), col(1), vec, vec, row, row],
        out_specs=(row, row, vec, vec),
        compiler_params=_cp(("arbitrary",)),
        name=name,
    )(proj, proj, gq, gk, dqh, dkh)


DIAG_W = 1024
N_VARIANTS = N_LEFT * CHUNK // A_Q + 1


def diagonal_onehot():
    jj = np.arange(DIAG_W)
    diff = np.where(jj < A_WIN, jj, jj - DIAG_W)
    out = np.zeros((N_VARIANTS, DIAG_W, REL_TABLE), np.float32)
    for v in range(N_VARIANTS):
        rel = np.clip(A_Q * v - diff, -(CHUNK - 1), MAX_REL) + (CHUNK - 1)
        out[v, jj, rel] = 1.0
    return out.reshape(N_VARIANTS * DIAG_W, REL_TABLE)


def exact_dot(a, b, *, name):
    def body(a_ref, b_ref, o_ref):
        o_ref[...] = jnp.dot(a_ref[...], b_ref[...], precision=lax.Precision.HIGHEST, preferred_element_type=F32)

    return pl.pallas_call(body, out_shape=jax.ShapeDtypeStruct((a.shape[0], b.shape[1]), F32),
                          compiler_params=_cp(), name=name)(a, b)


def _band_valid(v):
    qc = (lax.broadcasted_iota(jnp.int32, (A_Q, A_WIN), 0) + A_Q * v) // CHUNK
    kc = lax.broadcasted_iota(jnp.int32, (A_Q, A_WIN), 1) // CHUNK
    return (kc <= qc) & (kc >= qc - N_LEFT)


def bias_expand(diag, *, name):
    def body(d_ref, o_ref):
        rows = jnp.broadcast_to(d_ref[0, 0], (A_Q, DIAG_W))
        skew = pltpu.roll(rows, 0, 1, stride=1, stride_axis=0)
        o_ref[0, 0] = jnp.where(_band_valid(pl.program_id(0)), skew[:, :A_WIN], NEG)

    return pl.pallas_call(
        body,
        out_shape=jax.ShapeDtypeStruct((N_VARIANTS, N_HEADS, A_Q, A_WIN), F32),
        grid=(N_VARIANTS, N_HEADS),
        in_specs=[pl.BlockSpec((1, 1, 1, DIAG_W), lambda v, h: (v, h, 0, 0))],
        out_specs=pl.BlockSpec((1, 1, A_Q, A_WIN), lambda v, h: (v, h, 0, 0)),
        compiler_params=_cp(("parallel", "parallel")),
        name=name,
    )(diag)


def relbias_reduce(dbias, *, name):
    def body(db_ref, o_ref):
        acc = None
        for a in range(A_Q // 8):
            x = jnp.concatenate([db_ref[0, 0, 8 * a:8 * a + 8, :], jnp.zeros((8, DIAG_W - A_WIN), F32)], axis=1)
            x = pltpu.roll(x, DIAG_W - 8 * a, 1) if a else x
            acc = x if acc is None else acc + x
        row = lax.broadcasted_iota(jnp.int32, (8, DIAG_W), 0)
        for b in range(3):
            acc = jnp.where((row >> b) & 1 == 1, pltpu.roll(acc, DIAG_W - (1 << b), 1), acc)
        o_ref[0, 0] = jnp.sum(acc, axis=0, keepdims=True)

    return pl.pallas_call(
        body,
        out_shape=jax.ShapeDtypeStruct((N_VARIANTS, N_HEADS, 1, DIAG_W), F32),
        grid=(N_VARIANTS, N_HEADS),
        in_specs=[pl.BlockSpec((1, 1, A_Q, A_WIN), lambda v, h: (v, h, 0, 0))],
        out_specs=pl.BlockSpec((1, 1, 1, DIAG_W), lambda v, h: (v, h, 0, 0)),
        compiler_params=_cp(("parallel", "parallel")),
        name=name,
    )(dbias)


def _a_window_start(qb):
    return pl.multiple_of(jnp.maximum(qb * A_Q - N_LEFT * CHUNK, 0), A_Q)


def attn_a_fwd(q, k, v, biasm, *, name):
    S = q.shape[0]
    nq = S // A_Q

    def body(q_ref, k_ref, v_ref, b_ref, o_ref):
        qb = pl.program_id(1)
        start = _a_window_start(qb)
        outs = []
        for h in range(A_FWD_HEADS):
            lanes = slice(h * HEAD_DIM, (h + 1) * HEAD_DIM)
            qh = q_ref[:, lanes]
            kw = k_ref[pl.ds(start, A_WIN), lanes]
            vw = v_ref[pl.ds(start, A_WIN), lanes]
            s = _dot(qh, kw, 1, 1) + b_ref[0, h]
            m = jnp.max(s, axis=-1, keepdims=True)
            e = jnp.exp(s - m)
            outs.append(_dot(e.astype(BF16), vw, 1, 0) * (1.0 / jnp.sum(e, axis=-1, keepdims=True)))
        o_ref[...] = jnp.concatenate(outs, axis=1).astype(BF16)

    qspec = pl.BlockSpec((A_Q, A_FWD_HEADS * HEAD_DIM), lambda hp, qb: (qb, hp))
    kvspec = pl.BlockSpec((S, A_FWD_HEADS * HEAD_DIM), lambda hp, qb: (0, hp))
    bspec = pl.BlockSpec((1, A_FWD_HEADS, A_Q, A_WIN), lambda hp, qb: (jnp.minimum(qb, N_VARIANTS - 1), hp, 0, 0))
    return pl.pallas_call(
        body,
        out_shape=jax.ShapeDtypeStruct((S, WIDTH), BF16),
        grid=(N_HEADS // A_FWD_HEADS, nq),
        in_specs=[qspec, kvspec, kvspec, bspec],
        out_specs=qspec,
        compiler_params=_cp(("parallel", "arbitrary")),
        name=name,
    )(q, k, v, biasm)


def attn_a_bwd(q, k, v, biasm, do, *, name):
    S = q.shape[0]
    nq = S // A_Q

    def body(q_ref, k_ref, v_ref, b_ref, do_ref, dq_ref, dk_ref, dv_ref, db_ref):
        qb = pl.program_id(1)
        start = _a_window_start(qb)

        @pl.when(qb == 0)
        def _():
            dk_ref[...] = jnp.zeros_like(dk_ref)
            dv_ref[...] = jnp.zeros_like(dv_ref)

        @pl.when(qb < N_VARIANTS)
        def _():
            db_ref[...] = jnp.zeros_like(db_ref)

        dqs = []
        for h in range(A_BWD_HEADS):
            lanes = slice(h * HEAD_DIM, (h + 1) * HEAD_DIM)
            qh = q_ref[:, lanes]
            doh = do_ref[:, lanes]
            kw = k_ref[pl.ds(start, A_WIN), lanes]
            vw = v_ref[pl.ds(start, A_WIN), lanes]
            s = _dot(qh, kw, 1, 1) + b_ref[0, h]
            m = jnp.max(s, axis=-1, keepdims=True)
            e = jnp.exp(s - m)
            p = e * (1.0 / jnp.sum(e, axis=-1, keepdims=True))
            dp = _dot(doh, vw, 1, 1)
            delta = jnp.sum(p * dp, axis=-1, keepdims=True)
            ds = p * (dp - delta)
            db_ref[0, h] += ds
            dsb = ds.astype(BF16)
            dqs.append(_dot(dsb, kw, 1, 0) * QK_SCALE)
            dk_ref[pl.ds(start, A_WIN), lanes] += _dot(dsb, qh, 0, 0)
            dv_ref[pl.ds(start, A_WIN), lanes] += _dot(p.astype(BF16), doh, 0, 0)
        dq_ref[...] = jnp.concatenate(dqs, axis=1)

    once = pl.Buffered(1)
    qspec = pl.BlockSpec((A_Q, A_BWD_HEADS * HEAD_DIM), lambda hp, qb: (qb, hp))
    kvspec = pl.BlockSpec((S, A_BWD_HEADS * HEAD_DIM), lambda hp, qb: (0, hp), pipeline_mode=once)
    bspec = pl.BlockSpec((1, A_BWD_HEADS, A_Q, A_WIN), lambda hp, qb: (jnp.minimum(qb, N_VARIANTS - 1), hp, 0, 0))
    return pl.pallas_call(
        body,
        out_shape=(jax.ShapeDtypeStruct((S, WIDTH), F32), jax.ShapeDtypeStruct((S, WIDTH), F32),
                   jax.ShapeDtypeStruct((S, WIDTH), F32), jax.ShapeDtypeStruct((N_VARIANTS, N_HEADS, A_Q, A_WIN), F32)),
        grid=(N_HEADS // A_BWD_HEADS, nq),
        in_specs=[qspec, kvspec, kvspec, bspec, qspec],
        out_specs=(qspec, kvspec, kvspec, bspec),
        compiler_params=_cp(("parallel", "arbitrary")),
        name=name,
    )(q, k, v, biasm, do)


def _tri(kind):
    j = lax.broadcasted_iota(jnp.int32, (ATT_Q, ATT_Q), 0)
    s = lax.broadcasted_iota(jnp.int32, (ATT_Q, ATT_Q), 1)
    if kind == "gt":
        m = j > s
    elif kind == "le":
        m = j <= s
    else:
        m = j < s
    return jnp.where(m, 1.0, 0.0).astype(BF16)


def _cum(v, tri):
    hi, lo = _split_hi_lo(v)
    return _dot(hi, tri, 1, 0) + _dot(lo, tri, 1, 0)


def _log_sigmoids(z, mask):
    t = jnp.log(1.0 + jnp.exp(-jnp.abs(z)))
    keep = -(jnp.maximum(z, 0.0) + t)
    take = jnp.minimum(z, 0.0) - t
    return (keep if mask is None else jnp.where(mask, keep, 0.0)), take


def _strictly_before():
    row = lax.broadcasted_iota(jnp.int32, (ATT_Q, ATT_Q), 0)
    col = lax.broadcasted_iota(jnp.int32, (ATT_Q, ATT_Q), 1)
    return col < row


EXIT_LOG = -104.0


def attn_b_fwd(q, k, v, *, gather=None, name):
    S = q.shape[0]
    nq = S // ATT_Q
    heads = B_FWD_HEADS
    plan, shards, fulls = gather if gather is not None else (None, [], [])
    ng = len(shards)

    def body(q_ref, k_ref, v_ref, *rest):
        hp = pl.program_id(0)
        qb = pl.program_id(1)
        o_ref, t_ref, n_ref = rest[2 * ng:2 * ng + 3]
        if plan is not None:
            comm = (rest[:ng], rest[2 * ng + 3:3 * ng + 3], rest[3 * ng + 3:])

            @pl.when(jnp.logical_and(hp == 0, qb == 0))
            def _():
                plan.start(*comm)

        tri = _tri("gt")

        def block(kb, carry, mask):
            ks = pl.multiple_of(kb * ATT_Q, ATT_Q)
            new = []
            for h in range(heads):
                lanes = slice(h * HEAD_DIM, (h + 1) * HEAD_DIM)
                c, acc = carry[h]
                z = _dot(q_ref[:, lanes], k_ref[pl.ds(ks, ATT_Q), lanes], 1, 1)
                keep, take = _log_sigmoids(z, mask)
                w = jnp.exp(take + (_cum(keep, tri) + c))
                if mask is not None:
                    w = jnp.where(mask, w, 0.0)
                acc = acc + _dot(w.astype(BF16), v_ref[pl.ds(ks, ATT_Q), lanes], 1, 0)
                c = c + jnp.sum(keep, axis=-1, keepdims=True)
                new.append((c, acc))
            return tuple(new)

        def cond(state):
            it, cmax, _ = state
            return jnp.logical_and(it <= qb, cmax >= EXIT_LOG)

        def step(state):
            it, _, carry = state
            carry = block(qb - it, carry, None)
            worst = carry[0][0]
            for h in range(1, heads):
                worst = jnp.maximum(worst, carry[h][0])
            return it + 1, jnp.max(worst), carry

        init = tuple((jnp.zeros((ATT_Q, 1), F32), jnp.zeros((ATT_Q, HEAD_DIM), F32)) for _ in range(heads))
        diag = block(qb, init, _strictly_before())
        visited, _, res = lax.while_loop(cond, step, (jnp.int32(1), jnp.float32(0.0), diag))
        o_ref[...] = jnp.concatenate([res[h][1] for h in range(heads)], axis=1).astype(BF16)
        t_ref[...] = jnp.concatenate([jnp.broadcast_to(res[h][0], (ATT_Q, HEAD_DIM)) for h in range(heads)], axis=1)
        n_ref[hp, qb] = visited.astype(F32)
        if plan is not None:
            @pl.when(jnp.logical_and(hp == N_HEADS // heads - 1, qb == nq - 1))
            def _():
                plan.finish(*comm)

    qspec = pl.BlockSpec((ATT_Q, heads * HEAD_DIM), lambda hp, qb: (qb, hp))
    kvspec = pl.BlockSpec((S, heads * HEAD_DIM), lambda hp, qb: (0, hp), pipeline_mode=pl.Buffered(1))
    outs = pl.pallas_call(
        body,
        out_shape=(jax.ShapeDtypeStruct((S, WIDTH), BF16), jax.ShapeDtypeStruct((S, WIDTH), F32),
                   jax.ShapeDtypeStruct((N_HEADS // heads, nq), F32))
        + tuple(jax.ShapeDtypeStruct(f.shape, f.dtype) for f in fulls),
        grid=(N_HEADS // heads, nq),
        in_specs=[qspec, kvspec, kvspec] + [ANY] * (2 * ng),
        out_specs=(qspec, qspec, pl.BlockSpec(memory_space=pltpu.SMEM)) + tuple([ANY] * ng),
        scratch_shapes=plan.scratch_shapes() if plan is not None else (),
        input_output_aliases={3 + ng + i: 3 + i for i in range(ng)},
        compiler_params=_cp(("arbitrary", "arbitrary")),
        name=name,
    )(q, k, v, *shards, *fulls)
    return outs if plan is None else (outs[0], outs[1], outs[2], list(outs[3:]))


def attn_b_bwd(q, k, v, tot, nblk, do, *, scatter=None, name):
    S = q.shape[0]
    nq = S // ATT_Q
    heads = B_BWD_HEADS
    plan, pairs16 = scatter if scatter is not None else (None, [])
    ns = len(pairs16)

    def body(q_ref, k_ref, v_ref, t_ref, n_ref, do_ref, *rest):
        hp = pl.program_id(0)
        qb = pl.program_id(1)
        dq_ref, dk_ref, dv_ref = rest[ns:ns + 3]
        if plan is not None:
            comm = (rest[:ns], rest[ns + 3:2 * ns + 3], rest[2 * ns + 3:])

            @pl.when(jnp.logical_and(hp == 0, qb == 0))
            def _():
                plan.start(*comm)

        visited = n_ref[(hp * heads) // B_FWD_HEADS, qb].astype(jnp.int32)
        first = jnp.clip(qb + 1 - visited, 0, qb + 1)
        tri_le = _tri("le")
        tri_lt = _tri("lt")

        @pl.when(qb == 0)
        def _():
            dk_ref[...] = jnp.zeros_like(dk_ref)
            dv_ref[...] = jnp.zeros_like(dv_ref)

        def block(kb, carry, mask):
            ks = pl.multiple_of(kb * ATT_Q, ATT_Q)
            new = []
            for h in range(heads):
                lanes = slice(h * HEAD_DIM, (h + 1) * HEAD_DIM)
                cl, cg, dq = carry[h]
                qh = q_ref[:, lanes]
                doh = do_ref[:, lanes]
                kh = k_ref[pl.ds(ks, ATT_Q), lanes]
                vh = v_ref[pl.ds(ks, ATT_Q), lanes]
                totl = t_ref[:, h * HEAD_DIM:h * HEAD_DIM + 1]
                z = _dot(qh, kh, 1, 1)
                keep, take = _log_sigmoids(z, mask)
                sig = jnp.exp(take)
                w = sig * jnp.exp((totl - cl) - _cum(keep, tri_le))
                if mask is not None:
                    w = jnp.where(mask, w, 0.0)
                g = w * _dot(doh, vh, 1, 1)
                G = _dot(g.astype(BF16), tri_lt, 1, 0) + cg
                dz = g * (1.0 - sig) - sig * G
                if mask is not None:
                    dz = jnp.where(mask, dz, 0.0)
                dz = dz.astype(BF16)
                dq = dq + _dot(dz, kh, 1, 0)
                dk_ref[pl.ds(ks, ATT_Q), lanes] += _dot(dz, qh, 0, 0)
                dv_ref[pl.ds(ks, ATT_Q), lanes] += _dot(w.astype(BF16), doh, 0, 0)
                cl = cl + jnp.sum(keep, axis=-1, keepdims=True)
                cg = cg + jnp.sum(g, axis=-1, keepdims=True)
                new.append((cl, cg, dq))
            return tuple(new)

        init = tuple((jnp.zeros((ATT_Q, 1), F32), jnp.zeros((ATT_Q, 1), F32), jnp.zeros((ATT_Q, HEAD_DIM), F32))
                     for _ in range(heads))
        res = lax.fori_loop(jnp.minimum(first, qb), qb, lambda kb, carry: block(kb, carry, None), init)
        res = block(qb, res, _strictly_before())
        dq_ref[...] = (jnp.concatenate([res[h][2] for h in range(heads)], axis=1) * QK_SCALE).astype(BF16)
        if plan is not None:
            @pl.when(jnp.logical_and(hp == N_HEADS // heads - 1, qb == nq - 1))
            def _():
                plan.finish(*comm)

    qspec = pl.BlockSpec((ATT_Q, heads * HEAD_DIM), lambda hp, qb: (qb, hp))
    kvspec = pl.BlockSpec((S, heads * HEAD_DIM), lambda hp, qb: (0, hp), pipeline_mode=pl.Buffered(1))
    outs = pl.pallas_call(
        body,
        out_shape=(jax.ShapeDtypeStruct((S, WIDTH), BF16), jax.ShapeDtypeStruct((S, WIDTH), F32),
                   jax.ShapeDtypeStruct((S, WIDTH), F32)) + (tuple(plan.out_shapes()) if plan is not None else ()),
        grid=(N_HEADS // heads, nq),
        in_specs=[qspec, kvspec, kvspec, qspec, pl.BlockSpec(memory_space=pltpu.SMEM), qspec] + [ANY] * ns,
        out_specs=(qspec, kvspec, kvspec) + tuple([ANY] * ns),
        scratch_shapes=plan.scratch_shapes() if plan is not None else (),
        compiler_params=_cp(("arbitrary", "arbitrary")),
        name=name,
    )(q, k, v, tot, nblk, do, *pairs16)
    return outs if plan is None else (outs[0], outs[1], outs[2], list(outs[3:]))


U_COLBLK = 6


def _pool_counts(t0, rows):
    t = t0 + lax.broadcasted_iota(jnp.int32, (rows, WIDTH), 0)
    lane_grp = lax.broadcasted_iota(jnp.int32, (rows, WIDTH), 1) // GROUP_DIM
    w2, w4, w8, w16 = POOL_WINDOWS
    win = jnp.where(lane_grp == 0, w2, jnp.where(lane_grp == 1, w4, jnp.where(lane_grp == 2, w8, w16)))
    cnt = jnp.minimum(t + 1, win)
    return 1.0 / cnt.astype(F32), lane_grp


def _window_sums(ext, shift_fn):
    s2 = ext + shift_fn(ext, 1)
    s4 = s2 + shift_fn(s2, 2)
    s8 = s4 + shift_fn(s4, 4)
    s16 = s8 + shift_fn(s8, 8)
    return s2, s4, s8, s16


def _select_group(lane_grp, s2, s4, s8, s16):
    return jnp.where(lane_grp == 0, s2, jnp.where(lane_grp == 1, s4, jnp.where(lane_grp == 2, s8, s16)))


def _pooled_tile(u_ref, h_ref, i, T):
    halo = jnp.where(i > 0, h_ref[...], 0.0)
    ext = jnp.concatenate([halo, u_ref[...]], axis=0)
    n = T + HALO
    sums = _window_sums(ext, lambda v, k: pltpu.roll(v, k, 0))
    inv, lane_grp = _pool_counts(i * T - HALO, n)
    pooled = _select_group(lane_grp, *sums) * inv - ext
    return pooled[HALO:, :]


def pool_fwd(proj, w_pool, scale, *, name):
    S = proj.shape[0]
    T = _tile(S)
    hb = T // HALO

    def body(u_ref, h_ref, w_ref, s_ref, o_ref):
        i = pl.program_id(0)
        pooled = _pooled_tile(u_ref, h_ref, i, T).astype(BF16)
        outs = [_dot(pooled[:, g * GROUP_DIM:(g + 1) * GROUP_DIM], w_ref[g], 1, 0) for g in range(4)]
        o_ref[...] = (jnp.concatenate(outs, axis=1) * s_ref[...]).astype(BF16)

    return pl.pallas_call(
        body,
        out_shape=jax.ShapeDtypeStruct((S, WIDTH), BF16),
        grid=(S // T,),
        in_specs=[pl.BlockSpec((T, WIDTH), lambda i: (i, U_COLBLK)),
                  pl.BlockSpec((HALO, WIDTH), lambda i: (jnp.maximum(i * hb - 1, 0), U_COLBLK)),
                  pl.BlockSpec((4, GROUP_DIM, GROUP_DIM), lambda i: (0, 0, 0)),
                  pl.BlockSpec((1, WIDTH), lambda i: (0, 0))],
        out_specs=pl.BlockSpec((T, WIDTH), lambda i: (i, 0)),
        compiler_params=_cp(("parallel",)),
        name=name,
    )(proj, proj, w_pool, scale)


def pool_bwd(proj, w_pool, scale, do, *, name):
    S = proj.shape[0]
    T = _tile(S)
    hb = T // HALO
    nt = S // T

    def body(u_ref, h_ref, w_ref, s_ref, do_ref, dof_ref, du_ref, dw_ref, ds_ref):
        i = pl.program_id(0)

        @pl.when(i == 0)
        def _():
            dw_ref[...] = jnp.zeros_like(dw_ref)
            ds_ref[...] = jnp.zeros_like(ds_ref)

        pooled = _pooled_tile(u_ref, h_ref, i, T).astype(BF16)
        dov = do_ref[...].astype(F32)
        fut = jnp.where(i < nt - 1, dof_ref[...].astype(F32), 0.0)
        dmix = (jnp.concatenate([dov, fut], axis=0) * s_ref[...]).astype(BF16)
        mixed, dpool = [], []
        for g in range(4):
            lanes = slice(g * GROUP_DIM, (g + 1) * GROUP_DIM)
            mixed.append(_dot(pooled[:, lanes], w_ref[g], 1, 0))
            dw_ref[g] += _dot(pooled[:, lanes], dmix[:T, lanes], 0, 0)
            dpool.append(_dot(dmix[:, lanes], w_ref[g], 1, 1))
        ds_ref[...] += jnp.sum(dov * jnp.concatenate(mixed, axis=1), axis=0, keepdims=True)
        dp = jnp.concatenate(dpool, axis=1)
        n = T + HALO
        inv, lane_grp = _pool_counts(i * T, n)
        sums = _window_sums(dp * inv, lambda v, k: pltpu.roll(v, n - k, 0))
        du = _select_group(lane_grp, *sums) - dp
        du_ref[...] = du[:T, :].astype(BF16)

    row = pl.BlockSpec((T, WIDTH), lambda i: (i, 0))
    return pl.pallas_call(
        body,
        out_shape=(jax.ShapeDtypeStruct((S, WIDTH), BF16), jax.ShapeDtypeStruct((4, GROUP_DIM, GROUP_DIM), F32),
                   jax.ShapeDtypeStruct((1, WIDTH), F32)),
        grid=(nt,),
        in_specs=[pl.BlockSpec((T, WIDTH), lambda i: (i, U_COLBLK)),
                  pl.BlockSpec((HALO, WIDTH), lambda i: (jnp.maximum(i * hb - 1, 0), U_COLBLK)),
                  pl.BlockSpec((4, GROUP_DIM, GROUP_DIM), lambda i: (0, 0, 0)),
                  pl.BlockSpec((1, WIDTH), lambda i: (0, 0)),
                  row,
                  pl.BlockSpec((HALO, WIDTH), lambda i: (jnp.minimum((i + 1) * hb, S // HALO - 1), 0))],
        out_specs=(row, pl.BlockSpec((4, GROUP_DIM, GROUP_DIM), lambda i: (0, 0, 0)),
                   pl.BlockSpec((1, WIDTH), lambda i: (0, 0))),
        compiler_params=_cp(("arbitrary",)),
        name=name,
    )(proj, proj, w_pool, scale, do, do)


GATE_BLK0 = GATE_COL0 // WIDTH


def merge_fwd(oa, ob, oc, proj, b_gate, wa, wb, wc, *, name):
    S = oa.shape[0]
    T = _tile(S)

    def body(oa_ref, ob_ref, oc_ref, ga, gb, gc, ba, bb, bc, wa_ref, wb_ref, wc_ref, m_ref):
        acc = None
        for o_ref, g_ref, b_ref, w_ref in ((oa_ref, ga, ba, wa_ref), (ob_ref, gb, bb, wb_ref), (oc_ref, gc, bc, wc_ref)):
            y = _dot(o_ref[...], w_ref[...], 1, 0)
            t = jax.nn.sigmoid(g_ref[...] + b_ref[...]) * y
            acc = t if acc is None else acc + t
        m_ref[...] = acc.astype(BF16)

    row = pl.BlockSpec((T, WIDTH), lambda i, n: (i, 0))
    gate = lambda b: pl.BlockSpec((T, WIDTH), lambda i, n, b=b: (i, GATE_BLK0 + 2 * b + n))
    bias = lambda b: pl.BlockSpec((1, WIDTH), lambda i, n, b=b: (0, 2 * b + n))
    wspec = pl.BlockSpec((WIDTH, WIDTH), lambda i, n: (0, n))
    return pl.pallas_call(
        body,
        out_shape=jax.ShapeDtypeStruct((S, D_MODEL), BF16),
        grid=(S // T, 2),
        in_specs=[row, row, row, gate(0), gate(1), gate(2), bias(0), bias(1), bias(2), wspec, wspec, wspec],
        out_specs=pl.BlockSpec((T, WIDTH), lambda i, n: (i, n)),
        compiler_params=_cp(("parallel", "parallel")),
        name=name,
    )(oa, ob, oc, proj, proj, proj, b_gate, b_gate, b_gate, wa, wb, wc)


def merge_bwd(dm, oa, ob, oc, proj, b_gate, wa, wb, wc, *, name):
    S = oa.shape[0]
    T = _tile(S)

    def body(dm_ref, oa_ref, ob_ref, oc_ref, ga, gb, gc, ba, bb, bc, wa_ref, wb_ref, wc_ref,
             ta, tb, tc, dga, dgb, dgc, dba, dbb, dbc):
        i = pl.program_id(1)
        dmv = dm_ref[...].astype(F32)
        for o_ref, g_ref, b_ref, w_ref, t_ref, dg_ref, db_ref in (
                (oa_ref, ga, ba, wa_ref, ta, dga, dba), (ob_ref, gb, bb, wb_ref, tb, dgb, dbb),
                (oc_ref, gc, bc, wc_ref, tc, dgc, dbc)):
            y = _dot(o_ref[...], w_ref[...], 1, 0)
            gate = jax.nn.sigmoid(g_ref[...] + b_ref[...])
            t_ref[...] = (gate * dmv).astype(BF16)
            dgl = dmv * y * gate * (1.0 - gate)
            dg_ref[...] = dgl.astype(BF16)

            @pl.when(i == 0)
            def _():
                db_ref[...] = jnp.zeros_like(db_ref)

            db_ref[...] += jnp.sum(dgl, axis=0, keepdims=True)

    row = pl.BlockSpec((T, WIDTH), lambda n, i: (i, 0))
    half = pl.BlockSpec((T, WIDTH), lambda n, i: (i, n))
    gate = lambda b: pl.BlockSpec((T, WIDTH), lambda n, i, b=b: (i, GATE_BLK0 + 2 * b + n))
    bias = lambda b: pl.BlockSpec((1, WIDTH), lambda n, i, b=b: (0, 2 * b + n))
    wspec = pl.BlockSpec((WIDTH, WIDTH), lambda n, i: (0, n))
    bvec = pl.BlockSpec((1, WIDTH), lambda n, i: (0, n))
    act = jax.ShapeDtypeStruct((S, D_MODEL), BF16)
    vec = jax.ShapeDtypeStruct((1, D_MODEL), F32)
    return pl.pallas_call(
        body,
        out_shape=(act, act, act, act, act, act, vec, vec, vec),
        grid=(2, S // T),
        in_specs=[half, row, row, row, gate(0), gate(1), gate(2), bias(0), bias(1), bias(2), wspec, wspec, wspec],
        out_specs=(half, half, half, half, half, half, bvec, bvec, bvec),
        compiler_params=_cp(("parallel", "arbitrary")),
        name=name,
    )(dm, oa, ob, oc, proj, proj, proj, b_gate, b_gate, b_gate, wa, wb, wc)


FF_T = 256
FF_BLKS = D_FF // FF_T
CONV_ROWS = (2048, 1024, 512, 256, 128)
CONV_FWD_ROWS = (4096,) + CONV_ROWS


def _silu_parts(x):
    s = jax.nn.sigmoid(x)
    return x * s, s


def _conv3(ext, w_ref, b_ref):
    taps = (pltpu.roll(ext, 2, 0), pltpu.roll(ext, 1, 0), ext)
    return b_ref[...] + w_ref[0:1, :] * taps[0] + w_ref[1:2, :] * taps[1] + w_ref[2:3, :] * taps[2], taps


def conv_glu_fwd(u, conv_w, conv_b, *, name):
    S = u.shape[0]
    T = _tile(S, CONV_FWD_ROWS)
    hb = T // CONV_HALO

    def body(ug, ugh, uv, uvh, wg, wv, bg, bv, a_ref):
        i = pl.program_id(1)
        cs = []
        for m_ref, h_ref, w_ref, b_ref in ((ug, ugh, wg, bg), (uv, uvh, wv, bv)):
            halo = jnp.where(i > 0, h_ref[...], 0.0)
            ext = jnp.concatenate([halo, m_ref[...]], axis=0)
            cs.append(_conv3(ext, w_ref, b_ref)[0][CONV_HALO:, :])
        act, _ = _silu_parts(cs[0])
        a_ref[...] = (act * cs[1]).astype(BF16)

    main = lambda o: pl.BlockSpec((T, FF_T), lambda c, i, o=o: (i, c + o))
    halo = lambda o: pl.BlockSpec((CONV_HALO, FF_T), lambda c, i, o=o: (jnp.maximum(i * hb - 1, 0), c + o))
    wsp = lambda o: pl.BlockSpec((3, FF_T), lambda c, i, o=o: (0, c + o))
    bsp = lambda o: pl.BlockSpec((1, FF_T), lambda c, i, o=o: (0, c + o))
    return pl.pallas_call(
        body,
        out_shape=jax.ShapeDtypeStruct((S, D_FF), BF16),
        grid=(FF_BLKS, S // T),
        in_specs=[main(0), halo(0), main(FF_BLKS), halo(FF_BLKS), wsp(0), wsp(FF_BLKS), bsp(0), bsp(FF_BLKS)],
        out_specs=pl.BlockSpec((T, FF_T), lambda c, i: (i, c)),
        compiler_params=_cp(("parallel", "parallel")),
        name=name,
    )(u, u, u, u, conv_w, conv_w, conv_b, conv_b)


def conv_glu_bwd(u, conv_w, conv_b, da, *, name):
    S = u.shape[0]
    T = _tile(S, CONV_ROWS)
    hb = T // CONV_HALO
    nt = S // T
    n = T + 2 * CONV_HALO

    def body(ug, ugp, ugf, uv, uvp, uvf, wg, wv, bg, bv, da_ref, daf_ref,
             dug, duv, dwg, dwv, dbg, dbv):
        i = pl.program_id(1)
        first, last = i == 0, i == nt - 1
        taps, cs = [], []
        for m_ref, p_ref, f_ref, w_ref, b_ref in ((ug, ugp, ugf, wg, bg), (uv, uvp, uvf, wv, bv)):
            ext = jnp.concatenate([jnp.where(first, 0.0, p_ref[...]), m_ref[...], jnp.where(last, 0.0, f_ref[...])], axis=0)
            c, tp = _conv3(ext, w_ref, b_ref)
            cs.append(c)
            taps.append(tp)
        dae = jnp.concatenate([jnp.zeros((CONV_HALO, FF_T), F32), da_ref[...].astype(F32),
                               jnp.where(last, 0.0, daf_ref[...].astype(F32))], axis=0)
        act, sg = _silu_parts(cs[0])
        dcs = (dae * cs[1] * (sg * (1.0 + cs[0] * (1.0 - sg))), dae * act)
        main = slice(CONV_HALO, CONV_HALO + T)
        for tp, dc, w_ref, du_ref, dw_ref, db_ref in ((taps[0], dcs[0], wg, dug, dwg, dbg),
                                                      (taps[1], dcs[1], wv, duv, dwv, dbv)):
            du = (w_ref[2:3, :] * dc + w_ref[1:2, :] * pltpu.roll(dc, n - 1, 0) + w_ref[0:1, :] * pltpu.roll(dc, n - 2, 0))
            du_ref[...] = du[main, :].astype(BF16)
            dcm = dc[main, :]
            rows = [jnp.sum(dcm * tp[j][main, :], axis=0, keepdims=True) for j in range(3)]

            @pl.when(first)
            def _():
                dw_ref[...] = jnp.zeros_like(dw_ref)
                db_ref[...] = jnp.zeros_like(db_ref)

            dw_ref[...] += jnp.concatenate(rows, axis=0)
            db_ref[...] += jnp.sum(dcm, axis=0, keepdims=True)

    main = lambda o: pl.BlockSpec((T, FF_T), lambda c, i, o=o: (i, c + o))
    past = lambda o: pl.BlockSpec((CONV_HALO, FF_T), lambda c, i, o=o: (jnp.maximum(i * hb - 1, 0), c + o))
    fut = lambda o: pl.BlockSpec((CONV_HALO, FF_T), lambda c, i, o=o: (jnp.minimum((i + 1) * hb, S // CONV_HALO - 1), c + o))
    wsp = lambda o: pl.BlockSpec((3, FF_T), lambda c, i, o=o: (0, c + o))
    bsp = lambda o: pl.BlockSpec((1, FF_T), lambda c, i, o=o: (0, c + o))
    return pl.pallas_call(
        body,
        out_shape=(jax.ShapeDtypeStruct((S, D_FF), BF16), jax.ShapeDtypeStruct((S, D_FF), BF16),
                   jax.ShapeDtypeStruct((3, D_FF), F32), jax.ShapeDtypeStruct((3, D_FF), F32),
                   jax.ShapeDtypeStruct((1, D_FF), F32), jax.ShapeDtypeStruct((1, D_FF), F32)),
        grid=(FF_BLKS, nt),
        in_specs=[main(0), past(0), fut(0), main(FF_BLKS), past(FF_BLKS), fut(FF_BLKS),
                  wsp(0), wsp(FF_BLKS), bsp(0), bsp(FF_BLKS), main(0), fut(0)],
        out_specs=(main(0), main(0), wsp(0), wsp(0), bsp(0), bsp(0)),
        compiler_params=_cp(("parallel", "arbitrary")),
        name=name,
    )(u, u, u, u, u, u, conv_w, conv_w, conv_b, conv_b, da, da)


def loss_head(y, target, *, name):
    S, D = y.shape
    T = _tile(S)

    def body(y_ref, t_ref, dy_ref, l_ref):
        i = pl.program_id(0)
        err = y_ref[...] - t_ref[...]
        dy_ref[...] = err * (1.0 / D)

        @pl.when(i == 0)
        def _():
            l_ref[...] = jnp.zeros_like(l_ref)

        l_ref[...] += 0.5 * jnp.sum(jnp.mean(err * err, axis=-1, keepdims=True))

    row = pl.BlockSpec((T, D), lambda i: (i, 0))
    return pl.pallas_call(
        body,
        out_shape=(jax.ShapeDtypeStruct((S, D), F32), jax.ShapeDtypeStruct((8, 128), F32)),
        grid=(S // T,),
        in_specs=[row, row],
        out_specs=(row, pl.BlockSpec((8, 128), lambda i: (0, 0))),
        compiler_params=_cp(("arbitrary",)),
        name=name,
    )(y, target)


ELEMS_PER_BLOCK = 512 * 1024


def _rows_tile(rows, cols):
    if rows * cols <= ELEMS_PER_BLOCK or rows % 8:
        return rows
    best = 8
    for tr in range(8, rows + 1, 8):
        if rows % tr == 0 and tr * cols <= ELEMS_PER_BLOCK:
            best = tr
    return best


def _adamw_math(g, w_ref, m_ref, v_ref, g_out, d_out, m_out, v_out):
    mn = ADAM_B1 * m_ref[...] + (1.0 - ADAM_B1) * g
    vn = ADAM_B2 * v_ref[...] + (1.0 - ADAM_B2) * (g * g)
    m_hat = mn / (1.0 - ADAM_B1 ** ADAM_STEP)
    v_hat = vn / (1.0 - ADAM_B2 ** ADAM_STEP)
    g_out[...] = g
    d_out[...] = -ADAM_LR * (m_hat / (jnp.sqrt(v_hat) + ADAM_EPS) + ADAM_WD * w_ref[...])
    m_out[...] = mn
    v_out[...] = vn


def adamw(w, m, v, g, *, name):
    rows, cols = w.shape
    tr = _rows_tile(rows, cols)

    def body(w_ref, m_ref, v_ref, g_ref, g_out, d_out, m_out, v_out):
        _adamw_math(g_ref[...], w_ref, m_ref, v_ref, g_out, d_out, m_out, v_out)

    spec = pl.BlockSpec((tr, cols), lambda i: (i, 0))
    shp = jax.ShapeDtypeStruct((rows, cols), F32)
    return pl.pallas_call(
        body,
        out_shape=(shp, shp, shp, shp),
        grid=(rows // tr,),
        in_specs=[spec] * 4,
        out_specs=(spec, spec, spec, spec),
        compiler_params=_cp(("parallel",)),
        name=name,
    )(w, m, v, g)


ANY = pl.BlockSpec(memory_space=pl.ANY)
STAGE_BYTES = 2 * 1024 * 1024


def _mesh_pos():
    return lax.axis_index("x"), lax.axis_index("y"), lax.axis_index("c")


def _chip_peers(x, y):
    return [(1 - x, y), (x, 1 - y), (1 - x, 1 - y)]


def _all_peers(x, y, c):
    return [((1 - x) if (r >> 2) & 1 else x, (1 - y) if (r >> 1) & 1 else y, (1 - c) if r & 1 else c)
            for r in range(1, 8)]


class LayerGather:
    def __init__(self, shards, axes, layer):
        self.nt = len(shards)
        self.axes = list(axes)
        self.layer = layer
        self.shapes = [s.shape for s in shards]
        self.dtypes = [s.dtype for s in shards]
        self.sizes = [s.shape[a] for s, a in zip(shards, axes)]
        self.split = [s.shape[1] % 32 == 0 for s in shards]
        self.half_rows = [s.shape[1] // 2 if sp else s.shape[1] for s, sp in zip(shards, self.split)]
        self.chunk_rows = []
        for s in shards:
            rt = s.shape[1]
            while rt % 32 == 0 and rt * s.shape[2] * s.dtype.itemsize > STAGE_BYTES:
                rt //= 2
            self.chunk_rows.append(rt)

    def out_shapes(self):
        out = []
        for shp, a, sz, dt in zip(self.shapes, self.axes, self.sizes, self.dtypes):
            shp = list(shp)
            shp[a] = 4 * sz
            out.append(jax.ShapeDtypeStruct(tuple(shp), dt))
        return out

    def scratch_shapes(self):
        return ([pltpu.VMEM((1, rt, shp[2]), dt) for shp, rt, dt in zip(self.shapes, self.chunk_rows, self.dtypes)]
                + [pltpu.SemaphoreType.DMA((2 * self.nt,))] + [pltpu.SemaphoreType.DMA((3 * self.nt,)) for _ in range(4)])

    def _views(self, ins, outs, scratch):
        nt = self.nt
        stage, stage_sems = scratch[:nt], scratch[nt]
        ici_send, ici_recv, d2d_send, d2d_recv = scratch[nt + 1:]
        x, y, c = _mesh_pos()
        mine = 2 * x + y
        peers = _chip_peers(x, y)
        layer = pl.ds(self.layer, 1)

        def rows(t, half, r0=0, n=None):
            hr = self.half_rows[t]
            if n is None:
                return pl.ds(pl.multiple_of(half * hr, 16), hr) if self.split[t] else pl.ds(0, hr)
            return pl.ds(r0, n)

        def placed(t, blk, row_sel, row_len):
            sz = self.sizes[t]
            if self.axes[t] == 2:
                return outs[t].at[layer, row_sel, pl.ds(pl.multiple_of(blk * sz, 128), sz)]
            return outs[t].at[layer, pl.ds(pl.multiple_of(blk * sz, 16) + row_sel.start, row_len), :]

        def ici(t, k, blk):
            px, py = peers[k]
            sel = rows(t, c)
            return pltpu.make_async_remote_copy(
                src_ref=ins[t].at[layer, sel, :], dst_ref=placed(t, blk, sel, self.half_rows[t]),
                send_sem=ici_send.at[3 * t + k], recv_sem=ici_recv.at[3 * t + k],
                device_id=(px, py, c), device_id_type=MESH_T)

        def d2d(t, k, half):
            px, py = peers[k]
            piece = placed(t, 2 * px + py, rows(t, half), self.half_rows[t])
            return pltpu.make_async_remote_copy(
                src_ref=piece, dst_ref=piece, send_sem=d2d_send.at[3 * t + k], recv_sem=d2d_recv.at[3 * t + k],
                device_id=(x, y, 1 - c), device_id_type=MESH_T)

        def own_chunk(t, r0):
            rt = self.chunk_rows[t]
            sel = pl.ds(r0, rt)
            return ins[t].at[layer, sel, :], placed(t, mine, sel, rt), stage[t], stage_sems

        return c, mine, peers, ici, d2d, own_chunk

    def start(self, ins, outs, scratch):
        c, mine, peers, ici, d2d, own_chunk = self._views(ins, outs, scratch)
        for t in range(self.nt):
            for k in range(3):
                ici(t, k, mine).start()
        starts = [list(range(0, self.shapes[t][1], self.chunk_rows[t])) for t in range(self.nt)]
        for r in range(max(len(s) for s in starts)):
            active = [(t, *own_chunk(t, starts[t][r])) for t in range(self.nt) if r < len(starts[t])]
            loads = [pltpu.make_async_copy(src, buf, sems.at[2 * t]) for t, src, dst, buf, sems in active]
            for cp in loads:
                cp.start()
            for cp in loads:
                cp.wait()
            stores = [pltpu.make_async_copy(buf, dst, sems.at[2 * t + 1]) for t, src, dst, buf, sems in active]
            for cp in stores:
                cp.start()
            for cp in stores:
                cp.wait()

    def finish(self, ins, outs, scratch):
        c, mine, peers, ici, d2d, own_chunk = self._views(ins, outs, scratch)
        for t in range(self.nt):
            for k, (px, py) in enumerate(peers):
                ici(t, k, 2 * px + py).wait_recv()
                if self.split[t]:
                    d2d(t, k, c).start()
        for t in range(self.nt):
            for k in range(3):
                if self.split[t]:
                    d2d(t, k, 1 - c).wait_recv()
        for t in range(self.nt):
            for k in range(3):
                ici(t, k, mine).wait_send()
                if self.split[t]:
                    d2d(t, k, c).wait_send()


def all_gather_layer(shards, axes, layer, *, name):
    plan = LayerGather(shards, axes, layer)
    nt = plan.nt

    def body(*refs):
        ins, outs, scratch = refs[:nt], refs[nt:2 * nt], refs[2 * nt:]
        plan.start(ins, outs, scratch)
        plan.finish(ins, outs, scratch)

    return pl.pallas_call(
        body,
        out_shape=tuple(plan.out_shapes()),
        in_specs=[ANY] * nt,
        out_specs=tuple([ANY] * nt),
        scratch_shapes=plan.scratch_shapes(),
        name=name,
    )(*shards)


class HalfLayout:
    def __init__(self, shape, axis):
        self.R, self.C = shape
        self.axis = axis
        if axis == 1:
            self.hr, self.pw = self.R // 2, self.C // 4
            self.half_shape = (self.hr, self.C)
        else:
            self.hr, self.pw = self.R // 8, self.C
            self.half_shape = (4 * self.hr, self.C)
        self.tr = _rows_tile(self.hr, self.pw)
        self.nr = self.hr // self.tr

    def in_grad(self, ref, blk, half):
        if self.axis == 1:
            return ref.at[pl.ds(pl.multiple_of(half * self.hr, 16), self.hr), pl.ds(pl.multiple_of(blk * self.pw, 128), self.pw)]
        return ref.at[pl.ds(pl.multiple_of((2 * blk + half) * self.hr, 16), self.hr), :]

    def in_half(self, ref, blk):
        if self.axis == 1:
            return ref.at[:, pl.ds(pl.multiple_of(blk * self.pw, 128), self.pw)]
        return ref.at[pl.ds(pl.multiple_of(blk * self.hr, 16), self.hr), :]

    def grad_spec(self):
        if self.axis == 1:
            return pl.BlockSpec((self.tr, self.pw), lambda j, i, s: (s[0] * self.nr + i, j))
        return pl.BlockSpec((self.tr, self.pw), lambda j, i, s: ((2 * j + s[0]) * self.nr + i, 0))

    def half_spec(self):
        if self.axis == 1:
            return pl.BlockSpec((self.tr, self.pw), lambda j, i, s: (i, j))
        return pl.BlockSpec((self.tr, self.pw), lambda j, i, s: (j * self.nr + i, 0))


def half_exchange(grads, layouts, *, name):
    nt = len(grads)
    pieces = [(t, j) for t in range(nt) for j in (range(4) if layouts[t].axis == 0 else range(1))]

    def body(*refs):
        ins, outs = refs[:nt], refs[nt:2 * nt]
        send_sems, recv_sems = refs[2 * nt:]
        x, y, c = _mesh_pos()
        cps = []
        for n, (t, j) in enumerate(pieces):
            lay = layouts[t]
            if lay.axis == 1:
                src = ins[t].at[pl.ds(pl.multiple_of((1 - c) * lay.hr, 16), lay.hr), :]
                dst = outs[t]
            else:
                src = lay.in_grad(ins[t], j, 1 - c)
                dst = lay.in_half(outs[t], j)
            cp = pltpu.make_async_remote_copy(src_ref=src, dst_ref=dst, send_sem=send_sems.at[n], recv_sem=recv_sems.at[n],
                                              device_id=(x, y, 1 - c), device_id_type=MESH_T)
            cp.start()
            cps.append(cp)
        for cp in cps:
            cp.wait_recv()
        for cp in cps:
            cp.wait_send()

    return pl.pallas_call(
        body,
        out_shape=tuple(jax.ShapeDtypeStruct(lay.half_shape, F32) for lay in layouts),
        in_specs=[ANY] * nt,
        out_specs=tuple([ANY] * nt),
        scratch_shapes=[pltpu.SemaphoreType.DMA((len(pieces),)), pltpu.SemaphoreType.DMA((len(pieces),))],
        name=name,
    )(*grads)


def pair_sum(grad, other, lay, core, *, name):
    def body(c_ref, g_ref, o_ref, s32_ref, s16_ref):
        s = g_ref[...] + o_ref[...]
        s32_ref[...] = s
        s16_ref[...] = s.astype(BF16)

    return pl.pallas_call(
        body,
        out_shape=(jax.ShapeDtypeStruct(lay.half_shape, F32), jax.ShapeDtypeStruct(lay.half_shape, BF16)),
        grid_spec=pltpu.PrefetchScalarGridSpec(
            num_scalar_prefetch=1, grid=(4, lay.nr),
            in_specs=[lay.grad_spec(), lay.half_spec()],
            out_specs=(lay.half_spec(), lay.half_spec())),
        compiler_params=_cp(("parallel", "parallel")),
        name=name,
    )(core, grad, other)


class BlockScatter:
    def __init__(self, layouts):
        self.layouts = layouts
        self.nt = len(layouts)

    def out_shapes(self):
        return [jax.ShapeDtypeStruct((3, lay.hr, lay.pw), BF16) for lay in self.layouts]

    def scratch_shapes(self):
        return [pltpu.SemaphoreType.DMA((3 * self.nt,)), pltpu.SemaphoreType.DMA((3 * self.nt,))]

    def _copies(self, pairs16, recv, scratch):
        send_sems, recv_sems = scratch
        x, y, c = _mesh_pos()
        return [pltpu.make_async_remote_copy(
            src_ref=lay.in_half(pairs16[t], 2 * px + py), dst_ref=recv[t].at[k],
            send_sem=send_sems.at[3 * t + k], recv_sem=recv_sems.at[3 * t + k],
            device_id=(px, py, c), device_id_type=MESH_T)
            for t, lay in enumerate(self.layouts) for k, (px, py) in enumerate(_chip_peers(x, y))]

    def start(self, pairs16, recv, scratch):
        for cp in self._copies(pairs16, recv, scratch):
            cp.start()

    def finish(self, pairs16, recv, scratch):
        copies = self._copies(pairs16, recv, scratch)
        for cp in copies:
            cp.wait_recv()
        for cp in copies:
            cp.wait_send()


def gather_small(small, *, name):
    def body(small_in, small_out, ssend, srecv):
        x, y, c = _mesh_pos()
        me = 4 * x + 2 * y + c
        sends, recvs = [], []
        for r, (px, py, pc) in enumerate(_all_peers(x, y, c)):
            def mk(slot, r=r, px=px, py=py, pc=pc):
                return pltpu.make_async_remote_copy(
                    src_ref=small_in, dst_ref=small_out.at[slot], send_sem=ssend.at[r], recv_sem=srecv.at[r],
                    device_id=(px, py, pc), device_id_type=MESH_T)
            snd = mk(me)
            snd.start()
            sends.append(snd)
            recvs.append(mk(4 * px + 2 * py + pc))
        for r in recvs:
            r.wait_recv()
        for s in sends:
            s.wait_send()

    return pl.pallas_call(
        body,
        out_shape=jax.ShapeDtypeStruct((8,) + small.shape, F32),
        in_specs=[ANY],
        out_specs=ANY,
        scratch_shapes=[pltpu.SemaphoreType.DMA((7,)), pltpu.SemaphoreType.DMA((7,))],
        name=name,
    )(small)


def sum_chips(pair32, recv, lay, chip, *, name):
    def body(j_ref, p_ref, r_ref, s_ref):
        acc = p_ref[...]
        for k in range(3):
            acc = acc + r_ref[k].astype(F32)
        s_ref[...] = acc

    if lay.axis == 1:
        own = pl.BlockSpec((lay.tr, lay.pw), lambda i, j: (i, j[0]))
    else:
        own = pl.BlockSpec((lay.tr, lay.pw), lambda i, j: (j[0] * lay.nr + i, 0))
    return pl.pallas_call(
        body,
        out_shape=jax.ShapeDtypeStruct((lay.hr, lay.pw), F32),
        grid_spec=pltpu.PrefetchScalarGridSpec(
            num_scalar_prefetch=1, grid=(lay.nr,),
            in_specs=[own, pl.BlockSpec((3, lay.tr, lay.pw), lambda i, j: (0, i, 0))],
            out_specs=pl.BlockSpec((lay.tr, lay.pw), lambda i, j: (i, 0))),
        compiler_params=_cp(("parallel",)),
        name=name,
    )(chip, pair32, recv)


def sum_devices(gathered, own, me, *, name):
    _, R, C = gathered.shape

    def body(me_ref, g_ref, o_ref, s_ref):
        acc = None
        for k in range(8):
            part = jnp.where(me_ref[0] == k, o_ref[...], g_ref[k])
            acc = part if acc is None else acc + part
        s_ref[...] = acc

    return pl.pallas_call(
        body,
        out_shape=jax.ShapeDtypeStruct((R, C), F32),
        grid_spec=pltpu.PrefetchScalarGridSpec(
            num_scalar_prefetch=1, grid=(1,),
            in_specs=[pl.BlockSpec((8, R, C), lambda i, m: (0, 0, 0)), pl.BlockSpec((R, C), lambda i, m: (0, 0))],
            out_specs=pl.BlockSpec((R, C), lambda i, m: (0, 0))),
        compiler_params=_cp(("arbitrary",)),
        name=name,
    )(me, gathered, own)


def sibling_swap(parts, *, name):
    nt = len(parts)

    def body(*refs):
        ins, outs = refs[:nt], refs[nt:2 * nt]
        send_sems, recv_sems = refs[2 * nt:]
        x, y, c = _mesh_pos()
        cps = []
        for t in range(nt):
            cp = pltpu.make_async_remote_copy(src_ref=ins[t], dst_ref=outs[t], send_sem=send_sems.at[t],
                                              recv_sem=recv_sems.at[t], device_id=(x, y, 1 - c), device_id_type=MESH_T)
            cp.start()
            cps.append(cp)
        for cp in cps:
            cp.wait_recv()
        for cp in cps:
            cp.wait_send()

    return pl.pallas_call(
        body,
        out_shape=tuple(jax.ShapeDtypeStruct(p.shape, p.dtype) for p in parts),
        in_specs=[ANY] * nt,
        out_specs=tuple([ANY] * nt),
        scratch_shapes=[pltpu.SemaphoreType.DMA((nt,)), pltpu.SemaphoreType.DMA((nt,))],
        name=name,
    )(*parts)


def adamw_halves(w, m, v, mine, other, lay, core, *, name):
    _, r, c = w.shape
    tr, nr = lay.tr, lay.nr
    assert (r, c) == (2 * lay.hr, lay.pw), (w.shape, lay.hr, lay.pw)

    def body(c_ref, w_ref, m_ref, v_ref, *rest):
        g_refs, outs = rest[:2 * DEPTH], rest[2 * DEPTH:]
        l, h = pl.program_id(0), pl.program_id(1)
        g = None
        for d in range(DEPTH):
            gd = jnp.where(h == c_ref[0], g_refs[d][...], g_refs[DEPTH + d][...])
            g = gd if g is None else jnp.where(l == d, gd, g)
        _adamw_math(g, w_ref, m_ref, v_ref, *outs)

    full = pl.BlockSpec((None, tr, c), lambda l, h, i, s: (l, h * nr + i, 0))

    def part(d, is_mine):
        def index(l, h, i, s):
            used = jnp.logical_and(l == d, (h == s[0]) == is_mine)
            return jnp.where(used, i, 0), 0
        return pl.BlockSpec((tr, c), index)

    shp = jax.ShapeDtypeStruct((DEPTH, r, c), F32)
    return pl.pallas_call(
        body,
        out_shape=(shp, shp, shp, shp),
        grid_spec=pltpu.PrefetchScalarGridSpec(
            num_scalar_prefetch=1, grid=(DEPTH, 2, nr),
            in_specs=[full, full, full] + [part(d, True) for d in range(DEPTH)] + [part(d, False) for d in range(DEPTH)],
            out_specs=(full, full, full, full)),
        compiler_params=_cp(("arbitrary", "arbitrary", "arbitrary")),
        name=name,
    )(core, w, m, v, *mine, *other)


WEIGHTS = ("norm_mix", "w_in", "b_gate", "q_norm_a", "k_norm_a", "rel_bias_a", "w_pool", "pool_scale",
           "w_branch_a", "w_branch_b", "w_branch_c", "w_out", "norm_ffn", "w_up", "conv_w", "conv_b", "w_down")
SHARDED = {"w_in": 2, "w_branch_a": 2, "w_branch_b": 2, "w_branch_c": 2, "w_out": 1, "w_up": 2, "conv_w": 2,
           "w_down": 1}
REPLICATED = tuple(n for n in WEIGHTS if n not in SHARDED)
MATMUL_WEIGHTS = tuple(n for n in SHARDED if n != "conv_w")
SMALL_WEIGHTS = tuple(n for n in WEIGHTS if n not in MATMUL_WEIGHTS)
SMALL_ROWS = 1496


def _layer_fwd(x, p, full, tables, rest=None, prefetch=None):
    l = p["l"]
    diag = exact_dot(p["rel_bias_a"], tables["onehot_t"], name="bias_diagonals")
    diag = diag.reshape(N_HEADS, N_VARIANTS, 1, DIAG_W).transpose(1, 0, 2, 3)
    biasm = bias_expand(diag, name="bias_expand")
    gq8 = jnp.tile(p["q_norm_a"], N_HEADS)[None]
    gk8 = jnp.tile(p["k_norm_a"], N_HEADS)[None]
    h = rmsnorm_fwd(x, p["norm_mix"][None], name="rmsnorm_fwd")
    if rest is None:
        proj = matmul(h, full["w_in"], b_layer=l, name="mm_in")
    else:
        plan, shards, names = rest
        proj, gathered = matmul(h, full["w_in"], b_layer=l, behind=(plan, shards), name="mm_in_gather")
        full = {**full, **dict(zip(names, gathered))}
    qa, ka, va, qb, kb, vb = qkv_prep(proj, gq8, gk8, name="qkv_prep")
    oa = attn_a_fwd(qa, ka, va, biasm, name="attn_a_fwd")
    if prefetch is None:
        ob, tot, nblk = attn_b_fwd(qb, kb, vb, name="attn_b_fwd")
    else:
        plan, shards, names = prefetch
        ob, tot, nblk, filled = attn_b_fwd(qb, kb, vb, gather=(plan, shards, [full[n] for n in names]),
                                           name="attn_b_fwd_gather")
        full = dict(zip(names, filled))
    wpool = p["w_pool"].astype(BF16)
    oc = pool_fwd(proj, wpool, p["pool_scale"][None], name="pool_fwd")
    merged = merge_fwd(oa, ob, oc, proj, p["b_gate"][None], full["w_branch_a"][l], full["w_branch_b"][l],
                       full["w_branch_c"][l], name="merge_fwd")
    x1 = matmul(merged, full["w_out"], b_layer=l, add=x, name="mm_out")
    h2 = rmsnorm_fwd(x1, p["norm_ffn"][None], name="rmsnorm_fwd")
    u = matmul(h2, full["w_up"], b_layer=l, name="mm_up")
    a = conv_glu_fwd(u, full["conv_w"][l], p["conv_b"][None], name="conv_glu_fwd")
    x2 = matmul(a, full["w_down"], b_layer=l, add=x1, name="mm_down")
    saved = dict(x=x, h=h, proj=proj, qa=qa, ka=ka, va=va, qb=qb, kb=kb, vb=vb, oa=oa, ob=ob, tot=tot, nblk=nblk, oc=oc,
                 merged=merged, x1=x1, h2=h2, u=u, a=a, biasm=biasm, gq8=gq8, gk8=gk8, wpool=wpool)
    return x2, saved, full


class GradReducer:
    def __init__(self, layouts, core, chip):
        self.layouts, self.core, self.chip = layouts, core, chip
        self.pairs32, self.received, self.pending = {}, {}, []

    def prepare(self, layer, grads):
        names = list(grads)
        lays = [self.layouts[n] for n in names]
        others = half_exchange([grads[n] for n in names], lays, name="half_exchange")
        for n, lay, other in zip(names, lays, others):
            p32, p16 = pair_sum(grads[n], other, lay, self.core, name="pair_sum")
            self.pairs32[(layer, n)] = p32
            self.pending.append(((layer, n), p16))

    def take(self):
        keys = [k for k, _ in self.pending]
        pairs16 = [p for _, p in self.pending]
        self.pending = []
        return keys, BlockScatter([self.layouts[n] for _, n in keys]), pairs16

    def store(self, keys, received):
        self.received.update(zip(keys, received))

    def finish(self):
        keys = list(self.pairs32)
        mine = [sum_chips(self.pairs32[k], self.received[k], self.layouts[k[1]], self.chip, name="sum_chips") for k in keys]
        other = sibling_swap(mine, name="sibling_swap")
        return dict(zip(keys, mine)), dict(zip(keys, other))


EARLY_WEIGHTS = ("w_down", "w_up", "w_out", "w_branch_a", "w_branch_b", "w_branch_c")


def _layer_bwd(dx2, s, p, tables, reducer):
    g = {}
    full, l = p["full"], p["l"]
    da = matmul(dx2, full["w_down"], b_layer=l, tb=True, name="mm_down_dx")
    g["w_down"] = matmul(s["a"], dx2, ta=True, name="mm_down_dw")
    dug, duv, dcwg, dcwv, dcbg, dcbv = conv_glu_bwd(s["u"], p["conv_w"], p["conv_b"][None], da, name="conv_glu_bwd")
    du = jnp.concatenate([dug, duv], axis=1)
    g["conv_w"] = jnp.concatenate([dcwg, dcwv], axis=1)
    g["conv_b"] = jnp.concatenate([dcbg, dcbv], axis=1)[0]
    g["w_up"] = matmul(s["h2"], du, ta=True, name="mm_up_dw")
    dx1, dg2 = matmul(du, full["w_up"], b_layer=l, tb=True, norm_bwd=(s["x1"], p["norm_ffn"][None], dx2),
                      name="mm_up_dx_norm")
    g["norm_ffn"] = dg2[0]
    dmerged = matmul(dx1, full["w_out"], b_layer=l, tb=True, name="mm_out_dx")
    g["w_out"] = matmul(s["merged"], dx1, ta=True, name="mm_out_dw")
    t_a, t_b, t_c, dga, dgb, dgc, dba, dbb, dbc = merge_bwd(
        dmerged, s["oa"], s["ob"], s["oc"], s["proj"], p["b_gate"][None], p["w_branch_a"], p["w_branch_b"],
        p["w_branch_c"], name="merge_bwd")
    g["b_gate"] = jnp.concatenate([dba, dbb, dbc], axis=1)[0]
    g["w_branch_a"] = matmul(s["oa"], t_a, ta=True, name="mm_branch_dw")
    g["w_branch_b"] = matmul(s["ob"], t_b, ta=True, name="mm_branch_dw")
    g["w_branch_c"] = matmul(s["oc"], t_c, ta=True, name="mm_branch_dw")
    doa = matmul(t_a, full["w_branch_a"], b_layer=l, tb=True, out_dtype=BF16, name="mm_branch_dx")
    dob = matmul(t_b, full["w_branch_b"], b_layer=l, tb=True, out_dtype=BF16, name="mm_branch_dx")
    doc = matmul(t_c, full["w_branch_c"], b_layer=l, tb=True, name="mm_branch_dx_f32")
    dqh, dkh, dva, dbias = attn_a_bwd(s["qa"], s["ka"], s["va"], s["biasm"], doa, name="attn_a_bwd")
    ddiag = relbias_reduce(dbias, name="relbias_reduce")
    ddiag = ddiag.transpose(1, 0, 2, 3).reshape(N_HEADS, N_VARIANTS * DIAG_W)
    g["rel_bias_a"] = exact_dot(ddiag, tables["onehot"], name="relbias_table")
    dqa, dka, dgq8, dgk8 = qknorm_bwd(s["proj"], s["gq8"], s["gk8"], dqh, dkh, name="qknorm_bwd")
    g["q_norm_a"] = dgq8.reshape(N_HEADS, HEAD_DIM).sum(axis=0)
    g["k_norm_a"] = dgk8.reshape(N_HEADS, HEAD_DIM).sum(axis=0)
    reducer.prepare(l, {n: g[n] for n in EARLY_WEIGHTS})
    keys, plan, pairs16 = reducer.take()
    dqb, dkb, dvb, received = attn_b_bwd(s["qb"], s["kb"], s["vb"], s["tot"], s["nblk"], dob, scatter=(plan, pairs16),
                                         name="attn_b_bwd_scatter")
    reducer.store(keys, received)
    duc, dwp, dsc = pool_bwd(s["proj"], s["wpool"], p["pool_scale"][None], doc, name="pool_bwd")
    g["w_pool"] = dwp
    g["pool_scale"] = dsc[0]
    dproj = jnp.concatenate([dqa, dka, dva.astype(BF16), dqb, dkb.astype(BF16), dvb.astype(BF16), duc,
                             dga, dgb, dgc], axis=1)
    g["w_in"] = matmul(s["h"], dproj, ta=True, name="mm_in_dw")
    reducer.prepare(l, {"w_in": g["w_in"]})
    norm = (s["x"], p["norm_mix"][None], dx1)
    if l > 0:
        dx, dg1 = matmul(dproj, full["w_in"], b_layer=l, tb=True, norm_bwd=norm, name="mm_in_dx_norm")
    else:
        keys, plan, pairs16 = reducer.take()
        dx, dg1, received = matmul(dproj, full["w_in"], b_layer=l, tb=True, norm_bwd=norm, behind=(plan, pairs16),
                                   name="mm_in_dx_norm_scatter")
        reducer.store(keys, received)
    g["norm_mix"] = dg1[0]
    return dx, g


def kernel(x, norm_mix, w_in, b_gate, q_norm_a, k_norm_a, rel_bias_a, w_pool, pool_scale, w_branch_a, w_branch_b, w_branch_c, w_out, norm_ffn, w_up, conv_w, conv_b, w_down, loss_target, m_norm_mix, m_w_in, m_b_gate, m_q_norm_a, m_k_norm_a, m_rel_bias_a, m_w_pool, m_pool_scale, m_w_branch_a, m_w_branch_b, m_w_branch_c, m_w_out, m_norm_ffn, m_w_up, m_conv_w, m_conv_b, m_w_down, v_norm_mix, v_w_in, v_b_gate, v_q_norm_a, v_k_norm_a, v_rel_bias_a, v_w_pool, v_pool_scale, v_w_branch_a, v_w_branch_b, v_w_branch_c, v_w_out, v_norm_ffn, v_w_up, v_conv_w, v_conv_b, v_w_down):
    w = dict(zip(WEIGHTS, (norm_mix, w_in, b_gate, q_norm_a, k_norm_a, rel_bias_a, w_pool, pool_scale, w_branch_a,
                           w_branch_b, w_branch_c, w_out, norm_ffn, w_up, conv_w, conv_b, w_down)))
    m = dict(zip(WEIGHTS, (m_norm_mix, m_w_in, m_b_gate, m_q_norm_a, m_k_norm_a, m_rel_bias_a, m_w_pool, m_pool_scale,
                           m_w_branch_a, m_w_branch_b, m_w_branch_c, m_w_out, m_norm_ffn, m_w_up, m_conv_w, m_conv_b,
                           m_w_down)))
    v = dict(zip(WEIGHTS, (v_norm_mix, v_w_in, v_b_gate, v_q_norm_a, v_k_norm_a, v_rel_bias_a, v_w_pool, v_pool_scale,
                           v_w_branch_a, v_w_branch_b, v_w_branch_c, v_w_out, v_norm_ffn, v_w_up, v_conv_w, v_conv_b,
                           v_w_down)))
    onehot = diagonal_onehot()
    tables = dict(onehot=jnp.asarray(onehot), onehot_t=jnp.asarray(np.ascontiguousarray(onehot.T)))

    names = tuple(SHARDED)
    shards = [w[n] if n == "conv_w" else w[n].astype(BF16) for n in names]
    axes = [SHARDED[n] for n in names]
    later = [i for i, n in enumerate(names) if n != "w_in"]
    rest = (LayerGather([shards[i] for i in later], [axes[i] for i in later], 0), [shards[i] for i in later],
            [names[i] for i in later])
    first = names.index("w_in")
    full = {"w_in": all_gather_layer([shards[first]], [axes[first]], 0, name="all_gather_layer")[0]}

    def layer_params(l):
        p = {n: full[n][l] for n in ("w_branch_a", "w_branch_b", "w_branch_c", "conv_w")}
        p.update({n: w[n][l] for n in REPLICATED})
        p.update(full=full, l=l)
        return p

    xs = x[0]
    saved = []
    for l in range(DEPTH):
        prefetch = (LayerGather(shards, axes, l + 1), shards, names) if l + 1 < DEPTH else None
        replicated = {n: w[n][l] for n in REPLICATED}
        xs, s, full = _layer_fwd(xs, dict(replicated, l=l), full, tables, rest if l == 0 else None, prefetch)
        saved.append(s)
    dx, lpart = loss_head(xs, loss_target[0], name="loss_head")
    loss = lax.psum(lpart[0, 0], MESH_AXES)
    as_index = lambda i: jnp.reshape(i, (1,)).astype(jnp.int32)
    cx, cy, cc = _mesh_pos()
    core, chip, me = as_index(cc), as_index(2 * cx + cy), as_index(4 * cx + 2 * cy + cc)
    layouts = {n: HalfLayout((full[n].shape[1], full[n].shape[2]), SHARDED[n] - 1) for n in MATMUL_WEIGHTS}
    reducer = GradReducer(layouts, core, chip)
    grads = [None] * DEPTH
    for l in reversed(range(DEPTH)):
        dx, grads[l] = _layer_bwd(dx, saved[l], layer_params(l), tables, reducer)

    g = {n: jnp.stack([grads[l][n] for l in range(DEPTH)]) for n in SMALL_WEIGHTS}
    flat = jnp.concatenate([g[n].reshape(-1) for n in SMALL_WEIGHTS])
    small = jnp.pad(flat, (0, SMALL_ROWS * 128 - flat.shape[0])).reshape(SMALL_ROWS, 128)
    small_sum = sum_devices(gather_small(small, name="gather_small"), small, me, name="sum_devices").reshape(-1)
    mine, other = reducer.finish()

    res = {}
    for n in MATMUL_WEIGHTS:
        res[n] = adamw_halves(w[n], m[n], v[n], [mine[(l, n)] for l in range(DEPTH)],
                              [other[(l, n)] for l in range(DEPTH)], layouts[n], core, name="adamw_halves")
    off = 0
    for n in SMALL_WEIGHTS:
        shp = g[n].shape
        size = int(np.prod(shp))
        gn = small_sum[off:off + size].reshape(shp)
        off += size
        if n in SHARDED:
            gn = lax.dynamic_slice_in_dim(gn, (2 * cx + cy) * w[n].shape[-1], w[n].shape[-1], axis=len(shp) - 1)
        shp = w[n].shape
        cols = shp[-1]
        two_d = lambda t: t.reshape(int(np.prod(shp)) // cols, cols)
        res[n] = [t.reshape(shp) for t in adamw(two_d(w[n]), two_d(m[n]), two_d(v[n]), two_d(gn), name="adamw")]

    out = [loss, dx[None]]
    for k in range(4):
        out.extend(res[n][k] for n in WEIGHTS)
    return tuple(out)
```

```python
import jax
import jax.numpy as jnp
import numpy as np
from jax import lax
from jax.experimental import pallas as pl
from jax.experimental.pallas import tpu as pltpu

F32 = jnp.float32
BF16 = jnp.bfloat16

D_MODEL = 1024
DEPTH = 2
CHUNK = 64
N_LEFT = 8
HEAD_DIM = 64
N_HEADS = 8
WIDTH = 512
POOL_WINDOWS = (2, 4, 8, 16)
GROUP_DIM = 128
MAX_REL = 2 * CHUNK
REL_TABLE = MAX_REL + CHUNK
D_FF = 2816
EPS = 1e-6
QK_SCALE = 0.125
GATE_COL0 = 7 * WIDTH

ADAM_LR = 0.001
ADAM_B1 = 0.9
ADAM_B2 = 0.999
ADAM_EPS = 1e-08
ADAM_WD = 0.01
ADAM_STEP = 10

VMEM_LIMIT = 56 * 1024 * 1024
ATT_Q = 256
A_Q = 256
A_WIN = A_Q + N_LEFT * CHUNK
A_FWD_HEADS = 8
A_BWD_HEADS = 4
B_FWD_HEADS = 8
B_BWD_HEADS = 4
HALO = 16
CONV_HALO = 8
NEG = -1e30

MESH_AXES = ("x", "y", "c")
MESH_T = pl.DeviceIdType.MESH


def _cp(sem=None, vmem=VMEM_LIMIT):
    return pltpu.CompilerParams(dimension_semantics=sem, vmem_limit_bytes=vmem)


def _dot(a, b, ca, cb):
    return lax.dot_general(a, b, (((ca,), (cb,)), ((), ())), preferred_element_type=F32)


LIGHT_ROWS = (2048, 1024, 512, 256, 128)


def _tile(n, cands=(1024, 512, 256, 128)):
    for c in cands:
        if n % c == 0:
            return c
    return n


def _split_hi_lo(v):
    hi = v.astype(BF16)
    lo = (v - hi.astype(F32)).astype(BF16)
    return hi, lo


def matmul(a, b, *, ta=False, tb=False, add=None, norm_bwd=None, out_dtype=F32, b_layer=None, behind=None, name):
    plan, sources = behind if behind is not None else (None, [])
    n_src = len(sources)
    n_dst = len(plan.out_shapes()) if plan is not None else 0
    assert add is None or norm_bwd is None
    if ta:
        K, M = a.shape
    else:
        M, K = a.shape
    if tb:
        N, K2 = b.shape[-2:]
    else:
        K2, N = b.shape[-2:]
    assert K == K2, (a.shape, b.shape, ta, tb)
    big = (1024, 1408, 512, 256, 128)
    tm = _tile(M, big)
    tn = _tile(N, (1664,) + big)
    tk = _tile(K, big if ta else (1664, 1408) + big if norm_bwd is not None else (3328, 2816) + big)
    nk = K // tk

    n_extra = 1 if add is not None else 3 if norm_bwd is not None else 0
    n_in = 2 + n_extra + n_src
    n_out = 2 if norm_bwd is not None else 1
    grid = (M // tm, N // tn, nk)
    assert norm_bwd is None or tn == N, "the norm gradient needs whole rows in one output tile"

    def body(*refs):
        a_ref, b_ref = refs[:2]
        extra = refs[2:2 + n_extra]
        o_ref, acc = refs[n_in], refs[n_in + n_out + n_dst]
        i, j, k = pl.program_id(0), pl.program_id(1), pl.program_id(2)
        if plan is not None:
            comm = (refs[n_in - n_src:n_in], refs[n_in + n_out:n_in + n_out + n_dst], refs[n_in + n_out + n_dst + 1:])

            @pl.when((i == 0) & (j == 0) & (k == 0))
            def _():
                plan.start(*comm)

        @pl.when(k == 0)
        def _():
            acc[...] = jnp.zeros_like(acc)

        av = a_ref[...].astype(BF16)
        bv = b_ref[...].astype(BF16)
        acc[...] += _dot(av, bv, 0 if ta else 1, 1 if tb else 0)

        @pl.when(k == nk - 1)
        def _():
            r = acc[...]
            if add is not None:
                r = r + extra[0][...].astype(F32)
            if norm_bwd is not None:
                x_ref, g_ref, dres_ref = extra
                dg_ref = refs[n_in + 1]
                xv = x_ref[...]
                inv = lax.rsqrt(jnp.mean(xv * xv, axis=-1, keepdims=True) + EPS)
                gd = r * g_ref[...]
                mean = jnp.mean(xv * gd, axis=-1, keepdims=True)

                @pl.when(i == 0)
                def _():
                    dg_ref[...] = jnp.zeros_like(dg_ref)

                dg_ref[...] += jnp.sum(r * xv * inv, axis=0, keepdims=True)
                r = dres_ref[...] + inv * gd - xv * (inv * inv * inv * mean)
            o_ref[...] = r.astype(out_dtype)

        if plan is not None:
            @pl.when((i == grid[0] - 1) & (j == grid[1] - 1) & (k == nk - 1))
            def _():
                plan.finish(*comm)

    a_spec = pl.BlockSpec((tk, tm), lambda i, j, k: (k, i)) if ta else pl.BlockSpec((tm, tk), lambda i, j, k: (i, k))
    if b_layer is None:
        b_spec = pl.BlockSpec((tn, tk), lambda i, j, k: (j, k)) if tb else pl.BlockSpec((tk, tn), lambda i, j, k: (k, j))
    elif tb:
        b_spec = pl.BlockSpec((None, tn, tk), lambda i, j, k: (b_layer, j, k))
    else:
        b_spec = pl.BlockSpec((None, tk, tn), lambda i, j, k: (b_layer, k, j))
    o_spec = pl.BlockSpec((tm, tn), lambda i, j, k: (i, j))
    vec_spec = pl.BlockSpec((1, tn), lambda i, j, k: (0, j))
    in_specs = [a_spec, b_spec]
    args = [a, b]
    out_shape = [jax.ShapeDtypeStruct((M, N), out_dtype)]
    out_specs = [o_spec]
    if add is not None:
        in_specs.append(o_spec)
        args.append(add)
    if norm_bwd is not None:
        in_specs += [o_spec, vec_spec, o_spec]
        args += list(norm_bwd)
        out_shape.append(jax.ShapeDtypeStruct((1, N), F32))
        out_specs.append(vec_spec)
    sequential = plan is not None or norm_bwd is not None
    any_space = pl.BlockSpec(memory_space=pl.ANY)
    outs = pl.pallas_call(
        body,
        out_shape=tuple(out_shape) + (tuple(plan.out_shapes()) if plan is not None else ()),
        grid=grid,
        in_specs=in_specs + [any_space] * n_src,
        out_specs=tuple(out_specs) + tuple([any_space] * n_dst),
        scratch_shapes=[pltpu.VMEM((tm, tn), F32)] + (list(plan.scratch_shapes()) if plan is not None else []),
        compiler_params=_cp(("arbitrary" if sequential else "parallel",) * 2 + ("arbitrary",)),
        name=name,
    )(*args, *sources)
    if plan is None:
        return outs[0] if n_out == 1 else tuple(outs)
    return tuple(outs[:n_out]) + (list(outs[n_out:]),) if n_out > 1 else (outs[0], list(outs[1:]))


def rmsnorm_fwd(x, g, *, name):
    S, D = x.shape
    T = _tile(S, LIGHT_ROWS)

    def body(x_ref, g_ref, h_ref):
        xv = x_ref[...]
        r = lax.rsqrt(jnp.mean(xv * xv, axis=-1, keepdims=True) + EPS)
        h_ref[...] = (xv * r * g_ref[...]).astype(BF16)

    return pl.pallas_call(
        body,
        out_shape=jax.ShapeDtypeStruct((S, D), BF16),
        grid=(S // T,),
        in_specs=[pl.BlockSpec((T, D), lambda i: (i, 0)), pl.BlockSpec((1, D), lambda i: (0, 0))],
        out_specs=pl.BlockSpec((T, D), lambda i: (i, 0)),
        compiler_params=_cp(("parallel",)),
        name=name,
    )(x, g)


def _head_mean_matrix():
    r = lax.broadcasted_iota(jnp.int32, (WIDTH, WIDTH), 0) // HEAD_DIM
    c = lax.broadcasted_iota(jnp.int32, (WIDTH, WIDTH), 1) // HEAD_DIM
    return jnp.where(r == c, 1.0 / HEAD_DIM, 0.0).astype(BF16)


def _head_mean(v, mm):
    hi, lo = _split_hi_lo(v)
    return _dot(hi, mm, 1, 0) + _dot(lo, mm, 1, 0)


def qkv_prep(proj, gq, gk, *, name):
    S = proj.shape[0]
    T = _tile(S)

    def body(qa, ka, va, qb, kb, vb, gq_ref, gk_ref, oqa, oka, ova, oqb, okb, ovb):
        mm = _head_mean_matrix()
        for src, gref, dst, scale in ((qa, gq_ref, oqa, QK_SCALE), (ka, gk_ref, oka, 1.0)):
            v = src[...]
            r = lax.rsqrt(_head_mean(v * v, mm) + EPS)
            dst[...] = (v * r * gref[...] * scale).astype(BF16)
        oqb[...] = (qb[...] * QK_SCALE).astype(BF16)
        for src, dst in ((va, ova), (kb, okb), (vb, ovb)):
            dst[...] = src[...].astype(BF16)

    col = lambda j: pl.BlockSpec((T, WIDTH), lambda i, j=j: (i, j))
    vec = pl.BlockSpec((1, WIDTH), lambda i: (0, 0))
    out = pl.BlockSpec((T, WIDTH), lambda i: (i, 0))
    return pl.pallas_call(
        body,
        out_shape=tuple(jax.ShapeDtypeStruct((S, WIDTH), BF16) for _ in range(6)),
        grid=(S // T,),
        in_specs=[col(0), col(1), col(2), col(3), col(4), col(5), vec, vec],
        out_specs=tuple(out for _ in range(6)),
        compiler_params=_cp(("parallel",)),
        name=name,
    )(proj, proj, proj, proj, proj, proj, gq, gk)


def qknorm_bwd(proj, gq, gk, dqh, dkh, *, name):
    S = proj.shape[0]
    T = _tile(S)

    def body(qa, ka, gq_ref, gk_ref, dq_ref, dk_ref, oq, ok, ogq, ogk):
        i = pl.program_id(0)
        mm = _head_mean_matrix()

        @pl.when(i == 0)
        def _():
            ogq[...] = jnp.zeros_like(ogq)
            ogk[...] = jnp.zeros_like(ogk)

        for src, gref, dref, dst, gdst in ((qa, gq_ref, dq_ref, oq, ogq), (ka, gk_ref, dk_ref, ok, ogk)):
            v = src[...]
            dy = dref[...]
            r = lax.rsqrt(_head_mean(v * v, mm) + EPS)
            gd = dy * gref[...]
            m = _head_mean(v * gd, mm)
            dst[...] = (r * gd - v * (r * r * r * m)).astype(BF16)
            gdst[...] += jnp.sum(dy * v * r, axis=0, keepdims=True)

    col = lambda j: pl.BlockSpec((T, WIDTH), lambda i, j=j: (i, j))
    vec = pl.BlockSpec((1, WIDTH), lambda i: (0, 0))
    row = pl.BlockSpec((T, WIDTH), lambda i: (i, 0))
    return pl.pallas_call(
        body,
        out_shape=(jax.ShapeDtypeStruct((S, WIDTH), BF16), jax.ShapeDtypeStruct((S, WIDTH), BF16),
                   jax.ShapeDtypeStruct((1, WIDTH), F32), jax.ShapeDtypeStruct((1, WIDTH), F32)),
        grid=(S // T,),
        in_specs=[col(0), col(1), vec, vec, row, row],
        out_specs=(row, row, vec, vec),
        compiler_params=_cp(("arbitrary",)),
        name=name,
    )(proj, proj, gq, gk, dqh, dkh)


DIAG_W = 1024
N_VARIANTS = N_LEFT * CHUNK // A_Q + 1


def diagonal_onehot():
    jj = np.arange(DIAG_W)
    diff = np.where(jj < A_WIN, jj, jj - DIAG_W)
    out = np.zeros((N_VARIANTS, DIAG_W, REL_TABLE), np.float32)
    for v in range(N_VARIANTS):
        rel = np.clip(A_Q * v - diff, -(CHUNK - 1), MAX_REL) + (CHUNK - 1)
        out[v, jj, rel] = 1.0
    return out.reshape(N_VARIANTS * DIAG_W, REL_TABLE)


def exact_dot(a, b, *, name):
    def body(a_ref, b_ref, o_ref):
        o_ref[...] = jnp.dot(a_ref[...], b_ref[...], precision=lax.Precision.HIGHEST, preferred_element_type=F32)

    return pl.pallas_call(body, out_shape=jax.ShapeDtypeStruct((a.shape[0], b.shape[1]), F32),
                          compiler_params=_cp(), name=name)(a, b)


def _band_valid(v):
    qc = (lax.broadcasted_iota(jnp.int32, (A_Q, A_WIN), 0) + A_Q * v) // CHUNK
    kc = lax.broadcasted_iota(jnp.int32, (A_Q, A_WIN), 1) // CHUNK
    return (kc <= qc) & (kc >= qc - N_LEFT)


def bias_expand(diag, *, name):
    def body(d_ref, o_ref):
        rows = jnp.broadcast_to(d_ref[0, 0], (A_Q, DIAG_W))
        skew = pltpu.roll(rows, 0, 1, stride=1, stride_axis=0)
        o_ref[0, 0] = jnp.where(_band_valid(pl.program_id(0)), skew[:, :A_WIN], NEG)

    return pl.pallas_call(
        body,
        out_shape=jax.ShapeDtypeStruct((N_VARIANTS, N_HEADS, A_Q, A_WIN), F32),
        grid=(N_VARIANTS, N_HEADS),
        in_specs=[pl.BlockSpec((1, 1, 1, DIAG_W), lambda v, h: (v, h, 0, 0))],
        out_specs=pl.BlockSpec((1, 1, A_Q, A_WIN), lambda v, h: (v, h, 0, 0)),
        compiler_params=_cp(("parallel", "parallel")),
        name=name,
    )(diag)


def relbias_reduce(dbias, *, name):
    def body(db_ref, o_ref):
        acc = None
        for a in range(A_Q // 8):
            x = jnp.concatenate([db_ref[0, 0, 8 * a:8 * a + 8, :], jnp.zeros((8, DIAG_W - A_WIN), F32)], axis=1)
            x = pltpu.roll(x, DIAG_W - 8 * a, 1) if a else x
            acc = x if acc is None else acc + x
        row = lax.broadcasted_iota(jnp.int32, (8, DIAG_W), 0)
        for b in range(3):
            acc = jnp.where((row >> b) & 1 == 1, pltpu.roll(acc, DIAG_W - (1 << b), 1), acc)
        o_ref[0, 0] = jnp.sum(acc, axis=0, keepdims=True)

    return pl.pallas_call(
        body,
        out_shape=jax.ShapeDtypeStruct((N_VARIANTS, N_HEADS, 1, DIAG_W), F32),
        grid=(N_VARIANTS, N_HEADS),
        in_specs=[pl.BlockSpec((1, 1, A_Q, A_WIN), lambda v, h: (v, h, 0, 0))],
        out_specs=pl.BlockSpec((1, 1, 1, DIAG_W), lambda v, h: (v, h, 0, 0)),
        compiler_params=_cp(("parallel", "parallel")),
        name=name,
    )(dbias)


def _a_window_start(qb):
    return pl.multiple_of(jnp.maximum(qb * A_Q - N_LEFT * CHUNK, 0), A_Q)


def attn_a_fwd(q, k, v, biasm, *, name):
    S = q.shape[0]
    nq = S // A_Q

    def body(q_ref, k_ref, v_ref, b_ref, o_ref):
        qb = pl.program_id(1)
        start = _a_window_start(qb)
        outs = []
        for h in range(A_FWD_HEADS):
            lanes = slice(h * HEAD_DIM, (h + 1) * HEAD_DIM)
            qh = q_ref[:, lanes]
            kw = k_ref[pl.ds(start, A_WIN), lanes]
            vw = v_ref[pl.ds(start, A_WIN), lanes]
            s = _dot(qh, kw, 1, 1) + b_ref[0, h]
            m = jnp.max(s, axis=-1, keepdims=True)
            e = jnp.exp(s - m)
            outs.append(_dot(e.astype(BF16), vw, 1, 0) * (1.0 / jnp.sum(e, axis=-1, keepdims=True)))
        o_ref[...] = jnp.concatenate(outs, axis=1).astype(BF16)

    qspec = pl.BlockSpec((A_Q, A_FWD_HEADS * HEAD_DIM), lambda hp, qb: (qb, hp))
    kvspec = pl.BlockSpec((S, A_FWD_HEADS * HEAD_DIM), lambda hp, qb: (0, hp))
    bspec = pl.BlockSpec((1, A_FWD_HEADS, A_Q, A_WIN), lambda hp, qb: (jnp.minimum(qb, N_VARIANTS - 1), hp, 0, 0))
    return pl.pallas_call(
        body,
        out_shape=jax.ShapeDtypeStruct((S, WIDTH), BF16),
        grid=(N_HEADS // A_FWD_HEADS, nq),
        in_specs=[qspec, kvspec, kvspec, bspec],
        out_specs=qspec,
        compiler_params=_cp(("parallel", "arbitrary")),
        name=name,
    )(q, k, v, biasm)


def attn_a_bwd(q, k, v, biasm, do, *, name):
    S = q.shape[0]
    nq = S // A_Q

    def body(q_ref, k_ref, v_ref, b_ref, do_ref, dq_ref, dk_ref, dv_ref, db_ref):
        qb = pl.program_id(1)
        start = _a_window_start(qb)

        @pl.when(qb == 0)
        def _():
            dk_ref[...] = jnp.zeros_like(dk_ref)
            dv_ref[...] = jnp.zeros_like(dv_ref)

        @pl.when(qb < N_VARIANTS)
        def _():
            db_ref[...] = jnp.zeros_like(db_ref)

        dqs = []
        for h in range(A_BWD_HEADS):
            lanes = slice(h * HEAD_DIM, (h + 1) * HEAD_DIM)
            qh = q_ref[:, lanes]
            doh = do_ref[:, lanes]
            kw = k_ref[pl.ds(start, A_WIN), lanes]
            vw = v_ref[pl.ds(start, A_WIN), lanes]
            s = _dot(qh, kw, 1, 1) + b_ref[0, h]
            m = jnp.max(s, axis=-1, keepdims=True)
            e = jnp.exp(s - m)
            p = e * (1.0 / jnp.sum(e, axis=-1, keepdims=True))
            dp = _dot(doh, vw, 1, 1)
            delta = jnp.sum(p * dp, axis=-1, keepdims=True)
            ds = p * (dp - delta)
            db_ref[0, h] += ds
            dsb = ds.astype(BF16)
            dqs.append(_dot(dsb, kw, 1, 0) * QK_SCALE)
            dk_ref[pl.ds(start, A_WIN), lanes] += _dot(dsb, qh, 0, 0)
            dv_ref[pl.ds(start, A_WIN), lanes] += _dot(p.astype(BF16), doh, 0, 0)
        dq_ref[...] = jnp.concatenate(dqs, axis=1)

    once = pl.Buffered(1)
    qspec = pl.BlockSpec((A_Q, A_BWD_HEADS * HEAD_DIM), lambda hp, qb: (qb, hp))
    kvspec = pl.BlockSpec((S, A_BWD_HEADS * HEAD_DIM), lambda hp, qb: (0, hp), pipeline_mode=once)
    bspec = pl.BlockSpec((1, A_BWD_HEADS, A_Q, A_WIN), lambda hp, qb: (jnp.minimum(qb, N_VARIANTS - 1), hp, 0, 0))
    return pl.pallas_call(
        body,
        out_shape=(jax.ShapeDtypeStruct((S, WIDTH), F32), jax.ShapeDtypeStruct((S, WIDTH), F32),
                   jax.ShapeDtypeStruct((S, WIDTH), F32), jax.ShapeDtypeStruct((N_VARIANTS, N_HEADS, A_Q, A_WIN), F32)),
        grid=(N_HEADS // A_BWD_HEADS, nq),
        in_specs=[qspec, kvspec, kvspec, bspec, qspec],
        out_specs=(qspec, kvspec, kvspec, bspec),
        compiler_params=_cp(("parallel", "arbitrary")),
        name=name,
    )(q, k, v, biasm, do)


def _tri(kind):
    j = lax.broadcasted_iota(jnp.int32, (ATT_Q, ATT_Q), 0)
    s = lax.broadcasted_iota(jnp.int32, (ATT_Q, ATT_Q), 1)
    if kind == "gt":
        m = j > s
    elif kind == "le":
        m = j <= s
    else:
        m = j < s
    return jnp.where(m, 1.0, 0.0).astype(BF16)


def _cum(v, tri):
    hi, lo = _split_hi_lo(v)
    return _dot(hi, tri, 1, 0) + _dot(lo, tri, 1, 0)


def _log_sigmoids(z, mask):
    t = jnp.log(1.0 + jnp.exp(-jnp.abs(z)))
    keep = -(jnp.maximum(z, 0.0) + t)
    take = jnp.minimum(z, 0.0) - t
    return (keep if mask is None else jnp.where(mask, keep, 0.0)), take


def _strictly_before():
    row = lax.broadcasted_iota(jnp.int32, (ATT_Q, ATT_Q), 0)
    col = lax.broadcasted_iota(jnp.int32, (ATT_Q, ATT_Q), 1)
    return col < row


EXIT_LOG = -104.0


def attn_b_fwd(q, k, v, *, gather=None, name):
    S = q.shape[0]
    nq = S // ATT_Q
    heads = B_FWD_HEADS
    plan, shards, fulls = gather if gather is not None else (None, [], [])
    ng = len(shards)

    def body(q_ref, k_ref, v_ref, *rest):
        hp = pl.program_id(0)
        qb = pl.program_id(1)
        o_ref, t_ref, n_ref = rest[2 * ng:2 * ng + 3]
        if plan is not None:
            comm = (rest[:ng], rest[2 * ng + 3:3 * ng + 3], rest[3 * ng + 3:])

            @pl.when(jnp.logical_and(hp == 0, qb == 0))
            def _():
                plan.start(*comm)

        tri = _tri("gt")

        def block(kb, carry, mask):
            ks = pl.multiple_of(kb * ATT_Q, ATT_Q)
            new = []
            for h in range(heads):
                lanes = slice(h * HEAD_DIM, (h + 1) * HEAD_DIM)
                c, acc = carry[h]
                z = _dot(q_ref[:, lanes], k_ref[pl.ds(ks, ATT_Q), lanes], 1, 1)
                keep, take = _log_sigmoids(z, mask)
                w = jnp.exp(take + (_cum(keep, tri) + c))
                if mask is not None:
                    w = jnp.where(mask, w, 0.0)
                acc = acc + _dot(w.astype(BF16), v_ref[pl.ds(ks, ATT_Q), lanes], 1, 0)
                c = c + jnp.sum(keep, axis=-1, keepdims=True)
                new.append((c, acc))
            return tuple(new)

        def cond(state):
            it, cmax, _ = state
            return jnp.logical_and(it <= qb, cmax >= EXIT_LOG)

        def step(state):
            it, _, carry = state
            carry = block(qb - it, carry, None)
            worst = carry[0][0]
            for h in range(1, heads):
                worst = jnp.maximum(worst, carry[h][0])
            return it + 1, jnp.max(worst), carry

        init = tuple((jnp.zeros((ATT_Q, 1), F32), jnp.zeros((ATT_Q, HEAD_DIM), F32)) for _ in range(heads))
        diag = block(qb, init, _strictly_before())
        visited, _, res = lax.while_loop(cond, step, (jnp.int32(1), jnp.float32(0.0), diag))
        o_ref[...] = jnp.concatenate([res[h][1] for h in range(heads)], axis=1).astype(BF16)
        t_ref[...] = jnp.concatenate([jnp.broadcast_to(res[h][0], (ATT_Q, HEAD_DIM)) for h in range(heads)], axis=1)
        n_ref[hp, qb] = visited.astype(F32)
        if plan is not None:
            @pl.when(jnp.logical_and(hp == N_HEADS // heads - 1, qb == nq - 1))
            def _():
                plan.finish(*comm)

    qspec = pl.BlockSpec((ATT_Q, heads * HEAD_DIM), lambda hp, qb: (qb, hp))
    kvspec = pl.BlockSpec((S, heads * HEAD_DIM), lambda hp, qb: (0, hp), pipeline_mode=pl.Buffered(1))
    outs = pl.pallas_call(
        body,
        out_shape=(jax.ShapeDtypeStruct((S, WIDTH), BF16), jax.ShapeDtypeStruct((S, WIDTH), F32),
                   jax.ShapeDtypeStruct((N_HEADS // heads, nq), F32))
        + tuple(jax.ShapeDtypeStruct(f.shape, f.dtype) for f in fulls),
        grid=(N_HEADS // heads, nq),
        in_specs=[qspec, kvspec, kvspec] + [ANY] * (2 * ng),
        out_specs=(qspec, qspec, pl.BlockSpec(memory_space=pltpu.SMEM)) + tuple([ANY] * ng),
        scratch_shapes=plan.scratch_shapes() if plan is not None else (),
        input_output_aliases={3 + ng + i: 3 + i for i in range(ng)},
        compiler_params=_cp(("arbitrary", "arbitrary")),
        name=name,
    )(q, k, v, *shards, *fulls)
    return outs if plan is None else (outs[0], outs[1], outs[2], list(outs[3:]))


def attn_b_bwd(q, k, v, tot, nblk, do, *, scatter=None, name):
    S = q.shape[0]
    nq = S // ATT_Q
    heads = B_BWD_HEADS
    plan, pairs16 = scatter if scatter is not None else (None, [])
    ns = len(pairs16)

    def body(q_ref, k_ref, v_ref, t_ref, n_ref, do_ref, *rest):
        hp = pl.program_id(0)
        qb = pl.program_id(1)
        dq_ref, dk_ref, dv_ref = rest[ns:ns + 3]
        if plan is not None:
            comm = (rest[:ns], rest[ns + 3:2 * ns + 3], rest[2 * ns + 3:])

            @pl.when(jnp.logical_and(hp == 0, qb == 0))
            def _():
                plan.start(*comm)

        visited = n_ref[(hp * heads) // B_FWD_HEADS, qb].astype(jnp.int32)
        first = jnp.clip(qb + 1 - visited, 0, qb + 1)
        tri_le = _tri("le")
        tri_lt = _tri("lt")

        @pl.when(qb == 0)
        def _():
            dk_ref[...] = jnp.zeros_like(dk_ref)
            dv_ref[...] = jnp.zeros_like(dv_ref)

        def block(kb, carry, mask):
            ks = pl.multiple_of(kb * ATT_Q, ATT_Q)
            new = []
            for h in range(heads):
                lanes = slice(h * HEAD_DIM, (h + 1) * HEAD_DIM)
                cl, cg, dq = carry[h]
                qh = q_ref[:, lanes]
                doh = do_ref[:, lanes]
                kh = k_ref[pl.ds(ks, ATT_Q), lanes]
                vh = v_ref[pl.ds(ks, ATT_Q), lanes]
                totl = t_ref[:, h * HEAD_DIM:h * HEAD_DIM + 1]
                z = _dot(qh, kh, 1, 1)
                keep, take = _log_sigmoids(z, mask)
                sig = jnp.exp(take)
                w = sig * jnp.exp((totl - cl) - _cum(keep, tri_le))
                if mask is not None:
                    w = jnp.where(mask, w, 0.0)
                g = w * _dot(doh, vh, 1, 1)
                G = _dot(g.astype(BF16), tri_lt, 1, 0) + cg
                dz = g * (1.0 - sig) - sig * G
                if mask is not None:
                    dz = jnp.where(mask, dz, 0.0)
                dz = dz.astype(BF16)
                dq = dq + _dot(dz, kh, 1, 0)
                dk_ref[pl.ds(ks, ATT_Q), lanes] += _dot(dz, qh, 0, 0)
                dv_ref[pl.ds(ks, ATT_Q), lanes] += _dot(w.astype(BF16), doh, 0, 0)
                cl = cl + jnp.sum(keep, axis=-1, keepdims=True)
                cg = cg + jnp.sum(g, axis=-1, keepdims=True)
                new.append((cl, cg, dq))
            return tuple(new)

        init = tuple((jnp.zeros((ATT_Q, 1), F32), jnp.zeros((ATT_Q, 1), F32), jnp.zeros((ATT_Q, HEAD_DIM), F32))
                     for _ in range(heads))
        res = lax.fori_loop(jnp.minimum(first, qb), qb, lambda kb, carry: block(kb, carry, None), init)
        res = block(qb, res, _strictly_before())
        dq_ref[...] = (jnp.concatenate([res[h][2] for h in range(heads)], axis=1) * QK_SCALE).astype(BF16)
        if plan is not None:
            @pl.when(jnp.logical_and(hp == N_HEADS // heads - 1, qb == nq - 1))
            def _():
                plan.finish(*comm)

    qspec = pl.BlockSpec((ATT_Q, heads * HEAD_DIM), lambda hp, qb: (qb, hp))
    kvspec = pl.BlockSpec((S, heads * HEAD_DIM), lambda hp, qb: (0, hp), pipeline_mode=pl.Buffered(1))
    outs = pl.pallas_call(
        body,
        out_shape=(jax.ShapeDtypeStruct((S, WIDTH), BF16), jax.ShapeDtypeStruct((S, WIDTH), F32),
                   jax.ShapeDtypeStruct((S, WIDTH), F32)) + (tuple(plan.out_shapes()) if plan is not None else ()),
        grid=(N_HEADS // heads, nq),
        in_specs=[qspec, kvspec, kvspec, qspec, pl.BlockSpec(memory_space=pltpu.SMEM), qspec] + [ANY] * ns,
        out_specs=(qspec, kvspec, kvspec) + tuple([ANY] * ns),
        scratch_shapes=plan.scratch_shapes() if plan is not None else (),
        compiler_params=_cp(("arbitrary", "arbitrary")),
        name=name,
    )(q, k, v, tot, nblk, do, *pairs16)
    return outs if plan is None else (outs[0], outs[1], outs[2], list(outs[3:]))


U_COLBLK = 6


def _pool_counts(t0, rows):
    t = t0 + lax.broadcasted_iota(jnp.int32, (rows, WIDTH), 0)
    lane_grp = lax.broadcasted_iota(jnp.int32, (rows, WIDTH), 1) // GROUP_DIM
    w2, w4, w8, w16 = POOL_WINDOWS
    win = jnp.where(lane_grp == 0, w2, jnp.where(lane_grp == 1, w4, jnp.where(lane_grp == 2, w8, w16)))
    cnt = jnp.minimum(t + 1, win)
    return 1.0 / cnt.astype(F32), lane_grp


def _window_sums(ext, shift_fn):
    s2 = ext + shift_fn(ext, 1)
    s4 = s2 + shift_fn(s2, 2)
    s8 = s4 + shift_fn(s4, 4)
    s16 = s8 + shift_fn(s8, 8)
    return s2, s4, s8, s16


def _select_group(lane_grp, s2, s4, s8, s16):
    return jnp.where(lane_grp == 0, s2, jnp.where(lane_grp == 1, s4, jnp.where(lane_grp == 2, s8, s16)))


def _pooled_tile(u_ref, h_ref, i, T):
    halo = jnp.where(i > 0, h_ref[...], 0.0)
    ext = jnp.concatenate([halo, u_ref[...]], axis=0)
    n = T + HALO
    sums = _window_sums(ext, lambda v, k: pltpu.roll(v, k, 0))
    inv, lane_grp = _pool_counts(i * T - HALO, n)
    pooled = _select_group(lane_grp, *sums) * inv - ext
    return pooled[HALO:, :]


def pool_fwd(proj, w_pool, scale, *, name):
    S = proj.shape[0]
    T = _tile(S)
    hb = T // HALO

    def body(u_ref, h_ref, w_ref, s_ref, o_ref):
        i = pl.program_id(0)
        pooled = _pooled_tile(u_ref, h_ref, i, T).astype(BF16)
        outs = [_dot(pooled[:, g * GROUP_DIM:(g + 1) * GROUP_DIM], w_ref[g], 1, 0) for g in range(4)]
        o_ref[...] = (jnp.concatenate(outs, axis=1) * s_ref[...]).astype(BF16)

    return pl.pallas_call(
        body,
        out_shape=jax.ShapeDtypeStruct((S, WIDTH), BF16),
        grid=(S // T,),
        in_specs=[pl.BlockSpec((T, WIDTH), lambda i: (i, U_COLBLK)),
                  pl.BlockSpec((HALO, WIDTH), lambda i: (jnp.maximum(i * hb - 1, 0), U_COLBLK)),
                  pl.BlockSpec((4, GROUP_DIM, GROUP_DIM), lambda i: (0, 0, 0)),
                  pl.BlockSpec((1, WIDTH), lambda i: (0, 0))],
        out_specs=pl.BlockSpec((T, WIDTH), lambda i: (i, 0)),
        compiler_params=_cp(("parallel",)),
        name=name,
    )(proj, proj, w_pool, scale)


def pool_bwd(proj, w_pool, scale, do, *, name):
    S = proj.shape[0]
    T = _tile(S)
    hb = T // HALO
    nt = S // T

    def body(u_ref, h_ref, w_ref, s_ref, do_ref, dof_ref, du_ref, dw_ref, ds_ref):
        i = pl.program_id(0)

        @pl.when(i == 0)
        def _():
            dw_ref[...] = jnp.zeros_like(dw_ref)
            ds_ref[...] = jnp.zeros_like(ds_ref)

        pooled = _pooled_tile(u_ref, h_ref, i, T).astype(BF16)
        dov = do_ref[...].astype(F32)
        fut = jnp.where(i < nt - 1, dof_ref[...].astype(F32), 0.0)
        dmix = (jnp.concatenate([dov, fut], axis=0) * s_ref[...]).astype(BF16)
        mixed, dpool = [], []
        for g in range(4):
            lanes = slice(g * GROUP_DIM, (g + 1) * GROUP_DIM)
            mixed.append(_dot(pooled[:, lanes], w_ref[g], 1, 0))
            dw_ref[g] += _dot(pooled[:, lanes], dmix[:T, lanes], 0, 0)
            dpool.append(_dot(dmix[:, lanes], w_ref[g], 1, 1))
        ds_ref[...] += jnp.sum(dov * jnp.concatenate(mixed, axis=1), axis=0, keepdims=True)
        dp = jnp.concatenate(dpool, axis=1)
        n = T + HALO
        inv, lane_grp = _pool_counts(i * T, n)
        sums = _window_sums(dp * inv, lambda v, k: pltpu.roll(v, n - k, 0))
        du = _select_group(lane_grp, *sums) - dp
        du_ref[...] = du[:T, :].astype(BF16)

    row = pl.BlockSpec((T, WIDTH), lambda i: (i, 0))
    return pl.pallas_call(
        body,
        out_shape=(jax.ShapeDtypeStruct((S, WIDTH), BF16), jax.ShapeDtypeStruct((4, GROUP_DIM, GROUP_DIM), F32),
                   jax.ShapeDtypeStruct((1, WIDTH), F32)),
        grid=(nt,),
        in_specs=[pl.BlockSpec((T, WIDTH), lambda i: (i, U_COLBLK)),
                  pl.BlockSpec((HALO, WIDTH), lambda i: (jnp.maximum(i * hb - 1, 0), U_COLBLK)),
                  pl.BlockSpec((4, GROUP_DIM, GROUP_DIM), lambda i: (0, 0, 0)),
                  pl.BlockSpec((1, WIDTH), lambda i: (0, 0)),
                  row,
                  pl.BlockSpec((HALO, WIDTH), lambda i: (jnp.minimum((i + 1) * hb, S // HALO - 1), 0))],
        out_specs=(row, pl.BlockSpec((4, GROUP_DIM, GROUP_DIM), lambda i: (0, 0, 0)),
                   pl.BlockSpec((1, WIDTH), lambda i: (0, 0))),
        compiler_params=_cp(("arbitrary",)),
        name=name,
    )(proj, proj, w_pool, scale, do, do)


GATE_BLK0 = GATE_COL0 // WIDTH


def merge_fwd(oa, ob, oc, proj, b_gate, wa, wb, wc, *, name):
    S = oa.shape[0]
    T = _tile(S)

    def body(oa_ref, ob_ref, oc_ref, ga, gb, gc, ba, bb, bc, wa_ref, wb_ref, wc_ref, m_ref):
        acc = None
        for o_ref, g_ref, b_ref, w_ref in ((oa_ref, ga, ba, wa_ref), (ob_ref, gb, bb, wb_ref), (oc_ref, gc, bc, wc_ref)):
            y = _dot(o_ref[...], w_ref[...], 1, 0)
            t = jax.nn.sigmoid(g_ref[...] + b_ref[...]) * y
            acc = t if acc is None else acc + t
        m_ref[...] = acc.astype(BF16)

    row = pl.BlockSpec((T, WIDTH), lambda i, n: (i, 0))
    gate = lambda b: pl.BlockSpec((T, WIDTH), lambda i, n, b=b: (i, GATE_BLK0 + 2 * b + n))
    bias = lambda b: pl.BlockSpec((1, WIDTH), lambda i, n, b=b: (0, 2 * b + n))
    wspec = pl.BlockSpec((WIDTH, WIDTH), lambda i, n: (0, n))
    return pl.pallas_call(
        body,
        out_shape=jax.ShapeDtypeStruct((S, D_MODEL), BF16),
        grid=(S // T, 2),
        in_specs=[row, row, row, gate(0), gate(1), gate(2), bias(0), bias(1), bias(2), wspec, wspec, wspec],
        out_specs=pl.BlockSpec((T, WIDTH), lambda i, n: (i, n)),
        compiler_params=_cp(("parallel", "parallel")),
        name=name,
    )(oa, ob, oc, proj, proj, proj, b_gate, b_gate, b_gate, wa, wb, wc)


def merge_bwd(dm, oa, ob, oc, proj, b_gate, wa, wb, wc, *, name):
    S = oa.shape[0]
    T = _tile(S)

    def body(dm_ref, oa_ref, ob_ref, oc_ref, ga, gb, gc, ba, bb, bc, wa_ref, wb_ref, wc_ref,
             ta, tb, tc, dga, dgb, dgc, dba, dbb, dbc):
        i = pl.program_id(1)
        dmv = dm_ref[...].astype(F32)
        for o_ref, g_ref, b_ref, w_ref, t_ref, dg_ref, db_ref in (
                (oa_ref, ga, ba, wa_ref, ta, dga, dba), (ob_ref, gb, bb, wb_ref, tb, dgb, dbb),
                (oc_ref, gc, bc, wc_ref, tc, dgc, dbc)):
            y = _dot(o_ref[...], w_ref[...], 1, 0)
            gate = jax.nn.sigmoid(g_ref[...] + b_ref[...])
            t_ref[...] = (gate * dmv).astype(BF16)
            dgl = dmv * y * gate * (1.0 - gate)
            dg_ref[...] = dgl.astype(BF16)

            @pl.when(i == 0)
            def _():
                db_ref[...] = jnp.zeros_like(db_ref)

            db_ref[...] += jnp.sum(dgl, axis=0, keepdims=True)

    row = pl.BlockSpec((T, WIDTH), lambda n, i: (i, 0))
    half = pl.BlockSpec((T, WIDTH), lambda n, i: (i, n))
    gate = lambda b: pl.BlockSpec((T, WIDTH), lambda n, i, b=b: (i, GATE_BLK0 + 2 * b + n))
    bias = lambda b: pl.BlockSpec((1, WIDTH), lambda n, i, b=b: (0, 2 * b + n))
    wspec = pl.BlockSpec((WIDTH, WIDTH), lambda n, i: (0, n))
    bvec = pl.BlockSpec((1, WIDTH), lambda n, i: (0, n))
    act = jax.ShapeDtypeStruct((S, D_MODEL), BF16)
    vec = jax.ShapeDtypeStruct((1, D_MODEL), F32)
    return pl.pallas_call(
        body,
        out_shape=(act, act, act, act, act, act, vec, vec, vec),
        grid=(2, S // T),
        in_specs=[half, row, row, row, gate(0), gate(1), gate(2), bias(0), bias(1), bias(2), wspec, wspec, wspec],
        out_specs=(half, half, half, half, half, half, bvec, bvec, bvec),
        compiler_params=_cp(("parallel", "arbitrary")),
        name=name,
    )(dm, oa, ob, oc, proj, proj, proj, b_gate, b_gate, b_gate, wa, wb, wc)


FF_T = 256
FF_BLKS = D_FF // FF_T
CONV_ROWS = (2048, 1024, 512, 256, 128)
CONV_FWD_ROWS = (4096,) + CONV_ROWS


def _silu_parts(x):
    s = jax.nn.sigmoid(x)
    return x * s, s


def _conv3(ext, w_ref, b_ref):
    taps = (pltpu.roll(ext, 2, 0), pltpu.roll(ext, 1, 0), ext)
    return b_ref[...] + w_ref[0:1, :] * taps[0] + w_ref[1:2, :] * taps[1] + w_ref[2:3, :] * taps[2], taps


def conv_glu_fwd(u, conv_w, conv_b, *, name):
    S = u.shape[0]
    T = _tile(S, CONV_FWD_ROWS)
    hb = T // CONV_HALO

    def body(ug, ugh, uv, uvh, wg, wv, bg, bv, a_ref):
        i = pl.program_id(1)
        cs = []
        for m_ref, h_ref, w_ref, b_ref in ((ug, ugh, wg, bg), (uv, uvh, wv, bv)):
            halo = jnp.where(i > 0, h_ref[...], 0.0)
            ext = jnp.concatenate([halo, m_ref[...]], axis=0)
            cs.append(_conv3(ext, w_ref, b_ref)[0][CONV_HALO:, :])
        act, _ = _silu_parts(cs[0])
        a_ref[...] = (act * cs[1]).astype(BF16)

    main = lambda o: pl.BlockSpec((T, FF_T), lambda c, i, o=o: (i, c + o))
    halo = lambda o: pl.BlockSpec((CONV_HALO, FF_T), lambda c, i, o=o: (jnp.maximum(i * hb - 1, 0), c + o))
    wsp = lambda o: pl.BlockSpec((3, FF_T), lambda c, i, o=o: (0, c + o))
    bsp = lambda o: pl.BlockSpec((1, FF_T), lambda c, i, o=o: (0, c + o))
    return pl.pallas_call(
        body,
        out_shape=jax.ShapeDtypeStruct((S, D_FF), BF16),
        grid=(FF_BLKS, S // T),
        in_specs=[main(0), halo(0), main(FF_BLKS), halo(FF_BLKS), wsp(0), wsp(FF_BLKS), bsp(0), bsp(FF_BLKS)],
        out_specs=pl.BlockSpec((T, FF_T), lambda c, i: (i, c)),
        compiler_params=_cp(("parallel", "parallel")),
        name=name,
    )(u, u, u, u, conv_w, conv_w, conv_b, conv_b)


def conv_glu_bwd(u, conv_w, conv_b, da, *, name):
    S = u.shape[0]
    T = _tile(S, CONV_ROWS)
    hb = T // CONV_HALO
    nt = S // T
    n = T + 2 * CONV_HALO

    def body(ug, ugp, ugf, uv, uvp, uvf, wg, wv, bg, bv, da_ref, daf_ref,
             dug, duv, dwg, dwv, dbg, dbv):
        i = pl.program_id(1)
        first, last = i == 0, i == nt - 1
        taps, cs = [], []
        for m_ref, p_ref, f_ref, w_ref, b_ref in ((ug, ugp, ugf, wg, bg), (uv, uvp, uvf, wv, bv)):
            ext = jnp.concatenate([jnp.where(first, 0.0, p_ref[...]), m_ref[...], jnp.where(last, 0.0, f_ref[...])], axis=0)
            c, tp = _conv3(ext, w_ref, b_ref)
            cs.append(c)
            taps.append(tp)
        dae = jnp.concatenate([jnp.zeros((CONV_HALO, FF_T), F32), da_ref[...].astype(F32),
                               jnp.where(last, 0.0, daf_ref[...].astype(F32))], axis=0)
        act, sg = _silu_parts(cs[0])
        dcs = (dae * cs[1] * (sg * (1.0 + cs[0] * (1.0 - sg))), dae * act)
        main = slice(CONV_HALO, CONV_HALO + T)
        for tp, dc, w_ref, du_ref, dw_ref, db_ref in ((taps[0], dcs[0], wg, dug, dwg, dbg),
                                                      (taps[1], dcs[1], wv, duv, dwv, dbv)):
            du = (w_ref[2:3, :] * dc + w_ref[1:2, :] * pltpu.roll(dc, n - 1, 0) + w_ref[0:1, :] * pltpu.roll(dc, n - 2, 0))
            du_ref[...] = du[main, :].astype(BF16)
            dcm = dc[main, :]
            rows = [jnp.sum(dcm * tp[j][main, :], axis=0, keepdims=True) for j in range(3)]

            @pl.when(first)
            def _():
                dw_ref[...] = jnp.zeros_like(dw_ref)
                db_ref[...] = jnp.zeros_like(db_ref)

            dw_ref[...] += jnp.concatenate(rows, axis=0)
            db_ref[...] += jnp.sum(dcm, axis=0, keepdims=True)

    main = lambda o: pl.BlockSpec((T, FF_T), lambda c, i, o=o: (i, c + o))
    past = lambda o: pl.BlockSpec((CONV_HALO, FF_T), lambda c, i, o=o: (jnp.maximum(i * hb - 1, 0), c + o))
    fut = lambda o: pl.BlockSpec((CONV_HALO, FF_T), lambda c, i, o=o: (jnp.minimum((i + 1) * hb, S // CONV_HALO - 1), c + o))
    wsp = lambda o: pl.BlockSpec((3, FF_T), lambda c, i, o=o: (0, c + o))
    bsp = lambda o: pl.BlockSpec((1, FF_T), lambda c, i, o=o: (0, c + o))
    return pl.pallas_call(
        body,
        out_shape=(jax.ShapeDtypeStruct((S, D_FF), BF16), jax.ShapeDtypeStruct((S, D_FF), BF16),
                   jax.ShapeDtypeStruct((3, D_FF), F32), jax.ShapeDtypeStruct((3, D_FF), F32),
                   jax.ShapeDtypeStruct((1, D_FF), F32), jax.ShapeDtypeStruct((1, D_FF), F32)),
        grid=(FF_BLKS, nt),
        in_specs=[main(0), past(0), fut(0), main(FF_BLKS), past(FF_BLKS), fut(FF_BLKS),
                  wsp(0), wsp(FF_BLKS), bsp(0), bsp(FF_BLKS), main(0), fut(0)],
        out_specs=(main(0), main(0), wsp(0), wsp(0), bsp(0), bsp(0)),
        compiler_params=_cp(("parallel", "arbitrary")),
        name=name,
    )(u, u, u, u, u, u, conv_w, conv_w, conv_b, conv_b, da, da)


def loss_head(y, target, *, name):
    S, D = y.shape
    T = _tile(S, LIGHT_ROWS)

    def body(y_ref, t_ref, dy_ref, l_ref):
        i = pl.program_id(0)
        err = y_ref[...] - t_ref[...]
        dy_ref[...] = err * (1.0 / D)

        @pl.when(i == 0)
        def _():
            l_ref[...] = jnp.zeros_like(l_ref)

        l_ref[...] += 0.5 * jnp.sum(jnp.mean(err * err, axis=-1, keepdims=True))

    row = pl.BlockSpec((T, D), lambda i: (i, 0))
    return pl.pallas_call(
        body,
        out_shape=(jax.ShapeDtypeStruct((S, D), F32), jax.ShapeDtypeStruct((8, 128), F32)),
        grid=(S // T,),
        in_specs=[row, row],
        out_specs=(row, pl.BlockSpec((8, 128), lambda i: (0, 0))),
        compiler_params=_cp(("arbitrary",)),
        name=name,
    )(y, target)


ELEMS_PER_BLOCK = 512 * 1024


def _rows_tile(rows, cols):
    if rows * cols <= ELEMS_PER_BLOCK or rows % 8:
        return rows
    best = 8
    for tr in range(8, rows + 1, 8):
        if rows % tr == 0 and tr * cols <= ELEMS_PER_BLOCK:
            best = tr
    return best


def _adamw_math(g, w_ref, m_ref, v_ref, g_out, d_out, m_out, v_out):
    mn = ADAM_B1 * m_ref[...] + (1.0 - ADAM_B1) * g
    vn = ADAM_B2 * v_ref[...] + (1.0 - ADAM_B2) * (g * g)
    m_hat = mn / (1.0 - ADAM_B1 ** ADAM_STEP)
    v_hat = vn / (1.0 - ADAM_B2 ** ADAM_STEP)
    g_out[...] = g
    d_out[...] = -ADAM_LR * (m_hat / (jnp.sqrt(v_hat) + ADAM_EPS) + ADAM_WD * w_ref[...])
    m_out[...] = mn
    v_out[...] = vn


def adamw(w, m, v, g, *, name):
    rows, cols = w.shape
    tr = _rows_tile(rows, cols)

    def body(w_ref, m_ref, v_ref, g_ref, g_out, d_out, m_out, v_out):
        _adamw_math(g_ref[...], w_ref, m_ref, v_ref, g_out, d_out, m_out, v_out)

    spec = pl.BlockSpec((tr, cols), lambda i: (i, 0))
    shp = jax.ShapeDtypeStruct((rows, cols), F32)
    return pl.pallas_call(
        body,
        out_shape=(shp, shp, shp, shp),
        grid=(rows // tr,),
        in_specs=[spec] * 4,
        out_specs=(spec, spec, spec, spec),
        compiler_params=_cp(("parallel",)),
        name=name,
    )(w, m, v, g)


ANY = pl.BlockSpec(memory_space=pl.ANY)
STAGE_BYTES = 2 * 1024 * 1024


def _mesh_pos():
    return lax.axis_index("x"), lax.axis_index("y"), lax.axis_index("c")


def _chip_peers(x, y):
    return [(1 - x, y), (x, 1 - y), (1 - x, 1 - y)]


def _all_peers(x, y, c):
    return [((1 - x) if (r >> 2) & 1 else x, (1 - y) if (r >> 1) & 1 else y, (1 - c) if r & 1 else c)
            for r in range(1, 8)]


class LayerGather:
    def __init__(self, shards, axes, layer):
        self.nt = len(shards)
        self.axes = list(axes)
        self.layer = layer
        self.shapes = [s.shape for s in shards]
        self.dtypes = [s.dtype for s in shards]
        self.sizes = [s.shape[a] for s, a in zip(shards, axes)]
        self.split = [s.shape[1] % 32 == 0 for s in shards]
        self.half_rows = [s.shape[1] // 2 if sp else s.shape[1] for s, sp in zip(shards, self.split)]
        self.chunk_rows = []
        for s in shards:
            rt = s.shape[1]
            while rt % 32 == 0 and rt * s.shape[2] * s.dtype.itemsize > STAGE_BYTES:
                rt //= 2
            self.chunk_rows.append(rt)

    def out_shapes(self):
        out = []
        for shp, a, sz, dt in zip(self.shapes, self.axes, self.sizes, self.dtypes):
            shp = list(shp)
            shp[a] = 4 * sz
            out.append(jax.ShapeDtypeStruct(tuple(shp), dt))
        return out

    def scratch_shapes(self):
        return ([pltpu.VMEM((1, rt, shp[2]), dt) for shp, rt, dt in zip(self.shapes, self.chunk_rows, self.dtypes)]
                + [pltpu.SemaphoreType.DMA((2 * self.nt,))] + [pltpu.SemaphoreType.DMA((3 * self.nt,)) for _ in range(4)])

    def _views(self, ins, outs, scratch):
        nt = self.nt
        stage, stage_sems = scratch[:nt], scratch[nt]
        ici_send, ici_recv, d2d_send, d2d_recv = scratch[nt + 1:]
        x, y, c = _mesh_pos()
        mine = 2 * x + y
        peers = _chip_peers(x, y)
        layer = pl.ds(self.layer, 1)

        def rows(t, half, r0=0, n=None):
            hr = self.half_rows[t]
            if n is None:
                return pl.ds(pl.multiple_of(half * hr, 16), hr) if self.split[t] else pl.ds(0, hr)
            return pl.ds(r0, n)

        def placed(t, blk, row_sel, row_len):
            sz = self.sizes[t]
            if self.axes[t] == 2:
                return outs[t].at[layer, row_sel, pl.ds(pl.multiple_of(blk * sz, 128), sz)]
            return outs[t].at[layer, pl.ds(pl.multiple_of(blk * sz, 16) + row_sel.start, row_len), :]

        def ici(t, k, blk):
            px, py = peers[k]
            sel = rows(t, c)
            return pltpu.make_async_remote_copy(
                src_ref=ins[t].at[layer, sel, :], dst_ref=placed(t, blk, sel, self.half_rows[t]),
                send_sem=ici_send.at[3 * t + k], recv_sem=ici_recv.at[3 * t + k],
                device_id=(px, py, c), device_id_type=MESH_T)

        def d2d(t, k, half):
            px, py = peers[k]
            piece = placed(t, 2 * px + py, rows(t, half), self.half_rows[t])
            return pltpu.make_async_remote_copy(
                src_ref=piece, dst_ref=piece, send_sem=d2d_send.at[3 * t + k], recv_sem=d2d_recv.at[3 * t + k],
                device_id=(x, y, 1 - c), device_id_type=MESH_T)

        def own_chunk(t, r0):
            rt = self.chunk_rows[t]
            sel = pl.ds(r0, rt)
            return ins[t].at[layer, sel, :], placed(t, mine, sel, rt), stage[t], stage_sems

        return c, mine, peers, ici, d2d, own_chunk

    def start(self, ins, outs, scratch):
        c, mine, peers, ici, d2d, own_chunk = self._views(ins, outs, scratch)
        for t in range(self.nt):
            for k in range(3):
                ici(t, k, mine).start()
        starts = [list(range(0, self.shapes[t][1], self.chunk_rows[t])) for t in range(self.nt)]
        for r in range(max(len(s) for s in starts)):
            active = [(t, *own_chunk(t, starts[t][r])) for t in range(self.nt) if r < len(starts[t])]
            loads = [pltpu.make_async_copy(src, buf, sems.at[2 * t]) for t, src, dst, buf, sems in active]
            for cp in loads:
                cp.start()
            for cp in loads:
                cp.wait()
            stores = [pltpu.make_async_copy(buf, dst, sems.at[2 * t + 1]) for t, src, dst, buf, sems in active]
            for cp in stores:
                cp.start()
            for cp in stores:
                cp.wait()

    def finish(self, ins, outs, scratch):
        c, mine, peers, ici, d2d, own_chunk = self._views(ins, outs, scratch)
        for t in range(self.nt):
            for k, (px, py) in enumerate(peers):
                ici(t, k, 2 * px + py).wait_recv()
                if self.split[t]:
                    d2d(t, k, c).start()
        for t in range(self.nt):
            for k in range(3):
                if self.split[t]:
                    d2d(t, k, 1 - c).wait_recv()
        for t in range(self.nt):
            for k in range(3):
                ici(t, k, mine).wait_send()
                if self.split[t]:
                    d2d(t, k, c).wait_send()


def all_gather_layer(shards, axes, layer, *, name):
    plan = LayerGather(shards, axes, layer)
    nt = plan.nt

    def body(*refs):
        ins, outs, scratch = refs[:nt], refs[nt:2 * nt], refs[2 * nt:]
        plan.start(ins, outs, scratch)
        plan.finish(ins, outs, scratch)

    return pl.pallas_call(
        body,
        out_shape=tuple(plan.out_shapes()),
        in_specs=[ANY] * nt,
        out_specs=tuple([ANY] * nt),
        scratch_shapes=plan.scratch_shapes(),
        name=name,
    )(*shards)


class HalfLayout:
    def __init__(self, shape, axis):
        self.R, self.C = shape
        self.axis = axis
        if axis == 1:
            self.hr, self.pw = self.R // 2, self.C // 4
            self.half_shape = (self.hr, self.C)
        else:
            self.hr, self.pw = self.R // 8, self.C
            self.half_shape = (4 * self.hr, self.C)
        self.tr = _rows_tile(self.hr, self.pw)
        self.nr = self.hr // self.tr

    def in_grad(self, ref, blk, half):
        if self.axis == 1:
            return ref.at[pl.ds(pl.multiple_of(half * self.hr, 16), self.hr), pl.ds(pl.multiple_of(blk * self.pw, 128), self.pw)]
        return ref.at[pl.ds(pl.multiple_of((2 * blk + half) * self.hr, 16), self.hr), :]

    def in_half(self, ref, blk):
        if self.axis == 1:
            return ref.at[:, pl.ds(pl.multiple_of(blk * self.pw, 128), self.pw)]
        return ref.at[pl.ds(pl.multiple_of(blk * self.hr, 16), self.hr), :]

    def grad_spec(self):
        if self.axis == 1:
            return pl.BlockSpec((self.tr, self.pw), lambda j, i, s: (s[0] * self.nr + i, j))
        return pl.BlockSpec((self.tr, self.pw), lambda j, i, s: ((2 * j + s[0]) * self.nr + i, 0))

    def half_spec(self):
        if self.axis == 1:
            return pl.BlockSpec((self.tr, self.pw), lambda j, i, s: (i, j))
        return pl.BlockSpec((self.tr, self.pw), lambda j, i, s: (j * self.nr + i, 0))


def half_exchange(grads, layouts, *, name):
    nt = len(grads)
    pieces = [(t, j) for t in range(nt) for j in (range(4) if layouts[t].axis == 0 else range(1))]

    def body(*refs):
        ins, outs = refs[:nt], refs[nt:2 * nt]
        send_sems, recv_sems = refs[2 * nt:]
        x, y, c = _mesh_pos()
        cps = []
        for n, (t, j) in enumerate(pieces):
            lay = layouts[t]
            if lay.axis == 1:
                src = ins[t].at[pl.ds(pl.multiple_of((1 - c) * lay.hr, 16), lay.hr), :]
                dst = outs[t]
            else:
                src = lay.in_grad(ins[t], j, 1 - c)
                dst = lay.in_half(outs[t], j)
            cp = pltpu.make_async_remote_copy(src_ref=src, dst_ref=dst, send_sem=send_sems.at[n], recv_sem=recv_sems.at[n],
                                              device_id=(x, y, 1 - c), device_id_type=MESH_T)
            cp.start()
            cps.append(cp)
        for cp in cps:
            cp.wait_recv()
        for cp in cps:
            cp.wait_send()

    return pl.pallas_call(
        body,
        out_shape=tuple(jax.ShapeDtypeStruct(lay.half_shape, F32) for lay in layouts),
        in_specs=[ANY] * nt,
        out_specs=tuple([ANY] * nt),
        scratch_shapes=[pltpu.SemaphoreType.DMA((len(pieces),)), pltpu.SemaphoreType.DMA((len(pieces),))],
        name=name,
    )(*grads)


def pair_sum(grad, other, lay, core, *, name):
    def body(c_ref, g_ref, o_ref, s32_ref, s16_ref):
        s = g_ref[...] + o_ref[...]
        s32_ref[...] = s
        s16_ref[...] = s.astype(BF16)

    return pl.pallas_call(
        body,
        out_shape=(jax.ShapeDtypeStruct(lay.half_shape, F32), jax.ShapeDtypeStruct(lay.half_shape, BF16)),
        grid_spec=pltpu.PrefetchScalarGridSpec(
            num_scalar_prefetch=1, grid=(4, lay.nr),
            in_specs=[lay.grad_spec(), lay.half_spec()],
            out_specs=(lay.half_spec(), lay.half_spec())),
        compiler_params=_cp(("parallel", "parallel")),
        name=name,
    )(core, grad, other)


class BlockScatter:
    def __init__(self, layouts):
        self.layouts = layouts
        self.nt = len(layouts)

    def out_shapes(self):
        return [jax.ShapeDtypeStruct((3, lay.hr, lay.pw), BF16) for lay in self.layouts]

    def scratch_shapes(self):
        return [pltpu.SemaphoreType.DMA((3 * self.nt,)), pltpu.SemaphoreType.DMA((3 * self.nt,))]

    def _copies(self, pairs16, recv, scratch):
        send_sems, recv_sems = scratch
        x, y, c = _mesh_pos()
        return [pltpu.make_async_remote_copy(
            src_ref=lay.in_half(pairs16[t], 2 * px + py), dst_ref=recv[t].at[k],
            send_sem=send_sems.at[3 * t + k], recv_sem=recv_sems.at[3 * t + k],
            device_id=(px, py, c), device_id_type=MESH_T)
            for t, lay in enumerate(self.layouts) for k, (px, py) in enumerate(_chip_peers(x, y))]

    def start(self, pairs16, recv, scratch):
        for cp in self._copies(pairs16, recv, scratch):
            cp.start()

    def finish(self, pairs16, recv, scratch):
        copies = self._copies(pairs16, recv, scratch)
        for cp in copies:
            cp.wait_recv()
        for cp in copies:
            cp.wait_send()


def gather_small(small, *, name):
    def body(small_in, small_out, ssend, srecv):
        x, y, c = _mesh_pos()
        me = 4 * x + 2 * y + c
        sends, recvs = [], []
        for r, (px, py, pc) in enumerate(_all_peers(x, y, c)):
            def mk(slot, r=r, px=px, py=py, pc=pc):
                return pltpu.make_async_remote_copy(
                    src_ref=small_in, dst_ref=small_out.at[slot], send_sem=ssend.at[r], recv_sem=srecv.at[r],
                    device_id=(px, py, pc), device_id_type=MESH_T)
            snd = mk(me)
            snd.start()
            sends.append(snd)
            recvs.append(mk(4 * px + 2 * py + pc))
        for r in recvs:
            r.wait_recv()
        for s in sends:
            s.wait_send()

    return pl.pallas_call(
        body,
        out_shape=jax.ShapeDtypeStruct((8,) + small.shape, F32),
        in_specs=[ANY],
        out_specs=ANY,
        scratch_shapes=[pltpu.SemaphoreType.DMA((7,)), pltpu.SemaphoreType.DMA((7,))],
        name=name,
    )(small)


def sum_chips(pair32, recv, lay, chip, *, name):
    def body(j_ref, p_ref, r_ref, s_ref):
        acc = p_ref[...]
        for k in range(3):
            acc = acc + r_ref[k].astype(F32)
        s_ref[...] = acc

    if lay.axis == 1:
        own = pl.BlockSpec((lay.tr, lay.pw), lambda i, j: (i, j[0]))
    else:
        own = pl.BlockSpec((lay.tr, lay.pw), lambda i, j: (j[0] * lay.nr + i, 0))
    return pl.pallas_call(
        body,
        out_shape=jax.ShapeDtypeStruct((lay.hr, lay.pw), F32),
        grid_spec=pltpu.PrefetchScalarGridSpec(
            num_scalar_prefetch=1, grid=(lay.nr,),
            in_specs=[own, pl.BlockSpec((3, lay.tr, lay.pw), lambda i, j: (0, i, 0))],
            out_specs=pl.BlockSpec((lay.tr, lay.pw), lambda i, j: (i, 0))),
        compiler_params=_cp(("parallel",)),
        name=name,
    )(chip, pair32, recv)


def sum_devices(gathered, own, me, *, name):
    _, R, C = gathered.shape

    def body(me_ref, g_ref, o_ref, s_ref):
        acc = None
        for k in range(8):
            part = jnp.where(me_ref[0] == k, o_ref[...], g_ref[k])
            acc = part if acc is None else acc + part
        s_ref[...] = acc

    return pl.pallas_call(
        body,
        out_shape=jax.ShapeDtypeStruct((R, C), F32),
        grid_spec=pltpu.PrefetchScalarGridSpec(
            num_scalar_prefetch=1, grid=(1,),
            in_specs=[pl.BlockSpec((8, R, C), lambda i, m: (0, 0, 0)), pl.BlockSpec((R, C), lambda i, m: (0, 0))],
            out_specs=pl.BlockSpec((R, C), lambda i, m: (0, 0))),
        compiler_params=_cp(("arbitrary",)),
        name=name,
    )(me, gathered, own)


def sibling_swap(parts, *, name):
    nt = len(parts)

    def body(*refs):
        ins, outs = refs[:nt], refs[nt:2 * nt]
        send_sems, recv_sems = refs[2 * nt:]
        x, y, c = _mesh_pos()
        cps = []
        for t in range(nt):
            cp = pltpu.make_async_remote_copy(src_ref=ins[t], dst_ref=outs[t], send_sem=send_sems.at[t],
                                              recv_sem=recv_sems.at[t], device_id=(x, y, 1 - c), device_id_type=MESH_T)
            cp.start()
            cps.append(cp)
        for cp in cps:
            cp.wait_recv()
        for cp in cps:
            cp.wait_send()

    return pl.pallas_call(
        body,
        out_shape=tuple(jax.ShapeDtypeStruct(p.shape, p.dtype) for p in parts),
        in_specs=[ANY] * nt,
        out_specs=tuple([ANY] * nt),
        scratch_shapes=[pltpu.SemaphoreType.DMA((nt,)), pltpu.SemaphoreType.DMA((nt,))],
        name=name,
    )(*parts)


def adamw_halves(w, m, v, mine, other, lay, core, *, name):
    _, r, c = w.shape
    tr, nr = lay.tr, lay.nr
    assert (r, c) == (2 * lay.hr, lay.pw), (w.shape, lay.hr, lay.pw)

    def body(c_ref, w_ref, m_ref, v_ref, *rest):
        g_refs, outs = rest[:2 * DEPTH], rest[2 * DEPTH:]
        l, h = pl.program_id(0), pl.program_id(1)
        g = None
        for d in range(DEPTH):
            gd = jnp.where(h == c_ref[0], g_refs[d][...], g_refs[DEPTH + d][...])
            g = gd if g is None else jnp.where(l == d, gd, g)
        _adamw_math(g, w_ref, m_ref, v_ref, *outs)

    full = pl.BlockSpec((None, tr, c), lambda l, h, i, s: (l, h * nr + i, 0))

    def part(d, is_mine):
        def index(l, h, i, s):
            used = jnp.logical_and(l == d, (h == s[0]) == is_mine)
            return jnp.where(used, i, 0), 0
        return pl.BlockSpec((tr, c), index)

    shp = jax.ShapeDtypeStruct((DEPTH, r, c), F32)
    return pl.pallas_call(
        body,
        out_shape=(shp, shp, shp, shp),
        grid_spec=pltpu.PrefetchScalarGridSpec(
            num_scalar_prefetch=1, grid=(DEPTH, 2, nr),
            in_specs=[full, full, full] + [part(d, True) for d in range(DEPTH)] + [part(d, False) for d in range(DEPTH)],
            out_specs=(full, full, full, full)),
        compiler_params=_cp(("arbitrary", "arbitrary", "arbitrary")),
        name=name,
    )(core, w, m, v, *mine, *other)


WEIGHTS = ("norm_mix", "w_in", "b_gate", "q_norm_a", "k_norm_a", "rel_bias_a", "w_pool", "pool_scale",
           "w_branch_a", "w_branch_b", "w_branch_c", "w_out", "norm_ffn", "w_up", "conv_w", "conv_b", "w_down")
SHARDED = {"w_in": 2, "w_branch_a": 2, "w_branch_b": 2, "w_branch_c": 2, "w_out": 1, "w_up": 2, "conv_w": 2,
           "w_down": 1}
REPLICATED = tuple(n for n in WEIGHTS if n not in SHARDED)
MATMUL_WEIGHTS = tuple(n for n in SHARDED if n != "conv_w")
SMALL_WEIGHTS = tuple(n for n in WEIGHTS if n not in MATMUL_WEIGHTS)
SMALL_ROWS = 1496


def _layer_fwd(x, p, full, tables, rest=None, prefetch=None):
    l = p["l"]
    diag = exact_dot(p["rel_bias_a"], tables["onehot_t"], name="bias_diagonals")
    diag = diag.reshape(N_HEADS, N_VARIANTS, 1, DIAG_W).transpose(1, 0, 2, 3)
    biasm = bias_expand(diag, name="bias_expand")
    gq8 = jnp.tile(p["q_norm_a"], N_HEADS)[None]
    gk8 = jnp.tile(p["k_norm_a"], N_HEADS)[None]
    h = rmsnorm_fwd(x, p["norm_mix"][None], name="rmsnorm_fwd")
    if rest is None:
        proj = matmul(h, full["w_in"], b_layer=l, name="mm_in")
    else:
        plan, shards, names = rest
        proj, gathered = matmul(h, full["w_in"], b_layer=l, behind=(plan, shards), name="mm_in_gather")
        full = {**full, **dict(zip(names, gathered))}
    qa, ka, va, qb, kb, vb = qkv_prep(proj, gq8, gk8, name="qkv_prep")
    oa = attn_a_fwd(qa, ka, va, biasm, name="attn_a_fwd")
    if prefetch is None:
        ob, tot, nblk = attn_b_fwd(qb, kb, vb, name="attn_b_fwd")
    else:
        plan, shards, names = prefetch
        ob, tot, nblk, filled = attn_b_fwd(qb, kb, vb, gather=(plan, shards, [full[n] for n in names]),
                                           name="attn_b_fwd_gather")
        full = dict(zip(names, filled))
    wpool = p["w_pool"].astype(BF16)
    oc = pool_fwd(proj, wpool, p["pool_scale"][None], name="pool_fwd")
    merged = merge_fwd(oa, ob, oc, proj, p["b_gate"][None], full["w_branch_a"][l], full["w_branch_b"][l],
                       full["w_branch_c"][l], name="merge_fwd")
    x1 = matmul(merged, full["w_out"], b_layer=l, add=x, name="mm_out")
    h2 = rmsnorm_fwd(x1, p["norm_ffn"][None], name="rmsnorm_fwd")
    u = matmul(h2, full["w_up"], b_layer=l, name="mm_up")
    a = conv_glu_fwd(u, full["conv_w"][l], p["conv_b"][None], name="conv_glu_fwd")
    x2 = matmul(a, full["w_down"], b_layer=l, add=x1, name="mm_down")
    saved = dict(x=x, h=h, proj=proj, qa=qa, ka=ka, va=va, qb=qb, kb=kb, vb=vb, oa=oa, ob=ob, tot=tot, nblk=nblk, oc=oc,
                 merged=merged, x1=x1, h2=h2, u=u, a=a, biasm=biasm, gq8=gq8, gk8=gk8, wpool=wpool)
    return x2, saved, full


class GradReducer:
    def __init__(self, layouts, core, chip):
        self.layouts, self.core, self.chip = layouts, core, chip
        self.pairs32, self.received, self.pending = {}, {}, []

    def prepare(self, layer, grads):
        names = list(grads)
        lays = [self.layouts[n] for n in names]
        others = half_exchange([grads[n] for n in names], lays, name="half_exchange")
        for n, lay, other in zip(names, lays, others):
            p32, p16 = pair_sum(grads[n], other, lay, self.core, name="pair_sum")
            self.pairs32[(layer, n)] = p32
            self.pending.append(((layer, n), p16))

    def take(self):
        keys = [k for k, _ in self.pending]
        pairs16 = [p for _, p in self.pending]
        self.pending = []
        return keys, BlockScatter([self.layouts[n] for _, n in keys]), pairs16

    def store(self, keys, received):
        self.received.update(zip(keys, received))

    def finish(self):
        keys = list(self.pairs32)
        mine = [sum_chips(self.pairs32[k], self.received[k], self.layouts[k[1]], self.chip, name="sum_chips") for k in keys]
        other = sibling_swap(mine, name="sibling_swap")
        return dict(zip(keys, mine)), dict(zip(keys, other))


EARLY_WEIGHTS = ("w_down", "w_up", "w_out", "w_branch_a", "w_branch_b", "w_branch_c")


def _layer_bwd(dx2, s, p, tables, reducer):
    g = {}
    full, l = p["full"], p["l"]
    da = matmul(dx2, full["w_down"], b_layer=l, tb=True, name="mm_down_dx")
    g["w_down"] = matmul(s["a"], dx2, ta=True, name="mm_down_dw")
    dug, duv, dcwg, dcwv, dcbg, dcbv = conv_glu_bwd(s["u"], p["conv_w"], p["conv_b"][None], da, name="conv_glu_bwd")
    du = jnp.concatenate([dug, duv], axis=1)
    g["conv_w"] = jnp.concatenate([dcwg, dcwv], axis=1)
    g["conv_b"] = jnp.concatenate([dcbg, dcbv], axis=1)[0]
    g["w_up"] = matmul(s["h2"], du, ta=True, name="mm_up_dw")
    dx1, dg2 = matmul(du, full["w_up"], b_layer=l, tb=True, norm_bwd=(s["x1"], p["norm_ffn"][None], dx2),
                      name="mm_up_dx_norm")
    g["norm_ffn"] = dg2[0]
    dmerged = matmul(dx1, full["w_out"], b_layer=l, tb=True, name="mm_out_dx")
    g["w_out"] = matmul(s["merged"], dx1, ta=True, name="mm_out_dw")
    t_a, t_b, t_c, dga, dgb, dgc, dba, dbb, dbc = merge_bwd(
        dmerged, s["oa"], s["ob"], s["oc"], s["proj"], p["b_gate"][None], p["w_branch_a"], p["w_branch_b"],
        p["w_branch_c"], name="merge_bwd")
    g["b_gate"] = jnp.concatenate([dba, dbb, dbc], axis=1)[0]
    g["w_branch_a"] = matmul(s["oa"], t_a, ta=True, name="mm_branch_dw")
    g["w_branch_b"] = matmul(s["ob"], t_b, ta=True, name="mm_branch_dw")
    g["w_branch_c"] = matmul(s["oc"], t_c, ta=True, name="mm_branch_dw")
    doa = matmul(t_a, full["w_branch_a"], b_layer=l, tb=True, out_dtype=BF16, name="mm_branch_dx")
    dob = matmul(t_b, full["w_branch_b"], b_layer=l, tb=True, out_dtype=BF16, name="mm_branch_dx")
    doc = matmul(t_c, full["w_branch_c"], b_layer=l, tb=True, name="mm_branch_dx_f32")
    dqh, dkh, dva, dbias = attn_a_bwd(s["qa"], s["ka"], s["va"], s["biasm"], doa, name="attn_a_bwd")
    ddiag = relbias_reduce(dbias, name="relbias_reduce")
    ddiag = ddiag.transpose(1, 0, 2, 3).reshape(N_HEADS, N_VARIANTS * DIAG_W)
    g["rel_bias_a"] = exact_dot(ddiag, tables["onehot"], name="relbias_table")
    dqa, dka, dgq8, dgk8 = qknorm_bwd(s["proj"], s["gq8"], s["gk8"], dqh, dkh, name="qknorm_bwd")
    g["q_norm_a"] = dgq8.reshape(N_HEADS, HEAD_DIM).sum(axis=0)
    g["k_norm_a"] = dgk8.reshape(N_HEADS, HEAD_DIM).sum(axis=0)
    reducer.prepare(l, {n: g[n] for n in EARLY_WEIGHTS})
    keys, plan, pairs16 = reducer.take()
    dqb, dkb, dvb, received = attn_b_bwd(s["qb"], s["kb"], s["vb"], s["tot"], s["nblk"], dob, scatter=(plan, pairs16),
                                         name="attn_b_bwd_scatter")
    reducer.store(keys, received)
    duc, dwp, dsc = pool_bwd(s["proj"], s["wpool"], p["pool_scale"][None], doc, name="pool_bwd")
    g["w_pool"] = dwp
    g["pool_scale"] = dsc[0]
    dproj = jnp.concatenate([dqa, dka, dva.astype(BF16), dqb, dkb.astype(BF16), dvb.astype(BF16), duc,
                             dga, dgb, dgc], axis=1)
    g["w_in"] = matmul(s["h"], dproj, ta=True, name="mm_in_dw")
    reducer.prepare(l, {"w_in": g["w_in"]})
    norm = (s["x"], p["norm_mix"][None], dx1)
    if l > 0:
        dx, dg1 = matmul(dproj, full["w_in"], b_layer=l, tb=True, norm_bwd=norm, name="mm_in_dx_norm")
    else:
        keys, plan, pairs16 = reducer.take()
        dx, dg1, received = matmul(dproj, full["w_in"], b_layer=l, tb=True, norm_bwd=norm, behind=(plan, pairs16),
                                   name="mm_in_dx_norm_scatter")
        reducer.store(keys, received)
    g["norm_mix"] = dg1[0]
    return dx, g


def kernel(x, norm_mix, w_in, b_gate, q_norm_a, k_norm_a, rel_bias_a, w_pool, pool_scale, w_branch_a, w_branch_b, w_branch_c, w_out, norm_ffn, w_up, conv_w, conv_b, w_down, loss_target, m_norm_mix, m_w_in, m_b_gate, m_q_norm_a, m_k_norm_a, m_rel_bias_a, m_w_pool, m_pool_scale, m_w_branch_a, m_w_branch_b, m_w_branch_c, m_w_out, m_norm_ffn, m_w_up, m_conv_w, m_conv_b, m_w_down, v_norm_mix, v_w_in, v_b_gate, v_q_norm_a, v_k_norm_a, v_rel_bias_a, v_w_pool, v_pool_scale, v_w_branch_a, v_w_branch_b, v_w_branch_c, v_w_out, v_norm_ffn, v_w_up, v_conv_w, v_conv_b, v_w_down):
    w = dict(zip(WEIGHTS, (norm_mix, w_in, b_gate, q_norm_a, k_norm_a, rel_bias_a, w_pool, pool_scale, w_branch_a,
                           w_branch_b, w_branch_c, w_out, norm_ffn, w_up, conv_w, conv_b, w_down)))
    m = dict(zip(WEIGHTS, (m_norm_mix, m_w_in, m_b_gate, m_q_norm_a, m_k_norm_a, m_rel_bias_a, m_w_pool, m_pool_scale,
                           m_w_branch_a, m_w_branch_b, m_w_branch_c, m_w_out, m_norm_ffn, m_w_up, m_conv_w, m_conv_b,
                           m_w_down)))
    v = dict(zip(WEIGHTS, (v_norm_mix, v_w_in, v_b_gate, v_q_norm_a, v_k_norm_a, v_rel_bias_a, v_w_pool, v_pool_scale,
                           v_w_branch_a, v_w_branch_b, v_w_branch_c, v_w_out, v_norm_ffn, v_w_up, v_conv_w, v_conv_b,
                           v_w_down)))
    onehot = diagonal_onehot()
    tables = dict(onehot=jnp.asarray(onehot), onehot_t=jnp.asarray(np.ascontiguousarray(onehot.T)))

    names = tuple(SHARDED)
    shards = [w[n] if n == "conv_w" else w[n].astype(BF16) for n in names]
    axes = [SHARDED[n] for n in names]
    later = [i for i, n in enumerate(names) if n != "w_in"]
    rest = (LayerGather([shards[i] for i in later], [axes[i] for i in later], 0), [shards[i] for i in later],
            [names[i] for i in later])
    first = names.index("w_in")
    full = {"w_in": all_gather_layer([shards[first]], [axes[first]], 0, name="all_gather_layer")[0]}

    def layer_params(l):
        p = {n: full[n][l] for n in ("w_branch_a", "w_branch_b", "w_branch_c", "conv_w")}
        p.update({n: w[n][l] for n in REPLICATED})
        p.update(full=full, l=l)
        return p

    xs = x[0]
    saved = []
    for l in range(DEPTH):
        prefetch = (LayerGather(shards, axes, l + 1), shards, names) if l + 1 < DEPTH else None
        replicated = {n: w[n][l] for n in REPLICATED}
        xs, s, full = _layer_fwd(xs, dict(replicated, l=l), full, tables, rest if l == 0 else None, prefetch)
        saved.append(s)
    dx, lpart = loss_head(xs, loss_target[0], name="loss_head")
    loss = lax.psum(lpart[0, 0], MESH_AXES)
    as_index = lambda i: jnp.reshape(i, (1,)).astype(jnp.int32)
    cx, cy, cc = _mesh_pos()
    core, chip, me = as_index(cc), as_index(2 * cx + cy), as_index(4 * cx + 2 * cy + cc)
    layouts = {n: HalfLayout((full[n].shape[1], full[n].shape[2]), SHARDED[n] - 1) for n in MATMUL_WEIGHTS}
    reducer = GradReducer(layouts, core, chip)
    grads = [None] * DEPTH
    for l in reversed(range(DEPTH)):
        dx, grads[l] = _layer_bwd(dx, saved[l], layer_params(l), tables, reducer)

    g = {n: jnp.stack([grads[l][n] for l in range(DEPTH)]) for n in SMALL_WEIGHTS}
    flat = jnp.concatenate([g[n].reshape(-1) for n in SMALL_WEIGHTS])
    small = jnp.pad(flat, (0, SMALL_ROWS * 128 - flat.shape[0])).reshape(SMALL_ROWS, 128)
    small_sum = sum_devices(gather_small(small, name="gather_small"), small, me, name="sum_devices").reshape(-1)
    mine, other = reducer.finish()

    res = {}
    for n in MATMUL_WEIGHTS:
        res[n] = adamw_halves(w[n], m[n], v[n], [mine[(l, n)] for l in range(DEPTH)],
                              [other[(l, n)] for l in range(DEPTH)], layouts[n], core, name="adamw_halves")
    off = 0
    for n in SMALL_WEIGHTS:
        shp = g[n].shape
        size = int(np.prod(shp))
        gn = small_sum[off:off + size].reshape(shp)
        off += size
        if n in SHARDED:
            gn = lax.dynamic_slice_in_dim(gn, (2 * cx + cy) * w[n].shape[-1], w[n].shape[-1], axis=len(shp) - 1)
        shp = w[n].shape
        cols = shp[-1]
        two_d = lambda t: t.reshape(int(np.prod(shp)) // cols, cols)
        res[n] = [t.reshape(shp) for t in adamw(two_d(w[n]), two_d(m[n]), two_d(v[n]), two_d(gn), name="adamw")]

    out = [loss, dx[None]]
    for k in range(4):
        out.extend(res[n][k] for n in WEIGHTS)
    return tuple(out)
```
